```python
import math
import jax, jax.numpy as jnp
from jax import lax
import numpy as np

D_MODEL = 1024
BATCH = 8
SEQ = 2048
DEPTH = 4

MIX_WIDTH = D_MODEL
POOL_WIDTH = MIX_WIDTH // 2
POOL_WINDOWS = (2, 4, 8, 16)
POOL_GROUPS = len(POOL_WINDOWS)
POOL_GC = POOL_WIDTH // POOL_GROUPS
N_HEADS = 4
QK_NOPE = 128
QK_ROPE = 64
V_HEAD = 128
QK_HEAD = QK_NOPE + QK_ROPE
MLA_WIDTH = N_HEADS * V_HEAD
Q_LORA = 384
KV_LORA = 256
ROPE_THETA = 10000.0
SOFTMAX_SCALE = 1.0 / math.sqrt(QK_HEAD)
Q_BLOCK = 128
IN_COLS = POOL_WIDTH + Q_LORA + KV_LORA + QK_ROPE
D_FF = 2816
N_SUBLAYERS = 3
EPS = 1e-6

kernel_name = "hybrid_macaron_pool_mla_adaln"


def rms_norm(x, g):
    xf = x.astype(jnp.float32)
    y = xf * lax.rsqrt(jnp.mean(xf * xf, axis=-1, keepdims=True) + EPS)
    return (y * g.astype(jnp.float32)).astype(x.dtype)


def modulate(h, shift, scale):
    return h * (1 + scale[:, None, :]) + shift[:, None, :]


def swiglu(h, w_gate, w_up, w_down):
    return (jax.nn.silu(h @ w_gate) * (h @ w_up)) @ w_down


def rotate_half(x):
    x1, x2 = jnp.split(x, 2, axis=-1)
    return jnp.concatenate([-x2, x1], axis=-1)


def apply_rope(x, cos, sin):
    return x * cos + rotate_half(x) * sin


def causal_multiscale_pool(u, pool_w, pool_scale):
    B, S, C = u.shape
    cs = jnp.cumsum(u.astype(jnp.float32), axis=1)
    pos = jnp.arange(S)
    means = []
    for g, w in enumerate(POOL_WINDOWS):
        csg = cs[..., g * POOL_GC:(g + 1) * POOL_GC]
        lag = jnp.pad(csg, ((0, 0), (w, 0), (0, 0)))[:, :S]
        cnt = jnp.minimum(pos + 1, w).astype(jnp.float32)[None, :, None]
        means.append((csg - lag) / cnt)
    pooled = jnp.stack(means, axis=2).astype(u.dtype)
    diff = pooled - u.reshape(B, S, POOL_GROUPS, POOL_GC)
    y = jnp.einsum('bsgc,gcd->bsgd', diff, pool_w).reshape(B, S, C)
    return y * pool_scale


def mla_attention(cq, ckv, kr, q_a_norm, w_q_b, kv_a_norm, w_kv_b, cos, sin):
    B, S, _ = cq.shape
    q = (rms_norm(cq, q_a_norm) @ w_q_b).reshape(B, S, N_HEADS, QK_HEAD)
    q_nope, q_rope = q[..., :QK_NOPE], q[..., QK_NOPE:]
    q_rope = apply_rope(q_rope, cos[:, :, None, :], sin[:, :, None, :])
    kv = (rms_norm(ckv, kv_a_norm) @ w_kv_b).reshape(B, S, N_HEADS, QK_NOPE + V_HEAD)
    k_nope, v = kv[..., :QK_NOPE], kv[..., QK_NOPE:]
    k_rope = apply_rope(kr, cos, sin)

    nb = S // Q_BLOCK
    qn = q_nope.reshape(B, nb, Q_BLOCK, N_HEADS, QK_NOPE).transpose(1, 0, 3, 2, 4)
    qr = q_rope.reshape(B, nb, Q_BLOCK, N_HEADS, QK_ROPE).transpose(1, 0, 3, 2, 4)
    kn = k_nope.transpose(0, 2, 1, 3)
    vv = v.transpose(0, 2, 1, 3)
    kpos = jnp.arange(S)

    def block(args):
        qn_b, qr_b, i = args
        s = (jnp.einsum('bhqd,bhkd->bhqk', qn_b, kn)
             + jnp.einsum('bhqd,bkd->bhqk', qr_b, k_rope)).astype(jnp.float32) * SOFTMAX_SCALE
        qpos = i * Q_BLOCK + jnp.arange(Q_BLOCK)
        s = jnp.where(qpos[:, None] >= kpos[None, :], s, -jnp.inf)
        p = jax.nn.softmax(s, axis=-1).astype(vv.dtype)
        return jnp.einsum('bhqk,bhkd->bhqd', p, vv)

    o = lax.map(block, (qn, qr, jnp.arange(nb)))
    return o.transpose(1, 0, 3, 2, 4).reshape(B, S, MLA_WIDTH)


def _fwd_setup_inputs(seed: int = 0) -> dict:
    key = jax.random.key(seed)
    ks = jax.random.split(key, 24)
    f32 = jnp.float32
    nrm = lambda k, shape, s: (jax.random.normal(k, shape, f32) * s)
    gain = lambda k, shape: 1.0 + 0.05 * jax.random.normal(k, shape, f32)
    D, F, L = D_MODEL, D_FF, DEPTH
    x = jax.random.normal(ks[0], (BATCH, SEQ, D), f32)
    c = jax.random.normal(ks[1], (BATCH, D), f32)
    positions = jnp.broadcast_to(jnp.arange(SEQ, dtype=jnp.int32)[None, :], (BATCH, SEQ))
    return {
        "x": x,
        "c": c,
        "positions": positions,
        "ada_w": nrm(ks[2], (L, D, 3 * N_SUBLAYERS * D), 0.5 * D ** -0.5),
        "ada_b": nrm(ks[3], (L, 3 * N_SUBLAYERS * D), 0.01),
        "ffn1_norm": gain(ks[4], (L, D)),
        "ffn1_w_gate": nrm(ks[5], (L, D, F), D ** -0.5),
        "ffn1_w_up": nrm(ks[6], (L, D, F), D ** -0.5),
        "ffn1_w_down": nrm(ks[7], (L, F, D), F ** -0.5),
        "mix_norm": gain(ks[8], (L, D)),
        "w_in": nrm(ks[9], (L, D, IN_COLS), D ** -0.5),
        "pool_w": nrm(ks[10], (L, POOL_GROUPS, POOL_GC, POOL_GC), POOL_GC ** -0.5),
        "pool_scale": gain(ks[11], (L, POOL_WIDTH)),
        "q_a_norm": gain(ks[12], (L, Q_LORA)),
        "w_q_b": nrm(ks[13], (L, Q_LORA, N_HEADS * QK_HEAD), Q_LORA ** -0.5),
        "kv_a_norm": gain(ks[14], (L, KV_LORA)),
        "w_kv_b": nrm(ks[15], (L, KV_LORA, N_HEADS * (QK_NOPE + V_HEAD)), KV_LORA ** -0.5),
        "w_out": nrm(ks[16], (L, MIX_WIDTH, D), MIX_WIDTH ** -0.5),
        "ffn2_norm": gain(ks[17], (L, D)),
        "ffn2_w_gate": nrm(ks[18], (L, D, F), D ** -0.5),
        "ffn2_w_up": nrm(ks[19], (L, D, F), D ** -0.5),
        "ffn2_w_down": nrm(ks[20], (L, F, D), F ** -0.5),
        "final_norm": gain(ks[21], (D,)),
    }


def _fwd_reference(x, c, positions, ada_w, ada_b, ffn1_norm, ffn1_w_gate, ffn1_w_up, ffn1_w_down,
              mix_norm, w_in, pool_w, pool_scale, q_a_norm, w_q_b, kv_a_norm, w_kv_b, w_out,
              ffn2_norm, ffn2_w_gate, ffn2_w_up, ffn2_w_down, final_norm):
    inv_freq = 1.0 / (ROPE_THETA ** (jnp.arange(0, QK_ROPE, 2, dtype=jnp.float32) / QK_ROPE))
    ang = positions.astype(jnp.float32)[..., None] * inv_freq
    ang = jnp.concatenate([ang, ang], axis=-1)
    cos = jnp.cos(ang).astype(x.dtype)
    sin = jnp.sin(ang).astype(x.dtype)
    c_act = jax.nn.silu(c)

    for l in range(DEPTH):
        mod = c_act @ ada_w[l] + ada_b[l]
        (sh1, sc1, g1, sh2, sc2, g2, sh3, sc3, g3) = jnp.split(mod, 3 * N_SUBLAYERS, axis=-1)

        h = modulate(rms_norm(x, ffn1_norm[l]), sh1, sc1)
        x = x + 0.5 * g1[:, None, :] * swiglu(h, ffn1_w_gate[l], ffn1_w_up[l], ffn1_w_down[l])

        h = modulate(rms_norm(x, mix_norm[l]), sh2, sc2)
        z = h @ w_in[l]
        o1 = POOL_WIDTH
        o2 = o1 + Q_LORA
        o3 = o2 + KV_LORA
        y_pool = causal_multiscale_pool(z[..., :o1], pool_w[l], pool_scale[l])
        y_mla = mla_attention(z[..., o1:o2], z[..., o2:o3], z[..., o3:], q_a_norm[l], w_q_b[l],
                              kv_a_norm[l], w_kv_b[l], cos, sin)
        y = jnp.concatenate([y_pool, y_mla], axis=-1) @ w_out[l]
        x = x + g2[:, None, :] * y

        h = modulate(rms_norm(x, ffn2_norm[l]), sh3, sc3)
        x = x + 0.5 * g3[:, None, :] * swiglu(h, ffn2_w_gate[l], ffn2_w_up[l], ffn2_w_down[l])

    return rms_norm(x, final_norm)


import jax as _jax
import jax.numpy as _jnp

TWIN_FORMAT = 'train_step'
FWD_PARAMS = ['x', 'c', 'positions', 'ada_w', 'ada_b', 'ffn1_norm', 'ffn1_w_gate', 'ffn1_w_up', 'ffn1_w_down', 'mix_norm', 'w_in', 'pool_w', 'pool_scale', 'q_a_norm', 'w_q_b', 'kv_a_norm', 'w_kv_b', 'w_out', 'ffn2_norm', 'ffn2_w_gate', 'ffn2_w_up', 'ffn2_w_down', 'final_norm']
TWIN_WEIGHTS = ['ada_w', 'ada_b', 'ffn1_norm', 'ffn1_w_gate', 'ffn1_w_up', 'ffn1_w_down', 'mix_norm', 'w_in', 'pool_w', 'pool_scale', 'q_a_norm', 'w_q_b', 'kv_a_norm', 'w_kv_b', 'w_out', 'ffn2_norm', 'ffn2_w_gate', 'ffn2_w_up', 'ffn2_w_down', 'final_norm']
TWIN_DIFF_INPUT = 'x'
TWIN_INPUTS = ['x', 'c', 'positions', 'ada_w', 'ada_b', 'ffn1_norm', 'ffn1_w_gate', 'ffn1_w_up', 'ffn1_w_down', 'mix_norm', 'w_in', 'pool_w', 'pool_scale', 'q_a_norm', 'w_q_b', 'kv_a_norm', 'w_kv_b', 'w_out', 'ffn2_norm', 'ffn2_w_gate', 'ffn2_w_up', 'ffn2_w_down', 'final_norm', 'loss_target', 'm_ada_w', 'm_ada_b', 'm_ffn1_norm', 'm_ffn1_w_gate', 'm_ffn1_w_up', 'm_ffn1_w_down', 'm_mix_norm', 'm_w_in', 'm_pool_w', 'm_pool_scale', 'm_q_a_norm', 'm_w_q_b', 'm_kv_a_norm', 'm_w_kv_b', 'm_w_out', 'm_ffn2_norm', 'm_ffn2_w_gate', 'm_ffn2_w_up', 'm_ffn2_w_down', 'm_final_norm', 'v_ada_w', 'v_ada_b', 'v_ffn1_norm', 'v_ffn1_w_gate', 'v_ffn1_w_up', 'v_ffn1_w_down', 'v_mix_norm', 'v_w_in', 'v_pool_w', 'v_pool_scale', 'v_q_a_norm', 'v_w_q_b', 'v_kv_a_norm', 'v_w_kv_b', 'v_w_out', 'v_ffn2_norm', 'v_ffn2_w_gate', 'v_ffn2_w_up', 'v_ffn2_w_down', 'v_final_norm']
TWIN_OUTPUTS = ['loss', 'grad_x', 'grad_ada_w', 'grad_ada_b', 'grad_ffn1_norm', 'grad_ffn1_w_gate', 'grad_ffn1_w_up', 'grad_ffn1_w_down', 'grad_mix_norm', 'grad_w_in', 'grad_pool_w', 'grad_pool_scale', 'grad_q_a_norm', 'grad_w_q_b', 'grad_kv_a_norm', 'grad_w_kv_b', 'grad_w_out', 'grad_ffn2_norm', 'grad_ffn2_w_gate', 'grad_ffn2_w_up', 'grad_ffn2_w_down', 'grad_final_norm', 'delta_ada_w', 'delta_ada_b', 'delta_ffn1_norm', 'delta_ffn1_w_gate', 'delta_ffn1_w_up', 'delta_ffn1_w_down', 'delta_mix_norm', 'delta_w_in', 'delta_pool_w', 'delta_pool_scale', 'delta_q_a_norm', 'delta_w_q_b', 'delta_kv_a_norm', 'delta_w_kv_b', 'delta_w_out', 'delta_ffn2_norm', 'delta_ffn2_w_gate', 'delta_ffn2_w_up', 'delta_ffn2_w_down', 'delta_final_norm', 'new_m_ada_w', 'new_m_ada_b', 'new_m_ffn1_norm', 'new_m_ffn1_w_gate', 'new_m_ffn1_w_up', 'new_m_ffn1_w_down', 'new_m_mix_norm', 'new_m_w_in', 'new_m_pool_w', 'new_m_pool_scale', 'new_m_q_a_norm', 'new_m_w_q_b', 'new_m_kv_a_norm', 'new_m_w_kv_b', 'new_m_w_out', 'new_m_ffn2_norm', 'new_m_ffn2_w_gate', 'new_m_ffn2_w_up', 'new_m_ffn2_w_down', 'new_m_final_norm', 'new_v_ada_w', 'new_v_ada_b', 'new_v_ffn1_norm', 'new_v_ffn1_w_gate', 'new_v_ffn1_w_up', 'new_v_ffn1_w_down', 'new_v_mix_norm', 'new_v_w_in', 'new_v_pool_w', 'new_v_pool_scale', 'new_v_q_a_norm', 'new_v_w_q_b', 'new_v_kv_a_norm', 'new_v_w_kv_b', 'new_v_w_out', 'new_v_ffn2_norm', 'new_v_ffn2_w_gate', 'new_v_ffn2_w_up', 'new_v_ffn2_w_down', 'new_v_final_norm']
TWIN_LEAF_KINDS = {'loss': 'loss', 'grad_x': 'grad_x', 'grad_ada_w': 'grad_w', 'grad_ada_b': 'grad_w', 'grad_ffn1_norm': 'grad_w', 'grad_ffn1_w_gate': 'grad_w', 'grad_ffn1_w_up': 'grad_w', 'grad_ffn1_w_down': 'grad_w', 'grad_mix_norm': 'grad_w', 'grad_w_in': 'grad_w', 'grad_pool_w': 'grad_w', 'grad_pool_scale': 'grad_w', 'grad_q_a_norm': 'grad_w', 'grad_w_q_b': 'grad_w', 'grad_kv_a_norm': 'grad_w', 'grad_w_kv_b': 'grad_w', 'grad_w_out': 'grad_w', 'grad_ffn2_norm': 'grad_w', 'grad_ffn2_w_gate': 'grad_w', 'grad_ffn2_w_up': 'grad_w', 'grad_ffn2_w_down': 'grad_w', 'grad_final_norm': 'grad_w', 'delta_ada_w': 'delta_w', 'delta_ada_b': 'delta_w', 'delta_ffn1_norm': 'delta_w', 'delta_ffn1_w_gate': 'delta_w', 'delta_ffn1_w_up': 'delta_w', 'delta_ffn1_w_down': 'delta_w', 'delta_mix_norm': 'delta_w', 'delta_w_in': 'delta_w', 'delta_pool_w': 'delta_w', 'delta_pool_scale': 'delta_w', 'delta_q_a_norm': 'delta_w', 'delta_w_q_b': 'delta_w', 'delta_kv_a_norm': 'delta_w', 'delta_w_kv_b': 'delta_w', 'delta_w_out': 'delta_w', 'delta_ffn2_norm': 'delta_w', 'delta_ffn2_w_gate': 'delta_w', 'delta_ffn2_w_up': 'delta_w', 'delta_ffn2_w_down': 'delta_w', 'delta_final_norm': 'delta_w', 'new_m_ada_w': 'new_m', 'new_m_ada_b': 'new_m', 'new_m_ffn1_norm': 'new_m', 'new_m_ffn1_w_gate': 'new_m', 'new_m_ffn1_w_up': 'new_m', 'new_m_ffn1_w_down': 'new_m', 'new_m_mix_norm': 'new_m', 'new_m_w_in': 'new_m', 'new_m_pool_w': 'new_m', 'new_m_pool_scale': 'new_m', 'new_m_q_a_norm': 'new_m', 'new_m_w_q_b': 'new_m', 'new_m_kv_a_norm': 'new_m', 'new_m_w_kv_b': 'new_m', 'new_m_w_out': 'new_m', 'new_m_ffn2_norm': 'new_m', 'new_m_ffn2_w_gate': 'new_m', 'new_m_ffn2_w_up': 'new_m', 'new_m_ffn2_w_down': 'new_m', 'new_m_final_norm': 'new_m', 'new_v_ada_w': 'new_v', 'new_v_ada_b': 'new_v', 'new_v_ffn1_norm': 'new_v', 'new_v_ffn1_w_gate': 'new_v', 'new_v_ffn1_w_up': 'new_v', 'new_v_ffn1_w_down': 'new_v', 'new_v_mix_norm': 'new_v', 'new_v_w_in': 'new_v', 'new_v_pool_w': 'new_v', 'new_v_pool_scale': 'new_v', 'new_v_q_a_norm': 'new_v', 'new_v_w_q_b': 'new_v', 'new_v_kv_a_norm': 'new_v', 'new_v_w_kv_b': 'new_v', 'new_v_w_out': 'new_v', 'new_v_ffn2_norm': 'new_v', 'new_v_ffn2_w_gate': 'new_v', 'new_v_ffn2_w_up': 'new_v', 'new_v_ffn2_w_down': 'new_v', 'new_v_final_norm': 'new_v'}


def _forward(args):
    return _fwd_reference(*[args[k] for k in FWD_PARAMS])


def _output_shape():
    out = _jax.eval_shape(lambda: _forward(_fwd_setup_inputs(0)))
    return out.shape, out.dtype

N_MICROBATCH = 1
ADAM_LR = 0.001
ADAM_B1 = 0.9
ADAM_B2 = 0.999
ADAM_EPS = 1e-08
ADAM_WD = 0.01
ADAM_STEP = 10
PER_EXAMPLE_BATCH_AXIS = {'x': 0, 'c': 0, 'positions': 0, 'loss_target': 0}
SHARED_INPUTS = []
_WEIGHT_DTYPES = {'ada_w': _jnp.float32, 'ada_b': _jnp.float32, 'ffn1_norm': _jnp.float32, 'ffn1_w_gate': _jnp.float32, 'ffn1_w_up': _jnp.float32, 'ffn1_w_down': _jnp.float32, 'mix_norm': _jnp.float32, 'w_in': _jnp.float32, 'pool_w': _jnp.float32, 'pool_scale': _jnp.float32, 'q_a_norm': _jnp.float32, 'w_q_b': _jnp.float32, 'kv_a_norm': _jnp.float32, 'w_kv_b': _jnp.float32, 'w_out': _jnp.float32, 'ffn2_norm': _jnp.float32, 'ffn2_w_gate': _jnp.float32, 'ffn2_w_up': _jnp.float32, 'ffn2_w_down': _jnp.float32, 'final_norm': _jnp.float32}
MOMENT_SCALE = {'ada_w': 2.730083e-02, 'ada_b': 4.593247e-02, 'ffn1_norm': 1.892342e-02, 'ffn1_w_gate': 8.467768e-03, 'ffn1_w_up': 8.218307e-03, 'ffn1_w_down': 1.360664e-02, 'mix_norm': 2.528930e-02, 'w_in': 2.417729e-02, 'pool_w': 3.411228e-02, 'pool_scale': 3.352073e-02, 'q_a_norm': 7.472133e-03, 'w_q_b': 5.190190e-03, 'kv_a_norm': 2.088794e-02, 'w_kv_b': 1.010985e-02, 'w_out': 2.553213e-02, 'ffn2_norm': 1.865820e-02, 'ffn2_w_gate': 8.296617e-03, 'ffn2_w_up': 8.041054e-03, 'ffn2_w_down': 1.335585e-02, 'final_norm': 1.608238e+01}


def _to_microbatches(a, axis):
    t = _jnp.moveaxis(a, axis, 0)
    t = t.reshape((N_MICROBATCH, t.shape[0] // N_MICROBATCH) + t.shape[1:])
    return _jnp.moveaxis(t, 1, axis + 1)


def setup_inputs(seed: int = 0) -> dict:
    inp = _fwd_setup_inputs(seed)
    key = _jax.random.fold_in(_jax.random.key(seed), 7919)
    shape, _ = _output_shape()
    out = dict(inp)
    out["loss_target"] = _jax.random.normal(_jax.random.fold_in(key, 0), shape, _jnp.float32)
    for i, name in enumerate(TWIN_WEIGHTS):
        w = inp[name].astype(_jnp.float32)
        if MOMENT_SCALE is None:
            s = _jnp.sqrt(_jnp.mean(_jnp.square(w)) + 1e-30)
        else:
            s = MOMENT_SCALE[name]
        km, kv = _jax.random.split(_jax.random.fold_in(key, i + 1))
        out[name] = w
        out["m_" + name] = s * _jax.random.normal(km, w.shape, _jnp.float32)
        out["v_" + name] = (s * s) * _jax.random.uniform(kv, w.shape, _jnp.float32, 0.5, 1.5)
    if N_MICROBATCH > 1:
        for name, axis in PER_EXAMPLE_BATCH_AXIS.items():
            out[name] = _to_microbatches(out[name], axis)
    return {'x': out['x'], 'c': out['c'], 'positions': out['positions'], 'ada_w': out['ada_w'], 'ada_b': out['ada_b'], 'ffn1_norm': out['ffn1_norm'], 'ffn1_w_gate': out['ffn1_w_gate'], 'ffn1_w_up': out['ffn1_w_up'], 'ffn1_w_down': out['ffn1_w_down'], 'mix_norm': out['mix_norm'], 'w_in': out['w_in'], 'pool_w': out['pool_w'], 'pool_scale': out['pool_scale'], 'q_a_norm': out['q_a_norm'], 'w_q_b': out['w_q_b'], 'kv_a_norm': out['kv_a_norm'], 'w_kv_b': out['w_kv_b'], 'w_out': out['w_out'], 'ffn2_norm': out['ffn2_norm'], 'ffn2_w_gate': out['ffn2_w_gate'], 'ffn2_w_up': out['ffn2_w_up'], 'ffn2_w_down': out['ffn2_w_down'], 'final_norm': out['final_norm'], 'loss_target': out['loss_target'], 'm_ada_w': out['m_ada_w'], 'm_ada_b': out['m_ada_b'], 'm_ffn1_norm': out['m_ffn1_norm'], 'm_ffn1_w_gate': out['m_ffn1_w_gate'], 'm_ffn1_w_up': out['m_ffn1_w_up'], 'm_ffn1_w_down': out['m_ffn1_w_down'], 'm_mix_norm': out['m_mix_norm'], 'm_w_in': out['m_w_in'], 'm_pool_w': out['m_pool_w'], 'm_pool_scale': out['m_pool_scale'], 'm_q_a_norm': out['m_q_a_norm'], 'm_w_q_b': out['m_w_q_b'], 'm_kv_a_norm': out['m_kv_a_norm'], 'm_w_kv_b': out['m_w_kv_b'], 'm_w_out': out['m_w_out'], 'm_ffn2_norm': out['m_ffn2_norm'], 'm_ffn2_w_gate': out['m_ffn2_w_gate'], 'm_ffn2_w_up': out['m_ffn2_w_up'], 'm_ffn2_w_down': out['m_ffn2_w_down'], 'm_final_norm': out['m_final_norm'], 'v_ada_w': out['v_ada_w'], 'v_ada_b': out['v_ada_b'], 'v_ffn1_norm': out['v_ffn1_norm'], 'v_ffn1_w_gate': out['v_ffn1_w_gate'], 'v_ffn1_w_up': out['v_ffn1_w_up'], 'v_ffn1_w_down': out['v_ffn1_w_down'], 'v_mix_norm': out['v_mix_norm'], 'v_w_in': out['v_w_in'], 'v_pool_w': out['v_pool_w'], 'v_pool_scale': out['v_pool_scale'], 'v_q_a_norm': out['v_q_a_norm'], 'v_w_q_b': out['v_w_q_b'], 'v_kv_a_norm': out['v_kv_a_norm'], 'v_w_kv_b': out['v_w_kv_b'], 'v_w_out': out['v_w_out'], 'v_ffn2_norm': out['v_ffn2_norm'], 'v_ffn2_w_gate': out['v_ffn2_w_gate'], 'v_ffn2_w_up': out['v_ffn2_w_up'], 'v_ffn2_w_down': out['v_ffn2_w_down'], 'v_final_norm': out['v_final_norm']}


def _loss(weights, diff, rest, loss_target):
    with _jax.named_scope("forward"):
        args = {**rest, TWIN_DIFF_INPUT: diff, **{k: w.astype(_WEIGHT_DTYPES[k]) for k, w in weights.items()}}
        y = _forward(args)
    with _jax.named_scope("loss_head"):
        err = _jnp.square(y.astype(_jnp.float32) - loss_target)
        return 0.5 * _jnp.sum(_jnp.mean(err, axis=-1)) if err.ndim else 0.5 * err


def _adamw(w, g, m, v):
    m = ADAM_B1 * m + (1.0 - ADAM_B1) * g
    v = ADAM_B2 * v + (1.0 - ADAM_B2) * _jnp.square(g)
    m_hat = m / (1.0 - ADAM_B1 ** ADAM_STEP)
    v_hat = v / (1.0 - ADAM_B2 ** ADAM_STEP)
    delta = -ADAM_LR * (m_hat / (_jnp.sqrt(v_hat) + ADAM_EPS) + ADAM_WD * w)
    return delta, m, v


def reference(x, c, positions, ada_w, ada_b, ffn1_norm, ffn1_w_gate, ffn1_w_up, ffn1_w_down, mix_norm, w_in, pool_w, pool_scale, q_a_norm, w_q_b, kv_a_norm, w_kv_b, w_out, ffn2_norm, ffn2_w_gate, ffn2_w_up, ffn2_w_down, final_norm, loss_target, m_ada_w, m_ada_b, m_ffn1_norm, m_ffn1_w_gate, m_ffn1_w_up, m_ffn1_w_down, m_mix_norm, m_w_in, m_pool_w, m_pool_scale, m_q_a_norm, m_w_q_b, m_kv_a_norm, m_w_kv_b, m_w_out, m_ffn2_norm, m_ffn2_w_gate, m_ffn2_w_up, m_ffn2_w_down, m_final_norm, v_ada_w, v_ada_b, v_ffn1_norm, v_ffn1_w_gate, v_ffn1_w_up, v_ffn1_w_down, v_mix_norm, v_w_in, v_pool_w, v_pool_scale, v_q_a_norm, v_w_q_b, v_kv_a_norm, v_w_kv_b, v_w_out, v_ffn2_norm, v_ffn2_w_gate, v_ffn2_w_up, v_ffn2_w_down, v_final_norm):
    given = dict(x=x, c=c, positions=positions, ada_w=ada_w, ada_b=ada_b, ffn1_norm=ffn1_norm, ffn1_w_gate=ffn1_w_gate, ffn1_w_up=ffn1_w_up, ffn1_w_down=ffn1_w_down, mix_norm=mix_norm, w_in=w_in, pool_w=pool_w, pool_scale=pool_scale, q_a_norm=q_a_norm, w_q_b=w_q_b, kv_a_norm=kv_a_norm, w_kv_b=w_kv_b, w_out=w_out, ffn2_norm=ffn2_norm, ffn2_w_gate=ffn2_w_gate, ffn2_w_up=ffn2_w_up, ffn2_w_down=ffn2_w_down, final_norm=final_norm, loss_target=loss_target, m_ada_w=m_ada_w, m_ada_b=m_ada_b, m_ffn1_norm=m_ffn1_norm, m_ffn1_w_gate=m_ffn1_w_gate, m_ffn1_w_up=m_ffn1_w_up, m_ffn1_w_down=m_ffn1_w_down, m_mix_norm=m_mix_norm, m_w_in=m_w_in, m_pool_w=m_pool_w, m_pool_scale=m_pool_scale, m_q_a_norm=m_q_a_norm, m_w_q_b=m_w_q_b, m_kv_a_norm=m_kv_a_norm, m_w_kv_b=m_w_kv_b, m_w_out=m_w_out, m_ffn2_norm=m_ffn2_norm, m_ffn2_w_gate=m_ffn2_w_gate, m_ffn2_w_up=m_ffn2_w_up, m_ffn2_w_down=m_ffn2_w_down, m_final_norm=m_final_norm, v_ada_w=v_ada_w, v_ada_b=v_ada_b, v_ffn1_norm=v_ffn1_norm, v_ffn1_w_gate=v_ffn1_w_gate, v_ffn1_w_up=v_ffn1_w_up, v_ffn1_w_down=v_ffn1_w_down, v_mix_norm=v_mix_norm, v_w_in=v_w_in, v_pool_w=v_pool_w, v_pool_scale=v_pool_scale, v_q_a_norm=v_q_a_norm, v_w_q_b=v_w_q_b, v_kv_a_norm=v_kv_a_norm, v_w_kv_b=v_w_kv_b, v_w_out=v_w_out, v_ffn2_norm=v_ffn2_norm, v_ffn2_w_gate=v_ffn2_w_gate, v_ffn2_w_up=v_ffn2_w_up, v_ffn2_w_down=v_ffn2_w_down, v_final_norm=v_final_norm)
    weights = {n: given[n] for n in TWIN_WEIGHTS}
    shared = {n: given[n] for n in SHARED_INPUTS}
    per_example = {n: given[n] for n in ['x', 'c', 'positions']}
    grad_fn = _jax.value_and_grad(_loss, argnums=(0, 1))

    def one_microbatch(ex, loss_target):
        ex = dict(ex)
        diff = ex.pop(TWIN_DIFF_INPUT)
        return grad_fn(weights, diff, {**shared, **ex}, loss_target)

    if N_MICROBATCH == 1:
        loss, (grad_w, grad_x) = one_microbatch(per_example, given["loss_target"])
    else:
        def body(carry, xs):
            loss_sum, grad_sum = carry
            l_k, (gw_k, gx_k) = one_microbatch(xs[0], xs[1])
            with _jax.named_scope("update"):
                return (loss_sum + l_k, _jax.tree.map(_jnp.add, grad_sum, gw_k)), gx_k

        init = (_jnp.zeros((), _jnp.float32), _jax.tree.map(_jnp.zeros_like, weights))
        (loss, grad_w), grad_x = _jax.lax.scan(body, init, (per_example, given["loss_target"]))
    with _jax.named_scope("update"):
        delta_w, new_m, new_v = {}, {}, {}
        for n in TWIN_WEIGHTS:
            delta_w[n], new_m[n], new_v[n] = _adamw(weights[n], grad_w[n], given["m_" + n], given["v_" + n])
    return (loss, grad_x, *[grad_w[n] for n in TWIN_WEIGHTS], *[delta_w[n] for n in TWIN_WEIGHTS],
            *[new_m[n] for n in TWIN_WEIGHTS], *[new_v[n] for n in TWIN_WEIGHTS])
```

```python
import math

import numpy as np
import jax
import jax.numpy as jnp
from jax import lax
from jax.experimental import pallas as pl
from jax.experimental.pallas import tpu as pltpu

F32 = jnp.float32
BF16 = jnp.bfloat16

N_DEV = 8
D_MODEL = 1024
D_FF = 2816
POOL_WIDTH = 512
POOL_WINDOWS = (2, 4, 8, 16)
POOL_GC = 128
N_HEADS = 4
QK_NOPE = 128
QK_ROPE = 64
V_HEAD = 128
QK_HEAD = QK_NOPE + QK_ROPE
HEAD_PAD = 256
Q_LORA = 384
KV_LORA = 256
IN_COLS = POOL_WIDTH + Q_LORA + KV_LORA + QK_ROPE
IN_PAD = 1280
ROPE_THETA = 10000.0
SOFTMAX_SCALE = 1.0 / math.sqrt(QK_HEAD)
EPS = 1e-6
N_MOD = 9

ADAM_LR = 0.001
ADAM_B1 = 0.9
ADAM_B2 = 0.999
ADAM_EPS = 1e-08
ADAM_WD = 0.01
ADAM_STEP = 10

LANE = 128
VMEM_LIMIT = 56 * 1024 * 1024

FF_SH = D_FF // N_DEV
OFF_G1, OFF_U1, OFF_D1 = 0, FF_SH, 2 * FF_SH
OFF_G2, OFF_U2, OFF_D2 = 3 * FF_SH, 4 * FF_SH, 5 * FF_SH
OFF_OUT = 6 * FF_SH
OFF_KV = OFF_OUT + 128
OFF_IN = OFF_KV + 32
OFF_Q = OFF_IN + 160
ROWS_L = OFF_Q + 48
IN_SH = IN_COLS // N_DEV
Q_SH_ROWS = (N_HEADS * QK_HEAD // N_DEV) * Q_LORA // D_MODEL
KV_SH_ROWS = (N_HEADS * (QK_NOPE + V_HEAD) // N_DEV) * KV_LORA // D_MODEL


def _tile(dim, target):
    if dim <= target:
        return dim
    best = None
    for t in range(LANE, target + 1, LANE):
        if dim % t == 0:
            best = t
    assert best is not None, (dim, target)
    return best


def _params(sem):
    return pltpu.CompilerParams(dimension_semantics=sem, vmem_limit_bytes=VMEM_LIMIT)


def _mesh_pos():
    return lax.axis_index("x"), lax.axis_index("y"), lax.axis_index("c")


def _all_gather(x, name):
    m, n = x.shape

    def body(x_ref, out_ref, send_sems, recv_sems, local_sem):
        px, py, pc = _mesh_pos()
        me, sibling = (px, py, pc), (px, py, 1 - pc)
        chips = [(1 - px, py), (px, 1 - py), (1 - px, 1 - py)]

        def rows(bx, by, bc):
            return out_ref.at[pl.ds((4 * bx + 2 * by + bc) * m, m), :]

        def copy(k, block, to, src=None):
            return pltpu.make_async_remote_copy(
                src_ref=rows(*block) if src is None else src, dst_ref=rows(*block),
                send_sem=send_sems.at[k], recv_sem=recv_sems.at[k],
                device_id=to, device_id_type=pl.DeviceIdType.MESH)

        mine = pltpu.make_async_copy(x_ref, rows(*me), local_sem)
        mine.start()
        first = [copy(0, me, sibling, src=x_ref)]
        first += [copy(1 + j, me, (*chip, pc), src=x_ref) for j, chip in enumerate(chips)]
        for cp in first:
            cp.start()
        passed = [copy(4 + j, (*chip, pc), sibling) for j, chip in enumerate(chips)]
        for j, chip in enumerate(chips):
            copy(1 + j, (*chip, pc), me).wait_recv()
            passed[j].start()
        copy(0, sibling, me).wait_recv()
        for j, chip in enumerate(chips):
            copy(4 + j, (*chip, 1 - pc), me).wait_recv()
        for cp in first + passed:
            cp.wait_send()
        mine.wait()

    return pl.pallas_call(
        body, name=name,
        out_shape=jax.ShapeDtypeStruct((N_DEV * m, n), x.dtype),
        in_specs=[pl.BlockSpec(memory_space=pltpu.HBM)],
        out_specs=pl.BlockSpec(memory_space=pltpu.HBM),
        scratch_shapes=[pltpu.SemaphoreType.DMA((7,)), pltpu.SemaphoreType.DMA((7,)),
                        pltpu.SemaphoreType.DMA],
    )(x)


def _all_to_all(g, name):
    _, r, n = g.shape

    def body(g_ref, out_ref, send_sems, recv_sems, local_sem):
        px, py, pc = _mesh_pos()
        me_id = 4 * px + 2 * py + pc
        local = pltpu.make_async_copy(g_ref.at[me_id], out_ref.at[me_id], local_sem)
        local.start()
        copies = []
        for k in range(1, N_DEV):
            qx = 1 - px if k & 4 else px
            qy = 1 - py if k & 2 else py
            qc = 1 - pc if k & 1 else pc
            peer_id = 4 * qx + 2 * qy + qc
            copies.append(pltpu.make_async_remote_copy(
                src_ref=g_ref.at[peer_id], dst_ref=out_ref.at[me_id],
                send_sem=send_sems.at[k - 1], recv_sem=recv_sems.at[k - 1],
                device_id=(qx, qy, qc), device_id_type=pl.DeviceIdType.MESH))
        for cp in copies:
            cp.start()
        for cp in copies:
            cp.wait_recv()
        for cp in copies:
            cp.wait_send()
        local.wait()

    return pl.pallas_call(
        body, name=name,
        out_shape=jax.ShapeDtypeStruct(g.shape, g.dtype),
        in_specs=[pl.BlockSpec(memory_space=pltpu.HBM)],
        out_specs=pl.BlockSpec(memory_space=pltpu.HBM),
        scratch_shapes=[pltpu.SemaphoreType.DMA((7,)), pltpu.SemaphoreType.DMA((7,)),
                        pltpu.SemaphoreType.DMA],
    )(g)


def _sum_slots(recv, name):
    _, r, n = recv.shape
    tr = _row_tile(r, 512)

    def body(in_ref, out_ref):
        acc = in_ref[0].astype(F32)
        for j in range(1, N_DEV):
            acc = acc + in_ref[j].astype(F32)
        out_ref[...] = acc

    return pl.pallas_call(
        body, name=name, grid=(r // tr,),
        out_shape=jax.ShapeDtypeStruct((r, n), F32),
        in_specs=[pl.BlockSpec((N_DEV, tr, n), lambda i: (0, i, 0))],
        out_specs=pl.BlockSpec((tr, n), lambda i: (i, 0)),
        compiler_params=_params(("parallel",)),
    )(recv)


def _row_tile(rows, target):
    if rows <= target:
        return rows
    best = None
    for t in range(16, target + 1, 16):
        if rows % t == 0:
            best = t
    assert best is not None, rows
    return best


_DIMS = {"nn": ((1,), (0,)), "nt": ((1,), (1,)), "tn": ((0,), (0,))}


def _mm(a, b, mode, name, out_dtype=F32, res=None, gate=None, gate_factor=1.0, tm=512, tn=1408):
    if mode == "tn":
        kdim, m = a.shape
    else:
        m, kdim = a.shape
    n = b.shape[0] if mode == "nt" else b.shape[1]
    tm, tn = _tile(m, tm), _tile(n, tn)
    a_spec = (pl.BlockSpec((kdim, tm), lambda i, j: (0, i)) if mode == "tn"
              else pl.BlockSpec((tm, kdim), lambda i, j: (i, 0)))
    b_spec = (pl.BlockSpec((tn, kdim), lambda i, j: (j, 0)) if mode == "nt"
              else pl.BlockSpec((kdim, tn), lambda i, j: (0, j)))
    o_spec = pl.BlockSpec((tm, tn), lambda i, j: (i, j))
    dims = (_DIMS[mode], ((), ()))
    has_res, has_gate = res is not None, gate is not None

    def body(*refs):
        a_ref, b_ref = refs[0], refs[1]
        y = lax.dot_general(a_ref[...].astype(BF16), b_ref[...].astype(BF16), dims,
                            preferred_element_type=F32)
        if not has_res:
            refs[2][...] = y.astype(out_dtype)
            return
        res_ref = refs[2]
        if has_gate:
            y_ref, o_ref = refs[4], refs[5]
            y_ref[...] = y
            o_ref[...] = res_ref[...] + (gate_factor * refs[3][...]) * y
        else:
            refs[3][...] = res_ref[...] + y

    in_specs, args = [a_spec, b_spec], [a, b]
    if has_res:
        in_specs.append(o_spec)
        args.append(res)
        if has_gate:
            in_specs.append(pl.BlockSpec((1, tn), lambda i, j: (0, j)))
            args.append(gate)
            out_shape = (jax.ShapeDtypeStruct((m, n), F32), jax.ShapeDtypeStruct((m, n), F32))
            out_specs = (o_spec, o_spec)
        else:
            out_shape, out_specs = jax.ShapeDtypeStruct((m, n), F32), o_spec
    else:
        out_shape, out_specs = jax.ShapeDtypeStruct((m, n), out_dtype), o_spec
    return pl.pallas_call(
        body, name=name, grid=(m // tm, n // tn), out_shape=out_shape,
        in_specs=in_specs, out_specs=out_specs,
        compiler_params=_params(("parallel", "parallel")),
    )(*args)


def _vec_spec(width):
    return pl.BlockSpec((1, width), lambda i: (0, 0))


def _rm_fwd(x, gw, shift, scale, name):
    s, d = x.shape
    ts = _tile(s, 256)

    def body(x_ref, gw_ref, sh_ref, sc_ref, h_ref):
        xv = x_ref[...]
        r = lax.rsqrt(jnp.mean(xv * xv, axis=-1, keepdims=True) + EPS)
        y = (xv * r) * gw_ref[...]
        h_ref[...] = (y * (1 + sc_ref[...]) + sh_ref[...]).astype(BF16)

    row = pl.BlockSpec((ts, d), lambda i: (i, 0))
    return pl.pallas_call(
        body, name=name, grid=(s // ts,), out_shape=jax.ShapeDtypeStruct((s, d), BF16),
        in_specs=[row, _vec_spec(d), _vec_spec(d), _vec_spec(d)], out_specs=row,
        compiler_params=_params(("parallel",)),
    )(x, gw, shift, scale)


def _rm_bwd(dh, x, dres, gw, scale, name):
    s, d = x.shape
    ts = _tile(s, 256)

    def body(dh_ref, x_ref, dres_ref, gw_ref, sc_ref, dx_ref, dsh_ref, dsc_ref, dgw_ref):
        @pl.when(pl.program_id(0) == 0)
        def _():
            dsh_ref[...] = jnp.zeros_like(dsh_ref)
            dsc_ref[...] = jnp.zeros_like(dsc_ref)
            dgw_ref[...] = jnp.zeros_like(dgw_ref)

        xv, dhv, gwv = x_ref[...], dh_ref[...], gw_ref[...]
        r = lax.rsqrt(jnp.mean(xv * xv, axis=-1, keepdims=True) + EPS)
        xn = xv * r
        y = xn * gwv
        dsh_ref[...] += jnp.sum(dhv, axis=0, keepdims=True)
        dsc_ref[...] += jnp.sum(dhv * y, axis=0, keepdims=True)
        dy = dhv * (1 + sc_ref[...])
        dgw_ref[...] += jnp.sum(dy * xn, axis=0, keepdims=True)
        dxn = dy * gwv
        dx = r * (dxn - xn * jnp.mean(dxn * xn, axis=-1, keepdims=True))
        dx_ref[...] = dres_ref[...] + dx

    row = pl.BlockSpec((ts, d), lambda i: (i, 0))
    vec = jax.ShapeDtypeStruct((1, d), F32)
    return pl.pallas_call(
        body, name=name, grid=(s // ts,),
        out_shape=(jax.ShapeDtypeStruct((s, d), F32), vec, vec, vec),
        in_specs=[row, row, row, _vec_spec(d), _vec_spec(d)],
        out_specs=(row, _vec_spec(d), _vec_spec(d), _vec_spec(d)),
        compiler_params=_params(("arbitrary",)),
    )(dh, x, dres, gw, scale)


def _gate_bwd(dx, y, gate, factor, name):
    s, d = dx.shape
    ts = _tile(s, 256)

    def body(dx_ref, y_ref, g_ref, dy_ref, dg_ref):
        @pl.when(pl.program_id(0) == 0)
        def _():
            dg_ref[...] = jnp.zeros_like(dg_ref)

        dxv = dx_ref[...]
        dy_ref[...] = ((factor * g_ref[...]) * dxv).astype(BF16)
        dg_ref[...] += jnp.sum((factor * dxv) * y_ref[...], axis=0, keepdims=True)

    row = pl.BlockSpec((ts, d), lambda i: (i, 0))
    return pl.pallas_call(
        body, name=name, grid=(s // ts,),
        out_shape=(jax.ShapeDtypeStruct((s, d), BF16), jax.ShapeDtypeStruct((1, d), F32)),
        in_specs=[row, row, _vec_spec(d)], out_specs=(row, _vec_spec(d)),
        compiler_params=_params(("arbitrary",)),
    )(dx, y, gate)


def _swiglu_fwd(a, b, name):
    s, f = a.shape
    ts, tf = _tile(s, 256), _tile(f, 1408)

    def body(a_ref, b_ref, t_ref):
        av = a_ref[...]
        t_ref[...] = ((av * jax.nn.sigmoid(av)) * b_ref[...]).astype(BF16)

    blk = pl.BlockSpec((ts, tf), lambda i, j: (i, j))
    return pl.pallas_call(
        body, name=name, grid=(s // ts, f // tf), out_shape=jax.ShapeDtypeStruct((s, f), BF16),
        in_specs=[blk, blk], out_specs=blk, compiler_params=_params(("parallel", "parallel")),
    )(a, b)


def _swiglu_bwd(dt, a, b, name):
    s, f = a.shape
    ts, tf = _tile(s, 256), _tile(f, 1408)

    def body(dt_ref, a_ref, b_ref, da_ref, db_ref):
        av, dtv = a_ref[...], dt_ref[...]
        sg = jax.nn.sigmoid(av)
        silu = av * sg
        db_ref[...] = (dtv * silu).astype(BF16)
        da_ref[...] = ((dtv * b_ref[...]) * (sg * (1 + av * (1 - sg)))).astype(BF16)

    blk = pl.BlockSpec((ts, tf), lambda i, j: (i, j))
    out = jax.ShapeDtypeStruct((s, f), BF16)
    return pl.pallas_call(
        body, name=name, grid=(s // ts, f // tf), out_shape=(out, out),
        in_specs=[blk, blk, blk], out_specs=(blk, blk),
        compiler_params=_params(("parallel", "parallel")),
    )(dt, a, b)


def _pool_counts(s):
    return (lax.broadcasted_iota(jnp.int32, (s, POOL_GC), 0))


def _pool_fwd(z, pool_w, pool_scale, name):
    s = z.shape[0]

    def body(u_ref, w_ref, sc_ref, y_ref, diff_ref):
        t = lax.broadcasted_iota(jnp.int32, (s, POOL_GC), 0)
        for g, win in enumerate(POOL_WINDOWS):
            cols = slice(g * POOL_GC, (g + 1) * POOL_GC)
            u = u_ref[:, cols]
            acc, step = u, 1
            while step < win:
                acc = acc + jnp.where(t >= step, pltpu.roll(acc, step, 0), 0.0)
                step *= 2
            cnt = jnp.minimum(t + 1, win).astype(F32)
            diff = acc / cnt - u
            diff_ref[:, cols] = diff
            ypre = jnp.dot(diff.astype(BF16), w_ref[g].astype(BF16), preferred_element_type=F32)
            y_ref[:, cols] = (ypre * sc_ref[:, cols]).astype(BF16)

    return pl.pallas_call(
        body, name=name, grid=(1,),
        out_shape=(jax.ShapeDtypeStruct((s, POOL_WIDTH), BF16), jax.ShapeDtypeStruct((s, POOL_WIDTH), F32)),
        in_specs=[pl.BlockSpec((s, POOL_WIDTH), lambda i: (0, 0)),
                  pl.BlockSpec(pool_w.shape, lambda i: (0, 0, 0)),
                  pl.BlockSpec((1, POOL_WIDTH), lambda i: (0, 0))],
        out_specs=(pl.BlockSpec((s, POOL_WIDTH), lambda i: (0, 0)),
                   pl.BlockSpec((s, POOL_WIDTH), lambda i: (0, 0))),
        compiler_params=_params(("arbitrary",)),
    )(z, pool_w, pool_scale)


def _pool_bwd(dycat, diff, pool_w, pool_scale, name):
    s = diff.shape[0]

    def body(dy_ref, diff_ref, w_ref, sc_ref, du_ref, dw_ref, dsc_ref):
        t = lax.broadcasted_iota(jnp.int32, (s, POOL_GC), 0)
        for g, win in enumerate(POOL_WINDOWS):
            cols = slice(g * POOL_GC, (g + 1) * POOL_GC)
            dy, dfb, wb = dy_ref[:, cols], diff_ref[:, cols].astype(BF16), w_ref[g].astype(BF16)
            ypre = jnp.dot(dfb, wb, preferred_element_type=F32)
            dsc_ref[:, cols] = jnp.sum(dy * ypre, axis=0, keepdims=True)
            dypre = (dy * sc_ref[:, cols]).astype(BF16)
            ddiff = lax.dot_general(dypre, wb, (((1,), (1,)), ((), ())), preferred_element_type=F32)
            dw_ref[g] = lax.dot_general(dfb, dypre, (((0,), (0,)), ((), ())), preferred_element_type=F32)
            cnt = jnp.minimum(t + 1, win).astype(F32)
            acc, step = ddiff / cnt, 1
            while step < win:
                acc = acc + jnp.where(t < s - step, pltpu.roll(acc, s - step, 0), 0.0)
                step *= 2
            du_ref[:, cols] = acc - ddiff

    full = pl.BlockSpec((s, POOL_WIDTH), lambda i: (0, 0))
    return pl.pallas_call(
        body, name=name, grid=(1,),
        out_shape=(jax.ShapeDtypeStruct((s, POOL_WIDTH), F32),
                   jax.ShapeDtypeStruct(pool_w.shape, F32),
                   jax.ShapeDtypeStruct((1, POOL_WIDTH), F32)),
        in_specs=[full, full, pl.BlockSpec(pool_w.shape, lambda i: (0, 0, 0)),
                  pl.BlockSpec((1, POOL_WIDTH), lambda i: (0, 0))],
        out_specs=(full, pl.BlockSpec(pool_w.shape, lambda i: (0, 0, 0)),
                   pl.BlockSpec((1, POOL_WIDTH), lambda i: (0, 0))),
        compiler_params=_params(("arbitrary",)),
    )(dycat, diff, pool_w, pool_scale)


def _rope_tables(positions, name):
    s = positions.shape[0]
    ts = _tile(s, 512)
    freq = 1.0 / (ROPE_THETA ** (np.arange(0, QK_ROPE, 2, dtype=np.float32) / QK_ROPE))
    table = np.zeros((1, LANE), np.float32)
    table[0, :QK_ROPE // 2] = freq
    table[0, QK_ROPE // 2:QK_ROPE] = freq

    def body(pos_ref, f_ref, cos_ref, sin_ref):
        ang = pos_ref[...].astype(F32) * f_ref[...]
        cos_ref[...] = jnp.cos(ang)
        sin_ref[...] = jnp.sin(ang)

    out = jax.ShapeDtypeStruct((s, LANE), F32)
    blk = pl.BlockSpec((ts, LANE), lambda i: (i, 0))
    return pl.pallas_call(
        body, name=name, grid=(s // ts,), out_shape=(out, out),
        in_specs=[pl.BlockSpec((ts, 1), lambda i: (i, 0)), _vec_spec(LANE)], out_specs=(blk, blk),
        compiler_params=_params(("parallel",)),
    )(positions, jnp.asarray(table))


def _lane_mod64_low(shape):
    return (lax.broadcasted_iota(jnp.int32, shape, 1) % QK_ROPE) < (QK_ROPE // 2)


def _rope(x, cos, sin):
    rot = jnp.where(_lane_mod64_low(x.shape), -pltpu.roll(x, LANE - 32, 1), pltpu.roll(x, 32, 1))
    return x * cos + rot * sin


def _rope_t(dy, cos, sin):
    w = dy * sin
    rot_t = jnp.where(_lane_mod64_low(dy.shape), pltpu.roll(w, LANE - 32, 1), -pltpu.roll(w, 32, 1))
    return dy * cos + rot_t


def _plain_rms(x, g):
    r = lax.rsqrt(jnp.mean(x * x, axis=-1, keepdims=True) + EPS)
    return (x * r) * g, x * r, r


O_Q, O_KV, O_KR = POOL_WIDTH, POOL_WIDTH + Q_LORA, POOL_WIDTH + Q_LORA + KV_LORA


def _qkv_fwd(z, qn, kvn, wq, wkv, cos, sin, name):
    s = z.shape[0]
    ts = _tile(s, 256)

    def body(z_ref, qn_ref, kvn_ref, wq_ref, wkv_ref, cos_ref, sin_ref, q_ref, k_ref, v_ref, cqn_ref, ckvn_ref):
        cosv, sinv = cos_ref[...], sin_ref[...]
        cqn = _plain_rms(z_ref[:, O_Q:O_KV], qn_ref[...])[0].astype(BF16)
        ckvn = _plain_rms(z_ref[:, O_KV:O_KR], kvn_ref[...])[0].astype(BF16)
        cqn_ref[...] = cqn
        ckvn_ref[...] = ckvn
        nt = (((1,), (1,)), ((), ()))
        q = lax.dot_general(cqn, wq_ref[...], nt, preferred_element_type=F32)
        kv = lax.dot_general(ckvn, wkv_ref[...], nt, preferred_element_type=F32)
        kr = _rope(z_ref[:, O_KR:IN_PAD], cosv, sinv).astype(BF16)
        for h in range(N_HEADS):
            o = h * HEAD_PAD
            q_ref[:, o:o + QK_NOPE] = q[:, o:o + QK_NOPE].astype(BF16)
            q_ref[:, o + QK_NOPE:o + HEAD_PAD] = _rope(q[:, o + QK_NOPE:o + HEAD_PAD], cosv, sinv).astype(BF16)
            k_ref[:, o:o + QK_NOPE] = kv[:, o:o + QK_NOPE].astype(BF16)
            k_ref[:, o + QK_NOPE:o + HEAD_PAD] = kr
            v_ref[:, h * V_HEAD:(h + 1) * V_HEAD] = kv[:, o + QK_NOPE:o + HEAD_PAD].astype(BF16)

    def row(w):
        return pl.BlockSpec((ts, w), lambda i: (i, 0))

    def whole(arr):
        return pl.BlockSpec(arr.shape, lambda i: (0, 0))

    hp = N_HEADS * HEAD_PAD
    return pl.pallas_call(
        body, name=name, grid=(s // ts,),
        out_shape=(jax.ShapeDtypeStruct((s, hp), BF16), jax.ShapeDtypeStruct((s, hp), BF16),
                   jax.ShapeDtypeStruct((s, N_HEADS * V_HEAD), BF16),
                   jax.ShapeDtypeStruct((s, Q_LORA), BF16), jax.ShapeDtypeStruct((s, KV_LORA), BF16)),
        in_specs=[row(IN_PAD), whole(qn), whole(kvn), whole(wq), whole(wkv), row(LANE), row(LANE)],
        out_specs=(row(hp), row(hp), row(N_HEADS * V_HEAD), row(Q_LORA), row(KV_LORA)),
        compiler_params=_params(("parallel",)),
    )(z, qn, kvn, wq, wkv, cos, sin)


def _qkv_bwd(dq, dk, dv, du, z, qn, kvn, wq, wkv, cos, sin, name):
    s = z.shape[0]
    ts = _tile(s, 256)

    def norm_bwd(x, g, dy):
        _, xn, r = _plain_rms(x, g)
        dxn = dy * g
        return r * (dxn - xn * jnp.mean(dxn * xn, axis=-1, keepdims=True)), jnp.sum(dy * xn, axis=0, keepdims=True)

    def body(dq_ref, dk_ref, dv_ref, du_ref, z_ref, qn_ref, kvn_ref, wq_ref, wkv_ref, cos_ref, sin_ref,
             dz_ref, dqb_ref, dkvb_ref, dqn_ref, dkvn_ref):
        @pl.when(pl.program_id(0) == 0)
        def _():
            dqn_ref[...] = jnp.zeros_like(dqn_ref)
            dkvn_ref[...] = jnp.zeros_like(dkvn_ref)

        cosv, sinv = cos_ref[...], sin_ref[...]
        dkr = jnp.zeros((ts, LANE), F32)
        for h in range(N_HEADS):
            o = h * HEAD_PAD
            dqb_ref[:, o:o + QK_NOPE] = dq_ref[:, o:o + QK_NOPE].astype(BF16)
            dqb_ref[:, o + QK_NOPE:o + HEAD_PAD] = _rope_t(dq_ref[:, o + QK_NOPE:o + HEAD_PAD], cosv, sinv).astype(BF16)
            dkvb_ref[:, o:o + QK_NOPE] = dk_ref[:, o:o + QK_NOPE].astype(BF16)
            dkvb_ref[:, o + QK_NOPE:o + HEAD_PAD] = dv_ref[:, h * V_HEAD:(h + 1) * V_HEAD].astype(BF16)
            dkr = dkr + dk_ref[:, o + QK_NOPE:o + HEAD_PAD]
        dcqn = jnp.dot(dqb_ref[...], wq_ref[...], preferred_element_type=F32)
        dckvn = jnp.dot(dkvb_ref[...], wkv_ref[...], preferred_element_type=F32)
        dcq, dqn = norm_bwd(z_ref[:, O_Q:O_KV], qn_ref[...], dcqn)
        dckv, dkvn = norm_bwd(z_ref[:, O_KV:O_KR], kvn_ref[...], dckvn)
        dqn_ref[...] += dqn
        dkvn_ref[...] += dkvn
        dz_ref[:, 0:O_Q] = du_ref[...].astype(BF16)
        dz_ref[:, O_Q:O_KV] = dcq.astype(BF16)
        dz_ref[:, O_KV:O_KR] = dckv.astype(BF16)
        dz_ref[:, O_KR:IN_PAD] = _rope_t(dkr, cosv, sinv).astype(BF16)

    def row(w):
        return pl.BlockSpec((ts, w), lambda i: (i, 0))

    def whole(arr):
        return pl.BlockSpec(arr.shape, lambda i: (0, 0))

    hp = N_HEADS * HEAD_PAD
    return pl.pallas_call(
        body, name=name, grid=(s // ts,),
        out_shape=(jax.ShapeDtypeStruct((s, IN_PAD), BF16), jax.ShapeDtypeStruct((s, hp), BF16),
                   jax.ShapeDtypeStruct((s, hp), BF16),
                   jax.ShapeDtypeStruct((1, Q_LORA), F32), jax.ShapeDtypeStruct((1, KV_LORA), F32)),
        in_specs=[row(hp), row(hp), row(N_HEADS * V_HEAD), row(POOL_WIDTH), row(IN_PAD),
                  whole(qn), whole(kvn), whole(wq), whole(wkv), row(LANE), row(LANE)],
        out_specs=(row(IN_PAD), row(hp), row(hp), whole(qn), whole(kvn)),
        compiler_params=_params(("arbitrary",)),
    )(dq, dk, dv, du, z, qn, kvn, wq, wkv, cos, sin)


def _causal_scores(q, k, i, tq, s):
    sc = lax.dot_general(q, k, (((1,), (1,)), ((), ())), preferred_element_type=F32) * SOFTMAX_SCALE
    qpos = i * tq + lax.broadcasted_iota(jnp.int32, (tq, s), 0)
    kpos = lax.broadcasted_iota(jnp.int32, (tq, s), 1)
    return jnp.where(qpos >= kpos, sc, -jnp.inf)


def _attn_fwd(q, k, v, name):
    s = q.shape[0]
    tq = _tile(s, 256)

    def body(q_ref, k_ref, v_ref, o_ref, lse_ref):
        sc = _causal_scores(q_ref[...], k_ref[...], pl.program_id(1), tq, s)
        mx = jnp.max(sc, axis=-1, keepdims=True)
        p = jnp.exp(sc - mx)
        den = jnp.sum(p, axis=-1, keepdims=True)
        o_ref[...] = jnp.dot((p / den).astype(BF16), v_ref[...], preferred_element_type=F32)
        lse_ref[...] = mx + jnp.log(den)

    return pl.pallas_call(
        body, name=name, grid=(N_HEADS, s // tq),
        out_shape=(jax.ShapeDtypeStruct((s, N_HEADS * V_HEAD), F32), jax.ShapeDtypeStruct((N_HEADS, s, 1), F32)),
        in_specs=[pl.BlockSpec((tq, HEAD_PAD), lambda h, i: (i, h)),
                  pl.BlockSpec((s, HEAD_PAD), lambda h, i: (0, h)),
                  pl.BlockSpec((s, V_HEAD), lambda h, i: (0, h))],
        out_specs=(pl.BlockSpec((tq, V_HEAD), lambda h, i: (i, h)),
                   pl.BlockSpec((None, tq, 1), lambda h, i: (h, i, 0))),
        compiler_params=_params(("parallel", "parallel")),
    )(q, k, v)


def _attn_bwd(q, k, v, lse, dycat, name):
    s = q.shape[0]
    tq = _tile(s, 256)
    tn_dims = (((0,), (0,)), ((), ()))

    def body(q_ref, k_ref, v_ref, lse_ref, do_ref, dq_ref, dk_ref, dv_ref):
        @pl.when(pl.program_id(1) == 0)
        def _():
            dk_ref[...] = jnp.zeros_like(dk_ref)
            dv_ref[...] = jnp.zeros_like(dv_ref)

        qv, kv_, dob = q_ref[...], k_ref[...], do_ref[...].astype(BF16)
        sc = _causal_scores(qv, kv_, pl.program_id(1), tq, s)
        p = jnp.exp(sc - lse_ref[...])
        dp = lax.dot_general(dob, v_ref[...], (((1,), (1,)), ((), ())), preferred_element_type=F32)
        ds = (p * (dp - jnp.sum(dp * p, axis=-1, keepdims=True)) * SOFTMAX_SCALE).astype(BF16)
        dq_ref[...] = jnp.dot(ds, kv_, preferred_element_type=F32)
        dk_ref[...] += lax.dot_general(ds, qv, tn_dims, preferred_element_type=F32)
        dv_ref[...] += lax.dot_general(p.astype(BF16), dob, tn_dims, preferred_element_type=F32)

    n_pool_blocks = POOL_WIDTH // V_HEAD
    return pl.pallas_call(
        body, name=name, grid=(N_HEADS, s // tq),
        out_shape=(jax.ShapeDtypeStruct((s, N_HEADS * HEAD_PAD), F32),
                   jax.ShapeDtypeStruct((s, N_HEADS * HEAD_PAD), F32),
                   jax.ShapeDtypeStruct((s, N_HEADS * V_HEAD), F32)),
        in_specs=[pl.BlockSpec((tq, HEAD_PAD), lambda h, i: (i, h)),
                  pl.BlockSpec((s, HEAD_PAD), lambda h, i: (0, h)),
                  pl.BlockSpec((s, V_HEAD), lambda h, i: (0, h)),
                  pl.BlockSpec((None, tq, 1), lambda h, i: (h, i, 0)),
                  pl.BlockSpec((tq, V_HEAD), lambda h, i: (i, n_pool_blocks + h))],
        out_specs=(pl.BlockSpec((tq, HEAD_PAD), lambda h, i: (i, h)),
                   pl.BlockSpec((s, HEAD_PAD), lambda h, i: (0, h)),
                   pl.BlockSpec((s, V_HEAD), lambda h, i: (0, h))),
        compiler_params=_params(("parallel", "arbitrary")),
    )(q, k, v, lse, dycat)


def _loss_head(x, gw, target, name):
    s, d = x.shape
    ts = _tile(s, 256)

    def body(x_ref, gw_ref, tgt_ref, loss_ref, dx_ref, dgw_ref):
        @pl.when(pl.program_id(0) == 0)
        def _():
            loss_ref[...] = jnp.zeros_like(loss_ref)
            dgw_ref[...] = jnp.zeros_like(dgw_ref)

        xv, gwv = x_ref[...], gw_ref[...]
        r = lax.rsqrt(jnp.mean(xv * xv, axis=-1, keepdims=True) + EPS)
        xn = xv * r
        err = xn * gwv - tgt_ref[...]
        loss_ref[...] += 0.5 * jnp.sum(jnp.mean(err * err, axis=-1, keepdims=True))
        dy = err / d
        dgw_ref[...] += jnp.sum(dy * xn, axis=0, keepdims=True)
        dxn = dy * gwv
        dx_ref[...] = r * (dxn - xn * jnp.mean(dxn * xn, axis=-1, keepdims=True))

    row = pl.BlockSpec((ts, d), lambda i: (i, 0))
    return pl.pallas_call(
        body, name=name, grid=(s // ts,),
        out_shape=(jax.ShapeDtypeStruct((8, LANE), F32), jax.ShapeDtypeStruct((s, d), F32),
                   jax.ShapeDtypeStruct((1, d), F32)),
        in_specs=[row, _vec_spec(d), row],
        out_specs=(pl.BlockSpec((8, LANE), lambda i: (0, 0)), row, _vec_spec(d)),
        compiler_params=_params(("arbitrary",)),
    )(x, gw, target)


def _ada_mod(c_all, ada_w, ada_b, name):
    nl, d, cols = ada_w.shape

    def body(c_ref, w_ref, b_ref, o_ref):
        cv = c_ref[...]
        act = (cv * jax.nn.sigmoid(cv)).astype(BF16)
        o_ref[...] = jnp.dot(act, w_ref[...].astype(BF16), preferred_element_type=F32) + b_ref[...]

    return pl.pallas_call(
        body, name=name, grid=(nl,), out_shape=jax.ShapeDtypeStruct((nl, N_DEV, cols), F32),
        in_specs=[pl.BlockSpec((N_DEV, d), lambda l: (0, 0)),
                  pl.BlockSpec((None, d, cols), lambda l: (l, 0, 0)),
                  pl.BlockSpec((None, 1, cols), lambda l: (l, 0, 0))],
        out_specs=pl.BlockSpec((None, N_DEV, cols), lambda l: (l, 0, 0)),
        compiler_params=_params(("parallel",)),
    )(c_all, ada_w, ada_b)


def _ada_grad(c_pad, dmod_pad, name):
    nl, kpad, cols = dmod_pad.shape
    d = c_pad.shape[1]

    def body(c_ref, dm_ref, o_ref):
        cv = c_ref[...]
        act = (cv * jax.nn.sigmoid(cv)).astype(BF16)
        o_ref[...] = lax.dot_general(act, dm_ref[...].astype(BF16), (((0,), (0,)), ((), ())),
                                     preferred_element_type=F32)

    return pl.pallas_call(
        body, name=name, grid=(nl,), out_shape=jax.ShapeDtypeStruct((nl, d, cols), F32),
        in_specs=[pl.BlockSpec((kpad, d), lambda l: (0, 0)),
                  pl.BlockSpec((None, kpad, cols), lambda l: (l, 0, 0))],
        out_specs=pl.BlockSpec((None, d, cols), lambda l: (l, 0, 0)),
        compiler_params=_params(("parallel",)),
    )(c_pad, dmod_pad)


def _adamw(w, g, m, v, name):
    rows, cols = w.shape
    tr = _row_tile(rows, 256)

    def body(w_ref, g_ref, m_ref, v_ref, d_ref, nm_ref, nv_ref):
        gv = g_ref[...]
        nm = ADAM_B1 * m_ref[...] + (1.0 - ADAM_B1) * gv
        nv = ADAM_B2 * v_ref[...] + (1.0 - ADAM_B2) * (gv * gv)
        m_hat = nm / (1.0 - ADAM_B1 ** ADAM_STEP)
        v_hat = nv / (1.0 - ADAM_B2 ** ADAM_STEP)
        d_ref[...] = -ADAM_LR * (m_hat / (jnp.sqrt(v_hat) + ADAM_EPS) + ADAM_WD * w_ref[...])
        nm_ref[...] = nm
        nv_ref[...] = nv

    blk = pl.BlockSpec((tr, cols), lambda i: (i, 0))
    out = jax.ShapeDtypeStruct((rows, cols), F32)
    return pl.pallas_call(
        body, name=name, grid=(rows // tr,), out_shape=(out, out, out),
        in_specs=[blk, blk, blk, blk], out_specs=(blk, blk, blk),
        compiler_params=_params(("parallel",)),
    )(w, g, m, v)


def _adamw_nd(w, g, m, v, name):
    shape = w.shape
    flat = (lambda t: t.reshape(1, -1)) if w.ndim == 1 else (lambda t: t.reshape(-1, shape[-1]))
    return tuple(t.reshape(shape) for t in _adamw(flat(w), flat(g), flat(m), flat(v), name))


def _pad_rows(t, rows):
    return jnp.pad(t, ((0, rows - t.shape[0]), (0, 0)))


def _pack_shard_layer(l, wts):
    def tr(name):
        return wts[name][l].astype(BF16).T

    parts = [tr("ffn1_w_gate"), tr("ffn1_w_up"), wts["ffn1_w_down"][l].astype(BF16),
             tr("ffn2_w_gate"), tr("ffn2_w_up"), wts["ffn2_w_down"][l].astype(BF16),
             wts["w_out"][l].astype(BF16),
             tr("w_kv_b").reshape(KV_SH_ROWS, D_MODEL),
             _pad_rows(tr("w_in"), 160),
             _pad_rows(tr("w_q_b").reshape(Q_SH_ROWS, D_MODEL), 48)]
    return jnp.concatenate(parts, axis=0)


def _unpack_full_layer(buf, l):
    def piece(off, n):
        return buf[:, l, off:off + n]

    w = {
        "g1": piece(OFF_G1, FF_SH).reshape(D_FF, D_MODEL), "u1": piece(OFF_U1, FF_SH).reshape(D_FF, D_MODEL),
        "d1": piece(OFF_D1, FF_SH).reshape(D_FF, D_MODEL),
        "g2": piece(OFF_G2, FF_SH).reshape(D_FF, D_MODEL), "u2": piece(OFF_U2, FF_SH).reshape(D_FF, D_MODEL),
        "d2": piece(OFF_D2, FF_SH).reshape(D_FF, D_MODEL),
        "out": piece(OFF_OUT, 128).reshape(D_MODEL, D_MODEL),
        "kv": piece(OFF_KV, KV_SH_ROWS).reshape(N_HEADS * HEAD_PAD, KV_LORA),
        "in": _pad_rows(piece(OFF_IN, IN_SH).reshape(IN_COLS, D_MODEL), IN_PAD),
    }
    wq = piece(OFF_Q, Q_SH_ROWS).reshape(N_HEADS, QK_HEAD, Q_LORA)
    w["q"] = jnp.pad(wq, ((0, 0), (0, HEAD_PAD - QK_HEAD), (0, 0))).reshape(N_HEADS * HEAD_PAD, Q_LORA)
    return w


def _pack_grads_layer(gr):
    def cut(t, n):
        return t.reshape(N_DEV, n, D_MODEL)

    gq = gr["q"].reshape(N_HEADS, HEAD_PAD, Q_LORA)[:, :QK_HEAD].reshape(N_DEV, Q_SH_ROWS, D_MODEL)
    parts = [cut(gr["g1"], FF_SH), cut(gr["u1"], FF_SH), cut(gr["d1"], FF_SH),
             cut(gr["g2"], FF_SH), cut(gr["u2"], FF_SH), cut(gr["d2"], FF_SH),
             cut(gr["out"], 128), gr["kv"].reshape(N_DEV, KV_SH_ROWS, D_MODEL),
             jnp.pad(cut(gr["in"][:IN_COLS], IN_SH), ((0, 0), (0, 160 - IN_SH), (0, 0))),
             jnp.pad(gq, ((0, 0), (0, 48 - Q_SH_ROWS), (0, 0)))]
    return jnp.concatenate(parts, axis=1)


def _unpack_grad_shards(gs):
    nl = gs.shape[0]

    def tr(off, n):
        return gs[:, off:off + n].transpose(0, 2, 1)

    return {
        "ffn1_w_gate": tr(OFF_G1, FF_SH), "ffn1_w_up": tr(OFF_U1, FF_SH), "ffn1_w_down": gs[:, OFF_D1:OFF_D1 + FF_SH],
        "ffn2_w_gate": tr(OFF_G2, FF_SH), "ffn2_w_up": tr(OFF_U2, FF_SH), "ffn2_w_down": gs[:, OFF_D2:OFF_D2 + FF_SH],
        "w_out": gs[:, OFF_OUT:OFF_OUT + 128],
        "w_kv_b": gs[:, OFF_KV:OFF_KV + KV_SH_ROWS].reshape(nl, -1, KV_LORA).transpose(0, 2, 1),
        "w_in": tr(OFF_IN, IN_SH),
        "w_q_b": gs[:, OFF_Q:OFF_Q + Q_SH_ROWS].reshape(nl, -1, Q_LORA).transpose(0, 2, 1),
    }


def _small_layout(nl):
    names = [("dmod", nl * N_MOD), ("ffn1_norm", nl), ("mix_norm", nl), ("ffn2_norm", nl), ("q_a_norm", nl),
             ("kv_a_norm", nl), ("pool_scale", nl), ("final_norm", 1), ("loss", 1),
             ("pool_w", nl * 4 * POOL_GC * POOL_GC // D_MODEL)]
    off, table = 0, {}
    for name, n in names:
        table[name] = (off, n)
        off += -(-n // 8) * 8
    return table, off


def _to_rows(t, width=D_MODEL):
    n, w = t.shape
    return jnp.pad(t, ((0, -(-n // 8) * 8 - n), (0, width - w)))


def kernel(x, c, positions, ada_w, ada_b, ffn1_norm, ffn1_w_gate, ffn1_w_up, ffn1_w_down, mix_norm, w_in, pool_w, pool_scale, q_a_norm, w_q_b, kv_a_norm, w_kv_b, w_out, ffn2_norm, ffn2_w_gate, ffn2_w_up, ffn2_w_down, final_norm, loss_target, m_ada_w, m_ada_b, m_ffn1_norm, m_ffn1_w_gate, m_ffn1_w_up, m_ffn1_w_down, m_mix_norm, m_w_in, m_pool_w, m_pool_scale, m_q_a_norm, m_w_q_b, m_kv_a_norm, m_w_kv_b, m_w_out, m_ffn2_norm, m_ffn2_w_gate, m_ffn2_w_up, m_ffn2_w_down, m_final_norm, v_ada_w, v_ada_b, v_ffn1_norm, v_ffn1_w_gate, v_ffn1_w_up, v_ffn1_w_down, v_mix_norm, v_w_in, v_pool_w, v_pool_scale, v_q_a_norm, v_w_q_b, v_kv_a_norm, v_w_kv_b, v_w_out, v_ffn2_norm, v_ffn2_w_gate, v_ffn2_w_up, v_ffn2_w_down, v_final_norm):
    wts = dict(ada_w=ada_w, ada_b=ada_b, ffn1_norm=ffn1_norm, ffn1_w_gate=ffn1_w_gate, ffn1_w_up=ffn1_w_up,
               ffn1_w_down=ffn1_w_down, mix_norm=mix_norm, w_in=w_in, pool_w=pool_w, pool_scale=pool_scale,
               q_a_norm=q_a_norm, w_q_b=w_q_b, kv_a_norm=kv_a_norm, w_kv_b=w_kv_b, w_out=w_out,
               ffn2_norm=ffn2_norm, ffn2_w_gate=ffn2_w_gate, ffn2_w_up=ffn2_w_up, ffn2_w_down=ffn2_w_down,
               final_norm=final_norm)
    mom_m = dict(ada_w=m_ada_w, ada_b=m_ada_b, ffn1_norm=m_ffn1_norm, ffn1_w_gate=m_ffn1_w_gate,
                 ffn1_w_up=m_ffn1_w_up, ffn1_w_down=m_ffn1_w_down, mix_norm=m_mix_norm, w_in=m_w_in,
                 pool_w=m_pool_w, pool_scale=m_pool_scale, q_a_norm=m_q_a_norm, w_q_b=m_w_q_b,
                 kv_a_norm=m_kv_a_norm, w_kv_b=m_w_kv_b, w_out=m_w_out, ffn2_norm=m_ffn2_norm,
                 ffn2_w_gate=m_ffn2_w_gate, ffn2_w_up=m_ffn2_w_up, ffn2_w_down=m_ffn2_w_down,
                 final_norm=m_final_norm)
    mom_v = dict(ada_w=v_ada_w, ada_b=v_ada_b, ffn1_norm=v_ffn1_norm, ffn1_w_gate=v_ffn1_w_gate,
                 ffn1_w_up=v_ffn1_w_up, ffn1_w_down=v_ffn1_w_down, mix_norm=v_mix_norm, w_in=v_w_in,
                 pool_w=v_pool_w, pool_scale=v_pool_scale, q_a_norm=v_q_a_norm, w_q_b=v_w_q_b,
                 kv_a_norm=v_kv_a_norm, w_kv_b=v_w_kv_b, w_out=v_w_out, ffn2_norm=v_ffn2_norm,
                 ffn2_w_gate=v_ffn2_w_gate, ffn2_w_up=v_ffn2_w_up, ffn2_w_down=v_ffn2_w_down,
                 final_norm=v_final_norm)
    order = list(wts)
    nl = ada_w.shape[0]
    seq = x.shape[1]
    me = 4 * lax.axis_index("x") + 2 * lax.axis_index("y") + lax.axis_index("c")
    ada_cols = ada_w.shape[2]

    c_all = _all_gather(jnp.broadcast_to(c, (8, D_MODEL)), "gather_c")[::8]
    packed = jnp.concatenate([_pack_shard_layer(l, wts) for l in range(nl)], axis=0)
    full = _all_gather(packed, "gather_weights").reshape(N_DEV, nl, ROWS_L, D_MODEL)

    ada_b_mine = lax.dynamic_slice_in_dim(ada_b, me * ada_cols, ada_cols, axis=1).reshape(nl, 1, ada_cols)
    mod_part = _ada_mod(c_all, ada_w, ada_b_mine, "ada_mod")
    mod_all = _all_gather(mod_part.reshape(nl * N_DEV, ada_cols), "gather_mod")
    mod_all = mod_all.reshape(N_DEV, nl, N_DEV, ada_cols)
    mod = lax.dynamic_index_in_dim(mod_all, me, axis=2, keepdims=False)
    mod = mod.transpose(1, 0, 2).reshape(nl, N_MOD, 1, D_MODEL)

    cos, sin = _rope_tables(positions.reshape(seq, 1), "rope_tables")

    def vec(t):
        return t.reshape(1, -1)

    xs = x.reshape(seq, D_MODEL)
    saved = []
    for l in range(nl):
        w = _unpack_full_layer(full, l)
        sv = {"w": w}

        def ffn_fwd(xin, norm, k0, wg, wu, wd, tag):
            h = _rm_fwd(xin, vec(norm), mod[l, k0], mod[l, k0 + 1], "rm_fwd")
            a = _mm(h, wg, "nt", "ffn_gate")
            b = _mm(h, wu, "nt", "ffn_up")
            t = _swiglu_fwd(a, b, "swiglu_fwd")
            y, xout = _mm(t, wd, "nn", "ffn_down", res=xin, gate=mod[l, k0 + 2], gate_factor=0.5)
            sv[tag] = dict(x=xin, h=h, a=a, b=b, t=t, y=y)
            return xout

        xs = ffn_fwd(xs, ffn1_norm[l], 0, w["g1"], w["u1"], w["d1"], "f1")

        h2 = _rm_fwd(xs, vec(mix_norm[l]), mod[l, 3], mod[l, 4], "rm_fwd")
        z = _mm(h2, w["in"], "nt", "mix_in")
        y_pool, diff = _pool_fwd(z, pool_w[l], vec(pool_scale[l]), "pool_fwd")
        q, k, v, cqn, ckvn = _qkv_fwd(z, vec(q_a_norm[l]), vec(kv_a_norm[l]), w["q"], w["kv"], cos, sin, "qkv_fwd")
        o, lse = _attn_fwd(q, k, v, "attn_fwd")
        ycat = jnp.concatenate([y_pool, o.astype(BF16)], axis=1)
        y2, xmix = _mm(ycat, w["out"], "nn", "mix_out", res=xs, gate=mod[l, 5], gate_factor=1.0)
        sv["mix"] = dict(x=xs, h=h2, z=z, diff=diff, q=q, k=k, v=v, cqn=cqn, ckvn=ckvn, lse=lse, ycat=ycat, y=y2)
        xs = xmix

        xs = ffn_fwd(xs, ffn2_norm[l], 6, w["g2"], w["u2"], w["d2"], "f2")
        saved.append(sv)

    loss_part, dx, d_final = _loss_head(xs, vec(final_norm), loss_target.reshape(seq, D_MODEL), "loss_head")

    small = {name: [None] * nl for name in ("ffn1_norm", "mix_norm", "ffn2_norm", "q_a_norm", "kv_a_norm",
                                            "pool_scale", "pool_w", "dmod")}
    grads_packed = [None] * nl
    for l in reversed(range(nl)):
        sv = saved[l]
        w = sv["w"]
        dmod = [None] * N_MOD
        gr = {}

        def ffn_bwd(dxin, s_, norm, k0, wg, wu, wd, tag):
            dy, dmod[k0 + 2] = _gate_bwd(dxin, s_["y"], mod[l, k0 + 2], 0.5, "gate_bwd")
            dt = _mm(dy, wd, "nt", "ffn_down_dx")
            da, db = _swiglu_bwd(dt, s_["a"], s_["b"], "swiglu_bwd")
            gr["d" + tag] = _mm(s_["t"], dy, "tn", "ffn_down_dw", out_dtype=BF16, tm=256)
            gr["g" + tag] = _mm(da, s_["h"], "tn", "ffn_gate_dw", out_dtype=BF16, tm=256)
            gr["u" + tag] = _mm(db, s_["h"], "tn", "ffn_up_dw", out_dtype=BF16, tm=256)
            dh = _mm(da, wg, "nn", "ffn_gate_dx")
            dh = _mm(db, wu, "nn", "ffn_up_dx", res=dh)
            dxo, dmod[k0], dmod[k0 + 1], dnorm = _rm_bwd(dh, s_["x"], dxin, vec(norm), mod[l, k0 + 1], "rm_bwd")
            return dxo, dnorm

        dx, small["ffn2_norm"][l] = ffn_bwd(dx, sv["f2"], ffn2_norm[l], 6, w["g2"], w["u2"], w["d2"], "2")

        s_ = sv["mix"]
        dy, dmod[5] = _gate_bwd(dx, s_["y"], mod[l, 5], 1.0, "gate_bwd")
        gr["out"] = _mm(s_["ycat"], dy, "tn", "mix_out_dw", out_dtype=BF16, tm=256)
        dycat = _mm(dy, w["out"], "nt", "mix_out_dx")
        du, small["pool_w"][l], small["pool_scale"][l] = _pool_bwd(dycat, s_["diff"], pool_w[l], vec(pool_scale[l]), "pool_bwd")
        dq, dk, dv = _attn_bwd(s_["q"], s_["k"], s_["v"], s_["lse"], dycat, "attn_bwd")
        dz, dqb, dkvb, small["q_a_norm"][l], small["kv_a_norm"][l] = _qkv_bwd(
            dq, dk, dv, du, s_["z"], vec(q_a_norm[l]), vec(kv_a_norm[l]), w["q"], w["kv"], cos, sin, "qkv_bwd")
        gr["q"] = _mm(dqb, s_["cqn"], "tn", "q_b_dw", out_dtype=BF16, tm=256)
        gr["kv"] = _mm(dkvb, s_["ckvn"], "tn", "kv_b_dw", out_dtype=BF16, tm=256)
        gr["in"] = _mm(dz, s_["h"], "tn", "mix_in_dw", out_dtype=BF16, tm=256)
        dh2 = _mm(dz, w["in"], "nn", "mix_in_dx")
        dx, dmod[3], dmod[4], small["mix_norm"][l] = _rm_bwd(dh2, s_["x"], dx, vec(mix_norm[l]), mod[l, 4], "rm_bwd")

        dx, small["ffn1_norm"][l] = ffn_bwd(dx, sv["f1"], ffn1_norm[l], 0, w["g1"], w["u1"], w["d1"], "1")

        small["dmod"][l] = jnp.concatenate(dmod, axis=0)
        grads_packed[l] = _pack_grads_layer(gr)

    grad_x = dx.reshape(x.shape)

    gsend = jnp.concatenate(grads_packed, axis=1)
    grecv = _all_to_all(gsend, "exchange_grads")
    gshard = _sum_slots(grecv, "sum_grads").reshape(nl, ROWS_L, D_MODEL)
    grads = _unpack_grad_shards(gshard)

    layout, small_rows = _small_layout(nl)
    pieces = {
        "dmod": jnp.concatenate(small["dmod"], axis=0),
        "ffn1_norm": jnp.concatenate(small["ffn1_norm"], axis=0),
        "mix_norm": jnp.concatenate(small["mix_norm"], axis=0),
        "ffn2_norm": jnp.concatenate(small["ffn2_norm"], axis=0),
        "q_a_norm": jnp.concatenate(small["q_a_norm"], axis=0),
        "kv_a_norm": jnp.concatenate(small["kv_a_norm"], axis=0),
        "pool_scale": jnp.concatenate(small["pool_scale"], axis=0),
        "final_norm": d_final,
        "loss": jnp.broadcast_to(loss_part[0:1, 0:1], (1, D_MODEL)),
        "pool_w": jnp.stack(small["pool_w"]).reshape(-1, D_MODEL),
    }
    small_buf = jnp.concatenate([_to_rows(pieces[name]) for name in layout], axis=0)
    small_all = _all_gather(small_buf, "gather_small").reshape(N_DEV, small_rows, D_MODEL)
    small_sum = _sum_slots(small_all, "sum_small")

    def take(name, width=D_MODEL):
        off, n = layout[name]
        return small_sum[off:off + n, :width]

    grads["ada_b"] = take("dmod").reshape(nl, N_MOD * D_MODEL)
    grads["ffn1_norm"], grads["mix_norm"], grads["ffn2_norm"] = take("ffn1_norm"), take("mix_norm"), take("ffn2_norm")
    grads["q_a_norm"], grads["kv_a_norm"] = take("q_a_norm", Q_LORA), take("kv_a_norm", KV_LORA)
    grads["pool_scale"] = take("pool_scale", POOL_WIDTH)
    grads["final_norm"] = take("final_norm").reshape(D_MODEL)
    grads["pool_w"] = take("pool_w").reshape(pool_w.shape)
    loss = take("loss")[0, 0]

    off, n = layout["dmod"]
    dmod_all = small_all[:, off:off + n].reshape(N_DEV, nl, N_MOD * D_MODEL)
    dmod_mine = lax.dynamic_slice_in_dim(dmod_all, me * ada_cols, ada_cols, axis=2)
    dmod_pad = jnp.pad(dmod_mine.transpose(1, 0, 2), ((0, 0), (0, LANE - N_DEV), (0, 0)))
    grads["ada_w"] = _ada_grad(jnp.pad(c_all, ((0, LANE - N_DEV), (0, 0))), dmod_pad, "ada_grad")

    outs_g, outs_d, outs_m, outs_v = [], [], [], []
    for name in order:
        g = grads[name]
        d_, nm, nv = _adamw_nd(wts[name], g, mom_m[name], mom_v[name], "adamw")
        outs_g.append(g)
        outs_d.append(d_)
        outs_m.append(nm)
        outs_v.append(nv)
    return (loss, grad_x, *outs_g, *outs_d, *outs_m, *outs_v)
```

```python
import math

import numpy as np
import jax
import jax.numpy as jnp
from jax import lax
from jax.experimental import pallas as pl
from jax.experimental.pallas import tpu as pltpu

F32 = jnp.float32
BF16 = jnp.bfloat16

N_DEV = 8
D_MODEL = 1024
D_FF = 2816
POOL_WIDTH = 512
POOL_WINDOWS = (2, 4, 8, 16)
POOL_GC = 128
N_HEADS = 4
QK_NOPE = 128
QK_ROPE = 64
V_HEAD = 128
QK_HEAD = QK_NOPE + QK_ROPE
HEAD_PAD = 256
Q_LORA = 384
KV_LORA = 256
IN_COLS = POOL_WIDTH + Q_LORA + KV_LORA + QK_ROPE
IN_PAD = 1280
ROPE_THETA = 10000.0
SOFTMAX_SCALE = 1.0 / math.sqrt(QK_HEAD)
EPS = 1e-6
N_MOD = 9

ADAM_LR = 0.001
ADAM_B1 = 0.9
ADAM_B2 = 0.999
ADAM_EPS = 1e-08
ADAM_WD = 0.01
ADAM_STEP = 10

LANE = 128
VMEM_LIMIT = 56 * 1024 * 1024

FF_SH = D_FF // N_DEV
OFF_G1, OFF_U1, OFF_D1 = 0, FF_SH, 2 * FF_SH
OFF_G2, OFF_U2, OFF_D2 = 3 * FF_SH, 4 * FF_SH, 5 * FF_SH
OFF_OUT = 6 * FF_SH
OFF_KV = OFF_OUT + 128
OFF_IN = OFF_KV + 32
OFF_Q = OFF_IN + 160
ROWS_L = OFF_Q + 48
IN_SH = IN_COLS // N_DEV
Q_SH_ROWS = (N_HEADS * QK_HEAD // N_DEV) * Q_LORA // D_MODEL
KV_SH_ROWS = (N_HEADS * (QK_NOPE + V_HEAD) // N_DEV) * KV_LORA // D_MODEL


def _tile(dim, target):
    if dim <= target:
        return dim
    best = None
    for t in range(LANE, target + 1, LANE):
        if dim % t == 0:
            best = t
    assert best is not None, (dim, target)
    return best


def _params(sem):
    return pltpu.CompilerParams(dimension_semantics=sem, vmem_limit_bytes=VMEM_LIMIT)


def _mesh_pos():
    return lax.axis_index("x"), lax.axis_index("y"), lax.axis_index("c")


def _all_gather(x, name):
    m, n = x.shape

    def body(x_ref, out_ref, send_sems, recv_sems, local_sem):
        px, py, pc = _mesh_pos()
        me, sibling = (px, py, pc), (px, py, 1 - pc)
        chips = [(1 - px, py), (px, 1 - py), (1 - px, 1 - py)]

        def rows(bx, by, bc):
            return out_ref.at[pl.ds((4 * bx + 2 * by + bc) * m, m), :]

        def copy(k, block, to, src=None):
            return pltpu.make_async_remote_copy(
                src_ref=rows(*block) if src is None else src, dst_ref=rows(*block),
                send_sem=send_sems.at[k], recv_sem=recv_sems.at[k],
                device_id=to, device_id_type=pl.DeviceIdType.MESH)

        mine = pltpu.make_async_copy(x_ref, rows(*me), local_sem)
        mine.start()
        first = [copy(0, me, sibling, src=x_ref)]
        first += [copy(1 + j, me, (*chip, pc), src=x_ref) for j, chip in enumerate(chips)]
        for cp in first:
            cp.start()
        passed = [copy(4 + j, (*chip, pc), sibling) for j, chip in enumerate(chips)]
        for j, chip in enumerate(chips):
            copy(1 + j, (*chip, pc), me).wait_recv()
            passed[j].start()
        copy(0, sibling, me).wait_recv()
        for j, chip in enumerate(chips):
            copy(4 + j, (*chip, 1 - pc), me).wait_recv()
        for cp in first + passed:
            cp.wait_send()
        mine.wait()

    return pl.pallas_call(
        body, name=name,
        out_shape=jax.ShapeDtypeStruct((N_DEV * m, n), x.dtype),
        in_specs=[pl.BlockSpec(memory_space=pltpu.HBM)],
        out_specs=pl.BlockSpec(memory_space=pltpu.HBM),
        scratch_shapes=[pltpu.SemaphoreType.DMA((7,)), pltpu.SemaphoreType.DMA((7,)),
                        pltpu.SemaphoreType.DMA],
    )(x)


SMALL_ROWS = ROWS_L - OFF_KV
PIECES = ((OFF_G1, FF_SH), (OFF_U1, FF_SH), (OFF_D1, FF_SH), (OFF_G2, FF_SH), (OFF_U2, FF_SH), (OFF_D2, FF_SH),
          (OFF_OUT, 128), (OFF_KV, SMALL_ROWS))
N_PIECES = len(PIECES)
HBM_SPEC = pl.BlockSpec(memory_space=pltpu.HBM)
SEM_SPEC = pl.BlockSpec(memory_space=pltpu.SEMAPHORE)
ANY_SPEC = pl.BlockSpec(memory_space=pl.ANY)
EFFECT = pltpu.SideEffectType.DATAFLOW_SIDE_EFFECTING


def _hbm(t):
    return pltpu.with_memory_space_constraint(t, pltpu.HBM)


def _whole_wait(ref, send_sem, recv_sem, peer):
    return pltpu.make_async_remote_copy(src_ref=ref, dst_ref=ref, send_sem=send_sem, recv_sem=recv_sem,
                                        device_id=peer, device_id_type=pl.DeviceIdType.MESH)


def _gather_start(packed, after, name):
    lands = [_hbm(lax.empty((N_DEV * rows, D_MODEL), BF16)) for _, rows in PIECES]

    def body(packed_ref, *refs):
        land = refs[:N_PIECES]
        send_sems, recv_sems = refs[N_PIECES + 1], refs[N_PIECES + 2]
        token = refs[-1]
        px, py, pc = _mesh_pos()
        me_id = 4 * px + 2 * py + pc
        peers = [(px, py, 1 - pc), (1 - px, py, pc), (px, 1 - py, pc), (1 - px, 1 - py, pc)]
        for k, peer in enumerate(peers):
            for (off, rows), land_ref in zip(PIECES, land):
                pltpu.make_async_remote_copy(
                    src_ref=packed_ref.at[pl.ds(off, rows), :], dst_ref=land_ref.at[pl.ds(me_id * rows, rows), :],
                    send_sem=send_sems.at[k], recv_sem=recv_sems.at[k],
                    device_id=peer, device_id_type=pl.DeviceIdType.MESH).start()
        token[...] = jnp.zeros_like(token)

    outs = pl.pallas_call(
        body, name=name,
        out_shape=(pltpu.SemaphoreType.DMA((4,)), pltpu.SemaphoreType.DMA((4,)), pltpu.HBM(packed.shape, BF16),
                   *[pltpu.HBM(t.shape, BF16) for t in lands], jax.ShapeDtypeStruct((8, LANE), F32)),
        in_specs=(HBM_SPEC,) * (1 + N_PIECES) + (ANY_SPEC,),
        out_specs=(SEM_SPEC, SEM_SPEC) + (HBM_SPEC,) * (1 + N_PIECES) + (pl.BlockSpec(memory_space=pltpu.VMEM),),
        input_output_aliases={i: 2 + i for i in range(1 + N_PIECES)},
        compiler_params=pltpu.CompilerParams(has_side_effects=EFFECT),
    )(_hbm(packed), *lands, after)
    return outs[0], outs[1], outs[2], list(outs[3:3 + N_PIECES]), outs[-1]


def _gather_wait(send_sems, recv_sems, packed, lands, after, name):
    def body(packed_ref, *refs):
        s_sems, r_sems = refs[N_PIECES], refs[N_PIECES + 1]
        me = _mesh_pos()
        for k in range(4):
            cp = _whole_wait(packed_ref, s_sems.at[k], r_sems.at[k], me)
            cp.wait_send()
            cp.wait_recv()

    outs = pl.pallas_call(
        body, name=name,
        out_shape=(pltpu.HBM(packed.shape, BF16), *[pltpu.HBM(t.shape, BF16) for t in lands]),
        in_specs=(HBM_SPEC,) * (1 + N_PIECES) + (SEM_SPEC, SEM_SPEC, ANY_SPEC),
        out_specs=(HBM_SPEC,) * (1 + N_PIECES),
        input_output_aliases={i: i for i in range(1 + N_PIECES)},
        compiler_params=pltpu.CompilerParams(has_side_effects=EFFECT),
    )(packed, *lands, send_sems, recv_sems, after)
    return outs[0], list(outs[1:])


def _gather_finish(packed, lands, name):
    def body(packed_ref, *refs):
        land = refs[N_PIECES:2 * N_PIECES]
        send_sems, recv_sems, local_sem = refs[2 * N_PIECES:]
        px, py, pc = _mesh_pos()
        me_id = 4 * px + 2 * py + pc
        sibling = (px, py, 1 - pc)
        for (off, rows), land_ref in zip(PIECES, land):
            pltpu.make_async_copy(packed_ref.at[pl.ds(off, rows), :], land_ref.at[pl.ds(me_id * rows, rows), :],
                                  local_sem).start()
        for j, (cx, cy) in enumerate([(1 - px, py), (px, 1 - py), (1 - px, 1 - py)]):
            block = 4 * cx + 2 * cy + pc
            for (_, rows), land_ref in zip(PIECES, land):
                blk = land_ref.at[pl.ds(block * rows, rows), :]
                pltpu.make_async_remote_copy(src_ref=blk, dst_ref=blk, send_sem=send_sems.at[j],
                                             recv_sem=recv_sems.at[j], device_id=sibling,
                                             device_id_type=pl.DeviceIdType.MESH).start()
        for j in range(3):
            cp = _whole_wait(packed_ref, send_sems.at[j], recv_sems.at[j], sibling)
            cp.wait_recv()
            cp.wait_send()
        pltpu.make_async_copy(packed_ref, packed_ref, local_sem).wait()

    outs = pl.pallas_call(
        body, name=name,
        out_shape=tuple(jax.ShapeDtypeStruct(t.shape, BF16) for t in lands),
        in_specs=(HBM_SPEC,) * (1 + N_PIECES), out_specs=(HBM_SPEC,) * N_PIECES,
        input_output_aliases={1 + i: i for i in range(N_PIECES)},
        scratch_shapes=[pltpu.SemaphoreType.DMA((3,)), pltpu.SemaphoreType.DMA((3,)), pltpu.SemaphoreType.DMA],
    )(packed, *lands)
    return list(outs)


def _grad_place_own(srcs, name):
    def body(*refs):
        src, recv, local_sem = refs[:N_PIECES], refs[N_PIECES], refs[N_PIECES + 1]
        px, py, pc = _mesh_pos()
        me_id = 4 * px + 2 * py + pc
        for (off, rows), src_ref in zip(PIECES, src):
            pltpu.make_async_copy(src_ref.at[pl.ds(me_id * rows, rows), :], recv.at[me_id, pl.ds(off, rows), :],
                                  local_sem).start()
        pltpu.make_async_copy(recv.at[0], recv.at[0], local_sem).wait()

    return pl.pallas_call(
        body, name=name, out_shape=jax.ShapeDtypeStruct((N_DEV, ROWS_L, D_MODEL), BF16),
        in_specs=(HBM_SPEC,) * N_PIECES, out_specs=HBM_SPEC,
        scratch_shapes=[pltpu.SemaphoreType.DMA],
    )(*srcs)


def _exchange_start(srcs, recv, name):
    def body(*refs):
        src, recv_ref = refs[:N_PIECES], refs[N_PIECES]
        send_sems, recv_sems = refs[N_PIECES + 1], refs[N_PIECES + 2]
        token = refs[-1]
        px, py, pc = _mesh_pos()
        me_id = 4 * px + 2 * py + pc
        for k in range(1, N_DEV):
            qx = 1 - px if k & 4 else px
            qy = 1 - py if k & 2 else py
            qc = 1 - pc if k & 1 else pc
            peer_id = 4 * qx + 2 * qy + qc
            for (off, rows), src_ref in zip(PIECES, src):
                pltpu.make_async_remote_copy(
                    src_ref=src_ref.at[pl.ds(peer_id * rows, rows), :], dst_ref=recv_ref.at[me_id, pl.ds(off, rows), :],
                    send_sem=send_sems.at[k - 1], recv_sem=recv_sems.at[k - 1],
                    device_id=(qx, qy, qc), device_id_type=pl.DeviceIdType.MESH).start()
        token[...] = jnp.zeros_like(token)

    outs = pl.pallas_call(
        body, name=name,
        out_shape=(pltpu.SemaphoreType.DMA((N_DEV - 1,)), pltpu.SemaphoreType.DMA((N_DEV - 1,)),
                   *[pltpu.HBM(t.shape, BF16) for t in srcs], pltpu.HBM(recv.shape, BF16),
                   jax.ShapeDtypeStruct((8, LANE), F32)),
        in_specs=(HBM_SPEC,) * (N_PIECES + 1),
        out_specs=(SEM_SPEC, SEM_SPEC) + (HBM_SPEC,) * (N_PIECES + 1) + (pl.BlockSpec(memory_space=pltpu.VMEM),),
        input_output_aliases={i: 2 + i for i in range(N_PIECES + 1)},
        compiler_params=pltpu.CompilerParams(has_side_effects=EFFECT),
    )(*[_hbm(t) for t in srcs], _hbm(recv))
    return outs[0], outs[1], list(outs[2:2 + N_PIECES]), outs[2 + N_PIECES], outs[-1]


def _exchange_wait(send_sems, recv_sems, srcs, recv, after, name):
    def body(*refs):
        recv_ref = refs[N_PIECES]
        s_sems, r_sems = refs[N_PIECES + 1], refs[N_PIECES + 2]
        me = _mesh_pos()
        for k in range(N_DEV - 1):
            cp = _whole_wait(recv_ref.at[0], s_sems.at[k], r_sems.at[k], me)
            cp.wait_send()
            cp.wait_recv()

    outs = pl.pallas_call(
        body, name=name,
        out_shape=(*[pltpu.HBM(t.shape, BF16) for t in srcs], pltpu.HBM(recv.shape, BF16)),
        in_specs=(HBM_SPEC,) * (N_PIECES + 1) + (SEM_SPEC, SEM_SPEC, ANY_SPEC),
        out_specs=(HBM_SPEC,) * (N_PIECES + 1),
        input_output_aliases={i: i for i in range(N_PIECES + 1)},
        compiler_params=pltpu.CompilerParams(has_side_effects=EFFECT),
    )(*srcs, recv, send_sems, recv_sems, after)
    return outs[N_PIECES]


def _sum_slots(recv, name):
    _, r, n = recv.shape
    tr = _row_tile(r, 512)

    def body(in_ref, out_ref):
        acc = in_ref[0].astype(F32)
        for j in range(1, N_DEV):
            acc = acc + in_ref[j].astype(F32)
        out_ref[...] = acc

    return pl.pallas_call(
        body, name=name, grid=(r // tr,),
        out_shape=jax.ShapeDtypeStruct((r, n), F32),
        in_specs=[pl.BlockSpec((N_DEV, tr, n), lambda i: (0, i, 0))],
        out_specs=pl.BlockSpec((tr, n), lambda i: (i, 0)),
        compiler_params=_params(("parallel",)),
    )(recv)


def _row_tile(rows, target):
    if rows <= target:
        return rows
    best = None
    for t in range(16, target + 1, 16):
        if rows % t == 0:
            best = t
    assert best is not None, rows
    return best


_DIMS = {"nn": ((1,), (0,)), "nt": ((1,), (1,)), "tn": ((0,), (0,))}


def _mm(a, b, mode, name, out_dtype=F32, res=None, gate=None, gate_factor=1.0, tm=512, tn=1408):
    if mode == "tn":
        kdim, m = a.shape
    else:
        m, kdim = a.shape
    n = b.shape[0] if mode == "nt" else b.shape[1]
    tm, tn = _tile(m, tm), _tile(n, tn)
    a_spec = (pl.BlockSpec((kdim, tm), lambda i, j: (0, i)) if mode == "tn"
              else pl.BlockSpec((tm, kdim), lambda i, j: (i, 0)))
    b_spec = (pl.BlockSpec((tn, kdim), lambda i, j: (j, 0)) if mode == "nt"
              else pl.BlockSpec((kdim, tn), lambda i, j: (0, j)))
    o_spec = pl.BlockSpec((tm, tn), lambda i, j: (i, j))
    dims = (_DIMS[mode], ((), ()))
    has_res, has_gate = res is not None, gate is not None

    def body(*refs):
        a_ref, b_ref = refs[0], refs[1]
        y = lax.dot_general(a_ref[...].astype(BF16), b_ref[...].astype(BF16), dims,
                            preferred_element_type=F32)
        if not has_res:
            refs[2][...] = y.astype(out_dtype)
            return
        res_ref = refs[2]
        if has_gate:
            y_ref, o_ref = refs[4], refs[5]
            y_ref[...] = y
            o_ref[...] = res_ref[...] + (gate_factor * refs[3][...]) * y
        else:
            refs[3][...] = res_ref[...] + y

    in_specs, args = [a_spec, b_spec], [a, b]
    if has_res:
        in_specs.append(o_spec)
        args.append(res)
        if has_gate:
            in_specs.append(pl.BlockSpec((1, tn), lambda i, j: (0, j)))
            args.append(gate)
            out_shape = (jax.ShapeDtypeStruct((m, n), F32), jax.ShapeDtypeStruct((m, n), F32))
            out_specs = (o_spec, o_spec)
        else:
            out_shape, out_specs = jax.ShapeDtypeStruct((m, n), F32), o_spec
    else:
        out_shape, out_specs = jax.ShapeDtypeStruct((m, n), out_dtype), o_spec
    return pl.pallas_call(
        body, name=name, grid=(m // tm, n // tn), out_shape=out_shape,
        in_specs=in_specs, out_specs=out_specs,
        compiler_params=_params(("parallel", "parallel")),
    )(*args)


def _vec_spec(width):
    return pl.BlockSpec((1, width), lambda i: (0, 0))


def _rm_fwd(x, gw, shift, scale, name):
    s, d = x.shape
    ts = _tile(s, 256)

    def body(x_ref, gw_ref, sh_ref, sc_ref, h_ref):
        xv = x_ref[...]
        r = lax.rsqrt(jnp.mean(xv * xv, axis=-1, keepdims=True) + EPS)
        y = (xv * r) * gw_ref[...]
        h_ref[...] = (y * (1 + sc_ref[...]) + sh_ref[...]).astype(BF16)

    row = pl.BlockSpec((ts, d), lambda i: (i, 0))
    return pl.pallas_call(
        body, name=name, grid=(s // ts,), out_shape=jax.ShapeDtypeStruct((s, d), BF16),
        in_specs=[row, _vec_spec(d), _vec_spec(d), _vec_spec(d)], out_specs=row,
        compiler_params=_params(("parallel",)),
    )(x, gw, shift, scale)


def _rm_bwd(dh, x, dres, gw, scale, name):
    s, d = x.shape
    ts = _tile(s, 256)

    def body(dh_ref, x_ref, dres_ref, gw_ref, sc_ref, dx_ref, dsh_ref, dsc_ref, dgw_ref):
        @pl.when(pl.program_id(0) == 0)
        def _():
            dsh_ref[...] = jnp.zeros_like(dsh_ref)
            dsc_ref[...] = jnp.zeros_like(dsc_ref)
            dgw_ref[...] = jnp.zeros_like(dgw_ref)

        xv, dhv, gwv = x_ref[...], dh_ref[...], gw_ref[...]
        r = lax.rsqrt(jnp.mean(xv * xv, axis=-1, keepdims=True) + EPS)
        xn = xv * r
        y = xn * gwv
        dsh_ref[...] += jnp.sum(dhv, axis=0, keepdims=True)
        dsc_ref[...] += jnp.sum(dhv * y, axis=0, keepdims=True)
        dy = dhv * (1 + sc_ref[...])
        dgw_ref[...] += jnp.sum(dy * xn, axis=0, keepdims=True)
        dxn = dy * gwv
        dx = r * (dxn - xn * jnp.mean(dxn * xn, axis=-1, keepdims=True))
        dx_ref[...] = dres_ref[...] + dx

    row = pl.BlockSpec((ts, d), lambda i: (i, 0))
    vec = jax.ShapeDtypeStruct((1, d), F32)
    return pl.pallas_call(
        body, name=name, grid=(s // ts,),
        out_shape=(jax.ShapeDtypeStruct((s, d), F32), vec, vec, vec),
        in_specs=[row, row, row, _vec_spec(d), _vec_spec(d)],
        out_specs=(row, _vec_spec(d), _vec_spec(d), _vec_spec(d)),
        compiler_params=_params(("arbitrary",)),
    )(dh, x, dres, gw, scale)


def _gate_bwd(dx, y, gate, factor, name):
    s, d = dx.shape
    ts = _tile(s, 256)

    def body(dx_ref, y_ref, g_ref, dy_ref, dg_ref):
        @pl.when(pl.program_id(0) == 0)
        def _():
            dg_ref[...] = jnp.zeros_like(dg_ref)

        dxv = dx_ref[...]
        dy_ref[...] = ((factor * g_ref[...]) * dxv).astype(BF16)
        dg_ref[...] += jnp.sum((factor * dxv) * y_ref[...], axis=0, keepdims=True)

    row = pl.BlockSpec((ts, d), lambda i: (i, 0))
    return pl.pallas_call(
        body, name=name, grid=(s // ts,),
        out_shape=(jax.ShapeDtypeStruct((s, d), BF16), jax.ShapeDtypeStruct((1, d), F32)),
        in_specs=[row, row, _vec_spec(d)], out_specs=(row, _vec_spec(d)),
        compiler_params=_params(("arbitrary",)),
    )(dx, y, gate)


def _swiglu_fwd(a, b, name):
    s, f = a.shape
    ts, tf = _tile(s, 256), _tile(f, 1408)

    def body(a_ref, b_ref, t_ref):
        av = a_ref[...]
        t_ref[...] = ((av * jax.nn.sigmoid(av)) * b_ref[...]).astype(BF16)

    blk = pl.BlockSpec((ts, tf), lambda i, j: (i, j))
    return pl.pallas_call(
        body, name=name, grid=(s // ts, f // tf), out_shape=jax.ShapeDtypeStruct((s, f), BF16),
        in_specs=[blk, blk], out_specs=blk, compiler_params=_params(("parallel", "parallel")),
    )(a, b)


def _swiglu_bwd(dt, a, b, name):
    s, f = a.shape
    ts, tf = _tile(s, 256), _tile(f, 1408)

    def body(dt_ref, a_ref, b_ref, da_ref, db_ref):
        av, dtv = a_ref[...], dt_ref[...]
        sg = jax.nn.sigmoid(av)
        silu = av * sg
        db_ref[...] = (dtv * silu).astype(BF16)
        da_ref[...] = ((dtv * b_ref[...]) * (sg * (1 + av * (1 - sg)))).astype(BF16)

    blk = pl.BlockSpec((ts, tf), lambda i, j: (i, j))
    out = jax.ShapeDtypeStruct((s, f), BF16)
    return pl.pallas_call(
        body, name=name, grid=(s // ts, f // tf), out_shape=(out, out),
        in_specs=[blk, blk, blk], out_specs=(blk, blk),
        compiler_params=_params(("parallel", "parallel")),
    )(dt, a, b)


def _pool_counts(s):
    return (lax.broadcasted_iota(jnp.int32, (s, POOL_GC), 0))


def _pool_fwd(z, pool_w, pool_scale, name):
    s = z.shape[0]

    def body(u_ref, w_ref, sc_ref, y_ref, diff_ref):
        t = lax.broadcasted_iota(jnp.int32, (s, POOL_GC), 0)
        for g, win in enumerate(POOL_WINDOWS):
            cols = slice(g * POOL_GC, (g + 1) * POOL_GC)
            u = u_ref[:, cols]
            acc, step = u, 1
            while step < win:
                acc = acc + jnp.where(t >= step, pltpu.roll(acc, step, 0), 0.0)
                step *= 2
            cnt = jnp.minimum(t + 1, win).astype(F32)
            diff = acc / cnt - u
            diff_ref[:, cols] = diff
            ypre = jnp.dot(diff.astype(BF16), w_ref[g].astype(BF16), preferred_element_type=F32)
            y_ref[:, cols] = (ypre * sc_ref[:, cols]).astype(BF16)

    return pl.pallas_call(
        body, name=name, grid=(1,),
        out_shape=(jax.ShapeDtypeStruct((s, POOL_WIDTH), BF16), jax.ShapeDtypeStruct((s, POOL_WIDTH), F32)),
        in_specs=[pl.BlockSpec((s, POOL_WIDTH), lambda i: (0, 0)),
                  pl.BlockSpec(pool_w.shape, lambda i: (0, 0, 0)),
                  pl.BlockSpec((1, POOL_WIDTH), lambda i: (0, 0))],
        out_specs=(pl.BlockSpec((s, POOL_WIDTH), lambda i: (0, 0)),
                   pl.BlockSpec((s, POOL_WIDTH), lambda i: (0, 0))),
        compiler_params=_params(("arbitrary",)),
    )(z, pool_w, pool_scale)


def _pool_bwd(dycat, diff, pool_w, pool_scale, name):
    s = diff.shape[0]

    def body(dy_ref, diff_ref, w_ref, sc_ref, du_ref, dw_ref, dsc_ref):
        t = lax.broadcasted_iota(jnp.int32, (s, POOL_GC), 0)
        for g, win in enumerate(POOL_WINDOWS):
            cols = slice(g * POOL_GC, (g + 1) * POOL_GC)
            dy, dfb, wb = dy_ref[:, cols], diff_ref[:, cols].astype(BF16), w_ref[g].astype(BF16)
            ypre = jnp.dot(dfb, wb, preferred_element_type=F32)
            dsc_ref[:, cols] = jnp.sum(dy * ypre, axis=0, keepdims=True)
            dypre = (dy * sc_ref[:, cols]).astype(BF16)
            ddiff = lax.dot_general(dypre, wb, (((1,), (1,)), ((), ())), preferred_element_type=F32)
            dw_ref[g] = lax.dot_general(dfb, dypre, (((0,), (0,)), ((), ())), preferred_element_type=F32)
            cnt = jnp.minimum(t + 1, win).astype(F32)
            acc, step = ddiff / cnt, 1
            while step < win:
                acc = acc + jnp.where(t < s - step, pltpu.roll(acc, s - step, 0), 0.0)
                step *= 2
            du_ref[:, cols] = acc - ddiff

    full = pl.BlockSpec((s, POOL_WIDTH), lambda i: (0, 0))
    return pl.pallas_call(
        body, name=name, grid=(1,),
        out_shape=(jax.ShapeDtypeStruct((s, POOL_WIDTH), F32),
                   jax.ShapeDtypeStruct(pool_w.shape, F32),
                   jax.ShapeDtypeStruct((1, POOL_WIDTH), F32)),
        in_specs=[full, full, pl.BlockSpec(pool_w.shape, lambda i: (0, 0, 0)),
                  pl.BlockSpec((1, POOL_WIDTH), lambda i: (0, 0))],
        out_specs=(full, pl.BlockSpec(pool_w.shape, lambda i: (0, 0, 0)),
                   pl.BlockSpec((1, POOL_WIDTH), lambda i: (0, 0))),
        compiler_params=_params(("arbitrary",)),
    )(dycat, diff, pool_w, pool_scale)


def _rope_tables(positions, name):
    s = positions.shape[0]
    ts = _tile(s, 512)
    freq = 1.0 / (ROPE_THETA ** (np.arange(0, QK_ROPE, 2, dtype=np.float32) / QK_ROPE))
    table = np.zeros((1, LANE), np.float32)
    table[0, :QK_ROPE // 2] = freq
    table[0, QK_ROPE // 2:QK_ROPE] = freq

    def body(pos_ref, f_ref, cos_ref, sin_ref):
        ang = pos_ref[...].astype(F32) * f_ref[...]
        cos_ref[...] = jnp.cos(ang)
        sin_ref[...] = jnp.sin(ang)

    out = jax.ShapeDtypeStruct((s, LANE), F32)
    blk = pl.BlockSpec((ts, LANE), lambda i: (i, 0))
    return pl.pallas_call(
        body, name=name, grid=(s // ts,), out_shape=(out, out),
        in_specs=[pl.BlockSpec((ts, 1), lambda i: (i, 0)), _vec_spec(LANE)], out_specs=(blk, blk),
        compiler_params=_params(("parallel",)),
    )(positions, jnp.asarray(table))


def _lane_mod64_low(shape):
    return (lax.broadcasted_iota(jnp.int32, shape, 1) % QK_ROPE) < (QK_ROPE // 2)


def _rope(x, cos, sin):
    rot = jnp.where(_lane_mod64_low(x.shape), -pltpu.roll(x, LANE - 32, 1), pltpu.roll(x, 32, 1))
    return x * cos + rot * sin


def _rope_t(dy, cos, sin):
    w = dy * sin
    rot_t = jnp.where(_lane_mod64_low(dy.shape), pltpu.roll(w, LANE - 32, 1), -pltpu.roll(w, 32, 1))
    return dy * cos + rot_t


def _plain_rms(x, g):
    r = lax.rsqrt(jnp.mean(x * x, axis=-1, keepdims=True) + EPS)
    return (x * r) * g, x * r, r


O_Q, O_KV, O_KR = POOL_WIDTH, POOL_WIDTH + Q_LORA, POOL_WIDTH + Q_LORA + KV_LORA


def _qkv_fwd(z, qn, kvn, wq, wkv, cos, sin, name):
    s = z.shape[0]
    ts = _tile(s, 256)

    def body(z_ref, qn_ref, kvn_ref, wq_ref, wkv_ref, cos_ref, sin_ref, q_ref, k_ref, v_ref, cqn_ref, ckvn_ref):
        cosv, sinv = cos_ref[...], sin_ref[...]
        cqn = _plain_rms(z_ref[:, O_Q:O_KV], qn_ref[...])[0].astype(BF16)
        ckvn = _plain_rms(z_ref[:, O_KV:O_KR], kvn_ref[...])[0].astype(BF16)
        cqn_ref[...] = cqn
        ckvn_ref[...] = ckvn
        nt = (((1,), (1,)), ((), ()))
        q = lax.dot_general(cqn, wq_ref[...], nt, preferred_element_type=F32)
        kv = lax.dot_general(ckvn, wkv_ref[...], nt, preferred_element_type=F32)
        kr = _rope(z_ref[:, O_KR:IN_PAD], cosv, sinv).astype(BF16)
        for h in range(N_HEADS):
            o = h * HEAD_PAD
            q_ref[:, o:o + QK_NOPE] = q[:, o:o + QK_NOPE].astype(BF16)
            q_ref[:, o + QK_NOPE:o + HEAD_PAD] = _rope(q[:, o + QK_NOPE:o + HEAD_PAD], cosv, sinv).astype(BF16)
            k_ref[:, o:o + QK_NOPE] = kv[:, o:o + QK_NOPE].astype(BF16)
            k_ref[:, o + QK_NOPE:o + HEAD_PAD] = kr
            v_ref[:, h * V_HEAD:(h + 1) * V_HEAD] = kv[:, o + QK_NOPE:o + HEAD_PAD].astype(BF16)

    def row(w):
        return pl.BlockSpec((ts, w), lambda i: (i, 0))

    def whole(arr):
        return pl.BlockSpec(arr.shape, lambda i: (0, 0))

    hp = N_HEADS * HEAD_PAD
    return pl.pallas_call(
        body, name=name, grid=(s // ts,),
        out_shape=(jax.ShapeDtypeStruct((s, hp), BF16), jax.ShapeDtypeStruct((s, hp), BF16),
                   jax.ShapeDtypeStruct((s, N_HEADS * V_HEAD), BF16),
                   jax.ShapeDtypeStruct((s, Q_LORA), BF16), jax.ShapeDtypeStruct((s, KV_LORA), BF16)),
        in_specs=[row(IN_PAD), whole(qn), whole(kvn), whole(wq), whole(wkv), row(LANE), row(LANE)],
        out_specs=(row(hp), row(hp), row(N_HEADS * V_HEAD), row(Q_LORA), row(KV_LORA)),
        compiler_params=_params(("parallel",)),
    )(z, qn, kvn, wq, wkv, cos, sin)


def _qkv_bwd(dq, dk, dv, du, z, qn, kvn, wq, wkv, cos, sin, name):
    s = z.shape[0]
    ts = _tile(s, 256)

    def norm_bwd(x, g, dy):
        _, xn, r = _plain_rms(x, g)
        dxn = dy * g
        return r * (dxn - xn * jnp.mean(dxn * xn, axis=-1, keepdims=True)), jnp.sum(dy * xn, axis=0, keepdims=True)

    def body(dq_ref, dk_ref, dv_ref, du_ref, z_ref, qn_ref, kvn_ref, wq_ref, wkv_ref, cos_ref, sin_ref,
             dz_ref, dqb_ref, dkvb_ref, dqn_ref, dkvn_ref):
        @pl.when(pl.program_id(0) == 0)
        def _():
            dqn_ref[...] = jnp.zeros_like(dqn_ref)
            dkvn_ref[...] = jnp.zeros_like(dkvn_ref)

        cosv, sinv = cos_ref[...], sin_ref[...]
        dkr = jnp.zeros((ts, LANE), F32)
        for h in range(N_HEADS):
            o = h * HEAD_PAD
            dqb_ref[:, o:o + QK_NOPE] = dq_ref[:, o:o + QK_NOPE].astype(BF16)
            dqb_ref[:, o + QK_NOPE:o + HEAD_PAD] = _rope_t(dq_ref[:, o + QK_NOPE:o + HEAD_PAD], cosv, sinv).astype(BF16)
            dkvb_ref[:, o:o + QK_NOPE] = dk_ref[:, o:o + QK_NOPE].astype(BF16)
            dkvb_ref[:, o + QK_NOPE:o + HEAD_PAD] = dv_ref[:, h * V_HEAD:(h + 1) * V_HEAD].astype(BF16)
            dkr = dkr + dk_ref[:, o + QK_NOPE:o + HEAD_PAD]
        dcqn = jnp.dot(dqb_ref[...], wq_ref[...], preferred_element_type=F32)
        dckvn = jnp.dot(dkvb_ref[...], wkv_ref[...], preferred_element_type=F32)
        dcq, dqn = norm_bwd(z_ref[:, O_Q:O_KV], qn_ref[...], dcqn)
        dckv, dkvn = norm_bwd(z_ref[:, O_KV:O_KR], kvn_ref[...], dckvn)
        dqn_ref[...] += dqn
        dkvn_ref[...] += dkvn
        dz_ref[:, 0:O_Q] = du_ref[...].astype(BF16)
        dz_ref[:, O_Q:O_KV] = dcq.astype(BF16)
        dz_ref[:, O_KV:O_KR] = dckv.astype(BF16)
        dz_ref[:, O_KR:IN_PAD] = _rope_t(dkr, cosv, sinv).astype(BF16)

    def row(w):
        return pl.BlockSpec((ts, w), lambda i: (i, 0))

    def whole(arr):
        return pl.BlockSpec(arr.shape, lambda i: (0, 0))

    hp = N_HEADS * HEAD_PAD
    return pl.pallas_call(
        body, name=name, grid=(s // ts,),
        out_shape=(jax.ShapeDtypeStruct((s, IN_PAD), BF16), jax.ShapeDtypeStruct((s, hp), BF16),
                   jax.ShapeDtypeStruct((s, hp), BF16),
                   jax.ShapeDtypeStruct((1, Q_LORA), F32), jax.ShapeDtypeStruct((1, KV_LORA), F32)),
        in_specs=[row(hp), row(hp), row(N_HEADS * V_HEAD), row(POOL_WIDTH), row(IN_PAD),
                  whole(qn), whole(kvn), whole(wq), whole(wkv), row(LANE), row(LANE)],
        out_specs=(row(IN_PAD), row(hp), row(hp), whole(qn), whole(kvn)),
        compiler_params=_params(("arbitrary",)),
    )(dq, dk, dv, du, z, qn, kvn, wq, wkv, cos, sin)


def _causal_scores(q, k, i, tq, s):
    sc = lax.dot_general(q, k, (((1,), (1,)), ((), ())), preferred_element_type=F32) * SOFTMAX_SCALE
    qpos = i * tq + lax.broadcasted_iota(jnp.int32, (tq, s), 0)
    kpos = lax.broadcasted_iota(jnp.int32, (tq, s), 1)
    return jnp.where(qpos >= kpos, sc, -jnp.inf)


def _attn_fwd(q, k, v, name):
    s = q.shape[0]
    tq = _tile(s, 256)

    def body(q_ref, k_ref, v_ref, o_ref, lse_ref):
        sc = _causal_scores(q_ref[...], k_ref[...], pl.program_id(1), tq, s)
        mx = jnp.max(sc, axis=-1, keepdims=True)
        p = jnp.exp(sc - mx)
        den = jnp.sum(p, axis=-1, keepdims=True)
        o_ref[...] = jnp.dot((p / den).astype(BF16), v_ref[...], preferred_element_type=F32)
        lse_ref[...] = mx + jnp.log(den)

    return pl.pallas_call(
        body, name=name, grid=(N_HEADS, s // tq),
        out_shape=(jax.ShapeDtypeStruct((s, N_HEADS * V_HEAD), F32), jax.ShapeDtypeStruct((N_HEADS, s, 1), F32)),
        in_specs=[pl.BlockSpec((tq, HEAD_PAD), lambda h, i: (i, h)),
                  pl.BlockSpec((s, HEAD_PAD), lambda h, i: (0, h)),
                  pl.BlockSpec((s, V_HEAD), lambda h, i: (0, h))],
        out_specs=(pl.BlockSpec((tq, V_HEAD), lambda h, i: (i, h)),
                   pl.BlockSpec((None, tq, 1), lambda h, i: (h, i, 0))),
        compiler_params=_params(("parallel", "parallel")),
    )(q, k, v)


def _attn_bwd(q, k, v, lse, dycat, name):
    s = q.shape[0]
    tq = _tile(s, 256)
    tn_dims = (((0,), (0,)), ((), ()))

    def body(q_ref, k_ref, v_ref, lse_ref, do_ref, dq_ref, dk_ref, dv_ref):
        @pl.when(pl.program_id(1) == 0)
        def _():
            dk_ref[...] = jnp.zeros_like(dk_ref)
            dv_ref[...] = jnp.zeros_like(dv_ref)

        qv, kv_, dob = q_ref[...], k_ref[...], do_ref[...].astype(BF16)
        sc = _causal_scores(qv, kv_, pl.program_id(1), tq, s)
        p = jnp.exp(sc - lse_ref[...])
        dp = lax.dot_general(dob, v_ref[...], (((1,), (1,)), ((), ())), preferred_element_type=F32)
        ds = (p * (dp - jnp.sum(dp * p, axis=-1, keepdims=True)) * SOFTMAX_SCALE).astype(BF16)
        dq_ref[...] = jnp.dot(ds, kv_, preferred_element_type=F32)
        dk_ref[...] += lax.dot_general(ds, qv, tn_dims, preferred_element_type=F32)
        dv_ref[...] += lax.dot_general(p.astype(BF16), dob, tn_dims, preferred_element_type=F32)

    n_pool_blocks = POOL_WIDTH // V_HEAD
    return pl.pallas_call(
        body, name=name, grid=(N_HEADS, s // tq),
        out_shape=(jax.ShapeDtypeStruct((s, N_HEADS * HEAD_PAD), F32),
                   jax.ShapeDtypeStruct((s, N_HEADS * HEAD_PAD), F32),
                   jax.ShapeDtypeStruct((s, N_HEADS * V_HEAD), F32)),
        in_specs=[pl.BlockSpec((tq, HEAD_PAD), lambda h, i: (i, h)),
                  pl.BlockSpec((s, HEAD_PAD), lambda h, i: (0, h)),
                  pl.BlockSpec((s, V_HEAD), lambda h, i: (0, h)),
                  pl.BlockSpec((None, tq, 1), lambda h, i: (h, i, 0)),
                  pl.BlockSpec((tq, V_HEAD), lambda h, i: (i, n_pool_blocks + h))],
        out_specs=(pl.BlockSpec((tq, HEAD_PAD), lambda h, i: (i, h)),
                   pl.BlockSpec((s, HEAD_PAD), lambda h, i: (0, h)),
                   pl.BlockSpec((s, V_HEAD), lambda h, i: (0, h))),
        compiler_params=_params(("parallel", "arbitrary")),
    )(q, k, v, lse, dycat)


def _loss_head(x, gw, target, name):
    s, d = x.shape
    ts = _tile(s, 256)

    def body(x_ref, gw_ref, tgt_ref, loss_ref, dx_ref, dgw_ref):
        @pl.when(pl.program_id(0) == 0)
        def _():
            loss_ref[...] = jnp.zeros_like(loss_ref)
            dgw_ref[...] = jnp.zeros_like(dgw_ref)

        xv, gwv = x_ref[...], gw_ref[...]
        r = lax.rsqrt(jnp.mean(xv * xv, axis=-1, keepdims=True) + EPS)
        xn = xv * r
        err = xn * gwv - tgt_ref[...]
        loss_ref[...] += 0.5 * jnp.sum(jnp.mean(err * err, axis=-1, keepdims=True))
        dy = err / d
        dgw_ref[...] += jnp.sum(dy * xn, axis=0, keepdims=True)
        dxn = dy * gwv
        dx_ref[...] = r * (dxn - xn * jnp.mean(dxn * xn, axis=-1, keepdims=True))

    row = pl.BlockSpec((ts, d), lambda i: (i, 0))
    return pl.pallas_call(
        body, name=name, grid=(s // ts,),
        out_shape=(jax.ShapeDtypeStruct((8, LANE), F32), jax.ShapeDtypeStruct((s, d), F32),
                   jax.ShapeDtypeStruct((1, d), F32)),
        in_specs=[row, _vec_spec(d), row],
        out_specs=(pl.BlockSpec((8, LANE), lambda i: (0, 0)), row, _vec_spec(d)),
        compiler_params=_params(("arbitrary",)),
    )(x, gw, target)


def _ada_mod(c_all, ada_w, ada_b, name):
    nl, d, cols = ada_w.shape

    def body(c_ref, w_ref, b_ref, o_ref):
        cv = c_ref[...]
        act = (cv * jax.nn.sigmoid(cv)).astype(BF16)
        o_ref[...] = jnp.dot(act, w_ref[...].astype(BF16), preferred_element_type=F32) + b_ref[...]

    return pl.pallas_call(
        body, name=name, grid=(nl,), out_shape=jax.ShapeDtypeStruct((nl, N_DEV, cols), F32),
        in_specs=[pl.BlockSpec((N_DEV, d), lambda l: (0, 0)),
                  pl.BlockSpec((None, d, cols), lambda l: (l, 0, 0)),
                  pl.BlockSpec((None, 1, cols), lambda l: (l, 0, 0))],
        out_specs=pl.BlockSpec((None, N_DEV, cols), lambda l: (l, 0, 0)),
        compiler_params=_params(("parallel",)),
    )(c_all, ada_w, ada_b)


def _ada_grad(c_pad, dmod_pad, name):
    nl, kpad, cols = dmod_pad.shape
    d = c_pad.shape[1]

    def body(c_ref, dm_ref, o_ref):
        cv = c_ref[...]
        act = (cv * jax.nn.sigmoid(cv)).astype(BF16)
        o_ref[...] = lax.dot_general(act, dm_ref[...].astype(BF16), (((0,), (0,)), ((), ())),
                                     preferred_element_type=F32)

    return pl.pallas_call(
        body, name=name, grid=(nl,), out_shape=jax.ShapeDtypeStruct((nl, d, cols), F32),
        in_specs=[pl.BlockSpec((kpad, d), lambda l: (0, 0)),
                  pl.BlockSpec((None, kpad, cols), lambda l: (l, 0, 0))],
        out_specs=pl.BlockSpec((None, d, cols), lambda l: (l, 0, 0)),
        compiler_params=_params(("parallel",)),
    )(c_pad, dmod_pad)


def _adamw(w, g, m, v, name):
    rows, cols = w.shape
    tr = _row_tile(rows, 256)

    def body(w_ref, g_ref, m_ref, v_ref, d_ref, nm_ref, nv_ref):
        gv = g_ref[...]
        nm = ADAM_B1 * m_ref[...] + (1.0 - ADAM_B1) * gv
        nv = ADAM_B2 * v_ref[...] + (1.0 - ADAM_B2) * (gv * gv)
        m_hat = nm / (1.0 - ADAM_B1 ** ADAM_STEP)
        v_hat = nv / (1.0 - ADAM_B2 ** ADAM_STEP)
        d_ref[...] = -ADAM_LR * (m_hat / (jnp.sqrt(v_hat) + ADAM_EPS) + ADAM_WD * w_ref[...])
        nm_ref[...] = nm
        nv_ref[...] = nv

    blk = pl.BlockSpec((tr, cols), lambda i: (i, 0))
    out = jax.ShapeDtypeStruct((rows, cols), F32)
    return pl.pallas_call(
        body, name=name, grid=(rows // tr,), out_shape=(out, out, out),
        in_specs=[blk, blk, blk, blk], out_specs=(blk, blk, blk),
        compiler_params=_params(("parallel",)),
    )(w, g, m, v)


def _adamw_nd(w, g, m, v, name):
    shape = w.shape
    flat = (lambda t: t.reshape(1, -1)) if w.ndim == 1 else (lambda t: t.reshape(-1, shape[-1]))
    return tuple(t.reshape(shape) for t in _adamw(flat(w), flat(g), flat(m), flat(v), name))


def _pad_rows(t, rows):
    return jnp.pad(t, ((0, rows - t.shape[0]), (0, 0)))


def _pack_shard_layer(l, wts):
    def tr(name):
        return wts[name][l].astype(BF16).T

    parts = [tr("ffn1_w_gate"), tr("ffn1_w_up"), wts["ffn1_w_down"][l].astype(BF16),
             tr("ffn2_w_gate"), tr("ffn2_w_up"), wts["ffn2_w_down"][l].astype(BF16),
             wts["w_out"][l].astype(BF16),
             tr("w_kv_b").reshape(KV_SH_ROWS, D_MODEL),
             _pad_rows(tr("w_in"), 160),
             _pad_rows(tr("w_q_b").reshape(Q_SH_ROWS, D_MODEL), 48)]
    return jnp.concatenate(parts, axis=0)


def _full_weights(lands):
    w = dict(zip(("g1", "u1", "d1", "g2", "u2", "d2", "out"), lands))
    small = lands[-1].reshape(N_DEV, SMALL_ROWS, D_MODEL)
    o_in, o_q = OFF_IN - OFF_KV, OFF_Q - OFF_KV
    w["kv"] = small[:, :KV_SH_ROWS].reshape(N_HEADS * HEAD_PAD, KV_LORA)
    w["in"] = _pad_rows(small[:, o_in:o_in + IN_SH].reshape(IN_COLS, D_MODEL), IN_PAD)
    wq = small[:, o_q:o_q + Q_SH_ROWS].reshape(N_HEADS, QK_HEAD, Q_LORA)
    w["q"] = jnp.pad(wq, ((0, 0), (0, HEAD_PAD - QK_HEAD), (0, 0))).reshape(N_HEADS * HEAD_PAD, Q_LORA)
    return w


def _grad_sources(gr):
    gq = gr["q"].reshape(N_HEADS, HEAD_PAD, Q_LORA)[:, :QK_HEAD].reshape(N_DEV, Q_SH_ROWS, D_MODEL)
    small = jnp.concatenate([
        gr["kv"].reshape(N_DEV, KV_SH_ROWS, D_MODEL),
        jnp.pad(gr["in"][:IN_COLS].reshape(N_DEV, IN_SH, D_MODEL), ((0, 0), (0, 160 - IN_SH), (0, 0))),
        jnp.pad(gq, ((0, 0), (0, 48 - Q_SH_ROWS), (0, 0)))], axis=1)
    return [gr["g1"], gr["u1"], gr["d1"], gr["g2"], gr["u2"], gr["d2"], gr["out"],
            small.reshape(N_DEV * SMALL_ROWS, D_MODEL)]


def _unpack_grad_shards(gs):
    nl = gs.shape[0]

    def tr(off, n):
        return gs[:, off:off + n].transpose(0, 2, 1)

    return {
        "ffn1_w_gate": tr(OFF_G1, FF_SH), "ffn1_w_up": tr(OFF_U1, FF_SH), "ffn1_w_down": gs[:, OFF_D1:OFF_D1 + FF_SH],
        "ffn2_w_gate": tr(OFF_G2, FF_SH), "ffn2_w_up": tr(OFF_U2, FF_SH), "ffn2_w_down": gs[:, OFF_D2:OFF_D2 + FF_SH],
        "w_out": gs[:, OFF_OUT:OFF_OUT + 128],
        "w_kv_b": gs[:, OFF_KV:OFF_KV + KV_SH_ROWS].reshape(nl, -1, KV_LORA).transpose(0, 2, 1),
        "w_in": tr(OFF_IN, IN_SH),
        "w_q_b": gs[:, OFF_Q:OFF_Q + Q_SH_ROWS].reshape(nl, -1, Q_LORA).transpose(0, 2, 1),
    }


def _small_layout(nl):
    names = [("dmod", nl * N_MOD), ("ffn1_norm", nl), ("mix_norm", nl), ("ffn2_norm", nl), ("q_a_norm", nl),
             ("kv_a_norm", nl), ("pool_scale", nl), ("final_norm", 1), ("loss", 1),
             ("pool_w", nl * 4 * POOL_GC * POOL_GC // D_MODEL)]
    off, table = 0, {}
    for name, n in names:
        table[name] = (off, n)
        off += -(-n // 8) * 8
    return table, off


def _to_rows(t, width=D_MODEL):
    n, w = t.shape
    return jnp.pad(t, ((0, -(-n // 8) * 8 - n), (0, width - w)))


def kernel(x, c, positions, ada_w, ada_b, ffn1_norm, ffn1_w_gate, ffn1_w_up, ffn1_w_down, mix_norm, w_in, pool_w, pool_scale, q_a_norm, w_q_b, kv_a_norm, w_kv_b, w_out, ffn2_norm, ffn2_w_gate, ffn2_w_up, ffn2_w_down, final_norm, loss_target, m_ada_w, m_ada_b, m_ffn1_norm, m_ffn1_w_gate, m_ffn1_w_up, m_ffn1_w_down, m_mix_norm, m_w_in, m_pool_w, m_pool_scale, m_q_a_norm, m_w_q_b, m_kv_a_norm, m_w_kv_b, m_w_out, m_ffn2_norm, m_ffn2_w_gate, m_ffn2_w_up, m_ffn2_w_down, m_final_norm, v_ada_w, v_ada_b, v_ffn1_norm, v_ffn1_w_gate, v_ffn1_w_up, v_ffn1_w_down, v_mix_norm, v_w_in, v_pool_w, v_pool_scale, v_q_a_norm, v_w_q_b, v_kv_a_norm, v_w_kv_b, v_w_out, v_ffn2_norm, v_ffn2_w_gate, v_ffn2_w_up, v_ffn2_w_down, v_final_norm):
    wts = dict(ada_w=ada_w, ada_b=ada_b, ffn1_norm=ffn1_norm, ffn1_w_gate=ffn1_w_gate, ffn1_w_up=ffn1_w_up,
               ffn1_w_down=ffn1_w_down, mix_norm=mix_norm, w_in=w_in, pool_w=pool_w, pool_scale=pool_scale,
               q_a_norm=q_a_norm, w_q_b=w_q_b, kv_a_norm=kv_a_norm, w_kv_b=w_kv_b, w_out=w_out,
               ffn2_norm=ffn2_norm, ffn2_w_gate=ffn2_w_gate, ffn2_w_up=ffn2_w_up, ffn2_w_down=ffn2_w_down,
               final_norm=final_norm)
    mom_m = dict(ada_w=m_ada_w, ada_b=m_ada_b, ffn1_norm=m_ffn1_norm, ffn1_w_gate=m_ffn1_w_gate,
                 ffn1_w_up=m_ffn1_w_up, ffn1_w_down=m_ffn1_w_down, mix_norm=m_mix_norm, w_in=m_w_in,
                 pool_w=m_pool_w, pool_scale=m_pool_scale, q_a_norm=m_q_a_norm, w_q_b=m_w_q_b,
                 kv_a_norm=m_kv_a_norm, w_kv_b=m_w_kv_b, w_out=m_w_out, ffn2_norm=m_ffn2_norm,
                 ffn2_w_gate=m_ffn2_w_gate, ffn2_w_up=m_ffn2_w_up, ffn2_w_down=m_ffn2_w_down,
                 final_norm=m_final_norm)
    mom_v = dict(ada_w=v_ada_w, ada_b=v_ada_b, ffn1_norm=v_ffn1_norm, ffn1_w_gate=v_ffn1_w_gate,
                 ffn1_w_up=v_ffn1_w_up, ffn1_w_down=v_ffn1_w_down, mix_norm=v_mix_norm, w_in=v_w_in,
                 pool_w=v_pool_w, pool_scale=v_pool_scale, q_a_norm=v_q_a_norm, w_q_b=v_w_q_b,
                 kv_a_norm=v_kv_a_norm, w_kv_b=v_w_kv_b, w_out=v_w_out, ffn2_norm=v_ffn2_norm,
                 ffn2_w_gate=v_ffn2_w_gate, ffn2_w_up=v_ffn2_w_up, ffn2_w_down=v_ffn2_w_down,
                 final_norm=v_final_norm)
    order = list(wts)
    nl = ada_w.shape[0]
    seq = x.shape[1]
    me = 4 * lax.axis_index("x") + 2 * lax.axis_index("y") + lax.axis_index("c")
    ada_cols = ada_w.shape[2]

    def after_token(t, token):
        return t + token[0:1, 0:1].astype(t.dtype)

    packs = [_pack_shard_layer(l, wts) for l in range(nl)]
    in_flight = _gather_start(packs[0], packs[0], "gather_start_0")

    c_all = _all_gather(after_token(jnp.broadcast_to(c, (8, D_MODEL)), in_flight[4]), "gather_c")[::8]

    ada_b_mine = lax.dynamic_slice_in_dim(ada_b, me * ada_cols, ada_cols, axis=1).reshape(nl, 1, ada_cols)
    mod_part = _ada_mod(c_all, ada_w, ada_b_mine, "ada_mod")
    mod_all = _all_gather(mod_part.reshape(nl * N_DEV, ada_cols), "gather_mod")
    mod_all = mod_all.reshape(N_DEV, nl, N_DEV, ada_cols)
    mod = lax.dynamic_index_in_dim(mod_all, me, axis=2, keepdims=False)
    mod = mod.transpose(1, 0, 2).reshape(nl, N_MOD, 1, D_MODEL)

    cos, sin = _rope_tables(positions.reshape(seq, 1), "rope_tables")

    def vec(t):
        return t.reshape(1, -1)

    xs = x.reshape(seq, D_MODEL)
    saved = []
    def landed(flight, after, l):
        send_sems, recv_sems, pk, lands, _ = flight
        pk, lands = _gather_wait(send_sems, recv_sems, pk, lands, after, f"gather_wait_{l}")
        return _gather_finish(pk, lands, "gather_finish")

    lands = landed(in_flight, cos, 0)
    for l in range(nl):
        w = _full_weights(lands)
        sv = {"w": w}
        norm1 = vec(ffn1_norm[l])
        if l + 1 < nl:
            in_flight = _gather_start(packs[l + 1], lands[0], f"gather_start_{l + 1}")
            norm1 = after_token(norm1, in_flight[4])

        def ffn_fwd(xin, norm, k0, wg, wu, wd, tag):
            h = _rm_fwd(xin, norm, mod[l, k0], mod[l, k0 + 1], "rm_fwd")
            a = _mm(h, wg, "nt", "ffn_gate")
            b = _mm(h, wu, "nt", "ffn_up")
            t = _swiglu_fwd(a, b, "swiglu_fwd")
            y, xout = _mm(t, wd, "nn", "ffn_down", res=xin, gate=mod[l, k0 + 2], gate_factor=0.5)
            sv[tag] = dict(x=xin, h=h, a=a, b=b, t=t, y=y)
            return xout

        xs = ffn_fwd(xs, norm1, 0, w["g1"], w["u1"], w["d1"], "f1")

        h2 = _rm_fwd(xs, vec(mix_norm[l]), mod[l, 3], mod[l, 4], "rm_fwd")
        z = _mm(h2, w["in"], "nt", "mix_in")
        y_pool, diff = _pool_fwd(z, pool_w[l], vec(pool_scale[l]), "pool_fwd")
        q, k, v, cqn, ckvn = _qkv_fwd(z, vec(q_a_norm[l]), vec(kv_a_norm[l]), w["q"], w["kv"], cos, sin, "qkv_fwd")
        o, lse = _attn_fwd(q, k, v, "attn_fwd")
        ycat = jnp.concatenate([y_pool, o.astype(BF16)], axis=1)
        y2, xmix = _mm(ycat, w["out"], "nn", "mix_out", res=xs, gate=mod[l, 5], gate_factor=1.0)
        sv["mix"] = dict(x=xs, h=h2, z=z, diff=diff, q=q, k=k, v=v, cqn=cqn, ckvn=ckvn, lse=lse, ycat=ycat, y=y2)
        xs = xmix

        xs = ffn_fwd(xs, vec(ffn2_norm[l]), 6, w["g2"], w["u2"], w["d2"], "f2")
        saved.append(sv)
        if l + 1 < nl:
            lands = landed(in_flight, xs, l + 1)

    loss_part, dx, d_final = _loss_head(xs, vec(final_norm), loss_target.reshape(seq, D_MODEL), "loss_head")

    small = {name: [None] * nl for name in ("ffn1_norm", "mix_norm", "ffn2_norm", "q_a_norm", "kv_a_norm",
                                            "pool_scale", "pool_w", "dmod")}
    exchanges = [None] * nl
    last_token = None
    for l in reversed(range(nl)):
        sv = saved[l]
        w = sv["w"]
        dmod = [None] * N_MOD
        gr = {}
        gate3 = mod[l, 8] if last_token is None else after_token(mod[l, 8], last_token)

        def ffn_bwd(dxin, s_, norm, k0, wg, wu, wd, tag, gate):
            dy, dmod[k0 + 2] = _gate_bwd(dxin, s_["y"], gate, 0.5, "gate_bwd")
            dt = _mm(dy, wd, "nt", "ffn_down_dx")
            da, db = _swiglu_bwd(dt, s_["a"], s_["b"], "swiglu_bwd")
            gr["d" + tag] = _mm(s_["t"], dy, "tn", "ffn_down_dw", out_dtype=BF16, tm=256)
            gr["g" + tag] = _mm(da, s_["h"], "tn", "ffn_gate_dw", out_dtype=BF16, tm=256)
            gr["u" + tag] = _mm(db, s_["h"], "tn", "ffn_up_dw", out_dtype=BF16, tm=256)
            dh = _mm(da, wg, "nn", "ffn_gate_dx")
            dh = _mm(db, wu, "nn", "ffn_up_dx", res=dh)
            dxo, dmod[k0], dmod[k0 + 1], dnorm = _rm_bwd(dh, s_["x"], dxin, vec(norm), mod[l, k0 + 1], "rm_bwd")
            return dxo, dnorm

        dx, small["ffn2_norm"][l] = ffn_bwd(dx, sv["f2"], ffn2_norm[l], 6, w["g2"], w["u2"], w["d2"], "2", gate3)

        s_ = sv["mix"]
        dy, dmod[5] = _gate_bwd(dx, s_["y"], mod[l, 5], 1.0, "gate_bwd")
        gr["out"] = _mm(s_["ycat"], dy, "tn", "mix_out_dw", out_dtype=BF16, tm=256)
        dycat = _mm(dy, w["out"], "nt", "mix_out_dx")
        du, small["pool_w"][l], small["pool_scale"][l] = _pool_bwd(dycat, s_["diff"], pool_w[l], vec(pool_scale[l]), "pool_bwd")
        dq, dk, dv = _attn_bwd(s_["q"], s_["k"], s_["v"], s_["lse"], dycat, "attn_bwd")
        dz, dqb, dkvb, small["q_a_norm"][l], small["kv_a_norm"][l] = _qkv_bwd(
            dq, dk, dv, du, s_["z"], vec(q_a_norm[l]), vec(kv_a_norm[l]), w["q"], w["kv"], cos, sin, "qkv_bwd")
        gr["q"] = _mm(dqb, s_["cqn"], "tn", "q_b_dw", out_dtype=BF16, tm=256)
        gr["kv"] = _mm(dkvb, s_["ckvn"], "tn", "kv_b_dw", out_dtype=BF16, tm=256)
        gr["in"] = _mm(dz, s_["h"], "tn", "mix_in_dw", out_dtype=BF16, tm=256)
        dh2 = _mm(dz, w["in"], "nn", "mix_in_dx")
        dx, dmod[3], dmod[4], small["mix_norm"][l] = _rm_bwd(dh2, s_["x"], dx, vec(mix_norm[l]), mod[l, 4], "rm_bwd")

        dx, small["ffn1_norm"][l] = ffn_bwd(dx, sv["f1"], ffn1_norm[l], 0, w["g1"], w["u1"], w["d1"], "1", mod[l, 2])

        small["dmod"][l] = jnp.concatenate(dmod, axis=0)
        srcs = _grad_sources(gr)
        recv = _grad_place_own(srcs, "grad_place_own")
        exchanges[l] = _exchange_start(srcs, recv, f"exchange_start_{l}")
        last_token = exchanges[l][4]

    grad_x = dx.reshape(x.shape)

    layout, small_rows = _small_layout(nl)
    pieces = {
        "dmod": jnp.concatenate(small["dmod"], axis=0),
        "ffn1_norm": jnp.concatenate(small["ffn1_norm"], axis=0),
        "mix_norm": jnp.concatenate(small["mix_norm"], axis=0),
        "ffn2_norm": jnp.concatenate(small["ffn2_norm"], axis=0),
        "q_a_norm": jnp.concatenate(small["q_a_norm"], axis=0),
        "kv_a_norm": jnp.concatenate(small["kv_a_norm"], axis=0),
        "pool_scale": jnp.concatenate(small["pool_scale"], axis=0),
        "final_norm": d_final,
        "loss": jnp.broadcast_to(loss_part[0:1, 0:1], (1, D_MODEL)),
        "pool_w": jnp.stack(small["pool_w"]).reshape(-1, D_MODEL),
    }
    small_buf = jnp.concatenate([_to_rows(pieces[name]) for name in layout], axis=0)
    small_buf = after_token(small_buf, last_token)
    small_all = _all_gather(small_buf, "gather_small").reshape(N_DEV, small_rows, D_MODEL)
    small_sum = _sum_slots(small_all, "sum_small")

    def take(name, width=D_MODEL):
        off, n = layout[name]
        return small_sum[off:off + n, :width]

    grads = {}
    grads["ada_b"] = take("dmod").reshape(nl, N_MOD * D_MODEL)
    grads["ffn1_norm"], grads["mix_norm"], grads["ffn2_norm"] = take("ffn1_norm"), take("mix_norm"), take("ffn2_norm")
    grads["q_a_norm"], grads["kv_a_norm"] = take("q_a_norm", Q_LORA), take("kv_a_norm", KV_LORA)
    grads["pool_scale"] = take("pool_scale", POOL_WIDTH)
    grads["final_norm"] = take("final_norm").reshape(D_MODEL)
    grads["pool_w"] = take("pool_w").reshape(pool_w.shape)
    loss = take("loss")[0, 0]

    off, n = layout["dmod"]
    dmod_all = small_all[:, off:off + n].reshape(N_DEV, nl, N_MOD * D_MODEL)
    dmod_mine = lax.dynamic_slice_in_dim(dmod_all, me * ada_cols, ada_cols, axis=2)
    dmod_pad = jnp.pad(dmod_mine.transpose(1, 0, 2), ((0, 0), (0, LANE - N_DEV), (0, 0)))
    grads["ada_w"] = _ada_grad(jnp.pad(c_all, ((0, LANE - N_DEV), (0, 0))), dmod_pad, "ada_grad")

    updates = {name: _adamw_nd(wts[name], grads[name], mom_m[name], mom_v[name], "adamw") for name in grads}

    gshard = []
    for l in range(nl):
        send_sems, recv_sems, srcs, recv, _ = exchanges[l]
        recv = _exchange_wait(send_sems, recv_sems, srcs, recv, updates["ada_w"][0], f"exchange_wait_{l}")
        gshard.append(_sum_slots(recv, "sum_grads"))
    grads.update(_unpack_grad_shards(jnp.stack(gshard)))
    for name in order:
        if name not in updates:
            updates[name] = _adamw_nd(wts[name], grads[name], mom_m[name], mom_v[name], "adamw")

    return (loss, grad_x, *[grads[n] for n in order], *[updates[n][0] for n in order],
            *[updates[n][1] for n in order], *[updates[n][2] for n in order])
```

```python
import math

import numpy as np
import jax
import jax.numpy as jnp
from jax import lax
from jax.experimental import pallas as pl
from jax.experimental.pallas import tpu as pltpu

F32 = jnp.float32
BF16 = jnp.bfloat16

N_DEV = 8
D_MODEL = 1024
D_FF = 2816
POOL_WIDTH = 512
POOL_WINDOWS = (2, 4, 8, 16)
POOL_GC = 128
N_HEADS = 4
QK_NOPE = 128
QK_ROPE = 64
V_HEAD = 128
QK_HEAD = QK_NOPE + QK_ROPE
HEAD_PAD = 256
Q_LORA = 384
KV_LORA = 256
IN_COLS = POOL_WIDTH + Q_LORA + KV_LORA + QK_ROPE
IN_PAD = 1280
ROPE_THETA = 10000.0
SOFTMAX_SCALE = 1.0 / math.sqrt(QK_HEAD)
EPS = 1e-6
N_MOD = 9

ADAM_LR = 0.001
ADAM_B1 = 0.9
ADAM_B2 = 0.999
ADAM_EPS = 1e-08
ADAM_WD = 0.01
ADAM_STEP = 10

LANE = 128
VMEM_LIMIT = 56 * 1024 * 1024

FF_SH = D_FF // N_DEV
OFF_G1, OFF_U1, OFF_D1 = 0, FF_SH, 2 * FF_SH
OFF_G2, OFF_U2, OFF_D2 = 3 * FF_SH, 4 * FF_SH, 5 * FF_SH
OFF_OUT = 6 * FF_SH
OFF_KV = OFF_OUT + 128
OFF_IN = OFF_KV + 32
OFF_Q = OFF_IN + 160
ROWS_L = OFF_Q + 48
IN_SH = IN_COLS // N_DEV
Q_SH_ROWS = (N_HEADS * QK_HEAD // N_DEV) * Q_LORA // D_MODEL
KV_SH_ROWS = (N_HEADS * (QK_NOPE + V_HEAD) // N_DEV) * KV_LORA // D_MODEL


def _tile(dim, target):
    if dim <= target:
        return dim
    best = None
    for t in range(LANE, target + 1, LANE):
        if dim % t == 0:
            best = t
    assert best is not None, (dim, target)
    return best


def _params(sem):
    return pltpu.CompilerParams(dimension_semantics=sem, vmem_limit_bytes=VMEM_LIMIT)


def _mesh_pos():
    return lax.axis_index("x"), lax.axis_index("y"), lax.axis_index("c")


def _all_gather(x, name):
    m, n = x.shape

    def body(x_ref, out_ref, send_sems, recv_sems, local_sem):
        px, py, pc = _mesh_pos()
        me, sibling = (px, py, pc), (px, py, 1 - pc)
        chips = [(1 - px, py), (px, 1 - py), (1 - px, 1 - py)]

        def rows(bx, by, bc):
            return out_ref.at[pl.ds((4 * bx + 2 * by + bc) * m, m), :]

        def copy(k, block, to, src=None):
            return pltpu.make_async_remote_copy(
                src_ref=rows(*block) if src is None else src, dst_ref=rows(*block),
                send_sem=send_sems.at[k], recv_sem=recv_sems.at[k],
                device_id=to, device_id_type=pl.DeviceIdType.MESH)

        mine = pltpu.make_async_copy(x_ref, rows(*me), local_sem)
        mine.start()
        first = [copy(0, me, sibling, src=x_ref)]
        first += [copy(1 + j, me, (*chip, pc), src=x_ref) for j, chip in enumerate(chips)]
        for cp in first:
            cp.start()
        passed = [copy(4 + j, (*chip, pc), sibling) for j, chip in enumerate(chips)]
        for j, chip in enumerate(chips):
            copy(1 + j, (*chip, pc), me).wait_recv()
            passed[j].start()
        copy(0, sibling, me).wait_recv()
        for j, chip in enumerate(chips):
            copy(4 + j, (*chip, 1 - pc), me).wait_recv()
        for cp in first + passed:
            cp.wait_send()
        mine.wait()

    return pl.pallas_call(
        body, name=name,
        out_shape=jax.ShapeDtypeStruct((N_DEV * m, n), x.dtype),
        in_specs=[pl.BlockSpec(memory_space=pltpu.HBM)],
        out_specs=pl.BlockSpec(memory_space=pltpu.HBM),
        scratch_shapes=[pltpu.SemaphoreType.DMA((7,)), pltpu.SemaphoreType.DMA((7,)),
                        pltpu.SemaphoreType.DMA],
    )(x)


SMALL_ROWS = ROWS_L - OFF_KV
ROWS_A = [FF_SH] * 3
ROWS_B = [FF_SH] * 3 + [128, SMALL_ROWS]
ROWS_ALL = ROWS_A + ROWS_B
SPLIT_AB = sum(ROWS_A)
HBM_SPEC = pl.BlockSpec(memory_space=pltpu.HBM)
SEM_SPEC = pl.BlockSpec(memory_space=pltpu.SEMAPHORE)
ANY_SPEC = pl.BlockSpec(memory_space=pl.ANY)
EFFECT = pltpu.SideEffectType.DATAFLOW_SIDE_EFFECTING


def _hbm(t):
    return pltpu.with_memory_space_constraint(t, pltpu.HBM)


def _whole_wait(ref, send_sem, recv_sem, peer):
    return pltpu.make_async_remote_copy(src_ref=ref, dst_ref=ref, send_sem=send_sem, recv_sem=recv_sem,
                                        device_id=peer, device_id_type=pl.DeviceIdType.MESH)


def _offsets(rows_list):
    return [sum(rows_list[:i]) for i in range(len(rows_list))]


def _gather_start(packed, rows_list, me_id, after, name):
    n = len(rows_list)
    offs = _offsets(rows_list)
    lands = [_hbm(lax.dynamic_update_slice_in_dim(lax.empty((N_DEV * rows, D_MODEL), BF16),
                                                  packed[off:off + rows], me_id * rows, axis=0))
             for off, rows in zip(offs, rows_list)]

    def body(packed_ref, *refs):
        land = refs[:n]
        send_sems, recv_sems = refs[n + 1], refs[n + 2]
        token = refs[-1]
        px, py, pc = _mesh_pos()
        me = 4 * px + 2 * py + pc
        peers = [(px, py, 1 - pc), (1 - px, py, pc), (px, 1 - py, pc), (1 - px, 1 - py, pc)]
        for k, peer in enumerate(peers):
            for off, rows, land_ref in zip(offs, rows_list, land):
                pltpu.make_async_remote_copy(
                    src_ref=packed_ref.at[pl.ds(off, rows), :], dst_ref=land_ref.at[pl.ds(me * rows, rows), :],
                    send_sem=send_sems.at[k], recv_sem=recv_sems.at[k],
                    device_id=peer, device_id_type=pl.DeviceIdType.MESH).start()
        token[...] = jnp.zeros_like(token)

    outs = pl.pallas_call(
        body, name=name,
        out_shape=(pltpu.SemaphoreType.DMA((4,)), pltpu.SemaphoreType.DMA((4,)), pltpu.HBM(packed.shape, BF16),
                   *[pltpu.HBM(t.shape, BF16) for t in lands], jax.ShapeDtypeStruct((8, LANE), F32)),
        in_specs=(HBM_SPEC,) * (1 + n) + (ANY_SPEC,),
        out_specs=(SEM_SPEC, SEM_SPEC) + (HBM_SPEC,) * (1 + n) + (pl.BlockSpec(memory_space=pltpu.VMEM),),
        input_output_aliases={i: 2 + i for i in range(1 + n)},
        compiler_params=pltpu.CompilerParams(has_side_effects=EFFECT),
    )(_hbm(packed), *lands, after)
    return outs[0], outs[1], outs[2], list(outs[3:3 + n]), outs[-1]


def _gather_wait(send_sems, recv_sems, packed, lands, after, name):
    n = len(lands)

    def body(packed_ref, *refs):
        s_sems, r_sems = refs[n], refs[n + 1]
        me = _mesh_pos()
        for k in range(4):
            cp = _whole_wait(packed_ref, s_sems.at[k], r_sems.at[k], me)
            cp.wait_send()
            cp.wait_recv()

    outs = pl.pallas_call(
        body, name=name,
        out_shape=(pltpu.HBM(packed.shape, BF16), *[pltpu.HBM(t.shape, BF16) for t in lands]),
        in_specs=(HBM_SPEC,) * (1 + n) + (SEM_SPEC, SEM_SPEC, ANY_SPEC),
        out_specs=(HBM_SPEC,) * (1 + n),
        input_output_aliases={i: i for i in range(1 + n)},
        compiler_params=pltpu.CompilerParams(has_side_effects=EFFECT),
    )(packed, *lands, send_sems, recv_sems, after)
    return outs[0], list(outs[1:])


def _gather_finish(packed, rows_list, lands, name):
    n = len(rows_list)

    def body(packed_ref, *refs):
        land = refs[n:2 * n]
        send_sems, recv_sems = refs[2 * n:]
        px, py, pc = _mesh_pos()
        sibling = (px, py, 1 - pc)
        for j, (cx, cy) in enumerate([(1 - px, py), (px, 1 - py), (1 - px, 1 - py)]):
            block = 4 * cx + 2 * cy + pc
            for rows, land_ref in zip(rows_list, land):
                blk = land_ref.at[pl.ds(block * rows, rows), :]
                pltpu.make_async_remote_copy(src_ref=blk, dst_ref=blk, send_sem=send_sems.at[j],
                                             recv_sem=recv_sems.at[j], device_id=sibling,
                                             device_id_type=pl.DeviceIdType.MESH).start()
        for j in range(3):
            cp = _whole_wait(packed_ref, send_sems.at[j], recv_sems.at[j], sibling)
            cp.wait_recv()
            cp.wait_send()

    outs = pl.pallas_call(
        body, name=name,
        out_shape=tuple(jax.ShapeDtypeStruct(t.shape, BF16) for t in lands),
        in_specs=(HBM_SPEC,) * (1 + n), out_specs=(HBM_SPEC,) * n,
        input_output_aliases={1 + i: i for i in range(n)},
        scratch_shapes=[pltpu.SemaphoreType.DMA((3,)), pltpu.SemaphoreType.DMA((3,))],
    )(packed, *lands)
    return list(outs)


def _exchange_start(srcs, rows_list, me_id, after, name):
    n = len(rows_list)
    offs = _offsets(rows_list)
    own = jnp.concatenate([lax.dynamic_slice_in_dim(t, me_id * rows, rows, axis=0)
                           for t, rows in zip(srcs, rows_list)], axis=0)
    recv = lax.dynamic_update_slice_in_dim(lax.empty((N_DEV, sum(rows_list), D_MODEL), BF16), own[None], me_id, axis=0)

    def body(*refs):
        src, recv_ref = refs[:n], refs[n]
        send_sems, recv_sems = refs[n + 2], refs[n + 3]
        token = refs[-1]
        px, py, pc = _mesh_pos()
        me = 4 * px + 2 * py + pc
        for k in range(1, N_DEV):
            qx = 1 - px if k & 4 else px
            qy = 1 - py if k & 2 else py
            qc = 1 - pc if k & 1 else pc
            peer_id = 4 * qx + 2 * qy + qc
            for off, rows, src_ref in zip(offs, rows_list, src):
                pltpu.make_async_remote_copy(
                    src_ref=src_ref.at[pl.ds(peer_id * rows, rows), :], dst_ref=recv_ref.at[me, pl.ds(off, rows), :],
                    send_sem=send_sems.at[k - 1], recv_sem=recv_sems.at[k - 1],
                    device_id=(qx, qy, qc), device_id_type=pl.DeviceIdType.MESH).start()
        token[...] = jnp.zeros_like(token)

    outs = pl.pallas_call(
        body, name=name,
        out_shape=(pltpu.SemaphoreType.DMA((N_DEV - 1,)), pltpu.SemaphoreType.DMA((N_DEV - 1,)),
                   *[pltpu.HBM(t.shape, BF16) for t in srcs], pltpu.HBM(recv.shape, BF16),
                   jax.ShapeDtypeStruct((8, LANE), F32)),
        in_specs=(HBM_SPEC,) * (n + 1) + (ANY_SPEC,),
        out_specs=(SEM_SPEC, SEM_SPEC) + (HBM_SPEC,) * (n + 1) + (pl.BlockSpec(memory_space=pltpu.VMEM),),
        input_output_aliases={i: 2 + i for i in range(n + 1)},
        compiler_params=pltpu.CompilerParams(has_side_effects=EFFECT),
    )(*[_hbm(t) for t in srcs], _hbm(recv), after)
    return outs[0], outs[1], list(outs[2:2 + n]), outs[2 + n], outs[-1]


def _exchange_wait(send_sems, recv_sems, srcs, recv, after, name):
    n = len(srcs)

    def body(*refs):
        recv_ref = refs[n]
        s_sems, r_sems = refs[n + 1], refs[n + 2]
        me = _mesh_pos()
        for k in range(N_DEV - 1):
            cp = _whole_wait(recv_ref.at[0], s_sems.at[k], r_sems.at[k], me)
            cp.wait_send()
            cp.wait_recv()

    outs = pl.pallas_call(
        body, name=name,
        out_shape=(*[pltpu.HBM(t.shape, BF16) for t in srcs], pltpu.HBM(recv.shape, BF16)),
        in_specs=(HBM_SPEC,) * (n + 1) + (SEM_SPEC, SEM_SPEC, ANY_SPEC),
        out_specs=(HBM_SPEC,) * (n + 1),
        input_output_aliases={i: i for i in range(n + 1)},
        compiler_params=pltpu.CompilerParams(has_side_effects=EFFECT),
    )(*srcs, recv, send_sems, recv_sems, after)
    return outs[n]


def _sum_slots(recv, name, after=None):
    _, r, n = recv.shape
    tr = _row_tile(r, 512)

    def body(in_ref, *refs):
        acc = in_ref[0].astype(F32)
        for j in range(1, N_DEV):
            acc = acc + in_ref[j].astype(F32)
        refs[-1][...] = acc

    if tr >= LANE or tr == r:
        grid = (r // tr,)
        in_specs, out_spec = [pl.BlockSpec((N_DEV, tr, n), lambda i: (0, i, 0))], pl.BlockSpec((tr, n), lambda i: (i, 0))
    else:
        tl = 2 * LANE
        grid = (n // tl,)
        in_specs, out_spec = [pl.BlockSpec((N_DEV, r, tl), lambda i: (0, 0, i))], pl.BlockSpec((r, tl), lambda i: (0, i))
    args = [recv]
    if after is not None:
        in_specs.append(ANY_SPEC)
        args.append(after)
    return pl.pallas_call(
        body, name=name, grid=grid,
        out_shape=jax.ShapeDtypeStruct((r, n), F32),
        in_specs=in_specs, out_specs=out_spec,
        compiler_params=_params(("parallel",)),
    )(*args)


def _row_tile(rows, target):
    if rows <= target:
        return rows
    best = None
    for t in range(16, target + 1, 16):
        if rows % t == 0:
            best = t
    assert best is not None, rows
    return best


_DIMS = {"nn": ((1,), (0,)), "nt": ((1,), (1,)), "tn": ((0,), (0,))}


def _mm(a, b, mode, name, out_dtype=F32, res=None, gate=None, gate_factor=1.0, tm=512, tn=1408):
    if mode == "tn":
        kdim, m = a.shape
    else:
        m, kdim = a.shape
    n = b.shape[0] if mode == "nt" else b.shape[1]
    tm, tn = _tile(m, tm), _tile(n, tn)
    a_spec = (pl.BlockSpec((kdim, tm), lambda i, j: (0, i)) if mode == "tn"
              else pl.BlockSpec((tm, kdim), lambda i, j: (i, 0)))
    b_spec = (pl.BlockSpec((tn, kdim), lambda i, j: (j, 0)) if mode == "nt"
              else pl.BlockSpec((kdim, tn), lambda i, j: (0, j)))
    o_spec = pl.BlockSpec((tm, tn), lambda i, j: (i, j))
    dims = (_DIMS[mode], ((), ()))
    has_res, has_gate = res is not None, gate is not None

    def body(*refs):
        a_ref, b_ref = refs[0], refs[1]
        y = lax.dot_general(a_ref[...].astype(BF16), b_ref[...].astype(BF16), dims,
                            preferred_element_type=F32)
        if not has_res:
            refs[2][...] = y.astype(out_dtype)
            return
        res_ref = refs[2]
        if has_gate:
            y_ref, o_ref = refs[4], refs[5]
            y_ref[...] = y
            o_ref[...] = res_ref[...] + (gate_factor * refs[3][...]) * y
        else:
            refs[3][...] = res_ref[...] + y

    in_specs, args = [a_spec, b_spec], [a, b]
    if has_res:
        in_specs.append(o_spec)
        args.append(res)
        if has_gate:
            in_specs.append(pl.BlockSpec((1, tn), lambda i, j: (0, j)))
            args.append(gate)
            out_shape = (jax.ShapeDtypeStruct((m, n), F32), jax.ShapeDtypeStruct((m, n), F32))
            out_specs = (o_spec, o_spec)
        else:
            out_shape, out_specs = jax.ShapeDtypeStruct((m, n), F32), o_spec
    else:
        out_shape, out_specs = jax.ShapeDtypeStruct((m, n), out_dtype), o_spec
    return pl.pallas_call(
        body, name=name, grid=(m // tm, n // tn), out_shape=out_shape,
        in_specs=in_specs, out_specs=out_specs,
        compiler_params=_params(("parallel", "parallel")),
    )(*args)


def _vec_spec(width):
    return pl.BlockSpec((1, width), lambda i: (0, 0))


def _rm_fwd(x, gw, shift, scale, name):
    s, d = x.shape
    ts = _tile(s, 256)

    def body(x_ref, gw_ref, sh_ref, sc_ref, h_ref):
        xv = x_ref[...]
        r = lax.rsqrt(jnp.mean(xv * xv, axis=-1, keepdims=True) + EPS)
        y = (xv * r) * gw_ref[...]
        h_ref[...] = (y * (1 + sc_ref[...]) + sh_ref[...]).astype(BF16)

    row = pl.BlockSpec((ts, d), lambda i: (i, 0))
    return pl.pallas_call(
        body, name=name, grid=(s // ts,), out_shape=jax.ShapeDtypeStruct((s, d), BF16),
        in_specs=[row, _vec_spec(d), _vec_spec(d), _vec_spec(d)], out_specs=row,
        compiler_params=_params(("parallel",)),
    )(x, gw, shift, scale)


def _rm_bwd(dh, x, dres, gw, scale, name):
    s, d = x.shape
    ts = _tile(s, 256)

    def body(dh_ref, x_ref, dres_ref, gw_ref, sc_ref, dx_ref, dsh_ref, dsc_ref, dgw_ref):
        @pl.when(pl.program_id(0) == 0)
        def _():
            dsh_ref[...] = jnp.zeros_like(dsh_ref)
            dsc_ref[...] = jnp.zeros_like(dsc_ref)
            dgw_ref[...] = jnp.zeros_like(dgw_ref)

        xv, dhv, gwv = x_ref[...], dh_ref[...], gw_ref[...]
        r = lax.rsqrt(jnp.mean(xv * xv, axis=-1, keepdims=True) + EPS)
        xn = xv * r
        y = xn * gwv
        dsh_ref[...] += jnp.sum(dhv, axis=0, keepdims=True)
        dsc_ref[...] += jnp.sum(dhv * y, axis=0, keepdims=True)
        dy = dhv * (1 + sc_ref[...])
        dgw_ref[...] += jnp.sum(dy * xn, axis=0, keepdims=True)
        dxn = dy * gwv
        dx = r * (dxn - xn * jnp.mean(dxn * xn, axis=-1, keepdims=True))
        dx_ref[...] = dres_ref[...] + dx

    row = pl.BlockSpec((ts, d), lambda i: (i, 0))
    vec = jax.ShapeDtypeStruct((1, d), F32)
    return pl.pallas_call(
        body, name=name, grid=(s // ts,),
        out_shape=(jax.ShapeDtypeStruct((s, d), F32), vec, vec, vec),
        in_specs=[row, row, row, _vec_spec(d), _vec_spec(d)],
        out_specs=(row, _vec_spec(d), _vec_spec(d), _vec_spec(d)),
        compiler_params=_params(("arbitrary",)),
    )(dh, x, dres, gw, scale)


def _gate_bwd(dx, y, gate, factor, name):
    s, d = dx.shape
    ts = _tile(s, 256)

    def body(dx_ref, y_ref, g_ref, dy_ref, dg_ref):
        @pl.when(pl.program_id(0) == 0)
        def _():
            dg_ref[...] = jnp.zeros_like(dg_ref)

        dxv = dx_ref[...]
        dy_ref[...] = ((factor * g_ref[...]) * dxv).astype(BF16)
        dg_ref[...] += jnp.sum((factor * dxv) * y_ref[...], axis=0, keepdims=True)

    row = pl.BlockSpec((ts, d), lambda i: (i, 0))
    return pl.pallas_call(
        body, name=name, grid=(s // ts,),
        out_shape=(jax.ShapeDtypeStruct((s, d), BF16), jax.ShapeDtypeStruct((1, d), F32)),
        in_specs=[row, row, _vec_spec(d)], out_specs=(row, _vec_spec(d)),
        compiler_params=_params(("arbitrary",)),
    )(dx, y, gate)


def _swiglu_fwd(a, b, name):
    s, f = a.shape
    ts, tf = _tile(s, 256), _tile(f, 1408)

    def body(a_ref, b_ref, t_ref):
        av = a_ref[...]
        t_ref[...] = ((av * jax.nn.sigmoid(av)) * b_ref[...]).astype(BF16)

    blk = pl.BlockSpec((ts, tf), lambda i, j: (i, j))
    return pl.pallas_call(
        body, name=name, grid=(s // ts, f // tf), out_shape=jax.ShapeDtypeStruct((s, f), BF16),
        in_specs=[blk, blk], out_specs=blk, compiler_params=_params(("parallel", "parallel")),
    )(a, b)


def _swiglu_bwd(dt, a, b, name):
    s, f = a.shape
    ts, tf = _tile(s, 256), _tile(f, 1408)

    def body(dt_ref, a_ref, b_ref, da_ref, db_ref):
        av, dtv = a_ref[...], dt_ref[...]
        sg = jax.nn.sigmoid(av)
        silu = av * sg
        db_ref[...] = (dtv * silu).astype(BF16)
        da_ref[...] = ((dtv * b_ref[...]) * (sg * (1 + av * (1 - sg)))).astype(BF16)

    blk = pl.BlockSpec((ts, tf), lambda i, j: (i, j))
    out = jax.ShapeDtypeStruct((s, f), BF16)
    return pl.pallas_call(
        body, name=name, grid=(s // ts, f // tf), out_shape=(out, out),
        in_specs=[blk, blk, blk], out_specs=(blk, blk),
        compiler_params=_params(("parallel", "parallel")),
    )(dt, a, b)


def _pool_counts(s):
    return (lax.broadcasted_iota(jnp.int32, (s, POOL_GC), 0))


def _pool_fwd(z, pool_w, pool_scale, name):
    s = z.shape[0]

    def body(u_ref, w_ref, sc_ref, y_ref, diff_ref):
        t = lax.broadcasted_iota(jnp.int32, (s, POOL_GC), 0)
        for g, win in enumerate(POOL_WINDOWS):
            cols = slice(g * POOL_GC, (g + 1) * POOL_GC)
            u = u_ref[:, cols]
            acc, step = u, 1
            while step < win:
                acc = acc + jnp.where(t >= step, pltpu.roll(acc, step, 0), 0.0)
                step *= 2
            cnt = jnp.minimum(t + 1, win).astype(F32)
            diff = acc / cnt - u
            diff_ref[:, cols] = diff
            ypre = jnp.dot(diff.astype(BF16), w_ref[g].astype(BF16), preferred_element_type=F32)
            y_ref[:, cols] = (ypre * sc_ref[:, cols]).astype(BF16)

    return pl.pallas_call(
        body, name=name, grid=(1,),
        out_shape=(jax.ShapeDtypeStruct((s, POOL_WIDTH), BF16), jax.ShapeDtypeStruct((s, POOL_WIDTH), F32)),
        in_specs=[pl.BlockSpec((s, POOL_WIDTH), lambda i: (0, 0)),
                  pl.BlockSpec(pool_w.shape, lambda i: (0, 0, 0)),
                  pl.BlockSpec((1, POOL_WIDTH), lambda i: (0, 0))],
        out_specs=(pl.BlockSpec((s, POOL_WIDTH), lambda i: (0, 0)),
                   pl.BlockSpec((s, POOL_WIDTH), lambda i: (0, 0))),
        compiler_params=_params(("arbitrary",)),
    )(z, pool_w, pool_scale)


def _pool_bwd(dycat, diff, pool_w, pool_scale, name):
    s = diff.shape[0]

    def body(dy_ref, diff_ref, w_ref, sc_ref, du_ref, dw_ref, dsc_ref):
        t = lax.broadcasted_iota(jnp.int32, (s, POOL_GC), 0)
        for g, win in enumerate(POOL_WINDOWS):
            cols = slice(g * POOL_GC, (g + 1) * POOL_GC)
            dy, dfb, wb = dy_ref[:, cols], diff_ref[:, cols].astype(BF16), w_ref[g].astype(BF16)
            ypre = jnp.dot(dfb, wb, preferred_element_type=F32)
            dsc_ref[:, cols] = jnp.sum(dy * ypre, axis=0, keepdims=True)
            dypre = (dy * sc_ref[:, cols]).astype(BF16)
            ddiff = lax.dot_general(dypre, wb, (((1,), (1,)), ((), ())), preferred_element_type=F32)
            dw_ref[g] = lax.dot_general(dfb, dypre, (((0,), (0,)), ((), ())), preferred_element_type=F32)
            cnt = jnp.minimum(t + 1, win).astype(F32)
            acc, step = ddiff / cnt, 1
            while step < win:
                acc = acc + jnp.where(t < s - step, pltpu.roll(acc, s - step, 0), 0.0)
                step *= 2
            du_ref[:, cols] = acc - ddiff

    full = pl.BlockSpec((s, POOL_WIDTH), lambda i: (0, 0))
    return pl.pallas_call(
        body, name=name, grid=(1,),
        out_shape=(jax.ShapeDtypeStruct((s, POOL_WIDTH), F32),
                   jax.ShapeDtypeStruct(pool_w.shape, F32),
                   jax.ShapeDtypeStruct((1, POOL_WIDTH), F32)),
        in_specs=[full, full, pl.BlockSpec(pool_w.shape, lambda i: (0, 0, 0)),
                  pl.BlockSpec((1, POOL_WIDTH), lambda i: (0, 0))],
        out_specs=(full, pl.BlockSpec(pool_w.shape, lambda i: (0, 0, 0)),
                   pl.BlockSpec((1, POOL_WIDTH), lambda i: (0, 0))),
        compiler_params=_params(("arbitrary",)),
    )(dycat, diff, pool_w, pool_scale)


def _rope_tables(positions, name):
    s = positions.shape[0]
    ts = _tile(s, 512)
    freq = 1.0 / (ROPE_THETA ** (np.arange(0, QK_ROPE, 2, dtype=np.float32) / QK_ROPE))
    table = np.zeros((1, LANE), np.float32)
    table[0, :QK_ROPE // 2] = freq
    table[0, QK_ROPE // 2:QK_ROPE] = freq

    def body(pos_ref, f_ref, cos_ref, sin_ref):
        ang = pos_ref[...].astype(F32) * f_ref[...]
        cos_ref[...] = jnp.cos(ang)
        sin_ref[...] = jnp.sin(ang)

    out = jax.ShapeDtypeStruct((s, LANE), F32)
    blk = pl.BlockSpec((ts, LANE), lambda i: (i, 0))
    return pl.pallas_call(
        body, name=name, grid=(s // ts,), out_shape=(out, out),
        in_specs=[pl.BlockSpec((ts, 1), lambda i: (i, 0)), _vec_spec(LANE)], out_specs=(blk, blk),
        compiler_params=_params(("parallel",)),
    )(positions, jnp.asarray(table))


def _lane_mod64_low(shape):
    return (lax.broadcasted_iota(jnp.int32, shape, 1) % QK_ROPE) < (QK_ROPE // 2)


def _rope(x, cos, sin):
    rot = jnp.where(_lane_mod64_low(x.shape), -pltpu.roll(x, LANE - 32, 1), pltpu.roll(x, 32, 1))
    return x * cos + rot * sin


def _rope_t(dy, cos, sin):
    w = dy * sin
    rot_t = jnp.where(_lane_mod64_low(dy.shape), pltpu.roll(w, LANE - 32, 1), -pltpu.roll(w, 32, 1))
    return dy * cos + rot_t


def _plain_rms(x, g):
    r = lax.rsqrt(jnp.mean(x * x, axis=-1, keepdims=True) + EPS)
    return (x * r) * g, x * r, r


O_Q, O_KV, O_KR = POOL_WIDTH, POOL_WIDTH + Q_LORA, POOL_WIDTH + Q_LORA + KV_LORA


def _qkv_fwd(z, qn, kvn, wq, wkv, cos, sin, name):
    s = z.shape[0]
    ts = _tile(s, 256)

    def body(z_ref, qn_ref, kvn_ref, wq_ref, wkv_ref, cos_ref, sin_ref, q_ref, k_ref, v_ref, cqn_ref, ckvn_ref):
        cosv, sinv = cos_ref[...], sin_ref[...]
        cqn = _plain_rms(z_ref[:, O_Q:O_KV], qn_ref[...])[0].astype(BF16)
        ckvn = _plain_rms(z_ref[:, O_KV:O_KR], kvn_ref[...])[0].astype(BF16)
        cqn_ref[...] = cqn
        ckvn_ref[...] = ckvn
        nt = (((1,), (1,)), ((), ()))
        q = lax.dot_general(cqn, wq_ref[...], nt, preferred_element_type=F32)
        kv = lax.dot_general(ckvn, wkv_ref[...], nt, preferred_element_type=F32)
        kr = _rope(z_ref[:, O_KR:IN_PAD], cosv, sinv).astype(BF16)
        for h in range(N_HEADS):
            o = h * HEAD_PAD
            q_ref[:, o:o + QK_NOPE] = q[:, o:o + QK_NOPE].astype(BF16)
            q_ref[:, o + QK_NOPE:o + HEAD_PAD] = _rope(q[:, o + QK_NOPE:o + HEAD_PAD], cosv, sinv).astype(BF16)
            k_ref[:, o:o + QK_NOPE] = kv[:, o:o + QK_NOPE].astype(BF16)
            k_ref[:, o + QK_NOPE:o + HEAD_PAD] = kr
            v_ref[:, h * V_HEAD:(h + 1) * V_HEAD] = kv[:, o + QK_NOPE:o + HEAD_PAD].astype(BF16)

    def row(w):
        return pl.BlockSpec((ts, w), lambda i: (i, 0))

    def whole(arr):
        return pl.BlockSpec(arr.shape, lambda i: (0, 0))

    hp = N_HEADS * HEAD_PAD
    return pl.pallas_call(
        body, name=name, grid=(s // ts,),
        out_shape=(jax.ShapeDtypeStruct((s, hp), BF16), jax.ShapeDtypeStruct((s, hp), BF16),
                   jax.ShapeDtypeStruct((s, N_HEADS * V_HEAD), BF16),
                   jax.ShapeDtypeStruct((s, Q_LORA), BF16), jax.ShapeDtypeStruct((s, KV_LORA), BF16)),
        in_specs=[row(IN_PAD), whole(qn), whole(kvn), whole(wq), whole(wkv), row(LANE), row(LANE)],
        out_specs=(row(hp), row(hp), row(N_HEADS * V_HEAD), row(Q_LORA), row(KV_LORA)),
        compiler_params=_params(("parallel",)),
    )(z, qn, kvn, wq, wkv, cos, sin)


def _qkv_bwd(dq, dk, dv, du, z, qn, kvn, wq, wkv, cos, sin, name):
    s = z.shape[0]
    ts = _tile(s, 256)

    def norm_bwd(x, g, dy):
        _, xn, r = _plain_rms(x, g)
        dxn = dy * g
        return r * (dxn - xn * jnp.mean(dxn * xn, axis=-1, keepdims=True)), jnp.sum(dy * xn, axis=0, keepdims=True)

    def body(dq_ref, dk_ref, dv_ref, du_ref, z_ref, qn_ref, kvn_ref, wq_ref, wkv_ref, cos_ref, sin_ref,
             dz_ref, dqb_ref, dkvb_ref, dqn_ref, dkvn_ref):
        @pl.when(pl.program_id(0) == 0)
        def _():
            dqn_ref[...] = jnp.zeros_like(dqn_ref)
            dkvn_ref[...] = jnp.zeros_like(dkvn_ref)

        cosv, sinv = cos_ref[...], sin_ref[...]
        dkr = jnp.zeros((ts, LANE), F32)
        for h in range(N_HEADS):
            o = h * HEAD_PAD
            dqb_ref[:, o:o + QK_NOPE] = dq_ref[:, o:o + QK_NOPE].astype(BF16)
            dqb_ref[:, o + QK_NOPE:o + HEAD_PAD] = _rope_t(dq_ref[:, o + QK_NOPE:o + HEAD_PAD], cosv, sinv).astype(BF16)
            dkvb_ref[:, o:o + QK_NOPE] = dk_ref[:, o:o + QK_NOPE].astype(BF16)
            dkvb_ref[:, o + QK_NOPE:o + HEAD_PAD] = dv_ref[:, h * V_HEAD:(h + 1) * V_HEAD].astype(BF16)
            dkr = dkr + dk_ref[:, o + QK_NOPE:o + HEAD_PAD]
        dcqn = jnp.dot(dqb_ref[...], wq_ref[...], preferred_element_type=F32)
        dckvn = jnp.dot(dkvb_ref[...], wkv_ref[...], preferred_element_type=F32)
        dcq, dqn = norm_bwd(z_ref[:, O_Q:O_KV], qn_ref[...], dcqn)
        dckv, dkvn = norm_bwd(z_ref[:, O_KV:O_KR], kvn_ref[...], dckvn)
        dqn_ref[...] += dqn
        dkvn_ref[...] += dkvn
        dz_ref[:, 0:O_Q] = du_ref[...].astype(BF16)
        dz_ref[:, O_Q:O_KV] = dcq.astype(BF16)
        dz_ref[:, O_KV:O_KR] = dckv.astype(BF16)
        dz_ref[:, O_KR:IN_PAD] = _rope_t(dkr, cosv, sinv).astype(BF16)

    def row(w):
        return pl.BlockSpec((ts, w), lambda i: (i, 0))

    def whole(arr):
        return pl.BlockSpec(arr.shape, lambda i: (0, 0))

    hp = N_HEADS * HEAD_PAD
    return pl.pallas_call(
        body, name=name, grid=(s // ts,),
        out_shape=(jax.ShapeDtypeStruct((s, IN_PAD), BF16), jax.ShapeDtypeStruct((s, hp), BF16),
                   jax.ShapeDtypeStruct((s, hp), BF16),
                   jax.ShapeDtypeStruct((1, Q_LORA), F32), jax.ShapeDtypeStruct((1, KV_LORA), F32)),
        in_specs=[row(hp), row(hp), row(N_HEADS * V_HEAD), row(POOL_WIDTH), row(IN_PAD),
                  whole(qn), whole(kvn), whole(wq), whole(wkv), row(LANE), row(LANE)],
        out_specs=(row(IN_PAD), row(hp), row(hp), whole(qn), whole(kvn)),
        compiler_params=_params(("arbitrary",)),
    )(dq, dk, dv, du, z, qn, kvn, wq, wkv, cos, sin)


def _causal_scores(q, k, i, tq, s):
    sc = lax.dot_general(q, k, (((1,), (1,)), ((), ())), preferred_element_type=F32) * SOFTMAX_SCALE
    qpos = i * tq + lax.broadcasted_iota(jnp.int32, (tq, s), 0)
    kpos = lax.broadcasted_iota(jnp.int32, (tq, s), 1)
    return jnp.where(qpos >= kpos, sc, -jnp.inf)


def _attn_fwd(q, k, v, name):
    s = q.shape[0]
    tq = _tile(s, 256)

    def body(q_ref, k_ref, v_ref, o_ref, lse_ref):
        sc = _causal_scores(q_ref[...], k_ref[...], pl.program_id(1), tq, s)
        mx = jnp.max(sc, axis=-1, keepdims=True)
        p = jnp.exp(sc - mx)
        den = jnp.sum(p, axis=-1, keepdims=True)
        o_ref[...] = jnp.dot((p / den).astype(BF16), v_ref[...], preferred_element_type=F32)
        lse_ref[...] = mx + jnp.log(den)

    return pl.pallas_call(
        body, name=name, grid=(N_HEADS, s // tq),
        out_shape=(jax.ShapeDtypeStruct((s, N_HEADS * V_HEAD), F32), jax.ShapeDtypeStruct((N_HEADS, s, 1), F32)),
        in_specs=[pl.BlockSpec((tq, HEAD_PAD), lambda h, i: (i, h)),
                  pl.BlockSpec((s, HEAD_PAD), lambda h, i: (0, h)),
                  pl.BlockSpec((s, V_HEAD), lambda h, i: (0, h))],
        out_specs=(pl.BlockSpec((tq, V_HEAD), lambda h, i: (i, h)),
                   pl.BlockSpec((None, tq, 1), lambda h, i: (h, i, 0))),
        compiler_params=_params(("parallel", "parallel")),
    )(q, k, v)


def _attn_bwd(q, k, v, lse, dycat, name):
    s = q.shape[0]
    tq = _tile(s, 256)
    tn_dims = (((0,), (0,)), ((), ()))

    def body(q_ref, k_ref, v_ref, lse_ref, do_ref, dq_ref, dk_ref, dv_ref):
        @pl.when(pl.program_id(1) == 0)
        def _():
            dk_ref[...] = jnp.zeros_like(dk_ref)
            dv_ref[...] = jnp.zeros_like(dv_ref)

        qv, kv_, dob = q_ref[...], k_ref[...], do_ref[...].astype(BF16)
        sc = _causal_scores(qv, kv_, pl.program_id(1), tq, s)
        p = jnp.exp(sc - lse_ref[...])
        dp = lax.dot_general(dob, v_ref[...], (((1,), (1,)), ((), ())), preferred_element_type=F32)
        ds = (p * (dp - jnp.sum(dp * p, axis=-1, keepdims=True)) * SOFTMAX_SCALE).astype(BF16)
        dq_ref[...] = jnp.dot(ds, kv_, preferred_element_type=F32)
        dk_ref[...] += lax.dot_general(ds, qv, tn_dims, preferred_element_type=F32)
        dv_ref[...] += lax.dot_general(p.astype(BF16), dob, tn_dims, preferred_element_type=F32)

    n_pool_blocks = POOL_WIDTH // V_HEAD
    return pl.pallas_call(
        body, name=name, grid=(N_HEADS, s // tq),
        out_shape=(jax.ShapeDtypeStruct((s, N_HEADS * HEAD_PAD), F32),
                   jax.ShapeDtypeStruct((s, N_HEADS * HEAD_PAD), F32),
                   jax.ShapeDtypeStruct((s, N_HEADS * V_HEAD), F32)),
        in_specs=[pl.BlockSpec((tq, HEAD_PAD), lambda h, i: (i, h)),
                  pl.BlockSpec((s, HEAD_PAD), lambda h, i: (0, h)),
                  pl.BlockSpec((s, V_HEAD), lambda h, i: (0, h)),
                  pl.BlockSpec((None, tq, 1), lambda h, i: (h, i, 0)),
                  pl.BlockSpec((tq, V_HEAD), lambda h, i: (i, n_pool_blocks + h))],
        out_specs=(pl.BlockSpec((tq, HEAD_PAD), lambda h, i: (i, h)),
                   pl.BlockSpec((s, HEAD_PAD), lambda h, i: (0, h)),
                   pl.BlockSpec((s, V_HEAD), lambda h, i: (0, h))),
        compiler_params=_params(("parallel", "arbitrary")),
    )(q, k, v, lse, dycat)


def _loss_head(x, gw, target, name):
    s, d = x.shape
    ts = _tile(s, 256)

    def body(x_ref, gw_ref, tgt_ref, loss_ref, dx_ref, dgw_ref):
        @pl.when(pl.program_id(0) == 0)
        def _():
            loss_ref[...] = jnp.zeros_like(loss_ref)
            dgw_ref[...] = jnp.zeros_like(dgw_ref)

        xv, gwv = x_ref[...], gw_ref[...]
        r = lax.rsqrt(jnp.mean(xv * xv, axis=-1, keepdims=True) + EPS)
        xn = xv * r
        err = xn * gwv - tgt_ref[...]
        loss_ref[...] += 0.5 * jnp.sum(jnp.mean(err * err, axis=-1, keepdims=True))
        dy = err / d
        dgw_ref[...] += jnp.sum(dy * xn, axis=0, keepdims=True)
        dxn = dy * gwv
        dx_ref[...] = r * (dxn - xn * jnp.mean(dxn * xn, axis=-1, keepdims=True))

    row = pl.BlockSpec((ts, d), lambda i: (i, 0))
    return pl.pallas_call(
        body, name=name, grid=(s // ts,),
        out_shape=(jax.ShapeDtypeStruct((8, LANE), F32), jax.ShapeDtypeStruct((s, d), F32),
                   jax.ShapeDtypeStruct((1, d), F32)),
        in_specs=[row, _vec_spec(d), row],
        out_specs=(pl.BlockSpec((8, LANE), lambda i: (0, 0)), row, _vec_spec(d)),
        compiler_params=_params(("arbitrary",)),
    )(x, gw, target)


def _ada_mod(c_all, ada_w, ada_b, name):
    nl, d, cols = ada_w.shape

    def body(c_ref, w_ref, b_ref, o_ref):
        cv = c_ref[...]
        act = (cv * jax.nn.sigmoid(cv)).astype(BF16)
        o_ref[...] = jnp.dot(act, w_ref[...].astype(BF16), preferred_element_type=F32) + b_ref[...]

    return pl.pallas_call(
        body, name=name, grid=(nl,), out_shape=jax.ShapeDtypeStruct((nl, N_DEV, cols), F32),
        in_specs=[pl.BlockSpec((N_DEV, d), lambda l: (0, 0)),
                  pl.BlockSpec((None, d, cols), lambda l: (l, 0, 0)),
                  pl.BlockSpec((None, 1, cols), lambda l: (l, 0, 0))],
        out_specs=pl.BlockSpec((None, N_DEV, cols), lambda l: (l, 0, 0)),
        compiler_params=_params(("parallel",)),
    )(c_all, ada_w, ada_b)


def _ada_grad(c_pad, dmod_pad, name):
    nl, kpad, cols = dmod_pad.shape
    d = c_pad.shape[1]

    def body(c_ref, dm_ref, o_ref):
        cv = c_ref[...]
        act = (cv * jax.nn.sigmoid(cv)).astype(BF16)
        o_ref[...] = lax.dot_general(act, dm_ref[...].astype(BF16), (((0,), (0,)), ((), ())),
                                     preferred_element_type=F32)

    return pl.pallas_call(
        body, name=name, grid=(nl,), out_shape=jax.ShapeDtypeStruct((nl, d, cols), F32),
        in_specs=[pl.BlockSpec((kpad, d), lambda l: (0, 0)),
                  pl.BlockSpec((None, kpad, cols), lambda l: (l, 0, 0))],
        out_specs=pl.BlockSpec((None, d, cols), lambda l: (l, 0, 0)),
        compiler_params=_params(("parallel",)),
    )(c_pad, dmod_pad)


def _adamw(w, g, m, v, name):
    rows, cols = w.shape
    tr = _row_tile(rows, 256)

    def body(w_ref, g_ref, m_ref, v_ref, d_ref, nm_ref, nv_ref):
        gv = g_ref[...]
        nm = ADAM_B1 * m_ref[...] + (1.0 - ADAM_B1) * gv
        nv = ADAM_B2 * v_ref[...] + (1.0 - ADAM_B2) * (gv * gv)
        m_hat = nm / (1.0 - ADAM_B1 ** ADAM_STEP)
        v_hat = nv / (1.0 - ADAM_B2 ** ADAM_STEP)
        d_ref[...] = -ADAM_LR * (m_hat / (jnp.sqrt(v_hat) + ADAM_EPS) + ADAM_WD * w_ref[...])
        nm_ref[...] = nm
        nv_ref[...] = nv

    blk = pl.BlockSpec((tr, cols), lambda i: (i, 0))
    out = jax.ShapeDtypeStruct((rows, cols), F32)
    return pl.pallas_call(
        body, name=name, grid=(rows // tr,), out_shape=(out, out, out),
        in_specs=[blk, blk, blk, blk], out_specs=(blk, blk, blk),
        compiler_params=_params(("parallel",)),
    )(w, g, m, v)


def _adamw_nd(w, g, m, v, name):
    shape = w.shape
    flat = (lambda t: t.reshape(1, -1)) if w.ndim == 1 else (lambda t: t.reshape(-1, shape[-1]))
    return tuple(t.reshape(shape) for t in _adamw(flat(w), flat(g), flat(m), flat(v), name))


def _pad_rows(t, rows):
    return jnp.pad(t, ((0, rows - t.shape[0]), (0, 0)))


def _pack_shard_layer(l, wts):
    def tr(name):
        return wts[name][l].astype(BF16).T

    parts = [tr("ffn1_w_gate"), tr("ffn1_w_up"), wts["ffn1_w_down"][l].astype(BF16),
             tr("ffn2_w_gate"), tr("ffn2_w_up"), wts["ffn2_w_down"][l].astype(BF16),
             wts["w_out"][l].astype(BF16),
             tr("w_kv_b").reshape(KV_SH_ROWS, D_MODEL),
             _pad_rows(tr("w_in"), 160),
             _pad_rows(tr("w_q_b").reshape(Q_SH_ROWS, D_MODEL), 48)]
    return jnp.concatenate(parts, axis=0)


def _full_weights(lands):
    w = dict(zip(("g1", "u1", "d1", "g2", "u2", "d2", "out"), lands))
    small = lands[-1].reshape(N_DEV, SMALL_ROWS, D_MODEL)
    o_in, o_q = OFF_IN - OFF_KV, OFF_Q - OFF_KV
    w["kv"] = small[:, :KV_SH_ROWS].reshape(N_HEADS * HEAD_PAD, KV_LORA)
    w["in"] = _pad_rows(small[:, o_in:o_in + IN_SH].reshape(IN_COLS, D_MODEL), IN_PAD)
    wq = small[:, o_q:o_q + Q_SH_ROWS].reshape(N_HEADS, QK_HEAD, Q_LORA)
    w["q"] = jnp.pad(wq, ((0, 0), (0, HEAD_PAD - QK_HEAD), (0, 0))).reshape(N_HEADS * HEAD_PAD, Q_LORA)
    return w


def _grad_sources_b(gr):
    gq = gr["q"].reshape(N_HEADS, HEAD_PAD, Q_LORA)[:, :QK_HEAD].reshape(N_DEV, Q_SH_ROWS, D_MODEL)
    small = jnp.concatenate([
        gr["kv"].reshape(N_DEV, KV_SH_ROWS, D_MODEL),
        jnp.pad(gr["in"][:IN_COLS].reshape(N_DEV, IN_SH, D_MODEL), ((0, 0), (0, 160 - IN_SH), (0, 0))),
        jnp.pad(gq, ((0, 0), (0, 48 - Q_SH_ROWS), (0, 0)))], axis=1)
    return [gr["g2"], gr["u2"], gr["d2"], gr["out"], small.reshape(N_DEV * SMALL_ROWS, D_MODEL)]


def _unpack_grad_shards(gs):
    nl = gs.shape[0]

    def tr(off, n):
        return gs[:, off:off + n].transpose(0, 2, 1)

    return {
        "ffn1_w_gate": tr(OFF_G1, FF_SH), "ffn1_w_up": tr(OFF_U1, FF_SH), "ffn1_w_down": gs[:, OFF_D1:OFF_D1 + FF_SH],
        "ffn2_w_gate": tr(OFF_G2, FF_SH), "ffn2_w_up": tr(OFF_U2, FF_SH), "ffn2_w_down": gs[:, OFF_D2:OFF_D2 + FF_SH],
        "w_out": gs[:, OFF_OUT:OFF_OUT + 128],
        "w_kv_b": gs[:, OFF_KV:OFF_KV + KV_SH_ROWS].reshape(nl, -1, KV_LORA).transpose(0, 2, 1),
        "w_in": tr(OFF_IN, IN_SH),
        "w_q_b": gs[:, OFF_Q:OFF_Q + Q_SH_ROWS].reshape(nl, -1, Q_LORA).transpose(0, 2, 1),
    }


def _small_layout(nl):
    names = [("dmod", nl * N_MOD), ("ffn1_norm", nl), ("mix_norm", nl), ("ffn2_norm", nl), ("q_a_norm", nl),
             ("kv_a_norm", nl), ("pool_scale", nl), ("final_norm", 1), ("loss", 1),
             ("pool_w", nl * 4 * POOL_GC * POOL_GC // D_MODEL)]
    off, table = 0, {}
    for name, n in names:
        table[name] = (off, n)
        off += -(-n // 8) * 8
    return table, off


def _to_rows(t, width=D_MODEL):
    n, w = t.shape
    return jnp.pad(t, ((0, -(-n // 8) * 8 - n), (0, width - w)))


def kernel(x, c, positions, ada_w, ada_b, ffn1_norm, ffn1_w_gate, ffn1_w_up, ffn1_w_down, mix_norm, w_in, pool_w, pool_scale, q_a_norm, w_q_b, kv_a_norm, w_kv_b, w_out, ffn2_norm, ffn2_w_gate, ffn2_w_up, ffn2_w_down, final_norm, loss_target, m_ada_w, m_ada_b, m_ffn1_norm, m_ffn1_w_gate, m_ffn1_w_up, m_ffn1_w_down, m_mix_norm, m_w_in, m_pool_w, m_pool_scale, m_q_a_norm, m_w_q_b, m_kv_a_norm, m_w_kv_b, m_w_out, m_ffn2_norm, m_ffn2_w_gate, m_ffn2_w_up, m_ffn2_w_down, m_final_norm, v_ada_w, v_ada_b, v_ffn1_norm, v_ffn1_w_gate, v_ffn1_w_up, v_ffn1_w_down, v_mix_norm, v_w_in, v_pool_w, v_pool_scale, v_q_a_norm, v_w_q_b, v_kv_a_norm, v_w_kv_b, v_w_out, v_ffn2_norm, v_ffn2_w_gate, v_ffn2_w_up, v_ffn2_w_down, v_final_norm):
    wts = dict(ada_w=ada_w, ada_b=ada_b, ffn1_norm=ffn1_norm, ffn1_w_gate=ffn1_w_gate, ffn1_w_up=ffn1_w_up,
               ffn1_w_down=ffn1_w_down, mix_norm=mix_norm, w_in=w_in, pool_w=pool_w, pool_scale=pool_scale,
               q_a_norm=q_a_norm, w_q_b=w_q_b, kv_a_norm=kv_a_norm, w_kv_b=w_kv_b, w_out=w_out,
               ffn2_norm=ffn2_norm, ffn2_w_gate=ffn2_w_gate, ffn2_w_up=ffn2_w_up, ffn2_w_down=ffn2_w_down,
               final_norm=final_norm)
    mom_m = dict(ada_w=m_ada_w, ada_b=m_ada_b, ffn1_norm=m_ffn1_norm, ffn1_w_gate=m_ffn1_w_gate,
                 ffn1_w_up=m_ffn1_w_up, ffn1_w_down=m_ffn1_w_down, mix_norm=m_mix_norm, w_in=m_w_in,
                 pool_w=m_pool_w, pool_scale=m_pool_scale, q_a_norm=m_q_a_norm, w_q_b=m_w_q_b,
                 kv_a_norm=m_kv_a_norm, w_kv_b=m_w_kv_b, w_out=m_w_out, ffn2_norm=m_ffn2_norm,
                 ffn2_w_gate=m_ffn2_w_gate, ffn2_w_up=m_ffn2_w_up, ffn2_w_down=m_ffn2_w_down,
                 final_norm=m_final_norm)
    mom_v = dict(ada_w=v_ada_w, ada_b=v_ada_b, ffn1_norm=v_ffn1_norm, ffn1_w_gate=v_ffn1_w_gate,
                 ffn1_w_up=v_ffn1_w_up, ffn1_w_down=v_ffn1_w_down, mix_norm=v_mix_norm, w_in=v_w_in,
                 pool_w=v_pool_w, pool_scale=v_pool_scale, q_a_norm=v_q_a_norm, w_q_b=v_w_q_b,
                 kv_a_norm=v_kv_a_norm, w_kv_b=v_w_kv_b, w_out=v_w_out, ffn2_norm=v_ffn2_norm,
                 ffn2_w_gate=v_ffn2_w_gate, ffn2_w_up=v_ffn2_w_up, ffn2_w_down=v_ffn2_w_down,
                 final_norm=v_final_norm)
    order = list(wts)
    nl = ada_w.shape[0]
    seq = x.shape[1]
    me = 4 * lax.axis_index("x") + 2 * lax.axis_index("y") + lax.axis_index("c")
    ada_cols = ada_w.shape[2]

    def after_token(t, token):
        return t + token[0:1, 0:1].astype(t.dtype)

    packs = [_pack_shard_layer(l, wts) for l in range(nl)]

    c_all = _all_gather(jnp.broadcast_to(c, (8, D_MODEL)), "gather_c")[::8]

    ada_b_mine = lax.dynamic_slice_in_dim(ada_b, me * ada_cols, ada_cols, axis=1).reshape(nl, 1, ada_cols)
    mod_part = _ada_mod(c_all, ada_w, ada_b_mine, "ada_mod")
    mod_all = _all_gather(mod_part.reshape(nl * N_DEV, ada_cols), "gather_mod")
    mod_all = mod_all.reshape(N_DEV, nl, N_DEV, ada_cols)
    mod = lax.dynamic_index_in_dim(mod_all, me, axis=2, keepdims=False)
    mod = mod.transpose(1, 0, 2).reshape(nl, N_MOD, 1, D_MODEL)

    flight_a = _gather_start(packs[0][:SPLIT_AB], ROWS_A, me, mod, "gather_start_0a")
    flight_b = _gather_start(packs[0][SPLIT_AB:], ROWS_B, me, flight_a[4], "gather_start_0b")
    last_start = flight_b[4]
    if nl > 1:
        in_flight = _gather_start(packs[1], ROWS_ALL, me, last_start, "gather_start_1")
        last_start = in_flight[4]

    cos, sin = _rope_tables(after_token(positions.reshape(seq, 1), last_start), "rope_tables")

    def vec(t):
        return t.reshape(1, -1)

    def landed(flight, rows_list, after, tag):
        send_sems, recv_sems, pk, lands, _ = flight
        pk, lands = _gather_wait(send_sems, recv_sems, pk, lands, after, f"gather_wait_{tag}")
        return _gather_finish(pk, rows_list, lands, "gather_finish")

    xs = x.reshape(seq, D_MODEL)
    saved = []
    for l in range(nl):
        norm1 = vec(ffn1_norm[l])
        if l == 0:
            lands = landed(flight_a, ROWS_A, cos, "0a")
        elif l + 1 < nl:
            in_flight = _gather_start(packs[l + 1], ROWS_ALL, me, lands[0], f"gather_start_{l + 1}")
            norm1 = after_token(norm1, in_flight[4])
        sv = {}

        def ffn_fwd(xin, norm, k0, wg, wu, wd, tag):
            h = _rm_fwd(xin, norm, mod[l, k0], mod[l, k0 + 1], "rm_fwd")
            a = _mm(h, wg, "nt", "ffn_gate")
            b = _mm(h, wu, "nt", "ffn_up")
            t = _swiglu_fwd(a, b, "swiglu_fwd")
            y, xout = _mm(t, wd, "nn", "ffn_down", res=xin, gate=mod[l, k0 + 2], gate_factor=0.5)
            sv[tag] = dict(x=xin, h=h, a=a, b=b, t=t, y=y)
            return xout

        xs = ffn_fwd(xs, norm1, 0, lands[0], lands[1], lands[2], "f1")
        if l == 0:
            lands = lands + landed(flight_b, ROWS_B, xs, "0b")
        w = _full_weights(lands)
        sv["w"] = w

        h2 = _rm_fwd(xs, vec(mix_norm[l]), mod[l, 3], mod[l, 4], "rm_fwd")
        z = _mm(h2, w["in"], "nt", "mix_in")
        y_pool, diff = _pool_fwd(z, pool_w[l], vec(pool_scale[l]), "pool_fwd")
        q, k, v, cqn, ckvn = _qkv_fwd(z, vec(q_a_norm[l]), vec(kv_a_norm[l]), w["q"], w["kv"], cos, sin, "qkv_fwd")
        o, lse = _attn_fwd(q, k, v, "attn_fwd")
        ycat = jnp.concatenate([y_pool, o.astype(BF16)], axis=1)
        y2, xmix = _mm(ycat, w["out"], "nn", "mix_out", res=xs, gate=mod[l, 5], gate_factor=1.0)
        sv["mix"] = dict(x=xs, h=h2, z=z, diff=diff, q=q, k=k, v=v, cqn=cqn, ckvn=ckvn, lse=lse, ycat=ycat, y=y2)
        xs = xmix

        xs = ffn_fwd(xs, vec(ffn2_norm[l]), 6, w["g2"], w["u2"], w["d2"], "f2")
        saved.append(sv)
        if l + 1 < nl:
            lands = landed(in_flight, ROWS_ALL, xs, l + 1)

    loss_part, dx, d_final = _loss_head(xs, vec(final_norm), loss_target.reshape(seq, D_MODEL), "loss_head")

    small = {name: [None] * nl for name in ("ffn1_norm", "mix_norm", "ffn2_norm", "q_a_norm", "kv_a_norm",
                                            "pool_scale", "pool_w", "dmod")}
    exchanges = [None] * nl
    last_token = None
    for l in reversed(range(nl)):
        sv = saved[l]
        w = sv["w"]
        dmod = [None] * N_MOD
        gr = {}
        gate3 = mod[l, 8] if last_token is None else after_token(mod[l, 8], last_token)

        def ffn_bwd(dxin, s_, norm, k0, wg, wu, wd, tag, gate):
            dy, dmod[k0 + 2] = _gate_bwd(dxin, s_["y"], gate, 0.5, "gate_bwd")
            dt = _mm(dy, wd, "nt", "ffn_down_dx")
            da, db = _swiglu_bwd(dt, s_["a"], s_["b"], "swiglu_bwd")
            gr["d" + tag] = _mm(s_["t"], dy, "tn", "ffn_down_dw", out_dtype=BF16, tm=256)
            gr["g" + tag] = _mm(da, s_["h"], "tn", "ffn_gate_dw", out_dtype=BF16, tm=256)
            gr["u" + tag] = _mm(db, s_["h"], "tn", "ffn_up_dw", out_dtype=BF16, tm=256)
            dh = _mm(da, wg, "nn", "ffn_gate_dx")
            dh = _mm(db, wu, "nn", "ffn_up_dx", res=dh)
            dxo, dmod[k0], dmod[k0 + 1], dnorm = _rm_bwd(dh, s_["x"], dxin, vec(norm), mod[l, k0 + 1], "rm_bwd")
            return dxo, dnorm

        dx, small["ffn2_norm"][l] = ffn_bwd(dx, sv["f2"], ffn2_norm[l], 6, w["g2"], w["u2"], w["d2"], "2", gate3)

        s_ = sv["mix"]
        dy, dmod[5] = _gate_bwd(dx, s_["y"], mod[l, 5], 1.0, "gate_bwd")
        gr["out"] = _mm(s_["ycat"], dy, "tn", "mix_out_dw", out_dtype=BF16, tm=256)
        dycat = _mm(dy, w["out"], "nt", "mix_out_dx")
        du, small["pool_w"][l], small["pool_scale"][l] = _pool_bwd(dycat, s_["diff"], pool_w[l], vec(pool_scale[l]), "pool_bwd")
        dq, dk, dv = _attn_bwd(s_["q"], s_["k"], s_["v"], s_["lse"], dycat, "attn_bwd")
        dz, dqb, dkvb, small["q_a_norm"][l], small["kv_a_norm"][l] = _qkv_bwd(
            dq, dk, dv, du, s_["z"], vec(q_a_norm[l]), vec(kv_a_norm[l]), w["q"], w["kv"], cos, sin, "qkv_bwd")
        gr["q"] = _mm(dqb, s_["cqn"], "tn", "q_b_dw", out_dtype=BF16, tm=256)
        gr["kv"] = _mm(dkvb, s_["ckvn"], "tn", "kv_b_dw", out_dtype=BF16, tm=256)
        gr["in"] = _mm(dz, s_["h"], "tn", "mix_in_dw", out_dtype=BF16, tm=256)
        dh2 = _mm(dz, w["in"], "nn", "mix_in_dx")
        dx, dmod[3], dmod[4], small["mix_norm"][l] = _rm_bwd(dh2, s_["x"], dx, vec(mix_norm[l]), mod[l, 4], "rm_bwd")

        gate1 = mod[l, 2]
        if l == 0:
            exchange_0b = _exchange_start(_grad_sources_b(gr), ROWS_B, me, dx, "exchange_start_0b")
            gate1 = after_token(gate1, exchange_0b[4])
        dx, small["ffn1_norm"][l] = ffn_bwd(dx, sv["f1"], ffn1_norm[l], 0, w["g1"], w["u1"], w["d1"], "1", gate1)

        small["dmod"][l] = jnp.concatenate(dmod, axis=0)
        if l > 0:
            srcs = [gr["g1"], gr["u1"], gr["d1"]] + _grad_sources_b(gr)
            exchanges[l] = _exchange_start(srcs, ROWS_ALL, me, dx, f"exchange_start_{l}")
            last_token = exchanges[l][4]

    grad_x = dx.reshape(x.shape)

    layout, small_rows = _small_layout(nl)
    pieces = {
        "dmod": jnp.concatenate(small["dmod"], axis=0),
        "ffn1_norm": jnp.concatenate(small["ffn1_norm"], axis=0),
        "mix_norm": jnp.concatenate(small["mix_norm"], axis=0),
        "ffn2_norm": jnp.concatenate(small["ffn2_norm"], axis=0),
        "q_a_norm": jnp.concatenate(small["q_a_norm"], axis=0),
        "kv_a_norm": jnp.concatenate(small["kv_a_norm"], axis=0),
        "pool_scale": jnp.concatenate(small["pool_scale"], axis=0),
        "final_norm": d_final,
        "loss": jnp.broadcast_to(loss_part[0:1, 0:1], (1, D_MODEL)),
        "pool_w": jnp.stack(small["pool_w"]).reshape(-1, D_MODEL),
    }
    small_buf = jnp.concatenate([_to_rows(pieces[name]) for name in layout], axis=0)
    if last_token is not None:
        small_buf = after_token(small_buf, last_token)
    small_all = _all_gather(small_buf, "gather_small").reshape(N_DEV, small_rows, D_MODEL)
    exchange_0a = _exchange_start([gr["g1"], gr["u1"], gr["d1"]], ROWS_A, me, small_all, "exchange_start_0a")
    small_sum = _sum_slots(small_all, "sum_small", after=exchange_0a[4])

    def take(name, width=D_MODEL):
        off, n = layout[name]
        return small_sum[off:off + n, :width]

    grads = {}
    grads["ada_b"] = take("dmod").reshape(nl, N_MOD * D_MODEL)
    grads["ffn1_norm"], grads["mix_norm"], grads["ffn2_norm"] = take("ffn1_norm"), take("mix_norm"), take("ffn2_norm")
    grads["q_a_norm"], grads["kv_a_norm"] = take("q_a_norm", Q_LORA), take("kv_a_norm", KV_LORA)
    grads["pool_scale"] = take("pool_scale", POOL_WIDTH)
    grads["final_norm"] = take("final_norm").reshape(D_MODEL)
    grads["pool_w"] = take("pool_w").reshape(pool_w.shape)
    loss = take("loss")[0, 0]

    off, n = layout["dmod"]
    dmod_all = small_all[:, off:off + n].reshape(N_DEV, nl, N_MOD * D_MODEL)
    dmod_mine = lax.dynamic_slice_in_dim(dmod_all, me * ada_cols, ada_cols, axis=2)
    dmod_pad = jnp.pad(dmod_mine.transpose(1, 0, 2), ((0, 0), (0, LANE - N_DEV), (0, 0)))
    grads["ada_w"] = _ada_grad(jnp.pad(c_all, ((0, LANE - N_DEV), (0, 0))), dmod_pad, "ada_grad")

    updates = {name: _adamw_nd(wts[name], grads[name], mom_m[name], mom_v[name], "adamw") for name in grads}

    def summed(exchange, after, tag):
        send_sems, recv_sems, srcs, recv, _ = exchange
        recv = _exchange_wait(send_sems, recv_sems, srcs, recv, after, f"exchange_wait_{tag}")
        return _sum_slots(recv, "sum_grads")

    gshard = [None] * nl
    for l in reversed(range(1, nl)):
        gshard[l] = summed(exchanges[l], exchange_0a[4], l)
    gshard[0] = jnp.concatenate([summed(exchange_0a, updates["ada_w"][0], "0a"),
                                 summed(exchange_0b, updates["ada_w"][0], "0b")], axis=0)
    grads.update(_unpack_grad_shards(jnp.stack(gshard)))
    for name in order:
        if name not in updates:
            updates[name] = _adamw_nd(wts[name], grads[name], mom_m[name], mom_v[name], "adamw")

    return (loss, grad_x, *[grads[n] for n in order], *[updates[n][0] for n in order],
            *[updates[n][1] for n in order], *[updates[n][2] for n in order])
```

```python
import math

import numpy as np
import jax
import jax.numpy as jnp
from jax import lax
from jax.experimental import pallas as pl
from jax.experimental.pallas import tpu as pltpu

F32 = jnp.float32
BF16 = jnp.bfloat16

N_DEV = 8
D_MODEL = 1024
D_FF = 2816
POOL_WIDTH = 512
POOL_WINDOWS = (2, 4, 8, 16)
POOL_GC = 128
N_HEADS = 4
QK_NOPE = 128
QK_ROPE = 64
V_HEAD = 128
QK_HEAD = QK_NOPE + QK_ROPE
HEAD_PAD = 256
Q_LORA = 384
KV_LORA = 256
IN_COLS = POOL_WIDTH + Q_LORA + KV_LORA + QK_ROPE
IN_PAD = 1280
ROPE_THETA = 10000.0
SOFTMAX_SCALE = 1.0 / math.sqrt(QK_HEAD)
EPS = 1e-6
N_MOD = 9

ADAM_LR = 0.001
ADAM_B1 = 0.9
ADAM_B2 = 0.999
ADAM_EPS = 1e-08
ADAM_WD = 0.01
ADAM_STEP = 10

LANE = 128
VMEM_LIMIT = 56 * 1024 * 1024

FF_SH = D_FF // N_DEV
OFF_G1, OFF_U1, OFF_D1 = 0, FF_SH, 2 * FF_SH
OFF_G2, OFF_U2, OFF_D2 = 3 * FF_SH, 4 * FF_SH, 5 * FF_SH
OFF_OUT = 6 * FF_SH
OFF_KV = OFF_OUT + 128
OFF_IN = OFF_KV + 32
OFF_Q = OFF_IN + 160
ROWS_L = OFF_Q + 48
IN_SH = IN_COLS // N_DEV
Q_SH_ROWS = (N_HEADS * QK_HEAD // N_DEV) * Q_LORA // D_MODEL
KV_SH_ROWS = (N_HEADS * (QK_NOPE + V_HEAD) // N_DEV) * KV_LORA // D_MODEL


def _tile(dim, target):
    if dim <= target:
        return dim
    best = None
    for t in range(LANE, target + 1, LANE):
        if dim % t == 0:
            best = t
    assert best is not None, (dim, target)
    return best


def _params(sem):
    return pltpu.CompilerParams(dimension_semantics=sem, vmem_limit_bytes=VMEM_LIMIT)


def _mesh_pos():
    return lax.axis_index("x"), lax.axis_index("y"), lax.axis_index("c")


def _all_gather(x, name):
    m, n = x.shape

    def body(x_ref, out_ref, send_sems, recv_sems, local_sem):
        px, py, pc = _mesh_pos()
        me, sibling = (px, py, pc), (px, py, 1 - pc)
        chips = [(1 - px, py), (px, 1 - py), (1 - px, 1 - py)]

        def rows(bx, by, bc):
            return out_ref.at[pl.ds((4 * bx + 2 * by + bc) * m, m), :]

        def copy(k, block, to, src=None):
            return pltpu.make_async_remote_copy(
                src_ref=rows(*block) if src is None else src, dst_ref=rows(*block),
                send_sem=send_sems.at[k], recv_sem=recv_sems.at[k],
                device_id=to, device_id_type=pl.DeviceIdType.MESH)

        mine = pltpu.make_async_copy(x_ref, rows(*me), local_sem)
        mine.start()
        first = [copy(0, me, sibling, src=x_ref)]
        first += [copy(1 + j, me, (*chip, pc), src=x_ref) for j, chip in enumerate(chips)]
        for cp in first:
            cp.start()
        passed = [copy(4 + j, (*chip, pc), sibling) for j, chip in enumerate(chips)]
        for j, chip in enumerate(chips):
            copy(1 + j, (*chip, pc), me).wait_recv()
            passed[j].start()
        copy(0, sibling, me).wait_recv()
        for j, chip in enumerate(chips):
            copy(4 + j, (*chip, 1 - pc), me).wait_recv()
        for cp in first + passed:
            cp.wait_send()
        mine.wait()

    return pl.pallas_call(
        body, name=name,
        out_shape=jax.ShapeDtypeStruct((N_DEV * m, n), x.dtype),
        in_specs=[pl.BlockSpec(memory_space=pltpu.HBM)],
        out_specs=pl.BlockSpec(memory_space=pltpu.HBM),
        scratch_shapes=[pltpu.SemaphoreType.DMA((7,)), pltpu.SemaphoreType.DMA((7,)),
                        pltpu.SemaphoreType.DMA],
    )(x)


SMALL_ROWS = ROWS_L - OFF_KV
ROWS_A = [FF_SH] * 3
ROWS_B = [FF_SH] * 3 + [128, SMALL_ROWS]
ROWS_ALL = ROWS_A + ROWS_B
SPLIT_AB = sum(ROWS_A)
HBM_SPEC = pl.BlockSpec(memory_space=pltpu.HBM)
SEM_SPEC = pl.BlockSpec(memory_space=pltpu.SEMAPHORE)
ANY_SPEC = pl.BlockSpec(memory_space=pl.ANY)
EFFECT = pltpu.SideEffectType.DATAFLOW_SIDE_EFFECTING


def _hbm(t):
    return pltpu.with_memory_space_constraint(t, pltpu.HBM)


def _whole_wait(ref, send_sem, recv_sem, peer):
    return pltpu.make_async_remote_copy(src_ref=ref, dst_ref=ref, send_sem=send_sem, recv_sem=recv_sem,
                                        device_id=peer, device_id_type=pl.DeviceIdType.MESH)


def _offsets(rows_list):
    return [sum(rows_list[:i]) for i in range(len(rows_list))]


def _gather_start(packed, rows_list, me_id, after, name):
    n = len(rows_list)
    offs = _offsets(rows_list)
    lands = [_hbm(lax.dynamic_update_slice_in_dim(lax.empty((N_DEV * rows, D_MODEL), BF16),
                                                  packed[off:off + rows], me_id * rows, axis=0))
             for off, rows in zip(offs, rows_list)]

    def body(packed_ref, *refs):
        land = refs[:n]
        send_sems, recv_sems = refs[n + 1], refs[n + 2]
        token = refs[-1]
        px, py, pc = _mesh_pos()
        me = 4 * px + 2 * py + pc
        peers = [(px, py, 1 - pc), (1 - px, py, pc), (px, 1 - py, pc), (1 - px, 1 - py, pc)]
        for k, peer in enumerate(peers):
            for off, rows, land_ref in zip(offs, rows_list, land):
                pltpu.make_async_remote_copy(
                    src_ref=packed_ref.at[pl.ds(off, rows), :], dst_ref=land_ref.at[pl.ds(me * rows, rows), :],
                    send_sem=send_sems.at[k], recv_sem=recv_sems.at[k],
                    device_id=peer, device_id_type=pl.DeviceIdType.MESH).start()
        token[...] = jnp.zeros_like(token)

    outs = pl.pallas_call(
        body, name=name,
        out_shape=(pltpu.SemaphoreType.DMA((4,)), pltpu.SemaphoreType.DMA((4,)), pltpu.HBM(packed.shape, BF16),
                   *[pltpu.HBM(t.shape, BF16) for t in lands], jax.ShapeDtypeStruct((8, LANE), F32)),
        in_specs=(HBM_SPEC,) * (1 + n) + (ANY_SPEC,),
        out_specs=(SEM_SPEC, SEM_SPEC) + (HBM_SPEC,) * (1 + n) + (pl.BlockSpec(memory_space=pltpu.VMEM),),
        input_output_aliases={i: 2 + i for i in range(1 + n)},
        compiler_params=pltpu.CompilerParams(has_side_effects=EFFECT),
    )(_hbm(packed), *lands, after)
    return outs[0], outs[1], outs[2], list(outs[3:3 + n]), outs[-1]


def _gather_wait(send_sems, recv_sems, packed, lands, after, name):
    n = len(lands)

    def body(packed_ref, *refs):
        s_sems, r_sems = refs[n], refs[n + 1]
        me = _mesh_pos()
        for k in range(4):
            cp = _whole_wait(packed_ref, s_sems.at[k], r_sems.at[k], me)
            cp.wait_send()
            cp.wait_recv()

    outs = pl.pallas_call(
        body, name=name,
        out_shape=(pltpu.HBM(packed.shape, BF16), *[pltpu.HBM(t.shape, BF16) for t in lands]),
        in_specs=(HBM_SPEC,) * (1 + n) + (SEM_SPEC, SEM_SPEC, ANY_SPEC),
        out_specs=(HBM_SPEC,) * (1 + n),
        input_output_aliases={i: i for i in range(1 + n)},
        compiler_params=pltpu.CompilerParams(has_side_effects=EFFECT),
    )(packed, *lands, send_sems, recv_sems, after)
    return outs[0], list(outs[1:])


def _gather_finish(packed, rows_list, lands, name):
    n = len(rows_list)

    def body(packed_ref, *refs):
        land = refs[n:2 * n]
        send_sems, recv_sems = refs[2 * n:]
        px, py, pc = _mesh_pos()
        sibling = (px, py, 1 - pc)
        for j, (cx, cy) in enumerate([(1 - px, py), (px, 1 - py), (1 - px, 1 - py)]):
            block = 4 * cx + 2 * cy + pc
            for rows, land_ref in zip(rows_list, land):
                blk = land_ref.at[pl.ds(block * rows, rows), :]
                pltpu.make_async_remote_copy(src_ref=blk, dst_ref=blk, send_sem=send_sems.at[j],
                                             recv_sem=recv_sems.at[j], device_id=sibling,
                                             device_id_type=pl.DeviceIdType.MESH).start()
        for j in range(3):
            cp = _whole_wait(packed_ref, send_sems.at[j], recv_sems.at[j], sibling)
            cp.wait_recv()
            cp.wait_send()

    outs = pl.pallas_call(
        body, name=name,
        out_shape=tuple(jax.ShapeDtypeStruct(t.shape, BF16) for t in lands),
        in_specs=(HBM_SPEC,) * (1 + n), out_specs=(HBM_SPEC,) * n,
        input_output_aliases={1 + i: i for i in range(n)},
        scratch_shapes=[pltpu.SemaphoreType.DMA((3,)), pltpu.SemaphoreType.DMA((3,))],
    )(packed, *lands)
    return list(outs)


def _exchange_start(srcs, rows_list, me_id, after, name):
    n = len(rows_list)
    offs = _offsets(rows_list)
    own = jnp.concatenate([lax.dynamic_slice_in_dim(t, me_id * rows, rows, axis=0)
                           for t, rows in zip(srcs, rows_list)], axis=0)
    recv = lax.dynamic_update_slice_in_dim(lax.empty((N_DEV, sum(rows_list), D_MODEL), BF16), own[None], me_id, axis=0)

    def body(*refs):
        src, recv_ref = refs[:n], refs[n]
        send_sems, recv_sems = refs[n + 2], refs[n + 3]
        token = refs[-1]
        px, py, pc = _mesh_pos()
        me = 4 * px + 2 * py + pc
        for k in range(1, N_DEV):
            qx = 1 - px if k & 4 else px
            qy = 1 - py if k & 2 else py
            qc = 1 - pc if k & 1 else pc
            peer_id = 4 * qx + 2 * qy + qc
            for off, rows, src_ref in zip(offs, rows_list, src):
                pltpu.make_async_remote_copy(
                    src_ref=src_ref.at[pl.ds(peer_id * rows, rows), :], dst_ref=recv_ref.at[me, pl.ds(off, rows), :],
                    send_sem=send_sems.at[k - 1], recv_sem=recv_sems.at[k - 1],
                    device_id=(qx, qy, qc), device_id_type=pl.DeviceIdType.MESH).start()
        token[...] = jnp.zeros_like(token)

    outs = pl.pallas_call(
        body, name=name,
        out_shape=(pltpu.SemaphoreType.DMA((N_DEV - 1,)), pltpu.SemaphoreType.DMA((N_DEV - 1,)),
                   *[pltpu.HBM(t.shape, BF16) for t in srcs], pltpu.HBM(recv.shape, BF16),
                   jax.ShapeDtypeStruct((8, LANE), F32)),
        in_specs=(HBM_SPEC,) * (n + 1) + (ANY_SPEC,),
        out_specs=(SEM_SPEC, SEM_SPEC) + (HBM_SPEC,) * (n + 1) + (pl.BlockSpec(memory_space=pltpu.VMEM),),
        input_output_aliases={i: 2 + i for i in range(n + 1)},
        compiler_params=pltpu.CompilerParams(has_side_effects=EFFECT),
    )(*[_hbm(t) for t in srcs], _hbm(recv), after)
    return outs[0], outs[1], list(outs[2:2 + n]), outs[2 + n], outs[-1]


def _exchange_wait(send_sems, recv_sems, srcs, recv, after, name):
    n = len(srcs)

    def body(*refs):
        recv_ref = refs[n]
        s_sems, r_sems = refs[n + 1], refs[n + 2]
        me = _mesh_pos()
        for k in range(N_DEV - 1):
            cp = _whole_wait(recv_ref.at[0], s_sems.at[k], r_sems.at[k], me)
            cp.wait_send()
            cp.wait_recv()

    outs = pl.pallas_call(
        body, name=name,
        out_shape=(*[pltpu.HBM(t.shape, BF16) for t in srcs], pltpu.HBM(recv.shape, BF16)),
        in_specs=(HBM_SPEC,) * (n + 1) + (SEM_SPEC, SEM_SPEC, ANY_SPEC),
        out_specs=(HBM_SPEC,) * (n + 1),
        input_output_aliases={i: i for i in range(n + 1)},
        compiler_params=pltpu.CompilerParams(has_side_effects=EFFECT),
    )(*srcs, recv, send_sems, recv_sems, after)
    return outs[n]


def _sum_slots(recv, name, after=None):
    _, r, n = recv.shape
    tr = _row_tile(r, 512)

    def body(in_ref, *refs):
        acc = in_ref[0].astype(F32)
        for j in range(1, N_DEV):
            acc = acc + in_ref[j].astype(F32)
        refs[-1][...] = acc

    if tr >= LANE or tr == r:
        grid = (r // tr,)
        in_specs, out_spec = [pl.BlockSpec((N_DEV, tr, n), lambda i: (0, i, 0))], pl.BlockSpec((tr, n), lambda i: (i, 0))
    else:
        tl = 2 * LANE
        grid = (n // tl,)
        in_specs, out_spec = [pl.BlockSpec((N_DEV, r, tl), lambda i: (0, 0, i))], pl.BlockSpec((r, tl), lambda i: (0, i))
    args = [recv]
    if after is not None:
        in_specs.append(ANY_SPEC)
        args.append(after)
    return pl.pallas_call(
        body, name=name, grid=grid,
        out_shape=jax.ShapeDtypeStruct((r, n), F32),
        in_specs=in_specs, out_specs=out_spec,
        compiler_params=_params(("parallel",)),
    )(*args)


def _row_tile(rows, target):
    if rows <= target:
        return rows
    best = None
    for t in range(16, target + 1, 16):
        if rows % t == 0:
            best = t
    assert best is not None, rows
    return best


_DIMS = {"nn": ((1,), (0,)), "nt": ((1,), (1,)), "tn": ((0,), (0,))}


def _mm(a, b, mode, name, out_dtype=F32, res=None, gate=None, gate_factor=1.0, tm=512, tn=1408):
    if mode == "tn":
        kdim, m = a.shape
    else:
        m, kdim = a.shape
    n = b.shape[0] if mode == "nt" else b.shape[1]
    tm, tn = _tile(m, tm), _tile(n, tn)
    a_spec = (pl.BlockSpec((kdim, tm), lambda i, j: (0, i)) if mode == "tn"
              else pl.BlockSpec((tm, kdim), lambda i, j: (i, 0)))
    b_spec = (pl.BlockSpec((tn, kdim), lambda i, j: (j, 0)) if mode == "nt"
              else pl.BlockSpec((kdim, tn), lambda i, j: (0, j)))
    o_spec = pl.BlockSpec((tm, tn), lambda i, j: (i, j))
    dims = (_DIMS[mode], ((), ()))
    has_res, has_gate = res is not None, gate is not None

    def body(*refs):
        a_ref, b_ref = refs[0], refs[1]
        y = lax.dot_general(a_ref[...].astype(BF16), b_ref[...].astype(BF16), dims,
                            preferred_element_type=F32)
        if not has_res:
            refs[2][...] = y.astype(out_dtype)
            return
        res_ref = refs[2]
        if has_gate:
            y_ref, o_ref = refs[4], refs[5]
            y_ref[...] = y
            o_ref[...] = res_ref[...] + (gate_factor * refs[3][...]) * y
        else:
            refs[3][...] = res_ref[...] + y

    in_specs, args = [a_spec, b_spec], [a, b]
    if has_res:
        in_specs.append(o_spec)
        args.append(res)
        if has_gate:
            in_specs.append(pl.BlockSpec((1, tn), lambda i, j: (0, j)))
            args.append(gate)
            out_shape = (jax.ShapeDtypeStruct((m, n), F32), jax.ShapeDtypeStruct((m, n), F32))
            out_specs = (o_spec, o_spec)
        else:
            out_shape, out_specs = jax.ShapeDtypeStruct((m, n), F32), o_spec
    else:
        out_shape, out_specs = jax.ShapeDtypeStruct((m, n), out_dtype), o_spec
    return pl.pallas_call(
        body, name=name, grid=(m // tm, n // tn), out_shape=out_shape,
        in_specs=in_specs, out_specs=out_specs,
        compiler_params=_params(("parallel", "parallel")),
    )(*args)


def _vec_spec(width):
    return pl.BlockSpec((1, width), lambda i: (0, 0))


def _rm_fwd(x, gw, shift, scale, name):
    s, d = x.shape
    ts = _tile(s, 256)

    def body(x_ref, gw_ref, sh_ref, sc_ref, h_ref):
        xv = x_ref[...]
        r = lax.rsqrt(jnp.mean(xv * xv, axis=-1, keepdims=True) + EPS)
        y = (xv * r) * gw_ref[...]
        h_ref[...] = (y * (1 + sc_ref[...]) + sh_ref[...]).astype(BF16)

    row = pl.BlockSpec((ts, d), lambda i: (i, 0))
    return pl.pallas_call(
        body, name=name, grid=(s // ts,), out_shape=jax.ShapeDtypeStruct((s, d), BF16),
        in_specs=[row, _vec_spec(d), _vec_spec(d), _vec_spec(d)], out_specs=row,
        compiler_params=_params(("parallel",)),
    )(x, gw, shift, scale)


def _rm_bwd(dh, x, dres, gw, scale, name):
    s, d = x.shape
    ts = _tile(s, 256)

    def body(dh_ref, x_ref, dres_ref, gw_ref, sc_ref, dx_ref, dsh_ref, dsc_ref, dgw_ref):
        @pl.when(pl.program_id(0) == 0)
        def _():
            dsh_ref[...] = jnp.zeros_like(dsh_ref)
            dsc_ref[...] = jnp.zeros_like(dsc_ref)
            dgw_ref[...] = jnp.zeros_like(dgw_ref)

        xv, dhv, gwv = x_ref[...], dh_ref[...], gw_ref[...]
        r = lax.rsqrt(jnp.mean(xv * xv, axis=-1, keepdims=True) + EPS)
        xn = xv * r
        y = xn * gwv
        dsh_ref[...] += jnp.sum(dhv, axis=0, keepdims=True)
        dsc_ref[...] += jnp.sum(dhv * y, axis=0, keepdims=True)
        dy = dhv * (1 + sc_ref[...])
        dgw_ref[...] += jnp.sum(dy * xn, axis=0, keepdims=True)
        dxn = dy * gwv
        dx = r * (dxn - xn * jnp.mean(dxn * xn, axis=-1, keepdims=True))
        dx_ref[...] = dres_ref[...] + dx

    row = pl.BlockSpec((ts, d), lambda i: (i, 0))
    vec = jax.ShapeDtypeStruct((1, d), F32)
    return pl.pallas_call(
        body, name=name, grid=(s // ts,),
        out_shape=(jax.ShapeDtypeStruct((s, d), F32), vec, vec, vec),
        in_specs=[row, row, row, _vec_spec(d), _vec_spec(d)],
        out_specs=(row, _vec_spec(d), _vec_spec(d), _vec_spec(d)),
        compiler_params=_params(("arbitrary",)),
    )(dh, x, dres, gw, scale)


def _gate_bwd(dx, y, gate, factor, name):
    s, d = dx.shape
    ts = _tile(s, 256)

    def body(dx_ref, y_ref, g_ref, dy_ref, dg_ref):
        @pl.when(pl.program_id(0) == 0)
        def _():
            dg_ref[...] = jnp.zeros_like(dg_ref)

        dxv = dx_ref[...]
        dy_ref[...] = ((factor * g_ref[...]) * dxv).astype(BF16)
        dg_ref[...] += jnp.sum((factor * dxv) * y_ref[...], axis=0, keepdims=True)

    row = pl.BlockSpec((ts, d), lambda i: (i, 0))
    return pl.pallas_call(
        body, name=name, grid=(s // ts,),
        out_shape=(jax.ShapeDtypeStruct((s, d), BF16), jax.ShapeDtypeStruct((1, d), F32)),
        in_specs=[row, row, _vec_spec(d)], out_specs=(row, _vec_spec(d)),
        compiler_params=_params(("arbitrary",)),
    )(dx, y, gate)


def _swiglu_fwd(a, b, name):
    s, f = a.shape
    ts, tf = _tile(s, 256), _tile(f, 1408)

    def body(a_ref, b_ref, t_ref):
        av = a_ref[...]
        t_ref[...] = ((av * jax.nn.sigmoid(av)) * b_ref[...]).astype(BF16)

    blk = pl.BlockSpec((ts, tf), lambda i, j: (i, j))
    return pl.pallas_call(
        body, name=name, grid=(s // ts, f // tf), out_shape=jax.ShapeDtypeStruct((s, f), BF16),
        in_specs=[blk, blk], out_specs=blk, compiler_params=_params(("parallel", "parallel")),
    )(a, b)


def _swiglu_bwd(dt, a, b, name):
    s, f = a.shape
    ts, tf = _tile(s, 256), _tile(f, 1408)

    def body(dt_ref, a_ref, b_ref, da_ref, db_ref):
        av, dtv = a_ref[...], dt_ref[...]
        sg = jax.nn.sigmoid(av)
        silu = av * sg
        db_ref[...] = (dtv * silu).astype(BF16)
        da_ref[...] = ((dtv * b_ref[...]) * (sg * (1 + av * (1 - sg)))).astype(BF16)

    blk = pl.BlockSpec((ts, tf), lambda i, j: (i, j))
    out = jax.ShapeDtypeStruct((s, f), BF16)
    return pl.pallas_call(
        body, name=name, grid=(s // ts, f // tf), out_shape=(out, out),
        in_specs=[blk, blk, blk], out_specs=(blk, blk),
        compiler_params=_params(("parallel", "parallel")),
    )(dt, a, b)


FFN_TM, FFN_TF = 1024, 256


def _ffn_fwd(x, gw, shift, scale, gate, wg, wu, wd, name):
    s, d = x.shape
    f = wg.shape[0]
    tm, tf = _tile(s, FFN_TM), _tile(f, FFN_TF)
    nf = f // tf
    nt = (((1,), (1,)), ((), ()))

    def body(x_ref, gw_ref, sh_ref, sc_ref, g_ref, wg_ref, wu_ref, wd_ref,
             xo_ref, h_ref, a_ref, b_ref, t_ref, y_ref):
        j = pl.program_id(1)

        @pl.when(j == 0)
        def _():
            xv = x_ref[...]
            r = lax.rsqrt(jnp.mean(xv * xv, axis=-1, keepdims=True) + EPS)
            h_ref[...] = (((xv * r) * gw_ref[...]) * (1 + sc_ref[...]) + sh_ref[...]).astype(BF16)

        hb = h_ref[...]
        av = lax.dot_general(hb, wg_ref[...], nt, preferred_element_type=F32)
        bv = lax.dot_general(hb, wu_ref[...], nt, preferred_element_type=F32)
        tv = ((av * jax.nn.sigmoid(av)) * bv).astype(BF16)
        a_ref[...] = av
        b_ref[...] = bv
        t_ref[...] = tv
        part = jnp.dot(tv, wd_ref[...], preferred_element_type=F32)

        @pl.when(j == 0)
        def _():
            y_ref[...] = part

        @pl.when(j > 0)
        def _():
            y_ref[...] += part

        @pl.when(j == nf - 1)
        def _():
            xo_ref[...] = x_ref[...] + (0.5 * g_ref[...]) * y_ref[...]

    row = pl.BlockSpec((tm, d), lambda i, j: (i, 0))
    vec = pl.BlockSpec((1, d), lambda i, j: (0, 0))
    wblk = pl.BlockSpec((tf, d), lambda i, j: (j, 0))
    blk = pl.BlockSpec((tm, tf), lambda i, j: (i, j))
    return pl.pallas_call(
        body, name=name, grid=(s // tm, nf),
        out_shape=(jax.ShapeDtypeStruct((s, d), F32), jax.ShapeDtypeStruct((s, d), BF16),
                   jax.ShapeDtypeStruct((s, f), F32), jax.ShapeDtypeStruct((s, f), F32),
                   jax.ShapeDtypeStruct((s, f), BF16), jax.ShapeDtypeStruct((s, d), F32)),
        in_specs=[row, vec, vec, vec, vec, wblk, wblk, wblk],
        out_specs=(row, row, blk, blk, blk, row),
        compiler_params=_params(("parallel", "arbitrary")),
    )(x, gw, shift, scale, gate, wg, wu, wd)


def _ffn_bwd_mid(dy, a, b, wg, wu, wd, name):
    s, d = dy.shape
    f = wg.shape[0]
    tm, tf = _tile(s, FFN_TM), _tile(f, FFN_TF)
    nt = (((1,), (1,)), ((), ()))

    def body(dy_ref, a_ref, b_ref, wg_ref, wu_ref, wd_ref, da_ref, db_ref, dh_ref):
        j = pl.program_id(1)
        dtv = lax.dot_general(dy_ref[...], wd_ref[...], nt, preferred_element_type=F32)
        av = a_ref[...]
        sg = jax.nn.sigmoid(av)
        dbv = (dtv * (av * sg)).astype(BF16)
        dav = ((dtv * b_ref[...]) * (sg * (1 + av * (1 - sg)))).astype(BF16)
        da_ref[...] = dav
        db_ref[...] = dbv
        part = (jnp.dot(dav, wg_ref[...], preferred_element_type=F32)
                + jnp.dot(dbv, wu_ref[...], preferred_element_type=F32))

        @pl.when(j == 0)
        def _():
            dh_ref[...] = part

        @pl.when(j > 0)
        def _():
            dh_ref[...] += part

    row = pl.BlockSpec((tm, d), lambda i, j: (i, 0))
    wblk = pl.BlockSpec((tf, d), lambda i, j: (j, 0))
    blk = pl.BlockSpec((tm, tf), lambda i, j: (i, j))
    return pl.pallas_call(
        body, name=name, grid=(s // tm, f // tf),
        out_shape=(jax.ShapeDtypeStruct((s, f), BF16), jax.ShapeDtypeStruct((s, f), BF16),
                   jax.ShapeDtypeStruct((s, d), F32)),
        in_specs=[row, blk, blk, wblk, wblk, wblk], out_specs=(blk, blk, row),
        compiler_params=_params(("parallel", "arbitrary")),
    )(dy, a, b, wg, wu, wd)


def _pool_counts(s):
    return (lax.broadcasted_iota(jnp.int32, (s, POOL_GC), 0))


def _pool_fwd(z, pool_w, pool_scale, name):
    s = z.shape[0]

    def body(u_ref, w_ref, sc_ref, y_ref, diff_ref):
        t = lax.broadcasted_iota(jnp.int32, (s, POOL_GC), 0)
        for g, win in enumerate(POOL_WINDOWS):
            cols = slice(g * POOL_GC, (g + 1) * POOL_GC)
            u = u_ref[:, cols]
            acc, step = u, 1
            while step < win:
                acc = acc + jnp.where(t >= step, pltpu.roll(acc, step, 0), 0.0)
                step *= 2
            cnt = jnp.minimum(t + 1, win).astype(F32)
            diff = acc / cnt - u
            diff_ref[:, cols] = diff
            ypre = jnp.dot(diff.astype(BF16), w_ref[g].astype(BF16), preferred_element_type=F32)
            y_ref[:, cols] = (ypre * sc_ref[:, cols]).astype(BF16)

    return pl.pallas_call(
        body, name=name, grid=(1,),
        out_shape=(jax.ShapeDtypeStruct((s, POOL_WIDTH), BF16), jax.ShapeDtypeStruct((s, POOL_WIDTH), F32)),
        in_specs=[pl.BlockSpec((s, POOL_WIDTH), lambda i: (0, 0)),
                  pl.BlockSpec(pool_w.shape, lambda i: (0, 0, 0)),
                  pl.BlockSpec((1, POOL_WIDTH), lambda i: (0, 0))],
        out_specs=(pl.BlockSpec((s, POOL_WIDTH), lambda i: (0, 0)),
                   pl.BlockSpec((s, POOL_WIDTH), lambda i: (0, 0))),
        compiler_params=_params(("arbitrary",)),
    )(z, pool_w, pool_scale)


def _pool_bwd(dycat, diff, pool_w, pool_scale, name):
    s = diff.shape[0]

    def body(dy_ref, diff_ref, w_ref, sc_ref, du_ref, dw_ref, dsc_ref):
        t = lax.broadcasted_iota(jnp.int32, (s, POOL_GC), 0)
        for g, win in enumerate(POOL_WINDOWS):
            cols = slice(g * POOL_GC, (g + 1) * POOL_GC)
            dy, dfb, wb = dy_ref[:, cols], diff_ref[:, cols].astype(BF16), w_ref[g].astype(BF16)
            ypre = jnp.dot(dfb, wb, preferred_element_type=F32)
            dsc_ref[:, cols] = jnp.sum(dy * ypre, axis=0, keepdims=True)
            dypre = (dy * sc_ref[:, cols]).astype(BF16)
            ddiff = lax.dot_general(dypre, wb, (((1,), (1,)), ((), ())), preferred_element_type=F32)
            dw_ref[g] = lax.dot_general(dfb, dypre, (((0,), (0,)), ((), ())), preferred_element_type=F32)
            cnt = jnp.minimum(t + 1, win).astype(F32)
            acc, step = ddiff / cnt, 1
            while step < win:
                acc = acc + jnp.where(t < s - step, pltpu.roll(acc, s - step, 0), 0.0)
                step *= 2
            du_ref[:, cols] = acc - ddiff

    full = pl.BlockSpec((s, POOL_WIDTH), lambda i: (0, 0))
    return pl.pallas_call(
        body, name=name, grid=(1,),
        out_shape=(jax.ShapeDtypeStruct((s, POOL_WIDTH), F32),
                   jax.ShapeDtypeStruct(pool_w.shape, F32),
                   jax.ShapeDtypeStruct((1, POOL_WIDTH), F32)),
        in_specs=[full, full, pl.BlockSpec(pool_w.shape, lambda i: (0, 0, 0)),
                  pl.BlockSpec((1, POOL_WIDTH), lambda i: (0, 0))],
        out_specs=(full, pl.BlockSpec(pool_w.shape, lambda i: (0, 0, 0)),
                   pl.BlockSpec((1, POOL_WIDTH), lambda i: (0, 0))),
        compiler_params=_params(("arbitrary",)),
    )(dycat, diff, pool_w, pool_scale)


def _rope_tables(positions, name):
    s = positions.shape[0]
    ts = _tile(s, 512)
    freq = 1.0 / (ROPE_THETA ** (np.arange(0, QK_ROPE, 2, dtype=np.float32) / QK_ROPE))
    table = np.zeros((1, LANE), np.float32)
    table[0, :QK_ROPE // 2] = freq
    table[0, QK_ROPE // 2:QK_ROPE] = freq

    def body(pos_ref, f_ref, cos_ref, sin_ref):
        ang = pos_ref[...].astype(F32) * f_ref[...]
        cos_ref[...] = jnp.cos(ang)
        sin_ref[...] = jnp.sin(ang)

    out = jax.ShapeDtypeStruct((s, LANE), F32)
    blk = pl.BlockSpec((ts, LANE), lambda i: (i, 0))
    return pl.pallas_call(
        body, name=name, grid=(s // ts,), out_shape=(out, out),
        in_specs=[pl.BlockSpec((ts, 1), lambda i: (i, 0)), _vec_spec(LANE)], out_specs=(blk, blk),
        compiler_params=_params(("parallel",)),
    )(positions, jnp.asarray(table))


def _lane_mod64_low(shape):
    return (lax.broadcasted_iota(jnp.int32, shape, 1) % QK_ROPE) < (QK_ROPE // 2)


def _rope(x, cos, sin):
    rot = jnp.where(_lane_mod64_low(x.shape), -pltpu.roll(x, LANE - 32, 1), pltpu.roll(x, 32, 1))
    return x * cos + rot * sin


def _rope_t(dy, cos, sin):
    w = dy * sin
    rot_t = jnp.where(_lane_mod64_low(dy.shape), pltpu.roll(w, LANE - 32, 1), -pltpu.roll(w, 32, 1))
    return dy * cos + rot_t


def _plain_rms(x, g):
    r = lax.rsqrt(jnp.mean(x * x, axis=-1, keepdims=True) + EPS)
    return (x * r) * g, x * r, r


O_Q, O_KV, O_KR = POOL_WIDTH, POOL_WIDTH + Q_LORA, POOL_WIDTH + Q_LORA + KV_LORA


def _qkv_fwd(z, qn, kvn, wq, wkv, cos, sin, name):
    s = z.shape[0]
    ts = _tile(s, 256)

    def body(z_ref, qn_ref, kvn_ref, wq_ref, wkv_ref, cos_ref, sin_ref, q_ref, k_ref, v_ref, cqn_ref, ckvn_ref):
        cosv, sinv = cos_ref[...], sin_ref[...]
        cqn = _plain_rms(z_ref[:, O_Q:O_KV], qn_ref[...])[0].astype(BF16)
        ckvn = _plain_rms(z_ref[:, O_KV:O_KR], kvn_ref[...])[0].astype(BF16)
        cqn_ref[...] = cqn
        ckvn_ref[...] = ckvn
        nt = (((1,), (1,)), ((), ()))
        q = lax.dot_general(cqn, wq_ref[...], nt, preferred_element_type=F32)
        kv = lax.dot_general(ckvn, wkv_ref[...], nt, preferred_element_type=F32)
        kr = _rope(z_ref[:, O_KR:IN_PAD], cosv, sinv).astype(BF16)
        for h in range(N_HEADS):
            o = h * HEAD_PAD
            q_ref[:, o:o + QK_NOPE] = q[:, o:o + QK_NOPE].astype(BF16)
            q_ref[:, o + QK_NOPE:o + HEAD_PAD] = _rope(q[:, o + QK_NOPE:o + HEAD_PAD], cosv, sinv).astype(BF16)
            k_ref[:, o:o + QK_NOPE] = kv[:, o:o + QK_NOPE].astype(BF16)
            k_ref[:, o + QK_NOPE:o + HEAD_PAD] = kr
            v_ref[:, h * V_HEAD:(h + 1) * V_HEAD] = kv[:, o + QK_NOPE:o + HEAD_PAD].astype(BF16)

    def row(w):
        return pl.BlockSpec((ts, w), lambda i: (i, 0))

    def whole(arr):
        return pl.BlockSpec(arr.shape, lambda i: (0, 0))

    hp = N_HEADS * HEAD_PAD
    return pl.pallas_call(
        body, name=name, grid=(s // ts,),
        out_shape=(jax.ShapeDtypeStruct((s, hp), BF16), jax.ShapeDtypeStruct((s, hp), BF16),
                   jax.ShapeDtypeStruct((s, N_HEADS * V_HEAD), BF16),
                   jax.ShapeDtypeStruct((s, Q_LORA), BF16), jax.ShapeDtypeStruct((s, KV_LORA), BF16)),
        in_specs=[row(IN_PAD), whole(qn), whole(kvn), whole(wq), whole(wkv), row(LANE), row(LANE)],
        out_specs=(row(hp), row(hp), row(N_HEADS * V_HEAD), row(Q_LORA), row(KV_LORA)),
        compiler_params=_params(("parallel",)),
    )(z, qn, kvn, wq, wkv, cos, sin)


def _qkv_bwd(dq, dk, dv, du, z, qn, kvn, wq, wkv, cos, sin, name):
    s = z.shape[0]
    ts = _tile(s, 256)

    def norm_bwd(x, g, dy):
        _, xn, r = _plain_rms(x, g)
        dxn = dy * g
        return r * (dxn - xn * jnp.mean(dxn * xn, axis=-1, keepdims=True)), jnp.sum(dy * xn, axis=0, keepdims=True)

    def body(dq_ref, dk_ref, dv_ref, du_ref, z_ref, qn_ref, kvn_ref, wq_ref, wkv_ref, cos_ref, sin_ref,
             dz_ref, dqb_ref, dkvb_ref, dqn_ref, dkvn_ref):
        @pl.when(pl.program_id(0) == 0)
        def _():
            dqn_ref[...] = jnp.zeros_like(dqn_ref)
            dkvn_ref[...] = jnp.zeros_like(dkvn_ref)

        cosv, sinv = cos_ref[...], sin_ref[...]
        dkr = jnp.zeros((ts, LANE), F32)
        for h in range(N_HEADS):
            o = h * HEAD_PAD
            dqb_ref[:, o:o + QK_NOPE] = dq_ref[:, o:o + QK_NOPE].astype(BF16)
            dqb_ref[:, o + QK_NOPE:o + HEAD_PAD] = _rope_t(dq_ref[:, o + QK_NOPE:o + HEAD_PAD], cosv, sinv).astype(BF16)
            dkvb_ref[:, o:o + QK_NOPE] = dk_ref[:, o:o + QK_NOPE].astype(BF16)
            dkvb_ref[:, o + QK_NOPE:o + HEAD_PAD] = dv_ref[:, h * V_HEAD:(h + 1) * V_HEAD].astype(BF16)
            dkr = dkr + dk_ref[:, o + QK_NOPE:o + HEAD_PAD]
        dcqn = jnp.dot(dqb_ref[...], wq_ref[...], preferred_element_type=F32)
        dckvn = jnp.dot(dkvb_ref[...], wkv_ref[...], preferred_element_type=F32)
        dcq, dqn = norm_bwd(z_ref[:, O_Q:O_KV], qn_ref[...], dcqn)
        dckv, dkvn = norm_bwd(z_ref[:, O_KV:O_KR], kvn_ref[...], dckvn)
        dqn_ref[...] += dqn
        dkvn_ref[...] += dkvn
        dz_ref[:, 0:O_Q] = du_ref[...].astype(BF16)
        dz_ref[:, O_Q:O_KV] = dcq.astype(BF16)
        dz_ref[:, O_KV:O_KR] = dckv.astype(BF16)
        dz_ref[:, O_KR:IN_PAD] = _rope_t(dkr, cosv, sinv).astype(BF16)

    def row(w):
        return pl.BlockSpec((ts, w), lambda i: (i, 0))

    def whole(arr):
        return pl.BlockSpec(arr.shape, lambda i: (0, 0))

    hp = N_HEADS * HEAD_PAD
    return pl.pallas_call(
        body, name=name, grid=(s // ts,),
        out_shape=(jax.ShapeDtypeStruct((s, IN_PAD), BF16), jax.ShapeDtypeStruct((s, hp), BF16),
                   jax.ShapeDtypeStruct((s, hp), BF16),
                   jax.ShapeDtypeStruct((1, Q_LORA), F32), jax.ShapeDtypeStruct((1, KV_LORA), F32)),
        in_specs=[row(hp), row(hp), row(N_HEADS * V_HEAD), row(POOL_WIDTH), row(IN_PAD),
                  whole(qn), whole(kvn), whole(wq), whole(wkv), row(LANE), row(LANE)],
        out_specs=(row(IN_PAD), row(hp), row(hp), whole(qn), whole(kvn)),
        compiler_params=_params(("arbitrary",)),
    )(dq, dk, dv, du, z, qn, kvn, wq, wkv, cos, sin)


def _causal_scores(q, k, i, tq, s):
    sc = lax.dot_general(q, k, (((1,), (1,)), ((), ())), preferred_element_type=F32) * SOFTMAX_SCALE
    qpos = i * tq + lax.broadcasted_iota(jnp.int32, (tq, s), 0)
    kpos = lax.broadcasted_iota(jnp.int32, (tq, s), 1)
    return jnp.where(qpos >= kpos, sc, -jnp.inf)


def _attn_fwd(q, k, v, name):
    s = q.shape[0]
    tq = _tile(s, 256)

    def body(q_ref, k_ref, v_ref, o_ref, lse_ref):
        sc = _causal_scores(q_ref[...], k_ref[...], pl.program_id(1), tq, s)
        mx = jnp.max(sc, axis=-1, keepdims=True)
        p = jnp.exp(sc - mx)
        den = jnp.sum(p, axis=-1, keepdims=True)
        o_ref[...] = jnp.dot((p / den).astype(BF16), v_ref[...], preferred_element_type=F32)
        lse_ref[...] = mx + jnp.log(den)

    return pl.pallas_call(
        body, name=name, grid=(N_HEADS, s // tq),
        out_shape=(jax.ShapeDtypeStruct((s, N_HEADS * V_HEAD), F32), jax.ShapeDtypeStruct((N_HEADS, s, 1), F32)),
        in_specs=[pl.BlockSpec((tq, HEAD_PAD), lambda h, i: (i, h)),
                  pl.BlockSpec((s, HEAD_PAD), lambda h, i: (0, h)),
                  pl.BlockSpec((s, V_HEAD), lambda h, i: (0, h))],
        out_specs=(pl.BlockSpec((tq, V_HEAD), lambda h, i: (i, h)),
                   pl.BlockSpec((None, tq, 1), lambda h, i: (h, i, 0))),
        compiler_params=_params(("parallel", "parallel")),
    )(q, k, v)


def _attn_bwd(q, k, v, lse, dycat, name):
    s = q.shape[0]
    tq = _tile(s, 256)
    tn_dims = (((0,), (0,)), ((), ()))

    def body(q_ref, k_ref, v_ref, lse_ref, do_ref, dq_ref, dk_ref, dv_ref):
        @pl.when(pl.program_id(1) == 0)
        def _():
            dk_ref[...] = jnp.zeros_like(dk_ref)
            dv_ref[...] = jnp.zeros_like(dv_ref)

        qv, kv_, dob = q_ref[...], k_ref[...], do_ref[...].astype(BF16)
        sc = _causal_scores(qv, kv_, pl.program_id(1), tq, s)
        p = jnp.exp(sc - lse_ref[...])
        dp = lax.dot_general(dob, v_ref[...], (((1,), (1,)), ((), ())), preferred_element_type=F32)
        ds = (p * (dp - jnp.sum(dp * p, axis=-1, keepdims=True)) * SOFTMAX_SCALE).astype(BF16)
        dq_ref[...] = jnp.dot(ds, kv_, preferred_element_type=F32)
        dk_ref[...] += lax.dot_general(ds, qv, tn_dims, preferred_element_type=F32)
        dv_ref[...] += lax.dot_general(p.astype(BF16), dob, tn_dims, preferred_element_type=F32)

    n_pool_blocks = POOL_WIDTH // V_HEAD
    return pl.pallas_call(
        body, name=name, grid=(N_HEADS, s // tq),
        out_shape=(jax.ShapeDtypeStruct((s, N_HEADS * HEAD_PAD), F32),
                   jax.ShapeDtypeStruct((s, N_HEADS * HEAD_PAD), F32),
                   jax.ShapeDtypeStruct((s, N_HEADS * V_HEAD), F32)),
        in_specs=[pl.BlockSpec((tq, HEAD_PAD), lambda h, i: (i, h)),
                  pl.BlockSpec((s, HEAD_PAD), lambda h, i: (0, h)),
                  pl.BlockSpec((s, V_HEAD), lambda h, i: (0, h)),
                  pl.BlockSpec((None, tq, 1), lambda h, i: (h, i, 0)),
                  pl.BlockSpec((tq, V_HEAD), lambda h, i: (i, n_pool_blocks + h))],
        out_specs=(pl.BlockSpec((tq, HEAD_PAD), lambda h, i: (i, h)),
                   pl.BlockSpec((s, HEAD_PAD), lambda h, i: (0, h)),
                   pl.BlockSpec((s, V_HEAD), lambda h, i: (0, h))),
        compiler_params=_params(("parallel", "arbitrary")),
    )(q, k, v, lse, dycat)


def _loss_head(x, gw, target, name):
    s, d = x.shape
    ts = _tile(s, 256)

    def body(x_ref, gw_ref, tgt_ref, loss_ref, dx_ref, dgw_ref):
        @pl.when(pl.program_id(0) == 0)
        def _():
            loss_ref[...] = jnp.zeros_like(loss_ref)
            dgw_ref[...] = jnp.zeros_like(dgw_ref)

        xv, gwv = x_ref[...], gw_ref[...]
        r = lax.rsqrt(jnp.mean(xv * xv, axis=-1, keepdims=True) + EPS)
        xn = xv * r
        err = xn * gwv - tgt_ref[...]
        loss_ref[...] += 0.5 * jnp.sum(jnp.mean(err * err, axis=-1, keepdims=True))
        dy = err / d
        dgw_ref[...] += jnp.sum(dy * xn, axis=0, keepdims=True)
        dxn = dy * gwv
        dx_ref[...] = r * (dxn - xn * jnp.mean(dxn * xn, axis=-1, keepdims=True))

    row = pl.BlockSpec((ts, d), lambda i: (i, 0))
    return pl.pallas_call(
        body, name=name, grid=(s // ts,),
        out_shape=(jax.ShapeDtypeStruct((8, LANE), F32), jax.ShapeDtypeStruct((s, d), F32),
                   jax.ShapeDtypeStruct((1, d), F32)),
        in_specs=[row, _vec_spec(d), row],
        out_specs=(pl.BlockSpec((8, LANE), lambda i: (0, 0)), row, _vec_spec(d)),
        compiler_params=_params(("arbitrary",)),
    )(x, gw, target)


def _ada_mod(c_all, ada_w, ada_b, name):
    nl, d, cols = ada_w.shape

    def body(c_ref, w_ref, b_ref, o_ref):
        cv = c_ref[...]
        act = (cv * jax.nn.sigmoid(cv)).astype(BF16)
        o_ref[...] = jnp.dot(act, w_ref[...].astype(BF16), preferred_element_type=F32) + b_ref[...]

    return pl.pallas_call(
        body, name=name, grid=(nl,), out_shape=jax.ShapeDtypeStruct((nl, N_DEV, cols), F32),
        in_specs=[pl.BlockSpec((N_DEV, d), lambda l: (0, 0)),
                  pl.BlockSpec((None, d, cols), lambda l: (l, 0, 0)),
                  pl.BlockSpec((None, 1, cols), lambda l: (l, 0, 0))],
        out_specs=pl.BlockSpec((None, N_DEV, cols), lambda l: (l, 0, 0)),
        compiler_params=_params(("parallel",)),
    )(c_all, ada_w, ada_b)


def _ada_grad(c_pad, dmod_pad, name):
    nl, kpad, cols = dmod_pad.shape
    d = c_pad.shape[1]

    def body(c_ref, dm_ref, o_ref):
        cv = c_ref[...]
        act = (cv * jax.nn.sigmoid(cv)).astype(BF16)
        o_ref[...] = lax.dot_general(act, dm_ref[...].astype(BF16), (((0,), (0,)), ((), ())),
                                     preferred_element_type=F32)

    return pl.pallas_call(
        body, name=name, grid=(nl,), out_shape=jax.ShapeDtypeStruct((nl, d, cols), F32),
        in_specs=[pl.BlockSpec((kpad, d), lambda l: (0, 0)),
                  pl.BlockSpec((None, kpad, cols), lambda l: (l, 0, 0))],
        out_specs=pl.BlockSpec((None, d, cols), lambda l: (l, 0, 0)),
        compiler_params=_params(("parallel",)),
    )(c_pad, dmod_pad)


def _adamw(w, g, m, v, name):
    rows, cols = w.shape
    tr = _row_tile(rows, 256)

    def body(w_ref, g_ref, m_ref, v_ref, d_ref, nm_ref, nv_ref):
        gv = g_ref[...]
        nm = ADAM_B1 * m_ref[...] + (1.0 - ADAM_B1) * gv
        nv = ADAM_B2 * v_ref[...] + (1.0 - ADAM_B2) * (gv * gv)
        m_hat = nm / (1.0 - ADAM_B1 ** ADAM_STEP)
        v_hat = nv / (1.0 - ADAM_B2 ** ADAM_STEP)
        d_ref[...] = -ADAM_LR * (m_hat / (jnp.sqrt(v_hat) + ADAM_EPS) + ADAM_WD * w_ref[...])
        nm_ref[...] = nm
        nv_ref[...] = nv

    blk = pl.BlockSpec((tr, cols), lambda i: (i, 0))
    out = jax.ShapeDtypeStruct((rows, cols), F32)
    return pl.pallas_call(
        body, name=name, grid=(rows // tr,), out_shape=(out, out, out),
        in_specs=[blk, blk, blk, blk], out_specs=(blk, blk, blk),
        compiler_params=_params(("parallel",)),
    )(w, g, m, v)


def _adamw_nd(w, g, m, v, name):
    shape = w.shape
    flat = (lambda t: t.reshape(1, -1)) if w.ndim == 1 else (lambda t: t.reshape(-1, shape[-1]))
    return tuple(t.reshape(shape) for t in _adamw(flat(w), flat(g), flat(m), flat(v), name))


def _pad_rows(t, rows):
    return jnp.pad(t, ((0, rows - t.shape[0]), (0, 0)))


def _pack_shard_layer(l, wts):
    def tr(name):
        return wts[name][l].astype(BF16).T

    parts = [tr("ffn1_w_gate"), tr("ffn1_w_up"), wts["ffn1_w_down"][l].astype(BF16),
             tr("ffn2_w_gate"), tr("ffn2_w_up"), wts["ffn2_w_down"][l].astype(BF16),
             wts["w_out"][l].astype(BF16),
             tr("w_kv_b").reshape(KV_SH_ROWS, D_MODEL),
             _pad_rows(tr("w_in"), 160),
             _pad_rows(tr("w_q_b").reshape(Q_SH_ROWS, D_MODEL), 48)]
    return jnp.concatenate(parts, axis=0)


def _full_weights(lands):
    w = dict(zip(("g1", "u1", "d1", "g2", "u2", "d2", "out"), lands))
    small = lands[-1].reshape(N_DEV, SMALL_ROWS, D_MODEL)
    o_in, o_q = OFF_IN - OFF_KV, OFF_Q - OFF_KV
    w["kv"] = small[:, :KV_SH_ROWS].reshape(N_HEADS * HEAD_PAD, KV_LORA)
    w["in"] = _pad_rows(small[:, o_in:o_in + IN_SH].reshape(IN_COLS, D_MODEL), IN_PAD)
    wq = small[:, o_q:o_q + Q_SH_ROWS].reshape(N_HEADS, QK_HEAD, Q_LORA)
    w["q"] = jnp.pad(wq, ((0, 0), (0, HEAD_PAD - QK_HEAD), (0, 0))).reshape(N_HEADS * HEAD_PAD, Q_LORA)
    return w


def _grad_sources_b(gr):
    gq = gr["q"].reshape(N_HEADS, HEAD_PAD, Q_LORA)[:, :QK_HEAD].reshape(N_DEV, Q_SH_ROWS, D_MODEL)
    small = jnp.concatenate([
        gr["kv"].reshape(N_DEV, KV_SH_ROWS, D_MODEL),
        jnp.pad(gr["in"][:IN_COLS].reshape(N_DEV, IN_SH, D_MODEL), ((0, 0), (0, 160 - IN_SH), (0, 0))),
        jnp.pad(gq, ((0, 0), (0, 48 - Q_SH_ROWS), (0, 0)))], axis=1)
    return [gr["g2"], gr["u2"], gr["d2"], gr["out"], small.reshape(N_DEV * SMALL_ROWS, D_MODEL)]


def _unpack_grad_shards(gs):
    nl = gs.shape[0]

    def tr(off, n):
        return gs[:, off:off + n].transpose(0, 2, 1)

    return {
        "ffn1_w_gate": tr(OFF_G1, FF_SH), "ffn1_w_up": tr(OFF_U1, FF_SH), "ffn1_w_down": gs[:, OFF_D1:OFF_D1 + FF_SH],
        "ffn2_w_gate": tr(OFF_G2, FF_SH), "ffn2_w_up": tr(OFF_U2, FF_SH), "ffn2_w_down": gs[:, OFF_D2:OFF_D2 + FF_SH],
        "w_out": gs[:, OFF_OUT:OFF_OUT + 128],
        "w_kv_b": gs[:, OFF_KV:OFF_KV + KV_SH_ROWS].reshape(nl, -1, KV_LORA).transpose(0, 2, 1),
        "w_in": tr(OFF_IN, IN_SH),
        "w_q_b": gs[:, OFF_Q:OFF_Q + Q_SH_ROWS].reshape(nl, -1, Q_LORA).transpose(0, 2, 1),
    }


def _small_layout(nl):
    names = [("dmod", nl * N_MOD), ("ffn1_norm", nl), ("mix_norm", nl), ("ffn2_norm", nl), ("q_a_norm", nl),
             ("kv_a_norm", nl), ("pool_scale", nl), ("final_norm", 1), ("loss", 1),
             ("pool_w", nl * 4 * POOL_GC * POOL_GC // D_MODEL)]
    off, table = 0, {}
    for name, n in names:
        table[name] = (off, n)
        off += -(-n // 8) * 8
    return table, off


def _to_rows(t, width=D_MODEL):
    n, w = t.shape
    return jnp.pad(t, ((0, -(-n // 8) * 8 - n), (0, width - w)))


def kernel(x, c, positions, ada_w, ada_b, ffn1_norm, ffn1_w_gate, ffn1_w_up, ffn1_w_down, mix_norm, w_in, pool_w, pool_scale, q_a_norm, w_q_b, kv_a_norm, w_kv_b, w_out, ffn2_norm, ffn2_w_gate, ffn2_w_up, ffn2_w_down, final_norm, loss_target, m_ada_w, m_ada_b, m_ffn1_norm, m_ffn1_w_gate, m_ffn1_w_up, m_ffn1_w_down, m_mix_norm, m_w_in, m_pool_w, m_pool_scale, m_q_a_norm, m_w_q_b, m_kv_a_norm, m_w_kv_b, m_w_out, m_ffn2_norm, m_ffn2_w_gate, m_ffn2_w_up, m_ffn2_w_down, m_final_norm, v_ada_w, v_ada_b, v_ffn1_norm, v_ffn1_w_gate, v_ffn1_w_up, v_ffn1_w_down, v_mix_norm, v_w_in, v_pool_w, v_pool_scale, v_q_a_norm, v_w_q_b, v_kv_a_norm, v_w_kv_b, v_w_out, v_ffn2_norm, v_ffn2_w_gate, v_ffn2_w_up, v_ffn2_w_down, v_final_norm):
    wts = dict(ada_w=ada_w, ada_b=ada_b, ffn1_norm=ffn1_norm, ffn1_w_gate=ffn1_w_gate, ffn1_w_up=ffn1_w_up,
               ffn1_w_down=ffn1_w_down, mix_norm=mix_norm, w_in=w_in, pool_w=pool_w, pool_scale=pool_scale,
               q_a_norm=q_a_norm, w_q_b=w_q_b, kv_a_norm=kv_a_norm, w_kv_b=w_kv_b, w_out=w_out,
               ffn2_norm=ffn2_norm, ffn2_w_gate=ffn2_w_gate, ffn2_w_up=ffn2_w_up, ffn2_w_down=ffn2_w_down,
               final_norm=final_norm)
    mom_m = dict(ada_w=m_ada_w, ada_b=m_ada_b, ffn1_norm=m_ffn1_norm, ffn1_w_gate=m_ffn1_w_gate,
                 ffn1_w_up=m_ffn1_w_up, ffn1_w_down=m_ffn1_w_down, mix_norm=m_mix_norm, w_in=m_w_in,
                 pool_w=m_pool_w, pool_scale=m_pool_scale, q_a_norm=m_q_a_norm, w_q_b=m_w_q_b,
                 kv_a_norm=m_kv_a_norm, w_kv_b=m_w_kv_b, w_out=m_w_out, ffn2_norm=m_ffn2_norm,
                 ffn2_w_gate=m_ffn2_w_gate, ffn2_w_up=m_ffn2_w_up, ffn2_w_down=m_ffn2_w_down,
                 final_norm=m_final_norm)
    mom_v = dict(ada_w=v_ada_w, ada_b=v_ada_b, ffn1_norm=v_ffn1_norm, ffn1_w_gate=v_ffn1_w_gate,
                 ffn1_w_up=v_ffn1_w_up, ffn1_w_down=v_ffn1_w_down, mix_norm=v_mix_norm, w_in=v_w_in,
                 pool_w=v_pool_w, pool_scale=v_pool_scale, q_a_norm=v_q_a_norm, w_q_b=v_w_q_b,
                 kv_a_norm=v_kv_a_norm, w_kv_b=v_w_kv_b, w_out=v_w_out, ffn2_norm=v_ffn2_norm,
                 ffn2_w_gate=v_ffn2_w_gate, ffn2_w_up=v_ffn2_w_up, ffn2_w_down=v_ffn2_w_down,
                 final_norm=v_final_norm)
    order = list(wts)
    nl = ada_w.shape[0]
    seq = x.shape[1]
    me = 4 * lax.axis_index("x") + 2 * lax.axis_index("y") + lax.axis_index("c")
    ada_cols = ada_w.shape[2]

    def after_token(t, token):
        return t + token[0:1, 0:1].astype(t.dtype)

    packs = [_pack_shard_layer(l, wts) for l in range(nl)]

    c_all = _all_gather(jnp.broadcast_to(c, (8, D_MODEL)), "gather_c")[::8]

    ada_b_mine = lax.dynamic_slice_in_dim(ada_b, me * ada_cols, ada_cols, axis=1).reshape(nl, 1, ada_cols)
    mod_part = _ada_mod(c_all, ada_w, ada_b_mine, "ada_mod")
    mod_all = _all_gather(mod_part.reshape(nl * N_DEV, ada_cols), "gather_mod")
    mod_all = mod_all.reshape(N_DEV, nl, N_DEV, ada_cols)
    mod = lax.dynamic_index_in_dim(mod_all, me, axis=2, keepdims=False)
    mod = mod.transpose(1, 0, 2).reshape(nl, N_MOD, 1, D_MODEL)

    flight_a = _gather_start(packs[0][:SPLIT_AB], ROWS_A, me, mod, "gather_start_0a")
    flight_b = _gather_start(packs[0][SPLIT_AB:], ROWS_B, me, flight_a[4], "gather_start_0b")
    last_start = flight_b[4]
    if nl > 1:
        in_flight = _gather_start(packs[1], ROWS_ALL, me, last_start, "gather_start_1")
        last_start = in_flight[4]

    cos, sin = _rope_tables(after_token(positions.reshape(seq, 1), last_start), "rope_tables")

    def vec(t):
        return t.reshape(1, -1)

    def landed(flight, rows_list, after, tag):
        send_sems, recv_sems, pk, lands, _ = flight
        pk, lands = _gather_wait(send_sems, recv_sems, pk, lands, after, f"gather_wait_{tag}")
        return _gather_finish(pk, rows_list, lands, "gather_finish")

    xs = x.reshape(seq, D_MODEL)
    saved = []
    for l in range(nl):
        norm1 = vec(ffn1_norm[l])
        if l == 0:
            lands = landed(flight_a, ROWS_A, cos, "0a")
        elif l + 1 < nl:
            in_flight = _gather_start(packs[l + 1], ROWS_ALL, me, lands[0], f"gather_start_{l + 1}")
            norm1 = after_token(norm1, in_flight[4])
        sv = {}

        def ffn_fwd(xin, norm, k0, wg, wu, wd, tag):
            xout, h, a, b, t, y = _ffn_fwd(xin, norm, mod[l, k0], mod[l, k0 + 1], mod[l, k0 + 2], wg, wu, wd, "ffn_fwd")
            sv[tag] = dict(x=xin, h=h, a=a, b=b, t=t, y=y)
            return xout

        xs = ffn_fwd(xs, norm1, 0, lands[0], lands[1], lands[2], "f1")
        if l == 0:
            lands = lands + landed(flight_b, ROWS_B, xs, "0b")
        w = _full_weights(lands)
        sv["w"] = w

        h2 = _rm_fwd(xs, vec(mix_norm[l]), mod[l, 3], mod[l, 4], "rm_fwd")
        z = _mm(h2, w["in"], "nt", "mix_in")
        y_pool, diff = _pool_fwd(z, pool_w[l], vec(pool_scale[l]), "pool_fwd")
        q, k, v, cqn, ckvn = _qkv_fwd(z, vec(q_a_norm[l]), vec(kv_a_norm[l]), w["q"], w["kv"], cos, sin, "qkv_fwd")
        o, lse = _attn_fwd(q, k, v, "attn_fwd")
        ycat = jnp.concatenate([y_pool, o.astype(BF16)], axis=1)
        y2, xmix = _mm(ycat, w["out"], "nn", "mix_out", res=xs, gate=mod[l, 5], gate_factor=1.0)
        sv["mix"] = dict(x=xs, h=h2, z=z, diff=diff, q=q, k=k, v=v, cqn=cqn, ckvn=ckvn, lse=lse, ycat=ycat, y=y2)
        xs = xmix

        xs = ffn_fwd(xs, vec(ffn2_norm[l]), 6, w["g2"], w["u2"], w["d2"], "f2")
        saved.append(sv)
        if l + 1 < nl:
            lands = landed(in_flight, ROWS_ALL, xs, l + 1)

    loss_part, dx, d_final = _loss_head(xs, vec(final_norm), loss_target.reshape(seq, D_MODEL), "loss_head")

    small = {name: [None] * nl for name in ("ffn1_norm", "mix_norm", "ffn2_norm", "q_a_norm", "kv_a_norm",
                                            "pool_scale", "pool_w", "dmod")}
    exchanges = [None] * nl
    last_token = None
    for l in reversed(range(nl)):
        sv = saved[l]
        w = sv["w"]
        dmod = [None] * N_MOD
        gr = {}
        gate3 = mod[l, 8] if last_token is None else after_token(mod[l, 8], last_token)

        def ffn_bwd(dxin, s_, norm, k0, wg, wu, wd, tag, gate):
            dy, dmod[k0 + 2] = _gate_bwd(dxin, s_["y"], gate, 0.5, "gate_bwd")
            da, db, dh = _ffn_bwd_mid(dy, s_["a"], s_["b"], wg, wu, wd, "ffn_bwd_mid")
            gr["d" + tag] = _mm(s_["t"], dy, "tn", "ffn_down_dw", out_dtype=BF16, tm=256)
            gr["g" + tag] = _mm(da, s_["h"], "tn", "ffn_gate_dw", out_dtype=BF16, tm=256)
            gr["u" + tag] = _mm(db, s_["h"], "tn", "ffn_up_dw", out_dtype=BF16, tm=256)
            dxo, dmod[k0], dmod[k0 + 1], dnorm = _rm_bwd(dh, s_["x"], dxin, vec(norm), mod[l, k0 + 1], "rm_bwd")
            return dxo, dnorm

        dx, small["ffn2_norm"][l] = ffn_bwd(dx, sv["f2"], ffn2_norm[l], 6, w["g2"], w["u2"], w["d2"], "2", gate3)

        s_ = sv["mix"]
        dy, dmod[5] = _gate_bwd(dx, s_["y"], mod[l, 5], 1.0, "gate_bwd")
        gr["out"] = _mm(s_["ycat"], dy, "tn", "mix_out_dw", out_dtype=BF16, tm=256)
        dycat = _mm(dy, w["out"], "nt", "mix_out_dx")
        du, small["pool_w"][l], small["pool_scale"][l] = _pool_bwd(dycat, s_["diff"], pool_w[l], vec(pool_scale[l]), "pool_bwd")
        dq, dk, dv = _attn_bwd(s_["q"], s_["k"], s_["v"], s_["lse"], dycat, "attn_bwd")
        dz, dqb, dkvb, small["q_a_norm"][l], small["kv_a_norm"][l] = _qkv_bwd(
            dq, dk, dv, du, s_["z"], vec(q_a_norm[l]), vec(kv_a_norm[l]), w["q"], w["kv"], cos, sin, "qkv_bwd")
        gr["q"] = _mm(dqb, s_["cqn"], "tn", "q_b_dw", out_dtype=BF16, tm=256)
        gr["kv"] = _mm(dkvb, s_["ckvn"], "tn", "kv_b_dw", out_dtype=BF16, tm=256)
        gr["in"] = _mm(dz, s_["h"], "tn", "mix_in_dw", out_dtype=BF16, tm=256)
        dh2 = _mm(dz, w["in"], "nn", "mix_in_dx")
        dx, dmod[3], dmod[4], small["mix_norm"][l] = _rm_bwd(dh2, s_["x"], dx, vec(mix_norm[l]), mod[l, 4], "rm_bwd")

        gate1 = mod[l, 2]
        if l == 0:
            exchange_0b = _exchange_start(_grad_sources_b(gr), ROWS_B, me, dx, "exchange_start_0b")
            gate1 = after_token(gate1, exchange_0b[4])
        dx, small["ffn1_norm"][l] = ffn_bwd(dx, sv["f1"], ffn1_norm[l], 0, w["g1"], w["u1"], w["d1"], "1", gate1)

        small["dmod"][l] = jnp.concatenate(dmod, axis=0)
        if l > 0:
            srcs = [gr["g1"], gr["u1"], gr["d1"]] + _grad_sources_b(gr)
            exchanges[l] = _exchange_start(srcs, ROWS_ALL, me, dx, f"exchange_start_{l}")
            last_token = exchanges[l][4]

    grad_x = dx.reshape(x.shape)

    layout, small_rows = _small_layout(nl)
    pieces = {
        "dmod": jnp.concatenate(small["dmod"], axis=0),
        "ffn1_norm": jnp.concatenate(small["ffn1_norm"], axis=0),
        "mix_norm": jnp.concatenate(small["mix_norm"], axis=0),
        "ffn2_norm": jnp.concatenate(small["ffn2_norm"], axis=0),
        "q_a_norm": jnp.concatenate(small["q_a_norm"], axis=0),
        "kv_a_norm": jnp.concatenate(small["kv_a_norm"], axis=0),
        "pool_scale": jnp.concatenate(small["pool_scale"], axis=0),
        "final_norm": d_final,
        "loss": jnp.broadcast_to(loss_part[0:1, 0:1], (1, D_MODEL)),
        "pool_w": jnp.stack(small["pool_w"]).reshape(-1, D_MODEL),
    }
    small_buf = jnp.concatenate([_to_rows(pieces[name]) for name in layout], axis=0)
    if last_token is not None:
        small_buf = after_token(small_buf, last_token)
    small_all = _all_gather(small_buf, "gather_small").reshape(N_DEV, small_rows, D_MODEL)
    exchange_0a = _exchange_start([gr["g1"], gr["u1"], gr["d1"]], ROWS_A, me, small_all, "exchange_start_0a")
    small_sum = _sum_slots(small_all, "sum_small", after=exchange_0a[4])

    def take(name, width=D_MODEL):
        off, n = layout[name]
        return small_sum[off:off + n, :width]

    grads = {}
    grads["ada_b"] = take("dmod").reshape(nl, N_MOD * D_MODEL)
    grads["ffn1_norm"], grads["mix_norm"], grads["ffn2_norm"] = take("ffn1_norm"), take("mix_norm"), take("ffn2_norm")
    grads["q_a_norm"], grads["kv_a_norm"] = take("q_a_norm", Q_LORA), take("kv_a_norm", KV_LORA)
    grads["pool_scale"] = take("pool_scale", POOL_WIDTH)
    grads["final_norm"] = take("final_norm").reshape(D_MODEL)
    grads["pool_w"] = take("pool_w").reshape(pool_w.shape)
    loss = take("loss")[0, 0]

    off, n = layout["dmod"]
    dmod_all = small_all[:, off:off + n].reshape(N_DEV, nl, N_MOD * D_MODEL)
    dmod_mine = lax.dynamic_slice_in_dim(dmod_all, me * ada_cols, ada_cols, axis=2)
    dmod_pad = jnp.pad(dmod_mine.transpose(1, 0, 2), ((0, 0), (0, LANE - N_DEV), (0, 0)))
    grads["ada_w"] = _ada_grad(jnp.pad(c_all, ((0, LANE - N_DEV), (0, 0))), dmod_pad, "ada_grad")

    updates = {name: _adamw_nd(wts[name], grads[name], mom_m[name], mom_v[name], "adamw") for name in grads}

    def summed(exchange, after, tag):
        send_sems, recv_sems, srcs, recv, _ = exchange
        recv = _exchange_wait(send_sems, recv_sems, srcs, recv, after, f"exchange_wait_{tag}")
        return _sum_slots(recv, "sum_grads")

    gshard = [None] * nl
    for l in reversed(range(1, nl)):
        gshard[l] = summed(exchanges[l], exchange_0a[4], l)
    gshard[0] = jnp.concatenate([summed(exchange_0a, updates["ada_w"][0], "0a"),
                                 summed(exchange_0b, updates["ada_w"][0], "0b")], axis=0)
    grads.update(_unpack_grad_shards(jnp.stack(gshard)))
    for name in order:
        if name not in updates:
            updates[name] = _adamw_nd(wts[name], grads[name], mom_m[name], mom_v[name], "adamw")

    return (loss, grad_x, *[grads[n] for n in order], *[updates[n][0] for n in order],
            *[updates[n][1] for n in order], *[updates[n][2] for n in order])
```

```python
import math

import numpy as np
import jax
import jax.numpy as jnp
from jax import lax
from jax.experimental import pallas as pl
from jax.experimental.pallas import tpu as pltpu

F32 = jnp.float32
BF16 = jnp.bfloat16

N_DEV = 8
D_MODEL = 1024
D_FF = 2816
POOL_WIDTH = 512
POOL_WINDOWS = (2, 4, 8, 16)
POOL_GC = 128
N_HEADS = 4
QK_NOPE = 128
QK_ROPE = 64
V_HEAD = 128
QK_HEAD = QK_NOPE + QK_ROPE
HEAD_PAD = 256
Q_LORA = 384
KV_LORA = 256
IN_COLS = POOL_WIDTH + Q_LORA + KV_LORA + QK_ROPE
IN_PAD = 1280
ROPE_THETA = 10000.0
SOFTMAX_SCALE = 1.0 / math.sqrt(QK_HEAD)
EPS = 1e-6
N_MOD = 9

ADAM_LR = 0.001
ADAM_B1 = 0.9
ADAM_B2 = 0.999
ADAM_EPS = 1e-08
ADAM_WD = 0.01
ADAM_STEP = 10

LANE = 128
VMEM_LIMIT = 56 * 1024 * 1024

FF_SH = D_FF // N_DEV
OFF_G1, OFF_U1, OFF_D1 = 0, FF_SH, 2 * FF_SH
OFF_G2, OFF_U2, OFF_D2 = 3 * FF_SH, 4 * FF_SH, 5 * FF_SH
OFF_OUT = 6 * FF_SH
OFF_KV = OFF_OUT + 128
OFF_IN = OFF_KV + 32
OFF_Q = OFF_IN + 160
ROWS_L = OFF_Q + 48
IN_SH = IN_COLS // N_DEV
Q_SH_ROWS = (N_HEADS * QK_HEAD // N_DEV) * Q_LORA // D_MODEL
KV_SH_ROWS = (N_HEADS * (QK_NOPE + V_HEAD) // N_DEV) * KV_LORA // D_MODEL


def _tile(dim, target):
    if dim <= target:
        return dim
    best = None
    for t in range(LANE, target + 1, LANE):
        if dim % t == 0:
            best = t
    assert best is not None, (dim, target)
    return best


def _params(sem):
    return pltpu.CompilerParams(dimension_semantics=sem, vmem_limit_bytes=VMEM_LIMIT)


def _mesh_pos():
    return lax.axis_index("x"), lax.axis_index("y"), lax.axis_index("c")


def _all_gather(x, name):
    m, n = x.shape

    def body(x_ref, out_ref, send_sems, recv_sems, local_sem):
        px, py, pc = _mesh_pos()
        me, sibling = (px, py, pc), (px, py, 1 - pc)
        chips = [(1 - px, py), (px, 1 - py), (1 - px, 1 - py)]

        def rows(bx, by, bc):
            return out_ref.at[pl.ds((4 * bx + 2 * by + bc) * m, m), :]

        def copy(k, block, to, src=None):
            return pltpu.make_async_remote_copy(
                src_ref=rows(*block) if src is None else src, dst_ref=rows(*block),
                send_sem=send_sems.at[k], recv_sem=recv_sems.at[k],
                device_id=to, device_id_type=pl.DeviceIdType.MESH)

        mine = pltpu.make_async_copy(x_ref, rows(*me), local_sem)
        mine.start()
        first = [copy(0, me, sibling, src=x_ref)]
        first += [copy(1 + j, me, (*chip, pc), src=x_ref) for j, chip in enumerate(chips)]
        for cp in first:
            cp.start()
        passed = [copy(4 + j, (*chip, pc), sibling) for j, chip in enumerate(chips)]
        for j, chip in enumerate(chips):
            copy(1 + j, (*chip, pc), me).wait_recv()
            passed[j].start()
        copy(0, sibling, me).wait_recv()
        for j, chip in enumerate(chips):
            copy(4 + j, (*chip, 1 - pc), me).wait_recv()
        for cp in first + passed:
            cp.wait_send()
        mine.wait()

    return pl.pallas_call(
        body, name=name,
        out_shape=jax.ShapeDtypeStruct((N_DEV * m, n), x.dtype),
        in_specs=[pl.BlockSpec(memory_space=pltpu.HBM)],
        out_specs=pl.BlockSpec(memory_space=pltpu.HBM),
        scratch_shapes=[pltpu.SemaphoreType.DMA((7,)), pltpu.SemaphoreType.DMA((7,)),
                        pltpu.SemaphoreType.DMA],
    )(x)


SMALL_ROWS = ROWS_L - OFF_KV
ROWS_A = [FF_SH] * 3
ROWS_B = [FF_SH] * 3 + [128, SMALL_ROWS]
ROWS_ALL = ROWS_A + ROWS_B
SPLIT_AB = sum(ROWS_A)
HBM_SPEC = pl.BlockSpec(memory_space=pltpu.HBM)
SEM_SPEC = pl.BlockSpec(memory_space=pltpu.SEMAPHORE)
ANY_SPEC = pl.BlockSpec(memory_space=pl.ANY)
EFFECT = pltpu.SideEffectType.DATAFLOW_SIDE_EFFECTING


def _hbm(t):
    return pltpu.with_memory_space_constraint(t, pltpu.HBM)


def _whole_wait(ref, send_sem, recv_sem, peer):
    return pltpu.make_async_remote_copy(src_ref=ref, dst_ref=ref, send_sem=send_sem, recv_sem=recv_sem,
                                        device_id=peer, device_id_type=pl.DeviceIdType.MESH)


def _offsets(rows_list):
    return [sum(rows_list[:i]) for i in range(len(rows_list))]


def _gather_start(packed, rows_list, me_id, after, name):
    n = len(rows_list)
    offs = _offsets(rows_list)
    lands = [_hbm(lax.dynamic_update_slice_in_dim(lax.empty((N_DEV * rows, D_MODEL), BF16),
                                                  packed[off:off + rows], me_id * rows, axis=0))
             for off, rows in zip(offs, rows_list)]

    def body(packed_ref, *refs):
        land = refs[:n]
        send_sems, recv_sems = refs[n + 1], refs[n + 2]
        token = refs[-1]
        px, py, pc = _mesh_pos()
        me = 4 * px + 2 * py + pc
        peers = [(px, py, 1 - pc), (1 - px, py, pc), (px, 1 - py, pc), (1 - px, 1 - py, pc)]
        for k, peer in enumerate(peers):
            for off, rows, land_ref in zip(offs, rows_list, land):
                pltpu.make_async_remote_copy(
                    src_ref=packed_ref.at[pl.ds(off, rows), :], dst_ref=land_ref.at[pl.ds(me * rows, rows), :],
                    send_sem=send_sems.at[k], recv_sem=recv_sems.at[k],
                    device_id=peer, device_id_type=pl.DeviceIdType.MESH).start()
        token[...] = jnp.zeros_like(token)

    outs = pl.pallas_call(
        body, name=name,
        out_shape=(pltpu.SemaphoreType.DMA((4,)), pltpu.SemaphoreType.DMA((4,)), pltpu.HBM(packed.shape, BF16),
                   *[pltpu.HBM(t.shape, BF16) for t in lands], jax.ShapeDtypeStruct((8, LANE), F32)),
        in_specs=(HBM_SPEC,) * (1 + n) + (ANY_SPEC,),
        out_specs=(SEM_SPEC, SEM_SPEC) + (HBM_SPEC,) * (1 + n) + (pl.BlockSpec(memory_space=pltpu.VMEM),),
        input_output_aliases={i: 2 + i for i in range(1 + n)},
        compiler_params=pltpu.CompilerParams(has_side_effects=EFFECT),
    )(_hbm(packed), *lands, after)
    return outs[0], outs[1], outs[2], list(outs[3:3 + n]), outs[-1]


def _gather_wait(send_sems, recv_sems, packed, lands, after, name):
    n = len(lands)

    def body(packed_ref, *refs):
        s_sems, r_sems = refs[n], refs[n + 1]
        me = _mesh_pos()
        for k in range(4):
            cp = _whole_wait(packed_ref, s_sems.at[k], r_sems.at[k], me)
            cp.wait_send()
            cp.wait_recv()

    outs = pl.pallas_call(
        body, name=name,
        out_shape=(pltpu.HBM(packed.shape, BF16), *[pltpu.HBM(t.shape, BF16) for t in lands]),
        in_specs=(HBM_SPEC,) * (1 + n) + (SEM_SPEC, SEM_SPEC, ANY_SPEC),
        out_specs=(HBM_SPEC,) * (1 + n),
        input_output_aliases={i: i for i in range(1 + n)},
        compiler_params=pltpu.CompilerParams(has_side_effects=EFFECT),
    )(packed, *lands, send_sems, recv_sems, after)
    return outs[0], list(outs[1:])


def _gather_finish(packed, rows_list, lands, name):
    n = len(rows_list)

    def body(packed_ref, *refs):
        land = refs[n:2 * n]
        send_sems, recv_sems = refs[2 * n:]
        px, py, pc = _mesh_pos()
        sibling = (px, py, 1 - pc)
        for j, (cx, cy) in enumerate([(1 - px, py), (px, 1 - py), (1 - px, 1 - py)]):
            block = 4 * cx + 2 * cy + pc
            for rows, land_ref in zip(rows_list, land):
                blk = land_ref.at[pl.ds(block * rows, rows), :]
                pltpu.make_async_remote_copy(src_ref=blk, dst_ref=blk, send_sem=send_sems.at[j],
                                             recv_sem=recv_sems.at[j], device_id=sibling,
                                             device_id_type=pl.DeviceIdType.MESH).start()
        for j in range(3):
            cp = _whole_wait(packed_ref, send_sems.at[j], recv_sems.at[j], sibling)
            cp.wait_recv()
            cp.wait_send()

    outs = pl.pallas_call(
        body, name=name,
        out_shape=tuple(jax.ShapeDtypeStruct(t.shape, BF16) for t in lands),
        in_specs=(HBM_SPEC,) * (1 + n), out_specs=(HBM_SPEC,) * n,
        input_output_aliases={1 + i: i for i in range(n)},
        scratch_shapes=[pltpu.SemaphoreType.DMA((3,)), pltpu.SemaphoreType.DMA((3,))],
    )(packed, *lands)
    return list(outs)


def _exchange_start(srcs, rows_list, me_id, after, name):
    n = len(rows_list)
    offs = _offsets(rows_list)
    own = jnp.concatenate([lax.dynamic_slice_in_dim(t, me_id * rows, rows, axis=0)
                           for t, rows in zip(srcs, rows_list)], axis=0)
    recv = lax.dynamic_update_slice_in_dim(lax.empty((N_DEV, sum(rows_list), D_MODEL), BF16), own[None], me_id, axis=0)

    def body(*refs):
        src, recv_ref = refs[:n], refs[n]
        send_sems, recv_sems = refs[n + 2], refs[n + 3]
        token = refs[-1]
        px, py, pc = _mesh_pos()
        me = 4 * px + 2 * py + pc
        for k in range(1, N_DEV):
            qx = 1 - px if k & 4 else px
            qy = 1 - py if k & 2 else py
            qc = 1 - pc if k & 1 else pc
            peer_id = 4 * qx + 2 * qy + qc
            for off, rows, src_ref in zip(offs, rows_list, src):
                pltpu.make_async_remote_copy(
                    src_ref=src_ref.at[pl.ds(peer_id * rows, rows), :], dst_ref=recv_ref.at[me, pl.ds(off, rows), :],
                    send_sem=send_sems.at[k - 1], recv_sem=recv_sems.at[k - 1],
                    device_id=(qx, qy, qc), device_id_type=pl.DeviceIdType.MESH).start()
        token[...] = jnp.zeros_like(token)

    outs = pl.pallas_call(
        body, name=name,
        out_shape=(pltpu.SemaphoreType.DMA((N_DEV - 1,)), pltpu.SemaphoreType.DMA((N_DEV - 1,)),
                   *[pltpu.HBM(t.shape, BF16) for t in srcs], pltpu.HBM(recv.shape, BF16),
                   jax.ShapeDtypeStruct((8, LANE), F32)),
        in_specs=(HBM_SPEC,) * (n + 1) + (ANY_SPEC,),
        out_specs=(SEM_SPEC, SEM_SPEC) + (HBM_SPEC,) * (n + 1) + (pl.BlockSpec(memory_space=pltpu.VMEM),),
        input_output_aliases={i: 2 + i for i in range(n + 1)},
        compiler_params=pltpu.CompilerParams(has_side_effects=EFFECT),
    )(*[_hbm(t) for t in srcs], _hbm(recv), after)
    return outs[0], outs[1], list(outs[2:2 + n]), outs[2 + n], outs[-1]


def _exchange_wait(send_sems, recv_sems, srcs, recv, after, name):
    n = len(srcs)

    def body(*refs):
        recv_ref = refs[n]
        s_sems, r_sems = refs[n + 1], refs[n + 2]
        me = _mesh_pos()
        for k in range(N_DEV - 1):
            cp = _whole_wait(recv_ref.at[0], s_sems.at[k], r_sems.at[k], me)
            cp.wait_send()
            cp.wait_recv()

    outs = pl.pallas_call(
        body, name=name,
        out_shape=(*[pltpu.HBM(t.shape, BF16) for t in srcs], pltpu.HBM(recv.shape, BF16)),
        in_specs=(HBM_SPEC,) * (n + 1) + (SEM_SPEC, SEM_SPEC, ANY_SPEC),
        out_specs=(HBM_SPEC,) * (n + 1),
        input_output_aliases={i: i for i in range(n + 1)},
        compiler_params=pltpu.CompilerParams(has_side_effects=EFFECT),
    )(*srcs, recv, send_sems, recv_sems, after)
    return outs[n]


def _sum_slots(recv, name, after=None):
    _, r, n = recv.shape
    tr = _row_tile(r, 512)

    def body(in_ref, *refs):
        acc = in_ref[0].astype(F32)
        for j in range(1, N_DEV):
            acc = acc + in_ref[j].astype(F32)
        refs[-1][...] = acc

    if tr >= LANE or tr == r:
        grid = (r // tr,)
        in_specs, out_spec = [pl.BlockSpec((N_DEV, tr, n), lambda i: (0, i, 0))], pl.BlockSpec((tr, n), lambda i: (i, 0))
    else:
        tl = 2 * LANE
        grid = (n // tl,)
        in_specs, out_spec = [pl.BlockSpec((N_DEV, r, tl), lambda i: (0, 0, i))], pl.BlockSpec((r, tl), lambda i: (0, i))
    args = [recv]
    if after is not None:
        in_specs.append(ANY_SPEC)
        args.append(after)
    return pl.pallas_call(
        body, name=name, grid=grid,
        out_shape=jax.ShapeDtypeStruct((r, n), F32),
        in_specs=in_specs, out_specs=out_spec,
        compiler_params=_params(("parallel",)),
    )(*args)


def _row_tile(rows, target):
    if rows <= target:
        return rows
    best = None
    for t in range(16, target + 1, 16):
        if rows % t == 0:
            best = t
    assert best is not None, rows
    return best


_DIMS = {"nn": ((1,), (0,)), "nt": ((1,), (1,)), "tn": ((0,), (0,))}


def _mm(a, b, mode, name, out_dtype=F32, res=None, gate=None, gate_factor=1.0, tm=512, tn=1408):
    if mode == "tn":
        kdim, m = a.shape
    else:
        m, kdim = a.shape
    n = b.shape[0] if mode == "nt" else b.shape[1]
    tm, tn = _tile(m, tm), _tile(n, tn)
    a_spec = (pl.BlockSpec((kdim, tm), lambda i, j: (0, i)) if mode == "tn"
              else pl.BlockSpec((tm, kdim), lambda i, j: (i, 0)))
    b_spec = (pl.BlockSpec((tn, kdim), lambda i, j: (j, 0)) if mode == "nt"
              else pl.BlockSpec((kdim, tn), lambda i, j: (0, j)))
    o_spec = pl.BlockSpec((tm, tn), lambda i, j: (i, j))
    dims = (_DIMS[mode], ((), ()))
    has_res, has_gate = res is not None, gate is not None

    def body(*refs):
        a_ref, b_ref = refs[0], refs[1]
        y = lax.dot_general(a_ref[...].astype(BF16), b_ref[...].astype(BF16), dims,
                            preferred_element_type=F32)
        if not has_res:
            refs[2][...] = y.astype(out_dtype)
            return
        res_ref = refs[2]
        if has_gate:
            y_ref, o_ref = refs[4], refs[5]
            y_ref[...] = y
            o_ref[...] = res_ref[...] + (gate_factor * refs[3][...]) * y
        else:
            refs[3][...] = res_ref[...] + y

    in_specs, args = [a_spec, b_spec], [a, b]
    if has_res:
        in_specs.append(o_spec)
        args.append(res)
        if has_gate:
            in_specs.append(pl.BlockSpec((1, tn), lambda i, j: (0, j)))
            args.append(gate)
            out_shape = (jax.ShapeDtypeStruct((m, n), F32), jax.ShapeDtypeStruct((m, n), F32))
            out_specs = (o_spec, o_spec)
        else:
            out_shape, out_specs = jax.ShapeDtypeStruct((m, n), F32), o_spec
    else:
        out_shape, out_specs = jax.ShapeDtypeStruct((m, n), out_dtype), o_spec
    return pl.pallas_call(
        body, name=name, grid=(m // tm, n // tn), out_shape=out_shape,
        in_specs=in_specs, out_specs=out_specs,
        compiler_params=_params(("parallel", "parallel")),
    )(*args)


def _vec_spec(width):
    return pl.BlockSpec((1, width), lambda i: (0, 0))


def _rm_fwd(x, gw, shift, scale, name):
    s, d = x.shape
    ts = _tile(s, 256)

    def body(x_ref, gw_ref, sh_ref, sc_ref, h_ref):
        xv = x_ref[...]
        r = lax.rsqrt(jnp.mean(xv * xv, axis=-1, keepdims=True) + EPS)
        y = (xv * r) * gw_ref[...]
        h_ref[...] = (y * (1 + sc_ref[...]) + sh_ref[...]).astype(BF16)

    row = pl.BlockSpec((ts, d), lambda i: (i, 0))
    return pl.pallas_call(
        body, name=name, grid=(s // ts,), out_shape=jax.ShapeDtypeStruct((s, d), BF16),
        in_specs=[row, _vec_spec(d), _vec_spec(d), _vec_spec(d)], out_specs=row,
        compiler_params=_params(("parallel",)),
    )(x, gw, shift, scale)


def _rm_bwd(dh, x, dres, gw, scale, name):
    s, d = x.shape
    ts = _tile(s, 256)

    def body(dh_ref, x_ref, dres_ref, gw_ref, sc_ref, dx_ref, dsh_ref, dsc_ref, dgw_ref):
        @pl.when(pl.program_id(0) == 0)
        def _():
            dsh_ref[...] = jnp.zeros_like(dsh_ref)
            dsc_ref[...] = jnp.zeros_like(dsc_ref)
            dgw_ref[...] = jnp.zeros_like(dgw_ref)

        xv, dhv, gwv = x_ref[...], dh_ref[...], gw_ref[...]
        r = lax.rsqrt(jnp.mean(xv * xv, axis=-1, keepdims=True) + EPS)
        xn = xv * r
        y = xn * gwv
        dsh_ref[...] += jnp.sum(dhv, axis=0, keepdims=True)
        dsc_ref[...] += jnp.sum(dhv * y, axis=0, keepdims=True)
        dy = dhv * (1 + sc_ref[...])
        dgw_ref[...] += jnp.sum(dy * xn, axis=0, keepdims=True)
        dxn = dy * gwv
        dx = r * (dxn - xn * jnp.mean(dxn * xn, axis=-1, keepdims=True))
        dx_ref[...] = dres_ref[...] + dx

    row = pl.BlockSpec((ts, d), lambda i: (i, 0))
    vec = jax.ShapeDtypeStruct((1, d), F32)
    return pl.pallas_call(
        body, name=name, grid=(s // ts,),
        out_shape=(jax.ShapeDtypeStruct((s, d), F32), vec, vec, vec),
        in_specs=[row, row, row, _vec_spec(d), _vec_spec(d)],
        out_specs=(row, _vec_spec(d), _vec_spec(d), _vec_spec(d)),
        compiler_params=_params(("arbitrary",)),
    )(dh, x, dres, gw, scale)


def _gate_bwd(dx, y, gate, factor, name):
    s, d = dx.shape
    ts = _tile(s, 256)

    def body(dx_ref, y_ref, g_ref, dy_ref, dg_ref):
        @pl.when(pl.program_id(0) == 0)
        def _():
            dg_ref[...] = jnp.zeros_like(dg_ref)

        dxv = dx_ref[...]
        dy_ref[...] = ((factor * g_ref[...]) * dxv).astype(BF16)
        dg_ref[...] += jnp.sum((factor * dxv) * y_ref[...], axis=0, keepdims=True)

    row = pl.BlockSpec((ts, d), lambda i: (i, 0))
    return pl.pallas_call(
        body, name=name, grid=(s // ts,),
        out_shape=(jax.ShapeDtypeStruct((s, d), BF16), jax.ShapeDtypeStruct((1, d), F32)),
        in_specs=[row, row, _vec_spec(d)], out_specs=(row, _vec_spec(d)),
        compiler_params=_params(("arbitrary",)),
    )(dx, y, gate)


FFN_TM, FFN_TF = 1024, 256


def _ffn_up(x, gw, shift, scale, wg, wu, name):
    s, d = x.shape
    f = wg.shape[0]
    tm, tf = _tile(s, FFN_TM), _tile(f, FFN_TF)
    nt = (((1,), (1,)), ((), ()))

    def body(x_ref, gw_ref, sh_ref, sc_ref, wg_ref, wu_ref, h_ref, a_ref, b_ref, t_ref):
        @pl.when(pl.program_id(1) == 0)
        def _():
            xv = x_ref[...]
            r = lax.rsqrt(jnp.mean(xv * xv, axis=-1, keepdims=True) + EPS)
            h_ref[...] = (((xv * r) * gw_ref[...]) * (1 + sc_ref[...]) + sh_ref[...]).astype(BF16)

        hb = h_ref[...]
        av = lax.dot_general(hb, wg_ref[...], nt, preferred_element_type=F32)
        bv = lax.dot_general(hb, wu_ref[...], nt, preferred_element_type=F32)
        a_ref[...] = av.astype(BF16)
        b_ref[...] = bv.astype(BF16)
        t_ref[...] = ((av * jax.nn.sigmoid(av)) * bv).astype(BF16)

    row = pl.BlockSpec((tm, d), lambda i, j: (i, 0))
    vec = pl.BlockSpec((1, d), lambda i, j: (0, 0))
    wblk = pl.BlockSpec((tf, d), lambda i, j: (j, 0))
    blk = pl.BlockSpec((tm, tf), lambda i, j: (i, j))
    wide = jax.ShapeDtypeStruct((s, f), BF16)
    return pl.pallas_call(
        body, name=name, grid=(s // tm, f // tf),
        out_shape=(jax.ShapeDtypeStruct((s, d), BF16), wide, wide, wide),
        in_specs=[row, vec, vec, vec, wblk, wblk], out_specs=(row, blk, blk, blk),
        compiler_params=_params(("parallel", "arbitrary")),
    )(x, gw, shift, scale, wg, wu)


def _ffn_bwd_cols(dy, h, a, b, t, wd, name):
    s, d = dy.shape
    f = wd.shape[0]
    tf = _tile(f, FFN_TF)
    nt = (((1,), (1,)), ((), ()))
    tn = (((0,), (0,)), ((), ()))

    def body(dy_ref, h_ref, a_ref, b_ref, t_ref, wd_ref, da_ref, db_ref, gd_ref, gg_ref, gu_ref):
        dyb, hb = dy_ref[...], h_ref[...]
        dtv = lax.dot_general(dyb, wd_ref[...], nt, preferred_element_type=F32)
        av, bv = a_ref[...].astype(F32), b_ref[...].astype(F32)
        sg = jax.nn.sigmoid(av)
        dbv = (dtv * (av * sg)).astype(BF16)
        dav = ((dtv * bv) * (sg * (1 + av * (1 - sg)))).astype(BF16)
        da_ref[...] = dav
        db_ref[...] = dbv
        gd_ref[...] = lax.dot_general(t_ref[...], dyb, tn, preferred_element_type=F32).astype(BF16)
        gg_ref[...] = lax.dot_general(dav, hb, tn, preferred_element_type=F32).astype(BF16)
        gu_ref[...] = lax.dot_general(dbv, hb, tn, preferred_element_type=F32).astype(BF16)

    whole = pl.BlockSpec((s, d), lambda j: (0, 0))
    col = pl.BlockSpec((s, tf), lambda j: (0, j))
    wblk = pl.BlockSpec((tf, d), lambda j: (j, 0))
    wide, wgrad = jax.ShapeDtypeStruct((s, f), BF16), jax.ShapeDtypeStruct((f, d), BF16)
    return pl.pallas_call(
        body, name=name, grid=(f // tf,), out_shape=(wide, wide, wgrad, wgrad, wgrad),
        in_specs=[whole, whole, col, col, col, wblk], out_specs=(col, col, wblk, wblk, wblk),
        compiler_params=_params(("parallel",)),
    )(dy, h, a, b, t, wd)


def _mm_pair(a1, b1, a2, b2, name, tm=512, tn=512):
    m, kdim = a1.shape
    n = b1.shape[1]
    tm, tn = _tile(m, tm), _tile(n, tn)

    def body(a1_ref, b1_ref, a2_ref, b2_ref, o_ref):
        o_ref[...] = (jnp.dot(a1_ref[...], b1_ref[...], preferred_element_type=F32)
                      + jnp.dot(a2_ref[...], b2_ref[...], preferred_element_type=F32))

    a_spec = pl.BlockSpec((tm, kdim), lambda i, j: (i, 0))
    b_spec = pl.BlockSpec((kdim, tn), lambda i, j: (0, j))
    return pl.pallas_call(
        body, name=name, grid=(m // tm, n // tn), out_shape=jax.ShapeDtypeStruct((m, n), F32),
        in_specs=[a_spec, b_spec, a_spec, b_spec], out_specs=pl.BlockSpec((tm, tn), lambda i, j: (i, j)),
        compiler_params=_params(("parallel", "parallel")),
    )(a1, b1, a2, b2)


def _pool_counts(s):
    return (lax.broadcasted_iota(jnp.int32, (s, POOL_GC), 0))


def _pool_fwd(z, pool_w, pool_scale, name):
    s = z.shape[0]

    def body(u_ref, w_ref, sc_ref, y_ref, diff_ref):
        t = lax.broadcasted_iota(jnp.int32, (s, POOL_GC), 0)
        for g, win in enumerate(POOL_WINDOWS):
            cols = slice(g * POOL_GC, (g + 1) * POOL_GC)
            u = u_ref[:, cols]
            acc, step = u, 1
            while step < win:
                acc = acc + jnp.where(t >= step, pltpu.roll(acc, step, 0), 0.0)
                step *= 2
            cnt = jnp.minimum(t + 1, win).astype(F32)
            diff = acc / cnt - u
            diff_ref[:, cols] = diff
            ypre = jnp.dot(diff.astype(BF16), w_ref[g].astype(BF16), preferred_element_type=F32)
            y_ref[:, cols] = (ypre * sc_ref[:, cols]).astype(BF16)

    return pl.pallas_call(
        body, name=name, grid=(1,),
        out_shape=(jax.ShapeDtypeStruct((s, POOL_WIDTH), BF16), jax.ShapeDtypeStruct((s, POOL_WIDTH), F32)),
        in_specs=[pl.BlockSpec((s, POOL_WIDTH), lambda i: (0, 0)),
                  pl.BlockSpec(pool_w.shape, lambda i: (0, 0, 0)),
                  pl.BlockSpec((1, POOL_WIDTH), lambda i: (0, 0))],
        out_specs=(pl.BlockSpec((s, POOL_WIDTH), lambda i: (0, 0)),
                   pl.BlockSpec((s, POOL_WIDTH), lambda i: (0, 0))),
        compiler_params=_params(("arbitrary",)),
    )(z, pool_w, pool_scale)


def _pool_bwd(dycat, diff, pool_w, pool_scale, name):
    s = diff.shape[0]

    def body(dy_ref, diff_ref, w_ref, sc_ref, du_ref, dw_ref, dsc_ref):
        t = lax.broadcasted_iota(jnp.int32, (s, POOL_GC), 0)
        for g, win in enumerate(POOL_WINDOWS):
            cols = slice(g * POOL_GC, (g + 1) * POOL_GC)
            dy, dfb, wb = dy_ref[:, cols], diff_ref[:, cols].astype(BF16), w_ref[g].astype(BF16)
            ypre = jnp.dot(dfb, wb, preferred_element_type=F32)
            dsc_ref[:, cols] = jnp.sum(dy * ypre, axis=0, keepdims=True)
            dypre = (dy * sc_ref[:, cols]).astype(BF16)
            ddiff = lax.dot_general(dypre, wb, (((1,), (1,)), ((), ())), preferred_element_type=F32)
            dw_ref[g] = lax.dot_general(dfb, dypre, (((0,), (0,)), ((), ())), preferred_element_type=F32)
            cnt = jnp.minimum(t + 1, win).astype(F32)
            acc, step = ddiff / cnt, 1
            while step < win:
                acc = acc + jnp.where(t < s - step, pltpu.roll(acc, s - step, 0), 0.0)
                step *= 2
            du_ref[:, cols] = acc - ddiff

    full = pl.BlockSpec((s, POOL_WIDTH), lambda i: (0, 0))
    return pl.pallas_call(
        body, name=name, grid=(1,),
        out_shape=(jax.ShapeDtypeStruct((s, POOL_WIDTH), F32),
                   jax.ShapeDtypeStruct(pool_w.shape, F32),
                   jax.ShapeDtypeStruct((1, POOL_WIDTH), F32)),
        in_specs=[full, full, pl.BlockSpec(pool_w.shape, lambda i: (0, 0, 0)),
                  pl.BlockSpec((1, POOL_WIDTH), lambda i: (0, 0))],
        out_specs=(full, pl.BlockSpec(pool_w.shape, lambda i: (0, 0, 0)),
                   pl.BlockSpec((1, POOL_WIDTH), lambda i: (0, 0))),
        compiler_params=_params(("arbitrary",)),
    )(dycat, diff, pool_w, pool_scale)


def _rope_tables(positions, name):
    s = positions.shape[0]
    ts = _tile(s, 512)
    freq = 1.0 / (ROPE_THETA ** (np.arange(0, QK_ROPE, 2, dtype=np.float32) / QK_ROPE))
    table = np.zeros((1, LANE), np.float32)
    table[0, :QK_ROPE // 2] = freq
    table[0, QK_ROPE // 2:QK_ROPE] = freq

    def body(pos_ref, f_ref, cos_ref, sin_ref):
        ang = pos_ref[...].astype(F32) * f_ref[...]
        cos_ref[...] = jnp.cos(ang)
        sin_ref[...] = jnp.sin(ang)

    out = jax.ShapeDtypeStruct((s, LANE), F32)
    blk = pl.BlockSpec((ts, LANE), lambda i: (i, 0))
    return pl.pallas_call(
        body, name=name, grid=(s // ts,), out_shape=(out, out),
        in_specs=[pl.BlockSpec((ts, 1), lambda i: (i, 0)), _vec_spec(LANE)], out_specs=(blk, blk),
        compiler_params=_params(("parallel",)),
    )(positions, jnp.asarray(table))


def _lane_mod64_low(shape):
    return (lax.broadcasted_iota(jnp.int32, shape, 1) % QK_ROPE) < (QK_ROPE // 2)


def _rope(x, cos, sin):
    rot = jnp.where(_lane_mod64_low(x.shape), -pltpu.roll(x, LANE - 32, 1), pltpu.roll(x, 32, 1))
    return x * cos + rot * sin


def _rope_t(dy, cos, sin):
    w = dy * sin
    rot_t = jnp.where(_lane_mod64_low(dy.shape), pltpu.roll(w, LANE - 32, 1), -pltpu.roll(w, 32, 1))
    return dy * cos + rot_t


def _plain_rms(x, g):
    r = lax.rsqrt(jnp.mean(x * x, axis=-1, keepdims=True) + EPS)
    return (x * r) * g, x * r, r


O_Q, O_KV, O_KR = POOL_WIDTH, POOL_WIDTH + Q_LORA, POOL_WIDTH + Q_LORA + KV_LORA


def _qkv_fwd(z, qn, kvn, wq, wkv, cos, sin, name):
    s = z.shape[0]
    ts = _tile(s, 256)

    def body(z_ref, qn_ref, kvn_ref, wq_ref, wkv_ref, cos_ref, sin_ref, q_ref, k_ref, v_ref, cqn_ref, ckvn_ref):
        cosv, sinv = cos_ref[...], sin_ref[...]
        cqn = _plain_rms(z_ref[:, O_Q:O_KV], qn_ref[...])[0].astype(BF16)
        ckvn = _plain_rms(z_ref[:, O_KV:O_KR], kvn_ref[...])[0].astype(BF16)
        cqn_ref[...] = cqn
        ckvn_ref[...] = ckvn
        nt = (((1,), (1,)), ((), ()))
        q = lax.dot_general(cqn, wq_ref[...], nt, preferred_element_type=F32)
        kv = lax.dot_general(ckvn, wkv_ref[...], nt, preferred_element_type=F32)
        kr = _rope(z_ref[:, O_KR:IN_PAD], cosv, sinv).astype(BF16)
        for h in range(N_HEADS):
            o = h * HEAD_PAD
            q_ref[:, o:o + QK_NOPE] = q[:, o:o + QK_NOPE].astype(BF16)
            q_ref[:, o + QK_NOPE:o + HEAD_PAD] = _rope(q[:, o + QK_NOPE:o + HEAD_PAD], cosv, sinv).astype(BF16)
            k_ref[:, o:o + QK_NOPE] = kv[:, o:o + QK_NOPE].astype(BF16)
            k_ref[:, o + QK_NOPE:o + HEAD_PAD] = kr
            v_ref[:, h * V_HEAD:(h + 1) * V_HEAD] = kv[:, o + QK_NOPE:o + HEAD_PAD].astype(BF16)

    def row(w):
        return pl.BlockSpec((ts, w), lambda i: (i, 0))

    def whole(arr):
        return pl.BlockSpec(arr.shape, lambda i: (0, 0))

    hp = N_HEADS * HEAD_PAD
    return pl.pallas_call(
        body, name=name, grid=(s // ts,),
        out_shape=(jax.ShapeDtypeStruct((s, hp), BF16), jax.ShapeDtypeStruct((s, hp), BF16),
                   jax.ShapeDtypeStruct((s, N_HEADS * V_HEAD), BF16),
                   jax.ShapeDtypeStruct((s, Q_LORA), BF16), jax.ShapeDtypeStruct((s, KV_LORA), BF16)),
        in_specs=[row(IN_PAD), whole(qn), whole(kvn), whole(wq), whole(wkv), row(LANE), row(LANE)],
        out_specs=(row(hp), row(hp), row(N_HEADS * V_HEAD), row(Q_LORA), row(KV_LORA)),
        compiler_params=_params(("parallel",)),
    )(z, qn, kvn, wq, wkv, cos, sin)


def _qkv_bwd(dq, dk, dv, du, z, qn, kvn, wq, wkv, cos, sin, name):
    s = z.shape[0]
    ts = _tile(s, 256)

    def norm_bwd(x, g, dy):
        _, xn, r = _plain_rms(x, g)
        dxn = dy * g
        return r * (dxn - xn * jnp.mean(dxn * xn, axis=-1, keepdims=True)), jnp.sum(dy * xn, axis=0, keepdims=True)

    def body(dq_ref, dk_ref, dv_ref, du_ref, z_ref, qn_ref, kvn_ref, wq_ref, wkv_ref, cos_ref, sin_ref,
             dz_ref, dqb_ref, dkvb_ref, dqn_ref, dkvn_ref):
        @pl.when(pl.program_id(0) == 0)
        def _():
            dqn_ref[...] = jnp.zeros_like(dqn_ref)
            dkvn_ref[...] = jnp.zeros_like(dkvn_ref)

        cosv, sinv = cos_ref[...], sin_ref[...]
        dkr = jnp.zeros((ts, LANE), F32)
        for h in range(N_HEADS):
            o = h * HEAD_PAD
            dqb_ref[:, o:o + QK_NOPE] = dq_ref[:, o:o + QK_NOPE].astype(BF16)
            dqb_ref[:, o + QK_NOPE:o + HEAD_PAD] = _rope_t(dq_ref[:, o + QK_NOPE:o + HEAD_PAD], cosv, sinv).astype(BF16)
            dkvb_ref[:, o:o + QK_NOPE] = dk_ref[:, o:o + QK_NOPE].astype(BF16)
            dkvb_ref[:, o + QK_NOPE:o + HEAD_PAD] = dv_ref[:, h * V_HEAD:(h + 1) * V_HEAD].astype(BF16)
            dkr = dkr + dk_ref[:, o + QK_NOPE:o + HEAD_PAD]
        dcqn = jnp.dot(dqb_ref[...], wq_ref[...], preferred_element_type=F32)
        dckvn = jnp.dot(dkvb_ref[...], wkv_ref[...], preferred_element_type=F32)
        dcq, dqn = norm_bwd(z_ref[:, O_Q:O_KV], qn_ref[...], dcqn)
        dckv, dkvn = norm_bwd(z_ref[:, O_KV:O_KR], kvn_ref[...], dckvn)
        dqn_ref[...] += dqn
        dkvn_ref[...] += dkvn
        dz_ref[:, 0:O_Q] = du_ref[...].astype(BF16)
        dz_ref[:, O_Q:O_KV] = dcq.astype(BF16)
        dz_ref[:, O_KV:O_KR] = dckv.astype(BF16)
        dz_ref[:, O_KR:IN_PAD] = _rope_t(dkr, cosv, sinv).astype(BF16)

    def row(w):
        return pl.BlockSpec((ts, w), lambda i: (i, 0))

    def whole(arr):
        return pl.BlockSpec(arr.shape, lambda i: (0, 0))

    hp = N_HEADS * HEAD_PAD
    return pl.pallas_call(
        body, name=name, grid=(s // ts,),
        out_shape=(jax.ShapeDtypeStruct((s, IN_PAD), BF16), jax.ShapeDtypeStruct((s, hp), BF16),
                   jax.ShapeDtypeStruct((s, hp), BF16),
                   jax.ShapeDtypeStruct((1, Q_LORA), F32), jax.ShapeDtypeStruct((1, KV_LORA), F32)),
        in_specs=[row(hp), row(hp), row(N_HEADS * V_HEAD), row(POOL_WIDTH), row(IN_PAD),
                  whole(qn), whole(kvn), whole(wq), whole(wkv), row(LANE), row(LANE)],
        out_specs=(row(IN_PAD), row(hp), row(hp), whole(qn), whole(kvn)),
        compiler_params=_params(("arbitrary",)),
    )(dq, dk, dv, du, z, qn, kvn, wq, wkv, cos, sin)


def _causal_scores(q, k, i, tq, klen):
    sc = lax.dot_general(q, k, (((1,), (1,)), ((), ())), preferred_element_type=F32) * SOFTMAX_SCALE
    qpos = i * tq + lax.broadcasted_iota(jnp.int32, (tq, klen), 0)
    kpos = lax.broadcasted_iota(jnp.int32, (tq, klen), 1)
    return jnp.where(qpos >= kpos, sc, -jnp.inf)


ATTN_SEGMENTS = 4


def _by_key_prefix(i, nq, tq, compute):
    nseg = min(ATTN_SEGMENTS, nq)
    per = nq // nseg
    for r in range(nseg):
        pl.when(i // per == r)(lambda r=r: compute((r + 1) * per * tq))


def _attn_fwd(q, k, v, name):
    s = q.shape[0]
    tq = _tile(s, 256)
    nq = s // tq

    def body(q_ref, k_ref, v_ref, o_ref, lse_ref):
        i = pl.program_id(1)

        def compute(klen):
            sc = _causal_scores(q_ref[...], k_ref[0:klen, :], i, tq, klen)
            mx = jnp.max(sc, axis=-1, keepdims=True)
            p = jnp.exp(sc - mx)
            den = jnp.sum(p, axis=-1, keepdims=True)
            o_ref[...] = jnp.dot((p / den).astype(BF16), v_ref[0:klen, :], preferred_element_type=F32)
            lse_ref[...] = mx + jnp.log(den)

        _by_key_prefix(i, nq, tq, compute)

    return pl.pallas_call(
        body, name=name, grid=(N_HEADS, s // tq),
        out_shape=(jax.ShapeDtypeStruct((s, N_HEADS * V_HEAD), F32), jax.ShapeDtypeStruct((N_HEADS, s, 1), F32)),
        in_specs=[pl.BlockSpec((tq, HEAD_PAD), lambda h, i: (i, h)),
                  pl.BlockSpec((s, HEAD_PAD), lambda h, i: (0, h)),
                  pl.BlockSpec((s, V_HEAD), lambda h, i: (0, h))],
        out_specs=(pl.BlockSpec((tq, V_HEAD), lambda h, i: (i, h)),
                   pl.BlockSpec((None, tq, 1), lambda h, i: (h, i, 0))),
        compiler_params=_params(("parallel", "parallel")),
    )(q, k, v)


def _attn_bwd(q, k, v, lse, dycat, name):
    s = q.shape[0]
    tq = _tile(s, 256)
    nq = s // tq
    tn_dims = (((0,), (0,)), ((), ()))

    def body(q_ref, k_ref, v_ref, lse_ref, do_ref, dq_ref, dk_ref, dv_ref):
        i = pl.program_id(1)

        @pl.when(i == 0)
        def _():
            dk_ref[...] = jnp.zeros_like(dk_ref)
            dv_ref[...] = jnp.zeros_like(dv_ref)

        def compute(klen):
            qv, kv_, dob = q_ref[...], k_ref[0:klen, :], do_ref[...].astype(BF16)
            sc = _causal_scores(qv, kv_, i, tq, klen)
            p = jnp.exp(sc - lse_ref[...])
            dp = lax.dot_general(dob, v_ref[0:klen, :], (((1,), (1,)), ((), ())), preferred_element_type=F32)
            ds = (p * (dp - jnp.sum(dp * p, axis=-1, keepdims=True)) * SOFTMAX_SCALE).astype(BF16)
            dq_ref[...] = jnp.dot(ds, kv_, preferred_element_type=F32)
            dk_ref[0:klen, :] += lax.dot_general(ds, qv, tn_dims, preferred_element_type=F32)
            dv_ref[0:klen, :] += lax.dot_general(p.astype(BF16), dob, tn_dims, preferred_element_type=F32)

        _by_key_prefix(i, nq, tq, compute)

    n_pool_blocks = POOL_WIDTH // V_HEAD
    return pl.pallas_call(
        body, name=name, grid=(N_HEADS, s // tq),
        out_shape=(jax.ShapeDtypeStruct((s, N_HEADS * HEAD_PAD), F32),
                   jax.ShapeDtypeStruct((s, N_HEADS * HEAD_PAD), F32),
                   jax.ShapeDtypeStruct((s, N_HEADS * V_HEAD), F32)),
        in_specs=[pl.BlockSpec((tq, HEAD_PAD), lambda h, i: (i, h)),
                  pl.BlockSpec((s, HEAD_PAD), lambda h, i: (0, h)),
                  pl.BlockSpec((s, V_HEAD), lambda h, i: (0, h)),
                  pl.BlockSpec((None, tq, 1), lambda h, i: (h, i, 0)),
                  pl.BlockSpec((tq, V_HEAD), lambda h, i: (i, n_pool_blocks + h))],
        out_specs=(pl.BlockSpec((tq, HEAD_PAD), lambda h, i: (i, h)),
                   pl.BlockSpec((s, HEAD_PAD), lambda h, i: (0, h)),
                   pl.BlockSpec((s, V_HEAD), lambda h, i: (0, h))),
        compiler_params=_params(("parallel", "arbitrary")),
    )(q, k, v, lse, dycat)


def _loss_head(x, gw, target, name):
    s, d = x.shape
    ts = _tile(s, 256)

    def body(x_ref, gw_ref, tgt_ref, loss_ref, dx_ref, dgw_ref):
        @pl.when(pl.program_id(0) == 0)
        def _():
            loss_ref[...] = jnp.zeros_like(loss_ref)
            dgw_ref[...] = jnp.zeros_like(dgw_ref)

        xv, gwv = x_ref[...], gw_ref[...]
        r = lax.rsqrt(jnp.mean(xv * xv, axis=-1, keepdims=True) + EPS)
        xn = xv * r
        err = xn * gwv - tgt_ref[...]
        loss_ref[...] += 0.5 * jnp.sum(jnp.mean(err * err, axis=-1, keepdims=True))
        dy = err / d
        dgw_ref[...] += jnp.sum(dy * xn, axis=0, keepdims=True)
        dxn = dy * gwv
        dx_ref[...] = r * (dxn - xn * jnp.mean(dxn * xn, axis=-1, keepdims=True))

    row = pl.BlockSpec((ts, d), lambda i: (i, 0))
    return pl.pallas_call(
        body, name=name, grid=(s // ts,),
        out_shape=(jax.ShapeDtypeStruct((8, LANE), F32), jax.ShapeDtypeStruct((s, d), F32),
                   jax.ShapeDtypeStruct((1, d), F32)),
        in_specs=[row, _vec_spec(d), row],
        out_specs=(pl.BlockSpec((8, LANE), lambda i: (0, 0)), row, _vec_spec(d)),
        compiler_params=_params(("arbitrary",)),
    )(x, gw, target)


def _ada_mod(c_all, ada_w, ada_b, name):
    nl, d, cols = ada_w.shape

    def body(c_ref, w_ref, b_ref, o_ref):
        cv = c_ref[...]
        act = (cv * jax.nn.sigmoid(cv)).astype(BF16)
        o_ref[...] = jnp.dot(act, w_ref[...].astype(BF16), preferred_element_type=F32) + b_ref[...]

    return pl.pallas_call(
        body, name=name, grid=(nl,), out_shape=jax.ShapeDtypeStruct((nl, N_DEV, cols), F32),
        in_specs=[pl.BlockSpec((N_DEV, d), lambda l: (0, 0)),
                  pl.BlockSpec((None, d, cols), lambda l: (l, 0, 0)),
                  pl.BlockSpec((None, 1, cols), lambda l: (l, 0, 0))],
        out_specs=pl.BlockSpec((None, N_DEV, cols), lambda l: (l, 0, 0)),
        compiler_params=_params(("parallel",)),
    )(c_all, ada_w, ada_b)


def _ada_grad(c_pad, dmod_pad, name):
    nl, kpad, cols = dmod_pad.shape
    d = c_pad.shape[1]

    def body(c_ref, dm_ref, o_ref):
        cv = c_ref[...]
        act = (cv * jax.nn.sigmoid(cv)).astype(BF16)
        o_ref[...] = lax.dot_general(act, dm_ref[...].astype(BF16), (((0,), (0,)), ((), ())),
                                     preferred_element_type=F32)

    return pl.pallas_call(
        body, name=name, grid=(nl,), out_shape=jax.ShapeDtypeStruct((nl, d, cols), F32),
        in_specs=[pl.BlockSpec((kpad, d), lambda l: (0, 0)),
                  pl.BlockSpec((None, kpad, cols), lambda l: (l, 0, 0))],
        out_specs=pl.BlockSpec((None, d, cols), lambda l: (l, 0, 0)),
        compiler_params=_params(("parallel",)),
    )(c_pad, dmod_pad)


def _adamw(w, g, m, v, name):
    rows, cols = w.shape
    tr = _row_tile(rows, 256)

    def body(w_ref, g_ref, m_ref, v_ref, d_ref, nm_ref, nv_ref):
        gv = g_ref[...]
        nm = ADAM_B1 * m_ref[...] + (1.0 - ADAM_B1) * gv
        nv = ADAM_B2 * v_ref[...] + (1.0 - ADAM_B2) * (gv * gv)
        m_hat = nm / (1.0 - ADAM_B1 ** ADAM_STEP)
        v_hat = nv / (1.0 - ADAM_B2 ** ADAM_STEP)
        d_ref[...] = -ADAM_LR * (m_hat / (jnp.sqrt(v_hat) + ADAM_EPS) + ADAM_WD * w_ref[...])
        nm_ref[...] = nm
        nv_ref[...] = nv

    blk = pl.BlockSpec((tr, cols), lambda i: (i, 0))
    out = jax.ShapeDtypeStruct((rows, cols), F32)
    return pl.pallas_call(
        body, name=name, grid=(rows // tr,), out_shape=(out, out, out),
        in_specs=[blk, blk, blk, blk], out_specs=(blk, blk, blk),
        compiler_params=_params(("parallel",)),
    )(w, g, m, v)


def _adamw_nd(w, g, m, v, name):
    shape = w.shape
    flat = (lambda t: t.reshape(1, -1)) if w.ndim == 1 else (lambda t: t.reshape(-1, shape[-1]))
    return tuple(t.reshape(shape) for t in _adamw(flat(w), flat(g), flat(m), flat(v), name))


def _pad_rows(t, rows):
    return jnp.pad(t, ((0, rows - t.shape[0]), (0, 0)))


def _pack_shard_layer(l, wts):
    def tr(name):
        return wts[name][l].astype(BF16).T

    parts = [tr("ffn1_w_gate"), tr("ffn1_w_up"), wts["ffn1_w_down"][l].astype(BF16),
             tr("ffn2_w_gate"), tr("ffn2_w_up"), wts["ffn2_w_down"][l].astype(BF16),
             wts["w_out"][l].astype(BF16),
             tr("w_kv_b").reshape(KV_SH_ROWS, D_MODEL),
             _pad_rows(tr("w_in"), 160),
             _pad_rows(tr("w_q_b").reshape(Q_SH_ROWS, D_MODEL), 48)]
    return jnp.concatenate(parts, axis=0)


def _full_weights(lands):
    w = dict(zip(("g1", "u1", "d1", "g2", "u2", "d2", "out"), lands))
    small = lands[-1].reshape(N_DEV, SMALL_ROWS, D_MODEL)
    o_in, o_q = OFF_IN - OFF_KV, OFF_Q - OFF_KV
    w["kv"] = small[:, :KV_SH_ROWS].reshape(N_HEADS * HEAD_PAD, KV_LORA)
    w["in"] = _pad_rows(small[:, o_in:o_in + IN_SH].reshape(IN_COLS, D_MODEL), IN_PAD)
    wq = small[:, o_q:o_q + Q_SH_ROWS].reshape(N_HEADS, QK_HEAD, Q_LORA)
    w["q"] = jnp.pad(wq, ((0, 0), (0, HEAD_PAD - QK_HEAD), (0, 0))).reshape(N_HEADS * HEAD_PAD, Q_LORA)
    return w


def _grad_sources_b(gr):
    gq = gr["q"].reshape(N_HEADS, HEAD_PAD, Q_LORA)[:, :QK_HEAD].reshape(N_DEV, Q_SH_ROWS, D_MODEL)
    small = jnp.concatenate([
        gr["kv"].reshape(N_DEV, KV_SH_ROWS, D_MODEL),
        jnp.pad(gr["in"][:IN_COLS].reshape(N_DEV, IN_SH, D_MODEL), ((0, 0), (0, 160 - IN_SH), (0, 0))),
        jnp.pad(gq, ((0, 0), (0, 48 - Q_SH_ROWS), (0, 0)))], axis=1)
    return [gr["g2"], gr["u2"], gr["d2"], gr["out"], small.reshape(N_DEV * SMALL_ROWS, D_MODEL)]


def _unpack_grad_shards(gs):
    nl = gs.shape[0]

    def tr(off, n):
        return gs[:, off:off + n].transpose(0, 2, 1)

    return {
        "ffn1_w_gate": tr(OFF_G1, FF_SH), "ffn1_w_up": tr(OFF_U1, FF_SH), "ffn1_w_down": gs[:, OFF_D1:OFF_D1 + FF_SH],
        "ffn2_w_gate": tr(OFF_G2, FF_SH), "ffn2_w_up": tr(OFF_U2, FF_SH), "ffn2_w_down": gs[:, OFF_D2:OFF_D2 + FF_SH],
        "w_out": gs[:, OFF_OUT:OFF_OUT + 128],
        "w_kv_b": gs[:, OFF_KV:OFF_KV + KV_SH_ROWS].reshape(nl, -1, KV_LORA).transpose(0, 2, 1),
        "w_in": tr(OFF_IN, IN_SH),
        "w_q_b": gs[:, OFF_Q:OFF_Q + Q_SH_ROWS].reshape(nl, -1, Q_LORA).transpose(0, 2, 1),
    }


def _small_layout(nl):
    names = [("dmod", nl * N_MOD), ("ffn1_norm", nl), ("mix_norm", nl), ("ffn2_norm", nl), ("q_a_norm", nl),
             ("kv_a_norm", nl), ("pool_scale", nl), ("final_norm", 1), ("loss", 1),
             ("pool_w", nl * 4 * POOL_GC * POOL_GC // D_MODEL)]
    off, table = 0, {}
    for name, n in names:
        table[name] = (off, n)
        off += -(-n // 8) * 8
    return table, off


def _to_rows(t, width=D_MODEL):
    n, w = t.shape
    return jnp.pad(t, ((0, -(-n // 8) * 8 - n), (0, width - w)))


def kernel(x, c, positions, ada_w, ada_b, ffn1_norm, ffn1_w_gate, ffn1_w_up, ffn1_w_down, mix_norm, w_in, pool_w, pool_scale, q_a_norm, w_q_b, kv_a_norm, w_kv_b, w_out, ffn2_norm, ffn2_w_gate, ffn2_w_up, ffn2_w_down, final_norm, loss_target, m_ada_w, m_ada_b, m_ffn1_norm, m_ffn1_w_gate, m_ffn1_w_up, m_ffn1_w_down, m_mix_norm, m_w_in, m_pool_w, m_pool_scale, m_q_a_norm, m_w_q_b, m_kv_a_norm, m_w_kv_b, m_w_out, m_ffn2_norm, m_ffn2_w_gate, m_ffn2_w_up, m_ffn2_w_down, m_final_norm, v_ada_w, v_ada_b, v_ffn1_norm, v_ffn1_w_gate, v_ffn1_w_up, v_ffn1_w_down, v_mix_norm, v_w_in, v_pool_w, v_pool_scale, v_q_a_norm, v_w_q_b, v_kv_a_norm, v_w_kv_b, v_w_out, v_ffn2_norm, v_ffn2_w_gate, v_ffn2_w_up, v_ffn2_w_down, v_final_norm):
    wts = dict(ada_w=ada_w, ada_b=ada_b, ffn1_norm=ffn1_norm, ffn1_w_gate=ffn1_w_gate, ffn1_w_up=ffn1_w_up,
               ffn1_w_down=ffn1_w_down, mix_norm=mix_norm, w_in=w_in, pool_w=pool_w, pool_scale=pool_scale,
               q_a_norm=q_a_norm, w_q_b=w_q_b, kv_a_norm=kv_a_norm, w_kv_b=w_kv_b, w_out=w_out,
               ffn2_norm=ffn2_norm, ffn2_w_gate=ffn2_w_gate, ffn2_w_up=ffn2_w_up, ffn2_w_down=ffn2_w_down,
               final_norm=final_norm)
    mom_m = dict(ada_w=m_ada_w, ada_b=m_ada_b, ffn1_norm=m_ffn1_norm, ffn1_w_gate=m_ffn1_w_gate,
                 ffn1_w_up=m_ffn1_w_up, ffn1_w_down=m_ffn1_w_down, mix_norm=m_mix_norm, w_in=m_w_in,
                 pool_w=m_pool_w, pool_scale=m_pool_scale, q_a_norm=m_q_a_norm, w_q_b=m_w_q_b,
                 kv_a_norm=m_kv_a_norm, w_kv_b=m_w_kv_b, w_out=m_w_out, ffn2_norm=m_ffn2_norm,
                 ffn2_w_gate=m_ffn2_w_gate, ffn2_w_up=m_ffn2_w_up, ffn2_w_down=m_ffn2_w_down,
                 final_norm=m_final_norm)
    mom_v = dict(ada_w=v_ada_w, ada_b=v_ada_b, ffn1_norm=v_ffn1_norm, ffn1_w_gate=v_ffn1_w_gate,
                 ffn1_w_up=v_ffn1_w_up, ffn1_w_down=v_ffn1_w_down, mix_norm=v_mix_norm, w_in=v_w_in,
                 pool_w=v_pool_w, pool_scale=v_pool_scale, q_a_norm=v_q_a_norm, w_q_b=v_w_q_b,
                 kv_a_norm=v_kv_a_norm, w_kv_b=v_w_kv_b, w_out=v_w_out, ffn2_norm=v_ffn2_norm,
                 ffn2_w_gate=v_ffn2_w_gate, ffn2_w_up=v_ffn2_w_up, ffn2_w_down=v_ffn2_w_down,
                 final_norm=v_final_norm)
    order = list(wts)
    nl = ada_w.shape[0]
    seq = x.shape[1]
    me = 4 * lax.axis_index("x") + 2 * lax.axis_index("y") + lax.axis_index("c")
    ada_cols = ada_w.shape[2]

    def after_token(t, token):
        return t + token[0:1, 0:1].astype(t.dtype)

    packs = [_pack_shard_layer(l, wts) for l in range(nl)]

    c_all = _all_gather(jnp.broadcast_to(c, (8, D_MODEL)), "gather_c")[::8]

    ada_b_mine = lax.dynamic_slice_in_dim(ada_b, me * ada_cols, ada_cols, axis=1).reshape(nl, 1, ada_cols)
    mod_part = _ada_mod(c_all, ada_w, ada_b_mine, "ada_mod")
    mod_all = _all_gather(mod_part.reshape(nl * N_DEV, ada_cols), "gather_mod")
    mod_all = mod_all.reshape(N_DEV, nl, N_DEV, ada_cols)
    mod = lax.dynamic_index_in_dim(mod_all, me, axis=2, keepdims=False)
    mod = mod.transpose(1, 0, 2).reshape(nl, N_MOD, 1, D_MODEL)

    flight_a = _gather_start(packs[0][:SPLIT_AB], ROWS_A, me, mod, "gather_start_0a")
    flight_b = _gather_start(packs[0][SPLIT_AB:], ROWS_B, me, flight_a[4], "gather_start_0b")
    last_start = flight_b[4]
    if nl > 1:
        in_flight = _gather_start(packs[1], ROWS_ALL, me, last_start, "gather_start_1")
        last_start = in_flight[4]

    cos, sin = _rope_tables(after_token(positions.reshape(seq, 1), last_start), "rope_tables")

    def vec(t):
        return t.reshape(1, -1)

    def landed(flight, rows_list, after, tag):
        send_sems, recv_sems, pk, lands, _ = flight
        pk, lands = _gather_wait(send_sems, recv_sems, pk, lands, after, f"gather_wait_{tag}")
        return _gather_finish(pk, rows_list, lands, "gather_finish")

    xs = x.reshape(seq, D_MODEL)
    saved = []
    for l in range(nl):
        norm1 = vec(ffn1_norm[l])
        if l == 0:
            lands = landed(flight_a, ROWS_A, cos, "0a")
        elif l + 1 < nl:
            in_flight = _gather_start(packs[l + 1], ROWS_ALL, me, lands[0], f"gather_start_{l + 1}")
            norm1 = after_token(norm1, in_flight[4])
        sv = {}

        def ffn_fwd(xin, norm, k0, wg, wu, wd, tag):
            h, a, b, t = _ffn_up(xin, norm, mod[l, k0], mod[l, k0 + 1], wg, wu, "ffn_up")
            y, xout = _mm(t, wd, "nn", "ffn_down", res=xin, gate=mod[l, k0 + 2], gate_factor=0.5)
            sv[tag] = dict(x=xin, h=h, a=a, b=b, t=t, y=y)
            return xout

        xs = ffn_fwd(xs, norm1, 0, lands[0], lands[1], lands[2], "f1")
        if l == 0:
            lands = lands + landed(flight_b, ROWS_B, xs, "0b")
        w = _full_weights(lands)
        sv["w"] = w

        h2 = _rm_fwd(xs, vec(mix_norm[l]), mod[l, 3], mod[l, 4], "rm_fwd")
        z = _mm(h2, w["in"], "nt", "mix_in")
        y_pool, diff = _pool_fwd(z, pool_w[l], vec(pool_scale[l]), "pool_fwd")
        q, k, v, cqn, ckvn = _qkv_fwd(z, vec(q_a_norm[l]), vec(kv_a_norm[l]), w["q"], w["kv"], cos, sin, "qkv_fwd")
        o, lse = _attn_fwd(q, k, v, "attn_fwd")
        ycat = jnp.concatenate([y_pool, o.astype(BF16)], axis=1)
        y2, xmix = _mm(ycat, w["out"], "nn", "mix_out", res=xs, gate=mod[l, 5], gate_factor=1.0)
        sv["mix"] = dict(x=xs, h=h2, z=z, diff=diff, q=q, k=k, v=v, cqn=cqn, ckvn=ckvn, lse=lse, ycat=ycat, y=y2)
        xs = xmix

        xs = ffn_fwd(xs, vec(ffn2_norm[l]), 6, w["g2"], w["u2"], w["d2"], "f2")
        saved.append(sv)
        if l + 1 < nl:
            lands = landed(in_flight, ROWS_ALL, xs, l + 1)

    loss_part, dx, d_final = _loss_head(xs, vec(final_norm), loss_target.reshape(seq, D_MODEL), "loss_head")

    small = {name: [None] * nl for name in ("ffn1_norm", "mix_norm", "ffn2_norm", "q_a_norm", "kv_a_norm",
                                            "pool_scale", "pool_w", "dmod")}
    exchanges = [None] * nl
    last_token = None
    for l in reversed(range(nl)):
        sv = saved[l]
        w = sv["w"]
        dmod = [None] * N_MOD
        gr = {}
        gate3 = mod[l, 8] if last_token is None else after_token(mod[l, 8], last_token)

        def ffn_bwd(dxin, s_, norm, k0, wg, wu, wd, tag, gate):
            dy, dmod[k0 + 2] = _gate_bwd(dxin, s_["y"], gate, 0.5, "gate_bwd")
            da, db, gr["d" + tag], gr["g" + tag], gr["u" + tag] = _ffn_bwd_cols(
                dy, s_["h"], s_["a"], s_["b"], s_["t"], wd, "ffn_bwd_cols")
            dh = _mm_pair(da, wg, db, wu, "ffn_bwd_dh")
            dxo, dmod[k0], dmod[k0 + 1], dnorm = _rm_bwd(dh, s_["x"], dxin, vec(norm), mod[l, k0 + 1], "rm_bwd")
            return dxo, dnorm

        dx, small["ffn2_norm"][l] = ffn_bwd(dx, sv["f2"], ffn2_norm[l], 6, w["g2"], w["u2"], w["d2"], "2", gate3)

        s_ = sv["mix"]
        dy, dmod[5] = _gate_bwd(dx, s_["y"], mod[l, 5], 1.0, "gate_bwd")
        gr["out"] = _mm(s_["ycat"], dy, "tn", "mix_out_dw", out_dtype=BF16, tm=256)
        dycat = _mm(dy, w["out"], "nt", "mix_out_dx")
        du, small["pool_w"][l], small["pool_scale"][l] = _pool_bwd(dycat, s_["diff"], pool_w[l], vec(pool_scale[l]), "pool_bwd")
        dq, dk, dv = _attn_bwd(s_["q"], s_["k"], s_["v"], s_["lse"], dycat, "attn_bwd")
        dz, dqb, dkvb, small["q_a_norm"][l], small["kv_a_norm"][l] = _qkv_bwd(
            dq, dk, dv, du, s_["z"], vec(q_a_norm[l]), vec(kv_a_norm[l]), w["q"], w["kv"], cos, sin, "qkv_bwd")
        gr["q"] = _mm(dqb, s_["cqn"], "tn", "q_b_dw", out_dtype=BF16, tm=256)
        gr["kv"] = _mm(dkvb, s_["ckvn"], "tn", "kv_b_dw", out_dtype=BF16, tm=256)
        gr["in"] = _mm(dz, s_["h"], "tn", "mix_in_dw", out_dtype=BF16, tm=256)
        dh2 = _mm(dz, w["in"], "nn", "mix_in_dx")
        dx, dmod[3], dmod[4], small["mix_norm"][l] = _rm_bwd(dh2, s_["x"], dx, vec(mix_norm[l]), mod[l, 4], "rm_bwd")

        gate1 = mod[l, 2]
        if l == 0:
            exchange_0b = _exchange_start(_grad_sources_b(gr), ROWS_B, me, dx, "exchange_start_0b")
            gate1 = after_token(gate1, exchange_0b[4])
        dx, small["ffn1_norm"][l] = ffn_bwd(dx, sv["f1"], ffn1_norm[l], 0, w["g1"], w["u1"], w["d1"], "1", gate1)

        small["dmod"][l] = jnp.concatenate(dmod, axis=0)
        if l > 0:
            srcs = [gr["g1"], gr["u1"], gr["d1"]] + _grad_sources_b(gr)
            exchanges[l] = _exchange_start(srcs, ROWS_ALL, me, dx, f"exchange_start_{l}")
            last_token = exchanges[l][4]

    grad_x = dx.reshape(x.shape)

    layout, small_rows = _small_layout(nl)
    pieces = {
        "dmod": jnp.concatenate(small["dmod"], axis=0),
        "ffn1_norm": jnp.concatenate(small["ffn1_norm"], axis=0),
        "mix_norm": jnp.concatenate(small["mix_norm"], axis=0),
        "ffn2_norm": jnp.concatenate(small["ffn2_norm"], axis=0),
        "q_a_norm": jnp.concatenate(small["q_a_norm"], axis=0),
        "kv_a_norm": jnp.concatenate(small["kv_a_norm"], axis=0),
        "pool_scale": jnp.concatenate(small["pool_scale"], axis=0),
        "final_norm": d_final,
        "loss": jnp.broadcast_to(loss_part[0:1, 0:1], (1, D_MODEL)),
        "pool_w": jnp.stack(small["pool_w"]).reshape(-1, D_MODEL),
    }
    small_buf = jnp.concatenate([_to_rows(pieces[name]) for name in layout], axis=0)
    if last_token is not None:
        small_buf = after_token(small_buf, last_token)
    small_all = _all_gather(small_buf, "gather_small").reshape(N_DEV, small_rows, D_MODEL)
    exchange_0a = _exchange_start([gr["g1"], gr["u1"], gr["d1"]], ROWS_A, me, small_all, "exchange_start_0a")
    small_sum = _sum_slots(small_all, "sum_small", after=exchange_0a[4])

    def take(name, width=D_MODEL):
        off, n = layout[name]
        return small_sum[off:off + n, :width]

    grads = {}
    grads["ada_b"] = take("dmod").reshape(nl, N_MOD * D_MODEL)
    grads["ffn1_norm"], grads["mix_norm"], grads["ffn2_norm"] = take("ffn1_norm"), take("mix_norm"), take("ffn2_norm")
    grads["q_a_norm"], grads["kv_a_norm"] = take("q_a_norm", Q_LORA), take("kv_a_norm", KV_LORA)
    grads["pool_scale"] = take("pool_scale", POOL_WIDTH)
    grads["final_norm"] = take("final_norm").reshape(D_MODEL)
    grads["pool_w"] = take("pool_w").reshape(pool_w.shape)
    loss = take("loss")[0, 0]

    off, n = layout["dmod"]
    dmod_all = small_all[:, off:off + n].reshape(N_DEV, nl, N_MOD * D_MODEL)
    dmod_mine = lax.dynamic_slice_in_dim(dmod_all, me * ada_cols, ada_cols, axis=2)
    dmod_pad = jnp.pad(dmod_mine.transpose(1, 0, 2), ((0, 0), (0, LANE - N_DEV), (0, 0)))
    grads["ada_w"] = _ada_grad(jnp.pad(c_all, ((0, LANE - N_DEV), (0, 0))), dmod_pad, "ada_grad")

    updates = {name: _adamw_nd(wts[name], grads[name], mom_m[name], mom_v[name], "adamw") for name in grads}

    def summed(exchange, after, tag):
        send_sems, recv_sems, srcs, recv, _ = exchange
        recv = _exchange_wait(send_sems, recv_sems, srcs, recv, after, f"exchange_wait_{tag}")
        return _sum_slots(recv, "sum_grads")

    gshard = [None] * nl
    for l in reversed(range(1, nl)):
        gshard[l] = summed(exchanges[l], exchange_0a[4], l)
    gshard[0] = jnp.concatenate([summed(exchange_0a, updates["ada_w"][0], "0a"),
                                 summed(exchange_0b, updates["ada_w"][0], "0b")], axis=0)
    grads.update(_unpack_grad_shards(jnp.stack(gshard)))
    for name in order:
        if name not in updates:
            updates[name] = _adamw_nd(wts[name], grads[name], mom_m[name], mom_v[name], "adamw")

    return (loss, grad_x, *[grads[n] for n in order], *[updates[n][0] for n in order],
            *[updates[n][1] for n in order], *[updates[n][2] for n in order])
```

```python
import math

import numpy as np
import jax
import jax.numpy as jnp
from jax import lax
from jax.experimental import pallas as pl
from jax.experimental.pallas import tpu as pltpu

F32 = jnp.float32
BF16 = jnp.bfloat16

N_DEV = 8
D_MODEL = 1024
D_FF = 2816
POOL_WIDTH = 512
POOL_WINDOWS = (2, 4, 8, 16)
POOL_GC = 128
N_HEADS = 4
QK_NOPE = 128
QK_ROPE = 64
V_HEAD = 128
QK_HEAD = QK_NOPE + QK_ROPE
HEAD_PAD = 256
Q_LORA = 384
KV_LORA = 256
IN_COLS = POOL_WIDTH + Q_LORA + KV_LORA + QK_ROPE
IN_PAD = 1280
ROPE_THETA = 10000.0
SOFTMAX_SCALE = 1.0 / math.sqrt(QK_HEAD)
EPS = 1e-6
N_MOD = 9

ADAM_LR = 0.001
ADAM_B1 = 0.9
ADAM_B2 = 0.999
ADAM_EPS = 1e-08
ADAM_WD = 0.01
ADAM_STEP = 10

LANE = 128
VMEM_LIMIT = 56 * 1024 * 1024

FF_SH = D_FF // N_DEV
OFF_G1, OFF_U1, OFF_D1 = 0, FF_SH, 2 * FF_SH
OFF_G2, OFF_U2, OFF_D2 = 3 * FF_SH, 4 * FF_SH, 5 * FF_SH
OFF_OUT = 6 * FF_SH
OFF_KV = OFF_OUT + 128
OFF_IN = OFF_KV + 32
OFF_Q = OFF_IN + 160
Q_PAD_ROWS = 64
ROWS_L = OFF_Q + Q_PAD_ROWS
IN_SH = IN_COLS // N_DEV
Q_SH_ROWS = (N_HEADS * QK_HEAD // N_DEV) * Q_LORA // D_MODEL
KV_SH_ROWS = (N_HEADS * (QK_NOPE + V_HEAD) // N_DEV) * KV_LORA // D_MODEL


def _tile(dim, target):
    if dim <= target:
        return dim
    best = None
    for t in range(LANE, target + 1, LANE):
        if dim % t == 0:
            best = t
    assert best is not None, (dim, target)
    return best


def _params(sem):
    return pltpu.CompilerParams(dimension_semantics=sem, vmem_limit_bytes=VMEM_LIMIT)


def _mesh_pos():
    return lax.axis_index("x"), lax.axis_index("y"), lax.axis_index("c")


def _all_gather(x, name):
    m, n = x.shape

    def body(x_ref, out_ref, send_sems, recv_sems, local_sem):
        px, py, pc = _mesh_pos()
        me, sibling = (px, py, pc), (px, py, 1 - pc)
        chips = [(1 - px, py), (px, 1 - py), (1 - px, 1 - py)]

        def rows(bx, by, bc):
            return out_ref.at[pl.ds((4 * bx + 2 * by + bc) * m, m), :]

        def copy(k, block, to, src=None):
            return pltpu.make_async_remote_copy(
                src_ref=rows(*block) if src is None else src, dst_ref=rows(*block),
                send_sem=send_sems.at[k], recv_sem=recv_sems.at[k],
                device_id=to, device_id_type=pl.DeviceIdType.MESH)

        mine = pltpu.make_async_copy(x_ref, rows(*me), local_sem)
        mine.start()
        first = [copy(0, me, sibling, src=x_ref)]
        first += [copy(1 + j, me, (*chip, pc), src=x_ref) for j, chip in enumerate(chips)]
        for cp in first:
            cp.start()
        passed = [copy(4 + j, (*chip, pc), sibling) for j, chip in enumerate(chips)]
        for j, chip in enumerate(chips):
            copy(1 + j, (*chip, pc), me).wait_recv()
            passed[j].start()
        copy(0, sibling, me).wait_recv()
        for j, chip in enumerate(chips):
            copy(4 + j, (*chip, 1 - pc), me).wait_recv()
        for cp in first + passed:
            cp.wait_send()
        mine.wait()

    return pl.pallas_call(
        body, name=name,
        out_shape=jax.ShapeDtypeStruct((N_DEV * m, n), x.dtype),
        in_specs=[pl.BlockSpec(memory_space=pltpu.HBM)],
        out_specs=pl.BlockSpec(memory_space=pltpu.HBM),
        scratch_shapes=[pltpu.SemaphoreType.DMA((7,)), pltpu.SemaphoreType.DMA((7,)),
                        pltpu.SemaphoreType.DMA],
    )(x)


SMALL_ROWS = ROWS_L - OFF_KV
ROWS_A = [FF_SH] * 3
ROWS_B = [FF_SH] * 3 + [128, SMALL_ROWS]
ROWS_ALL = ROWS_A + ROWS_B
SPLIT_AB = sum(ROWS_A)
HBM_SPEC = pl.BlockSpec(memory_space=pltpu.HBM)
SEM_SPEC = pl.BlockSpec(memory_space=pltpu.SEMAPHORE)
ANY_SPEC = pl.BlockSpec(memory_space=pl.ANY)
EFFECT = pltpu.SideEffectType.DATAFLOW_SIDE_EFFECTING


def _hbm(t):
    return pltpu.with_memory_space_constraint(t, pltpu.HBM)


def _whole_wait(ref, send_sem, recv_sem, peer):
    return pltpu.make_async_remote_copy(src_ref=ref, dst_ref=ref, send_sem=send_sem, recv_sem=recv_sem,
                                        device_id=peer, device_id_type=pl.DeviceIdType.MESH)


def _offsets(rows_list):
    return [sum(rows_list[:i]) for i in range(len(rows_list))]


def _gather_start(packed, rows_list, after, name):
    n = len(rows_list)
    offs = _offsets(rows_list)
    lands = [_hbm(lax.empty((N_DEV * rows, D_MODEL), BF16)) for rows in rows_list]

    def body(packed_ref, *refs):
        land = refs[:n]
        send_sems, recv_sems = refs[n + 1], refs[n + 2]
        token = refs[-1]
        px, py, pc = _mesh_pos()
        me = 4 * px + 2 * py + pc
        peers = [(px, py, 1 - pc), (1 - px, py, pc), (px, 1 - py, pc), (1 - px, 1 - py, pc)]
        for k, peer in enumerate(peers):
            for off, rows, land_ref in zip(offs, rows_list, land):
                pltpu.make_async_remote_copy(
                    src_ref=packed_ref.at[pl.ds(off, rows), :], dst_ref=land_ref.at[pl.ds(me * rows, rows), :],
                    send_sem=send_sems.at[k], recv_sem=recv_sems.at[k],
                    device_id=peer, device_id_type=pl.DeviceIdType.MESH).start()
        token[...] = jnp.zeros_like(token)

    outs = pl.pallas_call(
        body, name=name,
        out_shape=(pltpu.SemaphoreType.DMA((4,)), pltpu.SemaphoreType.DMA((4,)), pltpu.HBM(packed.shape, BF16),
                   *[pltpu.HBM(t.shape, BF16) for t in lands], jax.ShapeDtypeStruct((8, LANE), F32)),
        in_specs=(HBM_SPEC,) * (1 + n) + (ANY_SPEC,),
        out_specs=(SEM_SPEC, SEM_SPEC) + (HBM_SPEC,) * (1 + n) + (pl.BlockSpec(memory_space=pltpu.VMEM),),
        input_output_aliases={i: 2 + i for i in range(1 + n)},
        compiler_params=pltpu.CompilerParams(has_side_effects=EFFECT),
    )(_hbm(packed), *lands, after)
    return outs[0], outs[1], outs[2], list(outs[3:3 + n]), outs[-1]


def _gather_wait(send_sems, recv_sems, packed, lands, after, name):
    n = len(lands)

    def body(packed_ref, *refs):
        s_sems, r_sems = refs[n], refs[n + 1]
        me = _mesh_pos()
        for k in range(4):
            cp = _whole_wait(packed_ref, s_sems.at[k], r_sems.at[k], me)
            cp.wait_send()
            cp.wait_recv()

    outs = pl.pallas_call(
        body, name=name,
        out_shape=(pltpu.HBM(packed.shape, BF16), *[pltpu.HBM(t.shape, BF16) for t in lands]),
        in_specs=(HBM_SPEC,) * (1 + n) + (SEM_SPEC, SEM_SPEC, ANY_SPEC),
        out_specs=(HBM_SPEC,) * (1 + n),
        input_output_aliases={i: i for i in range(1 + n)},
        compiler_params=pltpu.CompilerParams(has_side_effects=EFFECT),
    )(packed, *lands, send_sems, recv_sems, after)
    return outs[0], list(outs[1:])


def _gather_finish(packed, rows_list, lands, name):
    n = len(rows_list)
    offs = _offsets(rows_list)

    def body(packed_ref, *refs):
        land = refs[n:2 * n]
        send_sems, recv_sems, stage, stage_sem = refs[2 * n:]
        px, py, pc = _mesh_pos()
        me = 4 * px + 2 * py + pc
        sibling = (px, py, 1 - pc)
        load = pltpu.make_async_copy(packed_ref, stage, stage_sem)
        load.start()
        load.wait()
        for off, rows, land_ref in zip(offs, rows_list, land):
            pltpu.make_async_copy(stage.at[pl.ds(off, rows), :], land_ref.at[pl.ds(me * rows, rows), :],
                                  stage_sem).start()
        for j, (cx, cy) in enumerate([(1 - px, py), (px, 1 - py), (1 - px, 1 - py)]):
            block = 4 * cx + 2 * cy + pc
            for rows, land_ref in zip(rows_list, land):
                blk = land_ref.at[pl.ds(block * rows, rows), :]
                pltpu.make_async_remote_copy(src_ref=blk, dst_ref=blk, send_sem=send_sems.at[j],
                                             recv_sem=recv_sems.at[j], device_id=sibling,
                                             device_id_type=pl.DeviceIdType.MESH).start()
        for j in range(3):
            cp = _whole_wait(packed_ref, send_sems.at[j], recv_sems.at[j], sibling)
            cp.wait_recv()
            cp.wait_send()
        pltpu.make_async_copy(stage, packed_ref, stage_sem).wait()

    outs = pl.pallas_call(
        body, name=name,
        out_shape=tuple(jax.ShapeDtypeStruct(t.shape, BF16) for t in lands),
        in_specs=(HBM_SPEC,) * (1 + n), out_specs=(HBM_SPEC,) * n,
        input_output_aliases={1 + i: i for i in range(n)},
        scratch_shapes=[pltpu.SemaphoreType.DMA((3,)), pltpu.SemaphoreType.DMA((3,)),
                        pltpu.VMEM(packed.shape, BF16), pltpu.SemaphoreType.DMA],
    )(packed, *lands)
    return list(outs)


def _exchange_start(srcs, rows_list, me_id, after, name):
    n = len(rows_list)
    offs = _offsets(rows_list)
    own = jnp.concatenate([lax.dynamic_slice_in_dim(t, me_id * rows, rows, axis=0)
                           for t, rows in zip(srcs, rows_list)], axis=0)
    recv = lax.dynamic_update_slice_in_dim(lax.empty((N_DEV, sum(rows_list), D_MODEL), BF16), own[None], me_id, axis=0)

    def body(*refs):
        src, recv_ref = refs[:n], refs[n]
        send_sems, recv_sems = refs[n + 2], refs[n + 3]
        token = refs[-1]
        px, py, pc = _mesh_pos()
        me = 4 * px + 2 * py + pc
        for k in range(1, N_DEV):
            qx = 1 - px if k & 4 else px
            qy = 1 - py if k & 2 else py
            qc = 1 - pc if k & 1 else pc
            peer_id = 4 * qx + 2 * qy + qc
            for off, rows, src_ref in zip(offs, rows_list, src):
                pltpu.make_async_remote_copy(
                    src_ref=src_ref.at[pl.ds(peer_id * rows, rows), :], dst_ref=recv_ref.at[me, pl.ds(off, rows), :],
                    send_sem=send_sems.at[k - 1], recv_sem=recv_sems.at[k - 1],
                    device_id=(qx, qy, qc), device_id_type=pl.DeviceIdType.MESH).start()
        token[...] = jnp.zeros_like(token)

    outs = pl.pallas_call(
        body, name=name,
        out_shape=(pltpu.SemaphoreType.DMA((N_DEV - 1,)), pltpu.SemaphoreType.DMA((N_DEV - 1,)),
                   *[pltpu.HBM(t.shape, BF16) for t in srcs], pltpu.HBM(recv.shape, BF16),
                   jax.ShapeDtypeStruct((8, LANE), F32)),
        in_specs=(HBM_SPEC,) * (n + 1) + (ANY_SPEC,),
        out_specs=(SEM_SPEC, SEM_SPEC) + (HBM_SPEC,) * (n + 1) + (pl.BlockSpec(memory_space=pltpu.VMEM),),
        input_output_aliases={i: 2 + i for i in range(n + 1)},
        compiler_params=pltpu.CompilerParams(has_side_effects=EFFECT),
    )(*[_hbm(t) for t in srcs], _hbm(recv), after)
    return outs[0], outs[1], list(outs[2:2 + n]), outs[2 + n], outs[-1]


def _exchange_wait(send_sems, recv_sems, srcs, recv, after, name):
    n = len(srcs)

    def body(*refs):
        recv_ref = refs[n]
        s_sems, r_sems = refs[n + 1], refs[n + 2]
        me = _mesh_pos()
        for k in range(N_DEV - 1):
            cp = _whole_wait(recv_ref.at[0], s_sems.at[k], r_sems.at[k], me)
            cp.wait_send()
            cp.wait_recv()

    outs = pl.pallas_call(
        body, name=name,
        out_shape=(*[pltpu.HBM(t.shape, BF16) for t in srcs], pltpu.HBM(recv.shape, BF16)),
        in_specs=(HBM_SPEC,) * (n + 1) + (SEM_SPEC, SEM_SPEC, ANY_SPEC),
        out_specs=(HBM_SPEC,) * (n + 1),
        input_output_aliases={i: i for i in range(n + 1)},
        compiler_params=pltpu.CompilerParams(has_side_effects=EFFECT),
    )(*srcs, recv, send_sems, recv_sems, after)
    return outs[n]


def _sum_slots_into(recv, buf, layer, row_off, name):
    _, r, n = recv.shape
    tr = _row_tile(math.gcd(r, row_off) if row_off else r, 512)
    first = row_off // tr

    def body(in_ref, buf_ref, out_ref):
        acc = in_ref[0].astype(F32)
        for j in range(1, N_DEV):
            acc = acc + in_ref[j].astype(F32)
        out_ref[...] = acc

    return pl.pallas_call(
        body, name=name, grid=(r // tr,), out_shape=jax.ShapeDtypeStruct(buf.shape, F32),
        in_specs=[pl.BlockSpec((N_DEV, tr, n), lambda i: (0, i, 0)), ANY_SPEC],
        out_specs=pl.BlockSpec((None, tr, n), lambda i: (layer, first + i, 0)),
        input_output_aliases={1: 0},
        compiler_params=_params(("parallel",)),
    )(recv, buf)


def _sum_slots(recv, name, after=None):
    _, r, n = recv.shape
    tr = _row_tile(r, 512)

    def body(in_ref, *refs):
        acc = in_ref[0].astype(F32)
        for j in range(1, N_DEV):
            acc = acc + in_ref[j].astype(F32)
        refs[-1][...] = acc

    grid = (r // tr,)
    in_specs, out_spec = [pl.BlockSpec((N_DEV, tr, n), lambda i: (0, i, 0))], pl.BlockSpec((tr, n), lambda i: (i, 0))
    args = [recv]
    if after is not None:
        in_specs.append(ANY_SPEC)
        args.append(after)
    return pl.pallas_call(
        body, name=name, grid=grid,
        out_shape=jax.ShapeDtypeStruct((r, n), F32),
        in_specs=in_specs, out_specs=out_spec,
        compiler_params=_params(("parallel",)),
    )(*args)


def _row_tile(rows, target):
    if rows <= target:
        return rows
    best = None
    for t in range(16, target + 1, 16):
        if rows % t == 0:
            best = t
    assert best is not None, rows
    return best


_DIMS = {"nn": ((1,), (0,)), "nt": ((1,), (1,)), "tn": ((0,), (0,))}


def _mm(a, b, mode, name, out_dtype=F32, res=None, gate=None, gate_factor=1.0, tm=512, tn=1408):
    if mode == "tn":
        kdim, m = a.shape
    else:
        m, kdim = a.shape
    n = b.shape[0] if mode == "nt" else b.shape[1]
    tm, tn = _tile(m, tm), _tile(n, tn)
    a_spec = (pl.BlockSpec((kdim, tm), lambda i, j: (0, i)) if mode == "tn"
              else pl.BlockSpec((tm, kdim), lambda i, j: (i, 0)))
    b_spec = (pl.BlockSpec((tn, kdim), lambda i, j: (j, 0)) if mode == "nt"
              else pl.BlockSpec((kdim, tn), lambda i, j: (0, j)))
    o_spec = pl.BlockSpec((tm, tn), lambda i, j: (i, j))
    dims = (_DIMS[mode], ((), ()))
    has_res, has_gate = res is not None, gate is not None

    def body(*refs):
        a_ref, b_ref = refs[0], refs[1]
        y = lax.dot_general(a_ref[...].astype(BF16), b_ref[...].astype(BF16), dims,
                            preferred_element_type=F32)
        if not has_res:
            refs[2][...] = y.astype(out_dtype)
            return
        res_ref = refs[2]
        if has_gate:
            y_ref, o_ref = refs[4], refs[5]
            y_ref[...] = y
            o_ref[...] = res_ref[...] + (gate_factor * refs[3][...]) * y
        else:
            refs[3][...] = res_ref[...] + y

    in_specs, args = [a_spec, b_spec], [a, b]
    if has_res:
        in_specs.append(o_spec)
        args.append(res)
        if has_gate:
            in_specs.append(pl.BlockSpec((1, tn), lambda i, j: (0, j)))
            args.append(gate)
            out_shape = (jax.ShapeDtypeStruct((m, n), F32), jax.ShapeDtypeStruct((m, n), F32))
            out_specs = (o_spec, o_spec)
        else:
            out_shape, out_specs = jax.ShapeDtypeStruct((m, n), F32), o_spec
    else:
        out_shape, out_specs = jax.ShapeDtypeStruct((m, n), out_dtype), o_spec
    return pl.pallas_call(
        body, name=name, grid=(m // tm, n // tn), out_shape=out_shape,
        in_specs=in_specs, out_specs=out_specs,
        compiler_params=_params(("parallel", "parallel")),
    )(*args)


def _vec_spec(width):
    return pl.BlockSpec((1, width), lambda i: (0, 0))


def _rm_fwd(x, gw, shift, scale, name):
    s, d = x.shape
    ts = _tile(s, 256)

    def body(x_ref, gw_ref, sh_ref, sc_ref, h_ref):
        xv = x_ref[...]
        r = lax.rsqrt(jnp.mean(xv * xv, axis=-1, keepdims=True) + EPS)
        y = (xv * r) * gw_ref[...]
        h_ref[...] = (y * (1 + sc_ref[...]) + sh_ref[...]).astype(BF16)

    row = pl.BlockSpec((ts, d), lambda i: (i, 0))
    return pl.pallas_call(
        body, name=name, grid=(s // ts,), out_shape=jax.ShapeDtypeStruct((s, d), BF16),
        in_specs=[row, _vec_spec(d), _vec_spec(d), _vec_spec(d)], out_specs=row,
        compiler_params=_params(("parallel",)),
    )(x, gw, shift, scale)


def _rm_bwd(dh, x, dres, gw, scale, name):
    s, d = x.shape
    ts = _tile(s, 256)

    def body(dh_ref, x_ref, dres_ref, gw_ref, sc_ref, dx_ref, dsh_ref, dsc_ref, dgw_ref):
        @pl.when(pl.program_id(0) == 0)
        def _():
            dsh_ref[...] = jnp.zeros_like(dsh_ref)
            dsc_ref[...] = jnp.zeros_like(dsc_ref)
            dgw_ref[...] = jnp.zeros_like(dgw_ref)

        xv, dhv, gwv = x_ref[...], dh_ref[...], gw_ref[...]
        r = lax.rsqrt(jnp.mean(xv * xv, axis=-1, keepdims=True) + EPS)
        xn = xv * r
        y = xn * gwv
        dsh_ref[...] += jnp.sum(dhv, axis=0, keepdims=True)
        dsc_ref[...] += jnp.sum(dhv * y, axis=0, keepdims=True)
        dy = dhv * (1 + sc_ref[...])
        dgw_ref[...] += jnp.sum(dy * xn, axis=0, keepdims=True)
        dxn = dy * gwv
        dx = r * (dxn - xn * jnp.mean(dxn * xn, axis=-1, keepdims=True))
        dx_ref[...] = dres_ref[...] + dx

    row = pl.BlockSpec((ts, d), lambda i: (i, 0))
    vec = jax.ShapeDtypeStruct((1, d), F32)
    return pl.pallas_call(
        body, name=name, grid=(s // ts,),
        out_shape=(jax.ShapeDtypeStruct((s, d), F32), vec, vec, vec),
        in_specs=[row, row, row, _vec_spec(d), _vec_spec(d)],
        out_specs=(row, _vec_spec(d), _vec_spec(d), _vec_spec(d)),
        compiler_params=_params(("arbitrary",)),
    )(dh, x, dres, gw, scale)


def _gate_bwd(dx, y, gate, factor, name):
    s, d = dx.shape
    ts = _tile(s, 256)

    def body(dx_ref, y_ref, g_ref, dy_ref, dg_ref):
        @pl.when(pl.program_id(0) == 0)
        def _():
            dg_ref[...] = jnp.zeros_like(dg_ref)

        dxv = dx_ref[...]
        dy_ref[...] = ((factor * g_ref[...]) * dxv).astype(BF16)
        dg_ref[...] += jnp.sum((factor * dxv) * y_ref[...], axis=0, keepdims=True)

    row = pl.BlockSpec((ts, d), lambda i: (i, 0))
    return pl.pallas_call(
        body, name=name, grid=(s // ts,),
        out_shape=(jax.ShapeDtypeStruct((s, d), BF16), jax.ShapeDtypeStruct((1, d), F32)),
        in_specs=[row, row, _vec_spec(d)], out_specs=(row, _vec_spec(d)),
        compiler_params=_params(("arbitrary",)),
    )(dx, y, gate)


FFN_TM, FFN_TF = 1024, 256


def _ffn_up(x, gw, shift, scale, wg, wu, name):
    s, d = x.shape
    f = wg.shape[0]
    tm, tf = _tile(s, FFN_TM), _tile(f, FFN_TF)
    nt = (((1,), (1,)), ((), ()))

    def body(x_ref, gw_ref, sh_ref, sc_ref, wg_ref, wu_ref, h_ref, a_ref, b_ref, t_ref):
        @pl.when(pl.program_id(1) == 0)
        def _():
            xv = x_ref[...]
            r = lax.rsqrt(jnp.mean(xv * xv, axis=-1, keepdims=True) + EPS)
            h_ref[...] = (((xv * r) * gw_ref[...]) * (1 + sc_ref[...]) + sh_ref[...]).astype(BF16)

        hb = h_ref[...]
        av = lax.dot_general(hb, wg_ref[...], nt, preferred_element_type=F32)
        bv = lax.dot_general(hb, wu_ref[...], nt, preferred_element_type=F32)
        a_ref[...] = av.astype(BF16)
        b_ref[...] = bv.astype(BF16)
        t_ref[...] = ((av * jax.nn.sigmoid(av)) * bv).astype(BF16)

    row = pl.BlockSpec((tm, d), lambda i, j: (i, 0))
    vec = pl.BlockSpec((1, d), lambda i, j: (0, 0))
    wblk = pl.BlockSpec((tf, d), lambda i, j: (j, 0))
    blk = pl.BlockSpec((tm, tf), lambda i, j: (i, j))
    wide = jax.ShapeDtypeStruct((s, f), BF16)
    return pl.pallas_call(
        body, name=name, grid=(s // tm, f // tf),
        out_shape=(jax.ShapeDtypeStruct((s, d), BF16), wide, wide, wide),
        in_specs=[row, vec, vec, vec, wblk, wblk], out_specs=(row, blk, blk, blk),
        compiler_params=_params(("parallel", "arbitrary")),
    )(x, gw, shift, scale, wg, wu)


def _ffn_bwd_cols(dy, h, a, b, t, wd, name):
    s, d = dy.shape
    f = wd.shape[0]
    tf = _tile(f, FFN_TF)
    nt = (((1,), (1,)), ((), ()))
    tn = (((0,), (0,)), ((), ()))

    def body(dy_ref, h_ref, a_ref, b_ref, t_ref, wd_ref, da_ref, db_ref, gd_ref, gg_ref, gu_ref):
        dyb, hb = dy_ref[...], h_ref[...]
        dtv = lax.dot_general(dyb, wd_ref[...], nt, preferred_element_type=F32)
        av, bv = a_ref[...].astype(F32), b_ref[...].astype(F32)
        sg = jax.nn.sigmoid(av)
        dbv = (dtv * (av * sg)).astype(BF16)
        dav = ((dtv * bv) * (sg * (1 + av * (1 - sg)))).astype(BF16)
        da_ref[...] = dav
        db_ref[...] = dbv
        gd_ref[...] = lax.dot_general(t_ref[...], dyb, tn, preferred_element_type=F32).astype(BF16)
        gg_ref[...] = lax.dot_general(dav, hb, tn, preferred_element_type=F32).astype(BF16)
        gu_ref[...] = lax.dot_general(dbv, hb, tn, preferred_element_type=F32).astype(BF16)

    whole = pl.BlockSpec((s, d), lambda j: (0, 0))
    col = pl.BlockSpec((s, tf), lambda j: (0, j))
    wblk = pl.BlockSpec((tf, d), lambda j: (j, 0))
    wide, wgrad = jax.ShapeDtypeStruct((s, f), BF16), jax.ShapeDtypeStruct((f, d), BF16)
    return pl.pallas_call(
        body, name=name, grid=(f // tf,), out_shape=(wide, wide, wgrad, wgrad, wgrad),
        in_specs=[whole, whole, col, col, col, wblk], out_specs=(col, col, wblk, wblk, wblk),
        compiler_params=_params(("parallel",)),
    )(dy, h, a, b, t, wd)


def _mm_pair(a1, b1, a2, b2, name, tm=512, tn=512):
    m, kdim = a1.shape
    n = b1.shape[1]
    tm, tn = _tile(m, tm), _tile(n, tn)

    def body(a1_ref, b1_ref, a2_ref, b2_ref, o_ref):
        o_ref[...] = (jnp.dot(a1_ref[...], b1_ref[...], preferred_element_type=F32)
                      + jnp.dot(a2_ref[...], b2_ref[...], preferred_element_type=F32))

    a_spec = pl.BlockSpec((tm, kdim), lambda i, j: (i, 0))
    b_spec = pl.BlockSpec((kdim, tn), lambda i, j: (0, j))
    return pl.pallas_call(
        body, name=name, grid=(m // tm, n // tn), out_shape=jax.ShapeDtypeStruct((m, n), F32),
        in_specs=[a_spec, b_spec, a_spec, b_spec], out_specs=pl.BlockSpec((tm, tn), lambda i, j: (i, j)),
        compiler_params=_params(("parallel", "parallel")),
    )(a1, b1, a2, b2)


def _pool_counts(s):
    return (lax.broadcasted_iota(jnp.int32, (s, POOL_GC), 0))


def _pool_fwd(z, pool_w, pool_scale, name):
    s = z.shape[0]

    def body(u_ref, w_ref, sc_ref, y_ref, diff_ref):
        t = lax.broadcasted_iota(jnp.int32, (s, POOL_GC), 0)
        for g, win in enumerate(POOL_WINDOWS):
            cols = slice(g * POOL_GC, (g + 1) * POOL_GC)
            u = u_ref[:, cols]
            acc, step = u, 1
            while step < win:
                acc = acc + jnp.where(t >= step, pltpu.roll(acc, step, 0), 0.0)
                step *= 2
            cnt = jnp.minimum(t + 1, win).astype(F32)
            diff = acc / cnt - u
            diff_ref[:, cols] = diff
            ypre = jnp.dot(diff.astype(BF16), w_ref[g].astype(BF16), preferred_element_type=F32)
            y_ref[:, cols] = (ypre * sc_ref[:, cols]).astype(BF16)

    return pl.pallas_call(
        body, name=name, grid=(1,),
        out_shape=(jax.ShapeDtypeStruct((s, POOL_WIDTH), BF16), jax.ShapeDtypeStruct((s, POOL_WIDTH), F32)),
        in_specs=[pl.BlockSpec((s, POOL_WIDTH), lambda i: (0, 0)),
                  pl.BlockSpec(pool_w.shape, lambda i: (0, 0, 0)),
                  pl.BlockSpec((1, POOL_WIDTH), lambda i: (0, 0))],
        out_specs=(pl.BlockSpec((s, POOL_WIDTH), lambda i: (0, 0)),
                   pl.BlockSpec((s, POOL_WIDTH), lambda i: (0, 0))),
        compiler_params=_params(("arbitrary",)),
    )(z, pool_w, pool_scale)


def _pool_bwd(dycat, diff, pool_w, pool_scale, name):
    s = diff.shape[0]

    def body(dy_ref, diff_ref, w_ref, sc_ref, du_ref, dw_ref, dsc_ref):
        t = lax.broadcasted_iota(jnp.int32, (s, POOL_GC), 0)
        for g, win in enumerate(POOL_WINDOWS):
            cols = slice(g * POOL_GC, (g + 1) * POOL_GC)
            dy, dfb, wb = dy_ref[:, cols], diff_ref[:, cols].astype(BF16), w_ref[g].astype(BF16)
            ypre = jnp.dot(dfb, wb, preferred_element_type=F32)
            dsc_ref[:, cols] = jnp.sum(dy * ypre, axis=0, keepdims=True)
            dypre = (dy * sc_ref[:, cols]).astype(BF16)
            ddiff = lax.dot_general(dypre, wb, (((1,), (1,)), ((), ())), preferred_element_type=F32)
            dw_ref[g] = lax.dot_general(dfb, dypre, (((0,), (0,)), ((), ())), preferred_element_type=F32)
            cnt = jnp.minimum(t + 1, win).astype(F32)
            acc, step = ddiff / cnt, 1
            while step < win:
                acc = acc + jnp.where(t < s - step, pltpu.roll(acc, s - step, 0), 0.0)
                step *= 2
            du_ref[:, cols] = acc - ddiff

    full = pl.BlockSpec((s, POOL_WIDTH), lambda i: (0, 0))
    return pl.pallas_call(
        body, name=name, grid=(1,),
        out_shape=(jax.ShapeDtypeStruct((s, POOL_WIDTH), F32),
                   jax.ShapeDtypeStruct(pool_w.shape, F32),
                   jax.ShapeDtypeStruct((1, POOL_WIDTH), F32)),
        in_specs=[full, full, pl.BlockSpec(pool_w.shape, lambda i: (0, 0, 0)),
                  pl.BlockSpec((1, POOL_WIDTH), lambda i: (0, 0))],
        out_specs=(full, pl.BlockSpec(pool_w.shape, lambda i: (0, 0, 0)),
                   pl.BlockSpec((1, POOL_WIDTH), lambda i: (0, 0))),
        compiler_params=_params(("arbitrary",)),
    )(dycat, diff, pool_w, pool_scale)


def _rope_tables(positions, name):
    s = positions.shape[0]
    ts = _tile(s, 512)
    freq = 1.0 / (ROPE_THETA ** (np.arange(0, QK_ROPE, 2, dtype=np.float32) / QK_ROPE))
    table = np.zeros((1, LANE), np.float32)
    table[0, :QK_ROPE // 2] = freq
    table[0, QK_ROPE // 2:QK_ROPE] = freq

    def body(pos_ref, f_ref, cos_ref, sin_ref):
        ang = pos_ref[...].astype(F32) * f_ref[...]
        cos_ref[...] = jnp.cos(ang)
        sin_ref[...] = jnp.sin(ang)

    out = jax.ShapeDtypeStruct((s, LANE), F32)
    blk = pl.BlockSpec((ts, LANE), lambda i: (i, 0))
    return pl.pallas_call(
        body, name=name, grid=(s // ts,), out_shape=(out, out),
        in_specs=[pl.BlockSpec((ts, 1), lambda i: (i, 0)), _vec_spec(LANE)], out_specs=(blk, blk),
        compiler_params=_params(("parallel",)),
    )(positions, jnp.asarray(table))


def _lane_mod64_low(shape):
    return (lax.broadcasted_iota(jnp.int32, shape, 1) % QK_ROPE) < (QK_ROPE // 2)


def _rope(x, cos, sin):
    rot = jnp.where(_lane_mod64_low(x.shape), -pltpu.roll(x, LANE - 32, 1), pltpu.roll(x, 32, 1))
    return x * cos + rot * sin


def _rope_t(dy, cos, sin):
    w = dy * sin
    rot_t = jnp.where(_lane_mod64_low(dy.shape), pltpu.roll(w, LANE - 32, 1), -pltpu.roll(w, 32, 1))
    return dy * cos + rot_t


def _plain_rms(x, g):
    r = lax.rsqrt(jnp.mean(x * x, axis=-1, keepdims=True) + EPS)
    return (x * r) * g, x * r, r


O_Q, O_KV, O_KR = POOL_WIDTH, POOL_WIDTH + Q_LORA, POOL_WIDTH + Q_LORA + KV_LORA


def _qkv_fwd(z, qn, kvn, wq, wkv, cos, sin, name):
    s = z.shape[0]
    ts = _tile(s, 256)

    def body(z_ref, qn_ref, kvn_ref, wq_ref, wkv_ref, cos_ref, sin_ref, q_ref, k_ref, v_ref, cqn_ref, ckvn_ref):
        cosv, sinv = cos_ref[...], sin_ref[...]
        cqn = _plain_rms(z_ref[:, O_Q:O_KV], qn_ref[...])[0].astype(BF16)
        ckvn = _plain_rms(z_ref[:, O_KV:O_KR], kvn_ref[...])[0].astype(BF16)
        cqn_ref[...] = cqn
        ckvn_ref[...] = ckvn
        nt = (((1,), (1,)), ((), ()))
        q = lax.dot_general(cqn, wq_ref[...], nt, preferred_element_type=F32)
        kv = lax.dot_general(ckvn, wkv_ref[...], nt, preferred_element_type=F32)
        kr = _rope(z_ref[:, O_KR:IN_PAD], cosv, sinv).astype(BF16)
        for h in range(N_HEADS):
            o = h * HEAD_PAD
            q_ref[:, o:o + QK_NOPE] = q[:, o:o + QK_NOPE].astype(BF16)
            q_ref[:, o + QK_NOPE:o + HEAD_PAD] = _rope(q[:, o + QK_NOPE:o + HEAD_PAD], cosv, sinv).astype(BF16)
            k_ref[:, o:o + QK_NOPE] = kv[:, o:o + QK_NOPE].astype(BF16)
            k_ref[:, o + QK_NOPE:o + HEAD_PAD] = kr
            v_ref[:, h * V_HEAD:(h + 1) * V_HEAD] = kv[:, o + QK_NOPE:o + HEAD_PAD].astype(BF16)

    def row(w):
        return pl.BlockSpec((ts, w), lambda i: (i, 0))

    def whole(arr):
        return pl.BlockSpec(arr.shape, lambda i: (0, 0))

    hp = N_HEADS * HEAD_PAD
    return pl.pallas_call(
        body, name=name, grid=(s // ts,),
        out_shape=(jax.ShapeDtypeStruct((s, hp), BF16), jax.ShapeDtypeStruct((s, hp), BF16),
                   jax.ShapeDtypeStruct((s, N_HEADS * V_HEAD), BF16),
                   jax.ShapeDtypeStruct((s, Q_LORA), BF16), jax.ShapeDtypeStruct((s, KV_LORA), BF16)),
        in_specs=[row(IN_PAD), whole(qn), whole(kvn), whole(wq), whole(wkv), row(LANE), row(LANE)],
        out_specs=(row(hp), row(hp), row(N_HEADS * V_HEAD), row(Q_LORA), row(KV_LORA)),
        compiler_params=_params(("parallel",)),
    )(z, qn, kvn, wq, wkv, cos, sin)


def _qkv_bwd(dq, dk, dv, du, z, qn, kvn, wq, wkv, cos, sin, name):
    s = z.shape[0]
    ts = _tile(s, 256)

    def norm_bwd(x, g, dy):
        _, xn, r = _plain_rms(x, g)
        dxn = dy * g
        return r * (dxn - xn * jnp.mean(dxn * xn, axis=-1, keepdims=True)), jnp.sum(dy * xn, axis=0, keepdims=True)

    def body(dq_ref, dk_ref, dv_ref, du_ref, z_ref, qn_ref, kvn_ref, wq_ref, wkv_ref, cos_ref, sin_ref,
             dz_ref, dqb_ref, dkvb_ref, dqn_ref, dkvn_ref):
        @pl.when(pl.program_id(0) == 0)
        def _():
            dqn_ref[...] = jnp.zeros_like(dqn_ref)
            dkvn_ref[...] = jnp.zeros_like(dkvn_ref)

        cosv, sinv = cos_ref[...], sin_ref[...]
        dkr = jnp.zeros((ts, LANE), F32)
        for h in range(N_HEADS):
            o = h * HEAD_PAD
            dqb_ref[:, o:o + QK_NOPE] = dq_ref[:, o:o + QK_NOPE].astype(BF16)
            dqb_ref[:, o + QK_NOPE:o + HEAD_PAD] = _rope_t(dq_ref[:, o + QK_NOPE:o + HEAD_PAD], cosv, sinv).astype(BF16)
            dkvb_ref[:, o:o + QK_NOPE] = dk_ref[:, o:o + QK_NOPE].astype(BF16)
            dkvb_ref[:, o + QK_NOPE:o + HEAD_PAD] = dv_ref[:, h * V_HEAD:(h + 1) * V_HEAD].astype(BF16)
            dkr = dkr + dk_ref[:, o + QK_NOPE:o + HEAD_PAD]
        dcqn = jnp.dot(dqb_ref[...], wq_ref[...], preferred_element_type=F32)
        dckvn = jnp.dot(dkvb_ref[...], wkv_ref[...], preferred_element_type=F32)
        dcq, dqn = norm_bwd(z_ref[:, O_Q:O_KV], qn_ref[...], dcqn)
        dckv, dkvn = norm_bwd(z_ref[:, O_KV:O_KR], kvn_ref[...], dckvn)
        dqn_ref[...] += dqn
        dkvn_ref[...] += dkvn
        dz_ref[:, 0:O_Q] = du_ref[...].astype(BF16)
        dz_ref[:, O_Q:O_KV] = dcq.astype(BF16)
        dz_ref[:, O_KV:O_KR] = dckv.astype(BF16)
        dz_ref[:, O_KR:IN_PAD] = _rope_t(dkr, cosv, sinv).astype(BF16)

    def row(w):
        return pl.BlockSpec((ts, w), lambda i: (i, 0))

    def whole(arr):
        return pl.BlockSpec(arr.shape, lambda i: (0, 0))

    hp = N_HEADS * HEAD_PAD
    return pl.pallas_call(
        body, name=name, grid=(s // ts,),
        out_shape=(jax.ShapeDtypeStruct((s, IN_PAD), BF16), jax.ShapeDtypeStruct((s, hp), BF16),
                   jax.ShapeDtypeStruct((s, hp), BF16),
                   jax.ShapeDtypeStruct((1, Q_LORA), F32), jax.ShapeDtypeStruct((1, KV_LORA), F32)),
        in_specs=[row(hp), row(hp), row(N_HEADS * V_HEAD), row(POOL_WIDTH), row(IN_PAD),
                  whole(qn), whole(kvn), whole(wq), whole(wkv), row(LANE), row(LANE)],
        out_specs=(row(IN_PAD), row(hp), row(hp), whole(qn), whole(kvn)),
        compiler_params=_params(("arbitrary",)),
    )(dq, dk, dv, du, z, qn, kvn, wq, wkv, cos, sin)


def _causal_scores(q, k, i, tq, klen):
    sc = lax.dot_general(q, k, (((1,), (1,)), ((), ())), preferred_element_type=F32) * SOFTMAX_SCALE
    qpos = i * tq + lax.broadcasted_iota(jnp.int32, (tq, klen), 0)
    kpos = lax.broadcasted_iota(jnp.int32, (tq, klen), 1)
    return jnp.where(qpos >= kpos, sc, -jnp.inf)


ATTN_SEGMENTS = 4


def _by_key_prefix(i, nq, tq, compute):
    nseg = min(ATTN_SEGMENTS, nq)
    per = nq // nseg
    for r in range(nseg):
        pl.when(i // per == r)(lambda r=r: compute((r + 1) * per * tq))


def _attn_fwd(q, k, v, name):
    s = q.shape[0]
    tq = _tile(s, 256)
    nq = s // tq

    def body(q_ref, k_ref, v_ref, o_ref, lse_ref):
        i = pl.program_id(1)

        def compute(klen):
            sc = _causal_scores(q_ref[...], k_ref[0:klen, :], i, tq, klen)
            mx = jnp.max(sc, axis=-1, keepdims=True)
            p = jnp.exp(sc - mx)
            den = jnp.sum(p, axis=-1, keepdims=True)
            o_ref[...] = jnp.dot((p / den).astype(BF16), v_ref[0:klen, :], preferred_element_type=F32)
            lse_ref[...] = mx + jnp.log(den)

        _by_key_prefix(i, nq, tq, compute)

    return pl.pallas_call(
        body, name=name, grid=(N_HEADS, s // tq),
        out_shape=(jax.ShapeDtypeStruct((s, N_HEADS * V_HEAD), F32), jax.ShapeDtypeStruct((N_HEADS, s, 1), F32)),
        in_specs=[pl.BlockSpec((tq, HEAD_PAD), lambda h, i: (i, h)),
                  pl.BlockSpec((s, HEAD_PAD), lambda h, i: (0, h)),
                  pl.BlockSpec((s, V_HEAD), lambda h, i: (0, h))],
        out_specs=(pl.BlockSpec((tq, V_HEAD), lambda h, i: (i, h)),
                   pl.BlockSpec((None, tq, 1), lambda h, i: (h, i, 0))),
        compiler_params=_params(("parallel", "parallel")),
    )(q, k, v)


def _attn_bwd(q, k, v, lse, dycat, name):
    s = q.shape[0]
    tq = _tile(s, 256)
    nq = s // tq
    tn_dims = (((0,), (0,)), ((), ()))

    def body(q_ref, k_ref, v_ref, lse_ref, do_ref, dq_ref, dk_ref, dv_ref):
        i = pl.program_id(1)

        @pl.when(i == 0)
        def _():
            dk_ref[...] = jnp.zeros_like(dk_ref)
            dv_ref[...] = jnp.zeros_like(dv_ref)

        def compute(klen):
            qv, kv_, dob = q_ref[...], k_ref[0:klen, :], do_ref[...].astype(BF16)
            sc = _causal_scores(qv, kv_, i, tq, klen)
            p = jnp.exp(sc - lse_ref[...])
            dp = lax.dot_general(dob, v_ref[0:klen, :], (((1,), (1,)), ((), ())), preferred_element_type=F32)
            ds = (p * (dp - jnp.sum(dp * p, axis=-1, keepdims=True)) * SOFTMAX_SCALE).astype(BF16)
            dq_ref[...] = jnp.dot(ds, kv_, preferred_element_type=F32)
            dk_ref[0:klen, :] += lax.dot_general(ds, qv, tn_dims, preferred_element_type=F32)
            dv_ref[0:klen, :] += lax.dot_general(p.astype(BF16), dob, tn_dims, preferred_element_type=F32)

        _by_key_prefix(i, nq, tq, compute)

    n_pool_blocks = POOL_WIDTH // V_HEAD
    return pl.pallas_call(
        body, name=name, grid=(N_HEADS, s // tq),
        out_shape=(jax.ShapeDtypeStruct((s, N_HEADS * HEAD_PAD), F32),
                   jax.ShapeDtypeStruct((s, N_HEADS * HEAD_PAD), F32),
                   jax.ShapeDtypeStruct((s, N_HEADS * V_HEAD), F32)),
        in_specs=[pl.BlockSpec((tq, HEAD_PAD), lambda h, i: (i, h)),
                  pl.BlockSpec((s, HEAD_PAD), lambda h, i: (0, h)),
                  pl.BlockSpec((s, V_HEAD), lambda h, i: (0, h)),
                  pl.BlockSpec((None, tq, 1), lambda h, i: (h, i, 0)),
                  pl.BlockSpec((tq, V_HEAD), lambda h, i: (i, n_pool_blocks + h))],
        out_specs=(pl.BlockSpec((tq, HEAD_PAD), lambda h, i: (i, h)),
                   pl.BlockSpec((s, HEAD_PAD), lambda h, i: (0, h)),
                   pl.BlockSpec((s, V_HEAD), lambda h, i: (0, h))),
        compiler_params=_params(("parallel", "arbitrary")),
    )(q, k, v, lse, dycat)


def _loss_head(x, gw, target, name):
    s, d = x.shape
    ts = _tile(s, 256)

    def body(x_ref, gw_ref, tgt_ref, loss_ref, dx_ref, dgw_ref):
        @pl.when(pl.program_id(0) == 0)
        def _():
            loss_ref[...] = jnp.zeros_like(loss_ref)
            dgw_ref[...] = jnp.zeros_like(dgw_ref)

        xv, gwv = x_ref[...], gw_ref[...]
        r = lax.rsqrt(jnp.mean(xv * xv, axis=-1, keepdims=True) + EPS)
        xn = xv * r
        err = xn * gwv - tgt_ref[...]
        loss_ref[...] += 0.5 * jnp.sum(jnp.mean(err * err, axis=-1, keepdims=True))
        dy = err / d
        dgw_ref[...] += jnp.sum(dy * xn, axis=0, keepdims=True)
        dxn = dy * gwv
        dx_ref[...] = r * (dxn - xn * jnp.mean(dxn * xn, axis=-1, keepdims=True))

    row = pl.BlockSpec((ts, d), lambda i: (i, 0))
    return pl.pallas_call(
        body, name=name, grid=(s // ts,),
        out_shape=(jax.ShapeDtypeStruct((8, LANE), F32), jax.ShapeDtypeStruct((s, d), F32),
                   jax.ShapeDtypeStruct((1, d), F32)),
        in_specs=[row, _vec_spec(d), row],
        out_specs=(pl.BlockSpec((8, LANE), lambda i: (0, 0)), row, _vec_spec(d)),
        compiler_params=_params(("arbitrary",)),
    )(x, gw, target)


def _ada_mod(c_all, ada_w, ada_b, name):
    nl, d, cols = ada_w.shape

    def body(c_ref, w_ref, b_ref, o_ref):
        cv = c_ref[...]
        act = (cv * jax.nn.sigmoid(cv)).astype(BF16)
        o_ref[...] = jnp.dot(act, w_ref[...].astype(BF16), preferred_element_type=F32) + b_ref[...]

    return pl.pallas_call(
        body, name=name, grid=(nl,), out_shape=jax.ShapeDtypeStruct((nl, N_DEV, cols), F32),
        in_specs=[pl.BlockSpec((N_DEV, d), lambda l: (0, 0)),
                  pl.BlockSpec((None, d, cols), lambda l: (l, 0, 0)),
                  pl.BlockSpec((None, 1, cols), lambda l: (l, 0, 0))],
        out_specs=pl.BlockSpec((None, N_DEV, cols), lambda l: (l, 0, 0)),
        compiler_params=_params(("parallel",)),
    )(c_all, ada_w, ada_b)


def _ada_grad(c_pad, dmod_pad, name):
    nl, kpad, cols = dmod_pad.shape
    d = c_pad.shape[1]

    def body(c_ref, dm_ref, o_ref):
        cv = c_ref[...]
        act = (cv * jax.nn.sigmoid(cv)).astype(BF16)
        o_ref[...] = lax.dot_general(act, dm_ref[...].astype(BF16), (((0,), (0,)), ((), ())),
                                     preferred_element_type=F32)

    return pl.pallas_call(
        body, name=name, grid=(nl,), out_shape=jax.ShapeDtypeStruct((nl, d, cols), F32),
        in_specs=[pl.BlockSpec((kpad, d), lambda l: (0, 0)),
                  pl.BlockSpec((None, kpad, cols), lambda l: (l, 0, 0))],
        out_specs=pl.BlockSpec((None, d, cols), lambda l: (l, 0, 0)),
        compiler_params=_params(("parallel",)),
    )(c_pad, dmod_pad)


def _adamw_math(w, g, m, v):
    nm = ADAM_B1 * m + (1.0 - ADAM_B1) * g
    nv = ADAM_B2 * v + (1.0 - ADAM_B2) * (g * g)
    m_hat = nm / (1.0 - ADAM_B1 ** ADAM_STEP)
    v_hat = nv / (1.0 - ADAM_B2 ** ADAM_STEP)
    return -ADAM_LR * (m_hat / (jnp.sqrt(v_hat) + ADAM_EPS) + ADAM_WD * w), nm, nv


def _adamw_rows(w3, gbuf, row_off, m3, v3, name):
    nl, r, d = w3.shape
    tr = _row_tile(math.gcd(r, row_off) if row_off else r, 176)
    first = row_off // tr

    def body(w_ref, g_ref, m_ref, v_ref, go_ref, d_ref, nm_ref, nv_ref):
        gv = g_ref[...]
        go_ref[...] = gv
        d_ref[...], nm_ref[...], nv_ref[...] = _adamw_math(w_ref[...], gv, m_ref[...], v_ref[...])

    blk = pl.BlockSpec((None, tr, d), lambda l, i: (l, i, 0))
    gblk = pl.BlockSpec((None, tr, d), lambda l, i: (l, first + i, 0))
    out = jax.ShapeDtypeStruct((nl, r, d), F32)
    return pl.pallas_call(
        body, name=name, grid=(nl, r // tr), out_shape=(out, out, out, out),
        in_specs=[blk, gblk, blk, blk], out_specs=(blk, blk, blk, blk),
        compiler_params=_params(("parallel", "parallel")),
    )(w3, gbuf, m3, v3)


def _adamw(w, g, m, v, name):
    rows, cols = w.shape
    tr = _row_tile(rows, 256)

    def body(w_ref, g_ref, m_ref, v_ref, d_ref, nm_ref, nv_ref):
        d_ref[...], nm_ref[...], nv_ref[...] = _adamw_math(w_ref[...], g_ref[...], m_ref[...], v_ref[...])

    blk = pl.BlockSpec((tr, cols), lambda i: (i, 0))
    out = jax.ShapeDtypeStruct((rows, cols), F32)
    return pl.pallas_call(
        body, name=name, grid=(rows // tr,), out_shape=(out, out, out),
        in_specs=[blk, blk, blk, blk], out_specs=(blk, blk, blk),
        compiler_params=_params(("parallel",)),
    )(w, g, m, v)


def _adamw_nd(w, g, m, v, name):
    shape = w.shape
    flat = (lambda t: t.reshape(1, -1)) if w.ndim == 1 else (lambda t: t.reshape(-1, shape[-1]))
    return tuple(t.reshape(shape) for t in _adamw(flat(w), flat(g), flat(m), flat(v), name))


def _pad_rows(t, rows):
    return jnp.pad(t, ((0, rows - t.shape[0]), (0, 0)))


def _pack_shard_layer(l, wts):
    def tr(name):
        return wts[name][l].astype(BF16).T

    parts = [tr("ffn1_w_gate"), tr("ffn1_w_up"), wts["ffn1_w_down"][l].astype(BF16),
             tr("ffn2_w_gate"), tr("ffn2_w_up"), wts["ffn2_w_down"][l].astype(BF16),
             wts["w_out"][l].astype(BF16),
             tr("w_kv_b").reshape(KV_SH_ROWS, D_MODEL),
             _pad_rows(tr("w_in"), 160),
             _pad_rows(tr("w_q_b").reshape(Q_SH_ROWS, D_MODEL), Q_PAD_ROWS)]
    return jnp.concatenate(parts, axis=0)


def _full_weights(lands):
    w = dict(zip(("g1", "u1", "d1", "g2", "u2", "d2", "out"), lands))
    small = lands[-1].reshape(N_DEV, SMALL_ROWS, D_MODEL)
    o_in, o_q = OFF_IN - OFF_KV, OFF_Q - OFF_KV
    w["kv"] = small[:, :KV_SH_ROWS].reshape(N_HEADS * HEAD_PAD, KV_LORA)
    w["in"] = _pad_rows(small[:, o_in:o_in + IN_SH].reshape(IN_COLS, D_MODEL), IN_PAD)
    wq = small[:, o_q:o_q + Q_SH_ROWS].reshape(N_HEADS, QK_HEAD, Q_LORA)
    w["q"] = jnp.pad(wq, ((0, 0), (0, HEAD_PAD - QK_HEAD), (0, 0))).reshape(N_HEADS * HEAD_PAD, Q_LORA)
    return w


def _grad_sources_b(gr):
    gq = gr["q"].reshape(N_HEADS, HEAD_PAD, Q_LORA)[:, :QK_HEAD].reshape(N_DEV, Q_SH_ROWS, D_MODEL)
    small = jnp.concatenate([
        gr["kv"].reshape(N_DEV, KV_SH_ROWS, D_MODEL),
        jnp.pad(gr["in"][:IN_COLS].reshape(N_DEV, IN_SH, D_MODEL), ((0, 0), (0, 160 - IN_SH), (0, 0))),
        jnp.pad(gq, ((0, 0), (0, Q_PAD_ROWS - Q_SH_ROWS), (0, 0)))], axis=1)
    return [gr["g2"], gr["u2"], gr["d2"], gr["out"], small.reshape(N_DEV * SMALL_ROWS, D_MODEL)]


def _small_layout(nl):
    names = [("dmod", nl * N_MOD), ("ffn1_norm", nl), ("mix_norm", nl), ("ffn2_norm", nl), ("q_a_norm", nl),
             ("kv_a_norm", nl), ("pool_scale", nl), ("final_norm", 1), ("loss", 1),
             ("pool_w", nl * 4 * POOL_GC * POOL_GC // D_MODEL)]
    off, table = 0, {}
    for name, n in names:
        table[name] = (off, n)
        off += -(-n // 8) * 8
    return table, off


def _to_rows(t, width=D_MODEL):
    n, w = t.shape
    return jnp.pad(t, ((0, -(-n // 8) * 8 - n), (0, width - w)))


def kernel(x, c, positions, ada_w, ada_b, ffn1_norm, ffn1_w_gate, ffn1_w_up, ffn1_w_down, mix_norm, w_in, pool_w, pool_scale, q_a_norm, w_q_b, kv_a_norm, w_kv_b, w_out, ffn2_norm, ffn2_w_gate, ffn2_w_up, ffn2_w_down, final_norm, loss_target, m_ada_w, m_ada_b, m_ffn1_norm, m_ffn1_w_gate, m_ffn1_w_up, m_ffn1_w_down, m_mix_norm, m_w_in, m_pool_w, m_pool_scale, m_q_a_norm, m_w_q_b, m_kv_a_norm, m_w_kv_b, m_w_out, m_ffn2_norm, m_ffn2_w_gate, m_ffn2_w_up, m_ffn2_w_down, m_final_norm, v_ada_w, v_ada_b, v_ffn1_norm, v_ffn1_w_gate, v_ffn1_w_up, v_ffn1_w_down, v_mix_norm, v_w_in, v_pool_w, v_pool_scale, v_q_a_norm, v_w_q_b, v_kv_a_norm, v_w_kv_b, v_w_out, v_ffn2_norm, v_ffn2_w_gate, v_ffn2_w_up, v_ffn2_w_down, v_final_norm):
    wts = dict(ada_w=ada_w, ada_b=ada_b, ffn1_norm=ffn1_norm, ffn1_w_gate=ffn1_w_gate, ffn1_w_up=ffn1_w_up,
               ffn1_w_down=ffn1_w_down, mix_norm=mix_norm, w_in=w_in, pool_w=pool_w, pool_scale=pool_scale,
               q_a_norm=q_a_norm, w_q_b=w_q_b, kv_a_norm=kv_a_norm, w_kv_b=w_kv_b, w_out=w_out,
               ffn2_norm=ffn2_norm, ffn2_w_gate=ffn2_w_gate, ffn2_w_up=ffn2_w_up, ffn2_w_down=ffn2_w_down,
               final_norm=final_norm)
    mom_m = dict(ada_w=m_ada_w, ada_b=m_ada_b, ffn1_norm=m_ffn1_norm, ffn1_w_gate=m_ffn1_w_gate,
                 ffn1_w_up=m_ffn1_w_up, ffn1_w_down=m_ffn1_w_down, mix_norm=m_mix_norm, w_in=m_w_in,
                 pool_w=m_pool_w, pool_scale=m_pool_scale, q_a_norm=m_q_a_norm, w_q_b=m_w_q_b,
                 kv_a_norm=m_kv_a_norm, w_kv_b=m_w_kv_b, w_out=m_w_out, ffn2_norm=m_ffn2_norm,
                 ffn2_w_gate=m_ffn2_w_gate, ffn2_w_up=m_ffn2_w_up, ffn2_w_down=m_ffn2_w_down,
                 final_norm=m_final_norm)
    mom_v = dict(ada_w=v_ada_w, ada_b=v_ada_b, ffn1_norm=v_ffn1_norm, ffn1_w_gate=v_ffn1_w_gate,
                 ffn1_w_up=v_ffn1_w_up, ffn1_w_down=v_ffn1_w_down, mix_norm=v_mix_norm, w_in=v_w_in,
                 pool_w=v_pool_w, pool_scale=v_pool_scale, q_a_norm=v_q_a_norm, w_q_b=v_w_q_b,
                 kv_a_norm=v_kv_a_norm, w_kv_b=v_w_kv_b, w_out=v_w_out, ffn2_norm=v_ffn2_norm,
                 ffn2_w_gate=v_ffn2_w_gate, ffn2_w_up=v_ffn2_w_up, ffn2_w_down=v_ffn2_w_down,
                 final_norm=v_final_norm)
    order = list(wts)
    nl = ada_w.shape[0]
    seq = x.shape[1]
    me = 4 * lax.axis_index("x") + 2 * lax.axis_index("y") + lax.axis_index("c")
    ada_cols = ada_w.shape[2]

    def after_token(t, token):
        return t + token[0:1, 0:1].astype(t.dtype)

    packs = [_pack_shard_layer(l, wts) for l in range(nl)]

    c_all = _all_gather(jnp.broadcast_to(c, (8, D_MODEL)), "gather_c")[::8]

    ada_b_mine = lax.dynamic_slice_in_dim(ada_b, me * ada_cols, ada_cols, axis=1).reshape(nl, 1, ada_cols)
    mod_part = _ada_mod(c_all, ada_w, ada_b_mine, "ada_mod")
    mod_all = _all_gather(mod_part.reshape(nl * N_DEV, ada_cols), "gather_mod")
    mod_all = mod_all.reshape(N_DEV, nl, N_DEV, ada_cols)
    mod = lax.dynamic_index_in_dim(mod_all, me, axis=2, keepdims=False)
    mod = mod.transpose(1, 0, 2).reshape(nl, N_MOD, 1, D_MODEL)

    flight_a = _gather_start(packs[0][:SPLIT_AB], ROWS_A, mod, "gather_start_0a")
    flight_b = _gather_start(packs[0][SPLIT_AB:], ROWS_B, flight_a[4], "gather_start_0b")
    last_start = flight_b[4]
    if nl > 1:
        in_flight = _gather_start(packs[1], ROWS_ALL, last_start, "gather_start_1")
        last_start = in_flight[4]

    cos, sin = _rope_tables(after_token(positions.reshape(seq, 1), last_start), "rope_tables")

    def vec(t):
        return t.reshape(1, -1)

    def landed(flight, rows_list, after, tag):
        send_sems, recv_sems, pk, lands, _ = flight
        pk, lands = _gather_wait(send_sems, recv_sems, pk, lands, after, f"gather_wait_{tag}")
        return _gather_finish(pk, rows_list, lands, "gather_finish")

    xs = x.reshape(seq, D_MODEL)
    saved = []
    for l in range(nl):
        norm1 = vec(ffn1_norm[l])
        if l == 0:
            lands = landed(flight_a, ROWS_A, cos, "0a")
        elif l + 1 < nl:
            in_flight = _gather_start(packs[l + 1], ROWS_ALL, lands[0], f"gather_start_{l + 1}")
            norm1 = after_token(norm1, in_flight[4])
        sv = {}

        def ffn_fwd(xin, norm, k0, wg, wu, wd, tag):
            h, a, b, t = _ffn_up(xin, norm, mod[l, k0], mod[l, k0 + 1], wg, wu, "ffn_up")
            y, xout = _mm(t, wd, "nn", "ffn_down", res=xin, gate=mod[l, k0 + 2], gate_factor=0.5)
            sv[tag] = dict(x=xin, h=h, a=a, b=b, t=t, y=y)
            return xout

        xs = ffn_fwd(xs, norm1, 0, lands[0], lands[1], lands[2], "f1")
        if l == 0:
            lands = lands + landed(flight_b, ROWS_B, xs, "0b")
        w = _full_weights(lands)
        sv["w"] = w

        h2 = _rm_fwd(xs, vec(mix_norm[l]), mod[l, 3], mod[l, 4], "rm_fwd")
        z = _mm(h2, w["in"], "nt", "mix_in")
        y_pool, diff = _pool_fwd(z, pool_w[l], vec(pool_scale[l]), "pool_fwd")
        q, k, v, cqn, ckvn = _qkv_fwd(z, vec(q_a_norm[l]), vec(kv_a_norm[l]), w["q"], w["kv"], cos, sin, "qkv_fwd")
        o, lse = _attn_fwd(q, k, v, "attn_fwd")
        ycat = jnp.concatenate([y_pool, o.astype(BF16)], axis=1)
        y2, xmix = _mm(ycat, w["out"], "nn", "mix_out", res=xs, gate=mod[l, 5], gate_factor=1.0)
        sv["mix"] = dict(x=xs, h=h2, z=z, diff=diff, q=q, k=k, v=v, cqn=cqn, ckvn=ckvn, lse=lse, ycat=ycat, y=y2)
        xs = xmix

        xs = ffn_fwd(xs, vec(ffn2_norm[l]), 6, w["g2"], w["u2"], w["d2"], "f2")
        saved.append(sv)
        if l + 1 < nl:
            lands = landed(in_flight, ROWS_ALL, xs, l + 1)

    loss_part, dx, d_final = _loss_head(xs, vec(final_norm), loss_target.reshape(seq, D_MODEL), "loss_head")

    small = {name: [None] * nl for name in ("ffn1_norm", "mix_norm", "ffn2_norm", "q_a_norm", "kv_a_norm",
                                            "pool_scale", "pool_w", "dmod")}
    exchanges = [None] * nl
    last_token = None
    for l in reversed(range(nl)):
        sv = saved[l]
        w = sv["w"]
        dmod = [None] * N_MOD
        gr = {}
        gate3 = mod[l, 8] if last_token is None else after_token(mod[l, 8], last_token)

        def ffn_bwd(dxin, s_, norm, k0, wg, wu, wd, tag, gate):
            dy, dmod[k0 + 2] = _gate_bwd(dxin, s_["y"], gate, 0.5, "gate_bwd")
            da, db, gr["d" + tag], gr["g" + tag], gr["u" + tag] = _ffn_bwd_cols(
                dy, s_["h"], s_["a"], s_["b"], s_["t"], wd, "ffn_bwd_cols")
            dh = _mm_pair(da, wg, db, wu, "ffn_bwd_dh")
            dxo, dmod[k0], dmod[k0 + 1], dnorm = _rm_bwd(dh, s_["x"], dxin, vec(norm), mod[l, k0 + 1], "rm_bwd")
            return dxo, dnorm

        dx, small["ffn2_norm"][l] = ffn_bwd(dx, sv["f2"], ffn2_norm[l], 6, w["g2"], w["u2"], w["d2"], "2", gate3)

        s_ = sv["mix"]
        dy, dmod[5] = _gate_bwd(dx, s_["y"], mod[l, 5], 1.0, "gate_bwd")
        gr["out"] = _mm(s_["ycat"], dy, "tn", "mix_out_dw", out_dtype=BF16, tm=256)
        dycat = _mm(dy, w["out"], "nt", "mix_out_dx")
        du, small["pool_w"][l], small["pool_scale"][l] = _pool_bwd(dycat, s_["diff"], pool_w[l], vec(pool_scale[l]), "pool_bwd")
        dq, dk, dv = _attn_bwd(s_["q"], s_["k"], s_["v"], s_["lse"], dycat, "attn_bwd")
        dz, dqb, dkvb, small["q_a_norm"][l], small["kv_a_norm"][l] = _qkv_bwd(
            dq, dk, dv, du, s_["z"], vec(q_a_norm[l]), vec(kv_a_norm[l]), w["q"], w["kv"], cos, sin, "qkv_bwd")
        gr["q"] = _mm(dqb, s_["cqn"], "tn", "q_b_dw", out_dtype=BF16, tm=256)
        gr["kv"] = _mm(dkvb, s_["ckvn"], "tn", "kv_b_dw", out_dtype=BF16, tm=256)
        gr["in"] = _mm(dz, s_["h"], "tn", "mix_in_dw", out_dtype=BF16, tm=256)
        dh2 = _mm(dz, w["in"], "nn", "mix_in_dx")
        dx, dmod[3], dmod[4], small["mix_norm"][l] = _rm_bwd(dh2, s_["x"], dx, vec(mix_norm[l]), mod[l, 4], "rm_bwd")

        gate1 = mod[l, 2]
        if l == 0:
            exchange_0b = _exchange_start(_grad_sources_b(gr), ROWS_B, me, dx, "exchange_start_0b")
            gate1 = after_token(gate1, exchange_0b[4])
        dx, small["ffn1_norm"][l] = ffn_bwd(dx, sv["f1"], ffn1_norm[l], 0, w["g1"], w["u1"], w["d1"], "1", gate1)

        small["dmod"][l] = jnp.concatenate(dmod, axis=0)
        if l > 0:
            srcs = [gr["g1"], gr["u1"], gr["d1"]] + _grad_sources_b(gr)
            exchanges[l] = _exchange_start(srcs, ROWS_ALL, me, dx, f"exchange_start_{l}")
            last_token = exchanges[l][4]

    grad_x = dx.reshape(x.shape)

    layout, small_rows = _small_layout(nl)
    pieces = {
        "dmod": jnp.concatenate(small["dmod"], axis=0),
        "ffn1_norm": jnp.concatenate(small["ffn1_norm"], axis=0),
        "mix_norm": jnp.concatenate(small["mix_norm"], axis=0),
        "ffn2_norm": jnp.concatenate(small["ffn2_norm"], axis=0),
        "q_a_norm": jnp.concatenate(small["q_a_norm"], axis=0),
        "kv_a_norm": jnp.concatenate(small["kv_a_norm"], axis=0),
        "pool_scale": jnp.concatenate(small["pool_scale"], axis=0),
        "final_norm": d_final,
        "loss": jnp.broadcast_to(loss_part[0:1, 0:1], (1, D_MODEL)),
        "pool_w": jnp.stack(small["pool_w"]).reshape(-1, D_MODEL),
    }
    small_buf = jnp.concatenate([_to_rows(pieces[name]) for name in layout], axis=0)
    if last_token is not None:
        small_buf = after_token(small_buf, last_token)
    small_all = _all_gather(small_buf, "gather_small").reshape(N_DEV, small_rows, D_MODEL)
    exchange_0a = _exchange_start([gr["g1"], gr["u1"], gr["d1"]], ROWS_A, me, small_all, "exchange_start_0a")
    small_sum = _sum_slots(small_all, "sum_small", after=exchange_0a[4])

    def take(name, width=D_MODEL):
        off, n = layout[name]
        return small_sum[off:off + n, :width]

    grads = {}
    grads["ada_b"] = take("dmod").reshape(nl, N_MOD * D_MODEL)
    grads["ffn1_norm"], grads["mix_norm"], grads["ffn2_norm"] = take("ffn1_norm"), take("mix_norm"), take("ffn2_norm")
    grads["q_a_norm"], grads["kv_a_norm"] = take("q_a_norm", Q_LORA), take("kv_a_norm", KV_LORA)
    grads["pool_scale"] = take("pool_scale", POOL_WIDTH)
    grads["final_norm"] = take("final_norm").reshape(D_MODEL)
    grads["pool_w"] = take("pool_w").reshape(pool_w.shape)
    loss = take("loss")[0, 0]

    off, n = layout["dmod"]
    dmod_all = small_all[:, off:off + n].reshape(N_DEV, nl, N_MOD * D_MODEL)
    dmod_mine = lax.dynamic_slice_in_dim(dmod_all, me * ada_cols, ada_cols, axis=2)
    dmod_pad = jnp.pad(dmod_mine.transpose(1, 0, 2), ((0, 0), (0, LANE - N_DEV), (0, 0)))
    grads["ada_w"] = _ada_grad(jnp.pad(c_all, ((0, LANE - N_DEV), (0, 0))), dmod_pad, "ada_grad")

    updates = {name: _adamw_nd(wts[name], grads[name], mom_m[name], mom_v[name], "adamw") for name in grads}

    def summed(gbuf, exchange, layer, row_off, after, tag):
        send_sems, recv_sems, srcs, recv, _ = exchange
        recv = _exchange_wait(send_sems, recv_sems, srcs, recv, after, f"exchange_wait_{tag}")
        return _sum_slots_into(recv, gbuf, layer, row_off, "sum_grads")

    gbuf = lax.empty((nl, ROWS_L, D_MODEL), F32)
    for l in reversed(range(1, nl)):
        gbuf = summed(gbuf, exchanges[l], l, 0, exchange_0a[4], l)
    gbuf = summed(gbuf, exchange_0b, 0, SPLIT_AB, updates["ada_w"][0], "0b")
    gbuf = summed(gbuf, exchange_0a, 0, 0, updates["ada_w"][0], "0a")

    def swap(t):
        return t.transpose(0, 2, 1)

    def same(t):
        return t

    for wname, off, view in (("ffn1_w_gate", OFF_G1, swap), ("ffn1_w_up", OFF_U1, swap), ("ffn1_w_down", OFF_D1, same),
                             ("ffn2_w_gate", OFF_G2, swap), ("ffn2_w_up", OFF_U2, swap), ("ffn2_w_down", OFF_D2, same),
                             ("w_out", OFF_OUT, same)):
        g, d_, nm, nv = _adamw_rows(view(wts[wname]), gbuf, off, view(mom_m[wname]), view(mom_v[wname]), "adamw_rows")
        grads[wname], updates[wname] = view(g), (view(d_), view(nm), view(nv))
    small_grads = {
        "w_kv_b": (gbuf[:, OFF_KV:OFF_KV + KV_SH_ROWS].reshape(nl, -1, KV_LORA).transpose(0, 2, 1), same),
        "w_in": (gbuf[:, OFF_IN:OFF_IN + IN_SH], swap),
        "w_q_b": (gbuf[:, OFF_Q:OFF_Q + Q_SH_ROWS].reshape(nl, -1, Q_LORA), swap),
    }
    for wname, (g, view) in small_grads.items():
        upd = _adamw_nd(view(wts[wname]), g, view(mom_m[wname]), view(mom_v[wname]), "adamw")
        grads[wname], updates[wname] = view(g), tuple(view(t) for t in upd)

    return (loss, grad_x, *[grads[n] for n in order], *[updates[n][0] for n in order],
            *[updates[n][1] for n in order], *[updates[n][2] for n in order])
```

```python
import math

import numpy as np
import jax
import jax.numpy as jnp
from jax import lax
from jax.experimental import pallas as pl
from jax.experimental.pallas import tpu as pltpu

F32 = jnp.float32
BF16 = jnp.bfloat16

N_DEV = 8
D_MODEL = 1024
D_FF = 2816
POOL_WIDTH = 512
POOL_WINDOWS = (2, 4, 8, 16)
POOL_GC = 128
N_HEADS = 4
QK_NOPE = 128
QK_ROPE = 64
V_HEAD = 128
QK_HEAD = QK_NOPE + QK_ROPE
HEAD_PAD = 256
Q_LORA = 384
KV_LORA = 256
IN_COLS = POOL_WIDTH + Q_LORA + KV_LORA + QK_ROPE
IN_PAD = 1280
ROPE_THETA = 10000.0
SOFTMAX_SCALE = 1.0 / math.sqrt(QK_HEAD)
EPS = 1e-6
N_MOD = 9

ADAM_LR = 0.001
ADAM_B1 = 0.9
ADAM_B2 = 0.999
ADAM_EPS = 1e-08
ADAM_WD = 0.01
ADAM_STEP = 10

LANE = 128
VMEM_LIMIT = 56 * 1024 * 1024

FF_SH = D_FF // N_DEV
OFF_G1, OFF_U1, OFF_D1 = 0, FF_SH, 2 * FF_SH
OFF_G2, OFF_U2, OFF_D2 = 3 * FF_SH, 4 * FF_SH, 5 * FF_SH
OFF_OUT = 6 * FF_SH
OFF_KV = OFF_OUT + 128
OFF_IN = OFF_KV + 32
OFF_Q = OFF_IN + 160
Q_PAD_ROWS = 64
ROWS_L = OFF_Q + Q_PAD_ROWS
IN_SH = IN_COLS // N_DEV
Q_SH_ROWS = (N_HEADS * QK_HEAD // N_DEV) * Q_LORA // D_MODEL
KV_SH_ROWS = (N_HEADS * (QK_NOPE + V_HEAD) // N_DEV) * KV_LORA // D_MODEL


def _tile(dim, target):
    if dim <= target:
        return dim
    best = None
    for t in range(LANE, target + 1, LANE):
        if dim % t == 0:
            best = t
    assert best is not None, (dim, target)
    return best


def _params(sem):
    return pltpu.CompilerParams(dimension_semantics=sem, vmem_limit_bytes=VMEM_LIMIT)


def _mesh_pos():
    return lax.axis_index("x"), lax.axis_index("y"), lax.axis_index("c")


def _all_gather(x, name):
    m, n = x.shape

    def body(x_ref, out_ref, send_sems, recv_sems, local_sem):
        px, py, pc = _mesh_pos()
        me, sibling = (px, py, pc), (px, py, 1 - pc)
        chips = [(1 - px, py), (px, 1 - py), (1 - px, 1 - py)]

        def rows(bx, by, bc):
            return out_ref.at[pl.ds((4 * bx + 2 * by + bc) * m, m), :]

        def copy(k, block, to, src=None):
            return pltpu.make_async_remote_copy(
                src_ref=rows(*block) if src is None else src, dst_ref=rows(*block),
                send_sem=send_sems.at[k], recv_sem=recv_sems.at[k],
                device_id=to, device_id_type=pl.DeviceIdType.MESH)

        mine = pltpu.make_async_copy(x_ref, rows(*me), local_sem)
        mine.start()
        first = [copy(0, me, sibling, src=x_ref)]
        first += [copy(1 + j, me, (*chip, pc), src=x_ref) for j, chip in enumerate(chips)]
        for cp in first:
            cp.start()
        passed = [copy(4 + j, (*chip, pc), sibling) for j, chip in enumerate(chips)]
        for j, chip in enumerate(chips):
            copy(1 + j, (*chip, pc), me).wait_recv()
            passed[j].start()
        copy(0, sibling, me).wait_recv()
        for j, chip in enumerate(chips):
            copy(4 + j, (*chip, 1 - pc), me).wait_recv()
        for cp in first + passed:
            cp.wait_send()
        mine.wait()

    return pl.pallas_call(
        body, name=name,
        out_shape=jax.ShapeDtypeStruct((N_DEV * m, n), x.dtype),
        in_specs=[pl.BlockSpec(memory_space=pltpu.HBM)],
        out_specs=pl.BlockSpec(memory_space=pltpu.HBM),
        scratch_shapes=[pltpu.SemaphoreType.DMA((7,)), pltpu.SemaphoreType.DMA((7,)),
                        pltpu.SemaphoreType.DMA],
    )(x)


SMALL_ROWS = ROWS_L - OFF_KV
ROWS_A = [FF_SH] * 3
ROWS_B = [FF_SH] * 3 + [128, SMALL_ROWS]
ROWS_ALL = ROWS_A + ROWS_B
SPLIT_AB = sum(ROWS_A)
HBM_SPEC = pl.BlockSpec(memory_space=pltpu.HBM)
SEM_SPEC = pl.BlockSpec(memory_space=pltpu.SEMAPHORE)
ANY_SPEC = pl.BlockSpec(memory_space=pl.ANY)
EFFECT = pltpu.SideEffectType.DATAFLOW_SIDE_EFFECTING


def _hbm(t):
    return pltpu.with_memory_space_constraint(t, pltpu.HBM)


def _whole_wait(ref, send_sem, recv_sem, peer):
    return pltpu.make_async_remote_copy(src_ref=ref, dst_ref=ref, send_sem=send_sem, recv_sem=recv_sem,
                                        device_id=peer, device_id_type=pl.DeviceIdType.MESH)


def _offsets(rows_list):
    return [sum(rows_list[:i]) for i in range(len(rows_list))]


def _gather_start(packed, rows_list, after, name):
    n = len(rows_list)
    offs = _offsets(rows_list)
    lands = [_hbm(lax.empty((N_DEV * rows, D_MODEL), BF16)) for rows in rows_list]

    def body(packed_ref, *refs):
        land = refs[:n]
        send_sems, recv_sems = refs[n + 1], refs[n + 2]
        token = refs[-1]
        px, py, pc = _mesh_pos()
        me = 4 * px + 2 * py + pc
        peers = [(px, py, 1 - pc), (1 - px, py, pc), (px, 1 - py, pc), (1 - px, 1 - py, pc)]
        for k, peer in enumerate(peers):
            for off, rows, land_ref in zip(offs, rows_list, land):
                pltpu.make_async_remote_copy(
                    src_ref=packed_ref.at[pl.ds(off, rows), :], dst_ref=land_ref.at[pl.ds(me * rows, rows), :],
                    send_sem=send_sems.at[k], recv_sem=recv_sems.at[k],
                    device_id=peer, device_id_type=pl.DeviceIdType.MESH).start()
        token[...] = jnp.zeros_like(token)

    outs = pl.pallas_call(
        body, name=name,
        out_shape=(pltpu.SemaphoreType.DMA((4,)), pltpu.SemaphoreType.DMA((4,)), pltpu.HBM(packed.shape, BF16),
                   *[pltpu.HBM(t.shape, BF16) for t in lands], jax.ShapeDtypeStruct((8, LANE), F32)),
        in_specs=(HBM_SPEC,) * (1 + n) + (ANY_SPEC,),
        out_specs=(SEM_SPEC, SEM_SPEC) + (HBM_SPEC,) * (1 + n) + (pl.BlockSpec(memory_space=pltpu.VMEM),),
        input_output_aliases={i: 2 + i for i in range(1 + n)},
        compiler_params=pltpu.CompilerParams(has_side_effects=EFFECT),
    )(_hbm(packed), *lands, after)
    return outs[0], outs[1], outs[2], list(outs[3:3 + n]), outs[-1]


def _gather_wait(send_sems, recv_sems, packed, lands, after, name):
    n = len(lands)

    def body(packed_ref, *refs):
        s_sems, r_sems = refs[n], refs[n + 1]
        me = _mesh_pos()
        for k in range(4):
            cp = _whole_wait(packed_ref, s_sems.at[k], r_sems.at[k], me)
            cp.wait_send()
            cp.wait_recv()

    outs = pl.pallas_call(
        body, name=name,
        out_shape=(pltpu.HBM(packed.shape, BF16), *[pltpu.HBM(t.shape, BF16) for t in lands]),
        in_specs=(HBM_SPEC,) * (1 + n) + (SEM_SPEC, SEM_SPEC, ANY_SPEC),
        out_specs=(HBM_SPEC,) * (1 + n),
        input_output_aliases={i: i for i in range(1 + n)},
        compiler_params=pltpu.CompilerParams(has_side_effects=EFFECT),
    )(packed, *lands, send_sems, recv_sems, after)
    return outs[0], list(outs[1:])


def _gather_finish(packed, rows_list, lands, name):
    n = len(rows_list)
    offs = _offsets(rows_list)

    def body(packed_ref, *refs):
        land = refs[n:2 * n]
        send_sems, recv_sems, stage, stage_sem = refs[2 * n:]
        px, py, pc = _mesh_pos()
        me = 4 * px + 2 * py + pc
        sibling = (px, py, 1 - pc)
        load = pltpu.make_async_copy(packed_ref, stage, stage_sem)
        load.start()
        load.wait()
        for off, rows, land_ref in zip(offs, rows_list, land):
            pltpu.make_async_copy(stage.at[pl.ds(off, rows), :], land_ref.at[pl.ds(me * rows, rows), :],
                                  stage_sem).start()
        for j, (cx, cy) in enumerate([(1 - px, py), (px, 1 - py), (1 - px, 1 - py)]):
            block = 4 * cx + 2 * cy + pc
            for rows, land_ref in zip(rows_list, land):
                blk = land_ref.at[pl.ds(block * rows, rows), :]
                pltpu.make_async_remote_copy(src_ref=blk, dst_ref=blk, send_sem=send_sems.at[j],
                                             recv_sem=recv_sems.at[j], device_id=sibling,
                                             device_id_type=pl.DeviceIdType.MESH).start()
        for j in range(3):
            cp = _whole_wait(packed_ref, send_sems.at[j], recv_sems.at[j], sibling)
            cp.wait_recv()
            cp.wait_send()
        pltpu.make_async_copy(stage, packed_ref, stage_sem).wait()

    outs = pl.pallas_call(
        body, name=name,
        out_shape=tuple(jax.ShapeDtypeStruct(t.shape, BF16) for t in lands),
        in_specs=(HBM_SPEC,) * (1 + n), out_specs=(HBM_SPEC,) * n,
        input_output_aliases={1 + i: i for i in range(n)},
        scratch_shapes=[pltpu.SemaphoreType.DMA((3,)), pltpu.SemaphoreType.DMA((3,)),
                        pltpu.VMEM(packed.shape, BF16), pltpu.SemaphoreType.DMA],
    )(packed, *lands)
    return list(outs)


N_CHIPS = 4


def _pair_start(srcs, rows_list, after, name):
    n = len(rows_list)
    offs = _offsets(rows_list)
    land = lax.empty((N_CHIPS, sum(rows_list), D_MODEL), BF16)

    def body(*refs):
        src, land_ref = refs[:n], refs[n]
        send_sems, recv_sems = refs[n + 2], refs[n + 3]
        token = refs[-1]
        px, py, pc = _mesh_pos()
        for k in range(N_CHIPS):
            block = 2 * k + (1 - pc)
            for off, rows, src_ref in zip(offs, rows_list, src):
                pltpu.make_async_remote_copy(
                    src_ref=src_ref.at[pl.ds(block * rows, rows), :], dst_ref=land_ref.at[k, pl.ds(off, rows), :],
                    send_sem=send_sems.at[0], recv_sem=recv_sems.at[0],
                    device_id=(px, py, 1 - pc), device_id_type=pl.DeviceIdType.MESH).start()
        token[...] = jnp.zeros_like(token)

    outs = pl.pallas_call(
        body, name=name,
        out_shape=(pltpu.SemaphoreType.DMA((1,)), pltpu.SemaphoreType.DMA((1,)),
                   *[pltpu.HBM(t.shape, BF16) for t in srcs], pltpu.HBM(land.shape, BF16),
                   jax.ShapeDtypeStruct((8, LANE), F32)),
        in_specs=(HBM_SPEC,) * (n + 1) + (ANY_SPEC,),
        out_specs=(SEM_SPEC, SEM_SPEC) + (HBM_SPEC,) * (n + 1) + (pl.BlockSpec(memory_space=pltpu.VMEM),),
        input_output_aliases={i: 2 + i for i in range(n + 1)},
        compiler_params=pltpu.CompilerParams(has_side_effects=EFFECT),
    )(*[_hbm(t) for t in srcs], _hbm(land), after)
    return outs[0], outs[1], list(outs[2:2 + n]), outs[2 + n], outs[-1]


def _split_wait(send_sems, recv_sems, n_sems, srcs, land, after, name):
    n = len(srcs)

    def body(*refs):
        land_ref = refs[n]
        s_sems, r_sems = refs[n + 1], refs[n + 2]
        me = _mesh_pos()
        for k in range(n_sems):
            cp = _whole_wait(land_ref.at[0] if n_sems > 1 else land_ref, s_sems.at[k], r_sems.at[k], me)
            cp.wait_send()
            cp.wait_recv()

    outs = pl.pallas_call(
        body, name=name,
        out_shape=(*[pltpu.HBM(t.shape, BF16) for t in srcs], pltpu.HBM(land.shape, BF16)),
        in_specs=(HBM_SPEC,) * (n + 1) + (SEM_SPEC, SEM_SPEC, ANY_SPEC),
        out_specs=(HBM_SPEC,) * (n + 1),
        input_output_aliases={i: i for i in range(n + 1)},
        compiler_params=pltpu.CompilerParams(has_side_effects=EFFECT),
    )(*srcs, land, send_sems, recv_sems, after)
    return list(outs[:n]), outs[n]


def _pair_sum(srcs, rows_list, land, core, name):
    n = len(rows_list)
    offs = _offsets(rows_list)
    total = sum(rows_list)

    def body(core_ref, *refs):
        src, land_ref, out_ref = refs[:n], refs[n], refs[n + 1]
        for off, rows, src_ref in zip(offs, rows_list, src):
            out_ref[pl.ds(off, rows), :] = (src_ref[...].astype(F32)
                                            + land_ref[pl.ds(off, rows), :].astype(F32)).astype(BF16)

    slot = pl.BlockSpec((None, total, D_MODEL), lambda k, c: (k, 0, 0))
    grid_spec = pltpu.PrefetchScalarGridSpec(
        num_scalar_prefetch=1, grid=(N_CHIPS,),
        in_specs=[pl.BlockSpec((rows, D_MODEL), lambda k, c: (2 * k + c[0], 0)) for rows in rows_list] + [slot],
        out_specs=slot)
    return pl.pallas_call(
        body, name=name, grid_spec=grid_spec,
        out_shape=jax.ShapeDtypeStruct((N_CHIPS, total, D_MODEL), BF16),
        compiler_params=_params(("parallel",)),
    )(core, *srcs, land)


def _chip_exchange_start(sums, chip, after, name):
    own = lax.dynamic_index_in_dim(sums, chip, axis=0, keepdims=True)
    recv = lax.dynamic_update_slice_in_dim(lax.empty(sums.shape, BF16), own, chip, axis=0)

    def body(sums_ref, recv_ref, after_ref, send_sems, recv_sems, sums_thru, recv_thru, token):
        px, py, pc = _mesh_pos()
        for k in range(1, N_CHIPS):
            qx = 1 - px if k & 2 else px
            qy = 1 - py if k & 1 else py
            pltpu.make_async_remote_copy(
                src_ref=sums_ref.at[2 * qx + qy], dst_ref=recv_ref.at[2 * px + py],
                send_sem=send_sems.at[k - 1], recv_sem=recv_sems.at[k - 1],
                device_id=(qx, qy, pc), device_id_type=pl.DeviceIdType.MESH).start()
        token[...] = jnp.zeros_like(token)

    outs = pl.pallas_call(
        body, name=name,
        out_shape=(pltpu.SemaphoreType.DMA((N_CHIPS - 1,)), pltpu.SemaphoreType.DMA((N_CHIPS - 1,)),
                   pltpu.HBM(sums.shape, BF16), pltpu.HBM(recv.shape, BF16), jax.ShapeDtypeStruct((8, LANE), F32)),
        in_specs=(HBM_SPEC, HBM_SPEC, ANY_SPEC),
        out_specs=(SEM_SPEC, SEM_SPEC, HBM_SPEC, HBM_SPEC, pl.BlockSpec(memory_space=pltpu.VMEM)),
        input_output_aliases={0: 2, 1: 3},
        compiler_params=pltpu.CompilerParams(has_side_effects=EFFECT),
    )(_hbm(sums), _hbm(recv), after)
    return outs[0], outs[1], [outs[2]], outs[3], outs[4]


def _sum_slots_into(recv, buf, layer, row_off, name):
    slots, r, n = recv.shape
    tr = _row_tile(math.gcd(r, row_off) if row_off else r, 512)
    first = row_off // tr

    def body(in_ref, buf_ref, out_ref):
        acc = in_ref[0].astype(F32)
        for j in range(1, slots):
            acc = acc + in_ref[j].astype(F32)
        out_ref[...] = acc

    return pl.pallas_call(
        body, name=name, grid=(r // tr,), out_shape=jax.ShapeDtypeStruct(buf.shape, F32),
        in_specs=[pl.BlockSpec((slots, tr, n), lambda i: (0, i, 0)), ANY_SPEC],
        out_specs=pl.BlockSpec((None, tr, n), lambda i: (layer, first + i, 0)),
        input_output_aliases={1: 0},
        compiler_params=_params(("parallel",)),
    )(recv, buf)


def _sum_slots(recv, name, after=None):
    _, r, n = recv.shape
    tr = _row_tile(r, 512)

    def body(in_ref, *refs):
        acc = in_ref[0].astype(F32)
        for j in range(1, N_DEV):
            acc = acc + in_ref[j].astype(F32)
        refs[-1][...] = acc

    grid = (r // tr,)
    in_specs, out_spec = [pl.BlockSpec((N_DEV, tr, n), lambda i: (0, i, 0))], pl.BlockSpec((tr, n), lambda i: (i, 0))
    args = [recv]
    if after is not None:
        in_specs.append(ANY_SPEC)
        args.append(after)
    return pl.pallas_call(
        body, name=name, grid=grid,
        out_shape=jax.ShapeDtypeStruct((r, n), F32),
        in_specs=in_specs, out_specs=out_spec,
        compiler_params=_params(("parallel",)),
    )(*args)


def _row_tile(rows, target):
    if rows <= target:
        return rows
    best = None
    for t in range(16, target + 1, 16):
        if rows % t == 0:
            best = t
    assert best is not None, rows
    return best


_DIMS = {"nn": ((1,), (0,)), "nt": ((1,), (1,)), "tn": ((0,), (0,))}


def _mm(a, b, mode, name, out_dtype=F32, res=None, gate=None, gate_factor=1.0, tm=512, tn=1408):
    if mode == "tn":
        kdim, m = a.shape
    else:
        m, kdim = a.shape
    n = b.shape[0] if mode == "nt" else b.shape[1]
    tm, tn = _tile(m, tm), _tile(n, tn)
    a_spec = (pl.BlockSpec((kdim, tm), lambda i, j: (0, i)) if mode == "tn"
              else pl.BlockSpec((tm, kdim), lambda i, j: (i, 0)))
    b_spec = (pl.BlockSpec((tn, kdim), lambda i, j: (j, 0)) if mode == "nt"
              else pl.BlockSpec((kdim, tn), lambda i, j: (0, j)))
    o_spec = pl.BlockSpec((tm, tn), lambda i, j: (i, j))
    dims = (_DIMS[mode], ((), ()))
    has_res, has_gate = res is not None, gate is not None

    def body(*refs):
        a_ref, b_ref = refs[0], refs[1]
        y = lax.dot_general(a_ref[...].astype(BF16), b_ref[...].astype(BF16), dims,
                            preferred_element_type=F32)
        if not has_res:
            refs[2][...] = y.astype(out_dtype)
            return
        res_ref = refs[2]
        if has_gate:
            y_ref, o_ref = refs[4], refs[5]
            y_ref[...] = y
            o_ref[...] = res_ref[...] + (gate_factor * refs[3][...]) * y
        else:
            refs[3][...] = res_ref[...] + y

    in_specs, args = [a_spec, b_spec], [a, b]
    if has_res:
        in_specs.append(o_spec)
        args.append(res)
        if has_gate:
            in_specs.append(pl.BlockSpec((1, tn), lambda i, j: (0, j)))
            args.append(gate)
            out_shape = (jax.ShapeDtypeStruct((m, n), F32), jax.ShapeDtypeStruct((m, n), F32))
            out_specs = (o_spec, o_spec)
        else:
            out_shape, out_specs = jax.ShapeDtypeStruct((m, n), F32), o_spec
    else:
        out_shape, out_specs = jax.ShapeDtypeStruct((m, n), out_dtype), o_spec
    return pl.pallas_call(
        body, name=name, grid=(m // tm, n // tn), out_shape=out_shape,
        in_specs=in_specs, out_specs=out_specs,
        compiler_params=_params(("parallel", "parallel")),
    )(*args)


def _vec_spec(width):
    return pl.BlockSpec((1, width), lambda i: (0, 0))


def _rm_fwd(x, gw, shift, scale, name):
    s, d = x.shape
    ts = _tile(s, 256)

    def body(x_ref, gw_ref, sh_ref, sc_ref, h_ref):
        xv = x_ref[...]
        r = lax.rsqrt(jnp.mean(xv * xv, axis=-1, keepdims=True) + EPS)
        y = (xv * r) * gw_ref[...]
        h_ref[...] = (y * (1 + sc_ref[...]) + sh_ref[...]).astype(BF16)

    row = pl.BlockSpec((ts, d), lambda i: (i, 0))
    return pl.pallas_call(
        body, name=name, grid=(s // ts,), out_shape=jax.ShapeDtypeStruct((s, d), BF16),
        in_specs=[row, _vec_spec(d), _vec_spec(d), _vec_spec(d)], out_specs=row,
        compiler_params=_params(("parallel",)),
    )(x, gw, shift, scale)


def _rm_bwd(dh, x, dres, gw, scale, name):
    s, d = x.shape
    ts = _tile(s, 256)

    def body(dh_ref, x_ref, dres_ref, gw_ref, sc_ref, dx_ref, dsh_ref, dsc_ref, dgw_ref):
        @pl.when(pl.program_id(0) == 0)
        def _():
            dsh_ref[...] = jnp.zeros_like(dsh_ref)
            dsc_ref[...] = jnp.zeros_like(dsc_ref)
            dgw_ref[...] = jnp.zeros_like(dgw_ref)

        xv, dhv, gwv = x_ref[...], dh_ref[...], gw_ref[...]
        r = lax.rsqrt(jnp.mean(xv * xv, axis=-1, keepdims=True) + EPS)
        xn = xv * r
        y = xn * gwv
        dsh_ref[...] += jnp.sum(dhv, axis=0, keepdims=True)
        dsc_ref[...] += jnp.sum(dhv * y, axis=0, keepdims=True)
        dy = dhv * (1 + sc_ref[...])
        dgw_ref[...] += jnp.sum(dy * xn, axis=0, keepdims=True)
        dxn = dy * gwv
        dx = r * (dxn - xn * jnp.mean(dxn * xn, axis=-1, keepdims=True))
        dx_ref[...] = dres_ref[...] + dx

    row = pl.BlockSpec((ts, d), lambda i: (i, 0))
    vec = jax.ShapeDtypeStruct((1, d), F32)
    return pl.pallas_call(
        body, name=name, grid=(s // ts,),
        out_shape=(jax.ShapeDtypeStruct((s, d), F32), vec, vec, vec),
        in_specs=[row, row, row, _vec_spec(d), _vec_spec(d)],
        out_specs=(row, _vec_spec(d), _vec_spec(d), _vec_spec(d)),
        compiler_params=_params(("arbitrary",)),
    )(dh, x, dres, gw, scale)


def _gate_bwd(dx, y, gate, factor, name):
    s, d = dx.shape
    ts = _tile(s, 256)

    def body(dx_ref, y_ref, g_ref, dy_ref, dg_ref):
        @pl.when(pl.program_id(0) == 0)
        def _():
            dg_ref[...] = jnp.zeros_like(dg_ref)

        dxv = dx_ref[...]
        dy_ref[...] = ((factor * g_ref[...]) * dxv).astype(BF16)
        dg_ref[...] += jnp.sum((factor * dxv) * y_ref[...], axis=0, keepdims=True)

    row = pl.BlockSpec((ts, d), lambda i: (i, 0))
    return pl.pallas_call(
        body, name=name, grid=(s // ts,),
        out_shape=(jax.ShapeDtypeStruct((s, d), BF16), jax.ShapeDtypeStruct((1, d), F32)),
        in_specs=[row, row, _vec_spec(d)], out_specs=(row, _vec_spec(d)),
        compiler_params=_params(("arbitrary",)),
    )(dx, y, gate)


FFN_TM, FFN_TF = 2048, 256


def _ffn_up(x, gw, shift, scale, wg, wu, name):
    s, d = x.shape
    f = wg.shape[0]
    tm, tf = _tile(s, FFN_TM), _tile(f, FFN_TF)
    nt = (((1,), (1,)), ((), ()))

    def body(x_ref, gw_ref, sh_ref, sc_ref, wg_ref, wu_ref, h_ref, a_ref, b_ref, t_ref):
        @pl.when(pl.program_id(1) == 0)
        def _():
            xv = x_ref[...]
            r = lax.rsqrt(jnp.mean(xv * xv, axis=-1, keepdims=True) + EPS)
            h_ref[...] = (((xv * r) * gw_ref[...]) * (1 + sc_ref[...]) + sh_ref[...]).astype(BF16)

        hb = h_ref[...]
        av = lax.dot_general(hb, wg_ref[...], nt, preferred_element_type=F32)
        bv = lax.dot_general(hb, wu_ref[...], nt, preferred_element_type=F32)
        a_ref[...] = av.astype(BF16)
        b_ref[...] = bv.astype(BF16)
        t_ref[...] = ((av * jax.nn.sigmoid(av)) * bv).astype(BF16)

    row = pl.BlockSpec((tm, d), lambda i, j: (i, 0))
    vec = pl.BlockSpec((1, d), lambda i, j: (0, 0))
    wblk = pl.BlockSpec((tf, d), lambda i, j: (j, 0))
    blk = pl.BlockSpec((tm, tf), lambda i, j: (i, j))
    wide = jax.ShapeDtypeStruct((s, f), BF16)
    return pl.pallas_call(
        body, name=name, grid=(s // tm, f // tf),
        out_shape=(jax.ShapeDtypeStruct((s, d), BF16), wide, wide, wide),
        in_specs=[row, vec, vec, vec, wblk, wblk], out_specs=(row, blk, blk, blk),
        compiler_params=_params(("parallel", "arbitrary")),
    )(x, gw, shift, scale, wg, wu)


def _ffn_bwd_cols(dy, h, a, b, t, wd, name):
    s, d = dy.shape
    f = wd.shape[0]
    tf = _tile(f, FFN_TF)
    nt = (((1,), (1,)), ((), ()))
    tn = (((0,), (0,)), ((), ()))

    def body(dy_ref, h_ref, a_ref, b_ref, t_ref, wd_ref, da_ref, db_ref, gd_ref, gg_ref, gu_ref):
        dyb, hb = dy_ref[...], h_ref[...]
        dtv = lax.dot_general(dyb, wd_ref[...], nt, preferred_element_type=F32)
        av, bv = a_ref[...].astype(F32), b_ref[...].astype(F32)
        sg = jax.nn.sigmoid(av)
        dbv = (dtv * (av * sg)).astype(BF16)
        dav = ((dtv * bv) * (sg * (1 + av * (1 - sg)))).astype(BF16)
        da_ref[...] = dav
        db_ref[...] = dbv
        gd_ref[...] = lax.dot_general(t_ref[...], dyb, tn, preferred_element_type=F32).astype(BF16)
        gg_ref[...] = lax.dot_general(dav, hb, tn, preferred_element_type=F32).astype(BF16)
        gu_ref[...] = lax.dot_general(dbv, hb, tn, preferred_element_type=F32).astype(BF16)

    whole = pl.BlockSpec((s, d), lambda j: (0, 0))
    col = pl.BlockSpec((s, tf), lambda j: (0, j))
    wblk = pl.BlockSpec((tf, d), lambda j: (j, 0))
    wide, wgrad = jax.ShapeDtypeStruct((s, f), BF16), jax.ShapeDtypeStruct((f, d), BF16)
    return pl.pallas_call(
        body, name=name, grid=(f // tf,), out_shape=(wide, wide, wgrad, wgrad, wgrad),
        in_specs=[whole, whole, col, col, col, wblk], out_specs=(col, col, wblk, wblk, wblk),
        compiler_params=_params(("parallel",)),
    )(dy, h, a, b, t, wd)


def _mm_pair(a1, b1, a2, b2, name, tm=1024, tn=512, after=None):
    m, kdim = a1.shape
    n = b1.shape[1]
    tm, tn = _tile(m, tm), _tile(n, tn)

    def body(a1_ref, b1_ref, a2_ref, b2_ref, *refs):
        refs[-1][...] = (jnp.dot(a1_ref[...], b1_ref[...], preferred_element_type=F32)
                         + jnp.dot(a2_ref[...], b2_ref[...], preferred_element_type=F32))

    a_spec = pl.BlockSpec((tm, kdim), lambda i, j: (i, 0))
    b_spec = pl.BlockSpec((kdim, tn), lambda i, j: (0, j))
    in_specs, args = [a_spec, b_spec, a_spec, b_spec], [a1, b1, a2, b2]
    if after is not None:
        in_specs.append(ANY_SPEC)
        args.append(after)
    return pl.pallas_call(
        body, name=name, grid=(m // tm, n // tn), out_shape=jax.ShapeDtypeStruct((m, n), F32),
        in_specs=in_specs, out_specs=pl.BlockSpec((tm, tn), lambda i, j: (i, j)),
        compiler_params=_params(("parallel", "parallel")),
    )(*args)


def _pool_counts(s):
    return (lax.broadcasted_iota(jnp.int32, (s, POOL_GC), 0))


def _pool_fwd(z, pool_w, pool_scale, name):
    s = z.shape[0]

    def body(u_ref, w_ref, sc_ref, y_ref, diff_ref):
        t = lax.broadcasted_iota(jnp.int32, (s, POOL_GC), 0)
        for g, win in enumerate(POOL_WINDOWS):
            cols = slice(g * POOL_GC, (g + 1) * POOL_GC)
            u = u_ref[:, cols]
            acc, step = u, 1
            while step < win:
                acc = acc + jnp.where(t >= step, pltpu.roll(acc, step, 0), 0.0)
                step *= 2
            cnt = jnp.minimum(t + 1, win).astype(F32)
            diff = acc / cnt - u
            diff_ref[:, cols] = diff
            ypre = jnp.dot(diff.astype(BF16), w_ref[g].astype(BF16), preferred_element_type=F32)
            y_ref[:, cols] = (ypre * sc_ref[:, cols]).astype(BF16)

    return pl.pallas_call(
        body, name=name, grid=(1,),
        out_shape=(jax.ShapeDtypeStruct((s, POOL_WIDTH), BF16), jax.ShapeDtypeStruct((s, POOL_WIDTH), F32)),
        in_specs=[pl.BlockSpec((s, POOL_WIDTH), lambda i: (0, 0)),
                  pl.BlockSpec(pool_w.shape, lambda i: (0, 0, 0)),
                  pl.BlockSpec((1, POOL_WIDTH), lambda i: (0, 0))],
        out_specs=(pl.BlockSpec((s, POOL_WIDTH), lambda i: (0, 0)),
                   pl.BlockSpec((s, POOL_WIDTH), lambda i: (0, 0))),
        compiler_params=_params(("arbitrary",)),
    )(z, pool_w, pool_scale)


def _pool_bwd(dycat, diff, pool_w, pool_scale, name):
    s = diff.shape[0]

    def body(dy_ref, diff_ref, w_ref, sc_ref, du_ref, dw_ref, dsc_ref):
        t = lax.broadcasted_iota(jnp.int32, (s, POOL_GC), 0)
        for g, win in enumerate(POOL_WINDOWS):
            cols = slice(g * POOL_GC, (g + 1) * POOL_GC)
            dy, dfb, wb = dy_ref[:, cols], diff_ref[:, cols].astype(BF16), w_ref[g].astype(BF16)
            ypre = jnp.dot(dfb, wb, preferred_element_type=F32)
            dsc_ref[:, cols] = jnp.sum(dy * ypre, axis=0, keepdims=True)
            dypre = (dy * sc_ref[:, cols]).astype(BF16)
            ddiff = lax.dot_general(dypre, wb, (((1,), (1,)), ((), ())), preferred_element_type=F32)
            dw_ref[g] = lax.dot_general(dfb, dypre, (((0,), (0,)), ((), ())), preferred_element_type=F32)
            cnt = jnp.minimum(t + 1, win).astype(F32)
            acc, step = ddiff / cnt, 1
            while step < win:
                acc = acc + jnp.where(t < s - step, pltpu.roll(acc, s - step, 0), 0.0)
                step *= 2
            du_ref[:, cols] = acc - ddiff

    full = pl.BlockSpec((s, POOL_WIDTH), lambda i: (0, 0))
    return pl.pallas_call(
        body, name=name, grid=(1,),
        out_shape=(jax.ShapeDtypeStruct((s, POOL_WIDTH), F32),
                   jax.ShapeDtypeStruct(pool_w.shape, F32),
                   jax.ShapeDtypeStruct((1, POOL_WIDTH), F32)),
        in_specs=[full, full, pl.BlockSpec(pool_w.shape, lambda i: (0, 0, 0)),
                  pl.BlockSpec((1, POOL_WIDTH), lambda i: (0, 0))],
        out_specs=(full, pl.BlockSpec(pool_w.shape, lambda i: (0, 0, 0)),
                   pl.BlockSpec((1, POOL_WIDTH), lambda i: (0, 0))),
        compiler_params=_params(("arbitrary",)),
    )(dycat, diff, pool_w, pool_scale)


def _rope_tables(positions, name):
    s = positions.shape[0]
    ts = _tile(s, 512)
    freq = 1.0 / (ROPE_THETA ** (np.arange(0, QK_ROPE, 2, dtype=np.float32) / QK_ROPE))
    table = np.zeros((1, LANE), np.float32)
    table[0, :QK_ROPE // 2] = freq
    table[0, QK_ROPE // 2:QK_ROPE] = freq

    def body(pos_ref, f_ref, cos_ref, sin_ref):
        ang = pos_ref[...].astype(F32) * f_ref[...]
        cos_ref[...] = jnp.cos(ang)
        sin_ref[...] = jnp.sin(ang)

    out = jax.ShapeDtypeStruct((s, LANE), F32)
    blk = pl.BlockSpec((ts, LANE), lambda i: (i, 0))
    return pl.pallas_call(
        body, name=name, grid=(s // ts,), out_shape=(out, out),
        in_specs=[pl.BlockSpec((ts, 1), lambda i: (i, 0)), _vec_spec(LANE)], out_specs=(blk, blk),
        compiler_params=_params(("parallel",)),
    )(positions, jnp.asarray(table))


def _lane_mod64_low(shape):
    return (lax.broadcasted_iota(jnp.int32, shape, 1) % QK_ROPE) < (QK_ROPE // 2)


def _rope(x, cos, sin):
    rot = jnp.where(_lane_mod64_low(x.shape), -pltpu.roll(x, LANE - 32, 1), pltpu.roll(x, 32, 1))
    return x * cos + rot * sin


def _rope_t(dy, cos, sin):
    w = dy * sin
    rot_t = jnp.where(_lane_mod64_low(dy.shape), pltpu.roll(w, LANE - 32, 1), -pltpu.roll(w, 32, 1))
    return dy * cos + rot_t


def _plain_rms(x, g):
    r = lax.rsqrt(jnp.mean(x * x, axis=-1, keepdims=True) + EPS)
    return (x * r) * g, x * r, r


O_Q, O_KV, O_KR = POOL_WIDTH, POOL_WIDTH + Q_LORA, POOL_WIDTH + Q_LORA + KV_LORA


def _qkv_fwd(z, qn, kvn, wq, wkv, cos, sin, name):
    s = z.shape[0]
    ts = _tile(s, 256)

    def body(z_ref, qn_ref, kvn_ref, wq_ref, wkv_ref, cos_ref, sin_ref, q_ref, k_ref, v_ref, cqn_ref, ckvn_ref):
        cosv, sinv = cos_ref[...], sin_ref[...]
        cqn = _plain_rms(z_ref[:, O_Q:O_KV], qn_ref[...])[0].astype(BF16)
        ckvn = _plain_rms(z_ref[:, O_KV:O_KR], kvn_ref[...])[0].astype(BF16)
        cqn_ref[...] = cqn
        ckvn_ref[...] = ckvn
        nt = (((1,), (1,)), ((), ()))
        q = lax.dot_general(cqn, wq_ref[...], nt, preferred_element_type=F32)
        kv = lax.dot_general(ckvn, wkv_ref[...], nt, preferred_element_type=F32)
        kr = _rope(z_ref[:, O_KR:IN_PAD], cosv, sinv).astype(BF16)
        for h in range(N_HEADS):
            o = h * HEAD_PAD
            q_ref[:, o:o + QK_NOPE] = q[:, o:o + QK_NOPE].astype(BF16)
            q_ref[:, o + QK_NOPE:o + HEAD_PAD] = _rope(q[:, o + QK_NOPE:o + HEAD_PAD], cosv, sinv).astype(BF16)
            k_ref[:, o:o + QK_NOPE] = kv[:, o:o + QK_NOPE].astype(BF16)
            k_ref[:, o + QK_NOPE:o + HEAD_PAD] = kr
            v_ref[:, h * V_HEAD:(h + 1) * V_HEAD] = kv[:, o + QK_NOPE:o + HEAD_PAD].astype(BF16)

    def row(w):
        return pl.BlockSpec((ts, w), lambda i: (i, 0))

    def whole(arr):
        return pl.BlockSpec(arr.shape, lambda i: (0, 0))

    hp = N_HEADS * HEAD_PAD
    return pl.pallas_call(
        body, name=name, grid=(s // ts,),
        out_shape=(jax.ShapeDtypeStruct((s, hp), BF16), jax.ShapeDtypeStruct((s, hp), BF16),
                   jax.ShapeDtypeStruct((s, N_HEADS * V_HEAD), BF16),
                   jax.ShapeDtypeStruct((s, Q_LORA), BF16), jax.ShapeDtypeStruct((s, KV_LORA), BF16)),
        in_specs=[row(IN_PAD), whole(qn), whole(kvn), whole(wq), whole(wkv), row(LANE), row(LANE)],
        out_specs=(row(hp), row(hp), row(N_HEADS * V_HEAD), row(Q_LORA), row(KV_LORA)),
        compiler_params=_params(("parallel",)),
    )(z, qn, kvn, wq, wkv, cos, sin)


def _qkv_bwd(dq, dk, dv, du, z, qn, kvn, wq, wkv, cos, sin, name):
    s = z.shape[0]
    ts = _tile(s, 256)

    def norm_bwd(x, g, dy):
        _, xn, r = _plain_rms(x, g)
        dxn = dy * g
        return r * (dxn - xn * jnp.mean(dxn * xn, axis=-1, keepdims=True)), jnp.sum(dy * xn, axis=0, keepdims=True)

    def body(dq_ref, dk_ref, dv_ref, du_ref, z_ref, qn_ref, kvn_ref, wq_ref, wkv_ref, cos_ref, sin_ref,
             dz_ref, dqb_ref, dkvb_ref, dqn_ref, dkvn_ref):
        @pl.when(pl.program_id(0) == 0)
        def _():
            dqn_ref[...] = jnp.zeros_like(dqn_ref)
            dkvn_ref[...] = jnp.zeros_like(dkvn_ref)

        cosv, sinv = cos_ref[...], sin_ref[...]
        dkr = jnp.zeros((ts, LANE), F32)
        for h in range(N_HEADS):
            o = h * HEAD_PAD
            dqb_ref[:, o:o + QK_NOPE] = dq_ref[:, o:o + QK_NOPE].astype(BF16)
            dqb_ref[:, o + QK_NOPE:o + HEAD_PAD] = _rope_t(dq_ref[:, o + QK_NOPE:o + HEAD_PAD], cosv, sinv).astype(BF16)
            dkvb_ref[:, o:o + QK_NOPE] = dk_ref[:, o:o + QK_NOPE].astype(BF16)
            dkvb_ref[:, o + QK_NOPE:o + HEAD_PAD] = dv_ref[:, h * V_HEAD:(h + 1) * V_HEAD].astype(BF16)
            dkr = dkr + dk_ref[:, o + QK_NOPE:o + HEAD_PAD]
        dcqn = jnp.dot(dqb_ref[...], wq_ref[...], preferred_element_type=F32)
        dckvn = jnp.dot(dkvb_ref[...], wkv_ref[...], preferred_element_type=F32)
        dcq, dqn = norm_bwd(z_ref[:, O_Q:O_KV], qn_ref[...], dcqn)
        dckv, dkvn = norm_bwd(z_ref[:, O_KV:O_KR], kvn_ref[...], dckvn)
        dqn_ref[...] += dqn
        dkvn_ref[...] += dkvn
        dz_ref[:, 0:O_Q] = du_ref[...].astype(BF16)
        dz_ref[:, O_Q:O_KV] = dcq.astype(BF16)
        dz_ref[:, O_KV:O_KR] = dckv.astype(BF16)
        dz_ref[:, O_KR:IN_PAD] = _rope_t(dkr, cosv, sinv).astype(BF16)

    def row(w):
        return pl.BlockSpec((ts, w), lambda i: (i, 0))

    def whole(arr):
        return pl.BlockSpec(arr.shape, lambda i: (0, 0))

    hp = N_HEADS * HEAD_PAD
    return pl.pallas_call(
        body, name=name, grid=(s // ts,),
        out_shape=(jax.ShapeDtypeStruct((s, IN_PAD), BF16), jax.ShapeDtypeStruct((s, hp), BF16),
                   jax.ShapeDtypeStruct((s, hp), BF16),
                   jax.ShapeDtypeStruct((1, Q_LORA), F32), jax.ShapeDtypeStruct((1, KV_LORA), F32)),
        in_specs=[row(hp), row(hp), row(N_HEADS * V_HEAD), row(POOL_WIDTH), row(IN_PAD),
                  whole(qn), whole(kvn), whole(wq), whole(wkv), row(LANE), row(LANE)],
        out_specs=(row(IN_PAD), row(hp), row(hp), whole(qn), whole(kvn)),
        compiler_params=_params(("arbitrary",)),
    )(dq, dk, dv, du, z, qn, kvn, wq, wkv, cos, sin)


def _causal_scores(q, k, i, tq, klen):
    sc = lax.dot_general(q, k, (((1,), (1,)), ((), ())), preferred_element_type=F32) * SOFTMAX_SCALE
    qpos = i * tq + lax.broadcasted_iota(jnp.int32, (tq, klen), 0)
    kpos = lax.broadcasted_iota(jnp.int32, (tq, klen), 1)
    return jnp.where(qpos >= kpos, sc, -jnp.inf)


ATTN_SEGMENTS = 4


def _by_key_prefix(i, nq, tq, compute):
    nseg = min(ATTN_SEGMENTS, nq)
    per = nq // nseg
    for r in range(nseg):
        pl.when(i // per == r)(lambda r=r: compute((r + 1) * per * tq))


def _attn_fwd(q, k, v, name):
    s = q.shape[0]
    tq = _tile(s, 256)
    nq = s // tq

    def body(q_ref, k_ref, v_ref, o_ref, lse_ref):
        i = pl.program_id(1)

        def compute(klen):
            sc = _causal_scores(q_ref[...], k_ref[0:klen, :], i, tq, klen)
            mx = jnp.max(sc, axis=-1, keepdims=True)
            p = jnp.exp(sc - mx)
            den = jnp.sum(p, axis=-1, keepdims=True)
            o_ref[...] = jnp.dot((p / den).astype(BF16), v_ref[0:klen, :], preferred_element_type=F32)
            lse_ref[...] = mx + jnp.log(den)

        _by_key_prefix(i, nq, tq, compute)

    return pl.pallas_call(
        body, name=name, grid=(N_HEADS, s // tq),
        out_shape=(jax.ShapeDtypeStruct((s, N_HEADS * V_HEAD), F32), jax.ShapeDtypeStruct((N_HEADS, s, 1), F32)),
        in_specs=[pl.BlockSpec((tq, HEAD_PAD), lambda h, i: (i, h)),
                  pl.BlockSpec((s, HEAD_PAD), lambda h, i: (0, h)),
                  pl.BlockSpec((s, V_HEAD), lambda h, i: (0, h))],
        out_specs=(pl.BlockSpec((tq, V_HEAD), lambda h, i: (i, h)),
                   pl.BlockSpec((None, tq, 1), lambda h, i: (h, i, 0))),
        compiler_params=_params(("parallel", "parallel")),
    )(q, k, v)


def _attn_bwd(q, k, v, lse, dycat, name):
    s = q.shape[0]
    tq = _tile(s, 256)
    nq = s // tq
    tn_dims = (((0,), (0,)), ((), ()))

    def body(q_ref, k_ref, v_ref, lse_ref, do_ref, dq_ref, dk_ref, dv_ref):
        i = pl.program_id(1)

        @pl.when(i == 0)
        def _():
            dk_ref[...] = jnp.zeros_like(dk_ref)
            dv_ref[...] = jnp.zeros_like(dv_ref)

        def compute(klen):
            qv, kv_, dob = q_ref[...], k_ref[0:klen, :], do_ref[...].astype(BF16)
            sc = _causal_scores(qv, kv_, i, tq, klen)
            p = jnp.exp(sc - lse_ref[...])
            dp = lax.dot_general(dob, v_ref[0:klen, :], (((1,), (1,)), ((), ())), preferred_element_type=F32)
            ds = (p * (dp - jnp.sum(dp * p, axis=-1, keepdims=True)) * SOFTMAX_SCALE).astype(BF16)
            dq_ref[...] = jnp.dot(ds, kv_, preferred_element_type=F32)
            dk_ref[0:klen, :] += lax.dot_general(ds, qv, tn_dims, preferred_element_type=F32)
            dv_ref[0:klen, :] += lax.dot_general(p.astype(BF16), dob, tn_dims, preferred_element_type=F32)

        _by_key_prefix(i, nq, tq, compute)

    n_pool_blocks = POOL_WIDTH // V_HEAD
    return pl.pallas_call(
        body, name=name, grid=(N_HEADS, s // tq),
        out_shape=(jax.ShapeDtypeStruct((s, N_HEADS * HEAD_PAD), F32),
                   jax.ShapeDtypeStruct((s, N_HEADS * HEAD_PAD), F32),
                   jax.ShapeDtypeStruct((s, N_HEADS * V_HEAD), F32)),
        in_specs=[pl.BlockSpec((tq, HEAD_PAD), lambda h, i: (i, h)),
                  pl.BlockSpec((s, HEAD_PAD), lambda h, i: (0, h)),
                  pl.BlockSpec((s, V_HEAD), lambda h, i: (0, h)),
                  pl.BlockSpec((None, tq, 1), lambda h, i: (h, i, 0)),
                  pl.BlockSpec((tq, V_HEAD), lambda h, i: (i, n_pool_blocks + h))],
        out_specs=(pl.BlockSpec((tq, HEAD_PAD), lambda h, i: (i, h)),
                   pl.BlockSpec((s, HEAD_PAD), lambda h, i: (0, h)),
                   pl.BlockSpec((s, V_HEAD), lambda h, i: (0, h))),
        compiler_params=_params(("parallel", "arbitrary")),
    )(q, k, v, lse, dycat)


def _loss_head(x, gw, target, name):
    s, d = x.shape
    ts = _tile(s, 256)

    def body(x_ref, gw_ref, tgt_ref, loss_ref, dx_ref, dgw_ref):
        @pl.when(pl.program_id(0) == 0)
        def _():
            loss_ref[...] = jnp.zeros_like(loss_ref)
            dgw_ref[...] = jnp.zeros_like(dgw_ref)

        xv, gwv = x_ref[...], gw_ref[...]
        r = lax.rsqrt(jnp.mean(xv * xv, axis=-1, keepdims=True) + EPS)
        xn = xv * r
        err = xn * gwv - tgt_ref[...]
        loss_ref[...] += 0.5 * jnp.sum(jnp.mean(err * err, axis=-1, keepdims=True))
        dy = err / d
        dgw_ref[...] += jnp.sum(dy * xn, axis=0, keepdims=True)
        dxn = dy * gwv
        dx_ref[...] = r * (dxn - xn * jnp.mean(dxn * xn, axis=-1, keepdims=True))

    row = pl.BlockSpec((ts, d), lambda i: (i, 0))
    return pl.pallas_call(
        body, name=name, grid=(s // ts,),
        out_shape=(jax.ShapeDtypeStruct((8, LANE), F32), jax.ShapeDtypeStruct((s, d), F32),
                   jax.ShapeDtypeStruct((1, d), F32)),
        in_specs=[row, _vec_spec(d), row],
        out_specs=(pl.BlockSpec((8, LANE), lambda i: (0, 0)), row, _vec_spec(d)),
        compiler_params=_params(("arbitrary",)),
    )(x, gw, target)


def _ada_mod(c_all, ada_w, ada_b, name):
    nl, d, cols = ada_w.shape

    def body(c_ref, w_ref, b_ref, o_ref):
        cv = c_ref[...]
        act = (cv * jax.nn.sigmoid(cv)).astype(BF16)
        o_ref[...] = jnp.dot(act, w_ref[...].astype(BF16), preferred_element_type=F32) + b_ref[...]

    return pl.pallas_call(
        body, name=name, grid=(nl,), out_shape=jax.ShapeDtypeStruct((nl, N_DEV, cols), F32),
        in_specs=[pl.BlockSpec((N_DEV, d), lambda l: (0, 0)),
                  pl.BlockSpec((None, d, cols), lambda l: (l, 0, 0)),
                  pl.BlockSpec((None, 1, cols), lambda l: (l, 0, 0))],
        out_specs=pl.BlockSpec((None, N_DEV, cols), lambda l: (l, 0, 0)),
        compiler_params=_params(("parallel",)),
    )(c_all, ada_w, ada_b)


def _ada_grad(c_pad, dmod_pad, name):
    nl, kpad, cols = dmod_pad.shape
    d = c_pad.shape[1]

    def body(c_ref, dm_ref, o_ref):
        cv = c_ref[...]
        act = (cv * jax.nn.sigmoid(cv)).astype(BF16)
        o_ref[...] = lax.dot_general(act, dm_ref[...].astype(BF16), (((0,), (0,)), ((), ())),
                                     preferred_element_type=F32)

    return pl.pallas_call(
        body, name=name, grid=(nl,), out_shape=jax.ShapeDtypeStruct((nl, d, cols), F32),
        in_specs=[pl.BlockSpec((kpad, d), lambda l: (0, 0)),
                  pl.BlockSpec((None, kpad, cols), lambda l: (l, 0, 0))],
        out_specs=pl.BlockSpec((None, d, cols), lambda l: (l, 0, 0)),
        compiler_params=_params(("parallel",)),
    )(c_pad, dmod_pad)


def _adamw_math(w, g, m, v):
    nm = ADAM_B1 * m + (1.0 - ADAM_B1) * g
    nv = ADAM_B2 * v + (1.0 - ADAM_B2) * (g * g)
    m_hat = nm / (1.0 - ADAM_B1 ** ADAM_STEP)
    v_hat = nv / (1.0 - ADAM_B2 ** ADAM_STEP)
    return -ADAM_LR * (m_hat / (jnp.sqrt(v_hat) + ADAM_EPS) + ADAM_WD * w), nm, nv


def _adamw_rows(w3, gbuf, row_off, m3, v3, name):
    nl, r, d = w3.shape
    tr = _row_tile(math.gcd(r, row_off) if row_off else r, 176)
    first = row_off // tr

    def body(w_ref, g_ref, m_ref, v_ref, go_ref, d_ref, nm_ref, nv_ref):
        gv = g_ref[...]
        go_ref[...] = gv
        d_ref[...], nm_ref[...], nv_ref[...] = _adamw_math(w_ref[...], gv, m_ref[...], v_ref[...])

    blk = pl.BlockSpec((None, tr, d), lambda l, i: (l, i, 0))
    gblk = pl.BlockSpec((None, tr, d), lambda l, i: (l, first + i, 0))
    out = jax.ShapeDtypeStruct((nl, r, d), F32)
    return pl.pallas_call(
        body, name=name, grid=(nl, r // tr), out_shape=(out, out, out, out),
        in_specs=[blk, gblk, blk, blk], out_specs=(blk, blk, blk, blk),
        compiler_params=_params(("parallel", "parallel")),
    )(w3, gbuf, m3, v3)


def _adamw(w, g, m, v, name):
    rows, cols = w.shape
    tr = _row_tile(rows, 256)

    def body(w_ref, g_ref, m_ref, v_ref, d_ref, nm_ref, nv_ref):
        d_ref[...], nm_ref[...], nv_ref[...] = _adamw_math(w_ref[...], g_ref[...], m_ref[...], v_ref[...])

    blk = pl.BlockSpec((tr, cols), lambda i: (i, 0))
    out = jax.ShapeDtypeStruct((rows, cols), F32)
    return pl.pallas_call(
        body, name=name, grid=(rows // tr,), out_shape=(out, out, out),
        in_specs=[blk, blk, blk, blk], out_specs=(blk, blk, blk),
        compiler_params=_params(("parallel",)),
    )(w, g, m, v)


def _adamw_nd(w, g, m, v, name):
    shape = w.shape
    flat = (lambda t: t.reshape(1, -1)) if w.ndim == 1 else (lambda t: t.reshape(-1, shape[-1]))
    return tuple(t.reshape(shape) for t in _adamw(flat(w), flat(g), flat(m), flat(v), name))


def _pad_rows(t, rows):
    return jnp.pad(t, ((0, rows - t.shape[0]), (0, 0)))


def _pack_shard_layer(l, wts):
    def tr(name):
        return wts[name][l].astype(BF16).T

    parts = [tr("ffn1_w_gate"), tr("ffn1_w_up"), wts["ffn1_w_down"][l].astype(BF16),
             tr("ffn2_w_gate"), tr("ffn2_w_up"), wts["ffn2_w_down"][l].astype(BF16),
             wts["w_out"][l].astype(BF16),
             tr("w_kv_b").reshape(KV_SH_ROWS, D_MODEL),
             _pad_rows(tr("w_in"), 160),
             _pad_rows(tr("w_q_b").reshape(Q_SH_ROWS, D_MODEL), Q_PAD_ROWS)]
    return jnp.concatenate(parts, axis=0)


def _full_weights(lands):
    w = dict(zip(("g1", "u1", "d1", "g2", "u2", "d2", "out"), lands))
    small = lands[-1].reshape(N_DEV, SMALL_ROWS, D_MODEL)
    o_in, o_q = OFF_IN - OFF_KV, OFF_Q - OFF_KV
    w["kv"] = small[:, :KV_SH_ROWS].reshape(N_HEADS * HEAD_PAD, KV_LORA)
    w["in"] = _pad_rows(small[:, o_in:o_in + IN_SH].reshape(IN_COLS, D_MODEL), IN_PAD)
    wq = small[:, o_q:o_q + Q_SH_ROWS].reshape(N_HEADS, QK_HEAD, Q_LORA)
    w["q"] = jnp.pad(wq, ((0, 0), (0, HEAD_PAD - QK_HEAD), (0, 0))).reshape(N_HEADS * HEAD_PAD, Q_LORA)
    return w


def _grad_sources_b(gr):
    gq = gr["q"].reshape(N_HEADS, HEAD_PAD, Q_LORA)[:, :QK_HEAD].reshape(N_DEV, Q_SH_ROWS, D_MODEL)
    small = jnp.concatenate([
        gr["kv"].reshape(N_DEV, KV_SH_ROWS, D_MODEL),
        jnp.pad(gr["in"][:IN_COLS].reshape(N_DEV, IN_SH, D_MODEL), ((0, 0), (0, 160 - IN_SH), (0, 0))),
        jnp.pad(gq, ((0, 0), (0, Q_PAD_ROWS - Q_SH_ROWS), (0, 0)))], axis=1)
    return [gr["g2"], gr["u2"], gr["d2"], gr["out"], small.reshape(N_DEV * SMALL_ROWS, D_MODEL)]


def _small_layout(nl):
    names = [("dmod", nl * N_MOD), ("ffn1_norm", nl), ("mix_norm", nl), ("ffn2_norm", nl), ("q_a_norm", nl),
             ("kv_a_norm", nl), ("pool_scale", nl), ("final_norm", 1), ("loss", 1),
             ("pool_w", nl * 4 * POOL_GC * POOL_GC // D_MODEL)]
    off, table = 0, {}
    for name, n in names:
        table[name] = (off, n)
        off += -(-n // 8) * 8
    return table, off


def _to_rows(t, width=D_MODEL):
    n, w = t.shape
    return jnp.pad(t, ((0, -(-n // 8) * 8 - n), (0, width - w)))


def kernel(x, c, positions, ada_w, ada_b, ffn1_norm, ffn1_w_gate, ffn1_w_up, ffn1_w_down, mix_norm, w_in, pool_w, pool_scale, q_a_norm, w_q_b, kv_a_norm, w_kv_b, w_out, ffn2_norm, ffn2_w_gate, ffn2_w_up, ffn2_w_down, final_norm, loss_target, m_ada_w, m_ada_b, m_ffn1_norm, m_ffn1_w_gate, m_ffn1_w_up, m_ffn1_w_down, m_mix_norm, m_w_in, m_pool_w, m_pool_scale, m_q_a_norm, m_w_q_b, m_kv_a_norm, m_w_kv_b, m_w_out, m_ffn2_norm, m_ffn2_w_gate, m_ffn2_w_up, m_ffn2_w_down, m_final_norm, v_ada_w, v_ada_b, v_ffn1_norm, v_ffn1_w_gate, v_ffn1_w_up, v_ffn1_w_down, v_mix_norm, v_w_in, v_pool_w, v_pool_scale, v_q_a_norm, v_w_q_b, v_kv_a_norm, v_w_kv_b, v_w_out, v_ffn2_norm, v_ffn2_w_gate, v_ffn2_w_up, v_ffn2_w_down, v_final_norm):
    wts = dict(ada_w=ada_w, ada_b=ada_b, ffn1_norm=ffn1_norm, ffn1_w_gate=ffn1_w_gate, ffn1_w_up=ffn1_w_up,
               ffn1_w_down=ffn1_w_down, mix_norm=mix_norm, w_in=w_in, pool_w=pool_w, pool_scale=pool_scale,
               q_a_norm=q_a_norm, w_q_b=w_q_b, kv_a_norm=kv_a_norm, w_kv_b=w_kv_b, w_out=w_out,
               ffn2_norm=ffn2_norm, ffn2_w_gate=ffn2_w_gate, ffn2_w_up=ffn2_w_up, ffn2_w_down=ffn2_w_down,
               final_norm=final_norm)
    mom_m = dict(ada_w=m_ada_w, ada_b=m_ada_b, ffn1_norm=m_ffn1_norm, ffn1_w_gate=m_ffn1_w_gate,
                 ffn1_w_up=m_ffn1_w_up, ffn1_w_down=m_ffn1_w_down, mix_norm=m_mix_norm, w_in=m_w_in,
                 pool_w=m_pool_w, pool_scale=m_pool_scale, q_a_norm=m_q_a_norm, w_q_b=m_w_q_b,
                 kv_a_norm=m_kv_a_norm, w_kv_b=m_w_kv_b, w_out=m_w_out, ffn2_norm=m_ffn2_norm,
                 ffn2_w_gate=m_ffn2_w_gate, ffn2_w_up=m_ffn2_w_up, ffn2_w_down=m_ffn2_w_down,
                 final_norm=m_final_norm)
    mom_v = dict(ada_w=v_ada_w, ada_b=v_ada_b, ffn1_norm=v_ffn1_norm, ffn1_w_gate=v_ffn1_w_gate,
                 ffn1_w_up=v_ffn1_w_up, ffn1_w_down=v_ffn1_w_down, mix_norm=v_mix_norm, w_in=v_w_in,
                 pool_w=v_pool_w, pool_scale=v_pool_scale, q_a_norm=v_q_a_norm, w_q_b=v_w_q_b,
                 kv_a_norm=v_kv_a_norm, w_kv_b=v_w_kv_b, w_out=v_w_out, ffn2_norm=v_ffn2_norm,
                 ffn2_w_gate=v_ffn2_w_gate, ffn2_w_up=v_ffn2_w_up, ffn2_w_down=v_ffn2_w_down,
                 final_norm=v_final_norm)
    order = list(wts)
    nl = ada_w.shape[0]
    seq = x.shape[1]
    me = 4 * lax.axis_index("x") + 2 * lax.axis_index("y") + lax.axis_index("c")
    ada_cols = ada_w.shape[2]

    def after_token(t, token):
        return t + token[0:1, 0:1].astype(t.dtype)

    packs = [_pack_shard_layer(l, wts) for l in range(nl)]

    c_all = _all_gather(jnp.broadcast_to(c, (8, D_MODEL)), "gather_c")[::8]

    ada_b_mine = lax.dynamic_slice_in_dim(ada_b, me * ada_cols, ada_cols, axis=1).reshape(nl, 1, ada_cols)
    mod_part = _ada_mod(c_all, ada_w, ada_b_mine, "ada_mod")
    mod_all = _all_gather(mod_part.reshape(nl * N_DEV, ada_cols), "gather_mod")
    mod_all = mod_all.reshape(N_DEV, nl, N_DEV, ada_cols)
    mod = lax.dynamic_index_in_dim(mod_all, me, axis=2, keepdims=False)
    mod = mod.transpose(1, 0, 2).reshape(nl, N_MOD, 1, D_MODEL)

    flight_a = _gather_start(packs[0][:SPLIT_AB], ROWS_A, mod, "gather_start_0a")
    flight_b = _gather_start(packs[0][SPLIT_AB:], ROWS_B, flight_a[4], "gather_start_0b")
    last_start = flight_b[4]
    if nl > 1:
        in_flight = _gather_start(packs[1], ROWS_ALL, last_start, "gather_start_1")
        last_start = in_flight[4]

    cos, sin = _rope_tables(after_token(positions.reshape(seq, 1), last_start), "rope_tables")

    def vec(t):
        return t.reshape(1, -1)

    def landed(flight, rows_list, after, tag):
        send_sems, recv_sems, pk, lands, _ = flight
        pk, lands = _gather_wait(send_sems, recv_sems, pk, lands, after, f"gather_wait_{tag}")
        return _gather_finish(pk, rows_list, lands, "gather_finish")

    xs = x.reshape(seq, D_MODEL)
    saved = []
    for l in range(nl):
        norm1 = vec(ffn1_norm[l])
        if l == 0:
            lands = landed(flight_a, ROWS_A, cos, "0a")
        elif l + 1 < nl:
            in_flight = _gather_start(packs[l + 1], ROWS_ALL, lands[0], f"gather_start_{l + 1}")
            norm1 = after_token(norm1, in_flight[4])
        sv = {}

        def ffn_fwd(xin, norm, k0, wg, wu, wd, tag):
            h, a, b, t = _ffn_up(xin, norm, mod[l, k0], mod[l, k0 + 1], wg, wu, "ffn_up")
            y, xout = _mm(t, wd, "nn", "ffn_down", res=xin, gate=mod[l, k0 + 2], gate_factor=0.5, tm=1024, tn=512)
            sv[tag] = dict(x=xin, h=h, a=a, b=b, t=t, y=y)
            return xout

        xs = ffn_fwd(xs, norm1, 0, lands[0], lands[1], lands[2], "f1")
        if l == 0:
            lands = lands + landed(flight_b, ROWS_B, xs, "0b")
        w = _full_weights(lands)
        sv["w"] = w

        h2 = _rm_fwd(xs, vec(mix_norm[l]), mod[l, 3], mod[l, 4], "rm_fwd")
        z = _mm(h2, w["in"], "nt", "mix_in")
        y_pool, diff = _pool_fwd(z, pool_w[l], vec(pool_scale[l]), "pool_fwd")
        q, k, v, cqn, ckvn = _qkv_fwd(z, vec(q_a_norm[l]), vec(kv_a_norm[l]), w["q"], w["kv"], cos, sin, "qkv_fwd")
        o, lse = _attn_fwd(q, k, v, "attn_fwd")
        ycat = jnp.concatenate([y_pool, o.astype(BF16)], axis=1)
        y2, xmix = _mm(ycat, w["out"], "nn", "mix_out", res=xs, gate=mod[l, 5], gate_factor=1.0)
        sv["mix"] = dict(x=xs, h=h2, z=z, diff=diff, q=q, k=k, v=v, cqn=cqn, ckvn=ckvn, lse=lse, ycat=ycat, y=y2)
        xs = xmix

        xs = ffn_fwd(xs, vec(ffn2_norm[l]), 6, w["g2"], w["u2"], w["d2"], "f2")
        saved.append(sv)
        if l + 1 < nl:
            lands = landed(in_flight, ROWS_ALL, xs, l + 1)

    loss_part, dx, d_final = _loss_head(xs, vec(final_norm), loss_target.reshape(seq, D_MODEL), "loss_head")

    small = {name: [None] * nl for name in ("ffn1_norm", "mix_norm", "ffn2_norm", "q_a_norm", "kv_a_norm",
                                            "pool_scale", "pool_w", "dmod")}
    core = lax.axis_index("c").astype(jnp.int32).reshape(1)
    chip = 2 * lax.axis_index("x") + lax.axis_index("y")
    exchanges = []

    def leave(srcs, rows_list, after, tag):
        return _pair_start(srcs, rows_list, after, f"pair_start_{tag}"), rows_list, tag

    def forward_on(pending, after, layer, row_off):
        (send_sems, recv_sems, srcs, land, _), rows_list, tag = pending
        srcs, land = _split_wait(send_sems, recv_sems, 1, srcs, land, after, f"pair_wait_{tag}")
        sums = _pair_sum(srcs, rows_list, land, core, "pair_sum")
        flight = _chip_exchange_start(sums, chip, after, f"exchange_start_{tag}")
        exchanges.append((flight, layer, row_off, tag))
        return flight[4]

    pending = None
    for l in reversed(range(nl)):
        sv = saved[l]
        w = sv["w"]
        dmod = [None] * N_MOD
        gr = {}
        gate3 = mod[l, 8] if pending is None else after_token(mod[l, 8], pending[0][4])

        def ffn_bwd(dxin, s_, norm, k0, wg, wu, wd, tag, gate, mid=None):
            dy, dmod[k0 + 2] = _gate_bwd(dxin, s_["y"], gate, 0.5, "gate_bwd")
            da, db, gr["d" + tag], gr["g" + tag], gr["u" + tag] = _ffn_bwd_cols(
                dy, s_["h"], s_["a"], s_["b"], s_["t"], wd, "ffn_bwd_cols")
            dh = _mm_pair(da, wg, db, wu, "ffn_bwd_dh", after=None if mid is None else mid(da))
            dxo, dmod[k0], dmod[k0 + 1], dnorm = _rm_bwd(dh, s_["x"], dxin, vec(norm), mod[l, k0 + 1], "rm_bwd")
            return dxo, dnorm

        dx, small["ffn2_norm"][l] = ffn_bwd(dx, sv["f2"], ffn2_norm[l], 6, w["g2"], w["u2"], w["d2"], "2", gate3)

        s_ = sv["mix"]
        gate2 = mod[l, 5]
        if pending is not None:
            gate2 = after_token(gate2, forward_on(pending, dx, l + 1, 0))
            pending = None
        dy, dmod[5] = _gate_bwd(dx, s_["y"], gate2, 1.0, "gate_bwd")
        gr["out"] = _mm(s_["ycat"], dy, "tn", "mix_out_dw", out_dtype=BF16, tm=256)
        dycat = _mm(dy, w["out"], "nt", "mix_out_dx")
        du, small["pool_w"][l], small["pool_scale"][l] = _pool_bwd(dycat, s_["diff"], pool_w[l], vec(pool_scale[l]), "pool_bwd")
        dq, dk, dv = _attn_bwd(s_["q"], s_["k"], s_["v"], s_["lse"], dycat, "attn_bwd")
        dz, dqb, dkvb, small["q_a_norm"][l], small["kv_a_norm"][l] = _qkv_bwd(
            dq, dk, dv, du, s_["z"], vec(q_a_norm[l]), vec(kv_a_norm[l]), w["q"], w["kv"], cos, sin, "qkv_bwd")
        gr["q"] = _mm(dqb, s_["cqn"], "tn", "q_b_dw", out_dtype=BF16, tm=256)
        gr["kv"] = _mm(dkvb, s_["ckvn"], "tn", "kv_b_dw", out_dtype=BF16, tm=256)
        gr["in"] = _mm(dz, s_["h"], "tn", "mix_in_dw", out_dtype=BF16, tm=256)
        dh2 = _mm(dz, w["in"], "nn", "mix_in_dx")
        dx, dmod[3], dmod[4], small["mix_norm"][l] = _rm_bwd(dh2, s_["x"], dx, vec(mix_norm[l]), mod[l, 4], "rm_bwd")

        gate1, mid = mod[l, 2], None
        if l == 0:
            pending_b = leave(_grad_sources_b(gr), ROWS_B, dx, "0b")
            gate1 = after_token(gate1, pending_b[0][4])
            mid = lambda da: forward_on(pending_b, da, 0, SPLIT_AB)
        dx, small["ffn1_norm"][l] = ffn_bwd(dx, sv["f1"], ffn1_norm[l], 0, w["g1"], w["u1"], w["d1"], "1", gate1, mid)

        small["dmod"][l] = jnp.concatenate(dmod, axis=0)
        if l > 0:
            pending = leave([gr["g1"], gr["u1"], gr["d1"]] + _grad_sources_b(gr), ROWS_ALL, dx, l)

    grad_x = dx.reshape(x.shape)
    pending_a = leave([gr["g1"], gr["u1"], gr["d1"]], ROWS_A, dx, "0a")

    layout, small_rows = _small_layout(nl)
    pieces = {
        "dmod": jnp.concatenate(small["dmod"], axis=0),
        "ffn1_norm": jnp.concatenate(small["ffn1_norm"], axis=0),
        "mix_norm": jnp.concatenate(small["mix_norm"], axis=0),
        "ffn2_norm": jnp.concatenate(small["ffn2_norm"], axis=0),
        "q_a_norm": jnp.concatenate(small["q_a_norm"], axis=0),
        "kv_a_norm": jnp.concatenate(small["kv_a_norm"], axis=0),
        "pool_scale": jnp.concatenate(small["pool_scale"], axis=0),
        "final_norm": d_final,
        "loss": jnp.broadcast_to(loss_part[0:1, 0:1], (1, D_MODEL)),
        "pool_w": jnp.stack(small["pool_w"]).reshape(-1, D_MODEL),
    }
    small_buf = jnp.concatenate([_to_rows(pieces[name]) for name in layout], axis=0)
    small_buf = after_token(small_buf, pending_a[0][4])
    small_all = _all_gather(small_buf, "gather_small").reshape(N_DEV, small_rows, D_MODEL)
    token_0a = forward_on(pending_a, small_all, 0, 0)
    small_sum = _sum_slots(small_all, "sum_small", after=token_0a)

    def take(name, width=D_MODEL):
        off, n = layout[name]
        return small_sum[off:off + n, :width]

    grads = {}
    grads["ada_b"] = take("dmod").reshape(nl, N_MOD * D_MODEL)
    grads["ffn1_norm"], grads["mix_norm"], grads["ffn2_norm"] = take("ffn1_norm"), take("mix_norm"), take("ffn2_norm")
    grads["q_a_norm"], grads["kv_a_norm"] = take("q_a_norm", Q_LORA), take("kv_a_norm", KV_LORA)
    grads["pool_scale"] = take("pool_scale", POOL_WIDTH)
    grads["final_norm"] = take("final_norm").reshape(D_MODEL)
    grads["pool_w"] = take("pool_w").reshape(pool_w.shape)
    loss = take("loss")[0, 0]

    off, n = layout["dmod"]
    dmod_all = small_all[:, off:off + n].reshape(N_DEV, nl, N_MOD * D_MODEL)
    dmod_mine = lax.dynamic_slice_in_dim(dmod_all, me * ada_cols, ada_cols, axis=2)
    dmod_pad = jnp.pad(dmod_mine.transpose(1, 0, 2), ((0, 0), (0, LANE - N_DEV), (0, 0)))
    grads["ada_w"] = _ada_grad(jnp.pad(c_all, ((0, LANE - N_DEV), (0, 0))), dmod_pad, "ada_grad")

    updates = {name: _adamw_nd(wts[name], grads[name], mom_m[name], mom_v[name], "adamw") for name in grads}

    gbuf = lax.empty((nl, ROWS_L, D_MODEL), F32)
    for (send_sems, recv_sems, sums, recv, _), layer, row_off, tag in exchanges:
        after = token_0a if layer > 0 else updates["ada_w"][0]
        _, recv = _split_wait(send_sems, recv_sems, N_CHIPS - 1, sums, recv, after, f"exchange_wait_{tag}")
        gbuf = _sum_slots_into(recv, gbuf, layer, row_off, "sum_grads")

    def swap(t):
        return t.transpose(0, 2, 1)

    def same(t):
        return t

    for wname, off, view in (("ffn1_w_gate", OFF_G1, swap), ("ffn1_w_up", OFF_U1, swap), ("ffn1_w_down", OFF_D1, same),
                             ("ffn2_w_gate", OFF_G2, swap), ("ffn2_w_up", OFF_U2, swap), ("ffn2_w_down", OFF_D2, same),
                             ("w_out", OFF_OUT, same)):
        g, d_, nm, nv = _adamw_rows(view(wts[wname]), gbuf, off, view(mom_m[wname]), view(mom_v[wname]), "adamw_rows")
        grads[wname], updates[wname] = view(g), (view(d_), view(nm), view(nv))
    small_grads = {
        "w_kv_b": (gbuf[:, OFF_KV:OFF_KV + KV_SH_ROWS].reshape(nl, -1, KV_LORA).transpose(0, 2, 1), same),
        "w_in": (gbuf[:, OFF_IN:OFF_IN + IN_SH], swap),
        "w_q_b": (gbuf[:, OFF_Q:OFF_Q + Q_SH_ROWS].reshape(nl, -1, Q_LORA), swap),
    }
    for wname, (g, view) in small_grads.items():
        upd = _adamw_nd(view(wts[wname]), g, view(mom_m[wname]), view(mom_v[wname]), "adamw")
        grads[wname], updates[wname] = view(g), tuple(view(t) for t in upd)

    return (loss, grad_x, *[grads[n] for n in order], *[updates[n][0] for n in order],
            *[updates[n][1] for n in order], *[updates[n][2] for n in order])
```

```python
import math

import numpy as np
import jax
import jax.numpy as jnp
from jax import lax
from jax.experimental import pallas as pl
from jax.experimental.pallas import tpu as pltpu

F32 = jnp.float32
BF16 = jnp.bfloat16

N_DEV = 8
D_MODEL = 1024
D_FF = 2816
POOL_WIDTH = 512
POOL_WINDOWS = (2, 4, 8, 16)
POOL_GC = 128
N_HEADS = 4
QK_NOPE = 128
QK_ROPE = 64
V_HEAD = 128
QK_HEAD = QK_NOPE + QK_ROPE
HEAD_PAD = 256
Q_LORA = 384
KV_LORA = 256
IN_COLS = POOL_WIDTH + Q_LORA + KV_LORA + QK_ROPE
IN_PAD = 1280
ROPE_THETA = 10000.0
SOFTMAX_SCALE = 1.0 / math.sqrt(QK_HEAD)
EPS = 1e-6
N_MOD = 9

ADAM_LR = 0.001
ADAM_B1 = 0.9
ADAM_B2 = 0.999
ADAM_EPS = 1e-08
ADAM_WD = 0.01
ADAM_STEP = 10

LANE = 128
VMEM_LIMIT = 56 * 1024 * 1024

FF_SH = D_FF // N_DEV
OFF_G1, OFF_U1, OFF_D1 = 0, FF_SH, 2 * FF_SH
OFF_G2, OFF_U2, OFF_D2 = 3 * FF_SH, 4 * FF_SH, 5 * FF_SH
OFF_OUT = 6 * FF_SH
OFF_KV = OFF_OUT + 128
OFF_IN = OFF_KV + 32
OFF_Q = OFF_IN + 160
Q_PAD_ROWS = 64
ROWS_L = OFF_Q + Q_PAD_ROWS
IN_SH = IN_COLS // N_DEV
Q_SH_ROWS = (N_HEADS * QK_HEAD // N_DEV) * Q_LORA // D_MODEL
KV_SH_ROWS = (N_HEADS * (QK_NOPE + V_HEAD) // N_DEV) * KV_LORA // D_MODEL


def _tile(dim, target):
    if dim <= target:
        return dim
    best = None
    for t in range(LANE, target + 1, LANE):
        if dim % t == 0:
            best = t
    assert best is not None, (dim, target)
    return best


def _params(sem):
    return pltpu.CompilerParams(dimension_semantics=sem, vmem_limit_bytes=VMEM_LIMIT)


def _mesh_pos():
    return lax.axis_index("x"), lax.axis_index("y"), lax.axis_index("c")


def _all_gather(x, name, after=None):
    m, n = x.shape

    def body(x_ref, *refs):
        out_ref, send_sems, recv_sems, local_sem = refs[-4:]
        px, py, pc = _mesh_pos()
        me, sibling = (px, py, pc), (px, py, 1 - pc)
        chips = [(1 - px, py), (px, 1 - py), (1 - px, 1 - py)]

        def rows(bx, by, bc):
            return out_ref.at[pl.ds((4 * bx + 2 * by + bc) * m, m), :]

        def copy(k, block, to, src=None):
            return pltpu.make_async_remote_copy(
                src_ref=rows(*block) if src is None else src, dst_ref=rows(*block),
                send_sem=send_sems.at[k], recv_sem=recv_sems.at[k],
                device_id=to, device_id_type=pl.DeviceIdType.MESH)

        mine = pltpu.make_async_copy(x_ref, rows(*me), local_sem)
        mine.start()
        first = [copy(0, me, sibling, src=x_ref)]
        first += [copy(1 + j, me, (*chip, pc), src=x_ref) for j, chip in enumerate(chips)]
        for cp in first:
            cp.start()
        passed = [copy(4 + j, (*chip, pc), sibling) for j, chip in enumerate(chips)]
        for j, chip in enumerate(chips):
            copy(1 + j, (*chip, pc), me).wait_recv()
            passed[j].start()
        copy(0, sibling, me).wait_recv()
        for j, chip in enumerate(chips):
            copy(4 + j, (*chip, 1 - pc), me).wait_recv()
        for cp in first + passed:
            cp.wait_send()
        mine.wait()

    hbm = pl.BlockSpec(memory_space=pltpu.HBM)
    in_specs, args = [hbm], [x]
    if after is not None:
        in_specs.append(pl.BlockSpec(memory_space=pl.ANY))
        args.append(after)
    return pl.pallas_call(
        body, name=name,
        out_shape=jax.ShapeDtypeStruct((N_DEV * m, n), x.dtype),
        in_specs=in_specs, out_specs=hbm,
        scratch_shapes=[pltpu.SemaphoreType.DMA((7,)), pltpu.SemaphoreType.DMA((7,)),
                        pltpu.SemaphoreType.DMA],
    )(*args)


SMALL_ROWS = ROWS_L - OFF_KV
ROWS_A = [FF_SH] * 3
ROWS_B = [FF_SH] * 3 + [128, SMALL_ROWS]
ROWS_ALL = ROWS_A + ROWS_B
SPLIT_AB = sum(ROWS_A)
HBM_SPEC = pl.BlockSpec(memory_space=pltpu.HBM)
SEM_SPEC = pl.BlockSpec(memory_space=pltpu.SEMAPHORE)
ANY_SPEC = pl.BlockSpec(memory_space=pl.ANY)
EFFECT = pltpu.SideEffectType.DATAFLOW_SIDE_EFFECTING


def _hbm(t):
    return pltpu.with_memory_space_constraint(t, pltpu.HBM)


def _whole_wait(ref, send_sem, recv_sem, peer):
    return pltpu.make_async_remote_copy(src_ref=ref, dst_ref=ref, send_sem=send_sem, recv_sem=recv_sem,
                                        device_id=peer, device_id_type=pl.DeviceIdType.MESH)


def _offsets(rows_list):
    return [sum(rows_list[:i]) for i in range(len(rows_list))]


def _gather_start(packed, rows_list, after, name):
    n = len(rows_list)
    offs = _offsets(rows_list)
    lands = [_hbm(lax.empty((N_DEV * rows, D_MODEL), BF16)) for rows in rows_list]

    def body(packed_ref, *refs):
        land = refs[:n]
        send_sems, recv_sems = refs[n + 1], refs[n + 2]
        token = refs[-1]
        px, py, pc = _mesh_pos()
        me = 4 * px + 2 * py + pc
        peers = [(px, py, 1 - pc), (1 - px, py, pc), (px, 1 - py, pc), (1 - px, 1 - py, pc)]
        for k, peer in enumerate(peers):
            for off, rows, land_ref in zip(offs, rows_list, land):
                pltpu.make_async_remote_copy(
                    src_ref=packed_ref.at[pl.ds(off, rows), :], dst_ref=land_ref.at[pl.ds(me * rows, rows), :],
                    send_sem=send_sems.at[k], recv_sem=recv_sems.at[k],
                    device_id=peer, device_id_type=pl.DeviceIdType.MESH).start()
        token[...] = jnp.zeros_like(token)

    outs = pl.pallas_call(
        body, name=name,
        out_shape=(pltpu.SemaphoreType.DMA((4,)), pltpu.SemaphoreType.DMA((4,)), pltpu.HBM(packed.shape, BF16),
                   *[pltpu.HBM(t.shape, BF16) for t in lands], jax.ShapeDtypeStruct((8, LANE), F32)),
        in_specs=(HBM_SPEC,) * (1 + n) + (ANY_SPEC,),
        out_specs=(SEM_SPEC, SEM_SPEC) + (HBM_SPEC,) * (1 + n) + (pl.BlockSpec(memory_space=pltpu.VMEM),),
        input_output_aliases={i: 2 + i for i in range(1 + n)},
        compiler_params=pltpu.CompilerParams(has_side_effects=EFFECT),
    )(_hbm(packed), *lands, after)
    return outs[0], outs[1], outs[2], list(outs[3:3 + n]), outs[-1]


def _gather_wait(send_sems, recv_sems, packed, lands, after, name):
    n = len(lands)

    def body(packed_ref, *refs):
        s_sems, r_sems = refs[n], refs[n + 1]
        me = _mesh_pos()
        for k in range(4):
            cp = _whole_wait(packed_ref, s_sems.at[k], r_sems.at[k], me)
            cp.wait_send()
            cp.wait_recv()

    outs = pl.pallas_call(
        body, name=name,
        out_shape=(pltpu.HBM(packed.shape, BF16), *[pltpu.HBM(t.shape, BF16) for t in lands]),
        in_specs=(HBM_SPEC,) * (1 + n) + (SEM_SPEC, SEM_SPEC, ANY_SPEC),
        out_specs=(HBM_SPEC,) * (1 + n),
        input_output_aliases={i: i for i in range(1 + n)},
        compiler_params=pltpu.CompilerParams(has_side_effects=EFFECT),
    )(packed, *lands, send_sems, recv_sems, after)
    return outs[0], list(outs[1:])


def _gather_finish(packed, rows_list, lands, name):
    n = len(rows_list)
    offs = _offsets(rows_list)

    def body(packed_ref, *refs):
        land = refs[n:2 * n]
        send_sems, recv_sems, stage, stage_sem = refs[2 * n:]
        px, py, pc = _mesh_pos()
        me = 4 * px + 2 * py + pc
        sibling = (px, py, 1 - pc)
        load = pltpu.make_async_copy(packed_ref, stage, stage_sem)
        load.start()
        load.wait()
        for off, rows, land_ref in zip(offs, rows_list, land):
            pltpu.make_async_copy(stage.at[pl.ds(off, rows), :], land_ref.at[pl.ds(me * rows, rows), :],
                                  stage_sem).start()
        for j, (cx, cy) in enumerate([(1 - px, py), (px, 1 - py), (1 - px, 1 - py)]):
            block = 4 * cx + 2 * cy + pc
            for rows, land_ref in zip(rows_list, land):
                blk = land_ref.at[pl.ds(block * rows, rows), :]
                pltpu.make_async_remote_copy(src_ref=blk, dst_ref=blk, send_sem=send_sems.at[j],
                                             recv_sem=recv_sems.at[j], device_id=sibling,
                                             device_id_type=pl.DeviceIdType.MESH).start()
        for j in range(3):
            cp = _whole_wait(packed_ref, send_sems.at[j], recv_sems.at[j], sibling)
            cp.wait_recv()
            cp.wait_send()
        pltpu.make_async_copy(stage, packed_ref, stage_sem).wait()

    outs = pl.pallas_call(
        body, name=name,
        out_shape=tuple(jax.ShapeDtypeStruct(t.shape, BF16) for t in lands),
        in_specs=(HBM_SPEC,) * (1 + n), out_specs=(HBM_SPEC,) * n,
        input_output_aliases={1 + i: i for i in range(n)},
        scratch_shapes=[pltpu.SemaphoreType.DMA((3,)), pltpu.SemaphoreType.DMA((3,)),
                        pltpu.VMEM(packed.shape, BF16), pltpu.SemaphoreType.DMA],
    )(packed, *lands)
    return list(outs)


N_CHIPS = 4


def _pair_start(srcs, rows_list, after, name):
    n = len(rows_list)
    offs = _offsets(rows_list)
    land = lax.empty((N_CHIPS, sum(rows_list), D_MODEL), BF16)

    def body(*refs):
        src, land_ref = refs[:n], refs[n]
        send_sems, recv_sems = refs[n + 2], refs[n + 3]
        token = refs[-1]
        px, py, pc = _mesh_pos()
        for k in range(N_CHIPS):
            block = 2 * k + (1 - pc)
            for off, rows, src_ref in zip(offs, rows_list, src):
                pltpu.make_async_remote_copy(
                    src_ref=src_ref.at[pl.ds(block * rows, rows), :], dst_ref=land_ref.at[k, pl.ds(off, rows), :],
                    send_sem=send_sems.at[0], recv_sem=recv_sems.at[0],
                    device_id=(px, py, 1 - pc), device_id_type=pl.DeviceIdType.MESH).start()
        token[...] = jnp.zeros_like(token)

    outs = pl.pallas_call(
        body, name=name,
        out_shape=(pltpu.SemaphoreType.DMA((1,)), pltpu.SemaphoreType.DMA((1,)),
                   *[pltpu.HBM(t.shape, BF16) for t in srcs], pltpu.HBM(land.shape, BF16),
                   jax.ShapeDtypeStruct((8, LANE), F32)),
        in_specs=(HBM_SPEC,) * (n + 1) + (ANY_SPEC,),
        out_specs=(SEM_SPEC, SEM_SPEC) + (HBM_SPEC,) * (n + 1) + (pl.BlockSpec(memory_space=pltpu.VMEM),),
        input_output_aliases={i: 2 + i for i in range(n + 1)},
        compiler_params=pltpu.CompilerParams(has_side_effects=EFFECT),
    )(*[_hbm(t) for t in srcs], _hbm(land), after)
    return outs[0], outs[1], list(outs[2:2 + n]), outs[2 + n], outs[-1]


def _split_wait(send_sems, recv_sems, n_sems, srcs, land, after, name):
    n = len(srcs)

    def body(*refs):
        land_ref = refs[n]
        s_sems, r_sems = refs[n + 1], refs[n + 2]
        me = _mesh_pos()
        for k in range(n_sems):
            cp = _whole_wait(land_ref.at[0] if n_sems > 1 else land_ref, s_sems.at[k], r_sems.at[k], me)
            cp.wait_send()
            cp.wait_recv()

    outs = pl.pallas_call(
        body, name=name,
        out_shape=(*[pltpu.HBM(t.shape, BF16) for t in srcs], pltpu.HBM(land.shape, BF16)),
        in_specs=(HBM_SPEC,) * (n + 1) + (SEM_SPEC, SEM_SPEC, ANY_SPEC),
        out_specs=(HBM_SPEC,) * (n + 1),
        input_output_aliases={i: i for i in range(n + 1)},
        compiler_params=pltpu.CompilerParams(has_side_effects=EFFECT),
    )(*srcs, land, send_sems, recv_sems, after)
    return list(outs[:n]), outs[n]


def _pair_sum(srcs, rows_list, land, core, name):
    n = len(rows_list)
    offs = _offsets(rows_list)
    total = sum(rows_list)

    def body(core_ref, *refs):
        src, land_ref, out_ref = refs[:n], refs[n], refs[n + 1]
        for off, rows, src_ref in zip(offs, rows_list, src):
            out_ref[pl.ds(off, rows), :] = (src_ref[...].astype(F32)
                                            + land_ref[pl.ds(off, rows), :].astype(F32)).astype(BF16)

    slot = pl.BlockSpec((None, total, D_MODEL), lambda k, c: (k, 0, 0))
    grid_spec = pltpu.PrefetchScalarGridSpec(
        num_scalar_prefetch=1, grid=(N_CHIPS,),
        in_specs=[pl.BlockSpec((rows, D_MODEL), lambda k, c: (2 * k + c[0], 0)) for rows in rows_list] + [slot],
        out_specs=slot)
    return pl.pallas_call(
        body, name=name, grid_spec=grid_spec,
        out_shape=jax.ShapeDtypeStruct((N_CHIPS, total, D_MODEL), BF16),
        compiler_params=_params(("parallel",)),
    )(core, *srcs, land)


def _chip_exchange_start(sums, chip, after, name):
    own = lax.dynamic_index_in_dim(sums, chip, axis=0, keepdims=True)
    recv = lax.dynamic_update_slice_in_dim(lax.empty(sums.shape, BF16), own, chip, axis=0)

    def body(sums_ref, recv_ref, after_ref, send_sems, recv_sems, sums_thru, recv_thru, token):
        px, py, pc = _mesh_pos()
        for k in range(1, N_CHIPS):
            qx = 1 - px if k & 2 else px
            qy = 1 - py if k & 1 else py
            pltpu.make_async_remote_copy(
                src_ref=sums_ref.at[2 * qx + qy], dst_ref=recv_ref.at[2 * px + py],
                send_sem=send_sems.at[k - 1], recv_sem=recv_sems.at[k - 1],
                device_id=(qx, qy, pc), device_id_type=pl.DeviceIdType.MESH).start()
        token[...] = jnp.zeros_like(token)

    outs = pl.pallas_call(
        body, name=name,
        out_shape=(pltpu.SemaphoreType.DMA((N_CHIPS - 1,)), pltpu.SemaphoreType.DMA((N_CHIPS - 1,)),
                   pltpu.HBM(sums.shape, BF16), pltpu.HBM(recv.shape, BF16), jax.ShapeDtypeStruct((8, LANE), F32)),
        in_specs=(HBM_SPEC, HBM_SPEC, ANY_SPEC),
        out_specs=(SEM_SPEC, SEM_SPEC, HBM_SPEC, HBM_SPEC, pl.BlockSpec(memory_space=pltpu.VMEM)),
        input_output_aliases={0: 2, 1: 3},
        compiler_params=pltpu.CompilerParams(has_side_effects=EFFECT),
    )(_hbm(sums), _hbm(recv), after)
    return outs[0], outs[1], [outs[2]], outs[3], outs[4]


def _sum_slots_into(recv, buf, layer, row_off, name):
    slots, r, n = recv.shape
    tr = _row_tile(math.gcd(r, row_off) if row_off else r, 512)
    first = row_off // tr

    def body(in_ref, buf_ref, out_ref):
        acc = in_ref[0].astype(F32)
        for j in range(1, slots):
            acc = acc + in_ref[j].astype(F32)
        out_ref[...] = acc

    return pl.pallas_call(
        body, name=name, grid=(r // tr,), out_shape=jax.ShapeDtypeStruct(buf.shape, F32),
        in_specs=[pl.BlockSpec((slots, tr, n), lambda i: (0, i, 0)), ANY_SPEC],
        out_specs=pl.BlockSpec((None, tr, n), lambda i: (layer, first + i, 0)),
        input_output_aliases={1: 0},
        compiler_params=_params(("parallel",)),
    )(recv, buf)


def _sum_slots(recv, name, after=None):
    _, r, n = recv.shape
    tr = _row_tile(r, 512)

    def body(in_ref, *refs):
        acc = in_ref[0].astype(F32)
        for j in range(1, N_DEV):
            acc = acc + in_ref[j].astype(F32)
        refs[-1][...] = acc

    grid = (r // tr,)
    in_specs, out_spec = [pl.BlockSpec((N_DEV, tr, n), lambda i: (0, i, 0))], pl.BlockSpec((tr, n), lambda i: (i, 0))
    args = [recv]
    if after is not None:
        in_specs.append(ANY_SPEC)
        args.append(after)
    return pl.pallas_call(
        body, name=name, grid=grid,
        out_shape=jax.ShapeDtypeStruct((r, n), F32),
        in_specs=in_specs, out_specs=out_spec,
        compiler_params=_params(("parallel",)),
    )(*args)


def _row_tile(rows, target):
    if rows <= target:
        return rows
    best = None
    for t in range(16, target + 1, 16):
        if rows % t == 0:
            best = t
    assert best is not None, rows
    return best


_DIMS = {"nn": ((1,), (0,)), "nt": ((1,), (1,)), "tn": ((0,), (0,))}


def _mm(a, b, mode, name, out_dtype=F32, res=None, gate=None, gate_factor=1.0, tm=512, tn=1408):
    if mode == "tn":
        kdim, m = a.shape
    else:
        m, kdim = a.shape
    n = b.shape[0] if mode == "nt" else b.shape[1]
    tm, tn = _tile(m, tm), _tile(n, tn)
    a_spec = (pl.BlockSpec((kdim, tm), lambda i, j: (0, i)) if mode == "tn"
              else pl.BlockSpec((tm, kdim), lambda i, j: (i, 0)))
    b_spec = (pl.BlockSpec((tn, kdim), lambda i, j: (j, 0)) if mode == "nt"
              else pl.BlockSpec((kdim, tn), lambda i, j: (0, j)))
    o_spec = pl.BlockSpec((tm, tn), lambda i, j: (i, j))
    dims = (_DIMS[mode], ((), ()))
    has_res, has_gate = res is not None, gate is not None

    def body(*refs):
        a_ref, b_ref = refs[0], refs[1]
        y = lax.dot_general(a_ref[...].astype(BF16), b_ref[...].astype(BF16), dims,
                            preferred_element_type=F32)
        if not has_res:
            refs[2][...] = y.astype(out_dtype)
            return
        res_ref = refs[2]
        if has_gate:
            y_ref, o_ref = refs[4], refs[5]
            y_ref[...] = y.astype(BF16)
            o_ref[...] = res_ref[...] + (gate_factor * refs[3][...]) * y
        else:
            refs[3][...] = res_ref[...] + y

    in_specs, args = [a_spec, b_spec], [a, b]
    if has_res:
        in_specs.append(o_spec)
        args.append(res)
        if has_gate:
            in_specs.append(pl.BlockSpec((1, tn), lambda i, j: (0, j)))
            args.append(gate)
            out_shape = (jax.ShapeDtypeStruct((m, n), BF16), jax.ShapeDtypeStruct((m, n), F32))
            out_specs = (o_spec, o_spec)
        else:
            out_shape, out_specs = jax.ShapeDtypeStruct((m, n), F32), o_spec
    else:
        out_shape, out_specs = jax.ShapeDtypeStruct((m, n), out_dtype), o_spec
    return pl.pallas_call(
        body, name=name, grid=(m // tm, n // tn), out_shape=out_shape,
        in_specs=in_specs, out_specs=out_specs,
        compiler_params=_params(("parallel", "parallel")),
    )(*args)


def _vec_spec(width):
    return pl.BlockSpec((1, width), lambda i: (0, 0))


def _rm_bwd(dh, x, dres, gw, scale, name):
    s, d = x.shape
    ts = _tile(s, 256)

    def body(dh_ref, x_ref, dres_ref, gw_ref, sc_ref, dx_ref, dsh_ref, dsc_ref, dgw_ref):
        @pl.when(pl.program_id(0) == 0)
        def _():
            dsh_ref[...] = jnp.zeros_like(dsh_ref)
            dsc_ref[...] = jnp.zeros_like(dsc_ref)
            dgw_ref[...] = jnp.zeros_like(dgw_ref)

        xv, dhv, gwv = x_ref[...], dh_ref[...], gw_ref[...]
        r = lax.rsqrt(jnp.mean(xv * xv, axis=-1, keepdims=True) + EPS)
        xn = xv * r
        y = xn * gwv
        dsh_ref[...] += jnp.sum(dhv, axis=0, keepdims=True)
        dsc_ref[...] += jnp.sum(dhv * y, axis=0, keepdims=True)
        dy = dhv * (1 + sc_ref[...])
        dgw_ref[...] += jnp.sum(dy * xn, axis=0, keepdims=True)
        dxn = dy * gwv
        dx = r * (dxn - xn * jnp.mean(dxn * xn, axis=-1, keepdims=True))
        dx_ref[...] = dres_ref[...] + dx

    row = pl.BlockSpec((ts, d), lambda i: (i, 0))
    vec = jax.ShapeDtypeStruct((1, d), F32)
    return pl.pallas_call(
        body, name=name, grid=(s // ts,),
        out_shape=(jax.ShapeDtypeStruct((s, d), F32), vec, vec, vec),
        in_specs=[row, row, row, _vec_spec(d), _vec_spec(d)],
        out_specs=(row, _vec_spec(d), _vec_spec(d), _vec_spec(d)),
        compiler_params=_params(("arbitrary",)),
    )(dh, x, dres, gw, scale)


def _gate_bwd(dx, y, gate, factor, name):
    s, d = dx.shape
    ts = _tile(s, 256)

    def body(dx_ref, y_ref, g_ref, dy_ref, dg_ref):
        @pl.when(pl.program_id(0) == 0)
        def _():
            dg_ref[...] = jnp.zeros_like(dg_ref)

        dxv = dx_ref[...]
        dy_ref[...] = ((factor * g_ref[...]) * dxv).astype(BF16)
        dg_ref[...] += jnp.sum((factor * dxv) * y_ref[...].astype(F32), axis=0, keepdims=True)

    row = pl.BlockSpec((ts, d), lambda i: (i, 0))
    return pl.pallas_call(
        body, name=name, grid=(s // ts,),
        out_shape=(jax.ShapeDtypeStruct((s, d), BF16), jax.ShapeDtypeStruct((1, d), F32)),
        in_specs=[row, row, _vec_spec(d)], out_specs=(row, _vec_spec(d)),
        compiler_params=_params(("arbitrary",)),
    )(dx, y, gate)


def _norm_mm(x, gw, shift, scale, w, name, tm=1024):
    s, d = x.shape
    n = w.shape[0]
    tm = _tile(s, tm)

    def body(x_ref, gw_ref, sh_ref, sc_ref, w_ref, h_ref, z_ref):
        xv = x_ref[...]
        r = lax.rsqrt(jnp.mean(xv * xv, axis=-1, keepdims=True) + EPS)
        hb = (((xv * r) * gw_ref[...]) * (1 + sc_ref[...]) + sh_ref[...]).astype(BF16)
        h_ref[...] = hb
        z_ref[...] = lax.dot_general(hb, w_ref[...], (((1,), (1,)), ((), ())), preferred_element_type=F32)

    row = pl.BlockSpec((tm, d), lambda i: (i, 0))
    return pl.pallas_call(
        body, name=name, grid=(s // tm,),
        out_shape=(jax.ShapeDtypeStruct((s, d), BF16), jax.ShapeDtypeStruct((s, n), F32)),
        in_specs=[row, _vec_spec(d), _vec_spec(d), _vec_spec(d), pl.BlockSpec((n, d), lambda i: (0, 0))],
        out_specs=(row, pl.BlockSpec((tm, n), lambda i: (i, 0))),
        compiler_params=_params(("parallel",)),
    )(x, gw, shift, scale, w)


FFN_TM, FFN_TF = 2048, 256


def _ffn_up(x, gw, shift, scale, wg, wu, name):
    s, d = x.shape
    f = wg.shape[0]
    tm, tf = _tile(s, FFN_TM), _tile(f, FFN_TF)
    nt = (((1,), (1,)), ((), ()))

    def body(x_ref, gw_ref, sh_ref, sc_ref, wg_ref, wu_ref, h_ref, a_ref, b_ref, t_ref):
        @pl.when(pl.program_id(1) == 0)
        def _():
            xv = x_ref[...]
            r = lax.rsqrt(jnp.mean(xv * xv, axis=-1, keepdims=True) + EPS)
            h_ref[...] = (((xv * r) * gw_ref[...]) * (1 + sc_ref[...]) + sh_ref[...]).astype(BF16)

        hb = h_ref[...]
        av = lax.dot_general(hb, wg_ref[...], nt, preferred_element_type=F32)
        bv = lax.dot_general(hb, wu_ref[...], nt, preferred_element_type=F32)
        a_ref[...] = av.astype(BF16)
        b_ref[...] = bv.astype(BF16)
        t_ref[...] = ((av * jax.nn.sigmoid(av)) * bv).astype(BF16)

    row = pl.BlockSpec((tm, d), lambda i, j: (i, 0))
    vec = pl.BlockSpec((1, d), lambda i, j: (0, 0))
    wblk = pl.BlockSpec((tf, d), lambda i, j: (j, 0))
    blk = pl.BlockSpec((tm, tf), lambda i, j: (i, j))
    wide = jax.ShapeDtypeStruct((s, f), BF16)
    return pl.pallas_call(
        body, name=name, grid=(s // tm, f // tf),
        out_shape=(jax.ShapeDtypeStruct((s, d), BF16), wide, wide, wide),
        in_specs=[row, vec, vec, vec, wblk, wblk], out_specs=(row, blk, blk, blk),
        compiler_params=_params(("parallel", "arbitrary")),
    )(x, gw, shift, scale, wg, wu)


def _ffn_bwd_cols(dy, h, a, b, t, wd, name):
    s, d = dy.shape
    f = wd.shape[0]
    tf = _tile(f, FFN_TF)
    nt = (((1,), (1,)), ((), ()))
    tn = (((0,), (0,)), ((), ()))

    def body(dy_ref, h_ref, a_ref, b_ref, t_ref, wd_ref, da_ref, db_ref, gd_ref, gg_ref, gu_ref):
        dyb, hb = dy_ref[...], h_ref[...]
        dtv = lax.dot_general(dyb, wd_ref[...], nt, preferred_element_type=F32)
        av, bv = a_ref[...].astype(F32), b_ref[...].astype(F32)
        sg = jax.nn.sigmoid(av)
        dbv = (dtv * (av * sg)).astype(BF16)
        dav = ((dtv * bv) * (sg * (1 + av * (1 - sg)))).astype(BF16)
        da_ref[...] = dav
        db_ref[...] = dbv
        gd_ref[...] = lax.dot_general(t_ref[...], dyb, tn, preferred_element_type=F32).astype(BF16)
        gg_ref[...] = lax.dot_general(dav, hb, tn, preferred_element_type=F32).astype(BF16)
        gu_ref[...] = lax.dot_general(dbv, hb, tn, preferred_element_type=F32).astype(BF16)

    whole = pl.BlockSpec((s, d), lambda j: (0, 0))
    col = pl.BlockSpec((s, tf), lambda j: (0, j))
    wblk = pl.BlockSpec((tf, d), lambda j: (j, 0))
    wide, wgrad = jax.ShapeDtypeStruct((s, f), BF16), jax.ShapeDtypeStruct((f, d), BF16)
    return pl.pallas_call(
        body, name=name, grid=(f // tf,), out_shape=(wide, wide, wgrad, wgrad, wgrad),
        in_specs=[whole, whole, col, col, col, wblk], out_specs=(col, col, wblk, wblk, wblk),
        compiler_params=_params(("parallel",)),
    )(dy, h, a, b, t, wd)


def _mm_pair(a1, b1, a2, b2, name, tm=1024, tn=512, after=None):
    m, kdim = a1.shape
    n = b1.shape[1]
    tm, tn = _tile(m, tm), _tile(n, tn)

    def body(a1_ref, b1_ref, a2_ref, b2_ref, *refs):
        refs[-1][...] = (jnp.dot(a1_ref[...], b1_ref[...], preferred_element_type=F32)
                         + jnp.dot(a2_ref[...], b2_ref[...], preferred_element_type=F32))

    a_spec = pl.BlockSpec((tm, kdim), lambda i, j: (i, 0))
    b_spec = pl.BlockSpec((kdim, tn), lambda i, j: (0, j))
    in_specs, args = [a_spec, b_spec, a_spec, b_spec], [a1, b1, a2, b2]
    if after is not None:
        in_specs.append(ANY_SPEC)
        args.append(after)
    return pl.pallas_call(
        body, name=name, grid=(m // tm, n // tn), out_shape=jax.ShapeDtypeStruct((m, n), F32),
        in_specs=in_specs, out_specs=pl.BlockSpec((tm, tn), lambda i, j: (i, j)),
        compiler_params=_params(("parallel", "parallel")),
    )(*args)


def _pool_counts(s):
    return (lax.broadcasted_iota(jnp.int32, (s, POOL_GC), 0))


def _pool_fwd(z, pool_w, pool_scale, name):
    s = z.shape[0]

    def body(u_ref, w_ref, sc_ref, y_ref, diff_ref):
        t = lax.broadcasted_iota(jnp.int32, (s, POOL_GC), 0)
        for g, win in enumerate(POOL_WINDOWS):
            cols = slice(g * POOL_GC, (g + 1) * POOL_GC)
            u = u_ref[:, cols]
            acc, step = u, 1
            while step < win:
                acc = acc + jnp.where(t >= step, pltpu.roll(acc, step, 0), 0.0)
                step *= 2
            cnt = jnp.minimum(t + 1, win).astype(F32)
            diff = acc / cnt - u
            diff_ref[:, cols] = diff
            ypre = jnp.dot(diff.astype(BF16), w_ref[g].astype(BF16), preferred_element_type=F32)
            y_ref[:, cols] = (ypre * sc_ref[:, cols]).astype(BF16)

    return pl.pallas_call(
        body, name=name, grid=(1,),
        out_shape=(jax.ShapeDtypeStruct((s, POOL_WIDTH), BF16), jax.ShapeDtypeStruct((s, POOL_WIDTH), F32)),
        in_specs=[pl.BlockSpec((s, POOL_WIDTH), lambda i: (0, 0)),
                  pl.BlockSpec(pool_w.shape, lambda i: (0, 0, 0)),
                  pl.BlockSpec((1, POOL_WIDTH), lambda i: (0, 0))],
        out_specs=(pl.BlockSpec((s, POOL_WIDTH), lambda i: (0, 0)),
                   pl.BlockSpec((s, POOL_WIDTH), lambda i: (0, 0))),
        compiler_params=_params(("arbitrary",)),
    )(z, pool_w, pool_scale)


def _pool_bwd(dycat, diff, pool_w, pool_scale, name):
    s = diff.shape[0]

    def body(dy_ref, diff_ref, w_ref, sc_ref, du_ref, dw_ref, dsc_ref):
        t = lax.broadcasted_iota(jnp.int32, (s, POOL_GC), 0)
        for g, win in enumerate(POOL_WINDOWS):
            cols = slice(g * POOL_GC, (g + 1) * POOL_GC)
            dy, dfb, wb = dy_ref[:, cols], diff_ref[:, cols].astype(BF16), w_ref[g].astype(BF16)
            ypre = jnp.dot(dfb, wb, preferred_element_type=F32)
            dsc_ref[:, cols] = jnp.sum(dy * ypre, axis=0, keepdims=True)
            dypre = (dy * sc_ref[:, cols]).astype(BF16)
            ddiff = lax.dot_general(dypre, wb, (((1,), (1,)), ((), ())), preferred_element_type=F32)
            dw_ref[g] = lax.dot_general(dfb, dypre, (((0,), (0,)), ((), ())), preferred_element_type=F32)
            cnt = jnp.minimum(t + 1, win).astype(F32)
            acc, step = ddiff / cnt, 1
            while step < win:
                acc = acc + jnp.where(t < s - step, pltpu.roll(acc, s - step, 0), 0.0)
                step *= 2
            du_ref[:, cols] = acc - ddiff

    full = pl.BlockSpec((s, POOL_WIDTH), lambda i: (0, 0))
    return pl.pallas_call(
        body, name=name, grid=(1,),
        out_shape=(jax.ShapeDtypeStruct((s, POOL_WIDTH), F32),
                   jax.ShapeDtypeStruct(pool_w.shape, F32),
                   jax.ShapeDtypeStruct((1, POOL_WIDTH), F32)),
        in_specs=[full, full, pl.BlockSpec(pool_w.shape, lambda i: (0, 0, 0)),
                  pl.BlockSpec((1, POOL_WIDTH), lambda i: (0, 0))],
        out_specs=(full, pl.BlockSpec(pool_w.shape, lambda i: (0, 0, 0)),
                   pl.BlockSpec((1, POOL_WIDTH), lambda i: (0, 0))),
        compiler_params=_params(("arbitrary",)),
    )(dycat, diff, pool_w, pool_scale)


def _rope_tables(positions, name):
    s = positions.shape[0]
    ts = _tile(s, 512)
    freq = 1.0 / (ROPE_THETA ** (np.arange(0, QK_ROPE, 2, dtype=np.float32) / QK_ROPE))
    table = np.zeros((1, LANE), np.float32)
    table[0, :QK_ROPE // 2] = freq
    table[0, QK_ROPE // 2:QK_ROPE] = freq

    def body(pos_ref, f_ref, cos_ref, sin_ref):
        ang = pos_ref[...].astype(F32) * f_ref[...]
        cos_ref[...] = jnp.cos(ang)
        sin_ref[...] = jnp.sin(ang)

    out = jax.ShapeDtypeStruct((s, LANE), F32)
    blk = pl.BlockSpec((ts, LANE), lambda i: (i, 0))
    return pl.pallas_call(
        body, name=name, grid=(s // ts,), out_shape=(out, out),
        in_specs=[pl.BlockSpec((ts, 1), lambda i: (i, 0)), _vec_spec(LANE)], out_specs=(blk, blk),
        compiler_params=_params(("parallel",)),
    )(positions, jnp.asarray(table))


def _lane_mod64_low(shape):
    return (lax.broadcasted_iota(jnp.int32, shape, 1) % QK_ROPE) < (QK_ROPE // 2)


def _rope(x, cos, sin):
    rot = jnp.where(_lane_mod64_low(x.shape), -pltpu.roll(x, LANE - 32, 1), pltpu.roll(x, 32, 1))
    return x * cos + rot * sin


def _rope_t(dy, cos, sin):
    w = dy * sin
    rot_t = jnp.where(_lane_mod64_low(dy.shape), pltpu.roll(w, LANE - 32, 1), -pltpu.roll(w, 32, 1))
    return dy * cos + rot_t


def _plain_rms(x, g):
    r = lax.rsqrt(jnp.mean(x * x, axis=-1, keepdims=True) + EPS)
    return (x * r) * g, x * r, r


O_Q, O_KV, O_KR = POOL_WIDTH, POOL_WIDTH + Q_LORA, POOL_WIDTH + Q_LORA + KV_LORA


def _qkv_fwd(z, qn, kvn, wq, wkv, cos, sin, name):
    s = z.shape[0]
    ts = _tile(s, 256)

    def body(z_ref, qn_ref, kvn_ref, wq_ref, wkv_ref, cos_ref, sin_ref, q_ref, k_ref, v_ref, cqn_ref, ckvn_ref):
        cosv, sinv = cos_ref[...], sin_ref[...]
        cqn = _plain_rms(z_ref[:, O_Q:O_KV], qn_ref[...])[0].astype(BF16)
        ckvn = _plain_rms(z_ref[:, O_KV:O_KR], kvn_ref[...])[0].astype(BF16)
        cqn_ref[...] = cqn
        ckvn_ref[...] = ckvn
        nt = (((1,), (1,)), ((), ()))
        q = lax.dot_general(cqn, wq_ref[...], nt, preferred_element_type=F32)
        kv = lax.dot_general(ckvn, wkv_ref[...], nt, preferred_element_type=F32)
        kr = _rope(z_ref[:, O_KR:IN_PAD], cosv, sinv).astype(BF16)
        for h in range(N_HEADS):
            o = h * HEAD_PAD
            q_ref[:, o:o + QK_NOPE] = q[:, o:o + QK_NOPE].astype(BF16)
            q_ref[:, o + QK_NOPE:o + HEAD_PAD] = _rope(q[:, o + QK_NOPE:o + HEAD_PAD], cosv, sinv).astype(BF16)
            k_ref[:, o:o + QK_NOPE] = kv[:, o:o + QK_NOPE].astype(BF16)
            k_ref[:, o + QK_NOPE:o + HEAD_PAD] = kr
            v_ref[:, h * V_HEAD:(h + 1) * V_HEAD] = kv[:, o + QK_NOPE:o + HEAD_PAD].astype(BF16)

    def row(w):
        return pl.BlockSpec((ts, w), lambda i: (i, 0))

    def whole(arr):
        return pl.BlockSpec(arr.shape, lambda i: (0, 0))

    hp = N_HEADS * HEAD_PAD
    return pl.pallas_call(
        body, name=name, grid=(s // ts,),
        out_shape=(jax.ShapeDtypeStruct((s, hp), BF16), jax.ShapeDtypeStruct((s, hp), BF16),
                   jax.ShapeDtypeStruct((s, N_HEADS * V_HEAD), BF16),
                   jax.ShapeDtypeStruct((s, Q_LORA), BF16), jax.ShapeDtypeStruct((s, KV_LORA), BF16)),
        in_specs=[row(IN_PAD), whole(qn), whole(kvn), whole(wq), whole(wkv), row(LANE), row(LANE)],
        out_specs=(row(hp), row(hp), row(N_HEADS * V_HEAD), row(Q_LORA), row(KV_LORA)),
        compiler_params=_params(("parallel",)),
    )(z, qn, kvn, wq, wkv, cos, sin)


def _qkv_bwd(dq, dk, dv, du, z, qn, kvn, wq, wkv, cos, sin, name):
    s = z.shape[0]
    ts = _tile(s, 256)

    def norm_bwd(x, g, dy):
        _, xn, r = _plain_rms(x, g)
        dxn = dy * g
        return r * (dxn - xn * jnp.mean(dxn * xn, axis=-1, keepdims=True)), jnp.sum(dy * xn, axis=0, keepdims=True)

    def body(dq_ref, dk_ref, dv_ref, du_ref, z_ref, qn_ref, kvn_ref, wq_ref, wkv_ref, cos_ref, sin_ref,
             dz_ref, dqb_ref, dkvb_ref, dqn_ref, dkvn_ref):
        @pl.when(pl.program_id(0) == 0)
        def _():
            dqn_ref[...] = jnp.zeros_like(dqn_ref)
            dkvn_ref[...] = jnp.zeros_like(dkvn_ref)

        cosv, sinv = cos_ref[...], sin_ref[...]
        dkr = jnp.zeros((ts, LANE), F32)
        for h in range(N_HEADS):
            o = h * HEAD_PAD
            dqb_ref[:, o:o + QK_NOPE] = dq_ref[:, o:o + QK_NOPE].astype(BF16)
            dqb_ref[:, o + QK_NOPE:o + HEAD_PAD] = _rope_t(dq_ref[:, o + QK_NOPE:o + HEAD_PAD], cosv, sinv).astype(BF16)
            dkvb_ref[:, o:o + QK_NOPE] = dk_ref[:, o:o + QK_NOPE].astype(BF16)
            dkvb_ref[:, o + QK_NOPE:o + HEAD_PAD] = dv_ref[:, h * V_HEAD:(h + 1) * V_HEAD].astype(BF16)
            dkr = dkr + dk_ref[:, o + QK_NOPE:o + HEAD_PAD]
        dcqn = jnp.dot(dqb_ref[...], wq_ref[...], preferred_element_type=F32)
        dckvn = jnp.dot(dkvb_ref[...], wkv_ref[...], preferred_element_type=F32)
        dcq, dqn = norm_bwd(z_ref[:, O_Q:O_KV], qn_ref[...], dcqn)
        dckv, dkvn = norm_bwd(z_ref[:, O_KV:O_KR], kvn_ref[...], dckvn)
        dqn_ref[...] += dqn
        dkvn_ref[...] += dkvn
        dz_ref[:, 0:O_Q] = du_ref[...].astype(BF16)
        dz_ref[:, O_Q:O_KV] = dcq.astype(BF16)
        dz_ref[:, O_KV:O_KR] = dckv.astype(BF16)
        dz_ref[:, O_KR:IN_PAD] = _rope_t(dkr, cosv, sinv).astype(BF16)

    def row(w):
        return pl.BlockSpec((ts, w), lambda i: (i, 0))

    def whole(arr):
        return pl.BlockSpec(arr.shape, lambda i: (0, 0))

    hp = N_HEADS * HEAD_PAD
    return pl.pallas_call(
        body, name=name, grid=(s // ts,),
        out_shape=(jax.ShapeDtypeStruct((s, IN_PAD), BF16), jax.ShapeDtypeStruct((s, hp), BF16),
                   jax.ShapeDtypeStruct((s, hp), BF16),
                   jax.ShapeDtypeStruct((1, Q_LORA), F32), jax.ShapeDtypeStruct((1, KV_LORA), F32)),
        in_specs=[row(hp), row(hp), row(N_HEADS * V_HEAD), row(POOL_WIDTH), row(IN_PAD),
                  whole(qn), whole(kvn), whole(wq), whole(wkv), row(LANE), row(LANE)],
        out_specs=(row(IN_PAD), row(hp), row(hp), whole(qn), whole(kvn)),
        compiler_params=_params(("arbitrary",)),
    )(dq, dk, dv, du, z, qn, kvn, wq, wkv, cos, sin)


def _causal_scores(q, k, i, tq, klen):
    sc = lax.dot_general(q, k, (((1,), (1,)), ((), ())), preferred_element_type=F32) * SOFTMAX_SCALE
    qpos = i * tq + lax.broadcasted_iota(jnp.int32, (tq, klen), 0)
    kpos = lax.broadcasted_iota(jnp.int32, (tq, klen), 1)
    return jnp.where(qpos >= kpos, sc, -jnp.inf)


ATTN_TQ = 512
ATTN_SEGMENTS = 4


def _by_key_prefix(i, nq, tq, compute):
    nseg = min(ATTN_SEGMENTS, nq)
    per = nq // nseg
    for r in range(nseg):
        pl.when(i // per == r)(lambda r=r: compute((r + 1) * per * tq))


def _attn_fwd(q, k, v, name):
    s = q.shape[0]
    tq = _tile(s, ATTN_TQ)
    nq = s // tq

    def body(q_ref, k_ref, v_ref, o_ref, lse_ref):
        i = pl.program_id(1)

        def compute(klen):
            sc = _causal_scores(q_ref[...], k_ref[0:klen, :], i, tq, klen)
            mx = jnp.max(sc, axis=-1, keepdims=True)
            p = jnp.exp(sc - mx)
            den = jnp.sum(p, axis=-1, keepdims=True)
            o_ref[...] = jnp.dot((p / den).astype(BF16), v_ref[0:klen, :], preferred_element_type=F32)
            lse_ref[...] = mx + jnp.log(den)

        _by_key_prefix(i, nq, tq, compute)

    return pl.pallas_call(
        body, name=name, grid=(N_HEADS, s // tq),
        out_shape=(jax.ShapeDtypeStruct((s, N_HEADS * V_HEAD), F32), jax.ShapeDtypeStruct((N_HEADS, s, 1), F32)),
        in_specs=[pl.BlockSpec((tq, HEAD_PAD), lambda h, i: (i, h)),
                  pl.BlockSpec((s, HEAD_PAD), lambda h, i: (0, h)),
                  pl.BlockSpec((s, V_HEAD), lambda h, i: (0, h))],
        out_specs=(pl.BlockSpec((tq, V_HEAD), lambda h, i: (i, h)),
                   pl.BlockSpec((None, tq, 1), lambda h, i: (h, i, 0))),
        compiler_params=_params(("parallel", "parallel")),
    )(q, k, v)


def _attn_bwd(q, k, v, lse, dycat, name):
    s = q.shape[0]
    tq = _tile(s, ATTN_TQ)
    nq = s // tq
    tn_dims = (((0,), (0,)), ((), ()))

    def body(q_ref, k_ref, v_ref, lse_ref, do_ref, dq_ref, dk_ref, dv_ref):
        i = pl.program_id(1)

        @pl.when(i == 0)
        def _():
            dk_ref[...] = jnp.zeros_like(dk_ref)
            dv_ref[...] = jnp.zeros_like(dv_ref)

        def compute(klen):
            qv, kv_, dob = q_ref[...], k_ref[0:klen, :], do_ref[...].astype(BF16)
            sc = _causal_scores(qv, kv_, i, tq, klen)
            p = jnp.exp(sc - lse_ref[...])
            dp = lax.dot_general(dob, v_ref[0:klen, :], (((1,), (1,)), ((), ())), preferred_element_type=F32)
            ds = (p * (dp - jnp.sum(dp * p, axis=-1, keepdims=True)) * SOFTMAX_SCALE).astype(BF16)
            dq_ref[...] = jnp.dot(ds, kv_, preferred_element_type=F32)
            dk_ref[0:klen, :] += lax.dot_general(ds, qv, tn_dims, preferred_element_type=F32)
            dv_ref[0:klen, :] += lax.dot_general(p.astype(BF16), dob, tn_dims, preferred_element_type=F32)

        _by_key_prefix(i, nq, tq, compute)

    n_pool_blocks = POOL_WIDTH // V_HEAD
    return pl.pallas_call(
        body, name=name, grid=(N_HEADS, s // tq),
        out_shape=(jax.ShapeDtypeStruct((s, N_HEADS * HEAD_PAD), F32),
                   jax.ShapeDtypeStruct((s, N_HEADS * HEAD_PAD), F32),
                   jax.ShapeDtypeStruct((s, N_HEADS * V_HEAD), F32)),
        in_specs=[pl.BlockSpec((tq, HEAD_PAD), lambda h, i: (i, h)),
                  pl.BlockSpec((s, HEAD_PAD), lambda h, i: (0, h)),
                  pl.BlockSpec((s, V_HEAD), lambda h, i: (0, h)),
                  pl.BlockSpec((None, tq, 1), lambda h, i: (h, i, 0)),
                  pl.BlockSpec((tq, V_HEAD), lambda h, i: (i, n_pool_blocks + h))],
        out_specs=(pl.BlockSpec((tq, HEAD_PAD), lambda h, i: (i, h)),
                   pl.BlockSpec((s, HEAD_PAD), lambda h, i: (0, h)),
                   pl.BlockSpec((s, V_HEAD), lambda h, i: (0, h))),
        compiler_params=_params(("parallel", "arbitrary")),
    )(q, k, v, lse, dycat)


def _loss_head(x, gw, target, name):
    s, d = x.shape
    ts = _tile(s, 256)

    def body(x_ref, gw_ref, tgt_ref, loss_ref, dx_ref, dgw_ref):
        @pl.when(pl.program_id(0) == 0)
        def _():
            loss_ref[...] = jnp.zeros_like(loss_ref)
            dgw_ref[...] = jnp.zeros_like(dgw_ref)

        xv, gwv = x_ref[...], gw_ref[...]
        r = lax.rsqrt(jnp.mean(xv * xv, axis=-1, keepdims=True) + EPS)
        xn = xv * r
        err = xn * gwv - tgt_ref[...]
        loss_ref[...] += 0.5 * jnp.sum(jnp.mean(err * err, axis=-1, keepdims=True))
        dy = err / d
        dgw_ref[...] += jnp.sum(dy * xn, axis=0, keepdims=True)
        dxn = dy * gwv
        dx_ref[...] = r * (dxn - xn * jnp.mean(dxn * xn, axis=-1, keepdims=True))

    row = pl.BlockSpec((ts, d), lambda i: (i, 0))
    return pl.pallas_call(
        body, name=name, grid=(s // ts,),
        out_shape=(jax.ShapeDtypeStruct((8, LANE), F32), jax.ShapeDtypeStruct((s, d), F32),
                   jax.ShapeDtypeStruct((1, d), F32)),
        in_specs=[row, _vec_spec(d), row],
        out_specs=(pl.BlockSpec((8, LANE), lambda i: (0, 0)), row, _vec_spec(d)),
        compiler_params=_params(("arbitrary",)),
    )(x, gw, target)


def _ada_mod(c_all, ada_w, ada_b, name):
    nl, d, cols = ada_w.shape

    def body(c_ref, w_ref, b_ref, o_ref):
        cv = c_ref[...]
        act = (cv * jax.nn.sigmoid(cv)).astype(BF16)
        o_ref[...] = jnp.dot(act, w_ref[...].astype(BF16), preferred_element_type=F32) + b_ref[...]

    return pl.pallas_call(
        body, name=name, grid=(nl,), out_shape=jax.ShapeDtypeStruct((nl, N_DEV, cols), F32),
        in_specs=[pl.BlockSpec((N_DEV, d), lambda l: (0, 0)),
                  pl.BlockSpec((None, d, cols), lambda l: (l, 0, 0)),
                  pl.BlockSpec((None, 1, cols), lambda l: (l, 0, 0))],
        out_specs=pl.BlockSpec((None, N_DEV, cols), lambda l: (l, 0, 0)),
        compiler_params=_params(("parallel",)),
    )(c_all, ada_w, ada_b)


def _ada_grad(c_pad, dmod_pad, name):
    nl, kpad, cols = dmod_pad.shape
    d = c_pad.shape[1]

    def body(c_ref, dm_ref, o_ref):
        cv = c_ref[...]
        act = (cv * jax.nn.sigmoid(cv)).astype(BF16)
        o_ref[...] = lax.dot_general(act, dm_ref[...].astype(BF16), (((0,), (0,)), ((), ())),
                                     preferred_element_type=F32)

    return pl.pallas_call(
        body, name=name, grid=(nl,), out_shape=jax.ShapeDtypeStruct((nl, d, cols), F32),
        in_specs=[pl.BlockSpec((kpad, d), lambda l: (0, 0)),
                  pl.BlockSpec((None, kpad, cols), lambda l: (l, 0, 0))],
        out_specs=pl.BlockSpec((None, d, cols), lambda l: (l, 0, 0)),
        compiler_params=_params(("parallel",)),
    )(c_pad, dmod_pad)


def _adamw_math(w, g, m, v):
    nm = ADAM_B1 * m + (1.0 - ADAM_B1) * g
    nv = ADAM_B2 * v + (1.0 - ADAM_B2) * (g * g)
    m_hat = nm / (1.0 - ADAM_B1 ** ADAM_STEP)
    v_hat = nv / (1.0 - ADAM_B2 ** ADAM_STEP)
    return -ADAM_LR * (m_hat / (jnp.sqrt(v_hat) + ADAM_EPS) + ADAM_WD * w), nm, nv


def _adamw_rows(w3, gbuf, row_off, m3, v3, name):
    nl, r, d = w3.shape
    tr = _row_tile(math.gcd(r, row_off) if row_off else r, 176)
    first = row_off // tr

    def body(w_ref, g_ref, m_ref, v_ref, go_ref, d_ref, nm_ref, nv_ref):
        gv = g_ref[...]
        go_ref[...] = gv
        d_ref[...], nm_ref[...], nv_ref[...] = _adamw_math(w_ref[...], gv, m_ref[...], v_ref[...])

    blk = pl.BlockSpec((None, tr, d), lambda l, i: (l, i, 0))
    gblk = pl.BlockSpec((None, tr, d), lambda l, i: (l, first + i, 0))
    out = jax.ShapeDtypeStruct((nl, r, d), F32)
    return pl.pallas_call(
        body, name=name, grid=(nl, r // tr), out_shape=(out, out, out, out),
        in_specs=[blk, gblk, blk, blk], out_specs=(blk, blk, blk, blk),
        compiler_params=_params(("parallel", "parallel")),
    )(w3, gbuf, m3, v3)


def _adamw(w, g, m, v, name):
    rows, cols = w.shape
    tr = _row_tile(rows, 256)

    def body(w_ref, g_ref, m_ref, v_ref, d_ref, nm_ref, nv_ref):
        d_ref[...], nm_ref[...], nv_ref[...] = _adamw_math(w_ref[...], g_ref[...], m_ref[...], v_ref[...])

    blk = pl.BlockSpec((tr, cols), lambda i: (i, 0))
    out = jax.ShapeDtypeStruct((rows, cols), F32)
    return pl.pallas_call(
        body, name=name, grid=(rows // tr,), out_shape=(out, out, out),
        in_specs=[blk, blk, blk, blk], out_specs=(blk, blk, blk),
        compiler_params=_params(("parallel",)),
    )(w, g, m, v)


def _adamw_nd(w, g, m, v, name):
    shape = w.shape
    flat = (lambda t: t.reshape(1, -1)) if w.ndim == 1 else (lambda t: t.reshape(-1, shape[-1]))
    return tuple(t.reshape(shape) for t in _adamw(flat(w), flat(g), flat(m), flat(v), name))


def _pad_rows(t, rows):
    return jnp.pad(t, ((0, rows - t.shape[0]), (0, 0)))


def _pack_shard_layer(l, wts):
    def tr(name):
        return wts[name][l].astype(BF16).T

    parts = [tr("ffn1_w_gate"), tr("ffn1_w_up"), wts["ffn1_w_down"][l].astype(BF16),
             tr("ffn2_w_gate"), tr("ffn2_w_up"), wts["ffn2_w_down"][l].astype(BF16),
             wts["w_out"][l].astype(BF16),
             tr("w_kv_b").reshape(KV_SH_ROWS, D_MODEL),
             _pad_rows(tr("w_in"), 160),
             _pad_rows(tr("w_q_b").reshape(Q_SH_ROWS, D_MODEL), Q_PAD_ROWS)]
    return jnp.concatenate(parts, axis=0)


def _full_weights(lands):
    w = dict(zip(("g1", "u1", "d1", "g2", "u2", "d2", "out"), lands))
    small = lands[-1].reshape(N_DEV, SMALL_ROWS, D_MODEL)
    o_in, o_q = OFF_IN - OFF_KV, OFF_Q - OFF_KV
    w["kv"] = small[:, :KV_SH_ROWS].reshape(N_HEADS * HEAD_PAD, KV_LORA)
    w["in"] = _pad_rows(small[:, o_in:o_in + IN_SH].reshape(IN_COLS, D_MODEL), IN_PAD)
    wq = small[:, o_q:o_q + Q_SH_ROWS].reshape(N_HEADS, QK_HEAD, Q_LORA)
    w["q"] = jnp.pad(wq, ((0, 0), (0, HEAD_PAD - QK_HEAD), (0, 0))).reshape(N_HEADS * HEAD_PAD, Q_LORA)
    return w


def _grad_sources_b(gr):
    gq = gr["q"].reshape(N_HEADS, HEAD_PAD, Q_LORA)[:, :QK_HEAD].reshape(N_DEV, Q_SH_ROWS, D_MODEL)
    small = jnp.concatenate([
        gr["kv"].reshape(N_DEV, KV_SH_ROWS, D_MODEL),
        jnp.pad(gr["in"][:IN_COLS].reshape(N_DEV, IN_SH, D_MODEL), ((0, 0), (0, 160 - IN_SH), (0, 0))),
        jnp.pad(gq, ((0, 0), (0, Q_PAD_ROWS - Q_SH_ROWS), (0, 0)))], axis=1)
    return [gr["g2"], gr["u2"], gr["d2"], gr["out"], small.reshape(N_DEV * SMALL_ROWS, D_MODEL)]


def _small_layout(nl):
    names = [("dmod", nl * N_MOD), ("ffn1_norm", nl), ("mix_norm", nl), ("ffn2_norm", nl), ("q_a_norm", nl),
             ("kv_a_norm", nl), ("pool_scale", nl), ("final_norm", 1), ("loss", 1),
             ("pool_w", nl * 4 * POOL_GC * POOL_GC // D_MODEL)]
    off, table = 0, {}
    for name, n in names:
        table[name] = (off, n)
        off += -(-n // 8) * 8
    return table, off


def _to_rows(t, width=D_MODEL):
    n, w = t.shape
    return jnp.pad(t, ((0, -(-n // 8) * 8 - n), (0, width - w)))


def kernel(x, c, positions, ada_w, ada_b, ffn1_norm, ffn1_w_gate, ffn1_w_up, ffn1_w_down, mix_norm, w_in, pool_w, pool_scale, q_a_norm, w_q_b, kv_a_norm, w_kv_b, w_out, ffn2_norm, ffn2_w_gate, ffn2_w_up, ffn2_w_down, final_norm, loss_target, m_ada_w, m_ada_b, m_ffn1_norm, m_ffn1_w_gate, m_ffn1_w_up, m_ffn1_w_down, m_mix_norm, m_w_in, m_pool_w, m_pool_scale, m_q_a_norm, m_w_q_b, m_kv_a_norm, m_w_kv_b, m_w_out, m_ffn2_norm, m_ffn2_w_gate, m_ffn2_w_up, m_ffn2_w_down, m_final_norm, v_ada_w, v_ada_b, v_ffn1_norm, v_ffn1_w_gate, v_ffn1_w_up, v_ffn1_w_down, v_mix_norm, v_w_in, v_pool_w, v_pool_scale, v_q_a_norm, v_w_q_b, v_kv_a_norm, v_w_kv_b, v_w_out, v_ffn2_norm, v_ffn2_w_gate, v_ffn2_w_up, v_ffn2_w_down, v_final_norm):
    wts = dict(ada_w=ada_w, ada_b=ada_b, ffn1_norm=ffn1_norm, ffn1_w_gate=ffn1_w_gate, ffn1_w_up=ffn1_w_up,
               ffn1_w_down=ffn1_w_down, mix_norm=mix_norm, w_in=w_in, pool_w=pool_w, pool_scale=pool_scale,
               q_a_norm=q_a_norm, w_q_b=w_q_b, kv_a_norm=kv_a_norm, w_kv_b=w_kv_b, w_out=w_out,
               ffn2_norm=ffn2_norm, ffn2_w_gate=ffn2_w_gate, ffn2_w_up=ffn2_w_up, ffn2_w_down=ffn2_w_down,
               final_norm=final_norm)
    mom_m = dict(ada_w=m_ada_w, ada_b=m_ada_b, ffn1_norm=m_ffn1_norm, ffn1_w_gate=m_ffn1_w_gate,
                 ffn1_w_up=m_ffn1_w_up, ffn1_w_down=m_ffn1_w_down, mix_norm=m_mix_norm, w_in=m_w_in,
                 pool_w=m_pool_w, pool_scale=m_pool_scale, q_a_norm=m_q_a_norm, w_q_b=m_w_q_b,
                 kv_a_norm=m_kv_a_norm, w_kv_b=m_w_kv_b, w_out=m_w_out, ffn2_norm=m_ffn2_norm,
                 ffn2_w_gate=m_ffn2_w_gate, ffn2_w_up=m_ffn2_w_up, ffn2_w_down=m_ffn2_w_down,
                 final_norm=m_final_norm)
    mom_v = dict(ada_w=v_ada_w, ada_b=v_ada_b, ffn1_norm=v_ffn1_norm, ffn1_w_gate=v_ffn1_w_gate,
                 ffn1_w_up=v_ffn1_w_up, ffn1_w_down=v_ffn1_w_down, mix_norm=v_mix_norm, w_in=v_w_in,
                 pool_w=v_pool_w, pool_scale=v_pool_scale, q_a_norm=v_q_a_norm, w_q_b=v_w_q_b,
                 kv_a_norm=v_kv_a_norm, w_kv_b=v_w_kv_b, w_out=v_w_out, ffn2_norm=v_ffn2_norm,
                 ffn2_w_gate=v_ffn2_w_gate, ffn2_w_up=v_ffn2_w_up, ffn2_w_down=v_ffn2_w_down,
                 final_norm=v_final_norm)
    order = list(wts)
    nl = ada_w.shape[0]
    seq = x.shape[1]
    me = 4 * lax.axis_index("x") + 2 * lax.axis_index("y") + lax.axis_index("c")
    ada_cols = ada_w.shape[2]

    def after_token(t, token):
        return t + token[0:1, 0:1].astype(t.dtype)

    packs = [_pack_shard_layer(l, wts) for l in range(nl)]

    c_all = _all_gather(jnp.broadcast_to(c, (8, D_MODEL)), "gather_c")[::8]

    ada_b_mine = lax.dynamic_slice_in_dim(ada_b, me * ada_cols, ada_cols, axis=1).reshape(nl, 1, ada_cols)
    mod_part = _ada_mod(c_all, ada_w, ada_b_mine, "ada_mod")
    mod_all = _all_gather(mod_part.reshape(nl * N_DEV, ada_cols), "gather_mod")
    mod_all = mod_all.reshape(N_DEV, nl, N_DEV, ada_cols)
    mod = lax.dynamic_index_in_dim(mod_all, me, axis=2, keepdims=False)
    mod = mod.transpose(1, 0, 2).reshape(nl, N_MOD, 1, D_MODEL)

    flight_a = _gather_start(packs[0][:SPLIT_AB], ROWS_A, mod, "gather_start_0a")
    flight_b = _gather_start(packs[0][SPLIT_AB:], ROWS_B, flight_a[4], "gather_start_0b")
    last_start = flight_b[4]
    if nl > 1:
        in_flight = _gather_start(packs[1], ROWS_ALL, last_start, "gather_start_1")
        last_start = in_flight[4]

    cos, sin = _rope_tables(after_token(positions.reshape(seq, 1), last_start), "rope_tables")

    def vec(t):
        return t.reshape(1, -1)

    def landed(flight, rows_list, after, tag):
        send_sems, recv_sems, pk, lands, _ = flight
        pk, lands = _gather_wait(send_sems, recv_sems, pk, lands, after, f"gather_wait_{tag}")
        return _gather_finish(pk, rows_list, lands, "gather_finish")

    xs = x.reshape(seq, D_MODEL)
    saved = []
    for l in range(nl):
        norm1 = vec(ffn1_norm[l])
        if l == 0:
            lands = landed(flight_a, ROWS_A, cos, "0a")
        elif l + 1 < nl:
            in_flight = _gather_start(packs[l + 1], ROWS_ALL, lands[0], f"gather_start_{l + 1}")
            norm1 = after_token(norm1, in_flight[4])
        sv = {}

        def ffn_fwd(xin, norm, k0, wg, wu, wd, tag):
            h, a, b, t = _ffn_up(xin, norm, mod[l, k0], mod[l, k0 + 1], wg, wu, "ffn_up")
            y, xout = _mm(t, wd, "nn", "ffn_down", res=xin, gate=mod[l, k0 + 2], gate_factor=0.5)
            sv[tag] = dict(x=xin, h=h, a=a, b=b, t=t, y=y)
            return xout

        xs = ffn_fwd(xs, norm1, 0, lands[0], lands[1], lands[2], "f1")
        if l == 0:
            lands = lands + landed(flight_b, ROWS_B, xs, "0b")
        w = _full_weights(lands)
        sv["w"] = w

        h2, z = _norm_mm(xs, vec(mix_norm[l]), mod[l, 3], mod[l, 4], w["in"], "mix_in")
        y_pool, diff = _pool_fwd(z, pool_w[l], vec(pool_scale[l]), "pool_fwd")
        q, k, v, cqn, ckvn = _qkv_fwd(z, vec(q_a_norm[l]), vec(kv_a_norm[l]), w["q"], w["kv"], cos, sin, "qkv_fwd")
        o, lse = _attn_fwd(q, k, v, "attn_fwd")
        ycat = jnp.concatenate([y_pool, o.astype(BF16)], axis=1)
        y2, xmix = _mm(ycat, w["out"], "nn", "mix_out", res=xs, gate=mod[l, 5], gate_factor=1.0)
        sv["mix"] = dict(x=xs, h=h2, z=z, diff=diff, q=q, k=k, v=v, cqn=cqn, ckvn=ckvn, lse=lse, ycat=ycat, y=y2)
        xs = xmix

        xs = ffn_fwd(xs, vec(ffn2_norm[l]), 6, w["g2"], w["u2"], w["d2"], "f2")
        saved.append(sv)
        if l + 1 < nl:
            lands = landed(in_flight, ROWS_ALL, xs, l + 1)

    loss_part, dx, d_final = _loss_head(xs, vec(final_norm), loss_target.reshape(seq, D_MODEL), "loss_head")

    small = {name: [None] * nl for name in ("ffn1_norm", "mix_norm", "ffn2_norm", "q_a_norm", "kv_a_norm",
                                            "pool_scale", "pool_w", "dmod")}
    core = lax.axis_index("c").astype(jnp.int32).reshape(1)
    chip = 2 * lax.axis_index("x") + lax.axis_index("y")
    exchanges = []

    def leave(srcs, rows_list, after, tag):
        return _pair_start(srcs, rows_list, after, f"pair_start_{tag}"), rows_list, tag

    def forward_on(pending, after, layer, row_off):
        (send_sems, recv_sems, srcs, land, _), rows_list, tag = pending
        srcs, land = _split_wait(send_sems, recv_sems, 1, srcs, land, after, f"pair_wait_{tag}")
        sums = _pair_sum(srcs, rows_list, land, core, "pair_sum")
        flight = _chip_exchange_start(sums, chip, after, f"exchange_start_{tag}")
        exchanges.append((flight, layer, row_off, tag))
        return flight[4]

    pending = None
    for l in reversed(range(nl)):
        sv = saved[l]
        w = sv["w"]
        dmod = [None] * N_MOD
        gr = {}
        gate3 = mod[l, 8] if pending is None else after_token(mod[l, 8], pending[0][4])

        def ffn_bwd(dxin, s_, norm, k0, wg, wu, wd, tag, gate, mid=None):
            dy, dmod[k0 + 2] = _gate_bwd(dxin, s_["y"], gate, 0.5, "gate_bwd")
            da, db, gr["d" + tag], gr["g" + tag], gr["u" + tag] = _ffn_bwd_cols(
                dy, s_["h"], s_["a"], s_["b"], s_["t"], wd, "ffn_bwd_cols")
            dh = _mm_pair(da, wg, db, wu, "ffn_bwd_dh", after=None if mid is None else mid(da))
            dxo, dmod[k0], dmod[k0 + 1], dnorm = _rm_bwd(dh, s_["x"], dxin, vec(norm), mod[l, k0 + 1], "rm_bwd")
            return dxo, dnorm

        dx, small["ffn2_norm"][l] = ffn_bwd(dx, sv["f2"], ffn2_norm[l], 6, w["g2"], w["u2"], w["d2"], "2", gate3)

        s_ = sv["mix"]
        gate2 = mod[l, 5]
        if pending is not None:
            gate2 = after_token(gate2, forward_on(pending, dx, l + 1, 0))
            pending = None
        dy, dmod[5] = _gate_bwd(dx, s_["y"], gate2, 1.0, "gate_bwd")
        gr["out"] = _mm(s_["ycat"], dy, "tn", "mix_out_dw", out_dtype=BF16, tm=256)
        dycat = _mm(dy, w["out"], "nt", "mix_out_dx")
        du, small["pool_w"][l], small["pool_scale"][l] = _pool_bwd(dycat, s_["diff"], pool_w[l], vec(pool_scale[l]), "pool_bwd")
        dq, dk, dv = _attn_bwd(s_["q"], s_["k"], s_["v"], s_["lse"], dycat, "attn_bwd")
        dz, dqb, dkvb, small["q_a_norm"][l], small["kv_a_norm"][l] = _qkv_bwd(
            dq, dk, dv, du, s_["z"], vec(q_a_norm[l]), vec(kv_a_norm[l]), w["q"], w["kv"], cos, sin, "qkv_bwd")
        gr["q"] = _mm(dqb, s_["cqn"], "tn", "q_b_dw", out_dtype=BF16, tm=256)
        gr["kv"] = _mm(dkvb, s_["ckvn"], "tn", "kv_b_dw", out_dtype=BF16, tm=256)
        gr["in"] = _mm(dz, s_["h"], "tn", "mix_in_dw", out_dtype=BF16, tm=256)
        dh2 = _mm(dz, w["in"], "nn", "mix_in_dx")
        dx, dmod[3], dmod[4], small["mix_norm"][l] = _rm_bwd(dh2, s_["x"], dx, vec(mix_norm[l]), mod[l, 4], "rm_bwd")

        gate1, mid = mod[l, 2], None
        if l == 0:
            pending_b = leave(_grad_sources_b(gr), ROWS_B, dx, "0b")
            gate1 = after_token(gate1, pending_b[0][4])
            mid = lambda da: forward_on(pending_b, da, 0, SPLIT_AB)
        dx, small["ffn1_norm"][l] = ffn_bwd(dx, sv["f1"], ffn1_norm[l], 0, w["g1"], w["u1"], w["d1"], "1", gate1, mid)

        small["dmod"][l] = jnp.concatenate(dmod, axis=0)
        if l > 0:
            pending = leave([gr["g1"], gr["u1"], gr["d1"]] + _grad_sources_b(gr), ROWS_ALL, dx, l)

    grad_x = dx.reshape(x.shape)
    pending_a = leave([gr["g1"], gr["u1"], gr["d1"]], ROWS_A, dx, "0a")

    layout, small_rows = _small_layout(nl)
    pieces = {
        "dmod": jnp.concatenate(small["dmod"], axis=0),
        "ffn1_norm": jnp.concatenate(small["ffn1_norm"], axis=0),
        "mix_norm": jnp.concatenate(small["mix_norm"], axis=0),
        "ffn2_norm": jnp.concatenate(small["ffn2_norm"], axis=0),
        "q_a_norm": jnp.concatenate(small["q_a_norm"], axis=0),
        "kv_a_norm": jnp.concatenate(small["kv_a_norm"], axis=0),
        "pool_scale": jnp.concatenate(small["pool_scale"], axis=0),
        "final_norm": d_final,
        "loss": jnp.broadcast_to(loss_part[0:1, 0:1], (1, D_MODEL)),
        "pool_w": jnp.stack(small["pool_w"]).reshape(-1, D_MODEL),
    }
    small_buf = jnp.concatenate([_to_rows(pieces[name]) for name in layout], axis=0)
    def landed_sums(gbuf, entries, after):
        for (send_sems, recv_sems, sums, recv, _), layer, row_off, tag in entries:
            _, recv = _split_wait(send_sems, recv_sems, N_CHIPS - 1, sums, recv, after, f"exchange_wait_{tag}")
            gbuf = _sum_slots_into(recv, gbuf, layer, row_off, "sum_grads")
        return gbuf

    gbuf = lax.empty((nl, ROWS_L, D_MODEL), F32)
    gbuf = landed_sums(gbuf, [e for e in exchanges if e[1] > 0], pending_a[0][4])
    small_all = _all_gather(small_buf, "gather_small", after=gbuf if nl > 1 else pending_a[0][4])
    small_all = small_all.reshape(N_DEV, small_rows, D_MODEL)
    token_0a = forward_on(pending_a, small_all, 0, 0)
    small_sum = _sum_slots(small_all, "sum_small", after=token_0a)

    def take(name, width=D_MODEL):
        off, n = layout[name]
        return small_sum[off:off + n, :width]

    grads = {}
    grads["ada_b"] = take("dmod").reshape(nl, N_MOD * D_MODEL)
    grads["ffn1_norm"], grads["mix_norm"], grads["ffn2_norm"] = take("ffn1_norm"), take("mix_norm"), take("ffn2_norm")
    grads["q_a_norm"], grads["kv_a_norm"] = take("q_a_norm", Q_LORA), take("kv_a_norm", KV_LORA)
    grads["pool_scale"] = take("pool_scale", POOL_WIDTH)
    grads["final_norm"] = take("final_norm").reshape(D_MODEL)
    grads["pool_w"] = take("pool_w").reshape(pool_w.shape)
    loss = take("loss")[0, 0]

    off, n = layout["dmod"]
    dmod_all = small_all[:, off:off + n].reshape(N_DEV, nl, N_MOD * D_MODEL)
    dmod_mine = lax.dynamic_slice_in_dim(dmod_all, me * ada_cols, ada_cols, axis=2)
    dmod_pad = jnp.pad(dmod_mine.transpose(1, 0, 2), ((0, 0), (0, LANE - N_DEV), (0, 0)))
    grads["ada_w"] = _ada_grad(jnp.pad(c_all, ((0, LANE - N_DEV), (0, 0))), dmod_pad, "ada_grad")

    updates = {name: _adamw_nd(wts[name], grads[name], mom_m[name], mom_v[name], "adamw") for name in grads}

    gbuf = landed_sums(gbuf, [e for e in exchanges if e[1] == 0], updates["ada_w"][0])

    def swap(t):
        return t.transpose(0, 2, 1)

    def same(t):
        return t

    for wname, off, view in (("ffn1_w_gate", OFF_G1, swap), ("ffn1_w_up", OFF_U1, swap), ("ffn1_w_down", OFF_D1, same),
                             ("ffn2_w_gate", OFF_G2, swap), ("ffn2_w_up", OFF_U2, swap), ("ffn2_w_down", OFF_D2, same),
                             ("w_out", OFF_OUT, same)):
        g, d_, nm, nv = _adamw_rows(view(wts[wname]), gbuf, off, view(mom_m[wname]), view(mom_v[wname]), "adamw_rows")
        grads[wname], updates[wname] = view(g), (view(d_), view(nm), view(nv))
    small_grads = {
        "w_kv_b": (gbuf[:, OFF_KV:OFF_KV + KV_SH_ROWS].reshape(nl, -1, KV_LORA).transpose(0, 2, 1), same),
        "w_in": (gbuf[:, OFF_IN:OFF_IN + IN_SH], swap),
        "w_q_b": (gbuf[:, OFF_Q:OFF_Q + Q_SH_ROWS].reshape(nl, -1, Q_LORA), swap),
    }
    for wname, (g, view) in small_grads.items():
        upd = _adamw_nd(view(wts[wname]), g, view(mom_m[wname]), view(mom_v[wname]), "adamw")
        grads[wname], updates[wname] = view(g), tuple(view(t) for t in upd)

    return (loss, grad_x, *[grads[n] for n in order], *[updates[n][0] for n in order],
            *[updates[n][1] for n in order], *[updates[n][2] for n in order])
```

```python
import math

import numpy as np
import jax
import jax.numpy as jnp
from jax import lax
from jax.experimental import pallas as pl
from jax.experimental.pallas import tpu as pltpu

F32 = jnp.float32
BF16 = jnp.bfloat16

N_DEV = 8
D_MODEL = 1024
D_FF = 2816
POOL_WIDTH = 512
POOL_WINDOWS = (2, 4, 8, 16)
POOL_GC = 128
N_HEADS = 4
QK_NOPE = 128
QK_ROPE = 64
V_HEAD = 128
QK_HEAD = QK_NOPE + QK_ROPE
HEAD_PAD = 256
Q_LORA = 384
KV_LORA = 256
IN_COLS = POOL_WIDTH + Q_LORA + KV_LORA + QK_ROPE
IN_PAD = 1280
ROPE_THETA = 10000.0
SOFTMAX_SCALE = 1.0 / math.sqrt(QK_HEAD)
EPS = 1e-6
N_MOD = 9

ADAM_LR = 0.001
ADAM_B1 = 0.9
ADAM_B2 = 0.999
ADAM_EPS = 1e-08
ADAM_WD = 0.01
ADAM_STEP = 10

LANE = 128
VMEM_LIMIT = 56 * 1024 * 1024

FF_SH = D_FF // N_DEV
OFF_G1, OFF_U1, OFF_D1 = 0, FF_SH, 2 * FF_SH
OFF_G2, OFF_U2, OFF_D2 = 3 * FF_SH, 4 * FF_SH, 5 * FF_SH
OFF_OUT = 6 * FF_SH
OFF_KV = OFF_OUT + 128
OFF_IN = OFF_KV + 32
OFF_Q = OFF_IN + 160
Q_PAD_ROWS = 64
ROWS_L = OFF_Q + Q_PAD_ROWS
IN_SH = IN_COLS // N_DEV
Q_SH_ROWS = (N_HEADS * QK_HEAD // N_DEV) * Q_LORA // D_MODEL
KV_SH_ROWS = (N_HEADS * (QK_NOPE + V_HEAD) // N_DEV) * KV_LORA // D_MODEL


def _tile(dim, target):
    if dim <= target:
        return dim
    best = None
    for t in range(LANE, target + 1, LANE):
        if dim % t == 0:
            best = t
    assert best is not None, (dim, target)
    return best


def _params(sem):
    return pltpu.CompilerParams(dimension_semantics=sem, vmem_limit_bytes=VMEM_LIMIT)


def _mesh_pos():
    return lax.axis_index("x"), lax.axis_index("y"), lax.axis_index("c")


def _all_gather(x, name, after=None):
    m, n = x.shape

    def body(x_ref, *refs):
        out_ref, send_sems, recv_sems, local_sem = refs[-4:]
        px, py, pc = _mesh_pos()
        me, sibling = (px, py, pc), (px, py, 1 - pc)
        chips = [(1 - px, py), (px, 1 - py), (1 - px, 1 - py)]

        def rows(bx, by, bc):
            return out_ref.at[pl.ds((4 * bx + 2 * by + bc) * m, m), :]

        def copy(k, block, to, src=None):
            return pltpu.make_async_remote_copy(
                src_ref=rows(*block) if src is None else src, dst_ref=rows(*block),
                send_sem=send_sems.at[k], recv_sem=recv_sems.at[k],
                device_id=to, device_id_type=pl.DeviceIdType.MESH)

        mine = pltpu.make_async_copy(x_ref, rows(*me), local_sem)
        mine.start()
        first = [copy(0, me, sibling, src=x_ref)]
        first += [copy(1 + j, me, (*chip, pc), src=x_ref) for j, chip in enumerate(chips)]
        for cp in first:
            cp.start()
        passed = [copy(4 + j, (*chip, pc), sibling) for j, chip in enumerate(chips)]
        for j, chip in enumerate(chips):
            copy(1 + j, (*chip, pc), me).wait_recv()
            passed[j].start()
        copy(0, sibling, me).wait_recv()
        for j, chip in enumerate(chips):
            copy(4 + j, (*chip, 1 - pc), me).wait_recv()
        for cp in first + passed:
            cp.wait_send()
        mine.wait()

    hbm = pl.BlockSpec(memory_space=pltpu.HBM)
    in_specs, args = [hbm], [x]
    if after is not None:
        in_specs.append(pl.BlockSpec(memory_space=pl.ANY))
        args.append(after)
    return pl.pallas_call(
        body, name=name,
        out_shape=jax.ShapeDtypeStruct((N_DEV * m, n), x.dtype),
        in_specs=in_specs, out_specs=hbm,
        scratch_shapes=[pltpu.SemaphoreType.DMA((7,)), pltpu.SemaphoreType.DMA((7,)),
                        pltpu.SemaphoreType.DMA],
    )(*args)


SMALL_ROWS = ROWS_L - OFF_KV
ROWS_A = [FF_SH] * 3
ROWS_B = [FF_SH] * 3 + [128, SMALL_ROWS]
ROWS_ALL = ROWS_A + ROWS_B
SPLIT_AB = sum(ROWS_A)
HBM_SPEC = pl.BlockSpec(memory_space=pltpu.HBM)
SEM_SPEC = pl.BlockSpec(memory_space=pltpu.SEMAPHORE)
ANY_SPEC = pl.BlockSpec(memory_space=pl.ANY)
EFFECT = pltpu.SideEffectType.DATAFLOW_SIDE_EFFECTING


def _hbm(t):
    return pltpu.with_memory_space_constraint(t, pltpu.HBM)


def _whole_wait(ref, send_sem, recv_sem, peer):
    return pltpu.make_async_remote_copy(src_ref=ref, dst_ref=ref, send_sem=send_sem, recv_sem=recv_sem,
                                        device_id=peer, device_id_type=pl.DeviceIdType.MESH)


def _offsets(rows_list):
    return [sum(rows_list[:i]) for i in range(len(rows_list))]


def _gather_start(packed, rows_list, after, name):
    n = len(rows_list)
    offs = _offsets(rows_list)
    lands = [_hbm(lax.empty((N_DEV * rows, D_MODEL), BF16)) for rows in rows_list]

    def body(packed_ref, *refs):
        land = refs[:n]
        send_sems, recv_sems = refs[n + 1], refs[n + 2]
        token = refs[-1]
        px, py, pc = _mesh_pos()
        me = 4 * px + 2 * py + pc
        peers = [(px, py, 1 - pc), (1 - px, py, pc), (px, 1 - py, pc), (1 - px, 1 - py, pc)]
        for k, peer in enumerate(peers):
            for off, rows, land_ref in zip(offs, rows_list, land):
                pltpu.make_async_remote_copy(
                    src_ref=packed_ref.at[pl.ds(off, rows), :], dst_ref=land_ref.at[pl.ds(me * rows, rows), :],
                    send_sem=send_sems.at[k], recv_sem=recv_sems.at[k],
                    device_id=peer, device_id_type=pl.DeviceIdType.MESH).start()
        token[...] = jnp.zeros_like(token)

    outs = pl.pallas_call(
        body, name=name,
        out_shape=(pltpu.SemaphoreType.DMA((4,)), pltpu.SemaphoreType.DMA((4,)), pltpu.HBM(packed.shape, BF16),
                   *[pltpu.HBM(t.shape, BF16) for t in lands], jax.ShapeDtypeStruct((8, LANE), F32)),
        in_specs=(HBM_SPEC,) * (1 + n) + (ANY_SPEC,),
        out_specs=(SEM_SPEC, SEM_SPEC) + (HBM_SPEC,) * (1 + n) + (pl.BlockSpec(memory_space=pltpu.VMEM),),
        input_output_aliases={i: 2 + i for i in range(1 + n)},
        compiler_params=pltpu.CompilerParams(has_side_effects=EFFECT),
    )(_hbm(packed), *lands, after)
    return outs[0], outs[1], outs[2], list(outs[3:3 + n]), outs[-1]


def _gather_wait(send_sems, recv_sems, packed, lands, after, name):
    n = len(lands)

    def body(packed_ref, *refs):
        s_sems, r_sems = refs[n], refs[n + 1]
        me = _mesh_pos()
        for k in range(4):
            cp = _whole_wait(packed_ref, s_sems.at[k], r_sems.at[k], me)
            cp.wait_send()
            cp.wait_recv()

    outs = pl.pallas_call(
        body, name=name,
        out_shape=(pltpu.HBM(packed.shape, BF16), *[pltpu.HBM(t.shape, BF16) for t in lands]),
        in_specs=(HBM_SPEC,) * (1 + n) + (SEM_SPEC, SEM_SPEC, ANY_SPEC),
        out_specs=(HBM_SPEC,) * (1 + n),
        input_output_aliases={i: i for i in range(1 + n)},
        compiler_params=pltpu.CompilerParams(has_side_effects=EFFECT),
    )(packed, *lands, send_sems, recv_sems, after)
    return outs[0], list(outs[1:])


def _gather_finish(packed, rows_list, lands, name):
    n = len(rows_list)
    offs = _offsets(rows_list)

    def body(packed_ref, *refs):
        land = refs[n:2 * n]
        send_sems, recv_sems, stage, stage_sem = refs[2 * n:]
        px, py, pc = _mesh_pos()
        me = 4 * px + 2 * py + pc
        sibling = (px, py, 1 - pc)
        load = pltpu.make_async_copy(packed_ref, stage, stage_sem)
        load.start()
        load.wait()
        for off, rows, land_ref in zip(offs, rows_list, land):
            pltpu.make_async_copy(stage.at[pl.ds(off, rows), :], land_ref.at[pl.ds(me * rows, rows), :],
                                  stage_sem).start()
        for j, (cx, cy) in enumerate([(1 - px, py), (px, 1 - py), (1 - px, 1 - py)]):
            block = 4 * cx + 2 * cy + pc
            for rows, land_ref in zip(rows_list, land):
                blk = land_ref.at[pl.ds(block * rows, rows), :]
                pltpu.make_async_remote_copy(src_ref=blk, dst_ref=blk, send_sem=send_sems.at[j],
                                             recv_sem=recv_sems.at[j], device_id=sibling,
                                             device_id_type=pl.DeviceIdType.MESH).start()
        for j in range(3):
            cp = _whole_wait(packed_ref, send_sems.at[j], recv_sems.at[j], sibling)
            cp.wait_recv()
            cp.wait_send()
        pltpu.make_async_copy(stage, packed_ref, stage_sem).wait()

    outs = pl.pallas_call(
        body, name=name,
        out_shape=tuple(jax.ShapeDtypeStruct(t.shape, BF16) for t in lands),
        in_specs=(HBM_SPEC,) * (1 + n), out_specs=(HBM_SPEC,) * n,
        input_output_aliases={1 + i: i for i in range(n)},
        scratch_shapes=[pltpu.SemaphoreType.DMA((3,)), pltpu.SemaphoreType.DMA((3,)),
                        pltpu.VMEM(packed.shape, BF16), pltpu.SemaphoreType.DMA],
    )(packed, *lands)
    return list(outs)


N_CHIPS = 4


def _pair_start(srcs, rows_list, after, name):
    n = len(rows_list)
    offs = _offsets(rows_list)
    land = lax.empty((N_CHIPS, sum(rows_list), D_MODEL), BF16)

    def body(*refs):
        src, land_ref = refs[:n], refs[n]
        send_sems, recv_sems = refs[n + 2], refs[n + 3]
        token = refs[-1]
        px, py, pc = _mesh_pos()
        for k in range(N_CHIPS):
            block = 2 * k + (1 - pc)
            for off, rows, src_ref in zip(offs, rows_list, src):
                pltpu.make_async_remote_copy(
                    src_ref=src_ref.at[pl.ds(block * rows, rows), :], dst_ref=land_ref.at[k, pl.ds(off, rows), :],
                    send_sem=send_sems.at[0], recv_sem=recv_sems.at[0],
                    device_id=(px, py, 1 - pc), device_id_type=pl.DeviceIdType.MESH).start()
        token[...] = jnp.zeros_like(token)

    outs = pl.pallas_call(
        body, name=name,
        out_shape=(pltpu.SemaphoreType.DMA((1,)), pltpu.SemaphoreType.DMA((1,)),
                   *[pltpu.HBM(t.shape, BF16) for t in srcs], pltpu.HBM(land.shape, BF16),
                   jax.ShapeDtypeStruct((8, LANE), F32)),
        in_specs=(HBM_SPEC,) * (n + 1) + (ANY_SPEC,),
        out_specs=(SEM_SPEC, SEM_SPEC) + (HBM_SPEC,) * (n + 1) + (pl.BlockSpec(memory_space=pltpu.VMEM),),
        input_output_aliases={i: 2 + i for i in range(n + 1)},
        compiler_params=pltpu.CompilerParams(has_side_effects=EFFECT),
    )(*[_hbm(t) for t in srcs], _hbm(land), after)
    return outs[0], outs[1], list(outs[2:2 + n]), outs[2 + n], outs[-1]


def _split_wait(send_sems, recv_sems, n_sems, srcs, land, after, name):
    n = len(srcs)

    def body(*refs):
        land_ref = refs[n]
        s_sems, r_sems = refs[n + 1], refs[n + 2]
        me = _mesh_pos()
        for k in range(n_sems):
            cp = _whole_wait(land_ref.at[0] if n_sems > 1 else land_ref, s_sems.at[k], r_sems.at[k], me)
            cp.wait_send()
            cp.wait_recv()

    outs = pl.pallas_call(
        body, name=name,
        out_shape=(*[pltpu.HBM(t.shape, BF16) for t in srcs], pltpu.HBM(land.shape, BF16)),
        in_specs=(HBM_SPEC,) * (n + 1) + (SEM_SPEC, SEM_SPEC, ANY_SPEC),
        out_specs=(HBM_SPEC,) * (n + 1),
        input_output_aliases={i: i for i in range(n + 1)},
        compiler_params=pltpu.CompilerParams(has_side_effects=EFFECT),
    )(*srcs, land, send_sems, recv_sems, after)
    return list(outs[:n]), outs[n]


def _pair_sum(srcs, rows_list, land, core, name):
    n = len(rows_list)
    offs = _offsets(rows_list)
    total = sum(rows_list)

    def body(core_ref, *refs):
        src, land_ref, out_ref = refs[:n], refs[n], refs[n + 1]
        for off, rows, src_ref in zip(offs, rows_list, src):
            out_ref[pl.ds(off, rows), :] = (src_ref[...].astype(F32)
                                            + land_ref[pl.ds(off, rows), :].astype(F32)).astype(BF16)

    slot = pl.BlockSpec((None, total, D_MODEL), lambda k, c: (k, 0, 0))
    grid_spec = pltpu.PrefetchScalarGridSpec(
        num_scalar_prefetch=1, grid=(N_CHIPS,),
        in_specs=[pl.BlockSpec((rows, D_MODEL), lambda k, c: (2 * k + c[0], 0)) for rows in rows_list] + [slot],
        out_specs=slot)
    return pl.pallas_call(
        body, name=name, grid_spec=grid_spec,
        out_shape=jax.ShapeDtypeStruct((N_CHIPS, total, D_MODEL), BF16),
        compiler_params=_params(("parallel",)),
    )(core, *srcs, land)


def _chip_exchange_start(sums, chip, after, name):
    own = lax.dynamic_index_in_dim(sums, chip, axis=0, keepdims=True)
    recv = lax.dynamic_update_slice_in_dim(lax.empty(sums.shape, BF16), own, chip, axis=0)

    def body(sums_ref, recv_ref, after_ref, send_sems, recv_sems, sums_thru, recv_thru, token):
        px, py, pc = _mesh_pos()
        for k in range(1, N_CHIPS):
            qx = 1 - px if k & 2 else px
            qy = 1 - py if k & 1 else py
            pltpu.make_async_remote_copy(
                src_ref=sums_ref.at[2 * qx + qy], dst_ref=recv_ref.at[2 * px + py],
                send_sem=send_sems.at[k - 1], recv_sem=recv_sems.at[k - 1],
                device_id=(qx, qy, pc), device_id_type=pl.DeviceIdType.MESH).start()
        token[...] = jnp.zeros_like(token)

    outs = pl.pallas_call(
        body, name=name,
        out_shape=(pltpu.SemaphoreType.DMA((N_CHIPS - 1,)), pltpu.SemaphoreType.DMA((N_CHIPS - 1,)),
                   pltpu.HBM(sums.shape, BF16), pltpu.HBM(recv.shape, BF16), jax.ShapeDtypeStruct((8, LANE), F32)),
        in_specs=(HBM_SPEC, HBM_SPEC, ANY_SPEC),
        out_specs=(SEM_SPEC, SEM_SPEC, HBM_SPEC, HBM_SPEC, pl.BlockSpec(memory_space=pltpu.VMEM)),
        input_output_aliases={0: 2, 1: 3},
        compiler_params=pltpu.CompilerParams(has_side_effects=EFFECT),
    )(_hbm(sums), _hbm(recv), after)
    return outs[0], outs[1], [outs[2]], outs[3], outs[4]


def _sum_slots_into(recv, buf, layer, row_off, name):
    slots, r, n = recv.shape
    tr = _row_tile(math.gcd(r, row_off) if row_off else r, 512)
    first = row_off // tr

    def body(in_ref, buf_ref, out_ref):
        acc = in_ref[0].astype(F32)
        for j in range(1, slots):
            acc = acc + in_ref[j].astype(F32)
        out_ref[...] = acc

    return pl.pallas_call(
        body, name=name, grid=(r // tr,), out_shape=jax.ShapeDtypeStruct(buf.shape, F32),
        in_specs=[pl.BlockSpec((slots, tr, n), lambda i: (0, i, 0)), ANY_SPEC],
        out_specs=pl.BlockSpec((None, tr, n), lambda i: (layer, first + i, 0)),
        input_output_aliases={1: 0},
        compiler_params=_params(("parallel",)),
    )(recv, buf)


def _sum_slots(recv, name, after=None):
    _, r, n = recv.shape
    tr = _row_tile(r, 512)

    def body(in_ref, *refs):
        acc = in_ref[0].astype(F32)
        for j in range(1, N_DEV):
            acc = acc + in_ref[j].astype(F32)
        refs[-1][...] = acc

    grid = (r // tr,)
    in_specs, out_spec = [pl.BlockSpec((N_DEV, tr, n), lambda i: (0, i, 0))], pl.BlockSpec((tr, n), lambda i: (i, 0))
    args = [recv]
    if after is not None:
        in_specs.append(ANY_SPEC)
        args.append(after)
    return pl.pallas_call(
        body, name=name, grid=grid,
        out_shape=jax.ShapeDtypeStruct((r, n), F32),
        in_specs=in_specs, out_specs=out_spec,
        compiler_params=_params(("parallel",)),
    )(*args)


def _row_tile(rows, target):
    if rows <= target:
        return rows
    best = None
    for t in range(16, target + 1, 16):
        if rows % t == 0:
            best = t
    assert best is not None, rows
    return best


_DIMS = {"nn": ((1,), (0,)), "nt": ((1,), (1,)), "tn": ((0,), (0,))}


def _mm(a, b, mode, name, out_dtype=F32, res=None, gate=None, gate_factor=1.0, tm=512, tn=1408, after=None):
    assert (res is None) == (gate is None)
    if mode == "tn":
        kdim, m = a.shape
    else:
        m, kdim = a.shape
    n = b.shape[0] if mode == "nt" else b.shape[1]
    tm, tn = _tile(m, tm), _tile(n, tn)
    a_spec = (pl.BlockSpec((kdim, tm), lambda i, j: (0, i)) if mode == "tn"
              else pl.BlockSpec((tm, kdim), lambda i, j: (i, 0)))
    b_spec = (pl.BlockSpec((tn, kdim), lambda i, j: (j, 0)) if mode == "nt"
              else pl.BlockSpec((kdim, tn), lambda i, j: (0, j)))
    o_spec = pl.BlockSpec((tm, tn), lambda i, j: (i, j))
    dims = (_DIMS[mode], ((), ()))
    has_res = res is not None

    def body(a_ref, b_ref, *refs):
        y = lax.dot_general(a_ref[...].astype(BF16), b_ref[...].astype(BF16), dims,
                            preferred_element_type=F32)
        if has_res:
            res_ref, gate_ref = refs[0], refs[1]
            y_ref, o_ref = refs[-2], refs[-1]
            y_ref[...] = y.astype(BF16)
            o_ref[...] = res_ref[...] + (gate_factor * gate_ref[...]) * y
        else:
            refs[-1][...] = y.astype(out_dtype)

    in_specs, args = [a_spec, b_spec], [a, b]
    if has_res:
        in_specs += [o_spec, pl.BlockSpec((1, tn), lambda i, j: (0, j))]
        args += [res, gate]
        out_shape = (jax.ShapeDtypeStruct((m, n), BF16), jax.ShapeDtypeStruct((m, n), F32))
        out_specs = (o_spec, o_spec)
    else:
        out_shape, out_specs = jax.ShapeDtypeStruct((m, n), out_dtype), o_spec
    if after is not None:
        in_specs.append(ANY_SPEC)
        args.append(after)
    return pl.pallas_call(
        body, name=name, grid=(m // tm, n // tn), out_shape=out_shape,
        in_specs=in_specs, out_specs=out_specs,
        compiler_params=_params(("parallel", "parallel")),
    )(*args)


def _vec_spec(width):
    return pl.BlockSpec((1, width), lambda i: (0, 0))


def _rm_bwd(dh, x, dres, gw, scale, name, below=None):
    s, d = x.shape
    ts = _tile(s, 256)
    factor = None if below is None else below[2]

    def body(dh_ref, x_ref, dres_ref, gw_ref, sc_ref, *refs):
        dx_ref, dsh_ref, dsc_ref, dgw_ref = refs[-6:-2] if below is not None else refs[-4:]

        @pl.when(pl.program_id(0) == 0)
        def _():
            dsh_ref[...] = jnp.zeros_like(dsh_ref)
            dsc_ref[...] = jnp.zeros_like(dsc_ref)
            dgw_ref[...] = jnp.zeros_like(dgw_ref)
            if below is not None:
                refs[-1][...] = jnp.zeros_like(refs[-1])

        xv, dhv, gwv = x_ref[...], dh_ref[...], gw_ref[...]
        r = lax.rsqrt(jnp.mean(xv * xv, axis=-1, keepdims=True) + EPS)
        xn = xv * r
        y = xn * gwv
        dsh_ref[...] += jnp.sum(dhv, axis=0, keepdims=True)
        dsc_ref[...] += jnp.sum(dhv * y, axis=0, keepdims=True)
        dy = dhv * (1 + sc_ref[...])
        dgw_ref[...] += jnp.sum(dy * xn, axis=0, keepdims=True)
        dxn = dy * gwv
        dx = dres_ref[...] + r * (dxn - xn * jnp.mean(dxn * xn, axis=-1, keepdims=True))
        dx_ref[...] = dx
        if below is not None:
            yb_ref, gb_ref, dyb_ref, dgb_ref = refs[0], refs[1], refs[-2], refs[-1]
            dyb_ref[...] = ((factor * gb_ref[...]) * dx).astype(BF16)
            dgb_ref[...] += jnp.sum((factor * dx) * yb_ref[...].astype(F32), axis=0, keepdims=True)

    row = pl.BlockSpec((ts, d), lambda i: (i, 0))
    vec = jax.ShapeDtypeStruct((1, d), F32)
    in_specs, args = [row, row, row, _vec_spec(d), _vec_spec(d)], [dh, x, dres, gw, scale]
    out_shape = [jax.ShapeDtypeStruct((s, d), F32), vec, vec, vec]
    out_specs = [row, _vec_spec(d), _vec_spec(d), _vec_spec(d)]
    if below is not None:
        in_specs += [row, _vec_spec(d)]
        args += [below[0], below[1]]
        out_shape += [jax.ShapeDtypeStruct((s, d), BF16), vec]
        out_specs += [row, _vec_spec(d)]
    return pl.pallas_call(
        body, name=name, grid=(s // ts,), out_shape=tuple(out_shape),
        in_specs=in_specs, out_specs=tuple(out_specs),
        compiler_params=_params(("arbitrary",)),
    )(*args)


def _gate_bwd(dx, y, gate, factor, name):
    s, d = dx.shape
    ts = _tile(s, 256)

    def body(dx_ref, y_ref, g_ref, dy_ref, dg_ref):
        @pl.when(pl.program_id(0) == 0)
        def _():
            dg_ref[...] = jnp.zeros_like(dg_ref)

        dxv = dx_ref[...]
        dy_ref[...] = ((factor * g_ref[...]) * dxv).astype(BF16)
        dg_ref[...] += jnp.sum((factor * dxv) * y_ref[...].astype(F32), axis=0, keepdims=True)

    row = pl.BlockSpec((ts, d), lambda i: (i, 0))
    return pl.pallas_call(
        body, name=name, grid=(s // ts,),
        out_shape=(jax.ShapeDtypeStruct((s, d), BF16), jax.ShapeDtypeStruct((1, d), F32)),
        in_specs=[row, row, _vec_spec(d)], out_specs=(row, _vec_spec(d)),
        compiler_params=_params(("arbitrary",)),
    )(dx, y, gate)


def _norm_mm(x, gw, shift, scale, w, name, tm=1024):
    s, d = x.shape
    n = w.shape[0]
    tm = _tile(s, tm)

    def body(x_ref, gw_ref, sh_ref, sc_ref, w_ref, h_ref, z_ref):
        xv = x_ref[...]
        r = lax.rsqrt(jnp.mean(xv * xv, axis=-1, keepdims=True) + EPS)
        hb = (((xv * r) * gw_ref[...]) * (1 + sc_ref[...]) + sh_ref[...]).astype(BF16)
        h_ref[...] = hb
        z_ref[...] = lax.dot_general(hb, w_ref[...], (((1,), (1,)), ((), ())), preferred_element_type=F32)

    row = pl.BlockSpec((tm, d), lambda i: (i, 0))
    return pl.pallas_call(
        body, name=name, grid=(s // tm,),
        out_shape=(jax.ShapeDtypeStruct((s, d), BF16), jax.ShapeDtypeStruct((s, n), F32)),
        in_specs=[row, _vec_spec(d), _vec_spec(d), _vec_spec(d), pl.BlockSpec((n, d), lambda i: (0, 0))],
        out_specs=(row, pl.BlockSpec((tm, n), lambda i: (i, 0))),
        compiler_params=_params(("parallel",)),
    )(x, gw, shift, scale, w)


FFN_TM, FFN_TF = 2048, 256


def _ffn_up(x, gw, shift, scale, wg, wu, name):
    s, d = x.shape
    f = wg.shape[0]
    tm, tf = _tile(s, FFN_TM), _tile(f, FFN_TF)
    nt = (((1,), (1,)), ((), ()))

    def body(x_ref, gw_ref, sh_ref, sc_ref, wg_ref, wu_ref, h_ref, a_ref, b_ref, t_ref):
        @pl.when(pl.program_id(1) == 0)
        def _():
            xv = x_ref[...]
            r = lax.rsqrt(jnp.mean(xv * xv, axis=-1, keepdims=True) + EPS)
            h_ref[...] = (((xv * r) * gw_ref[...]) * (1 + sc_ref[...]) + sh_ref[...]).astype(BF16)

        hb = h_ref[...]
        av = lax.dot_general(hb, wg_ref[...], nt, preferred_element_type=F32)
        bv = lax.dot_general(hb, wu_ref[...], nt, preferred_element_type=F32)
        a_ref[...] = av.astype(BF16)
        b_ref[...] = bv.astype(BF16)
        t_ref[...] = ((av * jax.nn.sigmoid(av)) * bv).astype(BF16)

    row = pl.BlockSpec((tm, d), lambda i, j: (i, 0))
    vec = pl.BlockSpec((1, d), lambda i, j: (0, 0))
    wblk = pl.BlockSpec((tf, d), lambda i, j: (j, 0))
    blk = pl.BlockSpec((tm, tf), lambda i, j: (i, j))
    wide = jax.ShapeDtypeStruct((s, f), BF16)
    return pl.pallas_call(
        body, name=name, grid=(s // tm, f // tf),
        out_shape=(jax.ShapeDtypeStruct((s, d), BF16), wide, wide, wide),
        in_specs=[row, vec, vec, vec, wblk, wblk], out_specs=(row, blk, blk, blk),
        compiler_params=_params(("parallel", "arbitrary")),
    )(x, gw, shift, scale, wg, wu)


def _ffn_bwd_cols(dy, h, a, b, t, wd, name, after=None):
    s, d = dy.shape
    f = wd.shape[0]
    tf = _tile(f, FFN_TF)
    nt = (((1,), (1,)), ((), ()))
    tn = (((0,), (0,)), ((), ()))

    def body(dy_ref, h_ref, a_ref, b_ref, t_ref, wd_ref, *refs):
        da_ref, db_ref, gd_ref, gg_ref, gu_ref = refs[-5:]
        dyb, hb = dy_ref[...], h_ref[...]
        dtv = lax.dot_general(dyb, wd_ref[...], nt, preferred_element_type=F32)
        av, bv = a_ref[...].astype(F32), b_ref[...].astype(F32)
        sg = jax.nn.sigmoid(av)
        dbv = (dtv * (av * sg)).astype(BF16)
        dav = ((dtv * bv) * (sg * (1 + av * (1 - sg)))).astype(BF16)
        da_ref[...] = dav
        db_ref[...] = dbv
        gd_ref[...] = lax.dot_general(t_ref[...], dyb, tn, preferred_element_type=F32).astype(BF16)
        gg_ref[...] = lax.dot_general(dav, hb, tn, preferred_element_type=F32).astype(BF16)
        gu_ref[...] = lax.dot_general(dbv, hb, tn, preferred_element_type=F32).astype(BF16)

    whole = pl.BlockSpec((s, d), lambda j: (0, 0))
    col = pl.BlockSpec((s, tf), lambda j: (0, j))
    wblk = pl.BlockSpec((tf, d), lambda j: (j, 0))
    wide, wgrad = jax.ShapeDtypeStruct((s, f), BF16), jax.ShapeDtypeStruct((f, d), BF16)
    in_specs, args = [whole, whole, col, col, col, wblk], [dy, h, a, b, t, wd]
    if after is not None:
        in_specs.append(ANY_SPEC)
        args.append(after)
    return pl.pallas_call(
        body, name=name, grid=(f // tf,), out_shape=(wide, wide, wgrad, wgrad, wgrad),
        in_specs=in_specs, out_specs=(col, col, wblk, wblk, wblk),
        compiler_params=_params(("parallel",)),
    )(*args)


def _mm_pair(a1, b1, a2, b2, name, tm=1024, tn=512, after=None):
    m, kdim = a1.shape
    n = b1.shape[1]
    tm, tn = _tile(m, tm), _tile(n, tn)

    def body(a1_ref, b1_ref, a2_ref, b2_ref, *refs):
        refs[-1][...] = (jnp.dot(a1_ref[...], b1_ref[...], preferred_element_type=F32)
                         + jnp.dot(a2_ref[...], b2_ref[...], preferred_element_type=F32))

    a_spec = pl.BlockSpec((tm, kdim), lambda i, j: (i, 0))
    b_spec = pl.BlockSpec((kdim, tn), lambda i, j: (0, j))
    in_specs, args = [a_spec, b_spec, a_spec, b_spec], [a1, b1, a2, b2]
    if after is not None:
        in_specs.append(ANY_SPEC)
        args.append(after)
    return pl.pallas_call(
        body, name=name, grid=(m // tm, n // tn), out_shape=jax.ShapeDtypeStruct((m, n), F32),
        in_specs=in_specs, out_specs=pl.BlockSpec((tm, tn), lambda i, j: (i, j)),
        compiler_params=_params(("parallel", "parallel")),
    )(*args)


def _pool_counts(s):
    return (lax.broadcasted_iota(jnp.int32, (s, POOL_GC), 0))


def _pool_fwd(z, pool_w, pool_scale, name):
    s = z.shape[0]

    def body(u_ref, w_ref, sc_ref, y_ref, diff_ref):
        t = lax.broadcasted_iota(jnp.int32, (s, POOL_GC), 0)
        for g, win in enumerate(POOL_WINDOWS):
            cols = slice(g * POOL_GC, (g + 1) * POOL_GC)
            u = u_ref[:, cols]
            acc, step = u, 1
            while step < win:
                acc = acc + jnp.where(t >= step, pltpu.roll(acc, step, 0), 0.0)
                step *= 2
            cnt = jnp.minimum(t + 1, win).astype(F32)
            diff = acc / cnt - u
            diff_ref[:, cols] = diff
            ypre = jnp.dot(diff.astype(BF16), w_ref[g].astype(BF16), preferred_element_type=F32)
            y_ref[:, cols] = (ypre * sc_ref[:, cols]).astype(BF16)

    return pl.pallas_call(
        body, name=name, grid=(1,),
        out_shape=(jax.ShapeDtypeStruct((s, POOL_WIDTH), BF16), jax.ShapeDtypeStruct((s, POOL_WIDTH), F32)),
        in_specs=[pl.BlockSpec((s, POOL_WIDTH), lambda i: (0, 0)),
                  pl.BlockSpec(pool_w.shape, lambda i: (0, 0, 0)),
                  pl.BlockSpec((1, POOL_WIDTH), lambda i: (0, 0))],
        out_specs=(pl.BlockSpec((s, POOL_WIDTH), lambda i: (0, 0)),
                   pl.BlockSpec((s, POOL_WIDTH), lambda i: (0, 0))),
        compiler_params=_params(("arbitrary",)),
    )(z, pool_w, pool_scale)


def _pool_bwd(dycat, diff, pool_w, pool_scale, name):
    s = diff.shape[0]

    def body(dy_ref, diff_ref, w_ref, sc_ref, du_ref, dw_ref, dsc_ref):
        t = lax.broadcasted_iota(jnp.int32, (s, POOL_GC), 0)
        for g, win in enumerate(POOL_WINDOWS):
            cols = slice(g * POOL_GC, (g + 1) * POOL_GC)
            dy, dfb, wb = dy_ref[:, cols], diff_ref[:, cols].astype(BF16), w_ref[g].astype(BF16)
            ypre = jnp.dot(dfb, wb, preferred_element_type=F32)
            dsc_ref[:, cols] = jnp.sum(dy * ypre, axis=0, keepdims=True)
            dypre = (dy * sc_ref[:, cols]).astype(BF16)
            ddiff = lax.dot_general(dypre, wb, (((1,), (1,)), ((), ())), preferred_element_type=F32)
            dw_ref[g] = lax.dot_general(dfb, dypre, (((0,), (0,)), ((), ())), preferred_element_type=F32)
            cnt = jnp.minimum(t + 1, win).astype(F32)
            acc, step = ddiff / cnt, 1
            while step < win:
                acc = acc + jnp.where(t < s - step, pltpu.roll(acc, s - step, 0), 0.0)
                step *= 2
            du_ref[:, cols] = acc - ddiff

    full = pl.BlockSpec((s, POOL_WIDTH), lambda i: (0, 0))
    return pl.pallas_call(
        body, name=name, grid=(1,),
        out_shape=(jax.ShapeDtypeStruct((s, POOL_WIDTH), F32),
                   jax.ShapeDtypeStruct(pool_w.shape, F32),
                   jax.ShapeDtypeStruct((1, POOL_WIDTH), F32)),
        in_specs=[full, full, pl.BlockSpec(pool_w.shape, lambda i: (0, 0, 0)),
                  pl.BlockSpec((1, POOL_WIDTH), lambda i: (0, 0))],
        out_specs=(full, pl.BlockSpec(pool_w.shape, lambda i: (0, 0, 0)),
                   pl.BlockSpec((1, POOL_WIDTH), lambda i: (0, 0))),
        compiler_params=_params(("arbitrary",)),
    )(dycat, diff, pool_w, pool_scale)


def _rope_tables(positions, name):
    s = positions.shape[0]
    ts = _tile(s, 512)
    freq = 1.0 / (ROPE_THETA ** (np.arange(0, QK_ROPE, 2, dtype=np.float32) / QK_ROPE))
    table = np.zeros((1, LANE), np.float32)
    table[0, :QK_ROPE // 2] = freq
    table[0, QK_ROPE // 2:QK_ROPE] = freq

    def body(pos_ref, f_ref, cos_ref, sin_ref):
        ang = pos_ref[...].astype(F32) * f_ref[...]
        cos_ref[...] = jnp.cos(ang)
        sin_ref[...] = jnp.sin(ang)

    out = jax.ShapeDtypeStruct((s, LANE), F32)
    blk = pl.BlockSpec((ts, LANE), lambda i: (i, 0))
    return pl.pallas_call(
        body, name=name, grid=(s // ts,), out_shape=(out, out),
        in_specs=[pl.BlockSpec((ts, 1), lambda i: (i, 0)), _vec_spec(LANE)], out_specs=(blk, blk),
        compiler_params=_params(("parallel",)),
    )(positions, jnp.asarray(table))


def _lane_mod64_low(shape):
    return (lax.broadcasted_iota(jnp.int32, shape, 1) % QK_ROPE) < (QK_ROPE // 2)


def _rope(x, cos, sin):
    rot = jnp.where(_lane_mod64_low(x.shape), -pltpu.roll(x, LANE - 32, 1), pltpu.roll(x, 32, 1))
    return x * cos + rot * sin


def _rope_t(dy, cos, sin):
    w = dy * sin
    rot_t = jnp.where(_lane_mod64_low(dy.shape), pltpu.roll(w, LANE - 32, 1), -pltpu.roll(w, 32, 1))
    return dy * cos + rot_t


def _plain_rms(x, g):
    r = lax.rsqrt(jnp.mean(x * x, axis=-1, keepdims=True) + EPS)
    return (x * r) * g, x * r, r


O_Q, O_KV, O_KR = POOL_WIDTH, POOL_WIDTH + Q_LORA, POOL_WIDTH + Q_LORA + KV_LORA


def _qkv_fwd(z, qn, kvn, wq, wkv, cos, sin, name):
    s = z.shape[0]
    ts = _tile(s, 256)

    def body(z_ref, qn_ref, kvn_ref, wq_ref, wkv_ref, cos_ref, sin_ref, q_ref, k_ref, v_ref, cqn_ref, ckvn_ref):
        cosv, sinv = cos_ref[...], sin_ref[...]
        cqn = _plain_rms(z_ref[:, O_Q:O_KV], qn_ref[...])[0].astype(BF16)
        ckvn = _plain_rms(z_ref[:, O_KV:O_KR], kvn_ref[...])[0].astype(BF16)
        cqn_ref[...] = cqn
        ckvn_ref[...] = ckvn
        nt = (((1,), (1,)), ((), ()))
        q = lax.dot_general(cqn, wq_ref[...], nt, preferred_element_type=F32)
        kv = lax.dot_general(ckvn, wkv_ref[...], nt, preferred_element_type=F32)
        kr = _rope(z_ref[:, O_KR:IN_PAD], cosv, sinv).astype(BF16)
        for h in range(N_HEADS):
            o = h * HEAD_PAD
            q_ref[:, o:o + QK_NOPE] = q[:, o:o + QK_NOPE].astype(BF16)
            q_ref[:, o + QK_NOPE:o + HEAD_PAD] = _rope(q[:, o + QK_NOPE:o + HEAD_PAD], cosv, sinv).astype(BF16)
            k_ref[:, o:o + QK_NOPE] = kv[:, o:o + QK_NOPE].astype(BF16)
            k_ref[:, o + QK_NOPE:o + HEAD_PAD] = kr
            v_ref[:, h * V_HEAD:(h + 1) * V_HEAD] = kv[:, o + QK_NOPE:o + HEAD_PAD].astype(BF16)

    def row(w):
        return pl.BlockSpec((ts, w), lambda i: (i, 0))

    def whole(arr):
        return pl.BlockSpec(arr.shape, lambda i: (0, 0))

    hp = N_HEADS * HEAD_PAD
    return pl.pallas_call(
        body, name=name, grid=(s // ts,),
        out_shape=(jax.ShapeDtypeStruct((s, hp), BF16), jax.ShapeDtypeStruct((s, hp), BF16),
                   jax.ShapeDtypeStruct((s, N_HEADS * V_HEAD), BF16),
                   jax.ShapeDtypeStruct((s, Q_LORA), BF16), jax.ShapeDtypeStruct((s, KV_LORA), BF16)),
        in_specs=[row(IN_PAD), whole(qn), whole(kvn), whole(wq), whole(wkv), row(LANE), row(LANE)],
        out_specs=(row(hp), row(hp), row(N_HEADS * V_HEAD), row(Q_LORA), row(KV_LORA)),
        compiler_params=_params(("parallel",)),
    )(z, qn, kvn, wq, wkv, cos, sin)


def _qkv_bwd(dq, dk, dv, du, z, qn, kvn, wq, wkv, cos, sin, name):
    s = z.shape[0]
    ts = _tile(s, 256)

    def norm_bwd(x, g, dy):
        _, xn, r = _plain_rms(x, g)
        dxn = dy * g
        return r * (dxn - xn * jnp.mean(dxn * xn, axis=-1, keepdims=True)), jnp.sum(dy * xn, axis=0, keepdims=True)

    def body(dq_ref, dk_ref, dv_ref, du_ref, z_ref, qn_ref, kvn_ref, wq_ref, wkv_ref, cos_ref, sin_ref,
             dz_ref, dqb_ref, dkvb_ref, dqn_ref, dkvn_ref):
        @pl.when(pl.program_id(0) == 0)
        def _():
            dqn_ref[...] = jnp.zeros_like(dqn_ref)
            dkvn_ref[...] = jnp.zeros_like(dkvn_ref)

        cosv, sinv = cos_ref[...], sin_ref[...]
        dkr = jnp.zeros((ts, LANE), F32)
        for h in range(N_HEADS):
            o = h * HEAD_PAD
            dqb_ref[:, o:o + QK_NOPE] = dq_ref[:, o:o + QK_NOPE].astype(BF16)
            dqb_ref[:, o + QK_NOPE:o + HEAD_PAD] = _rope_t(dq_ref[:, o + QK_NOPE:o + HEAD_PAD], cosv, sinv).astype(BF16)
            dkvb_ref[:, o:o + QK_NOPE] = dk_ref[:, o:o + QK_NOPE].astype(BF16)
            dkvb_ref[:, o + QK_NOPE:o + HEAD_PAD] = dv_ref[:, h * V_HEAD:(h + 1) * V_HEAD].astype(BF16)
            dkr = dkr + dk_ref[:, o + QK_NOPE:o + HEAD_PAD]
        dcqn = jnp.dot(dqb_ref[...], wq_ref[...], preferred_element_type=F32)
        dckvn = jnp.dot(dkvb_ref[...], wkv_ref[...], preferred_element_type=F32)
        dcq, dqn = norm_bwd(z_ref[:, O_Q:O_KV], qn_ref[...], dcqn)
        dckv, dkvn = norm_bwd(z_ref[:, O_KV:O_KR], kvn_ref[...], dckvn)
        dqn_ref[...] += dqn
        dkvn_ref[...] += dkvn
        dz_ref[:, 0:O_Q] = du_ref[...].astype(BF16)
        dz_ref[:, O_Q:O_KV] = dcq.astype(BF16)
        dz_ref[:, O_KV:O_KR] = dckv.astype(BF16)
        dz_ref[:, O_KR:IN_PAD] = _rope_t(dkr, cosv, sinv).astype(BF16)

    def row(w):
        return pl.BlockSpec((ts, w), lambda i: (i, 0))

    def whole(arr):
        return pl.BlockSpec(arr.shape, lambda i: (0, 0))

    hp = N_HEADS * HEAD_PAD
    return pl.pallas_call(
        body, name=name, grid=(s // ts,),
        out_shape=(jax.ShapeDtypeStruct((s, IN_PAD), BF16), jax.ShapeDtypeStruct((s, hp), BF16),
                   jax.ShapeDtypeStruct((s, hp), BF16),
                   jax.ShapeDtypeStruct((1, Q_LORA), F32), jax.ShapeDtypeStruct((1, KV_LORA), F32)),
        in_specs=[row(hp), row(hp), row(N_HEADS * V_HEAD), row(POOL_WIDTH), row(IN_PAD),
                  whole(qn), whole(kvn), whole(wq), whole(wkv), row(LANE), row(LANE)],
        out_specs=(row(IN_PAD), row(hp), row(hp), whole(qn), whole(kvn)),
        compiler_params=_params(("arbitrary",)),
    )(dq, dk, dv, du, z, qn, kvn, wq, wkv, cos, sin)


def _causal_scores(q, k, i, tq, klen):
    sc = lax.dot_general(q, k, (((1,), (1,)), ((), ())), preferred_element_type=F32) * SOFTMAX_SCALE
    qpos = i * tq + lax.broadcasted_iota(jnp.int32, (tq, klen), 0)
    kpos = lax.broadcasted_iota(jnp.int32, (tq, klen), 1)
    return jnp.where(qpos >= kpos, sc, -jnp.inf)


ATTN_TQ = 512
ATTN_SEGMENTS = 4


def _by_key_prefix(i, nq, tq, compute):
    nseg = min(ATTN_SEGMENTS, nq)
    per = nq // nseg
    for r in range(nseg):
        pl.when(i // per == r)(lambda r=r: compute((r + 1) * per * tq))


def _attn_fwd(q, k, v, name):
    s = q.shape[0]
    tq = _tile(s, ATTN_TQ)
    nq = s // tq

    def body(q_ref, k_ref, v_ref, o_ref, lse_ref):
        i = pl.program_id(1)

        def compute(klen):
            sc = _causal_scores(q_ref[...], k_ref[0:klen, :], i, tq, klen)
            mx = jnp.max(sc, axis=-1, keepdims=True)
            p = jnp.exp(sc - mx)
            den = jnp.sum(p, axis=-1, keepdims=True)
            o_ref[...] = jnp.dot((p / den).astype(BF16), v_ref[0:klen, :], preferred_element_type=F32)
            lse_ref[...] = mx + jnp.log(den)

        _by_key_prefix(i, nq, tq, compute)

    return pl.pallas_call(
        body, name=name, grid=(N_HEADS, s // tq),
        out_shape=(jax.ShapeDtypeStruct((s, N_HEADS * V_HEAD), F32), jax.ShapeDtypeStruct((N_HEADS, s, 1), F32)),
        in_specs=[pl.BlockSpec((tq, HEAD_PAD), lambda h, i: (i, h)),
                  pl.BlockSpec((s, HEAD_PAD), lambda h, i: (0, h)),
                  pl.BlockSpec((s, V_HEAD), lambda h, i: (0, h))],
        out_specs=(pl.BlockSpec((tq, V_HEAD), lambda h, i: (i, h)),
                   pl.BlockSpec((None, tq, 1), lambda h, i: (h, i, 0))),
        compiler_params=_params(("parallel", "parallel")),
    )(q, k, v)


def _attn_bwd(q, k, v, lse, dycat, name):
    s = q.shape[0]
    tq = _tile(s, ATTN_TQ)
    nq = s // tq
    tn_dims = (((0,), (0,)), ((), ()))

    def body(q_ref, k_ref, v_ref, lse_ref, do_ref, dq_ref, dk_ref, dv_ref):
        i = pl.program_id(1)

        @pl.when(i == 0)
        def _():
            dk_ref[...] = jnp.zeros_like(dk_ref)
            dv_ref[...] = jnp.zeros_like(dv_ref)

        def compute(klen):
            qv, kv_, dob = q_ref[...], k_ref[0:klen, :], do_ref[...].astype(BF16)
            sc = _causal_scores(qv, kv_, i, tq, klen)
            p = jnp.exp(sc - lse_ref[...])
            dp = lax.dot_general(dob, v_ref[0:klen, :], (((1,), (1,)), ((), ())), preferred_element_type=F32)
            ds = (p * (dp - jnp.sum(dp * p, axis=-1, keepdims=True)) * SOFTMAX_SCALE).astype(BF16)
            dq_ref[...] = jnp.dot(ds, kv_, preferred_element_type=F32)
            dk_ref[0:klen, :] += lax.dot_general(ds, qv, tn_dims, preferred_element_type=F32)
            dv_ref[0:klen, :] += lax.dot_general(p.astype(BF16), dob, tn_dims, preferred_element_type=F32)

        _by_key_prefix(i, nq, tq, compute)

    n_pool_blocks = POOL_WIDTH // V_HEAD
    return pl.pallas_call(
        body, name=name, grid=(N_HEADS, s // tq),
        out_shape=(jax.ShapeDtypeStruct((s, N_HEADS * HEAD_PAD), F32),
                   jax.ShapeDtypeStruct((s, N_HEADS * HEAD_PAD), F32),
                   jax.ShapeDtypeStruct((s, N_HEADS * V_HEAD), F32)),
        in_specs=[pl.BlockSpec((tq, HEAD_PAD), lambda h, i: (i, h)),
                  pl.BlockSpec((s, HEAD_PAD), lambda h, i: (0, h)),
                  pl.BlockSpec((s, V_HEAD), lambda h, i: (0, h)),
                  pl.BlockSpec((None, tq, 1), lambda h, i: (h, i, 0)),
                  pl.BlockSpec((tq, V_HEAD), lambda h, i: (i, n_pool_blocks + h))],
        out_specs=(pl.BlockSpec((tq, HEAD_PAD), lambda h, i: (i, h)),
                   pl.BlockSpec((s, HEAD_PAD), lambda h, i: (0, h)),
                   pl.BlockSpec((s, V_HEAD), lambda h, i: (0, h))),
        compiler_params=_params(("parallel", "arbitrary")),
    )(q, k, v, lse, dycat)


def _loss_head(x, gw, target, name):
    s, d = x.shape
    ts = _tile(s, 256)

    def body(x_ref, gw_ref, tgt_ref, loss_ref, dx_ref, dgw_ref):
        @pl.when(pl.program_id(0) == 0)
        def _():
            loss_ref[...] = jnp.zeros_like(loss_ref)
            dgw_ref[...] = jnp.zeros_like(dgw_ref)

        xv, gwv = x_ref[...], gw_ref[...]
        r = lax.rsqrt(jnp.mean(xv * xv, axis=-1, keepdims=True) + EPS)
        xn = xv * r
        err = xn * gwv - tgt_ref[...]
        loss_ref[...] += 0.5 * jnp.sum(jnp.mean(err * err, axis=-1, keepdims=True))
        dy = err / d
        dgw_ref[...] += jnp.sum(dy * xn, axis=0, keepdims=True)
        dxn = dy * gwv
        dx_ref[...] = r * (dxn - xn * jnp.mean(dxn * xn, axis=-1, keepdims=True))

    row = pl.BlockSpec((ts, d), lambda i: (i, 0))
    return pl.pallas_call(
        body, name=name, grid=(s // ts,),
        out_shape=(jax.ShapeDtypeStruct((8, LANE), F32), jax.ShapeDtypeStruct((s, d), F32),
                   jax.ShapeDtypeStruct((1, d), F32)),
        in_specs=[row, _vec_spec(d), row],
        out_specs=(pl.BlockSpec((8, LANE), lambda i: (0, 0)), row, _vec_spec(d)),
        compiler_params=_params(("arbitrary",)),
    )(x, gw, target)


def _ada_mod(c_all, ada_w, ada_b, name):
    nl, d, cols = ada_w.shape

    def body(c_ref, w_ref, b_ref, o_ref):
        cv = c_ref[...]
        act = (cv * jax.nn.sigmoid(cv)).astype(BF16)
        o_ref[...] = jnp.dot(act, w_ref[...].astype(BF16), preferred_element_type=F32) + b_ref[...]

    return pl.pallas_call(
        body, name=name, grid=(nl,), out_shape=jax.ShapeDtypeStruct((nl, N_DEV, cols), F32),
        in_specs=[pl.BlockSpec((N_DEV, d), lambda l: (0, 0)),
                  pl.BlockSpec((None, d, cols), lambda l: (l, 0, 0)),
                  pl.BlockSpec((None, 1, cols), lambda l: (l, 0, 0))],
        out_specs=pl.BlockSpec((None, N_DEV, cols), lambda l: (l, 0, 0)),
        compiler_params=_params(("parallel",)),
    )(c_all, ada_w, ada_b)


def _ada_grad(c_pad, dmod_pad, name):
    nl, kpad, cols = dmod_pad.shape
    d = c_pad.shape[1]

    def body(c_ref, dm_ref, o_ref):
        cv = c_ref[...]
        act = (cv * jax.nn.sigmoid(cv)).astype(BF16)
        o_ref[...] = lax.dot_general(act, dm_ref[...].astype(BF16), (((0,), (0,)), ((), ())),
                                     preferred_element_type=F32)

    return pl.pallas_call(
        body, name=name, grid=(nl,), out_shape=jax.ShapeDtypeStruct((nl, d, cols), F32),
        in_specs=[pl.BlockSpec((kpad, d), lambda l: (0, 0)),
                  pl.BlockSpec((None, kpad, cols), lambda l: (l, 0, 0))],
        out_specs=pl.BlockSpec((None, d, cols), lambda l: (l, 0, 0)),
        compiler_params=_params(("parallel",)),
    )(c_pad, dmod_pad)


def _adamw_math(w, g, m, v):
    nm = ADAM_B1 * m + (1.0 - ADAM_B1) * g
    nv = ADAM_B2 * v + (1.0 - ADAM_B2) * (g * g)
    m_hat = nm / (1.0 - ADAM_B1 ** ADAM_STEP)
    v_hat = nv / (1.0 - ADAM_B2 ** ADAM_STEP)
    return -ADAM_LR * (m_hat / (jnp.sqrt(v_hat) + ADAM_EPS) + ADAM_WD * w), nm, nv


def _adamw_rows(w3, gbuf, row_off, m3, v3, name):
    nl, r, d = w3.shape
    tr = _row_tile(math.gcd(r, row_off) if row_off else r, 176)
    first = row_off // tr

    def body(w_ref, g_ref, m_ref, v_ref, go_ref, d_ref, nm_ref, nv_ref):
        gv = g_ref[...]
        go_ref[...] = gv
        d_ref[...], nm_ref[...], nv_ref[...] = _adamw_math(w_ref[...], gv, m_ref[...], v_ref[...])

    blk = pl.BlockSpec((None, tr, d), lambda l, i: (l, i, 0))
    gblk = pl.BlockSpec((None, tr, d), lambda l, i: (l, first + i, 0))
    out = jax.ShapeDtypeStruct((nl, r, d), F32)
    return pl.pallas_call(
        body, name=name, grid=(nl, r // tr), out_shape=(out, out, out, out),
        in_specs=[blk, gblk, blk, blk], out_specs=(blk, blk, blk, blk),
        compiler_params=_params(("parallel", "parallel")),
    )(w3, gbuf, m3, v3)


def _adamw(w, g, m, v, name):
    rows, cols = w.shape
    tr = _row_tile(rows, 512)

    def body(w_ref, g_ref, m_ref, v_ref, d_ref, nm_ref, nv_ref):
        d_ref[...], nm_ref[...], nv_ref[...] = _adamw_math(w_ref[...], g_ref[...], m_ref[...], v_ref[...])

    blk = pl.BlockSpec((tr, cols), lambda i: (i, 0))
    out = jax.ShapeDtypeStruct((rows, cols), F32)
    return pl.pallas_call(
        body, name=name, grid=(rows // tr,), out_shape=(out, out, out),
        in_specs=[blk, blk, blk, blk], out_specs=(blk, blk, blk),
        compiler_params=_params(("parallel",)),
    )(w, g, m, v)


def _adamw_nd(w, g, m, v, name):
    shape = w.shape
    flat = (lambda t: t.reshape(1, -1)) if w.ndim == 1 else (lambda t: t.reshape(-1, shape[-1]))
    return tuple(t.reshape(shape) for t in _adamw(flat(w), flat(g), flat(m), flat(v), name))


def _pad_rows(t, rows):
    return jnp.pad(t, ((0, rows - t.shape[0]), (0, 0)))


def _pack_shard_layer(l, wts):
    def tr(name):
        return wts[name][l].astype(BF16).T

    parts = [tr("ffn1_w_gate"), tr("ffn1_w_up"), wts["ffn1_w_down"][l].astype(BF16),
             tr("ffn2_w_gate"), tr("ffn2_w_up"), wts["ffn2_w_down"][l].astype(BF16),
             wts["w_out"][l].astype(BF16),
             tr("w_kv_b").reshape(KV_SH_ROWS, D_MODEL),
             _pad_rows(tr("w_in"), 160),
             _pad_rows(tr("w_q_b").reshape(Q_SH_ROWS, D_MODEL), Q_PAD_ROWS)]
    return jnp.concatenate(parts, axis=0)


def _full_weights(lands):
    w = dict(zip(("g1", "u1", "d1", "g2", "u2", "d2", "out"), lands))
    small = lands[-1].reshape(N_DEV, SMALL_ROWS, D_MODEL)
    o_in, o_q = OFF_IN - OFF_KV, OFF_Q - OFF_KV
    w["kv"] = small[:, :KV_SH_ROWS].reshape(N_HEADS * HEAD_PAD, KV_LORA)
    w["in"] = _pad_rows(small[:, o_in:o_in + IN_SH].reshape(IN_COLS, D_MODEL), IN_PAD)
    wq = small[:, o_q:o_q + Q_SH_ROWS].reshape(N_HEADS, QK_HEAD, Q_LORA)
    w["q"] = jnp.pad(wq, ((0, 0), (0, HEAD_PAD - QK_HEAD), (0, 0))).reshape(N_HEADS * HEAD_PAD, Q_LORA)
    return w


def _grad_sources_b(gr):
    gq = gr["q"].reshape(N_HEADS, HEAD_PAD, Q_LORA)[:, :QK_HEAD].reshape(N_DEV, Q_SH_ROWS, D_MODEL)
    small = jnp.concatenate([
        gr["kv"].reshape(N_DEV, KV_SH_ROWS, D_MODEL),
        jnp.pad(gr["in"][:IN_COLS].reshape(N_DEV, IN_SH, D_MODEL), ((0, 0), (0, 160 - IN_SH), (0, 0))),
        jnp.pad(gq, ((0, 0), (0, Q_PAD_ROWS - Q_SH_ROWS), (0, 0)))], axis=1)
    return [gr["g2"], gr["u2"], gr["d2"], gr["out"], small.reshape(N_DEV * SMALL_ROWS, D_MODEL)]


def _small_layout(nl):
    names = [("dmod", nl * N_MOD), ("ffn1_norm", nl), ("mix_norm", nl), ("ffn2_norm", nl), ("q_a_norm", nl),
             ("kv_a_norm", nl), ("pool_scale", nl), ("final_norm", 1), ("loss", 1),
             ("pool_w", nl * 4 * POOL_GC * POOL_GC // D_MODEL)]
    off, table = 0, {}
    for name, n in names:
        table[name] = (off, n)
        off += -(-n // 8) * 8
    return table, off


def _to_rows(t, width=D_MODEL):
    n, w = t.shape
    return jnp.pad(t, ((0, -(-n // 8) * 8 - n), (0, width - w)))


def kernel(x, c, positions, ada_w, ada_b, ffn1_norm, ffn1_w_gate, ffn1_w_up, ffn1_w_down, mix_norm, w_in, pool_w, pool_scale, q_a_norm, w_q_b, kv_a_norm, w_kv_b, w_out, ffn2_norm, ffn2_w_gate, ffn2_w_up, ffn2_w_down, final_norm, loss_target, m_ada_w, m_ada_b, m_ffn1_norm, m_ffn1_w_gate, m_ffn1_w_up, m_ffn1_w_down, m_mix_norm, m_w_in, m_pool_w, m_pool_scale, m_q_a_norm, m_w_q_b, m_kv_a_norm, m_w_kv_b, m_w_out, m_ffn2_norm, m_ffn2_w_gate, m_ffn2_w_up, m_ffn2_w_down, m_final_norm, v_ada_w, v_ada_b, v_ffn1_norm, v_ffn1_w_gate, v_ffn1_w_up, v_ffn1_w_down, v_mix_norm, v_w_in, v_pool_w, v_pool_scale, v_q_a_norm, v_w_q_b, v_kv_a_norm, v_w_kv_b, v_w_out, v_ffn2_norm, v_ffn2_w_gate, v_ffn2_w_up, v_ffn2_w_down, v_final_norm):
    wts = dict(ada_w=ada_w, ada_b=ada_b, ffn1_norm=ffn1_norm, ffn1_w_gate=ffn1_w_gate, ffn1_w_up=ffn1_w_up,
               ffn1_w_down=ffn1_w_down, mix_norm=mix_norm, w_in=w_in, pool_w=pool_w, pool_scale=pool_scale,
               q_a_norm=q_a_norm, w_q_b=w_q_b, kv_a_norm=kv_a_norm, w_kv_b=w_kv_b, w_out=w_out,
               ffn2_norm=ffn2_norm, ffn2_w_gate=ffn2_w_gate, ffn2_w_up=ffn2_w_up, ffn2_w_down=ffn2_w_down,
               final_norm=final_norm)
    mom_m = dict(ada_w=m_ada_w, ada_b=m_ada_b, ffn1_norm=m_ffn1_norm, ffn1_w_gate=m_ffn1_w_gate,
                 ffn1_w_up=m_ffn1_w_up, ffn1_w_down=m_ffn1_w_down, mix_norm=m_mix_norm, w_in=m_w_in,
                 pool_w=m_pool_w, pool_scale=m_pool_scale, q_a_norm=m_q_a_norm, w_q_b=m_w_q_b,
                 kv_a_norm=m_kv_a_norm, w_kv_b=m_w_kv_b, w_out=m_w_out, ffn2_norm=m_ffn2_norm,
                 ffn2_w_gate=m_ffn2_w_gate, ffn2_w_up=m_ffn2_w_up, ffn2_w_down=m_ffn2_w_down,
                 final_norm=m_final_norm)
    mom_v = dict(ada_w=v_ada_w, ada_b=v_ada_b, ffn1_norm=v_ffn1_norm, ffn1_w_gate=v_ffn1_w_gate,
                 ffn1_w_up=v_ffn1_w_up, ffn1_w_down=v_ffn1_w_down, mix_norm=v_mix_norm, w_in=v_w_in,
                 pool_w=v_pool_w, pool_scale=v_pool_scale, q_a_norm=v_q_a_norm, w_q_b=v_w_q_b,
                 kv_a_norm=v_kv_a_norm, w_kv_b=v_w_kv_b, w_out=v_w_out, ffn2_norm=v_ffn2_norm,
                 ffn2_w_gate=v_ffn2_w_gate, ffn2_w_up=v_ffn2_w_up, ffn2_w_down=v_ffn2_w_down,
                 final_norm=v_final_norm)
    order = list(wts)
    nl = ada_w.shape[0]
    seq = x.shape[1]
    me = 4 * lax.axis_index("x") + 2 * lax.axis_index("y") + lax.axis_index("c")
    ada_cols = ada_w.shape[2]

    def after_token(t, token):
        return t + token[0:1, 0:1].astype(t.dtype)

    packs = [_pack_shard_layer(l, wts) for l in range(nl)]

    c_all = _all_gather(jnp.broadcast_to(c, (8, D_MODEL)), "gather_c")[::8]

    ada_b_mine = lax.dynamic_slice_in_dim(ada_b, me * ada_cols, ada_cols, axis=1).reshape(nl, 1, ada_cols)
    mod_part = _ada_mod(c_all, ada_w, ada_b_mine, "ada_mod")
    mod_all = _all_gather(mod_part.reshape(nl * N_DEV, ada_cols), "gather_mod")
    mod_all = mod_all.reshape(N_DEV, nl, N_DEV, ada_cols)
    mod = lax.dynamic_index_in_dim(mod_all, me, axis=2, keepdims=False)
    mod = mod.transpose(1, 0, 2).reshape(nl, N_MOD, 1, D_MODEL)

    flight_a = _gather_start(packs[0][:SPLIT_AB], ROWS_A, mod, "gather_start_0a")
    flight_b = _gather_start(packs[0][SPLIT_AB:], ROWS_B, flight_a[4], "gather_start_0b")
    last_start = flight_b[4]
    if nl > 1:
        in_flight = _gather_start(packs[1], ROWS_ALL, last_start, "gather_start_1")
        last_start = in_flight[4]

    cos, sin = _rope_tables(after_token(positions.reshape(seq, 1), last_start), "rope_tables")

    def vec(t):
        return t.reshape(1, -1)

    def landed(flight, rows_list, after, tag):
        send_sems, recv_sems, pk, lands, _ = flight
        pk, lands = _gather_wait(send_sems, recv_sems, pk, lands, after, f"gather_wait_{tag}")
        return _gather_finish(pk, rows_list, lands, "gather_finish")

    xs = x.reshape(seq, D_MODEL)
    saved = []
    for l in range(nl):
        norm1 = vec(ffn1_norm[l])
        if l == 0:
            lands = landed(flight_a, ROWS_A, cos, "0a")
        elif l + 1 < nl:
            in_flight = _gather_start(packs[l + 1], ROWS_ALL, lands[0], f"gather_start_{l + 1}")
            norm1 = after_token(norm1, in_flight[4])
        sv = {}

        def ffn_fwd(xin, norm, k0, wg, wu, wd, tag):
            h, a, b, t = _ffn_up(xin, norm, mod[l, k0], mod[l, k0 + 1], wg, wu, "ffn_up")
            y, xout = _mm(t, wd, "nn", "ffn_down", res=xin, gate=mod[l, k0 + 2], gate_factor=0.5)
            sv[tag] = dict(x=xin, h=h, a=a, b=b, t=t, y=y)
            return xout

        xs = ffn_fwd(xs, norm1, 0, lands[0], lands[1], lands[2], "f1")
        if l == 0:
            lands = lands + landed(flight_b, ROWS_B, xs, "0b")
        w = _full_weights(lands)
        sv["w"] = w

        h2, z = _norm_mm(xs, vec(mix_norm[l]), mod[l, 3], mod[l, 4], w["in"], "mix_in")
        y_pool, diff = _pool_fwd(z, pool_w[l], vec(pool_scale[l]), "pool_fwd")
        q, k, v, cqn, ckvn = _qkv_fwd(z, vec(q_a_norm[l]), vec(kv_a_norm[l]), w["q"], w["kv"], cos, sin, "qkv_fwd")
        o, lse = _attn_fwd(q, k, v, "attn_fwd")
        ycat = jnp.concatenate([y_pool, o.astype(BF16)], axis=1)
        y2, xmix = _mm(ycat, w["out"], "nn", "mix_out", res=xs, gate=mod[l, 5], gate_factor=1.0)
        sv["mix"] = dict(x=xs, h=h2, z=z, diff=diff, q=q, k=k, v=v, cqn=cqn, ckvn=ckvn, lse=lse, ycat=ycat, y=y2)
        xs = xmix

        xs = ffn_fwd(xs, vec(ffn2_norm[l]), 6, w["g2"], w["u2"], w["d2"], "f2")
        saved.append(sv)
        if l + 1 < nl:
            lands = landed(in_flight, ROWS_ALL, xs, l + 1)

    loss_part, dx, d_final = _loss_head(xs, vec(final_norm), loss_target.reshape(seq, D_MODEL), "loss_head")

    small = {name: [None] * nl for name in ("ffn1_norm", "mix_norm", "ffn2_norm", "q_a_norm", "kv_a_norm",
                                            "pool_scale", "pool_w", "dmod")}
    core = lax.axis_index("c").astype(jnp.int32).reshape(1)
    chip = 2 * lax.axis_index("x") + lax.axis_index("y")
    exchanges = []

    def leave(srcs, rows_list, after, tag):
        return _pair_start(srcs, rows_list, after, f"pair_start_{tag}"), rows_list, tag

    def forward_on(pending, after, layer, row_off):
        (send_sems, recv_sems, srcs, land, _), rows_list, tag = pending
        srcs, land = _split_wait(send_sems, recv_sems, 1, srcs, land, after, f"pair_wait_{tag}")
        sums = _pair_sum(srcs, rows_list, land, core, "pair_sum")
        flight = _chip_exchange_start(sums, chip, after, f"exchange_start_{tag}")
        exchanges.append((flight, layer, row_off, tag))
        return flight[4]

    pending = None
    head = _gate_bwd(dx, saved[nl - 1]["f2"]["y"], mod[nl - 1, 8], 0.5, "gate_bwd")
    for l in reversed(range(nl)):
        sv = saved[l]
        w = sv["w"]
        dmod = [None] * N_MOD
        gr = {}

        def ffn_bwd(dxin, head, s_, norm, k0, wg, wu, wd, tag, below, first_after=None, mid=None):
            dy, dmod[k0 + 2] = head
            da, db, gr["d" + tag], gr["g" + tag], gr["u" + tag] = _ffn_bwd_cols(
                dy, s_["h"], s_["a"], s_["b"], s_["t"], wd, "ffn_bwd_cols", after=first_after)
            dh = _mm_pair(da, wg, db, wu, "ffn_bwd_dh", after=None if mid is None else mid(da))
            outs = _rm_bwd(dh, s_["x"], dxin, vec(norm), mod[l, k0 + 1], "rm_bwd", below=below)
            dmod[k0], dmod[k0 + 1] = outs[1], outs[2]
            return outs[0], outs[3], outs[4:]

        s_ = sv["mix"]
        dx, small["ffn2_norm"][l], head = ffn_bwd(
            dx, head, sv["f2"], ffn2_norm[l], 6, w["g2"], w["u2"], w["d2"], "2", (s_["y"], mod[l, 5], 1.0),
            first_after=None if pending is None else pending[0][4])

        mix_after = None
        if pending is not None:
            mix_after = forward_on(pending, dx, l + 1, 0)
            pending = None
        dy, dmod[5] = head
        gr["out"] = _mm(s_["ycat"], dy, "tn", "mix_out_dw", out_dtype=BF16, tm=256, after=mix_after)
        dycat = _mm(dy, w["out"], "nt", "mix_out_dx")
        du, small["pool_w"][l], small["pool_scale"][l] = _pool_bwd(dycat, s_["diff"], pool_w[l], vec(pool_scale[l]), "pool_bwd")
        dq, dk, dv = _attn_bwd(s_["q"], s_["k"], s_["v"], s_["lse"], dycat, "attn_bwd")
        dz, dqb, dkvb, small["q_a_norm"][l], small["kv_a_norm"][l] = _qkv_bwd(
            dq, dk, dv, du, s_["z"], vec(q_a_norm[l]), vec(kv_a_norm[l]), w["q"], w["kv"], cos, sin, "qkv_bwd")
        gr["q"] = _mm(dqb, s_["cqn"], "tn", "q_b_dw", out_dtype=BF16, tm=256)
        gr["kv"] = _mm(dkvb, s_["ckvn"], "tn", "kv_b_dw", out_dtype=BF16, tm=256)
        gr["in"] = _mm(dz, s_["h"], "tn", "mix_in_dw", out_dtype=BF16, tm=256)
        dh2 = _mm(dz, w["in"], "nn", "mix_in_dx")
        outs = _rm_bwd(dh2, s_["x"], dx, vec(mix_norm[l]), mod[l, 4], "rm_bwd", below=(sv["f1"]["y"], mod[l, 2], 0.5))
        dx, dmod[3], dmod[4], small["mix_norm"][l] = outs[:4]
        head = outs[4:]

        first_after, mid = None, None
        if l == 0:
            pending_b = leave(_grad_sources_b(gr), ROWS_B, dx, "0b")
            first_after = pending_b[0][4]
            mid = lambda da: forward_on(pending_b, da, 0, SPLIT_AB)
        below = (saved[l - 1]["f2"]["y"], mod[l - 1, 8], 0.5) if l > 0 else None
        dx, small["ffn1_norm"][l], head = ffn_bwd(
            dx, head, sv["f1"], ffn1_norm[l], 0, w["g1"], w["u1"], w["d1"], "1", below, first_after, mid)

        small["dmod"][l] = jnp.concatenate(dmod, axis=0)
        if l > 0:
            pending = leave([gr["g1"], gr["u1"], gr["d1"]] + _grad_sources_b(gr), ROWS_ALL, dx, l)

    grad_x = dx.reshape(x.shape)
    pending_a = leave([gr["g1"], gr["u1"], gr["d1"]], ROWS_A, dx, "0a")

    layout, small_rows = _small_layout(nl)
    pieces = {
        "dmod": jnp.concatenate(small["dmod"], axis=0),
        "ffn1_norm": jnp.concatenate(small["ffn1_norm"], axis=0),
        "mix_norm": jnp.concatenate(small["mix_norm"], axis=0),
        "ffn2_norm": jnp.concatenate(small["ffn2_norm"], axis=0),
        "q_a_norm": jnp.concatenate(small["q_a_norm"], axis=0),
        "kv_a_norm": jnp.concatenate(small["kv_a_norm"], axis=0),
        "pool_scale": jnp.concatenate(small["pool_scale"], axis=0),
        "final_norm": d_final,
        "loss": jnp.broadcast_to(loss_part[0:1, 0:1], (1, D_MODEL)),
        "pool_w": jnp.stack(small["pool_w"]).reshape(-1, D_MODEL),
    }
    small_buf = jnp.concatenate([_to_rows(pieces[name]) for name in layout], axis=0)
    def landed_sums(gbuf, entries, after):
        for (send_sems, recv_sems, sums, recv, _), layer, row_off, tag in entries:
            _, recv = _split_wait(send_sems, recv_sems, N_CHIPS - 1, sums, recv, after, f"exchange_wait_{tag}")
            gbuf = _sum_slots_into(recv, gbuf, layer, row_off, "sum_grads")
        return gbuf

    gbuf = lax.empty((nl, ROWS_L, D_MODEL), F32)
    gbuf = landed_sums(gbuf, [e for e in exchanges if e[1] > 0], pending_a[0][4])
    small_all = _all_gather(small_buf, "gather_small", after=gbuf if nl > 1 else pending_a[0][4])
    small_all = small_all.reshape(N_DEV, small_rows, D_MODEL)
    token_0a = forward_on(pending_a, small_all, 0, 0)
    small_sum = _sum_slots(small_all, "sum_small", after=token_0a)

    def take(name, width=D_MODEL):
        off, n = layout[name]
        return small_sum[off:off + n, :width]

    grads = {}
    grads["ada_b"] = take("dmod").reshape(nl, N_MOD * D_MODEL)
    grads["ffn1_norm"], grads["mix_norm"], grads["ffn2_norm"] = take("ffn1_norm"), take("mix_norm"), take("ffn2_norm")
    grads["q_a_norm"], grads["kv_a_norm"] = take("q_a_norm", Q_LORA), take("kv_a_norm", KV_LORA)
    grads["pool_scale"] = take("pool_scale", POOL_WIDTH)
    grads["final_norm"] = take("final_norm").reshape(D_MODEL)
    grads["pool_w"] = take("pool_w").reshape(pool_w.shape)
    loss = take("loss")[0, 0]

    off, n = layout["dmod"]
    dmod_all = small_all[:, off:off + n].reshape(N_DEV, nl, N_MOD * D_MODEL)
    dmod_mine = lax.dynamic_slice_in_dim(dmod_all, me * ada_cols, ada_cols, axis=2)
    dmod_pad = jnp.pad(dmod_mine.transpose(1, 0, 2), ((0, 0), (0, LANE - N_DEV), (0, 0)))
    grads["ada_w"] = _ada_grad(jnp.pad(c_all, ((0, LANE - N_DEV), (0, 0))), dmod_pad, "ada_grad")

    updates = {name: _adamw_nd(wts[name], grads[name], mom_m[name], mom_v[name], "adamw") for name in grads}

    gbuf = landed_sums(gbuf, [e for e in exchanges if e[1] == 0], updates["ada_w"][0])

    def swap(t):
        return t.transpose(0, 2, 1)

    def same(t):
        return t

    for wname, off, view in (("ffn1_w_gate", OFF_G1, swap), ("ffn1_w_up", OFF_U1, swap), ("ffn1_w_down", OFF_D1, same),
                             ("ffn2_w_gate", OFF_G2, swap), ("ffn2_w_up", OFF_U2, swap), ("ffn2_w_down", OFF_D2, same),
                             ("w_out", OFF_OUT, same)):
        g, d_, nm, nv = _adamw_rows(view(wts[wname]), gbuf, off, view(mom_m[wname]), view(mom_v[wname]), "adamw_rows")
        grads[wname], updates[wname] = view(g), (view(d_), view(nm), view(nv))
    small_grads = {
        "w_kv_b": (gbuf[:, OFF_KV:OFF_KV + KV_SH_ROWS].reshape(nl, -1, KV_LORA).transpose(0, 2, 1), same),
        "w_in": (gbuf[:, OFF_IN:OFF_IN + IN_SH], swap),
        "w_q_b": (gbuf[:, OFF_Q:OFF_Q + Q_SH_ROWS].reshape(nl, -1, Q_LORA), swap),
    }
    for wname, (g, view) in small_grads.items():
        upd = _adamw_nd(view(wts[wname]), g, view(mom_m[wname]), view(mom_v[wname]), "adamw")
        grads[wname], updates[wname] = view(g), tuple(view(t) for t in upd)

    return (loss, grad_x, *[grads[n] for n in order], *[updates[n][0] for n in order],
            *[updates[n][1] for n in order], *[updates[n][2] for n in order])
```

```python
import math

import numpy as np
import jax
import jax.numpy as jnp
from jax import lax
from jax.experimental import pallas as pl
from jax.experimental.pallas import tpu as pltpu

F32 = jnp.float32
BF16 = jnp.bfloat16

N_DEV = 8
D_MODEL = 1024
D_FF = 2816
POOL_WIDTH = 512
POOL_WINDOWS = (2, 4, 8, 16)
POOL_GC = 128
N_HEADS = 4
QK_NOPE = 128
QK_ROPE = 64
V_HEAD = 128
QK_HEAD = QK_NOPE + QK_ROPE
HEAD_PAD = 256
Q_LORA = 384
KV_LORA = 256
IN_COLS = POOL_WIDTH + Q_LORA + KV_LORA + QK_ROPE
IN_PAD = 1280
ROPE_THETA = 10000.0
SOFTMAX_SCALE = 1.0 / math.sqrt(QK_HEAD)
EPS = 1e-6
N_MOD = 9

ADAM_LR = 0.001
ADAM_B1 = 0.9
ADAM_B2 = 0.999
ADAM_EPS = 1e-08
ADAM_WD = 0.01
ADAM_STEP = 10

LANE = 128
VMEM_LIMIT = 56 * 1024 * 1024

FF_SH = D_FF // N_DEV
OFF_G1, OFF_U1, OFF_D1 = 0, FF_SH, 2 * FF_SH
OFF_G2, OFF_U2, OFF_D2 = 3 * FF_SH, 4 * FF_SH, 5 * FF_SH
OFF_OUT = 6 * FF_SH
OFF_KV = OFF_OUT + 128
OFF_IN = OFF_KV + 32
OFF_Q = OFF_IN + 160
Q_PAD_ROWS = 64
ROWS_L = OFF_Q + Q_PAD_ROWS
IN_SH = IN_COLS // N_DEV
Q_SH_ROWS = (N_HEADS * QK_HEAD // N_DEV) * Q_LORA // D_MODEL
KV_SH_ROWS = (N_HEADS * (QK_NOPE + V_HEAD) // N_DEV) * KV_LORA // D_MODEL


def _tile(dim, target):
    if dim <= target:
        return dim
    best = None
    for t in range(LANE, target + 1, LANE):
        if dim % t == 0:
            best = t
    assert best is not None, (dim, target)
    return best


def _params(sem):
    return pltpu.CompilerParams(dimension_semantics=sem, vmem_limit_bytes=VMEM_LIMIT)


def _mesh_pos():
    return lax.axis_index("x"), lax.axis_index("y"), lax.axis_index("c")


def _all_gather(x, name):
    m, n = x.shape

    def body(x_ref, out_ref, send_sems, recv_sems, local_sem):
        px, py, pc = _mesh_pos()
        me, sibling = (px, py, pc), (px, py, 1 - pc)
        chips = [(1 - px, py), (px, 1 - py), (1 - px, 1 - py)]

        def rows(bx, by, bc):
            return out_ref.at[pl.ds((4 * bx + 2 * by + bc) * m, m), :]

        def copy(k, block, to, src=None):
            return pltpu.make_async_remote_copy(
                src_ref=rows(*block) if src is None else src, dst_ref=rows(*block),
                send_sem=send_sems.at[k], recv_sem=recv_sems.at[k],
                device_id=to, device_id_type=pl.DeviceIdType.MESH)

        mine = pltpu.make_async_copy(x_ref, rows(*me), local_sem)
        mine.start()
        first = [copy(0, me, sibling, src=x_ref)]
        first += [copy(1 + j, me, (*chip, pc), src=x_ref) for j, chip in enumerate(chips)]
        for cp in first:
            cp.start()
        passed = [copy(4 + j, (*chip, pc), sibling) for j, chip in enumerate(chips)]
        for j, chip in enumerate(chips):
            copy(1 + j, (*chip, pc), me).wait_recv()
            passed[j].start()
        copy(0, sibling, me).wait_recv()
        for j, chip in enumerate(chips):
            copy(4 + j, (*chip, 1 - pc), me).wait_recv()
        for cp in first + passed:
            cp.wait_send()
        mine.wait()

    hbm = pl.BlockSpec(memory_space=pltpu.HBM)
    return pl.pallas_call(
        body, name=name,
        out_shape=jax.ShapeDtypeStruct((N_DEV * m, n), x.dtype),
        in_specs=[hbm], out_specs=hbm,
        scratch_shapes=[pltpu.SemaphoreType.DMA((7,)), pltpu.SemaphoreType.DMA((7,)),
                        pltpu.SemaphoreType.DMA],
    )(x)


SMALL_ROWS = ROWS_L - OFF_KV
ROWS_A = [FF_SH] * 3
ROWS_B = [FF_SH] * 3 + [128, SMALL_ROWS]
ROWS_ALL = ROWS_A + ROWS_B
SPLIT_AB = sum(ROWS_A)
HBM_SPEC = pl.BlockSpec(memory_space=pltpu.HBM)
SEM_SPEC = pl.BlockSpec(memory_space=pltpu.SEMAPHORE)
ANY_SPEC = pl.BlockSpec(memory_space=pl.ANY)
EFFECT = pltpu.SideEffectType.DATAFLOW_SIDE_EFFECTING


def _hbm(t):
    return pltpu.with_memory_space_constraint(t, pltpu.HBM)


def _whole_wait(ref, send_sem, recv_sem, peer):
    return pltpu.make_async_remote_copy(src_ref=ref, dst_ref=ref, send_sem=send_sem, recv_sem=recv_sem,
                                        device_id=peer, device_id_type=pl.DeviceIdType.MESH)


def _offsets(rows_list):
    return [sum(rows_list[:i]) for i in range(len(rows_list))]


def _gather_start(packed, rows_list, after, name):
    n = len(rows_list)
    offs = _offsets(rows_list)
    lands = [_hbm(lax.empty((N_DEV * rows, D_MODEL), BF16)) for rows in rows_list]

    def body(packed_ref, *refs):
        land = refs[:n]
        send_sems, recv_sems = refs[n + 1], refs[n + 2]
        token = refs[-1]
        px, py, pc = _mesh_pos()
        me = 4 * px + 2 * py + pc
        peers = [(px, py, 1 - pc), (1 - px, py, pc), (px, 1 - py, pc), (1 - px, 1 - py, pc)]
        for k, peer in enumerate(peers):
            for off, rows, land_ref in zip(offs, rows_list, land):
                pltpu.make_async_remote_copy(
                    src_ref=packed_ref.at[pl.ds(off, rows), :], dst_ref=land_ref.at[pl.ds(me * rows, rows), :],
                    send_sem=send_sems.at[k], recv_sem=recv_sems.at[k],
                    device_id=peer, device_id_type=pl.DeviceIdType.MESH).start()
        token[...] = jnp.zeros_like(token)

    outs = pl.pallas_call(
        body, name=name,
        out_shape=(pltpu.SemaphoreType.DMA((4,)), pltpu.SemaphoreType.DMA((4,)), pltpu.HBM(packed.shape, BF16),
                   *[pltpu.HBM(t.shape, BF16) for t in lands], jax.ShapeDtypeStruct((8, LANE), F32)),
        in_specs=(HBM_SPEC,) * (1 + n) + (ANY_SPEC,),
        out_specs=(SEM_SPEC, SEM_SPEC) + (HBM_SPEC,) * (1 + n) + (pl.BlockSpec(memory_space=pltpu.VMEM),),
        input_output_aliases={i: 2 + i for i in range(1 + n)},
        compiler_params=pltpu.CompilerParams(has_side_effects=EFFECT),
    )(_hbm(packed), *lands, after)
    return outs[0], outs[1], outs[2], list(outs[3:3 + n]), outs[-1]


def _gather_wait(send_sems, recv_sems, packed, lands, after, name):
    n = len(lands)

    def body(packed_ref, *refs):
        s_sems, r_sems = refs[n], refs[n + 1]
        me = _mesh_pos()
        for k in range(4):
            cp = _whole_wait(packed_ref, s_sems.at[k], r_sems.at[k], me)
            cp.wait_send()
            cp.wait_recv()

    outs = pl.pallas_call(
        body, name=name,
        out_shape=(pltpu.HBM(packed.shape, BF16), *[pltpu.HBM(t.shape, BF16) for t in lands]),
        in_specs=(HBM_SPEC,) * (1 + n) + (SEM_SPEC, SEM_SPEC, ANY_SPEC),
        out_specs=(HBM_SPEC,) * (1 + n),
        input_output_aliases={i: i for i in range(1 + n)},
        compiler_params=pltpu.CompilerParams(has_side_effects=EFFECT),
    )(packed, *lands, send_sems, recv_sems, after)
    return outs[0], list(outs[1:])


def _gather_finish(packed, rows_list, lands, name):
    n = len(rows_list)
    offs = _offsets(rows_list)

    def body(packed_ref, *refs):
        land = refs[n:2 * n]
        send_sems, recv_sems, stage, stage_sem = refs[2 * n:]
        px, py, pc = _mesh_pos()
        me = 4 * px + 2 * py + pc
        sibling = (px, py, 1 - pc)
        load = pltpu.make_async_copy(packed_ref, stage, stage_sem)
        load.start()
        load.wait()
        for off, rows, land_ref in zip(offs, rows_list, land):
            pltpu.make_async_copy(stage.at[pl.ds(off, rows), :], land_ref.at[pl.ds(me * rows, rows), :],
                                  stage_sem).start()
        for j, (cx, cy) in enumerate([(1 - px, py), (px, 1 - py), (1 - px, 1 - py)]):
            block = 4 * cx + 2 * cy + pc
            for rows, land_ref in zip(rows_list, land):
                blk = land_ref.at[pl.ds(block * rows, rows), :]
                pltpu.make_async_remote_copy(src_ref=blk, dst_ref=blk, send_sem=send_sems.at[j],
                                             recv_sem=recv_sems.at[j], device_id=sibling,
                                             device_id_type=pl.DeviceIdType.MESH).start()
        for j in range(3):
            cp = _whole_wait(packed_ref, send_sems.at[j], recv_sems.at[j], sibling)
            cp.wait_recv()
            cp.wait_send()
        pltpu.make_async_copy(stage, packed_ref, stage_sem).wait()

    outs = pl.pallas_call(
        body, name=name,
        out_shape=tuple(jax.ShapeDtypeStruct(t.shape, BF16) for t in lands),
        in_specs=(HBM_SPEC,) * (1 + n), out_specs=(HBM_SPEC,) * n,
        input_output_aliases={1 + i: i for i in range(n)},
        scratch_shapes=[pltpu.SemaphoreType.DMA((3,)), pltpu.SemaphoreType.DMA((3,)),
                        pltpu.VMEM(packed.shape, BF16), pltpu.SemaphoreType.DMA],
    )(packed, *lands)
    return list(outs)


N_CHIPS = 4


def _pair_start(srcs, rows_list, after, name):
    n = len(rows_list)
    offs = _offsets(rows_list)
    land = lax.empty((N_CHIPS, sum(rows_list), D_MODEL), BF16)

    def body(*refs):
        src, land_ref = refs[:n], refs[n]
        send_sems, recv_sems = refs[n + 2], refs[n + 3]
        token = refs[-1]
        px, py, pc = _mesh_pos()
        for k in range(N_CHIPS):
            block = 2 * k + (1 - pc)
            for off, rows, src_ref in zip(offs, rows_list, src):
                pltpu.make_async_remote_copy(
                    src_ref=src_ref.at[pl.ds(block * rows, rows), :], dst_ref=land_ref.at[k, pl.ds(off, rows), :],
                    send_sem=send_sems.at[0], recv_sem=recv_sems.at[0],
                    device_id=(px, py, 1 - pc), device_id_type=pl.DeviceIdType.MESH).start()
        token[...] = jnp.zeros_like(token)

    outs = pl.pallas_call(
        body, name=name,
        out_shape=(pltpu.SemaphoreType.DMA((1,)), pltpu.SemaphoreType.DMA((1,)),
                   *[pltpu.HBM(t.shape, BF16) for t in srcs], pltpu.HBM(land.shape, BF16),
                   jax.ShapeDtypeStruct((8, LANE), F32)),
        in_specs=(HBM_SPEC,) * (n + 1) + (ANY_SPEC,),
        out_specs=(SEM_SPEC, SEM_SPEC) + (HBM_SPEC,) * (n + 1) + (pl.BlockSpec(memory_space=pltpu.VMEM),),
        input_output_aliases={i: 2 + i for i in range(n + 1)},
        compiler_params=pltpu.CompilerParams(has_side_effects=EFFECT),
    )(*[_hbm(t) for t in srcs], _hbm(land), after)
    return outs[0], outs[1], list(outs[2:2 + n]), outs[2 + n], outs[-1]


def _split_wait(send_sems, recv_sems, n_sems, srcs, land, after, name):
    n = len(srcs)

    def body(*refs):
        land_ref = refs[n]
        s_sems, r_sems = refs[n + 1], refs[n + 2]
        me = _mesh_pos()
        for k in range(n_sems):
            cp = _whole_wait(land_ref.at[0] if n_sems > 1 else land_ref, s_sems.at[k], r_sems.at[k], me)
            cp.wait_send()
            cp.wait_recv()

    outs = pl.pallas_call(
        body, name=name,
        out_shape=(*[pltpu.HBM(t.shape, t.dtype) for t in srcs], pltpu.HBM(land.shape, land.dtype)),
        in_specs=(HBM_SPEC,) * (n + 1) + (SEM_SPEC, SEM_SPEC, ANY_SPEC),
        out_specs=(HBM_SPEC,) * (n + 1),
        input_output_aliases={i: i for i in range(n + 1)},
        compiler_params=pltpu.CompilerParams(has_side_effects=EFFECT),
    )(*srcs, land, send_sems, recv_sems, after)
    return list(outs[:n]), outs[n]


def _spread_start(x, me_id, after, name):
    land = lax.dynamic_update_slice_in_dim(lax.empty((N_DEV,) + x.shape, x.dtype), x[None], me_id, axis=0)

    def body(x_ref, land_ref, after_ref, send_sems, recv_sems, x_thru, land_thru, token):
        px, py, pc = _mesh_pos()
        me = 4 * px + 2 * py + pc
        for k in range(1, N_DEV):
            qx = 1 - px if k & 4 else px
            qy = 1 - py if k & 2 else py
            qc = 1 - pc if k & 1 else pc
            pltpu.make_async_remote_copy(
                src_ref=x_ref, dst_ref=land_ref.at[me], send_sem=send_sems.at[k - 1], recv_sem=recv_sems.at[k - 1],
                device_id=(qx, qy, qc), device_id_type=pl.DeviceIdType.MESH).start()
        token[...] = jnp.zeros_like(token)

    outs = pl.pallas_call(
        body, name=name,
        out_shape=(pltpu.SemaphoreType.DMA((N_DEV - 1,)), pltpu.SemaphoreType.DMA((N_DEV - 1,)),
                   pltpu.HBM(x.shape, x.dtype), pltpu.HBM(land.shape, land.dtype), jax.ShapeDtypeStruct((8, LANE), F32)),
        in_specs=(HBM_SPEC, HBM_SPEC, ANY_SPEC),
        out_specs=(SEM_SPEC, SEM_SPEC, HBM_SPEC, HBM_SPEC, pl.BlockSpec(memory_space=pltpu.VMEM)),
        input_output_aliases={0: 2, 1: 3},
        compiler_params=pltpu.CompilerParams(has_side_effects=EFFECT),
    )(_hbm(x), _hbm(land), after)
    return outs[0], outs[1], [outs[2]], outs[3], outs[4]


def _pair_sum(srcs, rows_list, land, core, name):
    n = len(rows_list)
    offs = _offsets(rows_list)
    total = sum(rows_list)

    def body(core_ref, *refs):
        src, land_ref, out_ref = refs[:n], refs[n], refs[n + 1]
        for off, rows, src_ref in zip(offs, rows_list, src):
            out_ref[pl.ds(off, rows), :] = (src_ref[...].astype(F32)
                                            + land_ref[pl.ds(off, rows), :].astype(F32)).astype(BF16)

    slot = pl.BlockSpec((None, total, D_MODEL), lambda k, c: (k, 0, 0))
    grid_spec = pltpu.PrefetchScalarGridSpec(
        num_scalar_prefetch=1, grid=(N_CHIPS,),
        in_specs=[pl.BlockSpec((rows, D_MODEL), lambda k, c: (2 * k + c[0], 0)) for rows in rows_list] + [slot],
        out_specs=slot)
    return pl.pallas_call(
        body, name=name, grid_spec=grid_spec,
        out_shape=jax.ShapeDtypeStruct((N_CHIPS, total, D_MODEL), BF16),
        compiler_params=_params(("parallel",)),
    )(core, *srcs, land)


def _chip_exchange_start(sums, chip, after, name):
    own = lax.dynamic_index_in_dim(sums, chip, axis=0, keepdims=True)
    recv = lax.dynamic_update_slice_in_dim(lax.empty(sums.shape, BF16), own, chip, axis=0)

    def body(sums_ref, recv_ref, after_ref, send_sems, recv_sems, sums_thru, recv_thru, token):
        px, py, pc = _mesh_pos()
        for k in range(1, N_CHIPS):
            qx = 1 - px if k & 2 else px
            qy = 1 - py if k & 1 else py
            pltpu.make_async_remote_copy(
                src_ref=sums_ref.at[2 * qx + qy], dst_ref=recv_ref.at[2 * px + py],
                send_sem=send_sems.at[k - 1], recv_sem=recv_sems.at[k - 1],
                device_id=(qx, qy, pc), device_id_type=pl.DeviceIdType.MESH).start()
        token[...] = jnp.zeros_like(token)

    outs = pl.pallas_call(
        body, name=name,
        out_shape=(pltpu.SemaphoreType.DMA((N_CHIPS - 1,)), pltpu.SemaphoreType.DMA((N_CHIPS - 1,)),
                   pltpu.HBM(sums.shape, BF16), pltpu.HBM(recv.shape, BF16), jax.ShapeDtypeStruct((8, LANE), F32)),
        in_specs=(HBM_SPEC, HBM_SPEC, ANY_SPEC),
        out_specs=(SEM_SPEC, SEM_SPEC, HBM_SPEC, HBM_SPEC, pl.BlockSpec(memory_space=pltpu.VMEM)),
        input_output_aliases={0: 2, 1: 3},
        compiler_params=pltpu.CompilerParams(has_side_effects=EFFECT),
    )(_hbm(sums), _hbm(recv), after)
    return outs[0], outs[1], [outs[2]], outs[3], outs[4]


def _sum_slots_into(recv, buf, layer, row_off, name):
    slots, r, n = recv.shape
    tr = _row_tile(math.gcd(r, row_off) if row_off else r, 512)
    first = row_off // tr

    def body(in_ref, buf_ref, out_ref):
        acc = in_ref[0].astype(F32)
        for j in range(1, slots):
            acc = acc + in_ref[j].astype(F32)
        out_ref[...] = acc

    return pl.pallas_call(
        body, name=name, grid=(r // tr,), out_shape=jax.ShapeDtypeStruct(buf.shape, F32),
        in_specs=[pl.BlockSpec((slots, tr, n), lambda i: (0, i, 0)), ANY_SPEC],
        out_specs=pl.BlockSpec((None, tr, n), lambda i: (layer, first + i, 0)),
        input_output_aliases={1: 0},
        compiler_params=_params(("parallel",)),
    )(recv, buf)


def _sum_slots(recv, name, after=None):
    _, r, n = recv.shape
    tr = _row_tile(r, 512)

    def body(in_ref, *refs):
        acc = in_ref[0].astype(F32)
        for j in range(1, N_DEV):
            acc = acc + in_ref[j].astype(F32)
        refs[-1][...] = acc

    grid = (r // tr,)
    in_specs, out_spec = [pl.BlockSpec((N_DEV, tr, n), lambda i: (0, i, 0))], pl.BlockSpec((tr, n), lambda i: (i, 0))
    args = [recv]
    if after is not None:
        in_specs.append(ANY_SPEC)
        args.append(after)
    return pl.pallas_call(
        body, name=name, grid=grid,
        out_shape=jax.ShapeDtypeStruct((r, n), F32),
        in_specs=in_specs, out_specs=out_spec,
        compiler_params=_params(("parallel",)),
    )(*args)


def _row_tile(rows, target):
    if rows <= target:
        return rows
    best = None
    for t in range(16, target + 1, 16):
        if rows % t == 0:
            best = t
    assert best is not None, rows
    return best


_DIMS = {"nn": ((1,), (0,)), "nt": ((1,), (1,)), "tn": ((0,), (0,))}


def _mm(a, b, mode, name, out_dtype=F32, res=None, gate=None, gate_factor=1.0, tm=512, tn=1408, after=None):
    assert (res is None) == (gate is None)
    if mode == "tn":
        kdim, m = a.shape
    else:
        m, kdim = a.shape
    n = b.shape[0] if mode == "nt" else b.shape[1]
    tm, tn = _tile(m, tm), _tile(n, tn)
    a_spec = (pl.BlockSpec((kdim, tm), lambda i, j: (0, i)) if mode == "tn"
              else pl.BlockSpec((tm, kdim), lambda i, j: (i, 0)))
    b_spec = (pl.BlockSpec((tn, kdim), lambda i, j: (j, 0)) if mode == "nt"
              else pl.BlockSpec((kdim, tn), lambda i, j: (0, j)))
    o_spec = pl.BlockSpec((tm, tn), lambda i, j: (i, j))
    dims = (_DIMS[mode], ((), ()))
    has_res = res is not None

    def body(a_ref, b_ref, *refs):
        y = lax.dot_general(a_ref[...].astype(BF16), b_ref[...].astype(BF16), dims,
                            preferred_element_type=F32)
        if has_res:
            res_ref, gate_ref = refs[0], refs[1]
            y_ref, o_ref = refs[-2], refs[-1]
            y_ref[...] = y.astype(BF16)
            o_ref[...] = res_ref[...] + (gate_factor * gate_ref[...]) * y
        else:
            refs[-1][...] = y.astype(out_dtype)

    in_specs, args = [a_spec, b_spec], [a, b]
    if has_res:
        in_specs += [o_spec, pl.BlockSpec((1, tn), lambda i, j: (0, j))]
        args += [res, gate]
        out_shape = (jax.ShapeDtypeStruct((m, n), BF16), jax.ShapeDtypeStruct((m, n), F32))
        out_specs = (o_spec, o_spec)
    else:
        out_shape, out_specs = jax.ShapeDtypeStruct((m, n), out_dtype), o_spec
    if after is not None:
        in_specs.append(ANY_SPEC)
        args.append(after)
    return pl.pallas_call(
        body, name=name, grid=(m // tm, n // tn), out_shape=out_shape,
        in_specs=in_specs, out_specs=out_specs,
        compiler_params=_params(("parallel", "parallel")),
    )(*args)


def _vec_spec(width):
    return pl.BlockSpec((1, width), lambda i: (0, 0))


def _rm_bwd(dh, x, dres, gw, scale, name, below=None):
    s, d = x.shape
    ts = _tile(s, 256)
    factor = None if below is None else below[2]

    def body(dh_ref, x_ref, dres_ref, gw_ref, sc_ref, *refs):
        dx_ref, dsh_ref, dsc_ref, dgw_ref = refs[-6:-2] if below is not None else refs[-4:]

        @pl.when(pl.program_id(0) == 0)
        def _():
            dsh_ref[...] = jnp.zeros_like(dsh_ref)
            dsc_ref[...] = jnp.zeros_like(dsc_ref)
            dgw_ref[...] = jnp.zeros_like(dgw_ref)
            if below is not None:
                refs[-1][...] = jnp.zeros_like(refs[-1])

        xv, dhv, gwv = x_ref[...], dh_ref[...], gw_ref[...]
        r = lax.rsqrt(jnp.mean(xv * xv, axis=-1, keepdims=True) + EPS)
        xn = xv * r
        y = xn * gwv
        dsh_ref[...] += jnp.sum(dhv, axis=0, keepdims=True)
        dsc_ref[...] += jnp.sum(dhv * y, axis=0, keepdims=True)
        dy = dhv * (1 + sc_ref[...])
        dgw_ref[...] += jnp.sum(dy * xn, axis=0, keepdims=True)
        dxn = dy * gwv
        dx = dres_ref[...] + r * (dxn - xn * jnp.mean(dxn * xn, axis=-1, keepdims=True))
        dx_ref[...] = dx
        if below is not None:
            yb_ref, gb_ref, dyb_ref, dgb_ref = refs[0], refs[1], refs[-2], refs[-1]
            dyb_ref[...] = ((factor * gb_ref[...]) * dx).astype(BF16)
            dgb_ref[...] += jnp.sum((factor * dx) * yb_ref[...].astype(F32), axis=0, keepdims=True)

    row = pl.BlockSpec((ts, d), lambda i: (i, 0))
    vec = jax.ShapeDtypeStruct((1, d), F32)
    in_specs, args = [row, row, row, _vec_spec(d), _vec_spec(d)], [dh, x, dres, gw, scale]
    out_shape = [jax.ShapeDtypeStruct((s, d), F32), vec, vec, vec]
    out_specs = [row, _vec_spec(d), _vec_spec(d), _vec_spec(d)]
    if below is not None:
        in_specs += [row, _vec_spec(d)]
        args += [below[0], below[1]]
        out_shape += [jax.ShapeDtypeStruct((s, d), BF16), vec]
        out_specs += [row, _vec_spec(d)]
    return pl.pallas_call(
        body, name=name, grid=(s // ts,), out_shape=tuple(out_shape),
        in_specs=in_specs, out_specs=tuple(out_specs),
        compiler_params=_params(("arbitrary",)),
    )(*args)


def _gate_bwd(dx, y, gate, factor, name):
    s, d = dx.shape
    ts = _tile(s, 256)

    def body(dx_ref, y_ref, g_ref, dy_ref, dg_ref):
        @pl.when(pl.program_id(0) == 0)
        def _():
            dg_ref[...] = jnp.zeros_like(dg_ref)

        dxv = dx_ref[...]
        dy_ref[...] = ((factor * g_ref[...]) * dxv).astype(BF16)
        dg_ref[...] += jnp.sum((factor * dxv) * y_ref[...].astype(F32), axis=0, keepdims=True)

    row = pl.BlockSpec((ts, d), lambda i: (i, 0))
    return pl.pallas_call(
        body, name=name, grid=(s // ts,),
        out_shape=(jax.ShapeDtypeStruct((s, d), BF16), jax.ShapeDtypeStruct((1, d), F32)),
        in_specs=[row, row, _vec_spec(d)], out_specs=(row, _vec_spec(d)),
        compiler_params=_params(("arbitrary",)),
    )(dx, y, gate)


def _norm_mm(x, gw, shift, scale, w, name, tm=1024):
    s, d = x.shape
    n = w.shape[0]
    tm = _tile(s, tm)

    def body(x_ref, gw_ref, sh_ref, sc_ref, w_ref, h_ref, z_ref):
        xv = x_ref[...]
        r = lax.rsqrt(jnp.mean(xv * xv, axis=-1, keepdims=True) + EPS)
        hb = (((xv * r) * gw_ref[...]) * (1 + sc_ref[...]) + sh_ref[...]).astype(BF16)
        h_ref[...] = hb
        z_ref[...] = lax.dot_general(hb, w_ref[...], (((1,), (1,)), ((), ())), preferred_element_type=F32)

    row = pl.BlockSpec((tm, d), lambda i: (i, 0))
    return pl.pallas_call(
        body, name=name, grid=(s // tm,),
        out_shape=(jax.ShapeDtypeStruct((s, d), BF16), jax.ShapeDtypeStruct((s, n), F32)),
        in_specs=[row, _vec_spec(d), _vec_spec(d), _vec_spec(d), pl.BlockSpec((n, d), lambda i: (0, 0))],
        out_specs=(row, pl.BlockSpec((tm, n), lambda i: (i, 0))),
        compiler_params=_params(("parallel",)),
    )(x, gw, shift, scale, w)


FFN_TM, FFN_TF = 2048, 256


def _ffn_up(x, gw, shift, scale, wg, wu, name):
    s, d = x.shape
    f = wg.shape[0]
    tm, tf = _tile(s, FFN_TM), _tile(f, FFN_TF)
    nt = (((1,), (1,)), ((), ()))

    def body(x_ref, gw_ref, sh_ref, sc_ref, wg_ref, wu_ref, h_ref, a_ref, b_ref, t_ref):
        @pl.when(pl.program_id(1) == 0)
        def _():
            xv = x_ref[...]
            r = lax.rsqrt(jnp.mean(xv * xv, axis=-1, keepdims=True) + EPS)
            h_ref[...] = (((xv * r) * gw_ref[...]) * (1 + sc_ref[...]) + sh_ref[...]).astype(BF16)

        hb = h_ref[...]
        av = lax.dot_general(hb, wg_ref[...], nt, preferred_element_type=F32)
        bv = lax.dot_general(hb, wu_ref[...], nt, preferred_element_type=F32)
        a_ref[...] = av.astype(BF16)
        b_ref[...] = bv.astype(BF16)
        t_ref[...] = ((av * jax.nn.sigmoid(av)) * bv).astype(BF16)

    row = pl.BlockSpec((tm, d), lambda i, j: (i, 0))
    vec = pl.BlockSpec((1, d), lambda i, j: (0, 0))
    wblk = pl.BlockSpec((tf, d), lambda i, j: (j, 0))
    blk = pl.BlockSpec((tm, tf), lambda i, j: (i, j))
    wide = jax.ShapeDtypeStruct((s, f), BF16)
    return pl.pallas_call(
        body, name=name, grid=(s // tm, f // tf),
        out_shape=(jax.ShapeDtypeStruct((s, d), BF16), wide, wide, wide),
        in_specs=[row, vec, vec, vec, wblk, wblk], out_specs=(row, blk, blk, blk),
        compiler_params=_params(("parallel", "arbitrary")),
    )(x, gw, shift, scale, wg, wu)


def _ffn_bwd_cols(dy, h, a, b, t, wd, name, after=None):
    s, d = dy.shape
    f = wd.shape[0]
    tf = _tile(f, FFN_TF)
    nt = (((1,), (1,)), ((), ()))
    tn = (((0,), (0,)), ((), ()))

    def body(dy_ref, h_ref, a_ref, b_ref, t_ref, wd_ref, *refs):
        da_ref, db_ref, gd_ref, gg_ref, gu_ref = refs[-5:]
        dyb, hb = dy_ref[...], h_ref[...]
        dtv = lax.dot_general(dyb, wd_ref[...], nt, preferred_element_type=F32)
        av, bv = a_ref[...].astype(F32), b_ref[...].astype(F32)
        sg = jax.nn.sigmoid(av)
        dbv = (dtv * (av * sg)).astype(BF16)
        dav = ((dtv * bv) * (sg * (1 + av * (1 - sg)))).astype(BF16)
        da_ref[...] = dav
        db_ref[...] = dbv
        gd_ref[...] = lax.dot_general(t_ref[...], dyb, tn, preferred_element_type=F32).astype(BF16)
        gg_ref[...] = lax.dot_general(dav, hb, tn, preferred_element_type=F32).astype(BF16)
        gu_ref[...] = lax.dot_general(dbv, hb, tn, preferred_element_type=F32).astype(BF16)

    whole = pl.BlockSpec((s, d), lambda j: (0, 0))
    col = pl.BlockSpec((s, tf), lambda j: (0, j))
    wblk = pl.BlockSpec((tf, d), lambda j: (j, 0))
    wide, wgrad = jax.ShapeDtypeStruct((s, f), BF16), jax.ShapeDtypeStruct((f, d), BF16)
    in_specs, args = [whole, whole, col, col, col, wblk], [dy, h, a, b, t, wd]
    if after is not None:
        in_specs.append(ANY_SPEC)
        args.append(after)
    return pl.pallas_call(
        body, name=name, grid=(f // tf,), out_shape=(wide, wide, wgrad, wgrad, wgrad),
        in_specs=in_specs, out_specs=(col, col, wblk, wblk, wblk),
        compiler_params=_params(("parallel",)),
    )(*args)


def _mm_pair(a1, b1, a2, b2, name, tm=1024, tn=512, after=None):
    m, kdim = a1.shape
    n = b1.shape[1]
    tm, tn = _tile(m, tm), _tile(n, tn)

    def body(a1_ref, b1_ref, a2_ref, b2_ref, *refs):
        refs[-1][...] = (jnp.dot(a1_ref[...], b1_ref[...], preferred_element_type=F32)
                         + jnp.dot(a2_ref[...], b2_ref[...], preferred_element_type=F32))

    a_spec = pl.BlockSpec((tm, kdim), lambda i, j: (i, 0))
    b_spec = pl.BlockSpec((kdim, tn), lambda i, j: (0, j))
    in_specs, args = [a_spec, b_spec, a_spec, b_spec], [a1, b1, a2, b2]
    if after is not None:
        in_specs.append(ANY_SPEC)
        args.append(after)
    return pl.pallas_call(
        body, name=name, grid=(m // tm, n // tn), out_shape=jax.ShapeDtypeStruct((m, n), F32),
        in_specs=in_specs, out_specs=pl.BlockSpec((tm, tn), lambda i, j: (i, j)),
        compiler_params=_params(("parallel", "parallel")),
    )(*args)


def _pool_counts(s):
    return (lax.broadcasted_iota(jnp.int32, (s, POOL_GC), 0))


def _pool_fwd(z, pool_w, pool_scale, name):
    s = z.shape[0]

    def body(u_ref, w_ref, sc_ref, y_ref, diff_ref):
        t = lax.broadcasted_iota(jnp.int32, (s, POOL_GC), 0)
        for g, win in enumerate(POOL_WINDOWS):
            cols = slice(g * POOL_GC, (g + 1) * POOL_GC)
            u = u_ref[:, cols]
            acc, step = u, 1
            while step < win:
                acc = acc + jnp.where(t >= step, pltpu.roll(acc, step, 0), 0.0)
                step *= 2
            cnt = jnp.minimum(t + 1, win).astype(F32)
            diff = acc / cnt - u
            diff_ref[:, cols] = diff
            ypre = jnp.dot(diff.astype(BF16), w_ref[g].astype(BF16), preferred_element_type=F32)
            y_ref[:, cols] = (ypre * sc_ref[:, cols]).astype(BF16)

    return pl.pallas_call(
        body, name=name, grid=(1,),
        out_shape=(jax.ShapeDtypeStruct((s, POOL_WIDTH), BF16), jax.ShapeDtypeStruct((s, POOL_WIDTH), F32)),
        in_specs=[pl.BlockSpec((s, POOL_WIDTH), lambda i: (0, 0)),
                  pl.BlockSpec(pool_w.shape, lambda i: (0, 0, 0)),
                  pl.BlockSpec((1, POOL_WIDTH), lambda i: (0, 0))],
        out_specs=(pl.BlockSpec((s, POOL_WIDTH), lambda i: (0, 0)),
                   pl.BlockSpec((s, POOL_WIDTH), lambda i: (0, 0))),
        compiler_params=_params(("arbitrary",)),
    )(z, pool_w, pool_scale)


def _pool_bwd(dycat, diff, pool_w, pool_scale, name):
    s = diff.shape[0]

    def body(dy_ref, diff_ref, w_ref, sc_ref, du_ref, dw_ref, dsc_ref):
        t = lax.broadcasted_iota(jnp.int32, (s, POOL_GC), 0)
        for g, win in enumerate(POOL_WINDOWS):
            cols = slice(g * POOL_GC, (g + 1) * POOL_GC)
            dy, dfb, wb = dy_ref[:, cols], diff_ref[:, cols].astype(BF16), w_ref[g].astype(BF16)
            ypre = jnp.dot(dfb, wb, preferred_element_type=F32)
            dsc_ref[:, cols] = jnp.sum(dy * ypre, axis=0, keepdims=True)
            dypre = (dy * sc_ref[:, cols]).astype(BF16)
            ddiff = lax.dot_general(dypre, wb, (((1,), (1,)), ((), ())), preferred_element_type=F32)
            dw_ref[g] = lax.dot_general(dfb, dypre, (((0,), (0,)), ((), ())), preferred_element_type=F32)
            cnt = jnp.minimum(t + 1, win).astype(F32)
            acc, step = ddiff / cnt, 1
            while step < win:
                acc = acc + jnp.where(t < s - step, pltpu.roll(acc, s - step, 0), 0.0)
                step *= 2
            du_ref[:, cols] = acc - ddiff

    full = pl.BlockSpec((s, POOL_WIDTH), lambda i: (0, 0))
    return pl.pallas_call(
        body, name=name, grid=(1,),
        out_shape=(jax.ShapeDtypeStruct((s, POOL_WIDTH), F32),
                   jax.ShapeDtypeStruct(pool_w.shape, F32),
                   jax.ShapeDtypeStruct((1, POOL_WIDTH), F32)),
        in_specs=[full, full, pl.BlockSpec(pool_w.shape, lambda i: (0, 0, 0)),
                  pl.BlockSpec((1, POOL_WIDTH), lambda i: (0, 0))],
        out_specs=(full, pl.BlockSpec(pool_w.shape, lambda i: (0, 0, 0)),
                   pl.BlockSpec((1, POOL_WIDTH), lambda i: (0, 0))),
        compiler_params=_params(("arbitrary",)),
    )(dycat, diff, pool_w, pool_scale)


def _rope_tables(positions, name):
    s = positions.shape[0]
    ts = _tile(s, 512)
    freq = 1.0 / (ROPE_THETA ** (np.arange(0, QK_ROPE, 2, dtype=np.float32) / QK_ROPE))
    table = np.zeros((1, LANE), np.float32)
    table[0, :QK_ROPE // 2] = freq
    table[0, QK_ROPE // 2:QK_ROPE] = freq

    def body(pos_ref, f_ref, cos_ref, sin_ref):
        ang = pos_ref[...].astype(F32) * f_ref[...]
        cos_ref[...] = jnp.cos(ang)
        sin_ref[...] = jnp.sin(ang)

    out = jax.ShapeDtypeStruct((s, LANE), F32)
    blk = pl.BlockSpec((ts, LANE), lambda i: (i, 0))
    return pl.pallas_call(
        body, name=name, grid=(s // ts,), out_shape=(out, out),
        in_specs=[pl.BlockSpec((ts, 1), lambda i: (i, 0)), _vec_spec(LANE)], out_specs=(blk, blk),
        compiler_params=_params(("parallel",)),
    )(positions, jnp.asarray(table))


def _lane_mod64_low(shape):
    return (lax.broadcasted_iota(jnp.int32, shape, 1) % QK_ROPE) < (QK_ROPE // 2)


def _rope(x, cos, sin):
    rot = jnp.where(_lane_mod64_low(x.shape), -pltpu.roll(x, LANE - 32, 1), pltpu.roll(x, 32, 1))
    return x * cos + rot * sin


def _rope_t(dy, cos, sin):
    w = dy * sin
    rot_t = jnp.where(_lane_mod64_low(dy.shape), pltpu.roll(w, LANE - 32, 1), -pltpu.roll(w, 32, 1))
    return dy * cos + rot_t


def _plain_rms(x, g):
    r = lax.rsqrt(jnp.mean(x * x, axis=-1, keepdims=True) + EPS)
    return (x * r) * g, x * r, r


O_Q, O_KV, O_KR = POOL_WIDTH, POOL_WIDTH + Q_LORA, POOL_WIDTH + Q_LORA + KV_LORA


def _qkv_fwd(z, qn, kvn, wq, wkv, cos, sin, name):
    s = z.shape[0]
    ts = _tile(s, 256)

    def body(z_ref, qn_ref, kvn_ref, wq_ref, wkv_ref, cos_ref, sin_ref, q_ref, k_ref, v_ref, cqn_ref, ckvn_ref):
        cosv, sinv = cos_ref[...], sin_ref[...]
        cqn = _plain_rms(z_ref[:, O_Q:O_KV], qn_ref[...])[0].astype(BF16)
        ckvn = _plain_rms(z_ref[:, O_KV:O_KR], kvn_ref[...])[0].astype(BF16)
        cqn_ref[...] = cqn
        ckvn_ref[...] = ckvn
        nt = (((1,), (1,)), ((), ()))
        q = lax.dot_general(cqn, wq_ref[...], nt, preferred_element_type=F32)
        kv = lax.dot_general(ckvn, wkv_ref[...], nt, preferred_element_type=F32)
        kr = _rope(z_ref[:, O_KR:IN_PAD], cosv, sinv).astype(BF16)
        for h in range(N_HEADS):
            o = h * HEAD_PAD
            q_ref[:, o:o + QK_NOPE] = q[:, o:o + QK_NOPE].astype(BF16)
            q_ref[:, o + QK_NOPE:o + HEAD_PAD] = _rope(q[:, o + QK_NOPE:o + HEAD_PAD], cosv, sinv).astype(BF16)
            k_ref[:, o:o + QK_NOPE] = kv[:, o:o + QK_NOPE].astype(BF16)
            k_ref[:, o + QK_NOPE:o + HEAD_PAD] = kr
            v_ref[:, h * V_HEAD:(h + 1) * V_HEAD] = kv[:, o + QK_NOPE:o + HEAD_PAD].astype(BF16)

    def row(w):
        return pl.BlockSpec((ts, w), lambda i: (i, 0))

    def whole(arr):
        return pl.BlockSpec(arr.shape, lambda i: (0, 0))

    hp = N_HEADS * HEAD_PAD
    return pl.pallas_call(
        body, name=name, grid=(s // ts,),
        out_shape=(jax.ShapeDtypeStruct((s, hp), BF16), jax.ShapeDtypeStruct((s, hp), BF16),
                   jax.ShapeDtypeStruct((s, N_HEADS * V_HEAD), BF16),
                   jax.ShapeDtypeStruct((s, Q_LORA), BF16), jax.ShapeDtypeStruct((s, KV_LORA), BF16)),
        in_specs=[row(IN_PAD), whole(qn), whole(kvn), whole(wq), whole(wkv), row(LANE), row(LANE)],
        out_specs=(row(hp), row(hp), row(N_HEADS * V_HEAD), row(Q_LORA), row(KV_LORA)),
        compiler_params=_params(("parallel",)),
    )(z, qn, kvn, wq, wkv, cos, sin)


def _qkv_bwd(dq, dk, dv, du, z, qn, kvn, wq, wkv, cos, sin, name):
    s = z.shape[0]
    ts = _tile(s, 256)

    def norm_bwd(x, g, dy):
        _, xn, r = _plain_rms(x, g)
        dxn = dy * g
        return r * (dxn - xn * jnp.mean(dxn * xn, axis=-1, keepdims=True)), jnp.sum(dy * xn, axis=0, keepdims=True)

    def body(dq_ref, dk_ref, dv_ref, du_ref, z_ref, qn_ref, kvn_ref, wq_ref, wkv_ref, cos_ref, sin_ref,
             dz_ref, dqb_ref, dkvb_ref, dqn_ref, dkvn_ref):
        @pl.when(pl.program_id(0) == 0)
        def _():
            dqn_ref[...] = jnp.zeros_like(dqn_ref)
            dkvn_ref[...] = jnp.zeros_like(dkvn_ref)

        cosv, sinv = cos_ref[...], sin_ref[...]
        dkr = jnp.zeros((ts, LANE), F32)
        for h in range(N_HEADS):
            o = h * HEAD_PAD
            dqb_ref[:, o:o + QK_NOPE] = dq_ref[:, o:o + QK_NOPE].astype(BF16)
            dqb_ref[:, o + QK_NOPE:o + HEAD_PAD] = _rope_t(dq_ref[:, o + QK_NOPE:o + HEAD_PAD], cosv, sinv).astype(BF16)
            dkvb_ref[:, o:o + QK_NOPE] = dk_ref[:, o:o + QK_NOPE].astype(BF16)
            dkvb_ref[:, o + QK_NOPE:o + HEAD_PAD] = dv_ref[:, h * V_HEAD:(h + 1) * V_HEAD].astype(BF16)
            dkr = dkr + dk_ref[:, o + QK_NOPE:o + HEAD_PAD]
        dcqn = jnp.dot(dqb_ref[...], wq_ref[...], preferred_element_type=F32)
        dckvn = jnp.dot(dkvb_ref[...], wkv_ref[...], preferred_element_type=F32)
        dcq, dqn = norm_bwd(z_ref[:, O_Q:O_KV], qn_ref[...], dcqn)
        dckv, dkvn = norm_bwd(z_ref[:, O_KV:O_KR], kvn_ref[...], dckvn)
        dqn_ref[...] += dqn
        dkvn_ref[...] += dkvn
        dz_ref[:, 0:O_Q] = du_ref[...].astype(BF16)
        dz_ref[:, O_Q:O_KV] = dcq.astype(BF16)
        dz_ref[:, O_KV:O_KR] = dckv.astype(BF16)
        dz_ref[:, O_KR:IN_PAD] = _rope_t(dkr, cosv, sinv).astype(BF16)

    def row(w):
        return pl.BlockSpec((ts, w), lambda i: (i, 0))

    def whole(arr):
        return pl.BlockSpec(arr.shape, lambda i: (0, 0))

    hp = N_HEADS * HEAD_PAD
    return pl.pallas_call(
        body, name=name, grid=(s // ts,),
        out_shape=(jax.ShapeDtypeStruct((s, IN_PAD), BF16), jax.ShapeDtypeStruct((s, hp), BF16),
                   jax.ShapeDtypeStruct((s, hp), BF16),
                   jax.ShapeDtypeStruct((1, Q_LORA), F32), jax.ShapeDtypeStruct((1, KV_LORA), F32)),
        in_specs=[row(hp), row(hp), row(N_HEADS * V_HEAD), row(POOL_WIDTH), row(IN_PAD),
                  whole(qn), whole(kvn), whole(wq), whole(wkv), row(LANE), row(LANE)],
        out_specs=(row(IN_PAD), row(hp), row(hp), whole(qn), whole(kvn)),
        compiler_params=_params(("arbitrary",)),
    )(dq, dk, dv, du, z, qn, kvn, wq, wkv, cos, sin)


def _causal_scores(q, k, i, tq, klen):
    sc = lax.dot_general(q, k, (((1,), (1,)), ((), ())), preferred_element_type=F32) * SOFTMAX_SCALE
    qpos = i * tq + lax.broadcasted_iota(jnp.int32, (tq, klen), 0)
    kpos = lax.broadcasted_iota(jnp.int32, (tq, klen), 1)
    return jnp.where(qpos >= kpos, sc, -jnp.inf)


ATTN_TQ = 512
ATTN_SEGMENTS = 4


def _by_key_prefix(i, nq, tq, compute):
    nseg = min(ATTN_SEGMENTS, nq)
    per = nq // nseg
    for r in range(nseg):
        pl.when(i // per == r)(lambda r=r: compute((r + 1) * per * tq))


def _attn_fwd(q, k, v, name):
    s = q.shape[0]
    tq = _tile(s, ATTN_TQ)
    nq = s // tq

    def body(q_ref, k_ref, v_ref, o_ref, lse_ref):
        i = pl.program_id(1)

        def compute(klen):
            sc = _causal_scores(q_ref[...], k_ref[0:klen, :], i, tq, klen)
            mx = jnp.max(sc, axis=-1, keepdims=True)
            p = jnp.exp(sc - mx)
            den = jnp.sum(p, axis=-1, keepdims=True)
            o_ref[...] = jnp.dot((p / den).astype(BF16), v_ref[0:klen, :], preferred_element_type=F32)
            lse_ref[...] = mx + jnp.log(den)

        _by_key_prefix(i, nq, tq, compute)

    return pl.pallas_call(
        body, name=name, grid=(N_HEADS, s // tq),
        out_shape=(jax.ShapeDtypeStruct((s, N_HEADS * V_HEAD), F32), jax.ShapeDtypeStruct((N_HEADS, s, 1), F32)),
        in_specs=[pl.BlockSpec((tq, HEAD_PAD), lambda h, i: (i, h)),
                  pl.BlockSpec((s, HEAD_PAD), lambda h, i: (0, h)),
                  pl.BlockSpec((s, V_HEAD), lambda h, i: (0, h))],
        out_specs=(pl.BlockSpec((tq, V_HEAD), lambda h, i: (i, h)),
                   pl.BlockSpec((None, tq, 1), lambda h, i: (h, i, 0))),
        compiler_params=_params(("parallel", "parallel")),
    )(q, k, v)


def _attn_bwd(q, k, v, lse, dycat, name):
    s = q.shape[0]
    tq = _tile(s, ATTN_TQ)
    nq = s // tq
    tn_dims = (((0,), (0,)), ((), ()))

    def body(q_ref, k_ref, v_ref, lse_ref, do_ref, dq_ref, dk_ref, dv_ref):
        i = pl.program_id(1)

        @pl.when(i == 0)
        def _():
            dk_ref[...] = jnp.zeros_like(dk_ref)
            dv_ref[...] = jnp.zeros_like(dv_ref)

        def compute(klen):
            qv, kv_, dob = q_ref[...], k_ref[0:klen, :], do_ref[...].astype(BF16)
            sc = _causal_scores(qv, kv_, i, tq, klen)
            p = jnp.exp(sc - lse_ref[...])
            dp = lax.dot_general(dob, v_ref[0:klen, :], (((1,), (1,)), ((), ())), preferred_element_type=F32)
            ds = (p * (dp - jnp.sum(dp * p, axis=-1, keepdims=True)) * SOFTMAX_SCALE).astype(BF16)
            dq_ref[...] = jnp.dot(ds, kv_, preferred_element_type=F32)
            dk_ref[0:klen, :] += lax.dot_general(ds, qv, tn_dims, preferred_element_type=F32)
            dv_ref[0:klen, :] += lax.dot_general(p.astype(BF16), dob, tn_dims, preferred_element_type=F32)

        _by_key_prefix(i, nq, tq, compute)

    n_pool_blocks = POOL_WIDTH // V_HEAD
    return pl.pallas_call(
        body, name=name, grid=(N_HEADS, s // tq),
        out_shape=(jax.ShapeDtypeStruct((s, N_HEADS * HEAD_PAD), F32),
                   jax.ShapeDtypeStruct((s, N_HEADS * HEAD_PAD), F32),
                   jax.ShapeDtypeStruct((s, N_HEADS * V_HEAD), F32)),
        in_specs=[pl.BlockSpec((tq, HEAD_PAD), lambda h, i: (i, h)),
                  pl.BlockSpec((s, HEAD_PAD), lambda h, i: (0, h)),
                  pl.BlockSpec((s, V_HEAD), lambda h, i: (0, h)),
                  pl.BlockSpec((None, tq, 1), lambda h, i: (h, i, 0)),
                  pl.BlockSpec((tq, V_HEAD), lambda h, i: (i, n_pool_blocks + h))],
        out_specs=(pl.BlockSpec((tq, HEAD_PAD), lambda h, i: (i, h)),
                   pl.BlockSpec((s, HEAD_PAD), lambda h, i: (0, h)),
                   pl.BlockSpec((s, V_HEAD), lambda h, i: (0, h))),
        compiler_params=_params(("parallel", "arbitrary")),
    )(q, k, v, lse, dycat)


def _loss_head(x, gw, target, name):
    s, d = x.shape
    ts = _tile(s, 256)

    def body(x_ref, gw_ref, tgt_ref, loss_ref, dx_ref, dgw_ref):
        @pl.when(pl.program_id(0) == 0)
        def _():
            loss_ref[...] = jnp.zeros_like(loss_ref)
            dgw_ref[...] = jnp.zeros_like(dgw_ref)

        xv, gwv = x_ref[...], gw_ref[...]
        r = lax.rsqrt(jnp.mean(xv * xv, axis=-1, keepdims=True) + EPS)
        xn = xv * r
        err = xn * gwv - tgt_ref[...]
        loss_ref[...] += 0.5 * jnp.sum(jnp.mean(err * err, axis=-1, keepdims=True))
        dy = err / d
        dgw_ref[...] += jnp.sum(dy * xn, axis=0, keepdims=True)
        dxn = dy * gwv
        dx_ref[...] = r * (dxn - xn * jnp.mean(dxn * xn, axis=-1, keepdims=True))

    row = pl.BlockSpec((ts, d), lambda i: (i, 0))
    return pl.pallas_call(
        body, name=name, grid=(s // ts,),
        out_shape=(jax.ShapeDtypeStruct((8, LANE), F32), jax.ShapeDtypeStruct((s, d), F32),
                   jax.ShapeDtypeStruct((1, d), F32)),
        in_specs=[row, _vec_spec(d), row],
        out_specs=(pl.BlockSpec((8, LANE), lambda i: (0, 0)), row, _vec_spec(d)),
        compiler_params=_params(("arbitrary",)),
    )(x, gw, target)


def _ada_mod(c_all, ada_w, ada_b, name):
    nl, d, cols = ada_w.shape

    def body(c_ref, w_ref, b_ref, o_ref):
        cv = c_ref[...]
        act = (cv * jax.nn.sigmoid(cv)).astype(BF16)
        o_ref[...] = jnp.dot(act, w_ref[...].astype(BF16), preferred_element_type=F32) + b_ref[...]

    return pl.pallas_call(
        body, name=name, grid=(nl,), out_shape=jax.ShapeDtypeStruct((nl, N_DEV, cols), F32),
        in_specs=[pl.BlockSpec((N_DEV, d), lambda l: (0, 0)),
                  pl.BlockSpec((None, d, cols), lambda l: (l, 0, 0)),
                  pl.BlockSpec((None, 1, cols), lambda l: (l, 0, 0))],
        out_specs=pl.BlockSpec((None, N_DEV, cols), lambda l: (l, 0, 0)),
        compiler_params=_params(("parallel",)),
    )(c_all, ada_w, ada_b)


def _ada_grad(c_pad, dmod_pad, name):
    nl, kpad, cols = dmod_pad.shape
    d = c_pad.shape[1]

    def body(c_ref, dm_ref, o_ref):
        cv = c_ref[...]
        act = (cv * jax.nn.sigmoid(cv)).astype(BF16)
        o_ref[...] = lax.dot_general(act, dm_ref[...].astype(BF16), (((0,), (0,)), ((), ())),
                                     preferred_element_type=F32)

    return pl.pallas_call(
        body, name=name, grid=(nl,), out_shape=jax.ShapeDtypeStruct((nl, d, cols), F32),
        in_specs=[pl.BlockSpec((kpad, d), lambda l: (0, 0)),
                  pl.BlockSpec((None, kpad, cols), lambda l: (l, 0, 0))],
        out_specs=pl.BlockSpec((None, d, cols), lambda l: (l, 0, 0)),
        compiler_params=_params(("parallel",)),
    )(c_pad, dmod_pad)


def _adamw_math(w, g, m, v):
    nm = ADAM_B1 * m + (1.0 - ADAM_B1) * g
    nv = ADAM_B2 * v + (1.0 - ADAM_B2) * (g * g)
    m_hat = nm / (1.0 - ADAM_B1 ** ADAM_STEP)
    v_hat = nv / (1.0 - ADAM_B2 ** ADAM_STEP)
    return -ADAM_LR * (m_hat / (jnp.sqrt(v_hat) + ADAM_EPS) + ADAM_WD * w), nm, nv


def _adamw_rows(w3, gbuf, row_off, m3, v3, name):
    nl, r, d = w3.shape
    tr = _row_tile(math.gcd(r, row_off) if row_off else r, 176)
    first = row_off // tr

    def body(w_ref, g_ref, m_ref, v_ref, go_ref, d_ref, nm_ref, nv_ref):
        gv = g_ref[...]
        go_ref[...] = gv
        d_ref[...], nm_ref[...], nv_ref[...] = _adamw_math(w_ref[...], gv, m_ref[...], v_ref[...])

    blk = pl.BlockSpec((None, tr, d), lambda l, i: (l, i, 0))
    gblk = pl.BlockSpec((None, tr, d), lambda l, i: (l, first + i, 0))
    out = jax.ShapeDtypeStruct((nl, r, d), F32)
    return pl.pallas_call(
        body, name=name, grid=(nl, r // tr), out_shape=(out, out, out, out),
        in_specs=[blk, gblk, blk, blk], out_specs=(blk, blk, blk, blk),
        compiler_params=_params(("parallel", "parallel")),
    )(w3, gbuf, m3, v3)


def _adamw(w, g, m, v, name):
    rows, cols = w.shape
    tr = _row_tile(rows, 512)

    def body(w_ref, g_ref, m_ref, v_ref, d_ref, nm_ref, nv_ref):
        d_ref[...], nm_ref[...], nv_ref[...] = _adamw_math(w_ref[...], g_ref[...], m_ref[...], v_ref[...])

    blk = pl.BlockSpec((tr, cols), lambda i: (i, 0))
    out = jax.ShapeDtypeStruct((rows, cols), F32)
    return pl.pallas_call(
        body, name=name, grid=(rows // tr,), out_shape=(out, out, out),
        in_specs=[blk, blk, blk, blk], out_specs=(blk, blk, blk),
        compiler_params=_params(("parallel",)),
    )(w, g, m, v)


def _adamw_nd(w, g, m, v, name):
    shape = w.shape
    flat = (lambda t: t.reshape(1, -1)) if w.ndim == 1 else (lambda t: t.reshape(-1, shape[-1]))
    return tuple(t.reshape(shape) for t in _adamw(flat(w), flat(g), flat(m), flat(v), name))


def _pad_rows(t, rows):
    return jnp.pad(t, ((0, rows - t.shape[0]), (0, 0)))


def _pack_shard_layer(l, wts):
    def tr(name):
        return wts[name][l].astype(BF16).T

    parts = [tr("ffn1_w_gate"), tr("ffn1_w_up"), wts["ffn1_w_down"][l].astype(BF16),
             tr("ffn2_w_gate"), tr("ffn2_w_up"), wts["ffn2_w_down"][l].astype(BF16),
             wts["w_out"][l].astype(BF16),
             tr("w_kv_b").reshape(KV_SH_ROWS, D_MODEL),
             _pad_rows(tr("w_in"), 160),
             _pad_rows(tr("w_q_b").reshape(Q_SH_ROWS, D_MODEL), Q_PAD_ROWS)]
    return jnp.concatenate(parts, axis=0)


def _full_weights(lands):
    w = dict(zip(("g1", "u1", "d1", "g2", "u2", "d2", "out"), lands))
    small = lands[-1].reshape(N_DEV, SMALL_ROWS, D_MODEL)
    o_in, o_q = OFF_IN - OFF_KV, OFF_Q - OFF_KV
    w["kv"] = small[:, :KV_SH_ROWS].reshape(N_HEADS * HEAD_PAD, KV_LORA)
    w["in"] = _pad_rows(small[:, o_in:o_in + IN_SH].reshape(IN_COLS, D_MODEL), IN_PAD)
    wq = small[:, o_q:o_q + Q_SH_ROWS].reshape(N_HEADS, QK_HEAD, Q_LORA)
    w["q"] = jnp.pad(wq, ((0, 0), (0, HEAD_PAD - QK_HEAD), (0, 0))).reshape(N_HEADS * HEAD_PAD, Q_LORA)
    return w


def _grad_sources_b(gr):
    gq = gr["q"].reshape(N_HEADS, HEAD_PAD, Q_LORA)[:, :QK_HEAD].reshape(N_DEV, Q_SH_ROWS, D_MODEL)
    small = jnp.concatenate([
        gr["kv"].reshape(N_DEV, KV_SH_ROWS, D_MODEL),
        jnp.pad(gr["in"][:IN_COLS].reshape(N_DEV, IN_SH, D_MODEL), ((0, 0), (0, 160 - IN_SH), (0, 0))),
        jnp.pad(gq, ((0, 0), (0, Q_PAD_ROWS - Q_SH_ROWS), (0, 0)))], axis=1)
    return [gr["g2"], gr["u2"], gr["d2"], gr["out"], small.reshape(N_DEV * SMALL_ROWS, D_MODEL)]


def _small_layout(nl):
    names = [("dmod", nl * N_MOD), ("ffn1_norm", nl), ("mix_norm", nl), ("ffn2_norm", nl), ("q_a_norm", nl),
             ("kv_a_norm", nl), ("pool_scale", nl), ("final_norm", 1), ("loss", 1),
             ("pool_w", nl * 4 * POOL_GC * POOL_GC // D_MODEL)]
    off, table = 0, {}
    for name, n in names:
        table[name] = (off, n)
        off += -(-n // 8) * 8
    return table, off


def _to_rows(t, width=D_MODEL):
    n, w = t.shape
    return jnp.pad(t, ((0, -(-n // 8) * 8 - n), (0, width - w)))


def kernel(x, c, positions, ada_w, ada_b, ffn1_norm, ffn1_w_gate, ffn1_w_up, ffn1_w_down, mix_norm, w_in, pool_w, pool_scale, q_a_norm, w_q_b, kv_a_norm, w_kv_b, w_out, ffn2_norm, ffn2_w_gate, ffn2_w_up, ffn2_w_down, final_norm, loss_target, m_ada_w, m_ada_b, m_ffn1_norm, m_ffn1_w_gate, m_ffn1_w_up, m_ffn1_w_down, m_mix_norm, m_w_in, m_pool_w, m_pool_scale, m_q_a_norm, m_w_q_b, m_kv_a_norm, m_w_kv_b, m_w_out, m_ffn2_norm, m_ffn2_w_gate, m_ffn2_w_up, m_ffn2_w_down, m_final_norm, v_ada_w, v_ada_b, v_ffn1_norm, v_ffn1_w_gate, v_ffn1_w_up, v_ffn1_w_down, v_mix_norm, v_w_in, v_pool_w, v_pool_scale, v_q_a_norm, v_w_q_b, v_kv_a_norm, v_w_kv_b, v_w_out, v_ffn2_norm, v_ffn2_w_gate, v_ffn2_w_up, v_ffn2_w_down, v_final_norm):
    wts = dict(ada_w=ada_w, ada_b=ada_b, ffn1_norm=ffn1_norm, ffn1_w_gate=ffn1_w_gate, ffn1_w_up=ffn1_w_up,
               ffn1_w_down=ffn1_w_down, mix_norm=mix_norm, w_in=w_in, pool_w=pool_w, pool_scale=pool_scale,
               q_a_norm=q_a_norm, w_q_b=w_q_b, kv_a_norm=kv_a_norm, w_kv_b=w_kv_b, w_out=w_out,
               ffn2_norm=ffn2_norm, ffn2_w_gate=ffn2_w_gate, ffn2_w_up=ffn2_w_up, ffn2_w_down=ffn2_w_down,
               final_norm=final_norm)
    mom_m = dict(ada_w=m_ada_w, ada_b=m_ada_b, ffn1_norm=m_ffn1_norm, ffn1_w_gate=m_ffn1_w_gate,
                 ffn1_w_up=m_ffn1_w_up, ffn1_w_down=m_ffn1_w_down, mix_norm=m_mix_norm, w_in=m_w_in,
                 pool_w=m_pool_w, pool_scale=m_pool_scale, q_a_norm=m_q_a_norm, w_q_b=m_w_q_b,
                 kv_a_norm=m_kv_a_norm, w_kv_b=m_w_kv_b, w_out=m_w_out, ffn2_norm=m_ffn2_norm,
                 ffn2_w_gate=m_ffn2_w_gate, ffn2_w_up=m_ffn2_w_up, ffn2_w_down=m_ffn2_w_down,
                 final_norm=m_final_norm)
    mom_v = dict(ada_w=v_ada_w, ada_b=v_ada_b, ffn1_norm=v_ffn1_norm, ffn1_w_gate=v_ffn1_w_gate,
                 ffn1_w_up=v_ffn1_w_up, ffn1_w_down=v_ffn1_w_down, mix_norm=v_mix_norm, w_in=v_w_in,
                 pool_w=v_pool_w, pool_scale=v_pool_scale, q_a_norm=v_q_a_norm, w_q_b=v_w_q_b,
                 kv_a_norm=v_kv_a_norm, w_kv_b=v_w_kv_b, w_out=v_w_out, ffn2_norm=v_ffn2_norm,
                 ffn2_w_gate=v_ffn2_w_gate, ffn2_w_up=v_ffn2_w_up, ffn2_w_down=v_ffn2_w_down,
                 final_norm=v_final_norm)
    order = list(wts)
    nl = ada_w.shape[0]
    seq = x.shape[1]
    me = 4 * lax.axis_index("x") + 2 * lax.axis_index("y") + lax.axis_index("c")
    ada_cols = ada_w.shape[2]

    def after_token(t, token):
        return t + token[0:1, 0:1].astype(t.dtype)

    packs = [_pack_shard_layer(l, wts) for l in range(nl)]

    c_all = _all_gather(jnp.broadcast_to(c, (8, D_MODEL)), "gather_c")[::8]

    ada_b_mine = lax.dynamic_slice_in_dim(ada_b, me * ada_cols, ada_cols, axis=1).reshape(nl, 1, ada_cols)
    mod_part = _ada_mod(c_all, ada_w, ada_b_mine, "ada_mod")
    mod_all = _all_gather(mod_part.reshape(nl * N_DEV, ada_cols), "gather_mod")
    mod_all = mod_all.reshape(N_DEV, nl, N_DEV, ada_cols)
    mod = lax.dynamic_index_in_dim(mod_all, me, axis=2, keepdims=False)
    mod = mod.transpose(1, 0, 2).reshape(nl, N_MOD, 1, D_MODEL)

    flight_a = _gather_start(packs[0][:SPLIT_AB], ROWS_A, mod, "gather_start_0a")
    flight_b = _gather_start(packs[0][SPLIT_AB:], ROWS_B, flight_a[4], "gather_start_0b")
    last_start = flight_b[4]
    if nl > 1:
        in_flight = _gather_start(packs[1], ROWS_ALL, last_start, "gather_start_1")
        last_start = in_flight[4]

    cos, sin = _rope_tables(after_token(positions.reshape(seq, 1), last_start), "rope_tables")

    def vec(t):
        return t.reshape(1, -1)

    def landed(flight, rows_list, after, tag):
        send_sems, recv_sems, pk, lands, _ = flight
        pk, lands = _gather_wait(send_sems, recv_sems, pk, lands, after, f"gather_wait_{tag}")
        return _gather_finish(pk, rows_list, lands, "gather_finish")

    xs = x.reshape(seq, D_MODEL)
    saved = []
    for l in range(nl):
        norm1 = vec(ffn1_norm[l])
        if l == 0:
            lands = landed(flight_a, ROWS_A, cos, "0a")
        elif l + 1 < nl:
            in_flight = _gather_start(packs[l + 1], ROWS_ALL, lands[0], f"gather_start_{l + 1}")
            norm1 = after_token(norm1, in_flight[4])
        sv = {}

        def ffn_fwd(xin, norm, k0, wg, wu, wd, tag):
            h, a, b, t = _ffn_up(xin, norm, mod[l, k0], mod[l, k0 + 1], wg, wu, "ffn_up")
            y, xout = _mm(t, wd, "nn", "ffn_down", res=xin, gate=mod[l, k0 + 2], gate_factor=0.5)
            sv[tag] = dict(x=xin, h=h, a=a, b=b, t=t, y=y)
            return xout

        xs = ffn_fwd(xs, norm1, 0, lands[0], lands[1], lands[2], "f1")
        if l == 0:
            lands = lands + landed(flight_b, ROWS_B, xs, "0b")
        w = _full_weights(lands)
        sv["w"] = w

        h2, z = _norm_mm(xs, vec(mix_norm[l]), mod[l, 3], mod[l, 4], w["in"], "mix_in")
        y_pool, diff = _pool_fwd(z, pool_w[l], vec(pool_scale[l]), "pool_fwd")
        q, k, v, cqn, ckvn = _qkv_fwd(z, vec(q_a_norm[l]), vec(kv_a_norm[l]), w["q"], w["kv"], cos, sin, "qkv_fwd")
        o, lse = _attn_fwd(q, k, v, "attn_fwd")
        ycat = jnp.concatenate([y_pool, o.astype(BF16)], axis=1)
        y2, xmix = _mm(ycat, w["out"], "nn", "mix_out", res=xs, gate=mod[l, 5], gate_factor=1.0)
        sv["mix"] = dict(x=xs, h=h2, z=z, diff=diff, q=q, k=k, v=v, cqn=cqn, ckvn=ckvn, lse=lse, ycat=ycat, y=y2)
        xs = xmix

        xs = ffn_fwd(xs, vec(ffn2_norm[l]), 6, w["g2"], w["u2"], w["d2"], "f2")
        saved.append(sv)
        if l + 1 < nl:
            lands = landed(in_flight, ROWS_ALL, xs, l + 1)

    loss_part, dx, d_final = _loss_head(xs, vec(final_norm), loss_target.reshape(seq, D_MODEL), "loss_head")

    small = {name: [None] * nl for name in ("ffn1_norm", "mix_norm", "ffn2_norm", "q_a_norm", "kv_a_norm",
                                            "pool_scale", "pool_w", "dmod")}
    core = lax.axis_index("c").astype(jnp.int32).reshape(1)
    chip = 2 * lax.axis_index("x") + lax.axis_index("y")
    exchanges = []

    def leave(srcs, rows_list, after, tag):
        return _pair_start(srcs, rows_list, after, f"pair_start_{tag}"), rows_list, tag

    def forward_on(pending, after, layer, row_off):
        (send_sems, recv_sems, srcs, land, _), rows_list, tag = pending
        srcs, land = _split_wait(send_sems, recv_sems, 1, srcs, land, after, f"pair_wait_{tag}")
        sums = _pair_sum(srcs, rows_list, land, core, "pair_sum")
        flight = _chip_exchange_start(sums, chip, after, f"exchange_start_{tag}")
        exchanges.append((flight, layer, row_off, tag))
        return flight[4]

    pending = None
    head = _gate_bwd(dx, saved[nl - 1]["f2"]["y"], mod[nl - 1, 8], 0.5, "gate_bwd")
    for l in reversed(range(nl)):
        sv = saved[l]
        w = sv["w"]
        dmod = [None] * N_MOD
        gr = {}

        def ffn_bwd(dxin, head, s_, norm, k0, wg, wu, wd, tag, below, first_after=None, mid=None):
            dy, dmod[k0 + 2] = head
            da, db, gr["d" + tag], gr["g" + tag], gr["u" + tag] = _ffn_bwd_cols(
                dy, s_["h"], s_["a"], s_["b"], s_["t"], wd, "ffn_bwd_cols", after=first_after)
            dh = _mm_pair(da, wg, db, wu, "ffn_bwd_dh", after=None if mid is None else mid(da))
            outs = _rm_bwd(dh, s_["x"], dxin, vec(norm), mod[l, k0 + 1], "rm_bwd", below=below)
            dmod[k0], dmod[k0 + 1] = outs[1], outs[2]
            return outs[0], outs[3], outs[4:]

        s_ = sv["mix"]
        dx, small["ffn2_norm"][l], head = ffn_bwd(
            dx, head, sv["f2"], ffn2_norm[l], 6, w["g2"], w["u2"], w["d2"], "2", (s_["y"], mod[l, 5], 1.0),
            first_after=None if pending is None else pending[0][4])

        mix_after = None
        if pending is not None:
            mix_after = forward_on(pending, dx, l + 1, 0)
            pending = None
        dy, dmod[5] = head
        gr["out"] = _mm(s_["ycat"], dy, "tn", "mix_out_dw", out_dtype=BF16, tm=256, after=mix_after)
        dycat = _mm(dy, w["out"], "nt", "mix_out_dx")
        du, small["pool_w"][l], small["pool_scale"][l] = _pool_bwd(dycat, s_["diff"], pool_w[l], vec(pool_scale[l]), "pool_bwd")
        dq, dk, dv = _attn_bwd(s_["q"], s_["k"], s_["v"], s_["lse"], dycat, "attn_bwd")
        dz, dqb, dkvb, small["q_a_norm"][l], small["kv_a_norm"][l] = _qkv_bwd(
            dq, dk, dv, du, s_["z"], vec(q_a_norm[l]), vec(kv_a_norm[l]), w["q"], w["kv"], cos, sin, "qkv_bwd")
        gr["q"] = _mm(dqb, s_["cqn"], "tn", "q_b_dw", out_dtype=BF16, tm=256)
        gr["kv"] = _mm(dkvb, s_["ckvn"], "tn", "kv_b_dw", out_dtype=BF16, tm=256)
        gr["in"] = _mm(dz, s_["h"], "tn", "mix_in_dw", out_dtype=BF16, tm=256)
        dh2 = _mm(dz, w["in"], "nn", "mix_in_dx")
        outs = _rm_bwd(dh2, s_["x"], dx, vec(mix_norm[l]), mod[l, 4], "rm_bwd", below=(sv["f1"]["y"], mod[l, 2], 0.5))
        dx, dmod[3], dmod[4], small["mix_norm"][l] = outs[:4]
        head = outs[4:]

        first_after, mid = None, None
        if l == 0:
            pending_b = leave(_grad_sources_b(gr), ROWS_B, dx, "0b")
            first_after = pending_b[0][4]
            mid = lambda da: forward_on(pending_b, da, 0, SPLIT_AB)
        below = (saved[l - 1]["f2"]["y"], mod[l - 1, 8], 0.5) if l > 0 else None
        dx, small["ffn1_norm"][l], head = ffn_bwd(
            dx, head, sv["f1"], ffn1_norm[l], 0, w["g1"], w["u1"], w["d1"], "1", below, first_after, mid)

        small["dmod"][l] = jnp.concatenate(dmod, axis=0)
        if l > 0:
            pending = leave([gr["g1"], gr["u1"], gr["d1"]] + _grad_sources_b(gr), ROWS_ALL, dx, l)

    grad_x = dx.reshape(x.shape)
    pending_a = leave([gr["g1"], gr["u1"], gr["d1"]], ROWS_A, dx, "0a")

    layout, small_rows = _small_layout(nl)
    pieces = {
        "dmod": jnp.concatenate(small["dmod"], axis=0),
        "ffn1_norm": jnp.concatenate(small["ffn1_norm"], axis=0),
        "mix_norm": jnp.concatenate(small["mix_norm"], axis=0),
        "ffn2_norm": jnp.concatenate(small["ffn2_norm"], axis=0),
        "q_a_norm": jnp.concatenate(small["q_a_norm"], axis=0),
        "kv_a_norm": jnp.concatenate(small["kv_a_norm"], axis=0),
        "pool_scale": jnp.concatenate(small["pool_scale"], axis=0),
        "final_norm": d_final,
        "loss": jnp.broadcast_to(loss_part[0:1, 0:1], (1, D_MODEL)),
        "pool_w": jnp.stack(small["pool_w"]).reshape(-1, D_MODEL),
    }
    small_buf = jnp.concatenate([_to_rows(pieces[name]) for name in layout], axis=0)
    def landed_sums(gbuf, entries, after):
        for (send_sems, recv_sems, sums, recv, _), layer, row_off, tag in entries:
            _, recv = _split_wait(send_sems, recv_sems, N_CHIPS - 1, sums, recv, after, f"exchange_wait_{tag}")
            gbuf = _sum_slots_into(recv, gbuf, layer, row_off, "sum_grads")
        return gbuf

    gbuf = lax.empty((nl, ROWS_L, D_MODEL), F32)
    gbuf = landed_sums(gbuf, [e for e in exchanges if e[1] > 0], pending_a[0][4])
    token_0a = forward_on(pending_a, gbuf if nl > 1 else pending_a[0][4], 0, 0)
    spread = _spread_start(small_buf, me, token_0a, "small_start")
    gbuf = landed_sums(gbuf, [e for e in exchanges if e[3] == "0b"], spread[4])

    def swap(t):
        return t.transpose(0, 2, 1)

    def same(t):
        return t

    grads, updates = {}, {}

    def update_rows(gbuf, table):
        for wname, off, view in table:
            g, d_, nm, nv = _adamw_rows(view(wts[wname]), gbuf, off, view(mom_m[wname]), view(mom_v[wname]), "adamw_rows")
            grads[wname], updates[wname] = view(g), (view(d_), view(nm), view(nv))

    update_rows(gbuf, (("ffn2_w_gate", OFF_G2, swap), ("ffn2_w_up", OFF_U2, swap), ("ffn2_w_down", OFF_D2, same),
                       ("w_out", OFF_OUT, same)))
    small_grads = {
        "w_kv_b": (gbuf[:, OFF_KV:OFF_KV + KV_SH_ROWS].reshape(nl, -1, KV_LORA).transpose(0, 2, 1), same),
        "w_in": (gbuf[:, OFF_IN:OFF_IN + IN_SH], swap),
        "w_q_b": (gbuf[:, OFF_Q:OFF_Q + Q_SH_ROWS].reshape(nl, -1, Q_LORA), swap),
    }
    for wname, (g, view) in small_grads.items():
        upd = _adamw_nd(view(wts[wname]), g, view(mom_m[wname]), view(mom_v[wname]), "adamw")
        grads[wname], updates[wname] = view(g), tuple(view(t) for t in upd)
    gbuf = landed_sums(gbuf, [e for e in exchanges if e[3] == "0a"], updates["w_q_b"][0])
    update_rows(gbuf, (("ffn1_w_gate", OFF_G1, swap), ("ffn1_w_up", OFF_U1, swap), ("ffn1_w_down", OFF_D1, same)))

    _, small_all = _split_wait(spread[0], spread[1], N_DEV - 1, spread[2], spread[3], updates["ffn1_w_down"][0],
                               "small_wait")
    small_sum = _sum_slots(small_all, "sum_small")

    def take(name, width=D_MODEL):
        off, n = layout[name]
        return small_sum[off:off + n, :width]

    late = {"ada_b": take("dmod").reshape(nl, N_MOD * D_MODEL),
            "ffn1_norm": take("ffn1_norm"), "mix_norm": take("mix_norm"), "ffn2_norm": take("ffn2_norm"),
            "q_a_norm": take("q_a_norm", Q_LORA), "kv_a_norm": take("kv_a_norm", KV_LORA),
            "pool_scale": take("pool_scale", POOL_WIDTH), "final_norm": take("final_norm").reshape(D_MODEL),
            "pool_w": take("pool_w").reshape(pool_w.shape)}
    loss = take("loss")[0, 0]

    off, n = layout["dmod"]
    dmod_all = small_all[:, off:off + n].reshape(N_DEV, nl, N_MOD * D_MODEL)
    dmod_mine = lax.dynamic_slice_in_dim(dmod_all, me * ada_cols, ada_cols, axis=2)
    dmod_pad = jnp.pad(dmod_mine.transpose(1, 0, 2), ((0, 0), (0, LANE - N_DEV), (0, 0)))
    late["ada_w"] = _ada_grad(jnp.pad(c_all, ((0, LANE - N_DEV), (0, 0))), dmod_pad, "ada_grad")
    for name, g in late.items():
        grads[name], updates[name] = g, _adamw_nd(wts[name], g, mom_m[name], mom_v[name], "adamw")

    return (loss, grad_x, *[grads[n] for n in order], *[updates[n][0] for n in order],
            *[updates[n][1] for n in order], *[updates[n][2] for n in order])
```

```python
import math

import numpy as np
import jax
import jax.numpy as jnp
from jax import lax
from jax.experimental import pallas as pl
from jax.experimental.pallas import tpu as pltpu

F32 = jnp.float32
BF16 = jnp.bfloat16

N_DEV = 8
D_MODEL = 1024
D_FF = 2816
POOL_WIDTH = 512
POOL_WINDOWS = (2, 4, 8, 16)
POOL_GC = 128
N_HEADS = 4
QK_NOPE = 128
QK_ROPE = 64
V_HEAD = 128
QK_HEAD = QK_NOPE + QK_ROPE
HEAD_PAD = 256
Q_LORA = 384
KV_LORA = 256
IN_COLS = POOL_WIDTH + Q_LORA + KV_LORA + QK_ROPE
IN_PAD = 1280
ROPE_THETA = 10000.0
SOFTMAX_SCALE = 1.0 / math.sqrt(QK_HEAD)
EPS = 1e-6
N_MOD = 9

ADAM_LR = 0.001
ADAM_B1 = 0.9
ADAM_B2 = 0.999
ADAM_EPS = 1e-08
ADAM_WD = 0.01
ADAM_STEP = 10

LANE = 128
VMEM_LIMIT = 56 * 1024 * 1024

FF_SH = D_FF // N_DEV
OFF_G1, OFF_U1, OFF_D1 = 0, FF_SH, 2 * FF_SH
OFF_G2, OFF_U2, OFF_D2 = 3 * FF_SH, 4 * FF_SH, 5 * FF_SH
OFF_OUT = 6 * FF_SH
OFF_KV = OFF_OUT + 128
OFF_IN = OFF_KV + 32
OFF_Q = OFF_IN + 160
Q_PAD_ROWS = 64
ROWS_L = OFF_Q + Q_PAD_ROWS
IN_SH = IN_COLS // N_DEV
Q_SH_ROWS = (N_HEADS * QK_HEAD // N_DEV) * Q_LORA // D_MODEL
KV_SH_ROWS = (N_HEADS * (QK_NOPE + V_HEAD) // N_DEV) * KV_LORA // D_MODEL


def _tile(dim, target):
    if dim <= target:
        return dim
    best = None
    for t in range(LANE, target + 1, LANE):
        if dim % t == 0:
            best = t
    assert best is not None, (dim, target)
    return best


def _params(sem):
    return pltpu.CompilerParams(dimension_semantics=sem, vmem_limit_bytes=VMEM_LIMIT)


def _mesh_pos():
    return lax.axis_index("x"), lax.axis_index("y"), lax.axis_index("c")


def _all_gather(x, name):
    m, n = x.shape

    def body(x_ref, out_ref, send_sems, recv_sems, local_sem):
        px, py, pc = _mesh_pos()
        me, sibling = (px, py, pc), (px, py, 1 - pc)
        chips = [(1 - px, py), (px, 1 - py), (1 - px, 1 - py)]

        def rows(bx, by, bc):
            return out_ref.at[pl.ds((4 * bx + 2 * by + bc) * m, m), :]

        def copy(k, block, to, src=None):
            return pltpu.make_async_remote_copy(
                src_ref=rows(*block) if src is None else src, dst_ref=rows(*block),
                send_sem=send_sems.at[k], recv_sem=recv_sems.at[k],
                device_id=to, device_id_type=pl.DeviceIdType.MESH)

        mine = pltpu.make_async_copy(x_ref, rows(*me), local_sem)
        mine.start()
        first = [copy(0, me, sibling, src=x_ref)]
        first += [copy(1 + j, me, (*chip, pc), src=x_ref) for j, chip in enumerate(chips)]
        for cp in first:
            cp.start()
        passed = [copy(4 + j, (*chip, pc), sibling) for j, chip in enumerate(chips)]
        for j, chip in enumerate(chips):
            copy(1 + j, (*chip, pc), me).wait_recv()
            passed[j].start()
        copy(0, sibling, me).wait_recv()
        for j, chip in enumerate(chips):
            copy(4 + j, (*chip, 1 - pc), me).wait_recv()
        for cp in first + passed:
            cp.wait_send()
        mine.wait()

    hbm = pl.BlockSpec(memory_space=pltpu.HBM)
    return pl.pallas_call(
        body, name=name,
        out_shape=jax.ShapeDtypeStruct((N_DEV * m, n), x.dtype),
        in_specs=[hbm], out_specs=hbm,
        scratch_shapes=[pltpu.SemaphoreType.DMA((7,)), pltpu.SemaphoreType.DMA((7,)),
                        pltpu.SemaphoreType.DMA],
    )(x)


SMALL_ROWS = ROWS_L - OFF_KV
ROWS_A = [FF_SH] * 3
ROWS_B = [FF_SH] * 3 + [128, SMALL_ROWS]
ROWS_ALL = ROWS_A + ROWS_B
SPLIT_AB = sum(ROWS_A)
HBM_SPEC = pl.BlockSpec(memory_space=pltpu.HBM)
SEM_SPEC = pl.BlockSpec(memory_space=pltpu.SEMAPHORE)
ANY_SPEC = pl.BlockSpec(memory_space=pl.ANY)
EFFECT = pltpu.SideEffectType.DATAFLOW_SIDE_EFFECTING


def _hbm(t):
    return pltpu.with_memory_space_constraint(t, pltpu.HBM)


def _whole_wait(ref, send_sem, recv_sem, peer):
    return pltpu.make_async_remote_copy(src_ref=ref, dst_ref=ref, send_sem=send_sem, recv_sem=recv_sem,
                                        device_id=peer, device_id_type=pl.DeviceIdType.MESH)


def _offsets(rows_list):
    return [sum(rows_list[:i]) for i in range(len(rows_list))]


def _gather_start(packed, rows_list, after, name):
    n = len(rows_list)
    offs = _offsets(rows_list)
    lands = [_hbm(lax.empty((N_DEV * rows, D_MODEL), BF16)) for rows in rows_list]

    def body(packed_ref, *refs):
        land = refs[:n]
        send_sems, recv_sems = refs[n + 1], refs[n + 2]
        token = refs[-1]
        px, py, pc = _mesh_pos()
        me = 4 * px + 2 * py + pc
        peers = [(px, py, 1 - pc), (1 - px, py, pc), (px, 1 - py, pc), (1 - px, 1 - py, pc)]
        for k, peer in enumerate(peers):
            for off, rows, land_ref in zip(offs, rows_list, land):
                pltpu.make_async_remote_copy(
                    src_ref=packed_ref.at[pl.ds(off, rows), :], dst_ref=land_ref.at[pl.ds(me * rows, rows), :],
                    send_sem=send_sems.at[k], recv_sem=recv_sems.at[k],
                    device_id=peer, device_id_type=pl.DeviceIdType.MESH).start()
        token[...] = jnp.zeros_like(token)

    outs = pl.pallas_call(
        body, name=name,
        out_shape=(pltpu.SemaphoreType.DMA((4,)), pltpu.SemaphoreType.DMA((4,)), pltpu.HBM(packed.shape, BF16),
                   *[pltpu.HBM(t.shape, BF16) for t in lands], jax.ShapeDtypeStruct((8, LANE), F32)),
        in_specs=(HBM_SPEC,) * (1 + n) + (ANY_SPEC,),
        out_specs=(SEM_SPEC, SEM_SPEC) + (HBM_SPEC,) * (1 + n) + (pl.BlockSpec(memory_space=pltpu.VMEM),),
        input_output_aliases={i: 2 + i for i in range(1 + n)},
        compiler_params=pltpu.CompilerParams(has_side_effects=EFFECT),
    )(_hbm(packed), *lands, after)
    return outs[0], outs[1], outs[2], list(outs[3:3 + n]), outs[-1]


def _gather_wait(send_sems, recv_sems, packed, lands, after, name):
    n = len(lands)

    def body(packed_ref, *refs):
        s_sems, r_sems = refs[n], refs[n + 1]
        me = _mesh_pos()
        for k in range(4):
            cp = _whole_wait(packed_ref, s_sems.at[k], r_sems.at[k], me)
            cp.wait_send()
            cp.wait_recv()

    outs = pl.pallas_call(
        body, name=name,
        out_shape=(pltpu.HBM(packed.shape, BF16), *[pltpu.HBM(t.shape, BF16) for t in lands]),
        in_specs=(HBM_SPEC,) * (1 + n) + (SEM_SPEC, SEM_SPEC, ANY_SPEC),
        out_specs=(HBM_SPEC,) * (1 + n),
        input_output_aliases={i: i for i in range(1 + n)},
        compiler_params=pltpu.CompilerParams(has_side_effects=EFFECT),
    )(packed, *lands, send_sems, recv_sems, after)
    return outs[0], list(outs[1:])


def _gather_finish(packed, rows_list, lands, name):
    n = len(rows_list)
    offs = _offsets(rows_list)

    def body(packed_ref, *refs):
        land = refs[n:2 * n]
        send_sems, recv_sems, stage, stage_sem = refs[2 * n:]
        px, py, pc = _mesh_pos()
        me = 4 * px + 2 * py + pc
        sibling = (px, py, 1 - pc)
        load = pltpu.make_async_copy(packed_ref, stage, stage_sem)
        load.start()
        load.wait()
        for off, rows, land_ref in zip(offs, rows_list, land):
            pltpu.make_async_copy(stage.at[pl.ds(off, rows), :], land_ref.at[pl.ds(me * rows, rows), :],
                                  stage_sem).start()
        for j, (cx, cy) in enumerate([(1 - px, py), (px, 1 - py), (1 - px, 1 - py)]):
            block = 4 * cx + 2 * cy + pc
            for rows, land_ref in zip(rows_list, land):
                blk = land_ref.at[pl.ds(block * rows, rows), :]
                pltpu.make_async_remote_copy(src_ref=blk, dst_ref=blk, send_sem=send_sems.at[j],
                                             recv_sem=recv_sems.at[j], device_id=sibling,
                                             device_id_type=pl.DeviceIdType.MESH).start()
        for j in range(3):
            cp = _whole_wait(packed_ref, send_sems.at[j], recv_sems.at[j], sibling)
            cp.wait_recv()
            cp.wait_send()
        pltpu.make_async_copy(stage, packed_ref, stage_sem).wait()

    outs = pl.pallas_call(
        body, name=name,
        out_shape=tuple(jax.ShapeDtypeStruct(t.shape, BF16) for t in lands),
        in_specs=(HBM_SPEC,) * (1 + n), out_specs=(HBM_SPEC,) * n,
        input_output_aliases={1 + i: i for i in range(n)},
        scratch_shapes=[pltpu.SemaphoreType.DMA((3,)), pltpu.SemaphoreType.DMA((3,)),
                        pltpu.VMEM(packed.shape, BF16), pltpu.SemaphoreType.DMA],
    )(packed, *lands)
    return list(outs)


N_CHIPS = 4


def _pair_start(srcs, rows_list, after, name):
    n = len(rows_list)
    offs = _offsets(rows_list)
    land = lax.empty((N_CHIPS, sum(rows_list), D_MODEL), BF16)

    def body(*refs):
        src, land_ref = refs[:n], refs[n]
        send_sems, recv_sems = refs[n + 2], refs[n + 3]
        token = refs[-1]
        px, py, pc = _mesh_pos()
        for k in range(N_CHIPS):
            block = 2 * k + (1 - pc)
            for off, rows, src_ref in zip(offs, rows_list, src):
                pltpu.make_async_remote_copy(
                    src_ref=src_ref.at[pl.ds(block * rows, rows), :], dst_ref=land_ref.at[k, pl.ds(off, rows), :],
                    send_sem=send_sems.at[0], recv_sem=recv_sems.at[0],
                    device_id=(px, py, 1 - pc), device_id_type=pl.DeviceIdType.MESH).start()
        token[...] = jnp.zeros_like(token)

    outs = pl.pallas_call(
        body, name=name,
        out_shape=(pltpu.SemaphoreType.DMA((1,)), pltpu.SemaphoreType.DMA((1,)),
                   *[pltpu.HBM(t.shape, BF16) for t in srcs], pltpu.HBM(land.shape, BF16),
                   jax.ShapeDtypeStruct((8, LANE), F32)),
        in_specs=(HBM_SPEC,) * (n + 1) + (ANY_SPEC,),
        out_specs=(SEM_SPEC, SEM_SPEC) + (HBM_SPEC,) * (n + 1) + (pl.BlockSpec(memory_space=pltpu.VMEM),),
        input_output_aliases={i: 2 + i for i in range(n + 1)},
        compiler_params=pltpu.CompilerParams(has_side_effects=EFFECT),
    )(*[_hbm(t) for t in srcs], _hbm(land), after)
    return outs[0], outs[1], list(outs[2:2 + n]), outs[2 + n], outs[-1]


def _split_wait(send_sems, recv_sems, n_sems, srcs, land, after, name):
    n = len(srcs)

    def body(*refs):
        land_ref = refs[n]
        s_sems, r_sems = refs[n + 1], refs[n + 2]
        me = _mesh_pos()
        for k in range(n_sems):
            cp = _whole_wait(land_ref.at[0] if n_sems > 1 else land_ref, s_sems.at[k], r_sems.at[k], me)
            cp.wait_send()
            cp.wait_recv()

    outs = pl.pallas_call(
        body, name=name,
        out_shape=(*[pltpu.HBM(t.shape, t.dtype) for t in srcs], pltpu.HBM(land.shape, land.dtype)),
        in_specs=(HBM_SPEC,) * (n + 1) + (SEM_SPEC, SEM_SPEC, ANY_SPEC),
        out_specs=(HBM_SPEC,) * (n + 1),
        input_output_aliases={i: i for i in range(n + 1)},
        compiler_params=pltpu.CompilerParams(has_side_effects=EFFECT),
    )(*srcs, land, send_sems, recv_sems, after)
    return list(outs[:n]), outs[n]


def _spread_start(x, me_id, after, name):
    land = lax.dynamic_update_slice_in_dim(lax.empty((N_DEV,) + x.shape, x.dtype), x[None], me_id, axis=0)

    def body(x_ref, land_ref, after_ref, send_sems, recv_sems, x_thru, land_thru, token):
        px, py, pc = _mesh_pos()
        me = 4 * px + 2 * py + pc
        for k in range(1, N_DEV):
            qx = 1 - px if k & 4 else px
            qy = 1 - py if k & 2 else py
            qc = 1 - pc if k & 1 else pc
            pltpu.make_async_remote_copy(
                src_ref=x_ref, dst_ref=land_ref.at[me], send_sem=send_sems.at[k - 1], recv_sem=recv_sems.at[k - 1],
                device_id=(qx, qy, qc), device_id_type=pl.DeviceIdType.MESH).start()
        token[...] = jnp.zeros_like(token)

    outs = pl.pallas_call(
        body, name=name,
        out_shape=(pltpu.SemaphoreType.DMA((N_DEV - 1,)), pltpu.SemaphoreType.DMA((N_DEV - 1,)),
                   pltpu.HBM(x.shape, x.dtype), pltpu.HBM(land.shape, land.dtype), jax.ShapeDtypeStruct((8, LANE), F32)),
        in_specs=(HBM_SPEC, HBM_SPEC, ANY_SPEC),
        out_specs=(SEM_SPEC, SEM_SPEC, HBM_SPEC, HBM_SPEC, pl.BlockSpec(memory_space=pltpu.VMEM)),
        input_output_aliases={0: 2, 1: 3},
        compiler_params=pltpu.CompilerParams(has_side_effects=EFFECT),
    )(_hbm(x), _hbm(land), after)
    return outs[0], outs[1], [outs[2]], outs[3], outs[4]


def _pair_sum(srcs, rows_list, land, core, name):
    n = len(rows_list)
    offs = _offsets(rows_list)
    total = sum(rows_list)

    def body(core_ref, *refs):
        src, land_ref, out_ref = refs[:n], refs[n], refs[n + 1]
        for off, rows, src_ref in zip(offs, rows_list, src):
            out_ref[pl.ds(off, rows), :] = (src_ref[...].astype(F32)
                                            + land_ref[pl.ds(off, rows), :].astype(F32)).astype(BF16)

    slot = pl.BlockSpec((None, total, D_MODEL), lambda k, c: (k, 0, 0))
    grid_spec = pltpu.PrefetchScalarGridSpec(
        num_scalar_prefetch=1, grid=(N_CHIPS,),
        in_specs=[pl.BlockSpec((rows, D_MODEL), lambda k, c: (2 * k + c[0], 0)) for rows in rows_list] + [slot],
        out_specs=slot)
    return pl.pallas_call(
        body, name=name, grid_spec=grid_spec,
        out_shape=jax.ShapeDtypeStruct((N_CHIPS, total, D_MODEL), BF16),
        compiler_params=_params(("parallel",)),
    )(core, *srcs, land)


def _chip_exchange_start(sums, chip, after, name):
    own = lax.dynamic_index_in_dim(sums, chip, axis=0, keepdims=True)
    recv = lax.dynamic_update_slice_in_dim(lax.empty(sums.shape, BF16), own, chip, axis=0)

    def body(sums_ref, recv_ref, after_ref, send_sems, recv_sems, sums_thru, recv_thru, token):
        px, py, pc = _mesh_pos()
        for k in range(1, N_CHIPS):
            qx = 1 - px if k & 2 else px
            qy = 1 - py if k & 1 else py
            pltpu.make_async_remote_copy(
                src_ref=sums_ref.at[2 * qx + qy], dst_ref=recv_ref.at[2 * px + py],
                send_sem=send_sems.at[k - 1], recv_sem=recv_sems.at[k - 1],
                device_id=(qx, qy, pc), device_id_type=pl.DeviceIdType.MESH).start()
        token[...] = jnp.zeros_like(token)

    outs = pl.pallas_call(
        body, name=name,
        out_shape=(pltpu.SemaphoreType.DMA((N_CHIPS - 1,)), pltpu.SemaphoreType.DMA((N_CHIPS - 1,)),
                   pltpu.HBM(sums.shape, BF16), pltpu.HBM(recv.shape, BF16), jax.ShapeDtypeStruct((8, LANE), F32)),
        in_specs=(HBM_SPEC, HBM_SPEC, ANY_SPEC),
        out_specs=(SEM_SPEC, SEM_SPEC, HBM_SPEC, HBM_SPEC, pl.BlockSpec(memory_space=pltpu.VMEM)),
        input_output_aliases={0: 2, 1: 3},
        compiler_params=pltpu.CompilerParams(has_side_effects=EFFECT),
    )(_hbm(sums), _hbm(recv), after)
    return outs[0], outs[1], [outs[2]], outs[3], outs[4]


def _sum_slots_into(recv, buf, layer, row_off, name):
    slots, r, n = recv.shape
    tr = _row_tile(math.gcd(r, row_off) if row_off else r, 512)
    first = row_off // tr

    def body(in_ref, buf_ref, out_ref):
        acc = in_ref[0].astype(F32)
        for j in range(1, slots):
            acc = acc + in_ref[j].astype(F32)
        out_ref[...] = acc

    return pl.pallas_call(
        body, name=name, grid=(r // tr,), out_shape=jax.ShapeDtypeStruct(buf.shape, F32),
        in_specs=[pl.BlockSpec((slots, tr, n), lambda i: (0, i, 0)), ANY_SPEC],
        out_specs=pl.BlockSpec((None, tr, n), lambda i: (layer, first + i, 0)),
        input_output_aliases={1: 0},
        compiler_params=_params(("parallel",)),
    )(recv, buf)


def _sum_slots(recv, name, after=None):
    _, r, n = recv.shape
    tr = _row_tile(r, 512)

    def body(in_ref, *refs):
        acc = in_ref[0].astype(F32)
        for j in range(1, N_DEV):
            acc = acc + in_ref[j].astype(F32)
        refs[-1][...] = acc

    grid = (r // tr,)
    in_specs, out_spec = [pl.BlockSpec((N_DEV, tr, n), lambda i: (0, i, 0))], pl.BlockSpec((tr, n), lambda i: (i, 0))
    args = [recv]
    if after is not None:
        in_specs.append(ANY_SPEC)
        args.append(after)
    return pl.pallas_call(
        body, name=name, grid=grid,
        out_shape=jax.ShapeDtypeStruct((r, n), F32),
        in_specs=in_specs, out_specs=out_spec,
        compiler_params=_params(("parallel",)),
    )(*args)


def _row_tile(rows, target):
    if rows <= target:
        return rows
    best = None
    for t in range(16, target + 1, 16):
        if rows % t == 0:
            best = t
    assert best is not None, rows
    return best


_DIMS = {"nn": ((1,), (0,)), "nt": ((1,), (1,)), "tn": ((0,), (0,))}


def _mm(a, b, mode, name, out_dtype=F32, res=None, gate=None, gate_factor=1.0, tm=512, tn=1408, after=None):
    assert (res is None) == (gate is None)
    if mode == "tn":
        kdim, m = a.shape
    else:
        m, kdim = a.shape
    n = b.shape[0] if mode == "nt" else b.shape[1]
    tm, tn = _tile(m, tm), _tile(n, tn)
    a_spec = (pl.BlockSpec((kdim, tm), lambda i, j: (0, i)) if mode == "tn"
              else pl.BlockSpec((tm, kdim), lambda i, j: (i, 0)))
    b_spec = (pl.BlockSpec((tn, kdim), lambda i, j: (j, 0)) if mode == "nt"
              else pl.BlockSpec((kdim, tn), lambda i, j: (0, j)))
    o_spec = pl.BlockSpec((tm, tn), lambda i, j: (i, j))
    dims = (_DIMS[mode], ((), ()))
    has_res = res is not None

    def body(a_ref, b_ref, *refs):
        y = lax.dot_general(a_ref[...].astype(BF16), b_ref[...].astype(BF16), dims,
                            preferred_element_type=F32)
        if has_res:
            res_ref, gate_ref = refs[0], refs[1]
            y_ref, o_ref = refs[-2], refs[-1]
            y_ref[...] = y.astype(BF16)
            o_ref[...] = res_ref[...] + (gate_factor * gate_ref[...]) * y
        else:
            refs[-1][...] = y.astype(out_dtype)

    in_specs, args = [a_spec, b_spec], [a, b]
    if has_res:
        in_specs += [o_spec, pl.BlockSpec((1, tn), lambda i, j: (0, j))]
        args += [res, gate]
        out_shape = (jax.ShapeDtypeStruct((m, n), BF16), jax.ShapeDtypeStruct((m, n), F32))
        out_specs = (o_spec, o_spec)
    else:
        out_shape, out_specs = jax.ShapeDtypeStruct((m, n), out_dtype), o_spec
    if after is not None:
        in_specs.append(ANY_SPEC)
        args.append(after)
    return pl.pallas_call(
        body, name=name, grid=(m // tm, n // tn), out_shape=out_shape,
        in_specs=in_specs, out_specs=out_specs,
        compiler_params=_params(("parallel", "parallel")),
    )(*args)


def _vec_spec(width):
    return pl.BlockSpec((1, width), lambda i: (0, 0))


def _rm_bwd(dh, x, dres, gw, scale, name, below=None):
    s, d = x.shape
    ts = _tile(s, 256)
    factor = None if below is None else below[2]

    def body(dh_ref, x_ref, dres_ref, gw_ref, sc_ref, *refs):
        dx_ref, dsh_ref, dsc_ref, dgw_ref = refs[-6:-2] if below is not None else refs[-4:]

        @pl.when(pl.program_id(0) == 0)
        def _():
            dsh_ref[...] = jnp.zeros_like(dsh_ref)
            dsc_ref[...] = jnp.zeros_like(dsc_ref)
            dgw_ref[...] = jnp.zeros_like(dgw_ref)
            if below is not None:
                refs[-1][...] = jnp.zeros_like(refs[-1])

        xv, dhv, gwv = x_ref[...], dh_ref[...], gw_ref[...]
        r = lax.rsqrt(jnp.mean(xv * xv, axis=-1, keepdims=True) + EPS)
        xn = xv * r
        y = xn * gwv
        dsh_ref[...] += jnp.sum(dhv, axis=0, keepdims=True)
        dsc_ref[...] += jnp.sum(dhv * y, axis=0, keepdims=True)
        dy = dhv * (1 + sc_ref[...])
        dgw_ref[...] += jnp.sum(dy * xn, axis=0, keepdims=True)
        dxn = dy * gwv
        dx = dres_ref[...] + r * (dxn - xn * jnp.mean(dxn * xn, axis=-1, keepdims=True))
        dx_ref[...] = dx
        if below is not None:
            yb_ref, gb_ref, dyb_ref, dgb_ref = refs[0], refs[1], refs[-2], refs[-1]
            dyb_ref[...] = ((factor * gb_ref[...]) * dx).astype(BF16)
            dgb_ref[...] += jnp.sum((factor * dx) * yb_ref[...].astype(F32), axis=0, keepdims=True)

    row = pl.BlockSpec((ts, d), lambda i: (i, 0))
    vec = jax.ShapeDtypeStruct((1, d), F32)
    in_specs, args = [row, row, row, _vec_spec(d), _vec_spec(d)], [dh, x, dres, gw, scale]
    out_shape = [jax.ShapeDtypeStruct((s, d), F32), vec, vec, vec]
    out_specs = [row, _vec_spec(d), _vec_spec(d), _vec_spec(d)]
    if below is not None:
        in_specs += [row, _vec_spec(d)]
        args += [below[0], below[1]]
        out_shape += [jax.ShapeDtypeStruct((s, d), BF16), vec]
        out_specs += [row, _vec_spec(d)]
    return pl.pallas_call(
        body, name=name, grid=(s // ts,), out_shape=tuple(out_shape),
        in_specs=in_specs, out_specs=tuple(out_specs),
        compiler_params=_params(("arbitrary",)),
    )(*args)


def _gate_bwd(dx, y, gate, factor, name):
    s, d = dx.shape
    ts = _tile(s, 256)

    def body(dx_ref, y_ref, g_ref, dy_ref, dg_ref):
        @pl.when(pl.program_id(0) == 0)
        def _():
            dg_ref[...] = jnp.zeros_like(dg_ref)

        dxv = dx_ref[...]
        dy_ref[...] = ((factor * g_ref[...]) * dxv).astype(BF16)
        dg_ref[...] += jnp.sum((factor * dxv) * y_ref[...].astype(F32), axis=0, keepdims=True)

    row = pl.BlockSpec((ts, d), lambda i: (i, 0))
    return pl.pallas_call(
        body, name=name, grid=(s // ts,),
        out_shape=(jax.ShapeDtypeStruct((s, d), BF16), jax.ShapeDtypeStruct((1, d), F32)),
        in_specs=[row, row, _vec_spec(d)], out_specs=(row, _vec_spec(d)),
        compiler_params=_params(("arbitrary",)),
    )(dx, y, gate)


def _norm_mm(x, gw, shift, scale, w, name, tm=1024):
    s, d = x.shape
    n = w.shape[0]
    tm = _tile(s, tm)

    def body(x_ref, gw_ref, sh_ref, sc_ref, w_ref, h_ref, z_ref):
        xv = x_ref[...]
        r = lax.rsqrt(jnp.mean(xv * xv, axis=-1, keepdims=True) + EPS)
        hb = (((xv * r) * gw_ref[...]) * (1 + sc_ref[...]) + sh_ref[...]).astype(BF16)
        h_ref[...] = hb
        z_ref[...] = lax.dot_general(hb, w_ref[...], (((1,), (1,)), ((), ())), preferred_element_type=F32)

    row = pl.BlockSpec((tm, d), lambda i: (i, 0))
    return pl.pallas_call(
        body, name=name, grid=(s // tm,),
        out_shape=(jax.ShapeDtypeStruct((s, d), BF16), jax.ShapeDtypeStruct((s, n), F32)),
        in_specs=[row, _vec_spec(d), _vec_spec(d), _vec_spec(d), pl.BlockSpec((n, d), lambda i: (0, 0))],
        out_specs=(row, pl.BlockSpec((tm, n), lambda i: (i, 0))),
        compiler_params=_params(("parallel",)),
    )(x, gw, shift, scale, w)


FFN_TM, FFN_TF = 2048, 256


def _ffn_up(x, gw, shift, scale, wg, wu, name):
    s, d = x.shape
    f = wg.shape[0]
    tm, tf = _tile(s, FFN_TM), _tile(f, FFN_TF)
    nt = (((1,), (1,)), ((), ()))

    def body(x_ref, gw_ref, sh_ref, sc_ref, wg_ref, wu_ref, h_ref, a_ref, b_ref, t_ref):
        @pl.when(pl.program_id(1) == 0)
        def _():
            xv = x_ref[...]
            r = lax.rsqrt(jnp.mean(xv * xv, axis=-1, keepdims=True) + EPS)
            h_ref[...] = (((xv * r) * gw_ref[...]) * (1 + sc_ref[...]) + sh_ref[...]).astype(BF16)

        hb = h_ref[...]
        av = lax.dot_general(hb, wg_ref[...], nt, preferred_element_type=F32)
        bv = lax.dot_general(hb, wu_ref[...], nt, preferred_element_type=F32)
        a_ref[...] = av.astype(BF16)
        b_ref[...] = bv.astype(BF16)
        t_ref[...] = ((av * jax.nn.sigmoid(av)) * bv).astype(BF16)

    row = pl.BlockSpec((tm, d), lambda i, j: (i, 0))
    vec = pl.BlockSpec((1, d), lambda i, j: (0, 0))
    wblk = pl.BlockSpec((tf, d), lambda i, j: (j, 0))
    blk = pl.BlockSpec((tm, tf), lambda i, j: (i, j))
    wide = jax.ShapeDtypeStruct((s, f), BF16)
    return pl.pallas_call(
        body, name=name, grid=(s // tm, f // tf),
        out_shape=(jax.ShapeDtypeStruct((s, d), BF16), wide, wide, wide),
        in_specs=[row, vec, vec, vec, wblk, wblk], out_specs=(row, blk, blk, blk),
        compiler_params=_params(("parallel", "arbitrary")),
    )(x, gw, shift, scale, wg, wu)


def _ffn_bwd_cols(dy, h, a, b, t, wd, name, after=None):
    s, d = dy.shape
    f = wd.shape[0]
    tf = _tile(f, FFN_TF)
    nt = (((1,), (1,)), ((), ()))
    tn = (((0,), (0,)), ((), ()))

    def body(dy_ref, h_ref, a_ref, b_ref, t_ref, wd_ref, *refs):
        da_ref, db_ref, gd_ref, gg_ref, gu_ref = refs[-5:]
        dyb, hb = dy_ref[...], h_ref[...]
        dtv = lax.dot_general(dyb, wd_ref[...], nt, preferred_element_type=F32)
        av, bv = a_ref[...].astype(F32), b_ref[...].astype(F32)
        sg = jax.nn.sigmoid(av)
        dbv = (dtv * (av * sg)).astype(BF16)
        dav = ((dtv * bv) * (sg * (1 + av * (1 - sg)))).astype(BF16)
        da_ref[...] = dav
        db_ref[...] = dbv
        gd_ref[...] = lax.dot_general(t_ref[...], dyb, tn, preferred_element_type=F32).astype(BF16)
        gg_ref[...] = lax.dot_general(dav, hb, tn, preferred_element_type=F32).astype(BF16)
        gu_ref[...] = lax.dot_general(dbv, hb, tn, preferred_element_type=F32).astype(BF16)

    whole = pl.BlockSpec((s, d), lambda j: (0, 0))
    col = pl.BlockSpec((s, tf), lambda j: (0, j))
    wblk = pl.BlockSpec((tf, d), lambda j: (j, 0))
    wide, wgrad = jax.ShapeDtypeStruct((s, f), BF16), jax.ShapeDtypeStruct((f, d), BF16)
    in_specs, args = [whole, whole, col, col, col, wblk], [dy, h, a, b, t, wd]
    if after is not None:
        in_specs.append(ANY_SPEC)
        args.append(after)
    return pl.pallas_call(
        body, name=name, grid=(f // tf,), out_shape=(wide, wide, wgrad, wgrad, wgrad),
        in_specs=in_specs, out_specs=(col, col, wblk, wblk, wblk),
        compiler_params=_params(("parallel",)),
    )(*args)


def _mm_pair(a1, b1, a2, b2, name, tm=1024, tn=512, tk=1408, after=None):
    m, kdim = a1.shape
    n = b1.shape[1]
    tm, tn = _tile(m, tm), _tile(n, tn)
    tk = tk if kdim % tk == 0 and tk % LANE == 0 else kdim

    def body(a1_ref, b1_ref, a2_ref, b2_ref, *refs):
        o_ref = refs[-1]
        part = (jnp.dot(a1_ref[...], b1_ref[...], preferred_element_type=F32)
                + jnp.dot(a2_ref[...], b2_ref[...], preferred_element_type=F32))

        @pl.when(pl.program_id(2) == 0)
        def _():
            o_ref[...] = part

        @pl.when(pl.program_id(2) > 0)
        def _():
            o_ref[...] += part

    a_spec = pl.BlockSpec((tm, tk), lambda i, j, k: (i, k))
    b_spec = pl.BlockSpec((tk, tn), lambda i, j, k: (k, j))
    in_specs, args = [a_spec, b_spec, a_spec, b_spec], [a1, b1, a2, b2]
    if after is not None:
        in_specs.append(ANY_SPEC)
        args.append(after)
    return pl.pallas_call(
        body, name=name, grid=(m // tm, n // tn, kdim // tk), out_shape=jax.ShapeDtypeStruct((m, n), F32),
        in_specs=in_specs, out_specs=pl.BlockSpec((tm, tn), lambda i, j, k: (i, j)),
        compiler_params=_params(("parallel", "parallel", "arbitrary")),
    )(*args)


def _pool_counts(s):
    return (lax.broadcasted_iota(jnp.int32, (s, POOL_GC), 0))


def _pool_fwd(z, pool_w, pool_scale, name):
    s = z.shape[0]

    def body(u_ref, w_ref, sc_ref, y_ref, diff_ref):
        t = lax.broadcasted_iota(jnp.int32, (s, POOL_GC), 0)
        for g, win in enumerate(POOL_WINDOWS):
            cols = slice(g * POOL_GC, (g + 1) * POOL_GC)
            u = u_ref[:, cols]
            acc, step = u, 1
            while step < win:
                acc = acc + jnp.where(t >= step, pltpu.roll(acc, step, 0), 0.0)
                step *= 2
            cnt = jnp.minimum(t + 1, win).astype(F32)
            diff = acc / cnt - u
            diff_ref[:, cols] = diff
            ypre = jnp.dot(diff.astype(BF16), w_ref[g].astype(BF16), preferred_element_type=F32)
            y_ref[:, cols] = (ypre * sc_ref[:, cols]).astype(BF16)

    return pl.pallas_call(
        body, name=name, grid=(1,),
        out_shape=(jax.ShapeDtypeStruct((s, POOL_WIDTH), BF16), jax.ShapeDtypeStruct((s, POOL_WIDTH), F32)),
        in_specs=[pl.BlockSpec((s, POOL_WIDTH), lambda i: (0, 0)),
                  pl.BlockSpec(pool_w.shape, lambda i: (0, 0, 0)),
                  pl.BlockSpec((1, POOL_WIDTH), lambda i: (0, 0))],
        out_specs=(pl.BlockSpec((s, POOL_WIDTH), lambda i: (0, 0)),
                   pl.BlockSpec((s, POOL_WIDTH), lambda i: (0, 0))),
        compiler_params=_params(("arbitrary",)),
    )(z, pool_w, pool_scale)


def _pool_bwd(dycat, diff, pool_w, pool_scale, name):
    s = diff.shape[0]

    def body(dy_ref, diff_ref, w_ref, sc_ref, du_ref, dw_ref, dsc_ref):
        t = lax.broadcasted_iota(jnp.int32, (s, POOL_GC), 0)
        for g, win in enumerate(POOL_WINDOWS):
            cols = slice(g * POOL_GC, (g + 1) * POOL_GC)
            dy, dfb, wb = dy_ref[:, cols], diff_ref[:, cols].astype(BF16), w_ref[g].astype(BF16)
            ypre = jnp.dot(dfb, wb, preferred_element_type=F32)
            dsc_ref[:, cols] = jnp.sum(dy * ypre, axis=0, keepdims=True)
            dypre = (dy * sc_ref[:, cols]).astype(BF16)
            ddiff = lax.dot_general(dypre, wb, (((1,), (1,)), ((), ())), preferred_element_type=F32)
            dw_ref[g] = lax.dot_general(dfb, dypre, (((0,), (0,)), ((), ())), preferred_element_type=F32)
            cnt = jnp.minimum(t + 1, win).astype(F32)
            acc, step = ddiff / cnt, 1
            while step < win:
                acc = acc + jnp.where(t < s - step, pltpu.roll(acc, s - step, 0), 0.0)
                step *= 2
            du_ref[:, cols] = acc - ddiff

    full = pl.BlockSpec((s, POOL_WIDTH), lambda i: (0, 0))
    return pl.pallas_call(
        body, name=name, grid=(1,),
        out_shape=(jax.ShapeDtypeStruct((s, POOL_WIDTH), F32),
                   jax.ShapeDtypeStruct(pool_w.shape, F32),
                   jax.ShapeDtypeStruct((1, POOL_WIDTH), F32)),
        in_specs=[full, full, pl.BlockSpec(pool_w.shape, lambda i: (0, 0, 0)),
                  pl.BlockSpec((1, POOL_WIDTH), lambda i: (0, 0))],
        out_specs=(full, pl.BlockSpec(pool_w.shape, lambda i: (0, 0, 0)),
                   pl.BlockSpec((1, POOL_WIDTH), lambda i: (0, 0))),
        compiler_params=_params(("arbitrary",)),
    )(dycat, diff, pool_w, pool_scale)


def _rope_tables(positions, name):
    s = positions.shape[0]
    ts = _tile(s, 512)
    freq = 1.0 / (ROPE_THETA ** (np.arange(0, QK_ROPE, 2, dtype=np.float32) / QK_ROPE))
    table = np.zeros((1, LANE), np.float32)
    table[0, :QK_ROPE // 2] = freq
    table[0, QK_ROPE // 2:QK_ROPE] = freq

    def body(pos_ref, f_ref, cos_ref, sin_ref):
        ang = pos_ref[...].astype(F32) * f_ref[...]
        cos_ref[...] = jnp.cos(ang)
        sin_ref[...] = jnp.sin(ang)

    out = jax.ShapeDtypeStruct((s, LANE), F32)
    blk = pl.BlockSpec((ts, LANE), lambda i: (i, 0))
    return pl.pallas_call(
        body, name=name, grid=(s // ts,), out_shape=(out, out),
        in_specs=[pl.BlockSpec((ts, 1), lambda i: (i, 0)), _vec_spec(LANE)], out_specs=(blk, blk),
        compiler_params=_params(("parallel",)),
    )(positions, jnp.asarray(table))


def _lane_mod64_low(shape):
    return (lax.broadcasted_iota(jnp.int32, shape, 1) % QK_ROPE) < (QK_ROPE // 2)


def _rope(x, cos, sin):
    rot = jnp.where(_lane_mod64_low(x.shape), -pltpu.roll(x, LANE - 32, 1), pltpu.roll(x, 32, 1))
    return x * cos + rot * sin


def _rope_t(dy, cos, sin):
    w = dy * sin
    rot_t = jnp.where(_lane_mod64_low(dy.shape), pltpu.roll(w, LANE - 32, 1), -pltpu.roll(w, 32, 1))
    return dy * cos + rot_t


def _plain_rms(x, g):
    r = lax.rsqrt(jnp.mean(x * x, axis=-1, keepdims=True) + EPS)
    return (x * r) * g, x * r, r


O_Q, O_KV, O_KR = POOL_WIDTH, POOL_WIDTH + Q_LORA, POOL_WIDTH + Q_LORA + KV_LORA


def _qkv_fwd(z, qn, kvn, wq, wkv, cos, sin, name):
    s = z.shape[0]
    ts = _tile(s, 256)

    def body(z_ref, qn_ref, kvn_ref, wq_ref, wkv_ref, cos_ref, sin_ref, q_ref, k_ref, v_ref, cqn_ref, ckvn_ref):
        cosv, sinv = cos_ref[...], sin_ref[...]
        cqn = _plain_rms(z_ref[:, O_Q:O_KV], qn_ref[...])[0].astype(BF16)
        ckvn = _plain_rms(z_ref[:, O_KV:O_KR], kvn_ref[...])[0].astype(BF16)
        cqn_ref[...] = cqn
        ckvn_ref[...] = ckvn
        nt = (((1,), (1,)), ((), ()))
        q = lax.dot_general(cqn, wq_ref[...], nt, preferred_element_type=F32)
        kv = lax.dot_general(ckvn, wkv_ref[...], nt, preferred_element_type=F32)
        kr = _rope(z_ref[:, O_KR:IN_PAD], cosv, sinv).astype(BF16)
        for h in range(N_HEADS):
            o = h * HEAD_PAD
            q_ref[:, o:o + QK_NOPE] = q[:, o:o + QK_NOPE].astype(BF16)
            q_ref[:, o + QK_NOPE:o + HEAD_PAD] = _rope(q[:, o + QK_NOPE:o + HEAD_PAD], cosv, sinv).astype(BF16)
            k_ref[:, o:o + QK_NOPE] = kv[:, o:o + QK_NOPE].astype(BF16)
            k_ref[:, o + QK_NOPE:o + HEAD_PAD] = kr
            v_ref[:, h * V_HEAD:(h + 1) * V_HEAD] = kv[:, o + QK_NOPE:o + HEAD_PAD].astype(BF16)

    def row(w):
        return pl.BlockSpec((ts, w), lambda i: (i, 0))

    def whole(arr):
        return pl.BlockSpec(arr.shape, lambda i: (0, 0))

    hp = N_HEADS * HEAD_PAD
    return pl.pallas_call(
        body, name=name, grid=(s // ts,),
        out_shape=(jax.ShapeDtypeStruct((s, hp), BF16), jax.ShapeDtypeStruct((s, hp), BF16),
                   jax.ShapeDtypeStruct((s, N_HEADS * V_HEAD), BF16),
                   jax.ShapeDtypeStruct((s, Q_LORA), BF16), jax.ShapeDtypeStruct((s, KV_LORA), BF16)),
        in_specs=[row(IN_PAD), whole(qn), whole(kvn), whole(wq), whole(wkv), row(LANE), row(LANE)],
        out_specs=(row(hp), row(hp), row(N_HEADS * V_HEAD), row(Q_LORA), row(KV_LORA)),
        compiler_params=_params(("parallel",)),
    )(z, qn, kvn, wq, wkv, cos, sin)


def _qkv_bwd(dq, dk, dv, du, z, qn, kvn, wq, wkv, cos, sin, name):
    s = z.shape[0]
    ts = _tile(s, 256)

    def norm_bwd(x, g, dy):
        _, xn, r = _plain_rms(x, g)
        dxn = dy * g
        return r * (dxn - xn * jnp.mean(dxn * xn, axis=-1, keepdims=True)), jnp.sum(dy * xn, axis=0, keepdims=True)

    def body(dq_ref, dk_ref, dv_ref, du_ref, z_ref, qn_ref, kvn_ref, wq_ref, wkv_ref, cos_ref, sin_ref,
             dz_ref, dqb_ref, dkvb_ref, dqn_ref, dkvn_ref):
        @pl.when(pl.program_id(0) == 0)
        def _():
            dqn_ref[...] = jnp.zeros_like(dqn_ref)
            dkvn_ref[...] = jnp.zeros_like(dkvn_ref)

        cosv, sinv = cos_ref[...], sin_ref[...]
        dkr = jnp.zeros((ts, LANE), F32)
        for h in range(N_HEADS):
            o = h * HEAD_PAD
            dqb_ref[:, o:o + QK_NOPE] = dq_ref[:, o:o + QK_NOPE].astype(BF16)
            dqb_ref[:, o + QK_NOPE:o + HEAD_PAD] = _rope_t(dq_ref[:, o + QK_NOPE:o + HEAD_PAD], cosv, sinv).astype(BF16)
            dkvb_ref[:, o:o + QK_NOPE] = dk_ref[:, o:o + QK_NOPE].astype(BF16)
            dkvb_ref[:, o + QK_NOPE:o + HEAD_PAD] = dv_ref[:, h * V_HEAD:(h + 1) * V_HEAD].astype(BF16)
            dkr = dkr + dk_ref[:, o + QK_NOPE:o + HEAD_PAD]
        dcqn = jnp.dot(dqb_ref[...], wq_ref[...], preferred_element_type=F32)
        dckvn = jnp.dot(dkvb_ref[...], wkv_ref[...], preferred_element_type=F32)
        dcq, dqn = norm_bwd(z_ref[:, O_Q:O_KV], qn_ref[...], dcqn)
        dckv, dkvn = norm_bwd(z_ref[:, O_KV:O_KR], kvn_ref[...], dckvn)
        dqn_ref[...] += dqn
        dkvn_ref[...] += dkvn
        dz_ref[:, 0:O_Q] = du_ref[...].astype(BF16)
        dz_ref[:, O_Q:O_KV] = dcq.astype(BF16)
        dz_ref[:, O_KV:O_KR] = dckv.astype(BF16)
        dz_ref[:, O_KR:IN_PAD] = _rope_t(dkr, cosv, sinv).astype(BF16)

    def row(w):
        return pl.BlockSpec((ts, w), lambda i: (i, 0))

    def whole(arr):
        return pl.BlockSpec(arr.shape, lambda i: (0, 0))

    hp = N_HEADS * HEAD_PAD
    return pl.pallas_call(
        body, name=name, grid=(s // ts,),
        out_shape=(jax.ShapeDtypeStruct((s, IN_PAD), BF16), jax.ShapeDtypeStruct((s, hp), BF16),
                   jax.ShapeDtypeStruct((s, hp), BF16),
                   jax.ShapeDtypeStruct((1, Q_LORA), F32), jax.ShapeDtypeStruct((1, KV_LORA), F32)),
        in_specs=[row(hp), row(hp), row(N_HEADS * V_HEAD), row(POOL_WIDTH), row(IN_PAD),
                  whole(qn), whole(kvn), whole(wq), whole(wkv), row(LANE), row(LANE)],
        out_specs=(row(IN_PAD), row(hp), row(hp), whole(qn), whole(kvn)),
        compiler_params=_params(("arbitrary",)),
    )(dq, dk, dv, du, z, qn, kvn, wq, wkv, cos, sin)


def _causal_scores(q, k, i, tq, klen):
    sc = lax.dot_general(q, k, (((1,), (1,)), ((), ())), preferred_element_type=F32) * SOFTMAX_SCALE
    qpos = i * tq + lax.broadcasted_iota(jnp.int32, (tq, klen), 0)
    kpos = lax.broadcasted_iota(jnp.int32, (tq, klen), 1)
    return jnp.where(qpos >= kpos, sc, -jnp.inf)


ATTN_TQ = 512
ATTN_SEGMENTS = 4


def _by_key_prefix(i, nq, tq, compute):
    nseg = min(ATTN_SEGMENTS, nq)
    per = nq // nseg
    for r in range(nseg):
        pl.when(i // per == r)(lambda r=r: compute((r + 1) * per * tq))


def _attn_fwd(q, k, v, name):
    s = q.shape[0]
    tq = _tile(s, ATTN_TQ)
    nq = s // tq

    def body(q_ref, k_ref, v_ref, o_ref, lse_ref):
        i = pl.program_id(1)

        def compute(klen):
            sc = _causal_scores(q_ref[...], k_ref[0:klen, :], i, tq, klen)
            mx = jnp.max(sc, axis=-1, keepdims=True)
            p = jnp.exp(sc - mx)
            den = jnp.sum(p, axis=-1, keepdims=True)
            o_ref[...] = jnp.dot((p / den).astype(BF16), v_ref[0:klen, :], preferred_element_type=F32)
            lse_ref[...] = mx + jnp.log(den)

        _by_key_prefix(i, nq, tq, compute)

    return pl.pallas_call(
        body, name=name, grid=(N_HEADS, s // tq),
        out_shape=(jax.ShapeDtypeStruct((s, N_HEADS * V_HEAD), F32), jax.ShapeDtypeStruct((N_HEADS, s, 1), F32)),
        in_specs=[pl.BlockSpec((tq, HEAD_PAD), lambda h, i: (i, h)),
                  pl.BlockSpec((s, HEAD_PAD), lambda h, i: (0, h)),
                  pl.BlockSpec((s, V_HEAD), lambda h, i: (0, h))],
        out_specs=(pl.BlockSpec((tq, V_HEAD), lambda h, i: (i, h)),
                   pl.BlockSpec((None, tq, 1), lambda h, i: (h, i, 0))),
        compiler_params=_params(("parallel", "parallel")),
    )(q, k, v)


def _attn_bwd(q, k, v, lse, dycat, name):
    s = q.shape[0]
    tq = _tile(s, ATTN_TQ)
    nq = s // tq
    tn_dims = (((0,), (0,)), ((), ()))

    def body(q_ref, k_ref, v_ref, lse_ref, do_ref, dq_ref, dk_ref, dv_ref):
        i = pl.program_id(1)

        @pl.when(i == 0)
        def _():
            dk_ref[...] = jnp.zeros_like(dk_ref)
            dv_ref[...] = jnp.zeros_like(dv_ref)

        def compute(klen):
            qv, kv_, dob = q_ref[...], k_ref[0:klen, :], do_ref[...].astype(BF16)
            sc = _causal_scores(qv, kv_, i, tq, klen)
            p = jnp.exp(sc - lse_ref[...])
            dp = lax.dot_general(dob, v_ref[0:klen, :], (((1,), (1,)), ((), ())), preferred_element_type=F32)
            ds = (p * (dp - jnp.sum(dp * p, axis=-1, keepdims=True)) * SOFTMAX_SCALE).astype(BF16)
            dq_ref[...] = jnp.dot(ds, kv_, preferred_element_type=F32)
            dk_ref[0:klen, :] += lax.dot_general(ds, qv, tn_dims, preferred_element_type=F32)
            dv_ref[0:klen, :] += lax.dot_general(p.astype(BF16), dob, tn_dims, preferred_element_type=F32)

        _by_key_prefix(i, nq, tq, compute)

    n_pool_blocks = POOL_WIDTH // V_HEAD
    return pl.pallas_call(
        body, name=name, grid=(N_HEADS, s // tq),
        out_shape=(jax.ShapeDtypeStruct((s, N_HEADS * HEAD_PAD), F32),
                   jax.ShapeDtypeStruct((s, N_HEADS * HEAD_PAD), F32),
                   jax.ShapeDtypeStruct((s, N_HEADS * V_HEAD), F32)),
        in_specs=[pl.BlockSpec((tq, HEAD_PAD), lambda h, i: (i, h)),
                  pl.BlockSpec((s, HEAD_PAD), lambda h, i: (0, h)),
                  pl.BlockSpec((s, V_HEAD), lambda h, i: (0, h)),
                  pl.BlockSpec((None, tq, 1), lambda h, i: (h, i, 0)),
                  pl.BlockSpec((tq, V_HEAD), lambda h, i: (i, n_pool_blocks + h))],
        out_specs=(pl.BlockSpec((tq, HEAD_PAD), lambda h, i: (i, h)),
                   pl.BlockSpec((s, HEAD_PAD), lambda h, i: (0, h)),
                   pl.BlockSpec((s, V_HEAD), lambda h, i: (0, h))),
        compiler_params=_params(("parallel", "arbitrary")),
    )(q, k, v, lse, dycat)


def _loss_head(x, gw, target, name):
    s, d = x.shape
    ts = _tile(s, 256)

    def body(x_ref, gw_ref, tgt_ref, loss_ref, dx_ref, dgw_ref):
        @pl.when(pl.program_id(0) == 0)
        def _():
            loss_ref[...] = jnp.zeros_like(loss_ref)
            dgw_ref[...] = jnp.zeros_like(dgw_ref)

        xv, gwv = x_ref[...], gw_ref[...]
        r = lax.rsqrt(jnp.mean(xv * xv, axis=-1, keepdims=True) + EPS)
        xn = xv * r
        err = xn * gwv - tgt_ref[...]
        loss_ref[...] += 0.5 * jnp.sum(jnp.mean(err * err, axis=-1, keepdims=True))
        dy = err / d
        dgw_ref[...] += jnp.sum(dy * xn, axis=0, keepdims=True)
        dxn = dy * gwv
        dx_ref[...] = r * (dxn - xn * jnp.mean(dxn * xn, axis=-1, keepdims=True))

    row = pl.BlockSpec((ts, d), lambda i: (i, 0))
    return pl.pallas_call(
        body, name=name, grid=(s // ts,),
        out_shape=(jax.ShapeDtypeStruct((8, LANE), F32), jax.ShapeDtypeStruct((s, d), F32),
                   jax.ShapeDtypeStruct((1, d), F32)),
        in_specs=[row, _vec_spec(d), row],
        out_specs=(pl.BlockSpec((8, LANE), lambda i: (0, 0)), row, _vec_spec(d)),
        compiler_params=_params(("arbitrary",)),
    )(x, gw, target)


def _ada_mod(c_all, ada_w, ada_b, name):
    nl, d, cols = ada_w.shape

    def body(c_ref, w_ref, b_ref, o_ref):
        cv = c_ref[...]
        act = (cv * jax.nn.sigmoid(cv)).astype(BF16)
        o_ref[...] = jnp.dot(act, w_ref[...].astype(BF16), preferred_element_type=F32) + b_ref[...]

    return pl.pallas_call(
        body, name=name, grid=(nl,), out_shape=jax.ShapeDtypeStruct((nl, N_DEV, cols), F32),
        in_specs=[pl.BlockSpec((N_DEV, d), lambda l: (0, 0)),
                  pl.BlockSpec((None, d, cols), lambda l: (l, 0, 0)),
                  pl.BlockSpec((None, 1, cols), lambda l: (l, 0, 0))],
        out_specs=pl.BlockSpec((None, N_DEV, cols), lambda l: (l, 0, 0)),
        compiler_params=_params(("parallel",)),
    )(c_all, ada_w, ada_b)


def _ada_grad(c_pad, dmod_pad, name):
    nl, kpad, cols = dmod_pad.shape
    d = c_pad.shape[1]

    def body(c_ref, dm_ref, o_ref):
        cv = c_ref[...]
        act = (cv * jax.nn.sigmoid(cv)).astype(BF16)
        o_ref[...] = lax.dot_general(act, dm_ref[...].astype(BF16), (((0,), (0,)), ((), ())),
                                     preferred_element_type=F32)

    return pl.pallas_call(
        body, name=name, grid=(nl,), out_shape=jax.ShapeDtypeStruct((nl, d, cols), F32),
        in_specs=[pl.BlockSpec((kpad, d), lambda l: (0, 0)),
                  pl.BlockSpec((None, kpad, cols), lambda l: (l, 0, 0))],
        out_specs=pl.BlockSpec((None, d, cols), lambda l: (l, 0, 0)),
        compiler_params=_params(("parallel",)),
    )(c_pad, dmod_pad)


def _adamw_math(w, g, m, v):
    nm = ADAM_B1 * m + (1.0 - ADAM_B1) * g
    nv = ADAM_B2 * v + (1.0 - ADAM_B2) * (g * g)
    m_hat = nm / (1.0 - ADAM_B1 ** ADAM_STEP)
    v_hat = nv / (1.0 - ADAM_B2 ** ADAM_STEP)
    return -ADAM_LR * (m_hat / (jnp.sqrt(v_hat) + ADAM_EPS) + ADAM_WD * w), nm, nv


def _adamw_rows(w3, gbuf, row_off, m3, v3, name):
    nl, r, d = w3.shape
    tr = _row_tile(math.gcd(r, row_off) if row_off else r, 176)
    first = row_off // tr

    def body(w_ref, g_ref, m_ref, v_ref, go_ref, d_ref, nm_ref, nv_ref):
        gv = g_ref[...]
        go_ref[...] = gv
        d_ref[...], nm_ref[...], nv_ref[...] = _adamw_math(w_ref[...], gv, m_ref[...], v_ref[...])

    blk = pl.BlockSpec((None, tr, d), lambda l, i: (l, i, 0))
    gblk = pl.BlockSpec((None, tr, d), lambda l, i: (l, first + i, 0))
    out = jax.ShapeDtypeStruct((nl, r, d), F32)
    return pl.pallas_call(
        body, name=name, grid=(nl, r // tr), out_shape=(out, out, out, out),
        in_specs=[blk, gblk, blk, blk], out_specs=(blk, blk, blk, blk),
        compiler_params=_params(("parallel", "parallel")),
    )(w3, gbuf, m3, v3)


def _adamw(w, g, m, v, name):
    rows, cols = w.shape
    tr = _row_tile(rows, 512)

    def body(w_ref, g_ref, m_ref, v_ref, d_ref, nm_ref, nv_ref):
        d_ref[...], nm_ref[...], nv_ref[...] = _adamw_math(w_ref[...], g_ref[...], m_ref[...], v_ref[...])

    blk = pl.BlockSpec((tr, cols), lambda i: (i, 0))
    out = jax.ShapeDtypeStruct((rows, cols), F32)
    return pl.pallas_call(
        body, name=name, grid=(rows // tr,), out_shape=(out, out, out),
        in_specs=[blk, blk, blk, blk], out_specs=(blk, blk, blk),
        compiler_params=_params(("parallel",)),
    )(w, g, m, v)


def _adamw_nd(w, g, m, v, name):
    shape = w.shape
    flat = (lambda t: t.reshape(1, -1)) if w.ndim == 1 else (lambda t: t.reshape(-1, shape[-1]))
    return tuple(t.reshape(shape) for t in _adamw(flat(w), flat(g), flat(m), flat(v), name))


def _pad_rows(t, rows):
    return jnp.pad(t, ((0, rows - t.shape[0]), (0, 0)))


def _pack_shard_layer(l, wts):
    def tr(name):
        return wts[name][l].astype(BF16).T

    parts = [tr("ffn1_w_gate"), tr("ffn1_w_up"), wts["ffn1_w_down"][l].astype(BF16),
             tr("ffn2_w_gate"), tr("ffn2_w_up"), wts["ffn2_w_down"][l].astype(BF16),
             wts["w_out"][l].astype(BF16),
             tr("w_kv_b").reshape(KV_SH_ROWS, D_MODEL),
             _pad_rows(tr("w_in"), 160),
             _pad_rows(tr("w_q_b").reshape(Q_SH_ROWS, D_MODEL), Q_PAD_ROWS)]
    return jnp.concatenate(parts, axis=0)


def _full_weights(lands):
    w = dict(zip(("g1", "u1", "d1", "g2", "u2", "d2", "out"), lands))
    small = lands[-1].reshape(N_DEV, SMALL_ROWS, D_MODEL)
    o_in, o_q = OFF_IN - OFF_KV, OFF_Q - OFF_KV
    w["kv"] = small[:, :KV_SH_ROWS].reshape(N_HEADS * HEAD_PAD, KV_LORA)
    w["in"] = _pad_rows(small[:, o_in:o_in + IN_SH].reshape(IN_COLS, D_MODEL), IN_PAD)
    wq = small[:, o_q:o_q + Q_SH_ROWS].reshape(N_HEADS, QK_HEAD, Q_LORA)
    w["q"] = jnp.pad(wq, ((0, 0), (0, HEAD_PAD - QK_HEAD), (0, 0))).reshape(N_HEADS * HEAD_PAD, Q_LORA)
    return w


def _grad_sources_b(gr):
    gq = gr["q"].reshape(N_HEADS, HEAD_PAD, Q_LORA)[:, :QK_HEAD].reshape(N_DEV, Q_SH_ROWS, D_MODEL)
    small = jnp.concatenate([
        gr["kv"].reshape(N_DEV, KV_SH_ROWS, D_MODEL),
        jnp.pad(gr["in"][:IN_COLS].reshape(N_DEV, IN_SH, D_MODEL), ((0, 0), (0, 160 - IN_SH), (0, 0))),
        jnp.pad(gq, ((0, 0), (0, Q_PAD_ROWS - Q_SH_ROWS), (0, 0)))], axis=1)
    return [gr["g2"], gr["u2"], gr["d2"], gr["out"], small.reshape(N_DEV * SMALL_ROWS, D_MODEL)]


def _small_layout(nl):
    names = [("dmod", nl * N_MOD), ("ffn1_norm", nl), ("mix_norm", nl), ("ffn2_norm", nl), ("q_a_norm", nl),
             ("kv_a_norm", nl), ("pool_scale", nl), ("final_norm", 1), ("loss", 1),
             ("pool_w", nl * 4 * POOL_GC * POOL_GC // D_MODEL)]
    off, table = 0, {}
    for name, n in names:
        table[name] = (off, n)
        off += -(-n // 8) * 8
    return table, off


def _to_rows(t, width=D_MODEL):
    n, w = t.shape
    return jnp.pad(t, ((0, -(-n // 8) * 8 - n), (0, width - w)))


def kernel(x, c, positions, ada_w, ada_b, ffn1_norm, ffn1_w_gate, ffn1_w_up, ffn1_w_down, mix_norm, w_in, pool_w, pool_scale, q_a_norm, w_q_b, kv_a_norm, w_kv_b, w_out, ffn2_norm, ffn2_w_gate, ffn2_w_up, ffn2_w_down, final_norm, loss_target, m_ada_w, m_ada_b, m_ffn1_norm, m_ffn1_w_gate, m_ffn1_w_up, m_ffn1_w_down, m_mix_norm, m_w_in, m_pool_w, m_pool_scale, m_q_a_norm, m_w_q_b, m_kv_a_norm, m_w_kv_b, m_w_out, m_ffn2_norm, m_ffn2_w_gate, m_ffn2_w_up, m_ffn2_w_down, m_final_norm, v_ada_w, v_ada_b, v_ffn1_norm, v_ffn1_w_gate, v_ffn1_w_up, v_ffn1_w_down, v_mix_norm, v_w_in, v_pool_w, v_pool_scale, v_q_a_norm, v_w_q_b, v_kv_a_norm, v_w_kv_b, v_w_out, v_ffn2_norm, v_ffn2_w_gate, v_ffn2_w_up, v_ffn2_w_down, v_final_norm):
    wts = dict(ada_w=ada_w, ada_b=ada_b, ffn1_norm=ffn1_norm, ffn1_w_gate=ffn1_w_gate, ffn1_w_up=ffn1_w_up,
               ffn1_w_down=ffn1_w_down, mix_norm=mix_norm, w_in=w_in, pool_w=pool_w, pool_scale=pool_scale,
               q_a_norm=q_a_norm, w_q_b=w_q_b, kv_a_norm=kv_a_norm, w_kv_b=w_kv_b, w_out=w_out,
               ffn2_norm=ffn2_norm, ffn2_w_gate=ffn2_w_gate, ffn2_w_up=ffn2_w_up, ffn2_w_down=ffn2_w_down,
               final_norm=final_norm)
    mom_m = dict(ada_w=m_ada_w, ada_b=m_ada_b, ffn1_norm=m_ffn1_norm, ffn1_w_gate=m_ffn1_w_gate,
                 ffn1_w_up=m_ffn1_w_up, ffn1_w_down=m_ffn1_w_down, mix_norm=m_mix_norm, w_in=m_w_in,
                 pool_w=m_pool_w, pool_scale=m_pool_scale, q_a_norm=m_q_a_norm, w_q_b=m_w_q_b,
                 kv_a_norm=m_kv_a_norm, w_kv_b=m_w_kv_b, w_out=m_w_out, ffn2_norm=m_ffn2_norm,
                 ffn2_w_gate=m_ffn2_w_gate, ffn2_w_up=m_ffn2_w_up, ffn2_w_down=m_ffn2_w_down,
                 final_norm=m_final_norm)
    mom_v = dict(ada_w=v_ada_w, ada_b=v_ada_b, ffn1_norm=v_ffn1_norm, ffn1_w_gate=v_ffn1_w_gate,
                 ffn1_w_up=v_ffn1_w_up, ffn1_w_down=v_ffn1_w_down, mix_norm=v_mix_norm, w_in=v_w_in,
                 pool_w=v_pool_w, pool_scale=v_pool_scale, q_a_norm=v_q_a_norm, w_q_b=v_w_q_b,
                 kv_a_norm=v_kv_a_norm, w_kv_b=v_w_kv_b, w_out=v_w_out, ffn2_norm=v_ffn2_norm,
                 ffn2_w_gate=v_ffn2_w_gate, ffn2_w_up=v_ffn2_w_up, ffn2_w_down=v_ffn2_w_down,
                 final_norm=v_final_norm)
    order = list(wts)
    nl = ada_w.shape[0]
    seq = x.shape[1]
    me = 4 * lax.axis_index("x") + 2 * lax.axis_index("y") + lax.axis_index("c")
    ada_cols = ada_w.shape[2]

    def after_token(t, token):
        return t + token[0:1, 0:1].astype(t.dtype)

    packs = [_pack_shard_layer(l, wts) for l in range(nl)]

    c_all = _all_gather(jnp.broadcast_to(c, (8, D_MODEL)), "gather_c")[::8]

    ada_b_mine = lax.dynamic_slice_in_dim(ada_b, me * ada_cols, ada_cols, axis=1).reshape(nl, 1, ada_cols)
    mod_part = _ada_mod(c_all, ada_w, ada_b_mine, "ada_mod")
    mod_all = _all_gather(mod_part.reshape(nl * N_DEV, ada_cols), "gather_mod")
    mod_all = mod_all.reshape(N_DEV, nl, N_DEV, ada_cols)
    mod = lax.dynamic_index_in_dim(mod_all, me, axis=2, keepdims=False)
    mod = mod.transpose(1, 0, 2).reshape(nl, N_MOD, 1, D_MODEL)

    flight_a = _gather_start(packs[0][:SPLIT_AB], ROWS_A, mod, "gather_start_0a")
    flight_b = _gather_start(packs[0][SPLIT_AB:], ROWS_B, flight_a[4], "gather_start_0b")
    last_start = flight_b[4]
    if nl > 1:
        in_flight = _gather_start(packs[1], ROWS_ALL, last_start, "gather_start_1")
        last_start = in_flight[4]

    cos, sin = _rope_tables(after_token(positions.reshape(seq, 1), last_start), "rope_tables")

    def vec(t):
        return t.reshape(1, -1)

    def landed(flight, rows_list, after, tag):
        send_sems, recv_sems, pk, lands, _ = flight
        pk, lands = _gather_wait(send_sems, recv_sems, pk, lands, after, f"gather_wait_{tag}")
        return _gather_finish(pk, rows_list, lands, "gather_finish")

    xs = x.reshape(seq, D_MODEL)
    saved = []
    for l in range(nl):
        norm1 = vec(ffn1_norm[l])
        if l == 0:
            lands = landed(flight_a, ROWS_A, cos, "0a")
        elif l + 1 < nl:
            in_flight = _gather_start(packs[l + 1], ROWS_ALL, lands[0], f"gather_start_{l + 1}")
            norm1 = after_token(norm1, in_flight[4])
        sv = {}

        def ffn_fwd(xin, norm, k0, wg, wu, wd, tag):
            h, a, b, t = _ffn_up(xin, norm, mod[l, k0], mod[l, k0 + 1], wg, wu, "ffn_up")
            y, xout = _mm(t, wd, "nn", "ffn_down", res=xin, gate=mod[l, k0 + 2], gate_factor=0.5)
            sv[tag] = dict(x=xin, h=h, a=a, b=b, t=t, y=y)
            return xout

        xs = ffn_fwd(xs, norm1, 0, lands[0], lands[1], lands[2], "f1")
        if l == 0:
            lands = lands + landed(flight_b, ROWS_B, xs, "0b")
        w = _full_weights(lands)
        sv["w"] = w

        h2, z = _norm_mm(xs, vec(mix_norm[l]), mod[l, 3], mod[l, 4], w["in"], "mix_in")
        y_pool, diff = _pool_fwd(z, pool_w[l], vec(pool_scale[l]), "pool_fwd")
        q, k, v, cqn, ckvn = _qkv_fwd(z, vec(q_a_norm[l]), vec(kv_a_norm[l]), w["q"], w["kv"], cos, sin, "qkv_fwd")
        o, lse = _attn_fwd(q, k, v, "attn_fwd")
        ycat = jnp.concatenate([y_pool, o.astype(BF16)], axis=1)
        y2, xmix = _mm(ycat, w["out"], "nn", "mix_out", res=xs, gate=mod[l, 5], gate_factor=1.0)
        sv["mix"] = dict(x=xs, h=h2, z=z, diff=diff, q=q, k=k, v=v, cqn=cqn, ckvn=ckvn, lse=lse, ycat=ycat, y=y2)
        xs = xmix

        xs = ffn_fwd(xs, vec(ffn2_norm[l]), 6, w["g2"], w["u2"], w["d2"], "f2")
        saved.append(sv)
        if l + 1 < nl:
            lands = landed(in_flight, ROWS_ALL, xs, l + 1)

    loss_part, dx, d_final = _loss_head(xs, vec(final_norm), loss_target.reshape(seq, D_MODEL), "loss_head")

    small = {name: [None] * nl for name in ("ffn1_norm", "mix_norm", "ffn2_norm", "q_a_norm", "kv_a_norm",
                                            "pool_scale", "pool_w", "dmod")}
    core = lax.axis_index("c").astype(jnp.int32).reshape(1)
    chip = 2 * lax.axis_index("x") + lax.axis_index("y")
    exchanges = []

    def leave(srcs, rows_list, after, tag):
        return _pair_start(srcs, rows_list, after, f"pair_start_{tag}"), rows_list, tag

    def forward_on(pending, after, layer, row_off):
        (send_sems, recv_sems, srcs, land, _), rows_list, tag = pending
        srcs, land = _split_wait(send_sems, recv_sems, 1, srcs, land, after, f"pair_wait_{tag}")
        sums = _pair_sum(srcs, rows_list, land, core, "pair_sum")
        flight = _chip_exchange_start(sums, chip, after, f"exchange_start_{tag}")
        exchanges.append((flight, layer, row_off, tag))
        return flight[4]

    pending = None
    head = _gate_bwd(dx, saved[nl - 1]["f2"]["y"], mod[nl - 1, 8], 0.5, "gate_bwd")
    for l in reversed(range(nl)):
        sv = saved[l]
        w = sv["w"]
        dmod = [None] * N_MOD
        gr = {}

        def ffn_bwd(dxin, head, s_, norm, k0, wg, wu, wd, tag, below, first_after=None, mid=None):
            dy, dmod[k0 + 2] = head
            da, db, gr["d" + tag], gr["g" + tag], gr["u" + tag] = _ffn_bwd_cols(
                dy, s_["h"], s_["a"], s_["b"], s_["t"], wd, "ffn_bwd_cols", after=first_after)
            dh = _mm_pair(da, wg, db, wu, "ffn_bwd_dh", after=None if mid is None else mid(da))
            outs = _rm_bwd(dh, s_["x"], dxin, vec(norm), mod[l, k0 + 1], "rm_bwd", below=below)
            dmod[k0], dmod[k0 + 1] = outs[1], outs[2]
            return outs[0], outs[3], outs[4:]

        s_ = sv["mix"]
        dx, small["ffn2_norm"][l], head = ffn_bwd(
            dx, head, sv["f2"], ffn2_norm[l], 6, w["g2"], w["u2"], w["d2"], "2", (s_["y"], mod[l, 5], 1.0),
            first_after=None if pending is None else pending[0][4])

        mix_after = None
        if pending is not None:
            mix_after = forward_on(pending, dx, l + 1, 0)
            pending = None
        dy, dmod[5] = head
        gr["out"] = _mm(s_["ycat"], dy, "tn", "mix_out_dw", out_dtype=BF16, tm=256, after=mix_after)
        dycat = _mm(dy, w["out"], "nt", "mix_out_dx")
        du, small["pool_w"][l], small["pool_scale"][l] = _pool_bwd(dycat, s_["diff"], pool_w[l], vec(pool_scale[l]), "pool_bwd")
        dq, dk, dv = _attn_bwd(s_["q"], s_["k"], s_["v"], s_["lse"], dycat, "attn_bwd")
        dz, dqb, dkvb, small["q_a_norm"][l], small["kv_a_norm"][l] = _qkv_bwd(
            dq, dk, dv, du, s_["z"], vec(q_a_norm[l]), vec(kv_a_norm[l]), w["q"], w["kv"], cos, sin, "qkv_bwd")
        gr["q"] = _mm(dqb, s_["cqn"], "tn", "q_b_dw", out_dtype=BF16, tm=256)
        gr["kv"] = _mm(dkvb, s_["ckvn"], "tn", "kv_b_dw", out_dtype=BF16, tm=256)
        gr["in"] = _mm(dz, s_["h"], "tn", "mix_in_dw", out_dtype=BF16, tm=256)
        dh2 = _mm(dz, w["in"], "nn", "mix_in_dx")
        outs = _rm_bwd(dh2, s_["x"], dx, vec(mix_norm[l]), mod[l, 4], "rm_bwd", below=(sv["f1"]["y"], mod[l, 2], 0.5))
        dx, dmod[3], dmod[4], small["mix_norm"][l] = outs[:4]
        head = outs[4:]

        first_after, mid = None, None
        if l == 0:
            pending_b = leave(_grad_sources_b(gr), ROWS_B, dx, "0b")
            first_after = pending_b[0][4]
            mid = lambda da: forward_on(pending_b, da, 0, SPLIT_AB)
        below = (saved[l - 1]["f2"]["y"], mod[l - 1, 8], 0.5) if l > 0 else None
        dx, small["ffn1_norm"][l], head = ffn_bwd(
            dx, head, sv["f1"], ffn1_norm[l], 0, w["g1"], w["u1"], w["d1"], "1", below, first_after, mid)

        small["dmod"][l] = jnp.concatenate(dmod, axis=0)
        if l > 0:
            pending = leave([gr["g1"], gr["u1"], gr["d1"]] + _grad_sources_b(gr), ROWS_ALL, dx, l)

    grad_x = dx.reshape(x.shape)
    pending_a = leave([gr["g1"], gr["u1"], gr["d1"]], ROWS_A, dx, "0a")

    layout, small_rows = _small_layout(nl)
    pieces = {
        "dmod": jnp.concatenate(small["dmod"], axis=0),
        "ffn1_norm": jnp.concatenate(small["ffn1_norm"], axis=0),
        "mix_norm": jnp.concatenate(small["mix_norm"], axis=0),
        "ffn2_norm": jnp.concatenate(small["ffn2_norm"], axis=0),
        "q_a_norm": jnp.concatenate(small["q_a_norm"], axis=0),
        "kv_a_norm": jnp.concatenate(small["kv_a_norm"], axis=0),
        "pool_scale": jnp.concatenate(small["pool_scale"], axis=0),
        "final_norm": d_final,
        "loss": jnp.broadcast_to(loss_part[0:1, 0:1], (1, D_MODEL)),
        "pool_w": jnp.stack(small["pool_w"]).reshape(-1, D_MODEL),
    }
    small_buf = jnp.concatenate([_to_rows(pieces[name]) for name in layout], axis=0)
    def landed_sums(gbuf, entries, after):
        for (send_sems, recv_sems, sums, recv, _), layer, row_off, tag in entries:
            _, recv = _split_wait(send_sems, recv_sems, N_CHIPS - 1, sums, recv, after, f"exchange_wait_{tag}")
            gbuf = _sum_slots_into(recv, gbuf, layer, row_off, "sum_grads")
        return gbuf

    gbuf = lax.empty((nl, ROWS_L, D_MODEL), F32)
    upper = [e for e in exchanges if e[1] > 0]
    gbuf = landed_sums(gbuf, upper[:1], pending_a[0][4])
    token_0a = forward_on(pending_a, gbuf if upper else pending_a[0][4], 0, 0)
    spread = _spread_start(small_buf, me, token_0a, "small_start")
    gbuf = landed_sums(gbuf, upper[1:] + [e for e in exchanges if e[3] == "0b"], spread[4])

    def swap(t):
        return t.transpose(0, 2, 1)

    def same(t):
        return t

    grads, updates = {}, {}

    def update_rows(gbuf, table):
        for wname, off, view in table:
            g, d_, nm, nv = _adamw_rows(view(wts[wname]), gbuf, off, view(mom_m[wname]), view(mom_v[wname]), "adamw_rows")
            grads[wname], updates[wname] = view(g), (view(d_), view(nm), view(nv))

    update_rows(gbuf, (("ffn2_w_gate", OFF_G2, swap), ("ffn2_w_up", OFF_U2, swap), ("ffn2_w_down", OFF_D2, same),
                       ("w_out", OFF_OUT, same)))
    small_grads = {
        "w_kv_b": (gbuf[:, OFF_KV:OFF_KV + KV_SH_ROWS].reshape(nl, -1, KV_LORA).transpose(0, 2, 1), same),
        "w_in": (gbuf[:, OFF_IN:OFF_IN + IN_SH], swap),
        "w_q_b": (gbuf[:, OFF_Q:OFF_Q + Q_SH_ROWS].reshape(nl, -1, Q_LORA), swap),
    }
    for wname, (g, view) in small_grads.items():
        upd = _adamw_nd(view(wts[wname]), g, view(mom_m[wname]), view(mom_v[wname]), "adamw")
        grads[wname], updates[wname] = view(g), tuple(view(t) for t in upd)
    gbuf = landed_sums(gbuf, [e for e in exchanges if e[3] == "0a"], updates["w_q_b"][0])
    update_rows(gbuf, (("ffn1_w_gate", OFF_G1, swap), ("ffn1_w_up", OFF_U1, swap), ("ffn1_w_down", OFF_D1, same)))

    _, small_all = _split_wait(spread[0], spread[1], N_DEV - 1, spread[2], spread[3], updates["ffn1_w_down"][0],
                               "small_wait")
    small_sum = _sum_slots(small_all, "sum_small")

    def take(name, width=D_MODEL):
        off, n = layout[name]
        return small_sum[off:off + n, :width]

    late = {"ada_b": take("dmod").reshape(nl, N_MOD * D_MODEL),
            "ffn1_norm": take("ffn1_norm"), "mix_norm": take("mix_norm"), "ffn2_norm": take("ffn2_norm"),
            "q_a_norm": take("q_a_norm", Q_LORA), "kv_a_norm": take("kv_a_norm", KV_LORA),
            "pool_scale": take("pool_scale", POOL_WIDTH), "final_norm": take("final_norm").reshape(D_MODEL),
            "pool_w": take("pool_w").reshape(pool_w.shape)}
    loss = take("loss")[0, 0]

    off, n = layout["dmod"]
    dmod_all = small_all[:, off:off + n].reshape(N_DEV, nl, N_MOD * D_MODEL)
    dmod_mine = lax.dynamic_slice_in_dim(dmod_all, me * ada_cols, ada_cols, axis=2)
    dmod_pad = jnp.pad(dmod_mine.transpose(1, 0, 2), ((0, 0), (0, LANE - N_DEV), (0, 0)))
    late["ada_w"] = _ada_grad(jnp.pad(c_all, ((0, LANE - N_DEV), (0, 0))), dmod_pad, "ada_grad")
    for name, g in late.items():
        grads[name], updates[name] = g, _adamw_nd(wts[name], g, mom_m[name], mom_v[name], "adamw")

    return (loss, grad_x, *[grads[n] for n in order], *[updates[n][0] for n in order],
            *[updates[n][1] for n in order], *[updates[n][2] for n in order])
```

```python
import math

import numpy as np
import jax
import jax.numpy as jnp
from jax import lax
from jax.experimental import pallas as pl
from jax.experimental.pallas import tpu as pltpu

F32 = jnp.float32
BF16 = jnp.bfloat16

N_DEV = 8
D_MODEL = 1024
D_FF = 2816
POOL_WIDTH = 512
POOL_WINDOWS = (2, 4, 8, 16)
POOL_GC = 128
N_HEADS = 4
QK_NOPE = 128
QK_ROPE = 64
V_HEAD = 128
QK_HEAD = QK_NOPE + QK_ROPE
HEAD_PAD = 256
Q_LORA = 384
KV_LORA = 256
IN_COLS = POOL_WIDTH + Q_LORA + KV_LORA + QK_ROPE
IN_PAD = 1280
ROPE_THETA = 10000.0
SOFTMAX_SCALE = 1.0 / math.sqrt(QK_HEAD)
EPS = 1e-6
N_MOD = 9

ADAM_LR = 0.001
ADAM_B1 = 0.9
ADAM_B2 = 0.999
ADAM_EPS = 1e-08
ADAM_WD = 0.01
ADAM_STEP = 10

LANE = 128
VMEM_LIMIT = 56 * 1024 * 1024

FF_SH = D_FF // N_DEV
OFF_G1, OFF_U1, OFF_D1 = 0, FF_SH, 2 * FF_SH
OFF_G2, OFF_U2, OFF_D2 = 3 * FF_SH, 4 * FF_SH, 5 * FF_SH
OFF_OUT = 6 * FF_SH
OFF_KV = OFF_OUT + 128
OFF_IN = OFF_KV + 32
OFF_Q = OFF_IN + 160
Q_PAD_ROWS = 64
ROWS_L = OFF_Q + Q_PAD_ROWS
IN_SH = IN_COLS // N_DEV
Q_SH_ROWS = (N_HEADS * QK_HEAD // N_DEV) * Q_LORA // D_MODEL
KV_SH_ROWS = (N_HEADS * (QK_NOPE + V_HEAD) // N_DEV) * KV_LORA // D_MODEL


def _tile(dim, target):
    if dim <= target:
        return dim
    best = None
    for t in range(LANE, target + 1, LANE):
        if dim % t == 0:
            best = t
    assert best is not None, (dim, target)
    return best


def _params(sem):
    return pltpu.CompilerParams(dimension_semantics=sem, vmem_limit_bytes=VMEM_LIMIT)


def _mesh_pos():
    return lax.axis_index("x"), lax.axis_index("y"), lax.axis_index("c")


def _all_gather(x, name):
    m, n = x.shape

    def body(x_ref, out_ref, send_sems, recv_sems, local_sem):
        px, py, pc = _mesh_pos()
        me, sibling = (px, py, pc), (px, py, 1 - pc)
        chips = [(1 - px, py), (px, 1 - py), (1 - px, 1 - py)]

        def rows(bx, by, bc):
            return out_ref.at[pl.ds((4 * bx + 2 * by + bc) * m, m), :]

        def copy(k, block, to, src=None):
            return pltpu.make_async_remote_copy(
                src_ref=rows(*block) if src is None else src, dst_ref=rows(*block),
                send_sem=send_sems.at[k], recv_sem=recv_sems.at[k],
                device_id=to, device_id_type=pl.DeviceIdType.MESH)

        mine = pltpu.make_async_copy(x_ref, rows(*me), local_sem)
        mine.start()
        first = [copy(0, me, sibling, src=x_ref)]
        first += [copy(1 + j, me, (*chip, pc), src=x_ref) for j, chip in enumerate(chips)]
        for cp in first:
            cp.start()
        passed = [copy(4 + j, (*chip, pc), sibling) for j, chip in enumerate(chips)]
        for j, chip in enumerate(chips):
            copy(1 + j, (*chip, pc), me).wait_recv()
            passed[j].start()
        copy(0, sibling, me).wait_recv()
        for j, chip in enumerate(chips):
            copy(4 + j, (*chip, 1 - pc), me).wait_recv()
        for cp in first + passed:
            cp.wait_send()
        mine.wait()

    hbm = pl.BlockSpec(memory_space=pltpu.HBM)
    return pl.pallas_call(
        body, name=name,
        out_shape=jax.ShapeDtypeStruct((N_DEV * m, n), x.dtype),
        in_specs=[hbm], out_specs=hbm,
        scratch_shapes=[pltpu.SemaphoreType.DMA((7,)), pltpu.SemaphoreType.DMA((7,)),
                        pltpu.SemaphoreType.DMA],
    )(x)


SMALL_ROWS = ROWS_L - OFF_KV
ROWS_A = [FF_SH] * 3
ROWS_B = [FF_SH] * 3 + [128, SMALL_ROWS]
ROWS_ALL = ROWS_A + ROWS_B
SPLIT_AB = sum(ROWS_A)
HBM_SPEC = pl.BlockSpec(memory_space=pltpu.HBM)
SEM_SPEC = pl.BlockSpec(memory_space=pltpu.SEMAPHORE)
ANY_SPEC = pl.BlockSpec(memory_space=pl.ANY)
EFFECT = pltpu.SideEffectType.DATAFLOW_SIDE_EFFECTING


def _hbm(t):
    return pltpu.with_memory_space_constraint(t, pltpu.HBM)


def _whole_wait(ref, send_sem, recv_sem, peer):
    return pltpu.make_async_remote_copy(src_ref=ref, dst_ref=ref, send_sem=send_sem, recv_sem=recv_sem,
                                        device_id=peer, device_id_type=pl.DeviceIdType.MESH)


def _offsets(rows_list):
    return [sum(rows_list[:i]) for i in range(len(rows_list))]


def _gather_start(packed, rows_list, after, name):
    n = len(rows_list)
    offs = _offsets(rows_list)
    lands = [_hbm(lax.empty((N_DEV * rows, D_MODEL), BF16)) for rows in rows_list]

    def body(packed_ref, *refs):
        land = refs[:n]
        send_sems, recv_sems = refs[n + 1], refs[n + 2]
        token = refs[-1]
        px, py, pc = _mesh_pos()
        me = 4 * px + 2 * py + pc
        peers = [(px, py, 1 - pc), (1 - px, py, pc), (px, 1 - py, pc), (1 - px, 1 - py, pc)]
        for k, peer in enumerate(peers):
            for off, rows, land_ref in zip(offs, rows_list, land):
                pltpu.make_async_remote_copy(
                    src_ref=packed_ref.at[pl.ds(off, rows), :], dst_ref=land_ref.at[pl.ds(me * rows, rows), :],
                    send_sem=send_sems.at[k], recv_sem=recv_sems.at[k],
                    device_id=peer, device_id_type=pl.DeviceIdType.MESH).start()
        token[...] = jnp.zeros_like(token)

    outs = pl.pallas_call(
        body, name=name,
        out_shape=(pltpu.SemaphoreType.DMA((4,)), pltpu.SemaphoreType.DMA((4,)), pltpu.HBM(packed.shape, BF16),
                   *[pltpu.HBM(t.shape, BF16) for t in lands], jax.ShapeDtypeStruct((8, LANE), F32)),
        in_specs=(HBM_SPEC,) * (1 + n) + (ANY_SPEC,),
        out_specs=(SEM_SPEC, SEM_SPEC) + (HBM_SPEC,) * (1 + n) + (pl.BlockSpec(memory_space=pltpu.VMEM),),
        input_output_aliases={i: 2 + i for i in range(1 + n)},
        compiler_params=pltpu.CompilerParams(has_side_effects=EFFECT),
    )(_hbm(packed), *lands, after)
    return outs[0], outs[1], outs[2], list(outs[3:3 + n]), outs[-1]


def _gather_wait(send_sems, recv_sems, packed, lands, after, name):
    n = len(lands)

    def body(packed_ref, *refs):
        s_sems, r_sems = refs[n], refs[n + 1]
        me = _mesh_pos()
        for k in range(4):
            cp = _whole_wait(packed_ref, s_sems.at[k], r_sems.at[k], me)
            cp.wait_send()
            cp.wait_recv()

    outs = pl.pallas_call(
        body, name=name,
        out_shape=(pltpu.HBM(packed.shape, BF16), *[pltpu.HBM(t.shape, BF16) for t in lands]),
        in_specs=(HBM_SPEC,) * (1 + n) + (SEM_SPEC, SEM_SPEC, ANY_SPEC),
        out_specs=(HBM_SPEC,) * (1 + n),
        input_output_aliases={i: i for i in range(1 + n)},
        compiler_params=pltpu.CompilerParams(has_side_effects=EFFECT),
    )(packed, *lands, send_sems, recv_sems, after)
    return outs[0], list(outs[1:])


def _gather_finish(packed, rows_list, lands, name):
    n = len(rows_list)
    offs = _offsets(rows_list)

    def body(packed_ref, *refs):
        land = refs[n:2 * n]
        send_sems, recv_sems, stage, stage_sem = refs[2 * n:]
        px, py, pc = _mesh_pos()
        me = 4 * px + 2 * py + pc
        sibling = (px, py, 1 - pc)
        load = pltpu.make_async_copy(packed_ref, stage, stage_sem)
        load.start()
        load.wait()
        for off, rows, land_ref in zip(offs, rows_list, land):
            pltpu.make_async_copy(stage.at[pl.ds(off, rows), :], land_ref.at[pl.ds(me * rows, rows), :],
                                  stage_sem).start()
        for j, (cx, cy) in enumerate([(1 - px, py), (px, 1 - py), (1 - px, 1 - py)]):
            block = 4 * cx + 2 * cy + pc
            for rows, land_ref in zip(rows_list, land):
                blk = land_ref.at[pl.ds(block * rows, rows), :]
                pltpu.make_async_remote_copy(src_ref=blk, dst_ref=blk, send_sem=send_sems.at[j],
                                             recv_sem=recv_sems.at[j], device_id=sibling,
                                             device_id_type=pl.DeviceIdType.MESH).start()
        for j in range(3):
            cp = _whole_wait(packed_ref, send_sems.at[j], recv_sems.at[j], sibling)
            cp.wait_recv()
            cp.wait_send()
        pltpu.make_async_copy(stage, packed_ref, stage_sem).wait()

    outs = pl.pallas_call(
        body, name=name,
        out_shape=tuple(jax.ShapeDtypeStruct(t.shape, BF16) for t in lands),
        in_specs=(HBM_SPEC,) * (1 + n), out_specs=(HBM_SPEC,) * n,
        input_output_aliases={1 + i: i for i in range(n)},
        scratch_shapes=[pltpu.SemaphoreType.DMA((3,)), pltpu.SemaphoreType.DMA((3,)),
                        pltpu.VMEM(packed.shape, BF16), pltpu.SemaphoreType.DMA],
    )(packed, *lands)
    return list(outs)


N_CHIPS = 4


def _pair_start(srcs, rows_list, after, name):
    n = len(rows_list)
    offs = _offsets(rows_list)
    land = lax.empty((N_CHIPS, sum(rows_list), D_MODEL), BF16)

    def body(*refs):
        src, land_ref = refs[:n], refs[n]
        send_sems, recv_sems = refs[n + 2], refs[n + 3]
        token = refs[-1]
        px, py, pc = _mesh_pos()
        for k in range(N_CHIPS):
            block = 2 * k + (1 - pc)
            for off, rows, src_ref in zip(offs, rows_list, src):
                pltpu.make_async_remote_copy(
                    src_ref=src_ref.at[pl.ds(block * rows, rows), :], dst_ref=land_ref.at[k, pl.ds(off, rows), :],
                    send_sem=send_sems.at[0], recv_sem=recv_sems.at[0],
                    device_id=(px, py, 1 - pc), device_id_type=pl.DeviceIdType.MESH).start()
        token[...] = jnp.zeros_like(token)

    outs = pl.pallas_call(
        body, name=name,
        out_shape=(pltpu.SemaphoreType.DMA((1,)), pltpu.SemaphoreType.DMA((1,)),
                   *[pltpu.HBM(t.shape, BF16) for t in srcs], pltpu.HBM(land.shape, BF16),
                   jax.ShapeDtypeStruct((8, LANE), F32)),
        in_specs=(HBM_SPEC,) * (n + 1) + (ANY_SPEC,),
        out_specs=(SEM_SPEC, SEM_SPEC) + (HBM_SPEC,) * (n + 1) + (pl.BlockSpec(memory_space=pltpu.VMEM),),
        input_output_aliases={i: 2 + i for i in range(n + 1)},
        compiler_params=pltpu.CompilerParams(has_side_effects=EFFECT),
    )(*[_hbm(t) for t in srcs], _hbm(land), after)
    return outs[0], outs[1], list(outs[2:2 + n]), outs[2 + n], outs[-1]


def _split_wait(send_sems, recv_sems, n_sems, srcs, land, after, name):
    n = len(srcs)

    def body(*refs):
        land_ref = refs[n]
        s_sems, r_sems = refs[n + 1], refs[n + 2]
        me = _mesh_pos()
        for k in range(n_sems):
            cp = _whole_wait(land_ref.at[0] if n_sems > 1 else land_ref, s_sems.at[k], r_sems.at[k], me)
            cp.wait_send()
            cp.wait_recv()

    outs = pl.pallas_call(
        body, name=name,
        out_shape=(*[pltpu.HBM(t.shape, t.dtype) for t in srcs], pltpu.HBM(land.shape, land.dtype)),
        in_specs=(HBM_SPEC,) * (n + 1) + (SEM_SPEC, SEM_SPEC, ANY_SPEC),
        out_specs=(HBM_SPEC,) * (n + 1),
        input_output_aliases={i: i for i in range(n + 1)},
        compiler_params=pltpu.CompilerParams(has_side_effects=EFFECT),
    )(*srcs, land, send_sems, recv_sems, after)
    return list(outs[:n]), outs[n]


def _spread_start(x, me_id, after, name):
    land = lax.dynamic_update_slice_in_dim(lax.empty((N_DEV,) + x.shape, x.dtype), x[None], me_id, axis=0)

    def body(x_ref, land_ref, after_ref, send_sems, recv_sems, x_thru, land_thru, token):
        px, py, pc = _mesh_pos()
        me = 4 * px + 2 * py + pc
        for k in range(1, N_DEV):
            qx = 1 - px if k & 4 else px
            qy = 1 - py if k & 2 else py
            qc = 1 - pc if k & 1 else pc
            pltpu.make_async_remote_copy(
                src_ref=x_ref, dst_ref=land_ref.at[me], send_sem=send_sems.at[k - 1], recv_sem=recv_sems.at[k - 1],
                device_id=(qx, qy, qc), device_id_type=pl.DeviceIdType.MESH).start()
        token[...] = jnp.zeros_like(token)

    outs = pl.pallas_call(
        body, name=name,
        out_shape=(pltpu.SemaphoreType.DMA((N_DEV - 1,)), pltpu.SemaphoreType.DMA((N_DEV - 1,)),
                   pltpu.HBM(x.shape, x.dtype), pltpu.HBM(land.shape, land.dtype), jax.ShapeDtypeStruct((8, LANE), F32)),
        in_specs=(HBM_SPEC, HBM_SPEC, ANY_SPEC),
        out_specs=(SEM_SPEC, SEM_SPEC, HBM_SPEC, HBM_SPEC, pl.BlockSpec(memory_space=pltpu.VMEM)),
        input_output_aliases={0: 2, 1: 3},
        compiler_params=pltpu.CompilerParams(has_side_effects=EFFECT),
    )(_hbm(x), _hbm(land), after)
    return outs[0], outs[1], [outs[2]], outs[3], outs[4]


def _pair_sum(srcs, rows_list, land, core, name):
    n = len(rows_list)
    offs = _offsets(rows_list)
    total = sum(rows_list)

    def body(core_ref, *refs):
        src, land_ref, out_ref = refs[:n], refs[n], refs[n + 1]
        for off, rows, src_ref in zip(offs, rows_list, src):
            out_ref[pl.ds(off, rows), :] = (src_ref[...].astype(F32)
                                            + land_ref[pl.ds(off, rows), :].astype(F32)).astype(BF16)

    slot = pl.BlockSpec((None, total, D_MODEL), lambda k, c: (k, 0, 0))
    grid_spec = pltpu.PrefetchScalarGridSpec(
        num_scalar_prefetch=1, grid=(N_CHIPS,),
        in_specs=[pl.BlockSpec((rows, D_MODEL), lambda k, c: (2 * k + c[0], 0)) for rows in rows_list] + [slot],
        out_specs=slot)
    return pl.pallas_call(
        body, name=name, grid_spec=grid_spec,
        out_shape=jax.ShapeDtypeStruct((N_CHIPS, total, D_MODEL), BF16),
        compiler_params=_params(("parallel",)),
    )(core, *srcs, land)


def _chip_exchange_start(sums, chip, after, name):
    own = lax.dynamic_index_in_dim(sums, chip, axis=0, keepdims=True)
    recv = lax.dynamic_update_slice_in_dim(lax.empty(sums.shape, BF16), own, chip, axis=0)

    def body(sums_ref, recv_ref, after_ref, send_sems, recv_sems, sums_thru, recv_thru, token):
        px, py, pc = _mesh_pos()
        for k in range(1, N_CHIPS):
            qx = 1 - px if k & 2 else px
            qy = 1 - py if k & 1 else py
            pltpu.make_async_remote_copy(
                src_ref=sums_ref.at[2 * qx + qy], dst_ref=recv_ref.at[2 * px + py],
                send_sem=send_sems.at[k - 1], recv_sem=recv_sems.at[k - 1],
                device_id=(qx, qy, pc), device_id_type=pl.DeviceIdType.MESH).start()
        token[...] = jnp.zeros_like(token)

    outs = pl.pallas_call(
        body, name=name,
        out_shape=(pltpu.SemaphoreType.DMA((N_CHIPS - 1,)), pltpu.SemaphoreType.DMA((N_CHIPS - 1,)),
                   pltpu.HBM(sums.shape, BF16), pltpu.HBM(recv.shape, BF16), jax.ShapeDtypeStruct((8, LANE), F32)),
        in_specs=(HBM_SPEC, HBM_SPEC, ANY_SPEC),
        out_specs=(SEM_SPEC, SEM_SPEC, HBM_SPEC, HBM_SPEC, pl.BlockSpec(memory_space=pltpu.VMEM)),
        input_output_aliases={0: 2, 1: 3},
        compiler_params=pltpu.CompilerParams(has_side_effects=EFFECT),
    )(_hbm(sums), _hbm(recv), after)
    return outs[0], outs[1], [outs[2]], outs[3], outs[4]


def _sum_slots_into(recv, buf, layer, row_off, name):
    slots, r, n = recv.shape
    tr = _row_tile(math.gcd(r, row_off) if row_off else r, 512)
    first = row_off // tr

    def body(in_ref, buf_ref, out_ref):
        acc = in_ref[0].astype(F32)
        for j in range(1, slots):
            acc = acc + in_ref[j].astype(F32)
        out_ref[...] = acc

    return pl.pallas_call(
        body, name=name, grid=(r // tr,), out_shape=jax.ShapeDtypeStruct(buf.shape, F32),
        in_specs=[pl.BlockSpec((slots, tr, n), lambda i: (0, i, 0)), ANY_SPEC],
        out_specs=pl.BlockSpec((None, tr, n), lambda i: (layer, first + i, 0)),
        input_output_aliases={1: 0},
        compiler_params=_params(("parallel",)),
    )(recv, buf)


def _sum_slots(recv, name, after=None):
    _, r, n = recv.shape
    tr = _row_tile(r, 512)

    def body(in_ref, *refs):
        acc = in_ref[0].astype(F32)
        for j in range(1, N_DEV):
            acc = acc + in_ref[j].astype(F32)
        refs[-1][...] = acc

    grid = (r // tr,)
    in_specs, out_spec = [pl.BlockSpec((N_DEV, tr, n), lambda i: (0, i, 0))], pl.BlockSpec((tr, n), lambda i: (i, 0))
    args = [recv]
    if after is not None:
        in_specs.append(ANY_SPEC)
        args.append(after)
    return pl.pallas_call(
        body, name=name, grid=grid,
        out_shape=jax.ShapeDtypeStruct((r, n), F32),
        in_specs=in_specs, out_specs=out_spec,
        compiler_params=_params(("parallel",)),
    )(*args)


def _row_tile(rows, target):
    if rows <= target:
        return rows
    best = None
    for t in range(16, target + 1, 16):
        if rows % t == 0:
            best = t
    assert best is not None, rows
    return best


_DIMS = {"nn": ((1,), (0,)), "nt": ((1,), (1,)), "tn": ((0,), (0,))}


def _mm(a, b, mode, name, out_dtype=F32, res=None, gate=None, gate_factor=1.0, tm=512, tn=1408, after=None):
    assert (res is None) == (gate is None)
    if mode == "tn":
        kdim, m = a.shape
    else:
        m, kdim = a.shape
    n = b.shape[0] if mode == "nt" else b.shape[1]
    tm, tn = _tile(m, tm), _tile(n, tn)
    a_spec = (pl.BlockSpec((kdim, tm), lambda i, j: (0, i)) if mode == "tn"
              else pl.BlockSpec((tm, kdim), lambda i, j: (i, 0)))
    b_spec = (pl.BlockSpec((tn, kdim), lambda i, j: (j, 0)) if mode == "nt"
              else pl.BlockSpec((kdim, tn), lambda i, j: (0, j)))
    o_spec = pl.BlockSpec((tm, tn), lambda i, j: (i, j))
    dims = (_DIMS[mode], ((), ()))
    has_res = res is not None

    def body(a_ref, b_ref, *refs):
        y = lax.dot_general(a_ref[...].astype(BF16), b_ref[...].astype(BF16), dims,
                            preferred_element_type=F32)
        if has_res:
            res_ref, gate_ref = refs[0], refs[1]
            y_ref, o_ref = refs[-2], refs[-1]
            y_ref[...] = y.astype(BF16)
            o_ref[...] = res_ref[...] + (gate_factor * gate_ref[...]) * y
        else:
            refs[-1][...] = y.astype(out_dtype)

    in_specs, args = [a_spec, b_spec], [a, b]
    if has_res:
        in_specs += [o_spec, pl.BlockSpec((1, tn), lambda i, j: (0, j))]
        args += [res, gate]
        out_shape = (jax.ShapeDtypeStruct((m, n), BF16), jax.ShapeDtypeStruct((m, n), F32))
        out_specs = (o_spec, o_spec)
    else:
        out_shape, out_specs = jax.ShapeDtypeStruct((m, n), out_dtype), o_spec
    if after is not None:
        in_specs.append(ANY_SPEC)
        args.append(after)
    return pl.pallas_call(
        body, name=name, grid=(m // tm, n // tn), out_shape=out_shape,
        in_specs=in_specs, out_specs=out_specs,
        compiler_params=_params(("parallel", "parallel")),
    )(*args)


def _vec_spec(width):
    return pl.BlockSpec((1, width), lambda i: (0, 0))


def _rm_bwd(dh, x, dres, gw, scale, name, below=None):
    s, d = x.shape
    ts = _tile(s, 256)
    factor = None if below is None else below[2]

    def body(dh_ref, x_ref, dres_ref, gw_ref, sc_ref, *refs):
        dx_ref, dsh_ref, dsc_ref, dgw_ref = refs[-6:-2] if below is not None else refs[-4:]

        @pl.when(pl.program_id(0) == 0)
        def _():
            dsh_ref[...] = jnp.zeros_like(dsh_ref)
            dsc_ref[...] = jnp.zeros_like(dsc_ref)
            dgw_ref[...] = jnp.zeros_like(dgw_ref)
            if below is not None:
                refs[-1][...] = jnp.zeros_like(refs[-1])

        xv, dhv, gwv = x_ref[...], dh_ref[...], gw_ref[...]
        r = lax.rsqrt(jnp.mean(xv * xv, axis=-1, keepdims=True) + EPS)
        xn = xv * r
        y = xn * gwv
        dsh_ref[...] += jnp.sum(dhv, axis=0, keepdims=True)
        dsc_ref[...] += jnp.sum(dhv * y, axis=0, keepdims=True)
        dy = dhv * (1 + sc_ref[...])
        dgw_ref[...] += jnp.sum(dy * xn, axis=0, keepdims=True)
        dxn = dy * gwv
        dx = dres_ref[...] + r * (dxn - xn * jnp.mean(dxn * xn, axis=-1, keepdims=True))
        dx_ref[...] = dx
        if below is not None:
            yb_ref, gb_ref, dyb_ref, dgb_ref = refs[0], refs[1], refs[-2], refs[-1]
            dyb_ref[...] = ((factor * gb_ref[...]) * dx).astype(BF16)
            dgb_ref[...] += jnp.sum((factor * dx) * yb_ref[...].astype(F32), axis=0, keepdims=True)

    row = pl.BlockSpec((ts, d), lambda i: (i, 0))
    vec = jax.ShapeDtypeStruct((1, d), F32)
    in_specs, args = [row, row, row, _vec_spec(d), _vec_spec(d)], [dh, x, dres, gw, scale]
    out_shape = [jax.ShapeDtypeStruct((s, d), F32), vec, vec, vec]
    out_specs = [row, _vec_spec(d), _vec_spec(d), _vec_spec(d)]
    if below is not None:
        in_specs += [row, _vec_spec(d)]
        args += [below[0], below[1]]
        out_shape += [jax.ShapeDtypeStruct((s, d), BF16), vec]
        out_specs += [row, _vec_spec(d)]
    return pl.pallas_call(
        body, name=name, grid=(s // ts,), out_shape=tuple(out_shape),
        in_specs=in_specs, out_specs=tuple(out_specs),
        compiler_params=_params(("arbitrary",)),
    )(*args)


def _gate_bwd(dx, y, gate, factor, name):
    s, d = dx.shape
    ts = _tile(s, 256)

    def body(dx_ref, y_ref, g_ref, dy_ref, dg_ref):
        @pl.when(pl.program_id(0) == 0)
        def _():
            dg_ref[...] = jnp.zeros_like(dg_ref)

        dxv = dx_ref[...]
        dy_ref[...] = ((factor * g_ref[...]) * dxv).astype(BF16)
        dg_ref[...] += jnp.sum((factor * dxv) * y_ref[...].astype(F32), axis=0, keepdims=True)

    row = pl.BlockSpec((ts, d), lambda i: (i, 0))
    return pl.pallas_call(
        body, name=name, grid=(s // ts,),
        out_shape=(jax.ShapeDtypeStruct((s, d), BF16), jax.ShapeDtypeStruct((1, d), F32)),
        in_specs=[row, row, _vec_spec(d)], out_specs=(row, _vec_spec(d)),
        compiler_params=_params(("arbitrary",)),
    )(dx, y, gate)


def _norm_mm(x, gw, shift, scale, w, name, tm=1024):
    s, d = x.shape
    n = w.shape[0]
    tm = _tile(s, tm)

    def body(x_ref, gw_ref, sh_ref, sc_ref, w_ref, h_ref, z_ref):
        xv = x_ref[...]
        r = lax.rsqrt(jnp.mean(xv * xv, axis=-1, keepdims=True) + EPS)
        hb = (((xv * r) * gw_ref[...]) * (1 + sc_ref[...]) + sh_ref[...]).astype(BF16)
        h_ref[...] = hb
        z_ref[...] = lax.dot_general(hb, w_ref[...], (((1,), (1,)), ((), ())), preferred_element_type=F32)

    row = pl.BlockSpec((tm, d), lambda i: (i, 0))
    return pl.pallas_call(
        body, name=name, grid=(s // tm,),
        out_shape=(jax.ShapeDtypeStruct((s, d), BF16), jax.ShapeDtypeStruct((s, n), F32)),
        in_specs=[row, _vec_spec(d), _vec_spec(d), _vec_spec(d), pl.BlockSpec((n, d), lambda i: (0, 0))],
        out_specs=(row, pl.BlockSpec((tm, n), lambda i: (i, 0))),
        compiler_params=_params(("parallel",)),
    )(x, gw, shift, scale, w)


FFN_TM, FFN_TF = 2048, 256


def _ffn_up(x, gw, shift, scale, wg, wu, name):
    s, d = x.shape
    f = wg.shape[0]
    tm, tf = _tile(s, FFN_TM), _tile(f, FFN_TF)
    nt = (((1,), (1,)), ((), ()))

    def body(x_ref, gw_ref, sh_ref, sc_ref, wg_ref, wu_ref, h_ref, a_ref, b_ref, t_ref):
        @pl.when(pl.program_id(1) == 0)
        def _():
            xv = x_ref[...]
            r = lax.rsqrt(jnp.mean(xv * xv, axis=-1, keepdims=True) + EPS)
            h_ref[...] = (((xv * r) * gw_ref[...]) * (1 + sc_ref[...]) + sh_ref[...]).astype(BF16)

        hb = h_ref[...]
        av = lax.dot_general(hb, wg_ref[...], nt, preferred_element_type=F32)
        bv = lax.dot_general(hb, wu_ref[...], nt, preferred_element_type=F32)
        a_ref[...] = av.astype(BF16)
        b_ref[...] = bv.astype(BF16)
        t_ref[...] = ((av * jax.nn.sigmoid(av)) * bv).astype(BF16)

    row = pl.BlockSpec((tm, d), lambda i, j: (i, 0))
    vec = pl.BlockSpec((1, d), lambda i, j: (0, 0))
    wblk = pl.BlockSpec((tf, d), lambda i, j: (j, 0))
    blk = pl.BlockSpec((tm, tf), lambda i, j: (i, j))
    wide = jax.ShapeDtypeStruct((s, f), BF16)
    return pl.pallas_call(
        body, name=name, grid=(s // tm, f // tf),
        out_shape=(jax.ShapeDtypeStruct((s, d), BF16), wide, wide, wide),
        in_specs=[row, vec, vec, vec, wblk, wblk], out_specs=(row, blk, blk, blk),
        compiler_params=_params(("parallel", "arbitrary")),
    )(x, gw, shift, scale, wg, wu)


def _ffn_bwd_cols(dy, h, a, b, t, wd, name, after=None):
    s, d = dy.shape
    f = wd.shape[0]
    tf = _tile(f, FFN_TF)
    nt = (((1,), (1,)), ((), ()))
    tn = (((0,), (0,)), ((), ()))

    def body(dy_ref, h_ref, a_ref, b_ref, t_ref, wd_ref, *refs):
        da_ref, db_ref, gd_ref, gg_ref, gu_ref = refs[-5:]
        dyb, hb = dy_ref[...], h_ref[...]
        dtv = lax.dot_general(dyb, wd_ref[...], nt, preferred_element_type=F32)
        av, bv = a_ref[...].astype(F32), b_ref[...].astype(F32)
        sg = jax.nn.sigmoid(av)
        dbv = (dtv * (av * sg)).astype(BF16)
        dav = ((dtv * bv) * (sg * (1 + av * (1 - sg)))).astype(BF16)
        da_ref[...] = dav
        db_ref[...] = dbv
        gd_ref[...] = lax.dot_general(t_ref[...], dyb, tn, preferred_element_type=F32).astype(BF16)
        gg_ref[...] = lax.dot_general(dav, hb, tn, preferred_element_type=F32).astype(BF16)
        gu_ref[...] = lax.dot_general(dbv, hb, tn, preferred_element_type=F32).astype(BF16)

    whole = pl.BlockSpec((s, d), lambda j: (0, 0))
    col = pl.BlockSpec((s, tf), lambda j: (0, j))
    wblk = pl.BlockSpec((tf, d), lambda j: (j, 0))
    wide, wgrad = jax.ShapeDtypeStruct((s, f), BF16), jax.ShapeDtypeStruct((f, d), BF16)
    in_specs, args = [whole, whole, col, col, col, wblk], [dy, h, a, b, t, wd]
    if after is not None:
        in_specs.append(ANY_SPEC)
        args.append(after)
    return pl.pallas_call(
        body, name=name, grid=(f // tf,), out_shape=(wide, wide, wgrad, wgrad, wgrad),
        in_specs=in_specs, out_specs=(col, col, wblk, wblk, wblk),
        compiler_params=_params(("parallel",)),
    )(*args)


def _mm_pair(a1, b1, a2, b2, name, tm=1024, tn=512, after=None):
    m, kdim = a1.shape
    n = b1.shape[1]
    tm, tn = _tile(m, tm), _tile(n, tn)

    def body(a1_ref, b1_ref, a2_ref, b2_ref, *refs):
        refs[-1][...] = (jnp.dot(a1_ref[...], b1_ref[...], preferred_element_type=F32)
                         + jnp.dot(a2_ref[...], b2_ref[...], preferred_element_type=F32))

    a_spec = pl.BlockSpec((tm, kdim), lambda i, j: (i, 0))
    b_spec = pl.BlockSpec((kdim, tn), lambda i, j: (0, j))
    in_specs, args = [a_spec, b_spec, a_spec, b_spec], [a1, b1, a2, b2]
    if after is not None:
        in_specs.append(ANY_SPEC)
        args.append(after)
    return pl.pallas_call(
        body, name=name, grid=(m // tm, n // tn), out_shape=jax.ShapeDtypeStruct((m, n), F32),
        in_specs=in_specs, out_specs=pl.BlockSpec((tm, tn), lambda i, j: (i, j)),
        compiler_params=_params(("parallel", "parallel")),
    )(*args)


def _pool_counts(s):
    return (lax.broadcasted_iota(jnp.int32, (s, POOL_GC), 0))


def _pool_fwd(z, pool_w, pool_scale, name):
    s = z.shape[0]

    def body(u_ref, w_ref, sc_ref, y_ref, diff_ref):
        t = lax.broadcasted_iota(jnp.int32, (s, POOL_GC), 0)
        for g, win in enumerate(POOL_WINDOWS):
            cols = slice(g * POOL_GC, (g + 1) * POOL_GC)
            u = u_ref[:, cols]
            acc, step = u, 1
            while step < win:
                acc = acc + jnp.where(t >= step, pltpu.roll(acc, step, 0), 0.0)
                step *= 2
            cnt = jnp.minimum(t + 1, win).astype(F32)
            diff = acc / cnt - u
            diff_ref[:, cols] = diff
            ypre = jnp.dot(diff.astype(BF16), w_ref[g].astype(BF16), preferred_element_type=F32)
            y_ref[:, cols] = (ypre * sc_ref[:, cols]).astype(BF16)

    return pl.pallas_call(
        body, name=name, grid=(1,),
        out_shape=(jax.ShapeDtypeStruct((s, POOL_WIDTH), BF16), jax.ShapeDtypeStruct((s, POOL_WIDTH), F32)),
        in_specs=[pl.BlockSpec((s, POOL_WIDTH), lambda i: (0, 0)),
                  pl.BlockSpec(pool_w.shape, lambda i: (0, 0, 0)),
                  pl.BlockSpec((1, POOL_WIDTH), lambda i: (0, 0))],
        out_specs=(pl.BlockSpec((s, POOL_WIDTH), lambda i: (0, 0)),
                   pl.BlockSpec((s, POOL_WIDTH), lambda i: (0, 0))),
        compiler_params=_params(("arbitrary",)),
    )(z, pool_w, pool_scale)


def _pool_bwd(dycat, diff, pool_w, pool_scale, name):
    s = diff.shape[0]

    def body(dy_ref, diff_ref, w_ref, sc_ref, du_ref, dw_ref, dsc_ref):
        t = lax.broadcasted_iota(jnp.int32, (s, POOL_GC), 0)
        for g, win in enumerate(POOL_WINDOWS):
            cols = slice(g * POOL_GC, (g + 1) * POOL_GC)
            dy, dfb, wb = dy_ref[:, cols], diff_ref[:, cols].astype(BF16), w_ref[g].astype(BF16)
            ypre = jnp.dot(dfb, wb, preferred_element_type=F32)
            dsc_ref[:, cols] = jnp.sum(dy * ypre, axis=0, keepdims=True)
            dypre = (dy * sc_ref[:, cols]).astype(BF16)
            ddiff = lax.dot_general(dypre, wb, (((1,), (1,)), ((), ())), preferred_element_type=F32)
            dw_ref[g] = lax.dot_general(dfb, dypre, (((0,), (0,)), ((), ())), preferred_element_type=F32)
            cnt = jnp.minimum(t + 1, win).astype(F32)
            acc, step = ddiff / cnt, 1
            while step < win:
                acc = acc + jnp.where(t < s - step, pltpu.roll(acc, s - step, 0), 0.0)
                step *= 2
            du_ref[:, cols] = acc - ddiff

    full = pl.BlockSpec((s, POOL_WIDTH), lambda i: (0, 0))
    return pl.pallas_call(
        body, name=name, grid=(1,),
        out_shape=(jax.ShapeDtypeStruct((s, POOL_WIDTH), F32),
                   jax.ShapeDtypeStruct(pool_w.shape, F32),
                   jax.ShapeDtypeStruct((1, POOL_WIDTH), F32)),
        in_specs=[full, full, pl.BlockSpec(pool_w.shape, lambda i: (0, 0, 0)),
                  pl.BlockSpec((1, POOL_WIDTH), lambda i: (0, 0))],
        out_specs=(full, pl.BlockSpec(pool_w.shape, lambda i: (0, 0, 0)),
                   pl.BlockSpec((1, POOL_WIDTH), lambda i: (0, 0))),
        compiler_params=_params(("arbitrary",)),
    )(dycat, diff, pool_w, pool_scale)


def _rope_tables(positions, name):
    s = positions.shape[0]
    ts = _tile(s, 512)
    freq = 1.0 / (ROPE_THETA ** (np.arange(0, QK_ROPE, 2, dtype=np.float32) / QK_ROPE))
    table = np.zeros((1, LANE), np.float32)
    table[0, :QK_ROPE // 2] = freq
    table[0, QK_ROPE // 2:QK_ROPE] = freq

    def body(pos_ref, f_ref, cos_ref, sin_ref):
        ang = pos_ref[...].astype(F32) * f_ref[...]
        cos_ref[...] = jnp.cos(ang)
        sin_ref[...] = jnp.sin(ang)

    out = jax.ShapeDtypeStruct((s, LANE), F32)
    blk = pl.BlockSpec((ts, LANE), lambda i: (i, 0))
    return pl.pallas_call(
        body, name=name, grid=(s // ts,), out_shape=(out, out),
        in_specs=[pl.BlockSpec((ts, 1), lambda i: (i, 0)), _vec_spec(LANE)], out_specs=(blk, blk),
        compiler_params=_params(("parallel",)),
    )(positions, jnp.asarray(table))


def _lane_mod64_low(shape):
    return (lax.broadcasted_iota(jnp.int32, shape, 1) % QK_ROPE) < (QK_ROPE // 2)


def _rope(x, cos, sin):
    rot = jnp.where(_lane_mod64_low(x.shape), -pltpu.roll(x, LANE - 32, 1), pltpu.roll(x, 32, 1))
    return x * cos + rot * sin


def _rope_t(dy, cos, sin):
    w = dy * sin
    rot_t = jnp.where(_lane_mod64_low(dy.shape), pltpu.roll(w, LANE - 32, 1), -pltpu.roll(w, 32, 1))
    return dy * cos + rot_t


def _plain_rms(x, g):
    r = lax.rsqrt(jnp.mean(x * x, axis=-1, keepdims=True) + EPS)
    return (x * r) * g, x * r, r


O_Q, O_KV, O_KR = POOL_WIDTH, POOL_WIDTH + Q_LORA, POOL_WIDTH + Q_LORA + KV_LORA


def _qkv_fwd(z, qn, kvn, wq, wkv, cos, sin, name):
    s = z.shape[0]
    ts = _tile(s, 256)

    def body(z_ref, qn_ref, kvn_ref, wq_ref, wkv_ref, cos_ref, sin_ref, q_ref, k_ref, v_ref, cqn_ref, ckvn_ref):
        cosv, sinv = cos_ref[...], sin_ref[...]
        cqn = _plain_rms(z_ref[:, O_Q:O_KV], qn_ref[...])[0].astype(BF16)
        ckvn = _plain_rms(z_ref[:, O_KV:O_KR], kvn_ref[...])[0].astype(BF16)
        cqn_ref[...] = cqn
        ckvn_ref[...] = ckvn
        nt = (((1,), (1,)), ((), ()))
        q = lax.dot_general(cqn, wq_ref[...], nt, preferred_element_type=F32)
        kv = lax.dot_general(ckvn, wkv_ref[...], nt, preferred_element_type=F32)
        kr = _rope(z_ref[:, O_KR:IN_PAD], cosv, sinv).astype(BF16)
        for h in range(N_HEADS):
            o = h * HEAD_PAD
            q_ref[:, o:o + QK_NOPE] = q[:, o:o + QK_NOPE].astype(BF16)
            q_ref[:, o + QK_NOPE:o + HEAD_PAD] = _rope(q[:, o + QK_NOPE:o + HEAD_PAD], cosv, sinv).astype(BF16)
            k_ref[:, o:o + QK_NOPE] = kv[:, o:o + QK_NOPE].astype(BF16)
            k_ref[:, o + QK_NOPE:o + HEAD_PAD] = kr
            v_ref[:, h * V_HEAD:(h + 1) * V_HEAD] = kv[:, o + QK_NOPE:o + HEAD_PAD].astype(BF16)

    def row(w):
        return pl.BlockSpec((ts, w), lambda i: (i, 0))

    def whole(arr):
        return pl.BlockSpec(arr.shape, lambda i: (0, 0))

    hp = N_HEADS * HEAD_PAD
    return pl.pallas_call(
        body, name=name, grid=(s // ts,),
        out_shape=(jax.ShapeDtypeStruct((s, hp), BF16), jax.ShapeDtypeStruct((s, hp), BF16),
                   jax.ShapeDtypeStruct((s, N_HEADS * V_HEAD), BF16),
                   jax.ShapeDtypeStruct((s, Q_LORA), BF16), jax.ShapeDtypeStruct((s, KV_LORA), BF16)),
        in_specs=[row(IN_PAD), whole(qn), whole(kvn), whole(wq), whole(wkv), row(LANE), row(LANE)],
        out_specs=(row(hp), row(hp), row(N_HEADS * V_HEAD), row(Q_LORA), row(KV_LORA)),
        compiler_params=_params(("parallel",)),
    )(z, qn, kvn, wq, wkv, cos, sin)


def _qkv_bwd(dq, dk, dv, du, z, qn, kvn, wq, wkv, cos, sin, name):
    s = z.shape[0]
    ts = _tile(s, 256)

    def norm_bwd(x, g, dy):
        _, xn, r = _plain_rms(x, g)
        dxn = dy * g
        return r * (dxn - xn * jnp.mean(dxn * xn, axis=-1, keepdims=True)), jnp.sum(dy * xn, axis=0, keepdims=True)

    def body(dq_ref, dk_ref, dv_ref, du_ref, z_ref, qn_ref, kvn_ref, wq_ref, wkv_ref, cos_ref, sin_ref,
             dz_ref, dqb_ref, dkvb_ref, dqn_ref, dkvn_ref):
        @pl.when(pl.program_id(0) == 0)
        def _():
            dqn_ref[...] = jnp.zeros_like(dqn_ref)
            dkvn_ref[...] = jnp.zeros_like(dkvn_ref)

        cosv, sinv = cos_ref[...], sin_ref[...]
        dkr = jnp.zeros((ts, LANE), F32)
        for h in range(N_HEADS):
            o = h * HEAD_PAD
            dqb_ref[:, o:o + QK_NOPE] = dq_ref[:, o:o + QK_NOPE].astype(BF16)
            dqb_ref[:, o + QK_NOPE:o + HEAD_PAD] = _rope_t(dq_ref[:, o + QK_NOPE:o + HEAD_PAD], cosv, sinv).astype(BF16)
            dkvb_ref[:, o:o + QK_NOPE] = dk_ref[:, o:o + QK_NOPE].astype(BF16)
            dkvb_ref[:, o + QK_NOPE:o + HEAD_PAD] = dv_ref[:, h * V_HEAD:(h + 1) * V_HEAD].astype(BF16)
            dkr = dkr + dk_ref[:, o + QK_NOPE:o + HEAD_PAD]
        dcqn = jnp.dot(dqb_ref[...], wq_ref[...], preferred_element_type=F32)
        dckvn = jnp.dot(dkvb_ref[...], wkv_ref[...], preferred_element_type=F32)
        dcq, dqn = norm_bwd(z_ref[:, O_Q:O_KV], qn_ref[...], dcqn)
        dckv, dkvn = norm_bwd(z_ref[:, O_KV:O_KR], kvn_ref[...], dckvn)
        dqn_ref[...] += dqn
        dkvn_ref[...] += dkvn
        dz_ref[:, 0:O_Q] = du_ref[...].astype(BF16)
        dz_ref[:, O_Q:O_KV] = dcq.astype(BF16)
        dz_ref[:, O_KV:O_KR] = dckv.astype(BF16)
        dz_ref[:, O_KR:IN_PAD] = _rope_t(dkr, cosv, sinv).astype(BF16)

    def row(w):
        return pl.BlockSpec((ts, w), lambda i: (i, 0))

    def whole(arr):
        return pl.BlockSpec(arr.shape, lambda i: (0, 0))

    hp = N_HEADS * HEAD_PAD
    return pl.pallas_call(
        body, name=name, grid=(s // ts,),
        out_shape=(jax.ShapeDtypeStruct((s, IN_PAD), BF16), jax.ShapeDtypeStruct((s, hp), BF16),
                   jax.ShapeDtypeStruct((s, hp), BF16),
                   jax.ShapeDtypeStruct((1, Q_LORA), F32), jax.ShapeDtypeStruct((1, KV_LORA), F32)),
        in_specs=[row(hp), row(hp), row(N_HEADS * V_HEAD), row(POOL_WIDTH), row(IN_PAD),
                  whole(qn), whole(kvn), whole(wq), whole(wkv), row(LANE), row(LANE)],
        out_specs=(row(IN_PAD), row(hp), row(hp), whole(qn), whole(kvn)),
        compiler_params=_params(("arbitrary",)),
    )(dq, dk, dv, du, z, qn, kvn, wq, wkv, cos, sin)


def _causal_scores(q, k, i, tq, klen):
    sc = lax.dot_general(q, k, (((1,), (1,)), ((), ())), preferred_element_type=F32) * SOFTMAX_SCALE
    qpos = i * tq + lax.broadcasted_iota(jnp.int32, (tq, klen), 0)
    kpos = lax.broadcasted_iota(jnp.int32, (tq, klen), 1)
    return jnp.where(qpos >= kpos, sc, -jnp.inf)


ATTN_TQ = 512
ATTN_SEGMENTS = 4


def _by_key_prefix(i, nq, tq, compute):
    nseg = min(ATTN_SEGMENTS, nq)
    per = nq // nseg
    for r in range(nseg):
        pl.when(i // per == r)(lambda r=r: compute((r + 1) * per * tq))


def _attn_fwd(q, k, v, name):
    s = q.shape[0]
    tq = _tile(s, ATTN_TQ)
    nq = s // tq

    def body(q_ref, k_ref, v_ref, o_ref, lse_ref):
        i = pl.program_id(1)

        def compute(klen):
            sc = _causal_scores(q_ref[...], k_ref[0:klen, :], i, tq, klen)
            mx = jnp.max(sc, axis=-1, keepdims=True)
            p = jnp.exp(sc - mx)
            den = jnp.sum(p, axis=-1, keepdims=True)
            o_ref[...] = jnp.dot((p / den).astype(BF16), v_ref[0:klen, :], preferred_element_type=F32)
            lse_ref[...] = mx + jnp.log(den)

        _by_key_prefix(i, nq, tq, compute)

    return pl.pallas_call(
        body, name=name, grid=(N_HEADS, s // tq),
        out_shape=(jax.ShapeDtypeStruct((s, N_HEADS * V_HEAD), F32), jax.ShapeDtypeStruct((N_HEADS, s, 1), F32)),
        in_specs=[pl.BlockSpec((tq, HEAD_PAD), lambda h, i: (i, h)),
                  pl.BlockSpec((s, HEAD_PAD), lambda h, i: (0, h)),
                  pl.BlockSpec((s, V_HEAD), lambda h, i: (0, h))],
        out_specs=(pl.BlockSpec((tq, V_HEAD), lambda h, i: (i, h)),
                   pl.BlockSpec((None, tq, 1), lambda h, i: (h, i, 0))),
        compiler_params=_params(("parallel", "parallel")),
    )(q, k, v)


def _attn_bwd(q, k, v, lse, dycat, name):
    s = q.shape[0]
    tq = _tile(s, ATTN_TQ)
    nq = s // tq
    tn_dims = (((0,), (0,)), ((), ()))

    def body(q_ref, k_ref, v_ref, lse_ref, do_ref, dq_ref, dk_ref, dv_ref):
        i = pl.program_id(1)

        @pl.when(i == 0)
        def _():
            dk_ref[...] = jnp.zeros_like(dk_ref)
            dv_ref[...] = jnp.zeros_like(dv_ref)

        def compute(klen):
            qv, kv_, dob = q_ref[...], k_ref[0:klen, :], do_ref[...].astype(BF16)
            sc = _causal_scores(qv, kv_, i, tq, klen)
            p = jnp.exp(sc - lse_ref[...])
            dp = lax.dot_general(dob, v_ref[0:klen, :], (((1,), (1,)), ((), ())), preferred_element_type=F32)
            ds = (p * (dp - jnp.sum(dp * p, axis=-1, keepdims=True)) * SOFTMAX_SCALE).astype(BF16)
            dq_ref[...] = jnp.dot(ds, kv_, preferred_element_type=F32)
            dk_ref[0:klen, :] += lax.dot_general(ds, qv, tn_dims, preferred_element_type=F32)
            dv_ref[0:klen, :] += lax.dot_general(p.astype(BF16), dob, tn_dims, preferred_element_type=F32)

        _by_key_prefix(i, nq, tq, compute)

    n_pool_blocks = POOL_WIDTH // V_HEAD
    return pl.pallas_call(
        body, name=name, grid=(N_HEADS, s // tq),
        out_shape=(jax.ShapeDtypeStruct((s, N_HEADS * HEAD_PAD), F32),
                   jax.ShapeDtypeStruct((s, N_HEADS * HEAD_PAD), F32),
                   jax.ShapeDtypeStruct((s, N_HEADS * V_HEAD), F32)),
        in_specs=[pl.BlockSpec((tq, HEAD_PAD), lambda h, i: (i, h)),
                  pl.BlockSpec((s, HEAD_PAD), lambda h, i: (0, h)),
                  pl.BlockSpec((s, V_HEAD), lambda h, i: (0, h)),
                  pl.BlockSpec((None, tq, 1), lambda h, i: (h, i, 0)),
                  pl.BlockSpec((tq, V_HEAD), lambda h, i: (i, n_pool_blocks + h))],
        out_specs=(pl.BlockSpec((tq, HEAD_PAD), lambda h, i: (i, h)),
                   pl.BlockSpec((s, HEAD_PAD), lambda h, i: (0, h)),
                   pl.BlockSpec((s, V_HEAD), lambda h, i: (0, h))),
        compiler_params=_params(("parallel", "arbitrary")),
    )(q, k, v, lse, dycat)


def _loss_head(x, gw, target, name):
    s, d = x.shape
    ts = _tile(s, 256)

    def body(x_ref, gw_ref, tgt_ref, loss_ref, dx_ref, dgw_ref):
        @pl.when(pl.program_id(0) == 0)
        def _():
            loss_ref[...] = jnp.zeros_like(loss_ref)
            dgw_ref[...] = jnp.zeros_like(dgw_ref)

        xv, gwv = x_ref[...], gw_ref[...]
        r = lax.rsqrt(jnp.mean(xv * xv, axis=-1, keepdims=True) + EPS)
        xn = xv * r
        err = xn * gwv - tgt_ref[...]
        loss_ref[...] += 0.5 * jnp.sum(jnp.mean(err * err, axis=-1, keepdims=True))
        dy = err / d
        dgw_ref[...] += jnp.sum(dy * xn, axis=0, keepdims=True)
        dxn = dy * gwv
        dx_ref[...] = r * (dxn - xn * jnp.mean(dxn * xn, axis=-1, keepdims=True))

    row = pl.BlockSpec((ts, d), lambda i: (i, 0))
    return pl.pallas_call(
        body, name=name, grid=(s // ts,),
        out_shape=(jax.ShapeDtypeStruct((8, LANE), F32), jax.ShapeDtypeStruct((s, d), F32),
                   jax.ShapeDtypeStruct((1, d), F32)),
        in_specs=[row, _vec_spec(d), row],
        out_specs=(pl.BlockSpec((8, LANE), lambda i: (0, 0)), row, _vec_spec(d)),
        compiler_params=_params(("arbitrary",)),
    )(x, gw, target)


def _ada_mod(c_all, ada_w, ada_b, name):
    nl, d, cols = ada_w.shape

    def body(c_ref, w_ref, b_ref, o_ref):
        cv = c_ref[...]
        act = (cv * jax.nn.sigmoid(cv)).astype(BF16)
        o_ref[...] = jnp.dot(act, w_ref[...].astype(BF16), preferred_element_type=F32) + b_ref[...]

    return pl.pallas_call(
        body, name=name, grid=(nl,), out_shape=jax.ShapeDtypeStruct((nl, N_DEV, cols), F32),
        in_specs=[pl.BlockSpec((N_DEV, d), lambda l: (0, 0)),
                  pl.BlockSpec((None, d, cols), lambda l: (l, 0, 0)),
                  pl.BlockSpec((None, 1, cols), lambda l: (l, 0, 0))],
        out_specs=pl.BlockSpec((None, N_DEV, cols), lambda l: (l, 0, 0)),
        compiler_params=_params(("parallel",)),
    )(c_all, ada_w, ada_b)


def _ada_grad(c_pad, dmod_pad, name):
    nl, kpad, cols = dmod_pad.shape
    d = c_pad.shape[1]

    def body(c_ref, dm_ref, o_ref):
        cv = c_ref[...]
        act = (cv * jax.nn.sigmoid(cv)).astype(BF16)
        o_ref[...] = lax.dot_general(act, dm_ref[...].astype(BF16), (((0,), (0,)), ((), ())),
                                     preferred_element_type=F32)

    return pl.pallas_call(
        body, name=name, grid=(nl,), out_shape=jax.ShapeDtypeStruct((nl, d, cols), F32),
        in_specs=[pl.BlockSpec((kpad, d), lambda l: (0, 0)),
                  pl.BlockSpec((None, kpad, cols), lambda l: (l, 0, 0))],
        out_specs=pl.BlockSpec((None, d, cols), lambda l: (l, 0, 0)),
        compiler_params=_params(("parallel",)),
    )(c_pad, dmod_pad)


def _adamw_math(w, g, m, v):
    nm = ADAM_B1 * m + (1.0 - ADAM_B1) * g
    nv = ADAM_B2 * v + (1.0 - ADAM_B2) * (g * g)
    m_hat = nm / (1.0 - ADAM_B1 ** ADAM_STEP)
    v_hat = nv / (1.0 - ADAM_B2 ** ADAM_STEP)
    return -ADAM_LR * (m_hat / (jnp.sqrt(v_hat) + ADAM_EPS) + ADAM_WD * w), nm, nv


def _adamw_rows(w3, gbuf, row_off, m3, v3, name):
    nl, r, d = w3.shape
    tr = _row_tile(math.gcd(r, row_off) if row_off else r, 176)
    first = row_off // tr

    def body(w_ref, g_ref, m_ref, v_ref, go_ref, d_ref, nm_ref, nv_ref):
        gv = g_ref[...]
        go_ref[...] = gv
        d_ref[...], nm_ref[...], nv_ref[...] = _adamw_math(w_ref[...], gv, m_ref[...], v_ref[...])

    blk = pl.BlockSpec((None, tr, d), lambda l, i: (l, i, 0))
    gblk = pl.BlockSpec((None, tr, d), lambda l, i: (l, first + i, 0))
    out = jax.ShapeDtypeStruct((nl, r, d), F32)
    return pl.pallas_call(
        body, name=name, grid=(nl, r // tr), out_shape=(out, out, out, out),
        in_specs=[blk, gblk, blk, blk], out_specs=(blk, blk, blk, blk),
        compiler_params=_params(("parallel", "parallel")),
    )(w3, gbuf, m3, v3)


def _adamw(w, g, m, v, name):
    rows, cols = w.shape
    tr = _row_tile(rows, 512)

    def body(w_ref, g_ref, m_ref, v_ref, d_ref, nm_ref, nv_ref):
        d_ref[...], nm_ref[...], nv_ref[...] = _adamw_math(w_ref[...], g_ref[...], m_ref[...], v_ref[...])

    blk = pl.BlockSpec((tr, cols), lambda i: (i, 0))
    out = jax.ShapeDtypeStruct((rows, cols), F32)
    return pl.pallas_call(
        body, name=name, grid=(rows // tr,), out_shape=(out, out, out),
        in_specs=[blk, blk, blk, blk], out_specs=(blk, blk, blk),
        compiler_params=_params(("parallel",)),
    )(w, g, m, v)


def _adamw_nd(w, g, m, v, name):
    shape = w.shape
    flat = (lambda t: t.reshape(1, -1)) if w.ndim == 1 else (lambda t: t.reshape(-1, shape[-1]))
    return tuple(t.reshape(shape) for t in _adamw(flat(w), flat(g), flat(m), flat(v), name))


def _pad_rows(t, rows):
    return jnp.pad(t, ((0, rows - t.shape[0]), (0, 0)))


def _pack_shard_layer(l, wts):
    def tr(name):
        return wts[name][l].astype(BF16).T

    parts = [tr("ffn1_w_gate"), tr("ffn1_w_up"), wts["ffn1_w_down"][l].astype(BF16),
             tr("ffn2_w_gate"), tr("ffn2_w_up"), wts["ffn2_w_down"][l].astype(BF16),
             wts["w_out"][l].astype(BF16),
             tr("w_kv_b").reshape(KV_SH_ROWS, D_MODEL),
             _pad_rows(tr("w_in"), 160),
             _pad_rows(tr("w_q_b").reshape(Q_SH_ROWS, D_MODEL), Q_PAD_ROWS)]
    return jnp.concatenate(parts, axis=0)


def _full_weights(lands):
    w = dict(zip(("g1", "u1", "d1", "g2", "u2", "d2", "out"), lands))
    small = lands[-1].reshape(N_DEV, SMALL_ROWS, D_MODEL)
    o_in, o_q = OFF_IN - OFF_KV, OFF_Q - OFF_KV
    w["kv"] = small[:, :KV_SH_ROWS].reshape(N_HEADS * HEAD_PAD, KV_LORA)
    w["in"] = _pad_rows(small[:, o_in:o_in + IN_SH].reshape(IN_COLS, D_MODEL), IN_PAD)
    wq = small[:, o_q:o_q + Q_SH_ROWS].reshape(N_HEADS, QK_HEAD, Q_LORA)
    w["q"] = jnp.pad(wq, ((0, 0), (0, HEAD_PAD - QK_HEAD), (0, 0))).reshape(N_HEADS * HEAD_PAD, Q_LORA)
    return w


def _grad_sources_b(gr):
    gq = gr["q"].reshape(N_HEADS, HEAD_PAD, Q_LORA)[:, :QK_HEAD].reshape(N_DEV, Q_SH_ROWS, D_MODEL)
    small = jnp.concatenate([
        gr["kv"].reshape(N_DEV, KV_SH_ROWS, D_MODEL),
        jnp.pad(gr["in"][:IN_COLS].reshape(N_DEV, IN_SH, D_MODEL), ((0, 0), (0, 160 - IN_SH), (0, 0))),
        jnp.pad(gq, ((0, 0), (0, Q_PAD_ROWS - Q_SH_ROWS), (0, 0)))], axis=1)
    return [gr["g2"], gr["u2"], gr["d2"], gr["out"], small.reshape(N_DEV * SMALL_ROWS, D_MODEL)]


def _pack_bf16_pairs(t):
    rows, d = t.shape
    return lax.bitcast_convert_type(t.astype(BF16).reshape(rows // 2, 2, d).transpose(0, 2, 1), F32)


def _unpack_bf16_pairs(p):
    pairs = jnp.swapaxes(lax.bitcast_convert_type(p, BF16), -1, -2)
    return pairs.reshape(p.shape[:-2] + (2 * p.shape[-2], p.shape[-1]))


def _small_layout(nl):
    names = [("dmod", nl * N_MOD), ("ffn1_norm", nl), ("mix_norm", nl), ("ffn2_norm", nl), ("q_a_norm", nl),
             ("kv_a_norm", nl), ("pool_scale", nl), ("final_norm", 1), ("loss", 1),
             ("pool_w", nl * 4 * POOL_GC * POOL_GC // D_MODEL // 2)]
    off, table = 0, {}
    for name, n in names:
        table[name] = (off, n)
        off += -(-n // 8) * 8
    return table, off


def _to_rows(t, width=D_MODEL):
    n, w = t.shape
    return jnp.pad(t, ((0, -(-n // 8) * 8 - n), (0, width - w)))


def kernel(x, c, positions, ada_w, ada_b, ffn1_norm, ffn1_w_gate, ffn1_w_up, ffn1_w_down, mix_norm, w_in, pool_w, pool_scale, q_a_norm, w_q_b, kv_a_norm, w_kv_b, w_out, ffn2_norm, ffn2_w_gate, ffn2_w_up, ffn2_w_down, final_norm, loss_target, m_ada_w, m_ada_b, m_ffn1_norm, m_ffn1_w_gate, m_ffn1_w_up, m_ffn1_w_down, m_mix_norm, m_w_in, m_pool_w, m_pool_scale, m_q_a_norm, m_w_q_b, m_kv_a_norm, m_w_kv_b, m_w_out, m_ffn2_norm, m_ffn2_w_gate, m_ffn2_w_up, m_ffn2_w_down, m_final_norm, v_ada_w, v_ada_b, v_ffn1_norm, v_ffn1_w_gate, v_ffn1_w_up, v_ffn1_w_down, v_mix_norm, v_w_in, v_pool_w, v_pool_scale, v_q_a_norm, v_w_q_b, v_kv_a_norm, v_w_kv_b, v_w_out, v_ffn2_norm, v_ffn2_w_gate, v_ffn2_w_up, v_ffn2_w_down, v_final_norm):
    wts = dict(ada_w=ada_w, ada_b=ada_b, ffn1_norm=ffn1_norm, ffn1_w_gate=ffn1_w_gate, ffn1_w_up=ffn1_w_up,
               ffn1_w_down=ffn1_w_down, mix_norm=mix_norm, w_in=w_in, pool_w=pool_w, pool_scale=pool_scale,
               q_a_norm=q_a_norm, w_q_b=w_q_b, kv_a_norm=kv_a_norm, w_kv_b=w_kv_b, w_out=w_out,
               ffn2_norm=ffn2_norm, ffn2_w_gate=ffn2_w_gate, ffn2_w_up=ffn2_w_up, ffn2_w_down=ffn2_w_down,
               final_norm=final_norm)
    mom_m = dict(ada_w=m_ada_w, ada_b=m_ada_b, ffn1_norm=m_ffn1_norm, ffn1_w_gate=m_ffn1_w_gate,
                 ffn1_w_up=m_ffn1_w_up, ffn1_w_down=m_ffn1_w_down, mix_norm=m_mix_norm, w_in=m_w_in,
                 pool_w=m_pool_w, pool_scale=m_pool_scale, q_a_norm=m_q_a_norm, w_q_b=m_w_q_b,
                 kv_a_norm=m_kv_a_norm, w_kv_b=m_w_kv_b, w_out=m_w_out, ffn2_norm=m_ffn2_norm,
                 ffn2_w_gate=m_ffn2_w_gate, ffn2_w_up=m_ffn2_w_up, ffn2_w_down=m_ffn2_w_down,
                 final_norm=m_final_norm)
    mom_v = dict(ada_w=v_ada_w, ada_b=v_ada_b, ffn1_norm=v_ffn1_norm, ffn1_w_gate=v_ffn1_w_gate,
                 ffn1_w_up=v_ffn1_w_up, ffn1_w_down=v_ffn1_w_down, mix_norm=v_mix_norm, w_in=v_w_in,
                 pool_w=v_pool_w, pool_scale=v_pool_scale, q_a_norm=v_q_a_norm, w_q_b=v_w_q_b,
                 kv_a_norm=v_kv_a_norm, w_kv_b=v_w_kv_b, w_out=v_w_out, ffn2_norm=v_ffn2_norm,
                 ffn2_w_gate=v_ffn2_w_gate, ffn2_w_up=v_ffn2_w_up, ffn2_w_down=v_ffn2_w_down,
                 final_norm=v_final_norm)
    order = list(wts)
    nl = ada_w.shape[0]
    seq = x.shape[1]
    me = 4 * lax.axis_index("x") + 2 * lax.axis_index("y") + lax.axis_index("c")
    ada_cols = ada_w.shape[2]

    def after_token(t, token):
        return t + token[0:1, 0:1].astype(t.dtype)

    packs = [_pack_shard_layer(l, wts) for l in range(nl)]

    c_all = _all_gather(jnp.broadcast_to(c, (8, D_MODEL)), "gather_c")[::8]

    ada_b_mine = lax.dynamic_slice_in_dim(ada_b, me * ada_cols, ada_cols, axis=1).reshape(nl, 1, ada_cols)
    mod_part = _ada_mod(c_all, ada_w, ada_b_mine, "ada_mod")
    mod_all = _all_gather(mod_part.reshape(nl * N_DEV, ada_cols), "gather_mod")
    mod_all = mod_all.reshape(N_DEV, nl, N_DEV, ada_cols)
    mod = lax.dynamic_index_in_dim(mod_all, me, axis=2, keepdims=False)
    mod = mod.transpose(1, 0, 2).reshape(nl, N_MOD, 1, D_MODEL)

    flight_a = _gather_start(packs[0][:SPLIT_AB], ROWS_A, mod, "gather_start_0a")
    flight_b = _gather_start(packs[0][SPLIT_AB:], ROWS_B, flight_a[4], "gather_start_0b")
    last_start = flight_b[4]
    if nl > 1:
        in_flight = _gather_start(packs[1], ROWS_ALL, last_start, "gather_start_1")
        last_start = in_flight[4]

    cos, sin = _rope_tables(after_token(positions.reshape(seq, 1), last_start), "rope_tables")

    def vec(t):
        return t.reshape(1, -1)

    def landed(flight, rows_list, after, tag):
        send_sems, recv_sems, pk, lands, _ = flight
        pk, lands = _gather_wait(send_sems, recv_sems, pk, lands, after, f"gather_wait_{tag}")
        return _gather_finish(pk, rows_list, lands, "gather_finish")

    xs = x.reshape(seq, D_MODEL)
    saved = []
    for l in range(nl):
        norm1 = vec(ffn1_norm[l])
        if l == 0:
            lands = landed(flight_a, ROWS_A, cos, "0a")
        elif l + 1 < nl:
            in_flight = _gather_start(packs[l + 1], ROWS_ALL, lands[0], f"gather_start_{l + 1}")
            norm1 = after_token(norm1, in_flight[4])
        sv = {}

        def ffn_fwd(xin, norm, k0, wg, wu, wd, tag):
            h, a, b, t = _ffn_up(xin, norm, mod[l, k0], mod[l, k0 + 1], wg, wu, "ffn_up")
            y, xout = _mm(t, wd, "nn", "ffn_down", res=xin, gate=mod[l, k0 + 2], gate_factor=0.5)
            sv[tag] = dict(x=xin, h=h, a=a, b=b, t=t, y=y)
            return xout

        xs = ffn_fwd(xs, norm1, 0, lands[0], lands[1], lands[2], "f1")
        if l == 0:
            lands = lands + landed(flight_b, ROWS_B, xs, "0b")
        w = _full_weights(lands)
        sv["w"] = w

        h2, z = _norm_mm(xs, vec(mix_norm[l]), mod[l, 3], mod[l, 4], w["in"], "mix_in")
        y_pool, diff = _pool_fwd(z, pool_w[l], vec(pool_scale[l]), "pool_fwd")
        q, k, v, cqn, ckvn = _qkv_fwd(z, vec(q_a_norm[l]), vec(kv_a_norm[l]), w["q"], w["kv"], cos, sin, "qkv_fwd")
        o, lse = _attn_fwd(q, k, v, "attn_fwd")
        ycat = jnp.concatenate([y_pool, o.astype(BF16)], axis=1)
        y2, xmix = _mm(ycat, w["out"], "nn", "mix_out", res=xs, gate=mod[l, 5], gate_factor=1.0)
        sv["mix"] = dict(x=xs, h=h2, z=z, diff=diff, q=q, k=k, v=v, cqn=cqn, ckvn=ckvn, lse=lse, ycat=ycat, y=y2)
        xs = xmix

        xs = ffn_fwd(xs, vec(ffn2_norm[l]), 6, w["g2"], w["u2"], w["d2"], "f2")
        saved.append(sv)
        if l + 1 < nl:
            lands = landed(in_flight, ROWS_ALL, xs, l + 1)

    loss_part, dx, d_final = _loss_head(xs, vec(final_norm), loss_target.reshape(seq, D_MODEL), "loss_head")

    small = {name: [None] * nl for name in ("ffn1_norm", "mix_norm", "ffn2_norm", "q_a_norm", "kv_a_norm",
                                            "pool_scale", "pool_w", "dmod")}
    core = lax.axis_index("c").astype(jnp.int32).reshape(1)
    chip = 2 * lax.axis_index("x") + lax.axis_index("y")
    exchanges = []

    def leave(srcs, rows_list, after, tag):
        return _pair_start(srcs, rows_list, after, f"pair_start_{tag}"), rows_list, tag

    def forward_on(pending, after, layer, row_off):
        (send_sems, recv_sems, srcs, land, _), rows_list, tag = pending
        srcs, land = _split_wait(send_sems, recv_sems, 1, srcs, land, after, f"pair_wait_{tag}")
        sums = _pair_sum(srcs, rows_list, land, core, "pair_sum")
        flight = _chip_exchange_start(sums, chip, after, f"exchange_start_{tag}")
        exchanges.append((flight, layer, row_off, tag))
        return flight[4]

    pending = None
    head = _gate_bwd(dx, saved[nl - 1]["f2"]["y"], mod[nl - 1, 8], 0.5, "gate_bwd")
    for l in reversed(range(nl)):
        sv = saved[l]
        w = sv["w"]
        dmod = [None] * N_MOD
        gr = {}

        def ffn_bwd(dxin, head, s_, norm, k0, wg, wu, wd, tag, below, first_after=None, mid=None):
            dy, dmod[k0 + 2] = head
            da, db, gr["d" + tag], gr["g" + tag], gr["u" + tag] = _ffn_bwd_cols(
                dy, s_["h"], s_["a"], s_["b"], s_["t"], wd, "ffn_bwd_cols", after=first_after)
            dh = _mm_pair(da, wg, db, wu, "ffn_bwd_dh", after=None if mid is None else mid(da))
            outs = _rm_bwd(dh, s_["x"], dxin, vec(norm), mod[l, k0 + 1], "rm_bwd", below=below)
            dmod[k0], dmod[k0 + 1] = outs[1], outs[2]
            return outs[0], outs[3], outs[4:]

        s_ = sv["mix"]
        dx, small["ffn2_norm"][l], head = ffn_bwd(
            dx, head, sv["f2"], ffn2_norm[l], 6, w["g2"], w["u2"], w["d2"], "2", (s_["y"], mod[l, 5], 1.0),
            first_after=None if pending is None else pending[0][4])

        mix_after = None
        if pending is not None:
            mix_after = forward_on(pending, dx, l + 1, 0)
            pending = None
        dy, dmod[5] = head
        gr["out"] = _mm(s_["ycat"], dy, "tn", "mix_out_dw", out_dtype=BF16, tm=256, after=mix_after)
        dycat = _mm(dy, w["out"], "nt", "mix_out_dx")
        du, small["pool_w"][l], small["pool_scale"][l] = _pool_bwd(dycat, s_["diff"], pool_w[l], vec(pool_scale[l]), "pool_bwd")
        dq, dk, dv = _attn_bwd(s_["q"], s_["k"], s_["v"], s_["lse"], dycat, "attn_bwd")
        dz, dqb, dkvb, small["q_a_norm"][l], small["kv_a_norm"][l] = _qkv_bwd(
            dq, dk, dv, du, s_["z"], vec(q_a_norm[l]), vec(kv_a_norm[l]), w["q"], w["kv"], cos, sin, "qkv_bwd")
        gr["q"] = _mm(dqb, s_["cqn"], "tn", "q_b_dw", out_dtype=BF16, tm=256)
        gr["kv"] = _mm(dkvb, s_["ckvn"], "tn", "kv_b_dw", out_dtype=BF16, tm=256)
        gr["in"] = _mm(dz, s_["h"], "tn", "mix_in_dw", out_dtype=BF16, tm=256)
        dh2 = _mm(dz, w["in"], "nn", "mix_in_dx")
        outs = _rm_bwd(dh2, s_["x"], dx, vec(mix_norm[l]), mod[l, 4], "rm_bwd", below=(sv["f1"]["y"], mod[l, 2], 0.5))
        dx, dmod[3], dmod[4], small["mix_norm"][l] = outs[:4]
        head = outs[4:]

        first_after, mid = None, None
        if l == 0:
            pending_b = leave(_grad_sources_b(gr), ROWS_B, dx, "0b")
            first_after = pending_b[0][4]
            mid = lambda da: forward_on(pending_b, da, 0, SPLIT_AB)
        below = (saved[l - 1]["f2"]["y"], mod[l - 1, 8], 0.5) if l > 0 else None
        dx, small["ffn1_norm"][l], head = ffn_bwd(
            dx, head, sv["f1"], ffn1_norm[l], 0, w["g1"], w["u1"], w["d1"], "1", below, first_after, mid)

        small["dmod"][l] = jnp.concatenate(dmod, axis=0)
        if l > 0:
            pending = leave([gr["g1"], gr["u1"], gr["d1"]] + _grad_sources_b(gr), ROWS_ALL, dx, l)

    grad_x = dx.reshape(x.shape)
    pending_a = leave([gr["g1"], gr["u1"], gr["d1"]], ROWS_A, dx, "0a")

    layout, small_rows = _small_layout(nl)
    pieces = {
        "dmod": jnp.concatenate(small["dmod"], axis=0),
        "ffn1_norm": jnp.concatenate(small["ffn1_norm"], axis=0),
        "mix_norm": jnp.concatenate(small["mix_norm"], axis=0),
        "ffn2_norm": jnp.concatenate(small["ffn2_norm"], axis=0),
        "q_a_norm": jnp.concatenate(small["q_a_norm"], axis=0),
        "kv_a_norm": jnp.concatenate(small["kv_a_norm"], axis=0),
        "pool_scale": jnp.concatenate(small["pool_scale"], axis=0),
        "final_norm": d_final,
        "loss": jnp.broadcast_to(loss_part[0:1, 0:1], (1, D_MODEL)),
        "pool_w": _pack_bf16_pairs(jnp.stack(small["pool_w"]).reshape(-1, D_MODEL)),
    }
    small_buf = jnp.concatenate([_to_rows(pieces[name]) for name in layout], axis=0)
    def landed_sums(gbuf, entries, after):
        for (send_sems, recv_sems, sums, recv, _), layer, row_off, tag in entries:
            _, recv = _split_wait(send_sems, recv_sems, N_CHIPS - 1, sums, recv, after, f"exchange_wait_{tag}")
            gbuf = _sum_slots_into(recv, gbuf, layer, row_off, "sum_grads")
        return gbuf

    gbuf = lax.empty((nl, ROWS_L, D_MODEL), F32)
    spread = _spread_start(small_buf, me, pending_a[0][4], "small_start")
    gbuf = landed_sums(gbuf, [e for e in exchanges if e[1] > 0], spread[4])
    token_0a = forward_on(pending_a, gbuf if nl > 1 else spread[4], 0, 0)
    gbuf = landed_sums(gbuf, [e for e in exchanges if e[3] == "0b"], token_0a)

    def swap(t):
        return t.transpose(0, 2, 1)

    def same(t):
        return t

    grads, updates = {}, {}

    def update_rows(gbuf, table):
        for wname, off, view in table:
            g, d_, nm, nv = _adamw_rows(view(wts[wname]), gbuf, off, view(mom_m[wname]), view(mom_v[wname]), "adamw_rows")
            grads[wname], updates[wname] = view(g), (view(d_), view(nm), view(nv))

    update_rows(gbuf, (("ffn2_w_gate", OFF_G2, swap), ("ffn2_w_up", OFF_U2, swap), ("ffn2_w_down", OFF_D2, same),
                       ("w_out", OFF_OUT, same)))
    small_grads = {
        "w_kv_b": (gbuf[:, OFF_KV:OFF_KV + KV_SH_ROWS].reshape(nl, -1, KV_LORA).transpose(0, 2, 1), same),
        "w_in": (gbuf[:, OFF_IN:OFF_IN + IN_SH], swap),
        "w_q_b": (gbuf[:, OFF_Q:OFF_Q + Q_SH_ROWS].reshape(nl, -1, Q_LORA), swap),
    }
    for wname, (g, view) in small_grads.items():
        upd = _adamw_nd(view(wts[wname]), g, view(mom_m[wname]), view(mom_v[wname]), "adamw")
        grads[wname], updates[wname] = view(g), tuple(view(t) for t in upd)

    _, small_all = _split_wait(spread[0], spread[1], N_DEV - 1, spread[2], spread[3], updates["w_q_b"][0],
                               "small_wait")
    pool_off, pool_rows = layout["pool_w"]
    small_sum = _sum_slots(small_all[:, :pool_off], "sum_small")
    pool_sum = _sum_slots(_unpack_bf16_pairs(small_all[:, pool_off:pool_off + pool_rows]), "sum_pool_w")

    def take(name, width=D_MODEL):
        off, n = layout[name]
        return small_sum[off:off + n, :width]

    late = {"ada_b": take("dmod").reshape(nl, N_MOD * D_MODEL),
            "ffn1_norm": take("ffn1_norm"), "mix_norm": take("mix_norm"), "ffn2_norm": take("ffn2_norm"),
            "q_a_norm": take("q_a_norm", Q_LORA), "kv_a_norm": take("kv_a_norm", KV_LORA),
            "pool_scale": take("pool_scale", POOL_WIDTH), "final_norm": take("final_norm").reshape(D_MODEL),
            "pool_w": pool_sum.reshape(pool_w.shape)}
    loss = take("loss")[0, 0]

    off, n = layout["dmod"]
    dmod_all = small_all[:, off:off + n].reshape(N_DEV, nl, N_MOD * D_MODEL)
    dmod_mine = lax.dynamic_slice_in_dim(dmod_all, me * ada_cols, ada_cols, axis=2)
    dmod_pad = jnp.pad(dmod_mine.transpose(1, 0, 2), ((0, 0), (0, LANE - N_DEV), (0, 0)))
    late["ada_w"] = _ada_grad(jnp.pad(c_all, ((0, LANE - N_DEV), (0, 0))), dmod_pad, "ada_grad")
    for name, g in late.items():
        grads[name], updates[name] = g, _adamw_nd(wts[name], g, mom_m[name], mom_v[name], "adamw")

    gbuf = landed_sums(gbuf, [e for e in exchanges if e[3] == "0a"], updates["ada_w"][0])
    update_rows(gbuf, (("ffn1_w_gate", OFF_G1, swap), ("ffn1_w_up", OFF_U1, swap), ("ffn1_w_down", OFF_D1, same)))

    return (loss, grad_x, *[grads[n] for n in order], *[updates[n][0] for n in order],
            *[updates[n][1] for n in order], *[updates[n][2] for n in order])
```

```python
import math

import numpy as np
import jax
import jax.numpy as jnp
from jax import lax
from jax.experimental import pallas as pl
from jax.experimental.pallas import tpu as pltpu

F32 = jnp.float32
BF16 = jnp.bfloat16

N_DEV = 8
D_MODEL = 1024
D_FF = 2816
POOL_WIDTH = 512
POOL_WINDOWS = (2, 4, 8, 16)
POOL_GC = 128
N_HEADS = 4
QK_NOPE = 128
QK_ROPE = 64
V_HEAD = 128
QK_HEAD = QK_NOPE + QK_ROPE
HEAD_PAD = 256
Q_LORA = 384
KV_LORA = 256
IN_COLS = POOL_WIDTH + Q_LORA + KV_LORA + QK_ROPE
IN_PAD = 1280
ROPE_THETA = 10000.0
SOFTMAX_SCALE = 1.0 / math.sqrt(QK_HEAD)
EPS = 1e-6
N_MOD = 9

ADAM_LR = 0.001
ADAM_B1 = 0.9
ADAM_B2 = 0.999
ADAM_EPS = 1e-08
ADAM_WD = 0.01
ADAM_STEP = 10

LANE = 128
VMEM_LIMIT = 56 * 1024 * 1024

FF_SH = D_FF // N_DEV
OFF_G1, OFF_U1, OFF_D1 = 0, FF_SH, 2 * FF_SH
OFF_G2, OFF_U2, OFF_D2 = 3 * FF_SH, 4 * FF_SH, 5 * FF_SH
OFF_OUT = 6 * FF_SH
OFF_KV = OFF_OUT + 128
OFF_IN = OFF_KV + 32
OFF_Q = OFF_IN + 160
Q_PAD_ROWS = 64
ROWS_L = OFF_Q + Q_PAD_ROWS
IN_SH = IN_COLS // N_DEV
Q_SH_ROWS = (N_HEADS * QK_HEAD // N_DEV) * Q_LORA // D_MODEL
KV_SH_ROWS = (N_HEADS * (QK_NOPE + V_HEAD) // N_DEV) * KV_LORA // D_MODEL


def _tile(dim, target):
    if dim <= target:
        return dim
    best = None
    for t in range(LANE, target + 1, LANE):
        if dim % t == 0:
            best = t
    assert best is not None, (dim, target)
    return best


def _params(sem):
    return pltpu.CompilerParams(dimension_semantics=sem, vmem_limit_bytes=VMEM_LIMIT)


def _mesh_pos():
    return lax.axis_index("x"), lax.axis_index("y"), lax.axis_index("c")


def _all_gather(x, name):
    m, n = x.shape

    def body(x_ref, out_ref, send_sems, recv_sems, local_sem):
        px, py, pc = _mesh_pos()
        me, sibling = (px, py, pc), (px, py, 1 - pc)
        chips = [(1 - px, py), (px, 1 - py), (1 - px, 1 - py)]

        def rows(bx, by, bc):
            return out_ref.at[pl.ds((4 * bx + 2 * by + bc) * m, m), :]

        def copy(k, block, to, src=None):
            return pltpu.make_async_remote_copy(
                src_ref=rows(*block) if src is None else src, dst_ref=rows(*block),
                send_sem=send_sems.at[k], recv_sem=recv_sems.at[k],
                device_id=to, device_id_type=pl.DeviceIdType.MESH)

        mine = pltpu.make_async_copy(x_ref, rows(*me), local_sem)
        mine.start()
        first = [copy(0, me, sibling, src=x_ref)]
        first += [copy(1 + j, me, (*chip, pc), src=x_ref) for j, chip in enumerate(chips)]
        for cp in first:
            cp.start()
        passed = [copy(4 + j, (*chip, pc), sibling) for j, chip in enumerate(chips)]
        for j, chip in enumerate(chips):
            copy(1 + j, (*chip, pc), me).wait_recv()
            passed[j].start()
        copy(0, sibling, me).wait_recv()
        for j, chip in enumerate(chips):
            copy(4 + j, (*chip, 1 - pc), me).wait_recv()
        for cp in first + passed:
            cp.wait_send()
        mine.wait()

    hbm = pl.BlockSpec(memory_space=pltpu.HBM)
    return pl.pallas_call(
        body, name=name,
        out_shape=jax.ShapeDtypeStruct((N_DEV * m, n), x.dtype),
        in_specs=[hbm], out_specs=hbm,
        scratch_shapes=[pltpu.SemaphoreType.DMA((7,)), pltpu.SemaphoreType.DMA((7,)),
                        pltpu.SemaphoreType.DMA],
    )(x)


SMALL_ROWS = ROWS_L - OFF_KV
ROWS_A = [FF_SH] * 3
ROWS_B = [FF_SH] * 3 + [128, SMALL_ROWS]
ROWS_ALL = ROWS_A + ROWS_B
SPLIT_AB = sum(ROWS_A)
HBM_SPEC = pl.BlockSpec(memory_space=pltpu.HBM)
SEM_SPEC = pl.BlockSpec(memory_space=pltpu.SEMAPHORE)
ANY_SPEC = pl.BlockSpec(memory_space=pl.ANY)
EFFECT = pltpu.SideEffectType.DATAFLOW_SIDE_EFFECTING


def _hbm(t):
    return pltpu.with_memory_space_constraint(t, pltpu.HBM)


def _whole_wait(ref, send_sem, recv_sem, peer):
    return pltpu.make_async_remote_copy(src_ref=ref, dst_ref=ref, send_sem=send_sem, recv_sem=recv_sem,
                                        device_id=peer, device_id_type=pl.DeviceIdType.MESH)


def _offsets(rows_list):
    return [sum(rows_list[:i]) for i in range(len(rows_list))]


def _gather_start(packed, rows_list, after, name):
    n = len(rows_list)
    offs = _offsets(rows_list)
    lands = [_hbm(lax.empty((N_DEV * rows, D_MODEL), BF16)) for rows in rows_list]

    def body(packed_ref, *refs):
        land = refs[:n]
        send_sems, recv_sems = refs[n + 1], refs[n + 2]
        token = refs[-1]
        px, py, pc = _mesh_pos()
        me = 4 * px + 2 * py + pc
        peers = [(px, py, 1 - pc), (1 - px, py, pc), (px, 1 - py, pc), (1 - px, 1 - py, pc)]
        for k, peer in enumerate(peers):
            for off, rows, land_ref in zip(offs, rows_list, land):
                pltpu.make_async_remote_copy(
                    src_ref=packed_ref.at[pl.ds(off, rows), :], dst_ref=land_ref.at[pl.ds(me * rows, rows), :],
                    send_sem=send_sems.at[k], recv_sem=recv_sems.at[k],
                    device_id=peer, device_id_type=pl.DeviceIdType.MESH).start()
        token[...] = jnp.zeros_like(token)

    outs = pl.pallas_call(
        body, name=name,
        out_shape=(pltpu.SemaphoreType.DMA((4,)), pltpu.SemaphoreType.DMA((4,)), pltpu.HBM(packed.shape, BF16),
                   *[pltpu.HBM(t.shape, BF16) for t in lands], jax.ShapeDtypeStruct((8, LANE), F32)),
        in_specs=(HBM_SPEC,) * (1 + n) + (ANY_SPEC,),
        out_specs=(SEM_SPEC, SEM_SPEC) + (HBM_SPEC,) * (1 + n) + (pl.BlockSpec(memory_space=pltpu.VMEM),),
        input_output_aliases={i: 2 + i for i in range(1 + n)},
        compiler_params=pltpu.CompilerParams(has_side_effects=EFFECT),
    )(_hbm(packed), *lands, after)
    return outs[0], outs[1], outs[2], list(outs[3:3 + n]), outs[-1]


def _gather_wait(send_sems, recv_sems, packed, lands, after, name):
    n = len(lands)

    def body(packed_ref, *refs):
        s_sems, r_sems = refs[n], refs[n + 1]
        me = _mesh_pos()
        for k in range(4):
            cp = _whole_wait(packed_ref, s_sems.at[k], r_sems.at[k], me)
            cp.wait_send()
            cp.wait_recv()

    outs = pl.pallas_call(
        body, name=name,
        out_shape=(pltpu.HBM(packed.shape, BF16), *[pltpu.HBM(t.shape, BF16) for t in lands]),
        in_specs=(HBM_SPEC,) * (1 + n) + (SEM_SPEC, SEM_SPEC, ANY_SPEC),
        out_specs=(HBM_SPEC,) * (1 + n),
        input_output_aliases={i: i for i in range(1 + n)},
        compiler_params=pltpu.CompilerParams(has_side_effects=EFFECT),
    )(packed, *lands, send_sems, recv_sems, after)
    return outs[0], list(outs[1:])


def _gather_finish(packed, rows_list, lands, name):
    n = len(rows_list)
    offs = _offsets(rows_list)

    def body(packed_ref, *refs):
        land = refs[n:2 * n]
        send_sems, recv_sems, stage, stage_sem = refs[2 * n:]
        px, py, pc = _mesh_pos()
        me = 4 * px + 2 * py + pc
        sibling = (px, py, 1 - pc)
        load = pltpu.make_async_copy(packed_ref, stage, stage_sem)
        load.start()
        load.wait()
        for off, rows, land_ref in zip(offs, rows_list, land):
            pltpu.make_async_copy(stage.at[pl.ds(off, rows), :], land_ref.at[pl.ds(me * rows, rows), :],
                                  stage_sem).start()
        for j, (cx, cy) in enumerate([(1 - px, py), (px, 1 - py), (1 - px, 1 - py)]):
            block = 4 * cx + 2 * cy + pc
            for rows, land_ref in zip(rows_list, land):
                blk = land_ref.at[pl.ds(block * rows, rows), :]
                pltpu.make_async_remote_copy(src_ref=blk, dst_ref=blk, send_sem=send_sems.at[j],
                                             recv_sem=recv_sems.at[j], device_id=sibling,
                                             device_id_type=pl.DeviceIdType.MESH).start()
        for j in range(3):
            cp = _whole_wait(packed_ref, send_sems.at[j], recv_sems.at[j], sibling)
            cp.wait_recv()
            cp.wait_send()
        pltpu.make_async_copy(stage, packed_ref, stage_sem).wait()

    outs = pl.pallas_call(
        body, name=name,
        out_shape=tuple(jax.ShapeDtypeStruct(t.shape, BF16) for t in lands),
        in_specs=(HBM_SPEC,) * (1 + n), out_specs=(HBM_SPEC,) * n,
        input_output_aliases={1 + i: i for i in range(n)},
        scratch_shapes=[pltpu.SemaphoreType.DMA((3,)), pltpu.SemaphoreType.DMA((3,)),
                        pltpu.VMEM(packed.shape, BF16), pltpu.SemaphoreType.DMA],
    )(packed, *lands)
    return list(outs)


N_CHIPS = 4


def _pair_start(srcs, rows_list, after, name):
    n = len(rows_list)
    offs = _offsets(rows_list)
    land = lax.empty((N_CHIPS, sum(rows_list), D_MODEL), BF16)

    def body(*refs):
        src, land_ref = refs[:n], refs[n]
        send_sems, recv_sems = refs[n + 2], refs[n + 3]
        token = refs[-1]
        px, py, pc = _mesh_pos()
        for k in range(N_CHIPS):
            block = 2 * k + (1 - pc)
            for off, rows, src_ref in zip(offs, rows_list, src):
                pltpu.make_async_remote_copy(
                    src_ref=src_ref.at[pl.ds(block * rows, rows), :], dst_ref=land_ref.at[k, pl.ds(off, rows), :],
                    send_sem=send_sems.at[0], recv_sem=recv_sems.at[0],
                    device_id=(px, py, 1 - pc), device_id_type=pl.DeviceIdType.MESH).start()
        token[...] = jnp.zeros_like(token)

    outs = pl.pallas_call(
        body, name=name,
        out_shape=(pltpu.SemaphoreType.DMA((1,)), pltpu.SemaphoreType.DMA((1,)),
                   *[pltpu.HBM(t.shape, BF16) for t in srcs], pltpu.HBM(land.shape, BF16),
                   jax.ShapeDtypeStruct((8, LANE), F32)),
        in_specs=(HBM_SPEC,) * (n + 1) + (ANY_SPEC,),
        out_specs=(SEM_SPEC, SEM_SPEC) + (HBM_SPEC,) * (n + 1) + (pl.BlockSpec(memory_space=pltpu.VMEM),),
        input_output_aliases={i: 2 + i for i in range(n + 1)},
        compiler_params=pltpu.CompilerParams(has_side_effects=EFFECT),
    )(*[_hbm(t) for t in srcs], _hbm(land), after)
    return outs[0], outs[1], list(outs[2:2 + n]), outs[2 + n], outs[-1]


def _split_wait(send_sems, recv_sems, n_sems, srcs, land, after, name):
    n = len(srcs)

    def body(*refs):
        land_ref = refs[n]
        s_sems, r_sems = refs[n + 1], refs[n + 2]
        me = _mesh_pos()
        for k in range(n_sems):
            cp = _whole_wait(land_ref.at[0] if n_sems > 1 else land_ref, s_sems.at[k], r_sems.at[k], me)
            cp.wait_send()
            cp.wait_recv()

    outs = pl.pallas_call(
        body, name=name,
        out_shape=(*[pltpu.HBM(t.shape, t.dtype) for t in srcs], pltpu.HBM(land.shape, land.dtype)),
        in_specs=(HBM_SPEC,) * (n + 1) + (SEM_SPEC, SEM_SPEC, ANY_SPEC),
        out_specs=(HBM_SPEC,) * (n + 1),
        input_output_aliases={i: i for i in range(n + 1)},
        compiler_params=pltpu.CompilerParams(has_side_effects=EFFECT),
    )(*srcs, land, send_sems, recv_sems, after)
    return list(outs[:n]), outs[n]


def _spread_start(x, me_id, after, name):
    land = lax.dynamic_update_slice_in_dim(lax.empty((N_DEV,) + x.shape, x.dtype), x[None], me_id, axis=0)

    def body(x_ref, land_ref, after_ref, send_sems, recv_sems, x_thru, land_thru, token):
        px, py, pc = _mesh_pos()
        me = 4 * px + 2 * py + pc
        for k in range(1, N_DEV):
            qx = 1 - px if k & 4 else px
            qy = 1 - py if k & 2 else py
            qc = 1 - pc if k & 1 else pc
            pltpu.make_async_remote_copy(
                src_ref=x_ref, dst_ref=land_ref.at[me], send_sem=send_sems.at[k - 1], recv_sem=recv_sems.at[k - 1],
                device_id=(qx, qy, qc), device_id_type=pl.DeviceIdType.MESH).start()
        token[...] = jnp.zeros_like(token)

    outs = pl.pallas_call(
        body, name=name,
        out_shape=(pltpu.SemaphoreType.DMA((N_DEV - 1,)), pltpu.SemaphoreType.DMA((N_DEV - 1,)),
                   pltpu.HBM(x.shape, x.dtype), pltpu.HBM(land.shape, land.dtype), jax.ShapeDtypeStruct((8, LANE), F32)),
        in_specs=(HBM_SPEC, HBM_SPEC, ANY_SPEC),
        out_specs=(SEM_SPEC, SEM_SPEC, HBM_SPEC, HBM_SPEC, pl.BlockSpec(memory_space=pltpu.VMEM)),
        input_output_aliases={0: 2, 1: 3},
        compiler_params=pltpu.CompilerParams(has_side_effects=EFFECT),
    )(_hbm(x), _hbm(land), after)
    return outs[0], outs[1], [outs[2]], outs[3], outs[4]


def _pair_sum(srcs, rows_list, land, core, name):
    n = len(rows_list)
    offs = _offsets(rows_list)
    total = sum(rows_list)

    def body(core_ref, *refs):
        src, land_ref, out_ref = refs[:n], refs[n], refs[n + 1]
        for off, rows, src_ref in zip(offs, rows_list, src):
            out_ref[pl.ds(off, rows), :] = (src_ref[...].astype(F32)
                                            + land_ref[pl.ds(off, rows), :].astype(F32)).astype(BF16)

    slot = pl.BlockSpec((None, total, D_MODEL), lambda k, c: (k, 0, 0))
    grid_spec = pltpu.PrefetchScalarGridSpec(
        num_scalar_prefetch=1, grid=(N_CHIPS,),
        in_specs=[pl.BlockSpec((rows, D_MODEL), lambda k, c: (2 * k + c[0], 0)) for rows in rows_list] + [slot],
        out_specs=slot)
    return pl.pallas_call(
        body, name=name, grid_spec=grid_spec,
        out_shape=jax.ShapeDtypeStruct((N_CHIPS, total, D_MODEL), BF16),
        compiler_params=_params(("parallel",)),
    )(core, *srcs, land)


def _chip_exchange_start(sums, chip, after, name):
    own = lax.dynamic_index_in_dim(sums, chip, axis=0, keepdims=True)
    recv = lax.dynamic_update_slice_in_dim(lax.empty(sums.shape, BF16), own, chip, axis=0)

    def body(sums_ref, recv_ref, after_ref, send_sems, recv_sems, sums_thru, recv_thru, token):
        px, py, pc = _mesh_pos()
        for k in range(1, N_CHIPS):
            qx = 1 - px if k & 2 else px
            qy = 1 - py if k & 1 else py
            pltpu.make_async_remote_copy(
                src_ref=sums_ref.at[2 * qx + qy], dst_ref=recv_ref.at[2 * px + py],
                send_sem=send_sems.at[k - 1], recv_sem=recv_sems.at[k - 1],
                device_id=(qx, qy, pc), device_id_type=pl.DeviceIdType.MESH).start()
        token[...] = jnp.zeros_like(token)

    outs = pl.pallas_call(
        body, name=name,
        out_shape=(pltpu.SemaphoreType.DMA((N_CHIPS - 1,)), pltpu.SemaphoreType.DMA((N_CHIPS - 1,)),
                   pltpu.HBM(sums.shape, BF16), pltpu.HBM(recv.shape, BF16), jax.ShapeDtypeStruct((8, LANE), F32)),
        in_specs=(HBM_SPEC, HBM_SPEC, ANY_SPEC),
        out_specs=(SEM_SPEC, SEM_SPEC, HBM_SPEC, HBM_SPEC, pl.BlockSpec(memory_space=pltpu.VMEM)),
        input_output_aliases={0: 2, 1: 3},
        compiler_params=pltpu.CompilerParams(has_side_effects=EFFECT),
    )(_hbm(sums), _hbm(recv), after)
    return outs[0], outs[1], [outs[2]], outs[3], outs[4]


def _sum_slots_into(recv, buf, layer, row_off, name):
    slots, r, n = recv.shape
    tr = _row_tile(math.gcd(r, row_off) if row_off else r, 512)
    first = row_off // tr

    def body(in_ref, buf_ref, out_ref):
        acc = in_ref[0].astype(F32)
        for j in range(1, slots):
            acc = acc + in_ref[j].astype(F32)
        out_ref[...] = acc

    return pl.pallas_call(
        body, name=name, grid=(r // tr,), out_shape=jax.ShapeDtypeStruct(buf.shape, F32),
        in_specs=[pl.BlockSpec((slots, tr, n), lambda i: (0, i, 0)), ANY_SPEC],
        out_specs=pl.BlockSpec((None, tr, n), lambda i: (layer, first + i, 0)),
        input_output_aliases={1: 0},
        compiler_params=_params(("parallel",)),
    )(recv, buf)


def _sum_slots(recv, name, after=None):
    _, r, n = recv.shape
    tr = _row_tile(r, 512)

    def body(in_ref, *refs):
        acc = in_ref[0].astype(F32)
        for j in range(1, N_DEV):
            acc = acc + in_ref[j].astype(F32)
        refs[-1][...] = acc

    grid = (r // tr,)
    in_specs, out_spec = [pl.BlockSpec((N_DEV, tr, n), lambda i: (0, i, 0))], pl.BlockSpec((tr, n), lambda i: (i, 0))
    args = [recv]
    if after is not None:
        in_specs.append(ANY_SPEC)
        args.append(after)
    return pl.pallas_call(
        body, name=name, grid=grid,
        out_shape=jax.ShapeDtypeStruct((r, n), F32),
        in_specs=in_specs, out_specs=out_spec,
        compiler_params=_params(("parallel",)),
    )(*args)


def _row_tile(rows, target):
    if rows <= target:
        return rows
    best = None
    for t in range(16, target + 1, 16):
        if rows % t == 0:
            best = t
    assert best is not None, rows
    return best


_DIMS = {"nn": ((1,), (0,)), "nt": ((1,), (1,)), "tn": ((0,), (0,))}


def _mm(a, b, mode, name, out_dtype=F32, res=None, gate=None, gate_factor=1.0, tm=512, tn=1408, after=None):
    assert (res is None) == (gate is None)
    if mode == "tn":
        kdim, m = a.shape
    else:
        m, kdim = a.shape
    n = b.shape[0] if mode == "nt" else b.shape[1]
    tm, tn = _tile(m, tm), _tile(n, tn)
    a_spec = (pl.BlockSpec((kdim, tm), lambda i, j: (0, i)) if mode == "tn"
              else pl.BlockSpec((tm, kdim), lambda i, j: (i, 0)))
    b_spec = (pl.BlockSpec((tn, kdim), lambda i, j: (j, 0)) if mode == "nt"
              else pl.BlockSpec((kdim, tn), lambda i, j: (0, j)))
    o_spec = pl.BlockSpec((tm, tn), lambda i, j: (i, j))
    dims = (_DIMS[mode], ((), ()))
    has_res = res is not None

    def body(a_ref, b_ref, *refs):
        y = lax.dot_general(a_ref[...].astype(BF16), b_ref[...].astype(BF16), dims,
                            preferred_element_type=F32)
        if has_res:
            res_ref, gate_ref = refs[0], refs[1]
            y_ref, o_ref = refs[-2], refs[-1]
            y_ref[...] = y.astype(BF16)
            o_ref[...] = res_ref[...] + (gate_factor * gate_ref[...]) * y
        else:
            refs[-1][...] = y.astype(out_dtype)

    in_specs, args = [a_spec, b_spec], [a, b]
    if has_res:
        in_specs += [o_spec, pl.BlockSpec((1, tn), lambda i, j: (0, j))]
        args += [res, gate]
        out_shape = (jax.ShapeDtypeStruct((m, n), BF16), jax.ShapeDtypeStruct((m, n), F32))
        out_specs = (o_spec, o_spec)
    else:
        out_shape, out_specs = jax.ShapeDtypeStruct((m, n), out_dtype), o_spec
    if after is not None:
        in_specs.append(ANY_SPEC)
        args.append(after)
    return pl.pallas_call(
        body, name=name, grid=(m // tm, n // tn), out_shape=out_shape,
        in_specs=in_specs, out_specs=out_specs,
        compiler_params=_params(("parallel", "parallel")),
    )(*args)


def _vec_spec(width):
    return pl.BlockSpec((1, width), lambda i: (0, 0))


def _rm_bwd(dh, x, dres, gw, scale, name, below=None):
    s, d = x.shape
    ts = _tile(s, 256)
    factor = None if below is None else below[2]

    def body(dh_ref, x_ref, dres_ref, gw_ref, sc_ref, *refs):
        dx_ref, dsh_ref, dsc_ref, dgw_ref = refs[-6:-2] if below is not None else refs[-4:]

        @pl.when(pl.program_id(0) == 0)
        def _():
            dsh_ref[...] = jnp.zeros_like(dsh_ref)
            dsc_ref[...] = jnp.zeros_like(dsc_ref)
            dgw_ref[...] = jnp.zeros_like(dgw_ref)
            if below is not None:
                refs[-1][...] = jnp.zeros_like(refs[-1])

        xv, dhv, gwv = x_ref[...], dh_ref[...], gw_ref[...]
        r = lax.rsqrt(jnp.mean(xv * xv, axis=-1, keepdims=True) + EPS)
        xn = xv * r
        y = xn * gwv
        dsh_ref[...] += jnp.sum(dhv, axis=0, keepdims=True)
        dsc_ref[...] += jnp.sum(dhv * y, axis=0, keepdims=True)
        dy = dhv * (1 + sc_ref[...])
        dgw_ref[...] += jnp.sum(dy * xn, axis=0, keepdims=True)
        dxn = dy * gwv
        dx = dres_ref[...] + r * (dxn - xn * jnp.mean(dxn * xn, axis=-1, keepdims=True))
        dx_ref[...] = dx
        if below is not None:
            yb_ref, gb_ref, dyb_ref, dgb_ref = refs[0], refs[1], refs[-2], refs[-1]
            dyb_ref[...] = ((factor * gb_ref[...]) * dx).astype(BF16)
            dgb_ref[...] += jnp.sum((factor * dx) * yb_ref[...].astype(F32), axis=0, keepdims=True)

    row = pl.BlockSpec((ts, d), lambda i: (i, 0))
    vec = jax.ShapeDtypeStruct((1, d), F32)
    in_specs, args = [row, row, row, _vec_spec(d), _vec_spec(d)], [dh, x, dres, gw, scale]
    out_shape = [jax.ShapeDtypeStruct((s, d), F32), vec, vec, vec]
    out_specs = [row, _vec_spec(d), _vec_spec(d), _vec_spec(d)]
    if below is not None:
        in_specs += [row, _vec_spec(d)]
        args += [below[0], below[1]]
        out_shape += [jax.ShapeDtypeStruct((s, d), BF16), vec]
        out_specs += [row, _vec_spec(d)]
    return pl.pallas_call(
        body, name=name, grid=(s // ts,), out_shape=tuple(out_shape),
        in_specs=in_specs, out_specs=tuple(out_specs),
        compiler_params=_params(("arbitrary",)),
    )(*args)


def _gate_bwd(dx, y, gate, factor, name):
    s, d = dx.shape
    ts = _tile(s, 256)

    def body(dx_ref, y_ref, g_ref, dy_ref, dg_ref):
        @pl.when(pl.program_id(0) == 0)
        def _():
            dg_ref[...] = jnp.zeros_like(dg_ref)

        dxv = dx_ref[...]
        dy_ref[...] = ((factor * g_ref[...]) * dxv).astype(BF16)
        dg_ref[...] += jnp.sum((factor * dxv) * y_ref[...].astype(F32), axis=0, keepdims=True)

    row = pl.BlockSpec((ts, d), lambda i: (i, 0))
    return pl.pallas_call(
        body, name=name, grid=(s // ts,),
        out_shape=(jax.ShapeDtypeStruct((s, d), BF16), jax.ShapeDtypeStruct((1, d), F32)),
        in_specs=[row, row, _vec_spec(d)], out_specs=(row, _vec_spec(d)),
        compiler_params=_params(("arbitrary",)),
    )(dx, y, gate)


def _norm_mm(x, gw, shift, scale, w, name, tm=1024):
    s, d = x.shape
    n = w.shape[0]
    tm = _tile(s, tm)

    def body(x_ref, gw_ref, sh_ref, sc_ref, w_ref, h_ref, z_ref):
        xv = x_ref[...]
        r = lax.rsqrt(jnp.mean(xv * xv, axis=-1, keepdims=True) + EPS)
        hb = (((xv * r) * gw_ref[...]) * (1 + sc_ref[...]) + sh_ref[...]).astype(BF16)
        h_ref[...] = hb
        z_ref[...] = lax.dot_general(hb, w_ref[...], (((1,), (1,)), ((), ())), preferred_element_type=F32)

    row = pl.BlockSpec((tm, d), lambda i: (i, 0))
    return pl.pallas_call(
        body, name=name, grid=(s // tm,),
        out_shape=(jax.ShapeDtypeStruct((s, d), BF16), jax.ShapeDtypeStruct((s, n), F32)),
        in_specs=[row, _vec_spec(d), _vec_spec(d), _vec_spec(d), pl.BlockSpec((n, d), lambda i: (0, 0))],
        out_specs=(row, pl.BlockSpec((tm, n), lambda i: (i, 0))),
        compiler_params=_params(("parallel",)),
    )(x, gw, shift, scale, w)


FFN_TM, FFN_TF = 2048, 256


def _ffn_up(x, gw, shift, scale, wg, wu, name):
    s, d = x.shape
    f = wg.shape[0]
    tm, tf = _tile(s, FFN_TM), _tile(f, FFN_TF)
    nt = (((1,), (1,)), ((), ()))

    def body(x_ref, gw_ref, sh_ref, sc_ref, wg_ref, wu_ref, h_ref, a_ref, b_ref, t_ref):
        @pl.when(pl.program_id(1) == 0)
        def _():
            xv = x_ref[...]
            r = lax.rsqrt(jnp.mean(xv * xv, axis=-1, keepdims=True) + EPS)
            h_ref[...] = (((xv * r) * gw_ref[...]) * (1 + sc_ref[...]) + sh_ref[...]).astype(BF16)

        hb = h_ref[...]
        av = lax.dot_general(hb, wg_ref[...], nt, preferred_element_type=F32)
        bv = lax.dot_general(hb, wu_ref[...], nt, preferred_element_type=F32)
        a_ref[...] = av.astype(BF16)
        b_ref[...] = bv.astype(BF16)
        t_ref[...] = ((av * jax.nn.sigmoid(av)) * bv).astype(BF16)

    row = pl.BlockSpec((tm, d), lambda i, j: (i, 0))
    vec = pl.BlockSpec((1, d), lambda i, j: (0, 0))
    wblk = pl.BlockSpec((tf, d), lambda i, j: (j, 0))
    blk = pl.BlockSpec((tm, tf), lambda i, j: (i, j))
    wide = jax.ShapeDtypeStruct((s, f), BF16)
    return pl.pallas_call(
        body, name=name, grid=(s // tm, f // tf),
        out_shape=(jax.ShapeDtypeStruct((s, d), BF16), wide, wide, wide),
        in_specs=[row, vec, vec, vec, wblk, wblk], out_specs=(row, blk, blk, blk),
        compiler_params=_params(("parallel", "arbitrary")),
    )(x, gw, shift, scale, wg, wu)


def _ffn_bwd_cols(dy, h, a, b, t, wd, name, after=None):
    s, d = dy.shape
    f = wd.shape[0]
    tf = _tile(f, FFN_TF)
    nt = (((1,), (1,)), ((), ()))
    tn = (((0,), (0,)), ((), ()))

    def body(dy_ref, h_ref, a_ref, b_ref, t_ref, wd_ref, *refs):
        da_ref, db_ref, gd_ref, gg_ref, gu_ref = refs[-5:]
        dyb, hb = dy_ref[...], h_ref[...]
        dtv = lax.dot_general(dyb, wd_ref[...], nt, preferred_element_type=F32)
        av, bv = a_ref[...].astype(F32), b_ref[...].astype(F32)
        sg = jax.nn.sigmoid(av)
        dbv = (dtv * (av * sg)).astype(BF16)
        dav = ((dtv * bv) * (sg * (1 + av * (1 - sg)))).astype(BF16)
        da_ref[...] = dav
        db_ref[...] = dbv
        gd_ref[...] = lax.dot_general(t_ref[...], dyb, tn, preferred_element_type=F32).astype(BF16)
        gg_ref[...] = lax.dot_general(dav, hb, tn, preferred_element_type=F32).astype(BF16)
        gu_ref[...] = lax.dot_general(dbv, hb, tn, preferred_element_type=F32).astype(BF16)

    whole = pl.BlockSpec((s, d), lambda j: (0, 0))
    col = pl.BlockSpec((s, tf), lambda j: (0, j))
    wblk = pl.BlockSpec((tf, d), lambda j: (j, 0))
    wide, wgrad = jax.ShapeDtypeStruct((s, f), BF16), jax.ShapeDtypeStruct((f, d), BF16)
    in_specs, args = [whole, whole, col, col, col, wblk], [dy, h, a, b, t, wd]
    if after is not None:
        in_specs.append(ANY_SPEC)
        args.append(after)
    return pl.pallas_call(
        body, name=name, grid=(f // tf,), out_shape=(wide, wide, wgrad, wgrad, wgrad),
        in_specs=in_specs, out_specs=(col, col, wblk, wblk, wblk),
        compiler_params=_params(("parallel",)),
    )(*args)


def _mm_pair(a1, b1, a2, b2, name, tm=1024, tn=512, after=None):
    m, kdim = a1.shape
    n = b1.shape[1]
    tm, tn = _tile(m, tm), _tile(n, tn)

    def body(a1_ref, b1_ref, a2_ref, b2_ref, *refs):
        refs[-1][...] = (jnp.dot(a1_ref[...], b1_ref[...], preferred_element_type=F32)
                         + jnp.dot(a2_ref[...], b2_ref[...], preferred_element_type=F32))

    a_spec = pl.BlockSpec((tm, kdim), lambda i, j: (i, 0))
    b_spec = pl.BlockSpec((kdim, tn), lambda i, j: (0, j))
    in_specs, args = [a_spec, b_spec, a_spec, b_spec], [a1, b1, a2, b2]
    if after is not None:
        in_specs.append(ANY_SPEC)
        args.append(after)
    return pl.pallas_call(
        body, name=name, grid=(m // tm, n // tn), out_shape=jax.ShapeDtypeStruct((m, n), F32),
        in_specs=in_specs, out_specs=pl.BlockSpec((tm, tn), lambda i, j: (i, j)),
        compiler_params=_params(("parallel", "parallel")),
    )(*args)


def _pool_counts(s):
    return (lax.broadcasted_iota(jnp.int32, (s, POOL_GC), 0))


def _pool_fwd(z, pool_w, pool_scale, name):
    s = z.shape[0]

    def body(u_ref, w_ref, sc_ref, y_ref, diff_ref):
        t = lax.broadcasted_iota(jnp.int32, (s, POOL_GC), 0)
        for g, win in enumerate(POOL_WINDOWS):
            cols = slice(g * POOL_GC, (g + 1) * POOL_GC)
            u = u_ref[:, cols]
            acc, step = u, 1
            while step < win:
                acc = acc + jnp.where(t >= step, pltpu.roll(acc, step, 0), 0.0)
                step *= 2
            cnt = jnp.minimum(t + 1, win).astype(F32)
            diff = acc / cnt - u
            diff_ref[:, cols] = diff
            ypre = jnp.dot(diff.astype(BF16), w_ref[g].astype(BF16), preferred_element_type=F32)
            y_ref[:, cols] = (ypre * sc_ref[:, cols]).astype(BF16)

    return pl.pallas_call(
        body, name=name, grid=(1,),
        out_shape=(jax.ShapeDtypeStruct((s, POOL_WIDTH), BF16), jax.ShapeDtypeStruct((s, POOL_WIDTH), F32)),
        in_specs=[pl.BlockSpec((s, POOL_WIDTH), lambda i: (0, 0)),
                  pl.BlockSpec(pool_w.shape, lambda i: (0, 0, 0)),
                  pl.BlockSpec((1, POOL_WIDTH), lambda i: (0, 0))],
        out_specs=(pl.BlockSpec((s, POOL_WIDTH), lambda i: (0, 0)),
                   pl.BlockSpec((s, POOL_WIDTH), lambda i: (0, 0))),
        compiler_params=_params(("arbitrary",)),
    )(z, pool_w, pool_scale)


def _pool_bwd(dycat, diff, pool_w, pool_scale, name):
    s = diff.shape[0]

    def body(dy_ref, diff_ref, w_ref, sc_ref, du_ref, dw_ref, dsc_ref):
        t = lax.broadcasted_iota(jnp.int32, (s, POOL_GC), 0)
        for g, win in enumerate(POOL_WINDOWS):
            cols = slice(g * POOL_GC, (g + 1) * POOL_GC)
            dy, dfb, wb = dy_ref[:, cols], diff_ref[:, cols].astype(BF16), w_ref[g].astype(BF16)
            ypre = jnp.dot(dfb, wb, preferred_element_type=F32)
            dsc_ref[:, cols] = jnp.sum(dy * ypre, axis=0, keepdims=True)
            dypre = (dy * sc_ref[:, cols]).astype(BF16)
            ddiff = lax.dot_general(dypre, wb, (((1,), (1,)), ((), ())), preferred_element_type=F32)
            dw_ref[g] = lax.dot_general(dfb, dypre, (((0,), (0,)), ((), ())), preferred_element_type=F32)
            cnt = jnp.minimum(t + 1, win).astype(F32)
            acc, step = ddiff / cnt, 1
            while step < win:
                acc = acc + jnp.where(t < s - step, pltpu.roll(acc, s - step, 0), 0.0)
                step *= 2
            du_ref[:, cols] = acc - ddiff

    full = pl.BlockSpec((s, POOL_WIDTH), lambda i: (0, 0))
    return pl.pallas_call(
        body, name=name, grid=(1,),
        out_shape=(jax.ShapeDtypeStruct((s, POOL_WIDTH), F32),
                   jax.ShapeDtypeStruct(pool_w.shape, F32),
                   jax.ShapeDtypeStruct((1, POOL_WIDTH), F32)),
        in_specs=[full, full, pl.BlockSpec(pool_w.shape, lambda i: (0, 0, 0)),
                  pl.BlockSpec((1, POOL_WIDTH), lambda i: (0, 0))],
        out_specs=(full, pl.BlockSpec(pool_w.shape, lambda i: (0, 0, 0)),
                   pl.BlockSpec((1, POOL_WIDTH), lambda i: (0, 0))),
        compiler_params=_params(("arbitrary",)),
    )(dycat, diff, pool_w, pool_scale)


def _rope_tables(positions, name):
    s = positions.shape[0]
    ts = _tile(s, 512)
    freq = 1.0 / (ROPE_THETA ** (np.arange(0, QK_ROPE, 2, dtype=np.float32) / QK_ROPE))
    table = np.zeros((1, LANE), np.float32)
    table[0, :QK_ROPE // 2] = freq
    table[0, QK_ROPE // 2:QK_ROPE] = freq

    def body(pos_ref, f_ref, cos_ref, sin_ref):
        ang = pos_ref[...].astype(F32) * f_ref[...]
        cos_ref[...] = jnp.cos(ang)
        sin_ref[...] = jnp.sin(ang)

    out = jax.ShapeDtypeStruct((s, LANE), F32)
    blk = pl.BlockSpec((ts, LANE), lambda i: (i, 0))
    return pl.pallas_call(
        body, name=name, grid=(s // ts,), out_shape=(out, out),
        in_specs=[pl.BlockSpec((ts, 1), lambda i: (i, 0)), _vec_spec(LANE)], out_specs=(blk, blk),
        compiler_params=_params(("parallel",)),
    )(positions, jnp.asarray(table))


def _lane_mod64_low(shape):
    return (lax.broadcasted_iota(jnp.int32, shape, 1) % QK_ROPE) < (QK_ROPE // 2)


def _rope(x, cos, sin):
    rot = jnp.where(_lane_mod64_low(x.shape), -pltpu.roll(x, LANE - 32, 1), pltpu.roll(x, 32, 1))
    return x * cos + rot * sin


def _rope_t(dy, cos, sin):
    w = dy * sin
    rot_t = jnp.where(_lane_mod64_low(dy.shape), pltpu.roll(w, LANE - 32, 1), -pltpu.roll(w, 32, 1))
    return dy * cos + rot_t


def _plain_rms(x, g):
    r = lax.rsqrt(jnp.mean(x * x, axis=-1, keepdims=True) + EPS)
    return (x * r) * g, x * r, r


O_Q, O_KV, O_KR = POOL_WIDTH, POOL_WIDTH + Q_LORA, POOL_WIDTH + Q_LORA + KV_LORA


def _qkv_fwd(z, qn, kvn, wq, wkv, cos, sin, name):
    s = z.shape[0]
    ts = _tile(s, 256)

    def body(z_ref, qn_ref, kvn_ref, wq_ref, wkv_ref, cos_ref, sin_ref, q_ref, k_ref, v_ref, cqn_ref, ckvn_ref):
        cosv, sinv = cos_ref[...], sin_ref[...]
        cqn = _plain_rms(z_ref[:, O_Q:O_KV], qn_ref[...])[0].astype(BF16)
        ckvn = _plain_rms(z_ref[:, O_KV:O_KR], kvn_ref[...])[0].astype(BF16)
        cqn_ref[...] = cqn
        ckvn_ref[...] = ckvn
        nt = (((1,), (1,)), ((), ()))
        q = lax.dot_general(cqn, wq_ref[...], nt, preferred_element_type=F32)
        kv = lax.dot_general(ckvn, wkv_ref[...], nt, preferred_element_type=F32)
        kr = _rope(z_ref[:, O_KR:IN_PAD], cosv, sinv).astype(BF16)
        for h in range(N_HEADS):
            o = h * HEAD_PAD
            q_ref[:, o:o + QK_NOPE] = q[:, o:o + QK_NOPE].astype(BF16)
            q_ref[:, o + QK_NOPE:o + HEAD_PAD] = _rope(q[:, o + QK_NOPE:o + HEAD_PAD], cosv, sinv).astype(BF16)
            k_ref[:, o:o + QK_NOPE] = kv[:, o:o + QK_NOPE].astype(BF16)
            k_ref[:, o + QK_NOPE:o + HEAD_PAD] = kr
            v_ref[:, h * V_HEAD:(h + 1) * V_HEAD] = kv[:, o + QK_NOPE:o + HEAD_PAD].astype(BF16)

    def row(w):
        return pl.BlockSpec((ts, w), lambda i: (i, 0))

    def whole(arr):
        return pl.BlockSpec(arr.shape, lambda i: (0, 0))

    hp = N_HEADS * HEAD_PAD
    return pl.pallas_call(
        body, name=name, grid=(s // ts,),
        out_shape=(jax.ShapeDtypeStruct((s, hp), BF16), jax.ShapeDtypeStruct((s, hp), BF16),
                   jax.ShapeDtypeStruct((s, N_HEADS * V_HEAD), BF16),
                   jax.ShapeDtypeStruct((s, Q_LORA), BF16), jax.ShapeDtypeStruct((s, KV_LORA), BF16)),
        in_specs=[row(IN_PAD), whole(qn), whole(kvn), whole(wq), whole(wkv), row(LANE), row(LANE)],
        out_specs=(row(hp), row(hp), row(N_HEADS * V_HEAD), row(Q_LORA), row(KV_LORA)),
        compiler_params=_params(("parallel",)),
    )(z, qn, kvn, wq, wkv, cos, sin)


def _qkv_bwd(dq, dk, dv, du, z, qn, kvn, wq, wkv, cos, sin, name):
    s = z.shape[0]
    ts = _tile(s, 256)

    def norm_bwd(x, g, dy):
        _, xn, r = _plain_rms(x, g)
        dxn = dy * g
        return r * (dxn - xn * jnp.mean(dxn * xn, axis=-1, keepdims=True)), jnp.sum(dy * xn, axis=0, keepdims=True)

    def body(dq_ref, dk_ref, dv_ref, du_ref, z_ref, qn_ref, kvn_ref, wq_ref, wkv_ref, cos_ref, sin_ref,
             dz_ref, dqb_ref, dkvb_ref, dqn_ref, dkvn_ref):
        @pl.when(pl.program_id(0) == 0)
        def _():
            dqn_ref[...] = jnp.zeros_like(dqn_ref)
            dkvn_ref[...] = jnp.zeros_like(dkvn_ref)

        cosv, sinv = cos_ref[...], sin_ref[...]
        dkr = jnp.zeros((ts, LANE), F32)
        for h in range(N_HEADS):
            o = h * HEAD_PAD
            dqb_ref[:, o:o + QK_NOPE] = dq_ref[:, o:o + QK_NOPE].astype(BF16)
            dqb_ref[:, o + QK_NOPE:o + HEAD_PAD] = _rope_t(dq_ref[:, o + QK_NOPE:o + HEAD_PAD], cosv, sinv).astype(BF16)
            dkvb_ref[:, o:o + QK_NOPE] = dk_ref[:, o:o + QK_NOPE].astype(BF16)
            dkvb_ref[:, o + QK_NOPE:o + HEAD_PAD] = dv_ref[:, h * V_HEAD:(h + 1) * V_HEAD].astype(BF16)
            dkr = dkr + dk_ref[:, o + QK_NOPE:o + HEAD_PAD]
        dcqn = jnp.dot(dqb_ref[...], wq_ref[...], preferred_element_type=F32)
        dckvn = jnp.dot(dkvb_ref[...], wkv_ref[...], preferred_element_type=F32)
        dcq, dqn = norm_bwd(z_ref[:, O_Q:O_KV], qn_ref[...], dcqn)
        dckv, dkvn = norm_bwd(z_ref[:, O_KV:O_KR], kvn_ref[...], dckvn)
        dqn_ref[...] += dqn
        dkvn_ref[...] += dkvn
        dz_ref[:, 0:O_Q] = du_ref[...].astype(BF16)
        dz_ref[:, O_Q:O_KV] = dcq.astype(BF16)
        dz_ref[:, O_KV:O_KR] = dckv.astype(BF16)
        dz_ref[:, O_KR:IN_PAD] = _rope_t(dkr, cosv, sinv).astype(BF16)

    def row(w):
        return pl.BlockSpec((ts, w), lambda i: (i, 0))

    def whole(arr):
        return pl.BlockSpec(arr.shape, lambda i: (0, 0))

    hp = N_HEADS * HEAD_PAD
    return pl.pallas_call(
        body, name=name, grid=(s // ts,),
        out_shape=(jax.ShapeDtypeStruct((s, IN_PAD), BF16), jax.ShapeDtypeStruct((s, hp), BF16),
                   jax.ShapeDtypeStruct((s, hp), BF16),
                   jax.ShapeDtypeStruct((1, Q_LORA), F32), jax.ShapeDtypeStruct((1, KV_LORA), F32)),
        in_specs=[row(hp), row(hp), row(N_HEADS * V_HEAD), row(POOL_WIDTH), row(IN_PAD),
                  whole(qn), whole(kvn), whole(wq), whole(wkv), row(LANE), row(LANE)],
        out_specs=(row(IN_PAD), row(hp), row(hp), whole(qn), whole(kvn)),
        compiler_params=_params(("arbitrary",)),
    )(dq, dk, dv, du, z, qn, kvn, wq, wkv, cos, sin)


def _causal_scores(q, k, i, tq, klen):
    sc = lax.dot_general(q, k, (((1,), (1,)), ((), ())), preferred_element_type=F32) * SOFTMAX_SCALE
    qpos = i * tq + lax.broadcasted_iota(jnp.int32, (tq, klen), 0)
    kpos = lax.broadcasted_iota(jnp.int32, (tq, klen), 1)
    return jnp.where(qpos >= kpos, sc, -jnp.inf)


ATTN_TQ = 512
ATTN_SEGMENTS = 4


def _by_key_prefix(i, nq, tq, compute):
    nseg = min(ATTN_SEGMENTS, nq)
    per = nq // nseg
    for r in range(nseg):
        pl.when(i // per == r)(lambda r=r: compute((r + 1) * per * tq))


def _attn_fwd(q, k, v, name):
    s = q.shape[0]
    tq = _tile(s, ATTN_TQ)
    nq = s // tq

    def body(q_ref, k_ref, v_ref, o_ref, lse_ref):
        i = pl.program_id(1)

        def compute(klen):
            sc = _causal_scores(q_ref[...], k_ref[0:klen, :], i, tq, klen)
            mx = jnp.max(sc, axis=-1, keepdims=True)
            p = jnp.exp(sc - mx)
            den = jnp.sum(p, axis=-1, keepdims=True)
            o_ref[...] = jnp.dot((p / den).astype(BF16), v_ref[0:klen, :], preferred_element_type=F32)
            lse_ref[...] = mx + jnp.log(den)

        _by_key_prefix(i, nq, tq, compute)

    return pl.pallas_call(
        body, name=name, grid=(N_HEADS, s // tq),
        out_shape=(jax.ShapeDtypeStruct((s, N_HEADS * V_HEAD), F32), jax.ShapeDtypeStruct((N_HEADS, s, 1), F32)),
        in_specs=[pl.BlockSpec((tq, HEAD_PAD), lambda h, i: (i, h)),
                  pl.BlockSpec((s, HEAD_PAD), lambda h, i: (0, h)),
                  pl.BlockSpec((s, V_HEAD), lambda h, i: (0, h))],
        out_specs=(pl.BlockSpec((tq, V_HEAD), lambda h, i: (i, h)),
                   pl.BlockSpec((None, tq, 1), lambda h, i: (h, i, 0))),
        compiler_params=_params(("parallel", "parallel")),
    )(q, k, v)


def _attn_bwd(q, k, v, lse, dycat, name):
    s = q.shape[0]
    tq = _tile(s, ATTN_TQ)
    nq = s // tq
    tn_dims = (((0,), (0,)), ((), ()))

    def body(q_ref, k_ref, v_ref, lse_ref, do_ref, dq_ref, dk_ref, dv_ref):
        i = pl.program_id(1)

        @pl.when(i == 0)
        def _():
            dk_ref[...] = jnp.zeros_like(dk_ref)
            dv_ref[...] = jnp.zeros_like(dv_ref)

        def compute(klen):
            qv, kv_, dob = q_ref[...], k_ref[0:klen, :], do_ref[...].astype(BF16)
            sc = _causal_scores(qv, kv_, i, tq, klen)
            p = jnp.exp(sc - lse_ref[...])
            dp = lax.dot_general(dob, v_ref[0:klen, :], (((1,), (1,)), ((), ())), preferred_element_type=F32)
            ds = (p * (dp - jnp.sum(dp * p, axis=-1, keepdims=True)) * SOFTMAX_SCALE).astype(BF16)
            dq_ref[...] = jnp.dot(ds, kv_, preferred_element_type=F32)
            dk_ref[0:klen, :] += lax.dot_general(ds, qv, tn_dims, preferred_element_type=F32)
            dv_ref[0:klen, :] += lax.dot_general(p.astype(BF16), dob, tn_dims, preferred_element_type=F32)

        _by_key_prefix(i, nq, tq, compute)

    n_pool_blocks = POOL_WIDTH // V_HEAD
    return pl.pallas_call(
        body, name=name, grid=(N_HEADS, s // tq),
        out_shape=(jax.ShapeDtypeStruct((s, N_HEADS * HEAD_PAD), F32),
                   jax.ShapeDtypeStruct((s, N_HEADS * HEAD_PAD), F32),
                   jax.ShapeDtypeStruct((s, N_HEADS * V_HEAD), F32)),
        in_specs=[pl.BlockSpec((tq, HEAD_PAD), lambda h, i: (i, h)),
                  pl.BlockSpec((s, HEAD_PAD), lambda h, i: (0, h)),
                  pl.BlockSpec((s, V_HEAD), lambda h, i: (0, h)),
                  pl.BlockSpec((None, tq, 1), lambda h, i: (h, i, 0)),
                  pl.BlockSpec((tq, V_HEAD), lambda h, i: (i, n_pool_blocks + h))],
        out_specs=(pl.BlockSpec((tq, HEAD_PAD), lambda h, i: (i, h)),
                   pl.BlockSpec((s, HEAD_PAD), lambda h, i: (0, h)),
                   pl.BlockSpec((s, V_HEAD), lambda h, i: (0, h))),
        compiler_params=_params(("parallel", "arbitrary")),
    )(q, k, v, lse, dycat)


def _loss_head(x, gw, target, name):
    s, d = x.shape
    ts = _tile(s, 256)

    def body(x_ref, gw_ref, tgt_ref, loss_ref, dx_ref, dgw_ref):
        @pl.when(pl.program_id(0) == 0)
        def _():
            loss_ref[...] = jnp.zeros_like(loss_ref)
            dgw_ref[...] = jnp.zeros_like(dgw_ref)

        xv, gwv = x_ref[...], gw_ref[...]
        r = lax.rsqrt(jnp.mean(xv * xv, axis=-1, keepdims=True) + EPS)
        xn = xv * r
        err = xn * gwv - tgt_ref[...]
        loss_ref[...] += 0.5 * jnp.sum(jnp.mean(err * err, axis=-1, keepdims=True))
        dy = err / d
        dgw_ref[...] += jnp.sum(dy * xn, axis=0, keepdims=True)
        dxn = dy * gwv
        dx_ref[...] = r * (dxn - xn * jnp.mean(dxn * xn, axis=-1, keepdims=True))

    row = pl.BlockSpec((ts, d), lambda i: (i, 0))
    return pl.pallas_call(
        body, name=name, grid=(s // ts,),
        out_shape=(jax.ShapeDtypeStruct((8, LANE), F32), jax.ShapeDtypeStruct((s, d), F32),
                   jax.ShapeDtypeStruct((1, d), F32)),
        in_specs=[row, _vec_spec(d), row],
        out_specs=(pl.BlockSpec((8, LANE), lambda i: (0, 0)), row, _vec_spec(d)),
        compiler_params=_params(("arbitrary",)),
    )(x, gw, target)


def _ada_mod(c_all, ada_w, ada_b, name):
    nl, d, cols = ada_w.shape

    def body(c_ref, w_ref, b_ref, o_ref):
        cv = c_ref[...]
        act = (cv * jax.nn.sigmoid(cv)).astype(BF16)
        o_ref[...] = jnp.dot(act, w_ref[...].astype(BF16), preferred_element_type=F32) + b_ref[...]

    return pl.pallas_call(
        body, name=name, grid=(nl,), out_shape=jax.ShapeDtypeStruct((nl, N_DEV, cols), F32),
        in_specs=[pl.BlockSpec((N_DEV, d), lambda l: (0, 0)),
                  pl.BlockSpec((None, d, cols), lambda l: (l, 0, 0)),
                  pl.BlockSpec((None, 1, cols), lambda l: (l, 0, 0))],
        out_specs=pl.BlockSpec((None, N_DEV, cols), lambda l: (l, 0, 0)),
        compiler_params=_params(("parallel",)),
    )(c_all, ada_w, ada_b)


def _ada_grad(c_pad, dmod_pad, name):
    nl, kpad, cols = dmod_pad.shape
    d = c_pad.shape[1]

    def body(c_ref, dm_ref, o_ref):
        cv = c_ref[...]
        act = (cv * jax.nn.sigmoid(cv)).astype(BF16)
        o_ref[...] = lax.dot_general(act, dm_ref[...].astype(BF16), (((0,), (0,)), ((), ())),
                                     preferred_element_type=F32)

    return pl.pallas_call(
        body, name=name, grid=(nl,), out_shape=jax.ShapeDtypeStruct((nl, d, cols), F32),
        in_specs=[pl.BlockSpec((kpad, d), lambda l: (0, 0)),
                  pl.BlockSpec((None, kpad, cols), lambda l: (l, 0, 0))],
        out_specs=pl.BlockSpec((None, d, cols), lambda l: (l, 0, 0)),
        compiler_params=_params(("parallel",)),
    )(c_pad, dmod_pad)


def _adamw_math(w, g, m, v):
    nm = ADAM_B1 * m + (1.0 - ADAM_B1) * g
    nv = ADAM_B2 * v + (1.0 - ADAM_B2) * (g * g)
    m_hat = nm / (1.0 - ADAM_B1 ** ADAM_STEP)
    v_hat = nv / (1.0 - ADAM_B2 ** ADAM_STEP)
    return -ADAM_LR * (m_hat / (jnp.sqrt(v_hat) + ADAM_EPS) + ADAM_WD * w), nm, nv


def _adamw_rows(w3, gbuf, row_off, m3, v3, name):
    nl, r, d = w3.shape
    tr = _row_tile(math.gcd(r, row_off) if row_off else r, 176)
    first = row_off // tr

    def body(w_ref, g_ref, m_ref, v_ref, go_ref, d_ref, nm_ref, nv_ref):
        gv = g_ref[...]
        go_ref[...] = gv
        d_ref[...], nm_ref[...], nv_ref[...] = _adamw_math(w_ref[...], gv, m_ref[...], v_ref[...])

    blk = pl.BlockSpec((None, tr, d), lambda l, i: (l, i, 0))
    gblk = pl.BlockSpec((None, tr, d), lambda l, i: (l, first + i, 0))
    out = jax.ShapeDtypeStruct((nl, r, d), F32)
    return pl.pallas_call(
        body, name=name, grid=(nl, r // tr), out_shape=(out, out, out, out),
        in_specs=[blk, gblk, blk, blk], out_specs=(blk, blk, blk, blk),
        compiler_params=_params(("parallel", "parallel")),
    )(w3, gbuf, m3, v3)


def _adamw(w, g, m, v, name):
    rows, cols = w.shape
    tr = _row_tile(rows, 512)

    def body(w_ref, g_ref, m_ref, v_ref, d_ref, nm_ref, nv_ref):
        d_ref[...], nm_ref[...], nv_ref[...] = _adamw_math(w_ref[...], g_ref[...], m_ref[...], v_ref[...])

    blk = pl.BlockSpec((tr, cols), lambda i: (i, 0))
    out = jax.ShapeDtypeStruct((rows, cols), F32)
    return pl.pallas_call(
        body, name=name, grid=(rows // tr,), out_shape=(out, out, out),
        in_specs=[blk, blk, blk, blk], out_specs=(blk, blk, blk),
        compiler_params=_params(("parallel",)),
    )(w, g, m, v)


def _adamw_nd(w, g, m, v, name):
    shape = w.shape
    flat = (lambda t: t.reshape(1, -1)) if w.ndim == 1 else (lambda t: t.reshape(-1, shape[-1]))
    return tuple(t.reshape(shape) for t in _adamw(flat(w), flat(g), flat(m), flat(v), name))


def _pad_rows(t, rows):
    return jnp.pad(t, ((0, rows - t.shape[0]), (0, 0)))


def _pack_shard_layer(l, wts):
    def tr(name):
        return wts[name][l].astype(BF16).T

    parts = [tr("ffn1_w_gate"), tr("ffn1_w_up"), wts["ffn1_w_down"][l].astype(BF16),
             tr("ffn2_w_gate"), tr("ffn2_w_up"), wts["ffn2_w_down"][l].astype(BF16),
             wts["w_out"][l].astype(BF16),
             tr("w_kv_b").reshape(KV_SH_ROWS, D_MODEL),
             _pad_rows(tr("w_in"), 160),
             _pad_rows(tr("w_q_b").reshape(Q_SH_ROWS, D_MODEL), Q_PAD_ROWS)]
    return jnp.concatenate(parts, axis=0)


def _full_weights(lands):
    w = dict(zip(("g1", "u1", "d1", "g2", "u2", "d2", "out"), lands))
    small = lands[-1].reshape(N_DEV, SMALL_ROWS, D_MODEL)
    o_in, o_q = OFF_IN - OFF_KV, OFF_Q - OFF_KV
    w["kv"] = small[:, :KV_SH_ROWS].reshape(N_HEADS * HEAD_PAD, KV_LORA)
    w["in"] = _pad_rows(small[:, o_in:o_in + IN_SH].reshape(IN_COLS, D_MODEL), IN_PAD)
    wq = small[:, o_q:o_q + Q_SH_ROWS].reshape(N_HEADS, QK_HEAD, Q_LORA)
    w["q"] = jnp.pad(wq, ((0, 0), (0, HEAD_PAD - QK_HEAD), (0, 0))).reshape(N_HEADS * HEAD_PAD, Q_LORA)
    return w


def _grad_sources_b(gr):
    gq = gr["q"].reshape(N_HEADS, HEAD_PAD, Q_LORA)[:, :QK_HEAD].reshape(N_DEV, Q_SH_ROWS, D_MODEL)
    small = jnp.concatenate([
        gr["kv"].reshape(N_DEV, KV_SH_ROWS, D_MODEL),
        jnp.pad(gr["in"][:IN_COLS].reshape(N_DEV, IN_SH, D_MODEL), ((0, 0), (0, 160 - IN_SH), (0, 0))),
        jnp.pad(gq, ((0, 0), (0, Q_PAD_ROWS - Q_SH_ROWS), (0, 0)))], axis=1)
    return [gr["g2"], gr["u2"], gr["d2"], gr["out"], small.reshape(N_DEV * SMALL_ROWS, D_MODEL)]


def _pack_bf16_pairs(t):
    rows, d = t.shape
    return lax.bitcast_convert_type(t.astype(BF16).reshape(rows // 2, 2, d).transpose(0, 2, 1), F32)


def _unpack_bf16_pairs(p):
    pairs = jnp.swapaxes(lax.bitcast_convert_type(p, BF16), -1, -2)
    return pairs.reshape(p.shape[:-2] + (2 * p.shape[-2], p.shape[-1]))


def _small_layout(nl):
    names = [("dmod", nl * N_MOD), ("ffn1_norm", nl), ("mix_norm", nl), ("ffn2_norm", nl), ("q_a_norm", nl),
             ("kv_a_norm", nl), ("pool_scale", nl), ("final_norm", 1), ("loss", 1),
             ("pool_w", nl * 4 * POOL_GC * POOL_GC // D_MODEL // 2)]
    off, table = 0, {}
    for name, n in names:
        table[name] = (off, n)
        off += -(-n // 8) * 8
    return table, off


def _to_rows(t, width=D_MODEL):
    n, w = t.shape
    return jnp.pad(t, ((0, -(-n // 8) * 8 - n), (0, width - w)))


def kernel(x, c, positions, ada_w, ada_b, ffn1_norm, ffn1_w_gate, ffn1_w_up, ffn1_w_down, mix_norm, w_in, pool_w, pool_scale, q_a_norm, w_q_b, kv_a_norm, w_kv_b, w_out, ffn2_norm, ffn2_w_gate, ffn2_w_up, ffn2_w_down, final_norm, loss_target, m_ada_w, m_ada_b, m_ffn1_norm, m_ffn1_w_gate, m_ffn1_w_up, m_ffn1_w_down, m_mix_norm, m_w_in, m_pool_w, m_pool_scale, m_q_a_norm, m_w_q_b, m_kv_a_norm, m_w_kv_b, m_w_out, m_ffn2_norm, m_ffn2_w_gate, m_ffn2_w_up, m_ffn2_w_down, m_final_norm, v_ada_w, v_ada_b, v_ffn1_norm, v_ffn1_w_gate, v_ffn1_w_up, v_ffn1_w_down, v_mix_norm, v_w_in, v_pool_w, v_pool_scale, v_q_a_norm, v_w_q_b, v_kv_a_norm, v_w_kv_b, v_w_out, v_ffn2_norm, v_ffn2_w_gate, v_ffn2_w_up, v_ffn2_w_down, v_final_norm):
    wts = dict(ada_w=ada_w, ada_b=ada_b, ffn1_norm=ffn1_norm, ffn1_w_gate=ffn1_w_gate, ffn1_w_up=ffn1_w_up,
               ffn1_w_down=ffn1_w_down, mix_norm=mix_norm, w_in=w_in, pool_w=pool_w, pool_scale=pool_scale,
               q_a_norm=q_a_norm, w_q_b=w_q_b, kv_a_norm=kv_a_norm, w_kv_b=w_kv_b, w_out=w_out,
               ffn2_norm=ffn2_norm, ffn2_w_gate=ffn2_w_gate, ffn2_w_up=ffn2_w_up, ffn2_w_down=ffn2_w_down,
               final_norm=final_norm)
    mom_m = dict(ada_w=m_ada_w, ada_b=m_ada_b, ffn1_norm=m_ffn1_norm, ffn1_w_gate=m_ffn1_w_gate,
                 ffn1_w_up=m_ffn1_w_up, ffn1_w_down=m_ffn1_w_down, mix_norm=m_mix_norm, w_in=m_w_in,
                 pool_w=m_pool_w, pool_scale=m_pool_scale, q_a_norm=m_q_a_norm, w_q_b=m_w_q_b,
                 kv_a_norm=m_kv_a_norm, w_kv_b=m_w_kv_b, w_out=m_w_out, ffn2_norm=m_ffn2_norm,
                 ffn2_w_gate=m_ffn2_w_gate, ffn2_w_up=m_ffn2_w_up, ffn2_w_down=m_ffn2_w_down,
                 final_norm=m_final_norm)
    mom_v = dict(ada_w=v_ada_w, ada_b=v_ada_b, ffn1_norm=v_ffn1_norm, ffn1_w_gate=v_ffn1_w_gate,
                 ffn1_w_up=v_ffn1_w_up, ffn1_w_down=v_ffn1_w_down, mix_norm=v_mix_norm, w_in=v_w_in,
                 pool_w=v_pool_w, pool_scale=v_pool_scale, q_a_norm=v_q_a_norm, w_q_b=v_w_q_b,
                 kv_a_norm=v_kv_a_norm, w_kv_b=v_w_kv_b, w_out=v_w_out, ffn2_norm=v_ffn2_norm,
                 ffn2_w_gate=v_ffn2_w_gate, ffn2_w_up=v_ffn2_w_up, ffn2_w_down=v_ffn2_w_down,
                 final_norm=v_final_norm)
    order = list(wts)
    nl = ada_w.shape[0]
    seq = x.shape[1]
    me = 4 * lax.axis_index("x") + 2 * lax.axis_index("y") + lax.axis_index("c")
    ada_cols = ada_w.shape[2]

    def after_token(t, token):
        return t + token[0:1, 0:1].astype(t.dtype)

    packs = [_pack_shard_layer(l, wts) for l in range(nl)]

    c_all = _all_gather(jnp.broadcast_to(c, (8, D_MODEL)), "gather_c")[::8]

    ada_b_mine = lax.dynamic_slice_in_dim(ada_b, me * ada_cols, ada_cols, axis=1).reshape(nl, 1, ada_cols)
    mod_part = _ada_mod(c_all, ada_w, ada_b_mine, "ada_mod")
    mod_all = _all_gather(mod_part.reshape(nl * N_DEV, ada_cols), "gather_mod")
    mod_all = mod_all.reshape(N_DEV, nl, N_DEV, ada_cols)
    mod = lax.dynamic_index_in_dim(mod_all, me, axis=2, keepdims=False)
    mod = mod.transpose(1, 0, 2).reshape(nl, N_MOD, 1, D_MODEL)

    flight_a = _gather_start(packs[0][:SPLIT_AB], ROWS_A, mod, "gather_start_0a")
    flight_b = _gather_start(packs[0][SPLIT_AB:], ROWS_B, flight_a[4], "gather_start_0b")
    last_start = flight_b[4]
    if nl > 1:
        in_flight = _gather_start(packs[1], ROWS_ALL, last_start, "gather_start_1")
        last_start = in_flight[4]

    cos, sin = _rope_tables(after_token(positions.reshape(seq, 1), last_start), "rope_tables")

    def vec(t):
        return t.reshape(1, -1)

    def landed(flight, rows_list, after, tag):
        send_sems, recv_sems, pk, lands, _ = flight
        pk, lands = _gather_wait(send_sems, recv_sems, pk, lands, after, f"gather_wait_{tag}")
        return _gather_finish(pk, rows_list, lands, "gather_finish")

    xs = x.reshape(seq, D_MODEL)
    saved = []
    for l in range(nl):
        norm1 = vec(ffn1_norm[l])
        if l == 0:
            lands = landed(flight_a, ROWS_A, cos, "0a")
        elif l + 1 < nl:
            in_flight = _gather_start(packs[l + 1], ROWS_ALL, lands[0], f"gather_start_{l + 1}")
            norm1 = after_token(norm1, in_flight[4])
        sv = {}

        def ffn_fwd(xin, norm, k0, wg, wu, wd, tag):
            h, a, b, t = _ffn_up(xin, norm, mod[l, k0], mod[l, k0 + 1], wg, wu, "ffn_up")
            y, xout = _mm(t, wd, "nn", "ffn_down", res=xin, gate=mod[l, k0 + 2], gate_factor=0.5)
            sv[tag] = dict(x=xin, h=h, a=a, b=b, t=t, y=y)
            return xout

        xs = ffn_fwd(xs, norm1, 0, lands[0], lands[1], lands[2], "f1")
        if l == 0:
            lands = lands + landed(flight_b, ROWS_B, xs, "0b")
        w = _full_weights(lands)
        sv["w"] = w

        h2, z = _norm_mm(xs, vec(mix_norm[l]), mod[l, 3], mod[l, 4], w["in"], "mix_in")
        y_pool, diff = _pool_fwd(z, pool_w[l], vec(pool_scale[l]), "pool_fwd")
        q, k, v, cqn, ckvn = _qkv_fwd(z, vec(q_a_norm[l]), vec(kv_a_norm[l]), w["q"], w["kv"], cos, sin, "qkv_fwd")
        o, lse = _attn_fwd(q, k, v, "attn_fwd")
        ycat = jnp.concatenate([y_pool, o.astype(BF16)], axis=1)
        y2, xmix = _mm(ycat, w["out"], "nn", "mix_out", res=xs, gate=mod[l, 5], gate_factor=1.0)
        sv["mix"] = dict(x=xs, h=h2, z=z, diff=diff, q=q, k=k, v=v, cqn=cqn, ckvn=ckvn, lse=lse, ycat=ycat, y=y2)
        xs = xmix

        xs = ffn_fwd(xs, vec(ffn2_norm[l]), 6, w["g2"], w["u2"], w["d2"], "f2")
        saved.append(sv)
        if l + 1 < nl:
            lands = landed(in_flight, ROWS_ALL, xs, l + 1)

    loss_part, dx, d_final = _loss_head(xs, vec(final_norm), loss_target.reshape(seq, D_MODEL), "loss_head")

    small = {name: [None] * nl for name in ("ffn1_norm", "mix_norm", "ffn2_norm", "q_a_norm", "kv_a_norm",
                                            "pool_scale", "pool_w", "dmod")}
    core = lax.axis_index("c").astype(jnp.int32).reshape(1)
    chip = 2 * lax.axis_index("x") + lax.axis_index("y")
    exchanges = []

    def leave(srcs, rows_list, after, tag):
        return _pair_start(srcs, rows_list, after, f"pair_start_{tag}"), rows_list, tag

    def forward_on(pending, after, layer, row_off):
        (send_sems, recv_sems, srcs, land, _), rows_list, tag = pending
        srcs, land = _split_wait(send_sems, recv_sems, 1, srcs, land, after, f"pair_wait_{tag}")
        sums = _pair_sum(srcs, rows_list, land, core, "pair_sum")
        flight = _chip_exchange_start(sums, chip, after, f"exchange_start_{tag}")
        exchanges.append((flight, layer, row_off, tag))
        return flight[4]

    pending = None
    head = _gate_bwd(dx, saved[nl - 1]["f2"]["y"], mod[nl - 1, 8], 0.5, "gate_bwd")
    for l in reversed(range(nl)):
        sv = saved[l]
        w = sv["w"]
        dmod = [None] * N_MOD
        gr = {}

        def ffn_bwd(dxin, head, s_, norm, k0, wg, wu, wd, tag, below, first_after=None, mid=None):
            dy, dmod[k0 + 2] = head
            da, db, gr["d" + tag], gr["g" + tag], gr["u" + tag] = _ffn_bwd_cols(
                dy, s_["h"], s_["a"], s_["b"], s_["t"], wd, "ffn_bwd_cols", after=first_after)
            dh = _mm_pair(da, wg, db, wu, "ffn_bwd_dh", after=None if mid is None else mid(da))
            outs = _rm_bwd(dh, s_["x"], dxin, vec(norm), mod[l, k0 + 1], "rm_bwd", below=below)
            dmod[k0], dmod[k0 + 1] = outs[1], outs[2]
            return outs[0], outs[3], outs[4:]

        s_ = sv["mix"]
        dx, small["ffn2_norm"][l], head = ffn_bwd(
            dx, head, sv["f2"], ffn2_norm[l], 6, w["g2"], w["u2"], w["d2"], "2", (s_["y"], mod[l, 5], 1.0),
            first_after=None if pending is None else pending[0][4])

        mix_after = None
        if pending is not None:
            mix_after = forward_on(pending, dx, l + 1, 0)
            pending = None
        dy, dmod[5] = head
        gr["out"] = _mm(s_["ycat"], dy, "tn", "mix_out_dw", out_dtype=BF16, tm=256, after=mix_after)
        dycat = _mm(dy, w["out"], "nt", "mix_out_dx")
        du, small["pool_w"][l], small["pool_scale"][l] = _pool_bwd(dycat, s_["diff"], pool_w[l], vec(pool_scale[l]), "pool_bwd")
        dq, dk, dv = _attn_bwd(s_["q"], s_["k"], s_["v"], s_["lse"], dycat, "attn_bwd")
        dz, dqb, dkvb, small["q_a_norm"][l], small["kv_a_norm"][l] = _qkv_bwd(
            dq, dk, dv, du, s_["z"], vec(q_a_norm[l]), vec(kv_a_norm[l]), w["q"], w["kv"], cos, sin, "qkv_bwd")
        gr["q"] = _mm(dqb, s_["cqn"], "tn", "q_b_dw", out_dtype=BF16, tm=256)
        gr["kv"] = _mm(dkvb, s_["ckvn"], "tn", "kv_b_dw", out_dtype=BF16, tm=256)
        gr["in"] = _mm(dz, s_["h"], "tn", "mix_in_dw", out_dtype=BF16, tm=256)
        dh2 = _mm(dz, w["in"], "nn", "mix_in_dx")
        outs = _rm_bwd(dh2, s_["x"], dx, vec(mix_norm[l]), mod[l, 4], "rm_bwd", below=(sv["f1"]["y"], mod[l, 2], 0.5))
        dx, dmod[3], dmod[4], small["mix_norm"][l] = outs[:4]
        head = outs[4:]

        first_after, mid = None, None
        if l == 0:
            pending_b = leave(_grad_sources_b(gr), ROWS_B, dx, "0b")
            first_after = pending_b[0][4]
            last_groups = []

            def mid(da):
                last_groups.append(leave([gr["g1"], gr["u1"], gr["d1"]], ROWS_A, da, "0a"))
                return forward_on(pending_b, last_groups[0][0][4], 0, SPLIT_AB)
        below = (saved[l - 1]["f2"]["y"], mod[l - 1, 8], 0.5) if l > 0 else None
        dx, small["ffn1_norm"][l], head = ffn_bwd(
            dx, head, sv["f1"], ffn1_norm[l], 0, w["g1"], w["u1"], w["d1"], "1", below, first_after, mid)

        small["dmod"][l] = jnp.concatenate(dmod, axis=0)
        if l > 0:
            pending = leave([gr["g1"], gr["u1"], gr["d1"]] + _grad_sources_b(gr), ROWS_ALL, dx, l)

    grad_x = dx.reshape(x.shape)
    pending_a = last_groups[0]

    layout, small_rows = _small_layout(nl)
    pieces = {
        "dmod": jnp.concatenate(small["dmod"], axis=0),
        "ffn1_norm": jnp.concatenate(small["ffn1_norm"], axis=0),
        "mix_norm": jnp.concatenate(small["mix_norm"], axis=0),
        "ffn2_norm": jnp.concatenate(small["ffn2_norm"], axis=0),
        "q_a_norm": jnp.concatenate(small["q_a_norm"], axis=0),
        "kv_a_norm": jnp.concatenate(small["kv_a_norm"], axis=0),
        "pool_scale": jnp.concatenate(small["pool_scale"], axis=0),
        "final_norm": d_final,
        "loss": jnp.broadcast_to(loss_part[0:1, 0:1], (1, D_MODEL)),
        "pool_w": _pack_bf16_pairs(jnp.stack(small["pool_w"]).reshape(-1, D_MODEL)),
    }
    small_buf = jnp.concatenate([_to_rows(pieces[name]) for name in layout], axis=0)
    def landed_sums(gbuf, entries, after):
        for (send_sems, recv_sems, sums, recv, _), layer, row_off, tag in entries:
            _, recv = _split_wait(send_sems, recv_sems, N_CHIPS - 1, sums, recv, after, f"exchange_wait_{tag}")
            gbuf = _sum_slots_into(recv, gbuf, layer, row_off, "sum_grads")
        return gbuf

    gbuf = lax.empty((nl, ROWS_L, D_MODEL), F32)
    token_0a = forward_on(pending_a, dx, 0, 0)
    spread = _spread_start(small_buf, me, token_0a, "small_start")
    gbuf = landed_sums(gbuf, [e for e in exchanges if e[1] > 0 or e[3] == "0b"], spread[4])

    def swap(t):
        return t.transpose(0, 2, 1)

    def same(t):
        return t

    grads, updates = {}, {}

    def update_rows(gbuf, table):
        for wname, off, view in table:
            g, d_, nm, nv = _adamw_rows(view(wts[wname]), gbuf, off, view(mom_m[wname]), view(mom_v[wname]), "adamw_rows")
            grads[wname], updates[wname] = view(g), (view(d_), view(nm), view(nv))

    update_rows(gbuf, (("ffn2_w_gate", OFF_G2, swap), ("ffn2_w_up", OFF_U2, swap), ("ffn2_w_down", OFF_D2, same),
                       ("w_out", OFF_OUT, same)))
    small_grads = {
        "w_kv_b": (gbuf[:, OFF_KV:OFF_KV + KV_SH_ROWS].reshape(nl, -1, KV_LORA).transpose(0, 2, 1), same),
        "w_in": (gbuf[:, OFF_IN:OFF_IN + IN_SH], swap),
        "w_q_b": (gbuf[:, OFF_Q:OFF_Q + Q_SH_ROWS].reshape(nl, -1, Q_LORA), swap),
    }
    for wname, (g, view) in small_grads.items():
        upd = _adamw_nd(view(wts[wname]), g, view(mom_m[wname]), view(mom_v[wname]), "adamw")
        grads[wname], updates[wname] = view(g), tuple(view(t) for t in upd)
    gbuf = landed_sums(gbuf, [e for e in exchanges if e[3] == "0a"], updates["w_q_b"][0])
    update_rows(gbuf, (("ffn1_w_gate", OFF_G1, swap), ("ffn1_w_up", OFF_U1, swap), ("ffn1_w_down", OFF_D1, same)))

    _, small_all = _split_wait(spread[0], spread[1], N_DEV - 1, spread[2], spread[3], updates["ffn1_w_down"][0],
                               "small_wait")
    pool_off, pool_rows = layout["pool_w"]
    small_sum = _sum_slots(small_all[:, :pool_off], "sum_small")
    pool_sum = _sum_slots(_unpack_bf16_pairs(small_all[:, pool_off:pool_off + pool_rows]), "sum_pool_w")

    def take(name, width=D_MODEL):
        off, n = layout[name]
        return small_sum[off:off + n, :width]

    late = {"ada_b": take("dmod").reshape(nl, N_MOD * D_MODEL),
            "ffn1_norm": take("ffn1_norm"), "mix_norm": take("mix_norm"), "ffn2_norm": take("ffn2_norm"),
            "q_a_norm": take("q_a_norm", Q_LORA), "kv_a_norm": take("kv_a_norm", KV_LORA),
            "pool_scale": take("pool_scale", POOL_WIDTH), "final_norm": take("final_norm").reshape(D_MODEL),
            "pool_w": pool_sum.reshape(pool_w.shape)}
    loss = take("loss")[0, 0]

    off, n = layout["dmod"]
    dmod_all = small_all[:, off:off + n].reshape(N_DEV, nl, N_MOD * D_MODEL)
    dmod_mine = lax.dynamic_slice_in_dim(dmod_all, me * ada_cols, ada_cols, axis=2)
    dmod_pad = jnp.pad(dmod_mine.transpose(1, 0, 2), ((0, 0), (0, LANE - N_DEV), (0, 0)))
    late["ada_w"] = _ada_grad(jnp.pad(c_all, ((0, LANE - N_DEV), (0, 0))), dmod_pad, "ada_grad")
    for name, g in late.items():
        grads[name], updates[name] = g, _adamw_nd(wts[name], g, mom_m[name], mom_v[name], "adamw")

    return (loss, grad_x, *[grads[n] for n in order], *[updates[n][0] for n in order],
            *[updates[n][1] for n in order], *[updates[n][2] for n in order])
```

```python
import math

import numpy as np
import jax
import jax.numpy as jnp
from jax import lax
from jax.experimental import pallas as pl
from jax.experimental.pallas import tpu as pltpu

F32 = jnp.float32
BF16 = jnp.bfloat16

N_DEV = 8
D_MODEL = 1024
D_FF = 2816
POOL_WIDTH = 512
POOL_WINDOWS = (2, 4, 8, 16)
POOL_GC = 128
N_HEADS = 4
QK_NOPE = 128
QK_ROPE = 64
V_HEAD = 128
QK_HEAD = QK_NOPE + QK_ROPE
HEAD_PAD = 256
Q_LORA = 384
KV_LORA = 256
IN_COLS = POOL_WIDTH + Q_LORA + KV_LORA + QK_ROPE
IN_PAD = 1280
ROPE_THETA = 10000.0
SOFTMAX_SCALE = 1.0 / math.sqrt(QK_HEAD)
EPS = 1e-6
N_MOD = 9

ADAM_LR = 0.001
ADAM_B1 = 0.9
ADAM_B2 = 0.999
ADAM_EPS = 1e-08
ADAM_WD = 0.01
ADAM_STEP = 10

LANE = 128
VMEM_LIMIT = 56 * 1024 * 1024

FF_SH = D_FF // N_DEV
OFF_G1, OFF_U1, OFF_D1 = 0, FF_SH, 2 * FF_SH
OFF_G2, OFF_U2, OFF_D2 = 3 * FF_SH, 4 * FF_SH, 5 * FF_SH
OFF_OUT = 6 * FF_SH
OFF_KV = OFF_OUT + 128
OFF_IN = OFF_KV + 32
OFF_Q = OFF_IN + 160
Q_PAD_ROWS = 64
ROWS_L = OFF_Q + Q_PAD_ROWS
IN_SH = IN_COLS // N_DEV
Q_SH_ROWS = (N_HEADS * QK_HEAD // N_DEV) * Q_LORA // D_MODEL
KV_SH_ROWS = (N_HEADS * (QK_NOPE + V_HEAD) // N_DEV) * KV_LORA // D_MODEL


def _tile(dim, target):
    if dim <= target:
        return dim
    best = None
    for t in range(LANE, target + 1, LANE):
        if dim % t == 0:
            best = t
    assert best is not None, (dim, target)
    return best


def _params(sem):
    return pltpu.CompilerParams(dimension_semantics=sem, vmem_limit_bytes=VMEM_LIMIT)


def _mesh_pos():
    return lax.axis_index("x"), lax.axis_index("y"), lax.axis_index("c")


def _all_gather(x, name):
    m, n = x.shape

    def body(x_ref, out_ref, send_sems, recv_sems, local_sem):
        px, py, pc = _mesh_pos()
        me, sibling = (px, py, pc), (px, py, 1 - pc)
        chips = [(1 - px, py), (px, 1 - py), (1 - px, 1 - py)]

        def rows(bx, by, bc):
            return out_ref.at[pl.ds((4 * bx + 2 * by + bc) * m, m), :]

        def copy(k, block, to, src=None):
            return pltpu.make_async_remote_copy(
                src_ref=rows(*block) if src is None else src, dst_ref=rows(*block),
                send_sem=send_sems.at[k], recv_sem=recv_sems.at[k],
                device_id=to, device_id_type=pl.DeviceIdType.MESH)

        mine = pltpu.make_async_copy(x_ref, rows(*me), local_sem)
        mine.start()
        first = [copy(0, me, sibling, src=x_ref)]
        first += [copy(1 + j, me, (*chip, pc), src=x_ref) for j, chip in enumerate(chips)]
        for cp in first:
            cp.start()
        passed = [copy(4 + j, (*chip, pc), sibling) for j, chip in enumerate(chips)]
        for j, chip in enumerate(chips):
            copy(1 + j, (*chip, pc), me).wait_recv()
            passed[j].start()
        copy(0, sibling, me).wait_recv()
        for j, chip in enumerate(chips):
            copy(4 + j, (*chip, 1 - pc), me).wait_recv()
        for cp in first + passed:
            cp.wait_send()
        mine.wait()

    hbm = pl.BlockSpec(memory_space=pltpu.HBM)
    return pl.pallas_call(
        body, name=name,
        out_shape=jax.ShapeDtypeStruct((N_DEV * m, n), x.dtype),
        in_specs=[hbm], out_specs=hbm,
        scratch_shapes=[pltpu.SemaphoreType.DMA((7,)), pltpu.SemaphoreType.DMA((7,)),
                        pltpu.SemaphoreType.DMA],
    )(x)


SMALL_ROWS = ROWS_L - OFF_KV
ROWS_A = [FF_SH] * 3
ROWS_B = [FF_SH] * 3 + [128, SMALL_ROWS]
ROWS_ALL = ROWS_A + ROWS_B
SPLIT_AB = sum(ROWS_A)
ROWS_TAIL = [128, SMALL_ROWS]
GB_F2, GB_F1, GB_TAIL = 0, SPLIT_AB, 2 * SPLIT_AB
HBM_SPEC = pl.BlockSpec(memory_space=pltpu.HBM)
SEM_SPEC = pl.BlockSpec(memory_space=pltpu.SEMAPHORE)
ANY_SPEC = pl.BlockSpec(memory_space=pl.ANY)
EFFECT = pltpu.SideEffectType.DATAFLOW_SIDE_EFFECTING


def _hbm(t):
    return pltpu.with_memory_space_constraint(t, pltpu.HBM)


def _whole_wait(ref, send_sem, recv_sem, peer):
    return pltpu.make_async_remote_copy(src_ref=ref, dst_ref=ref, send_sem=send_sem, recv_sem=recv_sem,
                                        device_id=peer, device_id_type=pl.DeviceIdType.MESH)


def _offsets(rows_list):
    return [sum(rows_list[:i]) for i in range(len(rows_list))]


def _gather_start(packed, rows_list, after, name):
    n = len(rows_list)
    offs = _offsets(rows_list)
    lands = [_hbm(lax.empty((N_DEV * rows, D_MODEL), BF16)) for rows in rows_list]

    def body(packed_ref, *refs):
        land = refs[:n]
        send_sems, recv_sems = refs[n + 1], refs[n + 2]
        token = refs[-1]
        px, py, pc = _mesh_pos()
        me = 4 * px + 2 * py + pc
        peers = [(px, py, 1 - pc), (1 - px, py, pc), (px, 1 - py, pc), (1 - px, 1 - py, pc)]
        for k, peer in enumerate(peers):
            for off, rows, land_ref in zip(offs, rows_list, land):
                pltpu.make_async_remote_copy(
                    src_ref=packed_ref.at[pl.ds(off, rows), :], dst_ref=land_ref.at[pl.ds(me * rows, rows), :],
                    send_sem=send_sems.at[k], recv_sem=recv_sems.at[k],
                    device_id=peer, device_id_type=pl.DeviceIdType.MESH).start()
        token[...] = jnp.zeros_like(token)

    outs = pl.pallas_call(
        body, name=name,
        out_shape=(pltpu.SemaphoreType.DMA((4,)), pltpu.SemaphoreType.DMA((4,)), pltpu.HBM(packed.shape, BF16),
                   *[pltpu.HBM(t.shape, BF16) for t in lands], jax.ShapeDtypeStruct((8, LANE), F32)),
        in_specs=(HBM_SPEC,) * (1 + n) + (ANY_SPEC,),
        out_specs=(SEM_SPEC, SEM_SPEC) + (HBM_SPEC,) * (1 + n) + (pl.BlockSpec(memory_space=pltpu.VMEM),),
        input_output_aliases={i: 2 + i for i in range(1 + n)},
        compiler_params=pltpu.CompilerParams(has_side_effects=EFFECT),
    )(_hbm(packed), *lands, after)
    return outs[0], outs[1], outs[2], list(outs[3:3 + n]), outs[-1]


def _gather_wait(send_sems, recv_sems, packed, lands, after, name):
    n = len(lands)

    def body(packed_ref, *refs):
        s_sems, r_sems = refs[n], refs[n + 1]
        me = _mesh_pos()
        for k in range(4):
            cp = _whole_wait(packed_ref, s_sems.at[k], r_sems.at[k], me)
            cp.wait_send()
            cp.wait_recv()

    outs = pl.pallas_call(
        body, name=name,
        out_shape=(pltpu.HBM(packed.shape, BF16), *[pltpu.HBM(t.shape, BF16) for t in lands]),
        in_specs=(HBM_SPEC,) * (1 + n) + (SEM_SPEC, SEM_SPEC, ANY_SPEC),
        out_specs=(HBM_SPEC,) * (1 + n),
        input_output_aliases={i: i for i in range(1 + n)},
        compiler_params=pltpu.CompilerParams(has_side_effects=EFFECT),
    )(packed, *lands, send_sems, recv_sems, after)
    return outs[0], list(outs[1:])


def _gather_finish(packed, rows_list, lands, name):
    n = len(rows_list)
    offs = _offsets(rows_list)

    def body(packed_ref, *refs):
        land = refs[n:2 * n]
        send_sems, recv_sems, stage, stage_sem = refs[2 * n:]
        px, py, pc = _mesh_pos()
        me = 4 * px + 2 * py + pc
        sibling = (px, py, 1 - pc)
        load = pltpu.make_async_copy(packed_ref, stage, stage_sem)
        load.start()
        load.wait()
        for off, rows, land_ref in zip(offs, rows_list, land):
            pltpu.make_async_copy(stage.at[pl.ds(off, rows), :], land_ref.at[pl.ds(me * rows, rows), :],
                                  stage_sem).start()
        for j, (cx, cy) in enumerate([(1 - px, py), (px, 1 - py), (1 - px, 1 - py)]):
            block = 4 * cx + 2 * cy + pc
            for rows, land_ref in zip(rows_list, land):
                blk = land_ref.at[pl.ds(block * rows, rows), :]
                pltpu.make_async_remote_copy(src_ref=blk, dst_ref=blk, send_sem=send_sems.at[j],
                                             recv_sem=recv_sems.at[j], device_id=sibling,
                                             device_id_type=pl.DeviceIdType.MESH).start()
        for j in range(3):
            cp = _whole_wait(packed_ref, send_sems.at[j], recv_sems.at[j], sibling)
            cp.wait_recv()
            cp.wait_send()
        pltpu.make_async_copy(stage, packed_ref, stage_sem).wait()

    outs = pl.pallas_call(
        body, name=name,
        out_shape=tuple(jax.ShapeDtypeStruct(t.shape, BF16) for t in lands),
        in_specs=(HBM_SPEC,) * (1 + n), out_specs=(HBM_SPEC,) * n,
        input_output_aliases={1 + i: i for i in range(n)},
        scratch_shapes=[pltpu.SemaphoreType.DMA((3,)), pltpu.SemaphoreType.DMA((3,)),
                        pltpu.VMEM(packed.shape, BF16), pltpu.SemaphoreType.DMA],
    )(packed, *lands)
    return list(outs)


N_CHIPS = 4


def _pair_start(srcs, rows_list, after, name):
    n = len(rows_list)
    offs = _offsets(rows_list)
    land = lax.empty((N_CHIPS, sum(rows_list), D_MODEL), BF16)

    def body(*refs):
        src, land_ref = refs[:n], refs[n]
        send_sems, recv_sems = refs[n + 2], refs[n + 3]
        token = refs[-1]
        px, py, pc = _mesh_pos()
        for k in range(N_CHIPS):
            block = 2 * k + (1 - pc)
            for off, rows, src_ref in zip(offs, rows_list, src):
                pltpu.make_async_remote_copy(
                    src_ref=src_ref.at[pl.ds(block * rows, rows), :], dst_ref=land_ref.at[k, pl.ds(off, rows), :],
                    send_sem=send_sems.at[0], recv_sem=recv_sems.at[0],
                    device_id=(px, py, 1 - pc), device_id_type=pl.DeviceIdType.MESH).start()
        token[...] = jnp.zeros_like(token)

    outs = pl.pallas_call(
        body, name=name,
        out_shape=(pltpu.SemaphoreType.DMA((1,)), pltpu.SemaphoreType.DMA((1,)),
                   *[pltpu.HBM(t.shape, BF16) for t in srcs], pltpu.HBM(land.shape, BF16),
                   jax.ShapeDtypeStruct((8, LANE), F32)),
        in_specs=(HBM_SPEC,) * (n + 1) + (ANY_SPEC,),
        out_specs=(SEM_SPEC, SEM_SPEC) + (HBM_SPEC,) * (n + 1) + (pl.BlockSpec(memory_space=pltpu.VMEM),),
        input_output_aliases={i: 2 + i for i in range(n + 1)},
        compiler_params=pltpu.CompilerParams(has_side_effects=EFFECT),
    )(*[_hbm(t) for t in srcs], _hbm(land), after)
    return outs[0], outs[1], list(outs[2:2 + n]), outs[2 + n], outs[-1]


def _split_wait(send_sems, recv_sems, n_sems, srcs, land, after, name):
    n = len(srcs)

    def body(*refs):
        land_ref = refs[n]
        s_sems, r_sems = refs[n + 1], refs[n + 2]
        me = _mesh_pos()
        for k in range(n_sems):
            cp = _whole_wait(land_ref.at[0] if n_sems > 1 else land_ref, s_sems.at[k], r_sems.at[k], me)
            cp.wait_send()
            cp.wait_recv()

    outs = pl.pallas_call(
        body, name=name,
        out_shape=(*[pltpu.HBM(t.shape, t.dtype) for t in srcs], pltpu.HBM(land.shape, land.dtype)),
        in_specs=(HBM_SPEC,) * (n + 1) + (SEM_SPEC, SEM_SPEC, ANY_SPEC),
        out_specs=(HBM_SPEC,) * (n + 1),
        input_output_aliases={i: i for i in range(n + 1)},
        compiler_params=pltpu.CompilerParams(has_side_effects=EFFECT),
    )(*srcs, land, send_sems, recv_sems, after)
    return list(outs[:n]), outs[n]


def _spread_start(x, me_id, after, name):
    land = lax.dynamic_update_slice_in_dim(lax.empty((N_DEV,) + x.shape, x.dtype), x[None], me_id, axis=0)

    def body(x_ref, land_ref, after_ref, send_sems, recv_sems, x_thru, land_thru, token):
        px, py, pc = _mesh_pos()
        me = 4 * px + 2 * py + pc
        for k in range(1, N_DEV):
            qx = 1 - px if k & 4 else px
            qy = 1 - py if k & 2 else py
            qc = 1 - pc if k & 1 else pc
            pltpu.make_async_remote_copy(
                src_ref=x_ref, dst_ref=land_ref.at[me], send_sem=send_sems.at[k - 1], recv_sem=recv_sems.at[k - 1],
                device_id=(qx, qy, qc), device_id_type=pl.DeviceIdType.MESH).start()
        token[...] = jnp.zeros_like(token)

    outs = pl.pallas_call(
        body, name=name,
        out_shape=(pltpu.SemaphoreType.DMA((N_DEV - 1,)), pltpu.SemaphoreType.DMA((N_DEV - 1,)),
                   pltpu.HBM(x.shape, x.dtype), pltpu.HBM(land.shape, land.dtype), jax.ShapeDtypeStruct((8, LANE), F32)),
        in_specs=(HBM_SPEC, HBM_SPEC, ANY_SPEC),
        out_specs=(SEM_SPEC, SEM_SPEC, HBM_SPEC, HBM_SPEC, pl.BlockSpec(memory_space=pltpu.VMEM)),
        input_output_aliases={0: 2, 1: 3},
        compiler_params=pltpu.CompilerParams(has_side_effects=EFFECT),
    )(_hbm(x), _hbm(land), after)
    return outs[0], outs[1], [outs[2]], outs[3], outs[4]


def _pair_sum(srcs, rows_list, land, core, name):
    n = len(rows_list)
    offs = _offsets(rows_list)
    total = sum(rows_list)

    def body(core_ref, *refs):
        src, land_ref, out_ref = refs[:n], refs[n], refs[n + 1]
        for off, rows, src_ref in zip(offs, rows_list, src):
            out_ref[pl.ds(off, rows), :] = (src_ref[...].astype(F32)
                                            + land_ref[pl.ds(off, rows), :].astype(F32)).astype(BF16)

    slot = pl.BlockSpec((None, total, D_MODEL), lambda k, c: (k, 0, 0))
    grid_spec = pltpu.PrefetchScalarGridSpec(
        num_scalar_prefetch=1, grid=(N_CHIPS,),
        in_specs=[pl.BlockSpec((rows, D_MODEL), lambda k, c: (2 * k + c[0], 0)) for rows in rows_list] + [slot],
        out_specs=slot)
    return pl.pallas_call(
        body, name=name, grid_spec=grid_spec,
        out_shape=jax.ShapeDtypeStruct((N_CHIPS, total, D_MODEL), BF16),
        compiler_params=_params(("parallel",)),
    )(core, *srcs, land)


def _chip_exchange_start(sums, chip, after, name):
    own = lax.dynamic_index_in_dim(sums, chip, axis=0, keepdims=True)
    recv = lax.dynamic_update_slice_in_dim(lax.empty(sums.shape, BF16), own, chip, axis=0)

    def body(sums_ref, recv_ref, after_ref, send_sems, recv_sems, sums_thru, recv_thru, token):
        px, py, pc = _mesh_pos()
        for k in range(1, N_CHIPS):
            qx = 1 - px if k & 2 else px
            qy = 1 - py if k & 1 else py
            pltpu.make_async_remote_copy(
                src_ref=sums_ref.at[2 * qx + qy], dst_ref=recv_ref.at[2 * px + py],
                send_sem=send_sems.at[k - 1], recv_sem=recv_sems.at[k - 1],
                device_id=(qx, qy, pc), device_id_type=pl.DeviceIdType.MESH).start()
        token[...] = jnp.zeros_like(token)

    outs = pl.pallas_call(
        body, name=name,
        out_shape=(pltpu.SemaphoreType.DMA((N_CHIPS - 1,)), pltpu.SemaphoreType.DMA((N_CHIPS - 1,)),
                   pltpu.HBM(sums.shape, BF16), pltpu.HBM(recv.shape, BF16), jax.ShapeDtypeStruct((8, LANE), F32)),
        in_specs=(HBM_SPEC, HBM_SPEC, ANY_SPEC),
        out_specs=(SEM_SPEC, SEM_SPEC, HBM_SPEC, HBM_SPEC, pl.BlockSpec(memory_space=pltpu.VMEM)),
        input_output_aliases={0: 2, 1: 3},
        compiler_params=pltpu.CompilerParams(has_side_effects=EFFECT),
    )(_hbm(sums), _hbm(recv), after)
    return outs[0], outs[1], [outs[2]], outs[3], outs[4]


def _sum_slots_into(recv, buf, layer, row_off, name):
    slots, r, n = recv.shape
    tr = _row_tile(math.gcd(r, row_off) if row_off else r, 512)
    first = row_off // tr

    def body(in_ref, buf_ref, out_ref):
        acc = in_ref[0].astype(F32)
        for j in range(1, slots):
            acc = acc + in_ref[j].astype(F32)
        out_ref[...] = acc

    return pl.pallas_call(
        body, name=name, grid=(r // tr,), out_shape=jax.ShapeDtypeStruct(buf.shape, F32),
        in_specs=[pl.BlockSpec((slots, tr, n), lambda i: (0, i, 0)), ANY_SPEC],
        out_specs=pl.BlockSpec((None, tr, n), lambda i: (layer, first + i, 0)),
        input_output_aliases={1: 0},
        compiler_params=_params(("parallel",)),
    )(recv, buf)


def _sum_slots(recv, name, after=None):
    _, r, n = recv.shape
    tr = _row_tile(r, 512)

    def body(in_ref, *refs):
        acc = in_ref[0].astype(F32)
        for j in range(1, N_DEV):
            acc = acc + in_ref[j].astype(F32)
        refs[-1][...] = acc

    grid = (r // tr,)
    in_specs, out_spec = [pl.BlockSpec((N_DEV, tr, n), lambda i: (0, i, 0))], pl.BlockSpec((tr, n), lambda i: (i, 0))
    args = [recv]
    if after is not None:
        in_specs.append(ANY_SPEC)
        args.append(after)
    return pl.pallas_call(
        body, name=name, grid=grid,
        out_shape=jax.ShapeDtypeStruct((r, n), F32),
        in_specs=in_specs, out_specs=out_spec,
        compiler_params=_params(("parallel",)),
    )(*args)


def _row_tile(rows, target):
    if rows <= target:
        return rows
    best = None
    for t in range(16, target + 1, 16):
        if rows % t == 0:
            best = t
    assert best is not None, rows
    return best


_DIMS = {"nn": ((1,), (0,)), "nt": ((1,), (1,)), "tn": ((0,), (0,))}


def _mm(a, b, mode, name, out_dtype=F32, res=None, gate=None, gate_factor=1.0, tm=512, tn=1408, after=None):
    assert (res is None) == (gate is None)
    if mode == "tn":
        kdim, m = a.shape
    else:
        m, kdim = a.shape
    n = b.shape[0] if mode == "nt" else b.shape[1]
    tm, tn = _tile(m, tm), _tile(n, tn)
    a_spec = (pl.BlockSpec((kdim, tm), lambda i, j: (0, i)) if mode == "tn"
              else pl.BlockSpec((tm, kdim), lambda i, j: (i, 0)))
    b_spec = (pl.BlockSpec((tn, kdim), lambda i, j: (j, 0)) if mode == "nt"
              else pl.BlockSpec((kdim, tn), lambda i, j: (0, j)))
    o_spec = pl.BlockSpec((tm, tn), lambda i, j: (i, j))
    dims = (_DIMS[mode], ((), ()))
    has_res = res is not None

    def body(a_ref, b_ref, *refs):
        y = lax.dot_general(a_ref[...].astype(BF16), b_ref[...].astype(BF16), dims,
                            preferred_element_type=F32)
        if has_res:
            res_ref, gate_ref = refs[0], refs[1]
            y_ref, o_ref = refs[-2], refs[-1]
            y_ref[...] = y.astype(BF16)
            o_ref[...] = res_ref[...] + (gate_factor * gate_ref[...]) * y
        else:
            refs[-1][...] = y.astype(out_dtype)

    in_specs, args = [a_spec, b_spec], [a, b]
    if has_res:
        in_specs += [o_spec, pl.BlockSpec((1, tn), lambda i, j: (0, j))]
        args += [res, gate]
        out_shape = (jax.ShapeDtypeStruct((m, n), BF16), jax.ShapeDtypeStruct((m, n), F32))
        out_specs = (o_spec, o_spec)
    else:
        out_shape, out_specs = jax.ShapeDtypeStruct((m, n), out_dtype), o_spec
    if after is not None:
        in_specs.append(ANY_SPEC)
        args.append(after)
    return pl.pallas_call(
        body, name=name, grid=(m // tm, n // tn), out_shape=out_shape,
        in_specs=in_specs, out_specs=out_specs,
        compiler_params=_params(("parallel", "parallel")),
    )(*args)


def _vec_spec(width):
    return pl.BlockSpec((1, width), lambda i: (0, 0))


def _rm_bwd(dh, x, dres, gw, scale, name, below=None):
    s, d = x.shape
    ts = _tile(s, 256)
    factor = None if below is None else below[2]

    def body(dh_ref, x_ref, dres_ref, gw_ref, sc_ref, *refs):
        dx_ref, dsh_ref, dsc_ref, dgw_ref = refs[-6:-2] if below is not None else refs[-4:]

        @pl.when(pl.program_id(0) == 0)
        def _():
            dsh_ref[...] = jnp.zeros_like(dsh_ref)
            dsc_ref[...] = jnp.zeros_like(dsc_ref)
            dgw_ref[...] = jnp.zeros_like(dgw_ref)
            if below is not None:
                refs[-1][...] = jnp.zeros_like(refs[-1])

        xv, dhv, gwv = x_ref[...], dh_ref[...], gw_ref[...]
        r = lax.rsqrt(jnp.mean(xv * xv, axis=-1, keepdims=True) + EPS)
        xn = xv * r
        y = xn * gwv
        dsh_ref[...] += jnp.sum(dhv, axis=0, keepdims=True)
        dsc_ref[...] += jnp.sum(dhv * y, axis=0, keepdims=True)
        dy = dhv * (1 + sc_ref[...])
        dgw_ref[...] += jnp.sum(dy * xn, axis=0, keepdims=True)
        dxn = dy * gwv
        dx = dres_ref[...] + r * (dxn - xn * jnp.mean(dxn * xn, axis=-1, keepdims=True))
        dx_ref[...] = dx
        if below is not None:
            yb_ref, gb_ref, dyb_ref, dgb_ref = refs[0], refs[1], refs[-2], refs[-1]
            dyb_ref[...] = ((factor * gb_ref[...]) * dx).astype(BF16)
            dgb_ref[...] += jnp.sum((factor * dx) * yb_ref[...].astype(F32), axis=0, keepdims=True)

    row = pl.BlockSpec((ts, d), lambda i: (i, 0))
    vec = jax.ShapeDtypeStruct((1, d), F32)
    in_specs, args = [row, row, row, _vec_spec(d), _vec_spec(d)], [dh, x, dres, gw, scale]
    out_shape = [jax.ShapeDtypeStruct((s, d), F32), vec, vec, vec]
    out_specs = [row, _vec_spec(d), _vec_spec(d), _vec_spec(d)]
    if below is not None:
        in_specs += [row, _vec_spec(d)]
        args += [below[0], below[1]]
        out_shape += [jax.ShapeDtypeStruct((s, d), BF16), vec]
        out_specs += [row, _vec_spec(d)]
    return pl.pallas_call(
        body, name=name, grid=(s // ts,), out_shape=tuple(out_shape),
        in_specs=in_specs, out_specs=tuple(out_specs),
        compiler_params=_params(("arbitrary",)),
    )(*args)


def _gate_bwd(dx, y, gate, factor, name):
    s, d = dx.shape
    ts = _tile(s, 256)

    def body(dx_ref, y_ref, g_ref, dy_ref, dg_ref):
        @pl.when(pl.program_id(0) == 0)
        def _():
            dg_ref[...] = jnp.zeros_like(dg_ref)

        dxv = dx_ref[...]
        dy_ref[...] = ((factor * g_ref[...]) * dxv).astype(BF16)
        dg_ref[...] += jnp.sum((factor * dxv) * y_ref[...].astype(F32), axis=0, keepdims=True)

    row = pl.BlockSpec((ts, d), lambda i: (i, 0))
    return pl.pallas_call(
        body, name=name, grid=(s // ts,),
        out_shape=(jax.ShapeDtypeStruct((s, d), BF16), jax.ShapeDtypeStruct((1, d), F32)),
        in_specs=[row, row, _vec_spec(d)], out_specs=(row, _vec_spec(d)),
        compiler_params=_params(("arbitrary",)),
    )(dx, y, gate)


def _norm_mm(x, gw, shift, scale, w, name, tm=1024):
    s, d = x.shape
    n = w.shape[0]
    tm = _tile(s, tm)

    def body(x_ref, gw_ref, sh_ref, sc_ref, w_ref, h_ref, z_ref):
        xv = x_ref[...]
        r = lax.rsqrt(jnp.mean(xv * xv, axis=-1, keepdims=True) + EPS)
        hb = (((xv * r) * gw_ref[...]) * (1 + sc_ref[...]) + sh_ref[...]).astype(BF16)
        h_ref[...] = hb
        z_ref[...] = lax.dot_general(hb, w_ref[...], (((1,), (1,)), ((), ())), preferred_element_type=F32)

    row = pl.BlockSpec((tm, d), lambda i: (i, 0))
    return pl.pallas_call(
        body, name=name, grid=(s // tm,),
        out_shape=(jax.ShapeDtypeStruct((s, d), BF16), jax.ShapeDtypeStruct((s, n), F32)),
        in_specs=[row, _vec_spec(d), _vec_spec(d), _vec_spec(d), pl.BlockSpec((n, d), lambda i: (0, 0))],
        out_specs=(row, pl.BlockSpec((tm, n), lambda i: (i, 0))),
        compiler_params=_params(("parallel",)),
    )(x, gw, shift, scale, w)


FFN_TM, FFN_TF = 2048, 256


def _ffn_up(x, gw, shift, scale, wg, wu, name):
    s, d = x.shape
    f = wg.shape[0]
    tm, tf = _tile(s, FFN_TM), _tile(f, FFN_TF)
    nt = (((1,), (1,)), ((), ()))

    def body(x_ref, gw_ref, sh_ref, sc_ref, wg_ref, wu_ref, h_ref, a_ref, b_ref, t_ref):
        @pl.when(pl.program_id(1) == 0)
        def _():
            xv = x_ref[...]
            r = lax.rsqrt(jnp.mean(xv * xv, axis=-1, keepdims=True) + EPS)
            h_ref[...] = (((xv * r) * gw_ref[...]) * (1 + sc_ref[...]) + sh_ref[...]).astype(BF16)

        hb = h_ref[...]
        av = lax.dot_general(hb, wg_ref[...], nt, preferred_element_type=F32)
        bv = lax.dot_general(hb, wu_ref[...], nt, preferred_element_type=F32)
        a_ref[...] = av.astype(BF16)
        b_ref[...] = bv.astype(BF16)
        t_ref[...] = ((av * jax.nn.sigmoid(av)) * bv).astype(BF16)

    row = pl.BlockSpec((tm, d), lambda i, j: (i, 0))
    vec = pl.BlockSpec((1, d), lambda i, j: (0, 0))
    wblk = pl.BlockSpec((tf, d), lambda i, j: (j, 0))
    blk = pl.BlockSpec((tm, tf), lambda i, j: (i, j))
    wide = jax.ShapeDtypeStruct((s, f), BF16)
    return pl.pallas_call(
        body, name=name, grid=(s // tm, f // tf),
        out_shape=(jax.ShapeDtypeStruct((s, d), BF16), wide, wide, wide),
        in_specs=[row, vec, vec, vec, wblk, wblk], out_specs=(row, blk, blk, blk),
        compiler_params=_params(("parallel", "arbitrary")),
    )(x, gw, shift, scale, wg, wu)


def _ffn_bwd_cols(dy, h, a, b, t, wd, name, after=None):
    s, d = dy.shape
    f = wd.shape[0]
    tf = _tile(f, FFN_TF)
    nt = (((1,), (1,)), ((), ()))
    tn = (((0,), (0,)), ((), ()))

    def body(dy_ref, h_ref, a_ref, b_ref, t_ref, wd_ref, *refs):
        da_ref, db_ref, gd_ref, gg_ref, gu_ref = refs[-5:]
        dyb, hb = dy_ref[...], h_ref[...]
        dtv = lax.dot_general(dyb, wd_ref[...], nt, preferred_element_type=F32)
        av, bv = a_ref[...].astype(F32), b_ref[...].astype(F32)
        sg = jax.nn.sigmoid(av)
        dbv = (dtv * (av * sg)).astype(BF16)
        dav = ((dtv * bv) * (sg * (1 + av * (1 - sg)))).astype(BF16)
        da_ref[...] = dav
        db_ref[...] = dbv
        gd_ref[...] = lax.dot_general(t_ref[...], dyb, tn, preferred_element_type=F32).astype(BF16)
        gg_ref[...] = lax.dot_general(dav, hb, tn, preferred_element_type=F32).astype(BF16)
        gu_ref[...] = lax.dot_general(dbv, hb, tn, preferred_element_type=F32).astype(BF16)

    whole = pl.BlockSpec((s, d), lambda j: (0, 0))
    col = pl.BlockSpec((s, tf), lambda j: (0, j))
    wblk = pl.BlockSpec((tf, d), lambda j: (j, 0))
    wide, wgrad = jax.ShapeDtypeStruct((s, f), BF16), jax.ShapeDtypeStruct((f, d), BF16)
    in_specs, args = [whole, whole, col, col, col, wblk], [dy, h, a, b, t, wd]
    if after is not None:
        in_specs.append(ANY_SPEC)
        args.append(after)
    return pl.pallas_call(
        body, name=name, grid=(f // tf,), out_shape=(wide, wide, wgrad, wgrad, wgrad),
        in_specs=in_specs, out_specs=(col, col, wblk, wblk, wblk),
        compiler_params=_params(("parallel",)),
    )(*args)


def _mm_pair(a1, b1, a2, b2, name, tm=1024, tn=512, after=None):
    m, kdim = a1.shape
    n = b1.shape[1]
    tm, tn = _tile(m, tm), _tile(n, tn)

    def body(a1_ref, b1_ref, a2_ref, b2_ref, *refs):
        refs[-1][...] = (jnp.dot(a1_ref[...], b1_ref[...], preferred_element_type=F32)
                         + jnp.dot(a2_ref[...], b2_ref[...], preferred_element_type=F32))

    a_spec = pl.BlockSpec((tm, kdim), lambda i, j: (i, 0))
    b_spec = pl.BlockSpec((kdim, tn), lambda i, j: (0, j))
    in_specs, args = [a_spec, b_spec, a_spec, b_spec], [a1, b1, a2, b2]
    if after is not None:
        in_specs.append(ANY_SPEC)
        args.append(after)
    return pl.pallas_call(
        body, name=name, grid=(m // tm, n // tn), out_shape=jax.ShapeDtypeStruct((m, n), F32),
        in_specs=in_specs, out_specs=pl.BlockSpec((tm, tn), lambda i, j: (i, j)),
        compiler_params=_params(("parallel", "parallel")),
    )(*args)


def _pool_counts(s):
    return (lax.broadcasted_iota(jnp.int32, (s, POOL_GC), 0))


def _pool_fwd(z, pool_w, pool_scale, name):
    s = z.shape[0]

    def body(u_ref, w_ref, sc_ref, y_ref, diff_ref):
        t = lax.broadcasted_iota(jnp.int32, (s, POOL_GC), 0)
        for g, win in enumerate(POOL_WINDOWS):
            cols = slice(g * POOL_GC, (g + 1) * POOL_GC)
            u = u_ref[:, cols]
            acc, step = u, 1
            while step < win:
                acc = acc + jnp.where(t >= step, pltpu.roll(acc, step, 0), 0.0)
                step *= 2
            cnt = jnp.minimum(t + 1, win).astype(F32)
            diff = acc / cnt - u
            diff_ref[:, cols] = diff
            ypre = jnp.dot(diff.astype(BF16), w_ref[g].astype(BF16), preferred_element_type=F32)
            y_ref[:, cols] = (ypre * sc_ref[:, cols]).astype(BF16)

    return pl.pallas_call(
        body, name=name, grid=(1,),
        out_shape=(jax.ShapeDtypeStruct((s, POOL_WIDTH), BF16), jax.ShapeDtypeStruct((s, POOL_WIDTH), F32)),
        in_specs=[pl.BlockSpec((s, POOL_WIDTH), lambda i: (0, 0)),
                  pl.BlockSpec(pool_w.shape, lambda i: (0, 0, 0)),
                  pl.BlockSpec((1, POOL_WIDTH), lambda i: (0, 0))],
        out_specs=(pl.BlockSpec((s, POOL_WIDTH), lambda i: (0, 0)),
                   pl.BlockSpec((s, POOL_WIDTH), lambda i: (0, 0))),
        compiler_params=_params(("arbitrary",)),
    )(z, pool_w, pool_scale)


def _pool_bwd(dycat, diff, pool_w, pool_scale, name):
    s = diff.shape[0]

    def body(dy_ref, diff_ref, w_ref, sc_ref, du_ref, dw_ref, dsc_ref):
        t = lax.broadcasted_iota(jnp.int32, (s, POOL_GC), 0)
        for g, win in enumerate(POOL_WINDOWS):
            cols = slice(g * POOL_GC, (g + 1) * POOL_GC)
            dy, dfb, wb = dy_ref[:, cols], diff_ref[:, cols].astype(BF16), w_ref[g].astype(BF16)
            ypre = jnp.dot(dfb, wb, preferred_element_type=F32)
            dsc_ref[:, cols] = jnp.sum(dy * ypre, axis=0, keepdims=True)
            dypre = (dy * sc_ref[:, cols]).astype(BF16)
            ddiff = lax.dot_general(dypre, wb, (((1,), (1,)), ((), ())), preferred_element_type=F32)
            dw_ref[g] = lax.dot_general(dfb, dypre, (((0,), (0,)), ((), ())), preferred_element_type=F32)
            cnt = jnp.minimum(t + 1, win).astype(F32)
            acc, step = ddiff / cnt, 1
            while step < win:
                acc = acc + jnp.where(t < s - step, pltpu.roll(acc, s - step, 0), 0.0)
                step *= 2
            du_ref[:, cols] = acc - ddiff

    full = pl.BlockSpec((s, POOL_WIDTH), lambda i: (0, 0))
    return pl.pallas_call(
        body, name=name, grid=(1,),
        out_shape=(jax.ShapeDtypeStruct((s, POOL_WIDTH), F32),
                   jax.ShapeDtypeStruct(pool_w.shape, F32),
                   jax.ShapeDtypeStruct((1, POOL_WIDTH), F32)),
        in_specs=[full, full, pl.BlockSpec(pool_w.shape, lambda i: (0, 0, 0)),
                  pl.BlockSpec((1, POOL_WIDTH), lambda i: (0, 0))],
        out_specs=(full, pl.BlockSpec(pool_w.shape, lambda i: (0, 0, 0)),
                   pl.BlockSpec((1, POOL_WIDTH), lambda i: (0, 0))),
        compiler_params=_params(("arbitrary",)),
    )(dycat, diff, pool_w, pool_scale)


def _rope_tables(positions, name):
    s = positions.shape[0]
    ts = _tile(s, 512)
    freq = 1.0 / (ROPE_THETA ** (np.arange(0, QK_ROPE, 2, dtype=np.float32) / QK_ROPE))
    table = np.zeros((1, LANE), np.float32)
    table[0, :QK_ROPE // 2] = freq
    table[0, QK_ROPE // 2:QK_ROPE] = freq

    def body(pos_ref, f_ref, cos_ref, sin_ref):
        ang = pos_ref[...].astype(F32) * f_ref[...]
        cos_ref[...] = jnp.cos(ang)
        sin_ref[...] = jnp.sin(ang)

    out = jax.ShapeDtypeStruct((s, LANE), F32)
    blk = pl.BlockSpec((ts, LANE), lambda i: (i, 0))
    return pl.pallas_call(
        body, name=name, grid=(s // ts,), out_shape=(out, out),
        in_specs=[pl.BlockSpec((ts, 1), lambda i: (i, 0)), _vec_spec(LANE)], out_specs=(blk, blk),
        compiler_params=_params(("parallel",)),
    )(positions, jnp.asarray(table))


def _lane_mod64_low(shape):
    return (lax.broadcasted_iota(jnp.int32, shape, 1) % QK_ROPE) < (QK_ROPE // 2)


def _rope(x, cos, sin):
    rot = jnp.where(_lane_mod64_low(x.shape), -pltpu.roll(x, LANE - 32, 1), pltpu.roll(x, 32, 1))
    return x * cos + rot * sin


def _rope_t(dy, cos, sin):
    w = dy * sin
    rot_t = jnp.where(_lane_mod64_low(dy.shape), pltpu.roll(w, LANE - 32, 1), -pltpu.roll(w, 32, 1))
    return dy * cos + rot_t


def _plain_rms(x, g):
    r = lax.rsqrt(jnp.mean(x * x, axis=-1, keepdims=True) + EPS)
    return (x * r) * g, x * r, r


O_Q, O_KV, O_KR = POOL_WIDTH, POOL_WIDTH + Q_LORA, POOL_WIDTH + Q_LORA + KV_LORA


def _qkv_fwd(z, qn, kvn, wq, wkv, cos, sin, name):
    s = z.shape[0]
    ts = _tile(s, 256)

    def body(z_ref, qn_ref, kvn_ref, wq_ref, wkv_ref, cos_ref, sin_ref, q_ref, k_ref, v_ref, cqn_ref, ckvn_ref):
        cosv, sinv = cos_ref[...], sin_ref[...]
        cqn = _plain_rms(z_ref[:, O_Q:O_KV], qn_ref[...])[0].astype(BF16)
        ckvn = _plain_rms(z_ref[:, O_KV:O_KR], kvn_ref[...])[0].astype(BF16)
        cqn_ref[...] = cqn
        ckvn_ref[...] = ckvn
        nt = (((1,), (1,)), ((), ()))
        q = lax.dot_general(cqn, wq_ref[...], nt, preferred_element_type=F32)
        kv = lax.dot_general(ckvn, wkv_ref[...], nt, preferred_element_type=F32)
        kr = _rope(z_ref[:, O_KR:IN_PAD], cosv, sinv).astype(BF16)
        for h in range(N_HEADS):
            o = h * HEAD_PAD
            q_ref[:, o:o + QK_NOPE] = q[:, o:o + QK_NOPE].astype(BF16)
            q_ref[:, o + QK_NOPE:o + HEAD_PAD] = _rope(q[:, o + QK_NOPE:o + HEAD_PAD], cosv, sinv).astype(BF16)
            k_ref[:, o:o + QK_NOPE] = kv[:, o:o + QK_NOPE].astype(BF16)
            k_ref[:, o + QK_NOPE:o + HEAD_PAD] = kr
            v_ref[:, h * V_HEAD:(h + 1) * V_HEAD] = kv[:, o + QK_NOPE:o + HEAD_PAD].astype(BF16)

    def row(w):
        return pl.BlockSpec((ts, w), lambda i: (i, 0))

    def whole(arr):
        return pl.BlockSpec(arr.shape, lambda i: (0, 0))

    hp = N_HEADS * HEAD_PAD
    return pl.pallas_call(
        body, name=name, grid=(s // ts,),
        out_shape=(jax.ShapeDtypeStruct((s, hp), BF16), jax.ShapeDtypeStruct((s, hp), BF16),
                   jax.ShapeDtypeStruct((s, N_HEADS * V_HEAD), BF16),
                   jax.ShapeDtypeStruct((s, Q_LORA), BF16), jax.ShapeDtypeStruct((s, KV_LORA), BF16)),
        in_specs=[row(IN_PAD), whole(qn), whole(kvn), whole(wq), whole(wkv), row(LANE), row(LANE)],
        out_specs=(row(hp), row(hp), row(N_HEADS * V_HEAD), row(Q_LORA), row(KV_LORA)),
        compiler_params=_params(("parallel",)),
    )(z, qn, kvn, wq, wkv, cos, sin)


def _qkv_bwd(dq, dk, dv, du, z, qn, kvn, wq, wkv, cos, sin, name):
    s = z.shape[0]
    ts = _tile(s, 256)

    def norm_bwd(x, g, dy):
        _, xn, r = _plain_rms(x, g)
        dxn = dy * g
        return r * (dxn - xn * jnp.mean(dxn * xn, axis=-1, keepdims=True)), jnp.sum(dy * xn, axis=0, keepdims=True)

    def body(dq_ref, dk_ref, dv_ref, du_ref, z_ref, qn_ref, kvn_ref, wq_ref, wkv_ref, cos_ref, sin_ref,
             dz_ref, dqb_ref, dkvb_ref, dqn_ref, dkvn_ref):
        @pl.when(pl.program_id(0) == 0)
        def _():
            dqn_ref[...] = jnp.zeros_like(dqn_ref)
            dkvn_ref[...] = jnp.zeros_like(dkvn_ref)

        cosv, sinv = cos_ref[...], sin_ref[...]
        dkr = jnp.zeros((ts, LANE), F32)
        for h in range(N_HEADS):
            o = h * HEAD_PAD
            dqb_ref[:, o:o + QK_NOPE] = dq_ref[:, o:o + QK_NOPE].astype(BF16)
            dqb_ref[:, o + QK_NOPE:o + HEAD_PAD] = _rope_t(dq_ref[:, o + QK_NOPE:o + HEAD_PAD], cosv, sinv).astype(BF16)
            dkvb_ref[:, o:o + QK_NOPE] = dk_ref[:, o:o + QK_NOPE].astype(BF16)
            dkvb_ref[:, o + QK_NOPE:o + HEAD_PAD] = dv_ref[:, h * V_HEAD:(h + 1) * V_HEAD].astype(BF16)
            dkr = dkr + dk_ref[:, o + QK_NOPE:o + HEAD_PAD]
        dcqn = jnp.dot(dqb_ref[...], wq_ref[...], preferred_element_type=F32)
        dckvn = jnp.dot(dkvb_ref[...], wkv_ref[...], preferred_element_type=F32)
        dcq, dqn = norm_bwd(z_ref[:, O_Q:O_KV], qn_ref[...], dcqn)
        dckv, dkvn = norm_bwd(z_ref[:, O_KV:O_KR], kvn_ref[...], dckvn)
        dqn_ref[...] += dqn
        dkvn_ref[...] += dkvn
        dz_ref[:, 0:O_Q] = du_ref[...].astype(BF16)
        dz_ref[:, O_Q:O_KV] = dcq.astype(BF16)
        dz_ref[:, O_KV:O_KR] = dckv.astype(BF16)
        dz_ref[:, O_KR:IN_PAD] = _rope_t(dkr, cosv, sinv).astype(BF16)

    def row(w):
        return pl.BlockSpec((ts, w), lambda i: (i, 0))

    def whole(arr):
        return pl.BlockSpec(arr.shape, lambda i: (0, 0))

    hp = N_HEADS * HEAD_PAD
    return pl.pallas_call(
        body, name=name, grid=(s // ts,),
        out_shape=(jax.ShapeDtypeStruct((s, IN_PAD), BF16), jax.ShapeDtypeStruct((s, hp), BF16),
                   jax.ShapeDtypeStruct((s, hp), BF16),
                   jax.ShapeDtypeStruct((1, Q_LORA), F32), jax.ShapeDtypeStruct((1, KV_LORA), F32)),
        in_specs=[row(hp), row(hp), row(N_HEADS * V_HEAD), row(POOL_WIDTH), row(IN_PAD),
                  whole(qn), whole(kvn), whole(wq), whole(wkv), row(LANE), row(LANE)],
        out_specs=(row(IN_PAD), row(hp), row(hp), whole(qn), whole(kvn)),
        compiler_params=_params(("arbitrary",)),
    )(dq, dk, dv, du, z, qn, kvn, wq, wkv, cos, sin)


def _causal_scores(q, k, i, tq, klen):
    sc = lax.dot_general(q, k, (((1,), (1,)), ((), ())), preferred_element_type=F32) * SOFTMAX_SCALE
    qpos = i * tq + lax.broadcasted_iota(jnp.int32, (tq, klen), 0)
    kpos = lax.broadcasted_iota(jnp.int32, (tq, klen), 1)
    return jnp.where(qpos >= kpos, sc, -jnp.inf)


ATTN_TQ = 512
ATTN_SEGMENTS = 4


def _by_key_prefix(i, nq, tq, compute):
    nseg = min(ATTN_SEGMENTS, nq)
    per = nq // nseg
    for r in range(nseg):
        pl.when(i // per == r)(lambda r=r: compute((r + 1) * per * tq))


def _attn_fwd(q, k, v, name):
    s = q.shape[0]
    tq = _tile(s, ATTN_TQ)
    nq = s // tq

    def body(q_ref, k_ref, v_ref, o_ref, lse_ref):
        i = pl.program_id(1)

        def compute(klen):
            sc = _causal_scores(q_ref[...], k_ref[0:klen, :], i, tq, klen)
            mx = jnp.max(sc, axis=-1, keepdims=True)
            p = jnp.exp(sc - mx)
            den = jnp.sum(p, axis=-1, keepdims=True)
            o_ref[...] = jnp.dot((p / den).astype(BF16), v_ref[0:klen, :], preferred_element_type=F32)
            lse_ref[...] = mx + jnp.log(den)

        _by_key_prefix(i, nq, tq, compute)

    return pl.pallas_call(
        body, name=name, grid=(N_HEADS, s // tq),
        out_shape=(jax.ShapeDtypeStruct((s, N_HEADS * V_HEAD), F32), jax.ShapeDtypeStruct((N_HEADS, s, 1), F32)),
        in_specs=[pl.BlockSpec((tq, HEAD_PAD), lambda h, i: (i, h)),
                  pl.BlockSpec((s, HEAD_PAD), lambda h, i: (0, h)),
                  pl.BlockSpec((s, V_HEAD), lambda h, i: (0, h))],
        out_specs=(pl.BlockSpec((tq, V_HEAD), lambda h, i: (i, h)),
                   pl.BlockSpec((None, tq, 1), lambda h, i: (h, i, 0))),
        compiler_params=_params(("parallel", "parallel")),
    )(q, k, v)


def _attn_bwd(q, k, v, lse, dycat, name):
    s = q.shape[0]
    tq = _tile(s, ATTN_TQ)
    nq = s // tq
    tn_dims = (((0,), (0,)), ((), ()))

    def body(q_ref, k_ref, v_ref, lse_ref, do_ref, dq_ref, dk_ref, dv_ref):
        i = pl.program_id(1)

        @pl.when(i == 0)
        def _():
            dk_ref[...] = jnp.zeros_like(dk_ref)
            dv_ref[...] = jnp.zeros_like(dv_ref)

        def compute(klen):
            qv, kv_, dob = q_ref[...], k_ref[0:klen, :], do_ref[...].astype(BF16)
            sc = _causal_scores(qv, kv_, i, tq, klen)
            p = jnp.exp(sc - lse_ref[...])
            dp = lax.dot_general(dob, v_ref[0:klen, :], (((1,), (1,)), ((), ())), preferred_element_type=F32)
            ds = (p * (dp - jnp.sum(dp * p, axis=-1, keepdims=True)) * SOFTMAX_SCALE).astype(BF16)
            dq_ref[...] = jnp.dot(ds, kv_, preferred_element_type=F32)
            dk_ref[0:klen, :] += lax.dot_general(ds, qv, tn_dims, preferred_element_type=F32)
            dv_ref[0:klen, :] += lax.dot_general(p.astype(BF16), dob, tn_dims, preferred_element_type=F32)

        _by_key_prefix(i, nq, tq, compute)

    n_pool_blocks = POOL_WIDTH // V_HEAD
    return pl.pallas_call(
        body, name=name, grid=(N_HEADS, s // tq),
        out_shape=(jax.ShapeDtypeStruct((s, N_HEADS * HEAD_PAD), F32),
                   jax.ShapeDtypeStruct((s, N_HEADS * HEAD_PAD), F32),
                   jax.ShapeDtypeStruct((s, N_HEADS * V_HEAD), F32)),
        in_specs=[pl.BlockSpec((tq, HEAD_PAD), lambda h, i: (i, h)),
                  pl.BlockSpec((s, HEAD_PAD), lambda h, i: (0, h)),
                  pl.BlockSpec((s, V_HEAD), lambda h, i: (0, h)),
                  pl.BlockSpec((None, tq, 1), lambda h, i: (h, i, 0)),
                  pl.BlockSpec((tq, V_HEAD), lambda h, i: (i, n_pool_blocks + h))],
        out_specs=(pl.BlockSpec((tq, HEAD_PAD), lambda h, i: (i, h)),
                   pl.BlockSpec((s, HEAD_PAD), lambda h, i: (0, h)),
                   pl.BlockSpec((s, V_HEAD), lambda h, i: (0, h))),
        compiler_params=_params(("parallel", "arbitrary")),
    )(q, k, v, lse, dycat)


def _loss_head(x, gw, target, name):
    s, d = x.shape
    ts = _tile(s, 256)

    def body(x_ref, gw_ref, tgt_ref, loss_ref, dx_ref, dgw_ref):
        @pl.when(pl.program_id(0) == 0)
        def _():
            loss_ref[...] = jnp.zeros_like(loss_ref)
            dgw_ref[...] = jnp.zeros_like(dgw_ref)

        xv, gwv = x_ref[...], gw_ref[...]
        r = lax.rsqrt(jnp.mean(xv * xv, axis=-1, keepdims=True) + EPS)
        xn = xv * r
        err = xn * gwv - tgt_ref[...]
        loss_ref[...] += 0.5 * jnp.sum(jnp.mean(err * err, axis=-1, keepdims=True))
        dy = err / d
        dgw_ref[...] += jnp.sum(dy * xn, axis=0, keepdims=True)
        dxn = dy * gwv
        dx_ref[...] = r * (dxn - xn * jnp.mean(dxn * xn, axis=-1, keepdims=True))

    row = pl.BlockSpec((ts, d), lambda i: (i, 0))
    return pl.pallas_call(
        body, name=name, grid=(s // ts,),
        out_shape=(jax.ShapeDtypeStruct((8, LANE), F32), jax.ShapeDtypeStruct((s, d), F32),
                   jax.ShapeDtypeStruct((1, d), F32)),
        in_specs=[row, _vec_spec(d), row],
        out_specs=(pl.BlockSpec((8, LANE), lambda i: (0, 0)), row, _vec_spec(d)),
        compiler_params=_params(("arbitrary",)),
    )(x, gw, target)


def _ada_mod(c_all, ada_w, ada_b, name):
    nl, d, cols = ada_w.shape

    def body(c_ref, w_ref, b_ref, o_ref):
        cv = c_ref[...]
        act = (cv * jax.nn.sigmoid(cv)).astype(BF16)
        o_ref[...] = jnp.dot(act, w_ref[...].astype(BF16), preferred_element_type=F32) + b_ref[...]

    return pl.pallas_call(
        body, name=name, grid=(nl,), out_shape=jax.ShapeDtypeStruct((nl, N_DEV, cols), F32),
        in_specs=[pl.BlockSpec((N_DEV, d), lambda l: (0, 0)),
                  pl.BlockSpec((None, d, cols), lambda l: (l, 0, 0)),
                  pl.BlockSpec((None, 1, cols), lambda l: (l, 0, 0))],
        out_specs=pl.BlockSpec((None, N_DEV, cols), lambda l: (l, 0, 0)),
        compiler_params=_params(("parallel",)),
    )(c_all, ada_w, ada_b)


def _ada_grad(c_pad, dmod_pad, name):
    nl, kpad, cols = dmod_pad.shape
    d = c_pad.shape[1]

    def body(c_ref, dm_ref, o_ref):
        cv = c_ref[...]
        act = (cv * jax.nn.sigmoid(cv)).astype(BF16)
        o_ref[...] = lax.dot_general(act, dm_ref[...].astype(BF16), (((0,), (0,)), ((), ())),
                                     preferred_element_type=F32)

    return pl.pallas_call(
        body, name=name, grid=(nl,), out_shape=jax.ShapeDtypeStruct((nl, d, cols), F32),
        in_specs=[pl.BlockSpec((kpad, d), lambda l: (0, 0)),
                  pl.BlockSpec((None, kpad, cols), lambda l: (l, 0, 0))],
        out_specs=pl.BlockSpec((None, d, cols), lambda l: (l, 0, 0)),
        compiler_params=_params(("parallel",)),
    )(c_pad, dmod_pad)


def _adamw_math(w, g, m, v):
    nm = ADAM_B1 * m + (1.0 - ADAM_B1) * g
    nv = ADAM_B2 * v + (1.0 - ADAM_B2) * (g * g)
    m_hat = nm / (1.0 - ADAM_B1 ** ADAM_STEP)
    v_hat = nv / (1.0 - ADAM_B2 ** ADAM_STEP)
    return -ADAM_LR * (m_hat / (jnp.sqrt(v_hat) + ADAM_EPS) + ADAM_WD * w), nm, nv


def _adamw_rows(w3, gbuf, row_off, m3, v3, name):
    nl, r, d = w3.shape
    tr = _row_tile(math.gcd(r, row_off) if row_off else r, 176)
    first = row_off // tr

    def body(w_ref, g_ref, m_ref, v_ref, go_ref, d_ref, nm_ref, nv_ref):
        gv = g_ref[...]
        go_ref[...] = gv
        d_ref[...], nm_ref[...], nv_ref[...] = _adamw_math(w_ref[...], gv, m_ref[...], v_ref[...])

    blk = pl.BlockSpec((None, tr, d), lambda l, i: (l, i, 0))
    gblk = pl.BlockSpec((None, tr, d), lambda l, i: (l, first + i, 0))
    out = jax.ShapeDtypeStruct((nl, r, d), F32)
    return pl.pallas_call(
        body, name=name, grid=(nl, r // tr), out_shape=(out, out, out, out),
        in_specs=[blk, gblk, blk, blk], out_specs=(blk, blk, blk, blk),
        compiler_params=_params(("parallel", "parallel")),
    )(w3, gbuf, m3, v3)


def _adamw(w, g, m, v, name):
    rows, cols = w.shape
    tr = _row_tile(rows, 512)

    def body(w_ref, g_ref, m_ref, v_ref, d_ref, nm_ref, nv_ref):
        d_ref[...], nm_ref[...], nv_ref[...] = _adamw_math(w_ref[...], g_ref[...], m_ref[...], v_ref[...])

    blk = pl.BlockSpec((tr, cols), lambda i: (i, 0))
    out = jax.ShapeDtypeStruct((rows, cols), F32)
    return pl.pallas_call(
        body, name=name, grid=(rows // tr,), out_shape=(out, out, out),
        in_specs=[blk, blk, blk, blk], out_specs=(blk, blk, blk),
        compiler_params=_params(("parallel",)),
    )(w, g, m, v)


def _adamw_nd(w, g, m, v, name):
    shape = w.shape
    flat = (lambda t: t.reshape(1, -1)) if w.ndim == 1 else (lambda t: t.reshape(-1, shape[-1]))
    return tuple(t.reshape(shape) for t in _adamw(flat(w), flat(g), flat(m), flat(v), name))


def _pad_rows(t, rows):
    return jnp.pad(t, ((0, rows - t.shape[0]), (0, 0)))


def _pack_shard_layer(l, wts):
    def tr(name):
        return wts[name][l].astype(BF16).T

    parts = [tr("ffn1_w_gate"), tr("ffn1_w_up"), wts["ffn1_w_down"][l].astype(BF16),
             tr("ffn2_w_gate"), tr("ffn2_w_up"), wts["ffn2_w_down"][l].astype(BF16),
             wts["w_out"][l].astype(BF16),
             tr("w_kv_b").reshape(KV_SH_ROWS, D_MODEL),
             _pad_rows(tr("w_in"), 160),
             _pad_rows(tr("w_q_b").reshape(Q_SH_ROWS, D_MODEL), Q_PAD_ROWS)]
    return jnp.concatenate(parts, axis=0)


def _full_weights(lands):
    w = dict(zip(("g1", "u1", "d1", "g2", "u2", "d2", "out"), lands))
    small = lands[-1].reshape(N_DEV, SMALL_ROWS, D_MODEL)
    o_in, o_q = OFF_IN - OFF_KV, OFF_Q - OFF_KV
    w["kv"] = small[:, :KV_SH_ROWS].reshape(N_HEADS * HEAD_PAD, KV_LORA)
    w["in"] = _pad_rows(small[:, o_in:o_in + IN_SH].reshape(IN_COLS, D_MODEL), IN_PAD)
    wq = small[:, o_q:o_q + Q_SH_ROWS].reshape(N_HEADS, QK_HEAD, Q_LORA)
    w["q"] = jnp.pad(wq, ((0, 0), (0, HEAD_PAD - QK_HEAD), (0, 0))).reshape(N_HEADS * HEAD_PAD, Q_LORA)
    return w


def _grad_sources_b(gr):
    gq = gr["q"].reshape(N_HEADS, HEAD_PAD, Q_LORA)[:, :QK_HEAD].reshape(N_DEV, Q_SH_ROWS, D_MODEL)
    small = jnp.concatenate([
        gr["kv"].reshape(N_DEV, KV_SH_ROWS, D_MODEL),
        jnp.pad(gr["in"][:IN_COLS].reshape(N_DEV, IN_SH, D_MODEL), ((0, 0), (0, 160 - IN_SH), (0, 0))),
        jnp.pad(gq, ((0, 0), (0, Q_PAD_ROWS - Q_SH_ROWS), (0, 0)))], axis=1)
    return [gr["g2"], gr["u2"], gr["d2"], gr["out"], small.reshape(N_DEV * SMALL_ROWS, D_MODEL)]


def _pack_bf16_pairs(t):
    rows, d = t.shape
    return lax.bitcast_convert_type(t.astype(BF16).reshape(rows // 2, 2, d).transpose(0, 2, 1), F32)


def _unpack_bf16_pairs(p):
    pairs = jnp.swapaxes(lax.bitcast_convert_type(p, BF16), -1, -2)
    return pairs.reshape(p.shape[:-2] + (2 * p.shape[-2], p.shape[-1]))


def _small_layout(nl):
    names = [("dmod", nl * N_MOD), ("ffn1_norm", nl), ("mix_norm", nl), ("ffn2_norm", nl), ("q_a_norm", nl),
             ("kv_a_norm", nl), ("pool_scale", nl), ("final_norm", 1), ("loss", 1),
             ("pool_w", nl * 4 * POOL_GC * POOL_GC // D_MODEL // 2)]
    off, table = 0, {}
    for name, n in names:
        table[name] = (off, n)
        off += -(-n // 8) * 8
    return table, off


def _to_rows(t, width=D_MODEL):
    n, w = t.shape
    return jnp.pad(t, ((0, -(-n // 8) * 8 - n), (0, width - w)))


def kernel(x, c, positions, ada_w, ada_b, ffn1_norm, ffn1_w_gate, ffn1_w_up, ffn1_w_down, mix_norm, w_in, pool_w, pool_scale, q_a_norm, w_q_b, kv_a_norm, w_kv_b, w_out, ffn2_norm, ffn2_w_gate, ffn2_w_up, ffn2_w_down, final_norm, loss_target, m_ada_w, m_ada_b, m_ffn1_norm, m_ffn1_w_gate, m_ffn1_w_up, m_ffn1_w_down, m_mix_norm, m_w_in, m_pool_w, m_pool_scale, m_q_a_norm, m_w_q_b, m_kv_a_norm, m_w_kv_b, m_w_out, m_ffn2_norm, m_ffn2_w_gate, m_ffn2_w_up, m_ffn2_w_down, m_final_norm, v_ada_w, v_ada_b, v_ffn1_norm, v_ffn1_w_gate, v_ffn1_w_up, v_ffn1_w_down, v_mix_norm, v_w_in, v_pool_w, v_pool_scale, v_q_a_norm, v_w_q_b, v_kv_a_norm, v_w_kv_b, v_w_out, v_ffn2_norm, v_ffn2_w_gate, v_ffn2_w_up, v_ffn2_w_down, v_final_norm):
    wts = dict(ada_w=ada_w, ada_b=ada_b, ffn1_norm=ffn1_norm, ffn1_w_gate=ffn1_w_gate, ffn1_w_up=ffn1_w_up,
               ffn1_w_down=ffn1_w_down, mix_norm=mix_norm, w_in=w_in, pool_w=pool_w, pool_scale=pool_scale,
               q_a_norm=q_a_norm, w_q_b=w_q_b, kv_a_norm=kv_a_norm, w_kv_b=w_kv_b, w_out=w_out,
               ffn2_norm=ffn2_norm, ffn2_w_gate=ffn2_w_gate, ffn2_w_up=ffn2_w_up, ffn2_w_down=ffn2_w_down,
               final_norm=final_norm)
    mom_m = dict(ada_w=m_ada_w, ada_b=m_ada_b, ffn1_norm=m_ffn1_norm, ffn1_w_gate=m_ffn1_w_gate,
                 ffn1_w_up=m_ffn1_w_up, ffn1_w_down=m_ffn1_w_down, mix_norm=m_mix_norm, w_in=m_w_in,
                 pool_w=m_pool_w, pool_scale=m_pool_scale, q_a_norm=m_q_a_norm, w_q_b=m_w_q_b,
                 kv_a_norm=m_kv_a_norm, w_kv_b=m_w_kv_b, w_out=m_w_out, ffn2_norm=m_ffn2_norm,
                 ffn2_w_gate=m_ffn2_w_gate, ffn2_w_up=m_ffn2_w_up, ffn2_w_down=m_ffn2_w_down,
                 final_norm=m_final_norm)
    mom_v = dict(ada_w=v_ada_w, ada_b=v_ada_b, ffn1_norm=v_ffn1_norm, ffn1_w_gate=v_ffn1_w_gate,
                 ffn1_w_up=v_ffn1_w_up, ffn1_w_down=v_ffn1_w_down, mix_norm=v_mix_norm, w_in=v_w_in,
                 pool_w=v_pool_w, pool_scale=v_pool_scale, q_a_norm=v_q_a_norm, w_q_b=v_w_q_b,
                 kv_a_norm=v_kv_a_norm, w_kv_b=v_w_kv_b, w_out=v_w_out, ffn2_norm=v_ffn2_norm,
                 ffn2_w_gate=v_ffn2_w_gate, ffn2_w_up=v_ffn2_w_up, ffn2_w_down=v_ffn2_w_down,
                 final_norm=v_final_norm)
    order = list(wts)
    nl = ada_w.shape[0]
    seq = x.shape[1]
    me = 4 * lax.axis_index("x") + 2 * lax.axis_index("y") + lax.axis_index("c")
    ada_cols = ada_w.shape[2]

    def after_token(t, token):
        return t + token[0:1, 0:1].astype(t.dtype)

    packs = [_pack_shard_layer(l, wts) for l in range(nl)]

    c_all = _all_gather(jnp.broadcast_to(c, (8, D_MODEL)), "gather_c")[::8]

    ada_b_mine = lax.dynamic_slice_in_dim(ada_b, me * ada_cols, ada_cols, axis=1).reshape(nl, 1, ada_cols)
    mod_part = _ada_mod(c_all, ada_w, ada_b_mine, "ada_mod")
    mod_all = _all_gather(mod_part.reshape(nl * N_DEV, ada_cols), "gather_mod")
    mod_all = mod_all.reshape(N_DEV, nl, N_DEV, ada_cols)
    mod = lax.dynamic_index_in_dim(mod_all, me, axis=2, keepdims=False)
    mod = mod.transpose(1, 0, 2).reshape(nl, N_MOD, 1, D_MODEL)

    flight_a = _gather_start(packs[0][:SPLIT_AB], ROWS_A, mod, "gather_start_0a")
    flight_b = _gather_start(packs[0][SPLIT_AB:], ROWS_B, flight_a[4], "gather_start_0b")
    last_start = flight_b[4]
    if nl > 1:
        in_flight = _gather_start(packs[1], ROWS_ALL, last_start, "gather_start_1")
        last_start = in_flight[4]

    cos, sin = _rope_tables(after_token(positions.reshape(seq, 1), last_start), "rope_tables")

    def vec(t):
        return t.reshape(1, -1)

    def landed(flight, rows_list, after, tag):
        send_sems, recv_sems, pk, lands, _ = flight
        pk, lands = _gather_wait(send_sems, recv_sems, pk, lands, after, f"gather_wait_{tag}")
        return _gather_finish(pk, rows_list, lands, "gather_finish")

    xs = x.reshape(seq, D_MODEL)
    saved = []
    for l in range(nl):
        norm1 = vec(ffn1_norm[l])
        if l == 0:
            lands = landed(flight_a, ROWS_A, cos, "0a")
        elif l + 1 < nl:
            in_flight = _gather_start(packs[l + 1], ROWS_ALL, lands[0], f"gather_start_{l + 1}")
            norm1 = after_token(norm1, in_flight[4])
        sv = {}

        def ffn_fwd(xin, norm, k0, wg, wu, wd, tag):
            h, a, b, t = _ffn_up(xin, norm, mod[l, k0], mod[l, k0 + 1], wg, wu, "ffn_up")
            y, xout = _mm(t, wd, "nn", "ffn_down", res=xin, gate=mod[l, k0 + 2], gate_factor=0.5)
            sv[tag] = dict(x=xin, h=h, a=a, b=b, t=t, y=y)
            return xout

        xs = ffn_fwd(xs, norm1, 0, lands[0], lands[1], lands[2], "f1")
        if l == 0:
            lands = lands + landed(flight_b, ROWS_B, xs, "0b")
        w = _full_weights(lands)
        sv["w"] = w

        h2, z = _norm_mm(xs, vec(mix_norm[l]), mod[l, 3], mod[l, 4], w["in"], "mix_in")
        y_pool, diff = _pool_fwd(z, pool_w[l], vec(pool_scale[l]), "pool_fwd")
        q, k, v, cqn, ckvn = _qkv_fwd(z, vec(q_a_norm[l]), vec(kv_a_norm[l]), w["q"], w["kv"], cos, sin, "qkv_fwd")
        o, lse = _attn_fwd(q, k, v, "attn_fwd")
        ycat = jnp.concatenate([y_pool, o.astype(BF16)], axis=1)
        y2, xmix = _mm(ycat, w["out"], "nn", "mix_out", res=xs, gate=mod[l, 5], gate_factor=1.0)
        sv["mix"] = dict(x=xs, h=h2, z=z, diff=diff, q=q, k=k, v=v, cqn=cqn, ckvn=ckvn, lse=lse, ycat=ycat, y=y2)
        xs = xmix

        xs = ffn_fwd(xs, vec(ffn2_norm[l]), 6, w["g2"], w["u2"], w["d2"], "f2")
        saved.append(sv)
        if l + 1 < nl:
            lands = landed(in_flight, ROWS_ALL, xs, l + 1)

    loss_part, dx, d_final = _loss_head(xs, vec(final_norm), loss_target.reshape(seq, D_MODEL), "loss_head")

    small = {name: [None] * nl for name in ("ffn1_norm", "mix_norm", "ffn2_norm", "q_a_norm", "kv_a_norm",
                                            "pool_scale", "pool_w", "dmod")}
    core = lax.axis_index("c").astype(jnp.int32).reshape(1)
    chip = 2 * lax.axis_index("x") + lax.axis_index("y")
    exchanges = []

    def leave(srcs, rows_list, after, tag):
        return _pair_start(srcs, rows_list, after, f"pair_start_{tag}"), rows_list, tag

    def forward_on(pending, after, layer, row_off):
        (send_sems, recv_sems, srcs, land, _), rows_list, tag = pending
        srcs, land = _split_wait(send_sems, recv_sems, 1, srcs, land, after, f"pair_wait_{tag}")
        sums = _pair_sum(srcs, rows_list, land, core, "pair_sum")
        flight = _chip_exchange_start(sums, chip, after, f"exchange_start_{tag}")
        exchanges.append((flight, layer, row_off, tag))
        return flight[4]

    pending = None
    head = _gate_bwd(dx, saved[nl - 1]["f2"]["y"], mod[nl - 1, 8], 0.5, "gate_bwd")
    for l in reversed(range(nl)):
        sv = saved[l]
        w = sv["w"]
        dmod = [None] * N_MOD
        gr = {}

        def ffn_bwd(dxin, head, s_, norm, k0, wg, wu, wd, tag, below, first_after=None, mid=None):
            dy, dmod[k0 + 2] = head
            da, db, gr["d" + tag], gr["g" + tag], gr["u" + tag] = _ffn_bwd_cols(
                dy, s_["h"], s_["a"], s_["b"], s_["t"], wd, "ffn_bwd_cols", after=first_after)
            dh = _mm_pair(da, wg, db, wu, "ffn_bwd_dh", after=None if mid is None else mid(da))
            outs = _rm_bwd(dh, s_["x"], dxin, vec(norm), mod[l, k0 + 1], "rm_bwd", below=below)
            dmod[k0], dmod[k0 + 1] = outs[1], outs[2]
            return outs[0], outs[3], outs[4:]

        s_ = sv["mix"]
        dx, small["ffn2_norm"][l], head = ffn_bwd(
            dx, head, sv["f2"], ffn2_norm[l], 6, w["g2"], w["u2"], w["d2"], "2", (s_["y"], mod[l, 5], 1.0),
            first_after=None if pending is None else pending[0][4])

        pending_c = leave([gr["g2"], gr["u2"], gr["d2"]], ROWS_A, dx, f"{l}c")
        mix_after = pending_c[0][4]
        if pending is not None:
            mix_after = forward_on(pending, mix_after, l + 1, GB_F1)
            pending = None
        dy, dmod[5] = head
        gr["out"] = _mm(s_["ycat"], dy, "tn", "mix_out_dw", out_dtype=BF16, tm=256, after=mix_after)
        dycat = _mm(dy, w["out"], "nt", "mix_out_dx")
        du, small["pool_w"][l], small["pool_scale"][l] = _pool_bwd(dycat, s_["diff"], pool_w[l], vec(pool_scale[l]), "pool_bwd")
        dq, dk, dv = _attn_bwd(s_["q"], s_["k"], s_["v"], s_["lse"], dycat, "attn_bwd")
        dz, dqb, dkvb, small["q_a_norm"][l], small["kv_a_norm"][l] = _qkv_bwd(
            dq, dk, dv, du, s_["z"], vec(q_a_norm[l]), vec(kv_a_norm[l]), w["q"], w["kv"], cos, sin, "qkv_bwd")
        gr["q"] = _mm(dqb, s_["cqn"], "tn", "q_b_dw", out_dtype=BF16, tm=256, after=forward_on(pending_c, dz, l, GB_F2))
        gr["kv"] = _mm(dkvb, s_["ckvn"], "tn", "kv_b_dw", out_dtype=BF16, tm=256)
        gr["in"] = _mm(dz, s_["h"], "tn", "mix_in_dw", out_dtype=BF16, tm=256)
        dh2 = _mm(dz, w["in"], "nn", "mix_in_dx")
        outs = _rm_bwd(dh2, s_["x"], dx, vec(mix_norm[l]), mod[l, 4], "rm_bwd", below=(sv["f1"]["y"], mod[l, 2], 0.5))
        dx, dmod[3], dmod[4], small["mix_norm"][l] = outs[:4]
        head = outs[4:]

        first_after, mid = None, None
        if l == 0:
            pending_b = leave(_grad_sources_b(gr)[3:], ROWS_TAIL, dx, "0b")
            first_after = pending_b[0][4]
            last_groups = []

            def mid(da):
                last_groups.append(leave([gr["g1"], gr["u1"], gr["d1"]], ROWS_A, da, "0a"))
                return forward_on(pending_b, last_groups[0][0][4], 0, GB_TAIL)
        below = (saved[l - 1]["f2"]["y"], mod[l - 1, 8], 0.5) if l > 0 else None
        dx, small["ffn1_norm"][l], head = ffn_bwd(
            dx, head, sv["f1"], ffn1_norm[l], 0, w["g1"], w["u1"], w["d1"], "1", below, first_after, mid)

        small["dmod"][l] = jnp.concatenate(dmod, axis=0)
        if l > 0:
            pending = leave([gr["g1"], gr["u1"], gr["d1"]] + _grad_sources_b(gr)[3:], ROWS_A + ROWS_TAIL, dx, l)

    grad_x = dx.reshape(x.shape)
    pending_a = last_groups[0]

    layout, small_rows = _small_layout(nl)
    pieces = {
        "dmod": jnp.concatenate(small["dmod"], axis=0),
        "ffn1_norm": jnp.concatenate(small["ffn1_norm"], axis=0),
        "mix_norm": jnp.concatenate(small["mix_norm"], axis=0),
        "ffn2_norm": jnp.concatenate(small["ffn2_norm"], axis=0),
        "q_a_norm": jnp.concatenate(small["q_a_norm"], axis=0),
        "kv_a_norm": jnp.concatenate(small["kv_a_norm"], axis=0),
        "pool_scale": jnp.concatenate(small["pool_scale"], axis=0),
        "final_norm": d_final,
        "loss": jnp.broadcast_to(loss_part[0:1, 0:1], (1, D_MODEL)),
        "pool_w": _pack_bf16_pairs(jnp.stack(small["pool_w"]).reshape(-1, D_MODEL)),
    }
    small_buf = jnp.concatenate([_to_rows(pieces[name]) for name in layout], axis=0)
    def landed_sums(gbuf, entries, after):
        for (send_sems, recv_sems, sums, recv, _), layer, row_off, tag in entries:
            _, recv = _split_wait(send_sems, recv_sems, N_CHIPS - 1, sums, recv, after, f"exchange_wait_{tag}")
            gbuf = _sum_slots_into(recv, gbuf, layer, row_off, "sum_grads")
        return gbuf

    gbuf = lax.empty((nl, ROWS_L, D_MODEL), F32)
    token_0a = forward_on(pending_a, dx, 0, GB_F1)
    spread = _spread_start(small_buf, me, token_0a, "small_start")
    gbuf = landed_sums(gbuf, [e for e in exchanges if e[3] != "0a"], spread[4])

    def swap(t):
        return t.transpose(0, 2, 1)

    def same(t):
        return t

    grads, updates = {}, {}

    def update_rows(gbuf, table):
        for wname, off, view in table:
            g, d_, nm, nv = _adamw_rows(view(wts[wname]), gbuf, off, view(mom_m[wname]), view(mom_v[wname]), "adamw_rows")
            grads[wname], updates[wname] = view(g), (view(d_), view(nm), view(nv))

    update_rows(gbuf, (("ffn2_w_gate", GB_F2, swap), ("ffn2_w_up", GB_F2 + FF_SH, swap),
                       ("ffn2_w_down", GB_F2 + 2 * FF_SH, same), ("w_out", GB_TAIL, same)))
    small_grads = {
        "w_kv_b": (gbuf[:, OFF_KV:OFF_KV + KV_SH_ROWS].reshape(nl, -1, KV_LORA).transpose(0, 2, 1), same),
        "w_in": (gbuf[:, OFF_IN:OFF_IN + IN_SH], swap),
        "w_q_b": (gbuf[:, OFF_Q:OFF_Q + Q_SH_ROWS].reshape(nl, -1, Q_LORA), swap),
    }
    for wname, (g, view) in small_grads.items():
        upd = _adamw_nd(view(wts[wname]), g, view(mom_m[wname]), view(mom_v[wname]), "adamw")
        grads[wname], updates[wname] = view(g), tuple(view(t) for t in upd)
    gbuf = landed_sums(gbuf, [e for e in exchanges if e[3] == "0a"], updates["w_q_b"][0])
    update_rows(gbuf, (("ffn1_w_gate", GB_F1, swap), ("ffn1_w_up", GB_F1 + FF_SH, swap),
                       ("ffn1_w_down", GB_F1 + 2 * FF_SH, same)))

    _, small_all = _split_wait(spread[0], spread[1], N_DEV - 1, spread[2], spread[3], updates["ffn1_w_down"][0],
                               "small_wait")
    pool_off, pool_rows = layout["pool_w"]
    small_sum = _sum_slots(small_all[:, :pool_off], "sum_small")
    pool_sum = _sum_slots(_unpack_bf16_pairs(small_all[:, pool_off:pool_off + pool_rows]), "sum_pool_w")

    def take(name, width=D_MODEL):
        off, n = layout[name]
        return small_sum[off:off + n, :width]

    late = {"ada_b": take("dmod").reshape(nl, N_MOD * D_MODEL),
            "ffn1_norm": take("ffn1_norm"), "mix_norm": take("mix_norm"), "ffn2_norm": take("ffn2_norm"),
            "q_a_norm": take("q_a_norm", Q_LORA), "kv_a_norm": take("kv_a_norm", KV_LORA),
            "pool_scale": take("pool_scale", POOL_WIDTH), "final_norm": take("final_norm").reshape(D_MODEL),
            "pool_w": pool_sum.reshape(pool_w.shape)}
    loss = take("loss")[0, 0]

    off, n = layout["dmod"]
    dmod_all = small_all[:, off:off + n].reshape(N_DEV, nl, N_MOD * D_MODEL)
    dmod_mine = lax.dynamic_slice_in_dim(dmod_all, me * ada_cols, ada_cols, axis=2)
    dmod_pad = jnp.pad(dmod_mine.transpose(1, 0, 2), ((0, 0), (0, LANE - N_DEV), (0, 0)))
    late["ada_w"] = _ada_grad(jnp.pad(c_all, ((0, LANE - N_DEV), (0, 0))), dmod_pad, "ada_grad")
    for name, g in late.items():
        grads[name], updates[name] = g, _adamw_nd(wts[name], g, mom_m[name], mom_v[name], "adamw")

    return (loss, grad_x, *[grads[n] for n in order], *[updates[n][0] for n in order],
            *[updates[n][1] for n in order], *[updates[n][2] for n in order])
```

```python
import math

import numpy as np
import jax
import jax.numpy as jnp
from jax import lax
from jax.experimental import pallas as pl
from jax.experimental.pallas import tpu as pltpu

F32 = jnp.float32
BF16 = jnp.bfloat16

N_DEV = 8
D_MODEL = 1024
D_FF = 2816
POOL_WIDTH = 512
POOL_WINDOWS = (2, 4, 8, 16)
POOL_GC = 128
N_HEADS = 4
QK_NOPE = 128
QK_ROPE = 64
V_HEAD = 128
QK_HEAD = QK_NOPE + QK_ROPE
HEAD_PAD = 256
Q_LORA = 384
KV_LORA = 256
IN_COLS = POOL_WIDTH + Q_LORA + KV_LORA + QK_ROPE
IN_PAD = 1280
ROPE_THETA = 10000.0
SOFTMAX_SCALE = 1.0 / math.sqrt(QK_HEAD)
EPS = 1e-6
N_MOD = 9

ADAM_LR = 0.001
ADAM_B1 = 0.9
ADAM_B2 = 0.999
ADAM_EPS = 1e-08
ADAM_WD = 0.01
ADAM_STEP = 10

LANE = 128
VMEM_LIMIT = 56 * 1024 * 1024

FF_SH = D_FF // N_DEV
OFF_G1, OFF_U1, OFF_D1 = 0, FF_SH, 2 * FF_SH
OFF_G2, OFF_U2, OFF_D2 = 3 * FF_SH, 4 * FF_SH, 5 * FF_SH
OFF_OUT = 6 * FF_SH
OFF_KV = OFF_OUT + 128
OFF_IN = OFF_KV + 32
OFF_Q = OFF_IN + 160
Q_PAD_ROWS = 64
ROWS_L = OFF_Q + Q_PAD_ROWS
IN_SH = IN_COLS // N_DEV
Q_SH_ROWS = (N_HEADS * QK_HEAD // N_DEV) * Q_LORA // D_MODEL
KV_SH_ROWS = (N_HEADS * (QK_NOPE + V_HEAD) // N_DEV) * KV_LORA // D_MODEL


def _tile(dim, target):
    if dim <= target:
        return dim
    best = None
    for t in range(LANE, target + 1, LANE):
        if dim % t == 0:
            best = t
    assert best is not None, (dim, target)
    return best


def _params(sem):
    return pltpu.CompilerParams(dimension_semantics=sem, vmem_limit_bytes=VMEM_LIMIT)


def _mesh_pos():
    return lax.axis_index("x"), lax.axis_index("y"), lax.axis_index("c")


def _all_gather(x, name):
    m, n = x.shape

    def body(x_ref, out_ref, send_sems, recv_sems, local_sem):
        px, py, pc = _mesh_pos()
        me, sibling = (px, py, pc), (px, py, 1 - pc)
        chips = [(1 - px, py), (px, 1 - py), (1 - px, 1 - py)]

        def rows(bx, by, bc):
            return out_ref.at[pl.ds((4 * bx + 2 * by + bc) * m, m), :]

        def copy(k, block, to, src=None):
            return pltpu.make_async_remote_copy(
                src_ref=rows(*block) if src is None else src, dst_ref=rows(*block),
                send_sem=send_sems.at[k], recv_sem=recv_sems.at[k],
                device_id=to, device_id_type=pl.DeviceIdType.MESH)

        mine = pltpu.make_async_copy(x_ref, rows(*me), local_sem)
        mine.start()
        first = [copy(0, me, sibling, src=x_ref)]
        first += [copy(1 + j, me, (*chip, pc), src=x_ref) for j, chip in enumerate(chips)]
        for cp in first:
            cp.start()
        passed = [copy(4 + j, (*chip, pc), sibling) for j, chip in enumerate(chips)]
        for j, chip in enumerate(chips):
            copy(1 + j, (*chip, pc), me).wait_recv()
            passed[j].start()
        copy(0, sibling, me).wait_recv()
        for j, chip in enumerate(chips):
            copy(4 + j, (*chip, 1 - pc), me).wait_recv()
        for cp in first + passed:
            cp.wait_send()
        mine.wait()

    hbm = pl.BlockSpec(memory_space=pltpu.HBM)
    return pl.pallas_call(
        body, name=name,
        out_shape=jax.ShapeDtypeStruct((N_DEV * m, n), x.dtype),
        in_specs=[hbm], out_specs=hbm,
        scratch_shapes=[pltpu.SemaphoreType.DMA((7,)), pltpu.SemaphoreType.DMA((7,)),
                        pltpu.SemaphoreType.DMA],
    )(x)


SMALL_ROWS = ROWS_L - OFF_KV
ROWS_A = [FF_SH] * 3
ROWS_B = [FF_SH] * 3 + [128, SMALL_ROWS]
ROWS_ALL = ROWS_A + ROWS_B
SPLIT_AB = sum(ROWS_A)
ROWS_TAIL = [128, SMALL_ROWS]
GB_F2, GB_F1, GB_TAIL = 0, SPLIT_AB, 2 * SPLIT_AB
HBM_SPEC = pl.BlockSpec(memory_space=pltpu.HBM)
SEM_SPEC = pl.BlockSpec(memory_space=pltpu.SEMAPHORE)
ANY_SPEC = pl.BlockSpec(memory_space=pl.ANY)
EFFECT = pltpu.SideEffectType.DATAFLOW_SIDE_EFFECTING


def _hbm(t):
    return pltpu.with_memory_space_constraint(t, pltpu.HBM)


def _whole_wait(ref, send_sem, recv_sem, peer):
    return pltpu.make_async_remote_copy(src_ref=ref, dst_ref=ref, send_sem=send_sem, recv_sem=recv_sem,
                                        device_id=peer, device_id_type=pl.DeviceIdType.MESH)


def _offsets(rows_list):
    return [sum(rows_list[:i]) for i in range(len(rows_list))]


def _gather_start(packed, rows_list, after, name):
    n = len(rows_list)
    offs = _offsets(rows_list)
    lands = [_hbm(lax.empty((N_DEV * rows, D_MODEL), BF16)) for rows in rows_list]

    def body(packed_ref, *refs):
        land = refs[:n]
        send_sems, recv_sems = refs[n + 1], refs[n + 2]
        token = refs[-1]
        px, py, pc = _mesh_pos()
        me = 4 * px + 2 * py + pc
        peers = [(px, py, 1 - pc), (1 - px, py, pc), (px, 1 - py, pc), (1 - px, 1 - py, pc)]
        for k, peer in enumerate(peers):
            for off, rows, land_ref in zip(offs, rows_list, land):
                pltpu.make_async_remote_copy(
                    src_ref=packed_ref.at[pl.ds(off, rows), :], dst_ref=land_ref.at[pl.ds(me * rows, rows), :],
                    send_sem=send_sems.at[k], recv_sem=recv_sems.at[k],
                    device_id=peer, device_id_type=pl.DeviceIdType.MESH).start()
        token[...] = jnp.zeros_like(token)

    outs = pl.pallas_call(
        body, name=name,
        out_shape=(pltpu.SemaphoreType.DMA((4,)), pltpu.SemaphoreType.DMA((4,)), pltpu.HBM(packed.shape, BF16),
                   *[pltpu.HBM(t.shape, BF16) for t in lands], jax.ShapeDtypeStruct((8, LANE), F32)),
        in_specs=(HBM_SPEC,) * (1 + n) + (ANY_SPEC,),
        out_specs=(SEM_SPEC, SEM_SPEC) + (HBM_SPEC,) * (1 + n) + (pl.BlockSpec(memory_space=pltpu.VMEM),),
        input_output_aliases={i: 2 + i for i in range(1 + n)},
        compiler_params=pltpu.CompilerParams(has_side_effects=EFFECT),
    )(_hbm(packed), *lands, after)
    return outs[0], outs[1], outs[2], list(outs[3:3 + n]), outs[-1]


def _gather_wait(send_sems, recv_sems, packed, lands, after, name):
    n = len(lands)

    def body(packed_ref, *refs):
        s_sems, r_sems = refs[n], refs[n + 1]
        me = _mesh_pos()
        for k in range(4):
            cp = _whole_wait(packed_ref, s_sems.at[k], r_sems.at[k], me)
            cp.wait_send()
            cp.wait_recv()

    outs = pl.pallas_call(
        body, name=name,
        out_shape=(pltpu.HBM(packed.shape, BF16), *[pltpu.HBM(t.shape, BF16) for t in lands]),
        in_specs=(HBM_SPEC,) * (1 + n) + (SEM_SPEC, SEM_SPEC, ANY_SPEC),
        out_specs=(HBM_SPEC,) * (1 + n),
        input_output_aliases={i: i for i in range(1 + n)},
        compiler_params=pltpu.CompilerParams(has_side_effects=EFFECT),
    )(packed, *lands, send_sems, recv_sems, after)
    return outs[0], list(outs[1:])


def _gather_finish(packed, rows_list, lands, name):
    n = len(rows_list)
    offs = _offsets(rows_list)

    def body(packed_ref, *refs):
        land = refs[n:2 * n]
        send_sems, recv_sems, stage, stage_sem = refs[2 * n:]
        px, py, pc = _mesh_pos()
        me = 4 * px + 2 * py + pc
        sibling = (px, py, 1 - pc)
        load = pltpu.make_async_copy(packed_ref, stage, stage_sem)
        load.start()
        load.wait()
        for off, rows, land_ref in zip(offs, rows_list, land):
            pltpu.make_async_copy(stage.at[pl.ds(off, rows), :], land_ref.at[pl.ds(me * rows, rows), :],
                                  stage_sem).start()
        for j, (cx, cy) in enumerate([(1 - px, py), (px, 1 - py), (1 - px, 1 - py)]):
            block = 4 * cx + 2 * cy + pc
            for rows, land_ref in zip(rows_list, land):
                blk = land_ref.at[pl.ds(block * rows, rows), :]
                pltpu.make_async_remote_copy(src_ref=blk, dst_ref=blk, send_sem=send_sems.at[j],
                                             recv_sem=recv_sems.at[j], device_id=sibling,
                                             device_id_type=pl.DeviceIdType.MESH).start()
        for j in range(3):
            cp = _whole_wait(packed_ref, send_sems.at[j], recv_sems.at[j], sibling)
            cp.wait_recv()
            cp.wait_send()
        pltpu.make_async_copy(stage, packed_ref, stage_sem).wait()

    outs = pl.pallas_call(
        body, name=name,
        out_shape=tuple(jax.ShapeDtypeStruct(t.shape, BF16) for t in lands),
        in_specs=(HBM_SPEC,) * (1 + n), out_specs=(HBM_SPEC,) * n,
        input_output_aliases={1 + i: i for i in range(n)},
        scratch_shapes=[pltpu.SemaphoreType.DMA((3,)), pltpu.SemaphoreType.DMA((3,)),
                        pltpu.VMEM(packed.shape, BF16), pltpu.SemaphoreType.DMA],
    )(packed, *lands)
    return list(outs)


N_CHIPS = 4


def _pair_start(srcs, rows_list, after, name):
    n = len(rows_list)
    offs = _offsets(rows_list)
    land = lax.empty((N_CHIPS, sum(rows_list), D_MODEL), BF16)

    def body(*refs):
        src, land_ref = refs[:n], refs[n]
        send_sems, recv_sems = refs[n + 2], refs[n + 3]
        token = refs[-1]
        px, py, pc = _mesh_pos()
        for k in range(N_CHIPS):
            block = 2 * k + (1 - pc)
            for off, rows, src_ref in zip(offs, rows_list, src):
                pltpu.make_async_remote_copy(
                    src_ref=src_ref.at[pl.ds(block * rows, rows), :], dst_ref=land_ref.at[k, pl.ds(off, rows), :],
                    send_sem=send_sems.at[0], recv_sem=recv_sems.at[0],
                    device_id=(px, py, 1 - pc), device_id_type=pl.DeviceIdType.MESH).start()
        token[...] = jnp.zeros_like(token)

    outs = pl.pallas_call(
        body, name=name,
        out_shape=(pltpu.SemaphoreType.DMA((1,)), pltpu.SemaphoreType.DMA((1,)),
                   *[pltpu.HBM(t.shape, BF16) for t in srcs], pltpu.HBM(land.shape, BF16),
                   jax.ShapeDtypeStruct((8, LANE), F32)),
        in_specs=(HBM_SPEC,) * (n + 1) + (ANY_SPEC,),
        out_specs=(SEM_SPEC, SEM_SPEC) + (HBM_SPEC,) * (n + 1) + (pl.BlockSpec(memory_space=pltpu.VMEM),),
        input_output_aliases={i: 2 + i for i in range(n + 1)},
        compiler_params=pltpu.CompilerParams(has_side_effects=EFFECT),
    )(*[_hbm(t) for t in srcs], _hbm(land), after)
    return outs[0], outs[1], list(outs[2:2 + n]), outs[2 + n], outs[-1]


def _split_wait(send_sems, recv_sems, n_sems, srcs, land, after, name):
    n = len(srcs)

    def body(*refs):
        land_ref = refs[n]
        s_sems, r_sems = refs[n + 1], refs[n + 2]
        me = _mesh_pos()
        for k in range(n_sems):
            cp = _whole_wait(land_ref.at[0] if n_sems > 1 else land_ref, s_sems.at[k], r_sems.at[k], me)
            cp.wait_send()
            cp.wait_recv()

    outs = pl.pallas_call(
        body, name=name,
        out_shape=(*[pltpu.HBM(t.shape, t.dtype) for t in srcs], pltpu.HBM(land.shape, land.dtype)),
        in_specs=(HBM_SPEC,) * (n + 1) + (SEM_SPEC, SEM_SPEC, ANY_SPEC),
        out_specs=(HBM_SPEC,) * (n + 1),
        input_output_aliases={i: i for i in range(n + 1)},
        compiler_params=pltpu.CompilerParams(has_side_effects=EFFECT),
    )(*srcs, land, send_sems, recv_sems, after)
    return list(outs[:n]), outs[n]


def _spread_start(x, me_id, after, name):
    land = lax.dynamic_update_slice_in_dim(lax.empty((N_DEV,) + x.shape, x.dtype), x[None], me_id, axis=0)

    def body(x_ref, land_ref, after_ref, send_sems, recv_sems, x_thru, land_thru, token):
        px, py, pc = _mesh_pos()
        me = 4 * px + 2 * py + pc
        for k in range(1, N_DEV):
            qx = 1 - px if k & 4 else px
            qy = 1 - py if k & 2 else py
            qc = 1 - pc if k & 1 else pc
            pltpu.make_async_remote_copy(
                src_ref=x_ref, dst_ref=land_ref.at[me], send_sem=send_sems.at[k - 1], recv_sem=recv_sems.at[k - 1],
                device_id=(qx, qy, qc), device_id_type=pl.DeviceIdType.MESH).start()
        token[...] = jnp.zeros_like(token)

    outs = pl.pallas_call(
        body, name=name,
        out_shape=(pltpu.SemaphoreType.DMA((N_DEV - 1,)), pltpu.SemaphoreType.DMA((N_DEV - 1,)),
                   pltpu.HBM(x.shape, x.dtype), pltpu.HBM(land.shape, land.dtype), jax.ShapeDtypeStruct((8, LANE), F32)),
        in_specs=(HBM_SPEC, HBM_SPEC, ANY_SPEC),
        out_specs=(SEM_SPEC, SEM_SPEC, HBM_SPEC, HBM_SPEC, pl.BlockSpec(memory_space=pltpu.VMEM)),
        input_output_aliases={0: 2, 1: 3},
        compiler_params=pltpu.CompilerParams(has_side_effects=EFFECT),
    )(_hbm(x), _hbm(land), after)
    return outs[0], outs[1], [outs[2]], outs[3], outs[4]


def _pair_sum(srcs, rows_list, land, core, name):
    n = len(rows_list)
    offs = _offsets(rows_list)
    total = sum(rows_list)

    def body(core_ref, *refs):
        src, land_ref, out_ref = refs[:n], refs[n], refs[n + 1]
        for off, rows, src_ref in zip(offs, rows_list, src):
            out_ref[pl.ds(off, rows), :] = (src_ref[...].astype(F32)
                                            + land_ref[pl.ds(off, rows), :].astype(F32)).astype(BF16)

    slot = pl.BlockSpec((None, total, D_MODEL), lambda k, c: (k, 0, 0))
    grid_spec = pltpu.PrefetchScalarGridSpec(
        num_scalar_prefetch=1, grid=(N_CHIPS,),
        in_specs=[pl.BlockSpec((rows, D_MODEL), lambda k, c: (2 * k + c[0], 0)) for rows in rows_list] + [slot],
        out_specs=slot)
    return pl.pallas_call(
        body, name=name, grid_spec=grid_spec,
        out_shape=jax.ShapeDtypeStruct((N_CHIPS, total, D_MODEL), BF16),
        compiler_params=_params(("parallel",)),
    )(core, *srcs, land)


def _chip_exchange_start(sums, chip, after, name):
    own = lax.dynamic_index_in_dim(sums, chip, axis=0, keepdims=True)
    recv = lax.dynamic_update_slice_in_dim(lax.empty(sums.shape, BF16), own, chip, axis=0)

    def body(sums_ref, recv_ref, after_ref, send_sems, recv_sems, sums_thru, recv_thru, token):
        px, py, pc = _mesh_pos()
        for k in range(1, N_CHIPS):
            qx = 1 - px if k & 2 else px
            qy = 1 - py if k & 1 else py
            pltpu.make_async_remote_copy(
                src_ref=sums_ref.at[2 * qx + qy], dst_ref=recv_ref.at[2 * px + py],
                send_sem=send_sems.at[k - 1], recv_sem=recv_sems.at[k - 1],
                device_id=(qx, qy, pc), device_id_type=pl.DeviceIdType.MESH).start()
        token[...] = jnp.zeros_like(token)

    outs = pl.pallas_call(
        body, name=name,
        out_shape=(pltpu.SemaphoreType.DMA((N_CHIPS - 1,)), pltpu.SemaphoreType.DMA((N_CHIPS - 1,)),
                   pltpu.HBM(sums.shape, BF16), pltpu.HBM(recv.shape, BF16), jax.ShapeDtypeStruct((8, LANE), F32)),
        in_specs=(HBM_SPEC, HBM_SPEC, ANY_SPEC),
        out_specs=(SEM_SPEC, SEM_SPEC, HBM_SPEC, HBM_SPEC, pl.BlockSpec(memory_space=pltpu.VMEM)),
        input_output_aliases={0: 2, 1: 3},
        compiler_params=pltpu.CompilerParams(has_side_effects=EFFECT),
    )(_hbm(sums), _hbm(recv), after)
    return outs[0], outs[1], [outs[2]], outs[3], outs[4]


def _sum_slots_into(recv, buf, layer, row_off, name):
    slots, r, n = recv.shape
    tr = _row_tile(math.gcd(r, row_off) if row_off else r, 512)
    first = row_off // tr

    def body(in_ref, buf_ref, out_ref):
        acc = in_ref[0].astype(F32)
        for j in range(1, slots):
            acc = acc + in_ref[j].astype(F32)
        out_ref[...] = acc

    return pl.pallas_call(
        body, name=name, grid=(r // tr,), out_shape=jax.ShapeDtypeStruct(buf.shape, F32),
        in_specs=[pl.BlockSpec((slots, tr, n), lambda i: (0, i, 0)), ANY_SPEC],
        out_specs=pl.BlockSpec((None, tr, n), lambda i: (layer, first + i, 0)),
        input_output_aliases={1: 0},
        compiler_params=_params(("parallel",)),
    )(recv, buf)


def _sum_slots(recv, name, after=None):
    _, r, n = recv.shape
    tr = _row_tile(r, 512)

    def body(in_ref, *refs):
        acc = in_ref[0].astype(F32)
        for j in range(1, N_DEV):
            acc = acc + in_ref[j].astype(F32)
        refs[-1][...] = acc

    grid = (r // tr,)
    in_specs, out_spec = [pl.BlockSpec((N_DEV, tr, n), lambda i: (0, i, 0))], pl.BlockSpec((tr, n), lambda i: (i, 0))
    args = [recv]
    if after is not None:
        in_specs.append(ANY_SPEC)
        args.append(after)
    return pl.pallas_call(
        body, name=name, grid=grid,
        out_shape=jax.ShapeDtypeStruct((r, n), F32),
        in_specs=in_specs, out_specs=out_spec,
        compiler_params=_params(("parallel",)),
    )(*args)


def _row_tile(rows, target):
    if rows <= target:
        return rows
    best = None
    for t in range(16, target + 1, 16):
        if rows % t == 0:
            best = t
    assert best is not None, rows
    return best


_DIMS = {"nn": ((1,), (0,)), "nt": ((1,), (1,)), "tn": ((0,), (0,))}


def _mm(a, b, mode, name, out_dtype=F32, res=None, gate=None, gate_factor=1.0, tm=512, tn=1408, after=None):
    assert (res is None) == (gate is None)
    if mode == "tn":
        kdim, m = a.shape
    else:
        m, kdim = a.shape
    n = b.shape[0] if mode == "nt" else b.shape[1]
    tm, tn = _tile(m, tm), _tile(n, tn)
    a_spec = (pl.BlockSpec((kdim, tm), lambda i, j: (0, i)) if mode == "tn"
              else pl.BlockSpec((tm, kdim), lambda i, j: (i, 0)))
    b_spec = (pl.BlockSpec((tn, kdim), lambda i, j: (j, 0)) if mode == "nt"
              else pl.BlockSpec((kdim, tn), lambda i, j: (0, j)))
    o_spec = pl.BlockSpec((tm, tn), lambda i, j: (i, j))
    dims = (_DIMS[mode], ((), ()))
    has_res = res is not None

    def body(a_ref, b_ref, *refs):
        y = lax.dot_general(a_ref[...].astype(BF16), b_ref[...].astype(BF16), dims,
                            preferred_element_type=F32)
        if has_res:
            res_ref, gate_ref = refs[0], refs[1]
            y_ref, o_ref = refs[-2], refs[-1]
            y_ref[...] = y.astype(BF16)
            o_ref[...] = res_ref[...] + (gate_factor * gate_ref[...]) * y
        else:
            refs[-1][...] = y.astype(out_dtype)

    in_specs, args = [a_spec, b_spec], [a, b]
    if has_res:
        gate_spec, gate_arg = _vec_in(gate, tile=tn)
        in_specs += [o_spec, gate_spec]
        args += [res, gate_arg]
        out_shape = (jax.ShapeDtypeStruct((m, n), BF16), jax.ShapeDtypeStruct((m, n), F32))
        out_specs = (o_spec, o_spec)
    else:
        out_shape, out_specs = jax.ShapeDtypeStruct((m, n), out_dtype), o_spec
    if after is not None:
        in_specs.append(ANY_SPEC)
        args.append(after)
    return pl.pallas_call(
        body, name=name, grid=(m // tm, n // tn), out_shape=out_shape,
        in_specs=in_specs, out_specs=out_specs,
        compiler_params=_params(("parallel", "parallel")),
    )(*args)


def _vec_in(v, tile=None):
    if isinstance(v, tuple):
        table, row = v
        if tile is None:
            return pl.BlockSpec((None, 1, table.shape[-1]), lambda *idx: (row, 0, 0)), table
        return pl.BlockSpec((None, 1, tile), lambda i, j: (row, 0, j)), table
    if tile is None:
        return pl.BlockSpec((1, v.shape[-1]), lambda *idx: (0, 0)), v
    return pl.BlockSpec((1, tile), lambda i, j: (0, j)), v


def _vec_spec(width):
    return pl.BlockSpec((1, width), lambda i: (0, 0))


def _rm_bwd(dh, x, dres, gw, scale, name, below=None):
    s, d = x.shape
    ts = _tile(s, 256)
    factor = None if below is None else below[2]

    def body(dh_ref, x_ref, dres_ref, gw_ref, sc_ref, *refs):
        dx_ref, dsh_ref, dsc_ref, dgw_ref = refs[-6:-2] if below is not None else refs[-4:]

        @pl.when(pl.program_id(0) == 0)
        def _():
            dsh_ref[...] = jnp.zeros_like(dsh_ref)
            dsc_ref[...] = jnp.zeros_like(dsc_ref)
            dgw_ref[...] = jnp.zeros_like(dgw_ref)
            if below is not None:
                refs[-1][...] = jnp.zeros_like(refs[-1])

        xv, dhv, gwv = x_ref[...], dh_ref[...], gw_ref[...]
        r = lax.rsqrt(jnp.mean(xv * xv, axis=-1, keepdims=True) + EPS)
        xn = xv * r
        y = xn * gwv
        dsh_ref[...] += jnp.sum(dhv, axis=0, keepdims=True)
        dsc_ref[...] += jnp.sum(dhv * y, axis=0, keepdims=True)
        dy = dhv * (1 + sc_ref[...])
        dgw_ref[...] += jnp.sum(dy * xn, axis=0, keepdims=True)
        dxn = dy * gwv
        dx = dres_ref[...] + r * (dxn - xn * jnp.mean(dxn * xn, axis=-1, keepdims=True))
        dx_ref[...] = dx
        if below is not None:
            yb_ref, gb_ref, dyb_ref, dgb_ref = refs[0], refs[1], refs[-2], refs[-1]
            dyb_ref[...] = ((factor * gb_ref[...]) * dx).astype(BF16)
            dgb_ref[...] += jnp.sum((factor * dx) * yb_ref[...].astype(F32), axis=0, keepdims=True)

    row = pl.BlockSpec((ts, d), lambda i: (i, 0))
    vec = jax.ShapeDtypeStruct((1, d), F32)
    (gw_spec, gw), (sc_spec, scale) = _vec_in(gw), _vec_in(scale)
    in_specs, args = [row, row, row, gw_spec, sc_spec], [dh, x, dres, gw, scale]
    out_shape = [jax.ShapeDtypeStruct((s, d), F32), vec, vec, vec]
    out_specs = [row, _vec_spec(d), _vec_spec(d), _vec_spec(d)]
    if below is not None:
        gate_spec, gate_arg = _vec_in(below[1])
        in_specs += [row, gate_spec]
        args += [below[0], gate_arg]
        out_shape += [jax.ShapeDtypeStruct((s, d), BF16), vec]
        out_specs += [row, _vec_spec(d)]
    return pl.pallas_call(
        body, name=name, grid=(s // ts,), out_shape=tuple(out_shape),
        in_specs=in_specs, out_specs=tuple(out_specs),
        compiler_params=_params(("arbitrary",)),
    )(*args)


def _gate_bwd(dx, y, gate, factor, name):
    s, d = dx.shape
    ts = _tile(s, 256)

    def body(dx_ref, y_ref, g_ref, dy_ref, dg_ref):
        @pl.when(pl.program_id(0) == 0)
        def _():
            dg_ref[...] = jnp.zeros_like(dg_ref)

        dxv = dx_ref[...]
        dy_ref[...] = ((factor * g_ref[...]) * dxv).astype(BF16)
        dg_ref[...] += jnp.sum((factor * dxv) * y_ref[...].astype(F32), axis=0, keepdims=True)

    row = pl.BlockSpec((ts, d), lambda i: (i, 0))
    return pl.pallas_call(
        body, name=name, grid=(s // ts,),
        out_shape=(jax.ShapeDtypeStruct((s, d), BF16), jax.ShapeDtypeStruct((1, d), F32)),
        in_specs=[row, row, _vec_in(gate)[0]], out_specs=(row, _vec_spec(d)),
        compiler_params=_params(("arbitrary",)),
    )(dx, y, _vec_in(gate)[1])


def _norm_mm(x, gw, shift, scale, w, name, tm=1024):
    s, d = x.shape
    n = w.shape[0]
    tm = _tile(s, tm)

    def body(x_ref, gw_ref, sh_ref, sc_ref, w_ref, h_ref, z_ref):
        xv = x_ref[...]
        r = lax.rsqrt(jnp.mean(xv * xv, axis=-1, keepdims=True) + EPS)
        hb = (((xv * r) * gw_ref[...]) * (1 + sc_ref[...]) + sh_ref[...]).astype(BF16)
        h_ref[...] = hb
        z_ref[...] = lax.dot_general(hb, w_ref[...], (((1,), (1,)), ((), ())), preferred_element_type=F32)

    row = pl.BlockSpec((tm, d), lambda i: (i, 0))
    return pl.pallas_call(
        body, name=name, grid=(s // tm,),
        out_shape=(jax.ShapeDtypeStruct((s, d), BF16), jax.ShapeDtypeStruct((s, n), F32)),
        in_specs=[row, _vec_in(gw)[0], _vec_in(shift)[0], _vec_in(scale)[0], pl.BlockSpec((n, d), lambda i: (0, 0))],
        out_specs=(row, pl.BlockSpec((tm, n), lambda i: (i, 0))),
        compiler_params=_params(("parallel",)),
    )(x, _vec_in(gw)[1], _vec_in(shift)[1], _vec_in(scale)[1], w)


FFN_TM, FFN_TF = 2048, 256


def _ffn_up(x, gw, shift, scale, wg, wu, name, after=None):
    s, d = x.shape
    f = wg.shape[0]
    tm, tf = _tile(s, FFN_TM), _tile(f, FFN_TF)
    nt = (((1,), (1,)), ((), ()))

    def body(x_ref, gw_ref, sh_ref, sc_ref, wg_ref, wu_ref, *refs):
        h_ref, a_ref, b_ref, t_ref = refs[-4:]

        @pl.when(pl.program_id(1) == 0)
        def _():
            xv = x_ref[...]
            r = lax.rsqrt(jnp.mean(xv * xv, axis=-1, keepdims=True) + EPS)
            h_ref[...] = (((xv * r) * gw_ref[...]) * (1 + sc_ref[...]) + sh_ref[...]).astype(BF16)

        hb = h_ref[...]
        av = lax.dot_general(hb, wg_ref[...], nt, preferred_element_type=F32)
        bv = lax.dot_general(hb, wu_ref[...], nt, preferred_element_type=F32)
        a_ref[...] = av.astype(BF16)
        b_ref[...] = bv.astype(BF16)
        t_ref[...] = ((av * jax.nn.sigmoid(av)) * bv).astype(BF16)

    row = pl.BlockSpec((tm, d), lambda i, j: (i, 0))
    wblk = pl.BlockSpec((tf, d), lambda i, j: (j, 0))
    blk = pl.BlockSpec((tm, tf), lambda i, j: (i, j))
    wide = jax.ShapeDtypeStruct((s, f), BF16)
    vec_specs, vec_args = zip(*[_vec_in(v) for v in (gw, shift, scale)])
    in_specs, args = [row, *vec_specs, wblk, wblk], [x, *vec_args, wg, wu]
    if after is not None:
        in_specs.append(ANY_SPEC)
        args.append(after)
    return pl.pallas_call(
        body, name=name, grid=(s // tm, f // tf),
        out_shape=(jax.ShapeDtypeStruct((s, d), BF16), wide, wide, wide),
        in_specs=in_specs, out_specs=(row, blk, blk, blk),
        compiler_params=_params(("parallel", "arbitrary")),
    )(*args)


def _ffn_bwd_cols(dy, h, a, b, t, wd, name, after=None):
    s, d = dy.shape
    f = wd.shape[0]
    tf = _tile(f, FFN_TF)
    nt = (((1,), (1,)), ((), ()))
    tn = (((0,), (0,)), ((), ()))

    def body(dy_ref, h_ref, a_ref, b_ref, t_ref, wd_ref, *refs):
        da_ref, db_ref, gd_ref, gg_ref, gu_ref = refs[-5:]
        dyb, hb = dy_ref[...], h_ref[...]
        dtv = lax.dot_general(dyb, wd_ref[...], nt, preferred_element_type=F32)
        av, bv = a_ref[...].astype(F32), b_ref[...].astype(F32)
        sg = jax.nn.sigmoid(av)
        dbv = (dtv * (av * sg)).astype(BF16)
        dav = ((dtv * bv) * (sg * (1 + av * (1 - sg)))).astype(BF16)
        da_ref[...] = dav
        db_ref[...] = dbv
        gd_ref[...] = lax.dot_general(t_ref[...], dyb, tn, preferred_element_type=F32).astype(BF16)
        gg_ref[...] = lax.dot_general(dav, hb, tn, preferred_element_type=F32).astype(BF16)
        gu_ref[...] = lax.dot_general(dbv, hb, tn, preferred_element_type=F32).astype(BF16)

    whole = pl.BlockSpec((s, d), lambda j: (0, 0))
    col = pl.BlockSpec((s, tf), lambda j: (0, j))
    wblk = pl.BlockSpec((tf, d), lambda j: (j, 0))
    wide, wgrad = jax.ShapeDtypeStruct((s, f), BF16), jax.ShapeDtypeStruct((f, d), BF16)
    in_specs, args = [whole, whole, col, col, col, wblk], [dy, h, a, b, t, wd]
    if after is not None:
        in_specs.append(ANY_SPEC)
        args.append(after)
    return pl.pallas_call(
        body, name=name, grid=(f // tf,), out_shape=(wide, wide, wgrad, wgrad, wgrad),
        in_specs=in_specs, out_specs=(col, col, wblk, wblk, wblk),
        compiler_params=_params(("parallel",)),
    )(*args)


def _mm_pair(a1, b1, a2, b2, name, tm=1024, tn=512, after=None):
    m, kdim = a1.shape
    n = b1.shape[1]
    tm, tn = _tile(m, tm), _tile(n, tn)

    def body(a1_ref, b1_ref, a2_ref, b2_ref, *refs):
        refs[-1][...] = (jnp.dot(a1_ref[...], b1_ref[...], preferred_element_type=F32)
                         + jnp.dot(a2_ref[...], b2_ref[...], preferred_element_type=F32))

    a_spec = pl.BlockSpec((tm, kdim), lambda i, j: (i, 0))
    b_spec = pl.BlockSpec((kdim, tn), lambda i, j: (0, j))
    in_specs, args = [a_spec, b_spec, a_spec, b_spec], [a1, b1, a2, b2]
    if after is not None:
        in_specs.append(ANY_SPEC)
        args.append(after)
    return pl.pallas_call(
        body, name=name, grid=(m // tm, n // tn), out_shape=jax.ShapeDtypeStruct((m, n), F32),
        in_specs=in_specs, out_specs=pl.BlockSpec((tm, tn), lambda i, j: (i, j)),
        compiler_params=_params(("parallel", "parallel")),
    )(*args)


def _pool_counts(s):
    return (lax.broadcasted_iota(jnp.int32, (s, POOL_GC), 0))


def _pool_fwd(z, pool_w, pool_scale, name):
    s = z.shape[0]

    def body(u_ref, w_ref, sc_ref, y_ref, diff_ref):
        t = lax.broadcasted_iota(jnp.int32, (s, POOL_GC), 0)
        for g, win in enumerate(POOL_WINDOWS):
            cols = slice(g * POOL_GC, (g + 1) * POOL_GC)
            u = u_ref[:, cols]
            acc, step = u, 1
            while step < win:
                acc = acc + jnp.where(t >= step, pltpu.roll(acc, step, 0), 0.0)
                step *= 2
            cnt = jnp.minimum(t + 1, win).astype(F32)
            diff = acc / cnt - u
            diff_ref[:, cols] = diff
            ypre = jnp.dot(diff.astype(BF16), w_ref[g].astype(BF16), preferred_element_type=F32)
            y_ref[:, cols] = (ypre * sc_ref[:, cols]).astype(BF16)

    return pl.pallas_call(
        body, name=name, grid=(1,),
        out_shape=(jax.ShapeDtypeStruct((s, POOL_WIDTH), BF16), jax.ShapeDtypeStruct((s, POOL_WIDTH), F32)),
        in_specs=[pl.BlockSpec((s, POOL_WIDTH), lambda i: (0, 0)),
                  pl.BlockSpec(pool_w.shape, lambda i: (0, 0, 0)),
                  pl.BlockSpec((1, POOL_WIDTH), lambda i: (0, 0))],
        out_specs=(pl.BlockSpec((s, POOL_WIDTH), lambda i: (0, 0)),
                   pl.BlockSpec((s, POOL_WIDTH), lambda i: (0, 0))),
        compiler_params=_params(("arbitrary",)),
    )(z, pool_w, pool_scale)


def _pool_bwd(dycat, diff, pool_w, pool_scale, name):
    s = diff.shape[0]

    def body(dy_ref, diff_ref, w_ref, sc_ref, du_ref, dw_ref, dsc_ref):
        t = lax.broadcasted_iota(jnp.int32, (s, POOL_GC), 0)
        for g, win in enumerate(POOL_WINDOWS):
            cols = slice(g * POOL_GC, (g + 1) * POOL_GC)
            dy, dfb, wb = dy_ref[:, cols], diff_ref[:, cols].astype(BF16), w_ref[g].astype(BF16)
            ypre = jnp.dot(dfb, wb, preferred_element_type=F32)
            dsc_ref[:, cols] = jnp.sum(dy * ypre, axis=0, keepdims=True)
            dypre = (dy * sc_ref[:, cols]).astype(BF16)
            ddiff = lax.dot_general(dypre, wb, (((1,), (1,)), ((), ())), preferred_element_type=F32)
            dw_ref[g] = lax.dot_general(dfb, dypre, (((0,), (0,)), ((), ())), preferred_element_type=F32)
            cnt = jnp.minimum(t + 1, win).astype(F32)
            acc, step = ddiff / cnt, 1
            while step < win:
                acc = acc + jnp.where(t < s - step, pltpu.roll(acc, s - step, 0), 0.0)
                step *= 2
            du_ref[:, cols] = acc - ddiff

    full = pl.BlockSpec((s, POOL_WIDTH), lambda i: (0, 0))
    return pl.pallas_call(
        body, name=name, grid=(1,),
        out_shape=(jax.ShapeDtypeStruct((s, POOL_WIDTH), F32),
                   jax.ShapeDtypeStruct(pool_w.shape, F32),
                   jax.ShapeDtypeStruct((1, POOL_WIDTH), F32)),
        in_specs=[full, full, pl.BlockSpec(pool_w.shape, lambda i: (0, 0, 0)),
                  pl.BlockSpec((1, POOL_WIDTH), lambda i: (0, 0))],
        out_specs=(full, pl.BlockSpec(pool_w.shape, lambda i: (0, 0, 0)),
                   pl.BlockSpec((1, POOL_WIDTH), lambda i: (0, 0))),
        compiler_params=_params(("arbitrary",)),
    )(dycat, diff, pool_w, pool_scale)


def _rope_tables(positions, name):
    s = positions.shape[0]
    ts = _tile(s, 512)
    freq = 1.0 / (ROPE_THETA ** (np.arange(0, QK_ROPE, 2, dtype=np.float32) / QK_ROPE))
    table = np.zeros((1, LANE), np.float32)
    table[0, :QK_ROPE // 2] = freq
    table[0, QK_ROPE // 2:QK_ROPE] = freq

    def body(pos_ref, f_ref, cos_ref, sin_ref):
        ang = pos_ref[...].astype(F32) * f_ref[...]
        cos_ref[...] = jnp.cos(ang)
        sin_ref[...] = jnp.sin(ang)

    out = jax.ShapeDtypeStruct((s, LANE), F32)
    blk = pl.BlockSpec((ts, LANE), lambda i: (i, 0))
    return pl.pallas_call(
        body, name=name, grid=(s // ts,), out_shape=(out, out),
        in_specs=[pl.BlockSpec((ts, 1), lambda i: (i, 0)), _vec_spec(LANE)], out_specs=(blk, blk),
        compiler_params=_params(("parallel",)),
    )(positions, jnp.asarray(table))


def _lane_mod64_low(shape):
    return (lax.broadcasted_iota(jnp.int32, shape, 1) % QK_ROPE) < (QK_ROPE // 2)


def _rope(x, cos, sin):
    rot = jnp.where(_lane_mod64_low(x.shape), -pltpu.roll(x, LANE - 32, 1), pltpu.roll(x, 32, 1))
    return x * cos + rot * sin


def _rope_t(dy, cos, sin):
    w = dy * sin
    rot_t = jnp.where(_lane_mod64_low(dy.shape), pltpu.roll(w, LANE - 32, 1), -pltpu.roll(w, 32, 1))
    return dy * cos + rot_t


def _plain_rms(x, g):
    r = lax.rsqrt(jnp.mean(x * x, axis=-1, keepdims=True) + EPS)
    return (x * r) * g, x * r, r


O_Q, O_KV, O_KR = POOL_WIDTH, POOL_WIDTH + Q_LORA, POOL_WIDTH + Q_LORA + KV_LORA


def _qkv_fwd(z, qn, kvn, wq, wkv, cos, sin, name):
    s = z.shape[0]
    ts = _tile(s, 256)

    def body(z_ref, qn_ref, kvn_ref, wq_ref, wkv_ref, cos_ref, sin_ref, q_ref, k_ref, v_ref, cqn_ref, ckvn_ref):
        cosv, sinv = cos_ref[...], sin_ref[...]
        cqn = _plain_rms(z_ref[:, O_Q:O_KV], qn_ref[...])[0].astype(BF16)
        ckvn = _plain_rms(z_ref[:, O_KV:O_KR], kvn_ref[...])[0].astype(BF16)
        cqn_ref[...] = cqn
        ckvn_ref[...] = ckvn
        nt = (((1,), (1,)), ((), ()))
        q = lax.dot_general(cqn, wq_ref[...], nt, preferred_element_type=F32)
        kv = lax.dot_general(ckvn, wkv_ref[...], nt, preferred_element_type=F32)
        kr = _rope(z_ref[:, O_KR:IN_PAD], cosv, sinv).astype(BF16)
        for h in range(N_HEADS):
            o = h * HEAD_PAD
            q_ref[:, o:o + QK_NOPE] = q[:, o:o + QK_NOPE].astype(BF16)
            q_ref[:, o + QK_NOPE:o + HEAD_PAD] = _rope(q[:, o + QK_NOPE:o + HEAD_PAD], cosv, sinv).astype(BF16)
            k_ref[:, o:o + QK_NOPE] = kv[:, o:o + QK_NOPE].astype(BF16)
            k_ref[:, o + QK_NOPE:o + HEAD_PAD] = kr
            v_ref[:, h * V_HEAD:(h + 1) * V_HEAD] = kv[:, o + QK_NOPE:o + HEAD_PAD].astype(BF16)

    def row(w):
        return pl.BlockSpec((ts, w), lambda i: (i, 0))

    def whole(arr):
        return pl.BlockSpec(arr.shape, lambda i: (0, 0))

    hp = N_HEADS * HEAD_PAD
    return pl.pallas_call(
        body, name=name, grid=(s // ts,),
        out_shape=(jax.ShapeDtypeStruct((s, hp), BF16), jax.ShapeDtypeStruct((s, hp), BF16),
                   jax.ShapeDtypeStruct((s, N_HEADS * V_HEAD), BF16),
                   jax.ShapeDtypeStruct((s, Q_LORA), BF16), jax.ShapeDtypeStruct((s, KV_LORA), BF16)),
        in_specs=[row(IN_PAD), whole(qn), whole(kvn), whole(wq), whole(wkv), row(LANE), row(LANE)],
        out_specs=(row(hp), row(hp), row(N_HEADS * V_HEAD), row(Q_LORA), row(KV_LORA)),
        compiler_params=_params(("parallel",)),
    )(z, qn, kvn, wq, wkv, cos, sin)


def _qkv_bwd(dq, dk, dv, du, z, qn, kvn, wq, wkv, cos, sin, name):
    s = z.shape[0]
    ts = _tile(s, 256)

    def norm_bwd(x, g, dy):
        _, xn, r = _plain_rms(x, g)
        dxn = dy * g
        return r * (dxn - xn * jnp.mean(dxn * xn, axis=-1, keepdims=True)), jnp.sum(dy * xn, axis=0, keepdims=True)

    def body(dq_ref, dk_ref, dv_ref, du_ref, z_ref, qn_ref, kvn_ref, wq_ref, wkv_ref, cos_ref, sin_ref,
             dz_ref, dqb_ref, dkvb_ref, dqn_ref, dkvn_ref):
        @pl.when(pl.program_id(0) == 0)
        def _():
            dqn_ref[...] = jnp.zeros_like(dqn_ref)
            dkvn_ref[...] = jnp.zeros_like(dkvn_ref)

        cosv, sinv = cos_ref[...], sin_ref[...]
        dkr = jnp.zeros((ts, LANE), F32)
        for h in range(N_HEADS):
            o = h * HEAD_PAD
            dqb_ref[:, o:o + QK_NOPE] = dq_ref[:, o:o + QK_NOPE].astype(BF16)
            dqb_ref[:, o + QK_NOPE:o + HEAD_PAD] = _rope_t(dq_ref[:, o + QK_NOPE:o + HEAD_PAD], cosv, sinv).astype(BF16)
            dkvb_ref[:, o:o + QK_NOPE] = dk_ref[:, o:o + QK_NOPE].astype(BF16)
            dkvb_ref[:, o + QK_NOPE:o + HEAD_PAD] = dv_ref[:, h * V_HEAD:(h + 1) * V_HEAD].astype(BF16)
            dkr = dkr + dk_ref[:, o + QK_NOPE:o + HEAD_PAD]
        dcqn = jnp.dot(dqb_ref[...], wq_ref[...], preferred_element_type=F32)
        dckvn = jnp.dot(dkvb_ref[...], wkv_ref[...], preferred_element_type=F32)
        dcq, dqn = norm_bwd(z_ref[:, O_Q:O_KV], qn_ref[...], dcqn)
        dckv, dkvn = norm_bwd(z_ref[:, O_KV:O_KR], kvn_ref[...], dckvn)
        dqn_ref[...] += dqn
        dkvn_ref[...] += dkvn
        dz_ref[:, 0:O_Q] = du_ref[...].astype(BF16)
        dz_ref[:, O_Q:O_KV] = dcq.astype(BF16)
        dz_ref[:, O_KV:O_KR] = dckv.astype(BF16)
        dz_ref[:, O_KR:IN_PAD] = _rope_t(dkr, cosv, sinv).astype(BF16)

    def row(w):
        return pl.BlockSpec((ts, w), lambda i: (i, 0))

    def whole(arr):
        return pl.BlockSpec(arr.shape, lambda i: (0, 0))

    hp = N_HEADS * HEAD_PAD
    return pl.pallas_call(
        body, name=name, grid=(s // ts,),
        out_shape=(jax.ShapeDtypeStruct((s, IN_PAD), BF16), jax.ShapeDtypeStruct((s, hp), BF16),
                   jax.ShapeDtypeStruct((s, hp), BF16),
                   jax.ShapeDtypeStruct((1, Q_LORA), F32), jax.ShapeDtypeStruct((1, KV_LORA), F32)),
        in_specs=[row(hp), row(hp), row(N_HEADS * V_HEAD), row(POOL_WIDTH), row(IN_PAD),
                  whole(qn), whole(kvn), whole(wq), whole(wkv), row(LANE), row(LANE)],
        out_specs=(row(IN_PAD), row(hp), row(hp), whole(qn), whole(kvn)),
        compiler_params=_params(("arbitrary",)),
    )(dq, dk, dv, du, z, qn, kvn, wq, wkv, cos, sin)


def _causal_scores(q, k, i, tq, klen):
    sc = lax.dot_general(q, k, (((1,), (1,)), ((), ())), preferred_element_type=F32) * SOFTMAX_SCALE
    qpos = i * tq + lax.broadcasted_iota(jnp.int32, (tq, klen), 0)
    kpos = lax.broadcasted_iota(jnp.int32, (tq, klen), 1)
    return jnp.where(qpos >= kpos, sc, -jnp.inf)


ATTN_TQ = 512
ATTN_SEGMENTS = 4


def _by_key_prefix(i, nq, tq, compute):
    nseg = min(ATTN_SEGMENTS, nq)
    per = nq // nseg
    for r in range(nseg):
        pl.when(i // per == r)(lambda r=r: compute((r + 1) * per * tq))


def _attn_fwd(q, k, v, name):
    s = q.shape[0]
    tq = _tile(s, ATTN_TQ)
    nq = s // tq

    def body(q_ref, k_ref, v_ref, o_ref, lse_ref):
        i = pl.program_id(1)

        def compute(klen):
            sc = _causal_scores(q_ref[...], k_ref[0:klen, :], i, tq, klen)
            mx = jnp.max(sc, axis=-1, keepdims=True)
            p = jnp.exp(sc - mx)
            den = jnp.sum(p, axis=-1, keepdims=True)
            o_ref[...] = jnp.dot((p / den).astype(BF16), v_ref[0:klen, :], preferred_element_type=F32)
            lse_ref[...] = mx + jnp.log(den)

        _by_key_prefix(i, nq, tq, compute)

    return pl.pallas_call(
        body, name=name, grid=(N_HEADS, s // tq),
        out_shape=(jax.ShapeDtypeStruct((s, N_HEADS * V_HEAD), F32), jax.ShapeDtypeStruct((N_HEADS, s, 1), F32)),
        in_specs=[pl.BlockSpec((tq, HEAD_PAD), lambda h, i: (i, h)),
                  pl.BlockSpec((s, HEAD_PAD), lambda h, i: (0, h)),
                  pl.BlockSpec((s, V_HEAD), lambda h, i: (0, h))],
        out_specs=(pl.BlockSpec((tq, V_HEAD), lambda h, i: (i, h)),
                   pl.BlockSpec((None, tq, 1), lambda h, i: (h, i, 0))),
        compiler_params=_params(("parallel", "parallel")),
    )(q, k, v)


def _attn_bwd(q, k, v, lse, dycat, name):
    s = q.shape[0]
    tq = _tile(s, ATTN_TQ)
    nq = s // tq
    tn_dims = (((0,), (0,)), ((), ()))

    def body(q_ref, k_ref, v_ref, lse_ref, do_ref, dq_ref, dk_ref, dv_ref):
        i = pl.program_id(1)

        @pl.when(i == 0)
        def _():
            dk_ref[...] = jnp.zeros_like(dk_ref)
            dv_ref[...] = jnp.zeros_like(dv_ref)

        def compute(klen):
            qv, kv_, dob = q_ref[...], k_ref[0:klen, :], do_ref[...].astype(BF16)
            sc = _causal_scores(qv, kv_, i, tq, klen)
            p = jnp.exp(sc - lse_ref[...])
            dp = lax.dot_general(dob, v_ref[0:klen, :], (((1,), (1,)), ((), ())), preferred_element_type=F32)
            ds = (p * (dp - jnp.sum(dp * p, axis=-1, keepdims=True)) * SOFTMAX_SCALE).astype(BF16)
            dq_ref[...] = jnp.dot(ds, kv_, preferred_element_type=F32)
            dk_ref[0:klen, :] += lax.dot_general(ds, qv, tn_dims, preferred_element_type=F32)
            dv_ref[0:klen, :] += lax.dot_general(p.astype(BF16), dob, tn_dims, preferred_element_type=F32)

        _by_key_prefix(i, nq, tq, compute)

    n_pool_blocks = POOL_WIDTH // V_HEAD
    return pl.pallas_call(
        body, name=name, grid=(N_HEADS, s // tq),
        out_shape=(jax.ShapeDtypeStruct((s, N_HEADS * HEAD_PAD), F32),
                   jax.ShapeDtypeStruct((s, N_HEADS * HEAD_PAD), F32),
                   jax.ShapeDtypeStruct((s, N_HEADS * V_HEAD), F32)),
        in_specs=[pl.BlockSpec((tq, HEAD_PAD), lambda h, i: (i, h)),
                  pl.BlockSpec((s, HEAD_PAD), lambda h, i: (0, h)),
                  pl.BlockSpec((s, V_HEAD), lambda h, i: (0, h)),
                  pl.BlockSpec((None, tq, 1), lambda h, i: (h, i, 0)),
                  pl.BlockSpec((tq, V_HEAD), lambda h, i: (i, n_pool_blocks + h))],
        out_specs=(pl.BlockSpec((tq, HEAD_PAD), lambda h, i: (i, h)),
                   pl.BlockSpec((s, HEAD_PAD), lambda h, i: (0, h)),
                   pl.BlockSpec((s, V_HEAD), lambda h, i: (0, h))),
        compiler_params=_params(("parallel", "arbitrary")),
    )(q, k, v, lse, dycat)


def _loss_head(x, gw, target, name):
    s, d = x.shape
    ts = _tile(s, 256)

    def body(x_ref, gw_ref, tgt_ref, loss_ref, dx_ref, dgw_ref):
        @pl.when(pl.program_id(0) == 0)
        def _():
            loss_ref[...] = jnp.zeros_like(loss_ref)
            dgw_ref[...] = jnp.zeros_like(dgw_ref)

        xv, gwv = x_ref[...], gw_ref[...]
        r = lax.rsqrt(jnp.mean(xv * xv, axis=-1, keepdims=True) + EPS)
        xn = xv * r
        err = xn * gwv - tgt_ref[...]
        loss_ref[...] += 0.5 * jnp.sum(jnp.mean(err * err, axis=-1, keepdims=True))
        dy = err / d
        dgw_ref[...] += jnp.sum(dy * xn, axis=0, keepdims=True)
        dxn = dy * gwv
        dx_ref[...] = r * (dxn - xn * jnp.mean(dxn * xn, axis=-1, keepdims=True))

    row = pl.BlockSpec((ts, d), lambda i: (i, 0))
    return pl.pallas_call(
        body, name=name, grid=(s // ts,),
        out_shape=(jax.ShapeDtypeStruct((8, LANE), F32), jax.ShapeDtypeStruct((s, d), F32),
                   jax.ShapeDtypeStruct((1, d), F32)),
        in_specs=[row, _vec_spec(d), row],
        out_specs=(pl.BlockSpec((8, LANE), lambda i: (0, 0)), row, _vec_spec(d)),
        compiler_params=_params(("arbitrary",)),
    )(x, gw, target)


def _ada_mod(c_all, ada_w, ada_b, name):
    nl, d, cols = ada_w.shape

    def body(c_ref, w_ref, b_ref, o_ref):
        cv = c_ref[...]
        act = (cv * jax.nn.sigmoid(cv)).astype(BF16)
        o_ref[...] = jnp.dot(act, w_ref[...].astype(BF16), preferred_element_type=F32) + b_ref[...]

    return pl.pallas_call(
        body, name=name, grid=(nl,), out_shape=jax.ShapeDtypeStruct((nl, N_DEV, cols), F32),
        in_specs=[pl.BlockSpec((N_DEV, d), lambda l: (0, 0)),
                  pl.BlockSpec((None, d, cols), lambda l: (l, 0, 0)),
                  pl.BlockSpec((None, 1, cols), lambda l: (l, 0, 0))],
        out_specs=pl.BlockSpec((None, N_DEV, cols), lambda l: (l, 0, 0)),
        compiler_params=_params(("parallel",)),
    )(c_all, ada_w, ada_b)


def _ada_grad(c_pad, dmod_pad, name):
    nl, kpad, cols = dmod_pad.shape
    d = c_pad.shape[1]

    def body(c_ref, dm_ref, o_ref):
        cv = c_ref[...]
        act = (cv * jax.nn.sigmoid(cv)).astype(BF16)
        o_ref[...] = lax.dot_general(act, dm_ref[...].astype(BF16), (((0,), (0,)), ((), ())),
                                     preferred_element_type=F32)

    return pl.pallas_call(
        body, name=name, grid=(nl,), out_shape=jax.ShapeDtypeStruct((nl, d, cols), F32),
        in_specs=[pl.BlockSpec((kpad, d), lambda l: (0, 0)),
                  pl.BlockSpec((None, kpad, cols), lambda l: (l, 0, 0))],
        out_specs=pl.BlockSpec((None, d, cols), lambda l: (l, 0, 0)),
        compiler_params=_params(("parallel",)),
    )(c_pad, dmod_pad)


def _adamw_math(w, g, m, v):
    nm = ADAM_B1 * m + (1.0 - ADAM_B1) * g
    nv = ADAM_B2 * v + (1.0 - ADAM_B2) * (g * g)
    m_hat = nm / (1.0 - ADAM_B1 ** ADAM_STEP)
    v_hat = nv / (1.0 - ADAM_B2 ** ADAM_STEP)
    return -ADAM_LR * (m_hat / (jnp.sqrt(v_hat) + ADAM_EPS) + ADAM_WD * w), nm, nv


def _adamw_rows(w3, gbuf, row_off, m3, v3, name):
    nl, r, d = w3.shape
    tr = _row_tile(math.gcd(r, row_off) if row_off else r, 176)
    first = row_off // tr

    def body(w_ref, g_ref, m_ref, v_ref, go_ref, d_ref, nm_ref, nv_ref):
        gv = g_ref[...]
        go_ref[...] = gv
        d_ref[...], nm_ref[...], nv_ref[...] = _adamw_math(w_ref[...], gv, m_ref[...], v_ref[...])

    blk = pl.BlockSpec((None, tr, d), lambda l, i: (l, i, 0))
    gblk = pl.BlockSpec((None, tr, d), lambda l, i: (l, first + i, 0))
    out = jax.ShapeDtypeStruct((nl, r, d), F32)
    return pl.pallas_call(
        body, name=name, grid=(nl, r // tr), out_shape=(out, out, out, out),
        in_specs=[blk, gblk, blk, blk], out_specs=(blk, blk, blk, blk),
        compiler_params=_params(("parallel", "parallel")),
    )(w3, gbuf, m3, v3)


def _adamw(w, g, m, v, name):
    rows, cols = w.shape
    tr = _row_tile(rows, 512)

    def body(w_ref, g_ref, m_ref, v_ref, d_ref, nm_ref, nv_ref):
        d_ref[...], nm_ref[...], nv_ref[...] = _adamw_math(w_ref[...], g_ref[...], m_ref[...], v_ref[...])

    blk = pl.BlockSpec((tr, cols), lambda i: (i, 0))
    out = jax.ShapeDtypeStruct((rows, cols), F32)
    return pl.pallas_call(
        body, name=name, grid=(rows // tr,), out_shape=(out, out, out),
        in_specs=[blk, blk, blk, blk], out_specs=(blk, blk, blk),
        compiler_params=_params(("parallel",)),
    )(w, g, m, v)


def _adamw_nd(w, g, m, v, name):
    shape = w.shape
    flat = (lambda t: t.reshape(1, -1)) if w.ndim == 1 else (lambda t: t.reshape(-1, shape[-1]))
    return tuple(t.reshape(shape) for t in _adamw(flat(w), flat(g), flat(m), flat(v), name))


def _pad_rows(t, rows):
    return jnp.pad(t, ((0, rows - t.shape[0]), (0, 0)))


def _pack_shard_layer(l, wts):
    def tr(name):
        return wts[name][l].astype(BF16).T

    parts = [tr("ffn1_w_gate"), tr("ffn1_w_up"), wts["ffn1_w_down"][l].astype(BF16),
             tr("ffn2_w_gate"), tr("ffn2_w_up"), wts["ffn2_w_down"][l].astype(BF16),
             wts["w_out"][l].astype(BF16),
             tr("w_kv_b").reshape(KV_SH_ROWS, D_MODEL),
             _pad_rows(tr("w_in"), 160),
             _pad_rows(tr("w_q_b").reshape(Q_SH_ROWS, D_MODEL), Q_PAD_ROWS)]
    return jnp.concatenate(parts, axis=0)


def _full_weights(lands):
    w = dict(zip(("g1", "u1", "d1", "g2", "u2", "d2", "out"), lands))
    small = lands[-1].reshape(N_DEV, SMALL_ROWS, D_MODEL)
    o_in, o_q = OFF_IN - OFF_KV, OFF_Q - OFF_KV
    w["kv"] = small[:, :KV_SH_ROWS].reshape(N_HEADS * HEAD_PAD, KV_LORA)
    w["in"] = _pad_rows(small[:, o_in:o_in + IN_SH].reshape(IN_COLS, D_MODEL), IN_PAD)
    wq = small[:, o_q:o_q + Q_SH_ROWS].reshape(N_HEADS, QK_HEAD, Q_LORA)
    w["q"] = jnp.pad(wq, ((0, 0), (0, HEAD_PAD - QK_HEAD), (0, 0))).reshape(N_HEADS * HEAD_PAD, Q_LORA)
    return w


def _grad_sources_b(gr):
    gq = gr["q"].reshape(N_HEADS, HEAD_PAD, Q_LORA)[:, :QK_HEAD].reshape(N_DEV, Q_SH_ROWS, D_MODEL)
    small = jnp.concatenate([
        gr["kv"].reshape(N_DEV, KV_SH_ROWS, D_MODEL),
        jnp.pad(gr["in"][:IN_COLS].reshape(N_DEV, IN_SH, D_MODEL), ((0, 0), (0, 160 - IN_SH), (0, 0))),
        jnp.pad(gq, ((0, 0), (0, Q_PAD_ROWS - Q_SH_ROWS), (0, 0)))], axis=1)
    return [gr["g2"], gr["u2"], gr["d2"], gr["out"], small.reshape(N_DEV * SMALL_ROWS, D_MODEL)]


def _pack_bf16_pairs(t):
    rows, d = t.shape
    return lax.bitcast_convert_type(t.astype(BF16).reshape(rows // 2, 2, d).transpose(0, 2, 1), F32)


def _unpack_bf16_pairs(p):
    pairs = jnp.swapaxes(lax.bitcast_convert_type(p, BF16), -1, -2)
    return pairs.reshape(p.shape[:-2] + (2 * p.shape[-2], p.shape[-1]))


def _small_layout(nl):
    names = [("dmod", nl * N_MOD), ("ffn1_norm", nl), ("mix_norm", nl), ("ffn2_norm", nl), ("q_a_norm", nl),
             ("kv_a_norm", nl), ("pool_scale", nl), ("final_norm", 1), ("loss", 1),
             ("pool_w", nl * 4 * POOL_GC * POOL_GC // D_MODEL // 2)]
    off, table = 0, {}
    for name, n in names:
        table[name] = (off, n)
        off += -(-n // 8) * 8
    return table, off


def _to_rows(t, width=D_MODEL):
    n, w = t.shape
    return jnp.pad(t, ((0, -(-n // 8) * 8 - n), (0, width - w)))


def kernel(x, c, positions, ada_w, ada_b, ffn1_norm, ffn1_w_gate, ffn1_w_up, ffn1_w_down, mix_norm, w_in, pool_w, pool_scale, q_a_norm, w_q_b, kv_a_norm, w_kv_b, w_out, ffn2_norm, ffn2_w_gate, ffn2_w_up, ffn2_w_down, final_norm, loss_target, m_ada_w, m_ada_b, m_ffn1_norm, m_ffn1_w_gate, m_ffn1_w_up, m_ffn1_w_down, m_mix_norm, m_w_in, m_pool_w, m_pool_scale, m_q_a_norm, m_w_q_b, m_kv_a_norm, m_w_kv_b, m_w_out, m_ffn2_norm, m_ffn2_w_gate, m_ffn2_w_up, m_ffn2_w_down, m_final_norm, v_ada_w, v_ada_b, v_ffn1_norm, v_ffn1_w_gate, v_ffn1_w_up, v_ffn1_w_down, v_mix_norm, v_w_in, v_pool_w, v_pool_scale, v_q_a_norm, v_w_q_b, v_kv_a_norm, v_w_kv_b, v_w_out, v_ffn2_norm, v_ffn2_w_gate, v_ffn2_w_up, v_ffn2_w_down, v_final_norm):
    wts = dict(ada_w=ada_w, ada_b=ada_b, ffn1_norm=ffn1_norm, ffn1_w_gate=ffn1_w_gate, ffn1_w_up=ffn1_w_up,
               ffn1_w_down=ffn1_w_down, mix_norm=mix_norm, w_in=w_in, pool_w=pool_w, pool_scale=pool_scale,
               q_a_norm=q_a_norm, w_q_b=w_q_b, kv_a_norm=kv_a_norm, w_kv_b=w_kv_b, w_out=w_out,
               ffn2_norm=ffn2_norm, ffn2_w_gate=ffn2_w_gate, ffn2_w_up=ffn2_w_up, ffn2_w_down=ffn2_w_down,
               final_norm=final_norm)
    mom_m = dict(ada_w=m_ada_w, ada_b=m_ada_b, ffn1_norm=m_ffn1_norm, ffn1_w_gate=m_ffn1_w_gate,
                 ffn1_w_up=m_ffn1_w_up, ffn1_w_down=m_ffn1_w_down, mix_norm=m_mix_norm, w_in=m_w_in,
                 pool_w=m_pool_w, pool_scale=m_pool_scale, q_a_norm=m_q_a_norm, w_q_b=m_w_q_b,
                 kv_a_norm=m_kv_a_norm, w_kv_b=m_w_kv_b, w_out=m_w_out, ffn2_norm=m_ffn2_norm,
                 ffn2_w_gate=m_ffn2_w_gate, ffn2_w_up=m_ffn2_w_up, ffn2_w_down=m_ffn2_w_down,
                 final_norm=m_final_norm)
    mom_v = dict(ada_w=v_ada_w, ada_b=v_ada_b, ffn1_norm=v_ffn1_norm, ffn1_w_gate=v_ffn1_w_gate,
                 ffn1_w_up=v_ffn1_w_up, ffn1_w_down=v_ffn1_w_down, mix_norm=v_mix_norm, w_in=v_w_in,
                 pool_w=v_pool_w, pool_scale=v_pool_scale, q_a_norm=v_q_a_norm, w_q_b=v_w_q_b,
                 kv_a_norm=v_kv_a_norm, w_kv_b=v_w_kv_b, w_out=v_w_out, ffn2_norm=v_ffn2_norm,
                 ffn2_w_gate=v_ffn2_w_gate, ffn2_w_up=v_ffn2_w_up, ffn2_w_down=v_ffn2_w_down,
                 final_norm=v_final_norm)
    order = list(wts)
    nl = ada_w.shape[0]
    seq = x.shape[1]
    me = 4 * lax.axis_index("x") + 2 * lax.axis_index("y") + lax.axis_index("c")
    ada_cols = ada_w.shape[2]

    def after_token(t, token):
        return t + token[0:1, 0:1].astype(t.dtype)

    packs = [_pack_shard_layer(l, wts) for l in range(nl)]

    c_all = _all_gather(jnp.broadcast_to(c, (8, D_MODEL)), "gather_c")[::8]

    ada_b_mine = lax.dynamic_slice_in_dim(ada_b, me * ada_cols, ada_cols, axis=1).reshape(nl, 1, ada_cols)
    mod_part = _ada_mod(c_all, ada_w, ada_b_mine, "ada_mod")
    mod_all = _all_gather(mod_part.reshape(nl * N_DEV, ada_cols), "gather_mod")
    mod_all = mod_all.reshape(N_DEV, nl, N_DEV, ada_cols)
    mod = lax.dynamic_index_in_dim(mod_all, me, axis=2, keepdims=False)
    mod = mod.transpose(1, 0, 2).reshape(nl * N_MOD, 1, D_MODEL)
    norm_tables = {name: wts[name].reshape(nl, 1, D_MODEL) for name in ("ffn1_norm", "mix_norm", "ffn2_norm")}

    def modrow(l, k):
        return mod, l * N_MOD + k

    def normrow(name, l):
        return norm_tables[name], l

    flight_a = _gather_start(packs[0][:SPLIT_AB], ROWS_A, mod, "gather_start_0a")
    flight_b = _gather_start(packs[0][SPLIT_AB:], ROWS_B, flight_a[4], "gather_start_0b")
    last_start = flight_b[4]
    if nl > 1:
        in_flight = _gather_start(packs[1], ROWS_ALL, last_start, "gather_start_1")
        last_start = in_flight[4]

    cos, sin = _rope_tables(after_token(positions.reshape(seq, 1), last_start), "rope_tables")

    def vec(t):
        return t.reshape(1, -1)

    def landed(flight, rows_list, after, tag):
        send_sems, recv_sems, pk, lands, _ = flight
        pk, lands = _gather_wait(send_sems, recv_sems, pk, lands, after, f"gather_wait_{tag}")
        return _gather_finish(pk, rows_list, lands, "gather_finish")

    xs = x.reshape(seq, D_MODEL)
    saved = []
    for l in range(nl):
        norm1, up_after = normrow("ffn1_norm", l), None
        if l == 0:
            lands = landed(flight_a, ROWS_A, cos, "0a")
        elif l + 1 < nl:
            in_flight = _gather_start(packs[l + 1], ROWS_ALL, lands[0], f"gather_start_{l + 1}")
            up_after = in_flight[4]
        sv = {}

        def ffn_fwd(xin, norm, k0, wg, wu, wd, tag, after=None):
            h, a, b, t = _ffn_up(xin, norm, modrow(l, k0), modrow(l, k0 + 1), wg, wu, "ffn_up", after=after)
            y, xout = _mm(t, wd, "nn", "ffn_down", res=xin, gate=modrow(l, k0 + 2), gate_factor=0.5)
            sv[tag] = dict(x=xin, h=h, a=a, b=b, t=t, y=y)
            return xout

        xs = ffn_fwd(xs, norm1, 0, lands[0], lands[1], lands[2], "f1", up_after)
        if l == 0:
            lands = lands + landed(flight_b, ROWS_B, xs, "0b")
        w = _full_weights(lands)
        sv["w"] = w

        h2, z = _norm_mm(xs, normrow("mix_norm", l), modrow(l, 3), modrow(l, 4), w["in"], "mix_in")
        y_pool, diff = _pool_fwd(z, pool_w[l], vec(pool_scale[l]), "pool_fwd")
        q, k, v, cqn, ckvn = _qkv_fwd(z, vec(q_a_norm[l]), vec(kv_a_norm[l]), w["q"], w["kv"], cos, sin, "qkv_fwd")
        o, lse = _attn_fwd(q, k, v, "attn_fwd")
        ycat = jnp.concatenate([y_pool, o.astype(BF16)], axis=1)
        y2, xmix = _mm(ycat, w["out"], "nn", "mix_out", res=xs, gate=modrow(l, 5), gate_factor=1.0)
        sv["mix"] = dict(x=xs, h=h2, z=z, diff=diff, q=q, k=k, v=v, cqn=cqn, ckvn=ckvn, lse=lse, ycat=ycat, y=y2)
        xs = xmix

        xs = ffn_fwd(xs, normrow("ffn2_norm", l), 6, w["g2"], w["u2"], w["d2"], "f2")
        saved.append(sv)
        if l + 1 < nl:
            lands = landed(in_flight, ROWS_ALL, xs, l + 1)

    loss_part, dx, d_final = _loss_head(xs, vec(final_norm), loss_target.reshape(seq, D_MODEL), "loss_head")

    small = {name: [None] * nl for name in ("ffn1_norm", "mix_norm", "ffn2_norm", "q_a_norm", "kv_a_norm",
                                            "pool_scale", "pool_w", "dmod")}
    core = lax.axis_index("c").astype(jnp.int32).reshape(1)
    chip = 2 * lax.axis_index("x") + lax.axis_index("y")
    exchanges = []

    def leave(srcs, rows_list, after, tag):
        return _pair_start(srcs, rows_list, after, f"pair_start_{tag}"), rows_list, tag

    def forward_on(pending, after, layer, row_off):
        (send_sems, recv_sems, srcs, land, _), rows_list, tag = pending
        srcs, land = _split_wait(send_sems, recv_sems, 1, srcs, land, after, f"pair_wait_{tag}")
        sums = _pair_sum(srcs, rows_list, land, core, "pair_sum")
        flight = _chip_exchange_start(sums, chip, after, f"exchange_start_{tag}")
        exchanges.append((flight, layer, row_off, tag))
        return flight[4]

    pending = None
    head = _gate_bwd(dx, saved[nl - 1]["f2"]["y"], modrow(nl - 1, 8), 0.5, "gate_bwd")
    for l in reversed(range(nl)):
        sv = saved[l]
        w = sv["w"]
        dmod = [None] * N_MOD
        gr = {}

        def ffn_bwd(dxin, head, s_, norm, k0, wg, wu, wd, tag, below, first_after=None, mid=None):
            dy, dmod[k0 + 2] = head
            da, db, gr["d" + tag], gr["g" + tag], gr["u" + tag] = _ffn_bwd_cols(
                dy, s_["h"], s_["a"], s_["b"], s_["t"], wd, "ffn_bwd_cols", after=first_after)
            dh = _mm_pair(da, wg, db, wu, "ffn_bwd_dh", after=None if mid is None else mid(da))
            outs = _rm_bwd(dh, s_["x"], dxin, norm, modrow(l, k0 + 1), "rm_bwd", below=below)
            dmod[k0], dmod[k0 + 1] = outs[1], outs[2]
            return outs[0], outs[3], outs[4:]

        s_ = sv["mix"]
        dx, small["ffn2_norm"][l], head = ffn_bwd(
            dx, head, sv["f2"], normrow("ffn2_norm", l), 6, w["g2"], w["u2"], w["d2"], "2", (s_["y"], modrow(l, 5), 1.0),
            first_after=None if pending is None else pending[0][4])

        pending_c = leave([gr["g2"], gr["u2"], gr["d2"]], ROWS_A, dx, f"{l}c")
        mix_after = pending_c[0][4]
        if pending is not None:
            mix_after = forward_on(pending, mix_after, l + 1, GB_F1)
            pending = None
        dy, dmod[5] = head
        gr["out"] = _mm(s_["ycat"], dy, "tn", "mix_out_dw", out_dtype=BF16, tm=256, after=mix_after)
        dycat = _mm(dy, w["out"], "nt", "mix_out_dx")
        du, small["pool_w"][l], small["pool_scale"][l] = _pool_bwd(dycat, s_["diff"], pool_w[l], vec(pool_scale[l]), "pool_bwd")
        dq, dk, dv = _attn_bwd(s_["q"], s_["k"], s_["v"], s_["lse"], dycat, "attn_bwd")
        dz, dqb, dkvb, small["q_a_norm"][l], small["kv_a_norm"][l] = _qkv_bwd(
            dq, dk, dv, du, s_["z"], vec(q_a_norm[l]), vec(kv_a_norm[l]), w["q"], w["kv"], cos, sin, "qkv_bwd")
        gr["q"] = _mm(dqb, s_["cqn"], "tn", "q_b_dw", out_dtype=BF16, tm=256, after=forward_on(pending_c, dz, l, GB_F2))
        gr["kv"] = _mm(dkvb, s_["ckvn"], "tn", "kv_b_dw", out_dtype=BF16, tm=256)
        gr["in"] = _mm(dz, s_["h"], "tn", "mix_in_dw", out_dtype=BF16, tm=256)
        dh2 = _mm(dz, w["in"], "nn", "mix_in_dx")
        outs = _rm_bwd(dh2, s_["x"], dx, normrow("mix_norm", l), modrow(l, 4), "rm_bwd",
                       below=(sv["f1"]["y"], modrow(l, 2), 0.5))
        dx, dmod[3], dmod[4], small["mix_norm"][l] = outs[:4]
        head = outs[4:]

        first_after, mid = None, None
        if l == 0:
            pending_b = leave(_grad_sources_b(gr)[3:], ROWS_TAIL, dx, "0b")
            first_after = pending_b[0][4]
            last_groups = []

            def mid(da):
                last_groups.append(leave([gr["g1"], gr["u1"], gr["d1"]], ROWS_A, da, "0a"))
                return forward_on(pending_b, last_groups[0][0][4], 0, GB_TAIL)
        below = (saved[l - 1]["f2"]["y"], modrow(l - 1, 8), 0.5) if l > 0 else None
        dx, small["ffn1_norm"][l], head = ffn_bwd(
            dx, head, sv["f1"], normrow("ffn1_norm", l), 0, w["g1"], w["u1"], w["d1"], "1", below, first_after, mid)

        small["dmod"][l] = jnp.concatenate(dmod, axis=0)
        if l > 0:
            pending = leave([gr["g1"], gr["u1"], gr["d1"]] + _grad_sources_b(gr)[3:], ROWS_A + ROWS_TAIL, dx, l)

    grad_x = dx.reshape(x.shape)
    pending_a = last_groups[0]

    layout, small_rows = _small_layout(nl)
    pieces = {
        "dmod": jnp.concatenate(small["dmod"], axis=0),
        "ffn1_norm": jnp.concatenate(small["ffn1_norm"], axis=0),
        "mix_norm": jnp.concatenate(small["mix_norm"], axis=0),
        "ffn2_norm": jnp.concatenate(small["ffn2_norm"], axis=0),
        "q_a_norm": jnp.concatenate(small["q_a_norm"], axis=0),
        "kv_a_norm": jnp.concatenate(small["kv_a_norm"], axis=0),
        "pool_scale": jnp.concatenate(small["pool_scale"], axis=0),
        "final_norm": d_final,
        "loss": jnp.broadcast_to(loss_part[0:1, 0:1], (1, D_MODEL)),
        "pool_w": _pack_bf16_pairs(jnp.stack(small["pool_w"]).reshape(-1, D_MODEL)),
    }
    small_buf = jnp.concatenate([_to_rows(pieces[name]) for name in layout], axis=0)
    def landed_sums(gbuf, entries, after):
        for (send_sems, recv_sems, sums, recv, _), layer, row_off, tag in entries:
            _, recv = _split_wait(send_sems, recv_sems, N_CHIPS - 1, sums, recv, after, f"exchange_wait_{tag}")
            gbuf = _sum_slots_into(recv, gbuf, layer, row_off, "sum_grads")
        return gbuf

    gbuf = lax.empty((nl, ROWS_L, D_MODEL), F32)
    token_0a = forward_on(pending_a, dx, 0, GB_F1)
    spread = _spread_start(small_buf, me, token_0a, "small_start")
    gbuf = landed_sums(gbuf, [e for e in exchanges if e[3] != "0a"], spread[4])

    def swap(t):
        return t.transpose(0, 2, 1)

    def same(t):
        return t

    grads, updates = {}, {}

    def update_rows(gbuf, table):
        for wname, off, view in table:
            g, d_, nm, nv = _adamw_rows(view(wts[wname]), gbuf, off, view(mom_m[wname]), view(mom_v[wname]), "adamw_rows")
            grads[wname], updates[wname] = view(g), (view(d_), view(nm), view(nv))

    update_rows(gbuf, (("ffn2_w_gate", GB_F2, swap), ("ffn2_w_up", GB_F2 + FF_SH, swap),
                       ("ffn2_w_down", GB_F2 + 2 * FF_SH, same), ("w_out", GB_TAIL, same)))
    small_grads = {
        "w_kv_b": (gbuf[:, OFF_KV:OFF_KV + KV_SH_ROWS].reshape(nl, -1, KV_LORA).transpose(0, 2, 1), same),
        "w_in": (gbuf[:, OFF_IN:OFF_IN + IN_SH], swap),
        "w_q_b": (gbuf[:, OFF_Q:OFF_Q + Q_SH_ROWS].reshape(nl, -1, Q_LORA), swap),
    }
    for wname, (g, view) in small_grads.items():
        upd = _adamw_nd(view(wts[wname]), g, view(mom_m[wname]), view(mom_v[wname]), "adamw")
        grads[wname], updates[wname] = view(g), tuple(view(t) for t in upd)
    gbuf = landed_sums(gbuf, [e for e in exchanges if e[3] == "0a"], updates["w_q_b"][0])
    update_rows(gbuf, (("ffn1_w_gate", GB_F1, swap), ("ffn1_w_up", GB_F1 + FF_SH, swap),
                       ("ffn1_w_down", GB_F1 + 2 * FF_SH, same)))

    _, small_all = _split_wait(spread[0], spread[1], N_DEV - 1, spread[2], spread[3], updates["ffn1_w_down"][0],
                               "small_wait")
    pool_off, pool_rows = layout["pool_w"]
    small_sum = _sum_slots(small_all[:, :pool_off], "sum_small")
    pool_sum = _sum_slots(_unpack_bf16_pairs(small_all[:, pool_off:pool_off + pool_rows]), "sum_pool_w")

    def take(name, width=D_MODEL):
        off, n = layout[name]
        return small_sum[off:off + n, :width]

    late = {"ada_b": take("dmod").reshape(nl, N_MOD * D_MODEL),
            "ffn1_norm": take("ffn1_norm"), "mix_norm": take("mix_norm"), "ffn2_norm": take("ffn2_norm"),
            "q_a_norm": take("q_a_norm", Q_LORA), "kv_a_norm": take("kv_a_norm", KV_LORA),
            "pool_scale": take("pool_scale", POOL_WIDTH), "final_norm": take("final_norm").reshape(D_MODEL),
            "pool_w": pool_sum.reshape(pool_w.shape)}
    loss = take("loss")[0, 0]

    off, n = layout["dmod"]
    dmod_all = small_all[:, off:off + n].reshape(N_DEV, nl, N_MOD * D_MODEL)
    dmod_mine = lax.dynamic_slice_in_dim(dmod_all, me * ada_cols, ada_cols, axis=2)
    dmod_pad = jnp.pad(dmod_mine.transpose(1, 0, 2), ((0, 0), (0, LANE - N_DEV), (0, 0)))
    late["ada_w"] = _ada_grad(jnp.pad(c_all, ((0, LANE - N_DEV), (0, 0))), dmod_pad, "ada_grad")
    for name, g in late.items():
        grads[name], updates[name] = g, _adamw_nd(wts[name], g, mom_m[name], mom_v[name], "adamw")

    return (loss, grad_x, *[grads[n] for n in order], *[updates[n][0] for n in order],
            *[updates[n][1] for n in order], *[updates[n][2] for n in order])
```

```python
import math

import numpy as np
import jax
import jax.numpy as jnp
from jax import lax
from jax.experimental import pallas as pl
from jax.experimental.pallas import tpu as pltpu

F32 = jnp.float32
BF16 = jnp.bfloat16

N_DEV = 8
D_MODEL = 1024
D_FF = 2816
POOL_WIDTH = 512
POOL_WINDOWS = (2, 4, 8, 16)
POOL_GC = 128
N_HEADS = 4
QK_NOPE = 128
QK_ROPE = 64
V_HEAD = 128
QK_HEAD = QK_NOPE + QK_ROPE
HEAD_PAD = 256
Q_LORA = 384
KV_LORA = 256
IN_COLS = POOL_WIDTH + Q_LORA + KV_LORA + QK_ROPE
IN_PAD = 1280
ROPE_THETA = 10000.0
SOFTMAX_SCALE = 1.0 / math.sqrt(QK_HEAD)
EPS = 1e-6
N_MOD = 9

ADAM_LR = 0.001
ADAM_B1 = 0.9
ADAM_B2 = 0.999
ADAM_EPS = 1e-08
ADAM_WD = 0.01
ADAM_STEP = 10

LANE = 128
VMEM_LIMIT = 56 * 1024 * 1024

FF_SH = D_FF // N_DEV
OFF_G1, OFF_U1, OFF_D1 = 0, FF_SH, 2 * FF_SH
OFF_G2, OFF_U2, OFF_D2 = 3 * FF_SH, 4 * FF_SH, 5 * FF_SH
OFF_OUT = 6 * FF_SH
OFF_KV = OFF_OUT + 128
OFF_IN = OFF_KV + 32
OFF_Q = OFF_IN + 160
Q_PAD_ROWS = 64
ROWS_L = OFF_Q + Q_PAD_ROWS
IN_SH = IN_COLS // N_DEV
Q_SH_ROWS = (N_HEADS * QK_HEAD // N_DEV) * Q_LORA // D_MODEL
KV_SH_ROWS = (N_HEADS * (QK_NOPE + V_HEAD) // N_DEV) * KV_LORA // D_MODEL


def _tile(dim, target):
    if dim <= target:
        return dim
    best = None
    for t in range(LANE, target + 1, LANE):
        if dim % t == 0:
            best = t
    assert best is not None, (dim, target)
    return best


def _params(sem):
    return pltpu.CompilerParams(dimension_semantics=sem, vmem_limit_bytes=VMEM_LIMIT)


def _mesh_pos():
    return lax.axis_index("x"), lax.axis_index("y"), lax.axis_index("c")


def _all_gather(x, name):
    m, n = x.shape

    def body(x_ref, out_ref, send_sems, recv_sems, local_sem):
        px, py, pc = _mesh_pos()
        me, sibling = (px, py, pc), (px, py, 1 - pc)
        chips = [(1 - px, py), (px, 1 - py), (1 - px, 1 - py)]

        def rows(bx, by, bc):
            return out_ref.at[pl.ds((4 * bx + 2 * by + bc) * m, m), :]

        def copy(k, block, to, src=None):
            return pltpu.make_async_remote_copy(
                src_ref=rows(*block) if src is None else src, dst_ref=rows(*block),
                send_sem=send_sems.at[k], recv_sem=recv_sems.at[k],
                device_id=to, device_id_type=pl.DeviceIdType.MESH)

        mine = pltpu.make_async_copy(x_ref, rows(*me), local_sem)
        mine.start()
        first = [copy(0, me, sibling, src=x_ref)]
        first += [copy(1 + j, me, (*chip, pc), src=x_ref) for j, chip in enumerate(chips)]
        for cp in first:
            cp.start()
        passed = [copy(4 + j, (*chip, pc), sibling) for j, chip in enumerate(chips)]
        for j, chip in enumerate(chips):
            copy(1 + j, (*chip, pc), me).wait_recv()
            passed[j].start()
        copy(0, sibling, me).wait_recv()
        for j, chip in enumerate(chips):
            copy(4 + j, (*chip, 1 - pc), me).wait_recv()
        for cp in first + passed:
            cp.wait_send()
        mine.wait()

    hbm = pl.BlockSpec(memory_space=pltpu.HBM)
    return pl.pallas_call(
        body, name=name,
        out_shape=jax.ShapeDtypeStruct((N_DEV * m, n), x.dtype),
        in_specs=[hbm], out_specs=hbm,
        scratch_shapes=[pltpu.SemaphoreType.DMA((7,)), pltpu.SemaphoreType.DMA((7,)),
                        pltpu.SemaphoreType.DMA],
    )(x)


SMALL_ROWS = ROWS_L - OFF_KV
ROWS_A = [FF_SH] * 3
ROWS_B = [FF_SH] * 3 + [128, SMALL_ROWS]
ROWS_ALL = ROWS_A + ROWS_B
SPLIT_AB = sum(ROWS_A)
ROWS_TAIL = [128, SMALL_ROWS]
GB_F2, GB_F1, GB_TAIL = 0, SPLIT_AB, 2 * SPLIT_AB
HBM_SPEC = pl.BlockSpec(memory_space=pltpu.HBM)
SEM_SPEC = pl.BlockSpec(memory_space=pltpu.SEMAPHORE)
ANY_SPEC = pl.BlockSpec(memory_space=pl.ANY)
EFFECT = pltpu.SideEffectType.DATAFLOW_SIDE_EFFECTING


def _hbm(t):
    return pltpu.with_memory_space_constraint(t, pltpu.HBM)


def _whole_wait(ref, send_sem, recv_sem, peer):
    return pltpu.make_async_remote_copy(src_ref=ref, dst_ref=ref, send_sem=send_sem, recv_sem=recv_sem,
                                        device_id=peer, device_id_type=pl.DeviceIdType.MESH)


def _offsets(rows_list):
    return [sum(rows_list[:i]) for i in range(len(rows_list))]


def _gather_start(packed, rows_list, after, name):
    n = len(rows_list)
    offs = _offsets(rows_list)
    lands = [_hbm(lax.empty((N_DEV * rows, D_MODEL), BF16)) for rows in rows_list]

    def body(packed_ref, *refs):
        land = refs[:n]
        send_sems, recv_sems = refs[n + 1], refs[n + 2]
        token = refs[-1]
        px, py, pc = _mesh_pos()
        me = 4 * px + 2 * py + pc
        peers = [(px, py, 1 - pc), (1 - px, py, pc), (px, 1 - py, pc), (1 - px, 1 - py, pc)]
        for k, peer in enumerate(peers):
            for off, rows, land_ref in zip(offs, rows_list, land):
                pltpu.make_async_remote_copy(
                    src_ref=packed_ref.at[pl.ds(off, rows), :], dst_ref=land_ref.at[pl.ds(me * rows, rows), :],
                    send_sem=send_sems.at[k], recv_sem=recv_sems.at[k],
                    device_id=peer, device_id_type=pl.DeviceIdType.MESH).start()
        token[...] = jnp.zeros_like(token)

    outs = pl.pallas_call(
        body, name=name,
        out_shape=(pltpu.SemaphoreType.DMA((4,)), pltpu.SemaphoreType.DMA((4,)), pltpu.HBM(packed.shape, BF16),
                   *[pltpu.HBM(t.shape, BF16) for t in lands], jax.ShapeDtypeStruct((8, LANE), F32)),
        in_specs=(HBM_SPEC,) * (1 + n) + (ANY_SPEC,),
        out_specs=(SEM_SPEC, SEM_SPEC) + (HBM_SPEC,) * (1 + n) + (pl.BlockSpec(memory_space=pltpu.VMEM),),
        input_output_aliases={i: 2 + i for i in range(1 + n)},
        compiler_params=pltpu.CompilerParams(has_side_effects=EFFECT),
    )(_hbm(packed), *lands, after)
    return outs[0], outs[1], outs[2], list(outs[3:3 + n]), outs[-1]


def _gather_wait(send_sems, recv_sems, packed, lands, after, name):
    n = len(lands)

    def body(packed_ref, *refs):
        s_sems, r_sems = refs[n], refs[n + 1]
        me = _mesh_pos()
        for k in range(4):
            cp = _whole_wait(packed_ref, s_sems.at[k], r_sems.at[k], me)
            cp.wait_send()
            cp.wait_recv()

    outs = pl.pallas_call(
        body, name=name,
        out_shape=(pltpu.HBM(packed.shape, BF16), *[pltpu.HBM(t.shape, BF16) for t in lands]),
        in_specs=(HBM_SPEC,) * (1 + n) + (SEM_SPEC, SEM_SPEC, ANY_SPEC),
        out_specs=(HBM_SPEC,) * (1 + n),
        input_output_aliases={i: i for i in range(1 + n)},
        compiler_params=pltpu.CompilerParams(has_side_effects=EFFECT),
    )(packed, *lands, send_sems, recv_sems, after)
    return outs[0], list(outs[1:])


def _gather_finish(packed, rows_list, lands, name):
    n = len(rows_list)
    offs = _offsets(rows_list)

    def body(packed_ref, *refs):
        land = refs[n:2 * n]
        send_sems, recv_sems, stage, stage_sem = refs[2 * n:]
        px, py, pc = _mesh_pos()
        me = 4 * px + 2 * py + pc
        sibling = (px, py, 1 - pc)
        load = pltpu.make_async_copy(packed_ref, stage, stage_sem)
        load.start()
        load.wait()
        for off, rows, land_ref in zip(offs, rows_list, land):
            pltpu.make_async_copy(stage.at[pl.ds(off, rows), :], land_ref.at[pl.ds(me * rows, rows), :],
                                  stage_sem).start()
        for j, (cx, cy) in enumerate([(1 - px, py), (px, 1 - py), (1 - px, 1 - py)]):
            block = 4 * cx + 2 * cy + pc
            for rows, land_ref in zip(rows_list, land):
                blk = land_ref.at[pl.ds(block * rows, rows), :]
                pltpu.make_async_remote_copy(src_ref=blk, dst_ref=blk, send_sem=send_sems.at[j],
                                             recv_sem=recv_sems.at[j], device_id=sibling,
                                             device_id_type=pl.DeviceIdType.MESH).start()
        for j in range(3):
            cp = _whole_wait(packed_ref, send_sems.at[j], recv_sems.at[j], sibling)
            cp.wait_recv()
            cp.wait_send()
        pltpu.make_async_copy(stage, packed_ref, stage_sem).wait()

    outs = pl.pallas_call(
        body, name=name,
        out_shape=tuple(jax.ShapeDtypeStruct(t.shape, BF16) for t in lands),
        in_specs=(HBM_SPEC,) * (1 + n), out_specs=(HBM_SPEC,) * n,
        input_output_aliases={1 + i: i for i in range(n)},
        scratch_shapes=[pltpu.SemaphoreType.DMA((3,)), pltpu.SemaphoreType.DMA((3,)),
                        pltpu.VMEM(packed.shape, BF16), pltpu.SemaphoreType.DMA],
    )(packed, *lands)
    return list(outs)


N_CHIPS = 4


def _pair_start(srcs, rows_list, after, name):
    n = len(rows_list)
    offs = _offsets(rows_list)
    land = lax.empty((N_CHIPS, sum(rows_list), D_MODEL), BF16)

    def body(*refs):
        src, land_ref = refs[:n], refs[n]
        send_sems, recv_sems = refs[n + 2], refs[n + 3]
        token = refs[-1]
        px, py, pc = _mesh_pos()
        for k in range(N_CHIPS):
            block = 2 * k + (1 - pc)
            for off, rows, src_ref in zip(offs, rows_list, src):
                pltpu.make_async_remote_copy(
                    src_ref=src_ref.at[pl.ds(block * rows, rows), :], dst_ref=land_ref.at[k, pl.ds(off, rows), :],
                    send_sem=send_sems.at[0], recv_sem=recv_sems.at[0],
                    device_id=(px, py, 1 - pc), device_id_type=pl.DeviceIdType.MESH).start()
        token[...] = jnp.zeros_like(token)

    outs = pl.pallas_call(
        body, name=name,
        out_shape=(pltpu.SemaphoreType.DMA((1,)), pltpu.SemaphoreType.DMA((1,)),
                   *[pltpu.HBM(t.shape, BF16) for t in srcs], pltpu.HBM(land.shape, BF16),
                   jax.ShapeDtypeStruct((8, LANE), F32)),
        in_specs=(HBM_SPEC,) * (n + 1) + (ANY_SPEC,),
        out_specs=(SEM_SPEC, SEM_SPEC) + (HBM_SPEC,) * (n + 1) + (pl.BlockSpec(memory_space=pltpu.VMEM),),
        input_output_aliases={i: 2 + i for i in range(n + 1)},
        compiler_params=pltpu.CompilerParams(has_side_effects=EFFECT),
    )(*[_hbm(t) for t in srcs], _hbm(land), after)
    return outs[0], outs[1], list(outs[2:2 + n]), outs[2 + n], outs[-1]


def _split_wait(send_sems, recv_sems, n_sems, srcs, land, after, name):
    n = len(srcs)

    def body(*refs):
        land_ref = refs[n]
        s_sems, r_sems = refs[n + 1], refs[n + 2]
        me = _mesh_pos()
        for k in range(n_sems):
            cp = _whole_wait(land_ref.at[0] if n_sems > 1 else land_ref, s_sems.at[k], r_sems.at[k], me)
            cp.wait_send()
            cp.wait_recv()

    outs = pl.pallas_call(
        body, name=name,
        out_shape=(*[pltpu.HBM(t.shape, t.dtype) for t in srcs], pltpu.HBM(land.shape, land.dtype)),
        in_specs=(HBM_SPEC,) * (n + 1) + (SEM_SPEC, SEM_SPEC, ANY_SPEC),
        out_specs=(HBM_SPEC,) * (n + 1),
        input_output_aliases={i: i for i in range(n + 1)},
        compiler_params=pltpu.CompilerParams(has_side_effects=EFFECT),
    )(*srcs, land, send_sems, recv_sems, after)
    return list(outs[:n]), outs[n]


def _spread_start(x, me_id, after, name):
    land = lax.dynamic_update_slice_in_dim(lax.empty((N_DEV,) + x.shape, x.dtype), x[None], me_id, axis=0)

    def body(x_ref, land_ref, after_ref, send_sems, recv_sems, x_thru, land_thru, token):
        px, py, pc = _mesh_pos()
        me = 4 * px + 2 * py + pc
        for k in range(1, N_DEV):
            qx = 1 - px if k & 4 else px
            qy = 1 - py if k & 2 else py
            qc = 1 - pc if k & 1 else pc
            pltpu.make_async_remote_copy(
                src_ref=x_ref, dst_ref=land_ref.at[me], send_sem=send_sems.at[k - 1], recv_sem=recv_sems.at[k - 1],
                device_id=(qx, qy, qc), device_id_type=pl.DeviceIdType.MESH).start()
        token[...] = jnp.zeros_like(token)

    outs = pl.pallas_call(
        body, name=name,
        out_shape=(pltpu.SemaphoreType.DMA((N_DEV - 1,)), pltpu.SemaphoreType.DMA((N_DEV - 1,)),
                   pltpu.HBM(x.shape, x.dtype), pltpu.HBM(land.shape, land.dtype), jax.ShapeDtypeStruct((8, LANE), F32)),
        in_specs=(HBM_SPEC, HBM_SPEC, ANY_SPEC),
        out_specs=(SEM_SPEC, SEM_SPEC, HBM_SPEC, HBM_SPEC, pl.BlockSpec(memory_space=pltpu.VMEM)),
        input_output_aliases={0: 2, 1: 3},
        compiler_params=pltpu.CompilerParams(has_side_effects=EFFECT),
    )(_hbm(x), _hbm(land), after)
    return outs[0], outs[1], [outs[2]], outs[3], outs[4]


def _pair_sum(srcs, rows_list, land, core, name):
    n = len(rows_list)
    offs = _offsets(rows_list)
    total = sum(rows_list)

    def body(core_ref, *refs):
        src, land_ref, out_ref = refs[:n], refs[n], refs[n + 1]
        for off, rows, src_ref in zip(offs, rows_list, src):
            out_ref[pl.ds(off, rows), :] = (src_ref[...].astype(F32)
                                            + land_ref[pl.ds(off, rows), :].astype(F32)).astype(BF16)

    slot = pl.BlockSpec((None, total, D_MODEL), lambda k, c: (k, 0, 0))
    grid_spec = pltpu.PrefetchScalarGridSpec(
        num_scalar_prefetch=1, grid=(N_CHIPS,),
        in_specs=[pl.BlockSpec((rows, D_MODEL), lambda k, c: (2 * k + c[0], 0)) for rows in rows_list] + [slot],
        out_specs=slot)
    return pl.pallas_call(
        body, name=name, grid_spec=grid_spec,
        out_shape=jax.ShapeDtypeStruct((N_CHIPS, total, D_MODEL), BF16),
        compiler_params=_params(("parallel",)),
    )(core, *srcs, land)


def _chip_exchange_start(sums, chip, after, name):
    own = lax.dynamic_index_in_dim(sums, chip, axis=0, keepdims=True)
    recv = lax.dynamic_update_slice_in_dim(lax.empty(sums.shape, BF16), own, chip, axis=0)

    def body(sums_ref, recv_ref, after_ref, send_sems, recv_sems, sums_thru, recv_thru, token):
        px, py, pc = _mesh_pos()
        for k in range(1, N_CHIPS):
            qx = 1 - px if k & 2 else px
            qy = 1 - py if k & 1 else py
            pltpu.make_async_remote_copy(
                src_ref=sums_ref.at[2 * qx + qy], dst_ref=recv_ref.at[2 * px + py],
                send_sem=send_sems.at[k - 1], recv_sem=recv_sems.at[k - 1],
                device_id=(qx, qy, pc), device_id_type=pl.DeviceIdType.MESH).start()
        token[...] = jnp.zeros_like(token)

    outs = pl.pallas_call(
        body, name=name,
        out_shape=(pltpu.SemaphoreType.DMA((N_CHIPS - 1,)), pltpu.SemaphoreType.DMA((N_CHIPS - 1,)),
                   pltpu.HBM(sums.shape, BF16), pltpu.HBM(recv.shape, BF16), jax.ShapeDtypeStruct((8, LANE), F32)),
        in_specs=(HBM_SPEC, HBM_SPEC, ANY_SPEC),
        out_specs=(SEM_SPEC, SEM_SPEC, HBM_SPEC, HBM_SPEC, pl.BlockSpec(memory_space=pltpu.VMEM)),
        input_output_aliases={0: 2, 1: 3},
        compiler_params=pltpu.CompilerParams(has_side_effects=EFFECT),
    )(_hbm(sums), _hbm(recv), after)
    return outs[0], outs[1], [outs[2]], outs[3], outs[4]


def _sum_slots_into(recv, buf, layer, row_off, name):
    slots, r, n = recv.shape
    tr = _row_tile(math.gcd(r, row_off) if row_off else r, 512)
    first = row_off // tr

    def body(in_ref, buf_ref, out_ref):
        acc = in_ref[0].astype(F32)
        for j in range(1, slots):
            acc = acc + in_ref[j].astype(F32)
        out_ref[...] = acc

    return pl.pallas_call(
        body, name=name, grid=(r // tr,), out_shape=jax.ShapeDtypeStruct(buf.shape, F32),
        in_specs=[pl.BlockSpec((slots, tr, n), lambda i: (0, i, 0)), ANY_SPEC],
        out_specs=pl.BlockSpec((None, tr, n), lambda i: (layer, first + i, 0)),
        input_output_aliases={1: 0},
        compiler_params=_params(("parallel",)),
    )(recv, buf)


def _sum_slots(recv, name, after=None):
    _, r, n = recv.shape
    tr = _row_tile(r, 512)

    def body(in_ref, *refs):
        acc = in_ref[0].astype(F32)
        for j in range(1, N_DEV):
            acc = acc + in_ref[j].astype(F32)
        refs[-1][...] = acc

    grid = (r // tr,)
    in_specs, out_spec = [pl.BlockSpec((N_DEV, tr, n), lambda i: (0, i, 0))], pl.BlockSpec((tr, n), lambda i: (i, 0))
    args = [recv]
    if after is not None:
        in_specs.append(ANY_SPEC)
        args.append(after)
    return pl.pallas_call(
        body, name=name, grid=grid,
        out_shape=jax.ShapeDtypeStruct((r, n), F32),
        in_specs=in_specs, out_specs=out_spec,
        compiler_params=_params(("parallel",)),
    )(*args)


def _row_tile(rows, target):
    if rows <= target:
        return rows
    best = None
    for t in range(16, target + 1, 16):
        if rows % t == 0:
            best = t
    assert best is not None, rows
    return best


_DIMS = {"nn": ((1,), (0,)), "nt": ((1,), (1,)), "tn": ((0,), (0,))}


def _mm(a, b, mode, name, out_dtype=F32, res=None, gate=None, gate_factor=1.0, tm=512, tn=1408, after=None):
    assert (res is None) == (gate is None)
    if mode == "tn":
        kdim, m = a.shape
    else:
        m, kdim = a.shape
    n = b.shape[0] if mode == "nt" else b.shape[1]
    tm, tn = _tile(m, tm), _tile(n, tn)
    a_spec = (pl.BlockSpec((kdim, tm), lambda i, j: (0, i)) if mode == "tn"
              else pl.BlockSpec((tm, kdim), lambda i, j: (i, 0)))
    b_spec = (pl.BlockSpec((tn, kdim), lambda i, j: (j, 0)) if mode == "nt"
              else pl.BlockSpec((kdim, tn), lambda i, j: (0, j)))
    o_spec = pl.BlockSpec((tm, tn), lambda i, j: (i, j))
    dims = (_DIMS[mode], ((), ()))
    has_res = res is not None

    def body(a_ref, b_ref, *refs):
        y = lax.dot_general(a_ref[...].astype(BF16), b_ref[...].astype(BF16), dims,
                            preferred_element_type=F32)
        if has_res:
            res_ref, gate_ref = refs[0], refs[1]
            y_ref, o_ref = refs[-2], refs[-1]
            y_ref[...] = y.astype(BF16)
            o_ref[...] = res_ref[...] + (gate_factor * gate_ref[...]) * y
        else:
            refs[-1][...] = y.astype(out_dtype)

    in_specs, args = [a_spec, b_spec], [a, b]
    if has_res:
        gate_spec, gate_arg = _vec_in(gate, tile=tn)
        in_specs += [o_spec, gate_spec]
        args += [res, gate_arg]
        out_shape = (jax.ShapeDtypeStruct((m, n), BF16), jax.ShapeDtypeStruct((m, n), F32))
        out_specs = (o_spec, o_spec)
    else:
        out_shape, out_specs = jax.ShapeDtypeStruct((m, n), out_dtype), o_spec
    if after is not None:
        in_specs.append(ANY_SPEC)
        args.append(after)
    return pl.pallas_call(
        body, name=name, grid=(m // tm, n // tn), out_shape=out_shape,
        in_specs=in_specs, out_specs=out_specs,
        compiler_params=_params(("parallel", "parallel")),
    )(*args)


def _vec_in(v, tile=None):
    if isinstance(v, tuple):
        table, row = v
        if tile is None:
            return pl.BlockSpec((None, 1, table.shape[-1]), lambda *idx: (row, 0, 0)), table
        return pl.BlockSpec((None, 1, tile), lambda i, j: (row, 0, j)), table
    if tile is None:
        return pl.BlockSpec((1, v.shape[-1]), lambda *idx: (0, 0)), v
    return pl.BlockSpec((1, tile), lambda i, j: (0, j)), v


def _vec_spec(width):
    return pl.BlockSpec((1, width), lambda i: (0, 0))


def _rm_bwd(dh, x, dres, gw, scale, name, below=None):
    s, d = x.shape
    ts = _tile(s, 512)
    factor = None if below is None else below[2]

    def body(dh_ref, x_ref, dres_ref, gw_ref, sc_ref, *refs):
        dx_ref, dsh_ref, dsc_ref, dgw_ref = refs[-6:-2] if below is not None else refs[-4:]

        @pl.when(pl.program_id(0) == 0)
        def _():
            dsh_ref[...] = jnp.zeros_like(dsh_ref)
            dsc_ref[...] = jnp.zeros_like(dsc_ref)
            dgw_ref[...] = jnp.zeros_like(dgw_ref)
            if below is not None:
                refs[-1][...] = jnp.zeros_like(refs[-1])

        xv, dhv, gwv = x_ref[...], dh_ref[...], gw_ref[...]
        r = lax.rsqrt(jnp.mean(xv * xv, axis=-1, keepdims=True) + EPS)
        xn = xv * r
        y = xn * gwv
        dsh_ref[...] += jnp.sum(dhv, axis=0, keepdims=True)
        dsc_ref[...] += jnp.sum(dhv * y, axis=0, keepdims=True)
        dy = dhv * (1 + sc_ref[...])
        dgw_ref[...] += jnp.sum(dy * xn, axis=0, keepdims=True)
        dxn = dy * gwv
        dx = dres_ref[...] + r * (dxn - xn * jnp.mean(dxn * xn, axis=-1, keepdims=True))
        dx_ref[...] = dx
        if below is not None:
            yb_ref, gb_ref, dyb_ref, dgb_ref = refs[0], refs[1], refs[-2], refs[-1]
            dyb_ref[...] = ((factor * gb_ref[...]) * dx).astype(BF16)
            dgb_ref[...] += jnp.sum((factor * dx) * yb_ref[...].astype(F32), axis=0, keepdims=True)

    row = pl.BlockSpec((ts, d), lambda i: (i, 0))
    vec = jax.ShapeDtypeStruct((1, d), F32)
    (gw_spec, gw), (sc_spec, scale) = _vec_in(gw), _vec_in(scale)
    in_specs, args = [row, row, row, gw_spec, sc_spec], [dh, x, dres, gw, scale]
    out_shape = [jax.ShapeDtypeStruct((s, d), F32), vec, vec, vec]
    out_specs = [row, _vec_spec(d), _vec_spec(d), _vec_spec(d)]
    if below is not None:
        gate_spec, gate_arg = _vec_in(below[1])
        in_specs += [row, gate_spec]
        args += [below[0], gate_arg]
        out_shape += [jax.ShapeDtypeStruct((s, d), BF16), vec]
        out_specs += [row, _vec_spec(d)]
    return pl.pallas_call(
        body, name=name, grid=(s // ts,), out_shape=tuple(out_shape),
        in_specs=in_specs, out_specs=tuple(out_specs),
        compiler_params=_params(("arbitrary",)),
    )(*args)


def _gate_bwd(dx, y, gate, factor, name):
    s, d = dx.shape
    ts = _tile(s, 256)

    def body(dx_ref, y_ref, g_ref, dy_ref, dg_ref):
        @pl.when(pl.program_id(0) == 0)
        def _():
            dg_ref[...] = jnp.zeros_like(dg_ref)

        dxv = dx_ref[...]
        dy_ref[...] = ((factor * g_ref[...]) * dxv).astype(BF16)
        dg_ref[...] += jnp.sum((factor * dxv) * y_ref[...].astype(F32), axis=0, keepdims=True)

    row = pl.BlockSpec((ts, d), lambda i: (i, 0))
    return pl.pallas_call(
        body, name=name, grid=(s // ts,),
        out_shape=(jax.ShapeDtypeStruct((s, d), BF16), jax.ShapeDtypeStruct((1, d), F32)),
        in_specs=[row, row, _vec_in(gate)[0]], out_specs=(row, _vec_spec(d)),
        compiler_params=_params(("arbitrary",)),
    )(dx, y, _vec_in(gate)[1])


def _norm_mm(x, gw, shift, scale, w, name, tm=1024):
    s, d = x.shape
    n = w.shape[0]
    tm = _tile(s, tm)

    def body(x_ref, gw_ref, sh_ref, sc_ref, w_ref, h_ref, z_ref):
        xv = x_ref[...]
        r = lax.rsqrt(jnp.mean(xv * xv, axis=-1, keepdims=True) + EPS)
        hb = (((xv * r) * gw_ref[...]) * (1 + sc_ref[...]) + sh_ref[...]).astype(BF16)
        h_ref[...] = hb
        z_ref[...] = lax.dot_general(hb, w_ref[...], (((1,), (1,)), ((), ())), preferred_element_type=F32)

    row = pl.BlockSpec((tm, d), lambda i: (i, 0))
    return pl.pallas_call(
        body, name=name, grid=(s // tm,),
        out_shape=(jax.ShapeDtypeStruct((s, d), BF16), jax.ShapeDtypeStruct((s, n), F32)),
        in_specs=[row, _vec_in(gw)[0], _vec_in(shift)[0], _vec_in(scale)[0], pl.BlockSpec((n, d), lambda i: (0, 0))],
        out_specs=(row, pl.BlockSpec((tm, n), lambda i: (i, 0))),
        compiler_params=_params(("parallel",)),
    )(x, _vec_in(gw)[1], _vec_in(shift)[1], _vec_in(scale)[1], w)


FFN_TM, FFN_TF = 2048, 256


def _ffn_up(x, gw, shift, scale, wg, wu, name, after=None):
    s, d = x.shape
    f = wg.shape[0]
    tm, tf = _tile(s, FFN_TM), _tile(f, FFN_TF)
    nt = (((1,), (1,)), ((), ()))

    def body(x_ref, gw_ref, sh_ref, sc_ref, wg_ref, wu_ref, *refs):
        h_ref, a_ref, b_ref, t_ref = refs[-4:]

        @pl.when(pl.program_id(1) == 0)
        def _():
            xv = x_ref[...]
            r = lax.rsqrt(jnp.mean(xv * xv, axis=-1, keepdims=True) + EPS)
            h_ref[...] = (((xv * r) * gw_ref[...]) * (1 + sc_ref[...]) + sh_ref[...]).astype(BF16)

        hb = h_ref[...]
        av = lax.dot_general(hb, wg_ref[...], nt, preferred_element_type=F32)
        bv = lax.dot_general(hb, wu_ref[...], nt, preferred_element_type=F32)
        a_ref[...] = av.astype(BF16)
        b_ref[...] = bv.astype(BF16)
        t_ref[...] = ((av * jax.nn.sigmoid(av)) * bv).astype(BF16)

    row = pl.BlockSpec((tm, d), lambda i, j: (i, 0))
    wblk = pl.BlockSpec((tf, d), lambda i, j: (j, 0))
    blk = pl.BlockSpec((tm, tf), lambda i, j: (i, j))
    wide = jax.ShapeDtypeStruct((s, f), BF16)
    vec_specs, vec_args = zip(*[_vec_in(v) for v in (gw, shift, scale)])
    in_specs, args = [row, *vec_specs, wblk, wblk], [x, *vec_args, wg, wu]
    if after is not None:
        in_specs.append(ANY_SPEC)
        args.append(after)
    return pl.pallas_call(
        body, name=name, grid=(s // tm, f // tf),
        out_shape=(jax.ShapeDtypeStruct((s, d), BF16), wide, wide, wide),
        in_specs=in_specs, out_specs=(row, blk, blk, blk),
        compiler_params=_params(("parallel", "arbitrary")),
    )(*args)


def _ffn_bwd_cols(dy, h, a, b, t, wd, name, after=None):
    s, d = dy.shape
    f = wd.shape[0]
    tf = _tile(f, FFN_TF)
    nt = (((1,), (1,)), ((), ()))
    tn = (((0,), (0,)), ((), ()))

    def body(dy_ref, h_ref, a_ref, b_ref, t_ref, wd_ref, *refs):
        da_ref, db_ref, gd_ref, gg_ref, gu_ref = refs[-5:]
        dyb, hb = dy_ref[...], h_ref[...]
        dtv = lax.dot_general(dyb, wd_ref[...], nt, preferred_element_type=F32)
        av, bv = a_ref[...].astype(F32), b_ref[...].astype(F32)
        sg = jax.nn.sigmoid(av)
        dbv = (dtv * (av * sg)).astype(BF16)
        dav = ((dtv * bv) * (sg * (1 + av * (1 - sg)))).astype(BF16)
        da_ref[...] = dav
        db_ref[...] = dbv
        gd_ref[...] = lax.dot_general(t_ref[...], dyb, tn, preferred_element_type=F32).astype(BF16)
        gg_ref[...] = lax.dot_general(dav, hb, tn, preferred_element_type=F32).astype(BF16)
        gu_ref[...] = lax.dot_general(dbv, hb, tn, preferred_element_type=F32).astype(BF16)

    whole = pl.BlockSpec((s, d), lambda j: (0, 0))
    col = pl.BlockSpec((s, tf), lambda j: (0, j))
    wblk = pl.BlockSpec((tf, d), lambda j: (j, 0))
    wide, wgrad = jax.ShapeDtypeStruct((s, f), BF16), jax.ShapeDtypeStruct((f, d), BF16)
    in_specs, args = [whole, whole, col, col, col, wblk], [dy, h, a, b, t, wd]
    if after is not None:
        in_specs.append(ANY_SPEC)
        args.append(after)
    return pl.pallas_call(
        body, name=name, grid=(f // tf,), out_shape=(wide, wide, wgrad, wgrad, wgrad),
        in_specs=in_specs, out_specs=(col, col, wblk, wblk, wblk),
        compiler_params=_params(("parallel",)),
    )(*args)


def _mm_pair(a1, b1, a2, b2, name, tm=1024, tn=512, after=None):
    m, kdim = a1.shape
    n = b1.shape[1]
    tm, tn = _tile(m, tm), _tile(n, tn)

    def body(a1_ref, b1_ref, a2_ref, b2_ref, *refs):
        refs[-1][...] = (jnp.dot(a1_ref[...], b1_ref[...], preferred_element_type=F32)
                         + jnp.dot(a2_ref[...], b2_ref[...], preferred_element_type=F32))

    a_spec = pl.BlockSpec((tm, kdim), lambda i, j: (i, 0))
    b_spec = pl.BlockSpec((kdim, tn), lambda i, j: (0, j))
    in_specs, args = [a_spec, b_spec, a_spec, b_spec], [a1, b1, a2, b2]
    if after is not None:
        in_specs.append(ANY_SPEC)
        args.append(after)
    return pl.pallas_call(
        body, name=name, grid=(m // tm, n // tn), out_shape=jax.ShapeDtypeStruct((m, n), F32),
        in_specs=in_specs, out_specs=pl.BlockSpec((tm, tn), lambda i, j: (i, j)),
        compiler_params=_params(("parallel", "parallel")),
    )(*args)


def _pool_counts(s):
    return (lax.broadcasted_iota(jnp.int32, (s, POOL_GC), 0))


def _pool_fwd(z, pool_w, pool_scale, name):
    s = z.shape[0]

    def body(u_ref, w_ref, sc_ref, y_ref, diff_ref):
        t = lax.broadcasted_iota(jnp.int32, (s, POOL_GC), 0)
        for g, win in enumerate(POOL_WINDOWS):
            cols = slice(g * POOL_GC, (g + 1) * POOL_GC)
            u = u_ref[:, cols]
            acc, step = u, 1
            while step < win:
                acc = acc + jnp.where(t >= step, pltpu.roll(acc, step, 0), 0.0)
                step *= 2
            cnt = jnp.minimum(t + 1, win).astype(F32)
            diff = acc / cnt - u
            diff_ref[:, cols] = diff
            ypre = jnp.dot(diff.astype(BF16), w_ref[g].astype(BF16), preferred_element_type=F32)
            y_ref[:, cols] = (ypre * sc_ref[:, cols]).astype(BF16)

    return pl.pallas_call(
        body, name=name, grid=(1,),
        out_shape=(jax.ShapeDtypeStruct((s, POOL_WIDTH), BF16), jax.ShapeDtypeStruct((s, POOL_WIDTH), F32)),
        in_specs=[pl.BlockSpec((s, POOL_WIDTH), lambda i: (0, 0)),
                  pl.BlockSpec(pool_w.shape, lambda i: (0, 0, 0)),
                  pl.BlockSpec((1, POOL_WIDTH), lambda i: (0, 0))],
        out_specs=(pl.BlockSpec((s, POOL_WIDTH), lambda i: (0, 0)),
                   pl.BlockSpec((s, POOL_WIDTH), lambda i: (0, 0))),
        compiler_params=_params(("arbitrary",)),
    )(z, pool_w, pool_scale)


def _pool_bwd(dycat, diff, pool_w, pool_scale, name):
    s = diff.shape[0]

    def body(dy_ref, diff_ref, w_ref, sc_ref, du_ref, dw_ref, dsc_ref):
        t = lax.broadcasted_iota(jnp.int32, (s, POOL_GC), 0)
        for g, win in enumerate(POOL_WINDOWS):
            cols = slice(g * POOL_GC, (g + 1) * POOL_GC)
            dy, dfb, wb = dy_ref[:, cols], diff_ref[:, cols].astype(BF16), w_ref[g].astype(BF16)
            ypre = jnp.dot(dfb, wb, preferred_element_type=F32)
            dsc_ref[:, cols] = jnp.sum(dy * ypre, axis=0, keepdims=True)
            dypre = (dy * sc_ref[:, cols]).astype(BF16)
            ddiff = lax.dot_general(dypre, wb, (((1,), (1,)), ((), ())), preferred_element_type=F32)
            dw_ref[g] = lax.dot_general(dfb, dypre, (((0,), (0,)), ((), ())), preferred_element_type=F32)
            cnt = jnp.minimum(t + 1, win).astype(F32)
            acc, step = ddiff / cnt, 1
            while step < win:
                acc = acc + jnp.where(t < s - step, pltpu.roll(acc, s - step, 0), 0.0)
                step *= 2
            du_ref[:, cols] = acc - ddiff

    full = pl.BlockSpec((s, POOL_WIDTH), lambda i: (0, 0))
    return pl.pallas_call(
        body, name=name, grid=(1,),
        out_shape=(jax.ShapeDtypeStruct((s, POOL_WIDTH), F32),
                   jax.ShapeDtypeStruct(pool_w.shape, F32),
                   jax.ShapeDtypeStruct((1, POOL_WIDTH), F32)),
        in_specs=[full, full, pl.BlockSpec(pool_w.shape, lambda i: (0, 0, 0)),
                  pl.BlockSpec((1, POOL_WIDTH), lambda i: (0, 0))],
        out_specs=(full, pl.BlockSpec(pool_w.shape, lambda i: (0, 0, 0)),
                   pl.BlockSpec((1, POOL_WIDTH), lambda i: (0, 0))),
        compiler_params=_params(("arbitrary",)),
    )(dycat, diff, pool_w, pool_scale)


def _rope_tables(positions, name):
    s = positions.shape[0]
    ts = _tile(s, 512)
    freq = 1.0 / (ROPE_THETA ** (np.arange(0, QK_ROPE, 2, dtype=np.float32) / QK_ROPE))
    table = np.zeros((1, LANE), np.float32)
    table[0, :QK_ROPE // 2] = freq
    table[0, QK_ROPE // 2:QK_ROPE] = freq

    def body(pos_ref, f_ref, cos_ref, sin_ref):
        ang = pos_ref[...].astype(F32) * f_ref[...]
        cos_ref[...] = jnp.cos(ang)
        sin_ref[...] = jnp.sin(ang)

    out = jax.ShapeDtypeStruct((s, LANE), F32)
    blk = pl.BlockSpec((ts, LANE), lambda i: (i, 0))
    return pl.pallas_call(
        body, name=name, grid=(s // ts,), out_shape=(out, out),
        in_specs=[pl.BlockSpec((ts, 1), lambda i: (i, 0)), _vec_spec(LANE)], out_specs=(blk, blk),
        compiler_params=_params(("parallel",)),
    )(positions, jnp.asarray(table))


def _lane_mod64_low(shape):
    return (lax.broadcasted_iota(jnp.int32, shape, 1) % QK_ROPE) < (QK_ROPE // 2)


def _rope(x, cos, sin):
    rot = jnp.where(_lane_mod64_low(x.shape), -pltpu.roll(x, LANE - 32, 1), pltpu.roll(x, 32, 1))
    return x * cos + rot * sin


def _rope_t(dy, cos, sin):
    w = dy * sin
    rot_t = jnp.where(_lane_mod64_low(dy.shape), pltpu.roll(w, LANE - 32, 1), -pltpu.roll(w, 32, 1))
    return dy * cos + rot_t


def _plain_rms(x, g):
    r = lax.rsqrt(jnp.mean(x * x, axis=-1, keepdims=True) + EPS)
    return (x * r) * g, x * r, r


O_Q, O_KV, O_KR = POOL_WIDTH, POOL_WIDTH + Q_LORA, POOL_WIDTH + Q_LORA + KV_LORA


def _qkv_fwd(z, qn, kvn, wq, wkv, cos, sin, name):
    s = z.shape[0]
    ts = _tile(s, 256)

    def body(z_ref, qn_ref, kvn_ref, wq_ref, wkv_ref, cos_ref, sin_ref, q_ref, k_ref, v_ref, cqn_ref, ckvn_ref):
        cosv, sinv = cos_ref[...], sin_ref[...]
        cqn = _plain_rms(z_ref[:, O_Q:O_KV], qn_ref[...])[0].astype(BF16)
        ckvn = _plain_rms(z_ref[:, O_KV:O_KR], kvn_ref[...])[0].astype(BF16)
        cqn_ref[...] = cqn
        ckvn_ref[...] = ckvn
        nt = (((1,), (1,)), ((), ()))
        q = lax.dot_general(cqn, wq_ref[...], nt, preferred_element_type=F32)
        kv = lax.dot_general(ckvn, wkv_ref[...], nt, preferred_element_type=F32)
        kr = _rope(z_ref[:, O_KR:IN_PAD], cosv, sinv).astype(BF16)
        for h in range(N_HEADS):
            o = h * HEAD_PAD
            q_ref[:, o:o + QK_NOPE] = q[:, o:o + QK_NOPE].astype(BF16)
            q_ref[:, o + QK_NOPE:o + HEAD_PAD] = _rope(q[:, o + QK_NOPE:o + HEAD_PAD], cosv, sinv).astype(BF16)
            k_ref[:, o:o + QK_NOPE] = kv[:, o:o + QK_NOPE].astype(BF16)
            k_ref[:, o + QK_NOPE:o + HEAD_PAD] = kr
            v_ref[:, h * V_HEAD:(h + 1) * V_HEAD] = kv[:, o + QK_NOPE:o + HEAD_PAD].astype(BF16)

    def row(w):
        return pl.BlockSpec((ts, w), lambda i: (i, 0))

    def whole(arr):
        return pl.BlockSpec(arr.shape, lambda i: (0, 0))

    hp = N_HEADS * HEAD_PAD
    return pl.pallas_call(
        body, name=name, grid=(s // ts,),
        out_shape=(jax.ShapeDtypeStruct((s, hp), BF16), jax.ShapeDtypeStruct((s, hp), BF16),
                   jax.ShapeDtypeStruct((s, N_HEADS * V_HEAD), BF16),
                   jax.ShapeDtypeStruct((s, Q_LORA), BF16), jax.ShapeDtypeStruct((s, KV_LORA), BF16)),
        in_specs=[row(IN_PAD), whole(qn), whole(kvn), whole(wq), whole(wkv), row(LANE), row(LANE)],
        out_specs=(row(hp), row(hp), row(N_HEADS * V_HEAD), row(Q_LORA), row(KV_LORA)),
        compiler_params=_params(("parallel",)),
    )(z, qn, kvn, wq, wkv, cos, sin)


def _qkv_bwd(dq, dk, dv, du, z, qn, kvn, wq, wkv, cos, sin, name):
    s = z.shape[0]
    ts = _tile(s, 256)

    def norm_bwd(x, g, dy):
        _, xn, r = _plain_rms(x, g)
        dxn = dy * g
        return r * (dxn - xn * jnp.mean(dxn * xn, axis=-1, keepdims=True)), jnp.sum(dy * xn, axis=0, keepdims=True)

    def body(dq_ref, dk_ref, dv_ref, du_ref, z_ref, qn_ref, kvn_ref, wq_ref, wkv_ref, cos_ref, sin_ref,
             dz_ref, dqb_ref, dkvb_ref, dqn_ref, dkvn_ref):
        @pl.when(pl.program_id(0) == 0)
        def _():
            dqn_ref[...] = jnp.zeros_like(dqn_ref)
            dkvn_ref[...] = jnp.zeros_like(dkvn_ref)

        cosv, sinv = cos_ref[...], sin_ref[...]
        dkr = jnp.zeros((ts, LANE), F32)
        for h in range(N_HEADS):
            o = h * HEAD_PAD
            dqb_ref[:, o:o + QK_NOPE] = dq_ref[:, o:o + QK_NOPE].astype(BF16)
            dqb_ref[:, o + QK_NOPE:o + HEAD_PAD] = _rope_t(dq_ref[:, o + QK_NOPE:o + HEAD_PAD], cosv, sinv).astype(BF16)
            dkvb_ref[:, o:o + QK_NOPE] = dk_ref[:, o:o + QK_NOPE].astype(BF16)
            dkvb_ref[:, o + QK_NOPE:o + HEAD_PAD] = dv_ref[:, h * V_HEAD:(h + 1) * V_HEAD].astype(BF16)
            dkr = dkr + dk_ref[:, o + QK_NOPE:o + HEAD_PAD]
        dcqn = jnp.dot(dqb_ref[...], wq_ref[...], preferred_element_type=F32)
        dckvn = jnp.dot(dkvb_ref[...], wkv_ref[...], preferred_element_type=F32)
        dcq, dqn = norm_bwd(z_ref[:, O_Q:O_KV], qn_ref[...], dcqn)
        dckv, dkvn = norm_bwd(z_ref[:, O_KV:O_KR], kvn_ref[...], dckvn)
        dqn_ref[...] += dqn
        dkvn_ref[...] += dkvn
        dz_ref[:, 0:O_Q] = du_ref[...].astype(BF16)
        dz_ref[:, O_Q:O_KV] = dcq.astype(BF16)
        dz_ref[:, O_KV:O_KR] = dckv.astype(BF16)
        dz_ref[:, O_KR:IN_PAD] = _rope_t(dkr, cosv, sinv).astype(BF16)

    def row(w):
        return pl.BlockSpec((ts, w), lambda i: (i, 0))

    def whole(arr):
        return pl.BlockSpec(arr.shape, lambda i: (0, 0))

    hp = N_HEADS * HEAD_PAD
    return pl.pallas_call(
        body, name=name, grid=(s // ts,),
        out_shape=(jax.ShapeDtypeStruct((s, IN_PAD), BF16), jax.ShapeDtypeStruct((s, hp), BF16),
                   jax.ShapeDtypeStruct((s, hp), BF16),
                   jax.ShapeDtypeStruct((1, Q_LORA), F32), jax.ShapeDtypeStruct((1, KV_LORA), F32)),
        in_specs=[row(hp), row(hp), row(N_HEADS * V_HEAD), row(POOL_WIDTH), row(IN_PAD),
                  whole(qn), whole(kvn), whole(wq), whole(wkv), row(LANE), row(LANE)],
        out_specs=(row(IN_PAD), row(hp), row(hp), whole(qn), whole(kvn)),
        compiler_params=_params(("arbitrary",)),
    )(dq, dk, dv, du, z, qn, kvn, wq, wkv, cos, sin)


def _causal_scores(q, k, i, tq, klen):
    sc = lax.dot_general(q, k, (((1,), (1,)), ((), ())), preferred_element_type=F32) * SOFTMAX_SCALE
    qpos = i * tq + lax.broadcasted_iota(jnp.int32, (tq, klen), 0)
    kpos = lax.broadcasted_iota(jnp.int32, (tq, klen), 1)
    return jnp.where(qpos >= kpos, sc, -jnp.inf)


ATTN_TQ = 512
ATTN_SEGMENTS = 4


def _by_key_prefix(i, nq, tq, compute):
    nseg = min(ATTN_SEGMENTS, nq)
    per = nq // nseg
    for r in range(nseg):
        pl.when(i // per == r)(lambda r=r: compute((r + 1) * per * tq))


def _attn_fwd(q, k, v, name):
    s = q.shape[0]
    tq = _tile(s, ATTN_TQ)
    nq = s // tq

    def body(q_ref, k_ref, v_ref, o_ref, lse_ref):
        i = pl.program_id(1)

        def compute(klen):
            sc = _causal_scores(q_ref[...], k_ref[0:klen, :], i, tq, klen)
            mx = jnp.max(sc, axis=-1, keepdims=True)
            p = jnp.exp(sc - mx)
            den = jnp.sum(p, axis=-1, keepdims=True)
            o_ref[...] = jnp.dot((p / den).astype(BF16), v_ref[0:klen, :], preferred_element_type=F32)
            lse_ref[...] = mx + jnp.log(den)

        _by_key_prefix(i, nq, tq, compute)

    return pl.pallas_call(
        body, name=name, grid=(N_HEADS, s // tq),
        out_shape=(jax.ShapeDtypeStruct((s, N_HEADS * V_HEAD), F32), jax.ShapeDtypeStruct((N_HEADS, s, 1), F32)),
        in_specs=[pl.BlockSpec((tq, HEAD_PAD), lambda h, i: (i, h)),
                  pl.BlockSpec((s, HEAD_PAD), lambda h, i: (0, h)),
                  pl.BlockSpec((s, V_HEAD), lambda h, i: (0, h))],
        out_specs=(pl.BlockSpec((tq, V_HEAD), lambda h, i: (i, h)),
                   pl.BlockSpec((None, tq, 1), lambda h, i: (h, i, 0))),
        compiler_params=_params(("parallel", "parallel")),
    )(q, k, v)


def _attn_bwd(q, k, v, lse, dycat, name):
    s = q.shape[0]
    tq = _tile(s, ATTN_TQ)
    nq = s // tq
    tn_dims = (((0,), (0,)), ((), ()))

    def body(q_ref, k_ref, v_ref, lse_ref, do_ref, dq_ref, dk_ref, dv_ref):
        i = pl.program_id(1)

        @pl.when(i == 0)
        def _():
            dk_ref[...] = jnp.zeros_like(dk_ref)
            dv_ref[...] = jnp.zeros_like(dv_ref)

        def compute(klen):
            qv, kv_, dob = q_ref[...], k_ref[0:klen, :], do_ref[...].astype(BF16)
            sc = _causal_scores(qv, kv_, i, tq, klen)
            p = jnp.exp(sc - lse_ref[...])
            dp = lax.dot_general(dob, v_ref[0:klen, :], (((1,), (1,)), ((), ())), preferred_element_type=F32)
            ds = (p * (dp - jnp.sum(dp * p, axis=-1, keepdims=True)) * SOFTMAX_SCALE).astype(BF16)
            dq_ref[...] = jnp.dot(ds, kv_, preferred_element_type=F32)
            dk_ref[0:klen, :] += lax.dot_general(ds, qv, tn_dims, preferred_element_type=F32)
            dv_ref[0:klen, :] += lax.dot_general(p.astype(BF16), dob, tn_dims, preferred_element_type=F32)

        _by_key_prefix(i, nq, tq, compute)

    n_pool_blocks = POOL_WIDTH // V_HEAD
    return pl.pallas_call(
        body, name=name, grid=(N_HEADS, s // tq),
        out_shape=(jax.ShapeDtypeStruct((s, N_HEADS * HEAD_PAD), F32),
                   jax.ShapeDtypeStruct((s, N_HEADS * HEAD_PAD), F32),
                   jax.ShapeDtypeStruct((s, N_HEADS * V_HEAD), F32)),
        in_specs=[pl.BlockSpec((tq, HEAD_PAD), lambda h, i: (i, h)),
                  pl.BlockSpec((s, HEAD_PAD), lambda h, i: (0, h)),
                  pl.BlockSpec((s, V_HEAD), lambda h, i: (0, h)),
                  pl.BlockSpec((None, tq, 1), lambda h, i: (h, i, 0)),
                  pl.BlockSpec((tq, V_HEAD), lambda h, i: (i, n_pool_blocks + h))],
        out_specs=(pl.BlockSpec((tq, HEAD_PAD), lambda h, i: (i, h)),
                   pl.BlockSpec((s, HEAD_PAD), lambda h, i: (0, h)),
                   pl.BlockSpec((s, V_HEAD), lambda h, i: (0, h))),
        compiler_params=_params(("parallel", "arbitrary")),
    )(q, k, v, lse, dycat)


def _loss_head(x, gw, target, name):
    s, d = x.shape
    ts = _tile(s, 256)

    def body(x_ref, gw_ref, tgt_ref, loss_ref, dx_ref, dgw_ref):
        @pl.when(pl.program_id(0) == 0)
        def _():
            loss_ref[...] = jnp.zeros_like(loss_ref)
            dgw_ref[...] = jnp.zeros_like(dgw_ref)

        xv, gwv = x_ref[...], gw_ref[...]
        r = lax.rsqrt(jnp.mean(xv * xv, axis=-1, keepdims=True) + EPS)
        xn = xv * r
        err = xn * gwv - tgt_ref[...]
        loss_ref[...] += 0.5 * jnp.sum(jnp.mean(err * err, axis=-1, keepdims=True))
        dy = err / d
        dgw_ref[...] += jnp.sum(dy * xn, axis=0, keepdims=True)
        dxn = dy * gwv
        dx_ref[...] = r * (dxn - xn * jnp.mean(dxn * xn, axis=-1, keepdims=True))

    row = pl.BlockSpec((ts, d), lambda i: (i, 0))
    return pl.pallas_call(
        body, name=name, grid=(s // ts,),
        out_shape=(jax.ShapeDtypeStruct((8, LANE), F32), jax.ShapeDtypeStruct((s, d), F32),
                   jax.ShapeDtypeStruct((1, d), F32)),
        in_specs=[row, _vec_spec(d), row],
        out_specs=(pl.BlockSpec((8, LANE), lambda i: (0, 0)), row, _vec_spec(d)),
        compiler_params=_params(("arbitrary",)),
    )(x, gw, target)


def _ada_mod(c_all, ada_w, ada_b, name):
    nl, d, cols = ada_w.shape

    def body(c_ref, w_ref, b_ref, o_ref):
        cv = c_ref[...]
        act = (cv * jax.nn.sigmoid(cv)).astype(BF16)
        o_ref[...] = jnp.dot(act, w_ref[...].astype(BF16), preferred_element_type=F32) + b_ref[...]

    return pl.pallas_call(
        body, name=name, grid=(nl,), out_shape=jax.ShapeDtypeStruct((nl, N_DEV, cols), F32),
        in_specs=[pl.BlockSpec((N_DEV, d), lambda l: (0, 0)),
                  pl.BlockSpec((None, d, cols), lambda l: (l, 0, 0)),
                  pl.BlockSpec((None, 1, cols), lambda l: (l, 0, 0))],
        out_specs=pl.BlockSpec((None, N_DEV, cols), lambda l: (l, 0, 0)),
        compiler_params=_params(("parallel",)),
    )(c_all, ada_w, ada_b)


def _ada_grad(c_pad, dmod_pad, name):
    nl, kpad, cols = dmod_pad.shape
    d = c_pad.shape[1]

    def body(c_ref, dm_ref, o_ref):
        cv = c_ref[...]
        act = (cv * jax.nn.sigmoid(cv)).astype(BF16)
        o_ref[...] = lax.dot_general(act, dm_ref[...].astype(BF16), (((0,), (0,)), ((), ())),
                                     preferred_element_type=F32)

    return pl.pallas_call(
        body, name=name, grid=(nl,), out_shape=jax.ShapeDtypeStruct((nl, d, cols), F32),
        in_specs=[pl.BlockSpec((kpad, d), lambda l: (0, 0)),
                  pl.BlockSpec((None, kpad, cols), lambda l: (l, 0, 0))],
        out_specs=pl.BlockSpec((None, d, cols), lambda l: (l, 0, 0)),
        compiler_params=_params(("parallel",)),
    )(c_pad, dmod_pad)


def _adamw_math(w, g, m, v):
    nm = ADAM_B1 * m + (1.0 - ADAM_B1) * g
    nv = ADAM_B2 * v + (1.0 - ADAM_B2) * (g * g)
    m_hat = nm / (1.0 - ADAM_B1 ** ADAM_STEP)
    v_hat = nv / (1.0 - ADAM_B2 ** ADAM_STEP)
    return -ADAM_LR * (m_hat / (jnp.sqrt(v_hat) + ADAM_EPS) + ADAM_WD * w), nm, nv


def _adamw_rows(w3, gbuf, row_off, m3, v3, name):
    nl, r, d = w3.shape
    tr = _row_tile(math.gcd(r, row_off) if row_off else r, 352)
    first = row_off // tr

    def body(w_ref, g_ref, m_ref, v_ref, go_ref, d_ref, nm_ref, nv_ref):
        gv = g_ref[...]
        go_ref[...] = gv
        d_ref[...], nm_ref[...], nv_ref[...] = _adamw_math(w_ref[...], gv, m_ref[...], v_ref[...])

    blk = pl.BlockSpec((None, tr, d), lambda l, i: (l, i, 0))
    gblk = pl.BlockSpec((None, tr, d), lambda l, i: (l, first + i, 0))
    out = jax.ShapeDtypeStruct((nl, r, d), F32)
    return pl.pallas_call(
        body, name=name, grid=(nl, r // tr), out_shape=(out, out, out, out),
        in_specs=[blk, gblk, blk, blk], out_specs=(blk, blk, blk, blk),
        compiler_params=_params(("parallel", "parallel")),
    )(w3, gbuf, m3, v3)


def _adamw(w, g, m, v, name):
    rows, cols = w.shape
    tr = _row_tile(rows, 512)

    def body(w_ref, g_ref, m_ref, v_ref, d_ref, nm_ref, nv_ref):
        d_ref[...], nm_ref[...], nv_ref[...] = _adamw_math(w_ref[...], g_ref[...], m_ref[...], v_ref[...])

    blk = pl.BlockSpec((tr, cols), lambda i: (i, 0))
    out = jax.ShapeDtypeStruct((rows, cols), F32)
    return pl.pallas_call(
        body, name=name, grid=(rows // tr,), out_shape=(out, out, out),
        in_specs=[blk, blk, blk, blk], out_specs=(blk, blk, blk),
        compiler_params=_params(("parallel",)),
    )(w, g, m, v)


def _adamw_nd(w, g, m, v, name):
    shape = w.shape
    flat = (lambda t: t.reshape(1, -1)) if w.ndim == 1 else (lambda t: t.reshape(-1, shape[-1]))
    return tuple(t.reshape(shape) for t in _adamw(flat(w), flat(g), flat(m), flat(v), name))


def _pad_rows(t, rows):
    return jnp.pad(t, ((0, rows - t.shape[0]), (0, 0)))


def _pack_shard_layer(l, wts):
    def tr(name):
        return wts[name][l].astype(BF16).T

    parts = [tr("ffn1_w_gate"), tr("ffn1_w_up"), wts["ffn1_w_down"][l].astype(BF16),
             tr("ffn2_w_gate"), tr("ffn2_w_up"), wts["ffn2_w_down"][l].astype(BF16),
             wts["w_out"][l].astype(BF16),
             tr("w_kv_b").reshape(KV_SH_ROWS, D_MODEL),
             _pad_rows(tr("w_in"), 160),
             _pad_rows(tr("w_q_b").reshape(Q_SH_ROWS, D_MODEL), Q_PAD_ROWS)]
    return jnp.concatenate(parts, axis=0)


def _full_weights(lands):
    w = dict(zip(("g1", "u1", "d1", "g2", "u2", "d2", "out"), lands))
    small = lands[-1].reshape(N_DEV, SMALL_ROWS, D_MODEL)
    o_in, o_q = OFF_IN - OFF_KV, OFF_Q - OFF_KV
    w["kv"] = small[:, :KV_SH_ROWS].reshape(N_HEADS * HEAD_PAD, KV_LORA)
    w["in"] = _pad_rows(small[:, o_in:o_in + IN_SH].reshape(IN_COLS, D_MODEL), IN_PAD)
    wq = small[:, o_q:o_q + Q_SH_ROWS].reshape(N_HEADS, QK_HEAD, Q_LORA)
    w["q"] = jnp.pad(wq, ((0, 0), (0, HEAD_PAD - QK_HEAD), (0, 0))).reshape(N_HEADS * HEAD_PAD, Q_LORA)
    return w


def _grad_sources_b(gr):
    gq = gr["q"].reshape(N_HEADS, HEAD_PAD, Q_LORA)[:, :QK_HEAD].reshape(N_DEV, Q_SH_ROWS, D_MODEL)
    small = jnp.concatenate([
        gr["kv"].reshape(N_DEV, KV_SH_ROWS, D_MODEL),
        jnp.pad(gr["in"][:IN_COLS].reshape(N_DEV, IN_SH, D_MODEL), ((0, 0), (0, 160 - IN_SH), (0, 0))),
        jnp.pad(gq, ((0, 0), (0, Q_PAD_ROWS - Q_SH_ROWS), (0, 0)))], axis=1)
    return [gr["g2"], gr["u2"], gr["d2"], gr["out"], small.reshape(N_DEV * SMALL_ROWS, D_MODEL)]


def _pack_bf16_pairs(t):
    rows, d = t.shape
    return lax.bitcast_convert_type(t.astype(BF16).reshape(rows // 2, 2, d).transpose(0, 2, 1), F32)


def _unpack_bf16_pairs(p):
    pairs = jnp.swapaxes(lax.bitcast_convert_type(p, BF16), -1, -2)
    return pairs.reshape(p.shape[:-2] + (2 * p.shape[-2], p.shape[-1]))


def _small_layout(nl):
    names = [("dmod", nl * N_MOD), ("ffn1_norm", nl), ("mix_norm", nl), ("ffn2_norm", nl), ("q_a_norm", nl),
             ("kv_a_norm", nl), ("pool_scale", nl), ("final_norm", 1), ("loss", 1),
             ("pool_w", nl * 4 * POOL_GC * POOL_GC // D_MODEL // 2)]
    off, table = 0, {}
    for name, n in names:
        table[name] = (off, n)
        off += -(-n // 8) * 8
    return table, off


def _to_rows(t, width=D_MODEL):
    n, w = t.shape
    return jnp.pad(t, ((0, -(-n // 8) * 8 - n), (0, width - w)))


def kernel(x, c, positions, ada_w, ada_b, ffn1_norm, ffn1_w_gate, ffn1_w_up, ffn1_w_down, mix_norm, w_in, pool_w, pool_scale, q_a_norm, w_q_b, kv_a_norm, w_kv_b, w_out, ffn2_norm, ffn2_w_gate, ffn2_w_up, ffn2_w_down, final_norm, loss_target, m_ada_w, m_ada_b, m_ffn1_norm, m_ffn1_w_gate, m_ffn1_w_up, m_ffn1_w_down, m_mix_norm, m_w_in, m_pool_w, m_pool_scale, m_q_a_norm, m_w_q_b, m_kv_a_norm, m_w_kv_b, m_w_out, m_ffn2_norm, m_ffn2_w_gate, m_ffn2_w_up, m_ffn2_w_down, m_final_norm, v_ada_w, v_ada_b, v_ffn1_norm, v_ffn1_w_gate, v_ffn1_w_up, v_ffn1_w_down, v_mix_norm, v_w_in, v_pool_w, v_pool_scale, v_q_a_norm, v_w_q_b, v_kv_a_norm, v_w_kv_b, v_w_out, v_ffn2_norm, v_ffn2_w_gate, v_ffn2_w_up, v_ffn2_w_down, v_final_norm):
    wts = dict(ada_w=ada_w, ada_b=ada_b, ffn1_norm=ffn1_norm, ffn1_w_gate=ffn1_w_gate, ffn1_w_up=ffn1_w_up,
               ffn1_w_down=ffn1_w_down, mix_norm=mix_norm, w_in=w_in, pool_w=pool_w, pool_scale=pool_scale,
               q_a_norm=q_a_norm, w_q_b=w_q_b, kv_a_norm=kv_a_norm, w_kv_b=w_kv_b, w_out=w_out,
               ffn2_norm=ffn2_norm, ffn2_w_gate=ffn2_w_gate, ffn2_w_up=ffn2_w_up, ffn2_w_down=ffn2_w_down,
               final_norm=final_norm)
    mom_m = dict(ada_w=m_ada_w, ada_b=m_ada_b, ffn1_norm=m_ffn1_norm, ffn1_w_gate=m_ffn1_w_gate,
                 ffn1_w_up=m_ffn1_w_up, ffn1_w_down=m_ffn1_w_down, mix_norm=m_mix_norm, w_in=m_w_in,
                 pool_w=m_pool_w, pool_scale=m_pool_scale, q_a_norm=m_q_a_norm, w_q_b=m_w_q_b,
                 kv_a_norm=m_kv_a_norm, w_kv_b=m_w_kv_b, w_out=m_w_out, ffn2_norm=m_ffn2_norm,
                 ffn2_w_gate=m_ffn2_w_gate, ffn2_w_up=m_ffn2_w_up, ffn2_w_down=m_ffn2_w_down,
                 final_norm=m_final_norm)
    mom_v = dict(ada_w=v_ada_w, ada_b=v_ada_b, ffn1_norm=v_ffn1_norm, ffn1_w_gate=v_ffn1_w_gate,
                 ffn1_w_up=v_ffn1_w_up, ffn1_w_down=v_ffn1_w_down, mix_norm=v_mix_norm, w_in=v_w_in,
                 pool_w=v_pool_w, pool_scale=v_pool_scale, q_a_norm=v_q_a_norm, w_q_b=v_w_q_b,
                 kv_a_norm=v_kv_a_norm, w_kv_b=v_w_kv_b, w_out=v_w_out, ffn2_norm=v_ffn2_norm,
                 ffn2_w_gate=v_ffn2_w_gate, ffn2_w_up=v_ffn2_w_up, ffn2_w_down=v_ffn2_w_down,
                 final_norm=v_final_norm)
    order = list(wts)
    nl = ada_w.shape[0]
    seq = x.shape[1]
    me = 4 * lax.axis_index("x") + 2 * lax.axis_index("y") + lax.axis_index("c")
    ada_cols = ada_w.shape[2]

    def after_token(t, token):
        return t + token[0:1, 0:1].astype(t.dtype)

    packs = [_pack_shard_layer(l, wts) for l in range(nl)]

    c_all = _all_gather(jnp.broadcast_to(c, (8, D_MODEL)), "gather_c")[::8]

    ada_b_mine = lax.dynamic_slice_in_dim(ada_b, me * ada_cols, ada_cols, axis=1).reshape(nl, 1, ada_cols)
    mod_part = _ada_mod(c_all, ada_w, ada_b_mine, "ada_mod")
    mod_all = _all_gather(mod_part.reshape(nl * N_DEV, ada_cols), "gather_mod")
    mod_all = mod_all.reshape(N_DEV, nl, N_DEV, ada_cols)
    mod = lax.dynamic_index_in_dim(mod_all, me, axis=2, keepdims=False)
    mod = mod.transpose(1, 0, 2).reshape(nl * N_MOD, 1, D_MODEL)
    norm_tables = {name: wts[name].reshape(nl, 1, D_MODEL) for name in ("ffn1_norm", "mix_norm", "ffn2_norm")}

    def modrow(l, k):
        return mod, l * N_MOD + k

    def normrow(name, l):
        return norm_tables[name], l

    flight_a = _gather_start(packs[0][:SPLIT_AB], ROWS_A, mod, "gather_start_0a")
    flight_b = _gather_start(packs[0][SPLIT_AB:], ROWS_B, flight_a[4], "gather_start_0b")
    last_start = flight_b[4]
    if nl > 1:
        in_flight = _gather_start(packs[1], ROWS_ALL, last_start, "gather_start_1")
        last_start = in_flight[4]

    cos, sin = _rope_tables(after_token(positions.reshape(seq, 1), last_start), "rope_tables")

    def vec(t):
        return t.reshape(1, -1)

    def landed(flight, rows_list, after, tag):
        send_sems, recv_sems, pk, lands, _ = flight
        pk, lands = _gather_wait(send_sems, recv_sems, pk, lands, after, f"gather_wait_{tag}")
        return _gather_finish(pk, rows_list, lands, "gather_finish")

    xs = x.reshape(seq, D_MODEL)
    saved = []
    for l in range(nl):
        norm1, up_after = normrow("ffn1_norm", l), None
        if l == 0:
            lands = landed(flight_a, ROWS_A, cos, "0a")
        elif l + 1 < nl:
            in_flight = _gather_start(packs[l + 1], ROWS_ALL, lands[0], f"gather_start_{l + 1}")
            up_after = in_flight[4]
        sv = {}

        def ffn_fwd(xin, norm, k0, wg, wu, wd, tag, after=None):
            h, a, b, t = _ffn_up(xin, norm, modrow(l, k0), modrow(l, k0 + 1), wg, wu, "ffn_up", after=after)
            y, xout = _mm(t, wd, "nn", "ffn_down", res=xin, gate=modrow(l, k0 + 2), gate_factor=0.5)
            sv[tag] = dict(x=xin, h=h, a=a, b=b, t=t, y=y)
            return xout

        xs = ffn_fwd(xs, norm1, 0, lands[0], lands[1], lands[2], "f1", up_after)
        if l == 0:
            lands = lands + landed(flight_b, ROWS_B, xs, "0b")
        w = _full_weights(lands)
        sv["w"] = w

        h2, z = _norm_mm(xs, normrow("mix_norm", l), modrow(l, 3), modrow(l, 4), w["in"], "mix_in")
        y_pool, diff = _pool_fwd(z, pool_w[l], vec(pool_scale[l]), "pool_fwd")
        q, k, v, cqn, ckvn = _qkv_fwd(z, vec(q_a_norm[l]), vec(kv_a_norm[l]), w["q"], w["kv"], cos, sin, "qkv_fwd")
        o, lse = _attn_fwd(q, k, v, "attn_fwd")
        ycat = jnp.concatenate([y_pool, o.astype(BF16)], axis=1)
        y2, xmix = _mm(ycat, w["out"], "nn", "mix_out", res=xs, gate=modrow(l, 5), gate_factor=1.0)
        sv["mix"] = dict(x=xs, h=h2, z=z, diff=diff, q=q, k=k, v=v, cqn=cqn, ckvn=ckvn, lse=lse, ycat=ycat, y=y2)
        xs = xmix

        xs = ffn_fwd(xs, normrow("ffn2_norm", l), 6, w["g2"], w["u2"], w["d2"], "f2")
        saved.append(sv)
        if l + 1 < nl:
            lands = landed(in_flight, ROWS_ALL, xs, l + 1)

    loss_part, dx, d_final = _loss_head(xs, vec(final_norm), loss_target.reshape(seq, D_MODEL), "loss_head")

    small = {name: [None] * nl for name in ("ffn1_norm", "mix_norm", "ffn2_norm", "q_a_norm", "kv_a_norm",
                                            "pool_scale", "pool_w", "dmod")}
    core = lax.axis_index("c").astype(jnp.int32).reshape(1)
    chip = 2 * lax.axis_index("x") + lax.axis_index("y")
    exchanges = []

    def leave(srcs, rows_list, after, tag):
        return _pair_start(srcs, rows_list, after, f"pair_start_{tag}"), rows_list, tag

    def forward_on(pending, after, layer, row_off):
        (send_sems, recv_sems, srcs, land, _), rows_list, tag = pending
        srcs, land = _split_wait(send_sems, recv_sems, 1, srcs, land, after, f"pair_wait_{tag}")
        sums = _pair_sum(srcs, rows_list, land, core, "pair_sum")
        flight = _chip_exchange_start(sums, chip, after, f"exchange_start_{tag}")
        exchanges.append((flight, layer, row_off, tag))
        return flight[4]

    pending = None
    head = _gate_bwd(dx, saved[nl - 1]["f2"]["y"], modrow(nl - 1, 8), 0.5, "gate_bwd")
    for l in reversed(range(nl)):
        sv = saved[l]
        w = sv["w"]
        dmod = [None] * N_MOD
        gr = {}

        def ffn_bwd(dxin, head, s_, norm, k0, wg, wu, wd, tag, below, first_after=None, mid=None):
            dy, dmod[k0 + 2] = head
            da, db, gr["d" + tag], gr["g" + tag], gr["u" + tag] = _ffn_bwd_cols(
                dy, s_["h"], s_["a"], s_["b"], s_["t"], wd, "ffn_bwd_cols", after=first_after)
            dh = _mm_pair(da, wg, db, wu, "ffn_bwd_dh", after=None if mid is None else mid(da))
            outs = _rm_bwd(dh, s_["x"], dxin, norm, modrow(l, k0 + 1), "rm_bwd", below=below)
            dmod[k0], dmod[k0 + 1] = outs[1], outs[2]
            return outs[0], outs[3], outs[4:]

        s_ = sv["mix"]
        dx, small["ffn2_norm"][l], head = ffn_bwd(
            dx, head, sv["f2"], normrow("ffn2_norm", l), 6, w["g2"], w["u2"], w["d2"], "2", (s_["y"], modrow(l, 5), 1.0),
            first_after=None if pending is None else pending[0][4])

        pending_c = leave([gr["g2"], gr["u2"], gr["d2"]], ROWS_A, dx, f"{l}c")
        mix_after = pending_c[0][4]
        if pending is not None:
            mix_after = forward_on(pending, mix_after, l + 1, GB_F1)
            pending = None
        dy, dmod[5] = head
        gr["out"] = _mm(s_["ycat"], dy, "tn", "mix_out_dw", out_dtype=BF16, tm=256, after=mix_after)
        dycat = _mm(dy, w["out"], "nt", "mix_out_dx")
        du, small["pool_w"][l], small["pool_scale"][l] = _pool_bwd(dycat, s_["diff"], pool_w[l], vec(pool_scale[l]), "pool_bwd")
        dq, dk, dv = _attn_bwd(s_["q"], s_["k"], s_["v"], s_["lse"], dycat, "attn_bwd")
        dz, dqb, dkvb, small["q_a_norm"][l], small["kv_a_norm"][l] = _qkv_bwd(
            dq, dk, dv, du, s_["z"], vec(q_a_norm[l]), vec(kv_a_norm[l]), w["q"], w["kv"], cos, sin, "qkv_bwd")
        gr["q"] = _mm(dqb, s_["cqn"], "tn", "q_b_dw", out_dtype=BF16, tm=256, after=forward_on(pending_c, dz, l, GB_F2))
        gr["kv"] = _mm(dkvb, s_["ckvn"], "tn", "kv_b_dw", out_dtype=BF16, tm=256)
        gr["in"] = _mm(dz, s_["h"], "tn", "mix_in_dw", out_dtype=BF16, tm=256)
        dh2 = _mm(dz, w["in"], "nn", "mix_in_dx")
        outs = _rm_bwd(dh2, s_["x"], dx, normrow("mix_norm", l), modrow(l, 4), "rm_bwd",
                       below=(sv["f1"]["y"], modrow(l, 2), 0.5))
        dx, dmod[3], dmod[4], small["mix_norm"][l] = outs[:4]
        head = outs[4:]

        first_after, mid = None, None
        if l == 0:
            pending_b = leave(_grad_sources_b(gr)[3:], ROWS_TAIL, dx, "0b")
            first_after = pending_b[0][4]
            last_groups = []

            def mid(da):
                last_groups.append(leave([gr["g1"], gr["u1"], gr["d1"]], ROWS_A, da, "0a"))
                return forward_on(pending_b, last_groups[0][0][4], 0, GB_TAIL)
        below = (saved[l - 1]["f2"]["y"], modrow(l - 1, 8), 0.5) if l > 0 else None
        dx, small["ffn1_norm"][l], head = ffn_bwd(
            dx, head, sv["f1"], normrow("ffn1_norm", l), 0, w["g1"], w["u1"], w["d1"], "1", below, first_after, mid)

        small["dmod"][l] = jnp.concatenate(dmod, axis=0)
        if l > 0:
            pending = leave([gr["g1"], gr["u1"], gr["d1"]] + _grad_sources_b(gr)[3:], ROWS_A + ROWS_TAIL, dx, l)

    grad_x = dx.reshape(x.shape)
    pending_a = last_groups[0]

    layout, small_rows = _small_layout(nl)
    pieces = {
        "dmod": jnp.concatenate(small["dmod"], axis=0),
        "ffn1_norm": jnp.concatenate(small["ffn1_norm"], axis=0),
        "mix_norm": jnp.concatenate(small["mix_norm"], axis=0),
        "ffn2_norm": jnp.concatenate(small["ffn2_norm"], axis=0),
        "q_a_norm": jnp.concatenate(small["q_a_norm"], axis=0),
        "kv_a_norm": jnp.concatenate(small["kv_a_norm"], axis=0),
        "pool_scale": jnp.concatenate(small["pool_scale"], axis=0),
        "final_norm": d_final,
        "loss": jnp.broadcast_to(loss_part[0:1, 0:1], (1, D_MODEL)),
        "pool_w": _pack_bf16_pairs(jnp.stack(small["pool_w"]).reshape(-1, D_MODEL)),
    }
    small_buf = jnp.concatenate([_to_rows(pieces[name]) for name in layout], axis=0)
    def landed_sums(gbuf, entries, after):
        for (send_sems, recv_sems, sums, recv, _), layer, row_off, tag in entries:
            _, recv = _split_wait(send_sems, recv_sems, N_CHIPS - 1, sums, recv, after, f"exchange_wait_{tag}")
            gbuf = _sum_slots_into(recv, gbuf, layer, row_off, "sum_grads")
        return gbuf

    gbuf = lax.empty((nl, ROWS_L, D_MODEL), F32)
    token_0a = forward_on(pending_a, dx, 0, GB_F1)
    spread = _spread_start(small_buf, me, token_0a, "small_start")
    gbuf = landed_sums(gbuf, [e for e in exchanges if e[3] != "0a"], spread[4])

    def swap(t):
        return t.transpose(0, 2, 1)

    def same(t):
        return t

    grads, updates = {}, {}

    def update_rows(gbuf, table):
        for wname, off, view in table:
            g, d_, nm, nv = _adamw_rows(view(wts[wname]), gbuf, off, view(mom_m[wname]), view(mom_v[wname]), "adamw_rows")
            grads[wname], updates[wname] = view(g), (view(d_), view(nm), view(nv))

    update_rows(gbuf, (("ffn2_w_gate", GB_F2, swap), ("ffn2_w_up", GB_F2 + FF_SH, swap),
                       ("ffn2_w_down", GB_F2 + 2 * FF_SH, same), ("w_out", GB_TAIL, same)))
    small_grads = {
        "w_kv_b": (gbuf[:, OFF_KV:OFF_KV + KV_SH_ROWS].reshape(nl, -1, KV_LORA).transpose(0, 2, 1), same),
        "w_in": (gbuf[:, OFF_IN:OFF_IN + IN_SH], swap),
        "w_q_b": (gbuf[:, OFF_Q:OFF_Q + Q_SH_ROWS].reshape(nl, -1, Q_LORA), swap),
    }
    for wname, (g, view) in small_grads.items():
        upd = _adamw_nd(view(wts[wname]), g, view(mom_m[wname]), view(mom_v[wname]), "adamw")
        grads[wname], updates[wname] = view(g), tuple(view(t) for t in upd)
    gbuf = landed_sums(gbuf, [e for e in exchanges if e[3] == "0a"], updates["w_q_b"][0])
    update_rows(gbuf, (("ffn1_w_gate", GB_F1, swap), ("ffn1_w_up", GB_F1 + FF_SH, swap),
                       ("ffn1_w_down", GB_F1 + 2 * FF_SH, same)))

    _, small_all = _split_wait(spread[0], spread[1], N_DEV - 1, spread[2], spread[3], updates["ffn1_w_down"][0],
                               "small_wait")
    pool_off, pool_rows = layout["pool_w"]
    small_sum = _sum_slots(small_all[:, :pool_off], "sum_small")
    pool_sum = _sum_slots(_unpack_bf16_pairs(small_all[:, pool_off:pool_off + pool_rows]), "sum_pool_w")

    def take(name, width=D_MODEL):
        off, n = layout[name]
        return small_sum[off:off + n, :width]

    late = {"ada_b": take("dmod").reshape(nl, N_MOD * D_MODEL),
            "ffn1_norm": take("ffn1_norm"), "mix_norm": take("mix_norm"), "ffn2_norm": take("ffn2_norm"),
            "q_a_norm": take("q_a_norm", Q_LORA), "kv_a_norm": take("kv_a_norm", KV_LORA),
            "pool_scale": take("pool_scale", POOL_WIDTH), "final_norm": take("final_norm").reshape(D_MODEL),
            "pool_w": pool_sum.reshape(pool_w.shape)}
    loss = take("loss")[0, 0]

    off, n = layout["dmod"]
    dmod_all = small_all[:, off:off + n].reshape(N_DEV, nl, N_MOD * D_MODEL)
    dmod_mine = lax.dynamic_slice_in_dim(dmod_all, me * ada_cols, ada_cols, axis=2)
    dmod_pad = jnp.pad(dmod_mine.transpose(1, 0, 2), ((0, 0), (0, LANE - N_DEV), (0, 0)))
    late["ada_w"] = _ada_grad(jnp.pad(c_all, ((0, LANE - N_DEV), (0, 0))), dmod_pad, "ada_grad")
    for name, g in late.items():
        grads[name], updates[name] = g, _adamw_nd(wts[name], g, mom_m[name], mom_v[name], "adamw")

    return (loss, grad_x, *[grads[n] for n in order], *[updates[n][0] for n in order],
            *[updates[n][1] for n in order], *[updates[n][2] for n in order])
```

```python
import math

import numpy as np
import jax
import jax.numpy as jnp
from jax import lax
from jax.experimental import pallas as pl
from jax.experimental.pallas import tpu as pltpu

F32 = jnp.float32
BF16 = jnp.bfloat16

N_DEV = 8
D_MODEL = 1024
D_FF = 2816
POOL_WIDTH = 512
POOL_WINDOWS = (2, 4, 8, 16)
POOL_GC = 128
N_HEADS = 4
QK_NOPE = 128
QK_ROPE = 64
V_HEAD = 128
QK_HEAD = QK_NOPE + QK_ROPE
HEAD_PAD = 256
Q_LORA = 384
KV_LORA = 256
IN_COLS = POOL_WIDTH + Q_LORA + KV_LORA + QK_ROPE
IN_PAD = 1280
ROPE_THETA = 10000.0
SOFTMAX_SCALE = 1.0 / math.sqrt(QK_HEAD)
EPS = 1e-6
N_MOD = 9

ADAM_LR = 0.001
ADAM_B1 = 0.9
ADAM_B2 = 0.999
ADAM_EPS = 1e-08
ADAM_WD = 0.01
ADAM_STEP = 10

LANE = 128
VMEM_LIMIT = 56 * 1024 * 1024

FF_SH = D_FF // N_DEV
OFF_G1, OFF_U1, OFF_D1 = 0, FF_SH, 2 * FF_SH
OFF_G2, OFF_U2, OFF_D2 = 3 * FF_SH, 4 * FF_SH, 5 * FF_SH
OFF_OUT = 6 * FF_SH
OFF_KV = OFF_OUT + 128
OFF_IN = OFF_KV + 32
OFF_Q = OFF_IN + 160
Q_PAD_ROWS = 64
ROWS_L = OFF_Q + Q_PAD_ROWS
IN_SH = IN_COLS // N_DEV
Q_SH_ROWS = (N_HEADS * QK_HEAD // N_DEV) * Q_LORA // D_MODEL
KV_SH_ROWS = (N_HEADS * (QK_NOPE + V_HEAD) // N_DEV) * KV_LORA // D_MODEL


def _tile(dim, target):
    if dim <= target:
        return dim
    best = None
    for t in range(LANE, target + 1, LANE):
        if dim % t == 0:
            best = t
    assert best is not None, (dim, target)
    return best


def _params(sem):
    return pltpu.CompilerParams(dimension_semantics=sem, vmem_limit_bytes=VMEM_LIMIT)


def _mesh_pos():
    return lax.axis_index("x"), lax.axis_index("y"), lax.axis_index("c")


def _all_gather(x, name):
    m, n = x.shape

    def body(x_ref, out_ref, send_sems, recv_sems, local_sem):
        px, py, pc = _mesh_pos()
        me, sibling = (px, py, pc), (px, py, 1 - pc)
        chips = [(1 - px, py), (px, 1 - py), (1 - px, 1 - py)]

        def rows(bx, by, bc):
            return out_ref.at[pl.ds((4 * bx + 2 * by + bc) * m, m), :]

        def copy(k, block, to, src=None):
            return pltpu.make_async_remote_copy(
                src_ref=rows(*block) if src is None else src, dst_ref=rows(*block),
                send_sem=send_sems.at[k], recv_sem=recv_sems.at[k],
                device_id=to, device_id_type=pl.DeviceIdType.MESH)

        mine = pltpu.make_async_copy(x_ref, rows(*me), local_sem)
        mine.start()
        first = [copy(0, me, sibling, src=x_ref)]
        first += [copy(1 + j, me, (*chip, pc), src=x_ref) for j, chip in enumerate(chips)]
        for cp in first:
            cp.start()
        passed = [copy(4 + j, (*chip, pc), sibling) for j, chip in enumerate(chips)]
        for j, chip in enumerate(chips):
            copy(1 + j, (*chip, pc), me).wait_recv()
            passed[j].start()
        copy(0, sibling, me).wait_recv()
        for j, chip in enumerate(chips):
            copy(4 + j, (*chip, 1 - pc), me).wait_recv()
        for cp in first + passed:
            cp.wait_send()
        mine.wait()

    hbm = pl.BlockSpec(memory_space=pltpu.HBM)
    return pl.pallas_call(
        body, name=name,
        out_shape=jax.ShapeDtypeStruct((N_DEV * m, n), x.dtype),
        in_specs=[hbm], out_specs=hbm,
        scratch_shapes=[pltpu.SemaphoreType.DMA((7,)), pltpu.SemaphoreType.DMA((7,)),
                        pltpu.SemaphoreType.DMA],
    )(x)


SMALL_ROWS = ROWS_L - OFF_KV
ROWS_A = [FF_SH] * 3
ROWS_B = [FF_SH] * 3 + [128, SMALL_ROWS]
ROWS_ALL = ROWS_A + ROWS_B
SPLIT_AB = sum(ROWS_A)
ROWS_TAIL = [128, SMALL_ROWS]
GB_F2, GB_F1, GB_TAIL = 0, SPLIT_AB, 2 * SPLIT_AB
HBM_SPEC = pl.BlockSpec(memory_space=pltpu.HBM)
SEM_SPEC = pl.BlockSpec(memory_space=pltpu.SEMAPHORE)
ANY_SPEC = pl.BlockSpec(memory_space=pl.ANY)
EFFECT = pltpu.SideEffectType.DATAFLOW_SIDE_EFFECTING


def _hbm(t):
    return pltpu.with_memory_space_constraint(t, pltpu.HBM)


def _whole_wait(ref, send_sem, recv_sem, peer):
    return pltpu.make_async_remote_copy(src_ref=ref, dst_ref=ref, send_sem=send_sem, recv_sem=recv_sem,
                                        device_id=peer, device_id_type=pl.DeviceIdType.MESH)


def _offsets(rows_list):
    return [sum(rows_list[:i]) for i in range(len(rows_list))]


def _gather_start(packed, rows_list, after, name):
    n = len(rows_list)
    offs = _offsets(rows_list)
    lands = [_hbm(lax.empty((N_DEV * rows, D_MODEL), BF16)) for rows in rows_list]

    def body(packed_ref, *refs):
        land = refs[:n]
        send_sems, recv_sems = refs[n + 1], refs[n + 2]
        token = refs[-1]
        px, py, pc = _mesh_pos()
        me = 4 * px + 2 * py + pc
        peers = [(px, py, 1 - pc), (1 - px, py, pc), (px, 1 - py, pc), (1 - px, 1 - py, pc)]
        for k, peer in enumerate(peers):
            for off, rows, land_ref in zip(offs, rows_list, land):
                pltpu.make_async_remote_copy(
                    src_ref=packed_ref.at[pl.ds(off, rows), :], dst_ref=land_ref.at[pl.ds(me * rows, rows), :],
                    send_sem=send_sems.at[k], recv_sem=recv_sems.at[k],
                    device_id=peer, device_id_type=pl.DeviceIdType.MESH).start()
        token[...] = jnp.zeros_like(token)

    outs = pl.pallas_call(
        body, name=name,
        out_shape=(pltpu.SemaphoreType.DMA((4,)), pltpu.SemaphoreType.DMA((4,)), pltpu.HBM(packed.shape, BF16),
                   *[pltpu.HBM(t.shape, BF16) for t in lands], jax.ShapeDtypeStruct((8, LANE), F32)),
        in_specs=(HBM_SPEC,) * (1 + n) + (ANY_SPEC,),
        out_specs=(SEM_SPEC, SEM_SPEC) + (HBM_SPEC,) * (1 + n) + (pl.BlockSpec(memory_space=pltpu.VMEM),),
        input_output_aliases={i: 2 + i for i in range(1 + n)},
        compiler_params=pltpu.CompilerParams(has_side_effects=EFFECT),
    )(_hbm(packed), *lands, after)
    return outs[0], outs[1], outs[2], list(outs[3:3 + n]), outs[-1]


def _gather_wait(send_sems, recv_sems, packed, lands, after, name):
    n = len(lands)

    def body(packed_ref, *refs):
        s_sems, r_sems = refs[n], refs[n + 1]
        me = _mesh_pos()
        for k in range(4):
            cp = _whole_wait(packed_ref, s_sems.at[k], r_sems.at[k], me)
            cp.wait_send()
            cp.wait_recv()

    outs = pl.pallas_call(
        body, name=name,
        out_shape=(pltpu.HBM(packed.shape, BF16), *[pltpu.HBM(t.shape, BF16) for t in lands]),
        in_specs=(HBM_SPEC,) * (1 + n) + (SEM_SPEC, SEM_SPEC, ANY_SPEC),
        out_specs=(HBM_SPEC,) * (1 + n),
        input_output_aliases={i: i for i in range(1 + n)},
        compiler_params=pltpu.CompilerParams(has_side_effects=EFFECT),
    )(packed, *lands, send_sems, recv_sems, after)
    return outs[0], list(outs[1:])


def _gather_finish(packed, rows_list, lands, name):
    n = len(rows_list)
    offs = _offsets(rows_list)

    def body(packed_ref, *refs):
        land = refs[n:2 * n]
        send_sems, recv_sems, stage, stage_sem = refs[2 * n:]
        px, py, pc = _mesh_pos()
        me = 4 * px + 2 * py + pc
        sibling = (px, py, 1 - pc)
        load = pltpu.make_async_copy(packed_ref, stage, stage_sem)
        load.start()
        load.wait()
        for off, rows, land_ref in zip(offs, rows_list, land):
            pltpu.make_async_copy(stage.at[pl.ds(off, rows), :], land_ref.at[pl.ds(me * rows, rows), :],
                                  stage_sem).start()
        for j, (cx, cy) in enumerate([(1 - px, py), (px, 1 - py), (1 - px, 1 - py)]):
            block = 4 * cx + 2 * cy + pc
            for rows, land_ref in zip(rows_list, land):
                blk = land_ref.at[pl.ds(block * rows, rows), :]
                pltpu.make_async_remote_copy(src_ref=blk, dst_ref=blk, send_sem=send_sems.at[j],
                                             recv_sem=recv_sems.at[j], device_id=sibling,
                                             device_id_type=pl.DeviceIdType.MESH).start()
        for j in range(3):
            cp = _whole_wait(packed_ref, send_sems.at[j], recv_sems.at[j], sibling)
            cp.wait_recv()
            cp.wait_send()
        pltpu.make_async_copy(stage, packed_ref, stage_sem).wait()

    outs = pl.pallas_call(
        body, name=name,
        out_shape=tuple(jax.ShapeDtypeStruct(t.shape, BF16) for t in lands),
        in_specs=(HBM_SPEC,) * (1 + n), out_specs=(HBM_SPEC,) * n,
        input_output_aliases={1 + i: i for i in range(n)},
        scratch_shapes=[pltpu.SemaphoreType.DMA((3,)), pltpu.SemaphoreType.DMA((3,)),
                        pltpu.VMEM(packed.shape, BF16), pltpu.SemaphoreType.DMA],
    )(packed, *lands)
    return list(outs)


N_CHIPS = 4


def _pair_start(srcs, rows_list, after, name):
    n = len(rows_list)
    offs = _offsets(rows_list)
    land = lax.empty((N_CHIPS, sum(rows_list), D_MODEL), BF16)

    def body(*refs):
        src, land_ref = refs[:n], refs[n]
        send_sems, recv_sems = refs[n + 2], refs[n + 3]
        token = refs[-1]
        px, py, pc = _mesh_pos()
        for k in range(N_CHIPS):
            block = 2 * k + (1 - pc)
            for off, rows, src_ref in zip(offs, rows_list, src):
                pltpu.make_async_remote_copy(
                    src_ref=src_ref.at[pl.ds(block * rows, rows), :], dst_ref=land_ref.at[k, pl.ds(off, rows), :],
                    send_sem=send_sems.at[0], recv_sem=recv_sems.at[0],
                    device_id=(px, py, 1 - pc), device_id_type=pl.DeviceIdType.MESH).start()
        token[...] = jnp.zeros_like(token)

    outs = pl.pallas_call(
        body, name=name,
        out_shape=(pltpu.SemaphoreType.DMA((1,)), pltpu.SemaphoreType.DMA((1,)),
                   *[pltpu.HBM(t.shape, BF16) for t in srcs], pltpu.HBM(land.shape, BF16),
                   jax.ShapeDtypeStruct((8, LANE), F32)),
        in_specs=(HBM_SPEC,) * (n + 1) + (ANY_SPEC,),
        out_specs=(SEM_SPEC, SEM_SPEC) + (HBM_SPEC,) * (n + 1) + (pl.BlockSpec(memory_space=pltpu.VMEM),),
        input_output_aliases={i: 2 + i for i in range(n + 1)},
        compiler_params=pltpu.CompilerParams(has_side_effects=EFFECT),
    )(*[_hbm(t) for t in srcs], _hbm(land), after)
    return outs[0], outs[1], list(outs[2:2 + n]), outs[2 + n], outs[-1]


def _split_wait(send_sems, recv_sems, n_sems, srcs, land, after, name):
    n = len(srcs)

    def body(*refs):
        land_ref = refs[n]
        s_sems, r_sems = refs[n + 1], refs[n + 2]
        me = _mesh_pos()
        for k in range(n_sems):
            cp = _whole_wait(land_ref.at[0] if n_sems > 1 else land_ref, s_sems.at[k], r_sems.at[k], me)
            cp.wait_send()
            cp.wait_recv()

    outs = pl.pallas_call(
        body, name=name,
        out_shape=(*[pltpu.HBM(t.shape, t.dtype) for t in srcs], pltpu.HBM(land.shape, land.dtype)),
        in_specs=(HBM_SPEC,) * (n + 1) + (SEM_SPEC, SEM_SPEC, ANY_SPEC),
        out_specs=(HBM_SPEC,) * (n + 1),
        input_output_aliases={i: i for i in range(n + 1)},
        compiler_params=pltpu.CompilerParams(has_side_effects=EFFECT),
    )(*srcs, land, send_sems, recv_sems, after)
    return list(outs[:n]), outs[n]


def _spread_start(x, me_id, after, name):
    land = lax.dynamic_update_slice_in_dim(lax.empty((N_DEV,) + x.shape, x.dtype), x[None], me_id, axis=0)

    def body(x_ref, land_ref, after_ref, send_sems, recv_sems, x_thru, land_thru, token):
        px, py, pc = _mesh_pos()
        me = 4 * px + 2 * py + pc
        for k in range(1, N_DEV):
            qx = 1 - px if k & 4 else px
            qy = 1 - py if k & 2 else py
            qc = 1 - pc if k & 1 else pc
            pltpu.make_async_remote_copy(
                src_ref=x_ref, dst_ref=land_ref.at[me], send_sem=send_sems.at[k - 1], recv_sem=recv_sems.at[k - 1],
                device_id=(qx, qy, qc), device_id_type=pl.DeviceIdType.MESH).start()
        token[...] = jnp.zeros_like(token)

    outs = pl.pallas_call(
        body, name=name,
        out_shape=(pltpu.SemaphoreType.DMA((N_DEV - 1,)), pltpu.SemaphoreType.DMA((N_DEV - 1,)),
                   pltpu.HBM(x.shape, x.dtype), pltpu.HBM(land.shape, land.dtype), jax.ShapeDtypeStruct((8, LANE), F32)),
        in_specs=(HBM_SPEC, HBM_SPEC, ANY_SPEC),
        out_specs=(SEM_SPEC, SEM_SPEC, HBM_SPEC, HBM_SPEC, pl.BlockSpec(memory_space=pltpu.VMEM)),
        input_output_aliases={0: 2, 1: 3},
        compiler_params=pltpu.CompilerParams(has_side_effects=EFFECT),
    )(_hbm(x), _hbm(land), after)
    return outs[0], outs[1], [outs[2]], outs[3], outs[4]


def _pair_sum(srcs, rows_list, land, core, name):
    n = len(rows_list)
    offs = _offsets(rows_list)
    total = sum(rows_list)

    def body(core_ref, *refs):
        src, land_ref, out_ref = refs[:n], refs[n], refs[n + 1]
        for off, rows, src_ref in zip(offs, rows_list, src):
            out_ref[pl.ds(off, rows), :] = (src_ref[...].astype(F32)
                                            + land_ref[pl.ds(off, rows), :].astype(F32)).astype(BF16)

    slot = pl.BlockSpec((None, total, D_MODEL), lambda k, c: (k, 0, 0))
    grid_spec = pltpu.PrefetchScalarGridSpec(
        num_scalar_prefetch=1, grid=(N_CHIPS,),
        in_specs=[pl.BlockSpec((rows, D_MODEL), lambda k, c: (2 * k + c[0], 0)) for rows in rows_list] + [slot],
        out_specs=slot)
    return pl.pallas_call(
        body, name=name, grid_spec=grid_spec,
        out_shape=jax.ShapeDtypeStruct((N_CHIPS, total, D_MODEL), BF16),
        compiler_params=_params(("parallel",)),
    )(core, *srcs, land)


def _chip_exchange_start(sums, chip, after, name):
    own = lax.dynamic_index_in_dim(sums, chip, axis=0, keepdims=True)
    recv = lax.dynamic_update_slice_in_dim(lax.empty(sums.shape, BF16), own, chip, axis=0)

    def body(sums_ref, recv_ref, after_ref, send_sems, recv_sems, sums_thru, recv_thru, token):
        px, py, pc = _mesh_pos()
        for k in range(1, N_CHIPS):
            qx = 1 - px if k & 2 else px
            qy = 1 - py if k & 1 else py
            pltpu.make_async_remote_copy(
                src_ref=sums_ref.at[2 * qx + qy], dst_ref=recv_ref.at[2 * px + py],
                send_sem=send_sems.at[k - 1], recv_sem=recv_sems.at[k - 1],
                device_id=(qx, qy, pc), device_id_type=pl.DeviceIdType.MESH).start()
        token[...] = jnp.zeros_like(token)

    outs = pl.pallas_call(
        body, name=name,
        out_shape=(pltpu.SemaphoreType.DMA((N_CHIPS - 1,)), pltpu.SemaphoreType.DMA((N_CHIPS - 1,)),
                   pltpu.HBM(sums.shape, BF16), pltpu.HBM(recv.shape, BF16), jax.ShapeDtypeStruct((8, LANE), F32)),
        in_specs=(HBM_SPEC, HBM_SPEC, ANY_SPEC),
        out_specs=(SEM_SPEC, SEM_SPEC, HBM_SPEC, HBM_SPEC, pl.BlockSpec(memory_space=pltpu.VMEM)),
        input_output_aliases={0: 2, 1: 3},
        compiler_params=pltpu.CompilerParams(has_side_effects=EFFECT),
    )(_hbm(sums), _hbm(recv), after)
    return outs[0], outs[1], [outs[2]], outs[3], outs[4]


def _sum_slots_into(recv, buf, layer, row_off, name):
    slots, r, n = recv.shape
    tr = _row_tile(math.gcd(r, row_off) if row_off else r, 512)
    first = row_off // tr

    def body(in_ref, buf_ref, out_ref):
        acc = in_ref[0].astype(F32)
        for j in range(1, slots):
            acc = acc + in_ref[j].astype(F32)
        out_ref[...] = acc

    return pl.pallas_call(
        body, name=name, grid=(r // tr,), out_shape=jax.ShapeDtypeStruct(buf.shape, F32),
        in_specs=[pl.BlockSpec((slots, tr, n), lambda i: (0, i, 0)), ANY_SPEC],
        out_specs=pl.BlockSpec((None, tr, n), lambda i: (layer, first + i, 0)),
        input_output_aliases={1: 0},
        compiler_params=_params(("parallel",)),
    )(recv, buf)


def _sum_slots(recv, name, after=None):
    _, r, n = recv.shape
    tr = _row_tile(r, 512)

    def body(in_ref, *refs):
        acc = in_ref[0].astype(F32)
        for j in range(1, N_DEV):
            acc = acc + in_ref[j].astype(F32)
        refs[-1][...] = acc

    grid = (r // tr,)
    in_specs, out_spec = [pl.BlockSpec((N_DEV, tr, n), lambda i: (0, i, 0))], pl.BlockSpec((tr, n), lambda i: (i, 0))
    args = [recv]
    if after is not None:
        in_specs.append(ANY_SPEC)
        args.append(after)
    return pl.pallas_call(
        body, name=name, grid=grid,
        out_shape=jax.ShapeDtypeStruct((r, n), F32),
        in_specs=in_specs, out_specs=out_spec,
        compiler_params=_params(("parallel",)),
    )(*args)


def _row_tile(rows, target):
    if rows <= target:
        return rows
    best = None
    for t in range(16, target + 1, 16):
        if rows % t == 0:
            best = t
    assert best is not None, rows
    return best


_DIMS = {"nn": ((1,), (0,)), "nt": ((1,), (1,)), "tn": ((0,), (0,))}


def _mm(a, b, mode, name, out_dtype=F32, res=None, gate=None, gate_factor=1.0, tm=512, tn=1408, after=None):
    assert (res is None) == (gate is None)
    if mode == "tn":
        kdim, m = a.shape
    else:
        m, kdim = a.shape
    n = b.shape[0] if mode == "nt" else b.shape[1]
    tm, tn = _tile(m, tm), _tile(n, tn)
    a_spec = (pl.BlockSpec((kdim, tm), lambda i, j: (0, i)) if mode == "tn"
              else pl.BlockSpec((tm, kdim), lambda i, j: (i, 0)))
    b_spec = (pl.BlockSpec((tn, kdim), lambda i, j: (j, 0)) if mode == "nt"
              else pl.BlockSpec((kdim, tn), lambda i, j: (0, j)))
    o_spec = pl.BlockSpec((tm, tn), lambda i, j: (i, j))
    dims = (_DIMS[mode], ((), ()))
    has_res = res is not None

    def body(a_ref, b_ref, *refs):
        y = lax.dot_general(a_ref[...].astype(BF16), b_ref[...].astype(BF16), dims,
                            preferred_element_type=F32)
        if has_res:
            res_ref, gate_ref = refs[0], refs[1]
            y_ref, o_ref = refs[-2], refs[-1]
            y_ref[...] = y.astype(BF16)
            o_ref[...] = res_ref[...] + (gate_factor * gate_ref[...]) * y
        else:
            refs[-1][...] = y.astype(out_dtype)

    in_specs, args = [a_spec, b_spec], [a, b]
    if has_res:
        gate_spec, gate_arg = _vec_in(gate, tile=tn)
        in_specs += [o_spec, gate_spec]
        args += [res, gate_arg]
        out_shape = (jax.ShapeDtypeStruct((m, n), BF16), jax.ShapeDtypeStruct((m, n), F32))
        out_specs = (o_spec, o_spec)
    else:
        out_shape, out_specs = jax.ShapeDtypeStruct((m, n), out_dtype), o_spec
    if after is not None:
        in_specs.append(ANY_SPEC)
        args.append(after)
    return pl.pallas_call(
        body, name=name, grid=(m // tm, n // tn), out_shape=out_shape,
        in_specs=in_specs, out_specs=out_specs,
        compiler_params=_params(("parallel", "parallel")),
    )(*args)


def _vec_in(v, tile=None):
    if isinstance(v, tuple):
        table, row = v
        if tile is None:
            return pl.BlockSpec((None, 1, table.shape[-1]), lambda *idx: (row, 0, 0)), table
        return pl.BlockSpec((None, 1, tile), lambda i, j: (row, 0, j)), table
    if tile is None:
        return pl.BlockSpec((1, v.shape[-1]), lambda *idx: (0, 0)), v
    return pl.BlockSpec((1, tile), lambda i, j: (0, j)), v


def _vec_spec(width):
    return pl.BlockSpec((1, width), lambda i: (0, 0))


def _rm_bwd(dh, x, dres, gw, scale, name, below=None):
    s, d = x.shape
    ts = _tile(s, 512)
    factor = None if below is None else below[2]

    def body(dh_ref, x_ref, dres_ref, gw_ref, sc_ref, *refs):
        dx_ref, dsh_ref, dsc_ref, dgw_ref = refs[-6:-2] if below is not None else refs[-4:]

        @pl.when(pl.program_id(0) == 0)
        def _():
            dsh_ref[...] = jnp.zeros_like(dsh_ref)
            dsc_ref[...] = jnp.zeros_like(dsc_ref)
            dgw_ref[...] = jnp.zeros_like(dgw_ref)
            if below is not None:
                refs[-1][...] = jnp.zeros_like(refs[-1])

        xv, dhv, gwv = x_ref[...], dh_ref[...], gw_ref[...]
        r = lax.rsqrt(jnp.mean(xv * xv, axis=-1, keepdims=True) + EPS)
        xn = xv * r
        y = xn * gwv
        dsh_ref[...] += jnp.sum(dhv, axis=0, keepdims=True)
        dsc_ref[...] += jnp.sum(dhv * y, axis=0, keepdims=True)
        dy = dhv * (1 + sc_ref[...])
        dgw_ref[...] += jnp.sum(dy * xn, axis=0, keepdims=True)
        dxn = dy * gwv
        dx = dres_ref[...] + r * (dxn - xn * jnp.mean(dxn * xn, axis=-1, keepdims=True))
        dx_ref[...] = dx
        if below is not None:
            yb_ref, gb_ref, dyb_ref, dgb_ref = refs[0], refs[1], refs[-2], refs[-1]
            dyb_ref[...] = ((factor * gb_ref[...]) * dx).astype(BF16)
            dgb_ref[...] += jnp.sum((factor * dx) * yb_ref[...].astype(F32), axis=0, keepdims=True)

    row = pl.BlockSpec((ts, d), lambda i: (i, 0))
    vec = jax.ShapeDtypeStruct((1, d), F32)
    (gw_spec, gw), (sc_spec, scale) = _vec_in(gw), _vec_in(scale)
    in_specs, args = [row, row, row, gw_spec, sc_spec], [dh, x, dres, gw, scale]
    out_shape = [jax.ShapeDtypeStruct((s, d), F32), vec, vec, vec]
    out_specs = [row, _vec_spec(d), _vec_spec(d), _vec_spec(d)]
    if below is not None:
        gate_spec, gate_arg = _vec_in(below[1])
        in_specs += [row, gate_spec]
        args += [below[0], gate_arg]
        out_shape += [jax.ShapeDtypeStruct((s, d), BF16), vec]
        out_specs += [row, _vec_spec(d)]
    return pl.pallas_call(
        body, name=name, grid=(s // ts,), out_shape=tuple(out_shape),
        in_specs=in_specs, out_specs=tuple(out_specs),
        compiler_params=_params(("arbitrary",)),
    )(*args)


def _gate_bwd(dx, y, gate, factor, name):
    s, d = dx.shape
    ts = _tile(s, 256)

    def body(dx_ref, y_ref, g_ref, dy_ref, dg_ref):
        @pl.when(pl.program_id(0) == 0)
        def _():
            dg_ref[...] = jnp.zeros_like(dg_ref)

        dxv = dx_ref[...]
        dy_ref[...] = ((factor * g_ref[...]) * dxv).astype(BF16)
        dg_ref[...] += jnp.sum((factor * dxv) * y_ref[...].astype(F32), axis=0, keepdims=True)

    row = pl.BlockSpec((ts, d), lambda i: (i, 0))
    return pl.pallas_call(
        body, name=name, grid=(s // ts,),
        out_shape=(jax.ShapeDtypeStruct((s, d), BF16), jax.ShapeDtypeStruct((1, d), F32)),
        in_specs=[row, row, _vec_in(gate)[0]], out_specs=(row, _vec_spec(d)),
        compiler_params=_params(("arbitrary",)),
    )(dx, y, _vec_in(gate)[1])


def _norm_mm(x, gw, shift, scale, w, name, tm=1024):
    s, d = x.shape
    n = w.shape[0]
    tm = _tile(s, tm)

    def body(x_ref, gw_ref, sh_ref, sc_ref, w_ref, h_ref, z_ref):
        xv = x_ref[...]
        r = lax.rsqrt(jnp.mean(xv * xv, axis=-1, keepdims=True) + EPS)
        hb = (((xv * r) * gw_ref[...]) * (1 + sc_ref[...]) + sh_ref[...]).astype(BF16)
        h_ref[...] = hb
        z_ref[...] = lax.dot_general(hb, w_ref[...], (((1,), (1,)), ((), ())), preferred_element_type=F32)

    row = pl.BlockSpec((tm, d), lambda i: (i, 0))
    return pl.pallas_call(
        body, name=name, grid=(s // tm,),
        out_shape=(jax.ShapeDtypeStruct((s, d), BF16), jax.ShapeDtypeStruct((s, n), F32)),
        in_specs=[row, _vec_in(gw)[0], _vec_in(shift)[0], _vec_in(scale)[0], pl.BlockSpec((n, d), lambda i: (0, 0))],
        out_specs=(row, pl.BlockSpec((tm, n), lambda i: (i, 0))),
        compiler_params=_params(("parallel",)),
    )(x, _vec_in(gw)[1], _vec_in(shift)[1], _vec_in(scale)[1], w)


FFN_TM, FFN_TF = 2048, 256


def _ffn_up(x, gw, shift, scale, wg, wu, name, after=None):
    s, d = x.shape
    f = wg.shape[0]
    tm, tf = _tile(s, FFN_TM), _tile(f, FFN_TF)
    nt = (((1,), (1,)), ((), ()))

    def body(x_ref, gw_ref, sh_ref, sc_ref, wg_ref, wu_ref, *refs):
        h_ref, a_ref, b_ref, t_ref = refs[-4:]

        @pl.when(pl.program_id(1) == 0)
        def _():
            xv = x_ref[...]
            r = lax.rsqrt(jnp.mean(xv * xv, axis=-1, keepdims=True) + EPS)
            h_ref[...] = (((xv * r) * gw_ref[...]) * (1 + sc_ref[...]) + sh_ref[...]).astype(BF16)

        hb = h_ref[...]
        av = lax.dot_general(hb, wg_ref[...], nt, preferred_element_type=F32)
        bv = lax.dot_general(hb, wu_ref[...], nt, preferred_element_type=F32)
        a_ref[...] = av.astype(BF16)
        b_ref[...] = bv.astype(BF16)
        t_ref[...] = ((av * jax.nn.sigmoid(av)) * bv).astype(BF16)

    row = pl.BlockSpec((tm, d), lambda i, j: (i, 0))
    wblk = pl.BlockSpec((tf, d), lambda i, j: (j, 0))
    blk = pl.BlockSpec((tm, tf), lambda i, j: (i, j))
    wide = jax.ShapeDtypeStruct((s, f), BF16)
    vec_specs, vec_args = zip(*[_vec_in(v) for v in (gw, shift, scale)])
    in_specs, args = [row, *vec_specs, wblk, wblk], [x, *vec_args, wg, wu]
    if after is not None:
        in_specs.append(ANY_SPEC)
        args.append(after)
    return pl.pallas_call(
        body, name=name, grid=(s // tm, f // tf),
        out_shape=(jax.ShapeDtypeStruct((s, d), BF16), wide, wide, wide),
        in_specs=in_specs, out_specs=(row, blk, blk, blk),
        compiler_params=_params(("parallel", "arbitrary")),
    )(*args)


def _ffn_bwd_cols(dy, h, a, b, t, wd, name, after=None):
    s, d = dy.shape
    f = wd.shape[0]
    tf = _tile(f, FFN_TF)
    nt = (((1,), (1,)), ((), ()))
    tn = (((0,), (0,)), ((), ()))

    def body(dy_ref, h_ref, a_ref, b_ref, t_ref, wd_ref, *refs):
        da_ref, db_ref, gd_ref, gg_ref, gu_ref = refs[-5:]
        dyb, hb = dy_ref[...], h_ref[...]
        dtv = lax.dot_general(dyb, wd_ref[...], nt, preferred_element_type=F32)
        av, bv = a_ref[...].astype(F32), b_ref[...].astype(F32)
        sg = jax.nn.sigmoid(av)
        dbv = (dtv * (av * sg)).astype(BF16)
        dav = ((dtv * bv) * (sg * (1 + av * (1 - sg)))).astype(BF16)
        da_ref[...] = dav
        db_ref[...] = dbv
        gd_ref[...] = lax.dot_general(t_ref[...], dyb, tn, preferred_element_type=F32).astype(BF16)
        gg_ref[...] = lax.dot_general(dav, hb, tn, preferred_element_type=F32).astype(BF16)
        gu_ref[...] = lax.dot_general(dbv, hb, tn, preferred_element_type=F32).astype(BF16)

    whole = pl.BlockSpec((s, d), lambda j: (0, 0))
    col = pl.BlockSpec((s, tf), lambda j: (0, j))
    wblk = pl.BlockSpec((tf, d), lambda j: (j, 0))
    wide, wgrad = jax.ShapeDtypeStruct((s, f), BF16), jax.ShapeDtypeStruct((f, d), BF16)
    in_specs, args = [whole, whole, col, col, col, wblk], [dy, h, a, b, t, wd]
    if after is not None:
        in_specs.append(ANY_SPEC)
        args.append(after)
    return pl.pallas_call(
        body, name=name, grid=(f // tf,), out_shape=(wide, wide, wgrad, wgrad, wgrad),
        in_specs=in_specs, out_specs=(col, col, wblk, wblk, wblk),
        compiler_params=_params(("parallel",)),
    )(*args)


def _mm_pair(a1, b1, a2, b2, name, tm=1024, tn=512, after=None):
    m, kdim = a1.shape
    n = b1.shape[1]
    tm, tn = _tile(m, tm), _tile(n, tn)

    def body(a1_ref, b1_ref, a2_ref, b2_ref, *refs):
        refs[-1][...] = (jnp.dot(a1_ref[...], b1_ref[...], preferred_element_type=F32)
                         + jnp.dot(a2_ref[...], b2_ref[...], preferred_element_type=F32))

    a_spec = pl.BlockSpec((tm, kdim), lambda i, j: (i, 0))
    b_spec = pl.BlockSpec((kdim, tn), lambda i, j: (0, j))
    in_specs, args = [a_spec, b_spec, a_spec, b_spec], [a1, b1, a2, b2]
    if after is not None:
        in_specs.append(ANY_SPEC)
        args.append(after)
    return pl.pallas_call(
        body, name=name, grid=(m // tm, n // tn), out_shape=jax.ShapeDtypeStruct((m, n), F32),
        in_specs=in_specs, out_specs=pl.BlockSpec((tm, tn), lambda i, j: (i, j)),
        compiler_params=_params(("parallel", "parallel")),
    )(*args)


def _pool_counts(s):
    return (lax.broadcasted_iota(jnp.int32, (s, POOL_GC), 0))


def _pool_fwd(z, pool_w, pool_scale, name):
    s = z.shape[0]

    def body(u_ref, w_ref, sc_ref, y_ref, diff_ref):
        t = lax.broadcasted_iota(jnp.int32, (s, POOL_GC), 0)
        for g, win in enumerate(POOL_WINDOWS):
            cols = slice(g * POOL_GC, (g + 1) * POOL_GC)
            u = u_ref[:, cols]
            acc, step = u, 1
            while step < win:
                acc = acc + jnp.where(t >= step, pltpu.roll(acc, step, 0), 0.0)
                step *= 2
            cnt = jnp.minimum(t + 1, win).astype(F32)
            diff = acc / cnt - u
            diff_ref[:, cols] = diff
            ypre = jnp.dot(diff.astype(BF16), w_ref[g].astype(BF16), preferred_element_type=F32)
            y_ref[:, cols] = (ypre * sc_ref[:, cols]).astype(BF16)

    return pl.pallas_call(
        body, name=name, grid=(1,),
        out_shape=(jax.ShapeDtypeStruct((s, POOL_WIDTH), BF16), jax.ShapeDtypeStruct((s, POOL_WIDTH), F32)),
        in_specs=[pl.BlockSpec((s, POOL_WIDTH), lambda i: (0, 0)),
                  pl.BlockSpec(pool_w.shape, lambda i: (0, 0, 0)),
                  pl.BlockSpec((1, POOL_WIDTH), lambda i: (0, 0))],
        out_specs=(pl.BlockSpec((s, POOL_WIDTH), lambda i: (0, 0)),
                   pl.BlockSpec((s, POOL_WIDTH), lambda i: (0, 0))),
        compiler_params=_params(("arbitrary",)),
    )(z, pool_w, pool_scale)


def _pool_bwd(dycat, diff, pool_w, pool_scale, name):
    s = diff.shape[0]

    def body(dy_ref, diff_ref, w_ref, sc_ref, du_ref, dw_ref, dsc_ref):
        t = lax.broadcasted_iota(jnp.int32, (s, POOL_GC), 0)
        for g, win in enumerate(POOL_WINDOWS):
            cols = slice(g * POOL_GC, (g + 1) * POOL_GC)
            dy, dfb, wb = dy_ref[:, cols], diff_ref[:, cols].astype(BF16), w_ref[g].astype(BF16)
            ypre = jnp.dot(dfb, wb, preferred_element_type=F32)
            dsc_ref[:, cols] = jnp.sum(dy * ypre, axis=0, keepdims=True)
            dypre = (dy * sc_ref[:, cols]).astype(BF16)
            ddiff = lax.dot_general(dypre, wb, (((1,), (1,)), ((), ())), preferred_element_type=F32)
            dw_ref[g] = lax.dot_general(dfb, dypre, (((0,), (0,)), ((), ())), preferred_element_type=F32)
            cnt = jnp.minimum(t + 1, win).astype(F32)
            acc, step = ddiff / cnt, 1
            while step < win:
                acc = acc + jnp.where(t < s - step, pltpu.roll(acc, s - step, 0), 0.0)
                step *= 2
            du_ref[:, cols] = acc - ddiff

    full = pl.BlockSpec((s, POOL_WIDTH), lambda i: (0, 0))
    return pl.pallas_call(
        body, name=name, grid=(1,),
        out_shape=(jax.ShapeDtypeStruct((s, POOL_WIDTH), F32),
                   jax.ShapeDtypeStruct(pool_w.shape, F32),
                   jax.ShapeDtypeStruct((1, POOL_WIDTH), F32)),
        in_specs=[full, full, pl.BlockSpec(pool_w.shape, lambda i: (0, 0, 0)),
                  pl.BlockSpec((1, POOL_WIDTH), lambda i: (0, 0))],
        out_specs=(full, pl.BlockSpec(pool_w.shape, lambda i: (0, 0, 0)),
                   pl.BlockSpec((1, POOL_WIDTH), lambda i: (0, 0))),
        compiler_params=_params(("arbitrary",)),
    )(dycat, diff, pool_w, pool_scale)


def _rope_tables(positions, name):
    s = positions.shape[0]
    ts = _tile(s, 512)
    freq = 1.0 / (ROPE_THETA ** (np.arange(0, QK_ROPE, 2, dtype=np.float32) / QK_ROPE))
    table = np.zeros((1, LANE), np.float32)
    table[0, :QK_ROPE // 2] = freq
    table[0, QK_ROPE // 2:QK_ROPE] = freq

    def body(pos_ref, f_ref, cos_ref, sin_ref):
        ang = pos_ref[...].astype(F32) * f_ref[...]
        cos_ref[...] = jnp.cos(ang)
        sin_ref[...] = jnp.sin(ang)

    out = jax.ShapeDtypeStruct((s, LANE), F32)
    blk = pl.BlockSpec((ts, LANE), lambda i: (i, 0))
    return pl.pallas_call(
        body, name=name, grid=(s // ts,), out_shape=(out, out),
        in_specs=[pl.BlockSpec((ts, 1), lambda i: (i, 0)), _vec_spec(LANE)], out_specs=(blk, blk),
        compiler_params=_params(("parallel",)),
    )(positions, jnp.asarray(table))


def _lane_mod64_low(shape):
    return (lax.broadcasted_iota(jnp.int32, shape, 1) % QK_ROPE) < (QK_ROPE // 2)


def _rope(x, cos, sin):
    rot = jnp.where(_lane_mod64_low(x.shape), -pltpu.roll(x, LANE - 32, 1), pltpu.roll(x, 32, 1))
    return x * cos + rot * sin


def _rope_t(dy, cos, sin):
    w = dy * sin
    rot_t = jnp.where(_lane_mod64_low(dy.shape), pltpu.roll(w, LANE - 32, 1), -pltpu.roll(w, 32, 1))
    return dy * cos + rot_t


def _plain_rms(x, g):
    r = lax.rsqrt(jnp.mean(x * x, axis=-1, keepdims=True) + EPS)
    return (x * r) * g, x * r, r


O_Q, O_KV, O_KR = POOL_WIDTH, POOL_WIDTH + Q_LORA, POOL_WIDTH + Q_LORA + KV_LORA


def _qkv_fwd(z, qn, kvn, wq, wkv, cos, sin, name):
    s = z.shape[0]
    ts = _tile(s, 512)

    def body(z_ref, qn_ref, kvn_ref, wq_ref, wkv_ref, cos_ref, sin_ref, q_ref, k_ref, v_ref, cqn_ref, ckvn_ref):
        cosv, sinv = cos_ref[...], sin_ref[...]
        cqn = _plain_rms(z_ref[:, O_Q:O_KV], qn_ref[...])[0].astype(BF16)
        ckvn = _plain_rms(z_ref[:, O_KV:O_KR], kvn_ref[...])[0].astype(BF16)
        cqn_ref[...] = cqn
        ckvn_ref[...] = ckvn
        nt = (((1,), (1,)), ((), ()))
        q = lax.dot_general(cqn, wq_ref[...], nt, preferred_element_type=F32)
        kv = lax.dot_general(ckvn, wkv_ref[...], nt, preferred_element_type=F32)
        kr = _rope(z_ref[:, O_KR:IN_PAD], cosv, sinv).astype(BF16)
        for h in range(N_HEADS):
            o = h * HEAD_PAD
            q_ref[:, o:o + QK_NOPE] = q[:, o:o + QK_NOPE].astype(BF16)
            q_ref[:, o + QK_NOPE:o + HEAD_PAD] = _rope(q[:, o + QK_NOPE:o + HEAD_PAD], cosv, sinv).astype(BF16)
            k_ref[:, o:o + QK_NOPE] = kv[:, o:o + QK_NOPE].astype(BF16)
            k_ref[:, o + QK_NOPE:o + HEAD_PAD] = kr
            v_ref[:, h * V_HEAD:(h + 1) * V_HEAD] = kv[:, o + QK_NOPE:o + HEAD_PAD].astype(BF16)

    def row(w):
        return pl.BlockSpec((ts, w), lambda i: (i, 0))

    def whole(arr):
        return pl.BlockSpec(arr.shape, lambda i: (0, 0))

    hp = N_HEADS * HEAD_PAD
    return pl.pallas_call(
        body, name=name, grid=(s // ts,),
        out_shape=(jax.ShapeDtypeStruct((s, hp), BF16), jax.ShapeDtypeStruct((s, hp), BF16),
                   jax.ShapeDtypeStruct((s, N_HEADS * V_HEAD), BF16),
                   jax.ShapeDtypeStruct((s, Q_LORA), BF16), jax.ShapeDtypeStruct((s, KV_LORA), BF16)),
        in_specs=[row(IN_PAD), whole(qn), whole(kvn), whole(wq), whole(wkv), row(LANE), row(LANE)],
        out_specs=(row(hp), row(hp), row(N_HEADS * V_HEAD), row(Q_LORA), row(KV_LORA)),
        compiler_params=_params(("parallel",)),
    )(z, qn, kvn, wq, wkv, cos, sin)


def _qkv_bwd(dq, dk, dv, du, z, qn, kvn, wq, wkv, cos, sin, name):
    s = z.shape[0]
    ts = _tile(s, 512)

    def norm_bwd(x, g, dy):
        _, xn, r = _plain_rms(x, g)
        dxn = dy * g
        return r * (dxn - xn * jnp.mean(dxn * xn, axis=-1, keepdims=True)), jnp.sum(dy * xn, axis=0, keepdims=True)

    def body(dq_ref, dk_ref, dv_ref, du_ref, z_ref, qn_ref, kvn_ref, wq_ref, wkv_ref, cos_ref, sin_ref,
             dz_ref, dqb_ref, dkvb_ref, dqn_ref, dkvn_ref):
        @pl.when(pl.program_id(0) == 0)
        def _():
            dqn_ref[...] = jnp.zeros_like(dqn_ref)
            dkvn_ref[...] = jnp.zeros_like(dkvn_ref)

        cosv, sinv = cos_ref[...], sin_ref[...]
        dkr = jnp.zeros((ts, LANE), F32)
        for h in range(N_HEADS):
            o = h * HEAD_PAD
            dqb_ref[:, o:o + QK_NOPE] = dq_ref[:, o:o + QK_NOPE].astype(BF16)
            dqb_ref[:, o + QK_NOPE:o + HEAD_PAD] = _rope_t(dq_ref[:, o + QK_NOPE:o + HEAD_PAD], cosv, sinv).astype(BF16)
            dkvb_ref[:, o:o + QK_NOPE] = dk_ref[:, o:o + QK_NOPE].astype(BF16)
            dkvb_ref[:, o + QK_NOPE:o + HEAD_PAD] = dv_ref[:, h * V_HEAD:(h + 1) * V_HEAD].astype(BF16)
            dkr = dkr + dk_ref[:, o + QK_NOPE:o + HEAD_PAD]
        dcqn = jnp.dot(dqb_ref[...], wq_ref[...], preferred_element_type=F32)
        dckvn = jnp.dot(dkvb_ref[...], wkv_ref[...], preferred_element_type=F32)
        dcq, dqn = norm_bwd(z_ref[:, O_Q:O_KV], qn_ref[...], dcqn)
        dckv, dkvn = norm_bwd(z_ref[:, O_KV:O_KR], kvn_ref[...], dckvn)
        dqn_ref[...] += dqn
        dkvn_ref[...] += dkvn
        dz_ref[:, 0:O_Q] = du_ref[...].astype(BF16)
        dz_ref[:, O_Q:O_KV] = dcq.astype(BF16)
        dz_ref[:, O_KV:O_KR] = dckv.astype(BF16)
        dz_ref[:, O_KR:IN_PAD] = _rope_t(dkr, cosv, sinv).astype(BF16)

    def row(w):
        return pl.BlockSpec((ts, w), lambda i: (i, 0))

    def whole(arr):
        return pl.BlockSpec(arr.shape, lambda i: (0, 0))

    hp = N_HEADS * HEAD_PAD
    return pl.pallas_call(
        body, name=name, grid=(s // ts,),
        out_shape=(jax.ShapeDtypeStruct((s, IN_PAD), BF16), jax.ShapeDtypeStruct((s, hp), BF16),
                   jax.ShapeDtypeStruct((s, hp), BF16),
                   jax.ShapeDtypeStruct((1, Q_LORA), F32), jax.ShapeDtypeStruct((1, KV_LORA), F32)),
        in_specs=[row(hp), row(hp), row(N_HEADS * V_HEAD), row(POOL_WIDTH), row(IN_PAD),
                  whole(qn), whole(kvn), whole(wq), whole(wkv), row(LANE), row(LANE)],
        out_specs=(row(IN_PAD), row(hp), row(hp), whole(qn), whole(kvn)),
        compiler_params=_params(("arbitrary",)),
    )(dq, dk, dv, du, z, qn, kvn, wq, wkv, cos, sin)


def _causal_scores(q, k, i, tq, klen):
    sc = lax.dot_general(q, k, (((1,), (1,)), ((), ())), preferred_element_type=F32) * SOFTMAX_SCALE
    qpos = i * tq + lax.broadcasted_iota(jnp.int32, (tq, klen), 0)
    kpos = lax.broadcasted_iota(jnp.int32, (tq, klen), 1)
    return jnp.where(qpos >= kpos, sc, -jnp.inf)


ATTN_TQ = 512
ATTN_SEGMENTS = 4


def _by_key_prefix(i, nq, tq, compute):
    nseg = min(ATTN_SEGMENTS, nq)
    per = nq // nseg
    for r in range(nseg):
        pl.when(i // per == r)(lambda r=r: compute((r + 1) * per * tq))


def _attn_fwd(q, k, v, name):
    s = q.shape[0]
    tq = _tile(s, ATTN_TQ)
    nq = s // tq

    def body(q_ref, k_ref, v_ref, o_ref, lse_ref):
        i = pl.program_id(1)

        def compute(klen):
            sc = _causal_scores(q_ref[...], k_ref[0:klen, :], i, tq, klen)
            mx = jnp.max(sc, axis=-1, keepdims=True)
            p = jnp.exp(sc - mx)
            den = jnp.sum(p, axis=-1, keepdims=True)
            o_ref[...] = jnp.dot((p / den).astype(BF16), v_ref[0:klen, :], preferred_element_type=F32)
            lse_ref[...] = mx + jnp.log(den)

        _by_key_prefix(i, nq, tq, compute)

    return pl.pallas_call(
        body, name=name, grid=(N_HEADS, s // tq),
        out_shape=(jax.ShapeDtypeStruct((s, N_HEADS * V_HEAD), F32), jax.ShapeDtypeStruct((N_HEADS, s, 1), F32)),
        in_specs=[pl.BlockSpec((tq, HEAD_PAD), lambda h, i: (i, h)),
                  pl.BlockSpec((s, HEAD_PAD), lambda h, i: (0, h)),
                  pl.BlockSpec((s, V_HEAD), lambda h, i: (0, h))],
        out_specs=(pl.BlockSpec((tq, V_HEAD), lambda h, i: (i, h)),
                   pl.BlockSpec((None, tq, 1), lambda h, i: (h, i, 0))),
        compiler_params=_params(("parallel", "parallel")),
    )(q, k, v)


def _attn_bwd(q, k, v, lse, dycat, name):
    s = q.shape[0]
    tq = _tile(s, ATTN_TQ)
    nq = s // tq
    tn_dims = (((0,), (0,)), ((), ()))

    def body(q_ref, k_ref, v_ref, lse_ref, do_ref, dq_ref, dk_ref, dv_ref):
        i = pl.program_id(1)

        @pl.when(i == 0)
        def _():
            dk_ref[...] = jnp.zeros_like(dk_ref)
            dv_ref[...] = jnp.zeros_like(dv_ref)

        def compute(klen):
            qv, kv_, dob = q_ref[...], k_ref[0:klen, :], do_ref[...].astype(BF16)
            sc = _causal_scores(qv, kv_, i, tq, klen)
            p = jnp.exp(sc - lse_ref[...])
            dp = lax.dot_general(dob, v_ref[0:klen, :], (((1,), (1,)), ((), ())), preferred_element_type=F32)
            ds = (p * (dp - jnp.sum(dp * p, axis=-1, keepdims=True)) * SOFTMAX_SCALE).astype(BF16)
            dq_ref[...] = jnp.dot(ds, kv_, preferred_element_type=F32)
            dk_ref[0:klen, :] += lax.dot_general(ds, qv, tn_dims, preferred_element_type=F32)
            dv_ref[0:klen, :] += lax.dot_general(p.astype(BF16), dob, tn_dims, preferred_element_type=F32)

        _by_key_prefix(i, nq, tq, compute)

    n_pool_blocks = POOL_WIDTH // V_HEAD
    return pl.pallas_call(
        body, name=name, grid=(N_HEADS, s // tq),
        out_shape=(jax.ShapeDtypeStruct((s, N_HEADS * HEAD_PAD), F32),
                   jax.ShapeDtypeStruct((s, N_HEADS * HEAD_PAD), F32),
                   jax.ShapeDtypeStruct((s, N_HEADS * V_HEAD), F32)),
        in_specs=[pl.BlockSpec((tq, HEAD_PAD), lambda h, i: (i, h)),
                  pl.BlockSpec((s, HEAD_PAD), lambda h, i: (0, h)),
                  pl.BlockSpec((s, V_HEAD), lambda h, i: (0, h)),
                  pl.BlockSpec((None, tq, 1), lambda h, i: (h, i, 0)),
                  pl.BlockSpec((tq, V_HEAD), lambda h, i: (i, n_pool_blocks + h))],
        out_specs=(pl.BlockSpec((tq, HEAD_PAD), lambda h, i: (i, h)),
                   pl.BlockSpec((s, HEAD_PAD), lambda h, i: (0, h)),
                   pl.BlockSpec((s, V_HEAD), lambda h, i: (0, h))),
        compiler_params=_params(("parallel", "arbitrary")),
    )(q, k, v, lse, dycat)


def _loss_head(x, gw, target, name):
    s, d = x.shape
    ts = _tile(s, 256)

    def body(x_ref, gw_ref, tgt_ref, loss_ref, dx_ref, dgw_ref):
        @pl.when(pl.program_id(0) == 0)
        def _():
            loss_ref[...] = jnp.zeros_like(loss_ref)
            dgw_ref[...] = jnp.zeros_like(dgw_ref)

        xv, gwv = x_ref[...], gw_ref[...]
        r = lax.rsqrt(jnp.mean(xv * xv, axis=-1, keepdims=True) + EPS)
        xn = xv * r
        err = xn * gwv - tgt_ref[...]
        loss_ref[...] += 0.5 * jnp.sum(jnp.mean(err * err, axis=-1, keepdims=True))
        dy = err / d
        dgw_ref[...] += jnp.sum(dy * xn, axis=0, keepdims=True)
        dxn = dy * gwv
        dx_ref[...] = r * (dxn - xn * jnp.mean(dxn * xn, axis=-1, keepdims=True))

    row = pl.BlockSpec((ts, d), lambda i: (i, 0))
    return pl.pallas_call(
        body, name=name, grid=(s // ts,),
        out_shape=(jax.ShapeDtypeStruct((8, LANE), F32), jax.ShapeDtypeStruct((s, d), F32),
                   jax.ShapeDtypeStruct((1, d), F32)),
        in_specs=[row, _vec_spec(d), row],
        out_specs=(pl.BlockSpec((8, LANE), lambda i: (0, 0)), row, _vec_spec(d)),
        compiler_params=_params(("arbitrary",)),
    )(x, gw, target)


def _ada_mod(c_all, ada_w, ada_b, name):
    nl, d, cols = ada_w.shape

    def body(c_ref, w_ref, b_ref, o_ref):
        cv = c_ref[...]
        act = (cv * jax.nn.sigmoid(cv)).astype(BF16)
        o_ref[...] = jnp.dot(act, w_ref[...].astype(BF16), preferred_element_type=F32) + b_ref[...]

    return pl.pallas_call(
        body, name=name, grid=(nl,), out_shape=jax.ShapeDtypeStruct((nl, N_DEV, cols), F32),
        in_specs=[pl.BlockSpec((N_DEV, d), lambda l: (0, 0)),
                  pl.BlockSpec((None, d, cols), lambda l: (l, 0, 0)),
                  pl.BlockSpec((None, 1, cols), lambda l: (l, 0, 0))],
        out_specs=pl.BlockSpec((None, N_DEV, cols), lambda l: (l, 0, 0)),
        compiler_params=_params(("parallel",)),
    )(c_all, ada_w, ada_b)


def _ada_grad(c_pad, dmod_pad, name):
    nl, kpad, cols = dmod_pad.shape
    d = c_pad.shape[1]

    def body(c_ref, dm_ref, o_ref):
        cv = c_ref[...]
        act = (cv * jax.nn.sigmoid(cv)).astype(BF16)
        o_ref[...] = lax.dot_general(act, dm_ref[...].astype(BF16), (((0,), (0,)), ((), ())),
                                     preferred_element_type=F32)

    return pl.pallas_call(
        body, name=name, grid=(nl,), out_shape=jax.ShapeDtypeStruct((nl, d, cols), F32),
        in_specs=[pl.BlockSpec((kpad, d), lambda l: (0, 0)),
                  pl.BlockSpec((None, kpad, cols), lambda l: (l, 0, 0))],
        out_specs=pl.BlockSpec((None, d, cols), lambda l: (l, 0, 0)),
        compiler_params=_params(("parallel",)),
    )(c_pad, dmod_pad)


def _adamw_math(w, g, m, v):
    nm = ADAM_B1 * m + (1.0 - ADAM_B1) * g
    nv = ADAM_B2 * v + (1.0 - ADAM_B2) * (g * g)
    m_hat = nm / (1.0 - ADAM_B1 ** ADAM_STEP)
    v_hat = nv / (1.0 - ADAM_B2 ** ADAM_STEP)
    return -ADAM_LR * (m_hat / (jnp.sqrt(v_hat) + ADAM_EPS) + ADAM_WD * w), nm, nv


def _adamw_rows(w3, gbuf, row_off, m3, v3, name):
    nl, r, d = w3.shape
    tr = _row_tile(math.gcd(r, row_off) if row_off else r, 352)
    first = row_off // tr

    def body(w_ref, g_ref, m_ref, v_ref, go_ref, d_ref, nm_ref, nv_ref):
        gv = g_ref[...]
        go_ref[...] = gv
        d_ref[...], nm_ref[...], nv_ref[...] = _adamw_math(w_ref[...], gv, m_ref[...], v_ref[...])

    blk = pl.BlockSpec((None, tr, d), lambda l, i: (l, i, 0))
    gblk = pl.BlockSpec((None, tr, d), lambda l, i: (l, first + i, 0))
    out = jax.ShapeDtypeStruct((nl, r, d), F32)
    return pl.pallas_call(
        body, name=name, grid=(nl, r // tr), out_shape=(out, out, out, out),
        in_specs=[blk, gblk, blk, blk], out_specs=(blk, blk, blk, blk),
        compiler_params=_params(("parallel", "parallel")),
    )(w3, gbuf, m3, v3)


def _adamw(w, g, m, v, name):
    rows, cols = w.shape
    tr = _row_tile(rows, 512)

    def body(w_ref, g_ref, m_ref, v_ref, d_ref, nm_ref, nv_ref):
        d_ref[...], nm_ref[...], nv_ref[...] = _adamw_math(w_ref[...], g_ref[...], m_ref[...], v_ref[...])

    blk = pl.BlockSpec((tr, cols), lambda i: (i, 0))
    out = jax.ShapeDtypeStruct((rows, cols), F32)
    return pl.pallas_call(
        body, name=name, grid=(rows // tr,), out_shape=(out, out, out),
        in_specs=[blk, blk, blk, blk], out_specs=(blk, blk, blk),
        compiler_params=_params(("parallel",)),
    )(w, g, m, v)


def _adamw_nd(w, g, m, v, name):
    shape = w.shape
    flat = (lambda t: t.reshape(1, -1)) if w.ndim == 1 else (lambda t: t.reshape(-1, shape[-1]))
    return tuple(t.reshape(shape) for t in _adamw(flat(w), flat(g), flat(m), flat(v), name))


def _pad_rows(t, rows):
    return jnp.pad(t, ((0, rows - t.shape[0]), (0, 0)))


def _pack_shard_layer(l, wts):
    def tr(name):
        return wts[name][l].astype(BF16).T

    parts = [tr("ffn1_w_gate"), tr("ffn1_w_up"), wts["ffn1_w_down"][l].astype(BF16),
             tr("ffn2_w_gate"), tr("ffn2_w_up"), wts["ffn2_w_down"][l].astype(BF16),
             wts["w_out"][l].astype(BF16),
             tr("w_kv_b").reshape(KV_SH_ROWS, D_MODEL),
             _pad_rows(tr("w_in"), 160),
             _pad_rows(tr("w_q_b").reshape(Q_SH_ROWS, D_MODEL), Q_PAD_ROWS)]
    return jnp.concatenate(parts, axis=0)


def _full_weights(lands):
    w = dict(zip(("g1", "u1", "d1", "g2", "u2", "d2", "out"), lands))
    small = lands[-1].reshape(N_DEV, SMALL_ROWS, D_MODEL)
    o_in, o_q = OFF_IN - OFF_KV, OFF_Q - OFF_KV
    w["kv"] = small[:, :KV_SH_ROWS].reshape(N_HEADS * HEAD_PAD, KV_LORA)
    w["in"] = _pad_rows(small[:, o_in:o_in + IN_SH].reshape(IN_COLS, D_MODEL), IN_PAD)
    wq = small[:, o_q:o_q + Q_SH_ROWS].reshape(N_HEADS, QK_HEAD, Q_LORA)
    w["q"] = jnp.pad(wq, ((0, 0), (0, HEAD_PAD - QK_HEAD), (0, 0))).reshape(N_HEADS * HEAD_PAD, Q_LORA)
    return w


def _grad_sources_b(gr):
    gq = gr["q"].reshape(N_HEADS, HEAD_PAD, Q_LORA)[:, :QK_HEAD].reshape(N_DEV, Q_SH_ROWS, D_MODEL)
    small = jnp.concatenate([
        gr["kv"].reshape(N_DEV, KV_SH_ROWS, D_MODEL),
        jnp.pad(gr["in"][:IN_COLS].reshape(N_DEV, IN_SH, D_MODEL), ((0, 0), (0, 160 - IN_SH), (0, 0))),
        jnp.pad(gq, ((0, 0), (0, Q_PAD_ROWS - Q_SH_ROWS), (0, 0)))], axis=1)
    return [gr["g2"], gr["u2"], gr["d2"], gr["out"], small.reshape(N_DEV * SMALL_ROWS, D_MODEL)]


def _pack_bf16_pairs(t):
    rows, d = t.shape
    return lax.bitcast_convert_type(t.astype(BF16).reshape(rows // 2, 2, d).transpose(0, 2, 1), F32)


def _unpack_bf16_pairs(p):
    pairs = jnp.swapaxes(lax.bitcast_convert_type(p, BF16), -1, -2)
    return pairs.reshape(p.shape[:-2] + (2 * p.shape[-2], p.shape[-1]))


def _small_layout(nl):
    names = [("dmod", nl * N_MOD), ("ffn1_norm", nl), ("mix_norm", nl), ("ffn2_norm", nl), ("q_a_norm", nl),
             ("kv_a_norm", nl), ("pool_scale", nl), ("final_norm", 1), ("loss", 1),
             ("pool_w", nl * 4 * POOL_GC * POOL_GC // D_MODEL // 2)]
    off, table = 0, {}
    for name, n in names:
        table[name] = (off, n)
        off += -(-n // 8) * 8
    return table, off


def _to_rows(t, width=D_MODEL):
    n, w = t.shape
    return jnp.pad(t, ((0, -(-n // 8) * 8 - n), (0, width - w)))


def kernel(x, c, positions, ada_w, ada_b, ffn1_norm, ffn1_w_gate, ffn1_w_up, ffn1_w_down, mix_norm, w_in, pool_w, pool_scale, q_a_norm, w_q_b, kv_a_norm, w_kv_b, w_out, ffn2_norm, ffn2_w_gate, ffn2_w_up, ffn2_w_down, final_norm, loss_target, m_ada_w, m_ada_b, m_ffn1_norm, m_ffn1_w_gate, m_ffn1_w_up, m_ffn1_w_down, m_mix_norm, m_w_in, m_pool_w, m_pool_scale, m_q_a_norm, m_w_q_b, m_kv_a_norm, m_w_kv_b, m_w_out, m_ffn2_norm, m_ffn2_w_gate, m_ffn2_w_up, m_ffn2_w_down, m_final_norm, v_ada_w, v_ada_b, v_ffn1_norm, v_ffn1_w_gate, v_ffn1_w_up, v_ffn1_w_down, v_mix_norm, v_w_in, v_pool_w, v_pool_scale, v_q_a_norm, v_w_q_b, v_kv_a_norm, v_w_kv_b, v_w_out, v_ffn2_norm, v_ffn2_w_gate, v_ffn2_w_up, v_ffn2_w_down, v_final_norm):
    wts = dict(ada_w=ada_w, ada_b=ada_b, ffn1_norm=ffn1_norm, ffn1_w_gate=ffn1_w_gate, ffn1_w_up=ffn1_w_up,
               ffn1_w_down=ffn1_w_down, mix_norm=mix_norm, w_in=w_in, pool_w=pool_w, pool_scale=pool_scale,
               q_a_norm=q_a_norm, w_q_b=w_q_b, kv_a_norm=kv_a_norm, w_kv_b=w_kv_b, w_out=w_out,
               ffn2_norm=ffn2_norm, ffn2_w_gate=ffn2_w_gate, ffn2_w_up=ffn2_w_up, ffn2_w_down=ffn2_w_down,
               final_norm=final_norm)
    mom_m = dict(ada_w=m_ada_w, ada_b=m_ada_b, ffn1_norm=m_ffn1_norm, ffn1_w_gate=m_ffn1_w_gate,
                 ffn1_w_up=m_ffn1_w_up, ffn1_w_down=m_ffn1_w_down, mix_norm=m_mix_norm, w_in=m_w_in,
                 pool_w=m_pool_w, pool_scale=m_pool_scale, q_a_norm=m_q_a_norm, w_q_b=m_w_q_b,
                 kv_a_norm=m_kv_a_norm, w_kv_b=m_w_kv_b, w_out=m_w_out, ffn2_norm=m_ffn2_norm,
                 ffn2_w_gate=m_ffn2_w_gate, ffn2_w_up=m_ffn2_w_up, ffn2_w_down=m_ffn2_w_down,
                 final_norm=m_final_norm)
    mom_v = dict(ada_w=v_ada_w, ada_b=v_ada_b, ffn1_norm=v_ffn1_norm, ffn1_w_gate=v_ffn1_w_gate,
                 ffn1_w_up=v_ffn1_w_up, ffn1_w_down=v_ffn1_w_down, mix_norm=v_mix_norm, w_in=v_w_in,
                 pool_w=v_pool_w, pool_scale=v_pool_scale, q_a_norm=v_q_a_norm, w_q_b=v_w_q_b,
                 kv_a_norm=v_kv_a_norm, w_kv_b=v_w_kv_b, w_out=v_w_out, ffn2_norm=v_ffn2_norm,
                 ffn2_w_gate=v_ffn2_w_gate, ffn2_w_up=v_ffn2_w_up, ffn2_w_down=v_ffn2_w_down,
                 final_norm=v_final_norm)
    order = list(wts)
    nl = ada_w.shape[0]
    seq = x.shape[1]
    me = 4 * lax.axis_index("x") + 2 * lax.axis_index("y") + lax.axis_index("c")
    ada_cols = ada_w.shape[2]

    def after_token(t, token):
        return t + token[0:1, 0:1].astype(t.dtype)

    packs = [_pack_shard_layer(l, wts) for l in range(nl)]

    c_all = _all_gather(jnp.broadcast_to(c, (8, D_MODEL)), "gather_c")[::8]

    ada_b_mine = lax.dynamic_slice_in_dim(ada_b, me * ada_cols, ada_cols, axis=1).reshape(nl, 1, ada_cols)
    mod_part = _ada_mod(c_all, ada_w, ada_b_mine, "ada_mod")
    mod_all = _all_gather(mod_part.reshape(nl * N_DEV, ada_cols), "gather_mod")
    mod_all = mod_all.reshape(N_DEV, nl, N_DEV, ada_cols)
    mod = lax.dynamic_index_in_dim(mod_all, me, axis=2, keepdims=False)
    mod = mod.transpose(1, 0, 2).reshape(nl * N_MOD, 1, D_MODEL)
    norm_tables = {name: wts[name].reshape(nl, 1, D_MODEL) for name in ("ffn1_norm", "mix_norm", "ffn2_norm")}

    def modrow(l, k):
        return mod, l * N_MOD + k

    def normrow(name, l):
        return norm_tables[name], l

    flight_a = _gather_start(packs[0][:SPLIT_AB], ROWS_A, mod, "gather_start_0a")
    flight_b = _gather_start(packs[0][SPLIT_AB:], ROWS_B, flight_a[4], "gather_start_0b")
    last_start = flight_b[4]
    if nl > 1:
        in_flight = _gather_start(packs[1], ROWS_ALL, last_start, "gather_start_1")
        last_start = in_flight[4]

    cos, sin = _rope_tables(after_token(positions.reshape(seq, 1), last_start), "rope_tables")

    def vec(t):
        return t.reshape(1, -1)

    def landed(flight, rows_list, after, tag):
        send_sems, recv_sems, pk, lands, _ = flight
        pk, lands = _gather_wait(send_sems, recv_sems, pk, lands, after, f"gather_wait_{tag}")
        return _gather_finish(pk, rows_list, lands, "gather_finish")

    xs = x.reshape(seq, D_MODEL)
    saved = []
    for l in range(nl):
        norm1, up_after = normrow("ffn1_norm", l), None
        if l == 0:
            lands = landed(flight_a, ROWS_A, cos, "0a")
        elif l + 1 < nl:
            in_flight = _gather_start(packs[l + 1], ROWS_ALL, lands[0], f"gather_start_{l + 1}")
            up_after = in_flight[4]
        sv = {}

        def ffn_fwd(xin, norm, k0, wg, wu, wd, tag, after=None):
            h, a, b, t = _ffn_up(xin, norm, modrow(l, k0), modrow(l, k0 + 1), wg, wu, "ffn_up", after=after)
            y, xout = _mm(t, wd, "nn", "ffn_down", res=xin, gate=modrow(l, k0 + 2), gate_factor=0.5)
            sv[tag] = dict(x=xin, h=h, a=a, b=b, t=t, y=y)
            return xout

        xs = ffn_fwd(xs, norm1, 0, lands[0], lands[1], lands[2], "f1", up_after)
        if l == 0:
            lands = lands + landed(flight_b, ROWS_B, xs, "0b")
        w = _full_weights(lands)
        sv["w"] = w

        h2, z = _norm_mm(xs, normrow("mix_norm", l), modrow(l, 3), modrow(l, 4), w["in"], "mix_in")
        y_pool, diff = _pool_fwd(z, pool_w[l], vec(pool_scale[l]), "pool_fwd")
        q, k, v, cqn, ckvn = _qkv_fwd(z, vec(q_a_norm[l]), vec(kv_a_norm[l]), w["q"], w["kv"], cos, sin, "qkv_fwd")
        o, lse = _attn_fwd(q, k, v, "attn_fwd")
        ycat = jnp.concatenate([y_pool, o.astype(BF16)], axis=1)
        y2, xmix = _mm(ycat, w["out"], "nn", "mix_out", res=xs, gate=modrow(l, 5), gate_factor=1.0)
        sv["mix"] = dict(x=xs, h=h2, z=z, diff=diff, q=q, k=k, v=v, cqn=cqn, ckvn=ckvn, lse=lse, ycat=ycat, y=y2)
        xs = xmix

        xs = ffn_fwd(xs, normrow("ffn2_norm", l), 6, w["g2"], w["u2"], w["d2"], "f2")
        saved.append(sv)
        if l + 1 < nl:
            lands = landed(in_flight, ROWS_ALL, xs, l + 1)

    loss_part, dx, d_final = _loss_head(xs, vec(final_norm), loss_target.reshape(seq, D_MODEL), "loss_head")

    small = {name: [None] * nl for name in ("ffn1_norm", "mix_norm", "ffn2_norm", "q_a_norm", "kv_a_norm",
                                            "pool_scale", "pool_w", "dmod")}
    core = lax.axis_index("c").astype(jnp.int32).reshape(1)
    chip = 2 * lax.axis_index("x") + lax.axis_index("y")
    exchanges = []

    def leave(srcs, rows_list, after, tag):
        return _pair_start(srcs, rows_list, after, f"pair_start_{tag}"), rows_list, tag

    def forward_on(pending, after, layer, row_off):
        (send_sems, recv_sems, srcs, land, _), rows_list, tag = pending
        srcs, land = _split_wait(send_sems, recv_sems, 1, srcs, land, after, f"pair_wait_{tag}")
        sums = _pair_sum(srcs, rows_list, land, core, "pair_sum")
        flight = _chip_exchange_start(sums, chip, after, f"exchange_start_{tag}")
        exchanges.append((flight, layer, row_off, tag))
        return flight[4]

    pending = None
    head = _gate_bwd(dx, saved[nl - 1]["f2"]["y"], modrow(nl - 1, 8), 0.5, "gate_bwd")
    for l in reversed(range(nl)):
        sv = saved[l]
        w = sv["w"]
        dmod = [None] * N_MOD
        gr = {}

        def ffn_bwd(dxin, head, s_, norm, k0, wg, wu, wd, tag, below, first_after=None, mid=None):
            dy, dmod[k0 + 2] = head
            da, db, gr["d" + tag], gr["g" + tag], gr["u" + tag] = _ffn_bwd_cols(
                dy, s_["h"], s_["a"], s_["b"], s_["t"], wd, "ffn_bwd_cols", after=first_after)
            dh = _mm_pair(da, wg, db, wu, "ffn_bwd_dh", after=None if mid is None else mid(da))
            outs = _rm_bwd(dh, s_["x"], dxin, norm, modrow(l, k0 + 1), "rm_bwd", below=below)
            dmod[k0], dmod[k0 + 1] = outs[1], outs[2]
            return outs[0], outs[3], outs[4:]

        s_ = sv["mix"]
        dx, small["ffn2_norm"][l], head = ffn_bwd(
            dx, head, sv["f2"], normrow("ffn2_norm", l), 6, w["g2"], w["u2"], w["d2"], "2", (s_["y"], modrow(l, 5), 1.0),
            first_after=None if pending is None else pending[0][4])

        pending_c = leave([gr["g2"], gr["u2"], gr["d2"]], ROWS_A, dx, f"{l}c")
        mix_after = pending_c[0][4]
        if pending is not None:
            mix_after = forward_on(pending, mix_after, l + 1, GB_F1)
            pending = None
        dy, dmod[5] = head
        gr["out"] = _mm(s_["ycat"], dy, "tn", "mix_out_dw", out_dtype=BF16, tm=512, after=mix_after)
        dycat = _mm(dy, w["out"], "nt", "mix_out_dx", tm=1024)
        du, small["pool_w"][l], small["pool_scale"][l] = _pool_bwd(dycat, s_["diff"], pool_w[l], vec(pool_scale[l]), "pool_bwd")
        dq, dk, dv = _attn_bwd(s_["q"], s_["k"], s_["v"], s_["lse"], dycat, "attn_bwd")
        dz, dqb, dkvb, small["q_a_norm"][l], small["kv_a_norm"][l] = _qkv_bwd(
            dq, dk, dv, du, s_["z"], vec(q_a_norm[l]), vec(kv_a_norm[l]), w["q"], w["kv"], cos, sin, "qkv_bwd")
        gr["q"] = _mm(dqb, s_["cqn"], "tn", "q_b_dw", out_dtype=BF16, tm=512, after=forward_on(pending_c, dz, l, GB_F2))
        gr["kv"] = _mm(dkvb, s_["ckvn"], "tn", "kv_b_dw", out_dtype=BF16, tm=512)
        gr["in"] = _mm(dz, s_["h"], "tn", "mix_in_dw", out_dtype=BF16, tm=512)
        dh2 = _mm(dz, w["in"], "nn", "mix_in_dx", tm=1024)
        outs = _rm_bwd(dh2, s_["x"], dx, normrow("mix_norm", l), modrow(l, 4), "rm_bwd",
                       below=(sv["f1"]["y"], modrow(l, 2), 0.5))
        dx, dmod[3], dmod[4], small["mix_norm"][l] = outs[:4]
        head = outs[4:]

        first_after, mid = None, None
        if l == 0:
            pending_b = leave(_grad_sources_b(gr)[3:], ROWS_TAIL, dx, "0b")
            first_after = pending_b[0][4]
            last_groups = []

            def mid(da):
                last_groups.append(leave([gr["g1"], gr["u1"], gr["d1"]], ROWS_A, da, "0a"))
                return forward_on(pending_b, last_groups[0][0][4], 0, GB_TAIL)
        below = (saved[l - 1]["f2"]["y"], modrow(l - 1, 8), 0.5) if l > 0 else None
        dx, small["ffn1_norm"][l], head = ffn_bwd(
            dx, head, sv["f1"], normrow("ffn1_norm", l), 0, w["g1"], w["u1"], w["d1"], "1", below, first_after, mid)

        small["dmod"][l] = jnp.concatenate(dmod, axis=0)
        if l > 0:
            pending = leave([gr["g1"], gr["u1"], gr["d1"]] + _grad_sources_b(gr)[3:], ROWS_A + ROWS_TAIL, dx, l)

    grad_x = dx.reshape(x.shape)
    pending_a = last_groups[0]

    layout, small_rows = _small_layout(nl)
    pieces = {
        "dmod": jnp.concatenate(small["dmod"], axis=0),
        "ffn1_norm": jnp.concatenate(small["ffn1_norm"], axis=0),
        "mix_norm": jnp.concatenate(small["mix_norm"], axis=0),
        "ffn2_norm": jnp.concatenate(small["ffn2_norm"], axis=0),
        "q_a_norm": jnp.concatenate(small["q_a_norm"], axis=0),
        "kv_a_norm": jnp.concatenate(small["kv_a_norm"], axis=0),
        "pool_scale": jnp.concatenate(small["pool_scale"], axis=0),
        "final_norm": d_final,
        "loss": jnp.broadcast_to(loss_part[0:1, 0:1], (1, D_MODEL)),
        "pool_w": _pack_bf16_pairs(jnp.stack(small["pool_w"]).reshape(-1, D_MODEL)),
    }
    small_buf = jnp.concatenate([_to_rows(pieces[name]) for name in layout], axis=0)
    def landed_sums(gbuf, entries, after):
        for (send_sems, recv_sems, sums, recv, _), layer, row_off, tag in entries:
            _, recv = _split_wait(send_sems, recv_sems, N_CHIPS - 1, sums, recv, after, f"exchange_wait_{tag}")
            gbuf = _sum_slots_into(recv, gbuf, layer, row_off, "sum_grads")
        return gbuf

    gbuf = lax.empty((nl, ROWS_L, D_MODEL), F32)
    token_0a = forward_on(pending_a, dx, 0, GB_F1)
    spread = _spread_start(small_buf, me, token_0a, "small_start")
    gbuf = landed_sums(gbuf, [e for e in exchanges if e[3] != "0a"], spread[4])

    def swap(t):
        return t.transpose(0, 2, 1)

    def same(t):
        return t

    grads, updates = {}, {}

    def update_rows(gbuf, table):
        for wname, off, view in table:
            g, d_, nm, nv = _adamw_rows(view(wts[wname]), gbuf, off, view(mom_m[wname]), view(mom_v[wname]), "adamw_rows")
            grads[wname], updates[wname] = view(g), (view(d_), view(nm), view(nv))

    update_rows(gbuf, (("ffn2_w_gate", GB_F2, swap), ("ffn2_w_up", GB_F2 + FF_SH, swap),
                       ("ffn2_w_down", GB_F2 + 2 * FF_SH, same), ("w_out", GB_TAIL, same)))
    small_grads = {
        "w_kv_b": (gbuf[:, OFF_KV:OFF_KV + KV_SH_ROWS].reshape(nl, -1, KV_LORA).transpose(0, 2, 1), same),
        "w_in": (gbuf[:, OFF_IN:OFF_IN + IN_SH], swap),
        "w_q_b": (gbuf[:, OFF_Q:OFF_Q + Q_SH_ROWS].reshape(nl, -1, Q_LORA), swap),
    }
    for wname, (g, view) in small_grads.items():
        upd = _adamw_nd(view(wts[wname]), g, view(mom_m[wname]), view(mom_v[wname]), "adamw")
        grads[wname], updates[wname] = view(g), tuple(view(t) for t in upd)
    gbuf = landed_sums(gbuf, [e for e in exchanges if e[3] == "0a"], updates["w_q_b"][0])
    update_rows(gbuf, (("ffn1_w_gate", GB_F1, swap), ("ffn1_w_up", GB_F1 + FF_SH, swap),
                       ("ffn1_w_down", GB_F1 + 2 * FF_SH, same)))

    _, small_all = _split_wait(spread[0], spread[1], N_DEV - 1, spread[2], spread[3], updates["ffn1_w_down"][0],
                               "small_wait")
    pool_off, pool_rows = layout["pool_w"]
    small_sum = _sum_slots(small_all[:, :pool_off], "sum_small")
    pool_sum = _sum_slots(_unpack_bf16_pairs(small_all[:, pool_off:pool_off + pool_rows]), "sum_pool_w")

    def take(name, width=D_MODEL):
        off, n = layout[name]
        return small_sum[off:off + n, :width]

    late = {"ada_b": take("dmod").reshape(nl, N_MOD * D_MODEL),
            "ffn1_norm": take("ffn1_norm"), "mix_norm": take("mix_norm"), "ffn2_norm": take("ffn2_norm"),
            "q_a_norm": take("q_a_norm", Q_LORA), "kv_a_norm": take("kv_a_norm", KV_LORA),
            "pool_scale": take("pool_scale", POOL_WIDTH), "final_norm": take("final_norm").reshape(D_MODEL),
            "pool_w": pool_sum.reshape(pool_w.shape)}
    loss = take("loss")[0, 0]

    off, n = layout["dmod"]
    dmod_all = small_all[:, off:off + n].reshape(N_DEV, nl, N_MOD * D_MODEL)
    dmod_mine = lax.dynamic_slice_in_dim(dmod_all, me * ada_cols, ada_cols, axis=2)
    dmod_pad = jnp.pad(dmod_mine.transpose(1, 0, 2), ((0, 0), (0, LANE - N_DEV), (0, 0)))
    late["ada_w"] = _ada_grad(jnp.pad(c_all, ((0, LANE - N_DEV), (0, 0))), dmod_pad, "ada_grad")
    for name, g in late.items():
        grads[name], updates[name] = g, _adamw_nd(wts[name], g, mom_m[name], mom_v[name], "adamw")

    return (loss, grad_x, *[grads[n] for n in order], *[updates[n][0] for n in order],
            *[updates[n][1] for n in order], *[updates[n][2] for n in order])
```

```python
import math

import numpy as np
import jax
import jax.numpy as jnp
from jax import lax
from jax.experimental import pallas as pl
from jax.experimental.pallas import tpu as pltpu

F32 = jnp.float32
BF16 = jnp.bfloat16

N_DEV = 8
D_MODEL = 1024
D_FF = 2816
POOL_WIDTH = 512
POOL_WINDOWS = (2, 4, 8, 16)
POOL_GC = 128
N_HEADS = 4
QK_NOPE = 128
QK_ROPE = 64
V_HEAD = 128
QK_HEAD = QK_NOPE + QK_ROPE
HEAD_PAD = 256
Q_LORA = 384
KV_LORA = 256
IN_COLS = POOL_WIDTH + Q_LORA + KV_LORA + QK_ROPE
IN_PAD = 1280
ROPE_THETA = 10000.0
SOFTMAX_SCALE = 1.0 / math.sqrt(QK_HEAD)
EPS = 1e-6
N_MOD = 9

ADAM_LR = 0.001
ADAM_B1 = 0.9
ADAM_B2 = 0.999
ADAM_EPS = 1e-08
ADAM_WD = 0.01
ADAM_STEP = 10

LANE = 128
VMEM_LIMIT = 56 * 1024 * 1024

FF_SH = D_FF // N_DEV
OFF_G1, OFF_U1, OFF_D1 = 0, FF_SH, 2 * FF_SH
OFF_G2, OFF_U2, OFF_D2 = 3 * FF_SH, 4 * FF_SH, 5 * FF_SH
OFF_OUT = 6 * FF_SH
OFF_KV = OFF_OUT + 128
OFF_IN = OFF_KV + 32
OFF_Q = OFF_IN + 160
Q_PAD_ROWS = 64
ROWS_L = OFF_Q + Q_PAD_ROWS
IN_SH = IN_COLS // N_DEV
Q_SH_ROWS = (N_HEADS * QK_HEAD // N_DEV) * Q_LORA // D_MODEL
KV_SH_ROWS = (N_HEADS * (QK_NOPE + V_HEAD) // N_DEV) * KV_LORA // D_MODEL


def _tile(dim, target):
    if dim <= target:
        return dim
    best = None
    for t in range(LANE, target + 1, LANE):
        if dim % t == 0:
            best = t
    assert best is not None, (dim, target)
    return best


def _params(sem):
    return pltpu.CompilerParams(dimension_semantics=sem, vmem_limit_bytes=VMEM_LIMIT)


def _mesh_pos():
    return lax.axis_index("x"), lax.axis_index("y"), lax.axis_index("c")


def _all_gather(x, name):
    m, n = x.shape

    def body(x_ref, out_ref, send_sems, recv_sems, local_sem):
        px, py, pc = _mesh_pos()
        me, sibling = (px, py, pc), (px, py, 1 - pc)
        chips = [(1 - px, py), (px, 1 - py), (1 - px, 1 - py)]

        def rows(bx, by, bc):
            return out_ref.at[pl.ds((4 * bx + 2 * by + bc) * m, m), :]

        def copy(k, block, to, src=None):
            return pltpu.make_async_remote_copy(
                src_ref=rows(*block) if src is None else src, dst_ref=rows(*block),
                send_sem=send_sems.at[k], recv_sem=recv_sems.at[k],
                device_id=to, device_id_type=pl.DeviceIdType.MESH)

        mine = pltpu.make_async_copy(x_ref, rows(*me), local_sem)
        mine.start()
        first = [copy(0, me, sibling, src=x_ref)]
        first += [copy(1 + j, me, (*chip, pc), src=x_ref) for j, chip in enumerate(chips)]
        for cp in first:
            cp.start()
        passed = [copy(4 + j, (*chip, pc), sibling) for j, chip in enumerate(chips)]
        for j, chip in enumerate(chips):
            copy(1 + j, (*chip, pc), me).wait_recv()
            passed[j].start()
        copy(0, sibling, me).wait_recv()
        for j, chip in enumerate(chips):
            copy(4 + j, (*chip, 1 - pc), me).wait_recv()
        for cp in first + passed:
            cp.wait_send()
        mine.wait()

    hbm = pl.BlockSpec(memory_space=pltpu.HBM)
    return pl.pallas_call(
        body, name=name,
        out_shape=jax.ShapeDtypeStruct((N_DEV * m, n), x.dtype),
        in_specs=[hbm], out_specs=hbm,
        scratch_shapes=[pltpu.SemaphoreType.DMA((7,)), pltpu.SemaphoreType.DMA((7,)),
                        pltpu.SemaphoreType.DMA],
    )(x)


SMALL_ROWS = ROWS_L - OFF_KV
ROWS_A = [FF_SH] * 3
ROWS_B = [FF_SH] * 3 + [128, SMALL_ROWS]
ROWS_ALL = ROWS_A + ROWS_B
SPLIT_AB = sum(ROWS_A)
ROWS_TAIL = [128, SMALL_ROWS]
GB_F2, GB_F1, GB_TAIL = 0, SPLIT_AB, 2 * SPLIT_AB
HBM_SPEC = pl.BlockSpec(memory_space=pltpu.HBM)
SEM_SPEC = pl.BlockSpec(memory_space=pltpu.SEMAPHORE)
ANY_SPEC = pl.BlockSpec(memory_space=pl.ANY)
EFFECT = pltpu.SideEffectType.DATAFLOW_SIDE_EFFECTING


def _hbm(t):
    return pltpu.with_memory_space_constraint(t, pltpu.HBM)


def _whole_wait(ref, send_sem, recv_sem, peer):
    return pltpu.make_async_remote_copy(src_ref=ref, dst_ref=ref, send_sem=send_sem, recv_sem=recv_sem,
                                        device_id=peer, device_id_type=pl.DeviceIdType.MESH)


def _offsets(rows_list):
    return [sum(rows_list[:i]) for i in range(len(rows_list))]


def _gather_start(packed, rows_list, after, name):
    n = len(rows_list)
    offs = _offsets(rows_list)
    lands = [_hbm(lax.empty((N_DEV * rows, D_MODEL), BF16)) for rows in rows_list]

    def body(packed_ref, *refs):
        land = refs[:n]
        send_sems, recv_sems = refs[n + 1], refs[n + 2]
        token = refs[-1]
        px, py, pc = _mesh_pos()
        me = 4 * px + 2 * py + pc
        peers = [(px, py, 1 - pc), (1 - px, py, pc), (px, 1 - py, pc), (1 - px, 1 - py, pc)]
        for k, peer in enumerate(peers):
            for off, rows, land_ref in zip(offs, rows_list, land):
                pltpu.make_async_remote_copy(
                    src_ref=packed_ref.at[pl.ds(off, rows), :], dst_ref=land_ref.at[pl.ds(me * rows, rows), :],
                    send_sem=send_sems.at[k], recv_sem=recv_sems.at[k],
                    device_id=peer, device_id_type=pl.DeviceIdType.MESH).start()
        token[...] = jnp.zeros_like(token)

    outs = pl.pallas_call(
        body, name=name,
        out_shape=(pltpu.SemaphoreType.DMA((4,)), pltpu.SemaphoreType.DMA((4,)), pltpu.HBM(packed.shape, BF16),
                   *[pltpu.HBM(t.shape, BF16) for t in lands], jax.ShapeDtypeStruct((8, LANE), F32)),
        in_specs=(HBM_SPEC,) * (1 + n) + (ANY_SPEC,),
        out_specs=(SEM_SPEC, SEM_SPEC) + (HBM_SPEC,) * (1 + n) + (pl.BlockSpec(memory_space=pltpu.VMEM),),
        input_output_aliases={i: 2 + i for i in range(1 + n)},
        compiler_params=pltpu.CompilerParams(has_side_effects=EFFECT),
    )(_hbm(packed), *lands, after)
    return outs[0], outs[1], outs[2], list(outs[3:3 + n]), outs[-1]


def _gather_wait(send_sems, recv_sems, packed, lands, after, name):
    n = len(lands)

    def body(packed_ref, *refs):
        s_sems, r_sems = refs[n], refs[n + 1]
        me = _mesh_pos()
        for k in range(4):
            cp = _whole_wait(packed_ref, s_sems.at[k], r_sems.at[k], me)
            cp.wait_send()
            cp.wait_recv()

    outs = pl.pallas_call(
        body, name=name,
        out_shape=(pltpu.HBM(packed.shape, BF16), *[pltpu.HBM(t.shape, BF16) for t in lands]),
        in_specs=(HBM_SPEC,) * (1 + n) + (SEM_SPEC, SEM_SPEC, ANY_SPEC),
        out_specs=(HBM_SPEC,) * (1 + n),
        input_output_aliases={i: i for i in range(1 + n)},
        compiler_params=pltpu.CompilerParams(has_side_effects=EFFECT),
    )(packed, *lands, send_sems, recv_sems, after)
    return outs[0], list(outs[1:])


def _gather_finish(packed, rows_list, lands, name):
    n = len(rows_list)
    offs = _offsets(rows_list)

    def body(packed_ref, *refs):
        land = refs[n:2 * n]
        send_sems, recv_sems, stage, stage_sem = refs[2 * n:]
        px, py, pc = _mesh_pos()
        me = 4 * px + 2 * py + pc
        sibling = (px, py, 1 - pc)
        load = pltpu.make_async_copy(packed_ref, stage, stage_sem)
        load.start()
        load.wait()
        for off, rows, land_ref in zip(offs, rows_list, land):
            pltpu.make_async_copy(stage.at[pl.ds(off, rows), :], land_ref.at[pl.ds(me * rows, rows), :],
                                  stage_sem).start()
        for j, (cx, cy) in enumerate([(1 - px, py), (px, 1 - py), (1 - px, 1 - py)]):
            block = 4 * cx + 2 * cy + pc
            for rows, land_ref in zip(rows_list, land):
                blk = land_ref.at[pl.ds(block * rows, rows), :]
                pltpu.make_async_remote_copy(src_ref=blk, dst_ref=blk, send_sem=send_sems.at[j],
                                             recv_sem=recv_sems.at[j], device_id=sibling,
                                             device_id_type=pl.DeviceIdType.MESH).start()
        for j in range(3):
            cp = _whole_wait(packed_ref, send_sems.at[j], recv_sems.at[j], sibling)
            cp.wait_recv()
            cp.wait_send()
        pltpu.make_async_copy(stage, packed_ref, stage_sem).wait()

    outs = pl.pallas_call(
        body, name=name,
        out_shape=tuple(jax.ShapeDtypeStruct(t.shape, BF16) for t in lands),
        in_specs=(HBM_SPEC,) * (1 + n), out_specs=(HBM_SPEC,) * n,
        input_output_aliases={1 + i: i for i in range(n)},
        scratch_shapes=[pltpu.SemaphoreType.DMA((3,)), pltpu.SemaphoreType.DMA((3,)),
                        pltpu.VMEM(packed.shape, BF16), pltpu.SemaphoreType.DMA],
    )(packed, *lands)
    return list(outs)


N_CHIPS = 4


def _pair_start(srcs, rows_list, after, name):
    n = len(rows_list)
    offs = _offsets(rows_list)
    land = lax.empty((N_CHIPS, sum(rows_list), D_MODEL), BF16)

    def body(*refs):
        src, land_ref = refs[:n], refs[n]
        send_sems, recv_sems = refs[n + 2], refs[n + 3]
        token = refs[-1]
        px, py, pc = _mesh_pos()
        for k in range(N_CHIPS):
            block = 2 * k + (1 - pc)
            for off, rows, src_ref in zip(offs, rows_list, src):
                pltpu.make_async_remote_copy(
                    src_ref=src_ref.at[pl.ds(block * rows, rows), :], dst_ref=land_ref.at[k, pl.ds(off, rows), :],
                    send_sem=send_sems.at[0], recv_sem=recv_sems.at[0],
                    device_id=(px, py, 1 - pc), device_id_type=pl.DeviceIdType.MESH).start()
        token[...] = jnp.zeros_like(token)

    outs = pl.pallas_call(
        body, name=name,
        out_shape=(pltpu.SemaphoreType.DMA((1,)), pltpu.SemaphoreType.DMA((1,)),
                   *[pltpu.HBM(t.shape, BF16) for t in srcs], pltpu.HBM(land.shape, BF16),
                   jax.ShapeDtypeStruct((8, LANE), F32)),
        in_specs=(HBM_SPEC,) * (n + 1) + (ANY_SPEC,),
        out_specs=(SEM_SPEC, SEM_SPEC) + (HBM_SPEC,) * (n + 1) + (pl.BlockSpec(memory_space=pltpu.VMEM),),
        input_output_aliases={i: 2 + i for i in range(n + 1)},
        compiler_params=pltpu.CompilerParams(has_side_effects=EFFECT),
    )(*[_hbm(t) for t in srcs], _hbm(land), after)
    return outs[0], outs[1], list(outs[2:2 + n]), outs[2 + n], outs[-1]


def _split_wait(send_sems, recv_sems, n_sems, srcs, land, after, name):
    n = len(srcs)

    def body(*refs):
        land_ref = refs[n]
        s_sems, r_sems = refs[n + 1], refs[n + 2]
        me = _mesh_pos()
        for k in range(n_sems):
            cp = _whole_wait(land_ref.at[0] if n_sems > 1 else land_ref, s_sems.at[k], r_sems.at[k], me)
            cp.wait_send()
            cp.wait_recv()

    outs = pl.pallas_call(
        body, name=name,
        out_shape=(*[pltpu.HBM(t.shape, t.dtype) for t in srcs], pltpu.HBM(land.shape, land.dtype)),
        in_specs=(HBM_SPEC,) * (n + 1) + (SEM_SPEC, SEM_SPEC, ANY_SPEC),
        out_specs=(HBM_SPEC,) * (n + 1),
        input_output_aliases={i: i for i in range(n + 1)},
        compiler_params=pltpu.CompilerParams(has_side_effects=EFFECT),
    )(*srcs, land, send_sems, recv_sems, after)
    return list(outs[:n]), outs[n]


def _spread_start(x, me_id, after, name):
    land = lax.dynamic_update_slice_in_dim(lax.empty((N_DEV,) + x.shape, x.dtype), x[None], me_id, axis=0)

    def body(x_ref, land_ref, after_ref, send_sems, recv_sems, x_thru, land_thru, token):
        px, py, pc = _mesh_pos()
        me = 4 * px + 2 * py + pc
        for k in range(1, N_DEV):
            qx = 1 - px if k & 4 else px
            qy = 1 - py if k & 2 else py
            qc = 1 - pc if k & 1 else pc
            pltpu.make_async_remote_copy(
                src_ref=x_ref, dst_ref=land_ref.at[me], send_sem=send_sems.at[k - 1], recv_sem=recv_sems.at[k - 1],
                device_id=(qx, qy, qc), device_id_type=pl.DeviceIdType.MESH).start()
        token[...] = jnp.zeros_like(token)

    outs = pl.pallas_call(
        body, name=name,
        out_shape=(pltpu.SemaphoreType.DMA((N_DEV - 1,)), pltpu.SemaphoreType.DMA((N_DEV - 1,)),
                   pltpu.HBM(x.shape, x.dtype), pltpu.HBM(land.shape, land.dtype), jax.ShapeDtypeStruct((8, LANE), F32)),
        in_specs=(HBM_SPEC, HBM_SPEC, ANY_SPEC),
        out_specs=(SEM_SPEC, SEM_SPEC, HBM_SPEC, HBM_SPEC, pl.BlockSpec(memory_space=pltpu.VMEM)),
        input_output_aliases={0: 2, 1: 3},
        compiler_params=pltpu.CompilerParams(has_side_effects=EFFECT),
    )(_hbm(x), _hbm(land), after)
    return outs[0], outs[1], [outs[2]], outs[3], outs[4]


def _pair_sum(srcs, rows_list, land, core, name):
    n = len(rows_list)
    offs = _offsets(rows_list)
    total = sum(rows_list)

    def body(core_ref, *refs):
        src, land_ref, out_ref = refs[:n], refs[n], refs[n + 1]
        for off, rows, src_ref in zip(offs, rows_list, src):
            out_ref[pl.ds(off, rows), :] = (src_ref[...].astype(F32)
                                            + land_ref[pl.ds(off, rows), :].astype(F32)).astype(BF16)

    slot = pl.BlockSpec((None, total, D_MODEL), lambda k, c: (k, 0, 0))
    grid_spec = pltpu.PrefetchScalarGridSpec(
        num_scalar_prefetch=1, grid=(N_CHIPS,),
        in_specs=[pl.BlockSpec((rows, D_MODEL), lambda k, c: (2 * k + c[0], 0)) for rows in rows_list] + [slot],
        out_specs=slot)
    return pl.pallas_call(
        body, name=name, grid_spec=grid_spec,
        out_shape=jax.ShapeDtypeStruct((N_CHIPS, total, D_MODEL), BF16),
        compiler_params=_params(("parallel",)),
    )(core, *srcs, land)


def _chip_exchange_start(sums, chip, after, name):
    own = lax.dynamic_index_in_dim(sums, chip, axis=0, keepdims=True)
    recv = lax.dynamic_update_slice_in_dim(lax.empty(sums.shape, BF16), own, chip, axis=0)

    def body(sums_ref, recv_ref, after_ref, send_sems, recv_sems, sums_thru, recv_thru, token):
        px, py, pc = _mesh_pos()
        for k in range(1, N_CHIPS):
            qx = 1 - px if k & 2 else px
            qy = 1 - py if k & 1 else py
            pltpu.make_async_remote_copy(
                src_ref=sums_ref.at[2 * qx + qy], dst_ref=recv_ref.at[2 * px + py],
                send_sem=send_sems.at[k - 1], recv_sem=recv_sems.at[k - 1],
                device_id=(qx, qy, pc), device_id_type=pl.DeviceIdType.MESH).start()
        token[...] = jnp.zeros_like(token)

    outs = pl.pallas_call(
        body, name=name,
        out_shape=(pltpu.SemaphoreType.DMA((N_CHIPS - 1,)), pltpu.SemaphoreType.DMA((N_CHIPS - 1,)),
                   pltpu.HBM(sums.shape, BF16), pltpu.HBM(recv.shape, BF16), jax.ShapeDtypeStruct((8, LANE), F32)),
        in_specs=(HBM_SPEC, HBM_SPEC, ANY_SPEC),
        out_specs=(SEM_SPEC, SEM_SPEC, HBM_SPEC, HBM_SPEC, pl.BlockSpec(memory_space=pltpu.VMEM)),
        input_output_aliases={0: 2, 1: 3},
        compiler_params=pltpu.CompilerParams(has_side_effects=EFFECT),
    )(_hbm(sums), _hbm(recv), after)
    return outs[0], outs[1], [outs[2]], outs[3], outs[4]


def _sum_slots_into(recv, buf, layer, row_off, name):
    slots, r, n = recv.shape
    tr = _row_tile(math.gcd(r, row_off) if row_off else r, 512)
    first = row_off // tr

    def body(in_ref, buf_ref, out_ref):
        acc = in_ref[0].astype(F32)
        for j in range(1, slots):
            acc = acc + in_ref[j].astype(F32)
        out_ref[...] = acc

    return pl.pallas_call(
        body, name=name, grid=(r // tr,), out_shape=jax.ShapeDtypeStruct(buf.shape, F32),
        in_specs=[pl.BlockSpec((slots, tr, n), lambda i: (0, i, 0)), ANY_SPEC],
        out_specs=pl.BlockSpec((None, tr, n), lambda i: (layer, first + i, 0)),
        input_output_aliases={1: 0},
        compiler_params=_params(("parallel",)),
    )(recv, buf)


def _sum_slots(recv, name, after=None):
    _, r, n = recv.shape
    tr = _row_tile(r, 512)

    def body(in_ref, *refs):
        acc = in_ref[0].astype(F32)
        for j in range(1, N_DEV):
            acc = acc + in_ref[j].astype(F32)
        refs[-1][...] = acc

    grid = (r // tr,)
    in_specs, out_spec = [pl.BlockSpec((N_DEV, tr, n), lambda i: (0, i, 0))], pl.BlockSpec((tr, n), lambda i: (i, 0))
    args = [recv]
    if after is not None:
        in_specs.append(ANY_SPEC)
        args.append(after)
    return pl.pallas_call(
        body, name=name, grid=grid,
        out_shape=jax.ShapeDtypeStruct((r, n), F32),
        in_specs=in_specs, out_specs=out_spec,
        compiler_params=_params(("parallel",)),
    )(*args)


def _row_tile(rows, target):
    if rows <= target:
        return rows
    best = None
    for t in range(16, target + 1, 16):
        if rows % t == 0:
            best = t
    assert best is not None, rows
    return best


_DIMS = {"nn": ((1,), (0,)), "nt": ((1,), (1,)), "tn": ((0,), (0,))}


def _mm(a, b, mode, name, out_dtype=F32, res=None, gate=None, gate_factor=1.0, tm=512, tn=1408, after=None):
    assert (res is None) == (gate is None)
    if mode == "tn":
        kdim, m = a.shape
    else:
        m, kdim = a.shape
    n = b.shape[0] if mode == "nt" else b.shape[1]
    tm, tn = _tile(m, tm), _tile(n, tn)
    a_spec = (pl.BlockSpec((kdim, tm), lambda i, j: (0, i)) if mode == "tn"
              else pl.BlockSpec((tm, kdim), lambda i, j: (i, 0)))
    b_spec = (pl.BlockSpec((tn, kdim), lambda i, j: (j, 0)) if mode == "nt"
              else pl.BlockSpec((kdim, tn), lambda i, j: (0, j)))
    o_spec = pl.BlockSpec((tm, tn), lambda i, j: (i, j))
    dims = (_DIMS[mode], ((), ()))
    has_res = res is not None

    def body(a_ref, b_ref, *refs):
        y = lax.dot_general(a_ref[...].astype(BF16), b_ref[...].astype(BF16), dims,
                            preferred_element_type=F32)
        if has_res:
            res_ref, gate_ref = refs[0], refs[1]
            y_ref, o_ref = refs[-2], refs[-1]
            y_ref[...] = y.astype(BF16)
            o_ref[...] = res_ref[...] + (gate_factor * gate_ref[...]) * y
        else:
            refs[-1][...] = y.astype(out_dtype)

    in_specs, args = [a_spec, b_spec], [a, b]
    if has_res:
        gate_spec, gate_arg = _vec_in(gate, tile=tn)
        in_specs += [o_spec, gate_spec]
        args += [res, gate_arg]
        out_shape = (jax.ShapeDtypeStruct((m, n), BF16), jax.ShapeDtypeStruct((m, n), F32))
        out_specs = (o_spec, o_spec)
    else:
        out_shape, out_specs = jax.ShapeDtypeStruct((m, n), out_dtype), o_spec
    if after is not None:
        in_specs.append(ANY_SPEC)
        args.append(after)
    return pl.pallas_call(
        body, name=name, grid=(m // tm, n // tn), out_shape=out_shape,
        in_specs=in_specs, out_specs=out_specs,
        compiler_params=_params(("parallel", "parallel")),
    )(*args)


def _vec_in(v, tile=None):
    if isinstance(v, tuple):
        table, row = v
        if tile is None:
            return pl.BlockSpec((None, 1, table.shape[-1]), lambda *idx: (row, 0, 0)), table
        return pl.BlockSpec((None, 1, tile), lambda i, j: (row, 0, j)), table
    if tile is None:
        return pl.BlockSpec((1, v.shape[-1]), lambda *idx: (0, 0)), v
    return pl.BlockSpec((1, tile), lambda i, j: (0, j)), v


def _vec_spec(width):
    return pl.BlockSpec((1, width), lambda i: (0, 0))


def _rm_bwd(dh, x, dres, gw, scale, name, below=None):
    s, d = x.shape
    ts = _tile(s, 512)
    factor = None if below is None else below[2]

    def body(dh_ref, x_ref, dres_ref, gw_ref, sc_ref, *refs):
        dx_ref, dsh_ref, dsc_ref, dgw_ref = refs[-6:-2] if below is not None else refs[-4:]

        @pl.when(pl.program_id(0) == 0)
        def _():
            dsh_ref[...] = jnp.zeros_like(dsh_ref)
            dsc_ref[...] = jnp.zeros_like(dsc_ref)
            dgw_ref[...] = jnp.zeros_like(dgw_ref)
            if below is not None:
                refs[-1][...] = jnp.zeros_like(refs[-1])

        xv, dhv, gwv = x_ref[...], dh_ref[...], gw_ref[...]
        r = lax.rsqrt(jnp.mean(xv * xv, axis=-1, keepdims=True) + EPS)
        xn = xv * r
        y = xn * gwv
        dsh_ref[...] += jnp.sum(dhv, axis=0, keepdims=True)
        dsc_ref[...] += jnp.sum(dhv * y, axis=0, keepdims=True)
        dy = dhv * (1 + sc_ref[...])
        dgw_ref[...] += jnp.sum(dy * xn, axis=0, keepdims=True)
        dxn = dy * gwv
        dx = dres_ref[...] + r * (dxn - xn * jnp.mean(dxn * xn, axis=-1, keepdims=True))
        dx_ref[...] = dx
        if below is not None:
            yb_ref, gb_ref, dyb_ref, dgb_ref = refs[0], refs[1], refs[-2], refs[-1]
            dyb_ref[...] = ((factor * gb_ref[...]) * dx).astype(BF16)
            dgb_ref[...] += jnp.sum((factor * dx) * yb_ref[...].astype(F32), axis=0, keepdims=True)

    row = pl.BlockSpec((ts, d), lambda i: (i, 0))
    vec = jax.ShapeDtypeStruct((1, d), F32)
    (gw_spec, gw), (sc_spec, scale) = _vec_in(gw), _vec_in(scale)
    in_specs, args = [row, row, row, gw_spec, sc_spec], [dh, x, dres, gw, scale]
    out_shape = [jax.ShapeDtypeStruct((s, d), F32), vec, vec, vec]
    out_specs = [row, _vec_spec(d), _vec_spec(d), _vec_spec(d)]
    if below is not None:
        gate_spec, gate_arg = _vec_in(below[1])
        in_specs += [row, gate_spec]
        args += [below[0], gate_arg]
        out_shape += [jax.ShapeDtypeStruct((s, d), BF16), vec]
        out_specs += [row, _vec_spec(d)]
    return pl.pallas_call(
        body, name=name, grid=(s // ts,), out_shape=tuple(out_shape),
        in_specs=in_specs, out_specs=tuple(out_specs),
        compiler_params=_params(("arbitrary",)),
    )(*args)


def _norm_mm(x, gw, shift, scale, w, name, tm=1024):
    s, d = x.shape
    n = w.shape[0]
    tm = _tile(s, tm)

    def body(x_ref, gw_ref, sh_ref, sc_ref, w_ref, h_ref, z_ref):
        xv = x_ref[...]
        r = lax.rsqrt(jnp.mean(xv * xv, axis=-1, keepdims=True) + EPS)
        hb = (((xv * r) * gw_ref[...]) * (1 + sc_ref[...]) + sh_ref[...]).astype(BF16)
        h_ref[...] = hb
        z_ref[...] = lax.dot_general(hb, w_ref[...], (((1,), (1,)), ((), ())), preferred_element_type=F32)

    row = pl.BlockSpec((tm, d), lambda i: (i, 0))
    return pl.pallas_call(
        body, name=name, grid=(s // tm,),
        out_shape=(jax.ShapeDtypeStruct((s, d), BF16), jax.ShapeDtypeStruct((s, n), F32)),
        in_specs=[row, _vec_in(gw)[0], _vec_in(shift)[0], _vec_in(scale)[0], pl.BlockSpec((n, d), lambda i: (0, 0))],
        out_specs=(row, pl.BlockSpec((tm, n), lambda i: (i, 0))),
        compiler_params=_params(("parallel",)),
    )(x, _vec_in(gw)[1], _vec_in(shift)[1], _vec_in(scale)[1], w)


FFN_TM, FFN_TF = 2048, 256


def _ffn_up(x, gw, shift, scale, wg, wu, name, after=None):
    s, d = x.shape
    f = wg.shape[0]
    tm, tf = _tile(s, FFN_TM), _tile(f, FFN_TF)
    nt = (((1,), (1,)), ((), ()))

    def body(x_ref, gw_ref, sh_ref, sc_ref, wg_ref, wu_ref, *refs):
        h_ref, a_ref, b_ref, t_ref = refs[-4:]

        @pl.when(pl.program_id(1) == 0)
        def _():
            xv = x_ref[...]
            r = lax.rsqrt(jnp.mean(xv * xv, axis=-1, keepdims=True) + EPS)
            h_ref[...] = (((xv * r) * gw_ref[...]) * (1 + sc_ref[...]) + sh_ref[...]).astype(BF16)

        hb = h_ref[...]
        av = lax.dot_general(hb, wg_ref[...], nt, preferred_element_type=F32)
        bv = lax.dot_general(hb, wu_ref[...], nt, preferred_element_type=F32)
        a_ref[...] = av.astype(BF16)
        b_ref[...] = bv.astype(BF16)
        t_ref[...] = ((av * jax.nn.sigmoid(av)) * bv).astype(BF16)

    row = pl.BlockSpec((tm, d), lambda i, j: (i, 0))
    wblk = pl.BlockSpec((tf, d), lambda i, j: (j, 0))
    blk = pl.BlockSpec((tm, tf), lambda i, j: (i, j))
    wide = jax.ShapeDtypeStruct((s, f), BF16)
    vec_specs, vec_args = zip(*[_vec_in(v) for v in (gw, shift, scale)])
    in_specs, args = [row, *vec_specs, wblk, wblk], [x, *vec_args, wg, wu]
    if after is not None:
        in_specs.append(ANY_SPEC)
        args.append(after)
    return pl.pallas_call(
        body, name=name, grid=(s // tm, f // tf),
        out_shape=(jax.ShapeDtypeStruct((s, d), BF16), wide, wide, wide),
        in_specs=in_specs, out_specs=(row, blk, blk, blk),
        compiler_params=_params(("parallel", "arbitrary")),
    )(*args)


def _ffn_bwd_cols(dy, h, a, b, t, wd, name, after=None):
    s, d = dy.shape
    f = wd.shape[0]
    tf = _tile(f, FFN_TF)
    nt = (((1,), (1,)), ((), ()))
    tn = (((0,), (0,)), ((), ()))

    def body(dy_ref, h_ref, a_ref, b_ref, t_ref, wd_ref, *refs):
        da_ref, db_ref, gd_ref, gg_ref, gu_ref = refs[-5:]
        dyb, hb = dy_ref[...], h_ref[...]
        dtv = lax.dot_general(dyb, wd_ref[...], nt, preferred_element_type=F32)
        av, bv = a_ref[...].astype(F32), b_ref[...].astype(F32)
        sg = jax.nn.sigmoid(av)
        dbv = (dtv * (av * sg)).astype(BF16)
        dav = ((dtv * bv) * (sg * (1 + av * (1 - sg)))).astype(BF16)
        da_ref[...] = dav
        db_ref[...] = dbv
        gd_ref[...] = lax.dot_general(t_ref[...], dyb, tn, preferred_element_type=F32).astype(BF16)
        gg_ref[...] = lax.dot_general(dav, hb, tn, preferred_element_type=F32).astype(BF16)
        gu_ref[...] = lax.dot_general(dbv, hb, tn, preferred_element_type=F32).astype(BF16)

    whole = pl.BlockSpec((s, d), lambda j: (0, 0))
    col = pl.BlockSpec((s, tf), lambda j: (0, j))
    wblk = pl.BlockSpec((tf, d), lambda j: (j, 0))
    wide, wgrad = jax.ShapeDtypeStruct((s, f), BF16), jax.ShapeDtypeStruct((f, d), BF16)
    in_specs, args = [whole, whole, col, col, col, wblk], [dy, h, a, b, t, wd]
    if after is not None:
        in_specs.append(ANY_SPEC)
        args.append(after)
    return pl.pallas_call(
        body, name=name, grid=(f // tf,), out_shape=(wide, wide, wgrad, wgrad, wgrad),
        in_specs=in_specs, out_specs=(col, col, wblk, wblk, wblk),
        compiler_params=_params(("parallel",)),
    )(*args)


def _mm_pair(a1, b1, a2, b2, name, tm=1024, tn=512, after=None):
    m, kdim = a1.shape
    n = b1.shape[1]
    tm, tn = _tile(m, tm), _tile(n, tn)

    def body(a1_ref, b1_ref, a2_ref, b2_ref, *refs):
        refs[-1][...] = (jnp.dot(a1_ref[...], b1_ref[...], preferred_element_type=F32)
                         + jnp.dot(a2_ref[...], b2_ref[...], preferred_element_type=F32))

    a_spec = pl.BlockSpec((tm, kdim), lambda i, j: (i, 0))
    b_spec = pl.BlockSpec((kdim, tn), lambda i, j: (0, j))
    in_specs, args = [a_spec, b_spec, a_spec, b_spec], [a1, b1, a2, b2]
    if after is not None:
        in_specs.append(ANY_SPEC)
        args.append(after)
    return pl.pallas_call(
        body, name=name, grid=(m // tm, n // tn), out_shape=jax.ShapeDtypeStruct((m, n), F32),
        in_specs=in_specs, out_specs=pl.BlockSpec((tm, tn), lambda i, j: (i, j)),
        compiler_params=_params(("parallel", "parallel")),
    )(*args)


def _pool_counts(s):
    return (lax.broadcasted_iota(jnp.int32, (s, POOL_GC), 0))


def _pool_fwd(z, pool_w, pool_scale, name):
    s = z.shape[0]

    def body(u_ref, w_ref, sc_ref, y_ref, diff_ref):
        t = lax.broadcasted_iota(jnp.int32, (s, POOL_GC), 0)
        for g, win in enumerate(POOL_WINDOWS):
            cols = slice(g * POOL_GC, (g + 1) * POOL_GC)
            u = u_ref[:, cols]
            acc, step = u, 1
            while step < win:
                acc = acc + jnp.where(t >= step, pltpu.roll(acc, step, 0), 0.0)
                step *= 2
            cnt = jnp.minimum(t + 1, win).astype(F32)
            diff = acc / cnt - u
            diff_ref[:, cols] = diff
            ypre = jnp.dot(diff.astype(BF16), w_ref[g].astype(BF16), preferred_element_type=F32)
            y_ref[:, cols] = (ypre * sc_ref[:, cols]).astype(BF16)

    return pl.pallas_call(
        body, name=name, grid=(1,),
        out_shape=(jax.ShapeDtypeStruct((s, POOL_WIDTH), BF16), jax.ShapeDtypeStruct((s, POOL_WIDTH), F32)),
        in_specs=[pl.BlockSpec((s, POOL_WIDTH), lambda i: (0, 0)),
                  pl.BlockSpec(pool_w.shape, lambda i: (0, 0, 0)),
                  pl.BlockSpec((1, POOL_WIDTH), lambda i: (0, 0))],
        out_specs=(pl.BlockSpec((s, POOL_WIDTH), lambda i: (0, 0)),
                   pl.BlockSpec((s, POOL_WIDTH), lambda i: (0, 0))),
        compiler_params=_params(("arbitrary",)),
    )(z, pool_w, pool_scale)


def _pool_bwd(dycat, diff, pool_w, pool_scale, name):
    s = diff.shape[0]

    def body(dy_ref, diff_ref, w_ref, sc_ref, du_ref, dw_ref, dsc_ref):
        t = lax.broadcasted_iota(jnp.int32, (s, POOL_GC), 0)
        for g, win in enumerate(POOL_WINDOWS):
            cols = slice(g * POOL_GC, (g + 1) * POOL_GC)
            dy, dfb, wb = dy_ref[:, cols], diff_ref[:, cols].astype(BF16), w_ref[g].astype(BF16)
            ypre = jnp.dot(dfb, wb, preferred_element_type=F32)
            dsc_ref[:, cols] = jnp.sum(dy * ypre, axis=0, keepdims=True)
            dypre = (dy * sc_ref[:, cols]).astype(BF16)
            ddiff = lax.dot_general(dypre, wb, (((1,), (1,)), ((), ())), preferred_element_type=F32)
            dw_ref[g] = lax.dot_general(dfb, dypre, (((0,), (0,)), ((), ())), preferred_element_type=F32)
            cnt = jnp.minimum(t + 1, win).astype(F32)
            acc, step = ddiff / cnt, 1
            while step < win:
                acc = acc + jnp.where(t < s - step, pltpu.roll(acc, s - step, 0), 0.0)
                step *= 2
            du_ref[:, cols] = acc - ddiff

    full = pl.BlockSpec((s, POOL_WIDTH), lambda i: (0, 0))
    return pl.pallas_call(
        body, name=name, grid=(1,),
        out_shape=(jax.ShapeDtypeStruct((s, POOL_WIDTH), F32),
                   jax.ShapeDtypeStruct(pool_w.shape, F32),
                   jax.ShapeDtypeStruct((1, POOL_WIDTH), F32)),
        in_specs=[full, full, pl.BlockSpec(pool_w.shape, lambda i: (0, 0, 0)),
                  pl.BlockSpec((1, POOL_WIDTH), lambda i: (0, 0))],
        out_specs=(full, pl.BlockSpec(pool_w.shape, lambda i: (0, 0, 0)),
                   pl.BlockSpec((1, POOL_WIDTH), lambda i: (0, 0))),
        compiler_params=_params(("arbitrary",)),
    )(dycat, diff, pool_w, pool_scale)


def _rope_tables(positions, name):
    s = positions.shape[0]
    ts = _tile(s, 512)
    freq = 1.0 / (ROPE_THETA ** (np.arange(0, QK_ROPE, 2, dtype=np.float32) / QK_ROPE))
    table = np.zeros((1, LANE), np.float32)
    table[0, :QK_ROPE // 2] = freq
    table[0, QK_ROPE // 2:QK_ROPE] = freq

    def body(pos_ref, f_ref, cos_ref, sin_ref):
        ang = pos_ref[...].astype(F32) * f_ref[...]
        cos_ref[...] = jnp.cos(ang)
        sin_ref[...] = jnp.sin(ang)

    out = jax.ShapeDtypeStruct((s, LANE), F32)
    blk = pl.BlockSpec((ts, LANE), lambda i: (i, 0))
    return pl.pallas_call(
        body, name=name, grid=(s // ts,), out_shape=(out, out),
        in_specs=[pl.BlockSpec((ts, 1), lambda i: (i, 0)), _vec_spec(LANE)], out_specs=(blk, blk),
        compiler_params=_params(("parallel",)),
    )(positions, jnp.asarray(table))


def _lane_mod64_low(shape):
    return (lax.broadcasted_iota(jnp.int32, shape, 1) % QK_ROPE) < (QK_ROPE // 2)


def _rope(x, cos, sin):
    rot = jnp.where(_lane_mod64_low(x.shape), -pltpu.roll(x, LANE - 32, 1), pltpu.roll(x, 32, 1))
    return x * cos + rot * sin


def _rope_t(dy, cos, sin):
    w = dy * sin
    rot_t = jnp.where(_lane_mod64_low(dy.shape), pltpu.roll(w, LANE - 32, 1), -pltpu.roll(w, 32, 1))
    return dy * cos + rot_t


def _plain_rms(x, g):
    r = lax.rsqrt(jnp.mean(x * x, axis=-1, keepdims=True) + EPS)
    return (x * r) * g, x * r, r


O_Q, O_KV, O_KR = POOL_WIDTH, POOL_WIDTH + Q_LORA, POOL_WIDTH + Q_LORA + KV_LORA


def _qkv_fwd(z, qn, kvn, wq, wkv, cos, sin, name):
    s = z.shape[0]
    ts = _tile(s, 512)

    def body(z_ref, qn_ref, kvn_ref, wq_ref, wkv_ref, cos_ref, sin_ref, q_ref, k_ref, v_ref, cqn_ref, ckvn_ref):
        cosv, sinv = cos_ref[...], sin_ref[...]
        cqn = _plain_rms(z_ref[:, O_Q:O_KV], qn_ref[...])[0].astype(BF16)
        ckvn = _plain_rms(z_ref[:, O_KV:O_KR], kvn_ref[...])[0].astype(BF16)
        cqn_ref[...] = cqn
        ckvn_ref[...] = ckvn
        nt = (((1,), (1,)), ((), ()))
        q = lax.dot_general(cqn, wq_ref[...], nt, preferred_element_type=F32)
        kv = lax.dot_general(ckvn, wkv_ref[...], nt, preferred_element_type=F32)
        kr = _rope(z_ref[:, O_KR:IN_PAD], cosv, sinv).astype(BF16)
        for h in range(N_HEADS):
            o = h * HEAD_PAD
            q_ref[:, o:o + QK_NOPE] = q[:, o:o + QK_NOPE].astype(BF16)
            q_ref[:, o + QK_NOPE:o + HEAD_PAD] = _rope(q[:, o + QK_NOPE:o + HEAD_PAD], cosv, sinv).astype(BF16)
            k_ref[:, o:o + QK_NOPE] = kv[:, o:o + QK_NOPE].astype(BF16)
            k_ref[:, o + QK_NOPE:o + HEAD_PAD] = kr
            v_ref[:, h * V_HEAD:(h + 1) * V_HEAD] = kv[:, o + QK_NOPE:o + HEAD_PAD].astype(BF16)

    def row(w):
        return pl.BlockSpec((ts, w), lambda i: (i, 0))

    def whole(arr):
        return pl.BlockSpec(arr.shape, lambda i: (0, 0))

    hp = N_HEADS * HEAD_PAD
    return pl.pallas_call(
        body, name=name, grid=(s // ts,),
        out_shape=(jax.ShapeDtypeStruct((s, hp), BF16), jax.ShapeDtypeStruct((s, hp), BF16),
                   jax.ShapeDtypeStruct((s, N_HEADS * V_HEAD), BF16),
                   jax.ShapeDtypeStruct((s, Q_LORA), BF16), jax.ShapeDtypeStruct((s, KV_LORA), BF16)),
        in_specs=[row(IN_PAD), whole(qn), whole(kvn), whole(wq), whole(wkv), row(LANE), row(LANE)],
        out_specs=(row(hp), row(hp), row(N_HEADS * V_HEAD), row(Q_LORA), row(KV_LORA)),
        compiler_params=_params(("parallel",)),
    )(z, qn, kvn, wq, wkv, cos, sin)


def _qkv_bwd(dq, dk, dv, du, z, qn, kvn, wq, wkv, cos, sin, name):
    s = z.shape[0]
    ts = _tile(s, 512)

    def norm_bwd(x, g, dy):
        _, xn, r = _plain_rms(x, g)
        dxn = dy * g
        return r * (dxn - xn * jnp.mean(dxn * xn, axis=-1, keepdims=True)), jnp.sum(dy * xn, axis=0, keepdims=True)

    def body(dq_ref, dk_ref, dv_ref, du_ref, z_ref, qn_ref, kvn_ref, wq_ref, wkv_ref, cos_ref, sin_ref,
             dz_ref, dqb_ref, dkvb_ref, dqn_ref, dkvn_ref):
        @pl.when(pl.program_id(0) == 0)
        def _():
            dqn_ref[...] = jnp.zeros_like(dqn_ref)
            dkvn_ref[...] = jnp.zeros_like(dkvn_ref)

        cosv, sinv = cos_ref[...], sin_ref[...]
        dkr = jnp.zeros((ts, LANE), F32)
        for h in range(N_HEADS):
            o = h * HEAD_PAD
            dqb_ref[:, o:o + QK_NOPE] = dq_ref[:, o:o + QK_NOPE].astype(BF16)
            dqb_ref[:, o + QK_NOPE:o + HEAD_PAD] = _rope_t(dq_ref[:, o + QK_NOPE:o + HEAD_PAD], cosv, sinv).astype(BF16)
            dkvb_ref[:, o:o + QK_NOPE] = dk_ref[:, o:o + QK_NOPE].astype(BF16)
            dkvb_ref[:, o + QK_NOPE:o + HEAD_PAD] = dv_ref[:, h * V_HEAD:(h + 1) * V_HEAD].astype(BF16)
            dkr = dkr + dk_ref[:, o + QK_NOPE:o + HEAD_PAD]
        dcqn = jnp.dot(dqb_ref[...], wq_ref[...], preferred_element_type=F32)
        dckvn = jnp.dot(dkvb_ref[...], wkv_ref[...], preferred_element_type=F32)
        dcq, dqn = norm_bwd(z_ref[:, O_Q:O_KV], qn_ref[...], dcqn)
        dckv, dkvn = norm_bwd(z_ref[:, O_KV:O_KR], kvn_ref[...], dckvn)
        dqn_ref[...] += dqn
        dkvn_ref[...] += dkvn
        dz_ref[:, 0:O_Q] = du_ref[...].astype(BF16)
        dz_ref[:, O_Q:O_KV] = dcq.astype(BF16)
        dz_ref[:, O_KV:O_KR] = dckv.astype(BF16)
        dz_ref[:, O_KR:IN_PAD] = _rope_t(dkr, cosv, sinv).astype(BF16)

    def row(w):
        return pl.BlockSpec((ts, w), lambda i: (i, 0))

    def whole(arr):
        return pl.BlockSpec(arr.shape, lambda i: (0, 0))

    hp = N_HEADS * HEAD_PAD
    return pl.pallas_call(
        body, name=name, grid=(s // ts,),
        out_shape=(jax.ShapeDtypeStruct((s, IN_PAD), BF16), jax.ShapeDtypeStruct((s, hp), BF16),
                   jax.ShapeDtypeStruct((s, hp), BF16),
                   jax.ShapeDtypeStruct((1, Q_LORA), F32), jax.ShapeDtypeStruct((1, KV_LORA), F32)),
        in_specs=[row(hp), row(hp), row(N_HEADS * V_HEAD), row(POOL_WIDTH), row(IN_PAD),
                  whole(qn), whole(kvn), whole(wq), whole(wkv), row(LANE), row(LANE)],
        out_specs=(row(IN_PAD), row(hp), row(hp), whole(qn), whole(kvn)),
        compiler_params=_params(("arbitrary",)),
    )(dq, dk, dv, du, z, qn, kvn, wq, wkv, cos, sin)


def _causal_scores(q, k, i, tq, klen):
    sc = lax.dot_general(q, k, (((1,), (1,)), ((), ())), preferred_element_type=F32) * SOFTMAX_SCALE
    qpos = i * tq + lax.broadcasted_iota(jnp.int32, (tq, klen), 0)
    kpos = lax.broadcasted_iota(jnp.int32, (tq, klen), 1)
    return jnp.where(qpos >= kpos, sc, -jnp.inf)


ATTN_TQ = 512
ATTN_SEGMENTS = 4


def _by_key_prefix(i, nq, tq, compute):
    nseg = min(ATTN_SEGMENTS, nq)
    per = nq // nseg
    for r in range(nseg):
        pl.when(i // per == r)(lambda r=r: compute((r + 1) * per * tq))


def _attn_fwd(q, k, v, name):
    s = q.shape[0]
    tq = _tile(s, ATTN_TQ)
    nq = s // tq

    def body(q_ref, k_ref, v_ref, o_ref, lse_ref):
        i = pl.program_id(1)

        def compute(klen):
            sc = _causal_scores(q_ref[...], k_ref[0:klen, :], i, tq, klen)
            mx = jnp.max(sc, axis=-1, keepdims=True)
            p = jnp.exp(sc - mx)
            den = jnp.sum(p, axis=-1, keepdims=True)
            o_ref[...] = jnp.dot((p / den).astype(BF16), v_ref[0:klen, :], preferred_element_type=F32)
            lse_ref[...] = mx + jnp.log(den)

        _by_key_prefix(i, nq, tq, compute)

    return pl.pallas_call(
        body, name=name, grid=(N_HEADS, s // tq),
        out_shape=(jax.ShapeDtypeStruct((s, N_HEADS * V_HEAD), F32), jax.ShapeDtypeStruct((N_HEADS, s, 1), F32)),
        in_specs=[pl.BlockSpec((tq, HEAD_PAD), lambda h, i: (i, h)),
                  pl.BlockSpec((s, HEAD_PAD), lambda h, i: (0, h)),
                  pl.BlockSpec((s, V_HEAD), lambda h, i: (0, h))],
        out_specs=(pl.BlockSpec((tq, V_HEAD), lambda h, i: (i, h)),
                   pl.BlockSpec((None, tq, 1), lambda h, i: (h, i, 0))),
        compiler_params=_params(("parallel", "parallel")),
    )(q, k, v)


def _attn_bwd(q, k, v, lse, dycat, name):
    s = q.shape[0]
    tq = _tile(s, ATTN_TQ)
    nq = s // tq
    tn_dims = (((0,), (0,)), ((), ()))

    def body(q_ref, k_ref, v_ref, lse_ref, do_ref, dq_ref, dk_ref, dv_ref):
        i = pl.program_id(1)

        @pl.when(i == 0)
        def _():
            dk_ref[...] = jnp.zeros_like(dk_ref)
            dv_ref[...] = jnp.zeros_like(dv_ref)

        def compute(klen):
            qv, kv_, dob = q_ref[...], k_ref[0:klen, :], do_ref[...].astype(BF16)
            sc = _causal_scores(qv, kv_, i, tq, klen)
            p = jnp.exp(sc - lse_ref[...])
            dp = lax.dot_general(dob, v_ref[0:klen, :], (((1,), (1,)), ((), ())), preferred_element_type=F32)
            ds = (p * (dp - jnp.sum(dp * p, axis=-1, keepdims=True)) * SOFTMAX_SCALE).astype(BF16)
            dq_ref[...] = jnp.dot(ds, kv_, preferred_element_type=F32)
            dk_ref[0:klen, :] += lax.dot_general(ds, qv, tn_dims, preferred_element_type=F32)
            dv_ref[0:klen, :] += lax.dot_general(p.astype(BF16), dob, tn_dims, preferred_element_type=F32)

        _by_key_prefix(i, nq, tq, compute)

    n_pool_blocks = POOL_WIDTH // V_HEAD
    return pl.pallas_call(
        body, name=name, grid=(N_HEADS, s // tq),
        out_shape=(jax.ShapeDtypeStruct((s, N_HEADS * HEAD_PAD), F32),
                   jax.ShapeDtypeStruct((s, N_HEADS * HEAD_PAD), F32),
                   jax.ShapeDtypeStruct((s, N_HEADS * V_HEAD), F32)),
        in_specs=[pl.BlockSpec((tq, HEAD_PAD), lambda h, i: (i, h)),
                  pl.BlockSpec((s, HEAD_PAD), lambda h, i: (0, h)),
                  pl.BlockSpec((s, V_HEAD), lambda h, i: (0, h)),
                  pl.BlockSpec((None, tq, 1), lambda h, i: (h, i, 0)),
                  pl.BlockSpec((tq, V_HEAD), lambda h, i: (i, n_pool_blocks + h))],
        out_specs=(pl.BlockSpec((tq, HEAD_PAD), lambda h, i: (i, h)),
                   pl.BlockSpec((s, HEAD_PAD), lambda h, i: (0, h)),
                   pl.BlockSpec((s, V_HEAD), lambda h, i: (0, h))),
        compiler_params=_params(("parallel", "arbitrary")),
    )(q, k, v, lse, dycat)


def _loss_head(x, gw, target, below, name):
    s, d = x.shape
    ts = _tile(s, 256)
    factor = below[2]

    def body(x_ref, gw_ref, tgt_ref, yb_ref, gb_ref, loss_ref, dx_ref, dgw_ref, dyb_ref, dgb_ref):
        @pl.when(pl.program_id(0) == 0)
        def _():
            loss_ref[...] = jnp.zeros_like(loss_ref)
            dgw_ref[...] = jnp.zeros_like(dgw_ref)
            dgb_ref[...] = jnp.zeros_like(dgb_ref)

        xv, gwv = x_ref[...], gw_ref[...]
        r = lax.rsqrt(jnp.mean(xv * xv, axis=-1, keepdims=True) + EPS)
        xn = xv * r
        err = xn * gwv - tgt_ref[...]
        loss_ref[...] += 0.5 * jnp.sum(jnp.mean(err * err, axis=-1, keepdims=True))
        dy = err / d
        dgw_ref[...] += jnp.sum(dy * xn, axis=0, keepdims=True)
        dxn = dy * gwv
        dx = r * (dxn - xn * jnp.mean(dxn * xn, axis=-1, keepdims=True))
        dx_ref[...] = dx
        dyb_ref[...] = ((factor * gb_ref[...]) * dx).astype(BF16)
        dgb_ref[...] += jnp.sum((factor * dx) * yb_ref[...].astype(F32), axis=0, keepdims=True)

    row = pl.BlockSpec((ts, d), lambda i: (i, 0))
    gate_spec, gate_arg = _vec_in(below[1])
    vec = jax.ShapeDtypeStruct((1, d), F32)
    return pl.pallas_call(
        body, name=name, grid=(s // ts,),
        out_shape=(jax.ShapeDtypeStruct((8, LANE), F32), jax.ShapeDtypeStruct((s, d), F32), vec,
                   jax.ShapeDtypeStruct((s, d), BF16), vec),
        in_specs=[row, _vec_spec(d), row, row, gate_spec],
        out_specs=(pl.BlockSpec((8, LANE), lambda i: (0, 0)), row, _vec_spec(d), row, _vec_spec(d)),
        compiler_params=_params(("arbitrary",)),
    )(x, gw, target, below[0], gate_arg)


def _ada_mod(c_all, ada_w, ada_b, name):
    nl, d, cols = ada_w.shape

    def body(c_ref, w_ref, b_ref, o_ref):
        cv = c_ref[...]
        act = (cv * jax.nn.sigmoid(cv)).astype(BF16)
        o_ref[...] = jnp.dot(act, w_ref[...].astype(BF16), preferred_element_type=F32) + b_ref[...]

    return pl.pallas_call(
        body, name=name, grid=(nl,), out_shape=jax.ShapeDtypeStruct((nl, N_DEV, cols), F32),
        in_specs=[pl.BlockSpec((N_DEV, d), lambda l: (0, 0)),
                  pl.BlockSpec((None, d, cols), lambda l: (l, 0, 0)),
                  pl.BlockSpec((None, 1, cols), lambda l: (l, 0, 0))],
        out_specs=pl.BlockSpec((None, N_DEV, cols), lambda l: (l, 0, 0)),
        compiler_params=_params(("parallel",)),
    )(c_all, ada_w, ada_b)


def _ada_grad(c_pad, dmod_pad, name):
    nl, kpad, cols = dmod_pad.shape
    d = c_pad.shape[1]

    def body(c_ref, dm_ref, o_ref):
        cv = c_ref[...]
        act = (cv * jax.nn.sigmoid(cv)).astype(BF16)
        o_ref[...] = lax.dot_general(act, dm_ref[...].astype(BF16), (((0,), (0,)), ((), ())),
                                     preferred_element_type=F32)

    return pl.pallas_call(
        body, name=name, grid=(nl,), out_shape=jax.ShapeDtypeStruct((nl, d, cols), F32),
        in_specs=[pl.BlockSpec((kpad, d), lambda l: (0, 0)),
                  pl.BlockSpec((None, kpad, cols), lambda l: (l, 0, 0))],
        out_specs=pl.BlockSpec((None, d, cols), lambda l: (l, 0, 0)),
        compiler_params=_params(("parallel",)),
    )(c_pad, dmod_pad)


def _adamw_math(w, g, m, v):
    nm = ADAM_B1 * m + (1.0 - ADAM_B1) * g
    nv = ADAM_B2 * v + (1.0 - ADAM_B2) * (g * g)
    m_hat = nm / (1.0 - ADAM_B1 ** ADAM_STEP)
    v_hat = nv / (1.0 - ADAM_B2 ** ADAM_STEP)
    return -ADAM_LR * (m_hat / (jnp.sqrt(v_hat) + ADAM_EPS) + ADAM_WD * w), nm, nv


def _adamw_rows(w3, gbuf, row_off, m3, v3, name):
    nl, r, d = w3.shape
    tr = _row_tile(math.gcd(r, row_off) if row_off else r, 352)
    first = row_off // tr

    def body(w_ref, g_ref, m_ref, v_ref, go_ref, d_ref, nm_ref, nv_ref):
        gv = g_ref[...]
        go_ref[...] = gv
        d_ref[...], nm_ref[...], nv_ref[...] = _adamw_math(w_ref[...], gv, m_ref[...], v_ref[...])

    blk = pl.BlockSpec((None, tr, d), lambda l, i: (l, i, 0))
    gblk = pl.BlockSpec((None, tr, d), lambda l, i: (l, first + i, 0))
    out = jax.ShapeDtypeStruct((nl, r, d), F32)
    return pl.pallas_call(
        body, name=name, grid=(nl, r // tr), out_shape=(out, out, out, out),
        in_specs=[blk, gblk, blk, blk], out_specs=(blk, blk, blk, blk),
        compiler_params=_params(("parallel", "parallel")),
    )(w3, gbuf, m3, v3)


def _adamw(w, g, m, v, name):
    rows, cols = w.shape
    tr = _row_tile(rows, 512)

    def body(w_ref, g_ref, m_ref, v_ref, d_ref, nm_ref, nv_ref):
        d_ref[...], nm_ref[...], nv_ref[...] = _adamw_math(w_ref[...], g_ref[...], m_ref[...], v_ref[...])

    blk = pl.BlockSpec((tr, cols), lambda i: (i, 0))
    out = jax.ShapeDtypeStruct((rows, cols), F32)
    return pl.pallas_call(
        body, name=name, grid=(rows // tr,), out_shape=(out, out, out),
        in_specs=[blk, blk, blk, blk], out_specs=(blk, blk, blk),
        compiler_params=_params(("parallel",)),
    )(w, g, m, v)


def _adamw_nd(w, g, m, v, name):
    shape = w.shape
    flat = (lambda t: t.reshape(1, -1)) if w.ndim == 1 else (lambda t: t.reshape(-1, shape[-1]))
    return tuple(t.reshape(shape) for t in _adamw(flat(w), flat(g), flat(m), flat(v), name))


def _pad_rows(t, rows):
    return jnp.pad(t, ((0, rows - t.shape[0]), (0, 0)))


def _pack_shard_layer(l, wts):
    def tr(name):
        return wts[name][l].astype(BF16).T

    parts = [tr("ffn1_w_gate"), tr("ffn1_w_up"), wts["ffn1_w_down"][l].astype(BF16),
             tr("ffn2_w_gate"), tr("ffn2_w_up"), wts["ffn2_w_down"][l].astype(BF16),
             wts["w_out"][l].astype(BF16),
             tr("w_kv_b").reshape(KV_SH_ROWS, D_MODEL),
             _pad_rows(tr("w_in"), 160),
             _pad_rows(tr("w_q_b").reshape(Q_SH_ROWS, D_MODEL), Q_PAD_ROWS)]
    return jnp.concatenate(parts, axis=0)


def _mixer_weights(w_out, small):
    w = {"out": w_out}
    small = small.reshape(N_DEV, SMALL_ROWS, D_MODEL)
    o_in, o_q = OFF_IN - OFF_KV, OFF_Q - OFF_KV
    w["kv"] = small[:, :KV_SH_ROWS].reshape(N_HEADS * HEAD_PAD, KV_LORA)
    w["in"] = _pad_rows(small[:, o_in:o_in + IN_SH].reshape(IN_COLS, D_MODEL), IN_PAD)
    wq = small[:, o_q:o_q + Q_SH_ROWS].reshape(N_HEADS, QK_HEAD, Q_LORA)
    w["q"] = jnp.pad(wq, ((0, 0), (0, HEAD_PAD - QK_HEAD), (0, 0))).reshape(N_HEADS * HEAD_PAD, Q_LORA)
    return w


def _grad_sources_b(gr):
    gq = gr["q"].reshape(N_HEADS, HEAD_PAD, Q_LORA)[:, :QK_HEAD].reshape(N_DEV, Q_SH_ROWS, D_MODEL)
    small = jnp.concatenate([
        gr["kv"].reshape(N_DEV, KV_SH_ROWS, D_MODEL),
        jnp.pad(gr["in"][:IN_COLS].reshape(N_DEV, IN_SH, D_MODEL), ((0, 0), (0, 160 - IN_SH), (0, 0))),
        jnp.pad(gq, ((0, 0), (0, Q_PAD_ROWS - Q_SH_ROWS), (0, 0)))], axis=1)
    return [gr["g2"], gr["u2"], gr["d2"], gr["out"], small.reshape(N_DEV * SMALL_ROWS, D_MODEL)]


def _pack_bf16_pairs(t):
    rows, d = t.shape
    return lax.bitcast_convert_type(t.astype(BF16).reshape(rows // 2, 2, d).transpose(0, 2, 1), F32)


def _unpack_bf16_pairs(p):
    pairs = jnp.swapaxes(lax.bitcast_convert_type(p, BF16), -1, -2)
    return pairs.reshape(p.shape[:-2] + (2 * p.shape[-2], p.shape[-1]))


def _small_layout(nl):
    names = [("dmod", nl * N_MOD), ("ffn1_norm", nl), ("mix_norm", nl), ("ffn2_norm", nl), ("q_a_norm", nl),
             ("kv_a_norm", nl), ("pool_scale", nl), ("final_norm", 1), ("loss", 1),
             ("pool_w", nl * 4 * POOL_GC * POOL_GC // D_MODEL // 2)]
    off, table = 0, {}
    for name, n in names:
        table[name] = (off, n)
        off += -(-n // 8) * 8
    return table, off


def _to_rows(t, width=D_MODEL):
    n, w = t.shape
    return jnp.pad(t, ((0, -(-n // 8) * 8 - n), (0, width - w)))


def kernel(x, c, positions, ada_w, ada_b, ffn1_norm, ffn1_w_gate, ffn1_w_up, ffn1_w_down, mix_norm, w_in, pool_w, pool_scale, q_a_norm, w_q_b, kv_a_norm, w_kv_b, w_out, ffn2_norm, ffn2_w_gate, ffn2_w_up, ffn2_w_down, final_norm, loss_target, m_ada_w, m_ada_b, m_ffn1_norm, m_ffn1_w_gate, m_ffn1_w_up, m_ffn1_w_down, m_mix_norm, m_w_in, m_pool_w, m_pool_scale, m_q_a_norm, m_w_q_b, m_kv_a_norm, m_w_kv_b, m_w_out, m_ffn2_norm, m_ffn2_w_gate, m_ffn2_w_up, m_ffn2_w_down, m_final_norm, v_ada_w, v_ada_b, v_ffn1_norm, v_ffn1_w_gate, v_ffn1_w_up, v_ffn1_w_down, v_mix_norm, v_w_in, v_pool_w, v_pool_scale, v_q_a_norm, v_w_q_b, v_kv_a_norm, v_w_kv_b, v_w_out, v_ffn2_norm, v_ffn2_w_gate, v_ffn2_w_up, v_ffn2_w_down, v_final_norm):
    wts = dict(ada_w=ada_w, ada_b=ada_b, ffn1_norm=ffn1_norm, ffn1_w_gate=ffn1_w_gate, ffn1_w_up=ffn1_w_up,
               ffn1_w_down=ffn1_w_down, mix_norm=mix_norm, w_in=w_in, pool_w=pool_w, pool_scale=pool_scale,
               q_a_norm=q_a_norm, w_q_b=w_q_b, kv_a_norm=kv_a_norm, w_kv_b=w_kv_b, w_out=w_out,
               ffn2_norm=ffn2_norm, ffn2_w_gate=ffn2_w_gate, ffn2_w_up=ffn2_w_up, ffn2_w_down=ffn2_w_down,
               final_norm=final_norm)
    mom_m = dict(ada_w=m_ada_w, ada_b=m_ada_b, ffn1_norm=m_ffn1_norm, ffn1_w_gate=m_ffn1_w_gate,
                 ffn1_w_up=m_ffn1_w_up, ffn1_w_down=m_ffn1_w_down, mix_norm=m_mix_norm, w_in=m_w_in,
                 pool_w=m_pool_w, pool_scale=m_pool_scale, q_a_norm=m_q_a_norm, w_q_b=m_w_q_b,
                 kv_a_norm=m_kv_a_norm, w_kv_b=m_w_kv_b, w_out=m_w_out, ffn2_norm=m_ffn2_norm,
                 ffn2_w_gate=m_ffn2_w_gate, ffn2_w_up=m_ffn2_w_up, ffn2_w_down=m_ffn2_w_down,
                 final_norm=m_final_norm)
    mom_v = dict(ada_w=v_ada_w, ada_b=v_ada_b, ffn1_norm=v_ffn1_norm, ffn1_w_gate=v_ffn1_w_gate,
                 ffn1_w_up=v_ffn1_w_up, ffn1_w_down=v_ffn1_w_down, mix_norm=v_mix_norm, w_in=v_w_in,
                 pool_w=v_pool_w, pool_scale=v_pool_scale, q_a_norm=v_q_a_norm, w_q_b=v_w_q_b,
                 kv_a_norm=v_kv_a_norm, w_kv_b=v_w_kv_b, w_out=v_w_out, ffn2_norm=v_ffn2_norm,
                 ffn2_w_gate=v_ffn2_w_gate, ffn2_w_up=v_ffn2_w_up, ffn2_w_down=v_ffn2_w_down,
                 final_norm=v_final_norm)
    order = list(wts)
    nl = ada_w.shape[0]
    seq = x.shape[1]
    me = 4 * lax.axis_index("x") + 2 * lax.axis_index("y") + lax.axis_index("c")
    ada_cols = ada_w.shape[2]

    def after_token(t, token):
        return t + token[0:1, 0:1].astype(t.dtype)

    packs = [_pack_shard_layer(l, wts) for l in range(nl)]

    c_all = _all_gather(jnp.broadcast_to(c, (8, D_MODEL)), "gather_c")[::8]

    ada_b_mine = lax.dynamic_slice_in_dim(ada_b, me * ada_cols, ada_cols, axis=1).reshape(nl, 1, ada_cols)
    mod_part = _ada_mod(c_all, ada_w, ada_b_mine, "ada_mod")
    mod_all = _all_gather(mod_part.reshape(nl * N_DEV, ada_cols), "gather_mod")
    mod_all = mod_all.reshape(N_DEV, nl, N_DEV, ada_cols)
    mod = lax.dynamic_index_in_dim(mod_all, me, axis=2, keepdims=False)
    mod = mod.transpose(1, 0, 2).reshape(nl * N_MOD, 1, D_MODEL)
    norm_tables = {name: wts[name].reshape(nl, 1, D_MODEL) for name in ("ffn1_norm", "mix_norm", "ffn2_norm")}

    def modrow(l, k):
        return mod, l * N_MOD + k

    def normrow(name, l):
        return norm_tables[name], l

    flight_a = _gather_start(packs[0][:SPLIT_AB], ROWS_A, mod, "gather_start_0a")
    flight_b = _gather_start(packs[0][OFF_OUT:], ROWS_TAIL, flight_a[4], "gather_start_0b")
    flight_c = _gather_start(packs[0][SPLIT_AB:OFF_OUT], ROWS_A, flight_b[4], "gather_start_0c")
    last_start = flight_c[4]
    if nl > 1:
        in_flight = _gather_start(packs[1], ROWS_ALL, last_start, "gather_start_1")
        last_start = in_flight[4]

    cos, sin = _rope_tables(after_token(positions.reshape(seq, 1), last_start), "rope_tables")

    def vec(t):
        return t.reshape(1, -1)

    def landed(flight, rows_list, after, tag):
        send_sems, recv_sems, pk, lands, _ = flight
        pk, lands = _gather_wait(send_sems, recv_sems, pk, lands, after, f"gather_wait_{tag}")
        return _gather_finish(pk, rows_list, lands, "gather_finish")

    xs = x.reshape(seq, D_MODEL)
    saved = []
    for l in range(nl):
        norm1, up_after = normrow("ffn1_norm", l), None
        if l == 0:
            lands = landed(flight_a, ROWS_A, cos, "0a")
        elif l + 1 < nl:
            in_flight = _gather_start(packs[l + 1], ROWS_ALL, lands[0], f"gather_start_{l + 1}")
            up_after = in_flight[4]
        sv = {}

        def ffn_fwd(xin, norm, k0, wg, wu, wd, tag, after=None):
            h, a, b, t = _ffn_up(xin, norm, modrow(l, k0), modrow(l, k0 + 1), wg, wu, "ffn_up", after=after)
            y, xout = _mm(t, wd, "nn", "ffn_down", res=xin, gate=modrow(l, k0 + 2), gate_factor=0.5)
            sv[tag] = dict(x=xin, h=h, a=a, b=b, t=t, y=y)
            return xout

        xs = ffn_fwd(xs, norm1, 0, lands[0], lands[1], lands[2], "f1", up_after)
        w = dict(zip(("g1", "u1", "d1"), lands[:3]))
        w.update(_mixer_weights(*(landed(flight_b, ROWS_TAIL, xs, "0b") if l == 0 else lands[6:])))
        sv["w"] = w

        h2, z = _norm_mm(xs, normrow("mix_norm", l), modrow(l, 3), modrow(l, 4), w["in"], "mix_in")
        y_pool, diff = _pool_fwd(z, pool_w[l], vec(pool_scale[l]), "pool_fwd")
        q, k, v, cqn, ckvn = _qkv_fwd(z, vec(q_a_norm[l]), vec(kv_a_norm[l]), w["q"], w["kv"], cos, sin, "qkv_fwd")
        o, lse = _attn_fwd(q, k, v, "attn_fwd")
        ycat = jnp.concatenate([y_pool, o.astype(BF16)], axis=1)
        y2, xmix = _mm(ycat, w["out"], "nn", "mix_out", res=xs, gate=modrow(l, 5), gate_factor=1.0)
        sv["mix"] = dict(x=xs, h=h2, z=z, diff=diff, q=q, k=k, v=v, cqn=cqn, ckvn=ckvn, lse=lse, ycat=ycat, y=y2)
        xs = xmix

        w.update(zip(("g2", "u2", "d2"), landed(flight_c, ROWS_A, xs, "0c") if l == 0 else lands[3:6]))
        xs = ffn_fwd(xs, normrow("ffn2_norm", l), 6, w["g2"], w["u2"], w["d2"], "f2")
        saved.append(sv)
        if l + 1 < nl:
            lands = landed(in_flight, ROWS_ALL, xs, l + 1)

    loss_part, dx, d_final, *head = _loss_head(xs, vec(final_norm), loss_target.reshape(seq, D_MODEL),
                                               (saved[nl - 1]["f2"]["y"], modrow(nl - 1, 8), 0.5), "loss_head")

    small = {name: [None] * nl for name in ("ffn1_norm", "mix_norm", "ffn2_norm", "q_a_norm", "kv_a_norm",
                                            "pool_scale", "pool_w", "dmod")}
    core = lax.axis_index("c").astype(jnp.int32).reshape(1)
    chip = 2 * lax.axis_index("x") + lax.axis_index("y")
    exchanges = []

    def leave(srcs, rows_list, after, tag):
        return _pair_start(srcs, rows_list, after, f"pair_start_{tag}"), rows_list, tag

    def forward_on(pending, after, layer, row_off):
        (send_sems, recv_sems, srcs, land, _), rows_list, tag = pending
        srcs, land = _split_wait(send_sems, recv_sems, 1, srcs, land, after, f"pair_wait_{tag}")
        sums = _pair_sum(srcs, rows_list, land, core, "pair_sum")
        flight = _chip_exchange_start(sums, chip, after, f"exchange_start_{tag}")
        exchanges.append((flight, layer, row_off, tag))
        return flight[4]

    pending = None
    for l in reversed(range(nl)):
        sv = saved[l]
        w = sv["w"]
        dmod = [None] * N_MOD
        gr = {}

        def ffn_bwd(dxin, head, s_, norm, k0, wg, wu, wd, tag, below, first_after=None, mid=None):
            dy, dmod[k0 + 2] = head
            da, db, gr["d" + tag], gr["g" + tag], gr["u" + tag] = _ffn_bwd_cols(
                dy, s_["h"], s_["a"], s_["b"], s_["t"], wd, "ffn_bwd_cols", after=first_after)
            dh = _mm_pair(da, wg, db, wu, "ffn_bwd_dh", after=None if mid is None else mid(da))
            outs = _rm_bwd(dh, s_["x"], dxin, norm, modrow(l, k0 + 1), "rm_bwd", below=below)
            dmod[k0], dmod[k0 + 1] = outs[1], outs[2]
            return outs[0], outs[3], outs[4:]

        s_ = sv["mix"]
        dx, small["ffn2_norm"][l], head = ffn_bwd(
            dx, head, sv["f2"], normrow("ffn2_norm", l), 6, w["g2"], w["u2"], w["d2"], "2", (s_["y"], modrow(l, 5), 1.0),
            first_after=None if pending is None else pending[0][4])

        pending_c = leave([gr["g2"], gr["u2"], gr["d2"]], ROWS_A, dx, f"{l}c")
        mix_after = pending_c[0][4]
        if pending is not None:
            mix_after = forward_on(pending, mix_after, l + 1, GB_F1)
            pending = None
        dy, dmod[5] = head
        gr["out"] = _mm(s_["ycat"], dy, "tn", "mix_out_dw", out_dtype=BF16, tm=512, after=mix_after)
        dycat = _mm(dy, w["out"], "nt", "mix_out_dx", tm=1024)
        du, small["pool_w"][l], small["pool_scale"][l] = _pool_bwd(dycat, s_["diff"], pool_w[l], vec(pool_scale[l]), "pool_bwd")
        dq, dk, dv = _attn_bwd(s_["q"], s_["k"], s_["v"], s_["lse"], dycat, "attn_bwd")
        dz, dqb, dkvb, small["q_a_norm"][l], small["kv_a_norm"][l] = _qkv_bwd(
            dq, dk, dv, du, s_["z"], vec(q_a_norm[l]), vec(kv_a_norm[l]), w["q"], w["kv"], cos, sin, "qkv_bwd")
        gr["q"] = _mm(dqb, s_["cqn"], "tn", "q_b_dw", out_dtype=BF16, tm=512, after=forward_on(pending_c, dz, l, GB_F2))
        gr["kv"] = _mm(dkvb, s_["ckvn"], "tn", "kv_b_dw", out_dtype=BF16, tm=512)
        gr["in"] = _mm(dz, s_["h"], "tn", "mix_in_dw", out_dtype=BF16, tm=512)
        dh2 = _mm(dz, w["in"], "nn", "mix_in_dx", tm=1024)
        outs = _rm_bwd(dh2, s_["x"], dx, normrow("mix_norm", l), modrow(l, 4), "rm_bwd",
                       below=(sv["f1"]["y"], modrow(l, 2), 0.5))
        dx, dmod[3], dmod[4], small["mix_norm"][l] = outs[:4]
        head = outs[4:]

        first_after, mid = None, None
        if l == 0:
            pending_b = leave(_grad_sources_b(gr)[3:], ROWS_TAIL, dx, "0b")
            first_after = pending_b[0][4]
            last_groups = []

            def mid(da):
                last_groups.append(leave([gr["g1"], gr["u1"], gr["d1"]], ROWS_A, da, "0a"))
                return forward_on(pending_b, last_groups[0][0][4], 0, GB_TAIL)
        below = (saved[l - 1]["f2"]["y"], modrow(l - 1, 8), 0.5) if l > 0 else None
        dx, small["ffn1_norm"][l], head = ffn_bwd(
            dx, head, sv["f1"], normrow("ffn1_norm", l), 0, w["g1"], w["u1"], w["d1"], "1", below, first_after, mid)

        small["dmod"][l] = jnp.concatenate(dmod, axis=0)
        if l > 0:
            pending = leave([gr["g1"], gr["u1"], gr["d1"]] + _grad_sources_b(gr)[3:], ROWS_A + ROWS_TAIL, dx, l)

    grad_x = dx.reshape(x.shape)
    pending_a = last_groups[0]

    layout, small_rows = _small_layout(nl)
    pieces = {
        "dmod": jnp.concatenate(small["dmod"], axis=0),
        "ffn1_norm": jnp.concatenate(small["ffn1_norm"], axis=0),
        "mix_norm": jnp.concatenate(small["mix_norm"], axis=0),
        "ffn2_norm": jnp.concatenate(small["ffn2_norm"], axis=0),
        "q_a_norm": jnp.concatenate(small["q_a_norm"], axis=0),
        "kv_a_norm": jnp.concatenate(small["kv_a_norm"], axis=0),
        "pool_scale": jnp.concatenate(small["pool_scale"], axis=0),
        "final_norm": d_final,
        "loss": jnp.broadcast_to(loss_part[0:1, 0:1], (1, D_MODEL)),
        "pool_w": _pack_bf16_pairs(jnp.stack(small["pool_w"]).reshape(-1, D_MODEL)),
    }
    small_buf = jnp.concatenate([_to_rows(pieces[name]) for name in layout], axis=0)
    def landed_sums(gbuf, entries, after):
        for (send_sems, recv_sems, sums, recv, _), layer, row_off, tag in entries:
            _, recv = _split_wait(send_sems, recv_sems, N_CHIPS - 1, sums, recv, after, f"exchange_wait_{tag}")
            gbuf = _sum_slots_into(recv, gbuf, layer, row_off, "sum_grads")
        return gbuf

    gbuf = lax.empty((nl, ROWS_L, D_MODEL), F32)
    token_0a = forward_on(pending_a, dx, 0, GB_F1)
    spread = _spread_start(small_buf, me, token_0a, "small_start")
    gbuf = landed_sums(gbuf, [e for e in exchanges if e[3] != "0a"], spread[4])

    def swap(t):
        return t.transpose(0, 2, 1)

    def same(t):
        return t

    grads, updates = {}, {}

    def update_rows(gbuf, table):
        for wname, off, view in table:
            g, d_, nm, nv = _adamw_rows(view(wts[wname]), gbuf, off, view(mom_m[wname]), view(mom_v[wname]), "adamw_rows")
            grads[wname], updates[wname] = view(g), (view(d_), view(nm), view(nv))

    update_rows(gbuf, (("ffn2_w_gate", GB_F2, swap), ("ffn2_w_up", GB_F2 + FF_SH, swap),
                       ("ffn2_w_down", GB_F2 + 2 * FF_SH, same), ("w_out", GB_TAIL, same)))
    small_grads = {
        "w_kv_b": (gbuf[:, OFF_KV:OFF_KV + KV_SH_ROWS].reshape(nl, -1, KV_LORA).transpose(0, 2, 1), same),
        "w_in": (gbuf[:, OFF_IN:OFF_IN + IN_SH], swap),
        "w_q_b": (gbuf[:, OFF_Q:OFF_Q + Q_SH_ROWS].reshape(nl, -1, Q_LORA), swap),
    }
    for wname, (g, view) in small_grads.items():
        upd = _adamw_nd(view(wts[wname]), g, view(mom_m[wname]), view(mom_v[wname]), "adamw")
        grads[wname], updates[wname] = view(g), tuple(view(t) for t in upd)
    gbuf = landed_sums(gbuf, [e for e in exchanges if e[3] == "0a"], updates["w_q_b"][0])
    update_rows(gbuf, (("ffn1_w_gate", GB_F1, swap), ("ffn1_w_up", GB_F1 + FF_SH, swap),
                       ("ffn1_w_down", GB_F1 + 2 * FF_SH, same)))

    _, small_all = _split_wait(spread[0], spread[1], N_DEV - 1, spread[2], spread[3], updates["ffn1_w_down"][0],
                               "small_wait")
    pool_off, pool_rows = layout["pool_w"]
    small_sum = _sum_slots(small_all[:, :pool_off], "sum_small")
    pool_sum = _sum_slots(_unpack_bf16_pairs(small_all[:, pool_off:pool_off + pool_rows]), "sum_pool_w")

    def take(name, width=D_MODEL):
        off, n = layout[name]
        return small_sum[off:off + n, :width]

    late = {"ada_b": take("dmod").reshape(nl, N_MOD * D_MODEL),
            "ffn1_norm": take("ffn1_norm"), "mix_norm": take("mix_norm"), "ffn2_norm": take("ffn2_norm"),
            "q_a_norm": take("q_a_norm", Q_LORA), "kv_a_norm": take("kv_a_norm", KV_LORA),
            "pool_scale": take("pool_scale", POOL_WIDTH), "final_norm": take("final_norm").reshape(D_MODEL),
            "pool_w": pool_sum.reshape(pool_w.shape)}
    loss = take("loss")[0, 0]

    off, n = layout["dmod"]
    dmod_all = small_all[:, off:off + n].reshape(N_DEV, nl, N_MOD * D_MODEL)
    dmod_mine = lax.dynamic_slice_in_dim(dmod_all, me * ada_cols, ada_cols, axis=2)
    dmod_pad = jnp.pad(dmod_mine.transpose(1, 0, 2), ((0, 0), (0, LANE - N_DEV), (0, 0)))
    late["ada_w"] = _ada_grad(jnp.pad(c_all, ((0, LANE - N_DEV), (0, 0))), dmod_pad, "ada_grad")
    for name, g in late.items():
        grads[name], updates[name] = g, _adamw_nd(wts[name], g, mom_m[name], mom_v[name], "adamw")

    return (loss, grad_x, *[grads[n] for n in order], *[updates[n][0] for n in order],
            *[updates[n][1] for n in order], *[updates[n][2] for n in order])
```

```python
import math

import numpy as np
import jax
import jax.numpy as jnp
from jax import lax
from jax.experimental import pallas as pl
from jax.experimental.pallas import tpu as pltpu

F32 = jnp.float32
BF16 = jnp.bfloat16

N_DEV = 8
D_MODEL = 1024
D_FF = 2816
POOL_WIDTH = 512
POOL_WINDOWS = (2, 4, 8, 16)
POOL_GC = 128
N_HEADS = 4
QK_NOPE = 128
QK_ROPE = 64
V_HEAD = 128
QK_HEAD = QK_NOPE + QK_ROPE
HEAD_PAD = 256
Q_LORA = 384
KV_LORA = 256
IN_COLS = POOL_WIDTH + Q_LORA + KV_LORA + QK_ROPE
IN_PAD = 1280
ROPE_THETA = 10000.0
SOFTMAX_SCALE = 1.0 / math.sqrt(QK_HEAD)
EPS = 1e-6
N_MOD = 9

ADAM_LR = 0.001
ADAM_B1 = 0.9
ADAM_B2 = 0.999
ADAM_EPS = 1e-08
ADAM_WD = 0.01
ADAM_STEP = 10

LANE = 128
VMEM_LIMIT = 56 * 1024 * 1024

FF_SH = D_FF // N_DEV
OFF_G1, OFF_U1, OFF_D1 = 0, FF_SH, 2 * FF_SH
OFF_G2, OFF_U2, OFF_D2 = 3 * FF_SH, 4 * FF_SH, 5 * FF_SH
OFF_OUT = 6 * FF_SH
OFF_KV = OFF_OUT + 128
OFF_IN = OFF_KV + 32
OFF_Q = OFF_IN + 160
Q_PAD_ROWS = 64
ROWS_L = OFF_Q + Q_PAD_ROWS
IN_SH = IN_COLS // N_DEV
Q_SH_ROWS = (N_HEADS * QK_HEAD // N_DEV) * Q_LORA // D_MODEL
KV_SH_ROWS = (N_HEADS * (QK_NOPE + V_HEAD) // N_DEV) * KV_LORA // D_MODEL


def _tile(dim, target):
    if dim <= target:
        return dim
    best = None
    for t in range(LANE, target + 1, LANE):
        if dim % t == 0:
            best = t
    assert best is not None, (dim, target)
    return best


def _params(sem):
    return pltpu.CompilerParams(dimension_semantics=sem, vmem_limit_bytes=VMEM_LIMIT)


def _mesh_pos():
    return lax.axis_index("x"), lax.axis_index("y"), lax.axis_index("c")


def _all_gather(x, name):
    m, n = x.shape

    def body(x_ref, out_ref, send_sems, recv_sems, local_sem):
        px, py, pc = _mesh_pos()
        me, sibling = (px, py, pc), (px, py, 1 - pc)
        chips = [(1 - px, py), (px, 1 - py), (1 - px, 1 - py)]

        def rows(bx, by, bc):
            return out_ref.at[pl.ds((4 * bx + 2 * by + bc) * m, m), :]

        def copy(k, block, to, src=None):
            return pltpu.make_async_remote_copy(
                src_ref=rows(*block) if src is None else src, dst_ref=rows(*block),
                send_sem=send_sems.at[k], recv_sem=recv_sems.at[k],
                device_id=to, device_id_type=pl.DeviceIdType.MESH)

        mine = pltpu.make_async_copy(x_ref, rows(*me), local_sem)
        mine.start()
        first = [copy(0, me, sibling, src=x_ref)]
        first += [copy(1 + j, me, (*chip, pc), src=x_ref) for j, chip in enumerate(chips)]
        for cp in first:
            cp.start()
        passed = [copy(4 + j, (*chip, pc), sibling) for j, chip in enumerate(chips)]
        for j, chip in enumerate(chips):
            copy(1 + j, (*chip, pc), me).wait_recv()
            passed[j].start()
        copy(0, sibling, me).wait_recv()
        for j, chip in enumerate(chips):
            copy(4 + j, (*chip, 1 - pc), me).wait_recv()
        for cp in first + passed:
            cp.wait_send()
        mine.wait()

    hbm = pl.BlockSpec(memory_space=pltpu.HBM)
    return pl.pallas_call(
        body, name=name,
        out_shape=jax.ShapeDtypeStruct((N_DEV * m, n), x.dtype),
        in_specs=[hbm], out_specs=hbm,
        scratch_shapes=[pltpu.SemaphoreType.DMA((7,)), pltpu.SemaphoreType.DMA((7,)),
                        pltpu.SemaphoreType.DMA],
    )(x)


SMALL_ROWS = ROWS_L - OFF_KV
ROWS_A = [FF_SH] * 3
ROWS_B = [FF_SH] * 3 + [128, SMALL_ROWS]
ROWS_ALL = ROWS_A + ROWS_B
SPLIT_AB = sum(ROWS_A)
ROWS_TAIL = [128, SMALL_ROWS]
GB_F2, GB_F1, GB_TAIL = 0, SPLIT_AB, 2 * SPLIT_AB
HBM_SPEC = pl.BlockSpec(memory_space=pltpu.HBM)
SEM_SPEC = pl.BlockSpec(memory_space=pltpu.SEMAPHORE)
ANY_SPEC = pl.BlockSpec(memory_space=pl.ANY)
EFFECT = pltpu.SideEffectType.DATAFLOW_SIDE_EFFECTING


def _hbm(t):
    return pltpu.with_memory_space_constraint(t, pltpu.HBM)


def _whole_wait(ref, send_sem, recv_sem, peer):
    return pltpu.make_async_remote_copy(src_ref=ref, dst_ref=ref, send_sem=send_sem, recv_sem=recv_sem,
                                        device_id=peer, device_id_type=pl.DeviceIdType.MESH)


def _offsets(rows_list):
    return [sum(rows_list[:i]) for i in range(len(rows_list))]


def _gather_start(packed, rows_list, after, name):
    n = len(rows_list)
    offs = _offsets(rows_list)
    lands = [_hbm(lax.empty((N_DEV * rows, D_MODEL), BF16)) for rows in rows_list]

    def body(packed_ref, *refs):
        land = refs[:n]
        send_sems, recv_sems = refs[n + 1], refs[n + 2]
        token = refs[-1]
        px, py, pc = _mesh_pos()
        me = 4 * px + 2 * py + pc
        peers = [(px, py, 1 - pc), (1 - px, py, pc), (px, 1 - py, pc), (1 - px, 1 - py, pc)]
        for k, peer in enumerate(peers):
            for off, rows, land_ref in zip(offs, rows_list, land):
                pltpu.make_async_remote_copy(
                    src_ref=packed_ref.at[pl.ds(off, rows), :], dst_ref=land_ref.at[pl.ds(me * rows, rows), :],
                    send_sem=send_sems.at[k], recv_sem=recv_sems.at[k],
                    device_id=peer, device_id_type=pl.DeviceIdType.MESH).start()
        token[...] = jnp.zeros_like(token)

    outs = pl.pallas_call(
        body, name=name,
        out_shape=(pltpu.SemaphoreType.DMA((4,)), pltpu.SemaphoreType.DMA((4,)), pltpu.HBM(packed.shape, BF16),
                   *[pltpu.HBM(t.shape, BF16) for t in lands], jax.ShapeDtypeStruct((8, LANE), F32)),
        in_specs=(HBM_SPEC,) * (1 + n) + (ANY_SPEC,),
        out_specs=(SEM_SPEC, SEM_SPEC) + (HBM_SPEC,) * (1 + n) + (pl.BlockSpec(memory_space=pltpu.VMEM),),
        input_output_aliases={i: 2 + i for i in range(1 + n)},
        compiler_params=pltpu.CompilerParams(has_side_effects=EFFECT),
    )(_hbm(packed), *lands, after)
    return outs[0], outs[1], outs[2], list(outs[3:3 + n]), outs[-1]


def _gather_wait(send_sems, recv_sems, packed, lands, after, name):
    n = len(lands)

    def body(packed_ref, *refs):
        s_sems, r_sems = refs[n], refs[n + 1]
        me = _mesh_pos()
        for k in range(4):
            cp = _whole_wait(packed_ref, s_sems.at[k], r_sems.at[k], me)
            cp.wait_send()
            cp.wait_recv()

    outs = pl.pallas_call(
        body, name=name,
        out_shape=(pltpu.HBM(packed.shape, BF16), *[pltpu.HBM(t.shape, BF16) for t in lands]),
        in_specs=(HBM_SPEC,) * (1 + n) + (SEM_SPEC, SEM_SPEC, ANY_SPEC),
        out_specs=(HBM_SPEC,) * (1 + n),
        input_output_aliases={i: i for i in range(1 + n)},
        compiler_params=pltpu.CompilerParams(has_side_effects=EFFECT),
    )(packed, *lands, send_sems, recv_sems, after)
    return outs[0], list(outs[1:])


def _gather_finish(packed, rows_list, lands, name):
    n = len(rows_list)
    offs = _offsets(rows_list)

    def body(packed_ref, *refs):
        land = refs[n:2 * n]
        send_sems, recv_sems, stage, stage_sem = refs[2 * n:]
        px, py, pc = _mesh_pos()
        me = 4 * px + 2 * py + pc
        sibling = (px, py, 1 - pc)
        load = pltpu.make_async_copy(packed_ref, stage, stage_sem)
        load.start()
        load.wait()
        for off, rows, land_ref in zip(offs, rows_list, land):
            pltpu.make_async_copy(stage.at[pl.ds(off, rows), :], land_ref.at[pl.ds(me * rows, rows), :],
                                  stage_sem).start()
        for j, (cx, cy) in enumerate([(1 - px, py), (px, 1 - py), (1 - px, 1 - py)]):
            block = 4 * cx + 2 * cy + pc
            for rows, land_ref in zip(rows_list, land):
                blk = land_ref.at[pl.ds(block * rows, rows), :]
                pltpu.make_async_remote_copy(src_ref=blk, dst_ref=blk, send_sem=send_sems.at[j],
                                             recv_sem=recv_sems.at[j], device_id=sibling,
                                             device_id_type=pl.DeviceIdType.MESH).start()
        for j in range(3):
            cp = _whole_wait(packed_ref, send_sems.at[j], recv_sems.at[j], sibling)
            cp.wait_recv()
            cp.wait_send()
        pltpu.make_async_copy(stage, packed_ref, stage_sem).wait()

    outs = pl.pallas_call(
        body, name=name,
        out_shape=tuple(jax.ShapeDtypeStruct(t.shape, BF16) for t in lands),
        in_specs=(HBM_SPEC,) * (1 + n), out_specs=(HBM_SPEC,) * n,
        input_output_aliases={1 + i: i for i in range(n)},
        scratch_shapes=[pltpu.SemaphoreType.DMA((3,)), pltpu.SemaphoreType.DMA((3,)),
                        pltpu.VMEM(packed.shape, BF16), pltpu.SemaphoreType.DMA],
    )(packed, *lands)
    return list(outs)


N_CHIPS = 4


def _pair_start(srcs, rows_list, after, name):
    n = len(rows_list)
    offs = _offsets(rows_list)
    land = lax.empty((N_CHIPS, sum(rows_list), D_MODEL), BF16)

    def body(*refs):
        src, land_ref = refs[:n], refs[n]
        send_sems, recv_sems = refs[n + 2], refs[n + 3]
        token = refs[-1]
        px, py, pc = _mesh_pos()
        for k in range(N_CHIPS):
            block = 2 * k + (1 - pc)
            for off, rows, src_ref in zip(offs, rows_list, src):
                pltpu.make_async_remote_copy(
                    src_ref=src_ref.at[pl.ds(block * rows, rows), :], dst_ref=land_ref.at[k, pl.ds(off, rows), :],
                    send_sem=send_sems.at[0], recv_sem=recv_sems.at[0],
                    device_id=(px, py, 1 - pc), device_id_type=pl.DeviceIdType.MESH).start()
        token[...] = jnp.zeros_like(token)

    outs = pl.pallas_call(
        body, name=name,
        out_shape=(pltpu.SemaphoreType.DMA((1,)), pltpu.SemaphoreType.DMA((1,)),
                   *[pltpu.HBM(t.shape, BF16) for t in srcs], pltpu.HBM(land.shape, BF16),
                   jax.ShapeDtypeStruct((8, LANE), F32)),
        in_specs=(HBM_SPEC,) * (n + 1) + (ANY_SPEC,),
        out_specs=(SEM_SPEC, SEM_SPEC) + (HBM_SPEC,) * (n + 1) + (pl.BlockSpec(memory_space=pltpu.VMEM),),
        input_output_aliases={i: 2 + i for i in range(n + 1)},
        compiler_params=pltpu.CompilerParams(has_side_effects=EFFECT),
    )(*[_hbm(t) for t in srcs], _hbm(land), after)
    return outs[0], outs[1], list(outs[2:2 + n]), outs[2 + n], outs[-1]


def _split_wait(send_sems, recv_sems, n_sems, srcs, land, after, name):
    n = len(srcs)

    def body(*refs):
        land_ref = refs[n]
        s_sems, r_sems = refs[n + 1], refs[n + 2]
        me = _mesh_pos()
        for k in range(n_sems):
            cp = _whole_wait(land_ref.at[0] if n_sems > 1 else land_ref, s_sems.at[k], r_sems.at[k], me)
            cp.wait_send()
            cp.wait_recv()

    outs = pl.pallas_call(
        body, name=name,
        out_shape=(*[pltpu.HBM(t.shape, t.dtype) for t in srcs], pltpu.HBM(land.shape, land.dtype)),
        in_specs=(HBM_SPEC,) * (n + 1) + (SEM_SPEC, SEM_SPEC, ANY_SPEC),
        out_specs=(HBM_SPEC,) * (n + 1),
        input_output_aliases={i: i for i in range(n + 1)},
        compiler_params=pltpu.CompilerParams(has_side_effects=EFFECT),
    )(*srcs, land, send_sems, recv_sems, after)
    return list(outs[:n]), outs[n]


def _spread_start(x, me_id, after, name):
    land = lax.dynamic_update_slice_in_dim(lax.empty((N_DEV,) + x.shape, x.dtype), x[None], me_id, axis=0)

    def body(x_ref, land_ref, after_ref, send_sems, recv_sems, x_thru, land_thru, token):
        px, py, pc = _mesh_pos()
        me = 4 * px + 2 * py + pc
        for k in range(1, N_DEV):
            qx = 1 - px if k & 4 else px
            qy = 1 - py if k & 2 else py
            qc = 1 - pc if k & 1 else pc
            pltpu.make_async_remote_copy(
                src_ref=x_ref, dst_ref=land_ref.at[me], send_sem=send_sems.at[k - 1], recv_sem=recv_sems.at[k - 1],
                device_id=(qx, qy, qc), device_id_type=pl.DeviceIdType.MESH).start()
        token[...] = jnp.zeros_like(token)

    outs = pl.pallas_call(
        body, name=name,
        out_shape=(pltpu.SemaphoreType.DMA((N_DEV - 1,)), pltpu.SemaphoreType.DMA((N_DEV - 1,)),
                   pltpu.HBM(x.shape, x.dtype), pltpu.HBM(land.shape, land.dtype), jax.ShapeDtypeStruct((8, LANE), F32)),
        in_specs=(HBM_SPEC, HBM_SPEC, ANY_SPEC),
        out_specs=(SEM_SPEC, SEM_SPEC, HBM_SPEC, HBM_SPEC, pl.BlockSpec(memory_space=pltpu.VMEM)),
        input_output_aliases={0: 2, 1: 3},
        compiler_params=pltpu.CompilerParams(has_side_effects=EFFECT),
    )(_hbm(x), _hbm(land), after)
    return outs[0], outs[1], [outs[2]], outs[3], outs[4]


def _pair_sum(srcs, rows_list, land, core, name):
    n = len(rows_list)
    offs = _offsets(rows_list)
    total = sum(rows_list)

    def body(core_ref, *refs):
        src, land_ref, out_ref = refs[:n], refs[n], refs[n + 1]
        for off, rows, src_ref in zip(offs, rows_list, src):
            out_ref[pl.ds(off, rows), :] = (src_ref[...].astype(F32)
                                            + land_ref[pl.ds(off, rows), :].astype(F32)).astype(BF16)

    slot = pl.BlockSpec((None, total, D_MODEL), lambda k, c: (k, 0, 0))
    grid_spec = pltpu.PrefetchScalarGridSpec(
        num_scalar_prefetch=1, grid=(N_CHIPS,),
        in_specs=[pl.BlockSpec((rows, D_MODEL), lambda k, c: (2 * k + c[0], 0)) for rows in rows_list] + [slot],
        out_specs=slot)
    return pl.pallas_call(
        body, name=name, grid_spec=grid_spec,
        out_shape=jax.ShapeDtypeStruct((N_CHIPS, total, D_MODEL), BF16),
        compiler_params=_params(("parallel",)),
    )(core, *srcs, land)


def _chip_exchange_start(sums, chip, after, name):
    own = lax.dynamic_index_in_dim(sums, chip, axis=0, keepdims=True)
    recv = lax.dynamic_update_slice_in_dim(lax.empty(sums.shape, BF16), own, chip, axis=0)

    def body(sums_ref, recv_ref, after_ref, send_sems, recv_sems, sums_thru, recv_thru, token):
        px, py, pc = _mesh_pos()
        for k in range(1, N_CHIPS):
            qx = 1 - px if k & 2 else px
            qy = 1 - py if k & 1 else py
            pltpu.make_async_remote_copy(
                src_ref=sums_ref.at[2 * qx + qy], dst_ref=recv_ref.at[2 * px + py],
                send_sem=send_sems.at[k - 1], recv_sem=recv_sems.at[k - 1],
                device_id=(qx, qy, pc), device_id_type=pl.DeviceIdType.MESH).start()
        token[...] = jnp.zeros_like(token)

    outs = pl.pallas_call(
        body, name=name,
        out_shape=(pltpu.SemaphoreType.DMA((N_CHIPS - 1,)), pltpu.SemaphoreType.DMA((N_CHIPS - 1,)),
                   pltpu.HBM(sums.shape, BF16), pltpu.HBM(recv.shape, BF16), jax.ShapeDtypeStruct((8, LANE), F32)),
        in_specs=(HBM_SPEC, HBM_SPEC, ANY_SPEC),
        out_specs=(SEM_SPEC, SEM_SPEC, HBM_SPEC, HBM_SPEC, pl.BlockSpec(memory_space=pltpu.VMEM)),
        input_output_aliases={0: 2, 1: 3},
        compiler_params=pltpu.CompilerParams(has_side_effects=EFFECT),
    )(_hbm(sums), _hbm(recv), after)
    return outs[0], outs[1], [outs[2]], outs[3], outs[4]


def _sum_slots_into(recv, buf, layer, row_off, name):
    slots, r, n = recv.shape
    tr = _row_tile(math.gcd(r, row_off) if row_off else r, 512)
    first = row_off // tr

    def body(in_ref, buf_ref, out_ref):
        acc = in_ref[0].astype(F32)
        for j in range(1, slots):
            acc = acc + in_ref[j].astype(F32)
        out_ref[...] = acc

    return pl.pallas_call(
        body, name=name, grid=(r // tr,), out_shape=jax.ShapeDtypeStruct(buf.shape, F32),
        in_specs=[pl.BlockSpec((slots, tr, n), lambda i: (0, i, 0)), ANY_SPEC],
        out_specs=pl.BlockSpec((None, tr, n), lambda i: (layer, first + i, 0)),
        input_output_aliases={1: 0},
        compiler_params=_params(("parallel",)),
    )(recv, buf)


def _sum_slots(recv, name, after=None):
    _, r, n = recv.shape
    tr = _row_tile(r, 512)

    def body(in_ref, *refs):
        acc = in_ref[0].astype(F32)
        for j in range(1, N_DEV):
            acc = acc + in_ref[j].astype(F32)
        refs[-1][...] = acc

    grid = (r // tr,)
    in_specs, out_spec = [pl.BlockSpec((N_DEV, tr, n), lambda i: (0, i, 0))], pl.BlockSpec((tr, n), lambda i: (i, 0))
    args = [recv]
    if after is not None:
        in_specs.append(ANY_SPEC)
        args.append(after)
    return pl.pallas_call(
        body, name=name, grid=grid,
        out_shape=jax.ShapeDtypeStruct((r, n), F32),
        in_specs=in_specs, out_specs=out_spec,
        compiler_params=_params(("parallel",)),
    )(*args)


def _row_tile(rows, target):
    if rows <= target:
        return rows
    best = None
    for t in range(16, target + 1, 16):
        if rows % t == 0:
            best = t
    assert best is not None, rows
    return best


_DIMS = {"nn": ((1,), (0,)), "nt": ((1,), (1,)), "tn": ((0,), (0,))}


def _mm(a, b, mode, name, out_dtype=F32, res=None, gate=None, gate_factor=1.0, tm=512, tn=1408, after=None):
    assert (res is None) == (gate is None)
    if mode == "tn":
        kdim, m = a.shape
    else:
        m, kdim = a.shape
    n = b.shape[0] if mode == "nt" else b.shape[1]
    tm, tn = _tile(m, tm), _tile(n, tn)
    a_spec = (pl.BlockSpec((kdim, tm), lambda i, j: (0, i)) if mode == "tn"
              else pl.BlockSpec((tm, kdim), lambda i, j: (i, 0)))
    b_spec = (pl.BlockSpec((tn, kdim), lambda i, j: (j, 0)) if mode == "nt"
              else pl.BlockSpec((kdim, tn), lambda i, j: (0, j)))
    o_spec = pl.BlockSpec((tm, tn), lambda i, j: (i, j))
    dims = (_DIMS[mode], ((), ()))
    has_res = res is not None

    def body(a_ref, b_ref, *refs):
        y = lax.dot_general(a_ref[...].astype(BF16), b_ref[...].astype(BF16), dims,
                            preferred_element_type=F32)
        if has_res:
            res_ref, gate_ref = refs[0], refs[1]
            y_ref, o_ref = refs[-2], refs[-1]
            y_ref[...] = y.astype(BF16)
            o_ref[...] = res_ref[...] + (gate_factor * gate_ref[...]) * y
        else:
            refs[-1][...] = y.astype(out_dtype)

    in_specs, args = [a_spec, b_spec], [a, b]
    if has_res:
        gate_spec, gate_arg = _vec_in(gate, tile=tn)
        in_specs += [o_spec, gate_spec]
        args += [res, gate_arg]
        out_shape = (jax.ShapeDtypeStruct((m, n), BF16), jax.ShapeDtypeStruct((m, n), F32))
        out_specs = (o_spec, o_spec)
    else:
        out_shape, out_specs = jax.ShapeDtypeStruct((m, n), out_dtype), o_spec
    if after is not None:
        in_specs.append(ANY_SPEC)
        args.append(after)
    return pl.pallas_call(
        body, name=name, grid=(m // tm, n // tn), out_shape=out_shape,
        in_specs=in_specs, out_specs=out_specs,
        compiler_params=_params(("parallel", "parallel")),
    )(*args)


def _vec_in(v, tile=None):
    if isinstance(v, tuple):
        table, row = v
        if tile is None:
            return pl.BlockSpec((None, 1, table.shape[-1]), lambda *idx: (row, 0, 0)), table
        return pl.BlockSpec((None, 1, tile), lambda i, j: (row, 0, j)), table
    if tile is None:
        return pl.BlockSpec((1, v.shape[-1]), lambda *idx: (0, 0)), v
    return pl.BlockSpec((1, tile), lambda i, j: (0, j)), v


def _vec_spec(width):
    return pl.BlockSpec((1, width), lambda i: (0, 0))


def _rm_bwd(dh, x, dres, gw, scale, name, below=None):
    s, d = x.shape
    ts = _tile(s, 512)
    factor = None if below is None else below[2]

    def body(dh_ref, x_ref, dres_ref, gw_ref, sc_ref, *refs):
        dx_ref, dsh_ref, dsc_ref, dgw_ref = refs[-6:-2] if below is not None else refs[-4:]

        @pl.when(pl.program_id(0) == 0)
        def _():
            dsh_ref[...] = jnp.zeros_like(dsh_ref)
            dsc_ref[...] = jnp.zeros_like(dsc_ref)
            dgw_ref[...] = jnp.zeros_like(dgw_ref)
            if below is not None:
                refs[-1][...] = jnp.zeros_like(refs[-1])

        xv, dhv, gwv = x_ref[...], dh_ref[...], gw_ref[...]
        r = lax.rsqrt(jnp.mean(xv * xv, axis=-1, keepdims=True) + EPS)
        xn = xv * r
        y = xn * gwv
        dsh_ref[...] += jnp.sum(dhv, axis=0, keepdims=True)
        dsc_ref[...] += jnp.sum(dhv * y, axis=0, keepdims=True)
        dy = dhv * (1 + sc_ref[...])
        dgw_ref[...] += jnp.sum(dy * xn, axis=0, keepdims=True)
        dxn = dy * gwv
        dx = dres_ref[...] + r * (dxn - xn * jnp.mean(dxn * xn, axis=-1, keepdims=True))
        dx_ref[...] = dx
        if below is not None:
            yb_ref, gb_ref, dyb_ref, dgb_ref = refs[0], refs[1], refs[-2], refs[-1]
            dyb_ref[...] = ((factor * gb_ref[...]) * dx).astype(BF16)
            dgb_ref[...] += jnp.sum((factor * dx) * yb_ref[...].astype(F32), axis=0, keepdims=True)

    row = pl.BlockSpec((ts, d), lambda i: (i, 0))
    vec = jax.ShapeDtypeStruct((1, d), F32)
    (gw_spec, gw), (sc_spec, scale) = _vec_in(gw), _vec_in(scale)
    in_specs, args = [row, row, row, gw_spec, sc_spec], [dh, x, dres, gw, scale]
    out_shape = [jax.ShapeDtypeStruct((s, d), F32), vec, vec, vec]
    out_specs = [row, _vec_spec(d), _vec_spec(d), _vec_spec(d)]
    if below is not None:
        gate_spec, gate_arg = _vec_in(below[1])
        in_specs += [row, gate_spec]
        args += [below[0], gate_arg]
        out_shape += [jax.ShapeDtypeStruct((s, d), BF16), vec]
        out_specs += [row, _vec_spec(d)]
    return pl.pallas_call(
        body, name=name, grid=(s // ts,), out_shape=tuple(out_shape),
        in_specs=in_specs, out_specs=tuple(out_specs),
        compiler_params=_params(("arbitrary",)),
    )(*args)


def _norm_mm(x, gw, shift, scale, w, name, tm=1024):
    s, d = x.shape
    n = w.shape[0]
    tm = _tile(s, tm)

    def body(x_ref, gw_ref, sh_ref, sc_ref, w_ref, h_ref, z_ref):
        xv = x_ref[...]
        r = lax.rsqrt(jnp.mean(xv * xv, axis=-1, keepdims=True) + EPS)
        hb = (((xv * r) * gw_ref[...]) * (1 + sc_ref[...]) + sh_ref[...]).astype(BF16)
        h_ref[...] = hb
        z_ref[...] = lax.dot_general(hb, w_ref[...], (((1,), (1,)), ((), ())), preferred_element_type=F32)

    row = pl.BlockSpec((tm, d), lambda i: (i, 0))
    return pl.pallas_call(
        body, name=name, grid=(s // tm,),
        out_shape=(jax.ShapeDtypeStruct((s, d), BF16), jax.ShapeDtypeStruct((s, n), F32)),
        in_specs=[row, _vec_in(gw)[0], _vec_in(shift)[0], _vec_in(scale)[0], pl.BlockSpec((n, d), lambda i: (0, 0))],
        out_specs=(row, pl.BlockSpec((tm, n), lambda i: (i, 0))),
        compiler_params=_params(("parallel",)),
    )(x, _vec_in(gw)[1], _vec_in(shift)[1], _vec_in(scale)[1], w)


FFN_TM, FFN_TF = 2048, 256


def _ffn_up(x, gw, shift, scale, wg, wu, name, after=None):
    s, d = x.shape
    f = wg.shape[0]
    tm, tf = _tile(s, FFN_TM), _tile(f, FFN_TF)
    nt = (((1,), (1,)), ((), ()))

    def body(x_ref, gw_ref, sh_ref, sc_ref, wg_ref, wu_ref, *refs):
        h_ref, a_ref, b_ref, t_ref = refs[-4:]

        @pl.when(pl.program_id(1) == 0)
        def _():
            xv = x_ref[...]
            r = lax.rsqrt(jnp.mean(xv * xv, axis=-1, keepdims=True) + EPS)
            h_ref[...] = (((xv * r) * gw_ref[...]) * (1 + sc_ref[...]) + sh_ref[...]).astype(BF16)

        hb = h_ref[...]
        av = lax.dot_general(hb, wg_ref[...], nt, preferred_element_type=F32)
        bv = lax.dot_general(hb, wu_ref[...], nt, preferred_element_type=F32)
        a_ref[...] = av.astype(BF16)
        b_ref[...] = bv.astype(BF16)
        t_ref[...] = ((av * jax.nn.sigmoid(av)) * bv).astype(BF16)

    row = pl.BlockSpec((tm, d), lambda i, j: (i, 0))
    wblk = pl.BlockSpec((tf, d), lambda i, j: (j, 0))
    blk = pl.BlockSpec((tm, tf), lambda i, j: (i, j))
    wide = jax.ShapeDtypeStruct((s, f), BF16)
    vec_specs, vec_args = zip(*[_vec_in(v) for v in (gw, shift, scale)])
    in_specs, args = [row, *vec_specs, wblk, wblk], [x, *vec_args, wg, wu]
    if after is not None:
        in_specs.append(ANY_SPEC)
        args.append(after)
    return pl.pallas_call(
        body, name=name, grid=(s // tm, f // tf),
        out_shape=(jax.ShapeDtypeStruct((s, d), BF16), wide, wide, wide),
        in_specs=in_specs, out_specs=(row, blk, blk, blk),
        compiler_params=_params(("parallel", "arbitrary")),
    )(*args)


def _ffn_bwd_cols(dy, h, a, b, t, wd, name, after=None):
    s, d = dy.shape
    f = wd.shape[0]
    tf = _tile(f, FFN_TF)
    nt = (((1,), (1,)), ((), ()))
    tn = (((0,), (0,)), ((), ()))

    def body(dy_ref, h_ref, a_ref, b_ref, t_ref, wd_ref, *refs):
        da_ref, db_ref, gd_ref, gg_ref, gu_ref = refs[-5:]
        dyb, hb = dy_ref[...], h_ref[...]
        dtv = lax.dot_general(dyb, wd_ref[...], nt, preferred_element_type=F32)
        av, bv = a_ref[...].astype(F32), b_ref[...].astype(F32)
        sg = jax.nn.sigmoid(av)
        dbv = (dtv * (av * sg)).astype(BF16)
        dav = ((dtv * bv) * (sg * (1 + av * (1 - sg)))).astype(BF16)
        da_ref[...] = dav
        db_ref[...] = dbv
        gd_ref[...] = lax.dot_general(t_ref[...], dyb, tn, preferred_element_type=F32).astype(BF16)
        gg_ref[...] = lax.dot_general(dav, hb, tn, preferred_element_type=F32).astype(BF16)
        gu_ref[...] = lax.dot_general(dbv, hb, tn, preferred_element_type=F32).astype(BF16)

    whole = pl.BlockSpec((s, d), lambda j: (0, 0))
    col = pl.BlockSpec((s, tf), lambda j: (0, j))
    wblk = pl.BlockSpec((tf, d), lambda j: (j, 0))
    wide, wgrad = jax.ShapeDtypeStruct((s, f), BF16), jax.ShapeDtypeStruct((f, d), BF16)
    in_specs, args = [whole, whole, col, col, col, wblk], [dy, h, a, b, t, wd]
    if after is not None:
        in_specs.append(ANY_SPEC)
        args.append(after)
    return pl.pallas_call(
        body, name=name, grid=(f // tf,), out_shape=(wide, wide, wgrad, wgrad, wgrad),
        in_specs=in_specs, out_specs=(col, col, wblk, wblk, wblk),
        compiler_params=_params(("parallel",)),
    )(*args)


def _mm_pair(a1, b1, a2, b2, name, tm=1024, tn=512, after=None):
    m, kdim = a1.shape
    n = b1.shape[1]
    tm, tn = _tile(m, tm), _tile(n, tn)

    def body(a1_ref, b1_ref, a2_ref, b2_ref, *refs):
        refs[-1][...] = (jnp.dot(a1_ref[...], b1_ref[...], preferred_element_type=F32)
                         + jnp.dot(a2_ref[...], b2_ref[...], preferred_element_type=F32))

    a_spec = pl.BlockSpec((tm, kdim), lambda i, j: (i, 0))
    b_spec = pl.BlockSpec((kdim, tn), lambda i, j: (0, j))
    in_specs, args = [a_spec, b_spec, a_spec, b_spec], [a1, b1, a2, b2]
    if after is not None:
        in_specs.append(ANY_SPEC)
        args.append(after)
    return pl.pallas_call(
        body, name=name, grid=(m // tm, n // tn), out_shape=jax.ShapeDtypeStruct((m, n), F32),
        in_specs=in_specs, out_specs=pl.BlockSpec((tm, tn), lambda i, j: (i, j)),
        compiler_params=_params(("parallel", "parallel")),
    )(*args)


def _pool_counts(s):
    return (lax.broadcasted_iota(jnp.int32, (s, POOL_GC), 0))


def _pool_fwd(z, pool_w, pool_scale, name):
    s = z.shape[0]

    def body(u_ref, w_ref, sc_ref, y_ref, diff_ref):
        t = lax.broadcasted_iota(jnp.int32, (s, POOL_GC), 0)
        for g, win in enumerate(POOL_WINDOWS):
            cols = slice(g * POOL_GC, (g + 1) * POOL_GC)
            u = u_ref[:, cols]
            acc, step = u, 1
            while step < win:
                acc = acc + jnp.where(t >= step, pltpu.roll(acc, step, 0), 0.0)
                step *= 2
            cnt = jnp.minimum(t + 1, win).astype(F32)
            diff = acc / cnt - u
            diff_ref[:, cols] = diff
            ypre = jnp.dot(diff.astype(BF16), w_ref[g].astype(BF16), preferred_element_type=F32)
            y_ref[:, cols] = (ypre * sc_ref[:, cols]).astype(BF16)

    return pl.pallas_call(
        body, name=name, grid=(1,),
        out_shape=(jax.ShapeDtypeStruct((s, POOL_WIDTH), BF16), jax.ShapeDtypeStruct((s, POOL_WIDTH), F32)),
        in_specs=[pl.BlockSpec((s, POOL_WIDTH), lambda i: (0, 0)),
                  pl.BlockSpec(pool_w.shape, lambda i: (0, 0, 0)),
                  pl.BlockSpec((1, POOL_WIDTH), lambda i: (0, 0))],
        out_specs=(pl.BlockSpec((s, POOL_WIDTH), lambda i: (0, 0)),
                   pl.BlockSpec((s, POOL_WIDTH), lambda i: (0, 0))),
        compiler_params=_params(("arbitrary",)),
    )(z, pool_w, pool_scale)


def _pool_bwd(dycat, diff, pool_w, pool_scale, name):
    s = diff.shape[0]

    def body(dy_ref, diff_ref, w_ref, sc_ref, du_ref, dw_ref, dsc_ref):
        t = lax.broadcasted_iota(jnp.int32, (s, POOL_GC), 0)
        for g, win in enumerate(POOL_WINDOWS):
            cols = slice(g * POOL_GC, (g + 1) * POOL_GC)
            dy, dfb, wb = dy_ref[:, cols], diff_ref[:, cols].astype(BF16), w_ref[g].astype(BF16)
            ypre = jnp.dot(dfb, wb, preferred_element_type=F32)
            dsc_ref[:, cols] = jnp.sum(dy * ypre, axis=0, keepdims=True)
            dypre = (dy * sc_ref[:, cols]).astype(BF16)
            ddiff = lax.dot_general(dypre, wb, (((1,), (1,)), ((), ())), preferred_element_type=F32)
            dw_ref[g] = lax.dot_general(dfb, dypre, (((0,), (0,)), ((), ())), preferred_element_type=F32)
            cnt = jnp.minimum(t + 1, win).astype(F32)
            acc, step = ddiff / cnt, 1
            while step < win:
                acc = acc + jnp.where(t < s - step, pltpu.roll(acc, s - step, 0), 0.0)
                step *= 2
            du_ref[:, cols] = acc - ddiff

    full = pl.BlockSpec((s, POOL_WIDTH), lambda i: (0, 0))
    return pl.pallas_call(
        body, name=name, grid=(1,),
        out_shape=(jax.ShapeDtypeStruct((s, POOL_WIDTH), F32),
                   jax.ShapeDtypeStruct(pool_w.shape, F32),
                   jax.ShapeDtypeStruct((1, POOL_WIDTH), F32)),
        in_specs=[full, full, pl.BlockSpec(pool_w.shape, lambda i: (0, 0, 0)),
                  pl.BlockSpec((1, POOL_WIDTH), lambda i: (0, 0))],
        out_specs=(full, pl.BlockSpec(pool_w.shape, lambda i: (0, 0, 0)),
                   pl.BlockSpec((1, POOL_WIDTH), lambda i: (0, 0))),
        compiler_params=_params(("arbitrary",)),
    )(dycat, diff, pool_w, pool_scale)


def _rope_tables(positions, name):
    s = positions.shape[0]
    ts = _tile(s, 512)
    freq = 1.0 / (ROPE_THETA ** (np.arange(0, QK_ROPE, 2, dtype=np.float32) / QK_ROPE))
    table = np.zeros((1, LANE), np.float32)
    table[0, :QK_ROPE // 2] = freq
    table[0, QK_ROPE // 2:QK_ROPE] = freq

    def body(pos_ref, f_ref, cos_ref, sin_ref):
        ang = pos_ref[...].astype(F32) * f_ref[...]
        cos_ref[...] = jnp.cos(ang)
        sin_ref[...] = jnp.sin(ang)

    out = jax.ShapeDtypeStruct((s, LANE), F32)
    blk = pl.BlockSpec((ts, LANE), lambda i: (i, 0))
    return pl.pallas_call(
        body, name=name, grid=(s // ts,), out_shape=(out, out),
        in_specs=[pl.BlockSpec((ts, 1), lambda i: (i, 0)), _vec_spec(LANE)], out_specs=(blk, blk),
        compiler_params=_params(("parallel",)),
    )(positions, jnp.asarray(table))


def _lane_mod64_low(shape):
    return (lax.broadcasted_iota(jnp.int32, shape, 1) % QK_ROPE) < (QK_ROPE // 2)


def _rope(x, cos, sin):
    rot = jnp.where(_lane_mod64_low(x.shape), -pltpu.roll(x, LANE - 32, 1), pltpu.roll(x, 32, 1))
    return x * cos + rot * sin


def _rope_t(dy, cos, sin):
    w = dy * sin
    rot_t = jnp.where(_lane_mod64_low(dy.shape), pltpu.roll(w, LANE - 32, 1), -pltpu.roll(w, 32, 1))
    return dy * cos + rot_t


def _plain_rms(x, g):
    r = lax.rsqrt(jnp.mean(x * x, axis=-1, keepdims=True) + EPS)
    return (x * r) * g, x * r, r


O_Q, O_KV, O_KR = POOL_WIDTH, POOL_WIDTH + Q_LORA, POOL_WIDTH + Q_LORA + KV_LORA


def _qkv_fwd(z, qn, kvn, wq, wkv, cos, sin, name):
    s = z.shape[0]
    ts = _tile(s, 512)

    def body(z_ref, qn_ref, kvn_ref, wq_ref, wkv_ref, cos_ref, sin_ref, q_ref, k_ref, v_ref, cqn_ref, ckvn_ref):
        cosv, sinv = cos_ref[...], sin_ref[...]
        cqn = _plain_rms(z_ref[:, O_Q:O_KV], qn_ref[...])[0].astype(BF16)
        ckvn = _plain_rms(z_ref[:, O_KV:O_KR], kvn_ref[...])[0].astype(BF16)
        cqn_ref[...] = cqn
        ckvn_ref[...] = ckvn
        nt = (((1,), (1,)), ((), ()))
        q = lax.dot_general(cqn, wq_ref[...], nt, preferred_element_type=F32)
        kv = lax.dot_general(ckvn, wkv_ref[...], nt, preferred_element_type=F32)
        kr = _rope(z_ref[:, O_KR:IN_PAD], cosv, sinv).astype(BF16)
        for h in range(N_HEADS):
            o = h * HEAD_PAD
            q_ref[:, o:o + QK_NOPE] = q[:, o:o + QK_NOPE].astype(BF16)
            q_ref[:, o + QK_NOPE:o + HEAD_PAD] = _rope(q[:, o + QK_NOPE:o + HEAD_PAD], cosv, sinv).astype(BF16)
            k_ref[:, o:o + QK_NOPE] = kv[:, o:o + QK_NOPE].astype(BF16)
            k_ref[:, o + QK_NOPE:o + HEAD_PAD] = kr
            v_ref[:, h * V_HEAD:(h + 1) * V_HEAD] = kv[:, o + QK_NOPE:o + HEAD_PAD].astype(BF16)

    def row(w):
        return pl.BlockSpec((ts, w), lambda i: (i, 0))

    def whole(arr):
        return pl.BlockSpec(arr.shape, lambda i: (0, 0))

    hp = N_HEADS * HEAD_PAD
    return pl.pallas_call(
        body, name=name, grid=(s // ts,),
        out_shape=(jax.ShapeDtypeStruct((s, hp), BF16), jax.ShapeDtypeStruct((s, hp), BF16),
                   jax.ShapeDtypeStruct((s, N_HEADS * V_HEAD), BF16),
                   jax.ShapeDtypeStruct((s, Q_LORA), BF16), jax.ShapeDtypeStruct((s, KV_LORA), BF16)),
        in_specs=[row(IN_PAD), whole(qn), whole(kvn), whole(wq), whole(wkv), row(LANE), row(LANE)],
        out_specs=(row(hp), row(hp), row(N_HEADS * V_HEAD), row(Q_LORA), row(KV_LORA)),
        compiler_params=_params(("parallel",)),
    )(z, qn, kvn, wq, wkv, cos, sin)


def _qkv_bwd(dq, dk, dv, du, z, qn, kvn, wq, wkv, cos, sin, name):
    s = z.shape[0]
    ts = _tile(s, 512)

    def norm_bwd(x, g, dy):
        _, xn, r = _plain_rms(x, g)
        dxn = dy * g
        return r * (dxn - xn * jnp.mean(dxn * xn, axis=-1, keepdims=True)), jnp.sum(dy * xn, axis=0, keepdims=True)

    def body(dq_ref, dk_ref, dv_ref, du_ref, z_ref, qn_ref, kvn_ref, wq_ref, wkv_ref, cos_ref, sin_ref,
             dz_ref, dqb_ref, dkvb_ref, dqn_ref, dkvn_ref):
        @pl.when(pl.program_id(0) == 0)
        def _():
            dqn_ref[...] = jnp.zeros_like(dqn_ref)
            dkvn_ref[...] = jnp.zeros_like(dkvn_ref)

        cosv, sinv = cos_ref[...], sin_ref[...]
        dkr = jnp.zeros((ts, LANE), F32)
        for h in range(N_HEADS):
            o = h * HEAD_PAD
            dqb_ref[:, o:o + QK_NOPE] = dq_ref[:, o:o + QK_NOPE].astype(BF16)
            dqb_ref[:, o + QK_NOPE:o + HEAD_PAD] = _rope_t(dq_ref[:, o + QK_NOPE:o + HEAD_PAD], cosv, sinv).astype(BF16)
            dkvb_ref[:, o:o + QK_NOPE] = dk_ref[:, o:o + QK_NOPE].astype(BF16)
            dkvb_ref[:, o + QK_NOPE:o + HEAD_PAD] = dv_ref[:, h * V_HEAD:(h + 1) * V_HEAD].astype(BF16)
            dkr = dkr + dk_ref[:, o + QK_NOPE:o + HEAD_PAD]
        dcqn = jnp.dot(dqb_ref[...], wq_ref[...], preferred_element_type=F32)
        dckvn = jnp.dot(dkvb_ref[...], wkv_ref[...], preferred_element_type=F32)
        dcq, dqn = norm_bwd(z_ref[:, O_Q:O_KV], qn_ref[...], dcqn)
        dckv, dkvn = norm_bwd(z_ref[:, O_KV:O_KR], kvn_ref[...], dckvn)
        dqn_ref[...] += dqn
        dkvn_ref[...] += dkvn
        dz_ref[:, 0:O_Q] = du_ref[...].astype(BF16)
        dz_ref[:, O_Q:O_KV] = dcq.astype(BF16)
        dz_ref[:, O_KV:O_KR] = dckv.astype(BF16)
        dz_ref[:, O_KR:IN_PAD] = _rope_t(dkr, cosv, sinv).astype(BF16)

    def row(w):
        return pl.BlockSpec((ts, w), lambda i: (i, 0))

    def whole(arr):
        return pl.BlockSpec(arr.shape, lambda i: (0, 0))

    hp = N_HEADS * HEAD_PAD
    return pl.pallas_call(
        body, name=name, grid=(s // ts,),
        out_shape=(jax.ShapeDtypeStruct((s, IN_PAD), BF16), jax.ShapeDtypeStruct((s, hp), BF16),
                   jax.ShapeDtypeStruct((s, hp), BF16),
                   jax.ShapeDtypeStruct((1, Q_LORA), F32), jax.ShapeDtypeStruct((1, KV_LORA), F32)),
        in_specs=[row(hp), row(hp), row(N_HEADS * V_HEAD), row(POOL_WIDTH), row(IN_PAD),
                  whole(qn), whole(kvn), whole(wq), whole(wkv), row(LANE), row(LANE)],
        out_specs=(row(IN_PAD), row(hp), row(hp), whole(qn), whole(kvn)),
        compiler_params=_params(("arbitrary",)),
    )(dq, dk, dv, du, z, qn, kvn, wq, wkv, cos, sin)


def _causal_scores(q, k, i, tq, klen):
    sc = lax.dot_general(q, k, (((1,), (1,)), ((), ())), preferred_element_type=F32) * SOFTMAX_SCALE
    qpos = i * tq + lax.broadcasted_iota(jnp.int32, (tq, klen), 0)
    kpos = lax.broadcasted_iota(jnp.int32, (tq, klen), 1)
    return jnp.where(qpos >= kpos, sc, -jnp.inf)


ATTN_TQ = 512
ATTN_SEGMENTS = 4


def _by_key_prefix(i, nq, tq, compute):
    nseg = min(ATTN_SEGMENTS, nq)
    per = nq // nseg
    for r in range(nseg):
        pl.when(i // per == r)(lambda r=r: compute((r + 1) * per * tq))


def _attn_fwd(q, k, v, name):
    s = q.shape[0]
    tq = _tile(s, ATTN_TQ)
    nq = s // tq

    def body(q_ref, k_ref, v_ref, o_ref, lse_ref):
        i = pl.program_id(1)

        def compute(klen):
            sc = _causal_scores(q_ref[...], k_ref[0:klen, :], i, tq, klen)
            mx = jnp.max(sc, axis=-1, keepdims=True)
            p = jnp.exp(sc - mx)
            den = jnp.sum(p, axis=-1, keepdims=True)
            o_ref[...] = jnp.dot((p / den).astype(BF16), v_ref[0:klen, :], preferred_element_type=F32)
            lse_ref[...] = mx + jnp.log(den)

        _by_key_prefix(i, nq, tq, compute)

    return pl.pallas_call(
        body, name=name, grid=(N_HEADS, s // tq),
        out_shape=(jax.ShapeDtypeStruct((s, N_HEADS * V_HEAD), F32), jax.ShapeDtypeStruct((N_HEADS, s, 1), F32)),
        in_specs=[pl.BlockSpec((tq, HEAD_PAD), lambda h, i: (i, h)),
                  pl.BlockSpec((s, HEAD_PAD), lambda h, i: (0, h)),
                  pl.BlockSpec((s, V_HEAD), lambda h, i: (0, h))],
        out_specs=(pl.BlockSpec((tq, V_HEAD), lambda h, i: (i, h)),
                   pl.BlockSpec((None, tq, 1), lambda h, i: (h, i, 0))),
        compiler_params=_params(("parallel", "parallel")),
    )(q, k, v)


def _attn_bwd(q, k, v, lse, dycat, name):
    s = q.shape[0]
    tq = _tile(s, ATTN_TQ)
    nq = s // tq
    tn_dims = (((0,), (0,)), ((), ()))

    def body(q_ref, k_ref, v_ref, lse_ref, do_ref, dq_ref, dk_ref, dv_ref):
        i = pl.program_id(1)

        @pl.when(i == 0)
        def _():
            dk_ref[...] = jnp.zeros_like(dk_ref)
            dv_ref[...] = jnp.zeros_like(dv_ref)

        def compute(klen):
            qv, kv_, dob = q_ref[...], k_ref[0:klen, :], do_ref[...].astype(BF16)
            sc = _causal_scores(qv, kv_, i, tq, klen)
            p = jnp.exp(sc - lse_ref[...])
            dp = lax.dot_general(dob, v_ref[0:klen, :], (((1,), (1,)), ((), ())), preferred_element_type=F32)
            ds = (p * (dp - jnp.sum(dp * p, axis=-1, keepdims=True)) * SOFTMAX_SCALE).astype(BF16)
            dq_ref[...] = jnp.dot(ds, kv_, preferred_element_type=F32)
            dk_ref[0:klen, :] += lax.dot_general(ds, qv, tn_dims, preferred_element_type=F32)
            dv_ref[0:klen, :] += lax.dot_general(p.astype(BF16), dob, tn_dims, preferred_element_type=F32)

        _by_key_prefix(i, nq, tq, compute)

    n_pool_blocks = POOL_WIDTH // V_HEAD
    return pl.pallas_call(
        body, name=name, grid=(N_HEADS, s // tq),
        out_shape=(jax.ShapeDtypeStruct((s, N_HEADS * HEAD_PAD), F32),
                   jax.ShapeDtypeStruct((s, N_HEADS * HEAD_PAD), F32),
                   jax.ShapeDtypeStruct((s, N_HEADS * V_HEAD), F32)),
        in_specs=[pl.BlockSpec((tq, HEAD_PAD), lambda h, i: (i, h)),
                  pl.BlockSpec((s, HEAD_PAD), lambda h, i: (0, h)),
                  pl.BlockSpec((s, V_HEAD), lambda h, i: (0, h)),
                  pl.BlockSpec((None, tq, 1), lambda h, i: (h, i, 0)),
                  pl.BlockSpec((tq, V_HEAD), lambda h, i: (i, n_pool_blocks + h))],
        out_specs=(pl.BlockSpec((tq, HEAD_PAD), lambda h, i: (i, h)),
                   pl.BlockSpec((s, HEAD_PAD), lambda h, i: (0, h)),
                   pl.BlockSpec((s, V_HEAD), lambda h, i: (0, h))),
        compiler_params=_params(("parallel", "arbitrary")),
    )(q, k, v, lse, dycat)


def _loss_head(x, gw, target, below, name):
    s, d = x.shape
    ts = _tile(s, 256)
    factor = below[2]

    def body(x_ref, gw_ref, tgt_ref, yb_ref, gb_ref, loss_ref, dx_ref, dgw_ref, dyb_ref, dgb_ref):
        @pl.when(pl.program_id(0) == 0)
        def _():
            loss_ref[...] = jnp.zeros_like(loss_ref)
            dgw_ref[...] = jnp.zeros_like(dgw_ref)
            dgb_ref[...] = jnp.zeros_like(dgb_ref)

        xv, gwv = x_ref[...], gw_ref[...]
        r = lax.rsqrt(jnp.mean(xv * xv, axis=-1, keepdims=True) + EPS)
        xn = xv * r
        err = xn * gwv - tgt_ref[...]
        loss_ref[...] += 0.5 * jnp.sum(jnp.mean(err * err, axis=-1, keepdims=True))
        dy = err / d
        dgw_ref[...] += jnp.sum(dy * xn, axis=0, keepdims=True)
        dxn = dy * gwv
        dx = r * (dxn - xn * jnp.mean(dxn * xn, axis=-1, keepdims=True))
        dx_ref[...] = dx
        dyb_ref[...] = ((factor * gb_ref[...]) * dx).astype(BF16)
        dgb_ref[...] += jnp.sum((factor * dx) * yb_ref[...].astype(F32), axis=0, keepdims=True)

    row = pl.BlockSpec((ts, d), lambda i: (i, 0))
    gate_spec, gate_arg = _vec_in(below[1])
    vec = jax.ShapeDtypeStruct((1, d), F32)
    return pl.pallas_call(
        body, name=name, grid=(s // ts,),
        out_shape=(jax.ShapeDtypeStruct((8, LANE), F32), jax.ShapeDtypeStruct((s, d), F32), vec,
                   jax.ShapeDtypeStruct((s, d), BF16), vec),
        in_specs=[row, _vec_spec(d), row, row, gate_spec],
        out_specs=(pl.BlockSpec((8, LANE), lambda i: (0, 0)), row, _vec_spec(d), row, _vec_spec(d)),
        compiler_params=_params(("arbitrary",)),
    )(x, gw, target, below[0], gate_arg)


def _ada_mod(c_all, ada_w, ada_b, name):
    nl, d, cols = ada_w.shape

    def body(c_ref, w_ref, b_ref, o_ref):
        cv = c_ref[...]
        act = (cv * jax.nn.sigmoid(cv)).astype(BF16)
        o_ref[...] = jnp.dot(act, w_ref[...].astype(BF16), preferred_element_type=F32) + b_ref[...]

    return pl.pallas_call(
        body, name=name, grid=(nl,), out_shape=jax.ShapeDtypeStruct((nl, N_DEV, cols), F32),
        in_specs=[pl.BlockSpec((N_DEV, d), lambda l: (0, 0)),
                  pl.BlockSpec((None, d, cols), lambda l: (l, 0, 0)),
                  pl.BlockSpec((None, 1, cols), lambda l: (l, 0, 0))],
        out_specs=pl.BlockSpec((None, N_DEV, cols), lambda l: (l, 0, 0)),
        compiler_params=_params(("parallel",)),
    )(c_all, ada_w, ada_b)


def _ada_grad(c_pad, dmod_pad, name):
    nl, kpad, cols = dmod_pad.shape
    d = c_pad.shape[1]

    def body(c_ref, dm_ref, o_ref):
        cv = c_ref[...]
        act = (cv * jax.nn.sigmoid(cv)).astype(BF16)
        o_ref[...] = lax.dot_general(act, dm_ref[...].astype(BF16), (((0,), (0,)), ((), ())),
                                     preferred_element_type=F32)

    return pl.pallas_call(
        body, name=name, grid=(nl,), out_shape=jax.ShapeDtypeStruct((nl, d, cols), F32),
        in_specs=[pl.BlockSpec((kpad, d), lambda l: (0, 0)),
                  pl.BlockSpec((None, kpad, cols), lambda l: (l, 0, 0))],
        out_specs=pl.BlockSpec((None, d, cols), lambda l: (l, 0, 0)),
        compiler_params=_params(("parallel",)),
    )(c_pad, dmod_pad)


def _adamw_math(w, g, m, v):
    nm = ADAM_B1 * m + (1.0 - ADAM_B1) * g
    nv = ADAM_B2 * v + (1.0 - ADAM_B2) * (g * g)
    m_hat = nm / (1.0 - ADAM_B1 ** ADAM_STEP)
    v_hat = nv / (1.0 - ADAM_B2 ** ADAM_STEP)
    return -ADAM_LR * (m_hat / (jnp.sqrt(v_hat) + ADAM_EPS) + ADAM_WD * w), nm, nv


def _adamw_rows(w3, gbuf, row_off, m3, v3, name):
    nl, r, d = w3.shape
    tr = _row_tile(math.gcd(r, row_off) if row_off else r, 352)
    first = row_off // tr

    def body(w_ref, g_ref, m_ref, v_ref, go_ref, d_ref, nm_ref, nv_ref):
        gv = g_ref[...]
        go_ref[...] = gv
        d_ref[...], nm_ref[...], nv_ref[...] = _adamw_math(w_ref[...], gv, m_ref[...], v_ref[...])

    blk = pl.BlockSpec((None, tr, d), lambda l, i: (l, i, 0))
    gblk = pl.BlockSpec((None, tr, d), lambda l, i: (l, first + i, 0))
    out = jax.ShapeDtypeStruct((nl, r, d), F32)
    return pl.pallas_call(
        body, name=name, grid=(nl, r // tr), out_shape=(out, out, out, out),
        in_specs=[blk, gblk, blk, blk], out_specs=(blk, blk, blk, blk),
        compiler_params=_params(("parallel", "parallel")),
    )(w3, gbuf, m3, v3)


def _adamw(w, g, m, v, name):
    rows, cols = w.shape
    tr = _row_tile(rows, 512)

    def body(w_ref, g_ref, m_ref, v_ref, d_ref, nm_ref, nv_ref):
        d_ref[...], nm_ref[...], nv_ref[...] = _adamw_math(w_ref[...], g_ref[...], m_ref[...], v_ref[...])

    blk = pl.BlockSpec((tr, cols), lambda i: (i, 0))
    out = jax.ShapeDtypeStruct((rows, cols), F32)
    return pl.pallas_call(
        body, name=name, grid=(rows // tr,), out_shape=(out, out, out),
        in_specs=[blk, blk, blk, blk], out_specs=(blk, blk, blk),
        compiler_params=_params(("parallel",)),
    )(w, g, m, v)


def _adamw_nd(w, g, m, v, name):
    shape = w.shape
    flat = (lambda t: t.reshape(1, -1)) if w.ndim == 1 else (lambda t: t.reshape(-1, shape[-1]))
    return tuple(t.reshape(shape) for t in _adamw(flat(w), flat(g), flat(m), flat(v), name))


def _pad_rows(t, rows):
    return jnp.pad(t, ((0, rows - t.shape[0]), (0, 0)))


def _pack_shard_layer(l, wts):
    def tr(name):
        return wts[name][l].astype(BF16).T

    parts = [tr("ffn1_w_gate"), tr("ffn1_w_up"), wts["ffn1_w_down"][l].astype(BF16),
             tr("ffn2_w_gate"), tr("ffn2_w_up"), wts["ffn2_w_down"][l].astype(BF16),
             wts["w_out"][l].astype(BF16),
             tr("w_kv_b").reshape(KV_SH_ROWS, D_MODEL),
             _pad_rows(tr("w_in"), 160),
             _pad_rows(tr("w_q_b").reshape(Q_SH_ROWS, D_MODEL), Q_PAD_ROWS)]
    return jnp.concatenate(parts, axis=0)


def _mixer_weights(w_out, small):
    w = {"out": w_out}
    small = small.reshape(N_DEV, SMALL_ROWS, D_MODEL)
    o_in, o_q = OFF_IN - OFF_KV, OFF_Q - OFF_KV
    w["kv"] = small[:, :KV_SH_ROWS].reshape(N_HEADS * HEAD_PAD, KV_LORA)
    w["in"] = _pad_rows(small[:, o_in:o_in + IN_SH].reshape(IN_COLS, D_MODEL), IN_PAD)
    wq = small[:, o_q:o_q + Q_SH_ROWS].reshape(N_HEADS, QK_HEAD, Q_LORA)
    w["q"] = jnp.pad(wq, ((0, 0), (0, HEAD_PAD - QK_HEAD), (0, 0))).reshape(N_HEADS * HEAD_PAD, Q_LORA)
    return w


def _grad_sources_b(gr):
    gq = gr["q"].reshape(N_HEADS, HEAD_PAD, Q_LORA)[:, :QK_HEAD].reshape(N_DEV, Q_SH_ROWS, D_MODEL)
    small = jnp.concatenate([
        gr["kv"].reshape(N_DEV, KV_SH_ROWS, D_MODEL),
        jnp.pad(gr["in"][:IN_COLS].reshape(N_DEV, IN_SH, D_MODEL), ((0, 0), (0, 160 - IN_SH), (0, 0))),
        jnp.pad(gq, ((0, 0), (0, Q_PAD_ROWS - Q_SH_ROWS), (0, 0)))], axis=1)
    return [gr["g2"], gr["u2"], gr["d2"], gr["out"], small.reshape(N_DEV * SMALL_ROWS, D_MODEL)]


def _pack_bf16_pairs(t):
    rows, d = t.shape
    return lax.bitcast_convert_type(t.astype(BF16).reshape(rows // 2, 2, d).transpose(0, 2, 1), F32)


def _unpack_bf16_pairs(p):
    pairs = jnp.swapaxes(lax.bitcast_convert_type(p, BF16), -1, -2)
    return pairs.reshape(p.shape[:-2] + (2 * p.shape[-2], p.shape[-1]))


def _small_layout(nl):
    names = [("dmod", nl * N_MOD), ("ffn1_norm", nl), ("mix_norm", nl), ("ffn2_norm", nl), ("q_a_norm", nl),
             ("kv_a_norm", nl), ("pool_scale", nl), ("final_norm", 1), ("loss", 1),
             ("pool_w", nl * 4 * POOL_GC * POOL_GC // D_MODEL // 2)]
    off, table = 0, {}
    for name, n in names:
        table[name] = (off, n)
        off += -(-n // 8) * 8
    return table, off


def _to_rows(t, width=D_MODEL):
    n, w = t.shape
    return jnp.pad(t, ((0, -(-n // 8) * 8 - n), (0, width - w)))


def kernel(x, c, positions, ada_w, ada_b, ffn1_norm, ffn1_w_gate, ffn1_w_up, ffn1_w_down, mix_norm, w_in, pool_w, pool_scale, q_a_norm, w_q_b, kv_a_norm, w_kv_b, w_out, ffn2_norm, ffn2_w_gate, ffn2_w_up, ffn2_w_down, final_norm, loss_target, m_ada_w, m_ada_b, m_ffn1_norm, m_ffn1_w_gate, m_ffn1_w_up, m_ffn1_w_down, m_mix_norm, m_w_in, m_pool_w, m_pool_scale, m_q_a_norm, m_w_q_b, m_kv_a_norm, m_w_kv_b, m_w_out, m_ffn2_norm, m_ffn2_w_gate, m_ffn2_w_up, m_ffn2_w_down, m_final_norm, v_ada_w, v_ada_b, v_ffn1_norm, v_ffn1_w_gate, v_ffn1_w_up, v_ffn1_w_down, v_mix_norm, v_w_in, v_pool_w, v_pool_scale, v_q_a_norm, v_w_q_b, v_kv_a_norm, v_w_kv_b, v_w_out, v_ffn2_norm, v_ffn2_w_gate, v_ffn2_w_up, v_ffn2_w_down, v_final_norm):
    wts = dict(ada_w=ada_w, ada_b=ada_b, ffn1_norm=ffn1_norm, ffn1_w_gate=ffn1_w_gate, ffn1_w_up=ffn1_w_up,
               ffn1_w_down=ffn1_w_down, mix_norm=mix_norm, w_in=w_in, pool_w=pool_w, pool_scale=pool_scale,
               q_a_norm=q_a_norm, w_q_b=w_q_b, kv_a_norm=kv_a_norm, w_kv_b=w_kv_b, w_out=w_out,
               ffn2_norm=ffn2_norm, ffn2_w_gate=ffn2_w_gate, ffn2_w_up=ffn2_w_up, ffn2_w_down=ffn2_w_down,
               final_norm=final_norm)
    mom_m = dict(ada_w=m_ada_w, ada_b=m_ada_b, ffn1_norm=m_ffn1_norm, ffn1_w_gate=m_ffn1_w_gate,
                 ffn1_w_up=m_ffn1_w_up, ffn1_w_down=m_ffn1_w_down, mix_norm=m_mix_norm, w_in=m_w_in,
                 pool_w=m_pool_w, pool_scale=m_pool_scale, q_a_norm=m_q_a_norm, w_q_b=m_w_q_b,
                 kv_a_norm=m_kv_a_norm, w_kv_b=m_w_kv_b, w_out=m_w_out, ffn2_norm=m_ffn2_norm,
                 ffn2_w_gate=m_ffn2_w_gate, ffn2_w_up=m_ffn2_w_up, ffn2_w_down=m_ffn2_w_down,
                 final_norm=m_final_norm)
    mom_v = dict(ada_w=v_ada_w, ada_b=v_ada_b, ffn1_norm=v_ffn1_norm, ffn1_w_gate=v_ffn1_w_gate,
                 ffn1_w_up=v_ffn1_w_up, ffn1_w_down=v_ffn1_w_down, mix_norm=v_mix_norm, w_in=v_w_in,
                 pool_w=v_pool_w, pool_scale=v_pool_scale, q_a_norm=v_q_a_norm, w_q_b=v_w_q_b,
                 kv_a_norm=v_kv_a_norm, w_kv_b=v_w_kv_b, w_out=v_w_out, ffn2_norm=v_ffn2_norm,
                 ffn2_w_gate=v_ffn2_w_gate, ffn2_w_up=v_ffn2_w_up, ffn2_w_down=v_ffn2_w_down,
                 final_norm=v_final_norm)
    order = list(wts)
    nl = ada_w.shape[0]
    seq = x.shape[1]
    me = 4 * lax.axis_index("x") + 2 * lax.axis_index("y") + lax.axis_index("c")
    ada_cols = ada_w.shape[2]

    def after_token(t, token):
        return t + token[0:1, 0:1].astype(t.dtype)

    packs = [_pack_shard_layer(l, wts) for l in range(nl)]

    c_all = _all_gather(jnp.broadcast_to(c, (8, D_MODEL)), "gather_c")[::8]

    ada_b_mine = lax.dynamic_slice_in_dim(ada_b, me * ada_cols, ada_cols, axis=1).reshape(nl, 1, ada_cols)
    mod_part = _ada_mod(c_all, ada_w, ada_b_mine, "ada_mod")
    mod_all = _all_gather(mod_part.reshape(nl * N_DEV, ada_cols), "gather_mod")
    mod_all = mod_all.reshape(N_DEV, nl, N_DEV, ada_cols)
    mod = lax.dynamic_index_in_dim(mod_all, me, axis=2, keepdims=False)
    mod = mod.transpose(1, 0, 2).reshape(nl * N_MOD, 1, D_MODEL)
    norm_tables = {name: wts[name].reshape(nl, 1, D_MODEL) for name in ("ffn1_norm", "mix_norm", "ffn2_norm")}

    def modrow(l, k):
        return mod, l * N_MOD + k

    def normrow(name, l):
        return norm_tables[name], l

    def start_layer(l, after):
        first = _gather_start(packs[l][:SPLIT_AB], ROWS_A, after, f"gather_start_{l}a")
        mixer = _gather_start(packs[l][OFF_OUT:], ROWS_TAIL, first[4], f"gather_start_{l}b")
        second = _gather_start(packs[l][SPLIT_AB:OFF_OUT], ROWS_A, mixer[4], f"gather_start_{l}c")
        return first, mixer, second

    flights = {0: start_layer(0, mod)}
    if nl > 1:
        flights[1] = start_layer(1, flights[0][2][4])
    last_start = flights[min(1, nl - 1)][2][4]

    cos, sin = _rope_tables(after_token(positions.reshape(seq, 1), last_start), "rope_tables")

    def vec(t):
        return t.reshape(1, -1)

    def landed(flight, rows_list, after, tag):
        send_sems, recv_sems, pk, lands, _ = flight
        pk, lands = _gather_wait(send_sems, recv_sems, pk, lands, after, f"gather_wait_{tag}")
        return _gather_finish(pk, rows_list, lands, "gather_finish")

    xs = x.reshape(seq, D_MODEL)
    saved = []
    for l in range(nl):
        norm1, up_after = normrow("ffn1_norm", l), None
        flight_a, flight_b, flight_c = flights[l]
        lands = landed(flight_a, ROWS_A, cos if l == 0 else xs, f"{l}a")
        if l >= 1 and l + 1 < nl:
            flights[l + 1] = start_layer(l + 1, lands[0])
            up_after = flights[l + 1][2][4]
        sv = {}

        def ffn_fwd(xin, norm, k0, wg, wu, wd, tag, after=None):
            h, a, b, t = _ffn_up(xin, norm, modrow(l, k0), modrow(l, k0 + 1), wg, wu, "ffn_up", after=after)
            y, xout = _mm(t, wd, "nn", "ffn_down", res=xin, gate=modrow(l, k0 + 2), gate_factor=0.5)
            sv[tag] = dict(x=xin, h=h, a=a, b=b, t=t, y=y)
            return xout

        xs = ffn_fwd(xs, norm1, 0, lands[0], lands[1], lands[2], "f1", up_after)
        w = dict(zip(("g1", "u1", "d1"), lands[:3]))
        w.update(_mixer_weights(*landed(flight_b, ROWS_TAIL, xs, f"{l}b")))
        sv["w"] = w

        h2, z = _norm_mm(xs, normrow("mix_norm", l), modrow(l, 3), modrow(l, 4), w["in"], "mix_in")
        y_pool, diff = _pool_fwd(z, pool_w[l], vec(pool_scale[l]), "pool_fwd")
        q, k, v, cqn, ckvn = _qkv_fwd(z, vec(q_a_norm[l]), vec(kv_a_norm[l]), w["q"], w["kv"], cos, sin, "qkv_fwd")
        o, lse = _attn_fwd(q, k, v, "attn_fwd")
        ycat = jnp.concatenate([y_pool, o.astype(BF16)], axis=1)
        y2, xmix = _mm(ycat, w["out"], "nn", "mix_out", res=xs, gate=modrow(l, 5), gate_factor=1.0)
        sv["mix"] = dict(x=xs, h=h2, z=z, diff=diff, q=q, k=k, v=v, cqn=cqn, ckvn=ckvn, lse=lse, ycat=ycat, y=y2)
        xs = xmix

        w.update(zip(("g2", "u2", "d2"), landed(flight_c, ROWS_A, xs, f"{l}c")))
        xs = ffn_fwd(xs, normrow("ffn2_norm", l), 6, w["g2"], w["u2"], w["d2"], "f2")
        saved.append(sv)

    loss_part, dx, d_final, *head = _loss_head(xs, vec(final_norm), loss_target.reshape(seq, D_MODEL),
                                               (saved[nl - 1]["f2"]["y"], modrow(nl - 1, 8), 0.5), "loss_head")

    small = {name: [None] * nl for name in ("ffn1_norm", "mix_norm", "ffn2_norm", "q_a_norm", "kv_a_norm",
                                            "pool_scale", "pool_w", "dmod")}
    core = lax.axis_index("c").astype(jnp.int32).reshape(1)
    chip = 2 * lax.axis_index("x") + lax.axis_index("y")
    exchanges = []

    def leave(srcs, rows_list, after, tag):
        return _pair_start(srcs, rows_list, after, f"pair_start_{tag}"), rows_list, tag

    def forward_on(pending, after, layer, row_off):
        (send_sems, recv_sems, srcs, land, _), rows_list, tag = pending
        srcs, land = _split_wait(send_sems, recv_sems, 1, srcs, land, after, f"pair_wait_{tag}")
        sums = _pair_sum(srcs, rows_list, land, core, "pair_sum")
        flight = _chip_exchange_start(sums, chip, after, f"exchange_start_{tag}")
        exchanges.append((flight, layer, row_off, tag))
        return flight[4]

    pending = None
    for l in reversed(range(nl)):
        sv = saved[l]
        w = sv["w"]
        dmod = [None] * N_MOD
        gr = {}

        def ffn_bwd(dxin, head, s_, norm, k0, wg, wu, wd, tag, below, first_after=None, mid=None):
            dy, dmod[k0 + 2] = head
            da, db, gr["d" + tag], gr["g" + tag], gr["u" + tag] = _ffn_bwd_cols(
                dy, s_["h"], s_["a"], s_["b"], s_["t"], wd, "ffn_bwd_cols", after=first_after)
            dh = _mm_pair(da, wg, db, wu, "ffn_bwd_dh", after=None if mid is None else mid(da))
            outs = _rm_bwd(dh, s_["x"], dxin, norm, modrow(l, k0 + 1), "rm_bwd", below=below)
            dmod[k0], dmod[k0 + 1] = outs[1], outs[2]
            return outs[0], outs[3], outs[4:]

        s_ = sv["mix"]
        dx, small["ffn2_norm"][l], head = ffn_bwd(
            dx, head, sv["f2"], normrow("ffn2_norm", l), 6, w["g2"], w["u2"], w["d2"], "2", (s_["y"], modrow(l, 5), 1.0),
            first_after=None if pending is None else pending[0][4])

        pending_c = leave([gr["g2"], gr["u2"], gr["d2"]], ROWS_A, dx, f"{l}c")
        mix_after = pending_c[0][4]
        if pending is not None:
            mix_after = forward_on(pending, mix_after, l + 1, GB_F1)
            pending = None
        dy, dmod[5] = head
        gr["out"] = _mm(s_["ycat"], dy, "tn", "mix_out_dw", out_dtype=BF16, tm=512, after=mix_after)
        dycat = _mm(dy, w["out"], "nt", "mix_out_dx", tm=1024)
        du, small["pool_w"][l], small["pool_scale"][l] = _pool_bwd(dycat, s_["diff"], pool_w[l], vec(pool_scale[l]), "pool_bwd")
        dq, dk, dv = _attn_bwd(s_["q"], s_["k"], s_["v"], s_["lse"], dycat, "attn_bwd")
        dz, dqb, dkvb, small["q_a_norm"][l], small["kv_a_norm"][l] = _qkv_bwd(
            dq, dk, dv, du, s_["z"], vec(q_a_norm[l]), vec(kv_a_norm[l]), w["q"], w["kv"], cos, sin, "qkv_bwd")
        gr["q"] = _mm(dqb, s_["cqn"], "tn", "q_b_dw", out_dtype=BF16, tm=512, after=forward_on(pending_c, dz, l, GB_F2))
        gr["kv"] = _mm(dkvb, s_["ckvn"], "tn", "kv_b_dw", out_dtype=BF16, tm=512)
        gr["in"] = _mm(dz, s_["h"], "tn", "mix_in_dw", out_dtype=BF16, tm=512)
        dh2 = _mm(dz, w["in"], "nn", "mix_in_dx", tm=1024)
        outs = _rm_bwd(dh2, s_["x"], dx, normrow("mix_norm", l), modrow(l, 4), "rm_bwd",
                       below=(sv["f1"]["y"], modrow(l, 2), 0.5))
        dx, dmod[3], dmod[4], small["mix_norm"][l] = outs[:4]
        head = outs[4:]

        first_after, mid = None, None
        if l == 0:
            pending_b = leave(_grad_sources_b(gr)[3:], ROWS_TAIL, dx, "0b")
            first_after = pending_b[0][4]
            last_groups = []

            def mid(da):
                last_groups.append(leave([gr["g1"], gr["u1"], gr["d1"]], ROWS_A, da, "0a"))
                return forward_on(pending_b, last_groups[0][0][4], 0, GB_TAIL)
        below = (saved[l - 1]["f2"]["y"], modrow(l - 1, 8), 0.5) if l > 0 else None
        dx, small["ffn1_norm"][l], head = ffn_bwd(
            dx, head, sv["f1"], normrow("ffn1_norm", l), 0, w["g1"], w["u1"], w["d1"], "1", below, first_after, mid)

        small["dmod"][l] = jnp.concatenate(dmod, axis=0)
        if l > 0:
            pending = leave([gr["g1"], gr["u1"], gr["d1"]] + _grad_sources_b(gr)[3:], ROWS_A + ROWS_TAIL, dx, l)

    grad_x = dx.reshape(x.shape)
    pending_a = last_groups[0]

    layout, small_rows = _small_layout(nl)
    pieces = {
        "dmod": jnp.concatenate(small["dmod"], axis=0),
        "ffn1_norm": jnp.concatenate(small["ffn1_norm"], axis=0),
        "mix_norm": jnp.concatenate(small["mix_norm"], axis=0),
        "ffn2_norm": jnp.concatenate(small["ffn2_norm"], axis=0),
        "q_a_norm": jnp.concatenate(small["q_a_norm"], axis=0),
        "kv_a_norm": jnp.concatenate(small["kv_a_norm"], axis=0),
        "pool_scale": jnp.concatenate(small["pool_scale"], axis=0),
        "final_norm": d_final,
        "loss": jnp.broadcast_to(loss_part[0:1, 0:1], (1, D_MODEL)),
        "pool_w": _pack_bf16_pairs(jnp.stack(small["pool_w"]).reshape(-1, D_MODEL)),
    }
    small_buf = jnp.concatenate([_to_rows(pieces[name]) for name in layout], axis=0)
    def landed_sums(gbuf, entries, after):
        for (send_sems, recv_sems, sums, recv, _), layer, row_off, tag in entries:
            _, recv = _split_wait(send_sems, recv_sems, N_CHIPS - 1, sums, recv, after, f"exchange_wait_{tag}")
            gbuf = _sum_slots_into(recv, gbuf, layer, row_off, "sum_grads")
        return gbuf

    gbuf = lax.empty((nl, ROWS_L, D_MODEL), F32)
    token_0a = forward_on(pending_a, dx, 0, GB_F1)
    spread = _spread_start(small_buf, me, token_0a, "small_start")
    gbuf = landed_sums(gbuf, [e for e in exchanges if e[3] != "0a"], spread[4])

    def swap(t):
        return t.transpose(0, 2, 1)

    def same(t):
        return t

    grads, updates = {}, {}

    def update_rows(gbuf, table):
        for wname, off, view in table:
            g, d_, nm, nv = _adamw_rows(view(wts[wname]), gbuf, off, view(mom_m[wname]), view(mom_v[wname]), "adamw_rows")
            grads[wname], updates[wname] = view(g), (view(d_), view(nm), view(nv))

    update_rows(gbuf, (("ffn2_w_gate", GB_F2, swap), ("ffn2_w_up", GB_F2 + FF_SH, swap),
                       ("ffn2_w_down", GB_F2 + 2 * FF_SH, same), ("w_out", GB_TAIL, same)))
    small_grads = {
        "w_kv_b": (gbuf[:, OFF_KV:OFF_KV + KV_SH_ROWS].reshape(nl, -1, KV_LORA).transpose(0, 2, 1), same),
        "w_in": (gbuf[:, OFF_IN:OFF_IN + IN_SH], swap),
        "w_q_b": (gbuf[:, OFF_Q:OFF_Q + Q_SH_ROWS].reshape(nl, -1, Q_LORA), swap),
    }
    for wname, (g, view) in small_grads.items():
        upd = _adamw_nd(view(wts[wname]), g, view(mom_m[wname]), view(mom_v[wname]), "adamw")
        grads[wname], updates[wname] = view(g), tuple(view(t) for t in upd)
    gbuf = landed_sums(gbuf, [e for e in exchanges if e[3] == "0a"], updates["w_q_b"][0])
    update_rows(gbuf, (("ffn1_w_gate", GB_F1, swap), ("ffn1_w_up", GB_F1 + FF_SH, swap),
                       ("ffn1_w_down", GB_F1 + 2 * FF_SH, same)))

    _, small_all = _split_wait(spread[0], spread[1], N_DEV - 1, spread[2], spread[3], updates["ffn1_w_down"][0],
                               "small_wait")
    pool_off, pool_rows = layout["pool_w"]
    small_sum = _sum_slots(small_all[:, :pool_off], "sum_small")
    pool_sum = _sum_slots(_unpack_bf16_pairs(small_all[:, pool_off:pool_off + pool_rows]), "sum_pool_w")

    def take(name, width=D_MODEL):
        off, n = layout[name]
        return small_sum[off:off + n, :width]

    late = {"ada_b": take("dmod").reshape(nl, N_MOD * D_MODEL),
            "ffn1_norm": take("ffn1_norm"), "mix_norm": take("mix_norm"), "ffn2_norm": take("ffn2_norm"),
            "q_a_norm": take("q_a_norm", Q_LORA), "kv_a_norm": take("kv_a_norm", KV_LORA),
            "pool_scale": take("pool_scale", POOL_WIDTH), "final_norm": take("final_norm").reshape(D_MODEL),
            "pool_w": pool_sum.reshape(pool_w.shape)}
    loss = take("loss")[0, 0]

    off, n = layout["dmod"]
    dmod_all = small_all[:, off:off + n].reshape(N_DEV, nl, N_MOD * D_MODEL)
    dmod_mine = lax.dynamic_slice_in_dim(dmod_all, me * ada_cols, ada_cols, axis=2)
    dmod_pad = jnp.pad(dmod_mine.transpose(1, 0, 2), ((0, 0), (0, LANE - N_DEV), (0, 0)))
    late["ada_w"] = _ada_grad(jnp.pad(c_all, ((0, LANE - N_DEV), (0, 0))), dmod_pad, "ada_grad")
    for name, g in late.items():
        grads[name], updates[name] = g, _adamw_nd(wts[name], g, mom_m[name], mom_v[name], "adamw")

    return (loss, grad_x, *[grads[n] for n in order], *[updates[n][0] for n in order],
            *[updates[n][1] for n in order], *[updates[n][2] for n in order])
```

```python
import math

import numpy as np
import jax
import jax.numpy as jnp
from jax import lax
from jax.experimental import pallas as pl
from jax.experimental.pallas import tpu as pltpu

F32 = jnp.float32
BF16 = jnp.bfloat16

N_DEV = 8
D_MODEL = 1024
D_FF = 2816
POOL_WIDTH = 512
POOL_WINDOWS = (2, 4, 8, 16)
POOL_GC = 128
N_HEADS = 4
QK_NOPE = 128
QK_ROPE = 64
V_HEAD = 128
QK_HEAD = QK_NOPE + QK_ROPE
HEAD_PAD = 256
Q_LORA = 384
KV_LORA = 256
IN_COLS = POOL_WIDTH + Q_LORA + KV_LORA + QK_ROPE
IN_PAD = 1280
ROPE_THETA = 10000.0
SOFTMAX_SCALE = 1.0 / math.sqrt(QK_HEAD)
EPS = 1e-6
N_MOD = 9

ADAM_LR = 0.001
ADAM_B1 = 0.9
ADAM_B2 = 0.999
ADAM_EPS = 1e-08
ADAM_WD = 0.01
ADAM_STEP = 10

LANE = 128
VMEM_LIMIT = 56 * 1024 * 1024

FF_SH = D_FF // N_DEV
OFF_G1, OFF_U1, OFF_D1 = 0, FF_SH, 2 * FF_SH
OFF_G2, OFF_U2, OFF_D2 = 3 * FF_SH, 4 * FF_SH, 5 * FF_SH
OFF_OUT = 6 * FF_SH
OFF_KV = OFF_OUT + 128
OFF_IN = OFF_KV + 32
OFF_Q = OFF_IN + 160
Q_PAD_ROWS = 64
ROWS_L = OFF_Q + Q_PAD_ROWS
IN_SH = IN_COLS // N_DEV
Q_SH_ROWS = (N_HEADS * QK_HEAD // N_DEV) * Q_LORA // D_MODEL
KV_SH_ROWS = (N_HEADS * (QK_NOPE + V_HEAD) // N_DEV) * KV_LORA // D_MODEL


def _tile(dim, target):
    if dim <= target:
        return dim
    best = None
    for t in range(LANE, target + 1, LANE):
        if dim % t == 0:
            best = t
    assert best is not None, (dim, target)
    return best


def _params(sem):
    return pltpu.CompilerParams(dimension_semantics=sem, vmem_limit_bytes=VMEM_LIMIT)


def _mesh_pos():
    return lax.axis_index("x"), lax.axis_index("y"), lax.axis_index("c")


def _all_gather(x, name):
    m, n = x.shape

    def body(x_ref, out_ref, send_sems, recv_sems, local_sem):
        px, py, pc = _mesh_pos()
        me, sibling = (px, py, pc), (px, py, 1 - pc)
        chips = [(1 - px, py), (px, 1 - py), (1 - px, 1 - py)]

        def rows(bx, by, bc):
            return out_ref.at[pl.ds((4 * bx + 2 * by + bc) * m, m), :]

        def copy(k, block, to, src=None):
            return pltpu.make_async_remote_copy(
                src_ref=rows(*block) if src is None else src, dst_ref=rows(*block),
                send_sem=send_sems.at[k], recv_sem=recv_sems.at[k],
                device_id=to, device_id_type=pl.DeviceIdType.MESH)

        mine = pltpu.make_async_copy(x_ref, rows(*me), local_sem)
        mine.start()
        first = [copy(0, me, sibling, src=x_ref)]
        first += [copy(1 + j, me, (*chip, pc), src=x_ref) for j, chip in enumerate(chips)]
        for cp in first:
            cp.start()
        passed = [copy(4 + j, (*chip, pc), sibling) for j, chip in enumerate(chips)]
        for j, chip in enumerate(chips):
            copy(1 + j, (*chip, pc), me).wait_recv()
            passed[j].start()
        copy(0, sibling, me).wait_recv()
        for j, chip in enumerate(chips):
            copy(4 + j, (*chip, 1 - pc), me).wait_recv()
        for cp in first + passed:
            cp.wait_send()
        mine.wait()

    hbm = pl.BlockSpec(memory_space=pltpu.HBM)
    return pl.pallas_call(
        body, name=name,
        out_shape=jax.ShapeDtypeStruct((N_DEV * m, n), x.dtype),
        in_specs=[hbm], out_specs=hbm,
        scratch_shapes=[pltpu.SemaphoreType.DMA((7,)), pltpu.SemaphoreType.DMA((7,)),
                        pltpu.SemaphoreType.DMA],
    )(x)


SMALL_ROWS = ROWS_L - OFF_KV
ROWS_A = [FF_SH] * 3
SPLIT_AB = sum(ROWS_A)
ROWS_TAIL = [128, SMALL_ROWS]
GB_F2, GB_F1, GB_TAIL = 0, SPLIT_AB, 2 * SPLIT_AB
HBM_SPEC = pl.BlockSpec(memory_space=pltpu.HBM)
SEM_SPEC = pl.BlockSpec(memory_space=pltpu.SEMAPHORE)
ANY_SPEC = pl.BlockSpec(memory_space=pl.ANY)
EFFECT = pltpu.SideEffectType.DATAFLOW_SIDE_EFFECTING


def _hbm(t):
    return pltpu.with_memory_space_constraint(t, pltpu.HBM)


def _whole_wait(ref, send_sem, recv_sem, peer):
    return pltpu.make_async_remote_copy(src_ref=ref, dst_ref=ref, send_sem=send_sem, recv_sem=recv_sem,
                                        device_id=peer, device_id_type=pl.DeviceIdType.MESH)


def _offsets(rows_list):
    return [sum(rows_list[:i]) for i in range(len(rows_list))]


def _gather_start(packed, rows_list, after, name):
    n = len(rows_list)
    offs = _offsets(rows_list)
    lands = [_hbm(lax.empty((N_DEV * rows, D_MODEL), BF16)) for rows in rows_list]

    def body(packed_ref, *refs):
        land = refs[:n]
        send_sems, recv_sems = refs[n + 1], refs[n + 2]
        token = refs[-1]
        px, py, pc = _mesh_pos()
        me = 4 * px + 2 * py + pc
        peers = [(px, py, 1 - pc), (1 - px, py, pc), (px, 1 - py, pc), (1 - px, 1 - py, pc)]
        for k, peer in enumerate(peers):
            for off, rows, land_ref in zip(offs, rows_list, land):
                pltpu.make_async_remote_copy(
                    src_ref=packed_ref.at[pl.ds(off, rows), :], dst_ref=land_ref.at[pl.ds(me * rows, rows), :],
                    send_sem=send_sems.at[k], recv_sem=recv_sems.at[k],
                    device_id=peer, device_id_type=pl.DeviceIdType.MESH).start()
        token[...] = jnp.zeros_like(token)

    outs = pl.pallas_call(
        body, name=name,
        out_shape=(pltpu.SemaphoreType.DMA((4,)), pltpu.SemaphoreType.DMA((4,)), pltpu.HBM(packed.shape, BF16),
                   *[pltpu.HBM(t.shape, BF16) for t in lands], jax.ShapeDtypeStruct((8, LANE), F32)),
        in_specs=(HBM_SPEC,) * (1 + n) + (ANY_SPEC,),
        out_specs=(SEM_SPEC, SEM_SPEC) + (HBM_SPEC,) * (1 + n) + (pl.BlockSpec(memory_space=pltpu.VMEM),),
        input_output_aliases={i: 2 + i for i in range(1 + n)},
        compiler_params=pltpu.CompilerParams(has_side_effects=EFFECT),
    )(_hbm(packed), *lands, after)
    return outs[0], outs[1], outs[2], list(outs[3:3 + n]), outs[-1]


def _gather_wait(send_sems, recv_sems, packed, lands, after, name):
    n = len(lands)

    def body(packed_ref, *refs):
        s_sems, r_sems = refs[n], refs[n + 1]
        me = _mesh_pos()
        for k in range(4):
            cp = _whole_wait(packed_ref, s_sems.at[k], r_sems.at[k], me)
            cp.wait_send()
            cp.wait_recv()

    outs = pl.pallas_call(
        body, name=name,
        out_shape=(pltpu.HBM(packed.shape, BF16), *[pltpu.HBM(t.shape, BF16) for t in lands]),
        in_specs=(HBM_SPEC,) * (1 + n) + (SEM_SPEC, SEM_SPEC, ANY_SPEC),
        out_specs=(HBM_SPEC,) * (1 + n),
        input_output_aliases={i: i for i in range(1 + n)},
        compiler_params=pltpu.CompilerParams(has_side_effects=EFFECT),
    )(packed, *lands, send_sems, recv_sems, after)
    return outs[0], list(outs[1:])


def _gather_finish(packed, rows_list, lands, name):
    n = len(rows_list)
    offs = _offsets(rows_list)

    def body(packed_ref, *refs):
        land = refs[n:2 * n]
        send_sems, recv_sems, stage, stage_sem = refs[2 * n:]
        px, py, pc = _mesh_pos()
        me = 4 * px + 2 * py + pc
        sibling = (px, py, 1 - pc)
        load = pltpu.make_async_copy(packed_ref, stage, stage_sem)
        load.start()
        load.wait()
        for off, rows, land_ref in zip(offs, rows_list, land):
            pltpu.make_async_copy(stage.at[pl.ds(off, rows), :], land_ref.at[pl.ds(me * rows, rows), :],
                                  stage_sem).start()
        for j, (cx, cy) in enumerate([(1 - px, py), (px, 1 - py), (1 - px, 1 - py)]):
            block = 4 * cx + 2 * cy + pc
            for rows, land_ref in zip(rows_list, land):
                blk = land_ref.at[pl.ds(block * rows, rows), :]
                pltpu.make_async_remote_copy(src_ref=blk, dst_ref=blk, send_sem=send_sems.at[j],
                                             recv_sem=recv_sems.at[j], device_id=sibling,
                                             device_id_type=pl.DeviceIdType.MESH).start()
        for j in range(3):
            cp = _whole_wait(packed_ref, send_sems.at[j], recv_sems.at[j], sibling)
            cp.wait_recv()
            cp.wait_send()
        pltpu.make_async_copy(stage, packed_ref, stage_sem).wait()

    outs = pl.pallas_call(
        body, name=name,
        out_shape=tuple(jax.ShapeDtypeStruct(t.shape, BF16) for t in lands),
        in_specs=(HBM_SPEC,) * (1 + n), out_specs=(HBM_SPEC,) * n,
        input_output_aliases={1 + i: i for i in range(n)},
        scratch_shapes=[pltpu.SemaphoreType.DMA((3,)), pltpu.SemaphoreType.DMA((3,)),
                        pltpu.VMEM(packed.shape, BF16), pltpu.SemaphoreType.DMA],
    )(packed, *lands)
    return list(outs)


N_CHIPS = 4


def _pair_start(srcs, rows_list, after, name):
    n = len(rows_list)
    offs = _offsets(rows_list)
    land = lax.empty((N_CHIPS, sum(rows_list), D_MODEL), BF16)

    def body(*refs):
        src, land_ref = refs[:n], refs[n]
        send_sems, recv_sems = refs[n + 2], refs[n + 3]
        token = refs[-1]
        px, py, pc = _mesh_pos()
        for k in range(N_CHIPS):
            block = 2 * k + (1 - pc)
            for off, rows, src_ref in zip(offs, rows_list, src):
                pltpu.make_async_remote_copy(
                    src_ref=src_ref.at[pl.ds(block * rows, rows), :], dst_ref=land_ref.at[k, pl.ds(off, rows), :],
                    send_sem=send_sems.at[0], recv_sem=recv_sems.at[0],
                    device_id=(px, py, 1 - pc), device_id_type=pl.DeviceIdType.MESH).start()
        token[...] = jnp.zeros_like(token)

    outs = pl.pallas_call(
        body, name=name,
        out_shape=(pltpu.SemaphoreType.DMA((1,)), pltpu.SemaphoreType.DMA((1,)),
                   *[pltpu.HBM(t.shape, BF16) for t in srcs], pltpu.HBM(land.shape, BF16),
                   jax.ShapeDtypeStruct((8, LANE), F32)),
        in_specs=(HBM_SPEC,) * (n + 1) + (ANY_SPEC,),
        out_specs=(SEM_SPEC, SEM_SPEC) + (HBM_SPEC,) * (n + 1) + (pl.BlockSpec(memory_space=pltpu.VMEM),),
        input_output_aliases={i: 2 + i for i in range(n + 1)},
        compiler_params=pltpu.CompilerParams(has_side_effects=EFFECT),
    )(*[_hbm(t) for t in srcs], _hbm(land), after)
    return outs[0], outs[1], list(outs[2:2 + n]), outs[2 + n], outs[-1]


def _split_wait(send_sems, recv_sems, n_sems, srcs, land, after, name):
    n = len(srcs)

    def body(*refs):
        land_ref = refs[n]
        s_sems, r_sems = refs[n + 1], refs[n + 2]
        me = _mesh_pos()
        for k in range(n_sems):
            cp = _whole_wait(land_ref.at[0] if n_sems > 1 else land_ref, s_sems.at[k], r_sems.at[k], me)
            cp.wait_send()
            cp.wait_recv()

    outs = pl.pallas_call(
        body, name=name,
        out_shape=(*[pltpu.HBM(t.shape, t.dtype) for t in srcs], pltpu.HBM(land.shape, land.dtype)),
        in_specs=(HBM_SPEC,) * (n + 1) + (SEM_SPEC, SEM_SPEC, ANY_SPEC),
        out_specs=(HBM_SPEC,) * (n + 1),
        input_output_aliases={i: i for i in range(n + 1)},
        compiler_params=pltpu.CompilerParams(has_side_effects=EFFECT),
    )(*srcs, land, send_sems, recv_sems, after)
    return list(outs[:n]), outs[n]


def _spread_start(x, me_id, after, name):
    land = lax.dynamic_update_slice_in_dim(lax.empty((N_DEV,) + x.shape, x.dtype), x[None], me_id, axis=0)

    def body(x_ref, land_ref, after_ref, send_sems, recv_sems, x_thru, land_thru, token):
        px, py, pc = _mesh_pos()
        me = 4 * px + 2 * py + pc
        for k in range(1, N_DEV):
            qx = 1 - px if k & 4 else px
            qy = 1 - py if k & 2 else py
            qc = 1 - pc if k & 1 else pc
            pltpu.make_async_remote_copy(
                src_ref=x_ref, dst_ref=land_ref.at[me], send_sem=send_sems.at[k - 1], recv_sem=recv_sems.at[k - 1],
                device_id=(qx, qy, qc), device_id_type=pl.DeviceIdType.MESH).start()
        token[...] = jnp.zeros_like(token)

    outs = pl.pallas_call(
        body, name=name,
        out_shape=(pltpu.SemaphoreType.DMA((N_DEV - 1,)), pltpu.SemaphoreType.DMA((N_DEV - 1,)),
                   pltpu.HBM(x.shape, x.dtype), pltpu.HBM(land.shape, land.dtype), jax.ShapeDtypeStruct((8, LANE), F32)),
        in_specs=(HBM_SPEC, HBM_SPEC, ANY_SPEC),
        out_specs=(SEM_SPEC, SEM_SPEC, HBM_SPEC, HBM_SPEC, pl.BlockSpec(memory_space=pltpu.VMEM)),
        input_output_aliases={0: 2, 1: 3},
        compiler_params=pltpu.CompilerParams(has_side_effects=EFFECT),
    )(_hbm(x), _hbm(land), after)
    return outs[0], outs[1], [outs[2]], outs[3], outs[4]


def _pair_sum(srcs, rows_list, land, core, name):
    n = len(rows_list)
    offs = _offsets(rows_list)
    total = sum(rows_list)

    def body(core_ref, *refs):
        src, land_ref, out_ref = refs[:n], refs[n], refs[n + 1]
        for off, rows, src_ref in zip(offs, rows_list, src):
            out_ref[pl.ds(off, rows), :] = (src_ref[...].astype(F32)
                                            + land_ref[pl.ds(off, rows), :].astype(F32)).astype(BF16)

    slot = pl.BlockSpec((None, total, D_MODEL), lambda k, c: (k, 0, 0))
    grid_spec = pltpu.PrefetchScalarGridSpec(
        num_scalar_prefetch=1, grid=(N_CHIPS,),
        in_specs=[pl.BlockSpec((rows, D_MODEL), lambda k, c: (2 * k + c[0], 0)) for rows in rows_list] + [slot],
        out_specs=slot)
    return pl.pallas_call(
        body, name=name, grid_spec=grid_spec,
        out_shape=jax.ShapeDtypeStruct((N_CHIPS, total, D_MODEL), BF16),
        compiler_params=_params(("parallel",)),
    )(core, *srcs, land)


def _chip_exchange_start(sums, chip, after, name):
    own = lax.dynamic_index_in_dim(sums, chip, axis=0, keepdims=True)
    recv = lax.dynamic_update_slice_in_dim(lax.empty(sums.shape, BF16), own, chip, axis=0)

    def body(sums_ref, recv_ref, after_ref, send_sems, recv_sems, sums_thru, recv_thru, token):
        px, py, pc = _mesh_pos()
        for k in range(1, N_CHIPS):
            qx = 1 - px if k & 2 else px
            qy = 1 - py if k & 1 else py
            pltpu.make_async_remote_copy(
                src_ref=sums_ref.at[2 * qx + qy], dst_ref=recv_ref.at[2 * px + py],
                send_sem=send_sems.at[k - 1], recv_sem=recv_sems.at[k - 1],
                device_id=(qx, qy, pc), device_id_type=pl.DeviceIdType.MESH).start()
        token[...] = jnp.zeros_like(token)

    outs = pl.pallas_call(
        body, name=name,
        out_shape=(pltpu.SemaphoreType.DMA((N_CHIPS - 1,)), pltpu.SemaphoreType.DMA((N_CHIPS - 1,)),
                   pltpu.HBM(sums.shape, BF16), pltpu.HBM(recv.shape, BF16), jax.ShapeDtypeStruct((8, LANE), F32)),
        in_specs=(HBM_SPEC, HBM_SPEC, ANY_SPEC),
        out_specs=(SEM_SPEC, SEM_SPEC, HBM_SPEC, HBM_SPEC, pl.BlockSpec(memory_space=pltpu.VMEM)),
        input_output_aliases={0: 2, 1: 3},
        compiler_params=pltpu.CompilerParams(has_side_effects=EFFECT),
    )(_hbm(sums), _hbm(recv), after)
    return outs[0], outs[1], [outs[2]], outs[3], outs[4]


def _sum_slots_into(recv, buf, layer, row_off, name):
    slots, r, n = recv.shape
    tr = _row_tile(math.gcd(r, row_off) if row_off else r, 512)
    first = row_off // tr

    def body(in_ref, buf_ref, out_ref):
        acc = in_ref[0].astype(F32)
        for j in range(1, slots):
            acc = acc + in_ref[j].astype(F32)
        out_ref[...] = acc

    return pl.pallas_call(
        body, name=name, grid=(r // tr,), out_shape=jax.ShapeDtypeStruct(buf.shape, F32),
        in_specs=[pl.BlockSpec((slots, tr, n), lambda i: (0, i, 0)), ANY_SPEC],
        out_specs=pl.BlockSpec((None, tr, n), lambda i: (layer, first + i, 0)),
        input_output_aliases={1: 0},
        compiler_params=_params(("parallel",)),
    )(recv, buf)


def _sum_slots(recv, name, after=None):
    _, r, n = recv.shape
    tr = _row_tile(r, 512)

    def body(in_ref, *refs):
        acc = in_ref[0].astype(F32)
        for j in range(1, N_DEV):
            acc = acc + in_ref[j].astype(F32)
        refs[-1][...] = acc

    grid = (r // tr,)
    in_specs, out_spec = [pl.BlockSpec((N_DEV, tr, n), lambda i: (0, i, 0))], pl.BlockSpec((tr, n), lambda i: (i, 0))
    args = [recv]
    if after is not None:
        in_specs.append(ANY_SPEC)
        args.append(after)
    return pl.pallas_call(
        body, name=name, grid=grid,
        out_shape=jax.ShapeDtypeStruct((r, n), F32),
        in_specs=in_specs, out_specs=out_spec,
        compiler_params=_params(("parallel",)),
    )(*args)


def _row_tile(rows, target):
    if rows <= target:
        return rows
    best = None
    for t in range(16, target + 1, 16):
        if rows % t == 0:
            best = t
    assert best is not None, rows
    return best


_DIMS = {"nn": ((1,), (0,)), "nt": ((1,), (1,)), "tn": ((0,), (0,))}


def _mm(a, b, mode, name, out_dtype=F32, res=None, gate=None, gate_factor=1.0, tm=512, tn=1408, after=None):
    assert (res is None) == (gate is None)
    if mode == "tn":
        kdim, m = a.shape
    else:
        m, kdim = a.shape
    n = b.shape[0] if mode == "nt" else b.shape[1]
    tm, tn = _tile(m, tm), _tile(n, tn)
    a_spec = (pl.BlockSpec((kdim, tm), lambda i, j: (0, i)) if mode == "tn"
              else pl.BlockSpec((tm, kdim), lambda i, j: (i, 0)))
    b_spec = (pl.BlockSpec((tn, kdim), lambda i, j: (j, 0)) if mode == "nt"
              else pl.BlockSpec((kdim, tn), lambda i, j: (0, j)))
    o_spec = pl.BlockSpec((tm, tn), lambda i, j: (i, j))
    dims = (_DIMS[mode], ((), ()))
    has_res = res is not None

    def body(a_ref, b_ref, *refs):
        y = lax.dot_general(a_ref[...].astype(BF16), b_ref[...].astype(BF16), dims,
                            preferred_element_type=F32)
        if has_res:
            res_ref, gate_ref = refs[0], refs[1]
            y_ref, o_ref = refs[-2], refs[-1]
            y_ref[...] = y.astype(BF16)
            o_ref[...] = res_ref[...] + (gate_factor * gate_ref[...]) * y
        else:
            refs[-1][...] = y.astype(out_dtype)

    in_specs, args = [a_spec, b_spec], [a, b]
    if has_res:
        gate_spec, gate_arg = _vec_in(gate, tile=tn)
        in_specs += [o_spec, gate_spec]
        args += [res, gate_arg]
        out_shape = (jax.ShapeDtypeStruct((m, n), BF16), jax.ShapeDtypeStruct((m, n), F32))
        out_specs = (o_spec, o_spec)
    else:
        out_shape, out_specs = jax.ShapeDtypeStruct((m, n), out_dtype), o_spec
    if after is not None:
        in_specs.append(ANY_SPEC)
        args.append(after)
    return pl.pallas_call(
        body, name=name, grid=(m // tm, n // tn), out_shape=out_shape,
        in_specs=in_specs, out_specs=out_specs,
        compiler_params=_params(("parallel", "parallel")),
    )(*args)


def _vec_in(v, tile=None):
    if isinstance(v, tuple):
        table, row = v
        if tile is None:
            return pl.BlockSpec((None, 1, table.shape[-1]), lambda *idx: (row, 0, 0)), table
        return pl.BlockSpec((None, 1, tile), lambda i, j: (row, 0, j)), table
    if tile is None:
        return pl.BlockSpec((1, v.shape[-1]), lambda *idx: (0, 0)), v
    return pl.BlockSpec((1, tile), lambda i, j: (0, j)), v


def _vec_spec(width):
    return pl.BlockSpec((1, width), lambda i: (0, 0))


def _rm_bwd(dh, x, dres, gw, scale, name, below=None):
    s, d = x.shape
    ts = _tile(s, 512)
    factor = None if below is None else below[2]

    def body(dh_ref, x_ref, dres_ref, gw_ref, sc_ref, *refs):
        dx_ref, dsh_ref, dsc_ref, dgw_ref = refs[-6:-2] if below is not None else refs[-4:]

        @pl.when(pl.program_id(0) == 0)
        def _():
            dsh_ref[...] = jnp.zeros_like(dsh_ref)
            dsc_ref[...] = jnp.zeros_like(dsc_ref)
            dgw_ref[...] = jnp.zeros_like(dgw_ref)
            if below is not None:
                refs[-1][...] = jnp.zeros_like(refs[-1])

        xv, dhv, gwv = x_ref[...], dh_ref[...], gw_ref[...]
        r = lax.rsqrt(jnp.mean(xv * xv, axis=-1, keepdims=True) + EPS)
        xn = xv * r
        y = xn * gwv
        dsh_ref[...] += jnp.sum(dhv, axis=0, keepdims=True)
        dsc_ref[...] += jnp.sum(dhv * y, axis=0, keepdims=True)
        dy = dhv * (1 + sc_ref[...])
        dgw_ref[...] += jnp.sum(dy * xn, axis=0, keepdims=True)
        dxn = dy * gwv
        dx = dres_ref[...] + r * (dxn - xn * jnp.mean(dxn * xn, axis=-1, keepdims=True))
        dx_ref[...] = dx
        if below is not None:
            yb_ref, gb_ref, dyb_ref, dgb_ref = refs[0], refs[1], refs[-2], refs[-1]
            dyb_ref[...] = ((factor * gb_ref[...]) * dx).astype(BF16)
            dgb_ref[...] += jnp.sum((factor * dx) * yb_ref[...].astype(F32), axis=0, keepdims=True)

    row = pl.BlockSpec((ts, d), lambda i: (i, 0))
    vec = jax.ShapeDtypeStruct((1, d), F32)
    (gw_spec, gw), (sc_spec, scale) = _vec_in(gw), _vec_in(scale)
    in_specs, args = [row, row, row, gw_spec, sc_spec], [dh, x, dres, gw, scale]
    out_shape = [jax.ShapeDtypeStruct((s, d), F32), vec, vec, vec]
    out_specs = [row, _vec_spec(d), _vec_spec(d), _vec_spec(d)]
    if below is not None:
        gate_spec, gate_arg = _vec_in(below[1])
        in_specs += [row, gate_spec]
        args += [below[0], gate_arg]
        out_shape += [jax.ShapeDtypeStruct((s, d), BF16), vec]
        out_specs += [row, _vec_spec(d)]
    return pl.pallas_call(
        body, name=name, grid=(s // ts,), out_shape=tuple(out_shape),
        in_specs=in_specs, out_specs=tuple(out_specs),
        compiler_params=_params(("arbitrary",)),
    )(*args)


def _norm_mm(x, gw, shift, scale, w, name, tm=1024):
    s, d = x.shape
    n = w.shape[0]
    tm = _tile(s, tm)

    def body(x_ref, gw_ref, sh_ref, sc_ref, w_ref, h_ref, z_ref):
        xv = x_ref[...]
        r = lax.rsqrt(jnp.mean(xv * xv, axis=-1, keepdims=True) + EPS)
        hb = (((xv * r) * gw_ref[...]) * (1 + sc_ref[...]) + sh_ref[...]).astype(BF16)
        h_ref[...] = hb
        z_ref[...] = lax.dot_general(hb, w_ref[...], (((1,), (1,)), ((), ())), preferred_element_type=F32)

    row = pl.BlockSpec((tm, d), lambda i: (i, 0))
    return pl.pallas_call(
        body, name=name, grid=(s // tm,),
        out_shape=(jax.ShapeDtypeStruct((s, d), BF16), jax.ShapeDtypeStruct((s, n), F32)),
        in_specs=[row, _vec_in(gw)[0], _vec_in(shift)[0], _vec_in(scale)[0], pl.BlockSpec((n, d), lambda i: (0, 0))],
        out_specs=(row, pl.BlockSpec((tm, n), lambda i: (i, 0))),
        compiler_params=_params(("parallel",)),
    )(x, _vec_in(gw)[1], _vec_in(shift)[1], _vec_in(scale)[1], w)


FFN_TM, FFN_TF = 2048, 256


def _ffn_up(x, gw, shift, scale, wg, wu, name, after=None):
    s, d = x.shape
    f = wg.shape[0]
    tm, tf = _tile(s, FFN_TM), _tile(f, FFN_TF)
    nt = (((1,), (1,)), ((), ()))

    def body(x_ref, gw_ref, sh_ref, sc_ref, wg_ref, wu_ref, *refs):
        h_ref, a_ref, b_ref, t_ref = refs[-4:]

        @pl.when(pl.program_id(1) == 0)
        def _():
            xv = x_ref[...]
            r = lax.rsqrt(jnp.mean(xv * xv, axis=-1, keepdims=True) + EPS)
            h_ref[...] = (((xv * r) * gw_ref[...]) * (1 + sc_ref[...]) + sh_ref[...]).astype(BF16)

        hb = h_ref[...]
        av = lax.dot_general(hb, wg_ref[...], nt, preferred_element_type=F32)
        bv = lax.dot_general(hb, wu_ref[...], nt, preferred_element_type=F32)
        a_ref[...] = av.astype(BF16)
        b_ref[...] = bv.astype(BF16)
        t_ref[...] = ((av * jax.nn.sigmoid(av)) * bv).astype(BF16)

    row = pl.BlockSpec((tm, d), lambda i, j: (i, 0))
    wblk = pl.BlockSpec((tf, d), lambda i, j: (j, 0))
    blk = pl.BlockSpec((tm, tf), lambda i, j: (i, j))
    wide = jax.ShapeDtypeStruct((s, f), BF16)
    vec_specs, vec_args = zip(*[_vec_in(v) for v in (gw, shift, scale)])
    in_specs, args = [row, *vec_specs, wblk, wblk], [x, *vec_args, wg, wu]
    if after is not None:
        in_specs.append(ANY_SPEC)
        args.append(after)
    return pl.pallas_call(
        body, name=name, grid=(s // tm, f // tf),
        out_shape=(jax.ShapeDtypeStruct((s, d), BF16), wide, wide, wide),
        in_specs=in_specs, out_specs=(row, blk, blk, blk),
        compiler_params=_params(("parallel", "arbitrary")),
    )(*args)


def _ffn_bwd_cols(dy, h, a, b, t, wd, name, after=None):
    s, d = dy.shape
    f = wd.shape[0]
    tf = _tile(f, FFN_TF)
    nt = (((1,), (1,)), ((), ()))
    tn = (((0,), (0,)), ((), ()))

    def body(dy_ref, h_ref, a_ref, b_ref, t_ref, wd_ref, *refs):
        da_ref, db_ref, gd_ref, gg_ref, gu_ref = refs[-5:]
        dyb, hb = dy_ref[...], h_ref[...]
        dtv = lax.dot_general(dyb, wd_ref[...], nt, preferred_element_type=F32)
        av, bv = a_ref[...].astype(F32), b_ref[...].astype(F32)
        sg = jax.nn.sigmoid(av)
        dbv = (dtv * (av * sg)).astype(BF16)
        dav = ((dtv * bv) * (sg * (1 + av * (1 - sg)))).astype(BF16)
        da_ref[...] = dav
        db_ref[...] = dbv
        gd_ref[...] = lax.dot_general(t_ref[...], dyb, tn, preferred_element_type=F32).astype(BF16)
        gg_ref[...] = lax.dot_general(dav, hb, tn, preferred_element_type=F32).astype(BF16)
        gu_ref[...] = lax.dot_general(dbv, hb, tn, preferred_element_type=F32).astype(BF16)

    whole = pl.BlockSpec((s, d), lambda j: (0, 0))
    col = pl.BlockSpec((s, tf), lambda j: (0, j))
    wblk = pl.BlockSpec((tf, d), lambda j: (j, 0))
    wide, wgrad = jax.ShapeDtypeStruct((s, f), BF16), jax.ShapeDtypeStruct((f, d), BF16)
    in_specs, args = [whole, whole, col, col, col, wblk], [dy, h, a, b, t, wd]
    if after is not None:
        in_specs.append(ANY_SPEC)
        args.append(after)
    return pl.pallas_call(
        body, name=name, grid=(f // tf,), out_shape=(wide, wide, wgrad, wgrad, wgrad),
        in_specs=in_specs, out_specs=(col, col, wblk, wblk, wblk),
        compiler_params=_params(("parallel",)),
    )(*args)


def _mm_pair(a1, b1, a2, b2, name, tm=1024, tn=512, after=None):
    m, kdim = a1.shape
    n = b1.shape[1]
    tm, tn = _tile(m, tm), _tile(n, tn)

    def body(a1_ref, b1_ref, a2_ref, b2_ref, *refs):
        refs[-1][...] = (jnp.dot(a1_ref[...], b1_ref[...], preferred_element_type=F32)
                         + jnp.dot(a2_ref[...], b2_ref[...], preferred_element_type=F32))

    a_spec = pl.BlockSpec((tm, kdim), lambda i, j: (i, 0))
    b_spec = pl.BlockSpec((kdim, tn), lambda i, j: (0, j))
    in_specs, args = [a_spec, b_spec, a_spec, b_spec], [a1, b1, a2, b2]
    if after is not None:
        in_specs.append(ANY_SPEC)
        args.append(after)
    return pl.pallas_call(
        body, name=name, grid=(m // tm, n // tn), out_shape=jax.ShapeDtypeStruct((m, n), F32),
        in_specs=in_specs, out_specs=pl.BlockSpec((tm, tn), lambda i, j: (i, j)),
        compiler_params=_params(("parallel", "parallel")),
    )(*args)


def _pool_counts(s):
    return (lax.broadcasted_iota(jnp.int32, (s, POOL_GC), 0))


def _pool_fwd(z, pool_w, pool_scale, name):
    s = z.shape[0]

    def body(u_ref, w_ref, sc_ref, y_ref, diff_ref):
        t = lax.broadcasted_iota(jnp.int32, (s, POOL_GC), 0)
        for g, win in enumerate(POOL_WINDOWS):
            cols = slice(g * POOL_GC, (g + 1) * POOL_GC)
            u = u_ref[:, cols]
            acc, step = u, 1
            while step < win:
                acc = acc + jnp.where(t >= step, pltpu.roll(acc, step, 0), 0.0)
                step *= 2
            cnt = jnp.minimum(t + 1, win).astype(F32)
            diff = acc / cnt - u
            diff_ref[:, cols] = diff
            ypre = jnp.dot(diff.astype(BF16), w_ref[g].astype(BF16), preferred_element_type=F32)
            y_ref[:, cols] = (ypre * sc_ref[:, cols]).astype(BF16)

    return pl.pallas_call(
        body, name=name, grid=(1,),
        out_shape=(jax.ShapeDtypeStruct((s, POOL_WIDTH), BF16), jax.ShapeDtypeStruct((s, POOL_WIDTH), F32)),
        in_specs=[pl.BlockSpec((s, POOL_WIDTH), lambda i: (0, 0)),
                  pl.BlockSpec(pool_w.shape, lambda i: (0, 0, 0)),
                  pl.BlockSpec((1, POOL_WIDTH), lambda i: (0, 0))],
        out_specs=(pl.BlockSpec((s, POOL_WIDTH), lambda i: (0, 0)),
                   pl.BlockSpec((s, POOL_WIDTH), lambda i: (0, 0))),
        compiler_params=_params(("arbitrary",)),
    )(z, pool_w, pool_scale)


def _pool_bwd(dycat, diff, pool_w, pool_scale, name):
    s = diff.shape[0]

    def body(dy_ref, diff_ref, w_ref, sc_ref, du_ref, dw_ref, dsc_ref):
        t = lax.broadcasted_iota(jnp.int32, (s, POOL_GC), 0)
        for g, win in enumerate(POOL_WINDOWS):
            cols = slice(g * POOL_GC, (g + 1) * POOL_GC)
            dy, dfb, wb = dy_ref[:, cols], diff_ref[:, cols].astype(BF16), w_ref[g].astype(BF16)
            ypre = jnp.dot(dfb, wb, preferred_element_type=F32)
            dsc_ref[:, cols] = jnp.sum(dy * ypre, axis=0, keepdims=True)
            dypre = (dy * sc_ref[:, cols]).astype(BF16)
            ddiff = lax.dot_general(dypre, wb, (((1,), (1,)), ((), ())), preferred_element_type=F32)
            dw_ref[g] = lax.dot_general(dfb, dypre, (((0,), (0,)), ((), ())), preferred_element_type=F32)
            cnt = jnp.minimum(t + 1, win).astype(F32)
            acc, step = ddiff / cnt, 1
            while step < win:
                acc = acc + jnp.where(t < s - step, pltpu.roll(acc, s - step, 0), 0.0)
                step *= 2
            du_ref[:, cols] = acc - ddiff

    full = pl.BlockSpec((s, POOL_WIDTH), lambda i: (0, 0))
    return pl.pallas_call(
        body, name=name, grid=(1,),
        out_shape=(jax.ShapeDtypeStruct((s, POOL_WIDTH), F32),
                   jax.ShapeDtypeStruct(pool_w.shape, F32),
                   jax.ShapeDtypeStruct((1, POOL_WIDTH), F32)),
        in_specs=[full, full, pl.BlockSpec(pool_w.shape, lambda i: (0, 0, 0)),
                  pl.BlockSpec((1, POOL_WIDTH), lambda i: (0, 0))],
        out_specs=(full, pl.BlockSpec(pool_w.shape, lambda i: (0, 0, 0)),
                   pl.BlockSpec((1, POOL_WIDTH), lambda i: (0, 0))),
        compiler_params=_params(("arbitrary",)),
    )(dycat, diff, pool_w, pool_scale)


def _rope_tables(positions, name):
    s = positions.shape[0]
    ts = _tile(s, 512)
    freq = 1.0 / (ROPE_THETA ** (np.arange(0, QK_ROPE, 2, dtype=np.float32) / QK_ROPE))
    table = np.zeros((1, LANE), np.float32)
    table[0, :QK_ROPE // 2] = freq
    table[0, QK_ROPE // 2:QK_ROPE] = freq

    def body(pos_ref, f_ref, cos_ref, sin_ref):
        ang = pos_ref[...].astype(F32) * f_ref[...]
        cos_ref[...] = jnp.cos(ang)
        sin_ref[...] = jnp.sin(ang)

    out = jax.ShapeDtypeStruct((s, LANE), F32)
    blk = pl.BlockSpec((ts, LANE), lambda i: (i, 0))
    return pl.pallas_call(
        body, name=name, grid=(s // ts,), out_shape=(out, out),
        in_specs=[pl.BlockSpec((ts, 1), lambda i: (i, 0)), _vec_spec(LANE)], out_specs=(blk, blk),
        compiler_params=_params(("parallel",)),
    )(positions, jnp.asarray(table))


def _lane_mod64_low(shape):
    return (lax.broadcasted_iota(jnp.int32, shape, 1) % QK_ROPE) < (QK_ROPE // 2)


def _rope(x, cos, sin):
    rot = jnp.where(_lane_mod64_low(x.shape), -pltpu.roll(x, LANE - 32, 1), pltpu.roll(x, 32, 1))
    return x * cos + rot * sin


def _rope_t(dy, cos, sin):
    w = dy * sin
    rot_t = jnp.where(_lane_mod64_low(dy.shape), pltpu.roll(w, LANE - 32, 1), -pltpu.roll(w, 32, 1))
    return dy * cos + rot_t


def _plain_rms(x, g):
    r = lax.rsqrt(jnp.mean(x * x, axis=-1, keepdims=True) + EPS)
    return (x * r) * g, x * r, r


O_Q, O_KV, O_KR = POOL_WIDTH, POOL_WIDTH + Q_LORA, POOL_WIDTH + Q_LORA + KV_LORA


def _qkv_fwd(z, qn, kvn, wq, wkv, cos, sin, name):
    s = z.shape[0]
    ts = _tile(s, 512)

    def body(z_ref, qn_ref, kvn_ref, wq_ref, wkv_ref, cos_ref, sin_ref, q_ref, k_ref, v_ref, cqn_ref, ckvn_ref):
        cosv, sinv = cos_ref[...], sin_ref[...]
        cqn = _plain_rms(z_ref[:, O_Q:O_KV], qn_ref[...])[0].astype(BF16)
        ckvn = _plain_rms(z_ref[:, O_KV:O_KR], kvn_ref[...])[0].astype(BF16)
        cqn_ref[...] = cqn
        ckvn_ref[...] = ckvn
        nt = (((1,), (1,)), ((), ()))
        q = lax.dot_general(cqn, wq_ref[...], nt, preferred_element_type=F32)
        kv = lax.dot_general(ckvn, wkv_ref[...], nt, preferred_element_type=F32)
        kr = _rope(z_ref[:, O_KR:IN_PAD], cosv, sinv).astype(BF16)
        for h in range(N_HEADS):
            o = h * HEAD_PAD
            q_ref[:, o:o + QK_NOPE] = q[:, o:o + QK_NOPE].astype(BF16)
            q_ref[:, o + QK_NOPE:o + HEAD_PAD] = _rope(q[:, o + QK_NOPE:o + HEAD_PAD], cosv, sinv).astype(BF16)
            k_ref[:, o:o + QK_NOPE] = kv[:, o:o + QK_NOPE].astype(BF16)
            k_ref[:, o + QK_NOPE:o + HEAD_PAD] = kr
            v_ref[:, h * V_HEAD:(h + 1) * V_HEAD] = kv[:, o + QK_NOPE:o + HEAD_PAD].astype(BF16)

    def row(w):
        return pl.BlockSpec((ts, w), lambda i: (i, 0))

    def whole(arr):
        return pl.BlockSpec(arr.shape, lambda i: (0, 0))

    hp = N_HEADS * HEAD_PAD
    return pl.pallas_call(
        body, name=name, grid=(s // ts,),
        out_shape=(jax.ShapeDtypeStruct((s, hp), BF16), jax.ShapeDtypeStruct((s, hp), BF16),
                   jax.ShapeDtypeStruct((s, N_HEADS * V_HEAD), BF16),
                   jax.ShapeDtypeStruct((s, Q_LORA), BF16), jax.ShapeDtypeStruct((s, KV_LORA), BF16)),
        in_specs=[row(IN_PAD), whole(qn), whole(kvn), whole(wq), whole(wkv), row(LANE), row(LANE)],
        out_specs=(row(hp), row(hp), row(N_HEADS * V_HEAD), row(Q_LORA), row(KV_LORA)),
        compiler_params=_params(("parallel",)),
    )(z, qn, kvn, wq, wkv, cos, sin)


def _qkv_bwd(dq, dk, dv, du, z, qn, kvn, wq, wkv, cos, sin, name):
    s = z.shape[0]
    ts = _tile(s, 512)

    def norm_bwd(x, g, dy):
        _, xn, r = _plain_rms(x, g)
        dxn = dy * g
        return r * (dxn - xn * jnp.mean(dxn * xn, axis=-1, keepdims=True)), jnp.sum(dy * xn, axis=0, keepdims=True)

    def body(dq_ref, dk_ref, dv_ref, du_ref, z_ref, qn_ref, kvn_ref, wq_ref, wkv_ref, cos_ref, sin_ref,
             dz_ref, dqb_ref, dkvb_ref, dqn_ref, dkvn_ref):
        @pl.when(pl.program_id(0) == 0)
        def _():
            dqn_ref[...] = jnp.zeros_like(dqn_ref)
            dkvn_ref[...] = jnp.zeros_like(dkvn_ref)

        cosv, sinv = cos_ref[...], sin_ref[...]
        dkr = jnp.zeros((ts, LANE), F32)
        for h in range(N_HEADS):
            o = h * HEAD_PAD
            dqb_ref[:, o:o + QK_NOPE] = dq_ref[:, o:o + QK_NOPE].astype(BF16)
            dqb_ref[:, o + QK_NOPE:o + HEAD_PAD] = _rope_t(dq_ref[:, o + QK_NOPE:o + HEAD_PAD], cosv, sinv).astype(BF16)
            dkvb_ref[:, o:o + QK_NOPE] = dk_ref[:, o:o + QK_NOPE].astype(BF16)
            dkvb_ref[:, o + QK_NOPE:o + HEAD_PAD] = dv_ref[:, h * V_HEAD:(h + 1) * V_HEAD].astype(BF16)
            dkr = dkr + dk_ref[:, o + QK_NOPE:o + HEAD_PAD]
        dcqn = jnp.dot(dqb_ref[...], wq_ref[...], preferred_element_type=F32)
        dckvn = jnp.dot(dkvb_ref[...], wkv_ref[...], preferred_element_type=F32)
        dcq, dqn = norm_bwd(z_ref[:, O_Q:O_KV], qn_ref[...], dcqn)
        dckv, dkvn = norm_bwd(z_ref[:, O_KV:O_KR], kvn_ref[...], dckvn)
        dqn_ref[...] += dqn
        dkvn_ref[...] += dkvn
        dz_ref[:, 0:O_Q] = du_ref[...].astype(BF16)
        dz_ref[:, O_Q:O_KV] = dcq.astype(BF16)
        dz_ref[:, O_KV:O_KR] = dckv.astype(BF16)
        dz_ref[:, O_KR:IN_PAD] = _rope_t(dkr, cosv, sinv).astype(BF16)

    def row(w):
        return pl.BlockSpec((ts, w), lambda i: (i, 0))

    def whole(arr):
        return pl.BlockSpec(arr.shape, lambda i: (0, 0))

    hp = N_HEADS * HEAD_PAD
    return pl.pallas_call(
        body, name=name, grid=(s // ts,),
        out_shape=(jax.ShapeDtypeStruct((s, IN_PAD), BF16), jax.ShapeDtypeStruct((s, hp), BF16),
                   jax.ShapeDtypeStruct((s, hp), BF16),
                   jax.ShapeDtypeStruct((1, Q_LORA), F32), jax.ShapeDtypeStruct((1, KV_LORA), F32)),
        in_specs=[row(hp), row(hp), row(N_HEADS * V_HEAD), row(POOL_WIDTH), row(IN_PAD),
                  whole(qn), whole(kvn), whole(wq), whole(wkv), row(LANE), row(LANE)],
        out_specs=(row(IN_PAD), row(hp), row(hp), whole(qn), whole(kvn)),
        compiler_params=_params(("arbitrary",)),
    )(dq, dk, dv, du, z, qn, kvn, wq, wkv, cos, sin)


def _causal_scores(q, k, i, tq, klen):
    sc = lax.dot_general(q, k, (((1,), (1,)), ((), ())), preferred_element_type=F32) * SOFTMAX_SCALE
    qpos = i * tq + lax.broadcasted_iota(jnp.int32, (tq, klen), 0)
    kpos = lax.broadcasted_iota(jnp.int32, (tq, klen), 1)
    return jnp.where(qpos >= kpos, sc, -jnp.inf)


ATTN_TQ = 512
ATTN_SEGMENTS = 4


def _by_key_prefix(i, nq, tq, compute):
    nseg = min(ATTN_SEGMENTS, nq)
    per = nq // nseg
    for r in range(nseg):
        pl.when(i // per == r)(lambda r=r: compute((r + 1) * per * tq))


def _attn_fwd(q, k, v, name):
    s = q.shape[0]
    tq = _tile(s, ATTN_TQ)
    nq = s // tq

    def body(q_ref, k_ref, v_ref, o_ref, lse_ref):
        i = pl.program_id(1)

        def compute(klen):
            sc = _causal_scores(q_ref[...], k_ref[0:klen, :], i, tq, klen)
            mx = jnp.max(sc, axis=-1, keepdims=True)
            p = jnp.exp(sc - mx)
            den = jnp.sum(p, axis=-1, keepdims=True)
            o_ref[...] = jnp.dot((p / den).astype(BF16), v_ref[0:klen, :], preferred_element_type=F32)
            lse_ref[...] = mx + jnp.log(den)

        _by_key_prefix(i, nq, tq, compute)

    return pl.pallas_call(
        body, name=name, grid=(N_HEADS, s // tq),
        out_shape=(jax.ShapeDtypeStruct((s, N_HEADS * V_HEAD), F32), jax.ShapeDtypeStruct((N_HEADS, s, 1), F32)),
        in_specs=[pl.BlockSpec((tq, HEAD_PAD), lambda h, i: (i, h)),
                  pl.BlockSpec((s, HEAD_PAD), lambda h, i: (0, h)),
                  pl.BlockSpec((s, V_HEAD), lambda h, i: (0, h))],
        out_specs=(pl.BlockSpec((tq, V_HEAD), lambda h, i: (i, h)),
                   pl.BlockSpec((None, tq, 1), lambda h, i: (h, i, 0))),
        compiler_params=_params(("parallel", "parallel")),
    )(q, k, v)


def _attn_bwd(q, k, v, lse, dycat, name):
    s = q.shape[0]
    tq = _tile(s, ATTN_TQ)
    nq = s // tq
    tn_dims = (((0,), (0,)), ((), ()))

    def body(q_ref, k_ref, v_ref, lse_ref, do_ref, dq_ref, dk_ref, dv_ref):
        i = pl.program_id(1)

        @pl.when(i == 0)
        def _():
            dk_ref[...] = jnp.zeros_like(dk_ref)
            dv_ref[...] = jnp.zeros_like(dv_ref)

        def compute(klen):
            qv, kv_, dob = q_ref[...], k_ref[0:klen, :], do_ref[...].astype(BF16)
            sc = _causal_scores(qv, kv_, i, tq, klen)
            p = jnp.exp(sc - lse_ref[...])
            dp = lax.dot_general(dob, v_ref[0:klen, :], (((1,), (1,)), ((), ())), preferred_element_type=F32)
            ds = (p * (dp - jnp.sum(dp * p, axis=-1, keepdims=True)) * SOFTMAX_SCALE).astype(BF16)
            dq_ref[...] = jnp.dot(ds, kv_, preferred_element_type=F32)
            dk_ref[0:klen, :] += lax.dot_general(ds, qv, tn_dims, preferred_element_type=F32)
            dv_ref[0:klen, :] += lax.dot_general(p.astype(BF16), dob, tn_dims, preferred_element_type=F32)

        _by_key_prefix(i, nq, tq, compute)

    n_pool_blocks = POOL_WIDTH // V_HEAD
    return pl.pallas_call(
        body, name=name, grid=(N_HEADS, s // tq),
        out_shape=(jax.ShapeDtypeStruct((s, N_HEADS * HEAD_PAD), F32),
                   jax.ShapeDtypeStruct((s, N_HEADS * HEAD_PAD), F32),
                   jax.ShapeDtypeStruct((s, N_HEADS * V_HEAD), F32)),
        in_specs=[pl.BlockSpec((tq, HEAD_PAD), lambda h, i: (i, h)),
                  pl.BlockSpec((s, HEAD_PAD), lambda h, i: (0, h)),
                  pl.BlockSpec((s, V_HEAD), lambda h, i: (0, h)),
                  pl.BlockSpec((None, tq, 1), lambda h, i: (h, i, 0)),
                  pl.BlockSpec((tq, V_HEAD), lambda h, i: (i, n_pool_blocks + h))],
        out_specs=(pl.BlockSpec((tq, HEAD_PAD), lambda h, i: (i, h)),
                   pl.BlockSpec((s, HEAD_PAD), lambda h, i: (0, h)),
                   pl.BlockSpec((s, V_HEAD), lambda h, i: (0, h))),
        compiler_params=_params(("parallel", "arbitrary")),
    )(q, k, v, lse, dycat)


def _loss_head(x, gw, target, below, name):
    s, d = x.shape
    ts = _tile(s, 512)
    factor = below[2]

    def body(x_ref, gw_ref, tgt_ref, yb_ref, gb_ref, loss_ref, dx_ref, dgw_ref, dyb_ref, dgb_ref):
        @pl.when(pl.program_id(0) == 0)
        def _():
            loss_ref[...] = jnp.zeros_like(loss_ref)
            dgw_ref[...] = jnp.zeros_like(dgw_ref)
            dgb_ref[...] = jnp.zeros_like(dgb_ref)

        xv, gwv = x_ref[...], gw_ref[...]
        r = lax.rsqrt(jnp.mean(xv * xv, axis=-1, keepdims=True) + EPS)
        xn = xv * r
        err = xn * gwv - tgt_ref[...]
        loss_ref[...] += 0.5 * jnp.sum(jnp.mean(err * err, axis=-1, keepdims=True))
        dy = err / d
        dgw_ref[...] += jnp.sum(dy * xn, axis=0, keepdims=True)
        dxn = dy * gwv
        dx = r * (dxn - xn * jnp.mean(dxn * xn, axis=-1, keepdims=True))
        dx_ref[...] = dx
        dyb_ref[...] = ((factor * gb_ref[...]) * dx).astype(BF16)
        dgb_ref[...] += jnp.sum((factor * dx) * yb_ref[...].astype(F32), axis=0, keepdims=True)

    row = pl.BlockSpec((ts, d), lambda i: (i, 0))
    gate_spec, gate_arg = _vec_in(below[1])
    vec = jax.ShapeDtypeStruct((1, d), F32)
    return pl.pallas_call(
        body, name=name, grid=(s // ts,),
        out_shape=(jax.ShapeDtypeStruct((8, LANE), F32), jax.ShapeDtypeStruct((s, d), F32), vec,
                   jax.ShapeDtypeStruct((s, d), BF16), vec),
        in_specs=[row, _vec_spec(d), row, row, gate_spec],
        out_specs=(pl.BlockSpec((8, LANE), lambda i: (0, 0)), row, _vec_spec(d), row, _vec_spec(d)),
        compiler_params=_params(("arbitrary",)),
    )(x, gw, target, below[0], gate_arg)


def _ada_mod(c_all, ada_w, ada_b, name):
    nl, d, cols = ada_w.shape

    def body(c_ref, w_ref, b_ref, o_ref):
        cv = c_ref[...]
        act = (cv * jax.nn.sigmoid(cv)).astype(BF16)
        o_ref[...] = jnp.dot(act, w_ref[...].astype(BF16), preferred_element_type=F32) + b_ref[...]

    return pl.pallas_call(
        body, name=name, grid=(nl,), out_shape=jax.ShapeDtypeStruct((nl, N_DEV, cols), F32),
        in_specs=[pl.BlockSpec((N_DEV, d), lambda l: (0, 0)),
                  pl.BlockSpec((None, d, cols), lambda l: (l, 0, 0)),
                  pl.BlockSpec((None, 1, cols), lambda l: (l, 0, 0))],
        out_specs=pl.BlockSpec((None, N_DEV, cols), lambda l: (l, 0, 0)),
        compiler_params=_params(("parallel",)),
    )(c_all, ada_w, ada_b)


def _ada_grad(c_pad, dmod_pad, name):
    nl, kpad, cols = dmod_pad.shape
    d = c_pad.shape[1]

    def body(c_ref, dm_ref, o_ref):
        cv = c_ref[...]
        act = (cv * jax.nn.sigmoid(cv)).astype(BF16)
        o_ref[...] = lax.dot_general(act, dm_ref[...].astype(BF16), (((0,), (0,)), ((), ())),
                                     preferred_element_type=F32)

    return pl.pallas_call(
        body, name=name, grid=(nl,), out_shape=jax.ShapeDtypeStruct((nl, d, cols), F32),
        in_specs=[pl.BlockSpec((kpad, d), lambda l: (0, 0)),
                  pl.BlockSpec((None, kpad, cols), lambda l: (l, 0, 0))],
        out_specs=pl.BlockSpec((None, d, cols), lambda l: (l, 0, 0)),
        compiler_params=_params(("parallel",)),
    )(c_pad, dmod_pad)


def _adamw_math(w, g, m, v):
    nm = ADAM_B1 * m + (1.0 - ADAM_B1) * g
    nv = ADAM_B2 * v + (1.0 - ADAM_B2) * (g * g)
    m_hat = nm / (1.0 - ADAM_B1 ** ADAM_STEP)
    v_hat = nv / (1.0 - ADAM_B2 ** ADAM_STEP)
    return -ADAM_LR * (m_hat / (jnp.sqrt(v_hat) + ADAM_EPS) + ADAM_WD * w), nm, nv


def _adamw_rows(w3, gbuf, row_off, m3, v3, name):
    nl, r, d = w3.shape
    tr = _row_tile(math.gcd(r, row_off) if row_off else r, 352)
    first = row_off // tr

    def body(w_ref, g_ref, m_ref, v_ref, go_ref, d_ref, nm_ref, nv_ref):
        gv = g_ref[...]
        go_ref[...] = gv
        d_ref[...], nm_ref[...], nv_ref[...] = _adamw_math(w_ref[...], gv, m_ref[...], v_ref[...])

    blk = pl.BlockSpec((None, tr, d), lambda l, i: (l, i, 0))
    gblk = pl.BlockSpec((None, tr, d), lambda l, i: (l, first + i, 0))
    out = jax.ShapeDtypeStruct((nl, r, d), F32)
    return pl.pallas_call(
        body, name=name, grid=(nl, r // tr), out_shape=(out, out, out, out),
        in_specs=[blk, gblk, blk, blk], out_specs=(blk, blk, blk, blk),
        compiler_params=_params(("parallel", "parallel")),
    )(w3, gbuf, m3, v3)


def _adamw(w, g, m, v, name):
    rows, cols = w.shape
    tr = _row_tile(rows, 512)

    def body(w_ref, g_ref, m_ref, v_ref, d_ref, nm_ref, nv_ref):
        d_ref[...], nm_ref[...], nv_ref[...] = _adamw_math(w_ref[...], g_ref[...], m_ref[...], v_ref[...])

    blk = pl.BlockSpec((tr, cols), lambda i: (i, 0))
    out = jax.ShapeDtypeStruct((rows, cols), F32)
    return pl.pallas_call(
        body, name=name, grid=(rows // tr,), out_shape=(out, out, out),
        in_specs=[blk, blk, blk, blk], out_specs=(blk, blk, blk),
        compiler_params=_params(("parallel",)),
    )(w, g, m, v)


def _adamw_nd(w, g, m, v, name):
    shape = w.shape
    flat = (lambda t: t.reshape(1, -1)) if w.ndim == 1 else (lambda t: t.reshape(-1, shape[-1]))
    return tuple(t.reshape(shape) for t in _adamw(flat(w), flat(g), flat(m), flat(v), name))


def _pad_rows(t, rows):
    return jnp.pad(t, ((0, rows - t.shape[0]), (0, 0)))


def _pack_shard_layer(l, wts):
    def tr(name):
        return wts[name][l].astype(BF16).T

    parts = [tr("ffn1_w_gate"), tr("ffn1_w_up"), wts["ffn1_w_down"][l].astype(BF16),
             tr("ffn2_w_gate"), tr("ffn2_w_up"), wts["ffn2_w_down"][l].astype(BF16),
             wts["w_out"][l].astype(BF16),
             tr("w_kv_b").reshape(KV_SH_ROWS, D_MODEL),
             _pad_rows(tr("w_in"), 160),
             _pad_rows(tr("w_q_b").reshape(Q_SH_ROWS, D_MODEL), Q_PAD_ROWS)]
    return jnp.concatenate(parts, axis=0)


def _mixer_weights(w_out, small):
    w = {"out": w_out}
    small = small.reshape(N_DEV, SMALL_ROWS, D_MODEL)
    o_in, o_q = OFF_IN - OFF_KV, OFF_Q - OFF_KV
    w["kv"] = small[:, :KV_SH_ROWS].reshape(N_HEADS * HEAD_PAD, KV_LORA)
    w["in"] = _pad_rows(small[:, o_in:o_in + IN_SH].reshape(IN_COLS, D_MODEL), IN_PAD)
    wq = small[:, o_q:o_q + Q_SH_ROWS].reshape(N_HEADS, QK_HEAD, Q_LORA)
    w["q"] = jnp.pad(wq, ((0, 0), (0, HEAD_PAD - QK_HEAD), (0, 0))).reshape(N_HEADS * HEAD_PAD, Q_LORA)
    return w


def _grad_sources_tail(gr):
    gq = gr["q"].reshape(N_HEADS, HEAD_PAD, Q_LORA)[:, :QK_HEAD].reshape(N_DEV, Q_SH_ROWS, D_MODEL)
    small = jnp.concatenate([
        gr["kv"].reshape(N_DEV, KV_SH_ROWS, D_MODEL),
        jnp.pad(gr["in"][:IN_COLS].reshape(N_DEV, IN_SH, D_MODEL), ((0, 0), (0, 160 - IN_SH), (0, 0))),
        jnp.pad(gq, ((0, 0), (0, Q_PAD_ROWS - Q_SH_ROWS), (0, 0)))], axis=1)
    return [gr["out"], small.reshape(N_DEV * SMALL_ROWS, D_MODEL)]


def _pack_bf16_pairs(t):
    rows, d = t.shape
    return lax.bitcast_convert_type(t.astype(BF16).reshape(rows // 2, 2, d).transpose(0, 2, 1), F32)


def _unpack_bf16_pairs(p):
    pairs = jnp.swapaxes(lax.bitcast_convert_type(p, BF16), -1, -2)
    return pairs.reshape(p.shape[:-2] + (2 * p.shape[-2], p.shape[-1]))


def _small_layout(nl):
    names = [("dmod", nl * N_MOD), ("ffn1_norm", nl), ("mix_norm", nl), ("ffn2_norm", nl), ("q_a_norm", nl),
             ("kv_a_norm", nl), ("pool_scale", nl), ("final_norm", 1), ("loss", 1),
             ("pool_w", nl * 4 * POOL_GC * POOL_GC // D_MODEL // 2)]
    off, table = 0, {}
    for name, n in names:
        table[name] = (off, n)
        off += -(-n // 8) * 8
    return table, off


def _to_rows(t, width=D_MODEL):
    n, w = t.shape
    return jnp.pad(t, ((0, -(-n // 8) * 8 - n), (0, width - w)))


def kernel(x, c, positions, ada_w, ada_b, ffn1_norm, ffn1_w_gate, ffn1_w_up, ffn1_w_down, mix_norm, w_in, pool_w, pool_scale, q_a_norm, w_q_b, kv_a_norm, w_kv_b, w_out, ffn2_norm, ffn2_w_gate, ffn2_w_up, ffn2_w_down, final_norm, loss_target, m_ada_w, m_ada_b, m_ffn1_norm, m_ffn1_w_gate, m_ffn1_w_up, m_ffn1_w_down, m_mix_norm, m_w_in, m_pool_w, m_pool_scale, m_q_a_norm, m_w_q_b, m_kv_a_norm, m_w_kv_b, m_w_out, m_ffn2_norm, m_ffn2_w_gate, m_ffn2_w_up, m_ffn2_w_down, m_final_norm, v_ada_w, v_ada_b, v_ffn1_norm, v_ffn1_w_gate, v_ffn1_w_up, v_ffn1_w_down, v_mix_norm, v_w_in, v_pool_w, v_pool_scale, v_q_a_norm, v_w_q_b, v_kv_a_norm, v_w_kv_b, v_w_out, v_ffn2_norm, v_ffn2_w_gate, v_ffn2_w_up, v_ffn2_w_down, v_final_norm):
    wts = dict(ada_w=ada_w, ada_b=ada_b, ffn1_norm=ffn1_norm, ffn1_w_gate=ffn1_w_gate, ffn1_w_up=ffn1_w_up,
               ffn1_w_down=ffn1_w_down, mix_norm=mix_norm, w_in=w_in, pool_w=pool_w, pool_scale=pool_scale,
               q_a_norm=q_a_norm, w_q_b=w_q_b, kv_a_norm=kv_a_norm, w_kv_b=w_kv_b, w_out=w_out,
               ffn2_norm=ffn2_norm, ffn2_w_gate=ffn2_w_gate, ffn2_w_up=ffn2_w_up, ffn2_w_down=ffn2_w_down,
               final_norm=final_norm)
    mom_m = dict(ada_w=m_ada_w, ada_b=m_ada_b, ffn1_norm=m_ffn1_norm, ffn1_w_gate=m_ffn1_w_gate,
                 ffn1_w_up=m_ffn1_w_up, ffn1_w_down=m_ffn1_w_down, mix_norm=m_mix_norm, w_in=m_w_in,
                 pool_w=m_pool_w, pool_scale=m_pool_scale, q_a_norm=m_q_a_norm, w_q_b=m_w_q_b,
                 kv_a_norm=m_kv_a_norm, w_kv_b=m_w_kv_b, w_out=m_w_out, ffn2_norm=m_ffn2_norm,
                 ffn2_w_gate=m_ffn2_w_gate, ffn2_w_up=m_ffn2_w_up, ffn2_w_down=m_ffn2_w_down,
                 final_norm=m_final_norm)
    mom_v = dict(ada_w=v_ada_w, ada_b=v_ada_b, ffn1_norm=v_ffn1_norm, ffn1_w_gate=v_ffn1_w_gate,
                 ffn1_w_up=v_ffn1_w_up, ffn1_w_down=v_ffn1_w_down, mix_norm=v_mix_norm, w_in=v_w_in,
                 pool_w=v_pool_w, pool_scale=v_pool_scale, q_a_norm=v_q_a_norm, w_q_b=v_w_q_b,
                 kv_a_norm=v_kv_a_norm, w_kv_b=v_w_kv_b, w_out=v_w_out, ffn2_norm=v_ffn2_norm,
                 ffn2_w_gate=v_ffn2_w_gate, ffn2_w_up=v_ffn2_w_up, ffn2_w_down=v_ffn2_w_down,
                 final_norm=v_final_norm)
    order = list(wts)
    nl = ada_w.shape[0]
    seq = x.shape[1]
    me = 4 * lax.axis_index("x") + 2 * lax.axis_index("y") + lax.axis_index("c")
    ada_cols = ada_w.shape[2]

    def after_token(t, token):
        return t + token[0:1, 0:1].astype(t.dtype)

    packs = [_pack_shard_layer(l, wts) for l in range(nl)]

    c_all = _all_gather(jnp.broadcast_to(c, (8, D_MODEL)), "gather_c")[::8]

    ada_b_mine = lax.dynamic_slice_in_dim(ada_b, me * ada_cols, ada_cols, axis=1).reshape(nl, 1, ada_cols)
    mod_part = _ada_mod(c_all, ada_w, ada_b_mine, "ada_mod")
    mod_all = _all_gather(mod_part.reshape(nl * N_DEV, ada_cols), "gather_mod")
    mod_all = mod_all.reshape(N_DEV, nl, N_DEV, ada_cols)
    mod = lax.dynamic_index_in_dim(mod_all, me, axis=2, keepdims=False)
    mod = mod.transpose(1, 0, 2).reshape(nl * N_MOD, 1, D_MODEL)
    norm_tables = {name: wts[name].reshape(nl, 1, D_MODEL) for name in ("ffn1_norm", "mix_norm", "ffn2_norm")}

    def modrow(l, k):
        return mod, l * N_MOD + k

    def normrow(name, l):
        return norm_tables[name], l

    def start_layer(l, after):
        first = _gather_start(packs[l][:SPLIT_AB], ROWS_A, after, f"gather_start_{l}a")
        mixer = _gather_start(packs[l][OFF_OUT:], ROWS_TAIL, first[4], f"gather_start_{l}b")
        second = _gather_start(packs[l][SPLIT_AB:OFF_OUT], ROWS_A, mixer[4], f"gather_start_{l}c")
        return first, mixer, second

    flights = {0: start_layer(0, mod)}
    if nl > 1:
        flights[1] = start_layer(1, flights[0][2][4])
    last_start = flights[min(1, nl - 1)][2][4]

    cos, sin = _rope_tables(after_token(positions.reshape(seq, 1), last_start), "rope_tables")

    def vec(t):
        return t.reshape(1, -1)

    def landed(flight, rows_list, after, tag):
        send_sems, recv_sems, pk, lands, _ = flight
        pk, lands = _gather_wait(send_sems, recv_sems, pk, lands, after, f"gather_wait_{tag}")
        return _gather_finish(pk, rows_list, lands, "gather_finish")

    xs = x.reshape(seq, D_MODEL)
    saved = []
    for l in range(nl):
        norm1, up_after = normrow("ffn1_norm", l), None
        flight_a, flight_b, flight_c = flights[l]
        lands = landed(flight_a, ROWS_A, cos if l == 0 else xs, f"{l}a")
        if l >= 1 and l + 1 < nl:
            flights[l + 1] = start_layer(l + 1, lands[0])
            up_after = flights[l + 1][2][4]
        sv = {}

        def ffn_fwd(xin, norm, k0, wg, wu, wd, tag, after=None):
            h, a, b, t = _ffn_up(xin, norm, modrow(l, k0), modrow(l, k0 + 1), wg, wu, "ffn_up", after=after)
            y, xout = _mm(t, wd, "nn", "ffn_down", res=xin, gate=modrow(l, k0 + 2), gate_factor=0.5)
            sv[tag] = dict(x=xin, h=h, a=a, b=b, t=t, y=y)
            return xout

        xs = ffn_fwd(xs, norm1, 0, lands[0], lands[1], lands[2], "f1", up_after)
        w = dict(zip(("g1", "u1", "d1"), lands[:3]))
        w.update(_mixer_weights(*landed(flight_b, ROWS_TAIL, xs, f"{l}b")))
        sv["w"] = w

        h2, z = _norm_mm(xs, normrow("mix_norm", l), modrow(l, 3), modrow(l, 4), w["in"], "mix_in")
        y_pool, diff = _pool_fwd(z, pool_w[l], vec(pool_scale[l]), "pool_fwd")
        q, k, v, cqn, ckvn = _qkv_fwd(z, vec(q_a_norm[l]), vec(kv_a_norm[l]), w["q"], w["kv"], cos, sin, "qkv_fwd")
        o, lse = _attn_fwd(q, k, v, "attn_fwd")
        ycat = jnp.concatenate([y_pool, o.astype(BF16)], axis=1)
        y2, xmix = _mm(ycat, w["out"], "nn", "mix_out", res=xs, gate=modrow(l, 5), gate_factor=1.0)
        sv["mix"] = dict(x=xs, h=h2, z=z, diff=diff, q=q, k=k, v=v, cqn=cqn, ckvn=ckvn, lse=lse, ycat=ycat, y=y2)
        xs = xmix

        w.update(zip(("g2", "u2", "d2"), landed(flight_c, ROWS_A, xs, f"{l}c")))
        xs = ffn_fwd(xs, normrow("ffn2_norm", l), 6, w["g2"], w["u2"], w["d2"], "f2")
        saved.append(sv)

    loss_part, dx, d_final, *head = _loss_head(xs, vec(final_norm), loss_target.reshape(seq, D_MODEL),
                                               (saved[nl - 1]["f2"]["y"], modrow(nl - 1, 8), 0.5), "loss_head")

    small = {name: [None] * nl for name in ("ffn1_norm", "mix_norm", "ffn2_norm", "q_a_norm", "kv_a_norm",
                                            "pool_scale", "pool_w", "dmod")}
    core = lax.axis_index("c").astype(jnp.int32).reshape(1)
    chip = 2 * lax.axis_index("x") + lax.axis_index("y")
    exchanges = []

    def leave(srcs, rows_list, after, tag):
        return _pair_start(srcs, rows_list, after, f"pair_start_{tag}"), rows_list, tag

    def forward_on(pending, after, layer, row_off):
        (send_sems, recv_sems, srcs, land, _), rows_list, tag = pending
        srcs, land = _split_wait(send_sems, recv_sems, 1, srcs, land, after, f"pair_wait_{tag}")
        sums = _pair_sum(srcs, rows_list, land, core, "pair_sum")
        flight = _chip_exchange_start(sums, chip, after, f"exchange_start_{tag}")
        exchanges.append((flight, layer, row_off, tag))
        return flight[4]

    pending = None
    for l in reversed(range(nl)):
        sv = saved[l]
        w = sv["w"]
        dmod = [None] * N_MOD
        gr = {}

        def ffn_bwd(dxin, head, s_, norm, k0, wg, wu, wd, tag, below, first_after=None, mid=None):
            dy, dmod[k0 + 2] = head
            da, db, gr["d" + tag], gr["g" + tag], gr["u" + tag] = _ffn_bwd_cols(
                dy, s_["h"], s_["a"], s_["b"], s_["t"], wd, "ffn_bwd_cols", after=first_after)
            dh = _mm_pair(da, wg, db, wu, "ffn_bwd_dh", after=None if mid is None else mid(da))
            outs = _rm_bwd(dh, s_["x"], dxin, norm, modrow(l, k0 + 1), "rm_bwd", below=below)
            dmod[k0], dmod[k0 + 1] = outs[1], outs[2]
            return outs[0], outs[3], outs[4:]

        s_ = sv["mix"]
        dx, small["ffn2_norm"][l], head = ffn_bwd(
            dx, head, sv["f2"], normrow("ffn2_norm", l), 6, w["g2"], w["u2"], w["d2"], "2", (s_["y"], modrow(l, 5), 1.0),
            first_after=None if pending is None else pending[0][4])

        pending_c = leave([gr["g2"], gr["u2"], gr["d2"]], ROWS_A, dx, f"{l}c")
        mix_after = pending_c[0][4]
        if pending is not None:
            mix_after = forward_on(pending, mix_after, l + 1, GB_F1)
            pending = None
        dy, dmod[5] = head
        gr["out"] = _mm(s_["ycat"], dy, "tn", "mix_out_dw", out_dtype=BF16, tm=512, after=mix_after)
        dycat = _mm(dy, w["out"], "nt", "mix_out_dx", tm=1024)
        du, small["pool_w"][l], small["pool_scale"][l] = _pool_bwd(dycat, s_["diff"], pool_w[l], vec(pool_scale[l]), "pool_bwd")
        dq, dk, dv = _attn_bwd(s_["q"], s_["k"], s_["v"], s_["lse"], dycat, "attn_bwd")
        dz, dqb, dkvb, small["q_a_norm"][l], small["kv_a_norm"][l] = _qkv_bwd(
            dq, dk, dv, du, s_["z"], vec(q_a_norm[l]), vec(kv_a_norm[l]), w["q"], w["kv"], cos, sin, "qkv_bwd")
        gr["q"] = _mm(dqb, s_["cqn"], "tn", "q_b_dw", out_dtype=BF16, tm=512, after=forward_on(pending_c, dz, l, GB_F2))
        gr["kv"] = _mm(dkvb, s_["ckvn"], "tn", "kv_b_dw", out_dtype=BF16, tm=512)
        gr["in"] = _mm(dz, s_["h"], "tn", "mix_in_dw", out_dtype=BF16, tm=512)
        dh2 = _mm(dz, w["in"], "nn", "mix_in_dx", tm=1024)
        outs = _rm_bwd(dh2, s_["x"], dx, normrow("mix_norm", l), modrow(l, 4), "rm_bwd",
                       below=(sv["f1"]["y"], modrow(l, 2), 0.5))
        dx, dmod[3], dmod[4], small["mix_norm"][l] = outs[:4]
        head = outs[4:]

        first_after, mid = None, None
        if l == 0:
            pending_b = leave(_grad_sources_tail(gr), ROWS_TAIL, dx, "0b")
            first_after = pending_b[0][4]
            last_groups = []

            def mid(da):
                last_groups.append(leave([gr["g1"], gr["u1"], gr["d1"]], ROWS_A, da, "0a"))
                return forward_on(pending_b, last_groups[0][0][4], 0, GB_TAIL)
        below = (saved[l - 1]["f2"]["y"], modrow(l - 1, 8), 0.5) if l > 0 else None
        dx, small["ffn1_norm"][l], head = ffn_bwd(
            dx, head, sv["f1"], normrow("ffn1_norm", l), 0, w["g1"], w["u1"], w["d1"], "1", below, first_after, mid)

        small["dmod"][l] = jnp.concatenate(dmod, axis=0)
        if l > 0:
            pending = leave([gr["g1"], gr["u1"], gr["d1"]] + _grad_sources_tail(gr), ROWS_A + ROWS_TAIL, dx, l)

    grad_x = dx.reshape(x.shape)
    pending_a = last_groups[0]

    layout, small_rows = _small_layout(nl)
    pieces = {
        "dmod": jnp.concatenate(small["dmod"], axis=0),
        "ffn1_norm": jnp.concatenate(small["ffn1_norm"], axis=0),
        "mix_norm": jnp.concatenate(small["mix_norm"], axis=0),
        "ffn2_norm": jnp.concatenate(small["ffn2_norm"], axis=0),
        "q_a_norm": jnp.concatenate(small["q_a_norm"], axis=0),
        "kv_a_norm": jnp.concatenate(small["kv_a_norm"], axis=0),
        "pool_scale": jnp.concatenate(small["pool_scale"], axis=0),
        "final_norm": d_final,
        "loss": jnp.broadcast_to(loss_part[0:1, 0:1], (1, D_MODEL)),
        "pool_w": _pack_bf16_pairs(jnp.stack(small["pool_w"]).reshape(-1, D_MODEL)),
    }
    small_buf = jnp.concatenate([_to_rows(pieces[name]) for name in layout], axis=0)
    def landed_sums(gbuf, entries, after):
        for (send_sems, recv_sems, sums, recv, _), layer, row_off, tag in entries:
            _, recv = _split_wait(send_sems, recv_sems, N_CHIPS - 1, sums, recv, after, f"exchange_wait_{tag}")
            gbuf = _sum_slots_into(recv, gbuf, layer, row_off, "sum_grads")
        return gbuf

    gbuf = lax.empty((nl, ROWS_L, D_MODEL), F32)
    token_0a = forward_on(pending_a, dx, 0, GB_F1)
    spread = _spread_start(small_buf, me, token_0a, "small_start")
    gbuf = landed_sums(gbuf, [e for e in exchanges if e[3] != "0a"], spread[4])

    def swap(t):
        return t.transpose(0, 2, 1)

    def same(t):
        return t

    grads, updates = {}, {}

    def update_rows(gbuf, table):
        for wname, off, view in table:
            g, d_, nm, nv = _adamw_rows(view(wts[wname]), gbuf, off, view(mom_m[wname]), view(mom_v[wname]), "adamw_rows")
            grads[wname], updates[wname] = view(g), (view(d_), view(nm), view(nv))

    update_rows(gbuf, (("ffn2_w_gate", GB_F2, swap), ("ffn2_w_up", GB_F2 + FF_SH, swap),
                       ("ffn2_w_down", GB_F2 + 2 * FF_SH, same), ("w_out", GB_TAIL, same)))
    small_grads = {
        "w_kv_b": (gbuf[:, OFF_KV:OFF_KV + KV_SH_ROWS].reshape(nl, -1, KV_LORA).transpose(0, 2, 1), same),
        "w_in": (gbuf[:, OFF_IN:OFF_IN + IN_SH], swap),
        "w_q_b": (gbuf[:, OFF_Q:OFF_Q + Q_SH_ROWS].reshape(nl, -1, Q_LORA), swap),
    }
    for wname, (g, view) in small_grads.items():
        upd = _adamw_nd(view(wts[wname]), g, view(mom_m[wname]), view(mom_v[wname]), "adamw")
        grads[wname], updates[wname] = view(g), tuple(view(t) for t in upd)
    gbuf = landed_sums(gbuf, [e for e in exchanges if e[3] == "0a"], updates["w_q_b"][0])
    update_rows(gbuf, (("ffn1_w_gate", GB_F1, swap), ("ffn1_w_up", GB_F1 + FF_SH, swap),
                       ("ffn1_w_down", GB_F1 + 2 * FF_SH, same)))

    _, small_all = _split_wait(spread[0], spread[1], N_DEV - 1, spread[2], spread[3], updates["ffn1_w_down"][0],
                               "small_wait")
    pool_off, pool_rows = layout["pool_w"]
    small_sum = _sum_slots(small_all[:, :pool_off], "sum_small")
    pool_sum = _sum_slots(_unpack_bf16_pairs(small_all[:, pool_off:pool_off + pool_rows]), "sum_pool_w")

    def take(name, width=D_MODEL):
        off, n = layout[name]
        return small_sum[off:off + n, :width]

    late = {"ada_b": take("dmod").reshape(nl, N_MOD * D_MODEL),
            "ffn1_norm": take("ffn1_norm"), "mix_norm": take("mix_norm"), "ffn2_norm": take("ffn2_norm"),
            "q_a_norm": take("q_a_norm", Q_LORA), "kv_a_norm": take("kv_a_norm", KV_LORA),
            "pool_scale": take("pool_scale", POOL_WIDTH), "final_norm": take("final_norm").reshape(D_MODEL),
            "pool_w": pool_sum.reshape(pool_w.shape)}
    loss = take("loss")[0, 0]

    off, n = layout["dmod"]
    dmod_all = small_all[:, off:off + n].reshape(N_DEV, nl, N_MOD * D_MODEL)
    dmod_mine = lax.dynamic_slice_in_dim(dmod_all, me * ada_cols, ada_cols, axis=2)
    dmod_pad = jnp.pad(dmod_mine.transpose(1, 0, 2), ((0, 0), (0, LANE - N_DEV), (0, 0)))
    late["ada_w"] = _ada_grad(jnp.pad(c_all, ((0, LANE - N_DEV), (0, 0))), dmod_pad, "ada_grad")
    for name, g in late.items():
        grads[name], updates[name] = g, _adamw_nd(wts[name], g, mom_m[name], mom_v[name], "adamw")

    return (loss, grad_x, *[grads[n] for n in order], *[updates[n][0] for n in order],
            *[updates[n][1] for n in order], *[updates[n][2] for n in order])
```

```python
import math

import numpy as np
import jax
import jax.numpy as jnp
from jax import lax
from jax.experimental import pallas as pl
from jax.experimental.pallas import tpu as pltpu

F32 = jnp.float32
BF16 = jnp.bfloat16

N_DEV = 8
D_MODEL = 1024
D_FF = 2816
POOL_WIDTH = 512
POOL_WINDOWS = (2, 4, 8, 16)
POOL_GC = 128
N_HEADS = 4
QK_NOPE = 128
QK_ROPE = 64
V_HEAD = 128
QK_HEAD = QK_NOPE + QK_ROPE
HEAD_PAD = 256
Q_LORA = 384
KV_LORA = 256
IN_COLS = POOL_WIDTH + Q_LORA + KV_LORA + QK_ROPE
IN_PAD = 1280
ROPE_THETA = 10000.0
SOFTMAX_SCALE = 1.0 / math.sqrt(QK_HEAD)
EPS = 1e-6
N_MOD = 9

ADAM_LR = 0.001
ADAM_B1 = 0.9
ADAM_B2 = 0.999
ADAM_EPS = 1e-08
ADAM_WD = 0.01
ADAM_STEP = 10

LANE = 128
VMEM_LIMIT = 56 * 1024 * 1024

FF_SH = D_FF // N_DEV
OFF_G1, OFF_U1, OFF_D1 = 0, FF_SH, 2 * FF_SH
OFF_G2, OFF_U2, OFF_D2 = 3 * FF_SH, 4 * FF_SH, 5 * FF_SH
OFF_OUT = 6 * FF_SH
OFF_KV = OFF_OUT + 128
OFF_IN = OFF_KV + 32
OFF_Q = OFF_IN + 160
Q_PAD_ROWS = 64
ROWS_L = OFF_Q + Q_PAD_ROWS
IN_SH = IN_COLS // N_DEV
Q_SH_ROWS = (N_HEADS * QK_HEAD // N_DEV) * Q_LORA // D_MODEL
KV_SH_ROWS = (N_HEADS * (QK_NOPE + V_HEAD) // N_DEV) * KV_LORA // D_MODEL


def _tile(dim, target):
    if dim <= target:
        return dim
    best = None
    for t in range(LANE, target + 1, LANE):
        if dim % t == 0:
            best = t
    assert best is not None, (dim, target)
    return best


def _params(sem):
    return pltpu.CompilerParams(dimension_semantics=sem, vmem_limit_bytes=VMEM_LIMIT)


def _mesh_pos():
    return lax.axis_index("x"), lax.axis_index("y"), lax.axis_index("c")


def _all_gather(x, name):
    m, n = x.shape

    def body(x_ref, out_ref, send_sems, recv_sems, local_sem):
        px, py, pc = _mesh_pos()
        me, sibling = (px, py, pc), (px, py, 1 - pc)
        chips = [(1 - px, py), (px, 1 - py), (1 - px, 1 - py)]

        def rows(bx, by, bc):
            return out_ref.at[pl.ds((4 * bx + 2 * by + bc) * m, m), :]

        def copy(k, block, to, src=None):
            return pltpu.make_async_remote_copy(
                src_ref=rows(*block) if src is None else src, dst_ref=rows(*block),
                send_sem=send_sems.at[k], recv_sem=recv_sems.at[k],
                device_id=to, device_id_type=pl.DeviceIdType.MESH)

        mine = pltpu.make_async_copy(x_ref, rows(*me), local_sem)
        mine.start()
        first = [copy(0, me, sibling, src=x_ref)]
        first += [copy(1 + j, me, (*chip, pc), src=x_ref) for j, chip in enumerate(chips)]
        for cp in first:
            cp.start()
        passed = [copy(4 + j, (*chip, pc), sibling) for j, chip in enumerate(chips)]
        for j, chip in enumerate(chips):
            copy(1 + j, (*chip, pc), me).wait_recv()
            passed[j].start()
        copy(0, sibling, me).wait_recv()
        for j, chip in enumerate(chips):
            copy(4 + j, (*chip, 1 - pc), me).wait_recv()
        for cp in first + passed:
            cp.wait_send()
        mine.wait()

    hbm = pl.BlockSpec(memory_space=pltpu.HBM)
    return pl.pallas_call(
        body, name=name,
        out_shape=jax.ShapeDtypeStruct((N_DEV * m, n), x.dtype),
        in_specs=[hbm], out_specs=hbm,
        scratch_shapes=[pltpu.SemaphoreType.DMA((7,)), pltpu.SemaphoreType.DMA((7,)),
                        pltpu.SemaphoreType.DMA],
    )(x)


SMALL_ROWS = ROWS_L - OFF_KV
ROWS_A = [FF_SH] * 3
SPLIT_AB = sum(ROWS_A)
ROWS_TAIL = [128, SMALL_ROWS]
GB_F2, GB_F1, GB_TAIL = 0, SPLIT_AB, 2 * SPLIT_AB
HBM_SPEC = pl.BlockSpec(memory_space=pltpu.HBM)
SEM_SPEC = pl.BlockSpec(memory_space=pltpu.SEMAPHORE)
ANY_SPEC = pl.BlockSpec(memory_space=pl.ANY)
EFFECT = pltpu.SideEffectType.DATAFLOW_SIDE_EFFECTING


def _hbm(t):
    return pltpu.with_memory_space_constraint(t, pltpu.HBM)


def _whole_wait(ref, send_sem, recv_sem, peer):
    return pltpu.make_async_remote_copy(src_ref=ref, dst_ref=ref, send_sem=send_sem, recv_sem=recv_sem,
                                        device_id=peer, device_id_type=pl.DeviceIdType.MESH)


def _offsets(rows_list):
    return [sum(rows_list[:i]) for i in range(len(rows_list))]


def _gather_start(packed, rows_list, after, name):
    n = len(rows_list)
    offs = _offsets(rows_list)
    lands = [_hbm(lax.empty((N_DEV * rows, D_MODEL), BF16)) for rows in rows_list]

    def body(packed_ref, *refs):
        land = refs[:n]
        send_sems, recv_sems = refs[n + 1], refs[n + 2]
        token = refs[-1]
        px, py, pc = _mesh_pos()
        me = 4 * px + 2 * py + pc
        peers = [(px, py, 1 - pc), (1 - px, py, pc), (px, 1 - py, pc), (1 - px, 1 - py, pc)]
        for k, peer in enumerate(peers):
            for off, rows, land_ref in zip(offs, rows_list, land):
                pltpu.make_async_remote_copy(
                    src_ref=packed_ref.at[pl.ds(off, rows), :], dst_ref=land_ref.at[pl.ds(me * rows, rows), :],
                    send_sem=send_sems.at[k], recv_sem=recv_sems.at[k],
                    device_id=peer, device_id_type=pl.DeviceIdType.MESH).start()
        token[...] = jnp.zeros_like(token)

    outs = pl.pallas_call(
        body, name=name,
        out_shape=(pltpu.SemaphoreType.DMA((4,)), pltpu.SemaphoreType.DMA((4,)), pltpu.HBM(packed.shape, BF16),
                   *[pltpu.HBM(t.shape, BF16) for t in lands], jax.ShapeDtypeStruct((8, LANE), F32)),
        in_specs=(HBM_SPEC,) * (1 + n) + (ANY_SPEC,),
        out_specs=(SEM_SPEC, SEM_SPEC) + (HBM_SPEC,) * (1 + n) + (pl.BlockSpec(memory_space=pltpu.VMEM),),
        input_output_aliases={i: 2 + i for i in range(1 + n)},
        compiler_params=pltpu.CompilerParams(has_side_effects=EFFECT),
    )(_hbm(packed), *lands, after)
    return outs[0], outs[1], outs[2], list(outs[3:3 + n]), outs[-1]


def _gather_wait(send_sems, recv_sems, packed, lands, after, name):
    n = len(lands)

    def body(packed_ref, *refs):
        s_sems, r_sems = refs[n], refs[n + 1]
        me = _mesh_pos()
        for k in range(4):
            cp = _whole_wait(packed_ref, s_sems.at[k], r_sems.at[k], me)
            cp.wait_send()
            cp.wait_recv()

    outs = pl.pallas_call(
        body, name=name,
        out_shape=(pltpu.HBM(packed.shape, BF16), *[pltpu.HBM(t.shape, BF16) for t in lands]),
        in_specs=(HBM_SPEC,) * (1 + n) + (SEM_SPEC, SEM_SPEC, ANY_SPEC),
        out_specs=(HBM_SPEC,) * (1 + n),
        input_output_aliases={i: i for i in range(1 + n)},
        compiler_params=pltpu.CompilerParams(has_side_effects=EFFECT),
    )(packed, *lands, send_sems, recv_sems, after)
    return outs[0], list(outs[1:])


def _gather_finish(packed, rows_list, lands, name):
    n = len(rows_list)
    offs = _offsets(rows_list)

    def body(packed_ref, *refs):
        land = refs[n:2 * n]
        send_sems, recv_sems, stage, stage_sem = refs[2 * n:]
        px, py, pc = _mesh_pos()
        me = 4 * px + 2 * py + pc
        sibling = (px, py, 1 - pc)
        load = pltpu.make_async_copy(packed_ref, stage, stage_sem)
        load.start()
        for j, (cx, cy) in enumerate([(1 - px, py), (px, 1 - py), (1 - px, 1 - py)]):
            block = 4 * cx + 2 * cy + pc
            for rows, land_ref in zip(rows_list, land):
                blk = land_ref.at[pl.ds(block * rows, rows), :]
                pltpu.make_async_remote_copy(src_ref=blk, dst_ref=blk, send_sem=send_sems.at[j],
                                             recv_sem=recv_sems.at[j], device_id=sibling,
                                             device_id_type=pl.DeviceIdType.MESH).start()
        load.wait()
        for off, rows, land_ref in zip(offs, rows_list, land):
            pltpu.make_async_copy(stage.at[pl.ds(off, rows), :], land_ref.at[pl.ds(me * rows, rows), :],
                                  stage_sem).start()
        for j in range(3):
            cp = _whole_wait(packed_ref, send_sems.at[j], recv_sems.at[j], sibling)
            cp.wait_recv()
            cp.wait_send()
        pltpu.make_async_copy(stage, packed_ref, stage_sem).wait()

    outs = pl.pallas_call(
        body, name=name,
        out_shape=tuple(jax.ShapeDtypeStruct(t.shape, BF16) for t in lands),
        in_specs=(HBM_SPEC,) * (1 + n), out_specs=(HBM_SPEC,) * n,
        input_output_aliases={1 + i: i for i in range(n)},
        scratch_shapes=[pltpu.SemaphoreType.DMA((3,)), pltpu.SemaphoreType.DMA((3,)),
                        pltpu.VMEM(packed.shape, BF16), pltpu.SemaphoreType.DMA],
    )(packed, *lands)
    return list(outs)


N_CHIPS = 4


def _pair_start(srcs, rows_list, after, name):
    n = len(rows_list)
    offs = _offsets(rows_list)
    land = lax.empty((N_CHIPS, sum(rows_list), D_MODEL), BF16)

    def body(*refs):
        src, land_ref = refs[:n], refs[n]
        send_sems, recv_sems = refs[n + 2], refs[n + 3]
        token = refs[-1]
        px, py, pc = _mesh_pos()
        for k in range(N_CHIPS):
            block = 2 * k + (1 - pc)
            for off, rows, src_ref in zip(offs, rows_list, src):
                pltpu.make_async_remote_copy(
                    src_ref=src_ref.at[pl.ds(block * rows, rows), :], dst_ref=land_ref.at[k, pl.ds(off, rows), :],
                    send_sem=send_sems.at[0], recv_sem=recv_sems.at[0],
                    device_id=(px, py, 1 - pc), device_id_type=pl.DeviceIdType.MESH).start()
        token[...] = jnp.zeros_like(token)

    outs = pl.pallas_call(
        body, name=name,
        out_shape=(pltpu.SemaphoreType.DMA((1,)), pltpu.SemaphoreType.DMA((1,)),
                   *[pltpu.HBM(t.shape, BF16) for t in srcs], pltpu.HBM(land.shape, BF16),
                   jax.ShapeDtypeStruct((8, LANE), F32)),
        in_specs=(HBM_SPEC,) * (n + 1) + (ANY_SPEC,),
        out_specs=(SEM_SPEC, SEM_SPEC) + (HBM_SPEC,) * (n + 1) + (pl.BlockSpec(memory_space=pltpu.VMEM),),
        input_output_aliases={i: 2 + i for i in range(n + 1)},
        compiler_params=pltpu.CompilerParams(has_side_effects=EFFECT),
    )(*[_hbm(t) for t in srcs], _hbm(land), after)
    return outs[0], outs[1], list(outs[2:2 + n]), outs[2 + n], outs[-1]


def _split_wait(send_sems, recv_sems, n_sems, srcs, land, after, name):
    n = len(srcs)

    def body(*refs):
        land_ref = refs[n]
        s_sems, r_sems = refs[n + 1], refs[n + 2]
        me = _mesh_pos()
        for k in range(n_sems):
            cp = _whole_wait(land_ref.at[0] if n_sems > 1 else land_ref, s_sems.at[k], r_sems.at[k], me)
            cp.wait_send()
            cp.wait_recv()

    outs = pl.pallas_call(
        body, name=name,
        out_shape=(*[pltpu.HBM(t.shape, t.dtype) for t in srcs], pltpu.HBM(land.shape, land.dtype)),
        in_specs=(HBM_SPEC,) * (n + 1) + (SEM_SPEC, SEM_SPEC, ANY_SPEC),
        out_specs=(HBM_SPEC,) * (n + 1),
        input_output_aliases={i: i for i in range(n + 1)},
        compiler_params=pltpu.CompilerParams(has_side_effects=EFFECT),
    )(*srcs, land, send_sems, recv_sems, after)
    return list(outs[:n]), outs[n]


def _spread_start(x, me_id, after, name):
    land = lax.dynamic_update_slice_in_dim(lax.empty((N_DEV,) + x.shape, x.dtype), x[None], me_id, axis=0)

    def body(x_ref, land_ref, after_ref, send_sems, recv_sems, x_thru, land_thru, token):
        px, py, pc = _mesh_pos()
        me = 4 * px + 2 * py + pc
        for k in range(1, N_DEV):
            qx = 1 - px if k & 4 else px
            qy = 1 - py if k & 2 else py
            qc = 1 - pc if k & 1 else pc
            pltpu.make_async_remote_copy(
                src_ref=x_ref, dst_ref=land_ref.at[me], send_sem=send_sems.at[k - 1], recv_sem=recv_sems.at[k - 1],
                device_id=(qx, qy, qc), device_id_type=pl.DeviceIdType.MESH).start()
        token[...] = jnp.zeros_like(token)

    outs = pl.pallas_call(
        body, name=name,
        out_shape=(pltpu.SemaphoreType.DMA((N_DEV - 1,)), pltpu.SemaphoreType.DMA((N_DEV - 1,)),
                   pltpu.HBM(x.shape, x.dtype), pltpu.HBM(land.shape, land.dtype), jax.ShapeDtypeStruct((8, LANE), F32)),
        in_specs=(HBM_SPEC, HBM_SPEC, ANY_SPEC),
        out_specs=(SEM_SPEC, SEM_SPEC, HBM_SPEC, HBM_SPEC, pl.BlockSpec(memory_space=pltpu.VMEM)),
        input_output_aliases={0: 2, 1: 3},
        compiler_params=pltpu.CompilerParams(has_side_effects=EFFECT),
    )(_hbm(x), _hbm(land), after)
    return outs[0], outs[1], [outs[2]], outs[3], outs[4]


def _pair_sum(srcs, rows_list, land, core, name):
    n = len(rows_list)
    offs = _offsets(rows_list)
    total = sum(rows_list)

    def body(core_ref, *refs):
        src, land_ref, out_ref = refs[:n], refs[n], refs[n + 1]
        for off, rows, src_ref in zip(offs, rows_list, src):
            out_ref[pl.ds(off, rows), :] = (src_ref[...].astype(F32)
                                            + land_ref[pl.ds(off, rows), :].astype(F32)).astype(BF16)

    slot = pl.BlockSpec((None, total, D_MODEL), lambda k, c: (k, 0, 0))
    grid_spec = pltpu.PrefetchScalarGridSpec(
        num_scalar_prefetch=1, grid=(N_CHIPS,),
        in_specs=[pl.BlockSpec((rows, D_MODEL), lambda k, c: (2 * k + c[0], 0)) for rows in rows_list] + [slot],
        out_specs=slot)
    return pl.pallas_call(
        body, name=name, grid_spec=grid_spec,
        out_shape=jax.ShapeDtypeStruct((N_CHIPS, total, D_MODEL), BF16),
        compiler_params=_params(("parallel",)),
    )(core, *srcs, land)


def _chip_exchange_start(sums, chip, after, name):
    own = lax.dynamic_index_in_dim(sums, chip, axis=0, keepdims=True)
    recv = lax.dynamic_update_slice_in_dim(lax.empty(sums.shape, BF16), own, chip, axis=0)

    def body(sums_ref, recv_ref, after_ref, send_sems, recv_sems, sums_thru, recv_thru, token):
        px, py, pc = _mesh_pos()
        for k in range(1, N_CHIPS):
            qx = 1 - px if k & 2 else px
            qy = 1 - py if k & 1 else py
            pltpu.make_async_remote_copy(
                src_ref=sums_ref.at[2 * qx + qy], dst_ref=recv_ref.at[2 * px + py],
                send_sem=send_sems.at[k - 1], recv_sem=recv_sems.at[k - 1],
                device_id=(qx, qy, pc), device_id_type=pl.DeviceIdType.MESH).start()
        token[...] = jnp.zeros_like(token)

    outs = pl.pallas_call(
        body, name=name,
        out_shape=(pltpu.SemaphoreType.DMA((N_CHIPS - 1,)), pltpu.SemaphoreType.DMA((N_CHIPS - 1,)),
                   pltpu.HBM(sums.shape, BF16), pltpu.HBM(recv.shape, BF16), jax.ShapeDtypeStruct((8, LANE), F32)),
        in_specs=(HBM_SPEC, HBM_SPEC, ANY_SPEC),
        out_specs=(SEM_SPEC, SEM_SPEC, HBM_SPEC, HBM_SPEC, pl.BlockSpec(memory_space=pltpu.VMEM)),
        input_output_aliases={0: 2, 1: 3},
        compiler_params=pltpu.CompilerParams(has_side_effects=EFFECT),
    )(_hbm(sums), _hbm(recv), after)
    return outs[0], outs[1], [outs[2]], outs[3], outs[4]


def _sum_slots_into(recv, buf, layer, row_off, name):
    slots, r, n = recv.shape
    tr = _row_tile(math.gcd(r, row_off) if row_off else r, 512)
    first = row_off // tr

    def body(in_ref, buf_ref, out_ref):
        acc = in_ref[0].astype(F32)
        for j in range(1, slots):
            acc = acc + in_ref[j].astype(F32)
        out_ref[...] = acc

    return pl.pallas_call(
        body, name=name, grid=(r // tr,), out_shape=jax.ShapeDtypeStruct(buf.shape, F32),
        in_specs=[pl.BlockSpec((slots, tr, n), lambda i: (0, i, 0)), ANY_SPEC],
        out_specs=pl.BlockSpec((None, tr, n), lambda i: (layer, first + i, 0)),
        input_output_aliases={1: 0},
        compiler_params=_params(("parallel",)),
    )(recv, buf)


def _sum_slots(recv, name, after=None):
    _, r, n = recv.shape
    tr = _row_tile(r, 512)

    def body(in_ref, *refs):
        acc = in_ref[0].astype(F32)
        for j in range(1, N_DEV):
            acc = acc + in_ref[j].astype(F32)
        refs[-1][...] = acc

    grid = (r // tr,)
    in_specs, out_spec = [pl.BlockSpec((N_DEV, tr, n), lambda i: (0, i, 0))], pl.BlockSpec((tr, n), lambda i: (i, 0))
    args = [recv]
    if after is not None:
        in_specs.append(ANY_SPEC)
        args.append(after)
    return pl.pallas_call(
        body, name=name, grid=grid,
        out_shape=jax.ShapeDtypeStruct((r, n), F32),
        in_specs=in_specs, out_specs=out_spec,
        compiler_params=_params(("parallel",)),
    )(*args)


def _row_tile(rows, target):
    if rows <= target:
        return rows
    best = None
    for t in range(16, target + 1, 16):
        if rows % t == 0:
            best = t
    assert best is not None, rows
    return best


_DIMS = {"nn": ((1,), (0,)), "nt": ((1,), (1,)), "tn": ((0,), (0,))}


def _mm(a, b, mode, name, out_dtype=F32, res=None, gate=None, gate_factor=1.0, tm=512, tn=1408, after=None):
    assert (res is None) == (gate is None)
    if mode == "tn":
        kdim, m = a.shape
    else:
        m, kdim = a.shape
    n = b.shape[0] if mode == "nt" else b.shape[1]
    tm, tn = _tile(m, tm), _tile(n, tn)
    a_spec = (pl.BlockSpec((kdim, tm), lambda i, j: (0, i)) if mode == "tn"
              else pl.BlockSpec((tm, kdim), lambda i, j: (i, 0)))
    b_spec = (pl.BlockSpec((tn, kdim), lambda i, j: (j, 0)) if mode == "nt"
              else pl.BlockSpec((kdim, tn), lambda i, j: (0, j)))
    o_spec = pl.BlockSpec((tm, tn), lambda i, j: (i, j))
    dims = (_DIMS[mode], ((), ()))
    has_res = res is not None

    def body(a_ref, b_ref, *refs):
        y = lax.dot_general(a_ref[...].astype(BF16), b_ref[...].astype(BF16), dims,
                            preferred_element_type=F32)
        if has_res:
            res_ref, gate_ref = refs[0], refs[1]
            y_ref, o_ref = refs[-2], refs[-1]
            y_ref[...] = y.astype(BF16)
            o_ref[...] = res_ref[...] + (gate_factor * gate_ref[...]) * y
        else:
            refs[-1][...] = y.astype(out_dtype)

    in_specs, args = [a_spec, b_spec], [a, b]
    if has_res:
        gate_spec, gate_arg = _vec_in(gate, tile=tn)
        in_specs += [o_spec, gate_spec]
        args += [res, gate_arg]
        out_shape = (jax.ShapeDtypeStruct((m, n), BF16), jax.ShapeDtypeStruct((m, n), F32))
        out_specs = (o_spec, o_spec)
    else:
        out_shape, out_specs = jax.ShapeDtypeStruct((m, n), out_dtype), o_spec
    if after is not None:
        in_specs.append(ANY_SPEC)
        args.append(after)
    return pl.pallas_call(
        body, name=name, grid=(m // tm, n // tn), out_shape=out_shape,
        in_specs=in_specs, out_specs=out_specs,
        compiler_params=_params(("parallel", "parallel")),
    )(*args)


def _vec_in(v, tile=None):
    if isinstance(v, tuple):
        table, row = v
        if tile is None:
            return pl.BlockSpec((None, 1, table.shape[-1]), lambda *idx: (row, 0, 0)), table
        return pl.BlockSpec((None, 1, tile), lambda i, j: (row, 0, j)), table
    if tile is None:
        return pl.BlockSpec((1, v.shape[-1]), lambda *idx: (0, 0)), v
    return pl.BlockSpec((1, tile), lambda i, j: (0, j)), v


def _vec_spec(width):
    return pl.BlockSpec((1, width), lambda i: (0, 0))


def _rm_bwd(dh, x, dres, gw, scale, name, below=None):
    s, d = x.shape
    ts = _tile(s, 512)
    factor = None if below is None else below[2]

    def body(dh_ref, x_ref, dres_ref, gw_ref, sc_ref, *refs):
        dx_ref, dsh_ref, dsc_ref, dgw_ref = refs[-6:-2] if below is not None else refs[-4:]

        @pl.when(pl.program_id(0) == 0)
        def _():
            dsh_ref[...] = jnp.zeros_like(dsh_ref)
            dsc_ref[...] = jnp.zeros_like(dsc_ref)
            dgw_ref[...] = jnp.zeros_like(dgw_ref)
            if below is not None:
                refs[-1][...] = jnp.zeros_like(refs[-1])

        xv, dhv, gwv = x_ref[...], dh_ref[...], gw_ref[...]
        r = lax.rsqrt(jnp.mean(xv * xv, axis=-1, keepdims=True) + EPS)
        xn = xv * r
        y = xn * gwv
        dsh_ref[...] += jnp.sum(dhv, axis=0, keepdims=True)
        dsc_ref[...] += jnp.sum(dhv * y, axis=0, keepdims=True)
        dy = dhv * (1 + sc_ref[...])
        dgw_ref[...] += jnp.sum(dy * xn, axis=0, keepdims=True)
        dxn = dy * gwv
        dx = dres_ref[...] + r * (dxn - xn * jnp.mean(dxn * xn, axis=-1, keepdims=True))
        dx_ref[...] = dx
        if below is not None:
            yb_ref, gb_ref, dyb_ref, dgb_ref = refs[0], refs[1], refs[-2], refs[-1]
            dyb_ref[...] = ((factor * gb_ref[...]) * dx).astype(BF16)
            dgb_ref[...] += jnp.sum((factor * dx) * yb_ref[...].astype(F32), axis=0, keepdims=True)

    row = pl.BlockSpec((ts, d), lambda i: (i, 0))
    vec = jax.ShapeDtypeStruct((1, d), F32)
    (gw_spec, gw), (sc_spec, scale) = _vec_in(gw), _vec_in(scale)
    in_specs, args = [row, row, row, gw_spec, sc_spec], [dh, x, dres, gw, scale]
    out_shape = [jax.ShapeDtypeStruct((s, d), F32), vec, vec, vec]
    out_specs = [row, _vec_spec(d), _vec_spec(d), _vec_spec(d)]
    if below is not None:
        gate_spec, gate_arg = _vec_in(below[1])
        in_specs += [row, gate_spec]
        args += [below[0], gate_arg]
        out_shape += [jax.ShapeDtypeStruct((s, d), BF16), vec]
        out_specs += [row, _vec_spec(d)]
    return pl.pallas_call(
        body, name=name, grid=(s // ts,), out_shape=tuple(out_shape),
        in_specs=in_specs, out_specs=tuple(out_specs),
        compiler_params=_params(("arbitrary",)),
    )(*args)


def _norm_mm(x, gw, shift, scale, w, name, tm=1024):
    s, d = x.shape
    n = w.shape[0]
    tm = _tile(s, tm)

    def body(x_ref, gw_ref, sh_ref, sc_ref, w_ref, h_ref, z_ref):
        xv = x_ref[...]
        r = lax.rsqrt(jnp.mean(xv * xv, axis=-1, keepdims=True) + EPS)
        hb = (((xv * r) * gw_ref[...]) * (1 + sc_ref[...]) + sh_ref[...]).astype(BF16)
        h_ref[...] = hb
        z_ref[...] = lax.dot_general(hb, w_ref[...], (((1,), (1,)), ((), ())), preferred_element_type=F32)

    row = pl.BlockSpec((tm, d), lambda i: (i, 0))
    return pl.pallas_call(
        body, name=name, grid=(s // tm,),
        out_shape=(jax.ShapeDtypeStruct((s, d), BF16), jax.ShapeDtypeStruct((s, n), F32)),
        in_specs=[row, _vec_in(gw)[0], _vec_in(shift)[0], _vec_in(scale)[0], pl.BlockSpec((n, d), lambda i: (0, 0))],
        out_specs=(row, pl.BlockSpec((tm, n), lambda i: (i, 0))),
        compiler_params=_params(("parallel",)),
    )(x, _vec_in(gw)[1], _vec_in(shift)[1], _vec_in(scale)[1], w)


FFN_TM, FFN_TF = 2048, 256


def _ffn_up(x, gw, shift, scale, wg, wu, name, after=None):
    s, d = x.shape
    f = wg.shape[0]
    tm, tf = _tile(s, FFN_TM), _tile(f, FFN_TF)
    nt = (((1,), (1,)), ((), ()))

    def body(x_ref, gw_ref, sh_ref, sc_ref, wg_ref, wu_ref, *refs):
        h_ref, a_ref, b_ref, t_ref = refs[-4:]

        @pl.when(pl.program_id(1) == 0)
        def _():
            xv = x_ref[...]
            r = lax.rsqrt(jnp.mean(xv * xv, axis=-1, keepdims=True) + EPS)
            h_ref[...] = (((xv * r) * gw_ref[...]) * (1 + sc_ref[...]) + sh_ref[...]).astype(BF16)

        hb = h_ref[...]
        av = lax.dot_general(hb, wg_ref[...], nt, preferred_element_type=F32)
        bv = lax.dot_general(hb, wu_ref[...], nt, preferred_element_type=F32)
        a_ref[...] = av.astype(BF16)
        b_ref[...] = bv.astype(BF16)
        t_ref[...] = ((av * jax.nn.sigmoid(av)) * bv).astype(BF16)

    row = pl.BlockSpec((tm, d), lambda i, j: (i, 0))
    wblk = pl.BlockSpec((tf, d), lambda i, j: (j, 0))
    blk = pl.BlockSpec((tm, tf), lambda i, j: (i, j))
    wide = jax.ShapeDtypeStruct((s, f), BF16)
    vec_specs, vec_args = zip(*[_vec_in(v) for v in (gw, shift, scale)])
    in_specs, args = [row, *vec_specs, wblk, wblk], [x, *vec_args, wg, wu]
    if after is not None:
        in_specs.append(ANY_SPEC)
        args.append(after)
    return pl.pallas_call(
        body, name=name, grid=(s // tm, f // tf),
        out_shape=(jax.ShapeDtypeStruct((s, d), BF16), wide, wide, wide),
        in_specs=in_specs, out_specs=(row, blk, blk, blk),
        compiler_params=_params(("parallel", "arbitrary")),
    )(*args)


def _ffn_bwd_cols(dy, h, a, b, t, wd, name, after=None):
    s, d = dy.shape
    f = wd.shape[0]
    tf = _tile(f, FFN_TF)
    nt = (((1,), (1,)), ((), ()))
    tn = (((0,), (0,)), ((), ()))

    def body(dy_ref, h_ref, a_ref, b_ref, t_ref, wd_ref, *refs):
        da_ref, db_ref, gd_ref, gg_ref, gu_ref = refs[-5:]
        dyb, hb = dy_ref[...], h_ref[...]
        dtv = lax.dot_general(dyb, wd_ref[...], nt, preferred_element_type=F32)
        av, bv = a_ref[...].astype(F32), b_ref[...].astype(F32)
        sg = jax.nn.sigmoid(av)
        dbv = (dtv * (av * sg)).astype(BF16)
        dav = ((dtv * bv) * (sg * (1 + av * (1 - sg)))).astype(BF16)
        da_ref[...] = dav
        db_ref[...] = dbv
        gd_ref[...] = lax.dot_general(t_ref[...], dyb, tn, preferred_element_type=F32).astype(BF16)
        gg_ref[...] = lax.dot_general(dav, hb, tn, preferred_element_type=F32).astype(BF16)
        gu_ref[...] = lax.dot_general(dbv, hb, tn, preferred_element_type=F32).astype(BF16)

    whole = pl.BlockSpec((s, d), lambda j: (0, 0))
    col = pl.BlockSpec((s, tf), lambda j: (0, j))
    wblk = pl.BlockSpec((tf, d), lambda j: (j, 0))
    wide, wgrad = jax.ShapeDtypeStruct((s, f), BF16), jax.ShapeDtypeStruct((f, d), BF16)
    in_specs, args = [whole, whole, col, col, col, wblk], [dy, h, a, b, t, wd]
    if after is not None:
        in_specs.append(ANY_SPEC)
        args.append(after)
    return pl.pallas_call(
        body, name=name, grid=(f // tf,), out_shape=(wide, wide, wgrad, wgrad, wgrad),
        in_specs=in_specs, out_specs=(col, col, wblk, wblk, wblk),
        compiler_params=_params(("parallel",)),
    )(*args)


def _mm_pair(a1, b1, a2, b2, name, tm=1024, tn=512, after=None):
    m, kdim = a1.shape
    n = b1.shape[1]
    tm, tn = _tile(m, tm), _tile(n, tn)

    def body(a1_ref, b1_ref, a2_ref, b2_ref, *refs):
        refs[-1][...] = (jnp.dot(a1_ref[...], b1_ref[...], preferred_element_type=F32)
                         + jnp.dot(a2_ref[...], b2_ref[...], preferred_element_type=F32))

    a_spec = pl.BlockSpec((tm, kdim), lambda i, j: (i, 0))
    b_spec = pl.BlockSpec((kdim, tn), lambda i, j: (0, j))
    in_specs, args = [a_spec, b_spec, a_spec, b_spec], [a1, b1, a2, b2]
    if after is not None:
        in_specs.append(ANY_SPEC)
        args.append(after)
    return pl.pallas_call(
        body, name=name, grid=(m // tm, n // tn), out_shape=jax.ShapeDtypeStruct((m, n), F32),
        in_specs=in_specs, out_specs=pl.BlockSpec((tm, tn), lambda i, j: (i, j)),
        compiler_params=_params(("parallel", "parallel")),
    )(*args)


def _pool_counts(s):
    return (lax.broadcasted_iota(jnp.int32, (s, POOL_GC), 0))


def _pool_fwd(z, pool_w, pool_scale, name):
    s = z.shape[0]

    def body(u_ref, w_ref, sc_ref, y_ref, diff_ref):
        t = lax.broadcasted_iota(jnp.int32, (s, POOL_GC), 0)
        for g, win in enumerate(POOL_WINDOWS):
            cols = slice(g * POOL_GC, (g + 1) * POOL_GC)
            u = u_ref[:, cols]
            acc, step = u, 1
            while step < win:
                acc = acc + jnp.where(t >= step, pltpu.roll(acc, step, 0), 0.0)
                step *= 2
            cnt = jnp.minimum(t + 1, win).astype(F32)
            diff = acc / cnt - u
            diff_ref[:, cols] = diff
            ypre = jnp.dot(diff.astype(BF16), w_ref[g].astype(BF16), preferred_element_type=F32)
            y_ref[:, cols] = (ypre * sc_ref[:, cols]).astype(BF16)

    return pl.pallas_call(
        body, name=name, grid=(1,),
        out_shape=(jax.ShapeDtypeStruct((s, POOL_WIDTH), BF16), jax.ShapeDtypeStruct((s, POOL_WIDTH), F32)),
        in_specs=[pl.BlockSpec((s, POOL_WIDTH), lambda i: (0, 0)),
                  pl.BlockSpec(pool_w.shape, lambda i: (0, 0, 0)),
                  pl.BlockSpec((1, POOL_WIDTH), lambda i: (0, 0))],
        out_specs=(pl.BlockSpec((s, POOL_WIDTH), lambda i: (0, 0)),
                   pl.BlockSpec((s, POOL_WIDTH), lambda i: (0, 0))),
        compiler_params=_params(("arbitrary",)),
    )(z, pool_w, pool_scale)


def _pool_bwd(dycat, diff, pool_w, pool_scale, name):
    s = diff.shape[0]

    def body(dy_ref, diff_ref, w_ref, sc_ref, du_ref, dw_ref, dsc_ref):
        t = lax.broadcasted_iota(jnp.int32, (s, POOL_GC), 0)
        for g, win in enumerate(POOL_WINDOWS):
            cols = slice(g * POOL_GC, (g + 1) * POOL_GC)
            dy, dfb, wb = dy_ref[:, cols], diff_ref[:, cols].astype(BF16), w_ref[g].astype(BF16)
            ypre = jnp.dot(dfb, wb, preferred_element_type=F32)
            dsc_ref[:, cols] = jnp.sum(dy * ypre, axis=0, keepdims=True)
            dypre = (dy * sc_ref[:, cols]).astype(BF16)
            ddiff = lax.dot_general(dypre, wb, (((1,), (1,)), ((), ())), preferred_element_type=F32)
            dw_ref[g] = lax.dot_general(dfb, dypre, (((0,), (0,)), ((), ())), preferred_element_type=F32)
            cnt = jnp.minimum(t + 1, win).astype(F32)
            acc, step = ddiff / cnt, 1
            while step < win:
                acc = acc + jnp.where(t < s - step, pltpu.roll(acc, s - step, 0), 0.0)
                step *= 2
            du_ref[:, cols] = acc - ddiff

    full = pl.BlockSpec((s, POOL_WIDTH), lambda i: (0, 0))
    return pl.pallas_call(
        body, name=name, grid=(1,),
        out_shape=(jax.ShapeDtypeStruct((s, POOL_WIDTH), F32),
                   jax.ShapeDtypeStruct(pool_w.shape, F32),
                   jax.ShapeDtypeStruct((1, POOL_WIDTH), F32)),
        in_specs=[full, full, pl.BlockSpec(pool_w.shape, lambda i: (0, 0, 0)),
                  pl.BlockSpec((1, POOL_WIDTH), lambda i: (0, 0))],
        out_specs=(full, pl.BlockSpec(pool_w.shape, lambda i: (0, 0, 0)),
                   pl.BlockSpec((1, POOL_WIDTH), lambda i: (0, 0))),
        compiler_params=_params(("arbitrary",)),
    )(dycat, diff, pool_w, pool_scale)


def _rope_tables(positions, name):
    s = positions.shape[0]
    ts = _tile(s, 512)
    freq = 1.0 / (ROPE_THETA ** (np.arange(0, QK_ROPE, 2, dtype=np.float32) / QK_ROPE))
    table = np.zeros((1, LANE), np.float32)
    table[0, :QK_ROPE // 2] = freq
    table[0, QK_ROPE // 2:QK_ROPE] = freq

    def body(pos_ref, f_ref, cos_ref, sin_ref):
        ang = pos_ref[...].astype(F32) * f_ref[...]
        cos_ref[...] = jnp.cos(ang)
        sin_ref[...] = jnp.sin(ang)

    out = jax.ShapeDtypeStruct((s, LANE), F32)
    blk = pl.BlockSpec((ts, LANE), lambda i: (i, 0))
    return pl.pallas_call(
        body, name=name, grid=(s // ts,), out_shape=(out, out),
        in_specs=[pl.BlockSpec((ts, 1), lambda i: (i, 0)), _vec_spec(LANE)], out_specs=(blk, blk),
        compiler_params=_params(("parallel",)),
    )(positions, jnp.asarray(table))


def _lane_mod64_low(shape):
    return (lax.broadcasted_iota(jnp.int32, shape, 1) % QK_ROPE) < (QK_ROPE // 2)


def _rope(x, cos, sin):
    rot = jnp.where(_lane_mod64_low(x.shape), -pltpu.roll(x, LANE - 32, 1), pltpu.roll(x, 32, 1))
    return x * cos + rot * sin


def _rope_t(dy, cos, sin):
    w = dy * sin
    rot_t = jnp.where(_lane_mod64_low(dy.shape), pltpu.roll(w, LANE - 32, 1), -pltpu.roll(w, 32, 1))
    return dy * cos + rot_t


def _plain_rms(x, g):
    r = lax.rsqrt(jnp.mean(x * x, axis=-1, keepdims=True) + EPS)
    return (x * r) * g, x * r, r


O_Q, O_KV, O_KR = POOL_WIDTH, POOL_WIDTH + Q_LORA, POOL_WIDTH + Q_LORA + KV_LORA


def _qkv_fwd(z, qn, kvn, wq, wkv, cos, sin, name):
    s = z.shape[0]
    ts = _tile(s, 512)

    def body(z_ref, qn_ref, kvn_ref, wq_ref, wkv_ref, cos_ref, sin_ref, q_ref, k_ref, v_ref, cqn_ref, ckvn_ref):
        cosv, sinv = cos_ref[...], sin_ref[...]
        cqn = _plain_rms(z_ref[:, O_Q:O_KV], qn_ref[...])[0].astype(BF16)
        ckvn = _plain_rms(z_ref[:, O_KV:O_KR], kvn_ref[...])[0].astype(BF16)
        cqn_ref[...] = cqn
        ckvn_ref[...] = ckvn
        nt = (((1,), (1,)), ((), ()))
        q = lax.dot_general(cqn, wq_ref[...], nt, preferred_element_type=F32)
        kv = lax.dot_general(ckvn, wkv_ref[...], nt, preferred_element_type=F32)
        kr = _rope(z_ref[:, O_KR:IN_PAD], cosv, sinv).astype(BF16)
        for h in range(N_HEADS):
            o = h * HEAD_PAD
            q_ref[:, o:o + QK_NOPE] = q[:, o:o + QK_NOPE].astype(BF16)
            q_ref[:, o + QK_NOPE:o + HEAD_PAD] = _rope(q[:, o + QK_NOPE:o + HEAD_PAD], cosv, sinv).astype(BF16)
            k_ref[:, o:o + QK_NOPE] = kv[:, o:o + QK_NOPE].astype(BF16)
            k_ref[:, o + QK_NOPE:o + HEAD_PAD] = kr
            v_ref[:, h * V_HEAD:(h + 1) * V_HEAD] = kv[:, o + QK_NOPE:o + HEAD_PAD].astype(BF16)

    def row(w):
        return pl.BlockSpec((ts, w), lambda i: (i, 0))

    def whole(arr):
        return pl.BlockSpec(arr.shape, lambda i: (0, 0))

    hp = N_HEADS * HEAD_PAD
    return pl.pallas_call(
        body, name=name, grid=(s // ts,),
        out_shape=(jax.ShapeDtypeStruct((s, hp), BF16), jax.ShapeDtypeStruct((s, hp), BF16),
                   jax.ShapeDtypeStruct((s, N_HEADS * V_HEAD), BF16),
                   jax.ShapeDtypeStruct((s, Q_LORA), BF16), jax.ShapeDtypeStruct((s, KV_LORA), BF16)),
        in_specs=[row(IN_PAD), whole(qn), whole(kvn), whole(wq), whole(wkv), row(LANE), row(LANE)],
        out_specs=(row(hp), row(hp), row(N_HEADS * V_HEAD), row(Q_LORA), row(KV_LORA)),
        compiler_params=_params(("parallel",)),
    )(z, qn, kvn, wq, wkv, cos, sin)


def _qkv_bwd(dq, dk, dv, du, z, qn, kvn, wq, wkv, cos, sin, name):
    s = z.shape[0]
    ts = _tile(s, 512)

    def norm_bwd(x, g, dy):
        _, xn, r = _plain_rms(x, g)
        dxn = dy * g
        return r * (dxn - xn * jnp.mean(dxn * xn, axis=-1, keepdims=True)), jnp.sum(dy * xn, axis=0, keepdims=True)

    def body(dq_ref, dk_ref, dv_ref, du_ref, z_ref, qn_ref, kvn_ref, wq_ref, wkv_ref, cos_ref, sin_ref,
             dz_ref, dqb_ref, dkvb_ref, dqn_ref, dkvn_ref):
        @pl.when(pl.program_id(0) == 0)
        def _():
            dqn_ref[...] = jnp.zeros_like(dqn_ref)
            dkvn_ref[...] = jnp.zeros_like(dkvn_ref)

        cosv, sinv = cos_ref[...], sin_ref[...]
        dkr = jnp.zeros((ts, LANE), F32)
        for h in range(N_HEADS):
            o = h * HEAD_PAD
            dqb_ref[:, o:o + QK_NOPE] = dq_ref[:, o:o + QK_NOPE].astype(BF16)
            dqb_ref[:, o + QK_NOPE:o + HEAD_PAD] = _rope_t(dq_ref[:, o + QK_NOPE:o + HEAD_PAD], cosv, sinv).astype(BF16)
            dkvb_ref[:, o:o + QK_NOPE] = dk_ref[:, o:o + QK_NOPE].astype(BF16)
            dkvb_ref[:, o + QK_NOPE:o + HEAD_PAD] = dv_ref[:, h * V_HEAD:(h + 1) * V_HEAD].astype(BF16)
            dkr = dkr + dk_ref[:, o + QK_NOPE:o + HEAD_PAD]
        dcqn = jnp.dot(dqb_ref[...], wq_ref[...], preferred_element_type=F32)
        dckvn = jnp.dot(dkvb_ref[...], wkv_ref[...], preferred_element_type=F32)
        dcq, dqn = norm_bwd(z_ref[:, O_Q:O_KV], qn_ref[...], dcqn)
        dckv, dkvn = norm_bwd(z_ref[:, O_KV:O_KR], kvn_ref[...], dckvn)
        dqn_ref[...] += dqn
        dkvn_ref[...] += dkvn
        dz_ref[:, 0:O_Q] = du_ref[...].astype(BF16)
        dz_ref[:, O_Q:O_KV] = dcq.astype(BF16)
        dz_ref[:, O_KV:O_KR] = dckv.astype(BF16)
        dz_ref[:, O_KR:IN_PAD] = _rope_t(dkr, cosv, sinv).astype(BF16)

    def row(w):
        return pl.BlockSpec((ts, w), lambda i: (i, 0))

    def whole(arr):
        return pl.BlockSpec(arr.shape, lambda i: (0, 0))

    hp = N_HEADS * HEAD_PAD
    return pl.pallas_call(
        body, name=name, grid=(s // ts,),
        out_shape=(jax.ShapeDtypeStruct((s, IN_PAD), BF16), jax.ShapeDtypeStruct((s, hp), BF16),
                   jax.ShapeDtypeStruct((s, hp), BF16),
                   jax.ShapeDtypeStruct((1, Q_LORA), F32), jax.ShapeDtypeStruct((1, KV_LORA), F32)),
        in_specs=[row(hp), row(hp), row(N_HEADS * V_HEAD), row(POOL_WIDTH), row(IN_PAD),
                  whole(qn), whole(kvn), whole(wq), whole(wkv), row(LANE), row(LANE)],
        out_specs=(row(IN_PAD), row(hp), row(hp), whole(qn), whole(kvn)),
        compiler_params=_params(("arbitrary",)),
    )(dq, dk, dv, du, z, qn, kvn, wq, wkv, cos, sin)


def _causal_scores(q, k, i, tq, klen):
    sc = lax.dot_general(q, k, (((1,), (1,)), ((), ())), preferred_element_type=F32) * SOFTMAX_SCALE
    qpos = i * tq + lax.broadcasted_iota(jnp.int32, (tq, klen), 0)
    kpos = lax.broadcasted_iota(jnp.int32, (tq, klen), 1)
    return jnp.where(qpos >= kpos, sc, -jnp.inf)


ATTN_TQ = 512
ATTN_SEGMENTS = 4


def _by_key_prefix(i, nq, tq, compute):
    nseg = min(ATTN_SEGMENTS, nq)
    per = nq // nseg
    for r in range(nseg):
        pl.when(i // per == r)(lambda r=r: compute((r + 1) * per * tq))


def _attn_fwd(q, k, v, name):
    s = q.shape[0]
    tq = _tile(s, ATTN_TQ)
    nq = s // tq

    def body(q_ref, k_ref, v_ref, o_ref, lse_ref):
        i = pl.program_id(1)

        def compute(klen):
            sc = _causal_scores(q_ref[...], k_ref[0:klen, :], i, tq, klen)
            mx = jnp.max(sc, axis=-1, keepdims=True)
            p = jnp.exp(sc - mx)
            den = jnp.sum(p, axis=-1, keepdims=True)
            o_ref[...] = jnp.dot((p / den).astype(BF16), v_ref[0:klen, :], preferred_element_type=F32)
            lse_ref[...] = mx + jnp.log(den)

        _by_key_prefix(i, nq, tq, compute)

    return pl.pallas_call(
        body, name=name, grid=(N_HEADS, s // tq),
        out_shape=(jax.ShapeDtypeStruct((s, N_HEADS * V_HEAD), F32), jax.ShapeDtypeStruct((N_HEADS, s, 1), F32)),
        in_specs=[pl.BlockSpec((tq, HEAD_PAD), lambda h, i: (i, h)),
                  pl.BlockSpec((s, HEAD_PAD), lambda h, i: (0, h)),
                  pl.BlockSpec((s, V_HEAD), lambda h, i: (0, h))],
        out_specs=(pl.BlockSpec((tq, V_HEAD), lambda h, i: (i, h)),
                   pl.BlockSpec((None, tq, 1), lambda h, i: (h, i, 0))),
        compiler_params=_params(("parallel", "parallel")),
    )(q, k, v)


def _attn_bwd(q, k, v, lse, dycat, name):
    s = q.shape[0]
    tq = _tile(s, ATTN_TQ)
    nq = s // tq
    tn_dims = (((0,), (0,)), ((), ()))

    def body(q_ref, k_ref, v_ref, lse_ref, do_ref, dq_ref, dk_ref, dv_ref):
        i = pl.program_id(1)

        @pl.when(i == 0)
        def _():
            dk_ref[...] = jnp.zeros_like(dk_ref)
            dv_ref[...] = jnp.zeros_like(dv_ref)

        def compute(klen):
            qv, kv_, dob = q_ref[...], k_ref[0:klen, :], do_ref[...].astype(BF16)
            sc = _causal_scores(qv, kv_, i, tq, klen)
            p = jnp.exp(sc - lse_ref[...])
            dp = lax.dot_general(dob, v_ref[0:klen, :], (((1,), (1,)), ((), ())), preferred_element_type=F32)
            ds = (p * (dp - jnp.sum(dp * p, axis=-1, keepdims=True)) * SOFTMAX_SCALE).astype(BF16)
            dq_ref[...] = jnp.dot(ds, kv_, preferred_element_type=F32)
            dk_ref[0:klen, :] += lax.dot_general(ds, qv, tn_dims, preferred_element_type=F32)
            dv_ref[0:klen, :] += lax.dot_general(p.astype(BF16), dob, tn_dims, preferred_element_type=F32)

        _by_key_prefix(i, nq, tq, compute)

    n_pool_blocks = POOL_WIDTH // V_HEAD
    return pl.pallas_call(
        body, name=name, grid=(N_HEADS, s // tq),
        out_shape=(jax.ShapeDtypeStruct((s, N_HEADS * HEAD_PAD), F32),
                   jax.ShapeDtypeStruct((s, N_HEADS * HEAD_PAD), F32),
                   jax.ShapeDtypeStruct((s, N_HEADS * V_HEAD), F32)),
        in_specs=[pl.BlockSpec((tq, HEAD_PAD), lambda h, i: (i, h)),
                  pl.BlockSpec((s, HEAD_PAD), lambda h, i: (0, h)),
                  pl.BlockSpec((s, V_HEAD), lambda h, i: (0, h)),
                  pl.BlockSpec((None, tq, 1), lambda h, i: (h, i, 0)),
                  pl.BlockSpec((tq, V_HEAD), lambda h, i: (i, n_pool_blocks + h))],
        out_specs=(pl.BlockSpec((tq, HEAD_PAD), lambda h, i: (i, h)),
                   pl.BlockSpec((s, HEAD_PAD), lambda h, i: (0, h)),
                   pl.BlockSpec((s, V_HEAD), lambda h, i: (0, h))),
        compiler_params=_params(("parallel", "arbitrary")),
    )(q, k, v, lse, dycat)


def _loss_head(x, gw, target, below, name):
    s, d = x.shape
    ts = _tile(s, 512)
    factor = below[2]

    def body(x_ref, gw_ref, tgt_ref, yb_ref, gb_ref, loss_ref, dx_ref, dgw_ref, dyb_ref, dgb_ref):
        @pl.when(pl.program_id(0) == 0)
        def _():
            loss_ref[...] = jnp.zeros_like(loss_ref)
            dgw_ref[...] = jnp.zeros_like(dgw_ref)
            dgb_ref[...] = jnp.zeros_like(dgb_ref)

        xv, gwv = x_ref[...], gw_ref[...]
        r = lax.rsqrt(jnp.mean(xv * xv, axis=-1, keepdims=True) + EPS)
        xn = xv * r
        err = xn * gwv - tgt_ref[...]
        loss_ref[...] += 0.5 * jnp.sum(jnp.mean(err * err, axis=-1, keepdims=True))
        dy = err / d
        dgw_ref[...] += jnp.sum(dy * xn, axis=0, keepdims=True)
        dxn = dy * gwv
        dx = r * (dxn - xn * jnp.mean(dxn * xn, axis=-1, keepdims=True))
        dx_ref[...] = dx
        dyb_ref[...] = ((factor * gb_ref[...]) * dx).astype(BF16)
        dgb_ref[...] += jnp.sum((factor * dx) * yb_ref[...].astype(F32), axis=0, keepdims=True)

    row = pl.BlockSpec((ts, d), lambda i: (i, 0))
    gate_spec, gate_arg = _vec_in(below[1])
    vec = jax.ShapeDtypeStruct((1, d), F32)
    return pl.pallas_call(
        body, name=name, grid=(s // ts,),
        out_shape=(jax.ShapeDtypeStruct((8, LANE), F32), jax.ShapeDtypeStruct((s, d), F32), vec,
                   jax.ShapeDtypeStruct((s, d), BF16), vec),
        in_specs=[row, _vec_spec(d), row, row, gate_spec],
        out_specs=(pl.BlockSpec((8, LANE), lambda i: (0, 0)), row, _vec_spec(d), row, _vec_spec(d)),
        compiler_params=_params(("arbitrary",)),
    )(x, gw, target, below[0], gate_arg)


def _ada_mod(c_all, ada_w, ada_b, name):
    nl, d, cols = ada_w.shape

    def body(c_ref, w_ref, b_ref, o_ref):
        cv = c_ref[...]
        act = (cv * jax.nn.sigmoid(cv)).astype(BF16)
        o_ref[...] = jnp.dot(act, w_ref[...].astype(BF16), preferred_element_type=F32) + b_ref[...]

    return pl.pallas_call(
        body, name=name, grid=(nl,), out_shape=jax.ShapeDtypeStruct((nl, N_DEV, cols), F32),
        in_specs=[pl.BlockSpec((N_DEV, d), lambda l: (0, 0)),
                  pl.BlockSpec((None, d, cols), lambda l: (l, 0, 0)),
                  pl.BlockSpec((None, 1, cols), lambda l: (l, 0, 0))],
        out_specs=pl.BlockSpec((None, N_DEV, cols), lambda l: (l, 0, 0)),
        compiler_params=_params(("parallel",)),
    )(c_all, ada_w, ada_b)


def _ada_grad(c_pad, dmod_pad, name):
    nl, kpad, cols = dmod_pad.shape
    d = c_pad.shape[1]

    def body(c_ref, dm_ref, o_ref):
        cv = c_ref[...]
        act = (cv * jax.nn.sigmoid(cv)).astype(BF16)
        o_ref[...] = lax.dot_general(act, dm_ref[...].astype(BF16), (((0,), (0,)), ((), ())),
                                     preferred_element_type=F32)

    return pl.pallas_call(
        body, name=name, grid=(nl,), out_shape=jax.ShapeDtypeStruct((nl, d, cols), F32),
        in_specs=[pl.BlockSpec((kpad, d), lambda l: (0, 0)),
                  pl.BlockSpec((None, kpad, cols), lambda l: (l, 0, 0))],
        out_specs=pl.BlockSpec((None, d, cols), lambda l: (l, 0, 0)),
        compiler_params=_params(("parallel",)),
    )(c_pad, dmod_pad)


def _adamw_math(w, g, m, v):
    nm = ADAM_B1 * m + (1.0 - ADAM_B1) * g
    nv = ADAM_B2 * v + (1.0 - ADAM_B2) * (g * g)
    m_hat = nm / (1.0 - ADAM_B1 ** ADAM_STEP)
    v_hat = nv / (1.0 - ADAM_B2 ** ADAM_STEP)
    return -ADAM_LR * (m_hat / (jnp.sqrt(v_hat) + ADAM_EPS) + ADAM_WD * w), nm, nv


def _adamw_rows(w3, gbuf, row_off, m3, v3, name):
    nl, r, d = w3.shape
    tr = _row_tile(math.gcd(r, row_off) if row_off else r, 352)
    first = row_off // tr

    def body(w_ref, g_ref, m_ref, v_ref, go_ref, d_ref, nm_ref, nv_ref):
        gv = g_ref[...]
        go_ref[...] = gv
        d_ref[...], nm_ref[...], nv_ref[...] = _adamw_math(w_ref[...], gv, m_ref[...], v_ref[...])

    blk = pl.BlockSpec((None, tr, d), lambda l, i: (l, i, 0))
    gblk = pl.BlockSpec((None, tr, d), lambda l, i: (l, first + i, 0))
    out = jax.ShapeDtypeStruct((nl, r, d), F32)
    return pl.pallas_call(
        body, name=name, grid=(nl, r // tr), out_shape=(out, out, out, out),
        in_specs=[blk, gblk, blk, blk], out_specs=(blk, blk, blk, blk),
        compiler_params=_params(("parallel", "parallel")),
    )(w3, gbuf, m3, v3)


def _adamw(w, g, m, v, name):
    rows, cols = w.shape
    tr = _row_tile(rows, 512)

    def body(w_ref, g_ref, m_ref, v_ref, d_ref, nm_ref, nv_ref):
        d_ref[...], nm_ref[...], nv_ref[...] = _adamw_math(w_ref[...], g_ref[...], m_ref[...], v_ref[...])

    blk = pl.BlockSpec((tr, cols), lambda i: (i, 0))
    out = jax.ShapeDtypeStruct((rows, cols), F32)
    return pl.pallas_call(
        body, name=name, grid=(rows // tr,), out_shape=(out, out, out),
        in_specs=[blk, blk, blk, blk], out_specs=(blk, blk, blk),
        compiler_params=_params(("parallel",)),
    )(w, g, m, v)


def _adamw_nd(w, g, m, v, name):
    shape = w.shape
    flat = (lambda t: t.reshape(1, -1)) if w.ndim == 1 else (lambda t: t.reshape(-1, shape[-1]))
    return tuple(t.reshape(shape) for t in _adamw(flat(w), flat(g), flat(m), flat(v), name))


def _pad_rows(t, rows):
    return jnp.pad(t, ((0, rows - t.shape[0]), (0, 0)))


def _pack_shard_layer(l, wts):
    def tr(name):
        return wts[name][l].astype(BF16).T

    parts = [tr("ffn1_w_gate"), tr("ffn1_w_up"), wts["ffn1_w_down"][l].astype(BF16),
             tr("ffn2_w_gate"), tr("ffn2_w_up"), wts["ffn2_w_down"][l].astype(BF16),
             wts["w_out"][l].astype(BF16),
             tr("w_kv_b").reshape(KV_SH_ROWS, D_MODEL),
             _pad_rows(tr("w_in"), 160),
             _pad_rows(tr("w_q_b").reshape(Q_SH_ROWS, D_MODEL), Q_PAD_ROWS)]
    return jnp.concatenate(parts, axis=0)


def _mixer_weights(w_out, small):
    w = {"out": w_out}
    small = small.reshape(N_DEV, SMALL_ROWS, D_MODEL)
    o_in, o_q = OFF_IN - OFF_KV, OFF_Q - OFF_KV
    w["kv"] = small[:, :KV_SH_ROWS].reshape(N_HEADS * HEAD_PAD, KV_LORA)
    w["in"] = _pad_rows(small[:, o_in:o_in + IN_SH].reshape(IN_COLS, D_MODEL), IN_PAD)
    wq = small[:, o_q:o_q + Q_SH_ROWS].reshape(N_HEADS, QK_HEAD, Q_LORA)
    w["q"] = jnp.pad(wq, ((0, 0), (0, HEAD_PAD - QK_HEAD), (0, 0))).reshape(N_HEADS * HEAD_PAD, Q_LORA)
    return w


def _grad_sources_tail(gr):
    gq = gr["q"].reshape(N_HEADS, HEAD_PAD, Q_LORA)[:, :QK_HEAD].reshape(N_DEV, Q_SH_ROWS, D_MODEL)
    small = jnp.concatenate([
        gr["kv"].reshape(N_DEV, KV_SH_ROWS, D_MODEL),
        jnp.pad(gr["in"][:IN_COLS].reshape(N_DEV, IN_SH, D_MODEL), ((0, 0), (0, 160 - IN_SH), (0, 0))),
        jnp.pad(gq, ((0, 0), (0, Q_PAD_ROWS - Q_SH_ROWS), (0, 0)))], axis=1)
    return [gr["out"], small.reshape(N_DEV * SMALL_ROWS, D_MODEL)]


def _pack_bf16_pairs(t):
    rows, d = t.shape
    return lax.bitcast_convert_type(t.astype(BF16).reshape(rows // 2, 2, d).transpose(0, 2, 1), F32)


def _unpack_bf16_pairs(p):
    pairs = jnp.swapaxes(lax.bitcast_convert_type(p, BF16), -1, -2)
    return pairs.reshape(p.shape[:-2] + (2 * p.shape[-2], p.shape[-1]))


def _small_layout(nl):
    names = [("dmod", nl * N_MOD), ("ffn1_norm", nl), ("mix_norm", nl), ("ffn2_norm", nl), ("q_a_norm", nl),
             ("kv_a_norm", nl), ("pool_scale", nl), ("final_norm", 1), ("loss", 1),
             ("pool_w", nl * 4 * POOL_GC * POOL_GC // D_MODEL // 2)]
    off, table = 0, {}
    for name, n in names:
        table[name] = (off, n)
        off += -(-n // 8) * 8
    return table, off


def _to_rows(t, width=D_MODEL):
    n, w = t.shape
    return jnp.pad(t, ((0, -(-n // 8) * 8 - n), (0, width - w)))


def kernel(x, c, positions, ada_w, ada_b, ffn1_norm, ffn1_w_gate, ffn1_w_up, ffn1_w_down, mix_norm, w_in, pool_w, pool_scale, q_a_norm, w_q_b, kv_a_norm, w_kv_b, w_out, ffn2_norm, ffn2_w_gate, ffn2_w_up, ffn2_w_down, final_norm, loss_target, m_ada_w, m_ada_b, m_ffn1_norm, m_ffn1_w_gate, m_ffn1_w_up, m_ffn1_w_down, m_mix_norm, m_w_in, m_pool_w, m_pool_scale, m_q_a_norm, m_w_q_b, m_kv_a_norm, m_w_kv_b, m_w_out, m_ffn2_norm, m_ffn2_w_gate, m_ffn2_w_up, m_ffn2_w_down, m_final_norm, v_ada_w, v_ada_b, v_ffn1_norm, v_ffn1_w_gate, v_ffn1_w_up, v_ffn1_w_down, v_mix_norm, v_w_in, v_pool_w, v_pool_scale, v_q_a_norm, v_w_q_b, v_kv_a_norm, v_w_kv_b, v_w_out, v_ffn2_norm, v_ffn2_w_gate, v_ffn2_w_up, v_ffn2_w_down, v_final_norm):
    wts = dict(ada_w=ada_w, ada_b=ada_b, ffn1_norm=ffn1_norm, ffn1_w_gate=ffn1_w_gate, ffn1_w_up=ffn1_w_up,
               ffn1_w_down=ffn1_w_down, mix_norm=mix_norm, w_in=w_in, pool_w=pool_w, pool_scale=pool_scale,
               q_a_norm=q_a_norm, w_q_b=w_q_b, kv_a_norm=kv_a_norm, w_kv_b=w_kv_b, w_out=w_out,
               ffn2_norm=ffn2_norm, ffn2_w_gate=ffn2_w_gate, ffn2_w_up=ffn2_w_up, ffn2_w_down=ffn2_w_down,
               final_norm=final_norm)
    mom_m = dict(ada_w=m_ada_w, ada_b=m_ada_b, ffn1_norm=m_ffn1_norm, ffn1_w_gate=m_ffn1_w_gate,
                 ffn1_w_up=m_ffn1_w_up, ffn1_w_down=m_ffn1_w_down, mix_norm=m_mix_norm, w_in=m_w_in,
                 pool_w=m_pool_w, pool_scale=m_pool_scale, q_a_norm=m_q_a_norm, w_q_b=m_w_q_b,
                 kv_a_norm=m_kv_a_norm, w_kv_b=m_w_kv_b, w_out=m_w_out, ffn2_norm=m_ffn2_norm,
                 ffn2_w_gate=m_ffn2_w_gate, ffn2_w_up=m_ffn2_w_up, ffn2_w_down=m_ffn2_w_down,
                 final_norm=m_final_norm)
    mom_v = dict(ada_w=v_ada_w, ada_b=v_ada_b, ffn1_norm=v_ffn1_norm, ffn1_w_gate=v_ffn1_w_gate,
                 ffn1_w_up=v_ffn1_w_up, ffn1_w_down=v_ffn1_w_down, mix_norm=v_mix_norm, w_in=v_w_in,
                 pool_w=v_pool_w, pool_scale=v_pool_scale, q_a_norm=v_q_a_norm, w_q_b=v_w_q_b,
                 kv_a_norm=v_kv_a_norm, w_kv_b=v_w_kv_b, w_out=v_w_out, ffn2_norm=v_ffn2_norm,
                 ffn2_w_gate=v_ffn2_w_gate, ffn2_w_up=v_ffn2_w_up, ffn2_w_down=v_ffn2_w_down,
                 final_norm=v_final_norm)
    order = list(wts)
    nl = ada_w.shape[0]
    seq = x.shape[1]
    me = 4 * lax.axis_index("x") + 2 * lax.axis_index("y") + lax.axis_index("c")
    ada_cols = ada_w.shape[2]

    def after_token(t, token):
        return t + token[0:1, 0:1].astype(t.dtype)

    packs = [_pack_shard_layer(l, wts) for l in range(nl)]

    c_all = _all_gather(jnp.broadcast_to(c, (8, D_MODEL)), "gather_c")[::8]

    ada_b_mine = lax.dynamic_slice_in_dim(ada_b, me * ada_cols, ada_cols, axis=1).reshape(nl, 1, ada_cols)
    mod_part = _ada_mod(c_all, ada_w, ada_b_mine, "ada_mod")
    mod_all = _all_gather(mod_part.reshape(nl * N_DEV, ada_cols), "gather_mod")
    mod_all = mod_all.reshape(N_DEV, nl, N_DEV, ada_cols)
    mod = lax.dynamic_index_in_dim(mod_all, me, axis=2, keepdims=False)
    mod = mod.transpose(1, 0, 2).reshape(nl * N_MOD, 1, D_MODEL)
    norm_tables = {name: wts[name].reshape(nl, 1, D_MODEL) for name in ("ffn1_norm", "mix_norm", "ffn2_norm")}

    def modrow(l, k):
        return mod, l * N_MOD + k

    def normrow(name, l):
        return norm_tables[name], l

    def start_layer(l, after):
        first = _gather_start(packs[l][:SPLIT_AB], ROWS_A, after, f"gather_start_{l}a")
        mixer = _gather_start(packs[l][OFF_OUT:], ROWS_TAIL, first[4], f"gather_start_{l}b")
        second = _gather_start(packs[l][SPLIT_AB:OFF_OUT], ROWS_A, mixer[4], f"gather_start_{l}c")
        return first, mixer, second

    flights = {0: start_layer(0, mod)}
    if nl > 1:
        flights[1] = start_layer(1, flights[0][2][4])
    last_start = flights[min(1, nl - 1)][2][4]

    cos, sin = _rope_tables(after_token(positions.reshape(seq, 1), last_start), "rope_tables")

    def vec(t):
        return t.reshape(1, -1)

    def landed(flight, rows_list, after, tag):
        send_sems, recv_sems, pk, lands, _ = flight
        pk, lands = _gather_wait(send_sems, recv_sems, pk, lands, after, f"gather_wait_{tag}")
        return _gather_finish(pk, rows_list, lands, "gather_finish")

    xs = x.reshape(seq, D_MODEL)
    saved = []
    for l in range(nl):
        norm1, up_after = normrow("ffn1_norm", l), None
        flight_a, flight_b, flight_c = flights[l]
        lands = landed(flight_a, ROWS_A, cos if l == 0 else xs, f"{l}a")
        if l >= 1 and l + 1 < nl:
            flights[l + 1] = start_layer(l + 1, lands[0])
            up_after = flights[l + 1][2][4]
        sv = {}

        def ffn_fwd(xin, norm, k0, wg, wu, wd, tag, after=None):
            h, a, b, t = _ffn_up(xin, norm, modrow(l, k0), modrow(l, k0 + 1), wg, wu, "ffn_up", after=after)
            y, xout = _mm(t, wd, "nn", "ffn_down", res=xin, gate=modrow(l, k0 + 2), gate_factor=0.5)
            sv[tag] = dict(x=xin, h=h, a=a, b=b, t=t, y=y)
            return xout

        xs = ffn_fwd(xs, norm1, 0, lands[0], lands[1], lands[2], "f1", up_after)
        w = dict(zip(("g1", "u1", "d1"), lands[:3]))
        w.update(_mixer_weights(*landed(flight_b, ROWS_TAIL, xs, f"{l}b")))
        sv["w"] = w

        h2, z = _norm_mm(xs, normrow("mix_norm", l), modrow(l, 3), modrow(l, 4), w["in"], "mix_in")
        y_pool, diff = _pool_fwd(z, pool_w[l], vec(pool_scale[l]), "pool_fwd")
        q, k, v, cqn, ckvn = _qkv_fwd(z, vec(q_a_norm[l]), vec(kv_a_norm[l]), w["q"], w["kv"], cos, sin, "qkv_fwd")
        o, lse = _attn_fwd(q, k, v, "attn_fwd")
        ycat = jnp.concatenate([y_pool, o.astype(BF16)], axis=1)
        y2, xmix = _mm(ycat, w["out"], "nn", "mix_out", res=xs, gate=modrow(l, 5), gate_factor=1.0)
        sv["mix"] = dict(x=xs, h=h2, z=z, diff=diff, q=q, k=k, v=v, cqn=cqn, ckvn=ckvn, lse=lse, ycat=ycat, y=y2)
        xs = xmix

        w.update(zip(("g2", "u2", "d2"), landed(flight_c, ROWS_A, xs, f"{l}c")))
        xs = ffn_fwd(xs, normrow("ffn2_norm", l), 6, w["g2"], w["u2"], w["d2"], "f2")
        saved.append(sv)

    loss_part, dx, d_final, *head = _loss_head(xs, vec(final_norm), loss_target.reshape(seq, D_MODEL),
                                               (saved[nl - 1]["f2"]["y"], modrow(nl - 1, 8), 0.5), "loss_head")

    small = {name: [None] * nl for name in ("ffn1_norm", "mix_norm", "ffn2_norm", "q_a_norm", "kv_a_norm",
                                            "pool_scale", "pool_w", "dmod")}
    core = lax.axis_index("c").astype(jnp.int32).reshape(1)
    chip = 2 * lax.axis_index("x") + lax.axis_index("y")
    exchanges = []

    def leave(srcs, rows_list, after, tag):
        return _pair_start(srcs, rows_list, after, f"pair_start_{tag}"), rows_list, tag

    def forward_on(pending, after, layer, row_off):
        (send_sems, recv_sems, srcs, land, _), rows_list, tag = pending
        srcs, land = _split_wait(send_sems, recv_sems, 1, srcs, land, after, f"pair_wait_{tag}")
        sums = _pair_sum(srcs, rows_list, land, core, "pair_sum")
        flight = _chip_exchange_start(sums, chip, after, f"exchange_start_{tag}")
        exchanges.append((flight, layer, row_off, tag))
        return flight[4]

    pending = None
    for l in reversed(range(nl)):
        sv = saved[l]
        w = sv["w"]
        dmod = [None] * N_MOD
        gr = {}

        def ffn_bwd(dxin, head, s_, norm, k0, wg, wu, wd, tag, below, first_after=None, mid=None):
            dy, dmod[k0 + 2] = head
            da, db, gr["d" + tag], gr["g" + tag], gr["u" + tag] = _ffn_bwd_cols(
                dy, s_["h"], s_["a"], s_["b"], s_["t"], wd, "ffn_bwd_cols", after=first_after)
            dh = _mm_pair(da, wg, db, wu, "ffn_bwd_dh", after=None if mid is None else mid(da))
            outs = _rm_bwd(dh, s_["x"], dxin, norm, modrow(l, k0 + 1), "rm_bwd", below=below)
            dmod[k0], dmod[k0 + 1] = outs[1], outs[2]
            return outs[0], outs[3], outs[4:]

        s_ = sv["mix"]
        dx, small["ffn2_norm"][l], head = ffn_bwd(
            dx, head, sv["f2"], normrow("ffn2_norm", l), 6, w["g2"], w["u2"], w["d2"], "2", (s_["y"], modrow(l, 5), 1.0),
            first_after=None if pending is None else pending[0][4])

        pending_c = leave([gr["g2"], gr["u2"], gr["d2"]], ROWS_A, dx, f"{l}c")
        mix_after = pending_c[0][4]
        if pending is not None:
            mix_after = forward_on(pending, mix_after, l + 1, GB_F1)
            pending = None
        dy, dmod[5] = head
        gr["out"] = _mm(s_["ycat"], dy, "tn", "mix_out_dw", out_dtype=BF16, tm=512, after=mix_after)
        dycat = _mm(dy, w["out"], "nt", "mix_out_dx", tm=1024)
        du, small["pool_w"][l], small["pool_scale"][l] = _pool_bwd(dycat, s_["diff"], pool_w[l], vec(pool_scale[l]), "pool_bwd")
        dq, dk, dv = _attn_bwd(s_["q"], s_["k"], s_["v"], s_["lse"], dycat, "attn_bwd")
        dz, dqb, dkvb, small["q_a_norm"][l], small["kv_a_norm"][l] = _qkv_bwd(
            dq, dk, dv, du, s_["z"], vec(q_a_norm[l]), vec(kv_a_norm[l]), w["q"], w["kv"], cos, sin, "qkv_bwd")
        gr["q"] = _mm(dqb, s_["cqn"], "tn", "q_b_dw", out_dtype=BF16, tm=512, after=forward_on(pending_c, dz, l, GB_F2))
        gr["kv"] = _mm(dkvb, s_["ckvn"], "tn", "kv_b_dw", out_dtype=BF16, tm=512)
        gr["in"] = _mm(dz, s_["h"], "tn", "mix_in_dw", out_dtype=BF16, tm=512)
        dh2 = _mm(dz, w["in"], "nn", "mix_in_dx", tm=1024)
        outs = _rm_bwd(dh2, s_["x"], dx, normrow("mix_norm", l), modrow(l, 4), "rm_bwd",
                       below=(sv["f1"]["y"], modrow(l, 2), 0.5))
        dx, dmod[3], dmod[4], small["mix_norm"][l] = outs[:4]
        head = outs[4:]

        first_after, mid = None, None
        if l == 0:
            pending_b = leave(_grad_sources_tail(gr), ROWS_TAIL, dx, "0b")
            first_after = pending_b[0][4]
            last_groups = []

            def mid(da):
                last_groups.append(leave([gr["g1"], gr["u1"], gr["d1"]], ROWS_A, da, "0a"))
                return forward_on(pending_b, last_groups[0][0][4], 0, GB_TAIL)
        below = (saved[l - 1]["f2"]["y"], modrow(l - 1, 8), 0.5) if l > 0 else None
        dx, small["ffn1_norm"][l], head = ffn_bwd(
            dx, head, sv["f1"], normrow("ffn1_norm", l), 0, w["g1"], w["u1"], w["d1"], "1", below, first_after, mid)

        small["dmod"][l] = jnp.concatenate(dmod, axis=0)
        if l > 0:
            pending = leave([gr["g1"], gr["u1"], gr["d1"]] + _grad_sources_tail(gr), ROWS_A + ROWS_TAIL, dx, l)

    grad_x = dx.reshape(x.shape)
    pending_a = last_groups[0]

    layout, small_rows = _small_layout(nl)
    pieces = {
        "dmod": jnp.concatenate(small["dmod"], axis=0),
        "ffn1_norm": jnp.concatenate(small["ffn1_norm"], axis=0),
        "mix_norm": jnp.concatenate(small["mix_norm"], axis=0),
        "ffn2_norm": jnp.concatenate(small["ffn2_norm"], axis=0),
        "q_a_norm": jnp.concatenate(small["q_a_norm"], axis=0),
        "kv_a_norm": jnp.concatenate(small["kv_a_norm"], axis=0),
        "pool_scale": jnp.concatenate(small["pool_scale"], axis=0),
        "final_norm": d_final,
        "loss": jnp.broadcast_to(loss_part[0:1, 0:1], (1, D_MODEL)),
        "pool_w": _pack_bf16_pairs(jnp.stack(small["pool_w"]).reshape(-1, D_MODEL)),
    }
    small_buf = jnp.concatenate([_to_rows(pieces[name]) for name in layout], axis=0)
    def landed_sums(gbuf, entries, after):
        for (send_sems, recv_sems, sums, recv, _), layer, row_off, tag in entries:
            _, recv = _split_wait(send_sems, recv_sems, N_CHIPS - 1, sums, recv, after, f"exchange_wait_{tag}")
            gbuf = _sum_slots_into(recv, gbuf, layer, row_off, "sum_grads")
        return gbuf

    gbuf = lax.empty((nl, ROWS_L, D_MODEL), F32)
    token_0a = forward_on(pending_a, dx, 0, GB_F1)
    spread = _spread_start(small_buf, me, token_0a, "small_start")
    gbuf = landed_sums(gbuf, [e for e in exchanges if e[3] != "0a"], spread[4])

    def swap(t):
        return t.transpose(0, 2, 1)

    def same(t):
        return t

    grads, updates = {}, {}

    def update_rows(gbuf, table):
        for wname, off, view in table:
            g, d_, nm, nv = _adamw_rows(view(wts[wname]), gbuf, off, view(mom_m[wname]), view(mom_v[wname]), "adamw_rows")
            grads[wname], updates[wname] = view(g), (view(d_), view(nm), view(nv))

    update_rows(gbuf, (("ffn2_w_gate", GB_F2, swap), ("ffn2_w_up", GB_F2 + FF_SH, swap),
                       ("ffn2_w_down", GB_F2 + 2 * FF_SH, same), ("w_out", GB_TAIL, same)))
    small_grads = {
        "w_kv_b": (gbuf[:, OFF_KV:OFF_KV + KV_SH_ROWS].reshape(nl, -1, KV_LORA).transpose(0, 2, 1), same),
        "w_in": (gbuf[:, OFF_IN:OFF_IN + IN_SH], swap),
        "w_q_b": (gbuf[:, OFF_Q:OFF_Q + Q_SH_ROWS].reshape(nl, -1, Q_LORA), swap),
    }
    for wname, (g, view) in small_grads.items():
        upd = _adamw_nd(view(wts[wname]), g, view(mom_m[wname]), view(mom_v[wname]), "adamw")
        grads[wname], updates[wname] = view(g), tuple(view(t) for t in upd)
    gbuf = landed_sums(gbuf, [e for e in exchanges if e[3] == "0a"], updates["w_q_b"][0])
    update_rows(gbuf, (("ffn1_w_gate", GB_F1, swap), ("ffn1_w_up", GB_F1 + FF_SH, swap),
                       ("ffn1_w_down", GB_F1 + 2 * FF_SH, same)))

    _, small_all = _split_wait(spread[0], spread[1], N_DEV - 1, spread[2], spread[3], updates["ffn1_w_down"][0],
                               "small_wait")
    pool_off, pool_rows = layout["pool_w"]
    small_sum = _sum_slots(small_all[:, :pool_off], "sum_small")
    pool_sum = _sum_slots(_unpack_bf16_pairs(small_all[:, pool_off:pool_off + pool_rows]), "sum_pool_w")

    def take(name, width=D_MODEL):
        off, n = layout[name]
        return small_sum[off:off + n, :width]

    late = {"ada_b": take("dmod").reshape(nl, N_MOD * D_MODEL),
            "ffn1_norm": take("ffn1_norm"), "mix_norm": take("mix_norm"), "ffn2_norm": take("ffn2_norm"),
            "q_a_norm": take("q_a_norm", Q_LORA), "kv_a_norm": take("kv_a_norm", KV_LORA),
            "pool_scale": take("pool_scale", POOL_WIDTH), "final_norm": take("final_norm").reshape(D_MODEL),
            "pool_w": pool_sum.reshape(pool_w.shape)}
    loss = take("loss")[0, 0]

    off, n = layout["dmod"]
    dmod_all = small_all[:, off:off + n].reshape(N_DEV, nl, N_MOD * D_MODEL)
    dmod_mine = lax.dynamic_slice_in_dim(dmod_all, me * ada_cols, ada_cols, axis=2)
    dmod_pad = jnp.pad(dmod_mine.transpose(1, 0, 2), ((0, 0), (0, LANE - N_DEV), (0, 0)))
    late["ada_w"] = _ada_grad(jnp.pad(c_all, ((0, LANE - N_DEV), (0, 0))), dmod_pad, "ada_grad")
    for name, g in late.items():
        grads[name], updates[name] = g, _adamw_nd(wts[name], g, mom_m[name], mom_v[name], "adamw")

    return (loss, grad_x, *[grads[n] for n in order], *[updates[n][0] for n in order],
            *[updates[n][1] for n in order], *[updates[n][2] for n in order])
```

```python
import math

import numpy as np
import jax
import jax.numpy as jnp
from jax import lax
from jax.experimental import pallas as pl
from jax.experimental.pallas import tpu as pltpu

F32 = jnp.float32
BF16 = jnp.bfloat16

N_DEV = 8
D_MODEL = 1024
D_FF = 2816
POOL_WIDTH = 512
POOL_WINDOWS = (2, 4, 8, 16)
POOL_GC = 128
N_HEADS = 4
QK_NOPE = 128
QK_ROPE = 64
V_HEAD = 128
QK_HEAD = QK_NOPE + QK_ROPE
HEAD_PAD = 256
Q_LORA = 384
KV_LORA = 256
IN_COLS = POOL_WIDTH + Q_LORA + KV_LORA + QK_ROPE
IN_PAD = 1280
ROPE_THETA = 10000.0
SOFTMAX_SCALE = 1.0 / math.sqrt(QK_HEAD)
EPS = 1e-6
N_MOD = 9

ADAM_LR = 0.001
ADAM_B1 = 0.9
ADAM_B2 = 0.999
ADAM_EPS = 1e-08
ADAM_WD = 0.01
ADAM_STEP = 10

LANE = 128
VMEM_LIMIT = 56 * 1024 * 1024

FF_SH = D_FF // N_DEV
OFF_G1, OFF_U1, OFF_D1 = 0, FF_SH, 2 * FF_SH
OFF_G2, OFF_U2, OFF_D2 = 3 * FF_SH, 4 * FF_SH, 5 * FF_SH
OFF_OUT = 6 * FF_SH
OFF_KV = OFF_OUT + 128
OFF_IN = OFF_KV + 32
OFF_Q = OFF_IN + 160
Q_PAD_ROWS = 64
ROWS_L = OFF_Q + Q_PAD_ROWS
IN_SH = IN_COLS // N_DEV
Q_SH_ROWS = (N_HEADS * QK_HEAD // N_DEV) * Q_LORA // D_MODEL
KV_SH_ROWS = (N_HEADS * (QK_NOPE + V_HEAD) // N_DEV) * KV_LORA // D_MODEL


def _tile(dim, target):
    if dim <= target:
        return dim
    best = None
    for t in range(LANE, target + 1, LANE):
        if dim % t == 0:
            best = t
    assert best is not None, (dim, target)
    return best


def _params(sem):
    return pltpu.CompilerParams(dimension_semantics=sem, vmem_limit_bytes=VMEM_LIMIT)


def _mesh_pos():
    return lax.axis_index("x"), lax.axis_index("y"), lax.axis_index("c")


def _all_gather(x, name):
    m, n = x.shape

    def body(x_ref, out_ref, send_sems, recv_sems, local_sem):
        px, py, pc = _mesh_pos()
        me, sibling = (px, py, pc), (px, py, 1 - pc)
        chips = [(1 - px, py), (px, 1 - py), (1 - px, 1 - py)]

        def rows(bx, by, bc):
            return out_ref.at[pl.ds((4 * bx + 2 * by + bc) * m, m), :]

        def copy(k, block, to, src=None):
            return pltpu.make_async_remote_copy(
                src_ref=rows(*block) if src is None else src, dst_ref=rows(*block),
                send_sem=send_sems.at[k], recv_sem=recv_sems.at[k],
                device_id=to, device_id_type=pl.DeviceIdType.MESH)

        mine = pltpu.make_async_copy(x_ref, rows(*me), local_sem)
        mine.start()
        first = [copy(0, me, sibling, src=x_ref)]
        first += [copy(1 + j, me, (*chip, pc), src=x_ref) for j, chip in enumerate(chips)]
        for cp in first:
            cp.start()
        passed = [copy(4 + j, (*chip, pc), sibling) for j, chip in enumerate(chips)]
        for j, chip in enumerate(chips):
            copy(1 + j, (*chip, pc), me).wait_recv()
            passed[j].start()
        copy(0, sibling, me).wait_recv()
        for j, chip in enumerate(chips):
            copy(4 + j, (*chip, 1 - pc), me).wait_recv()
        for cp in first + passed:
            cp.wait_send()
        mine.wait()

    hbm = pl.BlockSpec(memory_space=pltpu.HBM)
    return pl.pallas_call(
        body, name=name,
        out_shape=jax.ShapeDtypeStruct((N_DEV * m, n), x.dtype),
        in_specs=[hbm], out_specs=hbm,
        scratch_shapes=[pltpu.SemaphoreType.DMA((7,)), pltpu.SemaphoreType.DMA((7,)),
                        pltpu.SemaphoreType.DMA],
    )(x)


SMALL_ROWS = ROWS_L - OFF_KV
ROWS_A = [FF_SH] * 3
SPLIT_AB = sum(ROWS_A)
ROWS_TAIL = [128, SMALL_ROWS]
GB_F2, GB_F1, GB_TAIL = 0, SPLIT_AB, 2 * SPLIT_AB
HBM_SPEC = pl.BlockSpec(memory_space=pltpu.HBM)
SEM_SPEC = pl.BlockSpec(memory_space=pltpu.SEMAPHORE)
ANY_SPEC = pl.BlockSpec(memory_space=pl.ANY)
EFFECT = pltpu.SideEffectType.DATAFLOW_SIDE_EFFECTING


def _hbm(t):
    return pltpu.with_memory_space_constraint(t, pltpu.HBM)


def _whole_wait(ref, send_sem, recv_sem, peer):
    return pltpu.make_async_remote_copy(src_ref=ref, dst_ref=ref, send_sem=send_sem, recv_sem=recv_sem,
                                        device_id=peer, device_id_type=pl.DeviceIdType.MESH)


def _offsets(rows_list):
    return [sum(rows_list[:i]) for i in range(len(rows_list))]


def _gather_start(groups, after, name):
    g = len(groups)
    counts = [len(rows_list) for _, rows_list in groups]
    lands = [_hbm(lax.empty((N_DEV * rows, D_MODEL), BF16)) for _, rows_list in groups for rows in rows_list]
    n_in = g + len(lands)

    def body(*refs):
        packed_refs, land_refs = refs[:g], refs[g:n_in]
        sems = refs[n_in + 1:n_in + 1 + 2 * g]
        token = refs[-1]
        px, py, pc = _mesh_pos()
        me = 4 * px + 2 * py + pc
        peers = [(px, py, 1 - pc), (1 - px, py, pc), (px, 1 - py, pc), (1 - px, 1 - py, pc)]
        first = 0
        for gi, (_, rows_list) in enumerate(groups):
            land = land_refs[first:first + counts[gi]]
            first += counts[gi]
            for k, peer in enumerate(peers):
                for off, rows, land_ref in zip(_offsets(rows_list), rows_list, land):
                    pltpu.make_async_remote_copy(
                        src_ref=packed_refs[gi].at[pl.ds(off, rows), :],
                        dst_ref=land_ref.at[pl.ds(me * rows, rows), :],
                        send_sem=sems[2 * gi].at[k], recv_sem=sems[2 * gi + 1].at[k],
                        device_id=peer, device_id_type=pl.DeviceIdType.MESH).start()
        token[...] = jnp.zeros_like(token)

    packs = [_hbm(p) for p, _ in groups]
    outs = pl.pallas_call(
        body, name=name,
        out_shape=(*[pltpu.SemaphoreType.DMA((4,))] * (2 * g), *[pltpu.HBM(t.shape, BF16) for t in packs + lands],
                   jax.ShapeDtypeStruct((8, LANE), F32)),
        in_specs=(HBM_SPEC,) * n_in + (ANY_SPEC,),
        out_specs=(SEM_SPEC,) * (2 * g) + (HBM_SPEC,) * n_in + (pl.BlockSpec(memory_space=pltpu.VMEM),),
        input_output_aliases={i: 2 * g + i for i in range(n_in)},
        compiler_params=pltpu.CompilerParams(has_side_effects=EFFECT),
    )(*packs, *lands, after)
    flights, first = [], 3 * g
    for gi in range(g):
        flights.append((outs[2 * gi], outs[2 * gi + 1], outs[2 * g + gi], list(outs[first:first + counts[gi]]), outs[-1]))
        first += counts[gi]
    return flights


def _gather_wait(send_sems, recv_sems, packed, lands, after, name):
    n = len(lands)

    def body(packed_ref, *refs):
        s_sems, r_sems = refs[n], refs[n + 1]
        me = _mesh_pos()
        for k in range(4):
            cp = _whole_wait(packed_ref, s_sems.at[k], r_sems.at[k], me)
            cp.wait_send()
            cp.wait_recv()

    outs = pl.pallas_call(
        body, name=name,
        out_shape=(pltpu.HBM(packed.shape, BF16), *[pltpu.HBM(t.shape, BF16) for t in lands]),
        in_specs=(HBM_SPEC,) * (1 + n) + (SEM_SPEC, SEM_SPEC, ANY_SPEC),
        out_specs=(HBM_SPEC,) * (1 + n),
        input_output_aliases={i: i for i in range(1 + n)},
        compiler_params=pltpu.CompilerParams(has_side_effects=EFFECT),
    )(packed, *lands, send_sems, recv_sems, after)
    return outs[0], list(outs[1:])


def _gather_finish(packed, rows_list, lands, name):
    n = len(rows_list)
    offs = _offsets(rows_list)

    def body(packed_ref, *refs):
        land = refs[n:2 * n]
        send_sems, recv_sems, stage, stage_sem = refs[2 * n:]
        px, py, pc = _mesh_pos()
        me = 4 * px + 2 * py + pc
        sibling = (px, py, 1 - pc)
        load = pltpu.make_async_copy(packed_ref, stage, stage_sem)
        load.start()
        for j, (cx, cy) in enumerate([(1 - px, py), (px, 1 - py), (1 - px, 1 - py)]):
            block = 4 * cx + 2 * cy + pc
            for rows, land_ref in zip(rows_list, land):
                blk = land_ref.at[pl.ds(block * rows, rows), :]
                pltpu.make_async_remote_copy(src_ref=blk, dst_ref=blk, send_sem=send_sems.at[j],
                                             recv_sem=recv_sems.at[j], device_id=sibling,
                                             device_id_type=pl.DeviceIdType.MESH).start()
        load.wait()
        for off, rows, land_ref in zip(offs, rows_list, land):
            pltpu.make_async_copy(stage.at[pl.ds(off, rows), :], land_ref.at[pl.ds(me * rows, rows), :],
                                  stage_sem).start()
        for j in range(3):
            cp = _whole_wait(packed_ref, send_sems.at[j], recv_sems.at[j], sibling)
            cp.wait_recv()
            cp.wait_send()
        pltpu.make_async_copy(stage, packed_ref, stage_sem).wait()

    outs = pl.pallas_call(
        body, name=name,
        out_shape=tuple(jax.ShapeDtypeStruct(t.shape, BF16) for t in lands),
        in_specs=(HBM_SPEC,) * (1 + n), out_specs=(HBM_SPEC,) * n,
        input_output_aliases={1 + i: i for i in range(n)},
        scratch_shapes=[pltpu.SemaphoreType.DMA((3,)), pltpu.SemaphoreType.DMA((3,)),
                        pltpu.VMEM(packed.shape, BF16), pltpu.SemaphoreType.DMA],
    )(packed, *lands)
    return list(outs)


N_CHIPS = 4


def _pair_start(srcs, rows_list, after, name):
    n = len(rows_list)
    offs = _offsets(rows_list)
    land = lax.empty((N_CHIPS, sum(rows_list), D_MODEL), BF16)

    def body(*refs):
        src, land_ref = refs[:n], refs[n]
        send_sems, recv_sems = refs[n + 2], refs[n + 3]
        token = refs[-1]
        px, py, pc = _mesh_pos()
        for k in range(N_CHIPS):
            block = 2 * k + (1 - pc)
            for off, rows, src_ref in zip(offs, rows_list, src):
                pltpu.make_async_remote_copy(
                    src_ref=src_ref.at[pl.ds(block * rows, rows), :], dst_ref=land_ref.at[k, pl.ds(off, rows), :],
                    send_sem=send_sems.at[0], recv_sem=recv_sems.at[0],
                    device_id=(px, py, 1 - pc), device_id_type=pl.DeviceIdType.MESH).start()
        token[...] = jnp.zeros_like(token)

    outs = pl.pallas_call(
        body, name=name,
        out_shape=(pltpu.SemaphoreType.DMA((1,)), pltpu.SemaphoreType.DMA((1,)),
                   *[pltpu.HBM(t.shape, BF16) for t in srcs], pltpu.HBM(land.shape, BF16),
                   jax.ShapeDtypeStruct((8, LANE), F32)),
        in_specs=(HBM_SPEC,) * (n + 1) + (ANY_SPEC,),
        out_specs=(SEM_SPEC, SEM_SPEC) + (HBM_SPEC,) * (n + 1) + (pl.BlockSpec(memory_space=pltpu.VMEM),),
        input_output_aliases={i: 2 + i for i in range(n + 1)},
        compiler_params=pltpu.CompilerParams(has_side_effects=EFFECT),
    )(*[_hbm(t) for t in srcs], _hbm(land), after)
    return outs[0], outs[1], list(outs[2:2 + n]), outs[2 + n], outs[-1]


def _split_wait(send_sems, recv_sems, n_sems, srcs, land, after, name):
    n = len(srcs)

    def body(*refs):
        land_ref = refs[n]
        s_sems, r_sems = refs[n + 1], refs[n + 2]
        me = _mesh_pos()
        for k in range(n_sems):
            cp = _whole_wait(land_ref.at[0] if n_sems > 1 else land_ref, s_sems.at[k], r_sems.at[k], me)
            cp.wait_send()
            cp.wait_recv()

    outs = pl.pallas_call(
        body, name=name,
        out_shape=(*[pltpu.HBM(t.shape, t.dtype) for t in srcs], pltpu.HBM(land.shape, land.dtype)),
        in_specs=(HBM_SPEC,) * (n + 1) + (SEM_SPEC, SEM_SPEC, ANY_SPEC),
        out_specs=(HBM_SPEC,) * (n + 1),
        input_output_aliases={i: i for i in range(n + 1)},
        compiler_params=pltpu.CompilerParams(has_side_effects=EFFECT),
    )(*srcs, land, send_sems, recv_sems, after)
    return list(outs[:n]), outs[n]


def _spread_start(x, me_id, after, name):
    land = lax.dynamic_update_slice_in_dim(lax.empty((N_DEV,) + x.shape, x.dtype), x[None], me_id, axis=0)

    def body(x_ref, land_ref, after_ref, send_sems, recv_sems, x_thru, land_thru, token):
        px, py, pc = _mesh_pos()
        me = 4 * px + 2 * py + pc
        for k in range(1, N_DEV):
            qx = 1 - px if k & 4 else px
            qy = 1 - py if k & 2 else py
            qc = 1 - pc if k & 1 else pc
            pltpu.make_async_remote_copy(
                src_ref=x_ref, dst_ref=land_ref.at[me], send_sem=send_sems.at[k - 1], recv_sem=recv_sems.at[k - 1],
                device_id=(qx, qy, qc), device_id_type=pl.DeviceIdType.MESH).start()
        token[...] = jnp.zeros_like(token)

    outs = pl.pallas_call(
        body, name=name,
        out_shape=(pltpu.SemaphoreType.DMA((N_DEV - 1,)), pltpu.SemaphoreType.DMA((N_DEV - 1,)),
                   pltpu.HBM(x.shape, x.dtype), pltpu.HBM(land.shape, land.dtype), jax.ShapeDtypeStruct((8, LANE), F32)),
        in_specs=(HBM_SPEC, HBM_SPEC, ANY_SPEC),
        out_specs=(SEM_SPEC, SEM_SPEC, HBM_SPEC, HBM_SPEC, pl.BlockSpec(memory_space=pltpu.VMEM)),
        input_output_aliases={0: 2, 1: 3},
        compiler_params=pltpu.CompilerParams(has_side_effects=EFFECT),
    )(_hbm(x), _hbm(land), after)
    return outs[0], outs[1], [outs[2]], outs[3], outs[4]


def _pair_sum(srcs, rows_list, land, core, name):
    n = len(rows_list)
    offs = _offsets(rows_list)
    total = sum(rows_list)

    def body(core_ref, *refs):
        src, land_ref, out_ref = refs[:n], refs[n], refs[n + 1]
        for off, rows, src_ref in zip(offs, rows_list, src):
            out_ref[pl.ds(off, rows), :] = (src_ref[...].astype(F32)
                                            + land_ref[pl.ds(off, rows), :].astype(F32)).astype(BF16)

    slot = pl.BlockSpec((None, total, D_MODEL), lambda k, c: (k, 0, 0))
    grid_spec = pltpu.PrefetchScalarGridSpec(
        num_scalar_prefetch=1, grid=(N_CHIPS,),
        in_specs=[pl.BlockSpec((rows, D_MODEL), lambda k, c: (2 * k + c[0], 0)) for rows in rows_list] + [slot],
        out_specs=slot)
    return pl.pallas_call(
        body, name=name, grid_spec=grid_spec,
        out_shape=jax.ShapeDtypeStruct((N_CHIPS, total, D_MODEL), BF16),
        compiler_params=_params(("parallel",)),
    )(core, *srcs, land)


def _chip_exchange_start(sums, chip, after, name):
    own = lax.dynamic_index_in_dim(sums, chip, axis=0, keepdims=True)
    recv = lax.dynamic_update_slice_in_dim(lax.empty(sums.shape, BF16), own, chip, axis=0)

    def body(sums_ref, recv_ref, after_ref, send_sems, recv_sems, sums_thru, recv_thru, token):
        px, py, pc = _mesh_pos()
        for k in range(1, N_CHIPS):
            qx = 1 - px if k & 2 else px
            qy = 1 - py if k & 1 else py
            pltpu.make_async_remote_copy(
                src_ref=sums_ref.at[2 * qx + qy], dst_ref=recv_ref.at[2 * px + py],
                send_sem=send_sems.at[k - 1], recv_sem=recv_sems.at[k - 1],
                device_id=(qx, qy, pc), device_id_type=pl.DeviceIdType.MESH).start()
        token[...] = jnp.zeros_like(token)

    outs = pl.pallas_call(
        body, name=name,
        out_shape=(pltpu.SemaphoreType.DMA((N_CHIPS - 1,)), pltpu.SemaphoreType.DMA((N_CHIPS - 1,)),
                   pltpu.HBM(sums.shape, BF16), pltpu.HBM(recv.shape, BF16), jax.ShapeDtypeStruct((8, LANE), F32)),
        in_specs=(HBM_SPEC, HBM_SPEC, ANY_SPEC),
        out_specs=(SEM_SPEC, SEM_SPEC, HBM_SPEC, HBM_SPEC, pl.BlockSpec(memory_space=pltpu.VMEM)),
        input_output_aliases={0: 2, 1: 3},
        compiler_params=pltpu.CompilerParams(has_side_effects=EFFECT),
    )(_hbm(sums), _hbm(recv), after)
    return outs[0], outs[1], [outs[2]], outs[3], outs[4]


def _sum_slots_into(recv, buf, layer, row_off, name):
    slots, r, n = recv.shape
    tr = _row_tile(math.gcd(r, row_off) if row_off else r, 512)
    first = row_off // tr

    def body(in_ref, buf_ref, out_ref):
        acc = in_ref[0].astype(F32)
        for j in range(1, slots):
            acc = acc + in_ref[j].astype(F32)
        out_ref[...] = acc

    return pl.pallas_call(
        body, name=name, grid=(r // tr,), out_shape=jax.ShapeDtypeStruct(buf.shape, F32),
        in_specs=[pl.BlockSpec((slots, tr, n), lambda i: (0, i, 0)), ANY_SPEC],
        out_specs=pl.BlockSpec((None, tr, n), lambda i: (layer, first + i, 0)),
        input_output_aliases={1: 0},
        compiler_params=_params(("parallel",)),
    )(recv, buf)


def _sum_slots(recv, name, after=None):
    _, r, n = recv.shape
    tr = _row_tile(r, 512)

    def body(in_ref, *refs):
        acc = in_ref[0].astype(F32)
        for j in range(1, N_DEV):
            acc = acc + in_ref[j].astype(F32)
        refs[-1][...] = acc

    grid = (r // tr,)
    in_specs, out_spec = [pl.BlockSpec((N_DEV, tr, n), lambda i: (0, i, 0))], pl.BlockSpec((tr, n), lambda i: (i, 0))
    args = [recv]
    if after is not None:
        in_specs.append(ANY_SPEC)
        args.append(after)
    return pl.pallas_call(
        body, name=name, grid=grid,
        out_shape=jax.ShapeDtypeStruct((r, n), F32),
        in_specs=in_specs, out_specs=out_spec,
        compiler_params=_params(("parallel",)),
    )(*args)


def _row_tile(rows, target):
    if rows <= target:
        return rows
    best = None
    for t in range(16, target + 1, 16):
        if rows % t == 0:
            best = t
    assert best is not None, rows
    return best


_DIMS = {"nn": ((1,), (0,)), "nt": ((1,), (1,)), "tn": ((0,), (0,))}


def _mm(a, b, mode, name, out_dtype=F32, res=None, gate=None, gate_factor=1.0, tm=512, tn=1408, after=None):
    assert (res is None) == (gate is None)
    if mode == "tn":
        kdim, m = a.shape
    else:
        m, kdim = a.shape
    n = b.shape[0] if mode == "nt" else b.shape[1]
    tm, tn = _tile(m, tm), _tile(n, tn)
    a_spec = (pl.BlockSpec((kdim, tm), lambda i, j: (0, i)) if mode == "tn"
              else pl.BlockSpec((tm, kdim), lambda i, j: (i, 0)))
    b_spec = (pl.BlockSpec((tn, kdim), lambda i, j: (j, 0)) if mode == "nt"
              else pl.BlockSpec((kdim, tn), lambda i, j: (0, j)))
    o_spec = pl.BlockSpec((tm, tn), lambda i, j: (i, j))
    dims = (_DIMS[mode], ((), ()))
    has_res = res is not None

    def body(a_ref, b_ref, *refs):
        y = lax.dot_general(a_ref[...].astype(BF16), b_ref[...].astype(BF16), dims,
                            preferred_element_type=F32)
        if has_res:
            res_ref, gate_ref = refs[0], refs[1]
            y_ref, o_ref = refs[-2], refs[-1]
            y_ref[...] = y.astype(BF16)
            o_ref[...] = res_ref[...] + (gate_factor * gate_ref[...]) * y
        else:
            refs[-1][...] = y.astype(out_dtype)

    in_specs, args = [a_spec, b_spec], [a, b]
    if has_res:
        gate_spec, gate_arg = _vec_in(gate, tile=tn)
        in_specs += [o_spec, gate_spec]
        args += [res, gate_arg]
        out_shape = (jax.ShapeDtypeStruct((m, n), BF16), jax.ShapeDtypeStruct((m, n), F32))
        out_specs = (o_spec, o_spec)
    else:
        out_shape, out_specs = jax.ShapeDtypeStruct((m, n), out_dtype), o_spec
    if after is not None:
        in_specs.append(ANY_SPEC)
        args.append(after)
    return pl.pallas_call(
        body, name=name, grid=(m // tm, n // tn), out_shape=out_shape,
        in_specs=in_specs, out_specs=out_specs,
        compiler_params=_params(("parallel", "parallel")),
    )(*args)


def _vec_in(v, tile=None):
    if isinstance(v, tuple):
        table, row = v
        if tile is None:
            return pl.BlockSpec((None, 1, table.shape[-1]), lambda *idx: (row, 0, 0)), table
        return pl.BlockSpec((None, 1, tile), lambda i, j: (row, 0, j)), table
    if tile is None:
        return pl.BlockSpec((1, v.shape[-1]), lambda *idx: (0, 0)), v
    return pl.BlockSpec((1, tile), lambda i, j: (0, j)), v


def _vec_spec(width):
    return pl.BlockSpec((1, width), lambda i: (0, 0))


def _rm_bwd(dh, x, dres, gw, scale, name, below=None):
    s, d = x.shape
    ts = _tile(s, 512)
    factor = None if below is None else below[2]

    def body(dh_ref, x_ref, dres_ref, gw_ref, sc_ref, *refs):
        dx_ref, dsh_ref, dsc_ref, dgw_ref = refs[-6:-2] if below is not None else refs[-4:]

        @pl.when(pl.program_id(0) == 0)
        def _():
            dsh_ref[...] = jnp.zeros_like(dsh_ref)
            dsc_ref[...] = jnp.zeros_like(dsc_ref)
            dgw_ref[...] = jnp.zeros_like(dgw_ref)
            if below is not None:
                refs[-1][...] = jnp.zeros_like(refs[-1])

        xv, dhv, gwv = x_ref[...], dh_ref[...], gw_ref[...]
        r = lax.rsqrt(jnp.mean(xv * xv, axis=-1, keepdims=True) + EPS)
        xn = xv * r
        y = xn * gwv
        dsh_ref[...] += jnp.sum(dhv, axis=0, keepdims=True)
        dsc_ref[...] += jnp.sum(dhv * y, axis=0, keepdims=True)
        dy = dhv * (1 + sc_ref[...])
        dgw_ref[...] += jnp.sum(dy * xn, axis=0, keepdims=True)
        dxn = dy * gwv
        dx = dres_ref[...] + r * (dxn - xn * jnp.mean(dxn * xn, axis=-1, keepdims=True))
        dx_ref[...] = dx
        if below is not None:
            yb_ref, gb_ref, dyb_ref, dgb_ref = refs[0], refs[1], refs[-2], refs[-1]
            dyb_ref[...] = ((factor * gb_ref[...]) * dx).astype(BF16)
            dgb_ref[...] += jnp.sum((factor * dx) * yb_ref[...].astype(F32), axis=0, keepdims=True)

    row = pl.BlockSpec((ts, d), lambda i: (i, 0))
    vec = jax.ShapeDtypeStruct((1, d), F32)
    (gw_spec, gw), (sc_spec, scale) = _vec_in(gw), _vec_in(scale)
    in_specs, args = [row, row, row, gw_spec, sc_spec], [dh, x, dres, gw, scale]
    out_shape = [jax.ShapeDtypeStruct((s, d), F32), vec, vec, vec]
    out_specs = [row, _vec_spec(d), _vec_spec(d), _vec_spec(d)]
    if below is not None:
        gate_spec, gate_arg = _vec_in(below[1])
        in_specs += [row, gate_spec]
        args += [below[0], gate_arg]
        out_shape += [jax.ShapeDtypeStruct((s, d), BF16), vec]
        out_specs += [row, _vec_spec(d)]
    return pl.pallas_call(
        body, name=name, grid=(s // ts,), out_shape=tuple(out_shape),
        in_specs=in_specs, out_specs=tuple(out_specs),
        compiler_params=_params(("arbitrary",)),
    )(*args)


def _norm_mm(x, gw, shift, scale, w, name, tm=1024):
    s, d = x.shape
    n = w.shape[0]
    tm = _tile(s, tm)

    def body(x_ref, gw_ref, sh_ref, sc_ref, w_ref, h_ref, z_ref):
        xv = x_ref[...]
        r = lax.rsqrt(jnp.mean(xv * xv, axis=-1, keepdims=True) + EPS)
        hb = (((xv * r) * gw_ref[...]) * (1 + sc_ref[...]) + sh_ref[...]).astype(BF16)
        h_ref[...] = hb
        z_ref[...] = lax.dot_general(hb, w_ref[...], (((1,), (1,)), ((), ())), preferred_element_type=F32)

    row = pl.BlockSpec((tm, d), lambda i: (i, 0))
    return pl.pallas_call(
        body, name=name, grid=(s // tm,),
        out_shape=(jax.ShapeDtypeStruct((s, d), BF16), jax.ShapeDtypeStruct((s, n), F32)),
        in_specs=[row, _vec_in(gw)[0], _vec_in(shift)[0], _vec_in(scale)[0], pl.BlockSpec((n, d), lambda i: (0, 0))],
        out_specs=(row, pl.BlockSpec((tm, n), lambda i: (i, 0))),
        compiler_params=_params(("parallel",)),
    )(x, _vec_in(gw)[1], _vec_in(shift)[1], _vec_in(scale)[1], w)


FFN_TM, FFN_TF = 2048, 256


def _ffn_up(x, gw, shift, scale, wg, wu, name, after=None):
    s, d = x.shape
    f = wg.shape[0]
    tm, tf = _tile(s, FFN_TM), _tile(f, FFN_TF)
    nt = (((1,), (1,)), ((), ()))

    def body(x_ref, gw_ref, sh_ref, sc_ref, wg_ref, wu_ref, *refs):
        h_ref, a_ref, b_ref, t_ref = refs[-4:]

        @pl.when(pl.program_id(1) == 0)
        def _():
            xv = x_ref[...]
            r = lax.rsqrt(jnp.mean(xv * xv, axis=-1, keepdims=True) + EPS)
            h_ref[...] = (((xv * r) * gw_ref[...]) * (1 + sc_ref[...]) + sh_ref[...]).astype(BF16)

        hb = h_ref[...]
        av = lax.dot_general(hb, wg_ref[...], nt, preferred_element_type=F32)
        bv = lax.dot_general(hb, wu_ref[...], nt, preferred_element_type=F32)
        a_ref[...] = av.astype(BF16)
        b_ref[...] = bv.astype(BF16)
        t_ref[...] = ((av * jax.nn.sigmoid(av)) * bv).astype(BF16)

    row = pl.BlockSpec((tm, d), lambda i, j: (i, 0))
    wblk = pl.BlockSpec((tf, d), lambda i, j: (j, 0))
    blk = pl.BlockSpec((tm, tf), lambda i, j: (i, j))
    wide = jax.ShapeDtypeStruct((s, f), BF16)
    vec_specs, vec_args = zip(*[_vec_in(v) for v in (gw, shift, scale)])
    in_specs, args = [row, *vec_specs, wblk, wblk], [x, *vec_args, wg, wu]
    if after is not None:
        in_specs.append(ANY_SPEC)
        args.append(after)
    return pl.pallas_call(
        body, name=name, grid=(s // tm, f // tf),
        out_shape=(jax.ShapeDtypeStruct((s, d), BF16), wide, wide, wide),
        in_specs=in_specs, out_specs=(row, blk, blk, blk),
        compiler_params=_params(("parallel", "arbitrary")),
    )(*args)


def _ffn_bwd_cols(dy, h, a, b, t, wd, name, after=None):
    s, d = dy.shape
    f = wd.shape[0]
    tf = _tile(f, FFN_TF)
    nt = (((1,), (1,)), ((), ()))
    tn = (((0,), (0,)), ((), ()))

    def body(dy_ref, h_ref, a_ref, b_ref, t_ref, wd_ref, *refs):
        da_ref, db_ref, gd_ref, gg_ref, gu_ref = refs[-5:]
        dyb, hb = dy_ref[...], h_ref[...]
        dtv = lax.dot_general(dyb, wd_ref[...], nt, preferred_element_type=F32)
        av, bv = a_ref[...].astype(F32), b_ref[...].astype(F32)
        sg = jax.nn.sigmoid(av)
        dbv = (dtv * (av * sg)).astype(BF16)
        dav = ((dtv * bv) * (sg * (1 + av * (1 - sg)))).astype(BF16)
        da_ref[...] = dav
        db_ref[...] = dbv
        gd_ref[...] = lax.dot_general(t_ref[...], dyb, tn, preferred_element_type=F32).astype(BF16)
        gg_ref[...] = lax.dot_general(dav, hb, tn, preferred_element_type=F32).astype(BF16)
        gu_ref[...] = lax.dot_general(dbv, hb, tn, preferred_element_type=F32).astype(BF16)

    whole = pl.BlockSpec((s, d), lambda j: (0, 0))
    col = pl.BlockSpec((s, tf), lambda j: (0, j))
    wblk = pl.BlockSpec((tf, d), lambda j: (j, 0))
    wide, wgrad = jax.ShapeDtypeStruct((s, f), BF16), jax.ShapeDtypeStruct((f, d), BF16)
    in_specs, args = [whole, whole, col, col, col, wblk], [dy, h, a, b, t, wd]
    if after is not None:
        in_specs.append(ANY_SPEC)
        args.append(after)
    return pl.pallas_call(
        body, name=name, grid=(f // tf,), out_shape=(wide, wide, wgrad, wgrad, wgrad),
        in_specs=in_specs, out_specs=(col, col, wblk, wblk, wblk),
        compiler_params=_params(("parallel",)),
    )(*args)


def _mm_pair(a1, b1, a2, b2, name, tm=1024, tn=512, after=None):
    m, kdim = a1.shape
    n = b1.shape[1]
    tm, tn = _tile(m, tm), _tile(n, tn)

    def body(a1_ref, b1_ref, a2_ref, b2_ref, *refs):
        refs[-1][...] = (jnp.dot(a1_ref[...], b1_ref[...], preferred_element_type=F32)
                         + jnp.dot(a2_ref[...], b2_ref[...], preferred_element_type=F32))

    a_spec = pl.BlockSpec((tm, kdim), lambda i, j: (i, 0))
    b_spec = pl.BlockSpec((kdim, tn), lambda i, j: (0, j))
    in_specs, args = [a_spec, b_spec, a_spec, b_spec], [a1, b1, a2, b2]
    if after is not None:
        in_specs.append(ANY_SPEC)
        args.append(after)
    return pl.pallas_call(
        body, name=name, grid=(m // tm, n // tn), out_shape=jax.ShapeDtypeStruct((m, n), F32),
        in_specs=in_specs, out_specs=pl.BlockSpec((tm, tn), lambda i, j: (i, j)),
        compiler_params=_params(("parallel", "parallel")),
    )(*args)


def _pool_counts(s):
    return (lax.broadcasted_iota(jnp.int32, (s, POOL_GC), 0))


def _pool_fwd(z, pool_w, pool_scale, name):
    s = z.shape[0]

    def body(u_ref, w_ref, sc_ref, y_ref, diff_ref):
        t = lax.broadcasted_iota(jnp.int32, (s, POOL_GC), 0)
        for g, win in enumerate(POOL_WINDOWS):
            cols = slice(g * POOL_GC, (g + 1) * POOL_GC)
            u = u_ref[:, cols]
            acc, step = u, 1
            while step < win:
                acc = acc + jnp.where(t >= step, pltpu.roll(acc, step, 0), 0.0)
                step *= 2
            cnt = jnp.minimum(t + 1, win).astype(F32)
            diff = acc / cnt - u
            diff_ref[:, cols] = diff
            ypre = jnp.dot(diff.astype(BF16), w_ref[g].astype(BF16), preferred_element_type=F32)
            y_ref[:, cols] = (ypre * sc_ref[:, cols]).astype(BF16)

    return pl.pallas_call(
        body, name=name, grid=(1,),
        out_shape=(jax.ShapeDtypeStruct((s, POOL_WIDTH), BF16), jax.ShapeDtypeStruct((s, POOL_WIDTH), F32)),
        in_specs=[pl.BlockSpec((s, POOL_WIDTH), lambda i: (0, 0)),
                  pl.BlockSpec(pool_w.shape, lambda i: (0, 0, 0)),
                  pl.BlockSpec((1, POOL_WIDTH), lambda i: (0, 0))],
        out_specs=(pl.BlockSpec((s, POOL_WIDTH), lambda i: (0, 0)),
                   pl.BlockSpec((s, POOL_WIDTH), lambda i: (0, 0))),
        compiler_params=_params(("arbitrary",)),
    )(z, pool_w, pool_scale)


def _pool_bwd(dycat, diff, pool_w, pool_scale, name):
    s = diff.shape[0]

    def body(dy_ref, diff_ref, w_ref, sc_ref, du_ref, dw_ref, dsc_ref):
        t = lax.broadcasted_iota(jnp.int32, (s, POOL_GC), 0)
        for g, win in enumerate(POOL_WINDOWS):
            cols = slice(g * POOL_GC, (g + 1) * POOL_GC)
            dy, dfb, wb = dy_ref[:, cols], diff_ref[:, cols].astype(BF16), w_ref[g].astype(BF16)
            ypre = jnp.dot(dfb, wb, preferred_element_type=F32)
            dsc_ref[:, cols] = jnp.sum(dy * ypre, axis=0, keepdims=True)
            dypre = (dy * sc_ref[:, cols]).astype(BF16)
            ddiff = lax.dot_general(dypre, wb, (((1,), (1,)), ((), ())), preferred_element_type=F32)
            dw_ref[g] = lax.dot_general(dfb, dypre, (((0,), (0,)), ((), ())), preferred_element_type=F32)
            cnt = jnp.minimum(t + 1, win).astype(F32)
            acc, step = ddiff / cnt, 1
            while step < win:
                acc = acc + jnp.where(t < s - step, pltpu.roll(acc, s - step, 0), 0.0)
                step *= 2
            du_ref[:, cols] = acc - ddiff

    full = pl.BlockSpec((s, POOL_WIDTH), lambda i: (0, 0))
    return pl.pallas_call(
        body, name=name, grid=(1,),
        out_shape=(jax.ShapeDtypeStruct((s, POOL_WIDTH), F32),
                   jax.ShapeDtypeStruct(pool_w.shape, F32),
                   jax.ShapeDtypeStruct((1, POOL_WIDTH), F32)),
        in_specs=[full, full, pl.BlockSpec(pool_w.shape, lambda i: (0, 0, 0)),
                  pl.BlockSpec((1, POOL_WIDTH), lambda i: (0, 0))],
        out_specs=(full, pl.BlockSpec(pool_w.shape, lambda i: (0, 0, 0)),
                   pl.BlockSpec((1, POOL_WIDTH), lambda i: (0, 0))),
        compiler_params=_params(("arbitrary",)),
    )(dycat, diff, pool_w, pool_scale)


def _rope_tables(positions, name):
    s = positions.shape[0]
    ts = _tile(s, 512)
    freq = 1.0 / (ROPE_THETA ** (np.arange(0, QK_ROPE, 2, dtype=np.float32) / QK_ROPE))
    table = np.zeros((1, LANE), np.float32)
    table[0, :QK_ROPE // 2] = freq
    table[0, QK_ROPE // 2:QK_ROPE] = freq

    def body(pos_ref, f_ref, cos_ref, sin_ref):
        ang = pos_ref[...].astype(F32) * f_ref[...]
        cos_ref[...] = jnp.cos(ang)
        sin_ref[...] = jnp.sin(ang)

    out = jax.ShapeDtypeStruct((s, LANE), F32)
    blk = pl.BlockSpec((ts, LANE), lambda i: (i, 0))
    return pl.pallas_call(
        body, name=name, grid=(s // ts,), out_shape=(out, out),
        in_specs=[pl.BlockSpec((ts, 1), lambda i: (i, 0)), _vec_spec(LANE)], out_specs=(blk, blk),
        compiler_params=_params(("parallel",)),
    )(positions, jnp.asarray(table))


def _lane_mod64_low(shape):
    return (lax.broadcasted_iota(jnp.int32, shape, 1) % QK_ROPE) < (QK_ROPE // 2)


def _rope(x, cos, sin):
    rot = jnp.where(_lane_mod64_low(x.shape), -pltpu.roll(x, LANE - 32, 1), pltpu.roll(x, 32, 1))
    return x * cos + rot * sin


def _rope_t(dy, cos, sin):
    w = dy * sin
    rot_t = jnp.where(_lane_mod64_low(dy.shape), pltpu.roll(w, LANE - 32, 1), -pltpu.roll(w, 32, 1))
    return dy * cos + rot_t


def _plain_rms(x, g):
    r = lax.rsqrt(jnp.mean(x * x, axis=-1, keepdims=True) + EPS)
    return (x * r) * g, x * r, r


O_Q, O_KV, O_KR = POOL_WIDTH, POOL_WIDTH + Q_LORA, POOL_WIDTH + Q_LORA + KV_LORA


def _qkv_fwd(z, qn, kvn, wq, wkv, cos, sin, name):
    s = z.shape[0]
    ts = _tile(s, 512)

    def body(z_ref, qn_ref, kvn_ref, wq_ref, wkv_ref, cos_ref, sin_ref, q_ref, k_ref, v_ref, cqn_ref, ckvn_ref):
        cosv, sinv = cos_ref[...], sin_ref[...]
        cqn = _plain_rms(z_ref[:, O_Q:O_KV], qn_ref[...])[0].astype(BF16)
        ckvn = _plain_rms(z_ref[:, O_KV:O_KR], kvn_ref[...])[0].astype(BF16)
        cqn_ref[...] = cqn
        ckvn_ref[...] = ckvn
        nt = (((1,), (1,)), ((), ()))
        q = lax.dot_general(cqn, wq_ref[...], nt, preferred_element_type=F32)
        kv = lax.dot_general(ckvn, wkv_ref[...], nt, preferred_element_type=F32)
        kr = _rope(z_ref[:, O_KR:IN_PAD], cosv, sinv).astype(BF16)
        for h in range(N_HEADS):
            o = h * HEAD_PAD
            q_ref[:, o:o + QK_NOPE] = q[:, o:o + QK_NOPE].astype(BF16)
            q_ref[:, o + QK_NOPE:o + HEAD_PAD] = _rope(q[:, o + QK_NOPE:o + HEAD_PAD], cosv, sinv).astype(BF16)
            k_ref[:, o:o + QK_NOPE] = kv[:, o:o + QK_NOPE].astype(BF16)
            k_ref[:, o + QK_NOPE:o + HEAD_PAD] = kr
            v_ref[:, h * V_HEAD:(h + 1) * V_HEAD] = kv[:, o + QK_NOPE:o + HEAD_PAD].astype(BF16)

    def row(w):
        return pl.BlockSpec((ts, w), lambda i: (i, 0))

    def whole(arr):
        return pl.BlockSpec(arr.shape, lambda i: (0, 0))

    hp = N_HEADS * HEAD_PAD
    return pl.pallas_call(
        body, name=name, grid=(s // ts,),
        out_shape=(jax.ShapeDtypeStruct((s, hp), BF16), jax.ShapeDtypeStruct((s, hp), BF16),
                   jax.ShapeDtypeStruct((s, N_HEADS * V_HEAD), BF16),
                   jax.ShapeDtypeStruct((s, Q_LORA), BF16), jax.ShapeDtypeStruct((s, KV_LORA), BF16)),
        in_specs=[row(IN_PAD), whole(qn), whole(kvn), whole(wq), whole(wkv), row(LANE), row(LANE)],
        out_specs=(row(hp), row(hp), row(N_HEADS * V_HEAD), row(Q_LORA), row(KV_LORA)),
        compiler_params=_params(("parallel",)),
    )(z, qn, kvn, wq, wkv, cos, sin)


def _qkv_bwd(dq, dk, dv, du, z, qn, kvn, wq, wkv, cos, sin, name):
    s = z.shape[0]
    ts = _tile(s, 512)

    def norm_bwd(x, g, dy):
        _, xn, r = _plain_rms(x, g)
        dxn = dy * g
        return r * (dxn - xn * jnp.mean(dxn * xn, axis=-1, keepdims=True)), jnp.sum(dy * xn, axis=0, keepdims=True)

    def body(dq_ref, dk_ref, dv_ref, du_ref, z_ref, qn_ref, kvn_ref, wq_ref, wkv_ref, cos_ref, sin_ref,
             dz_ref, dqb_ref, dkvb_ref, dqn_ref, dkvn_ref):
        @pl.when(pl.program_id(0) == 0)
        def _():
            dqn_ref[...] = jnp.zeros_like(dqn_ref)
            dkvn_ref[...] = jnp.zeros_like(dkvn_ref)

        cosv, sinv = cos_ref[...], sin_ref[...]
        dkr = jnp.zeros((ts, LANE), F32)
        for h in range(N_HEADS):
            o = h * HEAD_PAD
            dqb_ref[:, o:o + QK_NOPE] = dq_ref[:, o:o + QK_NOPE].astype(BF16)
            dqb_ref[:, o + QK_NOPE:o + HEAD_PAD] = _rope_t(dq_ref[:, o + QK_NOPE:o + HEAD_PAD], cosv, sinv).astype(BF16)
            dkvb_ref[:, o:o + QK_NOPE] = dk_ref[:, o:o + QK_NOPE].astype(BF16)
            dkvb_ref[:, o + QK_NOPE:o + HEAD_PAD] = dv_ref[:, h * V_HEAD:(h + 1) * V_HEAD].astype(BF16)
            dkr = dkr + dk_ref[:, o + QK_NOPE:o + HEAD_PAD]
        dcqn = jnp.dot(dqb_ref[...], wq_ref[...], preferred_element_type=F32)
        dckvn = jnp.dot(dkvb_ref[...], wkv_ref[...], preferred_element_type=F32)
        dcq, dqn = norm_bwd(z_ref[:, O_Q:O_KV], qn_ref[...], dcqn)
        dckv, dkvn = norm_bwd(z_ref[:, O_KV:O_KR], kvn_ref[...], dckvn)
        dqn_ref[...] += dqn
        dkvn_ref[...] += dkvn
        dz_ref[:, 0:O_Q] = du_ref[...].astype(BF16)
        dz_ref[:, O_Q:O_KV] = dcq.astype(BF16)
        dz_ref[:, O_KV:O_KR] = dckv.astype(BF16)
        dz_ref[:, O_KR:IN_PAD] = _rope_t(dkr, cosv, sinv).astype(BF16)

    def row(w):
        return pl.BlockSpec((ts, w), lambda i: (i, 0))

    def whole(arr):
        return pl.BlockSpec(arr.shape, lambda i: (0, 0))

    hp = N_HEADS * HEAD_PAD
    return pl.pallas_call(
        body, name=name, grid=(s // ts,),
        out_shape=(jax.ShapeDtypeStruct((s, IN_PAD), BF16), jax.ShapeDtypeStruct((s, hp), BF16),
                   jax.ShapeDtypeStruct((s, hp), BF16),
                   jax.ShapeDtypeStruct((1, Q_LORA), F32), jax.ShapeDtypeStruct((1, KV_LORA), F32)),
        in_specs=[row(hp), row(hp), row(N_HEADS * V_HEAD), row(POOL_WIDTH), row(IN_PAD),
                  whole(qn), whole(kvn), whole(wq), whole(wkv), row(LANE), row(LANE)],
        out_specs=(row(IN_PAD), row(hp), row(hp), whole(qn), whole(kvn)),
        compiler_params=_params(("arbitrary",)),
    )(dq, dk, dv, du, z, qn, kvn, wq, wkv, cos, sin)


def _causal_scores(q, k, i, tq, klen):
    sc = lax.dot_general(q, k, (((1,), (1,)), ((), ())), preferred_element_type=F32) * SOFTMAX_SCALE
    qpos = i * tq + lax.broadcasted_iota(jnp.int32, (tq, klen), 0)
    kpos = lax.broadcasted_iota(jnp.int32, (tq, klen), 1)
    return jnp.where(qpos >= kpos, sc, -jnp.inf)


ATTN_TQ = 512
ATTN_SEGMENTS = 4


def _by_key_prefix(i, nq, tq, compute):
    nseg = min(ATTN_SEGMENTS, nq)
    per = nq // nseg
    for r in range(nseg):
        pl.when(i // per == r)(lambda r=r: compute((r + 1) * per * tq))


def _attn_fwd(q, k, v, name):
    s = q.shape[0]
    tq = _tile(s, ATTN_TQ)
    nq = s // tq

    def body(q_ref, k_ref, v_ref, o_ref, lse_ref):
        i = pl.program_id(1)

        def compute(klen):
            sc = _causal_scores(q_ref[...], k_ref[0:klen, :], i, tq, klen)
            mx = jnp.max(sc, axis=-1, keepdims=True)
            p = jnp.exp(sc - mx)
            den = jnp.sum(p, axis=-1, keepdims=True)
            o_ref[...] = jnp.dot((p / den).astype(BF16), v_ref[0:klen, :], preferred_element_type=F32)
            lse_ref[...] = mx + jnp.log(den)

        _by_key_prefix(i, nq, tq, compute)

    return pl.pallas_call(
        body, name=name, grid=(N_HEADS, s // tq),
        out_shape=(jax.ShapeDtypeStruct((s, N_HEADS * V_HEAD), F32), jax.ShapeDtypeStruct((N_HEADS, s, 1), F32)),
        in_specs=[pl.BlockSpec((tq, HEAD_PAD), lambda h, i: (i, h)),
                  pl.BlockSpec((s, HEAD_PAD), lambda h, i: (0, h)),
                  pl.BlockSpec((s, V_HEAD), lambda h, i: (0, h))],
        out_specs=(pl.BlockSpec((tq, V_HEAD), lambda h, i: (i, h)),
                   pl.BlockSpec((None, tq, 1), lambda h, i: (h, i, 0))),
        compiler_params=_params(("parallel", "parallel")),
    )(q, k, v)


def _attn_bwd(q, k, v, lse, dycat, name):
    s = q.shape[0]
    tq = _tile(s, ATTN_TQ)
    nq = s // tq
    tn_dims = (((0,), (0,)), ((), ()))

    def body(q_ref, k_ref, v_ref, lse_ref, do_ref, dq_ref, dk_ref, dv_ref):
        i = pl.program_id(1)

        @pl.when(i == 0)
        def _():
            dk_ref[...] = jnp.zeros_like(dk_ref)
            dv_ref[...] = jnp.zeros_like(dv_ref)

        def compute(klen):
            qv, kv_, dob = q_ref[...], k_ref[0:klen, :], do_ref[...].astype(BF16)
            sc = _causal_scores(qv, kv_, i, tq, klen)
            p = jnp.exp(sc - lse_ref[...])
            dp = lax.dot_general(dob, v_ref[0:klen, :], (((1,), (1,)), ((), ())), preferred_element_type=F32)
            ds = (p * (dp - jnp.sum(dp * p, axis=-1, keepdims=True)) * SOFTMAX_SCALE).astype(BF16)
            dq_ref[...] = jnp.dot(ds, kv_, preferred_element_type=F32)
            dk_ref[0:klen, :] += lax.dot_general(ds, qv, tn_dims, preferred_element_type=F32)
            dv_ref[0:klen, :] += lax.dot_general(p.astype(BF16), dob, tn_dims, preferred_element_type=F32)

        _by_key_prefix(i, nq, tq, compute)

    n_pool_blocks = POOL_WIDTH // V_HEAD
    return pl.pallas_call(
        body, name=name, grid=(N_HEADS, s // tq),
        out_shape=(jax.ShapeDtypeStruct((s, N_HEADS * HEAD_PAD), F32),
                   jax.ShapeDtypeStruct((s, N_HEADS * HEAD_PAD), F32),
                   jax.ShapeDtypeStruct((s, N_HEADS * V_HEAD), F32)),
        in_specs=[pl.BlockSpec((tq, HEAD_PAD), lambda h, i: (i, h)),
                  pl.BlockSpec((s, HEAD_PAD), lambda h, i: (0, h)),
                  pl.BlockSpec((s, V_HEAD), lambda h, i: (0, h)),
                  pl.BlockSpec((None, tq, 1), lambda h, i: (h, i, 0)),
                  pl.BlockSpec((tq, V_HEAD), lambda h, i: (i, n_pool_blocks + h))],
        out_specs=(pl.BlockSpec((tq, HEAD_PAD), lambda h, i: (i, h)),
                   pl.BlockSpec((s, HEAD_PAD), lambda h, i: (0, h)),
                   pl.BlockSpec((s, V_HEAD), lambda h, i: (0, h))),
        compiler_params=_params(("parallel", "arbitrary")),
    )(q, k, v, lse, dycat)


def _loss_head(x, gw, target, below, name):
    s, d = x.shape
    ts = _tile(s, 512)
    factor = below[2]

    def body(x_ref, gw_ref, tgt_ref, yb_ref, gb_ref, loss_ref, dx_ref, dgw_ref, dyb_ref, dgb_ref):
        @pl.when(pl.program_id(0) == 0)
        def _():
            loss_ref[...] = jnp.zeros_like(loss_ref)
            dgw_ref[...] = jnp.zeros_like(dgw_ref)
            dgb_ref[...] = jnp.zeros_like(dgb_ref)

        xv, gwv = x_ref[...], gw_ref[...]
        r = lax.rsqrt(jnp.mean(xv * xv, axis=-1, keepdims=True) + EPS)
        xn = xv * r
        err = xn * gwv - tgt_ref[...]
        loss_ref[...] += 0.5 * jnp.sum(jnp.mean(err * err, axis=-1, keepdims=True))
        dy = err / d
        dgw_ref[...] += jnp.sum(dy * xn, axis=0, keepdims=True)
        dxn = dy * gwv
        dx = r * (dxn - xn * jnp.mean(dxn * xn, axis=-1, keepdims=True))
        dx_ref[...] = dx
        dyb_ref[...] = ((factor * gb_ref[...]) * dx).astype(BF16)
        dgb_ref[...] += jnp.sum((factor * dx) * yb_ref[...].astype(F32), axis=0, keepdims=True)

    row = pl.BlockSpec((ts, d), lambda i: (i, 0))
    gate_spec, gate_arg = _vec_in(below[1])
    vec = jax.ShapeDtypeStruct((1, d), F32)
    return pl.pallas_call(
        body, name=name, grid=(s // ts,),
        out_shape=(jax.ShapeDtypeStruct((8, LANE), F32), jax.ShapeDtypeStruct((s, d), F32), vec,
                   jax.ShapeDtypeStruct((s, d), BF16), vec),
        in_specs=[row, _vec_spec(d), row, row, gate_spec],
        out_specs=(pl.BlockSpec((8, LANE), lambda i: (0, 0)), row, _vec_spec(d), row, _vec_spec(d)),
        compiler_params=_params(("arbitrary",)),
    )(x, gw, target, below[0], gate_arg)


def _ada_mod(c_all, ada_w, ada_b, name):
    nl, d, cols = ada_w.shape

    def body(c_ref, w_ref, b_ref, o_ref):
        cv = c_ref[...]
        act = (cv * jax.nn.sigmoid(cv)).astype(BF16)
        o_ref[...] = jnp.dot(act, w_ref[...].astype(BF16), preferred_element_type=F32) + b_ref[...]

    return pl.pallas_call(
        body, name=name, grid=(nl,), out_shape=jax.ShapeDtypeStruct((nl, N_DEV, cols), F32),
        in_specs=[pl.BlockSpec((N_DEV, d), lambda l: (0, 0)),
                  pl.BlockSpec((None, d, cols), lambda l: (l, 0, 0)),
                  pl.BlockSpec((None, 1, cols), lambda l: (l, 0, 0))],
        out_specs=pl.BlockSpec((None, N_DEV, cols), lambda l: (l, 0, 0)),
        compiler_params=_params(("parallel",)),
    )(c_all, ada_w, ada_b)


def _ada_grad(c_pad, dmod_pad, name):
    nl, kpad, cols = dmod_pad.shape
    d = c_pad.shape[1]

    def body(c_ref, dm_ref, o_ref):
        cv = c_ref[...]
        act = (cv * jax.nn.sigmoid(cv)).astype(BF16)
        o_ref[...] = lax.dot_general(act, dm_ref[...].astype(BF16), (((0,), (0,)), ((), ())),
                                     preferred_element_type=F32)

    return pl.pallas_call(
        body, name=name, grid=(nl,), out_shape=jax.ShapeDtypeStruct((nl, d, cols), F32),
        in_specs=[pl.BlockSpec((kpad, d), lambda l: (0, 0)),
                  pl.BlockSpec((None, kpad, cols), lambda l: (l, 0, 0))],
        out_specs=pl.BlockSpec((None, d, cols), lambda l: (l, 0, 0)),
        compiler_params=_params(("parallel",)),
    )(c_pad, dmod_pad)


def _adamw_math(w, g, m, v):
    nm = ADAM_B1 * m + (1.0 - ADAM_B1) * g
    nv = ADAM_B2 * v + (1.0 - ADAM_B2) * (g * g)
    m_hat = nm / (1.0 - ADAM_B1 ** ADAM_STEP)
    v_hat = nv / (1.0 - ADAM_B2 ** ADAM_STEP)
    return -ADAM_LR * (m_hat / (jnp.sqrt(v_hat) + ADAM_EPS) + ADAM_WD * w), nm, nv


def _adamw_rows(w3, gbuf, row_off, m3, v3, name):
    nl, r, d = w3.shape
    tr = _row_tile(math.gcd(r, row_off) if row_off else r, 352)
    first = row_off // tr

    def body(w_ref, g_ref, m_ref, v_ref, go_ref, d_ref, nm_ref, nv_ref):
        gv = g_ref[...]
        go_ref[...] = gv
        d_ref[...], nm_ref[...], nv_ref[...] = _adamw_math(w_ref[...], gv, m_ref[...], v_ref[...])

    blk = pl.BlockSpec((None, tr, d), lambda l, i: (l, i, 0))
    gblk = pl.BlockSpec((None, tr, d), lambda l, i: (l, first + i, 0))
    out = jax.ShapeDtypeStruct((nl, r, d), F32)
    return pl.pallas_call(
        body, name=name, grid=(nl, r // tr), out_shape=(out, out, out, out),
        in_specs=[blk, gblk, blk, blk], out_specs=(blk, blk, blk, blk),
        compiler_params=_params(("parallel", "parallel")),
    )(w3, gbuf, m3, v3)


def _adamw(w, g, m, v, name):
    rows, cols = w.shape
    tr = _row_tile(rows, 512)

    def body(w_ref, g_ref, m_ref, v_ref, d_ref, nm_ref, nv_ref):
        d_ref[...], nm_ref[...], nv_ref[...] = _adamw_math(w_ref[...], g_ref[...], m_ref[...], v_ref[...])

    blk = pl.BlockSpec((tr, cols), lambda i: (i, 0))
    out = jax.ShapeDtypeStruct((rows, cols), F32)
    return pl.pallas_call(
        body, name=name, grid=(rows // tr,), out_shape=(out, out, out),
        in_specs=[blk, blk, blk, blk], out_specs=(blk, blk, blk),
        compiler_params=_params(("parallel",)),
    )(w, g, m, v)


def _adamw_nd(w, g, m, v, name):
    shape = w.shape
    flat = (lambda t: t.reshape(1, -1)) if w.ndim == 1 else (lambda t: t.reshape(-1, shape[-1]))
    return tuple(t.reshape(shape) for t in _adamw(flat(w), flat(g), flat(m), flat(v), name))


def _pad_rows(t, rows):
    return jnp.pad(t, ((0, rows - t.shape[0]), (0, 0)))


def _pack_shard_layer(l, wts):
    def tr(name):
        return wts[name][l].astype(BF16).T

    parts = [tr("ffn1_w_gate"), tr("ffn1_w_up"), wts["ffn1_w_down"][l].astype(BF16),
             tr("ffn2_w_gate"), tr("ffn2_w_up"), wts["ffn2_w_down"][l].astype(BF16),
             wts["w_out"][l].astype(BF16),
             tr("w_kv_b").reshape(KV_SH_ROWS, D_MODEL),
             _pad_rows(tr("w_in"), 160),
             _pad_rows(tr("w_q_b").reshape(Q_SH_ROWS, D_MODEL), Q_PAD_ROWS)]
    return jnp.concatenate(parts, axis=0)


def _mixer_weights(w_out, small):
    w = {"out": w_out}
    small = small.reshape(N_DEV, SMALL_ROWS, D_MODEL)
    o_in, o_q = OFF_IN - OFF_KV, OFF_Q - OFF_KV
    w["kv"] = small[:, :KV_SH_ROWS].reshape(N_HEADS * HEAD_PAD, KV_LORA)
    w["in"] = _pad_rows(small[:, o_in:o_in + IN_SH].reshape(IN_COLS, D_MODEL), IN_PAD)
    wq = small[:, o_q:o_q + Q_SH_ROWS].reshape(N_HEADS, QK_HEAD, Q_LORA)
    w["q"] = jnp.pad(wq, ((0, 0), (0, HEAD_PAD - QK_HEAD), (0, 0))).reshape(N_HEADS * HEAD_PAD, Q_LORA)
    return w


def _grad_sources_tail(gr):
    gq = gr["q"].reshape(N_HEADS, HEAD_PAD, Q_LORA)[:, :QK_HEAD].reshape(N_DEV, Q_SH_ROWS, D_MODEL)
    small = jnp.concatenate([
        gr["kv"].reshape(N_DEV, KV_SH_ROWS, D_MODEL),
        jnp.pad(gr["in"][:IN_COLS].reshape(N_DEV, IN_SH, D_MODEL), ((0, 0), (0, 160 - IN_SH), (0, 0))),
        jnp.pad(gq, ((0, 0), (0, Q_PAD_ROWS - Q_SH_ROWS), (0, 0)))], axis=1)
    return [gr["out"], small.reshape(N_DEV * SMALL_ROWS, D_MODEL)]


def _pack_bf16_pairs(t):
    rows, d = t.shape
    return lax.bitcast_convert_type(t.astype(BF16).reshape(rows // 2, 2, d).transpose(0, 2, 1), F32)


def _unpack_bf16_pairs(p):
    pairs = jnp.swapaxes(lax.bitcast_convert_type(p, BF16), -1, -2)
    return pairs.reshape(p.shape[:-2] + (2 * p.shape[-2], p.shape[-1]))


def _small_layout(nl):
    names = [("dmod", nl * N_MOD), ("ffn1_norm", nl), ("mix_norm", nl), ("ffn2_norm", nl), ("q_a_norm", nl),
             ("kv_a_norm", nl), ("pool_scale", nl), ("final_norm", 1), ("loss", 1),
             ("pool_w", nl * 4 * POOL_GC * POOL_GC // D_MODEL // 2)]
    off, table = 0, {}
    for name, n in names:
        table[name] = (off, n)
        off += -(-n // 8) * 8
    return table, off


def _to_rows(t, width=D_MODEL):
    n, w = t.shape
    return jnp.pad(t, ((0, -(-n // 8) * 8 - n), (0, width - w)))


def kernel(x, c, positions, ada_w, ada_b, ffn1_norm, ffn1_w_gate, ffn1_w_up, ffn1_w_down, mix_norm, w_in, pool_w, pool_scale, q_a_norm, w_q_b, kv_a_norm, w_kv_b, w_out, ffn2_norm, ffn2_w_gate, ffn2_w_up, ffn2_w_down, final_norm, loss_target, m_ada_w, m_ada_b, m_ffn1_norm, m_ffn1_w_gate, m_ffn1_w_up, m_ffn1_w_down, m_mix_norm, m_w_in, m_pool_w, m_pool_scale, m_q_a_norm, m_w_q_b, m_kv_a_norm, m_w_kv_b, m_w_out, m_ffn2_norm, m_ffn2_w_gate, m_ffn2_w_up, m_ffn2_w_down, m_final_norm, v_ada_w, v_ada_b, v_ffn1_norm, v_ffn1_w_gate, v_ffn1_w_up, v_ffn1_w_down, v_mix_norm, v_w_in, v_pool_w, v_pool_scale, v_q_a_norm, v_w_q_b, v_kv_a_norm, v_w_kv_b, v_w_out, v_ffn2_norm, v_ffn2_w_gate, v_ffn2_w_up, v_ffn2_w_down, v_final_norm):
    wts = dict(ada_w=ada_w, ada_b=ada_b, ffn1_norm=ffn1_norm, ffn1_w_gate=ffn1_w_gate, ffn1_w_up=ffn1_w_up,
               ffn1_w_down=ffn1_w_down, mix_norm=mix_norm, w_in=w_in, pool_w=pool_w, pool_scale=pool_scale,
               q_a_norm=q_a_norm, w_q_b=w_q_b, kv_a_norm=kv_a_norm, w_kv_b=w_kv_b, w_out=w_out,
               ffn2_norm=ffn2_norm, ffn2_w_gate=ffn2_w_gate, ffn2_w_up=ffn2_w_up, ffn2_w_down=ffn2_w_down,
               final_norm=final_norm)
    mom_m = dict(ada_w=m_ada_w, ada_b=m_ada_b, ffn1_norm=m_ffn1_norm, ffn1_w_gate=m_ffn1_w_gate,
                 ffn1_w_up=m_ffn1_w_up, ffn1_w_down=m_ffn1_w_down, mix_norm=m_mix_norm, w_in=m_w_in,
                 pool_w=m_pool_w, pool_scale=m_pool_scale, q_a_norm=m_q_a_norm, w_q_b=m_w_q_b,
                 kv_a_norm=m_kv_a_norm, w_kv_b=m_w_kv_b, w_out=m_w_out, ffn2_norm=m_ffn2_norm,
                 ffn2_w_gate=m_ffn2_w_gate, ffn2_w_up=m_ffn2_w_up, ffn2_w_down=m_ffn2_w_down,
                 final_norm=m_final_norm)
    mom_v = dict(ada_w=v_ada_w, ada_b=v_ada_b, ffn1_norm=v_ffn1_norm, ffn1_w_gate=v_ffn1_w_gate,
                 ffn1_w_up=v_ffn1_w_up, ffn1_w_down=v_ffn1_w_down, mix_norm=v_mix_norm, w_in=v_w_in,
                 pool_w=v_pool_w, pool_scale=v_pool_scale, q_a_norm=v_q_a_norm, w_q_b=v_w_q_b,
                 kv_a_norm=v_kv_a_norm, w_kv_b=v_w_kv_b, w_out=v_w_out, ffn2_norm=v_ffn2_norm,
                 ffn2_w_gate=v_ffn2_w_gate, ffn2_w_up=v_ffn2_w_up, ffn2_w_down=v_ffn2_w_down,
                 final_norm=v_final_norm)
    order = list(wts)
    nl = ada_w.shape[0]
    seq = x.shape[1]
    me = 4 * lax.axis_index("x") + 2 * lax.axis_index("y") + lax.axis_index("c")
    ada_cols = ada_w.shape[2]

    def after_token(t, token):
        return t + token[0:1, 0:1].astype(t.dtype)

    packs = [_pack_shard_layer(l, wts) for l in range(nl)]

    c_all = _all_gather(jnp.broadcast_to(c, (8, D_MODEL)), "gather_c")[::8]

    ada_b_mine = lax.dynamic_slice_in_dim(ada_b, me * ada_cols, ada_cols, axis=1).reshape(nl, 1, ada_cols)
    mod_part = _ada_mod(c_all, ada_w, ada_b_mine, "ada_mod")
    mod_all = _all_gather(mod_part.reshape(nl * N_DEV, ada_cols), "gather_mod")
    mod_all = mod_all.reshape(N_DEV, nl, N_DEV, ada_cols)
    mod = lax.dynamic_index_in_dim(mod_all, me, axis=2, keepdims=False)
    mod = mod.transpose(1, 0, 2).reshape(nl * N_MOD, 1, D_MODEL)
    norm_tables = {name: wts[name].reshape(nl, 1, D_MODEL) for name in ("ffn1_norm", "mix_norm", "ffn2_norm")}

    def modrow(l, k):
        return mod, l * N_MOD + k

    def normrow(name, l):
        return norm_tables[name], l

    def start_layer(l, after):
        return _gather_start([(packs[l][:SPLIT_AB], ROWS_A), (packs[l][OFF_OUT:], ROWS_TAIL),
                              (packs[l][SPLIT_AB:OFF_OUT], ROWS_A)], after, f"gather_start_{l}")

    flights = {0: start_layer(0, mod)}
    if nl > 1:
        flights[1] = start_layer(1, flights[0][2][4])
    last_start = flights[min(1, nl - 1)][2][4]

    cos, sin = _rope_tables(after_token(positions.reshape(seq, 1), last_start), "rope_tables")

    def vec(t):
        return t.reshape(1, -1)

    def landed(flight, rows_list, after, tag):
        send_sems, recv_sems, pk, lands, _ = flight
        pk, lands = _gather_wait(send_sems, recv_sems, pk, lands, after, f"gather_wait_{tag}")
        return _gather_finish(pk, rows_list, lands, "gather_finish")

    xs = x.reshape(seq, D_MODEL)
    saved = []
    for l in range(nl):
        norm1, up_after = normrow("ffn1_norm", l), None
        flight_a, flight_b, flight_c = flights[l]
        lands = landed(flight_a, ROWS_A, cos if l == 0 else xs, f"{l}a")
        if l >= 1 and l + 1 < nl:
            flights[l + 1] = start_layer(l + 1, lands[0])
            up_after = flights[l + 1][2][4]
        sv = {}

        def ffn_fwd(xin, norm, k0, wg, wu, wd, tag, after=None):
            h, a, b, t = _ffn_up(xin, norm, modrow(l, k0), modrow(l, k0 + 1), wg, wu, "ffn_up", after=after)
            y, xout = _mm(t, wd, "nn", "ffn_down", res=xin, gate=modrow(l, k0 + 2), gate_factor=0.5)
            sv[tag] = dict(x=xin, h=h, a=a, b=b, t=t, y=y)
            return xout

        xs = ffn_fwd(xs, norm1, 0, lands[0], lands[1], lands[2], "f1", up_after)
        w = dict(zip(("g1", "u1", "d1"), lands[:3]))
        w.update(_mixer_weights(*landed(flight_b, ROWS_TAIL, xs, f"{l}b")))
        sv["w"] = w

        h2, z = _norm_mm(xs, normrow("mix_norm", l), modrow(l, 3), modrow(l, 4), w["in"], "mix_in")
        y_pool, diff = _pool_fwd(z, pool_w[l], vec(pool_scale[l]), "pool_fwd")
        q, k, v, cqn, ckvn = _qkv_fwd(z, vec(q_a_norm[l]), vec(kv_a_norm[l]), w["q"], w["kv"], cos, sin, "qkv_fwd")
        o, lse = _attn_fwd(q, k, v, "attn_fwd")
        ycat = jnp.concatenate([y_pool, o.astype(BF16)], axis=1)
        y2, xmix = _mm(ycat, w["out"], "nn", "mix_out", res=xs, gate=modrow(l, 5), gate_factor=1.0)
        sv["mix"] = dict(x=xs, h=h2, z=z, diff=diff, q=q, k=k, v=v, cqn=cqn, ckvn=ckvn, lse=lse, ycat=ycat, y=y2)
        xs = xmix

        w.update(zip(("g2", "u2", "d2"), landed(flight_c, ROWS_A, xs, f"{l}c")))
        xs = ffn_fwd(xs, normrow("ffn2_norm", l), 6, w["g2"], w["u2"], w["d2"], "f2")
        saved.append(sv)

    loss_part, dx, d_final, *head = _loss_head(xs, vec(final_norm), loss_target.reshape(seq, D_MODEL),
                                               (saved[nl - 1]["f2"]["y"], modrow(nl - 1, 8), 0.5), "loss_head")

    small = {name: [None] * nl for name in ("ffn1_norm", "mix_norm", "ffn2_norm", "q_a_norm", "kv_a_norm",
                                            "pool_scale", "pool_w", "dmod")}
    core = lax.axis_index("c").astype(jnp.int32).reshape(1)
    chip = 2 * lax.axis_index("x") + lax.axis_index("y")
    exchanges = []

    def leave(srcs, rows_list, after, tag):
        return _pair_start(srcs, rows_list, after, f"pair_start_{tag}"), rows_list, tag

    def forward_on(pending, after, layer, row_off):
        (send_sems, recv_sems, srcs, land, _), rows_list, tag = pending
        srcs, land = _split_wait(send_sems, recv_sems, 1, srcs, land, after, f"pair_wait_{tag}")
        sums = _pair_sum(srcs, rows_list, land, core, "pair_sum")
        flight = _chip_exchange_start(sums, chip, after, f"exchange_start_{tag}")
        exchanges.append((flight, layer, row_off, tag))
        return flight[4]

    pending = None
    for l in reversed(range(nl)):
        sv = saved[l]
        w = sv["w"]
        dmod = [None] * N_MOD
        gr = {}

        def ffn_bwd(dxin, head, s_, norm, k0, wg, wu, wd, tag, below, first_after=None, mid=None):
            dy, dmod[k0 + 2] = head
            da, db, gr["d" + tag], gr["g" + tag], gr["u" + tag] = _ffn_bwd_cols(
                dy, s_["h"], s_["a"], s_["b"], s_["t"], wd, "ffn_bwd_cols", after=first_after)
            dh = _mm_pair(da, wg, db, wu, "ffn_bwd_dh", after=None if mid is None else mid(da))
            outs = _rm_bwd(dh, s_["x"], dxin, norm, modrow(l, k0 + 1), "rm_bwd", below=below)
            dmod[k0], dmod[k0 + 1] = outs[1], outs[2]
            return outs[0], outs[3], outs[4:]

        s_ = sv["mix"]
        dx, small["ffn2_norm"][l], head = ffn_bwd(
            dx, head, sv["f2"], normrow("ffn2_norm", l), 6, w["g2"], w["u2"], w["d2"], "2", (s_["y"], modrow(l, 5), 1.0),
            first_after=None if pending is None else pending[0][4])

        pending_c = leave([gr["g2"], gr["u2"], gr["d2"]], ROWS_A, dx, f"{l}c")
        mix_after = pending_c[0][4]
        if pending is not None:
            mix_after = forward_on(pending, mix_after, l + 1, GB_F1)
            pending = None
        dy, dmod[5] = head
        gr["out"] = _mm(s_["ycat"], dy, "tn", "mix_out_dw", out_dtype=BF16, tm=512, after=mix_after)
        dycat = _mm(dy, w["out"], "nt", "mix_out_dx", tm=1024)
        du, small["pool_w"][l], small["pool_scale"][l] = _pool_bwd(dycat, s_["diff"], pool_w[l], vec(pool_scale[l]), "pool_bwd")
        dq, dk, dv = _attn_bwd(s_["q"], s_["k"], s_["v"], s_["lse"], dycat, "attn_bwd")
        dz, dqb, dkvb, small["q_a_norm"][l], small["kv_a_norm"][l] = _qkv_bwd(
            dq, dk, dv, du, s_["z"], vec(q_a_norm[l]), vec(kv_a_norm[l]), w["q"], w["kv"], cos, sin, "qkv_bwd")
        gr["q"] = _mm(dqb, s_["cqn"], "tn", "q_b_dw", out_dtype=BF16, tm=512, after=forward_on(pending_c, dz, l, GB_F2))
        gr["kv"] = _mm(dkvb, s_["ckvn"], "tn", "kv_b_dw", out_dtype=BF16, tm=512)
        gr["in"] = _mm(dz, s_["h"], "tn", "mix_in_dw", out_dtype=BF16, tm=512)
        dh2 = _mm(dz, w["in"], "nn", "mix_in_dx", tm=1024)
        outs = _rm_bwd(dh2, s_["x"], dx, normrow("mix_norm", l), modrow(l, 4), "rm_bwd",
                       below=(sv["f1"]["y"], modrow(l, 2), 0.5))
        dx, dmod[3], dmod[4], small["mix_norm"][l] = outs[:4]
        head = outs[4:]

        first_after, mid = None, None
        if l == 0:
            pending_b = leave(_grad_sources_tail(gr), ROWS_TAIL, dx, "0b")
            first_after = pending_b[0][4]
            last_groups = []

            def mid(da):
                last_groups.append(leave([gr["g1"], gr["u1"], gr["d1"]], ROWS_A, da, "0a"))
                return forward_on(pending_b, last_groups[0][0][4], 0, GB_TAIL)
        below = (saved[l - 1]["f2"]["y"], modrow(l - 1, 8), 0.5) if l > 0 else None
        dx, small["ffn1_norm"][l], head = ffn_bwd(
            dx, head, sv["f1"], normrow("ffn1_norm", l), 0, w["g1"], w["u1"], w["d1"], "1", below, first_after, mid)

        small["dmod"][l] = jnp.concatenate(dmod, axis=0)
        if l > 0:
            pending = leave([gr["g1"], gr["u1"], gr["d1"]] + _grad_sources_tail(gr), ROWS_A + ROWS_TAIL, dx, l)

    grad_x = dx.reshape(x.shape)
    pending_a = last_groups[0]

    layout, small_rows = _small_layout(nl)
    pieces = {
        "dmod": jnp.concatenate(small["dmod"], axis=0),
        "ffn1_norm": jnp.concatenate(small["ffn1_norm"], axis=0),
        "mix_norm": jnp.concatenate(small["mix_norm"], axis=0),
        "ffn2_norm": jnp.concatenate(small["ffn2_norm"], axis=0),
        "q_a_norm": jnp.concatenate(small["q_a_norm"], axis=0),
        "kv_a_norm": jnp.concatenate(small["kv_a_norm"], axis=0),
        "pool_scale": jnp.concatenate(small["pool_scale"], axis=0),
        "final_norm": d_final,
        "loss": jnp.broadcast_to(loss_part[0:1, 0:1], (1, D_MODEL)),
        "pool_w": _pack_bf16_pairs(jnp.stack(small["pool_w"]).reshape(-1, D_MODEL)),
    }
    small_buf = jnp.concatenate([_to_rows(pieces[name]) for name in layout], axis=0)
    def landed_sums(gbuf, entries, after):
        for (send_sems, recv_sems, sums, recv, _), layer, row_off, tag in entries:
            _, recv = _split_wait(send_sems, recv_sems, N_CHIPS - 1, sums, recv, after, f"exchange_wait_{tag}")
            gbuf = _sum_slots_into(recv, gbuf, layer, row_off, "sum_grads")
        return gbuf

    gbuf = lax.empty((nl, ROWS_L, D_MODEL), F32)
    token_0a = forward_on(pending_a, dx, 0, GB_F1)
    spread = _spread_start(small_buf, me, token_0a, "small_start")
    gbuf = landed_sums(gbuf, [e for e in exchanges if e[3] != "0a"], spread[4])

    def swap(t):
        return t.transpose(0, 2, 1)

    def same(t):
        return t

    grads, updates = {}, {}

    def update_rows(gbuf, table):
        for wname, off, view in table:
            g, d_, nm, nv = _adamw_rows(view(wts[wname]), gbuf, off, view(mom_m[wname]), view(mom_v[wname]), "adamw_rows")
            grads[wname], updates[wname] = view(g), (view(d_), view(nm), view(nv))

    update_rows(gbuf, (("ffn2_w_gate", GB_F2, swap), ("ffn2_w_up", GB_F2 + FF_SH, swap),
                       ("ffn2_w_down", GB_F2 + 2 * FF_SH, same), ("w_out", GB_TAIL, same)))
    small_grads = {
        "w_kv_b": (gbuf[:, OFF_KV:OFF_KV + KV_SH_ROWS].reshape(nl, -1, KV_LORA).transpose(0, 2, 1), same),
        "w_in": (gbuf[:, OFF_IN:OFF_IN + IN_SH], swap),
        "w_q_b": (gbuf[:, OFF_Q:OFF_Q + Q_SH_ROWS].reshape(nl, -1, Q_LORA), swap),
    }
    for wname, (g, view) in small_grads.items():
        upd = _adamw_nd(view(wts[wname]), g, view(mom_m[wname]), view(mom_v[wname]), "adamw")
        grads[wname], updates[wname] = view(g), tuple(view(t) for t in upd)
    gbuf = landed_sums(gbuf, [e for e in exchanges if e[3] == "0a"], updates["w_q_b"][0])
    update_rows(gbuf, (("ffn1_w_gate", GB_F1, swap), ("ffn1_w_up", GB_F1 + FF_SH, swap),
                       ("ffn1_w_down", GB_F1 + 2 * FF_SH, same)))

    _, small_all = _split_wait(spread[0], spread[1], N_DEV - 1, spread[2], spread[3], updates["ffn1_w_down"][0],
                               "small_wait")
    pool_off, pool_rows = layout["pool_w"]
    small_sum = _sum_slots(small_all[:, :pool_off], "sum_small")
    pool_sum = _sum_slots(_unpack_bf16_pairs(small_all[:, pool_off:pool_off + pool_rows]), "sum_pool_w")

    def take(name, width=D_MODEL):
        off, n = layout[name]
        return small_sum[off:off + n, :width]

    late = {"ada_b": take("dmod").reshape(nl, N_MOD * D_MODEL),
            "ffn1_norm": take("ffn1_norm"), "mix_norm": take("mix_norm"), "ffn2_norm": take("ffn2_norm"),
            "q_a_norm": take("q_a_norm", Q_LORA), "kv_a_norm": take("kv_a_norm", KV_LORA),
            "pool_scale": take("pool_scale", POOL_WIDTH), "final_norm": take("final_norm").reshape(D_MODEL),
            "pool_w": pool_sum.reshape(pool_w.shape)}
    loss = take("loss")[0, 0]

    off, n = layout["dmod"]
    dmod_all = small_all[:, off:off + n].reshape(N_DEV, nl, N_MOD * D_MODEL)
    dmod_mine = lax.dynamic_slice_in_dim(dmod_all, me * ada_cols, ada_cols, axis=2)
    dmod_pad = jnp.pad(dmod_mine.transpose(1, 0, 2), ((0, 0), (0, LANE - N_DEV), (0, 0)))
    late["ada_w"] = _ada_grad(jnp.pad(c_all, ((0, LANE - N_DEV), (0, 0))), dmod_pad, "ada_grad")
    for name, g in late.items():
        grads[name], updates[name] = g, _adamw_nd(wts[name], g, mom_m[name], mom_v[name], "adamw")

    return (loss, grad_x, *[grads[n] for n in order], *[updates[n][0] for n in order],
            *[updates[n][1] for n in order], *[updates[n][2] for n in order])
```

```python
import math

import numpy as np
import jax
import jax.numpy as jnp
from jax import lax
from jax.experimental import pallas as pl
from jax.experimental.pallas import tpu as pltpu

F32 = jnp.float32
BF16 = jnp.bfloat16

N_DEV = 8
D_MODEL = 1024
D_FF = 2816
POOL_WIDTH = 512
POOL_WINDOWS = (2, 4, 8, 16)
POOL_GC = 128
N_HEADS = 4
QK_NOPE = 128
QK_ROPE = 64
V_HEAD = 128
QK_HEAD = QK_NOPE + QK_ROPE
HEAD_PAD = 256
Q_LORA = 384
KV_LORA = 256
IN_COLS = POOL_WIDTH + Q_LORA + KV_LORA + QK_ROPE
IN_PAD = 1280
ROPE_THETA = 10000.0
SOFTMAX_SCALE = 1.0 / math.sqrt(QK_HEAD)
EPS = 1e-6
N_MOD = 9

ADAM_LR = 0.001
ADAM_B1 = 0.9
ADAM_B2 = 0.999
ADAM_EPS = 1e-08
ADAM_WD = 0.01
ADAM_STEP = 10

LANE = 128
VMEM_LIMIT = 56 * 1024 * 1024

FF_SH = D_FF // N_DEV
OFF_G1, OFF_U1, OFF_D1 = 0, FF_SH, 2 * FF_SH
OFF_G2, OFF_U2, OFF_D2 = 3 * FF_SH, 4 * FF_SH, 5 * FF_SH
OFF_OUT = 6 * FF_SH
OFF_KV = OFF_OUT + 128
OFF_IN = OFF_KV + 32
OFF_Q = OFF_IN + 160
Q_PAD_ROWS = 64
ROWS_L = OFF_Q + Q_PAD_ROWS
IN_SH = IN_COLS // N_DEV
Q_SH_ROWS = (N_HEADS * QK_HEAD // N_DEV) * Q_LORA // D_MODEL
KV_SH_ROWS = (N_HEADS * (QK_NOPE + V_HEAD) // N_DEV) * KV_LORA // D_MODEL


def _tile(dim, target):
    if dim <= target:
        return dim
    best = None
    for t in range(LANE, target + 1, LANE):
        if dim % t == 0:
            best = t
    assert best is not None, (dim, target)
    return best


def _params(sem):
    return pltpu.CompilerParams(dimension_semantics=sem, vmem_limit_bytes=VMEM_LIMIT)


def _mesh_pos():
    return lax.axis_index("x"), lax.axis_index("y"), lax.axis_index("c")


def _all_gather(x, name):
    m, n = x.shape

    def body(x_ref, out_ref, send_sems, recv_sems, local_sem):
        px, py, pc = _mesh_pos()
        me, sibling = (px, py, pc), (px, py, 1 - pc)
        chips = [(1 - px, py), (px, 1 - py), (1 - px, 1 - py)]

        def rows(bx, by, bc):
            return out_ref.at[pl.ds((4 * bx + 2 * by + bc) * m, m), :]

        def copy(k, block, to, src=None):
            return pltpu.make_async_remote_copy(
                src_ref=rows(*block) if src is None else src, dst_ref=rows(*block),
                send_sem=send_sems.at[k], recv_sem=recv_sems.at[k],
                device_id=to, device_id_type=pl.DeviceIdType.MESH)

        mine = pltpu.make_async_copy(x_ref, rows(*me), local_sem)
        mine.start()
        first = [copy(0, me, sibling, src=x_ref)]
        first += [copy(1 + j, me, (*chip, pc), src=x_ref) for j, chip in enumerate(chips)]
        for cp in first:
            cp.start()
        passed = [copy(4 + j, (*chip, pc), sibling) for j, chip in enumerate(chips)]
        for j, chip in enumerate(chips):
            copy(1 + j, (*chip, pc), me).wait_recv()
            passed[j].start()
        copy(0, sibling, me).wait_recv()
        for j, chip in enumerate(chips):
            copy(4 + j, (*chip, 1 - pc), me).wait_recv()
        for cp in first + passed:
            cp.wait_send()
        mine.wait()

    hbm = pl.BlockSpec(memory_space=pltpu.HBM)
    return pl.pallas_call(
        body, name=name,
        out_shape=jax.ShapeDtypeStruct((N_DEV * m, n), x.dtype),
        in_specs=[hbm], out_specs=hbm,
        scratch_shapes=[pltpu.SemaphoreType.DMA((7,)), pltpu.SemaphoreType.DMA((7,)),
                        pltpu.SemaphoreType.DMA],
    )(x)


SMALL_ROWS = ROWS_L - OFF_KV
ROWS_A = [FF_SH] * 3
SPLIT_AB = sum(ROWS_A)
ROWS_TAIL = [128, SMALL_ROWS]
GB_F2, GB_F1, GB_TAIL = 0, SPLIT_AB, 2 * SPLIT_AB
HBM_SPEC = pl.BlockSpec(memory_space=pltpu.HBM)
SEM_SPEC = pl.BlockSpec(memory_space=pltpu.SEMAPHORE)
ANY_SPEC = pl.BlockSpec(memory_space=pl.ANY)
EFFECT = pltpu.SideEffectType.DATAFLOW_SIDE_EFFECTING


def _hbm(t):
    return pltpu.with_memory_space_constraint(t, pltpu.HBM)


def _whole_wait(ref, send_sem, recv_sem, peer):
    return pltpu.make_async_remote_copy(src_ref=ref, dst_ref=ref, send_sem=send_sem, recv_sem=recv_sem,
                                        device_id=peer, device_id_type=pl.DeviceIdType.MESH)


def _offsets(rows_list):
    return [sum(rows_list[:i]) for i in range(len(rows_list))]


def _gather_start(packed, rows_list, after, name):
    n = len(rows_list)
    offs = _offsets(rows_list)
    lands = [_hbm(lax.empty((N_DEV * rows, D_MODEL), BF16)) for rows in rows_list]

    def body(packed_ref, *refs):
        land = refs[:n]
        send_sems, recv_sems = refs[n + 1], refs[n + 2]
        token = refs[-1]
        px, py, pc = _mesh_pos()
        me = 4 * px + 2 * py + pc
        peers = [(px, py, 1 - pc), (1 - px, py, pc), (px, 1 - py, pc), (1 - px, 1 - py, pc)]
        for k, peer in enumerate(peers):
            for off, rows, land_ref in zip(offs, rows_list, land):
                pltpu.make_async_remote_copy(
                    src_ref=packed_ref.at[pl.ds(off, rows), :], dst_ref=land_ref.at[pl.ds(me * rows, rows), :],
                    send_sem=send_sems.at[k], recv_sem=recv_sems.at[k],
                    device_id=peer, device_id_type=pl.DeviceIdType.MESH).start()
        token[...] = jnp.zeros_like(token)

    outs = pl.pallas_call(
        body, name=name,
        out_shape=(pltpu.SemaphoreType.DMA((4,)), pltpu.SemaphoreType.DMA((4,)), pltpu.HBM(packed.shape, BF16),
                   *[pltpu.HBM(t.shape, BF16) for t in lands], jax.ShapeDtypeStruct((8, LANE), F32)),
        in_specs=(HBM_SPEC,) * (1 + n) + (ANY_SPEC,),
        out_specs=(SEM_SPEC, SEM_SPEC) + (HBM_SPEC,) * (1 + n) + (pl.BlockSpec(memory_space=pltpu.VMEM),),
        input_output_aliases={i: 2 + i for i in range(1 + n)},
        compiler_params=pltpu.CompilerParams(has_side_effects=EFFECT),
    )(_hbm(packed), *lands, after)
    return outs[0], outs[1], outs[2], list(outs[3:3 + n]), outs[-1]


def _gather_wait(send_sems, recv_sems, packed, lands, after, name):
    n = len(lands)

    def body(packed_ref, *refs):
        s_sems, r_sems = refs[n], refs[n + 1]
        me = _mesh_pos()
        for k in range(4):
            cp = _whole_wait(packed_ref, s_sems.at[k], r_sems.at[k], me)
            cp.wait_send()
            cp.wait_recv()

    outs = pl.pallas_call(
        body, name=name,
        out_shape=(pltpu.HBM(packed.shape, BF16), *[pltpu.HBM(t.shape, BF16) for t in lands]),
        in_specs=(HBM_SPEC,) * (1 + n) + (SEM_SPEC, SEM_SPEC, ANY_SPEC),
        out_specs=(HBM_SPEC,) * (1 + n),
        input_output_aliases={i: i for i in range(1 + n)},
        compiler_params=pltpu.CompilerParams(has_side_effects=EFFECT),
    )(packed, *lands, send_sems, recv_sems, after)
    return outs[0], list(outs[1:])


def _gather_finish(packed, rows_list, lands, name):
    n = len(rows_list)
    offs = _offsets(rows_list)

    def body(packed_ref, *refs):
        land = refs[n:2 * n]
        send_sems, recv_sems, stage, stage_sem = refs[2 * n:]
        px, py, pc = _mesh_pos()
        me = 4 * px + 2 * py + pc
        sibling = (px, py, 1 - pc)
        load = pltpu.make_async_copy(packed_ref, stage, stage_sem)
        load.start()
        for j, (cx, cy) in enumerate([(1 - px, py), (px, 1 - py), (1 - px, 1 - py)]):
            block = 4 * cx + 2 * cy + pc
            for rows, land_ref in zip(rows_list, land):
                blk = land_ref.at[pl.ds(block * rows, rows), :]
                pltpu.make_async_remote_copy(src_ref=blk, dst_ref=blk, send_sem=send_sems.at[j],
                                             recv_sem=recv_sems.at[j], device_id=sibling,
                                             device_id_type=pl.DeviceIdType.MESH).start()
        load.wait()
        for off, rows, land_ref in zip(offs, rows_list, land):
            pltpu.make_async_copy(stage.at[pl.ds(off, rows), :], land_ref.at[pl.ds(me * rows, rows), :],
                                  stage_sem).start()
        for j in range(3):
            cp = _whole_wait(packed_ref, send_sems.at[j], recv_sems.at[j], sibling)
            cp.wait_recv()
            cp.wait_send()
        pltpu.make_async_copy(stage, packed_ref, stage_sem).wait()

    outs = pl.pallas_call(
        body, name=name,
        out_shape=tuple(jax.ShapeDtypeStruct(t.shape, BF16) for t in lands),
        in_specs=(HBM_SPEC,) * (1 + n), out_specs=(HBM_SPEC,) * n,
        input_output_aliases={1 + i: i for i in range(n)},
        scratch_shapes=[pltpu.SemaphoreType.DMA((3,)), pltpu.SemaphoreType.DMA((3,)),
                        pltpu.VMEM(packed.shape, BF16), pltpu.SemaphoreType.DMA],
    )(packed, *lands)
    return list(outs)


N_CHIPS = 4


def _pair_start(srcs, rows_list, after, name):
    n = len(rows_list)
    offs = _offsets(rows_list)
    land = lax.empty((N_CHIPS, sum(rows_list), D_MODEL), BF16)

    def body(*refs):
        src, land_ref = refs[:n], refs[n]
        send_sems, recv_sems = refs[n + 2], refs[n + 3]
        token = refs[-1]
        px, py, pc = _mesh_pos()
        for k in range(N_CHIPS):
            block = 2 * k + (1 - pc)
            for off, rows, src_ref in zip(offs, rows_list, src):
                pltpu.make_async_remote_copy(
                    src_ref=src_ref.at[pl.ds(block * rows, rows), :], dst_ref=land_ref.at[k, pl.ds(off, rows), :],
                    send_sem=send_sems.at[0], recv_sem=recv_sems.at[0],
                    device_id=(px, py, 1 - pc), device_id_type=pl.DeviceIdType.MESH).start()
        token[...] = jnp.zeros_like(token)

    outs = pl.pallas_call(
        body, name=name,
        out_shape=(pltpu.SemaphoreType.DMA((1,)), pltpu.SemaphoreType.DMA((1,)),
                   *[pltpu.HBM(t.shape, BF16) for t in srcs], pltpu.HBM(land.shape, BF16),
                   jax.ShapeDtypeStruct((8, LANE), F32)),
        in_specs=(HBM_SPEC,) * (n + 1) + (ANY_SPEC,),
        out_specs=(SEM_SPEC, SEM_SPEC) + (HBM_SPEC,) * (n + 1) + (pl.BlockSpec(memory_space=pltpu.VMEM),),
        input_output_aliases={i: 2 + i for i in range(n + 1)},
        compiler_params=pltpu.CompilerParams(has_side_effects=EFFECT),
    )(*[_hbm(t) for t in srcs], _hbm(land), after)
    return outs[0], outs[1], list(outs[2:2 + n]), outs[2 + n], outs[-1]


def _split_wait(send_sems, recv_sems, n_sems, srcs, land, after, name):
    n = len(srcs)

    def body(*refs):
        land_ref = refs[n]
        s_sems, r_sems = refs[n + 1], refs[n + 2]
        me = _mesh_pos()
        for k in range(n_sems):
            cp = _whole_wait(land_ref.at[0] if n_sems > 1 else land_ref, s_sems.at[k], r_sems.at[k], me)
            cp.wait_send()
            cp.wait_recv()

    outs = pl.pallas_call(
        body, name=name,
        out_shape=(*[pltpu.HBM(t.shape, t.dtype) for t in srcs], pltpu.HBM(land.shape, land.dtype)),
        in_specs=(HBM_SPEC,) * (n + 1) + (SEM_SPEC, SEM_SPEC, ANY_SPEC),
        out_specs=(HBM_SPEC,) * (n + 1),
        input_output_aliases={i: i for i in range(n + 1)},
        compiler_params=pltpu.CompilerParams(has_side_effects=EFFECT),
    )(*srcs, land, send_sems, recv_sems, after)
    return list(outs[:n]), outs[n]


def _spread_start(x, me_id, after, name):
    land = lax.dynamic_update_slice_in_dim(lax.empty((N_DEV,) + x.shape, x.dtype), x[None], me_id, axis=0)

    def body(x_ref, land_ref, after_ref, send_sems, recv_sems, x_thru, land_thru, token):
        px, py, pc = _mesh_pos()
        me = 4 * px + 2 * py + pc
        for k in range(1, N_DEV):
            qx = 1 - px if k & 4 else px
            qy = 1 - py if k & 2 else py
            qc = 1 - pc if k & 1 else pc
            pltpu.make_async_remote_copy(
                src_ref=x_ref, dst_ref=land_ref.at[me], send_sem=send_sems.at[k - 1], recv_sem=recv_sems.at[k - 1],
                device_id=(qx, qy, qc), device_id_type=pl.DeviceIdType.MESH).start()
        token[...] = jnp.zeros_like(token)

    outs = pl.pallas_call(
        body, name=name,
        out_shape=(pltpu.SemaphoreType.DMA((N_DEV - 1,)), pltpu.SemaphoreType.DMA((N_DEV - 1,)),
                   pltpu.HBM(x.shape, x.dtype), pltpu.HBM(land.shape, land.dtype), jax.ShapeDtypeStruct((8, LANE), F32)),
        in_specs=(HBM_SPEC, HBM_SPEC, ANY_SPEC),
        out_specs=(SEM_SPEC, SEM_SPEC, HBM_SPEC, HBM_SPEC, pl.BlockSpec(memory_space=pltpu.VMEM)),
        input_output_aliases={0: 2, 1: 3},
        compiler_params=pltpu.CompilerParams(has_side_effects=EFFECT),
    )(_hbm(x), _hbm(land), after)
    return outs[0], outs[1], [outs[2]], outs[3], outs[4]


def _pair_sum(srcs, rows_list, land, core, name):
    n = len(rows_list)
    offs = _offsets(rows_list)
    total = sum(rows_list)

    def body(core_ref, *refs):
        src, land_ref, out_ref = refs[:n], refs[n], refs[n + 1]
        for off, rows, src_ref in zip(offs, rows_list, src):
            out_ref[pl.ds(off, rows), :] = (src_ref[...].astype(F32)
                                            + land_ref[pl.ds(off, rows), :].astype(F32)).astype(BF16)

    slot = pl.BlockSpec((None, total, D_MODEL), lambda k, c: (k, 0, 0))
    grid_spec = pltpu.PrefetchScalarGridSpec(
        num_scalar_prefetch=1, grid=(N_CHIPS,),
        in_specs=[pl.BlockSpec((rows, D_MODEL), lambda k, c: (2 * k + c[0], 0)) for rows in rows_list] + [slot],
        out_specs=slot)
    return pl.pallas_call(
        body, name=name, grid_spec=grid_spec,
        out_shape=jax.ShapeDtypeStruct((N_CHIPS, total, D_MODEL), BF16),
        compiler_params=_params(("parallel",)),
    )(core, *srcs, land)


def _chip_exchange_start(sums, chip, after, name):
    own = lax.dynamic_index_in_dim(sums, chip, axis=0, keepdims=True)
    recv = lax.dynamic_update_slice_in_dim(lax.empty(sums.shape, BF16), own, chip, axis=0)

    def body(sums_ref, recv_ref, after_ref, send_sems, recv_sems, sums_thru, recv_thru, token):
        px, py, pc = _mesh_pos()
        for k in range(1, N_CHIPS):
            qx = 1 - px if k & 2 else px
            qy = 1 - py if k & 1 else py
            pltpu.make_async_remote_copy(
                src_ref=sums_ref.at[2 * qx + qy], dst_ref=recv_ref.at[2 * px + py],
                send_sem=send_sems.at[k - 1], recv_sem=recv_sems.at[k - 1],
                device_id=(qx, qy, pc), device_id_type=pl.DeviceIdType.MESH).start()
        token[...] = jnp.zeros_like(token)

    outs = pl.pallas_call(
        body, name=name,
        out_shape=(pltpu.SemaphoreType.DMA((N_CHIPS - 1,)), pltpu.SemaphoreType.DMA((N_CHIPS - 1,)),
                   pltpu.HBM(sums.shape, BF16), pltpu.HBM(recv.shape, BF16), jax.ShapeDtypeStruct((8, LANE), F32)),
        in_specs=(HBM_SPEC, HBM_SPEC, ANY_SPEC),
        out_specs=(SEM_SPEC, SEM_SPEC, HBM_SPEC, HBM_SPEC, pl.BlockSpec(memory_space=pltpu.VMEM)),
        input_output_aliases={0: 2, 1: 3},
        compiler_params=pltpu.CompilerParams(has_side_effects=EFFECT),
    )(_hbm(sums), _hbm(recv), after)
    return outs[0], outs[1], [outs[2]], outs[3], outs[4]


def _sum_slots_into(recv, buf, layer, row_off, name):
    slots, r, n = recv.shape
    tr = _row_tile(math.gcd(r, row_off) if row_off else r, 512)
    first = row_off // tr

    def body(in_ref, buf_ref, out_ref):
        acc = in_ref[0].astype(F32)
        for j in range(1, slots):
            acc = acc + in_ref[j].astype(F32)
        out_ref[...] = acc

    return pl.pallas_call(
        body, name=name, grid=(r // tr,), out_shape=jax.ShapeDtypeStruct(buf.shape, F32),
        in_specs=[pl.BlockSpec((slots, tr, n), lambda i: (0, i, 0)), ANY_SPEC],
        out_specs=pl.BlockSpec((None, tr, n), lambda i: (layer, first + i, 0)),
        input_output_aliases={1: 0},
        compiler_params=_params(("parallel",)),
    )(recv, buf)


def _sum_slots(recv, name, after=None):
    _, r, n = recv.shape
    tr = _row_tile(r, 512)

    def body(in_ref, *refs):
        acc = in_ref[0].astype(F32)
        for j in range(1, N_DEV):
            acc = acc + in_ref[j].astype(F32)
        refs[-1][...] = acc

    grid = (r // tr,)
    in_specs, out_spec = [pl.BlockSpec((N_DEV, tr, n), lambda i: (0, i, 0))], pl.BlockSpec((tr, n), lambda i: (i, 0))
    args = [recv]
    if after is not None:
        in_specs.append(ANY_SPEC)
        args.append(after)
    return pl.pallas_call(
        body, name=name, grid=grid,
        out_shape=jax.ShapeDtypeStruct((r, n), F32),
        in_specs=in_specs, out_specs=out_spec,
        compiler_params=_params(("parallel",)),
    )(*args)


def _row_tile(rows, target):
    if rows <= target:
        return rows
    best = None
    for t in range(16, target + 1, 16):
        if rows % t == 0:
            best = t
    assert best is not None, rows
    return best


_DIMS = {"nn": ((1,), (0,)), "nt": ((1,), (1,)), "tn": ((0,), (0,))}


def _mm(a, b, mode, name, out_dtype=F32, res=None, gate=None, gate_factor=1.0, tm=512, tn=1408, after=None):
    assert (res is None) == (gate is None)
    if mode == "tn":
        kdim, m = a.shape
    else:
        m, kdim = a.shape
    n = b.shape[0] if mode == "nt" else b.shape[1]
    tm, tn = _tile(m, tm), _tile(n, tn)
    a_spec = (pl.BlockSpec((kdim, tm), lambda i, j: (0, i)) if mode == "tn"
              else pl.BlockSpec((tm, kdim), lambda i, j: (i, 0)))
    b_spec = (pl.BlockSpec((tn, kdim), lambda i, j: (j, 0)) if mode == "nt"
              else pl.BlockSpec((kdim, tn), lambda i, j: (0, j)))
    o_spec = pl.BlockSpec((tm, tn), lambda i, j: (i, j))
    dims = (_DIMS[mode], ((), ()))
    has_res = res is not None

    def body(a_ref, b_ref, *refs):
        y = lax.dot_general(a_ref[...].astype(BF16), b_ref[...].astype(BF16), dims,
                            preferred_element_type=F32)
        if has_res:
            res_ref, gate_ref = refs[0], refs[1]
            y_ref, o_ref = refs[-2], refs[-1]
            y_ref[...] = y.astype(BF16)
            o_ref[...] = res_ref[...] + (gate_factor * gate_ref[...]) * y
        else:
            refs[-1][...] = y.astype(out_dtype)

    in_specs, args = [a_spec, b_spec], [a, b]
    if has_res:
        gate_spec, gate_arg = _vec_in(gate, tile=tn)
        in_specs += [o_spec, gate_spec]
        args += [res, gate_arg]
        out_shape = (jax.ShapeDtypeStruct((m, n), BF16), jax.ShapeDtypeStruct((m, n), F32))
        out_specs = (o_spec, o_spec)
    else:
        out_shape, out_specs = jax.ShapeDtypeStruct((m, n), out_dtype), o_spec
    if after is not None:
        in_specs.append(ANY_SPEC)
        args.append(after)
    return pl.pallas_call(
        body, name=name, grid=(m // tm, n // tn), out_shape=out_shape,
        in_specs=in_specs, out_specs=out_specs,
        compiler_params=_params(("parallel", "parallel")),
    )(*args)


def _vec_in(v, tile=None):
    if isinstance(v, tuple):
        table, row = v
        if tile is None:
            return pl.BlockSpec((None, 1, table.shape[-1]), lambda *idx: (row, 0, 0)), table
        return pl.BlockSpec((None, 1, tile), lambda i, j: (row, 0, j)), table
    if tile is None:
        return pl.BlockSpec((1, v.shape[-1]), lambda *idx: (0, 0)), v
    return pl.BlockSpec((1, tile), lambda i, j: (0, j)), v


def _vec_spec(width):
    return pl.BlockSpec((1, width), lambda i: (0, 0))


def _rm_bwd(dh, x, dres, gw, scale, name, below=None):
    s, d = x.shape
    ts = _tile(s, 512)
    factor = None if below is None else below[2]

    def body(dh_ref, x_ref, dres_ref, gw_ref, sc_ref, *refs):
        dx_ref, dsh_ref, dsc_ref, dgw_ref = refs[-6:-2] if below is not None else refs[-4:]

        @pl.when(pl.program_id(0) == 0)
        def _():
            dsh_ref[...] = jnp.zeros_like(dsh_ref)
            dsc_ref[...] = jnp.zeros_like(dsc_ref)
            dgw_ref[...] = jnp.zeros_like(dgw_ref)
            if below is not None:
                refs[-1][...] = jnp.zeros_like(refs[-1])

        xv, dhv, gwv = x_ref[...], dh_ref[...], gw_ref[...]
        r = lax.rsqrt(jnp.mean(xv * xv, axis=-1, keepdims=True) + EPS)
        xn = xv * r
        y = xn * gwv
        dsh_ref[...] += jnp.sum(dhv, axis=0, keepdims=True)
        dsc_ref[...] += jnp.sum(dhv * y, axis=0, keepdims=True)
        dy = dhv * (1 + sc_ref[...])
        dgw_ref[...] += jnp.sum(dy * xn, axis=0, keepdims=True)
        dxn = dy * gwv
        dx = dres_ref[...] + r * (dxn - xn * jnp.mean(dxn * xn, axis=-1, keepdims=True))
        dx_ref[...] = dx
        if below is not None:
            yb_ref, gb_ref, dyb_ref, dgb_ref = refs[0], refs[1], refs[-2], refs[-1]
            dyb_ref[...] = ((factor * gb_ref[...]) * dx).astype(BF16)
            dgb_ref[...] += jnp.sum((factor * dx) * yb_ref[...].astype(F32), axis=0, keepdims=True)

    row = pl.BlockSpec((ts, d), lambda i: (i, 0))
    vec = jax.ShapeDtypeStruct((1, d), F32)
    (gw_spec, gw), (sc_spec, scale) = _vec_in(gw), _vec_in(scale)
    in_specs, args = [row, row, row, gw_spec, sc_spec], [dh, x, dres, gw, scale]
    out_shape = [jax.ShapeDtypeStruct((s, d), F32), vec, vec, vec]
    out_specs = [row, _vec_spec(d), _vec_spec(d), _vec_spec(d)]
    if below is not None:
        gate_spec, gate_arg = _vec_in(below[1])
        in_specs += [row, gate_spec]
        args += [below[0], gate_arg]
        out_shape += [jax.ShapeDtypeStruct((s, d), BF16), vec]
        out_specs += [row, _vec_spec(d)]
    return pl.pallas_call(
        body, name=name, grid=(s // ts,), out_shape=tuple(out_shape),
        in_specs=in_specs, out_specs=tuple(out_specs),
        compiler_params=_params(("arbitrary",)),
    )(*args)


def _norm_mm(x, gw, shift, scale, w, name, tm=1024):
    s, d = x.shape
    n = w.shape[0]
    tm = _tile(s, tm)

    def body(x_ref, gw_ref, sh_ref, sc_ref, w_ref, h_ref, z_ref):
        xv = x_ref[...]
        r = lax.rsqrt(jnp.mean(xv * xv, axis=-1, keepdims=True) + EPS)
        hb = (((xv * r) * gw_ref[...]) * (1 + sc_ref[...]) + sh_ref[...]).astype(BF16)
        h_ref[...] = hb
        z_ref[...] = lax.dot_general(hb, w_ref[...], (((1,), (1,)), ((), ())), preferred_element_type=F32)

    row = pl.BlockSpec((tm, d), lambda i: (i, 0))
    return pl.pallas_call(
        body, name=name, grid=(s // tm,),
        out_shape=(jax.ShapeDtypeStruct((s, d), BF16), jax.ShapeDtypeStruct((s, n), F32)),
        in_specs=[row, _vec_in(gw)[0], _vec_in(shift)[0], _vec_in(scale)[0], pl.BlockSpec((n, d), lambda i: (0, 0))],
        out_specs=(row, pl.BlockSpec((tm, n), lambda i: (i, 0))),
        compiler_params=_params(("parallel",)),
    )(x, _vec_in(gw)[1], _vec_in(shift)[1], _vec_in(scale)[1], w)


FFN_TM, FFN_TF = 2048, 256


def _ffn_up(x, gw, shift, scale, wg, wu, name, after=None):
    s, d = x.shape
    f = wg.shape[0]
    tm, tf = _tile(s, FFN_TM), _tile(f, FFN_TF)
    nt = (((1,), (1,)), ((), ()))

    def body(x_ref, gw_ref, sh_ref, sc_ref, wg_ref, wu_ref, *refs):
        h_ref, a_ref, b_ref, t_ref = refs[-4:]

        @pl.when(pl.program_id(1) == 0)
        def _():
            xv = x_ref[...]
            r = lax.rsqrt(jnp.mean(xv * xv, axis=-1, keepdims=True) + EPS)
            h_ref[...] = (((xv * r) * gw_ref[...]) * (1 + sc_ref[...]) + sh_ref[...]).astype(BF16)

        hb = h_ref[...]
        av = lax.dot_general(hb, wg_ref[...], nt, preferred_element_type=F32)
        bv = lax.dot_general(hb, wu_ref[...], nt, preferred_element_type=F32)
        a_ref[...] = av.astype(BF16)
        b_ref[...] = bv.astype(BF16)
        t_ref[...] = ((av * jax.nn.sigmoid(av)) * bv).astype(BF16)

    row = pl.BlockSpec((tm, d), lambda i, j: (i, 0))
    wblk = pl.BlockSpec((tf, d), lambda i, j: (j, 0))
    blk = pl.BlockSpec((tm, tf), lambda i, j: (i, j))
    wide = jax.ShapeDtypeStruct((s, f), BF16)
    vec_specs, vec_args = zip(*[_vec_in(v) for v in (gw, shift, scale)])
    in_specs, args = [row, *vec_specs, wblk, wblk], [x, *vec_args, wg, wu]
    if after is not None:
        in_specs.append(ANY_SPEC)
        args.append(after)
    return pl.pallas_call(
        body, name=name, grid=(s // tm, f // tf),
        out_shape=(jax.ShapeDtypeStruct((s, d), BF16), wide, wide, wide),
        in_specs=in_specs, out_specs=(row, blk, blk, blk),
        compiler_params=_params(("parallel", "arbitrary")),
    )(*args)


def _ffn_bwd_cols(dy, h, a, b, t, wd, name, after=None):
    s, d = dy.shape
    f = wd.shape[0]
    tf = _tile(f, FFN_TF)
    nt = (((1,), (1,)), ((), ()))
    tn = (((0,), (0,)), ((), ()))

    def body(dy_ref, h_ref, a_ref, b_ref, t_ref, wd_ref, *refs):
        da_ref, db_ref, gd_ref, gg_ref, gu_ref = refs[-5:]
        dyb, hb = dy_ref[...], h_ref[...]
        dtv = lax.dot_general(dyb, wd_ref[...], nt, preferred_element_type=F32)
        av, bv = a_ref[...].astype(F32), b_ref[...].astype(F32)
        sg = jax.nn.sigmoid(av)
        dbv = (dtv * (av * sg)).astype(BF16)
        dav = ((dtv * bv) * (sg * (1 + av * (1 - sg)))).astype(BF16)
        da_ref[...] = dav
        db_ref[...] = dbv
        gd_ref[...] = lax.dot_general(t_ref[...], dyb, tn, preferred_element_type=F32).astype(BF16)
        gg_ref[...] = lax.dot_general(dav, hb, tn, preferred_element_type=F32).astype(BF16)
        gu_ref[...] = lax.dot_general(dbv, hb, tn, preferred_element_type=F32).astype(BF16)

    whole = pl.BlockSpec((s, d), lambda j: (0, 0))
    col = pl.BlockSpec((s, tf), lambda j: (0, j))
    wblk = pl.BlockSpec((tf, d), lambda j: (j, 0))
    wide, wgrad = jax.ShapeDtypeStruct((s, f), BF16), jax.ShapeDtypeStruct((f, d), BF16)
    in_specs, args = [whole, whole, col, col, col, wblk], [dy, h, a, b, t, wd]
    if after is not None:
        in_specs.append(ANY_SPEC)
        args.append(after)
    return pl.pallas_call(
        body, name=name, grid=(f // tf,), out_shape=(wide, wide, wgrad, wgrad, wgrad),
        in_specs=in_specs, out_specs=(col, col, wblk, wblk, wblk),
        compiler_params=_params(("parallel",)),
    )(*args)


def _mm_pair(a1, b1, a2, b2, name, tm=1024, tn=512, after=None):
    m, kdim = a1.shape
    n = b1.shape[1]
    tm, tn = _tile(m, tm), _tile(n, tn)

    def body(a1_ref, b1_ref, a2_ref, b2_ref, *refs):
        refs[-1][...] = (jnp.dot(a1_ref[...], b1_ref[...], preferred_element_type=F32)
                         + jnp.dot(a2_ref[...], b2_ref[...], preferred_element_type=F32))

    a_spec = pl.BlockSpec((tm, kdim), lambda i, j: (i, 0))
    b_spec = pl.BlockSpec((kdim, tn), lambda i, j: (0, j))
    in_specs, args = [a_spec, b_spec, a_spec, b_spec], [a1, b1, a2, b2]
    if after is not None:
        in_specs.append(ANY_SPEC)
        args.append(after)
    return pl.pallas_call(
        body, name=name, grid=(m // tm, n // tn), out_shape=jax.ShapeDtypeStruct((m, n), F32),
        in_specs=in_specs, out_specs=pl.BlockSpec((tm, tn), lambda i, j: (i, j)),
        compiler_params=_params(("parallel", "parallel")),
    )(*args)


def _pool_counts(s):
    return (lax.broadcasted_iota(jnp.int32, (s, POOL_GC), 0))


def _pool_fwd(z, pool_w, pool_scale, name):
    s = z.shape[0]

    def body(u_ref, w_ref, sc_ref, y_ref, diff_ref):
        t = lax.broadcasted_iota(jnp.int32, (s, POOL_GC), 0)
        for g, win in enumerate(POOL_WINDOWS):
            cols = slice(g * POOL_GC, (g + 1) * POOL_GC)
            u = u_ref[:, cols]
            acc, step = u, 1
            while step < win:
                acc = acc + jnp.where(t >= step, pltpu.roll(acc, step, 0), 0.0)
                step *= 2
            cnt = jnp.minimum(t + 1, win).astype(F32)
            diff = acc / cnt - u
            diff_ref[:, cols] = diff
            ypre = jnp.dot(diff.astype(BF16), w_ref[g].astype(BF16), preferred_element_type=F32)
            y_ref[:, cols] = (ypre * sc_ref[:, cols]).astype(BF16)

    return pl.pallas_call(
        body, name=name, grid=(1,),
        out_shape=(jax.ShapeDtypeStruct((s, POOL_WIDTH), BF16), jax.ShapeDtypeStruct((s, POOL_WIDTH), F32)),
        in_specs=[pl.BlockSpec((s, POOL_WIDTH), lambda i: (0, 0)),
                  pl.BlockSpec(pool_w.shape, lambda i: (0, 0, 0)),
                  pl.BlockSpec((1, POOL_WIDTH), lambda i: (0, 0))],
        out_specs=(pl.BlockSpec((s, POOL_WIDTH), lambda i: (0, 0)),
                   pl.BlockSpec((s, POOL_WIDTH), lambda i: (0, 0))),
        compiler_params=_params(("arbitrary",)),
    )(z, pool_w, pool_scale)


def _pool_bwd(dycat, diff, pool_w, pool_scale, name):
    s = diff.shape[0]

    def body(dy_ref, diff_ref, w_ref, sc_ref, du_ref, dw_ref, dsc_ref):
        t = lax.broadcasted_iota(jnp.int32, (s, POOL_GC), 0)
        for g, win in enumerate(POOL_WINDOWS):
            cols = slice(g * POOL_GC, (g + 1) * POOL_GC)
            dy, dfb, wb = dy_ref[:, cols], diff_ref[:, cols].astype(BF16), w_ref[g].astype(BF16)
            ypre = jnp.dot(dfb, wb, preferred_element_type=F32)
            dsc_ref[:, cols] = jnp.sum(dy * ypre, axis=0, keepdims=True)
            dypre = (dy * sc_ref[:, cols]).astype(BF16)
            ddiff = lax.dot_general(dypre, wb, (((1,), (1,)), ((), ())), preferred_element_type=F32)
            dw_ref[g] = lax.dot_general(dfb, dypre, (((0,), (0,)), ((), ())), preferred_element_type=F32)
            cnt = jnp.minimum(t + 1, win).astype(F32)
            acc, step = ddiff / cnt, 1
            while step < win:
                acc = acc + jnp.where(t < s - step, pltpu.roll(acc, s - step, 0), 0.0)
                step *= 2
            du_ref[:, cols] = acc - ddiff

    full = pl.BlockSpec((s, POOL_WIDTH), lambda i: (0, 0))
    return pl.pallas_call(
        body, name=name, grid=(1,),
        out_shape=(jax.ShapeDtypeStruct((s, POOL_WIDTH), F32),
                   jax.ShapeDtypeStruct(pool_w.shape, F32),
                   jax.ShapeDtypeStruct((1, POOL_WIDTH), F32)),
        in_specs=[full, full, pl.BlockSpec(pool_w.shape, lambda i: (0, 0, 0)),
                  pl.BlockSpec((1, POOL_WIDTH), lambda i: (0, 0))],
        out_specs=(full, pl.BlockSpec(pool_w.shape, lambda i: (0, 0, 0)),
                   pl.BlockSpec((1, POOL_WIDTH), lambda i: (0, 0))),
        compiler_params=_params(("arbitrary",)),
    )(dycat, diff, pool_w, pool_scale)


def _rope_tables(positions, name):
    s = positions.shape[0]
    ts = _tile(s, 512)
    freq = 1.0 / (ROPE_THETA ** (np.arange(0, QK_ROPE, 2, dtype=np.float32) / QK_ROPE))
    table = np.zeros((1, LANE), np.float32)
    table[0, :QK_ROPE // 2] = freq
    table[0, QK_ROPE // 2:QK_ROPE] = freq

    def body(pos_ref, f_ref, cos_ref, sin_ref):
        ang = pos_ref[...].astype(F32) * f_ref[...]
        cos_ref[...] = jnp.cos(ang)
        sin_ref[...] = jnp.sin(ang)

    out = jax.ShapeDtypeStruct((s, LANE), F32)
    blk = pl.BlockSpec((ts, LANE), lambda i: (i, 0))
    return pl.pallas_call(
        body, name=name, grid=(s // ts,), out_shape=(out, out),
        in_specs=[pl.BlockSpec((ts, 1), lambda i: (i, 0)), _vec_spec(LANE)], out_specs=(blk, blk),
        compiler_params=_params(("parallel",)),
    )(positions, jnp.asarray(table))


def _lane_mod64_low(shape):
    return (lax.broadcasted_iota(jnp.int32, shape, 1) % QK_ROPE) < (QK_ROPE // 2)


def _rope(x, cos, sin):
    rot = jnp.where(_lane_mod64_low(x.shape), -pltpu.roll(x, LANE - 32, 1), pltpu.roll(x, 32, 1))
    return x * cos + rot * sin


def _rope_t(dy, cos, sin):
    w = dy * sin
    rot_t = jnp.where(_lane_mod64_low(dy.shape), pltpu.roll(w, LANE - 32, 1), -pltpu.roll(w, 32, 1))
    return dy * cos + rot_t


def _plain_rms(x, g):
    r = lax.rsqrt(jnp.mean(x * x, axis=-1, keepdims=True) + EPS)
    return (x * r) * g, x * r, r


O_Q, O_KV, O_KR = POOL_WIDTH, POOL_WIDTH + Q_LORA, POOL_WIDTH + Q_LORA + KV_LORA


def _qkv_fwd(z, qn, kvn, wq, wkv, cos, sin, name):
    s = z.shape[0]
    ts = _tile(s, 512)

    def body(z_ref, qn_ref, kvn_ref, wq_ref, wkv_ref, cos_ref, sin_ref, q_ref, k_ref, v_ref, cqn_ref, ckvn_ref):
        cosv, sinv = cos_ref[...], sin_ref[...]
        cqn = _plain_rms(z_ref[:, O_Q:O_KV], qn_ref[...])[0].astype(BF16)
        ckvn = _plain_rms(z_ref[:, O_KV:O_KR], kvn_ref[...])[0].astype(BF16)
        cqn_ref[...] = cqn
        ckvn_ref[...] = ckvn
        nt = (((1,), (1,)), ((), ()))
        q = lax.dot_general(cqn, wq_ref[...], nt, preferred_element_type=F32)
        kv = lax.dot_general(ckvn, wkv_ref[...], nt, preferred_element_type=F32)
        kr = _rope(z_ref[:, O_KR:IN_PAD], cosv, sinv).astype(BF16)
        for h in range(N_HEADS):
            o = h * HEAD_PAD
            q_ref[:, o:o + QK_NOPE] = q[:, o:o + QK_NOPE].astype(BF16)
            q_ref[:, o + QK_NOPE:o + HEAD_PAD] = _rope(q[:, o + QK_NOPE:o + HEAD_PAD], cosv, sinv).astype(BF16)
            k_ref[:, o:o + QK_NOPE] = kv[:, o:o + QK_NOPE].astype(BF16)
            k_ref[:, o + QK_NOPE:o + HEAD_PAD] = kr
            v_ref[:, h * V_HEAD:(h + 1) * V_HEAD] = kv[:, o + QK_NOPE:o + HEAD_PAD].astype(BF16)

    def row(w):
        return pl.BlockSpec((ts, w), lambda i: (i, 0))

    def whole(arr):
        return pl.BlockSpec(arr.shape, lambda i: (0, 0))

    hp = N_HEADS * HEAD_PAD
    return pl.pallas_call(
        body, name=name, grid=(s // ts,),
        out_shape=(jax.ShapeDtypeStruct((s, hp), BF16), jax.ShapeDtypeStruct((s, hp), BF16),
                   jax.ShapeDtypeStruct((s, N_HEADS * V_HEAD), BF16),
                   jax.ShapeDtypeStruct((s, Q_LORA), BF16), jax.ShapeDtypeStruct((s, KV_LORA), BF16)),
        in_specs=[row(IN_PAD), whole(qn), whole(kvn), whole(wq), whole(wkv), row(LANE), row(LANE)],
        out_specs=(row(hp), row(hp), row(N_HEADS * V_HEAD), row(Q_LORA), row(KV_LORA)),
        compiler_params=_params(("parallel",)),
    )(z, qn, kvn, wq, wkv, cos, sin)


def _qkv_bwd(dq, dk, dv, du, z, qn, kvn, wq, wkv, cos, sin, name):
    s = z.shape[0]
    ts = _tile(s, 512)

    def norm_bwd(x, g, dy):
        _, xn, r = _plain_rms(x, g)
        dxn = dy * g
        return r * (dxn - xn * jnp.mean(dxn * xn, axis=-1, keepdims=True)), jnp.sum(dy * xn, axis=0, keepdims=True)

    def body(dq_ref, dk_ref, dv_ref, du_ref, z_ref, qn_ref, kvn_ref, wq_ref, wkv_ref, cos_ref, sin_ref,
             dz_ref, dqb_ref, dkvb_ref, dqn_ref, dkvn_ref):
        @pl.when(pl.program_id(0) == 0)
        def _():
            dqn_ref[...] = jnp.zeros_like(dqn_ref)
            dkvn_ref[...] = jnp.zeros_like(dkvn_ref)

        cosv, sinv = cos_ref[...], sin_ref[...]
        dkr = jnp.zeros((ts, LANE), F32)
        for h in range(N_HEADS):
            o = h * HEAD_PAD
            dqb_ref[:, o:o + QK_NOPE] = dq_ref[:, o:o + QK_NOPE].astype(BF16)
            dqb_ref[:, o + QK_NOPE:o + HEAD_PAD] = _rope_t(dq_ref[:, o + QK_NOPE:o + HEAD_PAD], cosv, sinv).astype(BF16)
            dkvb_ref[:, o:o + QK_NOPE] = dk_ref[:, o:o + QK_NOPE].astype(BF16)
            dkvb_ref[:, o + QK_NOPE:o + HEAD_PAD] = dv_ref[:, h * V_HEAD:(h + 1) * V_HEAD].astype(BF16)
            dkr = dkr + dk_ref[:, o + QK_NOPE:o + HEAD_PAD]
        dcqn = jnp.dot(dqb_ref[...], wq_ref[...], preferred_element_type=F32)
        dckvn = jnp.dot(dkvb_ref[...], wkv_ref[...], preferred_element_type=F32)
        dcq, dqn = norm_bwd(z_ref[:, O_Q:O_KV], qn_ref[...], dcqn)
        dckv, dkvn = norm_bwd(z_ref[:, O_KV:O_KR], kvn_ref[...], dckvn)
        dqn_ref[...] += dqn
        dkvn_ref[...] += dkvn
        dz_ref[:, 0:O_Q] = du_ref[...].astype(BF16)
        dz_ref[:, O_Q:O_KV] = dcq.astype(BF16)
        dz_ref[:, O_KV:O_KR] = dckv.astype(BF16)
        dz_ref[:, O_KR:IN_PAD] = _rope_t(dkr, cosv, sinv).astype(BF16)

    def row(w):
        return pl.BlockSpec((ts, w), lambda i: (i, 0))

    def whole(arr):
        return pl.BlockSpec(arr.shape, lambda i: (0, 0))

    hp = N_HEADS * HEAD_PAD
    return pl.pallas_call(
        body, name=name, grid=(s // ts,),
        out_shape=(jax.ShapeDtypeStruct((s, IN_PAD), BF16), jax.ShapeDtypeStruct((s, hp), BF16),
                   jax.ShapeDtypeStruct((s, hp), BF16),
                   jax.ShapeDtypeStruct((1, Q_LORA), F32), jax.ShapeDtypeStruct((1, KV_LORA), F32)),
        in_specs=[row(hp), row(hp), row(N_HEADS * V_HEAD), row(POOL_WIDTH), row(IN_PAD),
                  whole(qn), whole(kvn), whole(wq), whole(wkv), row(LANE), row(LANE)],
        out_specs=(row(IN_PAD), row(hp), row(hp), whole(qn), whole(kvn)),
        compiler_params=_params(("arbitrary",)),
    )(dq, dk, dv, du, z, qn, kvn, wq, wkv, cos, sin)


def _causal_scores(q, k, i, tq, klen):
    sc = lax.dot_general(q, k, (((1,), (1,)), ((), ())), preferred_element_type=F32) * SOFTMAX_SCALE
    qpos = i * tq + lax.broadcasted_iota(jnp.int32, (tq, klen), 0)
    kpos = lax.broadcasted_iota(jnp.int32, (tq, klen), 1)
    return jnp.where(qpos >= kpos, sc, -jnp.inf)


ATTN_TQ = 512
ATTN_SEGMENTS = 4


def _by_key_prefix(i, nq, tq, compute):
    nseg = min(ATTN_SEGMENTS, nq)
    per = nq // nseg
    for r in range(nseg):
        pl.when(i // per == r)(lambda r=r: compute((r + 1) * per * tq))


def _attn_fwd(q, k, v, name):
    s = q.shape[0]
    tq = _tile(s, ATTN_TQ)
    nq = s // tq

    def body(q_ref, k_ref, v_ref, o_ref, lse_ref):
        i = pl.program_id(1)

        def compute(klen):
            sc = _causal_scores(q_ref[...], k_ref[0:klen, :], i, tq, klen)
            mx = jnp.max(sc, axis=-1, keepdims=True)
            p = jnp.exp(sc - mx)
            den = jnp.sum(p, axis=-1, keepdims=True)
            o_ref[...] = jnp.dot((p / den).astype(BF16), v_ref[0:klen, :], preferred_element_type=F32)
            lse_ref[...] = mx + jnp.log(den)

        _by_key_prefix(i, nq, tq, compute)

    return pl.pallas_call(
        body, name=name, grid=(N_HEADS, s // tq),
        out_shape=(jax.ShapeDtypeStruct((s, N_HEADS * V_HEAD), F32), jax.ShapeDtypeStruct((N_HEADS, s, 1), F32)),
        in_specs=[pl.BlockSpec((tq, HEAD_PAD), lambda h, i: (i, h)),
                  pl.BlockSpec((s, HEAD_PAD), lambda h, i: (0, h)),
                  pl.BlockSpec((s, V_HEAD), lambda h, i: (0, h))],
        out_specs=(pl.BlockSpec((tq, V_HEAD), lambda h, i: (i, h)),
                   pl.BlockSpec((None, tq, 1), lambda h, i: (h, i, 0))),
        compiler_params=_params(("parallel", "parallel")),
    )(q, k, v)


def _attn_bwd(q, k, v, o, lse, dycat, name):
    s = q.shape[0]
    tq = _tile(s, ATTN_TQ)
    nq = s // tq
    tn_dims = (((0,), (0,)), ((), ()))

    def body(q_ref, k_ref, v_ref, o_ref, lse_ref, do_ref, dq_ref, dk_ref, dv_ref):
        i = pl.program_id(1)

        @pl.when(i == 0)
        def _():
            dk_ref[...] = jnp.zeros_like(dk_ref)
            dv_ref[...] = jnp.zeros_like(dv_ref)

        def compute(klen):
            qv, kv_, dov = q_ref[...], k_ref[0:klen, :], do_ref[...]
            dob = dov.astype(BF16)
            row_term = jnp.sum(dov * o_ref[...], axis=-1, keepdims=True)
            sc = _causal_scores(qv, kv_, i, tq, klen)
            p = jnp.exp(sc - lse_ref[...])
            dp = lax.dot_general(dob, v_ref[0:klen, :], (((1,), (1,)), ((), ())), preferred_element_type=F32)
            ds = (p * (dp - row_term) * SOFTMAX_SCALE).astype(BF16)
            dq_ref[...] = jnp.dot(ds, kv_, preferred_element_type=F32)
            dk_ref[0:klen, :] += lax.dot_general(ds, qv, tn_dims, preferred_element_type=F32)
            dv_ref[0:klen, :] += lax.dot_general(p.astype(BF16), dob, tn_dims, preferred_element_type=F32)

        _by_key_prefix(i, nq, tq, compute)

    n_pool_blocks = POOL_WIDTH // V_HEAD
    return pl.pallas_call(
        body, name=name, grid=(N_HEADS, s // tq),
        out_shape=(jax.ShapeDtypeStruct((s, N_HEADS * HEAD_PAD), F32),
                   jax.ShapeDtypeStruct((s, N_HEADS * HEAD_PAD), F32),
                   jax.ShapeDtypeStruct((s, N_HEADS * V_HEAD), F32)),
        in_specs=[pl.BlockSpec((tq, HEAD_PAD), lambda h, i: (i, h)),
                  pl.BlockSpec((s, HEAD_PAD), lambda h, i: (0, h)),
                  pl.BlockSpec((s, V_HEAD), lambda h, i: (0, h)),
                  pl.BlockSpec((tq, V_HEAD), lambda h, i: (i, h)),
                  pl.BlockSpec((None, tq, 1), lambda h, i: (h, i, 0)),
                  pl.BlockSpec((tq, V_HEAD), lambda h, i: (i, n_pool_blocks + h))],
        out_specs=(pl.BlockSpec((tq, HEAD_PAD), lambda h, i: (i, h)),
                   pl.BlockSpec((s, HEAD_PAD), lambda h, i: (0, h)),
                   pl.BlockSpec((s, V_HEAD), lambda h, i: (0, h))),
        compiler_params=_params(("parallel", "arbitrary")),
    )(q, k, v, o, lse, dycat)


def _loss_head(x, gw, target, below, name):
    s, d = x.shape
    ts = _tile(s, 512)
    factor = below[2]

    def body(x_ref, gw_ref, tgt_ref, yb_ref, gb_ref, loss_ref, dx_ref, dgw_ref, dyb_ref, dgb_ref):
        @pl.when(pl.program_id(0) == 0)
        def _():
            loss_ref[...] = jnp.zeros_like(loss_ref)
            dgw_ref[...] = jnp.zeros_like(dgw_ref)
            dgb_ref[...] = jnp.zeros_like(dgb_ref)

        xv, gwv = x_ref[...], gw_ref[...]
        r = lax.rsqrt(jnp.mean(xv * xv, axis=-1, keepdims=True) + EPS)
        xn = xv * r
        err = xn * gwv - tgt_ref[...]
        loss_ref[...] += 0.5 * jnp.sum(jnp.mean(err * err, axis=-1, keepdims=True))
        dy = err / d
        dgw_ref[...] += jnp.sum(dy * xn, axis=0, keepdims=True)
        dxn = dy * gwv
        dx = r * (dxn - xn * jnp.mean(dxn * xn, axis=-1, keepdims=True))
        dx_ref[...] = dx
        dyb_ref[...] = ((factor * gb_ref[...]) * dx).astype(BF16)
        dgb_ref[...] += jnp.sum((factor * dx) * yb_ref[...].astype(F32), axis=0, keepdims=True)

    row = pl.BlockSpec((ts, d), lambda i: (i, 0))
    gate_spec, gate_arg = _vec_in(below[1])
    vec = jax.ShapeDtypeStruct((1, d), F32)
    return pl.pallas_call(
        body, name=name, grid=(s // ts,),
        out_shape=(jax.ShapeDtypeStruct((8, LANE), F32), jax.ShapeDtypeStruct((s, d), F32), vec,
                   jax.ShapeDtypeStruct((s, d), BF16), vec),
        in_specs=[row, _vec_spec(d), row, row, gate_spec],
        out_specs=(pl.BlockSpec((8, LANE), lambda i: (0, 0)), row, _vec_spec(d), row, _vec_spec(d)),
        compiler_params=_params(("arbitrary",)),
    )(x, gw, target, below[0], gate_arg)


def _ada_mod(c_all, ada_w, ada_b, name):
    nl, d, cols = ada_w.shape

    def body(c_ref, w_ref, b_ref, o_ref):
        cv = c_ref[...]
        act = (cv * jax.nn.sigmoid(cv)).astype(BF16)
        o_ref[...] = jnp.dot(act, w_ref[...].astype(BF16), preferred_element_type=F32) + b_ref[...]

    return pl.pallas_call(
        body, name=name, grid=(nl,), out_shape=jax.ShapeDtypeStruct((nl, N_DEV, cols), F32),
        in_specs=[pl.BlockSpec((N_DEV, d), lambda l: (0, 0)),
                  pl.BlockSpec((None, d, cols), lambda l: (l, 0, 0)),
                  pl.BlockSpec((None, 1, cols), lambda l: (l, 0, 0))],
        out_specs=pl.BlockSpec((None, N_DEV, cols), lambda l: (l, 0, 0)),
        compiler_params=_params(("parallel",)),
    )(c_all, ada_w, ada_b)


def _ada_grad(c_pad, dmod_pad, name):
    nl, kpad, cols = dmod_pad.shape
    d = c_pad.shape[1]

    def body(c_ref, dm_ref, o_ref):
        cv = c_ref[...]
        act = (cv * jax.nn.sigmoid(cv)).astype(BF16)
        o_ref[...] = lax.dot_general(act, dm_ref[...].astype(BF16), (((0,), (0,)), ((), ())),
                                     preferred_element_type=F32)

    return pl.pallas_call(
        body, name=name, grid=(nl,), out_shape=jax.ShapeDtypeStruct((nl, d, cols), F32),
        in_specs=[pl.BlockSpec((kpad, d), lambda l: (0, 0)),
                  pl.BlockSpec((None, kpad, cols), lambda l: (l, 0, 0))],
        out_specs=pl.BlockSpec((None, d, cols), lambda l: (l, 0, 0)),
        compiler_params=_params(("parallel",)),
    )(c_pad, dmod_pad)


def _adamw_math(w, g, m, v):
    nm = ADAM_B1 * m + (1.0 - ADAM_B1) * g
    nv = ADAM_B2 * v + (1.0 - ADAM_B2) * (g * g)
    m_hat = nm / (1.0 - ADAM_B1 ** ADAM_STEP)
    v_hat = nv / (1.0 - ADAM_B2 ** ADAM_STEP)
    return -ADAM_LR * (m_hat / (jnp.sqrt(v_hat) + ADAM_EPS) + ADAM_WD * w), nm, nv


def _adamw_rows(w3, gbuf, row_off, m3, v3, name):
    nl, r, d = w3.shape
    tr = _row_tile(math.gcd(r, row_off) if row_off else r, 352)
    first = row_off // tr

    def body(w_ref, g_ref, m_ref, v_ref, go_ref, d_ref, nm_ref, nv_ref):
        gv = g_ref[...]
        go_ref[...] = gv
        d_ref[...], nm_ref[...], nv_ref[...] = _adamw_math(w_ref[...], gv, m_ref[...], v_ref[...])

    blk = pl.BlockSpec((None, tr, d), lambda l, i: (l, i, 0))
    gblk = pl.BlockSpec((None, tr, d), lambda l, i: (l, first + i, 0))
    out = jax.ShapeDtypeStruct((nl, r, d), F32)
    return pl.pallas_call(
        body, name=name, grid=(nl, r // tr), out_shape=(out, out, out, out),
        in_specs=[blk, gblk, blk, blk], out_specs=(blk, blk, blk, blk),
        compiler_params=_params(("parallel", "parallel")),
    )(w3, gbuf, m3, v3)


def _adamw(w, g, m, v, name):
    rows, cols = w.shape
    tr = _row_tile(rows, 512)

    def body(w_ref, g_ref, m_ref, v_ref, d_ref, nm_ref, nv_ref):
        d_ref[...], nm_ref[...], nv_ref[...] = _adamw_math(w_ref[...], g_ref[...], m_ref[...], v_ref[...])

    blk = pl.BlockSpec((tr, cols), lambda i: (i, 0))
    out = jax.ShapeDtypeStruct((rows, cols), F32)
    return pl.pallas_call(
        body, name=name, grid=(rows // tr,), out_shape=(out, out, out),
        in_specs=[blk, blk, blk, blk], out_specs=(blk, blk, blk),
        compiler_params=_params(("parallel",)),
    )(w, g, m, v)


def _adamw_nd(w, g, m, v, name):
    shape = w.shape
    flat = (lambda t: t.reshape(1, -1)) if w.ndim == 1 else (lambda t: t.reshape(-1, shape[-1]))
    return tuple(t.reshape(shape) for t in _adamw(flat(w), flat(g), flat(m), flat(v), name))


def _pad_rows(t, rows):
    return jnp.pad(t, ((0, rows - t.shape[0]), (0, 0)))


def _pack_shard_layer(l, wts):
    def tr(name):
        return wts[name][l].astype(BF16).T

    parts = [tr("ffn1_w_gate"), tr("ffn1_w_up"), wts["ffn1_w_down"][l].astype(BF16),
             tr("ffn2_w_gate"), tr("ffn2_w_up"), wts["ffn2_w_down"][l].astype(BF16),
             wts["w_out"][l].astype(BF16),
             tr("w_kv_b").reshape(KV_SH_ROWS, D_MODEL),
             _pad_rows(tr("w_in"), 160),
             _pad_rows(tr("w_q_b").reshape(Q_SH_ROWS, D_MODEL), Q_PAD_ROWS)]
    return jnp.concatenate(parts, axis=0)


def _mixer_weights(w_out, small):
    w = {"out": w_out}
    small = small.reshape(N_DEV, SMALL_ROWS, D_MODEL)
    o_in, o_q = OFF_IN - OFF_KV, OFF_Q - OFF_KV
    w["kv"] = small[:, :KV_SH_ROWS].reshape(N_HEADS * HEAD_PAD, KV_LORA)
    w["in"] = _pad_rows(small[:, o_in:o_in + IN_SH].reshape(IN_COLS, D_MODEL), IN_PAD)
    wq = small[:, o_q:o_q + Q_SH_ROWS].reshape(N_HEADS, QK_HEAD, Q_LORA)
    w["q"] = jnp.pad(wq, ((0, 0), (0, HEAD_PAD - QK_HEAD), (0, 0))).reshape(N_HEADS * HEAD_PAD, Q_LORA)
    return w


def _grad_sources_tail(gr):
    gq = gr["q"].reshape(N_HEADS, HEAD_PAD, Q_LORA)[:, :QK_HEAD].reshape(N_DEV, Q_SH_ROWS, D_MODEL)
    small = jnp.concatenate([
        gr["kv"].reshape(N_DEV, KV_SH_ROWS, D_MODEL),
        jnp.pad(gr["in"][:IN_COLS].reshape(N_DEV, IN_SH, D_MODEL), ((0, 0), (0, 160 - IN_SH), (0, 0))),
        jnp.pad(gq, ((0, 0), (0, Q_PAD_ROWS - Q_SH_ROWS), (0, 0)))], axis=1)
    return [gr["out"], small.reshape(N_DEV * SMALL_ROWS, D_MODEL)]


def _pack_bf16_pairs(t):
    rows, d = t.shape
    return lax.bitcast_convert_type(t.astype(BF16).reshape(rows // 2, 2, d).transpose(0, 2, 1), F32)


def _unpack_bf16_pairs(p):
    pairs = jnp.swapaxes(lax.bitcast_convert_type(p, BF16), -1, -2)
    return pairs.reshape(p.shape[:-2] + (2 * p.shape[-2], p.shape[-1]))


def _small_layout(nl):
    names = [("dmod", nl * N_MOD), ("ffn1_norm", nl), ("mix_norm", nl), ("ffn2_norm", nl), ("q_a_norm", nl),
             ("kv_a_norm", nl), ("pool_scale", nl), ("final_norm", 1), ("loss", 1),
             ("pool_w", nl * 4 * POOL_GC * POOL_GC // D_MODEL // 2)]
    off, table = 0, {}
    for name, n in names:
        table[name] = (off, n)
        off += -(-n // 8) * 8
    return table, off


def _to_rows(t, width=D_MODEL):
    n, w = t.shape
    return jnp.pad(t, ((0, -(-n // 8) * 8 - n), (0, width - w)))


def kernel(x, c, positions, ada_w, ada_b, ffn1_norm, ffn1_w_gate, ffn1_w_up, ffn1_w_down, mix_norm, w_in, pool_w, pool_scale, q_a_norm, w_q_b, kv_a_norm, w_kv_b, w_out, ffn2_norm, ffn2_w_gate, ffn2_w_up, ffn2_w_down, final_norm, loss_target, m_ada_w, m_ada_b, m_ffn1_norm, m_ffn1_w_gate, m_ffn1_w_up, m_ffn1_w_down, m_mix_norm, m_w_in, m_pool_w, m_pool_scale, m_q_a_norm, m_w_q_b, m_kv_a_norm, m_w_kv_b, m_w_out, m_ffn2_norm, m_ffn2_w_gate, m_ffn2_w_up, m_ffn2_w_down, m_final_norm, v_ada_w, v_ada_b, v_ffn1_norm, v_ffn1_w_gate, v_ffn1_w_up, v_ffn1_w_down, v_mix_norm, v_w_in, v_pool_w, v_pool_scale, v_q_a_norm, v_w_q_b, v_kv_a_norm, v_w_kv_b, v_w_out, v_ffn2_norm, v_ffn2_w_gate, v_ffn2_w_up, v_ffn2_w_down, v_final_norm):
    wts = dict(ada_w=ada_w, ada_b=ada_b, ffn1_norm=ffn1_norm, ffn1_w_gate=ffn1_w_gate, ffn1_w_up=ffn1_w_up,
               ffn1_w_down=ffn1_w_down, mix_norm=mix_norm, w_in=w_in, pool_w=pool_w, pool_scale=pool_scale,
               q_a_norm=q_a_norm, w_q_b=w_q_b, kv_a_norm=kv_a_norm, w_kv_b=w_kv_b, w_out=w_out,
               ffn2_norm=ffn2_norm, ffn2_w_gate=ffn2_w_gate, ffn2_w_up=ffn2_w_up, ffn2_w_down=ffn2_w_down,
               final_norm=final_norm)
    mom_m = dict(ada_w=m_ada_w, ada_b=m_ada_b, ffn1_norm=m_ffn1_norm, ffn1_w_gate=m_ffn1_w_gate,
                 ffn1_w_up=m_ffn1_w_up, ffn1_w_down=m_ffn1_w_down, mix_norm=m_mix_norm, w_in=m_w_in,
                 pool_w=m_pool_w, pool_scale=m_pool_scale, q_a_norm=m_q_a_norm, w_q_b=m_w_q_b,
                 kv_a_norm=m_kv_a_norm, w_kv_b=m_w_kv_b, w_out=m_w_out, ffn2_norm=m_ffn2_norm,
                 ffn2_w_gate=m_ffn2_w_gate, ffn2_w_up=m_ffn2_w_up, ffn2_w_down=m_ffn2_w_down,
                 final_norm=m_final_norm)
    mom_v = dict(ada_w=v_ada_w, ada_b=v_ada_b, ffn1_norm=v_ffn1_norm, ffn1_w_gate=v_ffn1_w_gate,
                 ffn1_w_up=v_ffn1_w_up, ffn1_w_down=v_ffn1_w_down, mix_norm=v_mix_norm, w_in=v_w_in,
                 pool_w=v_pool_w, pool_scale=v_pool_scale, q_a_norm=v_q_a_norm, w_q_b=v_w_q_b,
                 kv_a_norm=v_kv_a_norm, w_kv_b=v_w_kv_b, w_out=v_w_out, ffn2_norm=v_ffn2_norm,
                 ffn2_w_gate=v_ffn2_w_gate, ffn2_w_up=v_ffn2_w_up, ffn2_w_down=v_ffn2_w_down,
                 final_norm=v_final_norm)
    order = list(wts)
    nl = ada_w.shape[0]
    seq = x.shape[1]
    me = 4 * lax.axis_index("x") + 2 * lax.axis_index("y") + lax.axis_index("c")
    ada_cols = ada_w.shape[2]

    def after_token(t, token):
        return t + token[0:1, 0:1].astype(t.dtype)

    packs = [_pack_shard_layer(l, wts) for l in range(nl)]

    c_all = _all_gather(jnp.broadcast_to(c, (8, D_MODEL)), "gather_c")[::8]

    ada_b_mine = lax.dynamic_slice_in_dim(ada_b, me * ada_cols, ada_cols, axis=1).reshape(nl, 1, ada_cols)
    mod_part = _ada_mod(c_all, ada_w, ada_b_mine, "ada_mod")
    mod_all = _all_gather(mod_part.reshape(nl * N_DEV, ada_cols), "gather_mod")
    mod_all = mod_all.reshape(N_DEV, nl, N_DEV, ada_cols)
    mod = lax.dynamic_index_in_dim(mod_all, me, axis=2, keepdims=False)
    mod = mod.transpose(1, 0, 2).reshape(nl * N_MOD, 1, D_MODEL)
    norm_tables = {name: wts[name].reshape(nl, 1, D_MODEL) for name in ("ffn1_norm", "mix_norm", "ffn2_norm")}

    def modrow(l, k):
        return mod, l * N_MOD + k

    def normrow(name, l):
        return norm_tables[name], l

    def start_layer(l, after):
        first = _gather_start(packs[l][:SPLIT_AB], ROWS_A, after, f"gather_start_{l}a")
        mixer = _gather_start(packs[l][OFF_OUT:], ROWS_TAIL, first[4], f"gather_start_{l}b")
        second = _gather_start(packs[l][SPLIT_AB:OFF_OUT], ROWS_A, mixer[4], f"gather_start_{l}c")
        return first, mixer, second

    flights = {0: start_layer(0, mod)}
    if nl > 1:
        flights[1] = start_layer(1, flights[0][2][4])
    last_start = flights[min(1, nl - 1)][2][4]

    cos, sin = _rope_tables(after_token(positions.reshape(seq, 1), last_start), "rope_tables")

    def vec(t):
        return t.reshape(1, -1)

    def landed(flight, rows_list, after, tag):
        send_sems, recv_sems, pk, lands, _ = flight
        pk, lands = _gather_wait(send_sems, recv_sems, pk, lands, after, f"gather_wait_{tag}")
        return _gather_finish(pk, rows_list, lands, "gather_finish")

    xs = x.reshape(seq, D_MODEL)
    saved = []
    for l in range(nl):
        norm1, up_after = normrow("ffn1_norm", l), None
        flight_a, flight_b, flight_c = flights[l]
        lands = landed(flight_a, ROWS_A, cos if l == 0 else xs, f"{l}a")
        if l >= 1 and l + 1 < nl:
            flights[l + 1] = start_layer(l + 1, lands[0])
            up_after = flights[l + 1][2][4]
        sv = {}

        def ffn_fwd(xin, norm, k0, wg, wu, wd, tag, after=None):
            h, a, b, t = _ffn_up(xin, norm, modrow(l, k0), modrow(l, k0 + 1), wg, wu, "ffn_up", after=after)
            y, xout = _mm(t, wd, "nn", "ffn_down", res=xin, gate=modrow(l, k0 + 2), gate_factor=0.5)
            sv[tag] = dict(x=xin, h=h, a=a, b=b, t=t, y=y)
            return xout

        xs = ffn_fwd(xs, norm1, 0, lands[0], lands[1], lands[2], "f1", up_after)
        w = dict(zip(("g1", "u1", "d1"), lands[:3]))
        w.update(_mixer_weights(*landed(flight_b, ROWS_TAIL, xs, f"{l}b")))
        sv["w"] = w

        h2, z = _norm_mm(xs, normrow("mix_norm", l), modrow(l, 3), modrow(l, 4), w["in"], "mix_in")
        y_pool, diff = _pool_fwd(z, pool_w[l], vec(pool_scale[l]), "pool_fwd")
        q, k, v, cqn, ckvn = _qkv_fwd(z, vec(q_a_norm[l]), vec(kv_a_norm[l]), w["q"], w["kv"], cos, sin, "qkv_fwd")
        o, lse = _attn_fwd(q, k, v, "attn_fwd")
        ycat = jnp.concatenate([y_pool, o.astype(BF16)], axis=1)
        y2, xmix = _mm(ycat, w["out"], "nn", "mix_out", res=xs, gate=modrow(l, 5), gate_factor=1.0)
        sv["mix"] = dict(x=xs, h=h2, z=z, diff=diff, q=q, k=k, v=v, cqn=cqn, ckvn=ckvn, o=o, lse=lse, ycat=ycat, y=y2)
        xs = xmix

        w.update(zip(("g2", "u2", "d2"), landed(flight_c, ROWS_A, xs, f"{l}c")))
        xs = ffn_fwd(xs, normrow("ffn2_norm", l), 6, w["g2"], w["u2"], w["d2"], "f2")
        saved.append(sv)

    loss_part, dx, d_final, *head = _loss_head(xs, vec(final_norm), loss_target.reshape(seq, D_MODEL),
                                               (saved[nl - 1]["f2"]["y"], modrow(nl - 1, 8), 0.5), "loss_head")

    small = {name: [None] * nl for name in ("ffn1_norm", "mix_norm", "ffn2_norm", "q_a_norm", "kv_a_norm",
                                            "pool_scale", "pool_w", "dmod")}
    core = lax.axis_index("c").astype(jnp.int32).reshape(1)
    chip = 2 * lax.axis_index("x") + lax.axis_index("y")
    exchanges = []

    def leave(srcs, rows_list, after, tag):
        return _pair_start(srcs, rows_list, after, f"pair_start_{tag}"), rows_list, tag

    def forward_on(pending, after, layer, row_off):
        (send_sems, recv_sems, srcs, land, _), rows_list, tag = pending
        srcs, land = _split_wait(send_sems, recv_sems, 1, srcs, land, after, f"pair_wait_{tag}")
        sums = _pair_sum(srcs, rows_list, land, core, "pair_sum")
        flight = _chip_exchange_start(sums, chip, after, f"exchange_start_{tag}")
        exchanges.append((flight, layer, row_off, tag))
        return flight[4]

    pending = None
    for l in reversed(range(nl)):
        sv = saved[l]
        w = sv["w"]
        dmod = [None] * N_MOD
        gr = {}

        def ffn_bwd(dxin, head, s_, norm, k0, wg, wu, wd, tag, below, first_after=None, mid=None):
            dy, dmod[k0 + 2] = head
            da, db, gr["d" + tag], gr["g" + tag], gr["u" + tag] = _ffn_bwd_cols(
                dy, s_["h"], s_["a"], s_["b"], s_["t"], wd, "ffn_bwd_cols", after=first_after)
            dh = _mm_pair(da, wg, db, wu, "ffn_bwd_dh", after=None if mid is None else mid(da))
            outs = _rm_bwd(dh, s_["x"], dxin, norm, modrow(l, k0 + 1), "rm_bwd", below=below)
            dmod[k0], dmod[k0 + 1] = outs[1], outs[2]
            return outs[0], outs[3], outs[4:]

        s_ = sv["mix"]
        dx, small["ffn2_norm"][l], head = ffn_bwd(
            dx, head, sv["f2"], normrow("ffn2_norm", l), 6, w["g2"], w["u2"], w["d2"], "2", (s_["y"], modrow(l, 5), 1.0),
            first_after=None if pending is None else pending[0][4])

        pending_c = leave([gr["g2"], gr["u2"], gr["d2"]], ROWS_A, dx, f"{l}c")
        mix_after = pending_c[0][4]
        if pending is not None:
            mix_after = forward_on(pending, mix_after, l + 1, GB_F1)
            pending = None
        dy, dmod[5] = head
        gr["out"] = _mm(s_["ycat"], dy, "tn", "mix_out_dw", out_dtype=BF16, tm=512, after=mix_after)
        dycat = _mm(dy, w["out"], "nt", "mix_out_dx", tm=1024)
        du, small["pool_w"][l], small["pool_scale"][l] = _pool_bwd(dycat, s_["diff"], pool_w[l], vec(pool_scale[l]), "pool_bwd")
        dq, dk, dv = _attn_bwd(s_["q"], s_["k"], s_["v"], s_["o"], s_["lse"], dycat, "attn_bwd")
        dz, dqb, dkvb, small["q_a_norm"][l], small["kv_a_norm"][l] = _qkv_bwd(
            dq, dk, dv, du, s_["z"], vec(q_a_norm[l]), vec(kv_a_norm[l]), w["q"], w["kv"], cos, sin, "qkv_bwd")
        gr["q"] = _mm(dqb, s_["cqn"], "tn", "q_b_dw", out_dtype=BF16, tm=512, after=forward_on(pending_c, dz, l, GB_F2))
        gr["kv"] = _mm(dkvb, s_["ckvn"], "tn", "kv_b_dw", out_dtype=BF16, tm=512)
        gr["in"] = _mm(dz, s_["h"], "tn", "mix_in_dw", out_dtype=BF16, tm=512)
        dh2 = _mm(dz, w["in"], "nn", "mix_in_dx", tm=1024)
        outs = _rm_bwd(dh2, s_["x"], dx, normrow("mix_norm", l), modrow(l, 4), "rm_bwd",
                       below=(sv["f1"]["y"], modrow(l, 2), 0.5))
        dx, dmod[3], dmod[4], small["mix_norm"][l] = outs[:4]
        head = outs[4:]

        first_after, mid = None, None
        if l == 0:
            pending_b = leave(_grad_sources_tail(gr), ROWS_TAIL, dx, "0b")
            first_after = pending_b[0][4]
            last_groups = []

            def mid(da):
                last_groups.append(leave([gr["g1"], gr["u1"], gr["d1"]], ROWS_A, da, "0a"))
                return forward_on(pending_b, last_groups[0][0][4], 0, GB_TAIL)
        below = (saved[l - 1]["f2"]["y"], modrow(l - 1, 8), 0.5) if l > 0 else None
        dx, small["ffn1_norm"][l], head = ffn_bwd(
            dx, head, sv["f1"], normrow("ffn1_norm", l), 0, w["g1"], w["u1"], w["d1"], "1", below, first_after, mid)

        small["dmod"][l] = jnp.concatenate(dmod, axis=0)
        if l > 0:
            pending = leave([gr["g1"], gr["u1"], gr["d1"]] + _grad_sources_tail(gr), ROWS_A + ROWS_TAIL, dx, l)

    grad_x = dx.reshape(x.shape)
    pending_a = last_groups[0]

    layout, small_rows = _small_layout(nl)
    pieces = {
        "dmod": jnp.concatenate(small["dmod"], axis=0),
        "ffn1_norm": jnp.concatenate(small["ffn1_norm"], axis=0),
        "mix_norm": jnp.concatenate(small["mix_norm"], axis=0),
        "ffn2_norm": jnp.concatenate(small["ffn2_norm"], axis=0),
        "q_a_norm": jnp.concatenate(small["q_a_norm"], axis=0),
        "kv_a_norm": jnp.concatenate(small["kv_a_norm"], axis=0),
        "pool_scale": jnp.concatenate(small["pool_scale"], axis=0),
        "final_norm": d_final,
        "loss": jnp.broadcast_to(loss_part[0:1, 0:1], (1, D_MODEL)),
        "pool_w": _pack_bf16_pairs(jnp.stack(small["pool_w"]).reshape(-1, D_MODEL)),
    }
    small_buf = jnp.concatenate([_to_rows(pieces[name]) for name in layout], axis=0)
    def landed_sums(gbuf, entries, after):
        for (send_sems, recv_sems, sums, recv, _), layer, row_off, tag in entries:
            _, recv = _split_wait(send_sems, recv_sems, N_CHIPS - 1, sums, recv, after, f"exchange_wait_{tag}")
            gbuf = _sum_slots_into(recv, gbuf, layer, row_off, "sum_grads")
        return gbuf

    gbuf = lax.empty((nl, ROWS_L, D_MODEL), F32)
    token_0a = forward_on(pending_a, dx, 0, GB_F1)
    spread = _spread_start(small_buf, me, token_0a, "small_start")
    gbuf = landed_sums(gbuf, [e for e in exchanges if e[3] != "0a"], spread[4])

    def swap(t):
        return t.transpose(0, 2, 1)

    def same(t):
        return t

    grads, updates = {}, {}

    def update_rows(gbuf, table):
        for wname, off, view in table:
            g, d_, nm, nv = _adamw_rows(view(wts[wname]), gbuf, off, view(mom_m[wname]), view(mom_v[wname]), "adamw_rows")
            grads[wname], updates[wname] = view(g), (view(d_), view(nm), view(nv))

    update_rows(gbuf, (("ffn2_w_gate", GB_F2, swap), ("ffn2_w_up", GB_F2 + FF_SH, swap),
                       ("ffn2_w_down", GB_F2 + 2 * FF_SH, same), ("w_out", GB_TAIL, same)))
    small_grads = {
        "w_kv_b": (gbuf[:, OFF_KV:OFF_KV + KV_SH_ROWS].reshape(nl, -1, KV_LORA).transpose(0, 2, 1), same),
        "w_in": (gbuf[:, OFF_IN:OFF_IN + IN_SH], swap),
        "w_q_b": (gbuf[:, OFF_Q:OFF_Q + Q_SH_ROWS].reshape(nl, -1, Q_LORA), swap),
    }
    for wname, (g, view) in small_grads.items():
        upd = _adamw_nd(view(wts[wname]), g, view(mom_m[wname]), view(mom_v[wname]), "adamw")
        grads[wname], updates[wname] = view(g), tuple(view(t) for t in upd)
    gbuf = landed_sums(gbuf, [e for e in exchanges if e[3] == "0a"], updates["w_q_b"][0])
    update_rows(gbuf, (("ffn1_w_gate", GB_F1, swap), ("ffn1_w_up", GB_F1 + FF_SH, swap),
                       ("ffn1_w_down", GB_F1 + 2 * FF_SH, same)))

    _, small_all = _split_wait(spread[0], spread[1], N_DEV - 1, spread[2], spread[3], updates["ffn1_w_down"][0],
                               "small_wait")
    pool_off, pool_rows = layout["pool_w"]
    small_sum = _sum_slots(small_all[:, :pool_off], "sum_small")
    pool_sum = _sum_slots(_unpack_bf16_pairs(small_all[:, pool_off:pool_off + pool_rows]), "sum_pool_w")

    def take(name, width=D_MODEL):
        off, n = layout[name]
        return small_sum[off:off + n, :width]

    late = {"ada_b": take("dmod").reshape(nl, N_MOD * D_MODEL),
            "ffn1_norm": take("ffn1_norm"), "mix_norm": take("mix_norm"), "ffn2_norm": take("ffn2_norm"),
            "q_a_norm": take("q_a_norm", Q_LORA), "kv_a_norm": take("kv_a_norm", KV_LORA),
            "pool_scale": take("pool_scale", POOL_WIDTH), "final_norm": take("final_norm").reshape(D_MODEL),
            "pool_w": pool_sum.reshape(pool_w.shape)}
    loss = take("loss")[0, 0]

    off, n = layout["dmod"]
    dmod_all = small_all[:, off:off + n].reshape(N_DEV, nl, N_MOD * D_MODEL)
    dmod_mine = lax.dynamic_slice_in_dim(dmod_all, me * ada_cols, ada_cols, axis=2)
    dmod_pad = jnp.pad(dmod_mine.transpose(1, 0, 2), ((0, 0), (0, LANE - N_DEV), (0, 0)))
    late["ada_w"] = _ada_grad(jnp.pad(c_all, ((0, LANE - N_DEV), (0, 0))), dmod_pad, "ada_grad")
    for name, g in late.items():
        grads[name], updates[name] = g, _adamw_nd(wts[name], g, mom_m[name], mom_v[name], "adamw")

    return (loss, grad_x, *[grads[n] for n in order], *[updates[n][0] for n in order],
            *[updates[n][1] for n in order], *[updates[n][2] for n in order])
```

```python
import math

import numpy as np
import jax
import jax.numpy as jnp
from jax import lax
from jax.experimental import pallas as pl
from jax.experimental.pallas import tpu as pltpu

F32 = jnp.float32
BF16 = jnp.bfloat16

N_DEV = 8
D_MODEL = 1024
D_FF = 2816
POOL_WIDTH = 512
POOL_WINDOWS = (2, 4, 8, 16)
POOL_GC = 128
N_HEADS = 4
QK_NOPE = 128
QK_ROPE = 64
V_HEAD = 128
QK_HEAD = QK_NOPE + QK_ROPE
HEAD_PAD = 256
Q_LORA = 384
KV_LORA = 256
IN_COLS = POOL_WIDTH + Q_LORA + KV_LORA + QK_ROPE
IN_PAD = 1280
ROPE_THETA = 10000.0
SOFTMAX_SCALE = 1.0 / math.sqrt(QK_HEAD)
EPS = 1e-6
N_MOD = 9

ADAM_LR = 0.001
ADAM_B1 = 0.9
ADAM_B2 = 0.999
ADAM_EPS = 1e-08
ADAM_WD = 0.01
ADAM_STEP = 10

LANE = 128
VMEM_LIMIT = 56 * 1024 * 1024

FF_SH = D_FF // N_DEV
OFF_G1, OFF_U1, OFF_D1 = 0, FF_SH, 2 * FF_SH
OFF_G2, OFF_U2, OFF_D2 = 3 * FF_SH, 4 * FF_SH, 5 * FF_SH
OFF_OUT = 6 * FF_SH
OFF_KV = OFF_OUT + 128
OFF_IN = OFF_KV + 32
OFF_Q = OFF_IN + 160
Q_PAD_ROWS = 64
ROWS_L = OFF_Q + Q_PAD_ROWS
IN_SH = IN_COLS // N_DEV
Q_SH_ROWS = (N_HEADS * QK_HEAD // N_DEV) * Q_LORA // D_MODEL
KV_SH_ROWS = (N_HEADS * (QK_NOPE + V_HEAD) // N_DEV) * KV_LORA // D_MODEL


def _tile(dim, target):
    if dim <= target:
        return dim
    best = None
    for t in range(LANE, target + 1, LANE):
        if dim % t == 0:
            best = t
    assert best is not None, (dim, target)
    return best


def _params(sem):
    return pltpu.CompilerParams(dimension_semantics=sem, vmem_limit_bytes=VMEM_LIMIT)


def _mesh_pos():
    return lax.axis_index("x"), lax.axis_index("y"), lax.axis_index("c")


def _all_gather(x, name):
    m, n = x.shape

    def body(x_ref, out_ref, send_sems, recv_sems, local_sem):
        px, py, pc = _mesh_pos()
        me, sibling = (px, py, pc), (px, py, 1 - pc)
        chips = [(1 - px, py), (px, 1 - py), (1 - px, 1 - py)]

        def rows(bx, by, bc):
            return out_ref.at[pl.ds((4 * bx + 2 * by + bc) * m, m), :]

        def copy(k, block, to, src=None):
            return pltpu.make_async_remote_copy(
                src_ref=rows(*block) if src is None else src, dst_ref=rows(*block),
                send_sem=send_sems.at[k], recv_sem=recv_sems.at[k],
                device_id=to, device_id_type=pl.DeviceIdType.MESH)

        mine = pltpu.make_async_copy(x_ref, rows(*me), local_sem)
        mine.start()
        first = [copy(0, me, sibling, src=x_ref)]
        first += [copy(1 + j, me, (*chip, pc), src=x_ref) for j, chip in enumerate(chips)]
        for cp in first:
            cp.start()
        passed = [copy(4 + j, (*chip, pc), sibling) for j, chip in enumerate(chips)]
        for j, chip in enumerate(chips):
            copy(1 + j, (*chip, pc), me).wait_recv()
            passed[j].start()
        copy(0, sibling, me).wait_recv()
        for j, chip in enumerate(chips):
            copy(4 + j, (*chip, 1 - pc), me).wait_recv()
        for cp in first + passed:
            cp.wait_send()
        mine.wait()

    hbm = pl.BlockSpec(memory_space=pltpu.HBM)
    return pl.pallas_call(
        body, name=name,
        out_shape=jax.ShapeDtypeStruct((N_DEV * m, n), x.dtype),
        in_specs=[hbm], out_specs=hbm,
        scratch_shapes=[pltpu.SemaphoreType.DMA((7,)), pltpu.SemaphoreType.DMA((7,)),
                        pltpu.SemaphoreType.DMA],
    )(x)


SMALL_ROWS = ROWS_L - OFF_KV
ROWS_A = [FF_SH] * 3
SPLIT_AB = sum(ROWS_A)
ROWS_TAIL = [128, SMALL_ROWS]
GB_F2, GB_F1, GB_TAIL = 0, SPLIT_AB, 2 * SPLIT_AB
HBM_SPEC = pl.BlockSpec(memory_space=pltpu.HBM)
SEM_SPEC = pl.BlockSpec(memory_space=pltpu.SEMAPHORE)
ANY_SPEC = pl.BlockSpec(memory_space=pl.ANY)
EFFECT = pltpu.SideEffectType.DATAFLOW_SIDE_EFFECTING


def _hbm(t):
    return pltpu.with_memory_space_constraint(t, pltpu.HBM)


def _whole_wait(ref, send_sem, recv_sem, peer):
    return pltpu.make_async_remote_copy(src_ref=ref, dst_ref=ref, send_sem=send_sem, recv_sem=recv_sem,
                                        device_id=peer, device_id_type=pl.DeviceIdType.MESH)


def _offsets(rows_list):
    return [sum(rows_list[:i]) for i in range(len(rows_list))]


def _gather_start(packed, rows_list, after, name):
    n = len(rows_list)
    offs = _offsets(rows_list)
    lands = [_hbm(lax.empty((N_DEV * rows, D_MODEL), BF16)) for rows in rows_list]

    def body(packed_ref, *refs):
        land = refs[:n]
        send_sems, recv_sems = refs[n + 1], refs[n + 2]
        token = refs[-1]
        px, py, pc = _mesh_pos()
        me = 4 * px + 2 * py + pc
        peers = [(px, py, 1 - pc), (1 - px, py, pc), (px, 1 - py, pc), (1 - px, 1 - py, pc)]
        for k, peer in enumerate(peers):
            for off, rows, land_ref in zip(offs, rows_list, land):
                pltpu.make_async_remote_copy(
                    src_ref=packed_ref.at[pl.ds(off, rows), :], dst_ref=land_ref.at[pl.ds(me * rows, rows), :],
                    send_sem=send_sems.at[k], recv_sem=recv_sems.at[k],
                    device_id=peer, device_id_type=pl.DeviceIdType.MESH).start()
        token[...] = jnp.zeros_like(token)

    outs = pl.pallas_call(
        body, name=name,
        out_shape=(pltpu.SemaphoreType.DMA((4,)), pltpu.SemaphoreType.DMA((4,)), pltpu.HBM(packed.shape, BF16),
                   *[pltpu.HBM(t.shape, BF16) for t in lands], jax.ShapeDtypeStruct((8, LANE), F32)),
        in_specs=(HBM_SPEC,) * (1 + n) + (ANY_SPEC,),
        out_specs=(SEM_SPEC, SEM_SPEC) + (HBM_SPEC,) * (1 + n) + (pl.BlockSpec(memory_space=pltpu.VMEM),),
        input_output_aliases={i: 2 + i for i in range(1 + n)},
        compiler_params=pltpu.CompilerParams(has_side_effects=EFFECT),
    )(_hbm(packed), *lands, after)
    return outs[0], outs[1], outs[2], list(outs[3:3 + n]), outs[-1]


def _gather_wait(send_sems, recv_sems, packed, lands, after, name):
    n = len(lands)

    def body(packed_ref, *refs):
        s_sems, r_sems = refs[n], refs[n + 1]
        me = _mesh_pos()
        for k in range(4):
            cp = _whole_wait(packed_ref, s_sems.at[k], r_sems.at[k], me)
            cp.wait_send()
            cp.wait_recv()

    outs = pl.pallas_call(
        body, name=name,
        out_shape=(pltpu.HBM(packed.shape, BF16), *[pltpu.HBM(t.shape, BF16) for t in lands]),
        in_specs=(HBM_SPEC,) * (1 + n) + (SEM_SPEC, SEM_SPEC, ANY_SPEC),
        out_specs=(HBM_SPEC,) * (1 + n),
        input_output_aliases={i: i for i in range(1 + n)},
        compiler_params=pltpu.CompilerParams(has_side_effects=EFFECT),
    )(packed, *lands, send_sems, recv_sems, after)
    return outs[0], list(outs[1:])


def _gather_finish(packed, rows_list, lands, name):
    n = len(rows_list)
    offs = _offsets(rows_list)

    def body(packed_ref, *refs):
        land = refs[n:2 * n]
        send_sems, recv_sems, stage, stage_sem = refs[2 * n:]
        px, py, pc = _mesh_pos()
        me = 4 * px + 2 * py + pc
        sibling = (px, py, 1 - pc)
        load = pltpu.make_async_copy(packed_ref, stage, stage_sem)
        load.start()
        for j, (cx, cy) in enumerate([(1 - px, py), (px, 1 - py), (1 - px, 1 - py)]):
            block = 4 * cx + 2 * cy + pc
            for rows, land_ref in zip(rows_list, land):
                blk = land_ref.at[pl.ds(block * rows, rows), :]
                pltpu.make_async_remote_copy(src_ref=blk, dst_ref=blk, send_sem=send_sems.at[j],
                                             recv_sem=recv_sems.at[j], device_id=sibling,
                                             device_id_type=pl.DeviceIdType.MESH).start()
        load.wait()
        for off, rows, land_ref in zip(offs, rows_list, land):
            pltpu.make_async_copy(stage.at[pl.ds(off, rows), :], land_ref.at[pl.ds(me * rows, rows), :],
                                  stage_sem).start()
        for j in range(3):
            cp = _whole_wait(packed_ref, send_sems.at[j], recv_sems.at[j], sibling)
            cp.wait_recv()
            cp.wait_send()
        pltpu.make_async_copy(stage, packed_ref, stage_sem).wait()

    outs = pl.pallas_call(
        body, name=name,
        out_shape=tuple(jax.ShapeDtypeStruct(t.shape, BF16) for t in lands),
        in_specs=(HBM_SPEC,) * (1 + n), out_specs=(HBM_SPEC,) * n,
        input_output_aliases={1 + i: i for i in range(n)},
        scratch_shapes=[pltpu.SemaphoreType.DMA((3,)), pltpu.SemaphoreType.DMA((3,)),
                        pltpu.VMEM(packed.shape, BF16), pltpu.SemaphoreType.DMA],
    )(packed, *lands)
    return list(outs)


N_CHIPS = 4


def _pair_start(srcs, rows_list, after, name):
    n = len(rows_list)
    offs = _offsets(rows_list)
    land = lax.empty((N_CHIPS, sum(rows_list), D_MODEL), BF16)

    def body(*refs):
        src, land_ref = refs[:n], refs[n]
        send_sems, recv_sems = refs[n + 2], refs[n + 3]
        token = refs[-1]
        px, py, pc = _mesh_pos()
        for k in range(N_CHIPS):
            block = 2 * k + (1 - pc)
            for off, rows, src_ref in zip(offs, rows_list, src):
                pltpu.make_async_remote_copy(
                    src_ref=src_ref.at[pl.ds(block * rows, rows), :], dst_ref=land_ref.at[k, pl.ds(off, rows), :],
                    send_sem=send_sems.at[0], recv_sem=recv_sems.at[0],
                    device_id=(px, py, 1 - pc), device_id_type=pl.DeviceIdType.MESH).start()
        token[...] = jnp.zeros_like(token)

    outs = pl.pallas_call(
        body, name=name,
        out_shape=(pltpu.SemaphoreType.DMA((1,)), pltpu.SemaphoreType.DMA((1,)),
                   *[pltpu.HBM(t.shape, BF16) for t in srcs], pltpu.HBM(land.shape, BF16),
                   jax.ShapeDtypeStruct((8, LANE), F32)),
        in_specs=(HBM_SPEC,) * (n + 1) + (ANY_SPEC,),
        out_specs=(SEM_SPEC, SEM_SPEC) + (HBM_SPEC,) * (n + 1) + (pl.BlockSpec(memory_space=pltpu.VMEM),),
        input_output_aliases={i: 2 + i for i in range(n + 1)},
        compiler_params=pltpu.CompilerParams(has_side_effects=EFFECT),
    )(*[_hbm(t) for t in srcs], _hbm(land), after)
    return outs[0], outs[1], list(outs[2:2 + n]), outs[2 + n], outs[-1]


def _split_wait(send_sems, recv_sems, n_sems, srcs, land, after, name):
    n = len(srcs)

    def body(*refs):
        land_ref = refs[n]
        s_sems, r_sems = refs[n + 1], refs[n + 2]
        me = _mesh_pos()
        for k in range(n_sems):
            cp = _whole_wait(land_ref.at[0] if n_sems > 1 else land_ref, s_sems.at[k], r_sems.at[k], me)
            cp.wait_send()
            cp.wait_recv()

    outs = pl.pallas_call(
        body, name=name,
        out_shape=(*[pltpu.HBM(t.shape, t.dtype) for t in srcs], pltpu.HBM(land.shape, land.dtype)),
        in_specs=(HBM_SPEC,) * (n + 1) + (SEM_SPEC, SEM_SPEC, ANY_SPEC),
        out_specs=(HBM_SPEC,) * (n + 1),
        input_output_aliases={i: i for i in range(n + 1)},
        compiler_params=pltpu.CompilerParams(has_side_effects=EFFECT),
    )(*srcs, land, send_sems, recv_sems, after)
    return list(outs[:n]), outs[n]


def _spread_start(x, me_id, after, name):
    land = lax.dynamic_update_slice_in_dim(lax.empty((N_DEV,) + x.shape, x.dtype), x[None], me_id, axis=0)

    def body(x_ref, land_ref, after_ref, send_sems, recv_sems, x_thru, land_thru, token):
        px, py, pc = _mesh_pos()
        me = 4 * px + 2 * py + pc
        for k in range(1, N_DEV):
            qx = 1 - px if k & 4 else px
            qy = 1 - py if k & 2 else py
            qc = 1 - pc if k & 1 else pc
            pltpu.make_async_remote_copy(
                src_ref=x_ref, dst_ref=land_ref.at[me], send_sem=send_sems.at[k - 1], recv_sem=recv_sems.at[k - 1],
                device_id=(qx, qy, qc), device_id_type=pl.DeviceIdType.MESH).start()
        token[...] = jnp.zeros_like(token)

    outs = pl.pallas_call(
        body, name=name,
        out_shape=(pltpu.SemaphoreType.DMA((N_DEV - 1,)), pltpu.SemaphoreType.DMA((N_DEV - 1,)),
                   pltpu.HBM(x.shape, x.dtype), pltpu.HBM(land.shape, land.dtype), jax.ShapeDtypeStruct((8, LANE), F32)),
        in_specs=(HBM_SPEC, HBM_SPEC, ANY_SPEC),
        out_specs=(SEM_SPEC, SEM_SPEC, HBM_SPEC, HBM_SPEC, pl.BlockSpec(memory_space=pltpu.VMEM)),
        input_output_aliases={0: 2, 1: 3},
        compiler_params=pltpu.CompilerParams(has_side_effects=EFFECT),
    )(_hbm(x), _hbm(land), after)
    return outs[0], outs[1], [outs[2]], outs[3], outs[4]


def _pair_sum(srcs, rows_list, land, core, name):
    n = len(rows_list)
    offs = _offsets(rows_list)
    total = sum(rows_list)

    def body(core_ref, *refs):
        src, land_ref, out_ref = refs[:n], refs[n], refs[n + 1]
        for off, rows, src_ref in zip(offs, rows_list, src):
            out_ref[pl.ds(off, rows), :] = (src_ref[...].astype(F32)
                                            + land_ref[pl.ds(off, rows), :].astype(F32)).astype(BF16)

    slot = pl.BlockSpec((None, total, D_MODEL), lambda k, c: (k, 0, 0))
    grid_spec = pltpu.PrefetchScalarGridSpec(
        num_scalar_prefetch=1, grid=(N_CHIPS,),
        in_specs=[pl.BlockSpec((rows, D_MODEL), lambda k, c: (2 * k + c[0], 0)) for rows in rows_list] + [slot],
        out_specs=slot)
    return pl.pallas_call(
        body, name=name, grid_spec=grid_spec,
        out_shape=jax.ShapeDtypeStruct((N_CHIPS, total, D_MODEL), BF16),
        compiler_params=_params(("parallel",)),
    )(core, *srcs, land)


def _chip_exchange_start(sums, chip, after, name):
    own = lax.dynamic_index_in_dim(sums, chip, axis=0, keepdims=True)
    recv = lax.dynamic_update_slice_in_dim(lax.empty(sums.shape, BF16), own, chip, axis=0)

    def body(sums_ref, recv_ref, after_ref, send_sems, recv_sems, sums_thru, recv_thru, token):
        px, py, pc = _mesh_pos()
        for k in range(1, N_CHIPS):
            qx = 1 - px if k & 2 else px
            qy = 1 - py if k & 1 else py
            pltpu.make_async_remote_copy(
                src_ref=sums_ref.at[2 * qx + qy], dst_ref=recv_ref.at[2 * px + py],
                send_sem=send_sems.at[k - 1], recv_sem=recv_sems.at[k - 1],
                device_id=(qx, qy, pc), device_id_type=pl.DeviceIdType.MESH).start()
        token[...] = jnp.zeros_like(token)

    outs = pl.pallas_call(
        body, name=name,
        out_shape=(pltpu.SemaphoreType.DMA((N_CHIPS - 1,)), pltpu.SemaphoreType.DMA((N_CHIPS - 1,)),
                   pltpu.HBM(sums.shape, BF16), pltpu.HBM(recv.shape, BF16), jax.ShapeDtypeStruct((8, LANE), F32)),
        in_specs=(HBM_SPEC, HBM_SPEC, ANY_SPEC),
        out_specs=(SEM_SPEC, SEM_SPEC, HBM_SPEC, HBM_SPEC, pl.BlockSpec(memory_space=pltpu.VMEM)),
        input_output_aliases={0: 2, 1: 3},
        compiler_params=pltpu.CompilerParams(has_side_effects=EFFECT),
    )(_hbm(sums), _hbm(recv), after)
    return outs[0], outs[1], [outs[2]], outs[3], outs[4]


def _sum_slots_into(recv, buf, layer, row_off, name):
    slots, r, n = recv.shape
    tr = _row_tile(math.gcd(r, row_off) if row_off else r, 512)
    first = row_off // tr

    def body(in_ref, buf_ref, out_ref):
        acc = in_ref[0].astype(F32)
        for j in range(1, slots):
            acc = acc + in_ref[j].astype(F32)
        out_ref[...] = acc

    return pl.pallas_call(
        body, name=name, grid=(r // tr,), out_shape=jax.ShapeDtypeStruct(buf.shape, F32),
        in_specs=[pl.BlockSpec((slots, tr, n), lambda i: (0, i, 0)), ANY_SPEC],
        out_specs=pl.BlockSpec((None, tr, n), lambda i: (layer, first + i, 0)),
        input_output_aliases={1: 0},
        compiler_params=_params(("parallel",)),
    )(recv, buf)


def _sum_slots(recv, name, after=None):
    _, r, n = recv.shape
    tr = _row_tile(r, 512)

    def body(in_ref, *refs):
        acc = in_ref[0].astype(F32)
        for j in range(1, N_DEV):
            acc = acc + in_ref[j].astype(F32)
        refs[-1][...] = acc

    grid = (r // tr,)
    in_specs, out_spec = [pl.BlockSpec((N_DEV, tr, n), lambda i: (0, i, 0))], pl.BlockSpec((tr, n), lambda i: (i, 0))
    args = [recv]
    if after is not None:
        in_specs.append(ANY_SPEC)
        args.append(after)
    return pl.pallas_call(
        body, name=name, grid=grid,
        out_shape=jax.ShapeDtypeStruct((r, n), F32),
        in_specs=in_specs, out_specs=out_spec,
        compiler_params=_params(("parallel",)),
    )(*args)


def _row_tile(rows, target):
    if rows <= target:
        return rows
    best = None
    for t in range(16, target + 1, 16):
        if rows % t == 0:
            best = t
    assert best is not None, rows
    return best


_DIMS = {"nn": ((1,), (0,)), "nt": ((1,), (1,)), "tn": ((0,), (0,))}


def _mm(a, b, mode, name, out_dtype=F32, res=None, gate=None, gate_factor=1.0, tm=512, tn=1408, after=None):
    assert (res is None) == (gate is None)
    if mode == "tn":
        kdim, m = a.shape
    else:
        m, kdim = a.shape
    n = b.shape[0] if mode == "nt" else b.shape[1]
    tm, tn = _tile(m, tm), _tile(n, tn)
    a_spec = (pl.BlockSpec((kdim, tm), lambda i, j: (0, i)) if mode == "tn"
              else pl.BlockSpec((tm, kdim), lambda i, j: (i, 0)))
    b_spec = (pl.BlockSpec((tn, kdim), lambda i, j: (j, 0)) if mode == "nt"
              else pl.BlockSpec((kdim, tn), lambda i, j: (0, j)))
    o_spec = pl.BlockSpec((tm, tn), lambda i, j: (i, j))
    dims = (_DIMS[mode], ((), ()))
    has_res = res is not None

    def body(a_ref, b_ref, *refs):
        y = lax.dot_general(a_ref[...].astype(BF16), b_ref[...].astype(BF16), dims,
                            preferred_element_type=F32)
        if has_res:
            res_ref, gate_ref = refs[0], refs[1]
            y_ref, o_ref = refs[-2], refs[-1]
            y_ref[...] = y.astype(BF16)
            o_ref[...] = res_ref[...] + (gate_factor * gate_ref[...]) * y
        else:
            refs[-1][...] = y.astype(out_dtype)

    in_specs, args = [a_spec, b_spec], [a, b]
    if has_res:
        gate_spec, gate_arg = _vec_in(gate, tile=tn)
        in_specs += [o_spec, gate_spec]
        args += [res, gate_arg]
        out_shape = (jax.ShapeDtypeStruct((m, n), BF16), jax.ShapeDtypeStruct((m, n), F32))
        out_specs = (o_spec, o_spec)
    else:
        out_shape, out_specs = jax.ShapeDtypeStruct((m, n), out_dtype), o_spec
    if after is not None:
        in_specs.append(ANY_SPEC)
        args.append(after)
    return pl.pallas_call(
        body, name=name, grid=(m // tm, n // tn), out_shape=out_shape,
        in_specs=in_specs, out_specs=out_specs,
        compiler_params=_params(("parallel", "parallel")),
    )(*args)


def _vec_in(v, tile=None):
    if isinstance(v, tuple):
        table, row = v
        if tile is None:
            return pl.BlockSpec((None, 1, table.shape[-1]), lambda *idx: (row, 0, 0)), table
        return pl.BlockSpec((None, 1, tile), lambda i, j: (row, 0, j)), table
    if tile is None:
        return pl.BlockSpec((1, v.shape[-1]), lambda *idx: (0, 0)), v
    return pl.BlockSpec((1, tile), lambda i, j: (0, j)), v


def _vec_spec(width):
    return pl.BlockSpec((1, width), lambda i: (0, 0))


def _rm_bwd(dh, x, dres, gw, scale, name, below=None):
    s, d = x.shape
    ts = _tile(s, 512)
    factor = None if below is None else below[2]

    def body(dh_ref, x_ref, dres_ref, gw_ref, sc_ref, *refs):
        dx_ref, dsh_ref, dsc_ref, dgw_ref = refs[-6:-2] if below is not None else refs[-4:]

        @pl.when(pl.program_id(0) == 0)
        def _():
            dsh_ref[...] = jnp.zeros_like(dsh_ref)
            dsc_ref[...] = jnp.zeros_like(dsc_ref)
            dgw_ref[...] = jnp.zeros_like(dgw_ref)
            if below is not None:
                refs[-1][...] = jnp.zeros_like(refs[-1])

        xv, dhv, gwv = x_ref[...], dh_ref[...], gw_ref[...]
        r = lax.rsqrt(jnp.mean(xv * xv, axis=-1, keepdims=True) + EPS)
        xn = xv * r
        y = xn * gwv
        dsh_ref[...] += jnp.sum(dhv, axis=0, keepdims=True)
        dsc_ref[...] += jnp.sum(dhv * y, axis=0, keepdims=True)
        dy = dhv * (1 + sc_ref[...])
        dgw_ref[...] += jnp.sum(dy * xn, axis=0, keepdims=True)
        dxn = dy * gwv
        dx = dres_ref[...] + r * (dxn - xn * jnp.mean(dxn * xn, axis=-1, keepdims=True))
        dx_ref[...] = dx
        if below is not None:
            yb_ref, gb_ref, dyb_ref, dgb_ref = refs[0], refs[1], refs[-2], refs[-1]
            dyb_ref[...] = ((factor * gb_ref[...]) * dx).astype(BF16)
            dgb_ref[...] += jnp.sum((factor * dx) * yb_ref[...].astype(F32), axis=0, keepdims=True)

    row = pl.BlockSpec((ts, d), lambda i: (i, 0))
    vec = jax.ShapeDtypeStruct((1, d), F32)
    (gw_spec, gw), (sc_spec, scale) = _vec_in(gw), _vec_in(scale)
    in_specs, args = [row, row, row, gw_spec, sc_spec], [dh, x, dres, gw, scale]
    out_shape = [jax.ShapeDtypeStruct((s, d), F32), vec, vec, vec]
    out_specs = [row, _vec_spec(d), _vec_spec(d), _vec_spec(d)]
    if below is not None:
        gate_spec, gate_arg = _vec_in(below[1])
        in_specs += [row, gate_spec]
        args += [below[0], gate_arg]
        out_shape += [jax.ShapeDtypeStruct((s, d), BF16), vec]
        out_specs += [row, _vec_spec(d)]
    return pl.pallas_call(
        body, name=name, grid=(s // ts,), out_shape=tuple(out_shape),
        in_specs=in_specs, out_specs=tuple(out_specs),
        compiler_params=_params(("arbitrary",)),
    )(*args)


def _norm_mm(x, gw, shift, scale, w, name, tm=1024):
    s, d = x.shape
    n = w.shape[0]
    tm = _tile(s, tm)

    def body(x_ref, gw_ref, sh_ref, sc_ref, w_ref, h_ref, z_ref):
        xv = x_ref[...]
        r = lax.rsqrt(jnp.mean(xv * xv, axis=-1, keepdims=True) + EPS)
        hb = (((xv * r) * gw_ref[...]) * (1 + sc_ref[...]) + sh_ref[...]).astype(BF16)
        h_ref[...] = hb
        z_ref[...] = lax.dot_general(hb, w_ref[...], (((1,), (1,)), ((), ())), preferred_element_type=F32)

    row = pl.BlockSpec((tm, d), lambda i: (i, 0))
    return pl.pallas_call(
        body, name=name, grid=(s // tm,),
        out_shape=(jax.ShapeDtypeStruct((s, d), BF16), jax.ShapeDtypeStruct((s, n), F32)),
        in_specs=[row, _vec_in(gw)[0], _vec_in(shift)[0], _vec_in(scale)[0], pl.BlockSpec((n, d), lambda i: (0, 0))],
        out_specs=(row, pl.BlockSpec((tm, n), lambda i: (i, 0))),
        compiler_params=_params(("parallel",)),
    )(x, _vec_in(gw)[1], _vec_in(shift)[1], _vec_in(scale)[1], w)


FFN_TM, FFN_TF = 2048, 256


def _ffn_up(x, gw, shift, scale, wg, wu, name, after=None):
    s, d = x.shape
    f = wg.shape[0]
    tm, tf = _tile(s, FFN_TM), _tile(f, FFN_TF)
    nt = (((1,), (1,)), ((), ()))

    def body(x_ref, gw_ref, sh_ref, sc_ref, wg_ref, wu_ref, *refs):
        h_ref, a_ref, b_ref, t_ref = refs[-4:]

        @pl.when(pl.program_id(1) == 0)
        def _():
            xv = x_ref[...]
            r = lax.rsqrt(jnp.mean(xv * xv, axis=-1, keepdims=True) + EPS)
            h_ref[...] = (((xv * r) * gw_ref[...]) * (1 + sc_ref[...]) + sh_ref[...]).astype(BF16)

        hb = h_ref[...]
        av = lax.dot_general(hb, wg_ref[...], nt, preferred_element_type=F32)
        bv = lax.dot_general(hb, wu_ref[...], nt, preferred_element_type=F32)
        a_ref[...] = av.astype(BF16)
        b_ref[...] = bv.astype(BF16)
        t_ref[...] = ((av * jax.nn.sigmoid(av)) * bv).astype(BF16)

    row = pl.BlockSpec((tm, d), lambda i, j: (i, 0))
    wblk = pl.BlockSpec((tf, d), lambda i, j: (j, 0))
    blk = pl.BlockSpec((tm, tf), lambda i, j: (i, j))
    wide = jax.ShapeDtypeStruct((s, f), BF16)
    vec_specs, vec_args = zip(*[_vec_in(v) for v in (gw, shift, scale)])
    in_specs, args = [row, *vec_specs, wblk, wblk], [x, *vec_args, wg, wu]
    if after is not None:
        in_specs.append(ANY_SPEC)
        args.append(after)
    return pl.pallas_call(
        body, name=name, grid=(s // tm, f // tf),
        out_shape=(jax.ShapeDtypeStruct((s, d), BF16), wide, wide, wide),
        in_specs=in_specs, out_specs=(row, blk, blk, blk),
        compiler_params=_params(("parallel", "arbitrary")),
    )(*args)


def _ffn_bwd_cols(dy, h, a, b, t, wd, name, after=None):
    s, d = dy.shape
    f = wd.shape[0]
    tf = _tile(f, FFN_TF)
    nt = (((1,), (1,)), ((), ()))
    tn = (((0,), (0,)), ((), ()))

    def body(dy_ref, h_ref, a_ref, b_ref, t_ref, wd_ref, *refs):
        da_ref, db_ref, gd_ref, gg_ref, gu_ref = refs[-5:]
        dyb, hb = dy_ref[...], h_ref[...]
        dtv = lax.dot_general(dyb, wd_ref[...], nt, preferred_element_type=F32)
        av, bv = a_ref[...].astype(F32), b_ref[...].astype(F32)
        sg = jax.nn.sigmoid(av)
        dbv = (dtv * (av * sg)).astype(BF16)
        dav = ((dtv * bv) * (sg * (1 + av * (1 - sg)))).astype(BF16)
        da_ref[...] = dav
        db_ref[...] = dbv
        gd_ref[...] = lax.dot_general(t_ref[...], dyb, tn, preferred_element_type=F32).astype(BF16)
        gg_ref[...] = lax.dot_general(dav, hb, tn, preferred_element_type=F32).astype(BF16)
        gu_ref[...] = lax.dot_general(dbv, hb, tn, preferred_element_type=F32).astype(BF16)

    whole = pl.BlockSpec((s, d), lambda j: (0, 0))
    col = pl.BlockSpec((s, tf), lambda j: (0, j))
    wblk = pl.BlockSpec((tf, d), lambda j: (j, 0))
    wide, wgrad = jax.ShapeDtypeStruct((s, f), BF16), jax.ShapeDtypeStruct((f, d), BF16)
    in_specs, args = [whole, whole, col, col, col, wblk], [dy, h, a, b, t, wd]
    if after is not None:
        in_specs.append(ANY_SPEC)
        args.append(after)
    return pl.pallas_call(
        body, name=name, grid=(f // tf,), out_shape=(wide, wide, wgrad, wgrad, wgrad),
        in_specs=in_specs, out_specs=(col, col, wblk, wblk, wblk),
        compiler_params=_params(("parallel",)),
    )(*args)


def _mm_pair(a1, b1, a2, b2, name, tm=1024, tn=512, after=None):
    m, kdim = a1.shape
    n = b1.shape[1]
    tm, tn = _tile(m, tm), _tile(n, tn)

    def body(a1_ref, b1_ref, a2_ref, b2_ref, *refs):
        refs[-1][...] = (jnp.dot(a1_ref[...], b1_ref[...], preferred_element_type=F32)
                         + jnp.dot(a2_ref[...], b2_ref[...], preferred_element_type=F32))

    a_spec = pl.BlockSpec((tm, kdim), lambda i, j: (i, 0))
    b_spec = pl.BlockSpec((kdim, tn), lambda i, j: (0, j))
    in_specs, args = [a_spec, b_spec, a_spec, b_spec], [a1, b1, a2, b2]
    if after is not None:
        in_specs.append(ANY_SPEC)
        args.append(after)
    return pl.pallas_call(
        body, name=name, grid=(m // tm, n // tn), out_shape=jax.ShapeDtypeStruct((m, n), F32),
        in_specs=in_specs, out_specs=pl.BlockSpec((tm, tn), lambda i, j: (i, j)),
        compiler_params=_params(("parallel", "parallel")),
    )(*args)


def _pool_counts(s):
    return (lax.broadcasted_iota(jnp.int32, (s, POOL_GC), 0))


def _pool_fwd(z, pool_w, pool_scale, name):
    s = z.shape[0]

    def body(u_ref, w_ref, sc_ref, y_ref, diff_ref):
        t = lax.broadcasted_iota(jnp.int32, (s, POOL_GC), 0)
        for g, win in enumerate(POOL_WINDOWS):
            cols = slice(g * POOL_GC, (g + 1) * POOL_GC)
            u = u_ref[:, cols]
            acc, step = u, 1
            while step < win:
                acc = acc + jnp.where(t >= step, pltpu.roll(acc, step, 0), 0.0)
                step *= 2
            cnt = jnp.minimum(t + 1, win).astype(F32)
            diff = acc / cnt - u
            diff_ref[:, cols] = diff
            ypre = jnp.dot(diff.astype(BF16), w_ref[g].astype(BF16), preferred_element_type=F32)
            y_ref[:, cols] = (ypre * sc_ref[:, cols]).astype(BF16)

    return pl.pallas_call(
        body, name=name, grid=(1,),
        out_shape=(jax.ShapeDtypeStruct((s, POOL_WIDTH), BF16), jax.ShapeDtypeStruct((s, POOL_WIDTH), F32)),
        in_specs=[pl.BlockSpec((s, POOL_WIDTH), lambda i: (0, 0)),
                  pl.BlockSpec(pool_w.shape, lambda i: (0, 0, 0)),
                  pl.BlockSpec((1, POOL_WIDTH), lambda i: (0, 0))],
        out_specs=(pl.BlockSpec((s, POOL_WIDTH), lambda i: (0, 0)),
                   pl.BlockSpec((s, POOL_WIDTH), lambda i: (0, 0))),
        compiler_params=_params(("arbitrary",)),
    )(z, pool_w, pool_scale)


def _pool_bwd(dycat, diff, pool_w, pool_scale, name):
    s = diff.shape[0]

    def body(dy_ref, diff_ref, w_ref, sc_ref, du_ref, dw_ref, dsc_ref):
        t = lax.broadcasted_iota(jnp.int32, (s, POOL_GC), 0)
        for g, win in enumerate(POOL_WINDOWS):
            cols = slice(g * POOL_GC, (g + 1) * POOL_GC)
            dy, dfb, wb = dy_ref[:, cols], diff_ref[:, cols].astype(BF16), w_ref[g].astype(BF16)
            ypre = jnp.dot(dfb, wb, preferred_element_type=F32)
            dsc_ref[:, cols] = jnp.sum(dy * ypre, axis=0, keepdims=True)
            dypre = (dy * sc_ref[:, cols]).astype(BF16)
            ddiff = lax.dot_general(dypre, wb, (((1,), (1,)), ((), ())), preferred_element_type=F32)
            dw_ref[g] = lax.dot_general(dfb, dypre, (((0,), (0,)), ((), ())), preferred_element_type=F32)
            cnt = jnp.minimum(t + 1, win).astype(F32)
            acc, step = ddiff / cnt, 1
            while step < win:
                acc = acc + jnp.where(t < s - step, pltpu.roll(acc, s - step, 0), 0.0)
                step *= 2
            du_ref[:, cols] = acc - ddiff

    full = pl.BlockSpec((s, POOL_WIDTH), lambda i: (0, 0))
    return pl.pallas_call(
        body, name=name, grid=(1,),
        out_shape=(jax.ShapeDtypeStruct((s, POOL_WIDTH), F32),
                   jax.ShapeDtypeStruct(pool_w.shape, F32),
                   jax.ShapeDtypeStruct((1, POOL_WIDTH), F32)),
        in_specs=[full, full, pl.BlockSpec(pool_w.shape, lambda i: (0, 0, 0)),
                  pl.BlockSpec((1, POOL_WIDTH), lambda i: (0, 0))],
        out_specs=(full, pl.BlockSpec(pool_w.shape, lambda i: (0, 0, 0)),
                   pl.BlockSpec((1, POOL_WIDTH), lambda i: (0, 0))),
        compiler_params=_params(("arbitrary",)),
    )(dycat, diff, pool_w, pool_scale)


def _rope_tables(positions, name):
    s = positions.shape[0]
    ts = _tile(s, 512)
    freq = 1.0 / (ROPE_THETA ** (np.arange(0, QK_ROPE, 2, dtype=np.float32) / QK_ROPE))
    table = np.zeros((1, LANE), np.float32)
    table[0, :QK_ROPE // 2] = freq
    table[0, QK_ROPE // 2:QK_ROPE] = freq

    def body(pos_ref, f_ref, cos_ref, sin_ref):
        ang = pos_ref[...].astype(F32) * f_ref[...]
        cos_ref[...] = jnp.cos(ang)
        sin_ref[...] = jnp.sin(ang)

    out = jax.ShapeDtypeStruct((s, LANE), F32)
    blk = pl.BlockSpec((ts, LANE), lambda i: (i, 0))
    return pl.pallas_call(
        body, name=name, grid=(s // ts,), out_shape=(out, out),
        in_specs=[pl.BlockSpec((ts, 1), lambda i: (i, 0)), _vec_spec(LANE)], out_specs=(blk, blk),
        compiler_params=_params(("parallel",)),
    )(positions, jnp.asarray(table))


def _lane_mod64_low(shape):
    return (lax.broadcasted_iota(jnp.int32, shape, 1) % QK_ROPE) < (QK_ROPE // 2)


def _rope(x, cos, sin):
    rot = jnp.where(_lane_mod64_low(x.shape), -pltpu.roll(x, LANE - 32, 1), pltpu.roll(x, 32, 1))
    return x * cos + rot * sin


def _rope_t(dy, cos, sin):
    w = dy * sin
    rot_t = jnp.where(_lane_mod64_low(dy.shape), pltpu.roll(w, LANE - 32, 1), -pltpu.roll(w, 32, 1))
    return dy * cos + rot_t


def _plain_rms(x, g):
    r = lax.rsqrt(jnp.mean(x * x, axis=-1, keepdims=True) + EPS)
    return (x * r) * g, x * r, r


O_Q, O_KV, O_KR = POOL_WIDTH, POOL_WIDTH + Q_LORA, POOL_WIDTH + Q_LORA + KV_LORA


def _qkv_fwd(z, qn, kvn, wq, wkv, cos, sin, name):
    s = z.shape[0]
    ts = _tile(s, 512)

    def body(z_ref, qn_ref, kvn_ref, wq_ref, wkv_ref, cos_ref, sin_ref, q_ref, k_ref, v_ref, cqn_ref, ckvn_ref):
        cosv, sinv = cos_ref[...], sin_ref[...]
        cqn = _plain_rms(z_ref[:, O_Q:O_KV], qn_ref[...])[0].astype(BF16)
        ckvn = _plain_rms(z_ref[:, O_KV:O_KR], kvn_ref[...])[0].astype(BF16)
        cqn_ref[...] = cqn
        ckvn_ref[...] = ckvn
        nt = (((1,), (1,)), ((), ()))
        q = lax.dot_general(cqn, wq_ref[...], nt, preferred_element_type=F32)
        kv = lax.dot_general(ckvn, wkv_ref[...], nt, preferred_element_type=F32)
        kr = _rope(z_ref[:, O_KR:IN_PAD], cosv, sinv).astype(BF16)
        for h in range(N_HEADS):
            o = h * HEAD_PAD
            q_ref[:, o:o + QK_NOPE] = q[:, o:o + QK_NOPE].astype(BF16)
            q_ref[:, o + QK_NOPE:o + HEAD_PAD] = _rope(q[:, o + QK_NOPE:o + HEAD_PAD], cosv, sinv).astype(BF16)
            k_ref[:, o:o + QK_NOPE] = kv[:, o:o + QK_NOPE].astype(BF16)
            k_ref[:, o + QK_NOPE:o + HEAD_PAD] = kr
            v_ref[:, h * V_HEAD:(h + 1) * V_HEAD] = kv[:, o + QK_NOPE:o + HEAD_PAD].astype(BF16)

    def row(w):
        return pl.BlockSpec((ts, w), lambda i: (i, 0))

    def whole(arr):
        return pl.BlockSpec(arr.shape, lambda i: (0, 0))

    hp = N_HEADS * HEAD_PAD
    return pl.pallas_call(
        body, name=name, grid=(s // ts,),
        out_shape=(jax.ShapeDtypeStruct((s, hp), BF16), jax.ShapeDtypeStruct((s, hp), BF16),
                   jax.ShapeDtypeStruct((s, N_HEADS * V_HEAD), BF16),
                   jax.ShapeDtypeStruct((s, Q_LORA), BF16), jax.ShapeDtypeStruct((s, KV_LORA), BF16)),
        in_specs=[row(IN_PAD), whole(qn), whole(kvn), whole(wq), whole(wkv), row(LANE), row(LANE)],
        out_specs=(row(hp), row(hp), row(N_HEADS * V_HEAD), row(Q_LORA), row(KV_LORA)),
        compiler_params=_params(("parallel",)),
    )(z, qn, kvn, wq, wkv, cos, sin)


def _qkv_bwd(dq, dk, dv, du, z, qn, kvn, wq, wkv, cos, sin, name):
    s = z.shape[0]
    ts = _tile(s, 512)

    def norm_bwd(x, g, dy):
        _, xn, r = _plain_rms(x, g)
        dxn = dy * g
        return r * (dxn - xn * jnp.mean(dxn * xn, axis=-1, keepdims=True)), jnp.sum(dy * xn, axis=0, keepdims=True)

    def body(dq_ref, dk_ref, dv_ref, du_ref, z_ref, qn_ref, kvn_ref, wq_ref, wkv_ref, cos_ref, sin_ref,
             dz_ref, dqb_ref, dkvb_ref, dqn_ref, dkvn_ref):
        @pl.when(pl.program_id(0) == 0)
        def _():
            dqn_ref[...] = jnp.zeros_like(dqn_ref)
            dkvn_ref[...] = jnp.zeros_like(dkvn_ref)

        cosv, sinv = cos_ref[...], sin_ref[...]
        dkr = jnp.zeros((ts, LANE), F32)
        for h in range(N_HEADS):
            o = h * HEAD_PAD
            dqb_ref[:, o:o + QK_NOPE] = dq_ref[:, o:o + QK_NOPE].astype(BF16)
            dqb_ref[:, o + QK_NOPE:o + HEAD_PAD] = _rope_t(dq_ref[:, o + QK_NOPE:o + HEAD_PAD], cosv, sinv).astype(BF16)
            dkvb_ref[:, o:o + QK_NOPE] = dk_ref[:, o:o + QK_NOPE].astype(BF16)
            dkvb_ref[:, o + QK_NOPE:o + HEAD_PAD] = dv_ref[:, h * V_HEAD:(h + 1) * V_HEAD].astype(BF16)
            dkr = dkr + dk_ref[:, o + QK_NOPE:o + HEAD_PAD]
        dcqn = jnp.dot(dqb_ref[...], wq_ref[...], preferred_element_type=F32)
        dckvn = jnp.dot(dkvb_ref[...], wkv_ref[...], preferred_element_type=F32)
        dcq, dqn = norm_bwd(z_ref[:, O_Q:O_KV], qn_ref[...], dcqn)
        dckv, dkvn = norm_bwd(z_ref[:, O_KV:O_KR], kvn_ref[...], dckvn)
        dqn_ref[...] += dqn
        dkvn_ref[...] += dkvn
        dz_ref[:, 0:O_Q] = du_ref[...].astype(BF16)
        dz_ref[:, O_Q:O_KV] = dcq.astype(BF16)
        dz_ref[:, O_KV:O_KR] = dckv.astype(BF16)
        dz_ref[:, O_KR:IN_PAD] = _rope_t(dkr, cosv, sinv).astype(BF16)

    def row(w):
        return pl.BlockSpec((ts, w), lambda i: (i, 0))

    def whole(arr):
        return pl.BlockSpec(arr.shape, lambda i: (0, 0))

    hp = N_HEADS * HEAD_PAD
    return pl.pallas_call(
        body, name=name, grid=(s // ts,),
        out_shape=(jax.ShapeDtypeStruct((s, IN_PAD), BF16), jax.ShapeDtypeStruct((s, hp), BF16),
                   jax.ShapeDtypeStruct((s, hp), BF16),
                   jax.ShapeDtypeStruct((1, Q_LORA), F32), jax.ShapeDtypeStruct((1, KV_LORA), F32)),
        in_specs=[row(hp), row(hp), row(N_HEADS * V_HEAD), row(POOL_WIDTH), row(IN_PAD),
                  whole(qn), whole(kvn), whole(wq), whole(wkv), row(LANE), row(LANE)],
        out_specs=(row(IN_PAD), row(hp), row(hp), whole(qn), whole(kvn)),
        compiler_params=_params(("arbitrary",)),
    )(dq, dk, dv, du, z, qn, kvn, wq, wkv, cos, sin)


def _causal_scores(q, k, i, tq, klen):
    sc = lax.dot_general(q, k, (((1,), (1,)), ((), ())), preferred_element_type=F32) * SOFTMAX_SCALE
    qpos = i * tq + lax.broadcasted_iota(jnp.int32, (tq, klen), 0)
    kpos = lax.broadcasted_iota(jnp.int32, (tq, klen), 1)
    return jnp.where(qpos >= kpos, sc, -jnp.inf)


ATTN_TQ = 512
ATTN_SEGMENTS = 4


def _by_key_prefix(i, nq, tq, compute):
    nseg = min(ATTN_SEGMENTS, nq)
    per = nq // nseg
    for r in range(nseg):
        pl.when(i // per == r)(lambda r=r: compute((r + 1) * per * tq))


def _attn_fwd(q, k, v, name):
    s = q.shape[0]
    tq = _tile(s, ATTN_TQ)
    nq = s // tq

    def body(q_ref, k_ref, v_ref, o_ref, lse_ref):
        i = pl.program_id(1)

        def compute(klen):
            sc = _causal_scores(q_ref[...], k_ref[0:klen, :], i, tq, klen)
            mx = jnp.max(sc, axis=-1, keepdims=True)
            p = jnp.exp(sc - mx)
            den = jnp.sum(p, axis=-1, keepdims=True)
            o_ref[...] = jnp.dot((p * (1.0 / den)).astype(BF16), v_ref[0:klen, :], preferred_element_type=F32)
            lse_ref[...] = mx + jnp.log(den)

        _by_key_prefix(i, nq, tq, compute)

    return pl.pallas_call(
        body, name=name, grid=(N_HEADS, s // tq),
        out_shape=(jax.ShapeDtypeStruct((s, N_HEADS * V_HEAD), F32), jax.ShapeDtypeStruct((N_HEADS, s, 1), F32)),
        in_specs=[pl.BlockSpec((tq, HEAD_PAD), lambda h, i: (i, h)),
                  pl.BlockSpec((s, HEAD_PAD), lambda h, i: (0, h)),
                  pl.BlockSpec((s, V_HEAD), lambda h, i: (0, h))],
        out_specs=(pl.BlockSpec((tq, V_HEAD), lambda h, i: (i, h)),
                   pl.BlockSpec((None, tq, 1), lambda h, i: (h, i, 0))),
        compiler_params=_params(("parallel", "parallel")),
    )(q, k, v)


def _attn_bwd(q, k, v, lse, dycat, name):
    s = q.shape[0]
    tq = _tile(s, ATTN_TQ)
    nq = s // tq
    tn_dims = (((0,), (0,)), ((), ()))

    def body(q_ref, k_ref, v_ref, lse_ref, do_ref, dq_ref, dk_ref, dv_ref):
        i = pl.program_id(1)

        @pl.when(i == 0)
        def _():
            dk_ref[...] = jnp.zeros_like(dk_ref)
            dv_ref[...] = jnp.zeros_like(dv_ref)

        def compute(klen):
            qv, kv_, dob = q_ref[...], k_ref[0:klen, :], do_ref[...].astype(BF16)
            sc = _causal_scores(qv, kv_, i, tq, klen)
            p = jnp.exp(sc - lse_ref[...])
            dp = lax.dot_general(dob, v_ref[0:klen, :], (((1,), (1,)), ((), ())), preferred_element_type=F32)
            ds = (p * (dp - jnp.sum(dp * p, axis=-1, keepdims=True)) * SOFTMAX_SCALE).astype(BF16)
            dq_ref[...] = jnp.dot(ds, kv_, preferred_element_type=F32)
            dk_ref[0:klen, :] += lax.dot_general(ds, qv, tn_dims, preferred_element_type=F32)
            dv_ref[0:klen, :] += lax.dot_general(p.astype(BF16), dob, tn_dims, preferred_element_type=F32)

        _by_key_prefix(i, nq, tq, compute)

    n_pool_blocks = POOL_WIDTH // V_HEAD
    return pl.pallas_call(
        body, name=name, grid=(N_HEADS, s // tq),
        out_shape=(jax.ShapeDtypeStruct((s, N_HEADS * HEAD_PAD), F32),
                   jax.ShapeDtypeStruct((s, N_HEADS * HEAD_PAD), F32),
                   jax.ShapeDtypeStruct((s, N_HEADS * V_HEAD), F32)),
        in_specs=[pl.BlockSpec((tq, HEAD_PAD), lambda h, i: (i, h)),
                  pl.BlockSpec((s, HEAD_PAD), lambda h, i: (0, h)),
                  pl.BlockSpec((s, V_HEAD), lambda h, i: (0, h)),
                  pl.BlockSpec((None, tq, 1), lambda h, i: (h, i, 0)),
                  pl.BlockSpec((tq, V_HEAD), lambda h, i: (i, n_pool_blocks + h))],
        out_specs=(pl.BlockSpec((tq, HEAD_PAD), lambda h, i: (i, h)),
                   pl.BlockSpec((s, HEAD_PAD), lambda h, i: (0, h)),
                   pl.BlockSpec((s, V_HEAD), lambda h, i: (0, h))),
        compiler_params=_params(("parallel", "arbitrary")),
    )(q, k, v, lse, dycat)


def _loss_head(x, gw, target, below, name):
    s, d = x.shape
    ts = _tile(s, 512)
    factor = below[2]

    def body(x_ref, gw_ref, tgt_ref, yb_ref, gb_ref, loss_ref, dx_ref, dgw_ref, dyb_ref, dgb_ref):
        @pl.when(pl.program_id(0) == 0)
        def _():
            loss_ref[...] = jnp.zeros_like(loss_ref)
            dgw_ref[...] = jnp.zeros_like(dgw_ref)
            dgb_ref[...] = jnp.zeros_like(dgb_ref)

        xv, gwv = x_ref[...], gw_ref[...]
        r = lax.rsqrt(jnp.mean(xv * xv, axis=-1, keepdims=True) + EPS)
        xn = xv * r
        err = xn * gwv - tgt_ref[...]
        loss_ref[...] += 0.5 * jnp.sum(jnp.mean(err * err, axis=-1, keepdims=True))
        dy = err / d
        dgw_ref[...] += jnp.sum(dy * xn, axis=0, keepdims=True)
        dxn = dy * gwv
        dx = r * (dxn - xn * jnp.mean(dxn * xn, axis=-1, keepdims=True))
        dx_ref[...] = dx
        dyb_ref[...] = ((factor * gb_ref[...]) * dx).astype(BF16)
        dgb_ref[...] += jnp.sum((factor * dx) * yb_ref[...].astype(F32), axis=0, keepdims=True)

    row = pl.BlockSpec((ts, d), lambda i: (i, 0))
    gate_spec, gate_arg = _vec_in(below[1])
    vec = jax.ShapeDtypeStruct((1, d), F32)
    return pl.pallas_call(
        body, name=name, grid=(s // ts,),
        out_shape=(jax.ShapeDtypeStruct((8, LANE), F32), jax.ShapeDtypeStruct((s, d), F32), vec,
                   jax.ShapeDtypeStruct((s, d), BF16), vec),
        in_specs=[row, _vec_spec(d), row, row, gate_spec],
        out_specs=(pl.BlockSpec((8, LANE), lambda i: (0, 0)), row, _vec_spec(d), row, _vec_spec(d)),
        compiler_params=_params(("arbitrary",)),
    )(x, gw, target, below[0], gate_arg)


def _ada_mod(c_all, ada_w, ada_b, name):
    nl, d, cols = ada_w.shape

    def body(c_ref, w_ref, b_ref, o_ref):
        cv = c_ref[...]
        act = (cv * jax.nn.sigmoid(cv)).astype(BF16)
        o_ref[...] = jnp.dot(act, w_ref[...].astype(BF16), preferred_element_type=F32) + b_ref[...]

    return pl.pallas_call(
        body, name=name, grid=(nl,), out_shape=jax.ShapeDtypeStruct((nl, N_DEV, cols), F32),
        in_specs=[pl.BlockSpec((N_DEV, d), lambda l: (0, 0)),
                  pl.BlockSpec((None, d, cols), lambda l: (l, 0, 0)),
                  pl.BlockSpec((None, 1, cols), lambda l: (l, 0, 0))],
        out_specs=pl.BlockSpec((None, N_DEV, cols), lambda l: (l, 0, 0)),
        compiler_params=_params(("parallel",)),
    )(c_all, ada_w, ada_b)


def _ada_grad(c_pad, dmod_pad, name):
    nl, kpad, cols = dmod_pad.shape
    d = c_pad.shape[1]

    def body(c_ref, dm_ref, o_ref):
        cv = c_ref[...]
        act = (cv * jax.nn.sigmoid(cv)).astype(BF16)
        o_ref[...] = lax.dot_general(act, dm_ref[...].astype(BF16), (((0,), (0,)), ((), ())),
                                     preferred_element_type=F32)

    return pl.pallas_call(
        body, name=name, grid=(nl,), out_shape=jax.ShapeDtypeStruct((nl, d, cols), F32),
        in_specs=[pl.BlockSpec((kpad, d), lambda l: (0, 0)),
                  pl.BlockSpec((None, kpad, cols), lambda l: (l, 0, 0))],
        out_specs=pl.BlockSpec((None, d, cols), lambda l: (l, 0, 0)),
        compiler_params=_params(("parallel",)),
    )(c_pad, dmod_pad)


def _adamw_math(w, g, m, v):
    nm = ADAM_B1 * m + (1.0 - ADAM_B1) * g
    nv = ADAM_B2 * v + (1.0 - ADAM_B2) * (g * g)
    m_hat = nm / (1.0 - ADAM_B1 ** ADAM_STEP)
    v_hat = nv / (1.0 - ADAM_B2 ** ADAM_STEP)
    return -ADAM_LR * (m_hat / (jnp.sqrt(v_hat) + ADAM_EPS) + ADAM_WD * w), nm, nv


def _adamw_rows(w3, gbuf, row_off, m3, v3, name):
    nl, r, d = w3.shape
    tr = _row_tile(math.gcd(r, row_off) if row_off else r, 352)
    first = row_off // tr

    def body(w_ref, g_ref, m_ref, v_ref, go_ref, d_ref, nm_ref, nv_ref):
        gv = g_ref[...]
        go_ref[...] = gv
        d_ref[...], nm_ref[...], nv_ref[...] = _adamw_math(w_ref[...], gv, m_ref[...], v_ref[...])

    blk = pl.BlockSpec((None, tr, d), lambda l, i: (l, i, 0))
    gblk = pl.BlockSpec((None, tr, d), lambda l, i: (l, first + i, 0))
    out = jax.ShapeDtypeStruct((nl, r, d), F32)
    return pl.pallas_call(
        body, name=name, grid=(nl, r // tr), out_shape=(out, out, out, out),
        in_specs=[blk, gblk, blk, blk], out_specs=(blk, blk, blk, blk),
        compiler_params=_params(("parallel", "parallel")),
    )(w3, gbuf, m3, v3)


def _adamw(w, g, m, v, name):
    rows, cols = w.shape
    tr = _row_tile(rows, 512)

    def body(w_ref, g_ref, m_ref, v_ref, d_ref, nm_ref, nv_ref):
        d_ref[...], nm_ref[...], nv_ref[...] = _adamw_math(w_ref[...], g_ref[...], m_ref[...], v_ref[...])

    blk = pl.BlockSpec((tr, cols), lambda i: (i, 0))
    out = jax.ShapeDtypeStruct((rows, cols), F32)
    return pl.pallas_call(
        body, name=name, grid=(rows // tr,), out_shape=(out, out, out),
        in_specs=[blk, blk, blk, blk], out_specs=(blk, blk, blk),
        compiler_params=_params(("parallel",)),
    )(w, g, m, v)


def _adamw_nd(w, g, m, v, name):
    shape = w.shape
    flat = (lambda t: t.reshape(1, -1)) if w.ndim == 1 else (lambda t: t.reshape(-1, shape[-1]))
    return tuple(t.reshape(shape) for t in _adamw(flat(w), flat(g), flat(m), flat(v), name))


def _pad_rows(t, rows):
    return jnp.pad(t, ((0, rows - t.shape[0]), (0, 0)))


def _pack_shard_layer(l, wts):
    def tr(name):
        return wts[name][l].astype(BF16).T

    parts = [tr("ffn1_w_gate"), tr("ffn1_w_up"), wts["ffn1_w_down"][l].astype(BF16),
             tr("ffn2_w_gate"), tr("ffn2_w_up"), wts["ffn2_w_down"][l].astype(BF16),
             wts["w_out"][l].astype(BF16),
             tr("w_kv_b").reshape(KV_SH_ROWS, D_MODEL),
             _pad_rows(tr("w_in"), 160),
             _pad_rows(tr("w_q_b").reshape(Q_SH_ROWS, D_MODEL), Q_PAD_ROWS)]
    return jnp.concatenate(parts, axis=0)


def _mixer_weights(w_out, small):
    w = {"out": w_out}
    small = small.reshape(N_DEV, SMALL_ROWS, D_MODEL)
    o_in, o_q = OFF_IN - OFF_KV, OFF_Q - OFF_KV
    w["kv"] = small[:, :KV_SH_ROWS].reshape(N_HEADS * HEAD_PAD, KV_LORA)
    w["in"] = _pad_rows(small[:, o_in:o_in + IN_SH].reshape(IN_COLS, D_MODEL), IN_PAD)
    wq = small[:, o_q:o_q + Q_SH_ROWS].reshape(N_HEADS, QK_HEAD, Q_LORA)
    w["q"] = jnp.pad(wq, ((0, 0), (0, HEAD_PAD - QK_HEAD), (0, 0))).reshape(N_HEADS * HEAD_PAD, Q_LORA)
    return w


def _grad_sources_tail(gr):
    gq = gr["q"].reshape(N_HEADS, HEAD_PAD, Q_LORA)[:, :QK_HEAD].reshape(N_DEV, Q_SH_ROWS, D_MODEL)
    small = jnp.concatenate([
        gr["kv"].reshape(N_DEV, KV_SH_ROWS, D_MODEL),
        jnp.pad(gr["in"][:IN_COLS].reshape(N_DEV, IN_SH, D_MODEL), ((0, 0), (0, 160 - IN_SH), (0, 0))),
        jnp.pad(gq, ((0, 0), (0, Q_PAD_ROWS - Q_SH_ROWS), (0, 0)))], axis=1)
    return [gr["out"], small.reshape(N_DEV * SMALL_ROWS, D_MODEL)]


def _pack_bf16_pairs(t):
    rows, d = t.shape
    return lax.bitcast_convert_type(t.astype(BF16).reshape(rows // 2, 2, d).transpose(0, 2, 1), F32)


def _unpack_bf16_pairs(p):
    pairs = jnp.swapaxes(lax.bitcast_convert_type(p, BF16), -1, -2)
    return pairs.reshape(p.shape[:-2] + (2 * p.shape[-2], p.shape[-1]))


def _small_layout(nl):
    names = [("dmod", nl * N_MOD), ("ffn1_norm", nl), ("mix_norm", nl), ("ffn2_norm", nl), ("q_a_norm", nl),
             ("kv_a_norm", nl), ("pool_scale", nl), ("final_norm", 1), ("loss", 1),
             ("pool_w", nl * 4 * POOL_GC * POOL_GC // D_MODEL // 2)]
    off, table = 0, {}
    for name, n in names:
        table[name] = (off, n)
        off += -(-n // 8) * 8
    return table, off


def _to_rows(t, width=D_MODEL):
    n, w = t.shape
    return jnp.pad(t, ((0, -(-n // 8) * 8 - n), (0, width - w)))


def kernel(x, c, positions, ada_w, ada_b, ffn1_norm, ffn1_w_gate, ffn1_w_up, ffn1_w_down, mix_norm, w_in, pool_w, pool_scale, q_a_norm, w_q_b, kv_a_norm, w_kv_b, w_out, ffn2_norm, ffn2_w_gate, ffn2_w_up, ffn2_w_down, final_norm, loss_target, m_ada_w, m_ada_b, m_ffn1_norm, m_ffn1_w_gate, m_ffn1_w_up, m_ffn1_w_down, m_mix_norm, m_w_in, m_pool_w, m_pool_scale, m_q_a_norm, m_w_q_b, m_kv_a_norm, m_w_kv_b, m_w_out, m_ffn2_norm, m_ffn2_w_gate, m_ffn2_w_up, m_ffn2_w_down, m_final_norm, v_ada_w, v_ada_b, v_ffn1_norm, v_ffn1_w_gate, v_ffn1_w_up, v_ffn1_w_down, v_mix_norm, v_w_in, v_pool_w, v_pool_scale, v_q_a_norm, v_w_q_b, v_kv_a_norm, v_w_kv_b, v_w_out, v_ffn2_norm, v_ffn2_w_gate, v_ffn2_w_up, v_ffn2_w_down, v_final_norm):
    wts = dict(ada_w=ada_w, ada_b=ada_b, ffn1_norm=ffn1_norm, ffn1_w_gate=ffn1_w_gate, ffn1_w_up=ffn1_w_up,
               ffn1_w_down=ffn1_w_down, mix_norm=mix_norm, w_in=w_in, pool_w=pool_w, pool_scale=pool_scale,
               q_a_norm=q_a_norm, w_q_b=w_q_b, kv_a_norm=kv_a_norm, w_kv_b=w_kv_b, w_out=w_out,
               ffn2_norm=ffn2_norm, ffn2_w_gate=ffn2_w_gate, ffn2_w_up=ffn2_w_up, ffn2_w_down=ffn2_w_down,
               final_norm=final_norm)
    mom_m = dict(ada_w=m_ada_w, ada_b=m_ada_b, ffn1_norm=m_ffn1_norm, ffn1_w_gate=m_ffn1_w_gate,
                 ffn1_w_up=m_ffn1_w_up, ffn1_w_down=m_ffn1_w_down, mix_norm=m_mix_norm, w_in=m_w_in,
                 pool_w=m_pool_w, pool_scale=m_pool_scale, q_a_norm=m_q_a_norm, w_q_b=m_w_q_b,
                 kv_a_norm=m_kv_a_norm, w_kv_b=m_w_kv_b, w_out=m_w_out, ffn2_norm=m_ffn2_norm,
                 ffn2_w_gate=m_ffn2_w_gate, ffn2_w_up=m_ffn2_w_up, ffn2_w_down=m_ffn2_w_down,
                 final_norm=m_final_norm)
    mom_v = dict(ada_w=v_ada_w, ada_b=v_ada_b, ffn1_norm=v_ffn1_norm, ffn1_w_gate=v_ffn1_w_gate,
                 ffn1_w_up=v_ffn1_w_up, ffn1_w_down=v_ffn1_w_down, mix_norm=v_mix_norm, w_in=v_w_in,
                 pool_w=v_pool_w, pool_scale=v_pool_scale, q_a_norm=v_q_a_norm, w_q_b=v_w_q_b,
                 kv_a_norm=v_kv_a_norm, w_kv_b=v_w_kv_b, w_out=v_w_out, ffn2_norm=v_ffn2_norm,
                 ffn2_w_gate=v_ffn2_w_gate, ffn2_w_up=v_ffn2_w_up, ffn2_w_down=v_ffn2_w_down,
                 final_norm=v_final_norm)
    order = list(wts)
    nl = ada_w.shape[0]
    seq = x.shape[1]
    me = 4 * lax.axis_index("x") + 2 * lax.axis_index("y") + lax.axis_index("c")
    ada_cols = ada_w.shape[2]

    def after_token(t, token):
        return t + token[0:1, 0:1].astype(t.dtype)

    packs = [_pack_shard_layer(l, wts) for l in range(nl)]

    c_all = _all_gather(jnp.broadcast_to(c, (8, D_MODEL)), "gather_c")[::8]

    ada_b_mine = lax.dynamic_slice_in_dim(ada_b, me * ada_cols, ada_cols, axis=1).reshape(nl, 1, ada_cols)
    mod_part = _ada_mod(c_all, ada_w, ada_b_mine, "ada_mod")
    mod_all = _all_gather(mod_part.reshape(nl * N_DEV, ada_cols), "gather_mod")
    mod_all = mod_all.reshape(N_DEV, nl, N_DEV, ada_cols)
    mod = lax.dynamic_index_in_dim(mod_all, me, axis=2, keepdims=False)
    mod = mod.transpose(1, 0, 2).reshape(nl * N_MOD, 1, D_MODEL)
    norm_tables = {name: wts[name].reshape(nl, 1, D_MODEL) for name in ("ffn1_norm", "mix_norm", "ffn2_norm")}

    def modrow(l, k):
        return mod, l * N_MOD + k

    def normrow(name, l):
        return norm_tables[name], l

    def start_layer(l, after):
        first = _gather_start(packs[l][:SPLIT_AB], ROWS_A, after, f"gather_start_{l}a")
        mixer = _gather_start(packs[l][OFF_OUT:], ROWS_TAIL, first[4], f"gather_start_{l}b")
        second = _gather_start(packs[l][SPLIT_AB:OFF_OUT], ROWS_A, mixer[4], f"gather_start_{l}c")
        return first, mixer, second

    flights = {0: start_layer(0, mod)}
    if nl > 1:
        flights[1] = start_layer(1, flights[0][2][4])
    last_start = flights[min(1, nl - 1)][2][4]

    cos, sin = _rope_tables(after_token(positions.reshape(seq, 1), last_start), "rope_tables")

    def vec(t):
        return t.reshape(1, -1)

    def landed(flight, rows_list, after, tag):
        send_sems, recv_sems, pk, lands, _ = flight
        pk, lands = _gather_wait(send_sems, recv_sems, pk, lands, after, f"gather_wait_{tag}")
        return _gather_finish(pk, rows_list, lands, "gather_finish")

    xs = x.reshape(seq, D_MODEL)
    saved = []
    for l in range(nl):
        norm1, up_after = normrow("ffn1_norm", l), None
        flight_a, flight_b, flight_c = flights[l]
        lands = landed(flight_a, ROWS_A, cos if l == 0 else xs, f"{l}a")
        if l >= 1 and l + 1 < nl:
            flights[l + 1] = start_layer(l + 1, lands[0])
            up_after = flights[l + 1][2][4]
        sv = {}

        def ffn_fwd(xin, norm, k0, wg, wu, wd, tag, after=None):
            h, a, b, t = _ffn_up(xin, norm, modrow(l, k0), modrow(l, k0 + 1), wg, wu, "ffn_up", after=after)
            y, xout = _mm(t, wd, "nn", "ffn_down", res=xin, gate=modrow(l, k0 + 2), gate_factor=0.5)
            sv[tag] = dict(x=xin, h=h, a=a, b=b, t=t, y=y)
            return xout

        xs = ffn_fwd(xs, norm1, 0, lands[0], lands[1], lands[2], "f1", up_after)
        w = dict(zip(("g1", "u1", "d1"), lands[:3]))
        w.update(_mixer_weights(*landed(flight_b, ROWS_TAIL, xs, f"{l}b")))
        sv["w"] = w

        h2, z = _norm_mm(xs, normrow("mix_norm", l), modrow(l, 3), modrow(l, 4), w["in"], "mix_in")
        y_pool, diff = _pool_fwd(z, pool_w[l], vec(pool_scale[l]), "pool_fwd")
        q, k, v, cqn, ckvn = _qkv_fwd(z, vec(q_a_norm[l]), vec(kv_a_norm[l]), w["q"], w["kv"], cos, sin, "qkv_fwd")
        o, lse = _attn_fwd(q, k, v, "attn_fwd")
        ycat = jnp.concatenate([y_pool, o.astype(BF16)], axis=1)
        y2, xmix = _mm(ycat, w["out"], "nn", "mix_out", res=xs, gate=modrow(l, 5), gate_factor=1.0)
        sv["mix"] = dict(x=xs, h=h2, z=z, diff=diff, q=q, k=k, v=v, cqn=cqn, ckvn=ckvn, lse=lse, ycat=ycat, y=y2)
        xs = xmix

        w.update(zip(("g2", "u2", "d2"), landed(flight_c, ROWS_A, xs, f"{l}c")))
        xs = ffn_fwd(xs, normrow("ffn2_norm", l), 6, w["g2"], w["u2"], w["d2"], "f2")
        saved.append(sv)

    loss_part, dx, d_final, *head = _loss_head(xs, vec(final_norm), loss_target.reshape(seq, D_MODEL),
                                               (saved[nl - 1]["f2"]["y"], modrow(nl - 1, 8), 0.5), "loss_head")

    small = {name: [None] * nl for name in ("ffn1_norm", "mix_norm", "ffn2_norm", "q_a_norm", "kv_a_norm",
                                            "pool_scale", "pool_w", "dmod")}
    core = lax.axis_index("c").astype(jnp.int32).reshape(1)
    chip = 2 * lax.axis_index("x") + lax.axis_index("y")
    exchanges = []

    def leave(srcs, rows_list, after, tag):
        return _pair_start(srcs, rows_list, after, f"pair_start_{tag}"), rows_list, tag

    def forward_on(pending, after, layer, row_off):
        (send_sems, recv_sems, srcs, land, _), rows_list, tag = pending
        srcs, land = _split_wait(send_sems, recv_sems, 1, srcs, land, after, f"pair_wait_{tag}")
        sums = _pair_sum(srcs, rows_list, land, core, "pair_sum")
        flight = _chip_exchange_start(sums, chip, after, f"exchange_start_{tag}")
        exchanges.append((flight, layer, row_off, tag))
        return flight[4]

    pending = None
    for l in reversed(range(nl)):
        sv = saved[l]
        w = sv["w"]
        dmod = [None] * N_MOD
        gr = {}

        def ffn_bwd(dxin, head, s_, norm, k0, wg, wu, wd, tag, below, first_after=None, mid=None):
            dy, dmod[k0 + 2] = head
            da, db, gr["d" + tag], gr["g" + tag], gr["u" + tag] = _ffn_bwd_cols(
                dy, s_["h"], s_["a"], s_["b"], s_["t"], wd, "ffn_bwd_cols", after=first_after)
            dh = _mm_pair(da, wg, db, wu, "ffn_bwd_dh", after=None if mid is None else mid(da))
            outs = _rm_bwd(dh, s_["x"], dxin, norm, modrow(l, k0 + 1), "rm_bwd", below=below)
            dmod[k0], dmod[k0 + 1] = outs[1], outs[2]
            return outs[0], outs[3], outs[4:]

        s_ = sv["mix"]
        dx, small["ffn2_norm"][l], head = ffn_bwd(
            dx, head, sv["f2"], normrow("ffn2_norm", l), 6, w["g2"], w["u2"], w["d2"], "2", (s_["y"], modrow(l, 5), 1.0),
            first_after=None if pending is None else pending[0][4])

        pending_c = leave([gr["g2"], gr["u2"], gr["d2"]], ROWS_A, dx, f"{l}c")
        mix_after = pending_c[0][4]
        if pending is not None:
            mix_after = forward_on(pending, mix_after, l + 1, GB_F1)
            pending = None
        dy, dmod[5] = head
        gr["out"] = _mm(s_["ycat"], dy, "tn", "mix_out_dw", out_dtype=BF16, tm=512, after=mix_after)
        dycat = _mm(dy, w["out"], "nt", "mix_out_dx", tm=1024)
        du, small["pool_w"][l], small["pool_scale"][l] = _pool_bwd(dycat, s_["diff"], pool_w[l], vec(pool_scale[l]), "pool_bwd")
        dq, dk, dv = _attn_bwd(s_["q"], s_["k"], s_["v"], s_["lse"], dycat, "attn_bwd")
        dz, dqb, dkvb, small["q_a_norm"][l], small["kv_a_norm"][l] = _qkv_bwd(
            dq, dk, dv, du, s_["z"], vec(q_a_norm[l]), vec(kv_a_norm[l]), w["q"], w["kv"], cos, sin, "qkv_bwd")
        gr["q"] = _mm(dqb, s_["cqn"], "tn", "q_b_dw", out_dtype=BF16, tm=512, after=forward_on(pending_c, dz, l, GB_F2))
        gr["kv"] = _mm(dkvb, s_["ckvn"], "tn", "kv_b_dw", out_dtype=BF16, tm=512)
        gr["in"] = _mm(dz, s_["h"], "tn", "mix_in_dw", out_dtype=BF16, tm=512)
        dh2 = _mm(dz, w["in"], "nn", "mix_in_dx", tm=1024)
        outs = _rm_bwd(dh2, s_["x"], dx, normrow("mix_norm", l), modrow(l, 4), "rm_bwd",
                       below=(sv["f1"]["y"], modrow(l, 2), 0.5))
        dx, dmod[3], dmod[4], small["mix_norm"][l] = outs[:4]
        head = outs[4:]

        first_after, mid = None, None
        if l == 0:
            pending_b = leave(_grad_sources_tail(gr), ROWS_TAIL, dx, "0b")
            first_after = pending_b[0][4]
            last_groups = []

            def mid(da):
                last_groups.append(leave([gr["g1"], gr["u1"], gr["d1"]], ROWS_A, da, "0a"))
                return forward_on(pending_b, last_groups[0][0][4], 0, GB_TAIL)
        below = (saved[l - 1]["f2"]["y"], modrow(l - 1, 8), 0.5) if l > 0 else None
        dx, small["ffn1_norm"][l], head = ffn_bwd(
            dx, head, sv["f1"], normrow("ffn1_norm", l), 0, w["g1"], w["u1"], w["d1"], "1", below, first_after, mid)

        small["dmod"][l] = jnp.concatenate(dmod, axis=0)
        if l > 0:
            pending = leave([gr["g1"], gr["u1"], gr["d1"]] + _grad_sources_tail(gr), ROWS_A + ROWS_TAIL, dx, l)

    grad_x = dx.reshape(x.shape)
    pending_a = last_groups[0]

    layout, small_rows = _small_layout(nl)
    pieces = {
        "dmod": jnp.concatenate(small["dmod"], axis=0),
        "ffn1_norm": jnp.concatenate(small["ffn1_norm"], axis=0),
        "mix_norm": jnp.concatenate(small["mix_norm"], axis=0),
        "ffn2_norm": jnp.concatenate(small["ffn2_norm"], axis=0),
        "q_a_norm": jnp.concatenate(small["q_a_norm"], axis=0),
        "kv_a_norm": jnp.concatenate(small["kv_a_norm"], axis=0),
        "pool_scale": jnp.concatenate(small["pool_scale"], axis=0),
        "final_norm": d_final,
        "loss": jnp.broadcast_to(loss_part[0:1, 0:1], (1, D_MODEL)),
        "pool_w": _pack_bf16_pairs(jnp.stack(small["pool_w"]).reshape(-1, D_MODEL)),
    }
    small_buf = jnp.concatenate([_to_rows(pieces[name]) for name in layout], axis=0)
    def landed_sums(gbuf, entries, after):
        for (send_sems, recv_sems, sums, recv, _), layer, row_off, tag in entries:
            _, recv = _split_wait(send_sems, recv_sems, N_CHIPS - 1, sums, recv, after, f"exchange_wait_{tag}")
            gbuf = _sum_slots_into(recv, gbuf, layer, row_off, "sum_grads")
        return gbuf

    gbuf = lax.empty((nl, ROWS_L, D_MODEL), F32)
    token_0a = forward_on(pending_a, dx, 0, GB_F1)
    spread = _spread_start(small_buf, me, token_0a, "small_start")
    gbuf = landed_sums(gbuf, [e for e in exchanges if e[3] != "0a"], spread[4])

    def swap(t):
        return t.transpose(0, 2, 1)

    def same(t):
        return t

    grads, updates = {}, {}

    def update_rows(gbuf, table):
        for wname, off, view in table:
            g, d_, nm, nv = _adamw_rows(view(wts[wname]), gbuf, off, view(mom_m[wname]), view(mom_v[wname]), "adamw_rows")
            grads[wname], updates[wname] = view(g), (view(d_), view(nm), view(nv))

    update_rows(gbuf, (("ffn2_w_gate", GB_F2, swap), ("ffn2_w_up", GB_F2 + FF_SH, swap),
                       ("ffn2_w_down", GB_F2 + 2 * FF_SH, same), ("w_out", GB_TAIL, same)))
    small_grads = {
        "w_kv_b": (gbuf[:, OFF_KV:OFF_KV + KV_SH_ROWS].reshape(nl, -1, KV_LORA).transpose(0, 2, 1), same),
        "w_in": (gbuf[:, OFF_IN:OFF_IN + IN_SH], swap),
        "w_q_b": (gbuf[:, OFF_Q:OFF_Q + Q_SH_ROWS].reshape(nl, -1, Q_LORA), swap),
    }
    for wname, (g, view) in small_grads.items():
        upd = _adamw_nd(view(wts[wname]), g, view(mom_m[wname]), view(mom_v[wname]), "adamw")
        grads[wname], updates[wname] = view(g), tuple(view(t) for t in upd)
    gbuf = landed_sums(gbuf, [e for e in exchanges if e[3] == "0a"], updates["w_q_b"][0])
    update_rows(gbuf, (("ffn1_w_gate", GB_F1, swap), ("ffn1_w_up", GB_F1 + FF_SH, swap),
                       ("ffn1_w_down", GB_F1 + 2 * FF_SH, same)))

    _, small_all = _split_wait(spread[0], spread[1], N_DEV - 1, spread[2], spread[3], updates["ffn1_w_down"][0],
                               "small_wait")
    pool_off, pool_rows = layout["pool_w"]
    small_sum = _sum_slots(small_all[:, :pool_off], "sum_small")
    pool_sum = _sum_slots(_unpack_bf16_pairs(small_all[:, pool_off:pool_off + pool_rows]), "sum_pool_w")

    def take(name, width=D_MODEL):
        off, n = layout[name]
        return small_sum[off:off + n, :width]

    late = {"ada_b": take("dmod").reshape(nl, N_MOD * D_MODEL),
            "ffn1_norm": take("ffn1_norm"), "mix_norm": take("mix_norm"), "ffn2_norm": take("ffn2_norm"),
            "q_a_norm": take("q_a_norm", Q_LORA), "kv_a_norm": take("kv_a_norm", KV_LORA),
            "pool_scale": take("pool_scale", POOL_WIDTH), "final_norm": take("final_norm").reshape(D_MODEL),
            "pool_w": pool_sum.reshape(pool_w.shape)}
    loss = take("loss")[0, 0]

    off, n = layout["dmod"]
    dmod_all = small_all[:, off:off + n].reshape(N_DEV, nl, N_MOD * D_MODEL)
    dmod_mine = lax.dynamic_slice_in_dim(dmod_all, me * ada_cols, ada_cols, axis=2)
    dmod_pad = jnp.pad(dmod_mine.transpose(1, 0, 2), ((0, 0), (0, LANE - N_DEV), (0, 0)))
    late["ada_w"] = _ada_grad(jnp.pad(c_all, ((0, LANE - N_DEV), (0, 0))), dmod_pad, "ada_grad")
    for name, g in late.items():
        grads[name], updates[name] = g, _adamw_nd(wts[name], g, mom_m[name], mom_v[name], "adamw")

    return (loss, grad_x, *[grads[n] for n in order], *[updates[n][0] for n in order],
            *[updates[n][1] for n in order], *[updates[n][2] for n in order])
```

```python
import math

import numpy as np
import jax
import jax.numpy as jnp
from jax import lax
from jax.experimental import pallas as pl
from jax.experimental.pallas import tpu as pltpu

F32 = jnp.float32
BF16 = jnp.bfloat16

N_DEV = 8
D_MODEL = 1024
D_FF = 2816
POOL_WIDTH = 512
POOL_WINDOWS = (2, 4, 8, 16)
POOL_GC = 128
N_HEADS = 4
QK_NOPE = 128
QK_ROPE = 64
V_HEAD = 128
QK_HEAD = QK_NOPE + QK_ROPE
HEAD_PAD = 256
Q_LORA = 384
KV_LORA = 256
IN_COLS = POOL_WIDTH + Q_LORA + KV_LORA + QK_ROPE
IN_PAD = 1280
ROPE_THETA = 10000.0
SOFTMAX_SCALE = 1.0 / math.sqrt(QK_HEAD)
EPS = 1e-6
N_MOD = 9

ADAM_LR = 0.001
ADAM_B1 = 0.9
ADAM_B2 = 0.999
ADAM_EPS = 1e-08
ADAM_WD = 0.01
ADAM_STEP = 10

LANE = 128
VMEM_LIMIT = 56 * 1024 * 1024

FF_SH = D_FF // N_DEV
OFF_G1, OFF_U1, OFF_D1 = 0, FF_SH, 2 * FF_SH
OFF_G2, OFF_U2, OFF_D2 = 3 * FF_SH, 4 * FF_SH, 5 * FF_SH
OFF_OUT = 6 * FF_SH
OFF_KV = OFF_OUT + 128
OFF_IN = OFF_KV + 32
OFF_Q = OFF_IN + 160
Q_PAD_ROWS = 64
ROWS_L = OFF_Q + Q_PAD_ROWS
IN_SH = IN_COLS // N_DEV
Q_SH_ROWS = (N_HEADS * QK_HEAD // N_DEV) * Q_LORA // D_MODEL
KV_SH_ROWS = (N_HEADS * (QK_NOPE + V_HEAD) // N_DEV) * KV_LORA // D_MODEL


def _tile(dim, target):
    if dim <= target:
        return dim
    best = None
    for t in range(LANE, target + 1, LANE):
        if dim % t == 0:
            best = t
    assert best is not None, (dim, target)
    return best


def _params(sem):
    return pltpu.CompilerParams(dimension_semantics=sem, vmem_limit_bytes=VMEM_LIMIT)


def _mesh_pos():
    return lax.axis_index("x"), lax.axis_index("y"), lax.axis_index("c")


def _all_gather(x, name):
    m, n = x.shape

    def body(x_ref, out_ref, send_sems, recv_sems, local_sem):
        px, py, pc = _mesh_pos()
        me, sibling = (px, py, pc), (px, py, 1 - pc)
        chips = [(1 - px, py), (px, 1 - py), (1 - px, 1 - py)]

        def rows(bx, by, bc):
            return out_ref.at[pl.ds((4 * bx + 2 * by + bc) * m, m), :]

        def copy(k, block, to, src=None):
            return pltpu.make_async_remote_copy(
                src_ref=rows(*block) if src is None else src, dst_ref=rows(*block),
                send_sem=send_sems.at[k], recv_sem=recv_sems.at[k],
                device_id=to, device_id_type=pl.DeviceIdType.MESH)

        mine = pltpu.make_async_copy(x_ref, rows(*me), local_sem)
        mine.start()
        first = [copy(0, me, sibling, src=x_ref)]
        first += [copy(1 + j, me, (*chip, pc), src=x_ref) for j, chip in enumerate(chips)]
        for cp in first:
            cp.start()
        passed = [copy(4 + j, (*chip, pc), sibling) for j, chip in enumerate(chips)]
        for j, chip in enumerate(chips):
            copy(1 + j, (*chip, pc), me).wait_recv()
            passed[j].start()
        copy(0, sibling, me).wait_recv()
        for j, chip in enumerate(chips):
            copy(4 + j, (*chip, 1 - pc), me).wait_recv()
        for cp in first + passed:
            cp.wait_send()
        mine.wait()

    hbm = pl.BlockSpec(memory_space=pltpu.HBM)
    return pl.pallas_call(
        body, name=name,
        out_shape=jax.ShapeDtypeStruct((N_DEV * m, n), x.dtype),
        in_specs=[hbm], out_specs=hbm,
        scratch_shapes=[pltpu.SemaphoreType.DMA((7,)), pltpu.SemaphoreType.DMA((7,)),
                        pltpu.SemaphoreType.DMA],
    )(x)


SMALL_ROWS = ROWS_L - OFF_KV
ROWS_A = [FF_SH] * 3
SPLIT_AB = sum(ROWS_A)
ROWS_TAIL = [128, SMALL_ROWS]
GB_F2, GB_F1, GB_TAIL = 0, SPLIT_AB, 2 * SPLIT_AB
HBM_SPEC = pl.BlockSpec(memory_space=pltpu.HBM)
SEM_SPEC = pl.BlockSpec(memory_space=pltpu.SEMAPHORE)
ANY_SPEC = pl.BlockSpec(memory_space=pl.ANY)
EFFECT = pltpu.SideEffectType.DATAFLOW_SIDE_EFFECTING


def _hbm(t):
    return pltpu.with_memory_space_constraint(t, pltpu.HBM)


def _whole_wait(ref, send_sem, recv_sem, peer):
    return pltpu.make_async_remote_copy(src_ref=ref, dst_ref=ref, send_sem=send_sem, recv_sem=recv_sem,
                                        device_id=peer, device_id_type=pl.DeviceIdType.MESH)


def _offsets(rows_list):
    return [sum(rows_list[:i]) for i in range(len(rows_list))]


def _gather_start(packed, rows_list, after, name):
    n = len(rows_list)
    offs = _offsets(rows_list)
    lands = [_hbm(lax.empty((N_DEV * rows, D_MODEL), BF16)) for rows in rows_list]

    def body(packed_ref, *refs):
        land = refs[:n]
        send_sems, recv_sems = refs[n + 1], refs[n + 2]
        token = refs[-1]
        px, py, pc = _mesh_pos()
        me = 4 * px + 2 * py + pc
        peers = [(px, py, 1 - pc), (1 - px, py, pc), (px, 1 - py, pc), (1 - px, 1 - py, pc)]
        for k, peer in enumerate(peers):
            for off, rows, land_ref in zip(offs, rows_list, land):
                pltpu.make_async_remote_copy(
                    src_ref=packed_ref.at[pl.ds(off, rows), :], dst_ref=land_ref.at[pl.ds(me * rows, rows), :],
                    send_sem=send_sems.at[k], recv_sem=recv_sems.at[k],
                    device_id=peer, device_id_type=pl.DeviceIdType.MESH).start()
        token[...] = jnp.zeros_like(token)

    outs = pl.pallas_call(
        body, name=name,
        out_shape=(pltpu.SemaphoreType.DMA((4,)), pltpu.SemaphoreType.DMA((4,)), pltpu.HBM(packed.shape, BF16),
                   *[pltpu.HBM(t.shape, BF16) for t in lands], jax.ShapeDtypeStruct((8, LANE), F32)),
        in_specs=(HBM_SPEC,) * (1 + n) + (ANY_SPEC,),
        out_specs=(SEM_SPEC, SEM_SPEC) + (HBM_SPEC,) * (1 + n) + (pl.BlockSpec(memory_space=pltpu.VMEM),),
        input_output_aliases={i: 2 + i for i in range(1 + n)},
        compiler_params=pltpu.CompilerParams(has_side_effects=EFFECT),
    )(_hbm(packed), *lands, after)
    return outs[0], outs[1], outs[2], list(outs[3:3 + n]), outs[-1]


def _gather_wait(send_sems, recv_sems, packed, lands, after, name):
    n = len(lands)

    def body(packed_ref, *refs):
        s_sems, r_sems = refs[n], refs[n + 1]
        me = _mesh_pos()
        for k in range(4):
            cp = _whole_wait(packed_ref, s_sems.at[k], r_sems.at[k], me)
            cp.wait_send()
            cp.wait_recv()

    outs = pl.pallas_call(
        body, name=name,
        out_shape=(pltpu.HBM(packed.shape, BF16), *[pltpu.HBM(t.shape, BF16) for t in lands]),
        in_specs=(HBM_SPEC,) * (1 + n) + (SEM_SPEC, SEM_SPEC, ANY_SPEC),
        out_specs=(HBM_SPEC,) * (1 + n),
        input_output_aliases={i: i for i in range(1 + n)},
        compiler_params=pltpu.CompilerParams(has_side_effects=EFFECT),
    )(packed, *lands, send_sems, recv_sems, after)
    return outs[0], list(outs[1:])


def _gather_finish(packed, rows_list, lands, name):
    n = len(rows_list)
    offs = _offsets(rows_list)

    def body(packed_ref, *refs):
        land = refs[n:2 * n]
        send_sems, recv_sems, stage, stage_sem = refs[2 * n:]
        px, py, pc = _mesh_pos()
        me = 4 * px + 2 * py + pc
        sibling = (px, py, 1 - pc)
        load = pltpu.make_async_copy(packed_ref, stage, stage_sem)
        load.start()
        for j, (cx, cy) in enumerate([(1 - px, py), (px, 1 - py), (1 - px, 1 - py)]):
            block = 4 * cx + 2 * cy + pc
            for rows, land_ref in zip(rows_list, land):
                blk = land_ref.at[pl.ds(block * rows, rows), :]
                pltpu.make_async_remote_copy(src_ref=blk, dst_ref=blk, send_sem=send_sems.at[j],
                                             recv_sem=recv_sems.at[j], device_id=sibling,
                                             device_id_type=pl.DeviceIdType.MESH).start()
        load.wait()
        for off, rows, land_ref in zip(offs, rows_list, land):
            pltpu.make_async_copy(stage.at[pl.ds(off, rows), :], land_ref.at[pl.ds(me * rows, rows), :],
                                  stage_sem).start()
        for j in range(3):
            cp = _whole_wait(packed_ref, send_sems.at[j], recv_sems.at[j], sibling)
            cp.wait_recv()
            cp.wait_send()
        pltpu.make_async_copy(stage, packed_ref, stage_sem).wait()

    outs = pl.pallas_call(
        body, name=name,
        out_shape=tuple(jax.ShapeDtypeStruct(t.shape, BF16) for t in lands),
        in_specs=(HBM_SPEC,) * (1 + n), out_specs=(HBM_SPEC,) * n,
        input_output_aliases={1 + i: i for i in range(n)},
        scratch_shapes=[pltpu.SemaphoreType.DMA((3,)), pltpu.SemaphoreType.DMA((3,)),
                        pltpu.VMEM(packed.shape, BF16), pltpu.SemaphoreType.DMA],
    )(packed, *lands)
    return list(outs)


N_CHIPS = 4


def _pair_start(srcs, rows_list, after, name):
    n = len(rows_list)
    offs = _offsets(rows_list)
    land = lax.empty((N_CHIPS, sum(rows_list), D_MODEL), BF16)

    def body(*refs):
        src, land_ref = refs[:n], refs[n]
        send_sems, recv_sems = refs[n + 2], refs[n + 3]
        token = refs[-1]
        px, py, pc = _mesh_pos()
        for k in range(N_CHIPS):
            block = 2 * k + (1 - pc)
            for off, rows, src_ref in zip(offs, rows_list, src):
                pltpu.make_async_remote_copy(
                    src_ref=src_ref.at[pl.ds(block * rows, rows), :], dst_ref=land_ref.at[k, pl.ds(off, rows), :],
                    send_sem=send_sems.at[0], recv_sem=recv_sems.at[0],
                    device_id=(px, py, 1 - pc), device_id_type=pl.DeviceIdType.MESH).start()
        token[...] = jnp.zeros_like(token)

    outs = pl.pallas_call(
        body, name=name,
        out_shape=(pltpu.SemaphoreType.DMA((1,)), pltpu.SemaphoreType.DMA((1,)),
                   *[pltpu.HBM(t.shape, BF16) for t in srcs], pltpu.HBM(land.shape, BF16),
                   jax.ShapeDtypeStruct((8, LANE), F32)),
        in_specs=(HBM_SPEC,) * (n + 1) + (ANY_SPEC,),
        out_specs=(SEM_SPEC, SEM_SPEC) + (HBM_SPEC,) * (n + 1) + (pl.BlockSpec(memory_space=pltpu.VMEM),),
        input_output_aliases={i: 2 + i for i in range(n + 1)},
        compiler_params=pltpu.CompilerParams(has_side_effects=EFFECT),
    )(*[_hbm(t) for t in srcs], _hbm(land), after)
    return outs[0], outs[1], list(outs[2:2 + n]), outs[2 + n], outs[-1]


def _split_wait(send_sems, recv_sems, n_sems, srcs, land, after, name):
    n = len(srcs)

    def body(*refs):
        land_ref = refs[n]
        s_sems, r_sems = refs[n + 1], refs[n + 2]
        me = _mesh_pos()
        for k in range(n_sems):
            cp = _whole_wait(land_ref.at[0] if n_sems > 1 else land_ref, s_sems.at[k], r_sems.at[k], me)
            cp.wait_send()
            cp.wait_recv()

    outs = pl.pallas_call(
        body, name=name,
        out_shape=(*[pltpu.HBM(t.shape, t.dtype) for t in srcs], pltpu.HBM(land.shape, land.dtype)),
        in_specs=(HBM_SPEC,) * (n + 1) + (SEM_SPEC, SEM_SPEC, ANY_SPEC),
        out_specs=(HBM_SPEC,) * (n + 1),
        input_output_aliases={i: i for i in range(n + 1)},
        compiler_params=pltpu.CompilerParams(has_side_effects=EFFECT),
    )(*srcs, land, send_sems, recv_sems, after)
    return list(outs[:n]), outs[n]


def _spread_start(x, me_id, after, name):
    land = lax.dynamic_update_slice_in_dim(lax.empty((N_DEV,) + x.shape, x.dtype), x[None], me_id, axis=0)

    def body(x_ref, land_ref, after_ref, send_sems, recv_sems, x_thru, land_thru, token):
        px, py, pc = _mesh_pos()
        me = 4 * px + 2 * py + pc
        for k in range(1, N_DEV):
            qx = 1 - px if k & 4 else px
            qy = 1 - py if k & 2 else py
            qc = 1 - pc if k & 1 else pc
            pltpu.make_async_remote_copy(
                src_ref=x_ref, dst_ref=land_ref.at[me], send_sem=send_sems.at[k - 1], recv_sem=recv_sems.at[k - 1],
                device_id=(qx, qy, qc), device_id_type=pl.DeviceIdType.MESH).start()
        token[...] = jnp.zeros_like(token)

    outs = pl.pallas_call(
        body, name=name,
        out_shape=(pltpu.SemaphoreType.DMA((N_DEV - 1,)), pltpu.SemaphoreType.DMA((N_DEV - 1,)),
                   pltpu.HBM(x.shape, x.dtype), pltpu.HBM(land.shape, land.dtype), jax.ShapeDtypeStruct((8, LANE), F32)),
        in_specs=(HBM_SPEC, HBM_SPEC, ANY_SPEC),
        out_specs=(SEM_SPEC, SEM_SPEC, HBM_SPEC, HBM_SPEC, pl.BlockSpec(memory_space=pltpu.VMEM)),
        input_output_aliases={0: 2, 1: 3},
        compiler_params=pltpu.CompilerParams(has_side_effects=EFFECT),
    )(_hbm(x), _hbm(land), after)
    return outs[0], outs[1], [outs[2]], outs[3], outs[4]


def _pair_sum(srcs, rows_list, land, core, name):
    n = len(rows_list)
    offs = _offsets(rows_list)
    total = sum(rows_list)

    def body(core_ref, *refs):
        src, land_ref, out_ref = refs[:n], refs[n], refs[n + 1]
        for off, rows, src_ref in zip(offs, rows_list, src):
            out_ref[pl.ds(off, rows), :] = (src_ref[...].astype(F32)
                                            + land_ref[pl.ds(off, rows), :].astype(F32)).astype(BF16)

    slot = pl.BlockSpec((None, total, D_MODEL), lambda k, c: (k, 0, 0))
    grid_spec = pltpu.PrefetchScalarGridSpec(
        num_scalar_prefetch=1, grid=(N_CHIPS,),
        in_specs=[pl.BlockSpec((rows, D_MODEL), lambda k, c: (2 * k + c[0], 0)) for rows in rows_list] + [slot],
        out_specs=slot)
    return pl.pallas_call(
        body, name=name, grid_spec=grid_spec,
        out_shape=jax.ShapeDtypeStruct((N_CHIPS, total, D_MODEL), BF16),
        compiler_params=_params(("parallel",)),
    )(core, *srcs, land)


def _chip_exchange_start(sums, chip, after, name):
    own = lax.dynamic_index_in_dim(sums, chip, axis=0, keepdims=True)
    recv = lax.dynamic_update_slice_in_dim(lax.empty(sums.shape, BF16), own, chip, axis=0)

    def body(sums_ref, recv_ref, after_ref, send_sems, recv_sems, sums_thru, recv_thru, token):
        px, py, pc = _mesh_pos()
        for k in range(1, N_CHIPS):
            qx = 1 - px if k & 2 else px
            qy = 1 - py if k & 1 else py
            pltpu.make_async_remote_copy(
                src_ref=sums_ref.at[2 * qx + qy], dst_ref=recv_ref.at[2 * px + py],
                send_sem=send_sems.at[k - 1], recv_sem=recv_sems.at[k - 1],
                device_id=(qx, qy, pc), device_id_type=pl.DeviceIdType.MESH).start()
        token[...] = jnp.zeros_like(token)

    outs = pl.pallas_call(
        body, name=name,
        out_shape=(pltpu.SemaphoreType.DMA((N_CHIPS - 1,)), pltpu.SemaphoreType.DMA((N_CHIPS - 1,)),
                   pltpu.HBM(sums.shape, BF16), pltpu.HBM(recv.shape, BF16), jax.ShapeDtypeStruct((8, LANE), F32)),
        in_specs=(HBM_SPEC, HBM_SPEC, ANY_SPEC),
        out_specs=(SEM_SPEC, SEM_SPEC, HBM_SPEC, HBM_SPEC, pl.BlockSpec(memory_space=pltpu.VMEM)),
        input_output_aliases={0: 2, 1: 3},
        compiler_params=pltpu.CompilerParams(has_side_effects=EFFECT),
    )(_hbm(sums), _hbm(recv), after)
    return outs[0], outs[1], [outs[2]], outs[3], outs[4]


def _sum_slots_into(recv, buf, layer, row_off, name):
    slots, r, n = recv.shape
    tr = _row_tile(math.gcd(r, row_off) if row_off else r, 512)
    first = row_off // tr

    def body(in_ref, buf_ref, out_ref):
        acc = in_ref[0].astype(F32)
        for j in range(1, slots):
            acc = acc + in_ref[j].astype(F32)
        out_ref[...] = acc

    return pl.pallas_call(
        body, name=name, grid=(r // tr,), out_shape=jax.ShapeDtypeStruct(buf.shape, F32),
        in_specs=[pl.BlockSpec((slots, tr, n), lambda i: (0, i, 0)), ANY_SPEC],
        out_specs=pl.BlockSpec((None, tr, n), lambda i: (layer, first + i, 0)),
        input_output_aliases={1: 0},
        compiler_params=_params(("parallel",)),
    )(recv, buf)


def _sum_slots(recv, name, after=None):
    _, r, n = recv.shape
    tr = _row_tile(r, 512)

    def body(in_ref, *refs):
        acc = in_ref[0].astype(F32)
        for j in range(1, N_DEV):
            acc = acc + in_ref[j].astype(F32)
        refs[-1][...] = acc

    grid = (r // tr,)
    in_specs, out_spec = [pl.BlockSpec((N_DEV, tr, n), lambda i: (0, i, 0))], pl.BlockSpec((tr, n), lambda i: (i, 0))
    args = [recv]
    if after is not None:
        in_specs.append(ANY_SPEC)
        args.append(after)
    return pl.pallas_call(
        body, name=name, grid=grid,
        out_shape=jax.ShapeDtypeStruct((r, n), F32),
        in_specs=in_specs, out_specs=out_spec,
        compiler_params=_params(("parallel",)),
    )(*args)


def _row_tile(rows, target):
    if rows <= target:
        return rows
    best = None
    for t in range(16, target + 1, 16):
        if rows % t == 0:
            best = t
    assert best is not None, rows
    return best


_DIMS = {"nn": ((1,), (0,)), "nt": ((1,), (1,)), "tn": ((0,), (0,))}


def _mm(a, b, mode, name, out_dtype=F32, res=None, gate=None, gate_factor=1.0, tm=512, tn=1408, after=None):
    assert (res is None) == (gate is None)
    if mode == "tn":
        kdim, m = a.shape
    else:
        m, kdim = a.shape
    n = b.shape[0] if mode == "nt" else b.shape[1]
    tm, tn = _tile(m, tm), _tile(n, tn)
    a_spec = (pl.BlockSpec((kdim, tm), lambda i, j: (0, i)) if mode == "tn"
              else pl.BlockSpec((tm, kdim), lambda i, j: (i, 0)))
    b_mode = {"pipeline_mode": pl.Buffered(1)} if n == tn else {}
    b_spec = (pl.BlockSpec((tn, kdim), lambda i, j: (j, 0), **b_mode) if mode == "nt"
              else pl.BlockSpec((kdim, tn), lambda i, j: (0, j), **b_mode))
    o_spec = pl.BlockSpec((tm, tn), lambda i, j: (i, j))
    dims = (_DIMS[mode], ((), ()))
    has_res = res is not None

    def body(a_ref, b_ref, *refs):
        y = lax.dot_general(a_ref[...].astype(BF16), b_ref[...].astype(BF16), dims,
                            preferred_element_type=F32)
        if has_res:
            res_ref, gate_ref = refs[0], refs[1]
            y_ref, o_ref = refs[-2], refs[-1]
            y_ref[...] = y.astype(BF16)
            o_ref[...] = res_ref[...] + (gate_factor * gate_ref[...]) * y
        else:
            refs[-1][...] = y.astype(out_dtype)

    in_specs, args = [a_spec, b_spec], [a, b]
    if has_res:
        gate_spec, gate_arg = _vec_in(gate, tile=tn)
        in_specs += [o_spec, gate_spec]
        args += [res, gate_arg]
        out_shape = (jax.ShapeDtypeStruct((m, n), BF16), jax.ShapeDtypeStruct((m, n), F32))
        out_specs = (o_spec, o_spec)
    else:
        out_shape, out_specs = jax.ShapeDtypeStruct((m, n), out_dtype), o_spec
    if after is not None:
        in_specs.append(ANY_SPEC)
        args.append(after)
    return pl.pallas_call(
        body, name=name, grid=(m // tm, n // tn), out_shape=out_shape,
        in_specs=in_specs, out_specs=out_specs,
        compiler_params=_params(("parallel", "parallel")),
    )(*args)


def _vec_in(v, tile=None):
    if isinstance(v, tuple):
        table, row = v
        if tile is None:
            return pl.BlockSpec((None, 1, table.shape[-1]), lambda *idx: (row, 0, 0)), table
        return pl.BlockSpec((None, 1, tile), lambda i, j: (row, 0, j)), table
    if tile is None:
        return pl.BlockSpec((1, v.shape[-1]), lambda *idx: (0, 0)), v
    return pl.BlockSpec((1, tile), lambda i, j: (0, j)), v


def _vec_spec(width):
    return pl.BlockSpec((1, width), lambda i: (0, 0))


def _rm_bwd(dh, x, dres, gw, scale, name, below=None):
    s, d = x.shape
    ts = _tile(s, 512)
    factor = None if below is None else below[2]

    def body(dh_ref, x_ref, dres_ref, gw_ref, sc_ref, *refs):
        dx_ref, dsh_ref, dsc_ref, dgw_ref = refs[-6:-2] if below is not None else refs[-4:]

        @pl.when(pl.program_id(0) == 0)
        def _():
            dsh_ref[...] = jnp.zeros_like(dsh_ref)
            dsc_ref[...] = jnp.zeros_like(dsc_ref)
            dgw_ref[...] = jnp.zeros_like(dgw_ref)
            if below is not None:
                refs[-1][...] = jnp.zeros_like(refs[-1])

        xv, dhv, gwv = x_ref[...], dh_ref[...], gw_ref[...]
        r = lax.rsqrt(jnp.mean(xv * xv, axis=-1, keepdims=True) + EPS)
        xn = xv * r
        y = xn * gwv
        dsh_ref[...] += jnp.sum(dhv, axis=0, keepdims=True)
        dsc_ref[...] += jnp.sum(dhv * y, axis=0, keepdims=True)
        dy = dhv * (1 + sc_ref[...])
        dgw_ref[...] += jnp.sum(dy * xn, axis=0, keepdims=True)
        dxn = dy * gwv
        dx = dres_ref[...] + r * (dxn - xn * jnp.mean(dxn * xn, axis=-1, keepdims=True))
        dx_ref[...] = dx
        if below is not None:
            yb_ref, gb_ref, dyb_ref, dgb_ref = refs[0], refs[1], refs[-2], refs[-1]
            dyb_ref[...] = ((factor * gb_ref[...]) * dx).astype(BF16)
            dgb_ref[...] += jnp.sum((factor * dx) * yb_ref[...].astype(F32), axis=0, keepdims=True)

    row = pl.BlockSpec((ts, d), lambda i: (i, 0))
    vec = jax.ShapeDtypeStruct((1, d), F32)
    (gw_spec, gw), (sc_spec, scale) = _vec_in(gw), _vec_in(scale)
    in_specs, args = [row, row, row, gw_spec, sc_spec], [dh, x, dres, gw, scale]
    out_shape = [jax.ShapeDtypeStruct((s, d), F32), vec, vec, vec]
    out_specs = [row, _vec_spec(d), _vec_spec(d), _vec_spec(d)]
    if below is not None:
        gate_spec, gate_arg = _vec_in(below[1])
        in_specs += [row, gate_spec]
        args += [below[0], gate_arg]
        out_shape += [jax.ShapeDtypeStruct((s, d), BF16), vec]
        out_specs += [row, _vec_spec(d)]
    return pl.pallas_call(
        body, name=name, grid=(s // ts,), out_shape=tuple(out_shape),
        in_specs=in_specs, out_specs=tuple(out_specs),
        compiler_params=_params(("arbitrary",)),
    )(*args)


def _norm_mm(x, gw, shift, scale, w, name, tm=1024):
    s, d = x.shape
    n = w.shape[0]
    tm = _tile(s, tm)

    def body(x_ref, gw_ref, sh_ref, sc_ref, w_ref, h_ref, z_ref):
        xv = x_ref[...]
        r = lax.rsqrt(jnp.mean(xv * xv, axis=-1, keepdims=True) + EPS)
        hb = (((xv * r) * gw_ref[...]) * (1 + sc_ref[...]) + sh_ref[...]).astype(BF16)
        h_ref[...] = hb
        z_ref[...] = lax.dot_general(hb, w_ref[...], (((1,), (1,)), ((), ())), preferred_element_type=F32)

    row = pl.BlockSpec((tm, d), lambda i: (i, 0))
    return pl.pallas_call(
        body, name=name, grid=(s // tm,),
        out_shape=(jax.ShapeDtypeStruct((s, d), BF16), jax.ShapeDtypeStruct((s, n), F32)),
        in_specs=[row, _vec_in(gw)[0], _vec_in(shift)[0], _vec_in(scale)[0], pl.BlockSpec((n, d), lambda i: (0, 0))],
        out_specs=(row, pl.BlockSpec((tm, n), lambda i: (i, 0))),
        compiler_params=_params(("parallel",)),
    )(x, _vec_in(gw)[1], _vec_in(shift)[1], _vec_in(scale)[1], w)


FFN_TM, FFN_TF = 2048, 256


def _ffn_up(x, gw, shift, scale, wg, wu, name, after=None):
    s, d = x.shape
    f = wg.shape[0]
    tm, tf = _tile(s, FFN_TM), _tile(f, FFN_TF)
    nt = (((1,), (1,)), ((), ()))

    def body(x_ref, gw_ref, sh_ref, sc_ref, wg_ref, wu_ref, *refs):
        h_ref, a_ref, b_ref, t_ref = refs[-4:]

        @pl.when(pl.program_id(1) == 0)
        def _():
            xv = x_ref[...]
            r = lax.rsqrt(jnp.mean(xv * xv, axis=-1, keepdims=True) + EPS)
            h_ref[...] = (((xv * r) * gw_ref[...]) * (1 + sc_ref[...]) + sh_ref[...]).astype(BF16)

        hb = h_ref[...]
        av = lax.dot_general(hb, wg_ref[...], nt, preferred_element_type=F32)
        bv = lax.dot_general(hb, wu_ref[...], nt, preferred_element_type=F32)
        a_ref[...] = av.astype(BF16)
        b_ref[...] = bv.astype(BF16)
        t_ref[...] = ((av * jax.nn.sigmoid(av)) * bv).astype(BF16)

    row = pl.BlockSpec((tm, d), lambda i, j: (i, 0))
    wblk = pl.BlockSpec((tf, d), lambda i, j: (j, 0))
    blk = pl.BlockSpec((tm, tf), lambda i, j: (i, j))
    wide = jax.ShapeDtypeStruct((s, f), BF16)
    vec_specs, vec_args = zip(*[_vec_in(v) for v in (gw, shift, scale)])
    in_specs, args = [row, *vec_specs, wblk, wblk], [x, *vec_args, wg, wu]
    if after is not None:
        in_specs.append(ANY_SPEC)
        args.append(after)
    return pl.pallas_call(
        body, name=name, grid=(s // tm, f // tf),
        out_shape=(jax.ShapeDtypeStruct((s, d), BF16), wide, wide, wide),
        in_specs=in_specs, out_specs=(row, blk, blk, blk),
        compiler_params=_params(("parallel", "arbitrary")),
    )(*args)


def _ffn_bwd_cols(dy, h, a, b, t, wd, name, after=None):
    s, d = dy.shape
    f = wd.shape[0]
    tf = _tile(f, FFN_TF)
    nt = (((1,), (1,)), ((), ()))
    tn = (((0,), (0,)), ((), ()))

    def body(dy_ref, h_ref, a_ref, b_ref, t_ref, wd_ref, *refs):
        da_ref, db_ref, gd_ref, gg_ref, gu_ref = refs[-5:]
        dyb, hb = dy_ref[...], h_ref[...]
        dtv = lax.dot_general(dyb, wd_ref[...], nt, preferred_element_type=F32)
        av, bv = a_ref[...].astype(F32), b_ref[...].astype(F32)
        sg = jax.nn.sigmoid(av)
        dbv = (dtv * (av * sg)).astype(BF16)
        dav = ((dtv * bv) * (sg * (1 + av * (1 - sg)))).astype(BF16)
        da_ref[...] = dav
        db_ref[...] = dbv
        gd_ref[...] = lax.dot_general(t_ref[...], dyb, tn, preferred_element_type=F32).astype(BF16)
        gg_ref[...] = lax.dot_general(dav, hb, tn, preferred_element_type=F32).astype(BF16)
        gu_ref[...] = lax.dot_general(dbv, hb, tn, preferred_element_type=F32).astype(BF16)

    whole = pl.BlockSpec((s, d), lambda j: (0, 0))
    col = pl.BlockSpec((s, tf), lambda j: (0, j))
    wblk = pl.BlockSpec((tf, d), lambda j: (j, 0))
    wide, wgrad = jax.ShapeDtypeStruct((s, f), BF16), jax.ShapeDtypeStruct((f, d), BF16)
    in_specs, args = [whole, whole, col, col, col, wblk], [dy, h, a, b, t, wd]
    if after is not None:
        in_specs.append(ANY_SPEC)
        args.append(after)
    return pl.pallas_call(
        body, name=name, grid=(f // tf,), out_shape=(wide, wide, wgrad, wgrad, wgrad),
        in_specs=in_specs, out_specs=(col, col, wblk, wblk, wblk),
        compiler_params=_params(("parallel",)),
    )(*args)


def _mm_pair(a1, b1, a2, b2, name, tm=1024, tn=512, after=None):
    m, kdim = a1.shape
    n = b1.shape[1]
    tm, tn = _tile(m, tm), _tile(n, tn)

    def body(a1_ref, b1_ref, a2_ref, b2_ref, *refs):
        refs[-1][...] = (jnp.dot(a1_ref[...], b1_ref[...], preferred_element_type=F32)
                         + jnp.dot(a2_ref[...], b2_ref[...], preferred_element_type=F32))

    a_spec = pl.BlockSpec((tm, kdim), lambda i, j: (i, 0))
    b_spec = pl.BlockSpec((kdim, tn), lambda i, j: (0, j))
    in_specs, args = [a_spec, b_spec, a_spec, b_spec], [a1, b1, a2, b2]
    if after is not None:
        in_specs.append(ANY_SPEC)
        args.append(after)
    return pl.pallas_call(
        body, name=name, grid=(m // tm, n // tn), out_shape=jax.ShapeDtypeStruct((m, n), F32),
        in_specs=in_specs, out_specs=pl.BlockSpec((tm, tn), lambda i, j: (i, j)),
        compiler_params=_params(("parallel", "parallel")),
    )(*args)


def _pool_counts(s):
    return (lax.broadcasted_iota(jnp.int32, (s, POOL_GC), 0))


def _pool_fwd(z, pool_w, pool_scale, name):
    s = z.shape[0]

    def body(u_ref, w_ref, sc_ref, y_ref, diff_ref):
        t = lax.broadcasted_iota(jnp.int32, (s, POOL_GC), 0)
        for g, win in enumerate(POOL_WINDOWS):
            cols = slice(g * POOL_GC, (g + 1) * POOL_GC)
            u = u_ref[:, cols]
            acc, step = u, 1
            while step < win:
                acc = acc + jnp.where(t >= step, pltpu.roll(acc, step, 0), 0.0)
                step *= 2
            cnt = jnp.minimum(t + 1, win).astype(F32)
            diff = acc / cnt - u
            diff_ref[:, cols] = diff
            ypre = jnp.dot(diff.astype(BF16), w_ref[g].astype(BF16), preferred_element_type=F32)
            y_ref[:, cols] = (ypre * sc_ref[:, cols]).astype(BF16)

    return pl.pallas_call(
        body, name=name, grid=(1,),
        out_shape=(jax.ShapeDtypeStruct((s, POOL_WIDTH), BF16), jax.ShapeDtypeStruct((s, POOL_WIDTH), F32)),
        in_specs=[pl.BlockSpec((s, POOL_WIDTH), lambda i: (0, 0)),
                  pl.BlockSpec(pool_w.shape, lambda i: (0, 0, 0)),
                  pl.BlockSpec((1, POOL_WIDTH), lambda i: (0, 0))],
        out_specs=(pl.BlockSpec((s, POOL_WIDTH), lambda i: (0, 0)),
                   pl.BlockSpec((s, POOL_WIDTH), lambda i: (0, 0))),
        compiler_params=_params(("arbitrary",)),
    )(z, pool_w, pool_scale)


def _pool_bwd(dycat, diff, pool_w, pool_scale, name):
    s = diff.shape[0]

    def body(dy_ref, diff_ref, w_ref, sc_ref, du_ref, dw_ref, dsc_ref):
        t = lax.broadcasted_iota(jnp.int32, (s, POOL_GC), 0)
        for g, win in enumerate(POOL_WINDOWS):
            cols = slice(g * POOL_GC, (g + 1) * POOL_GC)
            dy, dfb, wb = dy_ref[:, cols], diff_ref[:, cols].astype(BF16), w_ref[g].astype(BF16)
            ypre = jnp.dot(dfb, wb, preferred_element_type=F32)
            dsc_ref[:, cols] = jnp.sum(dy * ypre, axis=0, keepdims=True)
            dypre = (dy * sc_ref[:, cols]).astype(BF16)
            ddiff = lax.dot_general(dypre, wb, (((1,), (1,)), ((), ())), preferred_element_type=F32)
            dw_ref[g] = lax.dot_general(dfb, dypre, (((0,), (0,)), ((), ())), preferred_element_type=F32)
            cnt = jnp.minimum(t + 1, win).astype(F32)
            acc, step = ddiff / cnt, 1
            while step < win:
                acc = acc + jnp.where(t < s - step, pltpu.roll(acc, s - step, 0), 0.0)
                step *= 2
            du_ref[:, cols] = acc - ddiff

    full = pl.BlockSpec((s, POOL_WIDTH), lambda i: (0, 0))
    return pl.pallas_call(
        body, name=name, grid=(1,),
        out_shape=(jax.ShapeDtypeStruct((s, POOL_WIDTH), F32),
                   jax.ShapeDtypeStruct(pool_w.shape, F32),
                   jax.ShapeDtypeStruct((1, POOL_WIDTH), F32)),
        in_specs=[full, full, pl.BlockSpec(pool_w.shape, lambda i: (0, 0, 0)),
                  pl.BlockSpec((1, POOL_WIDTH), lambda i: (0, 0))],
        out_specs=(full, pl.BlockSpec(pool_w.shape, lambda i: (0, 0, 0)),
                   pl.BlockSpec((1, POOL_WIDTH), lambda i: (0, 0))),
        compiler_params=_params(("arbitrary",)),
    )(dycat, diff, pool_w, pool_scale)


def _rope_tables(positions, name):
    s = positions.shape[0]
    ts = _tile(s, 512)
    freq = 1.0 / (ROPE_THETA ** (np.arange(0, QK_ROPE, 2, dtype=np.float32) / QK_ROPE))
    table = np.zeros((1, LANE), np.float32)
    table[0, :QK_ROPE // 2] = freq
    table[0, QK_ROPE // 2:QK_ROPE] = freq

    def body(pos_ref, f_ref, cos_ref, sin_ref):
        ang = pos_ref[...].astype(F32) * f_ref[...]
        cos_ref[...] = jnp.cos(ang)
        sin_ref[...] = jnp.sin(ang)

    out = jax.ShapeDtypeStruct((s, LANE), F32)
    blk = pl.BlockSpec((ts, LANE), lambda i: (i, 0))
    return pl.pallas_call(
        body, name=name, grid=(s // ts,), out_shape=(out, out),
        in_specs=[pl.BlockSpec((ts, 1), lambda i: (i, 0)), _vec_spec(LANE)], out_specs=(blk, blk),
        compiler_params=_params(("parallel",)),
    )(positions, jnp.asarray(table))


def _lane_mod64_low(shape):
    return (lax.broadcasted_iota(jnp.int32, shape, 1) % QK_ROPE) < (QK_ROPE // 2)


def _rope(x, cos, sin):
    rot = jnp.where(_lane_mod64_low(x.shape), -pltpu.roll(x, LANE - 32, 1), pltpu.roll(x, 32, 1))
    return x * cos + rot * sin


def _rope_t(dy, cos, sin):
    w = dy * sin
    rot_t = jnp.where(_lane_mod64_low(dy.shape), pltpu.roll(w, LANE - 32, 1), -pltpu.roll(w, 32, 1))
    return dy * cos + rot_t


def _plain_rms(x, g):
    r = lax.rsqrt(jnp.mean(x * x, axis=-1, keepdims=True) + EPS)
    return (x * r) * g, x * r, r


O_Q, O_KV, O_KR = POOL_WIDTH, POOL_WIDTH + Q_LORA, POOL_WIDTH + Q_LORA + KV_LORA


def _qkv_fwd(z, qn, kvn, wq, wkv, cos, sin, name):
    s = z.shape[0]
    ts = _tile(s, 512)

    def body(z_ref, qn_ref, kvn_ref, wq_ref, wkv_ref, cos_ref, sin_ref, q_ref, k_ref, v_ref, cqn_ref, ckvn_ref):
        cosv, sinv = cos_ref[...], sin_ref[...]
        cqn = _plain_rms(z_ref[:, O_Q:O_KV], qn_ref[...])[0].astype(BF16)
        ckvn = _plain_rms(z_ref[:, O_KV:O_KR], kvn_ref[...])[0].astype(BF16)
        cqn_ref[...] = cqn
        ckvn_ref[...] = ckvn
        nt = (((1,), (1,)), ((), ()))
        q = lax.dot_general(cqn, wq_ref[...], nt, preferred_element_type=F32)
        kv = lax.dot_general(ckvn, wkv_ref[...], nt, preferred_element_type=F32)
        kr = _rope(z_ref[:, O_KR:IN_PAD], cosv, sinv).astype(BF16)
        for h in range(N_HEADS):
            o = h * HEAD_PAD
            q_ref[:, o:o + QK_NOPE] = q[:, o:o + QK_NOPE].astype(BF16)
            q_ref[:, o + QK_NOPE:o + HEAD_PAD] = _rope(q[:, o + QK_NOPE:o + HEAD_PAD], cosv, sinv).astype(BF16)
            k_ref[:, o:o + QK_NOPE] = kv[:, o:o + QK_NOPE].astype(BF16)
            k_ref[:, o + QK_NOPE:o + HEAD_PAD] = kr
            v_ref[:, h * V_HEAD:(h + 1) * V_HEAD] = kv[:, o + QK_NOPE:o + HEAD_PAD].astype(BF16)

    def row(w):
        return pl.BlockSpec((ts, w), lambda i: (i, 0))

    def whole(arr):
        return pl.BlockSpec(arr.shape, lambda i: (0, 0))

    hp = N_HEADS * HEAD_PAD
    return pl.pallas_call(
        body, name=name, grid=(s // ts,),
        out_shape=(jax.ShapeDtypeStruct((s, hp), BF16), jax.ShapeDtypeStruct((s, hp), BF16),
                   jax.ShapeDtypeStruct((s, N_HEADS * V_HEAD), BF16),
                   jax.ShapeDtypeStruct((s, Q_LORA), BF16), jax.ShapeDtypeStruct((s, KV_LORA), BF16)),
        in_specs=[row(IN_PAD), whole(qn), whole(kvn), whole(wq), whole(wkv), row(LANE), row(LANE)],
        out_specs=(row(hp), row(hp), row(N_HEADS * V_HEAD), row(Q_LORA), row(KV_LORA)),
        compiler_params=_params(("parallel",)),
    )(z, qn, kvn, wq, wkv, cos, sin)


def _qkv_bwd(dq, dk, dv, du, z, qn, kvn, wq, wkv, cos, sin, name):
    s = z.shape[0]
    ts = _tile(s, 512)

    def norm_bwd(x, g, dy):
        _, xn, r = _plain_rms(x, g)
        dxn = dy * g
        return r * (dxn - xn * jnp.mean(dxn * xn, axis=-1, keepdims=True)), jnp.sum(dy * xn, axis=0, keepdims=True)

    def body(dq_ref, dk_ref, dv_ref, du_ref, z_ref, qn_ref, kvn_ref, wq_ref, wkv_ref, cos_ref, sin_ref,
             dz_ref, dqb_ref, dkvb_ref, dqn_ref, dkvn_ref):
        @pl.when(pl.program_id(0) == 0)
        def _():
            dqn_ref[...] = jnp.zeros_like(dqn_ref)
            dkvn_ref[...] = jnp.zeros_like(dkvn_ref)

        cosv, sinv = cos_ref[...], sin_ref[...]
        dkr = jnp.zeros((ts, LANE), F32)
        for h in range(N_HEADS):
            o = h * HEAD_PAD
            dqb_ref[:, o:o + QK_NOPE] = dq_ref[:, o:o + QK_NOPE].astype(BF16)
            dqb_ref[:, o + QK_NOPE:o + HEAD_PAD] = _rope_t(dq_ref[:, o + QK_NOPE:o + HEAD_PAD], cosv, sinv).astype(BF16)
            dkvb_ref[:, o:o + QK_NOPE] = dk_ref[:, o:o + QK_NOPE].astype(BF16)
            dkvb_ref[:, o + QK_NOPE:o + HEAD_PAD] = dv_ref[:, h * V_HEAD:(h + 1) * V_HEAD].astype(BF16)
            dkr = dkr + dk_ref[:, o + QK_NOPE:o + HEAD_PAD]
        dcqn = jnp.dot(dqb_ref[...], wq_ref[...], preferred_element_type=F32)
        dckvn = jnp.dot(dkvb_ref[...], wkv_ref[...], preferred_element_type=F32)
        dcq, dqn = norm_bwd(z_ref[:, O_Q:O_KV], qn_ref[...], dcqn)
        dckv, dkvn = norm_bwd(z_ref[:, O_KV:O_KR], kvn_ref[...], dckvn)
        dqn_ref[...] += dqn
        dkvn_ref[...] += dkvn
        dz_ref[:, 0:O_Q] = du_ref[...].astype(BF16)
        dz_ref[:, O_Q:O_KV] = dcq.astype(BF16)
        dz_ref[:, O_KV:O_KR] = dckv.astype(BF16)
        dz_ref[:, O_KR:IN_PAD] = _rope_t(dkr, cosv, sinv).astype(BF16)

    def row(w):
        return pl.BlockSpec((ts, w), lambda i: (i, 0))

    def whole(arr):
        return pl.BlockSpec(arr.shape, lambda i: (0, 0))

    hp = N_HEADS * HEAD_PAD
    return pl.pallas_call(
        body, name=name, grid=(s // ts,),
        out_shape=(jax.ShapeDtypeStruct((s, IN_PAD), BF16), jax.ShapeDtypeStruct((s, hp), BF16),
                   jax.ShapeDtypeStruct((s, hp), BF16),
                   jax.ShapeDtypeStruct((1, Q_LORA), F32), jax.ShapeDtypeStruct((1, KV_LORA), F32)),
        in_specs=[row(hp), row(hp), row(N_HEADS * V_HEAD), row(POOL_WIDTH), row(IN_PAD),
                  whole(qn), whole(kvn), whole(wq), whole(wkv), row(LANE), row(LANE)],
        out_specs=(row(IN_PAD), row(hp), row(hp), whole(qn), whole(kvn)),
        compiler_params=_params(("arbitrary",)),
    )(dq, dk, dv, du, z, qn, kvn, wq, wkv, cos, sin)


def _causal_scores(q, k, i, tq, klen):
    sc = lax.dot_general(q, k, (((1,), (1,)), ((), ())), preferred_element_type=F32) * SOFTMAX_SCALE
    qpos = i * tq + lax.broadcasted_iota(jnp.int32, (tq, klen), 0)
    kpos = lax.broadcasted_iota(jnp.int32, (tq, klen), 1)
    return jnp.where(qpos >= kpos, sc, -jnp.inf)


ATTN_TQ = 512
ATTN_SEGMENTS = 4


def _by_key_prefix(i, nq, tq, compute):
    nseg = min(ATTN_SEGMENTS, nq)
    per = nq // nseg
    for r in range(nseg):
        pl.when(i // per == r)(lambda r=r: compute((r + 1) * per * tq))


def _attn_fwd(q, k, v, name):
    s = q.shape[0]
    tq = _tile(s, ATTN_TQ)
    nq = s // tq

    def body(q_ref, k_ref, v_ref, o_ref, lse_ref):
        i = pl.program_id(1)

        def compute(klen):
            sc = _causal_scores(q_ref[...], k_ref[0:klen, :], i, tq, klen)
            mx = jnp.max(sc, axis=-1, keepdims=True)
            p = jnp.exp(sc - mx)
            den = jnp.sum(p, axis=-1, keepdims=True)
            o_ref[...] = jnp.dot((p / den).astype(BF16), v_ref[0:klen, :], preferred_element_type=F32)
            lse_ref[...] = mx + jnp.log(den)

        _by_key_prefix(i, nq, tq, compute)

    return pl.pallas_call(
        body, name=name, grid=(N_HEADS, s // tq),
        out_shape=(jax.ShapeDtypeStruct((s, N_HEADS * V_HEAD), F32), jax.ShapeDtypeStruct((N_HEADS, s, 1), F32)),
        in_specs=[pl.BlockSpec((tq, HEAD_PAD), lambda h, i: (i, h)),
                  pl.BlockSpec((s, HEAD_PAD), lambda h, i: (0, h)),
                  pl.BlockSpec((s, V_HEAD), lambda h, i: (0, h))],
        out_specs=(pl.BlockSpec((tq, V_HEAD), lambda h, i: (i, h)),
                   pl.BlockSpec((None, tq, 1), lambda h, i: (h, i, 0))),
        compiler_params=_params(("parallel", "parallel")),
    )(q, k, v)


def _attn_bwd(q, k, v, lse, dycat, name):
    s = q.shape[0]
    tq = _tile(s, ATTN_TQ)
    nq = s // tq
    tn_dims = (((0,), (0,)), ((), ()))

    def body(q_ref, k_ref, v_ref, lse_ref, do_ref, dq_ref, dk_ref, dv_ref):
        i = pl.program_id(1)

        @pl.when(i == 0)
        def _():
            dk_ref[...] = jnp.zeros_like(dk_ref)
            dv_ref[...] = jnp.zeros_like(dv_ref)

        def compute(klen):
            qv, kv_, dob = q_ref[...], k_ref[0:klen, :], do_ref[...].astype(BF16)
            sc = _causal_scores(qv, kv_, i, tq, klen)
            p = jnp.exp(sc - lse_ref[...])
            dp = lax.dot_general(dob, v_ref[0:klen, :], (((1,), (1,)), ((), ())), preferred_element_type=F32)
            ds = (p * (dp - jnp.sum(dp * p, axis=-1, keepdims=True)) * SOFTMAX_SCALE).astype(BF16)
            dq_ref[...] = jnp.dot(ds, kv_, preferred_element_type=F32)
            dk_ref[0:klen, :] += lax.dot_general(ds, qv, tn_dims, preferred_element_type=F32)
            dv_ref[0:klen, :] += lax.dot_general(p.astype(BF16), dob, tn_dims, preferred_element_type=F32)

        _by_key_prefix(i, nq, tq, compute)

    n_pool_blocks = POOL_WIDTH // V_HEAD
    return pl.pallas_call(
        body, name=name, grid=(N_HEADS, s // tq),
        out_shape=(jax.ShapeDtypeStruct((s, N_HEADS * HEAD_PAD), F32),
                   jax.ShapeDtypeStruct((s, N_HEADS * HEAD_PAD), F32),
                   jax.ShapeDtypeStruct((s, N_HEADS * V_HEAD), F32)),
        in_specs=[pl.BlockSpec((tq, HEAD_PAD), lambda h, i: (i, h)),
                  pl.BlockSpec((s, HEAD_PAD), lambda h, i: (0, h)),
                  pl.BlockSpec((s, V_HEAD), lambda h, i: (0, h)),
                  pl.BlockSpec((None, tq, 1), lambda h, i: (h, i, 0)),
                  pl.BlockSpec((tq, V_HEAD), lambda h, i: (i, n_pool_blocks + h))],
        out_specs=(pl.BlockSpec((tq, HEAD_PAD), lambda h, i: (i, h)),
                   pl.BlockSpec((s, HEAD_PAD), lambda h, i: (0, h)),
                   pl.BlockSpec((s, V_HEAD), lambda h, i: (0, h))),
        compiler_params=_params(("parallel", "arbitrary")),
    )(q, k, v, lse, dycat)


def _loss_head(x, gw, target, below, name):
    s, d = x.shape
    ts = _tile(s, 512)
    factor = below[2]

    def body(x_ref, gw_ref, tgt_ref, yb_ref, gb_ref, loss_ref, dx_ref, dgw_ref, dyb_ref, dgb_ref):
        @pl.when(pl.program_id(0) == 0)
        def _():
            loss_ref[...] = jnp.zeros_like(loss_ref)
            dgw_ref[...] = jnp.zeros_like(dgw_ref)
            dgb_ref[...] = jnp.zeros_like(dgb_ref)

        xv, gwv = x_ref[...], gw_ref[...]
        r = lax.rsqrt(jnp.mean(xv * xv, axis=-1, keepdims=True) + EPS)
        xn = xv * r
        err = xn * gwv - tgt_ref[...]
        loss_ref[...] += 0.5 * jnp.sum(jnp.mean(err * err, axis=-1, keepdims=True))
        dy = err / d
        dgw_ref[...] += jnp.sum(dy * xn, axis=0, keepdims=True)
        dxn = dy * gwv
        dx = r * (dxn - xn * jnp.mean(dxn * xn, axis=-1, keepdims=True))
        dx_ref[...] = dx
        dyb_ref[...] = ((factor * gb_ref[...]) * dx).astype(BF16)
        dgb_ref[...] += jnp.sum((factor * dx) * yb_ref[...].astype(F32), axis=0, keepdims=True)

    row = pl.BlockSpec((ts, d), lambda i: (i, 0))
    gate_spec, gate_arg = _vec_in(below[1])
    vec = jax.ShapeDtypeStruct((1, d), F32)
    return pl.pallas_call(
        body, name=name, grid=(s // ts,),
        out_shape=(jax.ShapeDtypeStruct((8, LANE), F32), jax.ShapeDtypeStruct((s, d), F32), vec,
                   jax.ShapeDtypeStruct((s, d), BF16), vec),
        in_specs=[row, _vec_spec(d), row, row, gate_spec],
        out_specs=(pl.BlockSpec((8, LANE), lambda i: (0, 0)), row, _vec_spec(d), row, _vec_spec(d)),
        compiler_params=_params(("arbitrary",)),
    )(x, gw, target, below[0], gate_arg)


def _ada_mod(c_all, ada_w, ada_b, name):
    nl, d, cols = ada_w.shape

    def body(c_ref, w_ref, b_ref, o_ref):
        cv = c_ref[...]
        act = (cv * jax.nn.sigmoid(cv)).astype(BF16)
        o_ref[...] = jnp.dot(act, w_ref[...].astype(BF16), preferred_element_type=F32) + b_ref[...]

    return pl.pallas_call(
        body, name=name, grid=(nl,), out_shape=jax.ShapeDtypeStruct((nl, N_DEV, cols), F32),
        in_specs=[pl.BlockSpec((N_DEV, d), lambda l: (0, 0)),
                  pl.BlockSpec((None, d, cols), lambda l: (l, 0, 0)),
                  pl.BlockSpec((None, 1, cols), lambda l: (l, 0, 0))],
        out_specs=pl.BlockSpec((None, N_DEV, cols), lambda l: (l, 0, 0)),
        compiler_params=_params(("parallel",)),
    )(c_all, ada_w, ada_b)


def _ada_grad(c_pad, dmod_pad, name):
    nl, kpad, cols = dmod_pad.shape
    d = c_pad.shape[1]

    def body(c_ref, dm_ref, o_ref):
        cv = c_ref[...]
        act = (cv * jax.nn.sigmoid(cv)).astype(BF16)
        o_ref[...] = lax.dot_general(act, dm_ref[...].astype(BF16), (((0,), (0,)), ((), ())),
                                     preferred_element_type=F32)

    return pl.pallas_call(
        body, name=name, grid=(nl,), out_shape=jax.ShapeDtypeStruct((nl, d, cols), F32),
        in_specs=[pl.BlockSpec((kpad, d), lambda l: (0, 0)),
                  pl.BlockSpec((None, kpad, cols), lambda l: (l, 0, 0))],
        out_specs=pl.BlockSpec((None, d, cols), lambda l: (l, 0, 0)),
        compiler_params=_params(("parallel",)),
    )(c_pad, dmod_pad)


def _adamw_math(w, g, m, v):
    nm = ADAM_B1 * m + (1.0 - ADAM_B1) * g
    nv = ADAM_B2 * v + (1.0 - ADAM_B2) * (g * g)
    m_hat = nm / (1.0 - ADAM_B1 ** ADAM_STEP)
    v_hat = nv / (1.0 - ADAM_B2 ** ADAM_STEP)
    return -ADAM_LR * (m_hat / (jnp.sqrt(v_hat) + ADAM_EPS) + ADAM_WD * w), nm, nv


def _adamw_rows(w3, gbuf, row_off, m3, v3, name):
    nl, r, d = w3.shape
    tr = _row_tile(math.gcd(r, row_off) if row_off else r, 352)
    first = row_off // tr

    def body(w_ref, g_ref, m_ref, v_ref, go_ref, d_ref, nm_ref, nv_ref):
        gv = g_ref[...]
        go_ref[...] = gv
        d_ref[...], nm_ref[...], nv_ref[...] = _adamw_math(w_ref[...], gv, m_ref[...], v_ref[...])

    blk = pl.BlockSpec((None, tr, d), lambda l, i: (l, i, 0))
    gblk = pl.BlockSpec((None, tr, d), lambda l, i: (l, first + i, 0))
    out = jax.ShapeDtypeStruct((nl, r, d), F32)
    return pl.pallas_call(
        body, name=name, grid=(nl, r // tr), out_shape=(out, out, out, out),
        in_specs=[blk, gblk, blk, blk], out_specs=(blk, blk, blk, blk),
        compiler_params=_params(("parallel", "parallel")),
    )(w3, gbuf, m3, v3)


def _adamw(w, g, m, v, name):
    rows, cols = w.shape
    tr = _row_tile(rows, 512)

    def body(w_ref, g_ref, m_ref, v_ref, d_ref, nm_ref, nv_ref):
        d_ref[...], nm_ref[...], nv_ref[...] = _adamw_math(w_ref[...], g_ref[...], m_ref[...], v_ref[...])

    blk = pl.BlockSpec((tr, cols), lambda i: (i, 0))
    out = jax.ShapeDtypeStruct((rows, cols), F32)
    return pl.pallas_call(
        body, name=name, grid=(rows // tr,), out_shape=(out, out, out),
        in_specs=[blk, blk, blk, blk], out_specs=(blk, blk, blk),
        compiler_params=_params(("parallel",)),
    )(w, g, m, v)


def _adamw_nd(w, g, m, v, name):
    shape = w.shape
    flat = (lambda t: t.reshape(1, -1)) if w.ndim == 1 else (lambda t: t.reshape(-1, shape[-1]))
    return tuple(t.reshape(shape) for t in _adamw(flat(w), flat(g), flat(m), flat(v), name))


def _pad_rows(t, rows):
    return jnp.pad(t, ((0, rows - t.shape[0]), (0, 0)))


def _pack_shard_layer(l, wts):
    def tr(name):
        return wts[name][l].astype(BF16).T

    parts = [tr("ffn1_w_gate"), tr("ffn1_w_up"), wts["ffn1_w_down"][l].astype(BF16),
             tr("ffn2_w_gate"), tr("ffn2_w_up"), wts["ffn2_w_down"][l].astype(BF16),
             wts["w_out"][l].astype(BF16),
             tr("w_kv_b").reshape(KV_SH_ROWS, D_MODEL),
             _pad_rows(tr("w_in"), 160),
             _pad_rows(tr("w_q_b").reshape(Q_SH_ROWS, D_MODEL), Q_PAD_ROWS)]
    return jnp.concatenate(parts, axis=0)


def _mixer_weights(w_out, small):
    w = {"out": w_out}
    small = small.reshape(N_DEV, SMALL_ROWS, D_MODEL)
    o_in, o_q = OFF_IN - OFF_KV, OFF_Q - OFF_KV
    w["kv"] = small[:, :KV_SH_ROWS].reshape(N_HEADS * HEAD_PAD, KV_LORA)
    w["in"] = _pad_rows(small[:, o_in:o_in + IN_SH].reshape(IN_COLS, D_MODEL), IN_PAD)
    wq = small[:, o_q:o_q + Q_SH_ROWS].reshape(N_HEADS, QK_HEAD, Q_LORA)
    w["q"] = jnp.pad(wq, ((0, 0), (0, HEAD_PAD - QK_HEAD), (0, 0))).reshape(N_HEADS * HEAD_PAD, Q_LORA)
    return w


def _grad_sources_tail(gr):
    gq = gr["q"].reshape(N_HEADS, HEAD_PAD, Q_LORA)[:, :QK_HEAD].reshape(N_DEV, Q_SH_ROWS, D_MODEL)
    small = jnp.concatenate([
        gr["kv"].reshape(N_DEV, KV_SH_ROWS, D_MODEL),
        jnp.pad(gr["in"][:IN_COLS].reshape(N_DEV, IN_SH, D_MODEL), ((0, 0), (0, 160 - IN_SH), (0, 0))),
        jnp.pad(gq, ((0, 0), (0, Q_PAD_ROWS - Q_SH_ROWS), (0, 0)))], axis=1)
    return [gr["out"], small.reshape(N_DEV * SMALL_ROWS, D_MODEL)]


def _pack_bf16_pairs(t):
    rows, d = t.shape
    return lax.bitcast_convert_type(t.astype(BF16).reshape(rows // 2, 2, d).transpose(0, 2, 1), F32)


def _unpack_bf16_pairs(p):
    pairs = jnp.swapaxes(lax.bitcast_convert_type(p, BF16), -1, -2)
    return pairs.reshape(p.shape[:-2] + (2 * p.shape[-2], p.shape[-1]))


def _small_layout(nl):
    names = [("dmod", nl * N_MOD), ("ffn1_norm", nl), ("mix_norm", nl), ("ffn2_norm", nl), ("q_a_norm", nl),
             ("kv_a_norm", nl), ("pool_scale", nl), ("final_norm", 1), ("loss", 1),
             ("pool_w", nl * 4 * POOL_GC * POOL_GC // D_MODEL // 2)]
    off, table = 0, {}
    for name, n in names:
        table[name] = (off, n)
        off += -(-n // 8) * 8
    return table, off


def _to_rows(t, width=D_MODEL):
    n, w = t.shape
    return jnp.pad(t, ((0, -(-n // 8) * 8 - n), (0, width - w)))


def kernel(x, c, positions, ada_w, ada_b, ffn1_norm, ffn1_w_gate, ffn1_w_up, ffn1_w_down, mix_norm, w_in, pool_w, pool_scale, q_a_norm, w_q_b, kv_a_norm, w_kv_b, w_out, ffn2_norm, ffn2_w_gate, ffn2_w_up, ffn2_w_down, final_norm, loss_target, m_ada_w, m_ada_b, m_ffn1_norm, m_ffn1_w_gate, m_ffn1_w_up, m_ffn1_w_down, m_mix_norm, m_w_in, m_pool_w, m_pool_scale, m_q_a_norm, m_w_q_b, m_kv_a_norm, m_w_kv_b, m_w_out, m_ffn2_norm, m_ffn2_w_gate, m_ffn2_w_up, m_ffn2_w_down, m_final_norm, v_ada_w, v_ada_b, v_ffn1_norm, v_ffn1_w_gate, v_ffn1_w_up, v_ffn1_w_down, v_mix_norm, v_w_in, v_pool_w, v_pool_scale, v_q_a_norm, v_w_q_b, v_kv_a_norm, v_w_kv_b, v_w_out, v_ffn2_norm, v_ffn2_w_gate, v_ffn2_w_up, v_ffn2_w_down, v_final_norm):
    wts = dict(ada_w=ada_w, ada_b=ada_b, ffn1_norm=ffn1_norm, ffn1_w_gate=ffn1_w_gate, ffn1_w_up=ffn1_w_up,
               ffn1_w_down=ffn1_w_down, mix_norm=mix_norm, w_in=w_in, pool_w=pool_w, pool_scale=pool_scale,
               q_a_norm=q_a_norm, w_q_b=w_q_b, kv_a_norm=kv_a_norm, w_kv_b=w_kv_b, w_out=w_out,
               ffn2_norm=ffn2_norm, ffn2_w_gate=ffn2_w_gate, ffn2_w_up=ffn2_w_up, ffn2_w_down=ffn2_w_down,
               final_norm=final_norm)
    mom_m = dict(ada_w=m_ada_w, ada_b=m_ada_b, ffn1_norm=m_ffn1_norm, ffn1_w_gate=m_ffn1_w_gate,
                 ffn1_w_up=m_ffn1_w_up, ffn1_w_down=m_ffn1_w_down, mix_norm=m_mix_norm, w_in=m_w_in,
                 pool_w=m_pool_w, pool_scale=m_pool_scale, q_a_norm=m_q_a_norm, w_q_b=m_w_q_b,
                 kv_a_norm=m_kv_a_norm, w_kv_b=m_w_kv_b, w_out=m_w_out, ffn2_norm=m_ffn2_norm,
                 ffn2_w_gate=m_ffn2_w_gate, ffn2_w_up=m_ffn2_w_up, ffn2_w_down=m_ffn2_w_down,
                 final_norm=m_final_norm)
    mom_v = dict(ada_w=v_ada_w, ada_b=v_ada_b, ffn1_norm=v_ffn1_norm, ffn1_w_gate=v_ffn1_w_gate,
                 ffn1_w_up=v_ffn1_w_up, ffn1_w_down=v_ffn1_w_down, mix_norm=v_mix_norm, w_in=v_w_in,
                 pool_w=v_pool_w, pool_scale=v_pool_scale, q_a_norm=v_q_a_norm, w_q_b=v_w_q_b,
                 kv_a_norm=v_kv_a_norm, w_kv_b=v_w_kv_b, w_out=v_w_out, ffn2_norm=v_ffn2_norm,
                 ffn2_w_gate=v_ffn2_w_gate, ffn2_w_up=v_ffn2_w_up, ffn2_w_down=v_ffn2_w_down,
                 final_norm=v_final_norm)
    order = list(wts)
    nl = ada_w.shape[0]
    seq = x.shape[1]
    me = 4 * lax.axis_index("x") + 2 * lax.axis_index("y") + lax.axis_index("c")
    ada_cols = ada_w.shape[2]

    def after_token(t, token):
        return t + token[0:1, 0:1].astype(t.dtype)

    packs = [_pack_shard_layer(l, wts) for l in range(nl)]

    c_all = _all_gather(jnp.broadcast_to(c, (8, D_MODEL)), "gather_c")[::8]

    ada_b_mine = lax.dynamic_slice_in_dim(ada_b, me * ada_cols, ada_cols, axis=1).reshape(nl, 1, ada_cols)
    mod_part = _ada_mod(c_all, ada_w, ada_b_mine, "ada_mod")
    mod_all = _all_gather(mod_part.reshape(nl * N_DEV, ada_cols), "gather_mod")
    mod_all = mod_all.reshape(N_DEV, nl, N_DEV, ada_cols)
    mod = lax.dynamic_index_in_dim(mod_all, me, axis=2, keepdims=False)
    mod = mod.transpose(1, 0, 2).reshape(nl * N_MOD, 1, D_MODEL)
    norm_tables = {name: wts[name].reshape(nl, 1, D_MODEL) for name in ("ffn1_norm", "mix_norm", "ffn2_norm")}

    def modrow(l, k):
        return mod, l * N_MOD + k

    def normrow(name, l):
        return norm_tables[name], l

    def start_layer(l, after):
        first = _gather_start(packs[l][:SPLIT_AB], ROWS_A, after, f"gather_start_{l}a")
        mixer = _gather_start(packs[l][OFF_OUT:], ROWS_TAIL, first[4], f"gather_start_{l}b")
        second = _gather_start(packs[l][SPLIT_AB:OFF_OUT], ROWS_A, mixer[4], f"gather_start_{l}c")
        return first, mixer, second

    flights = {0: start_layer(0, mod)}
    if nl > 1:
        flights[1] = start_layer(1, flights[0][2][4])
    last_start = flights[min(1, nl - 1)][2][4]

    cos, sin = _rope_tables(after_token(positions.reshape(seq, 1), last_start), "rope_tables")

    def vec(t):
        return t.reshape(1, -1)

    def landed(flight, rows_list, after, tag):
        send_sems, recv_sems, pk, lands, _ = flight
        pk, lands = _gather_wait(send_sems, recv_sems, pk, lands, after, f"gather_wait_{tag}")
        return _gather_finish(pk, rows_list, lands, "gather_finish")

    xs = x.reshape(seq, D_MODEL)
    saved = []
    for l in range(nl):
        norm1, up_after = normrow("ffn1_norm", l), None
        flight_a, flight_b, flight_c = flights[l]
        lands = landed(flight_a, ROWS_A, cos if l == 0 else xs, f"{l}a")
        if l >= 1 and l + 1 < nl:
            flights[l + 1] = start_layer(l + 1, lands[0])
            up_after = flights[l + 1][2][4]
        sv = {}

        def ffn_fwd(xin, norm, k0, wg, wu, wd, tag, after=None):
            h, a, b, t = _ffn_up(xin, norm, modrow(l, k0), modrow(l, k0 + 1), wg, wu, "ffn_up", after=after)
            y, xout = _mm(t, wd, "nn", "ffn_down", res=xin, gate=modrow(l, k0 + 2), gate_factor=0.5, tm=1024)
            sv[tag] = dict(x=xin, h=h, a=a, b=b, t=t, y=y)
            return xout

        xs = ffn_fwd(xs, norm1, 0, lands[0], lands[1], lands[2], "f1", up_after)
        w = dict(zip(("g1", "u1", "d1"), lands[:3]))
        w.update(_mixer_weights(*landed(flight_b, ROWS_TAIL, xs, f"{l}b")))
        sv["w"] = w

        h2, z = _norm_mm(xs, normrow("mix_norm", l), modrow(l, 3), modrow(l, 4), w["in"], "mix_in")
        y_pool, diff = _pool_fwd(z, pool_w[l], vec(pool_scale[l]), "pool_fwd")
        q, k, v, cqn, ckvn = _qkv_fwd(z, vec(q_a_norm[l]), vec(kv_a_norm[l]), w["q"], w["kv"], cos, sin, "qkv_fwd")
        o, lse = _attn_fwd(q, k, v, "attn_fwd")
        ycat = jnp.concatenate([y_pool, o.astype(BF16)], axis=1)
        y2, xmix = _mm(ycat, w["out"], "nn", "mix_out", res=xs, gate=modrow(l, 5), gate_factor=1.0)
        sv["mix"] = dict(x=xs, h=h2, z=z, diff=diff, q=q, k=k, v=v, cqn=cqn, ckvn=ckvn, lse=lse, ycat=ycat, y=y2)
        xs = xmix

        w.update(zip(("g2", "u2", "d2"), landed(flight_c, ROWS_A, xs, f"{l}c")))
        xs = ffn_fwd(xs, normrow("ffn2_norm", l), 6, w["g2"], w["u2"], w["d2"], "f2")
        saved.append(sv)

    loss_part, dx, d_final, *head = _loss_head(xs, vec(final_norm), loss_target.reshape(seq, D_MODEL),
                                               (saved[nl - 1]["f2"]["y"], modrow(nl - 1, 8), 0.5), "loss_head")

    small = {name: [None] * nl for name in ("ffn1_norm", "mix_norm", "ffn2_norm", "q_a_norm", "kv_a_norm",
                                            "pool_scale", "pool_w", "dmod")}
    core = lax.axis_index("c").astype(jnp.int32).reshape(1)
    chip = 2 * lax.axis_index("x") + lax.axis_index("y")
    exchanges = []

    def leave(srcs, rows_list, after, tag):
        return _pair_start(srcs, rows_list, after, f"pair_start_{tag}"), rows_list, tag

    def forward_on(pending, after, layer, row_off):
        (send_sems, recv_sems, srcs, land, _), rows_list, tag = pending
        srcs, land = _split_wait(send_sems, recv_sems, 1, srcs, land, after, f"pair_wait_{tag}")
        sums = _pair_sum(srcs, rows_list, land, core, "pair_sum")
        flight = _chip_exchange_start(sums, chip, after, f"exchange_start_{tag}")
        exchanges.append((flight, layer, row_off, tag))
        return flight[4]

    pending = None
    for l in reversed(range(nl)):
        sv = saved[l]
        w = sv["w"]
        dmod = [None] * N_MOD
        gr = {}

        def ffn_bwd(dxin, head, s_, norm, k0, wg, wu, wd, tag, below, first_after=None, mid=None):
            dy, dmod[k0 + 2] = head
            da, db, gr["d" + tag], gr["g" + tag], gr["u" + tag] = _ffn_bwd_cols(
                dy, s_["h"], s_["a"], s_["b"], s_["t"], wd, "ffn_bwd_cols", after=first_after)
            dh = _mm_pair(da, wg, db, wu, "ffn_bwd_dh", after=None if mid is None else mid(da))
            outs = _rm_bwd(dh, s_["x"], dxin, norm, modrow(l, k0 + 1), "rm_bwd", below=below)
            dmod[k0], dmod[k0 + 1] = outs[1], outs[2]
            return outs[0], outs[3], outs[4:]

        s_ = sv["mix"]
        dx, small["ffn2_norm"][l], head = ffn_bwd(
            dx, head, sv["f2"], normrow("ffn2_norm", l), 6, w["g2"], w["u2"], w["d2"], "2", (s_["y"], modrow(l, 5), 1.0),
            first_after=None if pending is None else pending[0][4])

        pending_c = leave([gr["g2"], gr["u2"], gr["d2"]], ROWS_A, dx, f"{l}c")
        mix_after = pending_c[0][4]
        if pending is not None:
            mix_after = forward_on(pending, mix_after, l + 1, GB_F1)
            pending = None
        dy, dmod[5] = head
        gr["out"] = _mm(s_["ycat"], dy, "tn", "mix_out_dw", out_dtype=BF16, tm=512, after=mix_after)
        dycat = _mm(dy, w["out"], "nt", "mix_out_dx", tm=1024)
        du, small["pool_w"][l], small["pool_scale"][l] = _pool_bwd(dycat, s_["diff"], pool_w[l], vec(pool_scale[l]), "pool_bwd")
        dq, dk, dv = _attn_bwd(s_["q"], s_["k"], s_["v"], s_["lse"], dycat, "attn_bwd")
        dz, dqb, dkvb, small["q_a_norm"][l], small["kv_a_norm"][l] = _qkv_bwd(
            dq, dk, dv, du, s_["z"], vec(q_a_norm[l]), vec(kv_a_norm[l]), w["q"], w["kv"], cos, sin, "qkv_bwd")
        gr["q"] = _mm(dqb, s_["cqn"], "tn", "q_b_dw", out_dtype=BF16, tm=512, after=forward_on(pending_c, dz, l, GB_F2))
        gr["kv"] = _mm(dkvb, s_["ckvn"], "tn", "kv_b_dw", out_dtype=BF16, tm=512)
        gr["in"] = _mm(dz, s_["h"], "tn", "mix_in_dw", out_dtype=BF16, tm=512)
        dh2 = _mm(dz, w["in"], "nn", "mix_in_dx", tm=1024)
        outs = _rm_bwd(dh2, s_["x"], dx, normrow("mix_norm", l), modrow(l, 4), "rm_bwd",
                       below=(sv["f1"]["y"], modrow(l, 2), 0.5))
        dx, dmod[3], dmod[4], small["mix_norm"][l] = outs[:4]
        head = outs[4:]

        first_after, mid = None, None
        if l == 0:
            pending_b = leave(_grad_sources_tail(gr), ROWS_TAIL, dx, "0b")
            first_after = pending_b[0][4]
            last_groups = []

            def mid(da):
                last_groups.append(leave([gr["g1"], gr["u1"], gr["d1"]], ROWS_A, da, "0a"))
                return forward_on(pending_b, last_groups[0][0][4], 0, GB_TAIL)
        below = (saved[l - 1]["f2"]["y"], modrow(l - 1, 8), 0.5) if l > 0 else None
        dx, small["ffn1_norm"][l], head = ffn_bwd(
            dx, head, sv["f1"], normrow("ffn1_norm", l), 0, w["g1"], w["u1"], w["d1"], "1", below, first_after, mid)

        small["dmod"][l] = jnp.concatenate(dmod, axis=0)
        if l > 0:
            pending = leave([gr["g1"], gr["u1"], gr["d1"]] + _grad_sources_tail(gr), ROWS_A + ROWS_TAIL, dx, l)

    grad_x = dx.reshape(x.shape)
    pending_a = last_groups[0]

    layout, small_rows = _small_layout(nl)
    pieces = {
        "dmod": jnp.concatenate(small["dmod"], axis=0),
        "ffn1_norm": jnp.concatenate(small["ffn1_norm"], axis=0),
        "mix_norm": jnp.concatenate(small["mix_norm"], axis=0),
        "ffn2_norm": jnp.concatenate(small["ffn2_norm"], axis=0),
        "q_a_norm": jnp.concatenate(small["q_a_norm"], axis=0),
        "kv_a_norm": jnp.concatenate(small["kv_a_norm"], axis=0),
        "pool_scale": jnp.concatenate(small["pool_scale"], axis=0),
        "final_norm": d_final,
        "loss": jnp.broadcast_to(loss_part[0:1, 0:1], (1, D_MODEL)),
        "pool_w": _pack_bf16_pairs(jnp.stack(small["pool_w"]).reshape(-1, D_MODEL)),
    }
    small_buf = jnp.concatenate([_to_rows(pieces[name]) for name in layout], axis=0)
    def landed_sums(gbuf, entries, after):
        for (send_sems, recv_sems, sums, recv, _), layer, row_off, tag in entries:
            _, recv = _split_wait(send_sems, recv_sems, N_CHIPS - 1, sums, recv, after, f"exchange_wait_{tag}")
            gbuf = _sum_slots_into(recv, gbuf, layer, row_off, "sum_grads")
        return gbuf

    gbuf = lax.empty((nl, ROWS_L, D_MODEL), F32)
    token_0a = forward_on(pending_a, dx, 0, GB_F1)
    spread = _spread_start(small_buf, me, token_0a, "small_start")
    gbuf = landed_sums(gbuf, [e for e in exchanges if e[3] != "0a"], spread[4])

    def swap(t):
        return t.transpose(0, 2, 1)

    def same(t):
        return t

    grads, updates = {}, {}

    def update_rows(gbuf, table):
        for wname, off, view in table:
            g, d_, nm, nv = _adamw_rows(view(wts[wname]), gbuf, off, view(mom_m[wname]), view(mom_v[wname]), "adamw_rows")
            grads[wname], updates[wname] = view(g), (view(d_), view(nm), view(nv))

    update_rows(gbuf, (("ffn2_w_gate", GB_F2, swap), ("ffn2_w_up", GB_F2 + FF_SH, swap),
                       ("ffn2_w_down", GB_F2 + 2 * FF_SH, same), ("w_out", GB_TAIL, same)))
    small_grads = {
        "w_kv_b": (gbuf[:, OFF_KV:OFF_KV + KV_SH_ROWS].reshape(nl, -1, KV_LORA).transpose(0, 2, 1), same),
        "w_in": (gbuf[:, OFF_IN:OFF_IN + IN_SH], swap),
        "w_q_b": (gbuf[:, OFF_Q:OFF_Q + Q_SH_ROWS].reshape(nl, -1, Q_LORA), swap),
    }
    for wname, (g, view) in small_grads.items():
        upd = _adamw_nd(view(wts[wname]), g, view(mom_m[wname]), view(mom_v[wname]), "adamw")
        grads[wname], updates[wname] = view(g), tuple(view(t) for t in upd)
    gbuf = landed_sums(gbuf, [e for e in exchanges if e[3] == "0a"], updates["w_q_b"][0])
    update_rows(gbuf, (("ffn1_w_gate", GB_F1, swap), ("ffn1_w_up", GB_F1 + FF_SH, swap),
                       ("ffn1_w_down", GB_F1 + 2 * FF_SH, same)))

    _, small_all = _split_wait(spread[0], spread[1], N_DEV - 1, spread[2], spread[3], updates["ffn1_w_down"][0],
                               "small_wait")
    pool_off, pool_rows = layout["pool_w"]
    small_sum = _sum_slots(small_all[:, :pool_off], "sum_small")
    pool_sum = _sum_slots(_unpack_bf16_pairs(small_all[:, pool_off:pool_off + pool_rows]), "sum_pool_w")

    def take(name, width=D_MODEL):
        off, n = layout[name]
        return small_sum[off:off + n, :width]

    late = {"ada_b": take("dmod").reshape(nl, N_MOD * D_MODEL),
            "ffn1_norm": take("ffn1_norm"), "mix_norm": take("mix_norm"), "ffn2_norm": take("ffn2_norm"),
            "q_a_norm": take("q_a_norm", Q_LORA), "kv_a_norm": take("kv_a_norm", KV_LORA),
            "pool_scale": take("pool_scale", POOL_WIDTH), "final_norm": take("final_norm").reshape(D_MODEL),
            "pool_w": pool_sum.reshape(pool_w.shape)}
    loss = take("loss")[0, 0]

    off, n = layout["dmod"]
    dmod_all = small_all[:, off:off + n].reshape(N_DEV, nl, N_MOD * D_MODEL)
    dmod_mine = lax.dynamic_slice_in_dim(dmod_all, me * ada_cols, ada_cols, axis=2)
    dmod_pad = jnp.pad(dmod_mine.transpose(1, 0, 2), ((0, 0), (0, LANE - N_DEV), (0, 0)))
    late["ada_w"] = _ada_grad(jnp.pad(c_all, ((0, LANE - N_DEV), (0, 0))), dmod_pad, "ada_grad")
    for name, g in late.items():
        grads[name], updates[name] = g, _adamw_nd(wts[name], g, mom_m[name], mom_v[name], "adamw")

    return (loss, grad_x, *[grads[n] for n in order], *[updates[n][0] for n in order],
            *[updates[n][1] for n in order], *[updates[n][2] for n in order])
```

```python
import math

import numpy as np
import jax
import jax.numpy as jnp
from jax import lax
from jax.experimental import pallas as pl
from jax.experimental.pallas import tpu as pltpu

F32 = jnp.float32
BF16 = jnp.bfloat16

N_DEV = 8
D_MODEL = 1024
D_FF = 2816
POOL_WIDTH = 512
POOL_WINDOWS = (2, 4, 8, 16)
POOL_GC = 128
N_HEADS = 4
QK_NOPE = 128
QK_ROPE = 64
V_HEAD = 128
QK_HEAD = QK_NOPE + QK_ROPE
HEAD_PAD = 256
Q_LORA = 384
KV_LORA = 256
IN_COLS = POOL_WIDTH + Q_LORA + KV_LORA + QK_ROPE
IN_PAD = 1280
ROPE_THETA = 10000.0
SOFTMAX_SCALE = 1.0 / math.sqrt(QK_HEAD)
EPS = 1e-6
N_MOD = 9

ADAM_LR = 0.001
ADAM_B1 = 0.9
ADAM_B2 = 0.999
ADAM_EPS = 1e-08
ADAM_WD = 0.01
ADAM_STEP = 10

LANE = 128
VMEM_LIMIT = 56 * 1024 * 1024

FF_SH = D_FF // N_DEV
OFF_G1, OFF_U1, OFF_D1 = 0, FF_SH, 2 * FF_SH
OFF_G2, OFF_U2, OFF_D2 = 3 * FF_SH, 4 * FF_SH, 5 * FF_SH
OFF_OUT = 6 * FF_SH
OFF_KV = OFF_OUT + 128
OFF_IN = OFF_KV + 32
OFF_Q = OFF_IN + 160
Q_PAD_ROWS = 64
ROWS_L = OFF_Q + Q_PAD_ROWS
IN_SH = IN_COLS // N_DEV
Q_SH_ROWS = (N_HEADS * QK_HEAD // N_DEV) * Q_LORA // D_MODEL
KV_SH_ROWS = (N_HEADS * (QK_NOPE + V_HEAD) // N_DEV) * KV_LORA // D_MODEL


def _tile(dim, target):
    if dim <= target:
        return dim
    best = None
    for t in range(LANE, target + 1, LANE):
        if dim % t == 0:
            best = t
    assert best is not None, (dim, target)
    return best


def _params(sem):
    return pltpu.CompilerParams(dimension_semantics=sem, vmem_limit_bytes=VMEM_LIMIT)


def _mesh_pos():
    return lax.axis_index("x"), lax.axis_index("y"), lax.axis_index("c")


def _all_gather(x, name):
    m, n = x.shape

    def body(x_ref, out_ref, send_sems, recv_sems, local_sem):
        px, py, pc = _mesh_pos()
        me, sibling = (px, py, pc), (px, py, 1 - pc)
        chips = [(1 - px, py), (px, 1 - py), (1 - px, 1 - py)]

        def rows(bx, by, bc):
            return out_ref.at[pl.ds((4 * bx + 2 * by + bc) * m, m), :]

        def copy(k, block, to, src=None):
            return pltpu.make_async_remote_copy(
                src_ref=rows(*block) if src is None else src, dst_ref=rows(*block),
                send_sem=send_sems.at[k], recv_sem=recv_sems.at[k],
                device_id=to, device_id_type=pl.DeviceIdType.MESH)

        mine = pltpu.make_async_copy(x_ref, rows(*me), local_sem)
        mine.start()
        first = [copy(0, me, sibling, src=x_ref)]
        first += [copy(1 + j, me, (*chip, pc), src=x_ref) for j, chip in enumerate(chips)]
        for cp in first:
            cp.start()
        passed = [copy(4 + j, (*chip, pc), sibling) for j, chip in enumerate(chips)]
        for j, chip in enumerate(chips):
            copy(1 + j, (*chip, pc), me).wait_recv()
            passed[j].start()
        copy(0, sibling, me).wait_recv()
        for j, chip in enumerate(chips):
            copy(4 + j, (*chip, 1 - pc), me).wait_recv()
        for cp in first + passed:
            cp.wait_send()
        mine.wait()

    hbm = pl.BlockSpec(memory_space=pltpu.HBM)
    return pl.pallas_call(
        body, name=name,
        out_shape=jax.ShapeDtypeStruct((N_DEV * m, n), x.dtype),
        in_specs=[hbm], out_specs=hbm,
        scratch_shapes=[pltpu.SemaphoreType.DMA((7,)), pltpu.SemaphoreType.DMA((7,)),
                        pltpu.SemaphoreType.DMA],
    )(x)


SMALL_ROWS = ROWS_L - OFF_KV
ROWS_A = [FF_SH] * 3
SPLIT_AB = sum(ROWS_A)
ROWS_TAIL = [128, SMALL_ROWS]
GB_F2, GB_F1, GB_TAIL = 0, SPLIT_AB, 2 * SPLIT_AB
HBM_SPEC = pl.BlockSpec(memory_space=pltpu.HBM)
SEM_SPEC = pl.BlockSpec(memory_space=pltpu.SEMAPHORE)
ANY_SPEC = pl.BlockSpec(memory_space=pl.ANY)
EFFECT = pltpu.SideEffectType.DATAFLOW_SIDE_EFFECTING


def _hbm(t):
    return pltpu.with_memory_space_constraint(t, pltpu.HBM)


def _whole_wait(ref, send_sem, recv_sem, peer):
    return pltpu.make_async_remote_copy(src_ref=ref, dst_ref=ref, send_sem=send_sem, recv_sem=recv_sem,
                                        device_id=peer, device_id_type=pl.DeviceIdType.MESH)


def _offsets(rows_list):
    return [sum(rows_list[:i]) for i in range(len(rows_list))]


def _gather_start(packed, rows_list, after, name):
    n = len(rows_list)
    offs = _offsets(rows_list)
    lands = [_hbm(lax.empty((N_DEV * rows, D_MODEL), BF16)) for rows in rows_list]

    def body(packed_ref, *refs):
        land = refs[:n]
        send_sems, recv_sems = refs[n + 1], refs[n + 2]
        token = refs[-1]
        px, py, pc = _mesh_pos()
        me = 4 * px + 2 * py + pc
        peers = [(px, py, 1 - pc), (1 - px, py, pc), (px, 1 - py, pc), (1 - px, 1 - py, pc)]
        for k, peer in enumerate(peers):
            for off, rows, land_ref in zip(offs, rows_list, land):
                pltpu.make_async_remote_copy(
                    src_ref=packed_ref.at[pl.ds(off, rows), :], dst_ref=land_ref.at[pl.ds(me * rows, rows), :],
                    send_sem=send_sems.at[k], recv_sem=recv_sems.at[k],
                    device_id=peer, device_id_type=pl.DeviceIdType.MESH).start()
        token[...] = jnp.zeros_like(token)

    outs = pl.pallas_call(
        body, name=name,
        out_shape=(pltpu.SemaphoreType.DMA((4,)), pltpu.SemaphoreType.DMA((4,)), pltpu.HBM(packed.shape, BF16),
                   *[pltpu.HBM(t.shape, BF16) for t in lands], jax.ShapeDtypeStruct((8, LANE), F32)),
        in_specs=(HBM_SPEC,) * (1 + n) + (ANY_SPEC,),
        out_specs=(SEM_SPEC, SEM_SPEC) + (HBM_SPEC,) * (1 + n) + (pl.BlockSpec(memory_space=pltpu.VMEM),),
        input_output_aliases={i: 2 + i for i in range(1 + n)},
        compiler_params=pltpu.CompilerParams(has_side_effects=EFFECT),
    )(_hbm(packed), *lands, after)
    return outs[0], outs[1], outs[2], list(outs[3:3 + n]), outs[-1]


def _gather_wait(send_sems, recv_sems, packed, lands, after, name):
    n = len(lands)

    def body(packed_ref, *refs):
        s_sems, r_sems = refs[n], refs[n + 1]
        me = _mesh_pos()
        for k in range(4):
            cp = _whole_wait(packed_ref, s_sems.at[k], r_sems.at[k], me)
            cp.wait_send()
            cp.wait_recv()

    outs = pl.pallas_call(
        body, name=name,
        out_shape=(pltpu.HBM(packed.shape, BF16), *[pltpu.HBM(t.shape, BF16) for t in lands]),
        in_specs=(HBM_SPEC,) * (1 + n) + (SEM_SPEC, SEM_SPEC, ANY_SPEC),
        out_specs=(HBM_SPEC,) * (1 + n),
        input_output_aliases={i: i for i in range(1 + n)},
        compiler_params=pltpu.CompilerParams(has_side_effects=EFFECT),
    )(packed, *lands, send_sems, recv_sems, after)
    return outs[0], list(outs[1:])


def _gather_finish(packed, rows_list, lands, name):
    n = len(rows_list)
    offs = _offsets(rows_list)

    def body(packed_ref, *refs):
        land = refs[n:2 * n]
        send_sems, recv_sems, stage, stage_sem = refs[2 * n:]
        px, py, pc = _mesh_pos()
        me = 4 * px + 2 * py + pc
        sibling = (px, py, 1 - pc)
        load = pltpu.make_async_copy(packed_ref, stage, stage_sem)
        load.start()
        for j, (cx, cy) in enumerate([(1 - px, py), (px, 1 - py), (1 - px, 1 - py)]):
            block = 4 * cx + 2 * cy + pc
            for rows, land_ref in zip(rows_list, land):
                blk = land_ref.at[pl.ds(block * rows, rows), :]
                pltpu.make_async_remote_copy(src_ref=blk, dst_ref=blk, send_sem=send_sems.at[j],
                                             recv_sem=recv_sems.at[j], device_id=sibling,
                                             device_id_type=pl.DeviceIdType.MESH).start()
        load.wait()
        for off, rows, land_ref in zip(offs, rows_list, land):
            pltpu.make_async_copy(stage.at[pl.ds(off, rows), :], land_ref.at[pl.ds(me * rows, rows), :],
                                  stage_sem).start()
        for j in range(3):
            cp = _whole_wait(packed_ref, send_sems.at[j], recv_sems.at[j], sibling)
            cp.wait_recv()
            cp.wait_send()
        pltpu.make_async_copy(stage, packed_ref, stage_sem).wait()

    outs = pl.pallas_call(
        body, name=name,
        out_shape=tuple(jax.ShapeDtypeStruct(t.shape, BF16) for t in lands),
        in_specs=(HBM_SPEC,) * (1 + n), out_specs=(HBM_SPEC,) * n,
        input_output_aliases={1 + i: i for i in range(n)},
        scratch_shapes=[pltpu.SemaphoreType.DMA((3,)), pltpu.SemaphoreType.DMA((3,)),
                        pltpu.VMEM(packed.shape, BF16), pltpu.SemaphoreType.DMA],
    )(packed, *lands)
    return list(outs)


N_CHIPS = 4


def _pair_start(srcs, rows_list, after, name):
    n = len(rows_list)
    offs = _offsets(rows_list)
    land = lax.empty((N_CHIPS, sum(rows_list), D_MODEL), BF16)

    def body(*refs):
        src, land_ref = refs[:n], refs[n]
        send_sems, recv_sems = refs[n + 2], refs[n + 3]
        token = refs[-1]
        px, py, pc = _mesh_pos()
        for k in range(N_CHIPS):
            block = 2 * k + (1 - pc)
            for off, rows, src_ref in zip(offs, rows_list, src):
                pltpu.make_async_remote_copy(
                    src_ref=src_ref.at[pl.ds(block * rows, rows), :], dst_ref=land_ref.at[k, pl.ds(off, rows), :],
                    send_sem=send_sems.at[0], recv_sem=recv_sems.at[0],
                    device_id=(px, py, 1 - pc), device_id_type=pl.DeviceIdType.MESH).start()
        token[...] = jnp.zeros_like(token)

    outs = pl.pallas_call(
        body, name=name,
        out_shape=(pltpu.SemaphoreType.DMA((1,)), pltpu.SemaphoreType.DMA((1,)),
                   *[pltpu.HBM(t.shape, BF16) for t in srcs], pltpu.HBM(land.shape, BF16),
                   jax.ShapeDtypeStruct((8, LANE), F32)),
        in_specs=(HBM_SPEC,) * (n + 1) + (ANY_SPEC,),
        out_specs=(SEM_SPEC, SEM_SPEC) + (HBM_SPEC,) * (n + 1) + (pl.BlockSpec(memory_space=pltpu.VMEM),),
        input_output_aliases={i: 2 + i for i in range(n + 1)},
        compiler_params=pltpu.CompilerParams(has_side_effects=EFFECT),
    )(*[_hbm(t) for t in srcs], _hbm(land), after)
    return outs[0], outs[1], list(outs[2:2 + n]), outs[2 + n], outs[-1]


def _split_wait(send_sems, recv_sems, n_sems, srcs, land, after, name):
    n = len(srcs)

    def body(*refs):
        land_ref = refs[n]
        s_sems, r_sems = refs[n + 1], refs[n + 2]
        me = _mesh_pos()
        for k in range(n_sems):
            cp = _whole_wait(land_ref.at[0] if n_sems > 1 else land_ref, s_sems.at[k], r_sems.at[k], me)
            cp.wait_send()
            cp.wait_recv()

    outs = pl.pallas_call(
        body, name=name,
        out_shape=(*[pltpu.HBM(t.shape, t.dtype) for t in srcs], pltpu.HBM(land.shape, land.dtype)),
        in_specs=(HBM_SPEC,) * (n + 1) + (SEM_SPEC, SEM_SPEC, ANY_SPEC),
        out_specs=(HBM_SPEC,) * (n + 1),
        input_output_aliases={i: i for i in range(n + 1)},
        compiler_params=pltpu.CompilerParams(has_side_effects=EFFECT),
    )(*srcs, land, send_sems, recv_sems, after)
    return list(outs[:n]), outs[n]


def _spread_start(x, me_id, after, name):
    land = lax.dynamic_update_slice_in_dim(lax.empty((N_DEV,) + x.shape, x.dtype), x[None], me_id, axis=0)

    def body(x_ref, land_ref, after_ref, send_sems, recv_sems, x_thru, land_thru, token):
        px, py, pc = _mesh_pos()
        me = 4 * px + 2 * py + pc
        for k in range(1, N_DEV):
            qx = 1 - px if k & 4 else px
            qy = 1 - py if k & 2 else py
            qc = 1 - pc if k & 1 else pc
            pltpu.make_async_remote_copy(
                src_ref=x_ref, dst_ref=land_ref.at[me], send_sem=send_sems.at[k - 1], recv_sem=recv_sems.at[k - 1],
                device_id=(qx, qy, qc), device_id_type=pl.DeviceIdType.MESH).start()
        token[...] = jnp.zeros_like(token)

    outs = pl.pallas_call(
        body, name=name,
        out_shape=(pltpu.SemaphoreType.DMA((N_DEV - 1,)), pltpu.SemaphoreType.DMA((N_DEV - 1,)),
                   pltpu.HBM(x.shape, x.dtype), pltpu.HBM(land.shape, land.dtype), jax.ShapeDtypeStruct((8, LANE), F32)),
        in_specs=(HBM_SPEC, HBM_SPEC, ANY_SPEC),
        out_specs=(SEM_SPEC, SEM_SPEC, HBM_SPEC, HBM_SPEC, pl.BlockSpec(memory_space=pltpu.VMEM)),
        input_output_aliases={0: 2, 1: 3},
        compiler_params=pltpu.CompilerParams(has_side_effects=EFFECT),
    )(_hbm(x), _hbm(land), after)
    return outs[0], outs[1], [outs[2]], outs[3], outs[4]


def _pair_sum(srcs, rows_list, land, core, name):
    n = len(rows_list)
    offs = _offsets(rows_list)
    total = sum(rows_list)

    def body(core_ref, *refs):
        src, land_ref, out_ref = refs[:n], refs[n], refs[n + 1]
        for off, rows, src_ref in zip(offs, rows_list, src):
            out_ref[pl.ds(off, rows), :] = (src_ref[...].astype(F32)
                                            + land_ref[pl.ds(off, rows), :].astype(F32)).astype(BF16)

    slot = pl.BlockSpec((None, total, D_MODEL), lambda k, c: (k, 0, 0))
    grid_spec = pltpu.PrefetchScalarGridSpec(
        num_scalar_prefetch=1, grid=(N_CHIPS,),
        in_specs=[pl.BlockSpec((rows, D_MODEL), lambda k, c: (2 * k + c[0], 0)) for rows in rows_list] + [slot],
        out_specs=slot)
    return pl.pallas_call(
        body, name=name, grid_spec=grid_spec,
        out_shape=jax.ShapeDtypeStruct((N_CHIPS, total, D_MODEL), BF16),
        compiler_params=_params(("parallel",)),
    )(core, *srcs, land)


def _chip_exchange_start(sums, chip, after, name):
    own = lax.dynamic_index_in_dim(sums, chip, axis=0, keepdims=True)
    recv = lax.dynamic_update_slice_in_dim(lax.empty(sums.shape, BF16), own, chip, axis=0)

    def body(sums_ref, recv_ref, after_ref, send_sems, recv_sems, sums_thru, recv_thru, token):
        px, py, pc = _mesh_pos()
        for k in range(1, N_CHIPS):
            qx = 1 - px if k & 2 else px
            qy = 1 - py if k & 1 else py
            pltpu.make_async_remote_copy(
                src_ref=sums_ref.at[2 * qx + qy], dst_ref=recv_ref.at[2 * px + py],
                send_sem=send_sems.at[k - 1], recv_sem=recv_sems.at[k - 1],
                device_id=(qx, qy, pc), device_id_type=pl.DeviceIdType.MESH).start()
        token[...] = jnp.zeros_like(token)

    outs = pl.pallas_call(
        body, name=name,
        out_shape=(pltpu.SemaphoreType.DMA((N_CHIPS - 1,)), pltpu.SemaphoreType.DMA((N_CHIPS - 1,)),
                   pltpu.HBM(sums.shape, BF16), pltpu.HBM(recv.shape, BF16), jax.ShapeDtypeStruct((8, LANE), F32)),
        in_specs=(HBM_SPEC, HBM_SPEC, ANY_SPEC),
        out_specs=(SEM_SPEC, SEM_SPEC, HBM_SPEC, HBM_SPEC, pl.BlockSpec(memory_space=pltpu.VMEM)),
        input_output_aliases={0: 2, 1: 3},
        compiler_params=pltpu.CompilerParams(has_side_effects=EFFECT),
    )(_hbm(sums), _hbm(recv), after)
    return outs[0], outs[1], [outs[2]], outs[3], outs[4]


def _sum_slots_into(recv, buf, layer, row_off, name):
    slots, r, n = recv.shape
    tr = _row_tile(math.gcd(r, row_off) if row_off else r, 512)
    first = row_off // tr

    def body(in_ref, buf_ref, out_ref):
        acc = in_ref[0].astype(F32)
        for j in range(1, slots):
            acc = acc + in_ref[j].astype(F32)
        out_ref[...] = acc

    return pl.pallas_call(
        body, name=name, grid=(r // tr,), out_shape=jax.ShapeDtypeStruct(buf.shape, F32),
        in_specs=[pl.BlockSpec((slots, tr, n), lambda i: (0, i, 0)), ANY_SPEC],
        out_specs=pl.BlockSpec((None, tr, n), lambda i: (layer, first + i, 0)),
        input_output_aliases={1: 0},
        compiler_params=_params(("parallel",)),
    )(recv, buf)


def _sum_slots(recv, name, after=None):
    _, r, n = recv.shape
    tr = _row_tile(r, 512)

    def body(in_ref, *refs):
        acc = in_ref[0].astype(F32)
        for j in range(1, N_DEV):
            acc = acc + in_ref[j].astype(F32)
        refs[-1][...] = acc

    grid = (r // tr,)
    in_specs, out_spec = [pl.BlockSpec((N_DEV, tr, n), lambda i: (0, i, 0))], pl.BlockSpec((tr, n), lambda i: (i, 0))
    args = [recv]
    if after is not None:
        in_specs.append(ANY_SPEC)
        args.append(after)
    return pl.pallas_call(
        body, name=name, grid=grid,
        out_shape=jax.ShapeDtypeStruct((r, n), F32),
        in_specs=in_specs, out_specs=out_spec,
        compiler_params=_params(("parallel",)),
    )(*args)


def _row_tile(rows, target):
    if rows <= target:
        return rows
    best = None
    for t in range(16, target + 1, 16):
        if rows % t == 0:
            best = t
    assert best is not None, rows
    return best


_DIMS = {"nn": ((1,), (0,)), "nt": ((1,), (1,)), "tn": ((0,), (0,))}


def _mm(a, b, mode, name, out_dtype=F32, res=None, gate=None, gate_factor=1.0, tm=512, tn=1408, after=None):
    assert (res is None) == (gate is None)
    if mode == "tn":
        kdim, m = a.shape
    else:
        m, kdim = a.shape
    n = b.shape[0] if mode == "nt" else b.shape[1]
    tm, tn = _tile(m, tm), _tile(n, tn)
    a_spec = (pl.BlockSpec((kdim, tm), lambda i, j: (0, i)) if mode == "tn"
              else pl.BlockSpec((tm, kdim), lambda i, j: (i, 0)))
    b_spec = (pl.BlockSpec((tn, kdim), lambda i, j: (j, 0)) if mode == "nt"
              else pl.BlockSpec((kdim, tn), lambda i, j: (0, j)))
    o_spec = pl.BlockSpec((tm, tn), lambda i, j: (i, j))
    dims = (_DIMS[mode], ((), ()))
    has_res = res is not None

    def body(a_ref, b_ref, *refs):
        y = lax.dot_general(a_ref[...].astype(BF16), b_ref[...].astype(BF16), dims,
                            preferred_element_type=F32)
        if has_res:
            res_ref, gate_ref = refs[0], refs[1]
            y_ref, o_ref = refs[-2], refs[-1]
            y_ref[...] = y.astype(BF16)
            o_ref[...] = res_ref[...] + (gate_factor * gate_ref[...]) * y
        else:
            refs[-1][...] = y.astype(out_dtype)

    in_specs, args = [a_spec, b_spec], [a, b]
    if has_res:
        gate_spec, gate_arg = _vec_in(gate, tile=tn)
        in_specs += [o_spec, gate_spec]
        args += [res, gate_arg]
        out_shape = (jax.ShapeDtypeStruct((m, n), BF16), jax.ShapeDtypeStruct((m, n), F32))
        out_specs = (o_spec, o_spec)
    else:
        out_shape, out_specs = jax.ShapeDtypeStruct((m, n), out_dtype), o_spec
    if after is not None:
        in_specs.append(ANY_SPEC)
        args.append(after)
    return pl.pallas_call(
        body, name=name, grid=(m // tm, n // tn), out_shape=out_shape,
        in_specs=in_specs, out_specs=out_specs,
        compiler_params=_params(("parallel", "parallel")),
    )(*args)


def _vec_in(v, tile=None):
    if isinstance(v, tuple):
        table, row = v
        if tile is None:
            return pl.BlockSpec((None, 1, table.shape[-1]), lambda *idx: (row, 0, 0)), table
        return pl.BlockSpec((None, 1, tile), lambda i, j: (row, 0, j)), table
    if tile is None:
        return pl.BlockSpec((1, v.shape[-1]), lambda *idx: (0, 0)), v
    return pl.BlockSpec((1, tile), lambda i, j: (0, j)), v


def _vec_spec(width):
    return pl.BlockSpec((1, width), lambda i: (0, 0))


def _rm_bwd(dh, x, dres, gw, scale, name, below=None):
    s, d = x.shape
    ts = _tile(s, 512)
    factor = None if below is None else below[2]

    def body(dh_ref, x_ref, dres_ref, gw_ref, sc_ref, *refs):
        dx_ref, dsh_ref, dsc_ref, dgw_ref = refs[-6:-2] if below is not None else refs[-4:]

        @pl.when(pl.program_id(0) == 0)
        def _():
            dsh_ref[...] = jnp.zeros_like(dsh_ref)
            dsc_ref[...] = jnp.zeros_like(dsc_ref)
            dgw_ref[...] = jnp.zeros_like(dgw_ref)
            if below is not None:
                refs[-1][...] = jnp.zeros_like(refs[-1])

        xv, dhv, gwv = x_ref[...], dh_ref[...], gw_ref[...]
        r = lax.rsqrt(jnp.mean(xv * xv, axis=-1, keepdims=True) + EPS)
        xn = xv * r
        y = xn * gwv
        dsh_ref[...] += jnp.sum(dhv, axis=0, keepdims=True)
        dsc_ref[...] += jnp.sum(dhv * y, axis=0, keepdims=True)
        dy = dhv * (1 + sc_ref[...])
        dgw_ref[...] += jnp.sum(dy * xn, axis=0, keepdims=True)
        dxn = dy * gwv
        dx = dres_ref[...] + r * (dxn - xn * jnp.mean(dxn * xn, axis=-1, keepdims=True))
        dx_ref[...] = dx
        if below is not None:
            yb_ref, gb_ref, dyb_ref, dgb_ref = refs[0], refs[1], refs[-2], refs[-1]
            dyb_ref[...] = ((factor * gb_ref[...]) * dx).astype(BF16)
            dgb_ref[...] += jnp.sum((factor * dx) * yb_ref[...].astype(F32), axis=0, keepdims=True)

    row = pl.BlockSpec((ts, d), lambda i: (i, 0))
    vec = jax.ShapeDtypeStruct((1, d), F32)
    (gw_spec, gw), (sc_spec, scale) = _vec_in(gw), _vec_in(scale)
    in_specs, args = [row, row, row, gw_spec, sc_spec], [dh, x, dres, gw, scale]
    out_shape = [jax.ShapeDtypeStruct((s, d), F32), vec, vec, vec]
    out_specs = [row, _vec_spec(d), _vec_spec(d), _vec_spec(d)]
    if below is not None:
        gate_spec, gate_arg = _vec_in(below[1])
        in_specs += [row, gate_spec]
        args += [below[0], gate_arg]
        out_shape += [jax.ShapeDtypeStruct((s, d), BF16), vec]
        out_specs += [row, _vec_spec(d)]
    return pl.pallas_call(
        body, name=name, grid=(s // ts,), out_shape=tuple(out_shape),
        in_specs=in_specs, out_specs=tuple(out_specs),
        compiler_params=_params(("arbitrary",)),
    )(*args)


def _norm_mm(x, gw, shift, scale, w, name, tm=1024):
    s, d = x.shape
    n = w.shape[0]
    tm = _tile(s, tm)

    def body(x_ref, gw_ref, sh_ref, sc_ref, w_ref, h_ref, z_ref):
        xv = x_ref[...]
        r = lax.rsqrt(jnp.mean(xv * xv, axis=-1, keepdims=True) + EPS)
        hb = (((xv * r) * gw_ref[...]) * (1 + sc_ref[...]) + sh_ref[...]).astype(BF16)
        h_ref[...] = hb
        z_ref[...] = lax.dot_general(hb, w_ref[...], (((1,), (1,)), ((), ())), preferred_element_type=F32)

    row = pl.BlockSpec((tm, d), lambda i: (i, 0))
    return pl.pallas_call(
        body, name=name, grid=(s // tm,),
        out_shape=(jax.ShapeDtypeStruct((s, d), BF16), jax.ShapeDtypeStruct((s, n), F32)),
        in_specs=[row, _vec_in(gw)[0], _vec_in(shift)[0], _vec_in(scale)[0], pl.BlockSpec((n, d), lambda i: (0, 0))],
        out_specs=(row, pl.BlockSpec((tm, n), lambda i: (i, 0))),
        compiler_params=_params(("parallel",)),
    )(x, _vec_in(gw)[1], _vec_in(shift)[1], _vec_in(scale)[1], w)


FFN_TM, FFN_TF = 2048, 256


def _ffn_up(x, gw, shift, scale, wg, wu, name, after=None):
    s, d = x.shape
    f = wg.shape[0]
    tm, tf = _tile(s, FFN_TM), _tile(f, FFN_TF)
    nt = (((1,), (1,)), ((), ()))

    def body(x_ref, gw_ref, sh_ref, sc_ref, wg_ref, wu_ref, *refs):
        h_ref, a_ref, b_ref, t_ref = refs[-4:]

        @pl.when(pl.program_id(1) == 0)
        def _():
            xv = x_ref[...]
            r = lax.rsqrt(jnp.mean(xv * xv, axis=-1, keepdims=True) + EPS)
            h_ref[...] = (((xv * r) * gw_ref[...]) * (1 + sc_ref[...]) + sh_ref[...]).astype(BF16)

        hb = h_ref[...]
        av = lax.dot_general(hb, wg_ref[...], nt, preferred_element_type=F32)
        bv = lax.dot_general(hb, wu_ref[...], nt, preferred_element_type=F32)
        a_ref[...] = av.astype(BF16)
        b_ref[...] = bv.astype(BF16)
        t_ref[...] = ((av * jax.nn.sigmoid(av)) * bv).astype(BF16)

    row = pl.BlockSpec((tm, d), lambda i, j: (i, 0))
    wblk = pl.BlockSpec((tf, d), lambda i, j: (j, 0))
    blk = pl.BlockSpec((tm, tf), lambda i, j: (i, j))
    wide = jax.ShapeDtypeStruct((s, f), BF16)
    vec_specs, vec_args = zip(*[_vec_in(v) for v in (gw, shift, scale)])
    in_specs, args = [row, *vec_specs, wblk, wblk], [x, *vec_args, wg, wu]
    if after is not None:
        in_specs.append(ANY_SPEC)
        args.append(after)
    return pl.pallas_call(
        body, name=name, grid=(s // tm, f // tf),
        out_shape=(jax.ShapeDtypeStruct((s, d), BF16), wide, wide, wide),
        in_specs=in_specs, out_specs=(row, blk, blk, blk),
        compiler_params=_params(("parallel", "arbitrary")),
    )(*args)


def _ffn_bwd_cols(dy, h, a, b, t, wd, name, after=None):
    s, d = dy.shape
    f = wd.shape[0]
    tf = _tile(f, FFN_TF)
    nt = (((1,), (1,)), ((), ()))
    tn = (((0,), (0,)), ((), ()))

    def body(dy_ref, h_ref, a_ref, b_ref, t_ref, wd_ref, *refs):
        da_ref, db_ref, gd_ref, gg_ref, gu_ref = refs[-5:]
        dyb, hb = dy_ref[...], h_ref[...]
        dtv = lax.dot_general(dyb, wd_ref[...], nt, preferred_element_type=F32)
        av, bv = a_ref[...].astype(F32), b_ref[...].astype(F32)
        sg = jax.nn.sigmoid(av)
        dbv = (dtv * (av * sg)).astype(BF16)
        dav = ((dtv * bv) * (sg * (1 + av * (1 - sg)))).astype(BF16)
        da_ref[...] = dav
        db_ref[...] = dbv
        gd_ref[...] = lax.dot_general(t_ref[...], dyb, tn, preferred_element_type=F32).astype(BF16)
        gg_ref[...] = lax.dot_general(dav, hb, tn, preferred_element_type=F32).astype(BF16)
        gu_ref[...] = lax.dot_general(dbv, hb, tn, preferred_element_type=F32).astype(BF16)

    whole = pl.BlockSpec((s, d), lambda j: (0, 0))
    col = pl.BlockSpec((s, tf), lambda j: (0, j))
    wblk = pl.BlockSpec((tf, d), lambda j: (j, 0))
    wide, wgrad = jax.ShapeDtypeStruct((s, f), BF16), jax.ShapeDtypeStruct((f, d), BF16)
    in_specs, args = [whole, whole, col, col, col, wblk], [dy, h, a, b, t, wd]
    if after is not None:
        in_specs.append(ANY_SPEC)
        args.append(after)
    return pl.pallas_call(
        body, name=name, grid=(f // tf,), out_shape=(wide, wide, wgrad, wgrad, wgrad),
        in_specs=in_specs, out_specs=(col, col, wblk, wblk, wblk),
        compiler_params=_params(("parallel",)),
    )(*args)


def _mm_pair(a1, b1, a2, b2, name, tm=1024, tn=512, after=None):
    m, kdim = a1.shape
    n = b1.shape[1]
    tm, tn = _tile(m, tm), _tile(n, tn)

    def body(a1_ref, b1_ref, a2_ref, b2_ref, *refs):
        refs[-1][...] = (jnp.dot(a1_ref[...], b1_ref[...], preferred_element_type=F32)
                         + jnp.dot(a2_ref[...], b2_ref[...], preferred_element_type=F32))

    a_spec = pl.BlockSpec((tm, kdim), lambda i, j: (i, 0))
    b_spec = pl.BlockSpec((kdim, tn), lambda i, j: (0, j))
    in_specs, args = [a_spec, b_spec, a_spec, b_spec], [a1, b1, a2, b2]
    if after is not None:
        in_specs.append(ANY_SPEC)
        args.append(after)
    return pl.pallas_call(
        body, name=name, grid=(m // tm, n // tn), out_shape=jax.ShapeDtypeStruct((m, n), F32),
        in_specs=in_specs, out_specs=pl.BlockSpec((tm, tn), lambda i, j: (i, j)),
        compiler_params=_params(("parallel", "parallel")),
    )(*args)


def _pool_counts(s):
    return (lax.broadcasted_iota(jnp.int32, (s, POOL_GC), 0))


def _pool_fwd(z, pool_w, pool_scale, name):
    s = z.shape[0]

    def body(u_ref, w_ref, sc_ref, y_ref, diff_ref):
        t = lax.broadcasted_iota(jnp.int32, (s, POOL_GC), 0)
        for g, win in enumerate(POOL_WINDOWS):
            cols = slice(g * POOL_GC, (g + 1) * POOL_GC)
            u = u_ref[:, cols]
            acc, step = u, 1
            while step < win:
                acc = acc + jnp.where(t >= step, pltpu.roll(acc, step, 0), 0.0)
                step *= 2
            cnt = jnp.minimum(t + 1, win).astype(F32)
            diff = acc / cnt - u
            diff_ref[:, cols] = diff
            ypre = jnp.dot(diff.astype(BF16), w_ref[g].astype(BF16), preferred_element_type=F32)
            y_ref[:, cols] = (ypre * sc_ref[:, cols]).astype(BF16)

    return pl.pallas_call(
        body, name=name, grid=(1,),
        out_shape=(jax.ShapeDtypeStruct((s, POOL_WIDTH), BF16), jax.ShapeDtypeStruct((s, POOL_WIDTH), F32)),
        in_specs=[pl.BlockSpec((s, POOL_WIDTH), lambda i: (0, 0)),
                  pl.BlockSpec(pool_w.shape, lambda i: (0, 0, 0)),
                  pl.BlockSpec((1, POOL_WIDTH), lambda i: (0, 0))],
        out_specs=(pl.BlockSpec((s, POOL_WIDTH), lambda i: (0, 0)),
                   pl.BlockSpec((s, POOL_WIDTH), lambda i: (0, 0))),
        compiler_params=_params(("arbitrary",)),
    )(z, pool_w, pool_scale)


def _pool_bwd(dycat, diff, pool_w, pool_scale, name):
    s = diff.shape[0]

    def body(dy_ref, diff_ref, w_ref, sc_ref, du_ref, dw_ref, dsc_ref):
        t = lax.broadcasted_iota(jnp.int32, (s, POOL_GC), 0)
        for g, win in enumerate(POOL_WINDOWS):
            cols = slice(g * POOL_GC, (g + 1) * POOL_GC)
            dy, dfb, wb = dy_ref[:, cols], diff_ref[:, cols].astype(BF16), w_ref[g].astype(BF16)
            ypre = jnp.dot(dfb, wb, preferred_element_type=F32)
            dsc_ref[:, cols] = jnp.sum(dy * ypre, axis=0, keepdims=True)
            dypre = (dy * sc_ref[:, cols]).astype(BF16)
            ddiff = lax.dot_general(dypre, wb, (((1,), (1,)), ((), ())), preferred_element_type=F32)
            dw_ref[g] = lax.dot_general(dfb, dypre, (((0,), (0,)), ((), ())), preferred_element_type=F32)
            cnt = jnp.minimum(t + 1, win).astype(F32)
            acc, step = ddiff / cnt, 1
            while step < win:
                acc = acc + jnp.where(t < s - step, pltpu.roll(acc, s - step, 0), 0.0)
                step *= 2
            du_ref[:, cols] = acc - ddiff

    full = pl.BlockSpec((s, POOL_WIDTH), lambda i: (0, 0))
    return pl.pallas_call(
        body, name=name, grid=(1,),
        out_shape=(jax.ShapeDtypeStruct((s, POOL_WIDTH), F32),
                   jax.ShapeDtypeStruct(pool_w.shape, F32),
                   jax.ShapeDtypeStruct((1, POOL_WIDTH), F32)),
        in_specs=[full, full, pl.BlockSpec(pool_w.shape, lambda i: (0, 0, 0)),
                  pl.BlockSpec((1, POOL_WIDTH), lambda i: (0, 0))],
        out_specs=(full, pl.BlockSpec(pool_w.shape, lambda i: (0, 0, 0)),
                   pl.BlockSpec((1, POOL_WIDTH), lambda i: (0, 0))),
        compiler_params=_params(("arbitrary",)),
    )(dycat, diff, pool_w, pool_scale)


def _rope_tables(positions, name):
    s = positions.shape[0]
    ts = _tile(s, 512)
    freq = 1.0 / (ROPE_THETA ** (np.arange(0, QK_ROPE, 2, dtype=np.float32) / QK_ROPE))
    table = np.zeros((1, LANE), np.float32)
    table[0, :QK_ROPE // 2] = freq
    table[0, QK_ROPE // 2:QK_ROPE] = freq

    def body(pos_ref, f_ref, cos_ref, sin_ref):
        ang = pos_ref[...].astype(F32) * f_ref[...]
        cos_ref[...] = jnp.cos(ang)
        sin_ref[...] = jnp.sin(ang)

    out = jax.ShapeDtypeStruct((s, LANE), F32)
    blk = pl.BlockSpec((ts, LANE), lambda i: (i, 0))
    return pl.pallas_call(
        body, name=name, grid=(s // ts,), out_shape=(out, out),
        in_specs=[pl.BlockSpec((ts, 1), lambda i: (i, 0)), _vec_spec(LANE)], out_specs=(blk, blk),
        compiler_params=_params(("parallel",)),
    )(positions, jnp.asarray(table))


def _lane_mod64_low(shape):
    return (lax.broadcasted_iota(jnp.int32, shape, 1) % QK_ROPE) < (QK_ROPE // 2)


def _rope(x, cos, sin):
    rot = jnp.where(_lane_mod64_low(x.shape), -pltpu.roll(x, LANE - 32, 1), pltpu.roll(x, 32, 1))
    return x * cos + rot * sin


def _rope_t(dy, cos, sin):
    w = dy * sin
    rot_t = jnp.where(_lane_mod64_low(dy.shape), pltpu.roll(w, LANE - 32, 1), -pltpu.roll(w, 32, 1))
    return dy * cos + rot_t


def _plain_rms(x, g):
    r = lax.rsqrt(jnp.mean(x * x, axis=-1, keepdims=True) + EPS)
    return (x * r) * g, x * r, r


O_Q, O_KV, O_KR = POOL_WIDTH, POOL_WIDTH + Q_LORA, POOL_WIDTH + Q_LORA + KV_LORA


def _qkv_fwd(z, qn, kvn, wq, wkv, cos, sin, name):
    s = z.shape[0]
    ts = _tile(s, 512)

    def body(z_ref, qn_ref, kvn_ref, wq_ref, wkv_ref, cos_ref, sin_ref, q_ref, k_ref, v_ref, cqn_ref, ckvn_ref):
        cosv, sinv = cos_ref[...], sin_ref[...]
        cqn = _plain_rms(z_ref[:, O_Q:O_KV], qn_ref[...])[0].astype(BF16)
        ckvn = _plain_rms(z_ref[:, O_KV:O_KR], kvn_ref[...])[0].astype(BF16)
        cqn_ref[...] = cqn
        ckvn_ref[...] = ckvn
        nt = (((1,), (1,)), ((), ()))
        q = lax.dot_general(cqn, wq_ref[...], nt, preferred_element_type=F32)
        kv = lax.dot_general(ckvn, wkv_ref[...], nt, preferred_element_type=F32)
        kr = _rope(z_ref[:, O_KR:IN_PAD], cosv, sinv).astype(BF16)
        for h in range(N_HEADS):
            o = h * HEAD_PAD
            q_ref[:, o:o + QK_NOPE] = q[:, o:o + QK_NOPE].astype(BF16)
            q_ref[:, o + QK_NOPE:o + HEAD_PAD] = _rope(q[:, o + QK_NOPE:o + HEAD_PAD], cosv, sinv).astype(BF16)
            k_ref[:, o:o + QK_NOPE] = kv[:, o:o + QK_NOPE].astype(BF16)
            k_ref[:, o + QK_NOPE:o + HEAD_PAD] = kr
            v_ref[:, h * V_HEAD:(h + 1) * V_HEAD] = kv[:, o + QK_NOPE:o + HEAD_PAD].astype(BF16)

    def row(w):
        return pl.BlockSpec((ts, w), lambda i: (i, 0))

    def whole(arr):
        return pl.BlockSpec(arr.shape, lambda i: (0, 0))

    hp = N_HEADS * HEAD_PAD
    return pl.pallas_call(
        body, name=name, grid=(s // ts,),
        out_shape=(jax.ShapeDtypeStruct((s, hp), BF16), jax.ShapeDtypeStruct((s, hp), BF16),
                   jax.ShapeDtypeStruct((s, N_HEADS * V_HEAD), BF16),
                   jax.ShapeDtypeStruct((s, Q_LORA), BF16), jax.ShapeDtypeStruct((s, KV_LORA), BF16)),
        in_specs=[row(IN_PAD), whole(qn), whole(kvn), whole(wq), whole(wkv), row(LANE), row(LANE)],
        out_specs=(row(hp), row(hp), row(N_HEADS * V_HEAD), row(Q_LORA), row(KV_LORA)),
        compiler_params=_params(("parallel",)),
    )(z, qn, kvn, wq, wkv, cos, sin)


def _qkv_bwd(dq, dk, dv, du, z, qn, kvn, wq, wkv, cos, sin, name):
    s = z.shape[0]
    ts = _tile(s, 512)

    def norm_bwd(x, g, dy):
        _, xn, r = _plain_rms(x, g)
        dxn = dy * g
        return r * (dxn - xn * jnp.mean(dxn * xn, axis=-1, keepdims=True)), jnp.sum(dy * xn, axis=0, keepdims=True)

    def body(dq_ref, dk_ref, dv_ref, du_ref, z_ref, qn_ref, kvn_ref, wq_ref, wkv_ref, cos_ref, sin_ref,
             dz_ref, dqb_ref, dkvb_ref, dqn_ref, dkvn_ref):
        @pl.when(pl.program_id(0) == 0)
        def _():
            dqn_ref[...] = jnp.zeros_like(dqn_ref)
            dkvn_ref[...] = jnp.zeros_like(dkvn_ref)

        cosv, sinv = cos_ref[...], sin_ref[...]
        dkr = jnp.zeros((ts, LANE), F32)
        for h in range(N_HEADS):
            o = h * HEAD_PAD
            dqb_ref[:, o:o + QK_NOPE] = dq_ref[:, o:o + QK_NOPE].astype(BF16)
            dqb_ref[:, o + QK_NOPE:o + HEAD_PAD] = _rope_t(dq_ref[:, o + QK_NOPE:o + HEAD_PAD], cosv, sinv).astype(BF16)
            dkvb_ref[:, o:o + QK_NOPE] = dk_ref[:, o:o + QK_NOPE].astype(BF16)
            dkvb_ref[:, o + QK_NOPE:o + HEAD_PAD] = dv_ref[:, h * V_HEAD:(h + 1) * V_HEAD].astype(BF16)
            dkr = dkr + dk_ref[:, o + QK_NOPE:o + HEAD_PAD]
        dcqn = jnp.dot(dqb_ref[...], wq_ref[...], preferred_element_type=F32)
        dckvn = jnp.dot(dkvb_ref[...], wkv_ref[...], preferred_element_type=F32)
        dcq, dqn = norm_bwd(z_ref[:, O_Q:O_KV], qn_ref[...], dcqn)
        dckv, dkvn = norm_bwd(z_ref[:, O_KV:O_KR], kvn_ref[...], dckvn)
        dqn_ref[...] += dqn
        dkvn_ref[...] += dkvn
        dz_ref[:, 0:O_Q] = du_ref[...].astype(BF16)
        dz_ref[:, O_Q:O_KV] = dcq.astype(BF16)
        dz_ref[:, O_KV:O_KR] = dckv.astype(BF16)
        dz_ref[:, O_KR:IN_PAD] = _rope_t(dkr, cosv, sinv).astype(BF16)

    def row(w):
        return pl.BlockSpec((ts, w), lambda i: (i, 0))

    def whole(arr):
        return pl.BlockSpec(arr.shape, lambda i: (0, 0))

    hp = N_HEADS * HEAD_PAD
    return pl.pallas_call(
        body, name=name, grid=(s // ts,),
        out_shape=(jax.ShapeDtypeStruct((s, IN_PAD), BF16), jax.ShapeDtypeStruct((s, hp), BF16),
                   jax.ShapeDtypeStruct((s, hp), BF16),
                   jax.ShapeDtypeStruct((1, Q_LORA), F32), jax.ShapeDtypeStruct((1, KV_LORA), F32)),
        in_specs=[row(hp), row(hp), row(N_HEADS * V_HEAD), row(POOL_WIDTH), row(IN_PAD),
                  whole(qn), whole(kvn), whole(wq), whole(wkv), row(LANE), row(LANE)],
        out_specs=(row(IN_PAD), row(hp), row(hp), whole(qn), whole(kvn)),
        compiler_params=_params(("arbitrary",)),
    )(dq, dk, dv, du, z, qn, kvn, wq, wkv, cos, sin)


def _causal_scores(q, k, i, tq, klen):
    sc = lax.dot_general(q, k, (((1,), (1,)), ((), ())), preferred_element_type=F32) * SOFTMAX_SCALE
    qpos = i * tq + lax.broadcasted_iota(jnp.int32, (tq, klen), 0)
    kpos = lax.broadcasted_iota(jnp.int32, (tq, klen), 1)
    return jnp.where(qpos >= kpos, sc, -jnp.inf)


ATTN_TQ = 256
ATTN_SEGMENTS = 8


def _by_key_prefix(i, nq, tq, compute):
    nseg = min(ATTN_SEGMENTS, nq)
    per = nq // nseg
    for r in range(nseg):
        pl.when(i // per == r)(lambda r=r: compute((r + 1) * per * tq))


def _attn_fwd(q, k, v, name):
    s = q.shape[0]
    tq = _tile(s, ATTN_TQ)
    nq = s // tq

    def body(q_ref, k_ref, v_ref, o_ref, lse_ref):
        i = pl.program_id(1)

        def compute(klen):
            sc = _causal_scores(q_ref[...], k_ref[0:klen, :], i, tq, klen)
            mx = jnp.max(sc, axis=-1, keepdims=True)
            p = jnp.exp(sc - mx)
            den = jnp.sum(p, axis=-1, keepdims=True)
            o_ref[...] = jnp.dot((p / den).astype(BF16), v_ref[0:klen, :], preferred_element_type=F32)
            lse_ref[...] = mx + jnp.log(den)

        _by_key_prefix(i, nq, tq, compute)

    return pl.pallas_call(
        body, name=name, grid=(N_HEADS, s // tq),
        out_shape=(jax.ShapeDtypeStruct((s, N_HEADS * V_HEAD), F32), jax.ShapeDtypeStruct((N_HEADS, s, 1), F32)),
        in_specs=[pl.BlockSpec((tq, HEAD_PAD), lambda h, i: (i, h)),
                  pl.BlockSpec((s, HEAD_PAD), lambda h, i: (0, h)),
                  pl.BlockSpec((s, V_HEAD), lambda h, i: (0, h))],
        out_specs=(pl.BlockSpec((tq, V_HEAD), lambda h, i: (i, h)),
                   pl.BlockSpec((None, tq, 1), lambda h, i: (h, i, 0))),
        compiler_params=_params(("parallel", "parallel")),
    )(q, k, v)


def _attn_bwd(q, k, v, lse, dycat, name):
    s = q.shape[0]
    tq = _tile(s, ATTN_TQ)
    nq = s // tq
    tn_dims = (((0,), (0,)), ((), ()))

    def body(q_ref, k_ref, v_ref, lse_ref, do_ref, dq_ref, dk_ref, dv_ref):
        i = pl.program_id(1)

        @pl.when(i == 0)
        def _():
            dk_ref[...] = jnp.zeros_like(dk_ref)
            dv_ref[...] = jnp.zeros_like(dv_ref)

        def compute(klen):
            qv, kv_, dob = q_ref[...], k_ref[0:klen, :], do_ref[...].astype(BF16)
            sc = _causal_scores(qv, kv_, i, tq, klen)
            p = jnp.exp(sc - lse_ref[...])
            dp = lax.dot_general(dob, v_ref[0:klen, :], (((1,), (1,)), ((), ())), preferred_element_type=F32)
            ds = (p * (dp - jnp.sum(dp * p, axis=-1, keepdims=True)) * SOFTMAX_SCALE).astype(BF16)
            dq_ref[...] = jnp.dot(ds, kv_, preferred_element_type=F32)
            dk_ref[0:klen, :] += lax.dot_general(ds, qv, tn_dims, preferred_element_type=F32)
            dv_ref[0:klen, :] += lax.dot_general(p.astype(BF16), dob, tn_dims, preferred_element_type=F32)

        _by_key_prefix(i, nq, tq, compute)

    n_pool_blocks = POOL_WIDTH // V_HEAD
    return pl.pallas_call(
        body, name=name, grid=(N_HEADS, s // tq),
        out_shape=(jax.ShapeDtypeStruct((s, N_HEADS * HEAD_PAD), F32),
                   jax.ShapeDtypeStruct((s, N_HEADS * HEAD_PAD), F32),
                   jax.ShapeDtypeStruct((s, N_HEADS * V_HEAD), F32)),
        in_specs=[pl.BlockSpec((tq, HEAD_PAD), lambda h, i: (i, h)),
                  pl.BlockSpec((s, HEAD_PAD), lambda h, i: (0, h)),
                  pl.BlockSpec((s, V_HEAD), lambda h, i: (0, h)),
                  pl.BlockSpec((None, tq, 1), lambda h, i: (h, i, 0)),
                  pl.BlockSpec((tq, V_HEAD), lambda h, i: (i, n_pool_blocks + h))],
        out_specs=(pl.BlockSpec((tq, HEAD_PAD), lambda h, i: (i, h)),
                   pl.BlockSpec((s, HEAD_PAD), lambda h, i: (0, h)),
                   pl.BlockSpec((s, V_HEAD), lambda h, i: (0, h))),
        compiler_params=_params(("parallel", "arbitrary")),
    )(q, k, v, lse, dycat)


def _loss_head(x, gw, target, below, name):
    s, d = x.shape
    ts = _tile(s, 512)
    factor = below[2]

    def body(x_ref, gw_ref, tgt_ref, yb_ref, gb_ref, loss_ref, dx_ref, dgw_ref, dyb_ref, dgb_ref):
        @pl.when(pl.program_id(0) == 0)
        def _():
            loss_ref[...] = jnp.zeros_like(loss_ref)
            dgw_ref[...] = jnp.zeros_like(dgw_ref)
            dgb_ref[...] = jnp.zeros_like(dgb_ref)

        xv, gwv = x_ref[...], gw_ref[...]
        r = lax.rsqrt(jnp.mean(xv * xv, axis=-1, keepdims=True) + EPS)
        xn = xv * r
        err = xn * gwv - tgt_ref[...]
        loss_ref[...] += 0.5 * jnp.sum(jnp.mean(err * err, axis=-1, keepdims=True))
        dy = err / d
        dgw_ref[...] += jnp.sum(dy * xn, axis=0, keepdims=True)
        dxn = dy * gwv
        dx = r * (dxn - xn * jnp.mean(dxn * xn, axis=-1, keepdims=True))
        dx_ref[...] = dx
        dyb_ref[...] = ((factor * gb_ref[...]) * dx).astype(BF16)
        dgb_ref[...] += jnp.sum((factor * dx) * yb_ref[...].astype(F32), axis=0, keepdims=True)

    row = pl.BlockSpec((ts, d), lambda i: (i, 0))
    gate_spec, gate_arg = _vec_in(below[1])
    vec = jax.ShapeDtypeStruct((1, d), F32)
    return pl.pallas_call(
        body, name=name, grid=(s // ts,),
        out_shape=(jax.ShapeDtypeStruct((8, LANE), F32), jax.ShapeDtypeStruct((s, d), F32), vec,
                   jax.ShapeDtypeStruct((s, d), BF16), vec),
        in_specs=[row, _vec_spec(d), row, row, gate_spec],
        out_specs=(pl.BlockSpec((8, LANE), lambda i: (0, 0)), row, _vec_spec(d), row, _vec_spec(d)),
        compiler_params=_params(("arbitrary",)),
    )(x, gw, target, below[0], gate_arg)


def _ada_mod(c_all, ada_w, ada_b, name):
    nl, d, cols = ada_w.shape

    def body(c_ref, w_ref, b_ref, o_ref):
        cv = c_ref[...]
        act = (cv * jax.nn.sigmoid(cv)).astype(BF16)
        o_ref[...] = jnp.dot(act, w_ref[...].astype(BF16), preferred_element_type=F32) + b_ref[...]

    return pl.pallas_call(
        body, name=name, grid=(nl,), out_shape=jax.ShapeDtypeStruct((nl, N_DEV, cols), F32),
        in_specs=[pl.BlockSpec((N_DEV, d), lambda l: (0, 0)),
                  pl.BlockSpec((None, d, cols), lambda l: (l, 0, 0)),
                  pl.BlockSpec((None, 1, cols), lambda l: (l, 0, 0))],
        out_specs=pl.BlockSpec((None, N_DEV, cols), lambda l: (l, 0, 0)),
        compiler_params=_params(("parallel",)),
    )(c_all, ada_w, ada_b)


def _ada_grad(c_pad, dmod_pad, name):
    nl, kpad, cols = dmod_pad.shape
    d = c_pad.shape[1]

    def body(c_ref, dm_ref, o_ref):
        cv = c_ref[...]
        act = (cv * jax.nn.sigmoid(cv)).astype(BF16)
        o_ref[...] = lax.dot_general(act, dm_ref[...].astype(BF16), (((0,), (0,)), ((), ())),
                                     preferred_element_type=F32)

    return pl.pallas_call(
        body, name=name, grid=(nl,), out_shape=jax.ShapeDtypeStruct((nl, d, cols), F32),
        in_specs=[pl.BlockSpec((kpad, d), lambda l: (0, 0)),
                  pl.BlockSpec((None, kpad, cols), lambda l: (l, 0, 0))],
        out_specs=pl.BlockSpec((None, d, cols), lambda l: (l, 0, 0)),
        compiler_params=_params(("parallel",)),
    )(c_pad, dmod_pad)


def _adamw_math(w, g, m, v):
    nm = ADAM_B1 * m + (1.0 - ADAM_B1) * g
    nv = ADAM_B2 * v + (1.0 - ADAM_B2) * (g * g)
    m_hat = nm / (1.0 - ADAM_B1 ** ADAM_STEP)
    v_hat = nv / (1.0 - ADAM_B2 ** ADAM_STEP)
    return -ADAM_LR * (m_hat / (jnp.sqrt(v_hat) + ADAM_EPS) + ADAM_WD * w), nm, nv


def _adamw_rows(w3, gbuf, row_off, m3, v3, name):
    nl, r, d = w3.shape
    tr = _row_tile(math.gcd(r, row_off) if row_off else r, 352)
    first = row_off // tr

    def body(w_ref, g_ref, m_ref, v_ref, go_ref, d_ref, nm_ref, nv_ref):
        gv = g_ref[...]
        go_ref[...] = gv
        d_ref[...], nm_ref[...], nv_ref[...] = _adamw_math(w_ref[...], gv, m_ref[...], v_ref[...])

    blk = pl.BlockSpec((None, tr, d), lambda l, i: (l, i, 0))
    gblk = pl.BlockSpec((None, tr, d), lambda l, i: (l, first + i, 0))
    out = jax.ShapeDtypeStruct((nl, r, d), F32)
    return pl.pallas_call(
        body, name=name, grid=(nl, r // tr), out_shape=(out, out, out, out),
        in_specs=[blk, gblk, blk, blk], out_specs=(blk, blk, blk, blk),
        compiler_params=_params(("parallel", "parallel")),
    )(w3, gbuf, m3, v3)


def _adamw(w, g, m, v, name):
    rows, cols = w.shape
    tr = _row_tile(rows, 512)

    def body(w_ref, g_ref, m_ref, v_ref, d_ref, nm_ref, nv_ref):
        d_ref[...], nm_ref[...], nv_ref[...] = _adamw_math(w_ref[...], g_ref[...], m_ref[...], v_ref[...])

    blk = pl.BlockSpec((tr, cols), lambda i: (i, 0))
    out = jax.ShapeDtypeStruct((rows, cols), F32)
    return pl.pallas_call(
        body, name=name, grid=(rows // tr,), out_shape=(out, out, out),
        in_specs=[blk, blk, blk, blk], out_specs=(blk, blk, blk),
        compiler_params=_params(("parallel",)),
    )(w, g, m, v)


def _adamw_nd(w, g, m, v, name):
    shape = w.shape
    flat = (lambda t: t.reshape(1, -1)) if w.ndim == 1 else (lambda t: t.reshape(-1, shape[-1]))
    return tuple(t.reshape(shape) for t in _adamw(flat(w), flat(g), flat(m), flat(v), name))


def _pad_rows(t, rows):
    return jnp.pad(t, ((0, rows - t.shape[0]), (0, 0)))


def _pack_shard_layer(l, wts):
    def tr(name):
        return wts[name][l].astype(BF16).T

    parts = [tr("ffn1_w_gate"), tr("ffn1_w_up"), wts["ffn1_w_down"][l].astype(BF16),
             tr("ffn2_w_gate"), tr("ffn2_w_up"), wts["ffn2_w_down"][l].astype(BF16),
             wts["w_out"][l].astype(BF16),
             tr("w_kv_b").reshape(KV_SH_ROWS, D_MODEL),
             _pad_rows(tr("w_in"), 160),
             _pad_rows(tr("w_q_b").reshape(Q_SH_ROWS, D_MODEL), Q_PAD_ROWS)]
    return jnp.concatenate(parts, axis=0)


def _mixer_weights(w_out, small):
    w = {"out": w_out}
    small = small.reshape(N_DEV, SMALL_ROWS, D_MODEL)
    o_in, o_q = OFF_IN - OFF_KV, OFF_Q - OFF_KV
    w["kv"] = small[:, :KV_SH_ROWS].reshape(N_HEADS * HEAD_PAD, KV_LORA)
    w["in"] = _pad_rows(small[:, o_in:o_in + IN_SH].reshape(IN_COLS, D_MODEL), IN_PAD)
    wq = small[:, o_q:o_q + Q_SH_ROWS].reshape(N_HEADS, QK_HEAD, Q_LORA)
    w["q"] = jnp.pad(wq, ((0, 0), (0, HEAD_PAD - QK_HEAD), (0, 0))).reshape(N_HEADS * HEAD_PAD, Q_LORA)
    return w


def _grad_sources_tail(gr):
    gq = gr["q"].reshape(N_HEADS, HEAD_PAD, Q_LORA)[:, :QK_HEAD].reshape(N_DEV, Q_SH_ROWS, D_MODEL)
    small = jnp.concatenate([
        gr["kv"].reshape(N_DEV, KV_SH_ROWS, D_MODEL),
        jnp.pad(gr["in"][:IN_COLS].reshape(N_DEV, IN_SH, D_MODEL), ((0, 0), (0, 160 - IN_SH), (0, 0))),
        jnp.pad(gq, ((0, 0), (0, Q_PAD_ROWS - Q_SH_ROWS), (0, 0)))], axis=1)
    return [gr["out"], small.reshape(N_DEV * SMALL_ROWS, D_MODEL)]


def _pack_bf16_pairs(t):
    rows, d = t.shape
    return lax.bitcast_convert_type(t.astype(BF16).reshape(rows // 2, 2, d).transpose(0, 2, 1), F32)


def _unpack_bf16_pairs(p):
    pairs = jnp.swapaxes(lax.bitcast_convert_type(p, BF16), -1, -2)
    return pairs.reshape(p.shape[:-2] + (2 * p.shape[-2], p.shape[-1]))


def _small_layout(nl):
    names = [("dmod", nl * N_MOD), ("ffn1_norm", nl), ("mix_norm", nl), ("ffn2_norm", nl), ("q_a_norm", nl),
             ("kv_a_norm", nl), ("pool_scale", nl), ("final_norm", 1), ("loss", 1),
             ("pool_w", nl * 4 * POOL_GC * POOL_GC // D_MODEL // 2)]
    off, table = 0, {}
    for name, n in names:
        table[name] = (off, n)
        off += -(-n // 8) * 8
    return table, off


def _to_rows(t, width=D_MODEL):
    n, w = t.shape
    return jnp.pad(t, ((0, -(-n // 8) * 8 - n), (0, width - w)))


def kernel(x, c, positions, ada_w, ada_b, ffn1_norm, ffn1_w_gate, ffn1_w_up, ffn1_w_down, mix_norm, w_in, pool_w, pool_scale, q_a_norm, w_q_b, kv_a_norm, w_kv_b, w_out, ffn2_norm, ffn2_w_gate, ffn2_w_up, ffn2_w_down, final_norm, loss_target, m_ada_w, m_ada_b, m_ffn1_norm, m_ffn1_w_gate, m_ffn1_w_up, m_ffn1_w_down, m_mix_norm, m_w_in, m_pool_w, m_pool_scale, m_q_a_norm, m_w_q_b, m_kv_a_norm, m_w_kv_b, m_w_out, m_ffn2_norm, m_ffn2_w_gate, m_ffn2_w_up, m_ffn2_w_down, m_final_norm, v_ada_w, v_ada_b, v_ffn1_norm, v_ffn1_w_gate, v_ffn1_w_up, v_ffn1_w_down, v_mix_norm, v_w_in, v_pool_w, v_pool_scale, v_q_a_norm, v_w_q_b, v_kv_a_norm, v_w_kv_b, v_w_out, v_ffn2_norm, v_ffn2_w_gate, v_ffn2_w_up, v_ffn2_w_down, v_final_norm):
    wts = dict(ada_w=ada_w, ada_b=ada_b, ffn1_norm=ffn1_norm, ffn1_w_gate=ffn1_w_gate, ffn1_w_up=ffn1_w_up,
               ffn1_w_down=ffn1_w_down, mix_norm=mix_norm, w_in=w_in, pool_w=pool_w, pool_scale=pool_scale,
               q_a_norm=q_a_norm, w_q_b=w_q_b, kv_a_norm=kv_a_norm, w_kv_b=w_kv_b, w_out=w_out,
               ffn2_norm=ffn2_norm, ffn2_w_gate=ffn2_w_gate, ffn2_w_up=ffn2_w_up, ffn2_w_down=ffn2_w_down,
               final_norm=final_norm)
    mom_m = dict(ada_w=m_ada_w, ada_b=m_ada_b, ffn1_norm=m_ffn1_norm, ffn1_w_gate=m_ffn1_w_gate,
                 ffn1_w_up=m_ffn1_w_up, ffn1_w_down=m_ffn1_w_down, mix_norm=m_mix_norm, w_in=m_w_in,
                 pool_w=m_pool_w, pool_scale=m_pool_scale, q_a_norm=m_q_a_norm, w_q_b=m_w_q_b,
                 kv_a_norm=m_kv_a_norm, w_kv_b=m_w_kv_b, w_out=m_w_out, ffn2_norm=m_ffn2_norm,
                 ffn2_w_gate=m_ffn2_w_gate, ffn2_w_up=m_ffn2_w_up, ffn2_w_down=m_ffn2_w_down,
                 final_norm=m_final_norm)
    mom_v = dict(ada_w=v_ada_w, ada_b=v_ada_b, ffn1_norm=v_ffn1_norm, ffn1_w_gate=v_ffn1_w_gate,
                 ffn1_w_up=v_ffn1_w_up, ffn1_w_down=v_ffn1_w_down, mix_norm=v_mix_norm, w_in=v_w_in,
                 pool_w=v_pool_w, pool_scale=v_pool_scale, q_a_norm=v_q_a_norm, w_q_b=v_w_q_b,
                 kv_a_norm=v_kv_a_norm, w_kv_b=v_w_kv_b, w_out=v_w_out, ffn2_norm=v_ffn2_norm,
                 ffn2_w_gate=v_ffn2_w_gate, ffn2_w_up=v_ffn2_w_up, ffn2_w_down=v_ffn2_w_down,
                 final_norm=v_final_norm)
    order = list(wts)
    nl = ada_w.shape[0]
    seq = x.shape[1]
    me = 4 * lax.axis_index("x") + 2 * lax.axis_index("y") + lax.axis_index("c")
    ada_cols = ada_w.shape[2]

    def after_token(t, token):
        return t + token[0:1, 0:1].astype(t.dtype)

    packs = [_pack_shard_layer(l, wts) for l in range(nl)]

    c_all = _all_gather(jnp.broadcast_to(c, (8, D_MODEL)), "gather_c")[::8]

    ada_b_mine = lax.dynamic_slice_in_dim(ada_b, me * ada_cols, ada_cols, axis=1).reshape(nl, 1, ada_cols)
    mod_part = _ada_mod(c_all, ada_w, ada_b_mine, "ada_mod")
    mod_all = _all_gather(mod_part.reshape(nl * N_DEV, ada_cols), "gather_mod")
    mod_all = mod_all.reshape(N_DEV, nl, N_DEV, ada_cols)
    mod = lax.dynamic_index_in_dim(mod_all, me, axis=2, keepdims=False)
    mod = mod.transpose(1, 0, 2).reshape(nl * N_MOD, 1, D_MODEL)
    norm_tables = {name: wts[name].reshape(nl, 1, D_MODEL) for name in ("ffn1_norm", "mix_norm", "ffn2_norm")}

    def modrow(l, k):
        return mod, l * N_MOD + k

    def normrow(name, l):
        return norm_tables[name], l

    def start_layer(l, after):
        first = _gather_start(packs[l][:SPLIT_AB], ROWS_A, after, f"gather_start_{l}a")
        mixer = _gather_start(packs[l][OFF_OUT:], ROWS_TAIL, first[4], f"gather_start_{l}b")
        second = _gather_start(packs[l][SPLIT_AB:OFF_OUT], ROWS_A, mixer[4], f"gather_start_{l}c")
        return first, mixer, second

    flights = {0: start_layer(0, mod)}
    if nl > 1:
        flights[1] = start_layer(1, flights[0][2][4])
    last_start = flights[min(1, nl - 1)][2][4]

    cos, sin = _rope_tables(after_token(positions.reshape(seq, 1), last_start), "rope_tables")

    def vec(t):
        return t.reshape(1, -1)

    def landed(flight, rows_list, after, tag):
        send_sems, recv_sems, pk, lands, _ = flight
        pk, lands = _gather_wait(send_sems, recv_sems, pk, lands, after, f"gather_wait_{tag}")
        return _gather_finish(pk, rows_list, lands, "gather_finish")

    xs = x.reshape(seq, D_MODEL)
    saved = []
    for l in range(nl):
        norm1, up_after = normrow("ffn1_norm", l), None
        flight_a, flight_b, flight_c = flights[l]
        lands = landed(flight_a, ROWS_A, cos if l == 0 else xs, f"{l}a")
        if l >= 1 and l + 1 < nl:
            flights[l + 1] = start_layer(l + 1, lands[0])
            up_after = flights[l + 1][2][4]
        sv = {}

        def ffn_fwd(xin, norm, k0, wg, wu, wd, tag, after=None):
            h, a, b, t = _ffn_up(xin, norm, modrow(l, k0), modrow(l, k0 + 1), wg, wu, "ffn_up", after=after)
            y, xout = _mm(t, wd, "nn", "ffn_down", res=xin, gate=modrow(l, k0 + 2), gate_factor=0.5)
            sv[tag] = dict(x=xin, h=h, a=a, b=b, t=t, y=y)
            return xout

        xs = ffn_fwd(xs, norm1, 0, lands[0], lands[1], lands[2], "f1", up_after)
        w = dict(zip(("g1", "u1", "d1"), lands[:3]))
        w.update(_mixer_weights(*landed(flight_b, ROWS_TAIL, xs, f"{l}b")))
        sv["w"] = w

        h2, z = _norm_mm(xs, normrow("mix_norm", l), modrow(l, 3), modrow(l, 4), w["in"], "mix_in")
        y_pool, diff = _pool_fwd(z, pool_w[l], vec(pool_scale[l]), "pool_fwd")
        q, k, v, cqn, ckvn = _qkv_fwd(z, vec(q_a_norm[l]), vec(kv_a_norm[l]), w["q"], w["kv"], cos, sin, "qkv_fwd")
        o, lse = _attn_fwd(q, k, v, "attn_fwd")
        ycat = jnp.concatenate([y_pool, o.astype(BF16)], axis=1)
        y2, xmix = _mm(ycat, w["out"], "nn", "mix_out", res=xs, gate=modrow(l, 5), gate_factor=1.0)
        sv["mix"] = dict(x=xs, h=h2, z=z, diff=diff, q=q, k=k, v=v, cqn=cqn, ckvn=ckvn, lse=lse, ycat=ycat, y=y2)
        xs = xmix

        w.update(zip(("g2", "u2", "d2"), landed(flight_c, ROWS_A, xs, f"{l}c")))
        xs = ffn_fwd(xs, normrow("ffn2_norm", l), 6, w["g2"], w["u2"], w["d2"], "f2")
        saved.append(sv)

    loss_part, dx, d_final, *head = _loss_head(xs, vec(final_norm), loss_target.reshape(seq, D_MODEL),
                                               (saved[nl - 1]["f2"]["y"], modrow(nl - 1, 8), 0.5), "loss_head")

    small = {name: [None] * nl for name in ("ffn1_norm", "mix_norm", "ffn2_norm", "q_a_norm", "kv_a_norm",
                                            "pool_scale", "pool_w", "dmod")}
    core = lax.axis_index("c").astype(jnp.int32).reshape(1)
    chip = 2 * lax.axis_index("x") + lax.axis_index("y")
    exchanges = []

    def leave(srcs, rows_list, after, tag):
        return _pair_start(srcs, rows_list, after, f"pair_start_{tag}"), rows_list, tag

    def forward_on(pending, after, layer, row_off):
        (send_sems, recv_sems, srcs, land, _), rows_list, tag = pending
        srcs, land = _split_wait(send_sems, recv_sems, 1, srcs, land, after, f"pair_wait_{tag}")
        sums = _pair_sum(srcs, rows_list, land, core, "pair_sum")
        flight = _chip_exchange_start(sums, chip, after, f"exchange_start_{tag}")
        exchanges.append((flight, layer, row_off, tag))
        return flight[4]

    pending = None
    for l in reversed(range(nl)):
        sv = saved[l]
        w = sv["w"]
        dmod = [None] * N_MOD
        gr = {}

        def ffn_bwd(dxin, head, s_, norm, k0, wg, wu, wd, tag, below, first_after=None, mid=None):
            dy, dmod[k0 + 2] = head
            da, db, gr["d" + tag], gr["g" + tag], gr["u" + tag] = _ffn_bwd_cols(
                dy, s_["h"], s_["a"], s_["b"], s_["t"], wd, "ffn_bwd_cols", after=first_after)
            dh = _mm_pair(da, wg, db, wu, "ffn_bwd_dh", after=None if mid is None else mid(da))
            outs = _rm_bwd(dh, s_["x"], dxin, norm, modrow(l, k0 + 1), "rm_bwd", below=below)
            dmod[k0], dmod[k0 + 1] = outs[1], outs[2]
            return outs[0], outs[3], outs[4:]

        s_ = sv["mix"]
        dx, small["ffn2_norm"][l], head = ffn_bwd(
            dx, head, sv["f2"], normrow("ffn2_norm", l), 6, w["g2"], w["u2"], w["d2"], "2", (s_["y"], modrow(l, 5), 1.0),
            first_after=None if pending is None else pending[0][4])

        pending_c = leave([gr["g2"], gr["u2"], gr["d2"]], ROWS_A, dx, f"{l}c")
        mix_after = pending_c[0][4]
        if pending is not None:
            mix_after = forward_on(pending, mix_after, l + 1, GB_F1)
            pending = None
        dy, dmod[5] = head
        gr["out"] = _mm(s_["ycat"], dy, "tn", "mix_out_dw", out_dtype=BF16, tm=512, after=mix_after)
        dycat = _mm(dy, w["out"], "nt", "mix_out_dx", tm=1024)
        du, small["pool_w"][l], small["pool_scale"][l] = _pool_bwd(dycat, s_["diff"], pool_w[l], vec(pool_scale[l]), "pool_bwd")
        dq, dk, dv = _attn_bwd(s_["q"], s_["k"], s_["v"], s_["lse"], dycat, "attn_bwd")
        dz, dqb, dkvb, small["q_a_norm"][l], small["kv_a_norm"][l] = _qkv_bwd(
            dq, dk, dv, du, s_["z"], vec(q_a_norm[l]), vec(kv_a_norm[l]), w["q"], w["kv"], cos, sin, "qkv_bwd")
        gr["q"] = _mm(dqb, s_["cqn"], "tn", "q_b_dw", out_dtype=BF16, tm=512, after=forward_on(pending_c, dz, l, GB_F2))
        gr["kv"] = _mm(dkvb, s_["ckvn"], "tn", "kv_b_dw", out_dtype=BF16, tm=512)
        gr["in"] = _mm(dz, s_["h"], "tn", "mix_in_dw", out_dtype=BF16, tm=512)
        dh2 = _mm(dz, w["in"], "nn", "mix_in_dx", tm=1024)
        outs = _rm_bwd(dh2, s_["x"], dx, normrow("mix_norm", l), modrow(l, 4), "rm_bwd",
                       below=(sv["f1"]["y"], modrow(l, 2), 0.5))
        dx, dmod[3], dmod[4], small["mix_norm"][l] = outs[:4]
        head = outs[4:]

        first_after, mid = None, None
        if l == 0:
            pending_b = leave(_grad_sources_tail(gr), ROWS_TAIL, dx, "0b")
            first_after = pending_b[0][4]
            last_groups = []

            def mid(da):
                last_groups.append(leave([gr["g1"], gr["u1"], gr["d1"]], ROWS_A, da, "0a"))
                return forward_on(pending_b, last_groups[0][0][4], 0, GB_TAIL)
        below = (saved[l - 1]["f2"]["y"], modrow(l - 1, 8), 0.5) if l > 0 else None
        dx, small["ffn1_norm"][l], head = ffn_bwd(
            dx, head, sv["f1"], normrow("ffn1_norm", l), 0, w["g1"], w["u1"], w["d1"], "1", below, first_after, mid)

        small["dmod"][l] = jnp.concatenate(dmod, axis=0)
        if l > 0:
            pending = leave([gr["g1"], gr["u1"], gr["d1"]] + _grad_sources_tail(gr), ROWS_A + ROWS_TAIL, dx, l)

    grad_x = dx.reshape(x.shape)
    pending_a = last_groups[0]

    layout, small_rows = _small_layout(nl)
    pieces = {
        "dmod": jnp.concatenate(small["dmod"], axis=0),
        "ffn1_norm": jnp.concatenate(small["ffn1_norm"], axis=0),
        "mix_norm": jnp.concatenate(small["mix_norm"], axis=0),
        "ffn2_norm": jnp.concatenate(small["ffn2_norm"], axis=0),
        "q_a_norm": jnp.concatenate(small["q_a_norm"], axis=0),
        "kv_a_norm": jnp.concatenate(small["kv_a_norm"], axis=0),
        "pool_scale": jnp.concatenate(small["pool_scale"], axis=0),
        "final_norm": d_final,
        "loss": jnp.broadcast_to(loss_part[0:1, 0:1], (1, D_MODEL)),
        "pool_w": _pack_bf16_pairs(jnp.stack(small["pool_w"]).reshape(-1, D_MODEL)),
    }
    small_buf = jnp.concatenate([_to_rows(pieces[name]) for name in layout], axis=0)
    def landed_sums(gbuf, entries, after):
        for (send_sems, recv_sems, sums, recv, _), layer, row_off, tag in entries:
            _, recv = _split_wait(send_sems, recv_sems, N_CHIPS - 1, sums, recv, after, f"exchange_wait_{tag}")
            gbuf = _sum_slots_into(recv, gbuf, layer, row_off, "sum_grads")
        return gbuf

    gbuf = lax.empty((nl, ROWS_L, D_MODEL), F32)
    token_0a = forward_on(pending_a, dx, 0, GB_F1)
    spread = _spread_start(small_buf, me, token_0a, "small_start")
    gbuf = landed_sums(gbuf, [e for e in exchanges if e[3] != "0a"], spread[4])

    def swap(t):
        return t.transpose(0, 2, 1)

    def same(t):
        return t

    grads, updates = {}, {}

    def update_rows(gbuf, table):
        for wname, off, view in table:
            g, d_, nm, nv = _adamw_rows(view(wts[wname]), gbuf, off, view(mom_m[wname]), view(mom_v[wname]), "adamw_rows")
            grads[wname], updates[wname] = view(g), (view(d_), view(nm), view(nv))

    update_rows(gbuf, (("ffn2_w_gate", GB_F2, swap), ("ffn2_w_up", GB_F2 + FF_SH, swap),
                       ("ffn2_w_down", GB_F2 + 2 * FF_SH, same), ("w_out", GB_TAIL, same)))
    small_grads = {
        "w_kv_b": (gbuf[:, OFF_KV:OFF_KV + KV_SH_ROWS].reshape(nl, -1, KV_LORA).transpose(0, 2, 1), same),
        "w_in": (gbuf[:, OFF_IN:OFF_IN + IN_SH], swap),
        "w_q_b": (gbuf[:, OFF_Q:OFF_Q + Q_SH_ROWS].reshape(nl, -1, Q_LORA), swap),
    }
    for wname, (g, view) in small_grads.items():
        upd = _adamw_nd(view(wts[wname]), g, view(mom_m[wname]), view(mom_v[wname]), "adamw")
        grads[wname], updates[wname] = view(g), tuple(view(t) for t in upd)
    gbuf = landed_sums(gbuf, [e for e in exchanges if e[3] == "0a"], updates["w_q_b"][0])
    update_rows(gbuf, (("ffn1_w_gate", GB_F1, swap), ("ffn1_w_up", GB_F1 + FF_SH, swap),
                       ("ffn1_w_down", GB_F1 + 2 * FF_SH, same)))

    _, small_all = _split_wait(spread[0], spread[1], N_DEV - 1, spread[2], spread[3], updates["ffn1_w_down"][0],
                               "small_wait")
    pool_off, pool_rows = layout["pool_w"]
    small_sum = _sum_slots(small_all[:, :pool_off], "sum_small")
    pool_sum = _sum_slots(_unpack_bf16_pairs(small_all[:, pool_off:pool_off + pool_rows]), "sum_pool_w")

    def take(name, width=D_MODEL):
        off, n = layout[name]
        return small_sum[off:off + n, :width]

    late = {"ada_b": take("dmod").reshape(nl, N_MOD * D_MODEL),
            "ffn1_norm": take("ffn1_norm"), "mix_norm": take("mix_norm"), "ffn2_norm": take("ffn2_norm"),
            "q_a_norm": take("q_a_norm", Q_LORA), "kv_a_norm": take("kv_a_norm", KV_LORA),
            "pool_scale": take("pool_scale", POOL_WIDTH), "final_norm": take("final_norm").reshape(D_MODEL),
            "pool_w": pool_sum.reshape(pool_w.shape)}
    loss = take("loss")[0, 0]

    off, n = layout["dmod"]
    dmod_all = small_all[:, off:off + n].reshape(N_DEV, nl, N_MOD * D_MODEL)
    dmod_mine = lax.dynamic_slice_in_dim(dmod_all, me * ada_cols, ada_cols, axis=2)
    dmod_pad = jnp.pad(dmod_mine.transpose(1, 0, 2), ((0, 0), (0, LANE - N_DEV), (0, 0)))
    late["ada_w"] = _ada_grad(jnp.pad(c_all, ((0, LANE - N_DEV), (0, 0))), dmod_pad, "ada_grad")
    for name, g in late.items():
        grads[name], updates[name] = g, _adamw_nd(wts[name], g, mom_m[name], mom_v[name], "adamw")

    return (loss, grad_x, *[grads[n] for n in order], *[updates[n][0] for n in order],
            *[updates[n][1] for n in order], *[updates[n][2] for n in order])
```
